```python
import jax, jax.numpy as jnp
from jax import lax
import numpy as np

D_MODEL = 1024
BATCH = 8
SEQ = 4096
DEPTH = 2

N_A_LAYERS = DEPTH // 2
N_B_LAYERS = DEPTH - N_A_LAYERS
CHUNK = 128
A_EXPAND = 2
A_WIDTH = A_EXPAND * D_MODEL
A_GROUPS = 16
A_GROUP_DIM = A_WIDTH // A_GROUPS
HEAD_DIM = 64
N_Q_HEADS = D_MODEL // HEAD_DIM
N_KV_HEADS = max(1, N_Q_HEADS // 8)
Q_PER_KV = N_Q_HEADS // N_KV_HEADS
B_WIDTH = N_Q_HEADS * HEAD_DIM
KV_WIDTH = N_KV_HEADS * HEAD_DIM
WINDOW = 128
ROPE_THETA = 10000.0
EPS = 1e-5

kernel_name = "yoco_gmlp_swa_sink_hybrid"


def rms_norm(x, g):
    xf = x.astype(jnp.float32)
    y = xf * lax.rsqrt(jnp.mean(xf * xf, axis=-1, keepdims=True) + EPS) * g.astype(jnp.float32)
    return y.astype(x.dtype)


def layer_norm(x, g, b):
    xf = x.astype(jnp.float32)
    mu = jnp.mean(xf, axis=-1, keepdims=True)
    xc = xf - mu
    var = jnp.mean(xc * xc, axis=-1, keepdims=True)
    y = xc * lax.rsqrt(var + EPS) * g.astype(jnp.float32) + b.astype(jnp.float32)
    return y.astype(x.dtype)


def rotary(x, pos):
    dh = x.shape[-1]
    inv_freq = ROPE_THETA ** (-jnp.arange(0, dh, 2, dtype=jnp.float32) / dh)
    ang = pos[:, None] * inv_freq[None, :]
    cos = jnp.cos(ang)[None, :, None, :].astype(x.dtype)
    sin = jnp.sin(ang)[None, :, None, :].astype(x.dtype)
    x1, x2 = jnp.split(x, 2, axis=-1)
    return jnp.concatenate([x1 * cos - x2 * sin, x2 * cos + x1 * sin], axis=-1)


def band(t):
    b, s, h, d = t.shape
    blk = t.reshape(b, s // CHUNK, CHUNK, h, d)
    prev = jnp.concatenate([jnp.zeros_like(blk[:, :1]), blk[:, :-1]], axis=1)
    return jnp.concatenate([prev, blk], axis=2)


def gmlp_mixer(h, w_in, ln_g, ln_b, ws, bs, w_out):
    b, s, _ = h.shape
    nc = s // CHUNK
    z = h @ w_in
    u, v, g = jnp.split(z, 3, axis=-1)
    v = layer_norm(v, ln_g, ln_b)
    v = v.reshape(b, nc, CHUNK, A_GROUPS, A_GROUP_DIM)
    causal = jnp.tril(jnp.ones((CHUNK, CHUNK), dtype=bool))
    wsm = jnp.where(causal[None], ws, jnp.zeros_like(ws)).astype(v.dtype)
    sv = jnp.einsum('gts,bcsgd->bctgd', wsm, v) + bs.T[:, :, None].astype(v.dtype)
    sv = sv.reshape(b, s, A_WIDTH)
    y = u * sv * jax.nn.silu(g)
    return y @ w_out


def swa_mixer(h, k_band, v_band, pos, w_in, b_q, sinks, w_out):
    b, s, _ = h.shape
    nb = s // CHUNK
    z = h @ w_in
    q, g = jnp.split(z, 2, axis=-1)
    q = (q + b_q).reshape(b, s, N_Q_HEADS, HEAD_DIM)
    q = rotary(q, pos).reshape(b, nb, CHUNK, N_KV_HEADS, Q_PER_KV, HEAD_DIM)
    scores = jnp.einsum('bnqhrd,bnkhd->bnhrqk', q, k_band).astype(jnp.float32) * (HEAD_DIM ** -0.5)
    qi = jnp.arange(CHUNK)[:, None]
    kj = jnp.arange(2 * CHUNK)[None, :]
    rel = kj - CHUNK - qi
    in_window = (rel <= 0) & (rel > -WINDOW)
    key_valid = (jnp.arange(nb)[:, None] * CHUNK + jnp.arange(2 * CHUNK)[None, :] - CHUNK) >= 0
    mask = in_window[None] & key_valid[:, None, :]
    scores = jnp.where(mask[None, :, None, None], scores, -jnp.inf)
    sink = sinks.astype(jnp.float32).reshape(N_KV_HEADS, Q_PER_KV)[None, None, :, :, None, None]
    m = jnp.maximum(jnp.max(scores, axis=-1, keepdims=True), sink)
    p = jnp.exp(scores - m)
    denom = jnp.sum(p, axis=-1, keepdims=True) + jnp.exp(sink - m)
    p = (p / denom).astype(v_band.dtype)
    o = jnp.einsum('bnhrqk,bnkhd->bnqhrd', p, v_band).reshape(b, s, B_WIDTH)
    y = o * jax.nn.silu(g)
    return y @ w_out


def _fwd_setup_inputs(seed: int = 0) -> dict:
    key = jax.random.key(seed)
    ks = jax.random.split(key, 20)
    f32 = jnp.float32
    nrm = lambda k, shp, sc: jax.random.normal(k, shp, f32) * sc
    return {
        "x": nrm(ks[0], (BATCH, SEQ, D_MODEL), 1.0),
        "a_norm_g": 1.0 + nrm(ks[1], (N_A_LAYERS, D_MODEL), 0.02),
        "a_w_in": nrm(ks[2], (N_A_LAYERS, D_MODEL, 3 * A_WIDTH), D_MODEL ** -0.5),
        "a_ln_g": 1.0 + nrm(ks[3], (N_A_LAYERS, A_WIDTH), 0.02),
        "a_ln_b": nrm(ks[4], (N_A_LAYERS, A_WIDTH), 0.02),
        "a_ws": nrm(ks[5], (N_A_LAYERS, A_GROUPS, CHUNK, CHUNK), 0.5 * CHUNK ** -0.5),
        "a_bs": 1.0 + nrm(ks[6], (N_A_LAYERS, A_GROUPS, CHUNK), 0.02),
        "a_w_out": nrm(ks[7], (N_A_LAYERS, A_WIDTH, D_MODEL), 0.5 * A_WIDTH ** -0.5),
        "kv_norm_g": 1.0 + nrm(ks[8], (D_MODEL,), 0.02),
        "w_kv": nrm(ks[9], (D_MODEL, 2 * KV_WIDTH), D_MODEL ** -0.5),
        "b_kv": nrm(ks[10], (2 * KV_WIDTH,), 0.02),
        "b_norm_g": 1.0 + nrm(ks[11], (N_B_LAYERS, D_MODEL), 0.02),
        "b_w_in": nrm(ks[12], (N_B_LAYERS, D_MODEL, 2 * B_WIDTH), D_MODEL ** -0.5),
        "b_bq": nrm(ks[13], (N_B_LAYERS, B_WIDTH), 0.02),
        "b_sinks": nrm(ks[14], (N_B_LAYERS, N_Q_HEADS), 1.0),
        "b_w_out": nrm(ks[15], (N_B_LAYERS, B_WIDTH, D_MODEL), B_WIDTH ** -0.5),
        "final_norm_g": 1.0 + nrm(ks[16], (D_MODEL,), 0.02),
    }


def _fwd_reference(x, a_norm_g, a_w_in, a_ln_g, a_ln_b, a_ws, a_bs, a_w_out, kv_norm_g, w_kv, b_kv,
              b_norm_g, b_w_in, b_bq, b_sinks, b_w_out, final_norm_g):
    b, s, _ = x.shape
    pos = jnp.arange(s, dtype=jnp.float32)
    h = x
    k_band = None
    v_band = None
    for l in range(DEPTH):
        if l < N_A_LAYERS:
            i = l
            h = h + gmlp_mixer(rms_norm(h, a_norm_g[i]), a_w_in[i], a_ln_g[i], a_ln_b[i],
                               a_ws[i], a_bs[i], a_w_out[i])
        else:
            if l == N_A_LAYERS:
                kv = rms_norm(h, kv_norm_g) @ w_kv + b_kv
                k, v = jnp.split(kv, 2, axis=-1)
                k = rotary(k.reshape(b, s, N_KV_HEADS, HEAD_DIM), pos)
                v = v.reshape(b, s, N_KV_HEADS, HEAD_DIM)
                k_band = band(k)
                v_band = band(v)
            i = l - N_A_LAYERS
            h = h + swa_mixer(rms_norm(h, b_norm_g[i]), k_band, v_band, pos,
                              b_w_in[i], b_bq[i], b_sinks[i], b_w_out[i])
    return rms_norm(h, final_norm_g)


import jax as _jax
import jax.numpy as _jnp

TWIN_FORMAT = 'train_step'
FWD_PARAMS = ['x', 'a_norm_g', 'a_w_in', 'a_ln_g', 'a_ln_b', 'a_ws', 'a_bs', 'a_w_out', 'kv_norm_g', 'w_kv', 'b_kv', 'b_norm_g', 'b_w_in', 'b_bq', 'b_sinks', 'b_w_out', 'final_norm_g']
TWIN_WEIGHTS = ['a_norm_g', 'a_w_in', 'a_ln_g', 'a_ln_b', 'a_ws', 'a_bs', 'a_w_out', 'kv_norm_g', 'w_kv', 'b_kv', 'b_norm_g', 'b_w_in', 'b_bq', 'b_sinks', 'b_w_out', 'final_norm_g']
TWIN_DIFF_INPUT = 'x'
TWIN_INPUTS = ['x', 'a_norm_g', 'a_w_in', 'a_ln_g', 'a_ln_b', 'a_ws', 'a_bs', 'a_w_out', 'kv_norm_g', 'w_kv', 'b_kv', 'b_norm_g', 'b_w_in', 'b_bq', 'b_sinks', 'b_w_out', 'final_norm_g', 'loss_target', 'm_a_norm_g', 'm_a_w_in', 'm_a_ln_g', 'm_a_ln_b', 'm_a_ws', 'm_a_bs', 'm_a_w_out', 'm_kv_norm_g', 'm_w_kv', 'm_b_kv', 'm_b_norm_g', 'm_b_w_in', 'm_b_bq', 'm_b_sinks', 'm_b_w_out', 'm_final_norm_g', 'v_a_norm_g', 'v_a_w_in', 'v_a_ln_g', 'v_a_ln_b', 'v_a_ws', 'v_a_bs', 'v_a_w_out', 'v_kv_norm_g', 'v_w_kv', 'v_b_kv', 'v_b_norm_g', 'v_b_w_in', 'v_b_bq', 'v_b_sinks', 'v_b_w_out', 'v_final_norm_g']
TWIN_OUTPUTS = ['loss', 'grad_x', 'grad_a_norm_g', 'grad_a_w_in', 'grad_a_ln_g', 'grad_a_ln_b', 'grad_a_ws', 'grad_a_bs', 'grad_a_w_out', 'grad_kv_norm_g', 'grad_w_kv', 'grad_b_kv', 'grad_b_norm_g', 'grad_b_w_in', 'grad_b_bq', 'grad_b_sinks', 'grad_b_w_out', 'grad_final_norm_g', 'delta_a_norm_g', 'delta_a_w_in', 'delta_a_ln_g', 'delta_a_ln_b', 'delta_a_ws', 'delta_a_bs', 'delta_a_w_out', 'delta_kv_norm_g', 'delta_w_kv', 'delta_b_kv', 'delta_b_norm_g', 'delta_b_w_in', 'delta_b_bq', 'delta_b_sinks', 'delta_b_w_out', 'delta_final_norm_g', 'new_m_a_norm_g', 'new_m_a_w_in', 'new_m_a_ln_g', 'new_m_a_ln_b', 'new_m_a_ws', 'new_m_a_bs', 'new_m_a_w_out', 'new_m_kv_norm_g', 'new_m_w_kv', 'new_m_b_kv', 'new_m_b_norm_g', 'new_m_b_w_in', 'new_m_b_bq', 'new_m_b_sinks', 'new_m_b_w_out', 'new_m_final_norm_g', 'new_v_a_norm_g', 'new_v_a_w_in', 'new_v_a_ln_g', 'new_v_a_ln_b', 'new_v_a_ws', 'new_v_a_bs', 'new_v_a_w_out', 'new_v_kv_norm_g', 'new_v_w_kv', 'new_v_b_kv', 'new_v_b_norm_g', 'new_v_b_w_in', 'new_v_b_bq', 'new_v_b_sinks', 'new_v_b_w_out', 'new_v_final_norm_g']
TWIN_LEAF_KINDS = {'loss': 'loss', 'grad_x': 'grad_x', 'grad_a_norm_g': 'grad_w', 'grad_a_w_in': 'grad_w', 'grad_a_ln_g': 'grad_w', 'grad_a_ln_b': 'grad_w', 'grad_a_ws': 'grad_w', 'grad_a_bs': 'grad_w', 'grad_a_w_out': 'grad_w', 'grad_kv_norm_g': 'grad_w', 'grad_w_kv': 'grad_w', 'grad_b_kv': 'grad_w', 'grad_b_norm_g': 'grad_w', 'grad_b_w_in': 'grad_w', 'grad_b_bq': 'grad_w', 'grad_b_sinks': 'grad_w', 'grad_b_w_out': 'grad_w', 'grad_final_norm_g': 'grad_w', 'delta_a_norm_g': 'delta_w', 'delta_a_w_in': 'delta_w', 'delta_a_ln_g': 'delta_w', 'delta_a_ln_b': 'delta_w', 'delta_a_ws': 'delta_w', 'delta_a_bs': 'delta_w', 'delta_a_w_out': 'delta_w', 'delta_kv_norm_g': 'delta_w', 'delta_w_kv': 'delta_w', 'delta_b_kv': 'delta_w', 'delta_b_norm_g': 'delta_w', 'delta_b_w_in': 'delta_w', 'delta_b_bq': 'delta_w', 'delta_b_sinks': 'delta_w', 'delta_b_w_out': 'delta_w', 'delta_final_norm_g': 'delta_w', 'new_m_a_norm_g': 'new_m', 'new_m_a_w_in': 'new_m', 'new_m_a_ln_g': 'new_m', 'new_m_a_ln_b': 'new_m', 'new_m_a_ws': 'new_m', 'new_m_a_bs': 'new_m', 'new_m_a_w_out': 'new_m', 'new_m_kv_norm_g': 'new_m', 'new_m_w_kv': 'new_m', 'new_m_b_kv': 'new_m', 'new_m_b_norm_g': 'new_m', 'new_m_b_w_in': 'new_m', 'new_m_b_bq': 'new_m', 'new_m_b_sinks': 'new_m', 'new_m_b_w_out': 'new_m', 'new_m_final_norm_g': 'new_m', 'new_v_a_norm_g': 'new_v', 'new_v_a_w_in': 'new_v', 'new_v_a_ln_g': 'new_v', 'new_v_a_ln_b': 'new_v', 'new_v_a_ws': 'new_v', 'new_v_a_bs': 'new_v', 'new_v_a_w_out': 'new_v', 'new_v_kv_norm_g': 'new_v', 'new_v_w_kv': 'new_v', 'new_v_b_kv': 'new_v', 'new_v_b_norm_g': 'new_v', 'new_v_b_w_in': 'new_v', 'new_v_b_bq': 'new_v', 'new_v_b_sinks': 'new_v', 'new_v_b_w_out': 'new_v', 'new_v_final_norm_g': 'new_v'}


def _forward(args):
    return _fwd_reference(*[args[k] for k in FWD_PARAMS])


def _output_shape():
    def fwd():
        inp = _fwd_setup_inputs(0)
        return _fwd_reference(*[inp[k] for k in FWD_PARAMS])
    out = _jax.eval_shape(fwd)
    return out.shape, out.dtype

N_MICROBATCH = 1
ADAM_LR = 0.001
ADAM_B1 = 0.9
ADAM_B2 = 0.999
ADAM_EPS = 1e-08
ADAM_WD = 0.01
ADAM_STEP = 10
PER_EXAMPLE_BATCH_AXIS = {'x': 0, 'loss_target': 0}
SHARED_INPUTS = []
_WEIGHT_DTYPES = {'a_norm_g': _jnp.float32, 'a_w_in': _jnp.float32, 'a_ln_g': _jnp.float32, 'a_ln_b': _jnp.float32, 'a_ws': _jnp.float32, 'a_bs': _jnp.float32, 'a_w_out': _jnp.float32, 'kv_norm_g': _jnp.float32, 'w_kv': _jnp.float32, 'b_kv': _jnp.float32, 'b_norm_g': _jnp.float32, 'b_w_in': _jnp.float32, 'b_bq': _jnp.float32, 'b_sinks': _jnp.float32, 'b_w_out': _jnp.float32, 'final_norm_g': _jnp.float32}
MOMENT_SCALE = {'a_norm_g': 8.229656e-02, 'a_w_in': 3.265692e-02, 'a_ln_g': 1.270194e-02, 'a_ln_b': 1.230281e-02, 'a_ws': 2.542161e-02, 'a_bs': 3.617930e-02, 'a_w_out': 1.083165e-01, 'kv_norm_g': 2.851345e-02, 'w_kv': 5.409363e-02, 'b_kv': 1.971836e-01, 'b_norm_g': 2.830734e-02, 'b_w_in': 1.947107e-02, 'b_bq': 1.777066e-02, 'b_sinks': 1.651759e-02, 'b_w_out': 2.050347e-02, 'final_norm_g': 3.201081e+01}


def _to_microbatches(a, axis):
    t = _jnp.moveaxis(a, axis, 0)
    t = t.reshape((N_MICROBATCH, t.shape[0] // N_MICROBATCH) + t.shape[1:])
    return _jnp.moveaxis(t, 1, axis + 1)


def setup_inputs(seed: int = 0) -> dict:
    inp = _fwd_setup_inputs(seed)
    key = _jax.random.fold_in(_jax.random.key(seed), 7919)
    shape, _ = _output_shape()
    out = dict(inp)
    out["loss_target"] = _jax.random.normal(_jax.random.fold_in(key, 0), shape, _jnp.float32)
    for i, name in enumerate(TWIN_WEIGHTS):
        w = inp[name].astype(_jnp.float32)
        if MOMENT_SCALE is None:
            s = _jnp.sqrt(_jnp.mean(_jnp.square(w)) + 1e-30)
        else:
            s = MOMENT_SCALE[name]
        km, kv = _jax.random.split(_jax.random.fold_in(key, i + 1))
        out[name] = w
        out["m_" + name] = s * _jax.random.normal(km, w.shape, _jnp.float32)
        out["v_" + name] = (s * s) * _jax.random.uniform(kv, w.shape, _jnp.float32, 0.5, 1.5)
    if N_MICROBATCH > 1:
        for name, axis in PER_EXAMPLE_BATCH_AXIS.items():
            out[name] = _to_microbatches(out[name], axis)
    return {'x': out['x'], 'a_norm_g': out['a_norm_g'], 'a_w_in': out['a_w_in'], 'a_ln_g': out['a_ln_g'], 'a_ln_b': out['a_ln_b'], 'a_ws': out['a_ws'], 'a_bs': out['a_bs'], 'a_w_out': out['a_w_out'], 'kv_norm_g': out['kv_norm_g'], 'w_kv': out['w_kv'], 'b_kv': out['b_kv'], 'b_norm_g': out['b_norm_g'], 'b_w_in': out['b_w_in'], 'b_bq': out['b_bq'], 'b_sinks': out['b_sinks'], 'b_w_out': out['b_w_out'], 'final_norm_g': out['final_norm_g'], 'loss_target': out['loss_target'], 'm_a_norm_g': out['m_a_norm_g'], 'm_a_w_in': out['m_a_w_in'], 'm_a_ln_g': out['m_a_ln_g'], 'm_a_ln_b': out['m_a_ln_b'], 'm_a_ws': out['m_a_ws'], 'm_a_bs': out['m_a_bs'], 'm_a_w_out': out['m_a_w_out'], 'm_kv_norm_g': out['m_kv_norm_g'], 'm_w_kv': out['m_w_kv'], 'm_b_kv': out['m_b_kv'], 'm_b_norm_g': out['m_b_norm_g'], 'm_b_w_in': out['m_b_w_in'], 'm_b_bq': out['m_b_bq'], 'm_b_sinks': out['m_b_sinks'], 'm_b_w_out': out['m_b_w_out'], 'm_final_norm_g': out['m_final_norm_g'], 'v_a_norm_g': out['v_a_norm_g'], 'v_a_w_in': out['v_a_w_in'], 'v_a_ln_g': out['v_a_ln_g'], 'v_a_ln_b': out['v_a_ln_b'], 'v_a_ws': out['v_a_ws'], 'v_a_bs': out['v_a_bs'], 'v_a_w_out': out['v_a_w_out'], 'v_kv_norm_g': out['v_kv_norm_g'], 'v_w_kv': out['v_w_kv'], 'v_b_kv': out['v_b_kv'], 'v_b_norm_g': out['v_b_norm_g'], 'v_b_w_in': out['v_b_w_in'], 'v_b_bq': out['v_b_bq'], 'v_b_sinks': out['v_b_sinks'], 'v_b_w_out': out['v_b_w_out'], 'v_final_norm_g': out['v_final_norm_g']}


def _loss(weights, diff, rest, loss_target):
    with _jax.named_scope("forward"):
        args = {**rest, TWIN_DIFF_INPUT: diff, **{k: w.astype(_WEIGHT_DTYPES[k]) for k, w in weights.items()}}
        y = _forward(args)
    with _jax.named_scope("loss_head"):
        err = _jnp.square(y.astype(_jnp.float32) - loss_target)
        return 0.5 * _jnp.sum(_jnp.mean(err, axis=-1)) if err.ndim else 0.5 * err


def _adamw(w, g, m, v):
    m = ADAM_B1 * m + (1.0 - ADAM_B1) * g
    v = ADAM_B2 * v + (1.0 - ADAM_B2) * _jnp.square(g)
    m_hat = m / (1.0 - ADAM_B1 ** ADAM_STEP)
    v_hat = v / (1.0 - ADAM_B2 ** ADAM_STEP)
    delta = -ADAM_LR * (m_hat / (_jnp.sqrt(v_hat) + ADAM_EPS) + ADAM_WD * w)
    return delta, m, v


def reference(x, a_norm_g, a_w_in, a_ln_g, a_ln_b, a_ws, a_bs, a_w_out, kv_norm_g, w_kv, b_kv, b_norm_g, b_w_in, b_bq, b_sinks, b_w_out, final_norm_g, loss_target, m_a_norm_g, m_a_w_in, m_a_ln_g, m_a_ln_b, m_a_ws, m_a_bs, m_a_w_out, m_kv_norm_g, m_w_kv, m_b_kv, m_b_norm_g, m_b_w_in, m_b_bq, m_b_sinks, m_b_w_out, m_final_norm_g, v_a_norm_g, v_a_w_in, v_a_ln_g, v_a_ln_b, v_a_ws, v_a_bs, v_a_w_out, v_kv_norm_g, v_w_kv, v_b_kv, v_b_norm_g, v_b_w_in, v_b_bq, v_b_sinks, v_b_w_out, v_final_norm_g):
    given = dict(x=x, a_norm_g=a_norm_g, a_w_in=a_w_in, a_ln_g=a_ln_g, a_ln_b=a_ln_b, a_ws=a_ws, a_bs=a_bs, a_w_out=a_w_out, kv_norm_g=kv_norm_g, w_kv=w_kv, b_kv=b_kv, b_norm_g=b_norm_g, b_w_in=b_w_in, b_bq=b_bq, b_sinks=b_sinks, b_w_out=b_w_out, final_norm_g=final_norm_g, loss_target=loss_target, m_a_norm_g=m_a_norm_g, m_a_w_in=m_a_w_in, m_a_ln_g=m_a_ln_g, m_a_ln_b=m_a_ln_b, m_a_ws=m_a_ws, m_a_bs=m_a_bs, m_a_w_out=m_a_w_out, m_kv_norm_g=m_kv_norm_g, m_w_kv=m_w_kv, m_b_kv=m_b_kv, m_b_norm_g=m_b_norm_g, m_b_w_in=m_b_w_in, m_b_bq=m_b_bq, m_b_sinks=m_b_sinks, m_b_w_out=m_b_w_out, m_final_norm_g=m_final_norm_g, v_a_norm_g=v_a_norm_g, v_a_w_in=v_a_w_in, v_a_ln_g=v_a_ln_g, v_a_ln_b=v_a_ln_b, v_a_ws=v_a_ws, v_a_bs=v_a_bs, v_a_w_out=v_a_w_out, v_kv_norm_g=v_kv_norm_g, v_w_kv=v_w_kv, v_b_kv=v_b_kv, v_b_norm_g=v_b_norm_g, v_b_w_in=v_b_w_in, v_b_bq=v_b_bq, v_b_sinks=v_b_sinks, v_b_w_out=v_b_w_out, v_final_norm_g=v_final_norm_g)
    weights = {n: given[n] for n in TWIN_WEIGHTS}
    shared = {n: given[n] for n in SHARED_INPUTS}
    per_example = {n: given[n] for n in ['x']}
    grad_fn = _jax.value_and_grad(_loss, argnums=(0, 1))

    def one_microbatch(ex, loss_target):
        ex = dict(ex)
        diff = ex.pop(TWIN_DIFF_INPUT)
        return grad_fn(weights, diff, {**shared, **ex}, loss_target)

    if N_MICROBATCH == 1:
        loss, (grad_w, grad_x) = one_microbatch(per_example, given["loss_target"])
    else:
        def body(carry, xs):
            loss_sum, grad_sum = carry
            l_k, (gw_k, gx_k) = one_microbatch(xs[0], xs[1])
            with _jax.named_scope("update"):
                return (loss_sum + l_k, _jax.tree.map(_jnp.add, grad_sum, gw_k)), gx_k

        init = (_jnp.zeros((), _jnp.float32), _jax.tree.map(_jnp.zeros_like, weights))
        (loss, grad_w), grad_x = _jax.lax.scan(body, init, (per_example, given["loss_target"]))
    with _jax.named_scope("update"):
        delta_w, new_m, new_v = {}, {}, {}
        for n in TWIN_WEIGHTS:
            delta_w[n], new_m[n], new_v[n] = _adamw(weights[n], grad_w[n], given["m_" + n], given["v_" + n])
    return (loss, grad_x, *[grad_w[n] for n in TWIN_WEIGHTS], *[delta_w[n] for n in TWIN_WEIGHTS],
            *[new_m[n] for n in TWIN_WEIGHTS], *[new_v[n] for n in TWIN_WEIGHTS])
```

```python
import functools
import math

import jax
import jax.numpy as jnp
from jax import lax
from jax.experimental import pallas as pl
from jax.experimental.pallas import tpu as pltpu

CHUNK = 128
HEAD_DIM = 64
ROPE_THETA = 10000.0
EPS = 1e-5
ADAM_LR = 0.001
ADAM_B1 = 0.9
ADAM_B2 = 0.999
ADAM_EPS = 1e-08
ADAM_WD = 0.01
ADAM_STEP = 10
N_DEV = 8
LANES = 128
NEG = -1e30

BF = jnp.bfloat16
F32 = jnp.float32
MESH = pl.DeviceIdType.MESH
AXES = ("x", "y", "c")
VMEM_LIMIT = 56 * 1024 * 1024


def _dot(a, b):
    return jnp.dot(a, b, preferred_element_type=F32)


def _dot_nt(a, b):
    return lax.dot_general(a, b, (((1,), (1,)), ((), ())), preferred_element_type=F32)


def _dot_tn(a, b):
    return lax.dot_general(a, b, (((0,), (0,)), ((), ())), preferred_element_type=F32)


def _const_spec(shape):
    nd = len(shape)
    return pl.BlockSpec(shape, lambda *_: (0,) * nd, pipeline_mode=pl.Buffered(1))


def _acc_spec(shape):
    nd = len(shape)
    return pl.BlockSpec(shape, lambda *_: (0,) * nd)


def _row_spec(tm, width):
    return pl.BlockSpec((tm, width), lambda i: (i, 0))


def _params(sem):
    return pltpu.CompilerParams(dimension_semantics=sem, vmem_limit_bytes=VMEM_LIMIT)


def _rot(x, c, s1, s2):
    return x * c + pltpu.roll(x, 96, 1) * s1 + pltpu.roll(x, 32, 1) * s2


def _rot_bwd(d, c, s1, s2):
    return d * c + pltpu.roll(d * s1, 32, 1) + pltpu.roll(d * s2, 96, 1)


def _silu_parts(g):
    sg = jax.nn.sigmoid(g)
    return g * sg, sg * (1.0 + g * (1.0 - sg))


def _rms_bwd(dn, xh, r, g):
    a = dn * g
    return r * (a - xh * jnp.mean(a * xh, axis=-1, keepdims=True))


def _lane_lo(shape):
    return lax.broadcasted_iota(jnp.int32, shape, 1) < HEAD_DIM


def _split4(t):
    lo = _lane_lo(t.shape)
    tr = pltpu.roll(t, HEAD_DIM, 1)
    z = jnp.zeros_like(t)
    return jnp.concatenate([jnp.where(lo, t, z), jnp.where(lo, z, tr), jnp.where(lo, tr, z), jnp.where(lo, z, t)], axis=1)


def _band_mask(first_thr):
    shape = (4 * CHUNK, 2 * CHUNK)
    row = lax.broadcasted_iota(jnp.int32, shape, 0) & (CHUNK - 1)
    col = lax.broadcasted_iota(jnp.int32, shape, 1)
    return (col > row) & (col <= row + CHUNK) & (col >= first_thr)


def _sink_col(sink_ref, h, ab):
    return jnp.concatenate([jnp.full((CHUNK, 1), sink_ref[h * 8 + 2 * j + ab], F32) for j in range(4)], axis=0)


def _softmax_sink(s, valid, sink):
    s = jnp.where(valid, s, NEG)
    m = jnp.maximum(jnp.max(s, axis=-1, keepdims=True), sink)
    p = jnp.exp(s - m)
    es = jnp.exp(sink - m)
    inv = 1.0 / (jnp.sum(p, axis=-1, keepdims=True) + es)
    return p * inv, es * inv


def _stack_pairs(t, h):
    return jnp.concatenate([t[:, (h * 4 + j) * LANES:(h * 4 + j + 1) * LANES] for j in range(4)], axis=0)


def _a_fwd(x, g_a, wa_in, ln_g, ln_b, ws, bs_t, wa_out, g_kv, w_kv, b_kv, rc, rs1, rs2):
    T, D = x.shape
    AW = wa_out.shape[0]
    G = ws.shape[0]
    SH = wa_in.shape[2]
    TM = min(256, T)
    nC = TM // CHUNK

    def body(x_ref, ga_ref, wain_ref, lng_ref, lnb_ref, ws_ref, bst_ref, waout_ref, gkv_ref, wkv_ref, bkv_ref,
             rc_ref, rs1_ref, rs2_ref,
             h1_ref, u_ref, gt_ref, sv_ref, vhat_ref, rstd_ref, k4_ref, v4_ref, z_scr, sv_scr):
        xv = x_ref[...]
        r1 = lax.rsqrt(jnp.mean(xv * xv, axis=-1, keepdims=True) + EPS)
        n1 = (xv * r1 * ga_ref[...]).astype(BF)
        for j in range(N_DEV):
            z_scr[:, j * SH:(j + 1) * SH] = _dot(n1, wain_ref[j])
        u = z_scr[:, :AW]
        v = z_scr[:, AW:2 * AW]
        gt = z_scr[:, 2 * AW:]
        mu = jnp.mean(v, axis=-1, keepdims=True)
        xc = v - mu
        rstd = lax.rsqrt(jnp.mean(xc * xc, axis=-1, keepdims=True) + EPS)
        vhat = xc * rstd
        vln = (vhat * lng_ref[...] + lnb_ref[...]).astype(BF)
        tri = lax.broadcasted_iota(jnp.int32, (CHUNK, CHUNK), 0) >= lax.broadcasted_iota(jnp.int32, (CHUNK, CHUNK), 1)
        for g in range(G):
            wsm = jnp.where(tri, ws_ref[g], 0.0).astype(BF)
            bias = bst_ref[:, g:g + 1]
            for c in range(nC):
                blk = vln[c * CHUNK:(c + 1) * CHUNK, g * CHUNK:(g + 1) * CHUNK]
                sv_scr[c * CHUNK:(c + 1) * CHUNK, g * CHUNK:(g + 1) * CHUNK] = _dot(wsm, blk) + bias
        sv = sv_scr[...]
        silu, _ = _silu_parts(gt)
        y = (u * sv * silu).astype(BF)
        h1 = xv + _dot(y, waout_ref[...])
        h1_ref[...] = h1
        u_ref[...] = u.astype(BF)
        gt_ref[...] = gt.astype(BF)
        sv_ref[...] = sv.astype(BF)
        vhat_ref[...] = vhat.astype(BF)
        rstd_ref[...] = jnp.broadcast_to(rstd, rstd_ref.shape)
        rkv = lax.rsqrt(jnp.mean(h1 * h1, axis=-1, keepdims=True) + EPS)
        nkv = (h1 * rkv * gkv_ref[...]).astype(BF)
        kv = _dot(nkv, wkv_ref[...]) + bkv_ref[...]
        k_rot = _rot(kv[:, :LANES], rc_ref[...], rs1_ref[...], rs2_ref[...])
        k4_ref[...] = _split4(k_rot).astype(BF)
        v4_ref[...] = _split4(kv[:, LANES:]).astype(BF)

    row = functools.partial(_row_spec, TM)
    S = jax.ShapeDtypeStruct
    return pl.pallas_call(
        body, name="a_fwd", grid=(T // TM,),
        in_specs=[row(D), _const_spec((1, D)), _const_spec(wa_in.shape), _const_spec((1, AW)), _const_spec((1, AW)),
                  _const_spec(ws.shape), _const_spec(bs_t.shape), _const_spec(wa_out.shape), _const_spec((1, D)),
                  _const_spec(w_kv.shape), _const_spec((1, 2 * LANES)), row(LANES), row(LANES), row(LANES)],
        out_specs=[row(D), row(AW), row(AW), row(AW), row(AW), row(LANES), row(4 * LANES), row(4 * LANES)],
        out_shape=(S((T, D), F32), S((T, AW), BF), S((T, AW), BF), S((T, AW), BF), S((T, AW), BF), S((T, LANES), F32),
                   S((T, 4 * LANES), BF), S((T, 4 * LANES), BF)),
        scratch_shapes=[pltpu.VMEM((TM, 3 * AW), F32), pltpu.VMEM((TM, AW), F32)],
        compiler_params=_params(("arbitrary",)),
    )(x, g_a, wa_in, ln_g, ln_b, ws, bs_t, wa_out, g_kv, w_kv, b_kv, rc, rs1, rs2)


def _b_fwd(h1, g_b, wb_in, bq, rc, rs1, rs2, k4, v4, sinks, wb_out, g_f, target):
    T, D = h1.shape
    BW = wb_out.shape[0]
    SH = wb_in.shape[2]
    TM = min(256, T)
    nC = TM // CHUNK
    nP = BW // LANES

    def body(h1_ref, gb_ref, wbin_ref, bq_ref, rc_ref, rs1_ref, rs2_ref, k4_ref, v4_ref, sink_ref, wbout_ref, gf_ref,
             tgt_ref, q_ref, g2_ref, o_ref, dh2_ref, dh2b_ref, loss_ref, dgf_ref, z_scr, o_scr):
        i = pl.program_id(0)
        h1v = h1_ref[...]
        r2 = lax.rsqrt(jnp.mean(h1v * h1v, axis=-1, keepdims=True) + EPS)
        n2 = (h1v * r2 * gb_ref[...]).astype(BF)
        for j in range(N_DEV):
            z_scr[:, j * SH:(j + 1) * SH] = _dot(n2, wbin_ref[j])
        c_t, s1_t, s2_t = rc_ref[...], rs1_ref[...], rs2_ref[...]
        for p in range(nP):
            cols = slice(p * LANES, (p + 1) * LANES)
            qp = _rot(z_scr[:, cols] + bq_ref[:, cols], c_t, s1_t, s2_t) * (HEAD_DIM ** -0.5)
            q_ref[:, cols] = qp.astype(BF)
        g2 = z_scr[:, BW:]
        g2_ref[...] = g2.astype(BF)
        for c in range(nC):
            ci = i * nC + c
            rows = slice(c * CHUNK, (c + 1) * CHUNK)
            prev = pl.multiple_of(jnp.maximum(ci - 1, 0) * CHUNK, CHUNK)
            cur = pl.multiple_of(ci * CHUNK, CHUNK)
            kb = jnp.concatenate([k4_ref[pl.ds(prev, CHUNK), :], k4_ref[pl.ds(cur, CHUNK), :]], axis=0)
            vb = jnp.concatenate([v4_ref[pl.ds(prev, CHUNK), :], v4_ref[pl.ds(cur, CHUNK), :]], axis=0)
            valid = _band_mask(jnp.where(ci == 0, CHUNK, 0))
            qc = q_ref[rows, :]
            for h in range(2):
                qs = _stack_pairs(qc, h)
                acc = None
                for ab in range(2):
                    sl = slice((2 * h + ab) * LANES, (2 * h + ab + 1) * LANES)
                    p_ab, _ = _softmax_sink(_dot_nt(qs, kb[:, sl]), valid, _sink_col(sink_ref, h, ab))
                    part = _dot(p_ab.astype(BF), vb[:, sl])
                    acc = part if acc is None else acc + part
                for j in range(4):
                    o_scr[rows, (h * 4 + j) * LANES:(h * 4 + j + 1) * LANES] = acc[j * CHUNK:(j + 1) * CHUNK]
        o = o_scr[...]
        o_ref[...] = o.astype(BF)
        silu, _ = _silu_parts(g2)
        h2 = h1v + _dot((o * silu).astype(BF), wbout_ref[...])
        rf = lax.rsqrt(jnp.mean(h2 * h2, axis=-1, keepdims=True) + EPS)
        xh = h2 * rf
        gf = gf_ref[...]
        err = xh * gf - tgt_ref[...]
        dyf = err * (1.0 / D)
        dh2 = _rms_bwd(dyf, xh, rf, gf)
        dh2_ref[...] = dh2
        dh2b_ref[...] = dh2.astype(BF)

        @pl.when(i == 0)
        def _():
            loss_ref[...] = jnp.zeros_like(loss_ref)
            dgf_ref[...] = jnp.zeros_like(dgf_ref)

        loss_ref[...] += 0.5 * jnp.sum(jnp.mean(err * err, axis=-1, keepdims=True), axis=0, keepdims=True)
        dgf_ref[...] += jnp.sum(dyf * xh, axis=0, keepdims=True)

    row = functools.partial(_row_spec, TM)
    S = jax.ShapeDtypeStruct
    return pl.pallas_call(
        body, name="b_fwd", grid=(T // TM,),
        in_specs=[row(D), _const_spec((1, D)), _const_spec(wb_in.shape), _const_spec((1, BW)), row(LANES), row(LANES),
                  row(LANES), _const_spec(k4.shape), _const_spec(v4.shape), pl.BlockSpec(memory_space=pltpu.SMEM),
                  _const_spec(wb_out.shape), _const_spec((1, D)), row(D)],
        out_specs=[row(BW), row(BW), row(BW), row(D), row(D), _acc_spec((1, 1)), _acc_spec((1, D))],
        out_shape=(S((T, BW), BF), S((T, BW), BF), S((T, BW), BF), S((T, D), F32), S((T, D), BF), S((1, 1), F32),
                   S((1, D), F32)),
        scratch_shapes=[pltpu.VMEM((TM, 2 * BW), F32), pltpu.VMEM((TM, BW), F32)],
        compiler_params=_params(("arbitrary",)),
    )(h1, g_b, wb_in, bq, rc, rs1, rs2, k4, v4, sinks, wb_out, g_f, target)


def _b_bwd(dh2, h1, q, g2, o, k4, v4, sinks, wb_out, wb_in, g_b, rc, rs1, rs2):
    T, D = h1.shape
    BW = wb_out.shape[0]
    SH = wb_in.shape[2]
    TM = min(256, T)
    nC = TM // CHUNK
    nP = BW // LANES

    def body(dh2_ref, h1_ref, q_ref, g2_ref, o_ref, k4_ref, v4_ref, sink_ref, wbout_ref, wbin_ref, gb_ref, rc_ref,
             rs1_ref, rs2_ref,
             dh1_ref, dz2_ref, n2_ref, y2_ref, dk_ref, dv_ref, dbq_ref, dgb_ref, dsink_ref, do_scr, dq_scr):
        i = pl.program_id(0)

        @pl.when(i == 0)
        def _():
            dk_ref[...] = jnp.zeros_like(dk_ref)
            dv_ref[...] = jnp.zeros_like(dv_ref)
            dbq_ref[...] = jnp.zeros_like(dbq_ref)
            dgb_ref[...] = jnp.zeros_like(dgb_ref)
            dsink_ref[...] = jnp.zeros_like(dsink_ref)

        dh2 = dh2_ref[...]
        dy2 = _dot_nt(dh2.astype(BF), wbout_ref[...])
        g2v = g2_ref[...].astype(F32)
        ov = o_ref[...].astype(F32)
        silu, dsilu = _silu_parts(g2v)
        y2_ref[...] = (ov * silu).astype(BF)
        do_scr[...] = (dy2 * silu).astype(BF)
        dz2_ref[:, BW:] = (dy2 * ov * dsilu).astype(BF)
        lane = lax.broadcasted_iota(jnp.int32, (1, LANES), 1)
        dsink = jnp.zeros((1, LANES), F32)
        for c in range(nC):
            ci = i * nC + c
            rows = slice(c * CHUNK, (c + 1) * CHUNK)
            prev = pl.multiple_of(jnp.maximum(ci - 1, 0) * CHUNK, CHUNK)
            cur = pl.multiple_of(ci * CHUNK, CHUNK)
            kb = jnp.concatenate([k4_ref[pl.ds(prev, CHUNK), :], k4_ref[pl.ds(cur, CHUNK), :]], axis=0)
            vb = jnp.concatenate([v4_ref[pl.ds(prev, CHUNK), :], v4_ref[pl.ds(cur, CHUNK), :]], axis=0)
            valid = _band_mask(jnp.where(ci == 0, CHUNK, 0))
            qc = q_ref[rows, :]
            doc = do_scr[rows, :]
            dkb = jnp.zeros((2 * CHUNK, LANES), F32)
            dvb = jnp.zeros((2 * CHUNK, LANES), F32)
            lo = _lane_lo((2 * CHUNK, LANES))
            for h in range(2):
                qs = _stack_pairs(qc, h)
                dos = _stack_pairs(doc, h)
                dqs = None
                for ab in range(2):
                    sl = slice((2 * h + ab) * LANES, (2 * h + ab + 1) * LANES)
                    ka, va = kb[:, sl], vb[:, sl]
                    p_ab, ps = _softmax_sink(_dot_nt(qs, ka), valid, _sink_col(sink_ref, h, ab))
                    dp = _dot_nt(dos, va)
                    delta = jnp.sum(p_ab * dp, axis=-1, keepdims=True)
                    ds = (p_ab * (dp - delta)).astype(BF)
                    part = _dot(ds, ka)
                    dqs = part if dqs is None else dqs + part
                    dka = _dot_tn(ds, qs)
                    dva = _dot_tn(p_ab.astype(BF), dos)
                    if (h == 0) != (ab == 0):
                        dka = pltpu.roll(dka, HEAD_DIM, 1)
                        dva = pltpu.roll(dva, HEAD_DIM, 1)
                    dkb += jnp.where(lo, dka, 0.0) if h == 0 else jnp.where(lo, 0.0, dka)
                    dvb += jnp.where(lo, dva, 0.0) if h == 0 else jnp.where(lo, 0.0, dva)
                    dsk = -(ps * delta)
                    for j in range(4):
                        tot = jnp.sum(dsk[j * CHUNK:(j + 1) * CHUNK], axis=0, keepdims=True)
                        dsink += jnp.where(lane == h * 8 + 2 * j + ab, tot, 0.0)
                for j in range(4):
                    dq_scr[rows, (h * 4 + j) * LANES:(h * 4 + j + 1) * LANES] = dqs[j * CHUNK:(j + 1) * CHUNK]
            dk_ref[pl.ds(prev, CHUNK), :] += dkb[:CHUNK]
            dk_ref[pl.ds(cur, CHUNK), :] += dkb[CHUNK:]
            dv_ref[pl.ds(prev, CHUNK), :] += dvb[:CHUNK]
            dv_ref[pl.ds(cur, CHUNK), :] += dvb[CHUNK:]
        dsink_ref[...] += dsink
        c_t, s1_t, s2_t = rc_ref[...], rs1_ref[...], rs2_ref[...]
        for p in range(nP):
            cols = slice(p * LANES, (p + 1) * LANES)
            dqp = _rot_bwd(dq_scr[:, cols] * (HEAD_DIM ** -0.5), c_t, s1_t, s2_t)
            dbq_ref[:, cols] += jnp.sum(dqp, axis=0, keepdims=True)
            dz2_ref[:, cols] = dqp.astype(BF)
        h1v = h1_ref[...]
        r2 = lax.rsqrt(jnp.mean(h1v * h1v, axis=-1, keepdims=True) + EPS)
        xh = h1v * r2
        gb = gb_ref[...]
        n2_ref[...] = (xh * gb).astype(BF)
        dn2 = None
        for j in range(N_DEV):
            part = _dot_nt(dz2_ref[:, j * SH:(j + 1) * SH], wbin_ref[j])
            dn2 = part if dn2 is None else dn2 + part
        dgb_ref[...] += jnp.sum(dn2 * xh, axis=0, keepdims=True)
        dh1_ref[...] = dh2 + _rms_bwd(dn2, xh, r2, gb)

    row = functools.partial(_row_spec, TM)
    S = jax.ShapeDtypeStruct
    return pl.pallas_call(
        body, name="b_bwd", grid=(T // TM,),
        in_specs=[row(D), row(D), row(BW), row(BW), row(BW), _const_spec(k4.shape), _const_spec(v4.shape),
                  pl.BlockSpec(memory_space=pltpu.SMEM), _const_spec(wb_out.shape), _const_spec(wb_in.shape),
                  _const_spec((1, D)), row(LANES), row(LANES), row(LANES)],
        out_specs=[row(D), row(2 * BW), row(D), row(BW), _acc_spec((T, LANES)), _acc_spec((T, LANES)),
                   _acc_spec((1, BW)), _acc_spec((1, D)), _acc_spec((1, LANES))],
        out_shape=(S((T, D), F32), S((T, 2 * BW), BF), S((T, D), BF), S((T, BW), BF), S((T, LANES), F32),
                   S((T, LANES), F32), S((1, BW), F32), S((1, D), F32), S((1, LANES), F32)),
        scratch_shapes=[pltpu.VMEM((TM, BW), BF), pltpu.VMEM((TM, BW), F32)],
        compiler_params=_params(("arbitrary",)),
    )(dh2, h1, q, g2, o, k4, v4, sinks, wb_out, wb_in, g_b, rc, rs1, rs2)


def _a_bwd(dh1p, dk, dv, h1, x, g_kv, w_kv, g_a, wa_out, wa_in, ws, ln_g, ln_b, u, gt, sv, vhat, rstd, rc, rs1, rs2):
    T, D = x.shape
    AW = wa_out.shape[0]
    G = ws.shape[0]
    SH = wa_in.shape[2]
    TM = min(128, T)
    nC = TM // CHUNK

    def body(dh1p_ref, dk_ref, dv_ref, h1_ref, x_ref, gkv_ref, wkv_ref, ga_ref, waout_ref, wain_ref, ws_ref, lng_ref,
             lnb_ref, u_ref, gt_ref, sv_ref, vhat_ref, rstd_ref, rc_ref, rs1_ref, rs2_ref,
             dx_ref, dz_ref, y_ref, n1_ref, nkv_ref, dkv_ref, dh1_ref, dga_ref, dgkv_ref, dbkv_ref, dlng_ref, dlnb_ref,
             dws_ref, dbs_ref, dsv_scr, dvln_scr):
        i = pl.program_id(0)

        @pl.when(i == 0)
        def _():
            for r in (dga_ref, dgkv_ref, dbkv_ref, dlng_ref, dlnb_ref, dws_ref, dbs_ref):
                r[...] = jnp.zeros_like(r)

        dk_pre = _rot_bwd(dk_ref[...], rc_ref[...], rs1_ref[...], rs2_ref[...])
        dkv = jnp.concatenate([dk_pre, dv_ref[...]], axis=1)
        dbkv_ref[...] += jnp.sum(dkv, axis=0, keepdims=True)
        dkv_b = dkv.astype(BF)
        dkv_ref[...] = dkv_b
        h1v = h1_ref[...]
        rkv = lax.rsqrt(jnp.mean(h1v * h1v, axis=-1, keepdims=True) + EPS)
        xh_kv = h1v * rkv
        gkv = gkv_ref[...]
        nkv_ref[...] = (xh_kv * gkv).astype(BF)
        dnkv = _dot_nt(dkv_b, wkv_ref[...])
        dgkv_ref[...] += jnp.sum(dnkv * xh_kv, axis=0, keepdims=True)
        dh1 = dh1p_ref[...] + _rms_bwd(dnkv, xh_kv, rkv, gkv)
        dh1_b = dh1.astype(BF)
        dh1_ref[...] = dh1_b
        dy = _dot_nt(dh1_b, waout_ref[...])
        uv = u_ref[...].astype(F32)
        gtv = gt_ref[...].astype(F32)
        svv = sv_ref[...].astype(F32)
        silu, dsilu = _silu_parts(gtv)
        us = uv * silu
        y_ref[...] = (us * svv).astype(BF)
        dz_ref[:, :AW] = (dy * svv * silu).astype(BF)
        dz_ref[:, 2 * AW:] = (dy * uv * svv * dsilu).astype(BF)
        dsv_scr[...] = (dy * us).astype(BF)
        vhat_v = vhat_ref[...].astype(F32)
        lng = lng_ref[...]
        vln_b = (vhat_v * lng + lnb_ref[...]).astype(BF)
        tri = lax.broadcasted_iota(jnp.int32, (CHUNK, CHUNK), 0) >= lax.broadcasted_iota(jnp.int32, (CHUNK, CHUNK), 1)
        lane = lax.broadcasted_iota(jnp.int32, (CHUNK, LANES), 1)
        dbs = jnp.zeros((CHUNK, LANES), F32)
        for g in range(G):
            wsm = jnp.where(tri, ws_ref[g], 0.0).astype(BF)
            cols = slice(g * CHUNK, (g + 1) * CHUNK)
            dws_g = None
            for c in range(nC):
                rows = slice(c * CHUNK, (c + 1) * CHUNK)
                dsv_cg = dsv_scr[rows, cols]
                dvln_scr[rows, cols] = _dot_tn(wsm, dsv_cg)
                part = _dot_nt(dsv_cg, vln_b[rows, cols])
                dws_g = part if dws_g is None else dws_g + part
                dbs += jnp.where(lane == g, jnp.sum(dsv_cg.astype(F32), axis=-1, keepdims=True), 0.0)
            dws_ref[g] += jnp.where(tri, dws_g, 0.0)
        dbs_ref[...] += dbs
        dvln = dvln_scr[...]
        dlng_ref[...] += jnp.sum(dvln * vhat_v, axis=0, keepdims=True)
        dlnb_ref[...] += jnp.sum(dvln, axis=0, keepdims=True)
        a = dvln * lng
        dvv = rstd_ref[:, 0:1] * (a - jnp.mean(a, axis=-1, keepdims=True)
                                  - vhat_v * jnp.mean(a * vhat_v, axis=-1, keepdims=True))
        dz_ref[:, AW:2 * AW] = dvv.astype(BF)
        xv = x_ref[...]
        r1 = lax.rsqrt(jnp.mean(xv * xv, axis=-1, keepdims=True) + EPS)
        xh = xv * r1
        ga = ga_ref[...]
        n1_ref[...] = (xh * ga).astype(BF)
        dn1 = None
        for j in range(N_DEV):
            part = _dot_nt(dz_ref[:, j * SH:(j + 1) * SH], wain_ref[j])
            dn1 = part if dn1 is None else dn1 + part
        dga_ref[...] += jnp.sum(dn1 * xh, axis=0, keepdims=True)
        dx_ref[...] = dh1 + _rms_bwd(dn1, xh, r1, ga)

    row = functools.partial(_row_spec, TM)
    S = jax.ShapeDtypeStruct
    return pl.pallas_call(
        body, name="a_bwd", grid=(T // TM,),
        in_specs=[row(D), row(LANES), row(LANES), row(D), row(D), _const_spec((1, D)), _const_spec(w_kv.shape),
                  _const_spec((1, D)), _const_spec(wa_out.shape), _const_spec(wa_in.shape), _const_spec(ws.shape),
                  _const_spec((1, AW)), _const_spec((1, AW)), row(AW), row(AW), row(AW), row(AW), row(LANES),
                  row(LANES), row(LANES), row(LANES)],
        out_specs=[row(D), row(3 * AW), row(AW), row(D), row(D), row(2 * LANES), row(D),
                   _acc_spec((1, D)), _acc_spec((1, D)), _acc_spec((1, 2 * LANES)), _acc_spec((1, AW)),
                   _acc_spec((1, AW)), _acc_spec(ws.shape), _acc_spec((CHUNK, LANES))],
        out_shape=(S((T, D), F32), S((T, 3 * AW), BF), S((T, AW), BF), S((T, D), BF), S((T, D), BF),
                   S((T, 2 * LANES), BF), S((T, D), BF),
                   S((1, D), F32), S((1, D), F32), S((1, 2 * LANES), F32), S((1, AW), F32), S((1, AW), F32),
                   S(ws.shape, F32), S((CHUNK, LANES), F32)),
        scratch_shapes=[pltpu.VMEM((TM, AW), BF), pltpu.VMEM((TM, AW), F32)],
        compiler_params=_params(("arbitrary",)),
    )(dh1p, dk, dv, h1, x, g_kv, w_kv, g_a, wa_out, wa_in, ws, ln_g, ln_b, u, gt, sv, vhat, rstd, rc, rs1, rs2)


def _wgrad(a, b, nblk, name):
    T, K = a.shape
    N = b.shape[1] // nblk
    BT = min(512, T)
    nt = T // BT

    def body(a_ref, b_ref, o_ref, acc):
        t = pl.program_id(1)

        @pl.when(t == 0)
        def _():
            acc[...] = jnp.zeros_like(acc)

        acc[...] += _dot_tn(a_ref[...], b_ref[...])

        @pl.when(t == nt - 1)
        def _():
            o_ref[0] = acc[...].astype(BF)

    return pl.pallas_call(
        body, name=name, grid=(nblk, nt),
        in_specs=[pl.BlockSpec((BT, K), lambda j, t: (t, 0)), pl.BlockSpec((BT, N), lambda j, t: (t, j))],
        out_specs=pl.BlockSpec((1, K, N), lambda j, t: (j, 0, 0)),
        out_shape=jax.ShapeDtypeStruct((nblk, K, N), BF),
        scratch_shapes=[pltpu.VMEM((K, N), F32)],
        compiler_params=_params(("arbitrary", "arbitrary")),
    )(a, b)


def _my_index():
    return 4 * lax.axis_index("x") + 2 * lax.axis_index("y") + lax.axis_index("c")


def _all_gather(arrs, dtypes, name):
    n = len(arrs)

    def body(*refs):
        ins, outs = refs[:n], refs[n:2 * n]
        stages = refs[2 * n:3 * n]
        send_sems, recv_sems, local_sems = refs[3 * n:]
        x, y, c = lax.axis_index("x"), lax.axis_index("y"), lax.axis_index("c")
        me, sibling = (x, y, c), (x, y, 1 - c)
        chips = [(1 - x, y), (x, 1 - y), (1 - x, 1 - y)]

        def idx(p):
            return 4 * p[0] + 2 * p[1] + p[2]

        def copy(a, k, block, to, src=None):
            dst = outs[a].at[idx(block)]
            return pltpu.make_async_remote_copy(src_ref=dst if src is None else src, dst_ref=dst,
                                                send_sem=send_sems.at[a, k], recv_sem=recv_sems.at[a, k],
                                                device_id=to, device_id_type=MESH)

        owns, firsts, passed = [], [], []
        for a in range(n):
            stages[a][...] = ins[a][...].astype(stages[a].dtype)
            own = pltpu.make_async_copy(stages[a], outs[a].at[idx(me)], local_sems.at[a])
            own.start()
            owns.append(own)
            first = [copy(a, 0, me, sibling, src=stages[a])]
            first += [copy(a, 1 + j, me, (*chip, c), src=stages[a]) for j, chip in enumerate(chips)]
            for cp in first:
                cp.start()
            firsts += first
        for a in range(n):
            for j, chip in enumerate(chips):
                copy(a, 1 + j, (*chip, c), me).wait_recv()
                fwd = copy(a, 4 + j, (*chip, c), sibling)
                fwd.start()
                passed.append(fwd)
        for a in range(n):
            copy(a, 0, sibling, me).wait_recv()
            for j, chip in enumerate(chips):
                copy(a, 4 + j, (*chip, 1 - c), me).wait_recv()
        for cp in firsts + passed:
            cp.wait_send()
        for own in owns:
            own.wait()

    vm = pl.BlockSpec(memory_space=pltpu.VMEM)
    hbm = pl.BlockSpec(memory_space=pl.ANY)
    return pl.pallas_call(
        body, name=name,
        in_specs=[vm] * n, out_specs=[hbm] * n,
        out_shape=[jax.ShapeDtypeStruct((N_DEV,) + a.shape, dt) for a, dt in zip(arrs, dtypes)],
        scratch_shapes=[pltpu.VMEM(a.shape, dt) for a, dt in zip(arrs, dtypes)]
        + [pltpu.SemaphoreType.DMA((n, 7)), pltpu.SemaphoreType.DMA((n, 7)), pltpu.SemaphoreType.DMA((n,))],
        compiler_params=pltpu.CompilerParams(vmem_limit_bytes=VMEM_LIMIT),
    )(*arrs)


def _all_to_all(arrs, name):
    n = len(arrs)

    def body(*refs):
        ins, outs = refs[:n], refs[n:2 * n]
        send_sems, recv_sems, local_sems = refs[2 * n:]
        x, y, c = lax.axis_index("x"), lax.axis_index("y"), lax.axis_index("c")
        me = 4 * x + 2 * y + c
        copies = []
        for a in range(n):
            own = pltpu.make_async_copy(ins[a].at[me], outs[a].at[me], local_sems.at[a])
            own.start()
            for k in range(1, N_DEV):
                bx, by, bc = (k >> 2) & 1, (k >> 1) & 1, k & 1
                px = jnp.where(bx == 1, 1 - x, x)
                py = jnp.where(by == 1, 1 - y, y)
                pc = jnp.where(bc == 1, 1 - c, c)
                cp = pltpu.make_async_remote_copy(src_ref=ins[a].at[4 * px + 2 * py + pc], dst_ref=outs[a].at[me],
                                                  send_sem=send_sems.at[a, k - 1], recv_sem=recv_sems.at[a, k - 1],
                                                  device_id=(px, py, pc), device_id_type=MESH)
                cp.start()
                copies.append(cp)
        for cp in copies:
            cp.wait_send()
        for a in range(n):
            for k in range(1, N_DEV):
                bx, by, bc = (k >> 2) & 1, (k >> 1) & 1, k & 1
                px = jnp.where(bx == 1, 1 - x, x)
                py = jnp.where(by == 1, 1 - y, y)
                pc = jnp.where(bc == 1, 1 - c, c)
                peer = 4 * px + 2 * py + pc
                pltpu.make_async_remote_copy(src_ref=ins[a].at[me], dst_ref=outs[a].at[peer],
                                             send_sem=send_sems.at[a, k - 1], recv_sem=recv_sems.at[a, k - 1],
                                             device_id=(px, py, pc), device_id_type=MESH).wait_recv()
            pltpu.make_async_copy(ins[a].at[me], outs[a].at[me], local_sems.at[a]).wait()

    hbm = pl.BlockSpec(memory_space=pl.ANY)
    return pl.pallas_call(
        body, name=name,
        in_specs=[hbm] * n, out_specs=[hbm] * n,
        out_shape=[jax.ShapeDtypeStruct(a.shape, a.dtype) for a in arrs],
        scratch_shapes=[pltpu.SemaphoreType.DMA((n, 7)), pltpu.SemaphoreType.DMA((n, 7)), pltpu.SemaphoreType.DMA((n,))],
    )(*arrs)


def _adam_math(w, g, m, v):
    m = ADAM_B1 * m + (1.0 - ADAM_B1) * g
    v = ADAM_B2 * v + (1.0 - ADAM_B2) * (g * g)
    m_hat = m / (1.0 - ADAM_B1 ** ADAM_STEP)
    v_hat = v / (1.0 - ADAM_B2 ** ADAM_STEP)
    delta = -ADAM_LR * (m_hat / (jnp.sqrt(v_hat) + ADAM_EPS) + ADAM_WD * w)
    return delta, m, v


def _sum_adam(parts, w, m, v, name):
    R, C = w.shape
    BR = CHUNK if R % CHUNK == 0 else R

    def body(p_ref, w_ref, m_ref, v_ref, g_ref, d_ref, nm_ref, nv_ref):
        g = p_ref[0].astype(F32)
        for i in range(1, N_DEV):
            g = g + p_ref[i].astype(F32)
        g_ref[...] = g
        d_ref[...], nm_ref[...], nv_ref[...] = _adam_math(w_ref[...], g, m_ref[...], v_ref[...])

    blk = pl.BlockSpec((BR, C), lambda i: (i, 0))
    S = jax.ShapeDtypeStruct((R, C), F32)
    return pl.pallas_call(
        body, name=name, grid=(R // BR,),
        in_specs=[pl.BlockSpec((N_DEV, BR, C), lambda i: (0, i, 0)), blk, blk, blk],
        out_specs=[blk] * 4, out_shape=(S,) * 4,
        compiler_params=_params(("arbitrary",)),
    )(parts, w, m, v)


def _sum8(parts, name):
    _, R, C = parts.shape

    def body(p_ref, o_ref):
        g = p_ref[0]
        for i in range(1, N_DEV):
            g = g + p_ref[i]
        o_ref[...] = g

    return pl.pallas_call(body, name=name, out_shape=jax.ShapeDtypeStruct((R, C), F32))(parts)


def _adam_only(g, w, m, v, name):
    def body(g_ref, w_ref, m_ref, v_ref, d_ref, nm_ref, nv_ref):
        d_ref[...], nm_ref[...], nv_ref[...] = _adam_math(w_ref[...], g_ref[...], m_ref[...], v_ref[...])

    S = jax.ShapeDtypeStruct(w.shape, F32)
    return pl.pallas_call(body, name=name, out_shape=(S,) * 3)(g, w, m, v)


def _rope_tables(T):
    pos = jnp.arange(T, dtype=F32)
    inv_freq = ROPE_THETA ** (-jnp.arange(0, HEAD_DIM, 2, dtype=F32) / HEAD_DIM)
    ang = pos[:, None] * inv_freq[None, :]
    cos, sin, zero = jnp.cos(ang), jnp.sin(ang), jnp.zeros_like(ang)
    c = jnp.concatenate([cos, cos, cos, cos], axis=1)
    s1 = jnp.concatenate([-sin, zero, -sin, zero], axis=1)
    s2 = jnp.concatenate([zero, sin, zero, sin], axis=1)
    return c, s1, s2


def _rows(a):
    flat = a.reshape(-1)
    pad = (-flat.shape[0]) % LANES
    if pad:
        flat = jnp.concatenate([flat, jnp.zeros((pad,), flat.dtype)])
    return flat.reshape(-1, LANES)


def _pack(arrs, total_rows):
    rows = [_rows(a) for a in arrs]
    used = sum(r.shape[0] for r in rows)
    if total_rows > used:
        rows.append(jnp.zeros((total_rows - used, LANES), F32))
    return jnp.concatenate(rows, axis=0)


def _unpack(packed, shapes):
    out, at = [], 0
    for shp in shapes:
        size = math.prod(shp)
        nrow = -(-size // LANES)
        out.append(packed[at:at + nrow].reshape(-1)[:size].reshape(shp))
        at += nrow
    return out


def kernel(x, a_norm_g, a_w_in, a_ln_g, a_ln_b, a_ws, a_bs, a_w_out, kv_norm_g, w_kv, b_kv, b_norm_g, b_w_in, b_bq, b_sinks, b_w_out, final_norm_g, loss_target, m_a_norm_g, m_a_w_in, m_a_ln_g, m_a_ln_b, m_a_ws, m_a_bs, m_a_w_out, m_kv_norm_g, m_w_kv, m_b_kv, m_b_norm_g, m_b_w_in, m_b_bq, m_b_sinks, m_b_w_out, m_final_norm_g, v_a_norm_g, v_a_w_in, v_a_ln_g, v_a_ln_b, v_a_ws, v_a_bs, v_a_w_out, v_kv_norm_g, v_w_kv, v_b_kv, v_b_norm_g, v_b_w_in, v_b_bq, v_b_sinks, v_b_w_out, v_final_norm_g):
    T, D = x.shape[1], x.shape[2]
    AW = a_ln_g.shape[1] * N_DEV
    G = a_ws.shape[1]
    assert w_kv.shape[1] == 2 * LANES and a_ws.shape[2] == CHUNK and T % CHUNK == 0
    me = _my_index()

    vec = jnp.concatenate([a_norm_g, a_ln_g, a_ln_b], axis=1)
    vec = jnp.broadcast_to(vec, (8, vec.shape[1]))
    wa_in, wa_out, wkv, wb_in, wb_out, vecs = _all_gather(
        [a_w_in[0], a_w_out[0], w_kv, b_w_in[0], b_w_out[0], vec], [BF, BF, BF, BF, BF, F32], "gather_weights")
    wa_out = wa_out.reshape(AW, D)
    wkv = wkv.reshape(D, 2 * LANES)
    wb_out = wb_out.reshape(-1, D)
    vecs = vecs[:, 0, :]
    ds = D // N_DEV
    g_a = vecs[:, :ds].reshape(1, D)
    ln_g = vecs[:, ds:ds + AW // N_DEV].reshape(1, AW)
    ln_b = vecs[:, ds + AW // N_DEV:].reshape(1, AW)

    rc, rs1, rs2 = _rope_tables(T)
    ws = a_ws[0]
    bs_t = a_bs[0].T
    g_kv = kv_norm_g.reshape(1, D)
    bkv = b_kv.reshape(1, -1)
    g_f = final_norm_g.reshape(1, D)
    sinks = b_sinks.reshape(-1)
    xs, tgt = x[0], loss_target[0]

    h1, u, gt, sv, vhat, rstd, k4, v4 = _a_fwd(xs, g_a, wa_in, ln_g, ln_b, ws, bs_t, wa_out, g_kv, wkv, bkv, rc, rs1, rs2)
    q, g2, o, dh2, dh2_b, loss, d_gf = _b_fwd(h1, b_norm_g, wb_in, b_bq, rc, rs1, rs2, k4, v4, sinks, wb_out, g_f, tgt)
    dh1p, dz2, n2, y2, dk, dv, d_bq, d_gb, d_sink = _b_bwd(dh2, h1, q, g2, o, k4, v4, sinks, wb_out, wb_in, b_norm_g,
                                                           rc, rs1, rs2)
    (dx, dz, y, n1, nkv, dkv, dh1, d_ga, d_gkv, d_bkv, d_lng, d_lnb, d_ws, d_bst) = _a_bwd(
        dh1p, dk, dv, h1, xs, g_kv, wkv, g_a, wa_out, wa_in, ws, ln_g, ln_b, u, gt, sv, vhat, rstd, rc, rs1, rs2)
    gw_a_in = _wgrad(n1, dz, N_DEV, "wgrad_a_in")
    gw_a_out = _wgrad(y, dh1, 1, "wgrad_a_out").reshape(N_DEV, AW // N_DEV, D)
    gw_kv = _wgrad(nkv, dkv, 1, "wgrad_kv").reshape(N_DEV, D // N_DEV, 2 * LANES)
    gw_b_in = _wgrad(n2, dz2, N_DEV, "wgrad_b_in")
    gw_b_out = _wgrad(y2, dh2_b, 1, "wgrad_b_out").reshape(N_DEV, -1, D)

    small = [d_ws, d_bst[:, :G].T, d_gkv, d_bkv, d_gb, d_bq, d_sink[:, :b_sinks.shape[1]], d_gf, d_ga, d_lng, d_lnb]
    used = sum(-(-a.size // LANES) for a in small)
    per = -(-used // (8 * N_DEV)) * 8
    small_pack = _pack(small, per * N_DEV).reshape(N_DEV, per, LANES)
    r_a_in, r_a_out, r_kv, r_b_in, r_b_out, r_small = _all_to_all(
        [gw_a_in, gw_a_out, gw_kv, gw_b_in, gw_b_out, small_pack], "exchange_grads")

    g_a_in, d_a_in, nm_a_in, nv_a_in = _sum_adam(r_a_in, a_w_in[0], m_a_w_in[0], v_a_w_in[0], "adam_a_in")
    g_a_out, d_a_out, nm_a_out, nv_a_out = _sum_adam(r_a_out, a_w_out[0], m_a_w_out[0], v_a_w_out[0], "adam_a_out")
    g_kvw, d_kvw, nm_kvw, nv_kvw = _sum_adam(r_kv, w_kv, m_w_kv, v_w_kv, "adam_kv")
    g_b_in, d_b_in, nm_b_in, nv_b_in = _sum_adam(r_b_in, b_w_in[0], m_b_w_in[0], v_b_w_in[0], "adam_b_in")
    g_b_out, d_b_out, nm_b_out, nv_b_out = _sum_adam(r_b_out, b_w_out[0], m_b_w_out[0], v_b_w_out[0], "adam_b_out")

    red = _sum8(r_small, "sum_small")
    (full_small,) = _all_gather([red], [F32], "gather_small")
    full_small = full_small.reshape(N_DEV * per, LANES)
    rep_shapes = [a_ws.shape, a_bs.shape, kv_norm_g.shape, b_kv.shape, b_norm_g.shape, b_bq.shape, b_sinks.shape,
                  final_norm_g.shape]
    n_rep = sum(-(-math.prod(s) // LANES) for s in rep_shapes)
    gs = _unpack(full_small, rep_shapes + [(N_DEV, a_norm_g.shape[1]), (N_DEV, a_ln_g.shape[1]), (N_DEV, a_ln_b.shape[1])])
    g_ang = lax.dynamic_slice_in_dim(gs[8], me, 1, axis=0)
    g_alng = lax.dynamic_slice_in_dim(gs[9], me, 1, axis=0)
    g_alnb = lax.dynamic_slice_in_dim(gs[10], me, 1, axis=0)
    sm_g = gs[:8] + [g_ang, g_alng, g_alnb]
    sm_shapes = [a.shape for a in sm_g]
    tot = -(-(n_rep + sum(-(-a.size // LANES) for a in sm_g[8:])) // 8) * 8
    pw = _pack([a_ws, a_bs, kv_norm_g, b_kv, b_norm_g, b_bq, b_sinks, final_norm_g, a_norm_g, a_ln_g, a_ln_b], tot)
    pm = _pack([m_a_ws, m_a_bs, m_kv_norm_g, m_b_kv, m_b_norm_g, m_b_bq, m_b_sinks, m_final_norm_g, m_a_norm_g,
                m_a_ln_g, m_a_ln_b], tot)
    pv = _pack([v_a_ws, v_a_bs, v_kv_norm_g, v_b_kv, v_b_norm_g, v_b_bq, v_b_sinks, v_final_norm_g, v_a_norm_g,
                v_a_ln_g, v_a_ln_b], tot)
    pg = _pack(sm_g, tot)
    pd, pnm, pnv = _adam_only(pg, pw, pm, pv, "adam_small")
    sd, snm, snv = _unpack(pd, sm_shapes), _unpack(pnm, sm_shapes), _unpack(pnv, sm_shapes)

    loss = lax.psum(loss[0, 0], AXES)

    def order(big, sm):
        a_in, a_out, kvw, b_in, b_out = big
        ws_, bs_, kvg, bkv_, bng, bq_, snk, fng, ang, alng, alnb = sm
        return (ang, a_in[None], alng, alnb, ws_, bs_, a_out[None], kvg, kvw, bkv_, bng, b_in[None], bq_, snk,
                b_out[None], fng)

    grads = order((g_a_in, g_a_out, g_kvw, g_b_in, g_b_out), sm_g)
    deltas = order((d_a_in, d_a_out, d_kvw, d_b_in, d_b_out), sd)
    new_m = order((nm_a_in, nm_a_out, nm_kvw, nm_b_in, nm_b_out), snm)
    new_v = order((nv_a_in, nv_a_out, nv_kvw, nv_b_in, nv_b_out), snv)
    return (loss, dx[None], *grads, *deltas, *new_m, *new_v)
```

```python
import functools
import math

import jax
import jax.numpy as jnp
from jax import lax
from jax.experimental import pallas as pl
from jax.experimental.pallas import tpu as pltpu

CHUNK = 128
HEAD_DIM = 64
ROPE_THETA = 10000.0
EPS = 1e-5
ADAM_LR = 0.001
ADAM_B1 = 0.9
ADAM_B2 = 0.999
ADAM_EPS = 1e-08
ADAM_WD = 0.01
ADAM_STEP = 10
N_DEV = 8
LANES = 128
NEG = -1e30

BF = jnp.bfloat16
F32 = jnp.float32
MESH = pl.DeviceIdType.MESH
AXES = ("x", "y", "c")
VMEM_LIMIT = 56 * 1024 * 1024


def _dot(a, b):
    return jnp.dot(a, b, preferred_element_type=F32)


def _dot_nt(a, b):
    return lax.dot_general(a, b, (((1,), (1,)), ((), ())), preferred_element_type=F32)


def _dot_tn(a, b):
    return lax.dot_general(a, b, (((0,), (0,)), ((), ())), preferred_element_type=F32)


def _const_spec(shape):
    nd = len(shape)
    return pl.BlockSpec(shape, lambda *_: (0,) * nd, pipeline_mode=pl.Buffered(1))


def _acc_spec(shape):
    nd = len(shape)
    return pl.BlockSpec(shape, lambda *_: (0,) * nd)


def _row_spec(tm, width):
    return pl.BlockSpec((tm, width), lambda i: (i, 0))


def _params(sem):
    return pltpu.CompilerParams(dimension_semantics=sem, vmem_limit_bytes=VMEM_LIMIT)


def _rot(x, c, s1, s2):
    return x * c + pltpu.roll(x, 96, 1) * s1 + pltpu.roll(x, 32, 1) * s2


def _rot_bwd(d, c, s1, s2):
    return d * c + pltpu.roll(d * s1, 32, 1) + pltpu.roll(d * s2, 96, 1)


def _silu_parts(g):
    sg = jax.nn.sigmoid(g)
    return g * sg, sg * (1.0 + g * (1.0 - sg))


def _rms_bwd(dn, xh, r, g):
    a = dn * g
    return r * (a - xh * jnp.mean(a * xh, axis=-1, keepdims=True))


def _lane_lo(shape):
    return lax.broadcasted_iota(jnp.int32, shape, 1) < HEAD_DIM


def _split4(t):
    lo = _lane_lo(t.shape)
    tr = pltpu.roll(t, HEAD_DIM, 1)
    z = jnp.zeros_like(t)
    return jnp.concatenate([jnp.where(lo, t, z), jnp.where(lo, z, tr), jnp.where(lo, tr, z), jnp.where(lo, z, t)], axis=1)


def _stack_pairs(t, h):
    return jnp.concatenate([t[:, (h * 4 + j) * LANES:(h * 4 + j + 1) * LANES] for j in range(4)], axis=0)


def _upper():
    shape = (CHUNK, 4 * CHUNK)
    return lax.broadcasted_iota(jnp.int32, shape, 0) > (lax.broadcasted_iota(jnp.int32, shape, 1) & (CHUNK - 1))


def _band_rows(ref, prev, cur, h):
    a = slice(2 * h * LANES, (2 * h + 1) * LANES)
    b = slice((2 * h + 1) * LANES, (2 * h + 2) * LANES)
    return jnp.concatenate([ref[pl.ds(prev, CHUNK), a], ref[pl.ds(cur, CHUNK), a],
                            ref[pl.ds(prev, CHUNK), b], ref[pl.ds(cur, CHUNK), b]], axis=0)


def _band_cols(ref, pci, ci, h):
    a = slice(2 * h * LANES, (2 * h + 1) * LANES)
    b = slice((2 * h + 1) * LANES, (2 * h + 2) * LANES)
    return jnp.concatenate([ref[pci, a, :], ref[ci, a, :], ref[pci, b, :], ref[ci, b, :]], axis=1)


def _fold(t, upper, has_prev=None):
    out = []
    for k in range(2):
        prev = t[2 * k * CHUNK:(2 * k + 1) * CHUNK]
        if has_prev is not None:
            prev = jnp.where(has_prev, prev, NEG)
        out.append(jnp.where(upper, prev, t[(2 * k + 1) * CHUNK:(2 * k + 2) * CHUNK]))
    return out


def _unfold(fa, fb, upper):
    z = jnp.zeros_like(fa)
    return jnp.concatenate([jnp.where(upper, fa, z), jnp.where(upper, z, fa),
                            jnp.where(upper, fb, z), jnp.where(upper, z, fb)], axis=0)


def _softmax_sink(f, sink):
    m = jnp.maximum(jnp.max(f, axis=0, keepdims=True), sink)
    p = jnp.exp(f - m)
    es = jnp.exp(sink - m)
    inv = 1.0 / (jnp.sum(p, axis=0, keepdims=True) + es)
    return p * inv, es * inv


def _a_fwd(x, g_a, wa_in, ln_g, ln_b, ws, bs_t, wa_out, g_kv, w_kv, b_kv, rc, rs1, rs2):
    T, D = x.shape
    AW = wa_out.shape[0]
    G = ws.shape[0]
    SH = wa_in.shape[2]
    TM = min(256, T)
    nC = TM // CHUNK

    def body(x_ref, ga_ref, wain_ref, lng_ref, lnb_ref, ws_ref, bst_ref, waout_ref, gkv_ref, wkv_ref, bkv_ref,
             rc_ref, rs1_ref, rs2_ref,
             h1_ref, u_ref, gt_ref, sv_ref, vhat_ref, rstd_ref, k4_ref, v4_ref, kt_ref, vt_ref, z_scr, sv_scr):
        xv = x_ref[...]
        r1 = lax.rsqrt(jnp.mean(xv * xv, axis=-1, keepdims=True) + EPS)
        n1 = (xv * r1 * ga_ref[...]).astype(BF)
        for j in range(N_DEV):
            z_scr[:, j * SH:(j + 1) * SH] = _dot(n1, wain_ref[j])
        u = z_scr[:, :AW]
        v = z_scr[:, AW:2 * AW]
        gt = z_scr[:, 2 * AW:]
        mu = jnp.mean(v, axis=-1, keepdims=True)
        xc = v - mu
        rstd = lax.rsqrt(jnp.mean(xc * xc, axis=-1, keepdims=True) + EPS)
        vhat = xc * rstd
        vln = (vhat * lng_ref[...] + lnb_ref[...]).astype(BF)
        tri = lax.broadcasted_iota(jnp.int32, (CHUNK, CHUNK), 0) >= lax.broadcasted_iota(jnp.int32, (CHUNK, CHUNK), 1)
        for g in range(G):
            wsm = jnp.where(tri, ws_ref[g], 0.0).astype(BF)
            bias = bst_ref[:, g:g + 1]
            for c in range(nC):
                blk = vln[c * CHUNK:(c + 1) * CHUNK, g * CHUNK:(g + 1) * CHUNK]
                sv_scr[c * CHUNK:(c + 1) * CHUNK, g * CHUNK:(g + 1) * CHUNK] = _dot(wsm, blk) + bias
        sv = sv_scr[...]
        silu, _ = _silu_parts(gt)
        y = (u * sv * silu).astype(BF)
        h1 = xv + _dot(y, waout_ref[...])
        h1_ref[...] = h1
        u_ref[...] = u.astype(BF)
        gt_ref[...] = gt.astype(BF)
        sv_ref[...] = sv.astype(BF)
        vhat_ref[...] = vhat.astype(BF)
        rstd_ref[...] = jnp.broadcast_to(rstd, rstd_ref.shape)
        rkv = lax.rsqrt(jnp.mean(h1 * h1, axis=-1, keepdims=True) + EPS)
        nkv = (h1 * rkv * gkv_ref[...]).astype(BF)
        kv = _dot(nkv, wkv_ref[...]) + bkv_ref[...]
        k_rot = _rot(kv[:, :LANES], rc_ref[...], rs1_ref[...], rs2_ref[...])
        for src, ref, tref in ((k_rot, k4_ref, kt_ref), (kv[:, LANES:], v4_ref, vt_ref)):
            t4 = _split4(src)
            ref[...] = t4.astype(BF)
            for c in range(nC):
                for b in range(4):
                    blk = t4[c * CHUNK:(c + 1) * CHUNK, b * LANES:(b + 1) * LANES]
                    tref[c, b * LANES:(b + 1) * LANES, :] = blk.T.astype(BF)

    row = functools.partial(_row_spec, TM)
    tr = pl.BlockSpec((nC, 4 * LANES, CHUNK), lambda i: (i, 0, 0))
    S = jax.ShapeDtypeStruct
    return pl.pallas_call(
        body, name="a_fwd", grid=(T // TM,),
        in_specs=[row(D), _const_spec((1, D)), _const_spec(wa_in.shape), _const_spec((1, AW)), _const_spec((1, AW)),
                  _const_spec(ws.shape), _const_spec(bs_t.shape), _const_spec(wa_out.shape), _const_spec((1, D)),
                  _const_spec(w_kv.shape), _const_spec((1, 2 * LANES)), row(LANES), row(LANES), row(LANES)],
        out_specs=[row(D), row(AW), row(AW), row(AW), row(AW), row(LANES), row(4 * LANES), row(4 * LANES), tr, tr],
        out_shape=(S((T, D), F32), S((T, AW), BF), S((T, AW), BF), S((T, AW), BF), S((T, AW), BF), S((T, LANES), F32),
                   S((T, 4 * LANES), BF), S((T, 4 * LANES), BF),
                   S((T // CHUNK, 4 * LANES, CHUNK), BF), S((T // CHUNK, 4 * LANES, CHUNK), BF)),
        scratch_shapes=[pltpu.VMEM((TM, 3 * AW), F32), pltpu.VMEM((TM, AW), F32)],
        compiler_params=_params(("arbitrary",)),
    )(x, g_a, wa_in, ln_g, ln_b, ws, bs_t, wa_out, g_kv, w_kv, b_kv, rc, rs1, rs2)


def _b_fwd(h1, g_b, wb_in, bq, rc, rs1, rs2, k4, vt, sinks, wb_out, g_f, target):
    T, D = h1.shape
    BW = wb_out.shape[0]
    SH = wb_in.shape[2]
    TM = min(256, T)
    nC = TM // CHUNK
    nP = BW // LANES

    def body(h1_ref, gb_ref, wbin_ref, bq_ref, rc_ref, rs1_ref, rs2_ref, k4_ref, vt_ref, sink_ref, wbout_ref, gf_ref,
             tgt_ref, q_ref, g2_ref, o_ref, dh2_ref, dh2b_ref, loss_ref, dgf_ref, z_scr, o_scr):
        i = pl.program_id(0)
        h1v = h1_ref[...]
        r2 = lax.rsqrt(jnp.mean(h1v * h1v, axis=-1, keepdims=True) + EPS)
        n2 = (h1v * r2 * gb_ref[...]).astype(BF)
        for j in range(N_DEV):
            z_scr[:, j * SH:(j + 1) * SH] = _dot(n2, wbin_ref[j])
        c_t, s1_t, s2_t = rc_ref[...], rs1_ref[...], rs2_ref[...]
        for p in range(nP):
            cols = slice(p * LANES, (p + 1) * LANES)
            qp = _rot(z_scr[:, cols] + bq_ref[:, cols], c_t, s1_t, s2_t) * (HEAD_DIM ** -0.5)
            q_ref[:, cols] = qp.astype(BF)
        g2 = z_scr[:, BW:]
        g2_ref[...] = g2.astype(BF)
        upper = _upper()
        for c in range(nC):
            ci = i * nC + c
            rows = slice(c * CHUNK, (c + 1) * CHUNK)
            pci = jnp.maximum(ci - 1, 0)
            prev = pl.multiple_of(pci * CHUNK, CHUNK)
            cur = pl.multiple_of(ci * CHUNK, CHUNK)
            qc = q_ref[rows, :]
            for h in range(2):
                st = _dot_nt(_band_rows(k4_ref, prev, cur, h), _stack_pairs(qc, h))
                fa, fb = _fold(st, upper, ci > 0)
                pa, _ = _softmax_sink(fa, sink_ref[2 * h:2 * h + 1, :])
                pb, _ = _softmax_sink(fb, sink_ref[2 * h + 1:2 * h + 2, :])
                ot = _dot(_band_cols(vt_ref, pci, ci, h), _unfold(pa, pb, upper).astype(BF))
                for j in range(4):
                    o_scr[rows, (h * 4 + j) * LANES:(h * 4 + j + 1) * LANES] = ot[:, j * CHUNK:(j + 1) * CHUNK].T
        o = o_scr[...]
        o_ref[...] = o.astype(BF)
        silu, _ = _silu_parts(g2)
        h2 = h1v + _dot((o * silu).astype(BF), wbout_ref[...])
        rf = lax.rsqrt(jnp.mean(h2 * h2, axis=-1, keepdims=True) + EPS)
        xh = h2 * rf
        gf = gf_ref[...]
        err = xh * gf - tgt_ref[...]
        dyf = err * (1.0 / D)
        dh2 = _rms_bwd(dyf, xh, rf, gf)
        dh2_ref[...] = dh2
        dh2b_ref[...] = dh2.astype(BF)

        @pl.when(i == 0)
        def _():
            loss_ref[...] = jnp.zeros_like(loss_ref)
            dgf_ref[...] = jnp.zeros_like(dgf_ref)

        loss_ref[...] += 0.5 * jnp.sum(jnp.mean(err * err, axis=-1, keepdims=True), axis=0, keepdims=True)
        dgf_ref[...] += jnp.sum(dyf * xh, axis=0, keepdims=True)

    row = functools.partial(_row_spec, TM)
    S = jax.ShapeDtypeStruct
    return pl.pallas_call(
        body, name="b_fwd", grid=(T // TM,),
        in_specs=[row(D), _const_spec((1, D)), _const_spec(wb_in.shape), _const_spec((1, BW)), row(LANES), row(LANES),
                  row(LANES), _const_spec(k4.shape), _const_spec(vt.shape), _const_spec(sinks.shape),
                  _const_spec(wb_out.shape), _const_spec((1, D)), row(D)],
        out_specs=[row(BW), row(BW), row(BW), row(D), row(D), _acc_spec((1, 1)), _acc_spec((1, D))],
        out_shape=(S((T, BW), BF), S((T, BW), BF), S((T, BW), BF), S((T, D), F32), S((T, D), BF), S((1, 1), F32),
                   S((1, D), F32)),
        scratch_shapes=[pltpu.VMEM((TM, 2 * BW), F32), pltpu.VMEM((TM, BW), F32)],
        compiler_params=_params(("arbitrary",)),
    )(h1, g_b, wb_in, bq, rc, rs1, rs2, k4, vt, sinks, wb_out, g_f, target)


def _b_bwd(dh2, h1, q, g2, o, k4, v4, kt, sinks, wb_out, wb_in, g_b, rc, rs1, rs2):
    T, D = h1.shape
    BW = wb_out.shape[0]
    SH = wb_in.shape[2]
    TM = min(256, T)
    nT = T // TM
    nC = TM // CHUNK
    nP = BW // LANES

    def body(dh2_ref, h1_ref, q_ref, g2_ref, o_ref, k4_ref, v4_ref, kt_ref, sink_ref, wbout_ref, wbin_ref, gb_ref,
             rc_ref, rs1_ref, rs2_ref,
             dh1_ref, dz2_ref, n2_ref, y2_ref, dk_ref, dv_ref, dbq_ref, dgb_ref, dsink_ref, do_scr, dq_scr, dsacc_scr):
        i = pl.program_id(0)

        @pl.when(i == 0)
        def _():
            dk_ref[...] = jnp.zeros_like(dk_ref)
            dv_ref[...] = jnp.zeros_like(dv_ref)
            dbq_ref[...] = jnp.zeros_like(dbq_ref)
            dgb_ref[...] = jnp.zeros_like(dgb_ref)
            dsacc_scr[...] = jnp.zeros_like(dsacc_scr)

        dh2 = dh2_ref[...]
        dy2 = _dot_nt(dh2.astype(BF), wbout_ref[...])
        g2v = g2_ref[...].astype(F32)
        ov = o_ref[...].astype(F32)
        silu, dsilu = _silu_parts(g2v)
        y2_ref[...] = (ov * silu).astype(BF)
        do_scr[...] = (dy2 * silu).astype(BF)
        dz2_ref[:, BW:] = (dy2 * ov * dsilu).astype(BF)
        upper = _upper()
        lo = _lane_lo((2 * CHUNK, LANES))
        for c in range(nC):
            ci = i * nC + c
            rows = slice(c * CHUNK, (c + 1) * CHUNK)
            pci = jnp.maximum(ci - 1, 0)
            prev = pl.multiple_of(pci * CHUNK, CHUNK)
            cur = pl.multiple_of(ci * CHUNK, CHUNK)
            qc = q_ref[rows, :]
            doc = do_scr[rows, :]
            dkb = jnp.zeros((2 * CHUNK, LANES), F32)
            dvb = jnp.zeros((2 * CHUNK, LANES), F32)
            for h in range(2):
                qs = _stack_pairs(qc, h)
                dos = _stack_pairs(doc, h)
                fa, fb = _fold(_dot_nt(_band_rows(k4_ref, prev, cur, h), qs), upper, ci > 0)
                dfa, dfb = _fold(_dot_nt(_band_rows(v4_ref, prev, cur, h), dos), upper)
                folded = []
                for k, (f, df) in enumerate(((fa, dfa), (fb, dfb))):
                    p, ps = _softmax_sink(f, sink_ref[2 * h + k:2 * h + k + 1, :])
                    delta = jnp.sum(p * df, axis=0, keepdims=True)
                    dsacc_scr[2 * h + k:2 * h + k + 1, :] -= ps * delta
                    folded.append((p, p * (df - delta)))
                pt = _unfold(folded[0][0], folded[1][0], upper).astype(BF)
                dst = _unfold(folded[0][1], folded[1][1], upper).astype(BF)
                dqt = _dot(_band_cols(kt_ref, pci, ci, h), dst)
                for j in range(4):
                    dq_scr[rows, (h * 4 + j) * LANES:(h * 4 + j + 1) * LANES] = dqt[:, j * CHUNK:(j + 1) * CHUNK].T
                for acc_name, g in (("k", _dot(dst, qs)), ("v", _dot(pt, dos))):
                    a, b = g[:2 * CHUNK], g[2 * CHUNK:]
                    if h == 0:
                        part = jnp.where(lo, a + pltpu.roll(b, HEAD_DIM, 1), 0.0)
                    else:
                        part = jnp.where(lo, 0.0, pltpu.roll(a, HEAD_DIM, 1) + b)
                    if acc_name == "k":
                        dkb += part
                    else:
                        dvb += part
            dk_ref[pl.ds(prev, CHUNK), :] += dkb[:CHUNK]
            dk_ref[pl.ds(cur, CHUNK), :] += dkb[CHUNK:]
            dv_ref[pl.ds(prev, CHUNK), :] += dvb[:CHUNK]
            dv_ref[pl.ds(cur, CHUNK), :] += dvb[CHUNK:]

        @pl.when(i == nT - 1)
        def _():
            lane = lax.broadcasted_iota(jnp.int32, dsink_ref.shape, 1)
            tot = jnp.zeros(dsink_ref.shape, F32)
            for j in range(4):
                tot += jnp.where(lane == j, jnp.sum(dsacc_scr[:, j * CHUNK:(j + 1) * CHUNK], axis=1, keepdims=True), 0.0)
            dsink_ref[...] = tot
        c_t, s1_t, s2_t = rc_ref[...], rs1_ref[...], rs2_ref[...]
        for p in range(nP):
            cols = slice(p * LANES, (p + 1) * LANES)
            dqp = _rot_bwd(dq_scr[:, cols] * (HEAD_DIM ** -0.5), c_t, s1_t, s2_t)
            dbq_ref[:, cols] += jnp.sum(dqp, axis=0, keepdims=True)
            dz2_ref[:, cols] = dqp.astype(BF)
        h1v = h1_ref[...]
        r2 = lax.rsqrt(jnp.mean(h1v * h1v, axis=-1, keepdims=True) + EPS)
        xh = h1v * r2
        gb = gb_ref[...]
        n2_ref[...] = (xh * gb).astype(BF)
        dn2 = None
        for j in range(N_DEV):
            part = _dot_nt(dz2_ref[:, j * SH:(j + 1) * SH], wbin_ref[j])
            dn2 = part if dn2 is None else dn2 + part
        dgb_ref[...] += jnp.sum(dn2 * xh, axis=0, keepdims=True)
        dh1_ref[...] = dh2 + _rms_bwd(dn2, xh, r2, gb)

    row = functools.partial(_row_spec, TM)
    S = jax.ShapeDtypeStruct
    return pl.pallas_call(
        body, name="b_bwd", grid=(T // TM,),
        in_specs=[row(D), row(D), row(BW), row(BW), row(BW), _const_spec(k4.shape), _const_spec(v4.shape),
                  _const_spec(kt.shape), _const_spec(sinks.shape), _const_spec(wb_out.shape), _const_spec(wb_in.shape),
                  _const_spec((1, D)), row(LANES), row(LANES), row(LANES)],
        out_specs=[row(D), row(2 * BW), row(D), row(BW), _acc_spec((T, LANES)), _acc_spec((T, LANES)),
                   _acc_spec((1, BW)), _acc_spec((1, D)), _acc_spec((4, LANES))],
        out_shape=(S((T, D), F32), S((T, 2 * BW), BF), S((T, D), BF), S((T, BW), BF), S((T, LANES), F32),
                   S((T, LANES), F32), S((1, BW), F32), S((1, D), F32), S((4, LANES), F32)),
        scratch_shapes=[pltpu.VMEM((TM, BW), BF), pltpu.VMEM((TM, BW), F32), pltpu.VMEM((4, 4 * CHUNK), F32)],
        compiler_params=_params(("arbitrary",)),
    )(dh2, h1, q, g2, o, k4, v4, kt, sinks, wb_out, wb_in, g_b, rc, rs1, rs2)


def _a_bwd(dh1p, dk, dv, h1, x, g_kv, w_kv, g_a, wa_out, wa_in, ws, ln_g, ln_b, u, gt, sv, vhat, rstd, rc, rs1, rs2):
    T, D = x.shape
    AW = wa_out.shape[0]
    G = ws.shape[0]
    SH = wa_in.shape[2]
    TM = min(128, T)
    nC = TM // CHUNK

    def body(dh1p_ref, dk_ref, dv_ref, h1_ref, x_ref, gkv_ref, wkv_ref, ga_ref, waout_ref, wain_ref, ws_ref, lng_ref,
             lnb_ref, u_ref, gt_ref, sv_ref, vhat_ref, rstd_ref, rc_ref, rs1_ref, rs2_ref,
             dx_ref, dz_ref, y_ref, n1_ref, nkv_ref, dkv_ref, dh1_ref, dga_ref, dgkv_ref, dbkv_ref, dlng_ref, dlnb_ref,
             dws_ref, dbs_ref, dsv_scr, dvln_scr):
        i = pl.program_id(0)

        @pl.when(i == 0)
        def _():
            for r in (dga_ref, dgkv_ref, dbkv_ref, dlng_ref, dlnb_ref, dws_ref, dbs_ref):
                r[...] = jnp.zeros_like(r)

        dk_pre = _rot_bwd(dk_ref[...], rc_ref[...], rs1_ref[...], rs2_ref[...])
        dkv = jnp.concatenate([dk_pre, dv_ref[...]], axis=1)
        dbkv_ref[...] += jnp.sum(dkv, axis=0, keepdims=True)
        dkv_b = dkv.astype(BF)
        dkv_ref[...] = dkv_b
        h1v = h1_ref[...]
        rkv = lax.rsqrt(jnp.mean(h1v * h1v, axis=-1, keepdims=True) + EPS)
        xh_kv = h1v * rkv
        gkv = gkv_ref[...]
        nkv_ref[...] = (xh_kv * gkv).astype(BF)
        dnkv = _dot_nt(dkv_b, wkv_ref[...])
        dgkv_ref[...] += jnp.sum(dnkv * xh_kv, axis=0, keepdims=True)
        dh1 = dh1p_ref[...] + _rms_bwd(dnkv, xh_kv, rkv, gkv)
        dh1_b = dh1.astype(BF)
        dh1_ref[...] = dh1_b
        dy = _dot_nt(dh1_b, waout_ref[...])
        uv = u_ref[...].astype(F32)
        gtv = gt_ref[...].astype(F32)
        svv = sv_ref[...].astype(F32)
        silu, dsilu = _silu_parts(gtv)
        us = uv * silu
        y_ref[...] = (us * svv).astype(BF)
        dz_ref[:, :AW] = (dy * svv * silu).astype(BF)
        dz_ref[:, 2 * AW:] = (dy * uv * svv * dsilu).astype(BF)
        dsv_scr[...] = (dy * us).astype(BF)
        vhat_v = vhat_ref[...].astype(F32)
        lng = lng_ref[...]
        vln_b = (vhat_v * lng + lnb_ref[...]).astype(BF)
        tri = lax.broadcasted_iota(jnp.int32, (CHUNK, CHUNK), 0) >= lax.broadcasted_iota(jnp.int32, (CHUNK, CHUNK), 1)
        lane = lax.broadcasted_iota(jnp.int32, (CHUNK, LANES), 1)
        dbs = jnp.zeros((CHUNK, LANES), F32)
        for g in range(G):
            wsm = jnp.where(tri, ws_ref[g], 0.0).astype(BF)
            cols = slice(g * CHUNK, (g + 1) * CHUNK)
            dws_g = None
            for c in range(nC):
                rows = slice(c * CHUNK, (c + 1) * CHUNK)
                dsv_cg = dsv_scr[rows, cols]
                dvln_scr[rows, cols] = _dot_tn(wsm, dsv_cg)
                part = _dot_nt(dsv_cg, vln_b[rows, cols])
                dws_g = part if dws_g is None else dws_g + part
                dbs += jnp.where(lane == g, jnp.sum(dsv_cg.astype(F32), axis=-1, keepdims=True), 0.0)
            dws_ref[g] += jnp.where(tri, dws_g, 0.0)
        dbs_ref[...] += dbs
        dvln = dvln_scr[...]
        dlng_ref[...] += jnp.sum(dvln * vhat_v, axis=0, keepdims=True)
        dlnb_ref[...] += jnp.sum(dvln, axis=0, keepdims=True)
        a = dvln * lng
        dvv = rstd_ref[:, 0:1] * (a - jnp.mean(a, axis=-1, keepdims=True)
                                  - vhat_v * jnp.mean(a * vhat_v, axis=-1, keepdims=True))
        dz_ref[:, AW:2 * AW] = dvv.astype(BF)
        xv = x_ref[...]
        r1 = lax.rsqrt(jnp.mean(xv * xv, axis=-1, keepdims=True) + EPS)
        xh = xv * r1
        ga = ga_ref[...]
        n1_ref[...] = (xh * ga).astype(BF)
        dn1 = None
        for j in range(N_DEV):
            part = _dot_nt(dz_ref[:, j * SH:(j + 1) * SH], wain_ref[j])
            dn1 = part if dn1 is None else dn1 + part
        dga_ref[...] += jnp.sum(dn1 * xh, axis=0, keepdims=True)
        dx_ref[...] = dh1 + _rms_bwd(dn1, xh, r1, ga)

    row = functools.partial(_row_spec, TM)
    S = jax.ShapeDtypeStruct
    return pl.pallas_call(
        body, name="a_bwd", grid=(T // TM,),
        in_specs=[row(D), row(LANES), row(LANES), row(D), row(D), _const_spec((1, D)), _const_spec(w_kv.shape),
                  _const_spec((1, D)), _const_spec(wa_out.shape), _const_spec(wa_in.shape), _const_spec(ws.shape),
                  _const_spec((1, AW)), _const_spec((1, AW)), row(AW), row(AW), row(AW), row(AW), row(LANES),
                  row(LANES), row(LANES), row(LANES)],
        out_specs=[row(D), row(3 * AW), row(AW), row(D), row(D), row(2 * LANES), row(D),
                   _acc_spec((1, D)), _acc_spec((1, D)), _acc_spec((1, 2 * LANES)), _acc_spec((1, AW)),
                   _acc_spec((1, AW)), _acc_spec(ws.shape), _acc_spec((CHUNK, LANES))],
        out_shape=(S((T, D), F32), S((T, 3 * AW), BF), S((T, AW), BF), S((T, D), BF), S((T, D), BF),
                   S((T, 2 * LANES), BF), S((T, D), BF),
                   S((1, D), F32), S((1, D), F32), S((1, 2 * LANES), F32), S((1, AW), F32), S((1, AW), F32),
                   S(ws.shape, F32), S((CHUNK, LANES), F32)),
        scratch_shapes=[pltpu.VMEM((TM, AW), BF), pltpu.VMEM((TM, AW), F32)],
        compiler_params=_params(("arbitrary",)),
    )(dh1p, dk, dv, h1, x, g_kv, w_kv, g_a, wa_out, wa_in, ws, ln_g, ln_b, u, gt, sv, vhat, rstd, rc, rs1, rs2)


def _wgrad(a, b, nblk, name):
    T, K = a.shape
    N = b.shape[1] // nblk
    BT = min(512, T)
    nt = T // BT

    def body(a_ref, b_ref, o_ref, acc):
        t = pl.program_id(1)

        @pl.when(t == 0)
        def _():
            acc[...] = jnp.zeros_like(acc)

        acc[...] += _dot_tn(a_ref[...], b_ref[...])

        @pl.when(t == nt - 1)
        def _():
            o_ref[0] = acc[...].astype(BF)

    return pl.pallas_call(
        body, name=name, grid=(nblk, nt),
        in_specs=[pl.BlockSpec((BT, K), lambda j, t: (t, 0)), pl.BlockSpec((BT, N), lambda j, t: (t, j))],
        out_specs=pl.BlockSpec((1, K, N), lambda j, t: (j, 0, 0)),
        out_shape=jax.ShapeDtypeStruct((nblk, K, N), BF),
        scratch_shapes=[pltpu.VMEM((K, N), F32)],
        compiler_params=_params(("arbitrary", "arbitrary")),
    )(a, b)


def _my_index():
    return 4 * lax.axis_index("x") + 2 * lax.axis_index("y") + lax.axis_index("c")


def _all_gather(arrs, dtypes, name):
    n = len(arrs)

    def body(*refs):
        ins, outs = refs[:n], refs[n:2 * n]
        stages = refs[2 * n:3 * n]
        send_sems, recv_sems, local_sems = refs[3 * n:]
        x, y, c = lax.axis_index("x"), lax.axis_index("y"), lax.axis_index("c")
        me, sibling = (x, y, c), (x, y, 1 - c)
        chips = [(1 - x, y), (x, 1 - y), (1 - x, 1 - y)]

        def idx(p):
            return 4 * p[0] + 2 * p[1] + p[2]

        def copy(a, k, block, to, src=None):
            dst = outs[a].at[idx(block)]
            return pltpu.make_async_remote_copy(src_ref=dst if src is None else src, dst_ref=dst,
                                                send_sem=send_sems.at[a, k], recv_sem=recv_sems.at[a, k],
                                                device_id=to, device_id_type=MESH)

        owns, firsts, passed = [], [], []
        for a in range(n):
            stages[a][...] = ins[a][...].astype(stages[a].dtype)
            own = pltpu.make_async_copy(stages[a], outs[a].at[idx(me)], local_sems.at[a])
            own.start()
            owns.append(own)
            first = [copy(a, 0, me, sibling, src=stages[a])]
            first += [copy(a, 1 + j, me, (*chip, c), src=stages[a]) for j, chip in enumerate(chips)]
            for cp in first:
                cp.start()
            firsts += first
        for a in range(n):
            for j, chip in enumerate(chips):
                copy(a, 1 + j, (*chip, c), me).wait_recv()
                fwd = copy(a, 4 + j, (*chip, c), sibling)
                fwd.start()
                passed.append(fwd)
        for a in range(n):
            copy(a, 0, sibling, me).wait_recv()
            for j, chip in enumerate(chips):
                copy(a, 4 + j, (*chip, 1 - c), me).wait_recv()
        for cp in firsts + passed:
            cp.wait_send()
        for own in owns:
            own.wait()

    vm = pl.BlockSpec(memory_space=pltpu.VMEM)
    hbm = pl.BlockSpec(memory_space=pl.ANY)
    return pl.pallas_call(
        body, name=name,
        in_specs=[vm] * n, out_specs=[hbm] * n,
        out_shape=[jax.ShapeDtypeStruct((N_DEV,) + a.shape, dt) for a, dt in zip(arrs, dtypes)],
        scratch_shapes=[pltpu.VMEM(a.shape, dt) for a, dt in zip(arrs, dtypes)]
        + [pltpu.SemaphoreType.DMA((n, 7)), pltpu.SemaphoreType.DMA((n, 7)), pltpu.SemaphoreType.DMA((n,))],
        compiler_params=pltpu.CompilerParams(vmem_limit_bytes=VMEM_LIMIT),
    )(*arrs)


def _all_to_all(arrs, name):
    n = len(arrs)

    def body(*refs):
        ins, outs = refs[:n], refs[n:2 * n]
        send_sems, recv_sems, local_sems = refs[2 * n:]
        x, y, c = lax.axis_index("x"), lax.axis_index("y"), lax.axis_index("c")
        me = 4 * x + 2 * y + c
        copies = []
        for a in range(n):
            own = pltpu.make_async_copy(ins[a].at[me], outs[a].at[me], local_sems.at[a])
            own.start()
            for k in range(1, N_DEV):
                bx, by, bc = (k >> 2) & 1, (k >> 1) & 1, k & 1
                px = jnp.where(bx == 1, 1 - x, x)
                py = jnp.where(by == 1, 1 - y, y)
                pc = jnp.where(bc == 1, 1 - c, c)
                cp = pltpu.make_async_remote_copy(src_ref=ins[a].at[4 * px + 2 * py + pc], dst_ref=outs[a].at[me],
                                                  send_sem=send_sems.at[a, k - 1], recv_sem=recv_sems.at[a, k - 1],
                                                  device_id=(px, py, pc), device_id_type=MESH)
                cp.start()
                copies.append(cp)
        for cp in copies:
            cp.wait_send()
        for a in range(n):
            for k in range(1, N_DEV):
                bx, by, bc = (k >> 2) & 1, (k >> 1) & 1, k & 1
                px = jnp.where(bx == 1, 1 - x, x)
                py = jnp.where(by == 1, 1 - y, y)
                pc = jnp.where(bc == 1, 1 - c, c)
                peer = 4 * px + 2 * py + pc
                pltpu.make_async_remote_copy(src_ref=ins[a].at[me], dst_ref=outs[a].at[peer],
                                             send_sem=send_sems.at[a, k - 1], recv_sem=recv_sems.at[a, k - 1],
                                             device_id=(px, py, pc), device_id_type=MESH).wait_recv()
            pltpu.make_async_copy(ins[a].at[me], outs[a].at[me], local_sems.at[a]).wait()

    hbm = pl.BlockSpec(memory_space=pl.ANY)
    return pl.pallas_call(
        body, name=name,
        in_specs=[hbm] * n, out_specs=[hbm] * n,
        out_shape=[jax.ShapeDtypeStruct(a.shape, a.dtype) for a in arrs],
        scratch_shapes=[pltpu.SemaphoreType.DMA((n, 7)), pltpu.SemaphoreType.DMA((n, 7)), pltpu.SemaphoreType.DMA((n,))],
    )(*arrs)


def _adam_math(w, g, m, v):
    m = ADAM_B1 * m + (1.0 - ADAM_B1) * g
    v = ADAM_B2 * v + (1.0 - ADAM_B2) * (g * g)
    m_hat = m / (1.0 - ADAM_B1 ** ADAM_STEP)
    v_hat = v / (1.0 - ADAM_B2 ** ADAM_STEP)
    delta = -ADAM_LR * (m_hat / (jnp.sqrt(v_hat) + ADAM_EPS) + ADAM_WD * w)
    return delta, m, v


def _sum_adam(parts, w, m, v, name):
    R, C = w.shape
    BR = CHUNK if R % CHUNK == 0 else R

    def body(p_ref, w_ref, m_ref, v_ref, g_ref, d_ref, nm_ref, nv_ref):
        g = p_ref[0].astype(F32)
        for i in range(1, N_DEV):
            g = g + p_ref[i].astype(F32)
        g_ref[...] = g
        d_ref[...], nm_ref[...], nv_ref[...] = _adam_math(w_ref[...], g, m_ref[...], v_ref[...])

    blk = pl.BlockSpec((BR, C), lambda i: (i, 0))
    S = jax.ShapeDtypeStruct((R, C), F32)
    return pl.pallas_call(
        body, name=name, grid=(R // BR,),
        in_specs=[pl.BlockSpec((N_DEV, BR, C), lambda i: (0, i, 0)), blk, blk, blk],
        out_specs=[blk] * 4, out_shape=(S,) * 4,
        compiler_params=_params(("arbitrary",)),
    )(parts, w, m, v)


def _sum8(parts, name):
    _, R, C = parts.shape

    def body(p_ref, o_ref):
        g = p_ref[0]
        for i in range(1, N_DEV):
            g = g + p_ref[i]
        o_ref[...] = g

    return pl.pallas_call(body, name=name, out_shape=jax.ShapeDtypeStruct((R, C), F32))(parts)


def _adam_only(g, w, m, v, name):
    def body(g_ref, w_ref, m_ref, v_ref, d_ref, nm_ref, nv_ref):
        d_ref[...], nm_ref[...], nv_ref[...] = _adam_math(w_ref[...], g_ref[...], m_ref[...], v_ref[...])

    S = jax.ShapeDtypeStruct(w.shape, F32)
    return pl.pallas_call(body, name=name, out_shape=(S,) * 3)(g, w, m, v)


def _rope_tables(T):
    pos = jnp.arange(T, dtype=F32)
    inv_freq = ROPE_THETA ** (-jnp.arange(0, HEAD_DIM, 2, dtype=F32) / HEAD_DIM)
    ang = pos[:, None] * inv_freq[None, :]
    cos, sin, zero = jnp.cos(ang), jnp.sin(ang), jnp.zeros_like(ang)
    c = jnp.concatenate([cos, cos, cos, cos], axis=1)
    s1 = jnp.concatenate([-sin, zero, -sin, zero], axis=1)
    s2 = jnp.concatenate([zero, sin, zero, sin], axis=1)
    return c, s1, s2


SUBLANES = 8


def _nrows(size):
    return -(-size // (SUBLANES * LANES)) * SUBLANES


def _rows(a):
    flat = a.reshape(-1)
    pad = _nrows(flat.shape[0]) * LANES - flat.shape[0]
    if pad:
        flat = jnp.concatenate([flat, jnp.zeros((pad,), flat.dtype)])
    return flat.reshape(-1, LANES)


def _pack(arrs, total_rows):
    rows = [_rows(a) for a in arrs]
    used = sum(r.shape[0] for r in rows)
    if total_rows > used:
        rows.append(jnp.zeros((total_rows - used, LANES), F32))
    return jnp.concatenate(rows, axis=0)


def _unpack(packed, shapes):
    out, at = [], 0
    for shp in shapes:
        size = math.prod(shp)
        nrow = _nrows(size)
        out.append(packed[at:at + nrow].reshape(-1)[:size].reshape(shp))
        at += nrow
    return out


def kernel(x, a_norm_g, a_w_in, a_ln_g, a_ln_b, a_ws, a_bs, a_w_out, kv_norm_g, w_kv, b_kv, b_norm_g, b_w_in, b_bq, b_sinks, b_w_out, final_norm_g, loss_target, m_a_norm_g, m_a_w_in, m_a_ln_g, m_a_ln_b, m_a_ws, m_a_bs, m_a_w_out, m_kv_norm_g, m_w_kv, m_b_kv, m_b_norm_g, m_b_w_in, m_b_bq, m_b_sinks, m_b_w_out, m_final_norm_g, v_a_norm_g, v_a_w_in, v_a_ln_g, v_a_ln_b, v_a_ws, v_a_bs, v_a_w_out, v_kv_norm_g, v_w_kv, v_b_kv, v_b_norm_g, v_b_w_in, v_b_bq, v_b_sinks, v_b_w_out, v_final_norm_g):
    T, D = x.shape[1], x.shape[2]
    AW = a_ln_g.shape[1] * N_DEV
    G = a_ws.shape[1]
    assert w_kv.shape[1] == 2 * LANES and a_ws.shape[2] == CHUNK and T % CHUNK == 0
    me = _my_index()

    vec = jnp.concatenate([a_norm_g, a_ln_g, a_ln_b], axis=1)
    vec = jnp.broadcast_to(vec, (8, vec.shape[1]))
    wa_in, wa_out, wkv, wb_in, wb_out, vecs = _all_gather(
        [a_w_in[0], a_w_out[0], w_kv, b_w_in[0], b_w_out[0], vec], [BF, BF, BF, BF, BF, F32], "gather_weights")
    wa_out = wa_out.reshape(AW, D)
    wkv = wkv.reshape(D, 2 * LANES)
    wb_out = wb_out.reshape(-1, D)
    vecs = vecs[:, 0, :]
    ds = D // N_DEV
    g_a = vecs[:, :ds].reshape(1, D)
    ln_g = vecs[:, ds:ds + AW // N_DEV].reshape(1, AW)
    ln_b = vecs[:, ds + AW // N_DEV:].reshape(1, AW)

    rc, rs1, rs2 = _rope_tables(T)
    ws = a_ws[0]
    bs_t = a_bs[0].T
    g_kv = kv_norm_g.reshape(1, D)
    bkv = b_kv.reshape(1, -1)
    g_f = final_norm_g.reshape(1, D)
    sinks = jnp.repeat(b_sinks.reshape(2, 4, 2).transpose(0, 2, 1).reshape(4, 4), CHUNK, axis=1)
    xs, tgt = x[0], loss_target[0]

    h1, u, gt, sv, vhat, rstd, k4, v4, kt, vt = _a_fwd(xs, g_a, wa_in, ln_g, ln_b, ws, bs_t, wa_out, g_kv, wkv, bkv,
                                                       rc, rs1, rs2)
    q, g2, o, dh2, dh2_b, loss, d_gf = _b_fwd(h1, b_norm_g, wb_in, b_bq, rc, rs1, rs2, k4, vt, sinks, wb_out, g_f, tgt)
    dh1p, dz2, n2, y2, dk, dv, d_bq, d_gb, d_sink = _b_bwd(dh2, h1, q, g2, o, k4, v4, kt, sinks, wb_out, wb_in,
                                                           b_norm_g, rc, rs1, rs2)
    d_sink = d_sink[:, :4].reshape(2, 2, 4).transpose(0, 2, 1).reshape(1, 16)
    (dx, dz, y, n1, nkv, dkv, dh1, d_ga, d_gkv, d_bkv, d_lng, d_lnb, d_ws, d_bst) = _a_bwd(
        dh1p, dk, dv, h1, xs, g_kv, wkv, g_a, wa_out, wa_in, ws, ln_g, ln_b, u, gt, sv, vhat, rstd, rc, rs1, rs2)
    gw_a_in = _wgrad(n1, dz, N_DEV, "wgrad_a_in")
    gw_a_out = _wgrad(y, dh1, 1, "wgrad_a_out").reshape(N_DEV, AW // N_DEV, D)
    gw_kv = _wgrad(nkv, dkv, 1, "wgrad_kv").reshape(N_DEV, D // N_DEV, 2 * LANES)
    gw_b_in = _wgrad(n2, dz2, N_DEV, "wgrad_b_in")
    gw_b_out = _wgrad(y2, dh2_b, 1, "wgrad_b_out").reshape(N_DEV, -1, D)

    small = [d_ws, d_bst[:, :G].T, d_gkv, d_bkv, d_gb, d_bq, d_sink, d_gf, d_ga, d_lng, d_lnb]
    used = sum(_nrows(a.size) for a in small)
    per = -(-used // (SUBLANES * N_DEV)) * SUBLANES
    small_pack = _pack(small, per * N_DEV).reshape(N_DEV, per, LANES)
    r_a_in, r_a_out, r_kv, r_b_in, r_b_out, r_small = _all_to_all(
        [gw_a_in, gw_a_out, gw_kv, gw_b_in, gw_b_out, small_pack], "exchange_grads")

    g_a_in, d_a_in, nm_a_in, nv_a_in = _sum_adam(r_a_in, a_w_in[0], m_a_w_in[0], v_a_w_in[0], "adam_a_in")
    g_a_out, d_a_out, nm_a_out, nv_a_out = _sum_adam(r_a_out, a_w_out[0], m_a_w_out[0], v_a_w_out[0], "adam_a_out")
    g_kvw, d_kvw, nm_kvw, nv_kvw = _sum_adam(r_kv, w_kv, m_w_kv, v_w_kv, "adam_kv")
    g_b_in, d_b_in, nm_b_in, nv_b_in = _sum_adam(r_b_in, b_w_in[0], m_b_w_in[0], v_b_w_in[0], "adam_b_in")
    g_b_out, d_b_out, nm_b_out, nv_b_out = _sum_adam(r_b_out, b_w_out[0], m_b_w_out[0], v_b_w_out[0], "adam_b_out")

    red = _sum8(r_small, "sum_small")
    (full_small,) = _all_gather([red], [F32], "gather_small")
    full_small = full_small.reshape(N_DEV * per, LANES)
    rep_shapes = [a_ws.shape, a_bs.shape, kv_norm_g.shape, b_kv.shape, b_norm_g.shape, b_bq.shape, b_sinks.shape,
                  final_norm_g.shape]
    gs = _unpack(full_small, rep_shapes + [(N_DEV, a_norm_g.shape[1]), (N_DEV, a_ln_g.shape[1]), (N_DEV, a_ln_b.shape[1])])
    g_ang = lax.dynamic_slice_in_dim(gs[8], me, 1, axis=0)
    g_alng = lax.dynamic_slice_in_dim(gs[9], me, 1, axis=0)
    g_alnb = lax.dynamic_slice_in_dim(gs[10], me, 1, axis=0)
    sm_g = gs[:8] + [g_ang, g_alng, g_alnb]
    sm_shapes = [a.shape for a in sm_g]
    tot = sum(_nrows(a.size) for a in sm_g)
    pw = _pack([a_ws, a_bs, kv_norm_g, b_kv, b_norm_g, b_bq, b_sinks, final_norm_g, a_norm_g, a_ln_g, a_ln_b], tot)
    pm = _pack([m_a_ws, m_a_bs, m_kv_norm_g, m_b_kv, m_b_norm_g, m_b_bq, m_b_sinks, m_final_norm_g, m_a_norm_g,
                m_a_ln_g, m_a_ln_b], tot)
    pv = _pack([v_a_ws, v_a_bs, v_kv_norm_g, v_b_kv, v_b_norm_g, v_b_bq, v_b_sinks, v_final_norm_g, v_a_norm_g,
                v_a_ln_g, v_a_ln_b], tot)
    pg = _pack(sm_g, tot)
    pd, pnm, pnv = _adam_only(pg, pw, pm, pv, "adam_small")
    sd, snm, snv = _unpack(pd, sm_shapes), _unpack(pnm, sm_shapes), _unpack(pnv, sm_shapes)

    loss = lax.psum(loss[0, 0], AXES)

    def order(big, sm):
        a_in, a_out, kvw, b_in, b_out = big
        ws_, bs_, kvg, bkv_, bng, bq_, snk, fng, ang, alng, alnb = sm
        return (ang, a_in[None], alng, alnb, ws_, bs_, a_out[None], kvg, kvw, bkv_, bng, b_in[None], bq_, snk,
                b_out[None], fng)

    grads = order((g_a_in, g_a_out, g_kvw, g_b_in, g_b_out), sm_g)
    deltas = order((d_a_in, d_a_out, d_kvw, d_b_in, d_b_out), sd)
    new_m = order((nm_a_in, nm_a_out, nm_kvw, nm_b_in, nm_b_out), snm)
    new_v = order((nv_a_in, nv_a_out, nv_kvw, nv_b_in, nv_b_out), snv)
    return (loss, dx[None], *grads, *deltas, *new_m, *new_v)
```

```python
import functools
import math

import jax
import jax.numpy as jnp
from jax import lax
from jax.experimental import pallas as pl
from jax.experimental.pallas import tpu as pltpu

CHUNK = 128
HEAD_DIM = 64
ROPE_THETA = 10000.0
EPS = 1e-5
ADAM_LR = 0.001
ADAM_B1 = 0.9
ADAM_B2 = 0.999
ADAM_EPS = 1e-08
ADAM_WD = 0.01
ADAM_STEP = 10
N_DEV = 8
LANES = 128
NEG = -1e30

BF = jnp.bfloat16
F32 = jnp.float32
MESH = pl.DeviceIdType.MESH
AXES = ("x", "y", "c")
VMEM_LIMIT = 56 * 1024 * 1024


def _dot(a, b):
    return jnp.dot(a, b, preferred_element_type=F32)


def _dot_nt(a, b):
    return lax.dot_general(a, b, (((1,), (1,)), ((), ())), preferred_element_type=F32)


def _dot_tn(a, b):
    return lax.dot_general(a, b, (((0,), (0,)), ((), ())), preferred_element_type=F32)


def _const_spec(shape):
    nd = len(shape)
    return pl.BlockSpec(shape, lambda *_: (0,) * nd, pipeline_mode=pl.Buffered(1))


def _acc_spec(shape):
    nd = len(shape)
    return pl.BlockSpec(shape, lambda *_: (0,) * nd)


def _row_spec(tm, width):
    return pl.BlockSpec((tm, width), lambda i: (i, 0))


def _params(sem):
    return pltpu.CompilerParams(dimension_semantics=sem, vmem_limit_bytes=VMEM_LIMIT)


def _rot(x, c, s1, s2):
    return x * c + pltpu.roll(x, 96, 1) * s1 + pltpu.roll(x, 32, 1) * s2


def _rot_bwd(d, c, s1, s2):
    return d * c + pltpu.roll(d * s1, 32, 1) + pltpu.roll(d * s2, 96, 1)


def _silu_parts(g):
    sg = jax.nn.sigmoid(g)
    return g * sg, sg * (1.0 + g * (1.0 - sg))


def _rms_bwd(dn, xh, r, g):
    a = dn * g
    return r * (a - xh * jnp.mean(a * xh, axis=-1, keepdims=True))


def _lane_lo(shape):
    return lax.broadcasted_iota(jnp.int32, shape, 1) < HEAD_DIM


def _split4(t):
    lo = _lane_lo(t.shape)
    tr = pltpu.roll(t, HEAD_DIM, 1)
    z = jnp.zeros_like(t)
    return jnp.concatenate([jnp.where(lo, t, z), jnp.where(lo, z, tr), jnp.where(lo, tr, z), jnp.where(lo, z, t)], axis=1)


def _stack_pairs(t, h):
    return jnp.concatenate([t[:, (h * 4 + j) * LANES:(h * 4 + j + 1) * LANES] for j in range(4)], axis=0)


def _upper():
    shape = (CHUNK, 4 * CHUNK)
    return lax.broadcasted_iota(jnp.int32, shape, 0) > (lax.broadcasted_iota(jnp.int32, shape, 1) & (CHUNK - 1))


def _band_rows(ref, prev, cur, h):
    a = slice(2 * h * LANES, (2 * h + 1) * LANES)
    b = slice((2 * h + 1) * LANES, (2 * h + 2) * LANES)
    return jnp.concatenate([ref[pl.ds(prev, CHUNK), a], ref[pl.ds(cur, CHUNK), a],
                            ref[pl.ds(prev, CHUNK), b], ref[pl.ds(cur, CHUNK), b]], axis=0)


def _band_cols(ref, pci, ci, h):
    a = slice(2 * h * LANES, (2 * h + 1) * LANES)
    b = slice((2 * h + 1) * LANES, (2 * h + 2) * LANES)
    return jnp.concatenate([ref[pci, a, :], ref[ci, a, :], ref[pci, b, :], ref[ci, b, :]], axis=1)


def _fold(t, upper, has_prev=None):
    out = []
    for k in range(2):
        prev = t[2 * k * CHUNK:(2 * k + 1) * CHUNK]
        if has_prev is not None:
            prev = jnp.where(has_prev, prev, NEG)
        out.append(jnp.where(upper, prev, t[(2 * k + 1) * CHUNK:(2 * k + 2) * CHUNK]))
    return out


def _unfold(fa, fb, upper):
    z = jnp.zeros_like(fa)
    return jnp.concatenate([jnp.where(upper, fa, z), jnp.where(upper, z, fa),
                            jnp.where(upper, fb, z), jnp.where(upper, z, fb)], axis=0)


def _softmax_sink(f, sink):
    m = jnp.maximum(jnp.max(f, axis=0, keepdims=True), sink)
    p = jnp.exp(f - m)
    es = jnp.exp(sink - m)
    inv = 1.0 / (jnp.sum(p, axis=0, keepdims=True) + es)
    return p * inv, es * inv


def _a_fwd(x, g_a, wa_in, ln_g, ln_b, ws, bs_t, wa_out, g_kv, w_kv, b_kv, rc, rs1, rs2, later):
    T, D = x.shape
    AW = wa_out.shape[0]
    G = ws.shape[0]
    SH = wa_in.shape[2]
    TM = min(256, T)
    nT = T // TM
    nC = TM // CHUNK
    nl = len(later)

    def body(x_ref, ga_ref, wain_ref, lng_ref, lnb_ref, ws_ref, bst_ref, waout_ref, gkv_ref, wkv_ref, bkv_ref,
             rc_ref, rs1_ref, rs2_ref, *rest):
        shards, rest = rest[:nl], rest[nl:]
        (h1_ref, u_ref, gt_ref, sv_ref, vhat_ref, rstd_ref, k4_ref, v4_ref, kt_ref, vt_ref), rest = rest[:10], rest[10:]
        gathered, rest = rest[:nl], rest[nl:]
        z_scr, sv_scr = rest[:2]
        stages, (ssem, rsem, lsem) = rest[2:2 + nl], rest[2 + nl:]
        i = pl.program_id(0)
        gathers = [_Direct(stages[k], gathered[k], ssem.at[k], rsem.at[k], lsem.at[k], scatter=False) for k in range(nl)]

        @pl.when(i == 0)
        def _():
            for k in range(nl):
                stages[k][...] = shards[k][...].astype(BF)
                gathers[k].start()

        xv = x_ref[...]
        r1 = lax.rsqrt(jnp.mean(xv * xv, axis=-1, keepdims=True) + EPS)
        n1 = (xv * r1 * ga_ref[...]).astype(BF)
        for j in range(N_DEV):
            z_scr[:, j * SH:(j + 1) * SH] = _dot(n1, wain_ref[j])
        u = z_scr[:, :AW]
        v = z_scr[:, AW:2 * AW]
        gt = z_scr[:, 2 * AW:]
        mu = jnp.mean(v, axis=-1, keepdims=True)
        xc = v - mu
        rstd = lax.rsqrt(jnp.mean(xc * xc, axis=-1, keepdims=True) + EPS)
        vhat = xc * rstd
        vln = (vhat * lng_ref[...] + lnb_ref[...]).astype(BF)
        tri = lax.broadcasted_iota(jnp.int32, (CHUNK, CHUNK), 0) >= lax.broadcasted_iota(jnp.int32, (CHUNK, CHUNK), 1)
        for g in range(G):
            wsm = jnp.where(tri, ws_ref[g], 0.0).astype(BF)
            bias = bst_ref[:, g:g + 1]
            for c in range(nC):
                blk = vln[c * CHUNK:(c + 1) * CHUNK, g * CHUNK:(g + 1) * CHUNK]
                sv_scr[c * CHUNK:(c + 1) * CHUNK, g * CHUNK:(g + 1) * CHUNK] = _dot(wsm, blk) + bias
        sv = sv_scr[...]
        silu, _ = _silu_parts(gt)
        y = (u * sv * silu).astype(BF)
        h1 = xv + _dot(y, waout_ref[...])
        h1_ref[...] = h1
        u_ref[...] = u.astype(BF)
        gt_ref[...] = gt.astype(BF)
        sv_ref[...] = sv.astype(BF)
        vhat_ref[...] = vhat.astype(BF)
        rstd_ref[...] = jnp.broadcast_to(rstd, rstd_ref.shape)
        rkv = lax.rsqrt(jnp.mean(h1 * h1, axis=-1, keepdims=True) + EPS)
        nkv = (h1 * rkv * gkv_ref[...]).astype(BF)
        kv = _dot(nkv, wkv_ref[...]) + bkv_ref[...]
        k_rot = _rot(kv[:, :LANES], rc_ref[...], rs1_ref[...], rs2_ref[...])
        for src, ref, tref in ((k_rot, k4_ref, kt_ref), (kv[:, LANES:], v4_ref, vt_ref)):
            t4 = _split4(src)
            ref[...] = t4.astype(BF)
            for c in range(nC):
                for b in range(4):
                    blk = t4[c * CHUNK:(c + 1) * CHUNK, b * LANES:(b + 1) * LANES]
                    tref[c, b * LANES:(b + 1) * LANES, :] = blk.T.astype(BF)

        @pl.when(i == nT - 1)
        def _():
            for gth in gathers:
                gth.finish()

    row = functools.partial(_row_spec, TM)
    tr = pl.BlockSpec((nC, 4 * LANES, CHUNK), lambda i: (i, 0, 0))
    hbm = pl.BlockSpec(memory_space=pl.ANY)
    S = jax.ShapeDtypeStruct
    return pl.pallas_call(
        body, name="a_fwd", grid=(nT,),
        in_specs=[row(D), _const_spec((1, D)), _const_spec(wa_in.shape), _const_spec((1, AW)), _const_spec((1, AW)),
                  _const_spec(ws.shape), _const_spec(bs_t.shape), _const_spec(wa_out.shape), _const_spec((1, D)),
                  _const_spec(w_kv.shape), _const_spec((1, 2 * LANES)), row(LANES), row(LANES), row(LANES)]
        + [_const_spec(w.shape) for w in later],
        out_specs=[row(D), row(AW), row(AW), row(AW), row(AW), row(LANES), row(4 * LANES), row(4 * LANES), tr, tr]
        + [hbm] * nl,
        out_shape=(S((T, D), F32), S((T, AW), BF), S((T, AW), BF), S((T, AW), BF), S((T, AW), BF), S((T, LANES), F32),
                   S((T, 4 * LANES), BF), S((T, 4 * LANES), BF),
                   S((T // CHUNK, 4 * LANES, CHUNK), BF), S((T // CHUNK, 4 * LANES, CHUNK), BF))
        + tuple(S((N_DEV,) + w.shape, BF) for w in later),
        scratch_shapes=[pltpu.VMEM((TM, 3 * AW), F32), pltpu.VMEM((TM, AW), F32)]
        + [pltpu.VMEM(w.shape, BF) for w in later] + _direct_sems(nl),
        compiler_params=_params(("arbitrary",)),
    )(x, g_a, wa_in, ln_g, ln_b, ws, bs_t, wa_out, g_kv, w_kv, b_kv, rc, rs1, rs2, *later)


def _b_fwd(h1, g_b, wb_in, bq, rc, rs1, rs2, k4, vt, sinks, wb_out, g_f, target):
    T, D = h1.shape
    BW = wb_out.shape[0]
    SH = wb_in.shape[2]
    TM = min(256, T)
    nC = TM // CHUNK
    nP = BW // LANES

    def body(h1_ref, gb_ref, wbin_ref, bq_ref, rc_ref, rs1_ref, rs2_ref, k4_ref, vt_ref, sink_ref, wbout_ref, gf_ref,
             tgt_ref, q_ref, g2_ref, o_ref, dh2_ref, dh2b_ref, loss_ref, dgf_ref, z_scr, o_scr):
        i = pl.program_id(0)
        h1v = h1_ref[...]
        r2 = lax.rsqrt(jnp.mean(h1v * h1v, axis=-1, keepdims=True) + EPS)
        n2 = (h1v * r2 * gb_ref[...]).astype(BF)
        for j in range(N_DEV):
            z_scr[:, j * SH:(j + 1) * SH] = _dot(n2, wbin_ref[j])
        c_t, s1_t, s2_t = rc_ref[...], rs1_ref[...], rs2_ref[...]
        for p in range(nP):
            cols = slice(p * LANES, (p + 1) * LANES)
            qp = _rot(z_scr[:, cols] + bq_ref[:, cols], c_t, s1_t, s2_t) * (HEAD_DIM ** -0.5)
            q_ref[:, cols] = qp.astype(BF)
        g2 = z_scr[:, BW:]
        g2_ref[...] = g2.astype(BF)
        upper = _upper()
        for c in range(nC):
            ci = i * nC + c
            rows = slice(c * CHUNK, (c + 1) * CHUNK)
            pci = jnp.maximum(ci - 1, 0)
            prev = pl.multiple_of(pci * CHUNK, CHUNK)
            cur = pl.multiple_of(ci * CHUNK, CHUNK)
            qc = q_ref[rows, :]
            for h in range(2):
                st = _dot_nt(_band_rows(k4_ref, prev, cur, h), _stack_pairs(qc, h))
                fa, fb = _fold(st, upper, ci > 0)
                pa, _ = _softmax_sink(fa, sink_ref[2 * h:2 * h + 1, :])
                pb, _ = _softmax_sink(fb, sink_ref[2 * h + 1:2 * h + 2, :])
                ot = _dot(_band_cols(vt_ref, pci, ci, h), _unfold(pa, pb, upper).astype(BF))
                for j in range(4):
                    o_scr[rows, (h * 4 + j) * LANES:(h * 4 + j + 1) * LANES] = ot[:, j * CHUNK:(j + 1) * CHUNK].T
        o = o_scr[...]
        o_ref[...] = o.astype(BF)
        silu, _ = _silu_parts(g2)
        h2 = h1v + _dot((o * silu).astype(BF), wbout_ref[...])
        rf = lax.rsqrt(jnp.mean(h2 * h2, axis=-1, keepdims=True) + EPS)
        xh = h2 * rf
        gf = gf_ref[...]
        err = xh * gf - tgt_ref[...]
        dyf = err * (1.0 / D)
        dh2 = _rms_bwd(dyf, xh, rf, gf)
        dh2_ref[...] = dh2
        dh2b_ref[...] = dh2.astype(BF)

        @pl.when(i == 0)
        def _():
            loss_ref[...] = jnp.zeros_like(loss_ref)
            dgf_ref[...] = jnp.zeros_like(dgf_ref)

        loss_ref[...] += 0.5 * jnp.sum(jnp.mean(err * err, axis=-1, keepdims=True), axis=0, keepdims=True)
        dgf_ref[...] += jnp.sum(dyf * xh, axis=0, keepdims=True)

    row = functools.partial(_row_spec, TM)
    S = jax.ShapeDtypeStruct
    return pl.pallas_call(
        body, name="b_fwd", grid=(T // TM,),
        in_specs=[row(D), _const_spec((1, D)), _const_spec(wb_in.shape), _const_spec((1, BW)), row(LANES), row(LANES),
                  row(LANES), _const_spec(k4.shape), _const_spec(vt.shape), _const_spec(sinks.shape),
                  _const_spec(wb_out.shape), _const_spec((1, D)), row(D)],
        out_specs=[row(BW), row(BW), row(BW), row(D), row(D), _acc_spec((1, 1)), _acc_spec((1, D))],
        out_shape=(S((T, BW), BF), S((T, BW), BF), S((T, BW), BF), S((T, D), F32), S((T, D), BF), S((1, 1), F32),
                   S((1, D), F32)),
        scratch_shapes=[pltpu.VMEM((TM, 2 * BW), F32), pltpu.VMEM((TM, BW), F32)],
        compiler_params=_params(("arbitrary",)),
    )(h1, g_b, wb_in, bq, rc, rs1, rs2, k4, vt, sinks, wb_out, g_f, target)


def _b_bwd(dh2, h1, q, g2, o, k4, v4, kt, sinks, wb_out, wb_in, g_b, rc, rs1, rs2):
    T, D = h1.shape
    BW = wb_out.shape[0]
    SH = wb_in.shape[2]
    TM = min(256, T)
    nT = T // TM
    nC = TM // CHUNK
    nP = BW // LANES

    def body(dh2_ref, h1_ref, q_ref, g2_ref, o_ref, k4_ref, v4_ref, kt_ref, sink_ref, wbout_ref, wbin_ref, gb_ref,
             rc_ref, rs1_ref, rs2_ref,
             dh1_ref, dz2_ref, n2_ref, y2_ref, dk_ref, dv_ref, dbq_ref, dgb_ref, dsink_ref, do_scr, dq_scr, dsacc_scr):
        i = pl.program_id(0)

        @pl.when(i == 0)
        def _():
            dk_ref[...] = jnp.zeros_like(dk_ref)
            dv_ref[...] = jnp.zeros_like(dv_ref)
            dbq_ref[...] = jnp.zeros_like(dbq_ref)
            dgb_ref[...] = jnp.zeros_like(dgb_ref)
            dsacc_scr[...] = jnp.zeros_like(dsacc_scr)

        dh2 = dh2_ref[...]
        dy2 = _dot_nt(dh2.astype(BF), wbout_ref[...])
        g2v = g2_ref[...].astype(F32)
        ov = o_ref[...].astype(F32)
        silu, dsilu = _silu_parts(g2v)
        y2_ref[...] = (ov * silu).astype(BF)
        do_scr[...] = (dy2 * silu).astype(BF)
        dz2_ref[:, BW:] = (dy2 * ov * dsilu).astype(BF)
        upper = _upper()
        lo = _lane_lo((2 * CHUNK, LANES))
        for c in range(nC):
            ci = i * nC + c
            rows = slice(c * CHUNK, (c + 1) * CHUNK)
            pci = jnp.maximum(ci - 1, 0)
            prev = pl.multiple_of(pci * CHUNK, CHUNK)
            cur = pl.multiple_of(ci * CHUNK, CHUNK)
            qc = q_ref[rows, :]
            doc = do_scr[rows, :]
            dkb = jnp.zeros((2 * CHUNK, LANES), F32)
            dvb = jnp.zeros((2 * CHUNK, LANES), F32)
            for h in range(2):
                qs = _stack_pairs(qc, h)
                dos = _stack_pairs(doc, h)
                fa, fb = _fold(_dot_nt(_band_rows(k4_ref, prev, cur, h), qs), upper, ci > 0)
                dfa, dfb = _fold(_dot_nt(_band_rows(v4_ref, prev, cur, h), dos), upper)
                folded = []
                for k, (f, df) in enumerate(((fa, dfa), (fb, dfb))):
                    p, ps = _softmax_sink(f, sink_ref[2 * h + k:2 * h + k + 1, :])
                    delta = jnp.sum(p * df, axis=0, keepdims=True)
                    dsacc_scr[2 * h + k:2 * h + k + 1, :] -= ps * delta
                    folded.append((p, p * (df - delta)))
                pt = _unfold(folded[0][0], folded[1][0], upper).astype(BF)
                dst = _unfold(folded[0][1], folded[1][1], upper).astype(BF)
                dqt = _dot(_band_cols(kt_ref, pci, ci, h), dst)
                for j in range(4):
                    dq_scr[rows, (h * 4 + j) * LANES:(h * 4 + j + 1) * LANES] = dqt[:, j * CHUNK:(j + 1) * CHUNK].T
                for acc_name, g in (("k", _dot(dst, qs)), ("v", _dot(pt, dos))):
                    a, b = g[:2 * CHUNK], g[2 * CHUNK:]
                    if h == 0:
                        part = jnp.where(lo, a + pltpu.roll(b, HEAD_DIM, 1), 0.0)
                    else:
                        part = jnp.where(lo, 0.0, pltpu.roll(a, HEAD_DIM, 1) + b)
                    if acc_name == "k":
                        dkb += part
                    else:
                        dvb += part
            dk_ref[pl.ds(prev, CHUNK), :] += dkb[:CHUNK]
            dk_ref[pl.ds(cur, CHUNK), :] += dkb[CHUNK:]
            dv_ref[pl.ds(prev, CHUNK), :] += dvb[:CHUNK]
            dv_ref[pl.ds(cur, CHUNK), :] += dvb[CHUNK:]

        @pl.when(i == nT - 1)
        def _():
            lane = lax.broadcasted_iota(jnp.int32, dsink_ref.shape, 1)
            tot = jnp.zeros(dsink_ref.shape, F32)
            for j in range(4):
                tot += jnp.where(lane == j, jnp.sum(dsacc_scr[:, j * CHUNK:(j + 1) * CHUNK], axis=1, keepdims=True), 0.0)
            dsink_ref[...] = tot
        c_t, s1_t, s2_t = rc_ref[...], rs1_ref[...], rs2_ref[...]
        for p in range(nP):
            cols = slice(p * LANES, (p + 1) * LANES)
            dqp = _rot_bwd(dq_scr[:, cols] * (HEAD_DIM ** -0.5), c_t, s1_t, s2_t)
            dbq_ref[:, cols] += jnp.sum(dqp, axis=0, keepdims=True)
            dz2_ref[:, cols] = dqp.astype(BF)
        h1v = h1_ref[...]
        r2 = lax.rsqrt(jnp.mean(h1v * h1v, axis=-1, keepdims=True) + EPS)
        xh = h1v * r2
        gb = gb_ref[...]
        n2_ref[...] = (xh * gb).astype(BF)
        dn2 = None
        for j in range(N_DEV):
            part = _dot_nt(dz2_ref[:, j * SH:(j + 1) * SH], wbin_ref[j])
            dn2 = part if dn2 is None else dn2 + part
        dgb_ref[...] += jnp.sum(dn2 * xh, axis=0, keepdims=True)
        dh1_ref[...] = dh2 + _rms_bwd(dn2, xh, r2, gb)

    row = functools.partial(_row_spec, TM)
    S = jax.ShapeDtypeStruct
    return pl.pallas_call(
        body, name="b_bwd", grid=(T // TM,),
        in_specs=[row(D), row(D), row(BW), row(BW), row(BW), _const_spec(k4.shape), _const_spec(v4.shape),
                  _const_spec(kt.shape), _const_spec(sinks.shape), _const_spec(wb_out.shape), _const_spec(wb_in.shape),
                  _const_spec((1, D)), row(LANES), row(LANES), row(LANES)],
        out_specs=[row(D), row(2 * BW), row(D), row(BW), _acc_spec((T, LANES)), _acc_spec((T, LANES)),
                   _acc_spec((1, BW)), _acc_spec((1, D)), _acc_spec((4, LANES))],
        out_shape=(S((T, D), F32), S((T, 2 * BW), BF), S((T, D), BF), S((T, BW), BF), S((T, LANES), F32),
                   S((T, LANES), F32), S((1, BW), F32), S((1, D), F32), S((4, LANES), F32)),
        scratch_shapes=[pltpu.VMEM((TM, BW), BF), pltpu.VMEM((TM, BW), F32), pltpu.VMEM((4, 4 * CHUNK), F32)],
        compiler_params=_params(("arbitrary",)),
    )(dh2, h1, q, g2, o, k4, v4, kt, sinks, wb_out, wb_in, g_b, rc, rs1, rs2)


def _a_bwd(dh1p, dk, dv, h1, x, g_kv, w_kv, g_a, wa_out, wa_in, ws, ln_g, ln_b, u, gt, sv, vhat, rstd, rc, rs1, rs2,
           ready):
    T, D = x.shape
    AW = wa_out.shape[0]
    G = ws.shape[0]
    SH = wa_in.shape[2]
    TM = min(128, T)
    nT = T // TM
    nC = TM // CHUNK
    nr = len(ready)

    def body(dh1p_ref, dk_ref, dv_ref, h1_ref, x_ref, gkv_ref, wkv_ref, ga_ref, waout_ref, wain_ref, ws_ref, lng_ref,
             lnb_ref, u_ref, gt_ref, sv_ref, vhat_ref, rstd_ref, rc_ref, rs1_ref, rs2_ref, *rest):
        ready_refs, rest = rest[:nr], rest[nr:]
        (dx_ref, dz_ref, y_ref, n1_ref, nkv_ref, dkv_ref, dh1_ref, dga_ref, dgkv_ref, dbkv_ref, dlng_ref, dlnb_ref,
         dws_ref, dbs_ref), rest = rest[:14], rest[14:]
        recv_refs, (dsv_scr, dvln_scr, ssem, rsem, lsem) = rest[:nr], rest[nr:]
        i = pl.program_id(0)
        exchanges = [_Direct(ready_refs[k], recv_refs[k], ssem.at[k], rsem.at[k], lsem.at[k], scatter=True)
                     for k in range(nr)]

        @pl.when(i == 0)
        def _():
            for e in exchanges:
                e.start()
            for r in (dga_ref, dgkv_ref, dbkv_ref, dlng_ref, dlnb_ref, dws_ref, dbs_ref):
                r[...] = jnp.zeros_like(r)

        dk_pre = _rot_bwd(dk_ref[...], rc_ref[...], rs1_ref[...], rs2_ref[...])
        dkv = jnp.concatenate([dk_pre, dv_ref[...]], axis=1)
        dbkv_ref[...] += jnp.sum(dkv, axis=0, keepdims=True)
        dkv_b = dkv.astype(BF)
        dkv_ref[...] = dkv_b
        h1v = h1_ref[...]
        rkv = lax.rsqrt(jnp.mean(h1v * h1v, axis=-1, keepdims=True) + EPS)
        xh_kv = h1v * rkv
        gkv = gkv_ref[...]
        nkv_ref[...] = (xh_kv * gkv).astype(BF)
        dnkv = _dot_nt(dkv_b, wkv_ref[...])
        dgkv_ref[...] += jnp.sum(dnkv * xh_kv, axis=0, keepdims=True)
        dh1 = dh1p_ref[...] + _rms_bwd(dnkv, xh_kv, rkv, gkv)
        dh1_b = dh1.astype(BF)
        dh1_ref[...] = dh1_b
        dy = _dot_nt(dh1_b, waout_ref[...])
        uv = u_ref[...].astype(F32)
        gtv = gt_ref[...].astype(F32)
        svv = sv_ref[...].astype(F32)
        silu, dsilu = _silu_parts(gtv)
        us = uv * silu
        y_ref[...] = (us * svv).astype(BF)
        dz_ref[:, :AW] = (dy * svv * silu).astype(BF)
        dz_ref[:, 2 * AW:] = (dy * uv * svv * dsilu).astype(BF)
        dsv_scr[...] = (dy * us).astype(BF)
        vhat_v = vhat_ref[...].astype(F32)
        lng = lng_ref[...]
        vln_b = (vhat_v * lng + lnb_ref[...]).astype(BF)
        tri = lax.broadcasted_iota(jnp.int32, (CHUNK, CHUNK), 0) >= lax.broadcasted_iota(jnp.int32, (CHUNK, CHUNK), 1)
        lane = lax.broadcasted_iota(jnp.int32, (CHUNK, LANES), 1)
        dbs = jnp.zeros((CHUNK, LANES), F32)
        for g in range(G):
            wsm = jnp.where(tri, ws_ref[g], 0.0).astype(BF)
            cols = slice(g * CHUNK, (g + 1) * CHUNK)
            dws_g = None
            for c in range(nC):
                rows = slice(c * CHUNK, (c + 1) * CHUNK)
                dsv_cg = dsv_scr[rows, cols]
                dvln_scr[rows, cols] = _dot_tn(wsm, dsv_cg)
                part = _dot_nt(dsv_cg, vln_b[rows, cols])
                dws_g = part if dws_g is None else dws_g + part
                dbs += jnp.where(lane == g, jnp.sum(dsv_cg.astype(F32), axis=-1, keepdims=True), 0.0)
            dws_ref[g] += jnp.where(tri, dws_g, 0.0)
        dbs_ref[...] += dbs
        dvln = dvln_scr[...]
        dlng_ref[...] += jnp.sum(dvln * vhat_v, axis=0, keepdims=True)
        dlnb_ref[...] += jnp.sum(dvln, axis=0, keepdims=True)
        a = dvln * lng
        dvv = rstd_ref[:, 0:1] * (a - jnp.mean(a, axis=-1, keepdims=True)
                                  - vhat_v * jnp.mean(a * vhat_v, axis=-1, keepdims=True))
        dz_ref[:, AW:2 * AW] = dvv.astype(BF)
        xv = x_ref[...]
        r1 = lax.rsqrt(jnp.mean(xv * xv, axis=-1, keepdims=True) + EPS)
        xh = xv * r1
        ga = ga_ref[...]
        n1_ref[...] = (xh * ga).astype(BF)
        dn1 = None
        for j in range(N_DEV):
            part = _dot_nt(dz_ref[:, j * SH:(j + 1) * SH], wain_ref[j])
            dn1 = part if dn1 is None else dn1 + part
        dga_ref[...] += jnp.sum(dn1 * xh, axis=0, keepdims=True)
        dx_ref[...] = dh1 + _rms_bwd(dn1, xh, r1, ga)

        @pl.when(i == nT - 1)
        def _():
            for e in exchanges:
                e.finish()

    row = functools.partial(_row_spec, TM)
    hbm = pl.BlockSpec(memory_space=pl.ANY)
    S = jax.ShapeDtypeStruct
    return pl.pallas_call(
        body, name="a_bwd", grid=(nT,),
        in_specs=[row(D), row(LANES), row(LANES), row(D), row(D), _const_spec((1, D)), _const_spec(w_kv.shape),
                  _const_spec((1, D)), _const_spec(wa_out.shape), _const_spec(wa_in.shape), _const_spec(ws.shape),
                  _const_spec((1, AW)), _const_spec((1, AW)), row(AW), row(AW), row(AW), row(AW), row(LANES),
                  row(LANES), row(LANES), row(LANES)] + [hbm] * nr,
        out_specs=[row(D), row(3 * AW), row(AW), row(D), row(D), row(2 * LANES), row(D),
                   _acc_spec((1, D)), _acc_spec((1, D)), _acc_spec((1, 2 * LANES)), _acc_spec((1, AW)),
                   _acc_spec((1, AW)), _acc_spec(ws.shape), _acc_spec((CHUNK, LANES))] + [hbm] * nr,
        out_shape=(S((T, D), F32), S((T, 3 * AW), BF), S((T, AW), BF), S((T, D), BF), S((T, D), BF),
                   S((T, 2 * LANES), BF), S((T, D), BF),
                   S((1, D), F32), S((1, D), F32), S((1, 2 * LANES), F32), S((1, AW), F32), S((1, AW), F32),
                   S(ws.shape, F32), S((CHUNK, LANES), F32)) + tuple(S(r.shape, r.dtype) for r in ready),
        scratch_shapes=[pltpu.VMEM((TM, AW), BF), pltpu.VMEM((TM, AW), F32)] + _direct_sems(nr),
        compiler_params=_params(("arbitrary",)),
    )(dh1p, dk, dv, h1, x, g_kv, w_kv, g_a, wa_out, wa_in, ws, ln_g, ln_b, u, gt, sv, vhat, rstd, rc, rs1, rs2, *ready)


def _wgrad(a, b, nblk, name):
    T, K = a.shape
    N = b.shape[1] // nblk
    BT = min(512, T)
    nt = T // BT

    def body(a_ref, b_ref, o_ref, acc):
        t = pl.program_id(1)

        @pl.when(t == 0)
        def _():
            acc[...] = jnp.zeros_like(acc)

        acc[...] += _dot_tn(a_ref[...], b_ref[...])

        @pl.when(t == nt - 1)
        def _():
            o_ref[0] = acc[...].astype(BF)

    return pl.pallas_call(
        body, name=name, grid=(nblk, nt),
        in_specs=[pl.BlockSpec((BT, K), lambda j, t: (t, 0)), pl.BlockSpec((BT, N), lambda j, t: (t, j))],
        out_specs=pl.BlockSpec((1, K, N), lambda j, t: (j, 0, 0)),
        out_shape=jax.ShapeDtypeStruct((nblk, K, N), BF),
        scratch_shapes=[pltpu.VMEM((K, N), F32)],
        compiler_params=_params(("arbitrary", "arbitrary")),
    )(a, b)


def _wgrad_exchange(a, b, me, extras, name):
    T, K = a.shape
    N = b.shape[1] // N_DEV
    BT = min(512, T)
    nt = T // BT
    ne = len(extras)
    last = N_DEV - 1

    def body(me_ref, a_ref, b_ref, *rest):
        ex_in, recv_ref, ex_out = rest[:ne], rest[ne], rest[ne + 1:2 * ne + 1]
        acc, stage, ssem, rsem, lsem, ex_ssem, ex_rsem, ex_lsem = rest[2 * ne + 1:]
        s, t = pl.program_id(0), pl.program_id(1)
        me_i = _my_index()
        ex = [_Direct(ex_in[k], ex_out[k], ex_ssem.at[k], ex_rsem.at[k], ex_lsem.at[k], scatter=True) for k in range(ne)]

        def block_copy(step, slot, dst_slot):
            peer = _peer(last - step)
            return pltpu.make_async_remote_copy(src_ref=stage.at[slot], dst_ref=recv_ref.at[dst_slot],
                                                send_sem=ssem.at[step], recv_sem=rsem.at[step],
                                                device_id=peer, device_id_type=MESH)

        @pl.when((s == 0) & (t == 0))
        def _():
            for e in ex:
                e.start()

        @pl.when(t == 0)
        def _():
            acc[...] = jnp.zeros_like(acc)

        acc[...] += _dot_tn(a_ref[...], b_ref[...])

        @pl.when(t == nt - 1)
        def _():
            slot = lax.rem(s, 2)

            @pl.when(s >= 2)
            def _():
                block_copy(s - 2, slot, me_i).wait_send()

            stage[slot] = acc[...].astype(BF)

            @pl.when(s < last)
            def _():
                block_copy(s, slot, me_i).start()

            @pl.when(s == last)
            def _():
                own = pltpu.make_async_copy(stage.at[slot], recv_ref.at[me_i], lsem)
                own.start()
                block_copy(last - 1, 0, me_i).wait_send()
                for step in range(last):
                    block_copy(step, 0, _dev_index(_peer(last - step))).wait_recv()
                own.wait()
                for e in ex:
                    e.finish()

    hbm = pl.BlockSpec(memory_space=pl.ANY)
    grid_spec = pltpu.PrefetchScalarGridSpec(
        num_scalar_prefetch=1, grid=(N_DEV, nt),
        in_specs=[pl.BlockSpec((BT, K), lambda s, t, me_ref: (t, 0)),
                  pl.BlockSpec((BT, N), lambda s, t, me_ref: (t, me_ref[0] ^ (last - s)))] + [hbm] * ne,
        out_specs=[hbm] * (ne + 1),
        scratch_shapes=[pltpu.VMEM((K, N), F32), pltpu.VMEM((2, K, N), BF), pltpu.SemaphoreType.DMA((last,)),
                        pltpu.SemaphoreType.DMA((last,)), pltpu.SemaphoreType.DMA] + _direct_sems(ne))
    return pl.pallas_call(
        body, name=name, grid_spec=grid_spec,
        out_shape=[jax.ShapeDtypeStruct((N_DEV, K, N), BF)] + [jax.ShapeDtypeStruct(e.shape, e.dtype) for e in extras],
        compiler_params=_params(("arbitrary", "arbitrary")),
    )(me, a, b, *extras)


def _my_index():
    return 4 * lax.axis_index("x") + 2 * lax.axis_index("y") + lax.axis_index("c")


def _all_gather(arrs, dtypes, name):
    n = len(arrs)

    def body(*refs):
        ins, outs = refs[:n], refs[n:2 * n]
        stages = refs[2 * n:3 * n]
        send_sems, recv_sems, local_sems = refs[3 * n:]
        x, y, c = lax.axis_index("x"), lax.axis_index("y"), lax.axis_index("c")
        me, sibling = (x, y, c), (x, y, 1 - c)
        chips = [(1 - x, y), (x, 1 - y), (1 - x, 1 - y)]

        def idx(p):
            return 4 * p[0] + 2 * p[1] + p[2]

        def copy(a, k, block, to, src=None):
            dst = outs[a].at[idx(block)]
            return pltpu.make_async_remote_copy(src_ref=dst if src is None else src, dst_ref=dst,
                                                send_sem=send_sems.at[a, k], recv_sem=recv_sems.at[a, k],
                                                device_id=to, device_id_type=MESH)

        owns, firsts, passed = [], [], []
        for a in range(n):
            stages[a][...] = ins[a][...].astype(stages[a].dtype)
            own = pltpu.make_async_copy(stages[a], outs[a].at[idx(me)], local_sems.at[a])
            own.start()
            owns.append(own)
            first = [copy(a, 0, me, sibling, src=stages[a])]
            first += [copy(a, 1 + j, me, (*chip, c), src=stages[a]) for j, chip in enumerate(chips)]
            for cp in first:
                cp.start()
            firsts += first
        for a in range(n):
            for j, chip in enumerate(chips):
                copy(a, 1 + j, (*chip, c), me).wait_recv()
                fwd = copy(a, 4 + j, (*chip, c), sibling)
                fwd.start()
                passed.append(fwd)
        for a in range(n):
            copy(a, 0, sibling, me).wait_recv()
            for j, chip in enumerate(chips):
                copy(a, 4 + j, (*chip, 1 - c), me).wait_recv()
        for cp in firsts + passed:
            cp.wait_send()
        for own in owns:
            own.wait()

    vm = pl.BlockSpec(memory_space=pltpu.VMEM)
    hbm = pl.BlockSpec(memory_space=pl.ANY)
    return pl.pallas_call(
        body, name=name,
        in_specs=[vm] * n, out_specs=[hbm] * n,
        out_shape=[jax.ShapeDtypeStruct((N_DEV,) + a.shape, dt) for a, dt in zip(arrs, dtypes)],
        scratch_shapes=[pltpu.VMEM(a.shape, dt) for a, dt in zip(arrs, dtypes)]
        + [pltpu.SemaphoreType.DMA((n, 7)), pltpu.SemaphoreType.DMA((n, 7)), pltpu.SemaphoreType.DMA((n,))],
        compiler_params=pltpu.CompilerParams(vmem_limit_bytes=VMEM_LIMIT),
    )(*arrs)


def _peer(mask):
    x, y, c = (lax.axis_index(a) for a in AXES)
    return (x ^ ((mask >> 2) & 1), y ^ ((mask >> 1) & 1), c ^ (mask & 1))


def _dev_index(p):
    return 4 * p[0] + 2 * p[1] + p[2]


class _Direct:
    def __init__(self, src, dst, send_sems, recv_sems, local_sem, scatter):
        me = _my_index()
        self.own = pltpu.make_async_copy(src.at[me] if scatter else src, dst.at[me], local_sem)
        self.sends, self.recvs = [], []
        for k in range(1, N_DEV):
            p = _peer(k)
            pi = _dev_index(p)
            sems = dict(send_sem=send_sems.at[k - 1], recv_sem=recv_sems.at[k - 1], device_id=p, device_id_type=MESH)
            self.sends.append(pltpu.make_async_remote_copy(src_ref=src.at[pi] if scatter else src, dst_ref=dst.at[me],
                                                           **sems))
            self.recvs.append(pltpu.make_async_remote_copy(src_ref=src.at[me] if scatter else src, dst_ref=dst.at[pi],
                                                           **sems))

    def start(self):
        self.own.start()
        for cp in self.sends:
            cp.start()

    def finish(self):
        for cp in self.sends:
            cp.wait_send()
        for cp in self.recvs:
            cp.wait_recv()
        self.own.wait()


def _direct_sems(n):
    return [pltpu.SemaphoreType.DMA((n, 7)), pltpu.SemaphoreType.DMA((n, 7)), pltpu.SemaphoreType.DMA((n,))]


def _adam_math(w, g, m, v):
    m = ADAM_B1 * m + (1.0 - ADAM_B1) * g
    v = ADAM_B2 * v + (1.0 - ADAM_B2) * (g * g)
    m_hat = m / (1.0 - ADAM_B1 ** ADAM_STEP)
    v_hat = v / (1.0 - ADAM_B2 ** ADAM_STEP)
    delta = -ADAM_LR * (m_hat / (jnp.sqrt(v_hat) + ADAM_EPS) + ADAM_WD * w)
    return delta, m, v


def _sum_adam(parts, w, m, v, name):
    R, C = w.shape
    BR = CHUNK if R % CHUNK == 0 else R

    def body(p_ref, w_ref, m_ref, v_ref, g_ref, d_ref, nm_ref, nv_ref):
        g = p_ref[0].astype(F32)
        for i in range(1, N_DEV):
            g = g + p_ref[i].astype(F32)
        g_ref[...] = g
        d_ref[...], nm_ref[...], nv_ref[...] = _adam_math(w_ref[...], g, m_ref[...], v_ref[...])

    blk = pl.BlockSpec((BR, C), lambda i: (i, 0))
    S = jax.ShapeDtypeStruct((R, C), F32)
    return pl.pallas_call(
        body, name=name, grid=(R // BR,),
        in_specs=[pl.BlockSpec((N_DEV, BR, C), lambda i: (0, i, 0)), blk, blk, blk],
        out_specs=[blk] * 4, out_shape=(S,) * 4,
        compiler_params=_params(("arbitrary",)),
    )(parts, w, m, v)


def _sum8(parts, name):
    _, R, C = parts.shape

    def body(p_ref, o_ref):
        g = p_ref[0]
        for i in range(1, N_DEV):
            g = g + p_ref[i]
        o_ref[...] = g

    return pl.pallas_call(body, name=name, out_shape=jax.ShapeDtypeStruct((R, C), F32))(parts)


def _adam_only(g, w, m, v, name):
    def body(g_ref, w_ref, m_ref, v_ref, d_ref, nm_ref, nv_ref):
        d_ref[...], nm_ref[...], nv_ref[...] = _adam_math(w_ref[...], g_ref[...], m_ref[...], v_ref[...])

    S = jax.ShapeDtypeStruct(w.shape, F32)
    return pl.pallas_call(body, name=name, out_shape=(S,) * 3)(g, w, m, v)


def _rope_tables(T):
    pos = jnp.arange(T, dtype=F32)
    inv_freq = ROPE_THETA ** (-jnp.arange(0, HEAD_DIM, 2, dtype=F32) / HEAD_DIM)
    ang = pos[:, None] * inv_freq[None, :]
    cos, sin, zero = jnp.cos(ang), jnp.sin(ang), jnp.zeros_like(ang)
    c = jnp.concatenate([cos, cos, cos, cos], axis=1)
    s1 = jnp.concatenate([-sin, zero, -sin, zero], axis=1)
    s2 = jnp.concatenate([zero, sin, zero, sin], axis=1)
    return c, s1, s2


SUBLANES = 8


def _nrows(size):
    return -(-size // (SUBLANES * LANES)) * SUBLANES


def _rows(a):
    flat = a.reshape(-1)
    pad = _nrows(flat.shape[0]) * LANES - flat.shape[0]
    if pad:
        flat = jnp.concatenate([flat, jnp.zeros((pad,), flat.dtype)])
    return flat.reshape(-1, LANES)


def _pack(arrs, total_rows):
    rows = [_rows(a) for a in arrs]
    used = sum(r.shape[0] for r in rows)
    if total_rows > used:
        rows.append(jnp.zeros((total_rows - used, LANES), F32))
    return jnp.concatenate(rows, axis=0)


def _unpack(packed, shapes):
    out, at = [], 0
    for shp in shapes:
        size = math.prod(shp)
        nrow = _nrows(size)
        out.append(packed[at:at + nrow].reshape(-1)[:size].reshape(shp))
        at += nrow
    return out


def kernel(x, a_norm_g, a_w_in, a_ln_g, a_ln_b, a_ws, a_bs, a_w_out, kv_norm_g, w_kv, b_kv, b_norm_g, b_w_in, b_bq, b_sinks, b_w_out, final_norm_g, loss_target, m_a_norm_g, m_a_w_in, m_a_ln_g, m_a_ln_b, m_a_ws, m_a_bs, m_a_w_out, m_kv_norm_g, m_w_kv, m_b_kv, m_b_norm_g, m_b_w_in, m_b_bq, m_b_sinks, m_b_w_out, m_final_norm_g, v_a_norm_g, v_a_w_in, v_a_ln_g, v_a_ln_b, v_a_ws, v_a_bs, v_a_w_out, v_kv_norm_g, v_w_kv, v_b_kv, v_b_norm_g, v_b_w_in, v_b_bq, v_b_sinks, v_b_w_out, v_final_norm_g):
    T, D = x.shape[1], x.shape[2]
    AW = a_ln_g.shape[1] * N_DEV
    G = a_ws.shape[1]
    assert w_kv.shape[1] == 2 * LANES and a_ws.shape[2] == CHUNK and T % CHUNK == 0
    me = _my_index()

    vec = jnp.concatenate([a_norm_g, a_ln_g, a_ln_b], axis=1)
    vec = jnp.broadcast_to(vec, (8, vec.shape[1]))
    wa_in, wa_out, wkv, vecs = _all_gather([a_w_in[0], a_w_out[0], w_kv, vec], [BF, BF, BF, F32], "gather_weights")
    wa_out = wa_out.reshape(AW, D)
    wkv = wkv.reshape(D, 2 * LANES)
    vecs = vecs[:, 0, :]
    ds = D // N_DEV
    g_a = vecs[:, :ds].reshape(1, D)
    ln_g = vecs[:, ds:ds + AW // N_DEV].reshape(1, AW)
    ln_b = vecs[:, ds + AW // N_DEV:].reshape(1, AW)

    rc, rs1, rs2 = _rope_tables(T)
    ws = a_ws[0]
    bs_t = a_bs[0].T
    g_kv = kv_norm_g.reshape(1, D)
    bkv = b_kv.reshape(1, -1)
    g_f = final_norm_g.reshape(1, D)
    sinks = jnp.repeat(b_sinks.reshape(2, 4, 2).transpose(0, 2, 1).reshape(4, 4), CHUNK, axis=1)
    xs, tgt = x[0], loss_target[0]

    h1, u, gt, sv, vhat, rstd, k4, v4, kt, vt, wb_in, wb_out = _a_fwd(
        xs, g_a, wa_in, ln_g, ln_b, ws, bs_t, wa_out, g_kv, wkv, bkv, rc, rs1, rs2, [b_w_in[0], b_w_out[0]])
    wb_out = wb_out.reshape(-1, D)
    q, g2, o, dh2, dh2_b, loss, d_gf = _b_fwd(h1, b_norm_g, wb_in, b_bq, rc, rs1, rs2, k4, vt, sinks, wb_out, g_f, tgt)
    dh1p, dz2, n2, y2, dk, dv, d_bq, d_gb, d_sink = _b_bwd(dh2, h1, q, g2, o, k4, v4, kt, sinks, wb_out, wb_in,
                                                           b_norm_g, rc, rs1, rs2)
    d_sink = d_sink[:, :4].reshape(2, 2, 4).transpose(0, 2, 1).reshape(1, 16)
    gw_b_in = _wgrad(n2, dz2, N_DEV, "wgrad_b_in")
    gw_b_out = _wgrad(y2, dh2_b, 1, "wgrad_b_out").reshape(N_DEV, -1, D)
    (dx, dz, y, n1, nkv, dkv, dh1, d_ga, d_gkv, d_bkv, d_lng, d_lnb, d_ws, d_bst, r_b_in, r_b_out) = _a_bwd(
        dh1p, dk, dv, h1, xs, g_kv, wkv, g_a, wa_out, wa_in, ws, ln_g, ln_b, u, gt, sv, vhat, rstd, rc, rs1, rs2,
        [gw_b_in, gw_b_out])
    gw_a_out = _wgrad(y, dh1, 1, "wgrad_a_out").reshape(N_DEV, AW // N_DEV, D)
    gw_kv = _wgrad(nkv, dkv, 1, "wgrad_kv").reshape(N_DEV, D // N_DEV, 2 * LANES)
    small = [d_ws, d_bst[:, :G].T, d_gkv, d_bkv, d_gb, d_bq, d_sink, d_gf, d_ga, d_lng, d_lnb]
    used = sum(_nrows(a.size) for a in small)
    per = -(-used // (SUBLANES * N_DEV)) * SUBLANES
    small_pack = _pack(small, per * N_DEV).reshape(N_DEV, per, LANES)
    r_a_in, r_a_out, r_kv, r_small = _wgrad_exchange(n1, dz, me.reshape(1), [gw_a_out, gw_kv, small_pack],
                                                     "wgrad_a_in")

    g_a_in, d_a_in, nm_a_in, nv_a_in = _sum_adam(r_a_in, a_w_in[0], m_a_w_in[0], v_a_w_in[0], "adam_a_in")
    g_a_out, d_a_out, nm_a_out, nv_a_out = _sum_adam(r_a_out, a_w_out[0], m_a_w_out[0], v_a_w_out[0], "adam_a_out")
    g_kvw, d_kvw, nm_kvw, nv_kvw = _sum_adam(r_kv, w_kv, m_w_kv, v_w_kv, "adam_kv")
    g_b_in, d_b_in, nm_b_in, nv_b_in = _sum_adam(r_b_in, b_w_in[0], m_b_w_in[0], v_b_w_in[0], "adam_b_in")
    g_b_out, d_b_out, nm_b_out, nv_b_out = _sum_adam(r_b_out, b_w_out[0], m_b_w_out[0], v_b_w_out[0], "adam_b_out")

    red = _sum8(r_small, "sum_small")
    (full_small,) = _all_gather([red], [F32], "gather_small")
    full_small = full_small.reshape(N_DEV * per, LANES)
    rep_shapes = [a_ws.shape, a_bs.shape, kv_norm_g.shape, b_kv.shape, b_norm_g.shape, b_bq.shape, b_sinks.shape,
                  final_norm_g.shape]
    gs = _unpack(full_small, rep_shapes + [(N_DEV, a_norm_g.shape[1]), (N_DEV, a_ln_g.shape[1]), (N_DEV, a_ln_b.shape[1])])
    g_ang = lax.dynamic_slice_in_dim(gs[8], me, 1, axis=0)
    g_alng = lax.dynamic_slice_in_dim(gs[9], me, 1, axis=0)
    g_alnb = lax.dynamic_slice_in_dim(gs[10], me, 1, axis=0)
    sm_g = gs[:8] + [g_ang, g_alng, g_alnb]
    sm_shapes = [a.shape for a in sm_g]
    tot = sum(_nrows(a.size) for a in sm_g)
    pw = _pack([a_ws, a_bs, kv_norm_g, b_kv, b_norm_g, b_bq, b_sinks, final_norm_g, a_norm_g, a_ln_g, a_ln_b], tot)
    pm = _pack([m_a_ws, m_a_bs, m_kv_norm_g, m_b_kv, m_b_norm_g, m_b_bq, m_b_sinks, m_final_norm_g, m_a_norm_g,
                m_a_ln_g, m_a_ln_b], tot)
    pv = _pack([v_a_ws, v_a_bs, v_kv_norm_g, v_b_kv, v_b_norm_g, v_b_bq, v_b_sinks, v_final_norm_g, v_a_norm_g,
                v_a_ln_g, v_a_ln_b], tot)
    pg = _pack(sm_g, tot)
    pd, pnm, pnv = _adam_only(pg, pw, pm, pv, "adam_small")
    sd, snm, snv = _unpack(pd, sm_shapes), _unpack(pnm, sm_shapes), _unpack(pnv, sm_shapes)

    loss = lax.psum(loss[0, 0], AXES)

    def order(big, sm):
        a_in, a_out, kvw, b_in, b_out = big
        ws_, bs_, kvg, bkv_, bng, bq_, snk, fng, ang, alng, alnb = sm
        return (ang, a_in[None], alng, alnb, ws_, bs_, a_out[None], kvg, kvw, bkv_, bng, b_in[None], bq_, snk,
                b_out[None], fng)

    grads = order((g_a_in, g_a_out, g_kvw, g_b_in, g_b_out), sm_g)
    deltas = order((d_a_in, d_a_out, d_kvw, d_b_in, d_b_out), sd)
    new_m = order((nm_a_in, nm_a_out, nm_kvw, nm_b_in, nm_b_out), snm)
    new_v = order((nv_a_in, nv_a_out, nv_kvw, nv_b_in, nv_b_out), snv)
    return (loss, dx[None], *grads, *deltas, *new_m, *new_v)
```

```python
import functools
import math

import jax
import jax.numpy as jnp
from jax import lax
from jax.experimental import pallas as pl
from jax.experimental.pallas import tpu as pltpu

CHUNK = 128
HEAD_DIM = 64
ROPE_THETA = 10000.0
EPS = 1e-5
ADAM_LR = 0.001
ADAM_B1 = 0.9
ADAM_B2 = 0.999
ADAM_EPS = 1e-08
ADAM_WD = 0.01
ADAM_STEP = 10
N_DEV = 8
LANES = 128
NEG = -1e30

BF = jnp.bfloat16
F32 = jnp.float32
MESH = pl.DeviceIdType.MESH
AXES = ("x", "y", "c")
VMEM_LIMIT = 56 * 1024 * 1024


def _dot(a, b):
    return jnp.dot(a, b, preferred_element_type=F32)


def _dot_nt(a, b):
    return lax.dot_general(a, b, (((1,), (1,)), ((), ())), preferred_element_type=F32)


def _dot_tn(a, b):
    return lax.dot_general(a, b, (((0,), (0,)), ((), ())), preferred_element_type=F32)


def _const_spec(shape):
    nd = len(shape)
    return pl.BlockSpec(shape, lambda *_: (0,) * nd, pipeline_mode=pl.Buffered(1))


def _acc_spec(shape):
    nd = len(shape)
    return pl.BlockSpec(shape, lambda *_: (0,) * nd)


def _row_spec(tm, width):
    return pl.BlockSpec((tm, width), lambda i: (i, 0))


def _col_spec(tm, height):
    return pl.BlockSpec((height, tm), lambda i: (0, i))


def _params(sem):
    return pltpu.CompilerParams(dimension_semantics=sem, vmem_limit_bytes=VMEM_LIMIT)


def _rot(x, c, s1, s2):
    return x * c + pltpu.roll(x, 96, 1) * s1 + pltpu.roll(x, 32, 1) * s2


def _rot_bwd(d, c, s1, s2):
    return d * c + pltpu.roll(d * s1, 32, 1) + pltpu.roll(d * s2, 96, 1)


def _silu_parts(g):
    sg = jax.nn.sigmoid(g)
    return g * sg, sg * (1.0 + g * (1.0 - sg))


def _rms_bwd(dn, xh, r, g):
    a = dn * g
    return r * (a - xh * jnp.mean(a * xh, axis=-1, keepdims=True))


def _lane_lo(shape):
    return lax.broadcasted_iota(jnp.int32, shape, 1) < HEAD_DIM


def _split4(t):
    lo = _lane_lo(t.shape)
    tr = pltpu.roll(t, HEAD_DIM, 1)
    z = jnp.zeros_like(t)
    return jnp.concatenate([jnp.where(lo, t, z), jnp.where(lo, z, tr), jnp.where(lo, tr, z), jnp.where(lo, z, t)], axis=1)


def _stack_pairs(t, h):
    return jnp.concatenate([t[:, (h * 4 + j) * LANES:(h * 4 + j + 1) * LANES] for j in range(4)], axis=0)


def _upper():
    shape = (CHUNK, 4 * CHUNK)
    return lax.broadcasted_iota(jnp.int32, shape, 0) > (lax.broadcasted_iota(jnp.int32, shape, 1) & (CHUNK - 1))


def _band_rows(ref, prev, cur, h):
    a = slice(2 * h * LANES, (2 * h + 1) * LANES)
    b = slice((2 * h + 1) * LANES, (2 * h + 2) * LANES)
    return jnp.concatenate([ref[pl.ds(prev, CHUNK), a], ref[pl.ds(cur, CHUNK), a],
                            ref[pl.ds(prev, CHUNK), b], ref[pl.ds(cur, CHUNK), b]], axis=0)


def _band_cols(ref, pci, ci, h):
    a = slice(2 * h * LANES, (2 * h + 1) * LANES)
    b = slice((2 * h + 1) * LANES, (2 * h + 2) * LANES)
    return jnp.concatenate([ref[pci, a, :], ref[ci, a, :], ref[pci, b, :], ref[ci, b, :]], axis=1)


def _fold(t, upper, has_prev=None):
    out = []
    for k in range(2):
        prev = t[2 * k * CHUNK:(2 * k + 1) * CHUNK]
        if has_prev is not None:
            prev = jnp.where(has_prev, prev, NEG)
        out.append(jnp.where(upper, prev, t[(2 * k + 1) * CHUNK:(2 * k + 2) * CHUNK]))
    return out


def _unfold(fa, fb, upper):
    z = jnp.zeros_like(fa)
    return jnp.concatenate([jnp.where(upper, fa, z), jnp.where(upper, z, fa),
                            jnp.where(upper, fb, z), jnp.where(upper, z, fb)], axis=0)


def _softmax_sink(f, sink):
    m = jnp.maximum(jnp.max(f, axis=0, keepdims=True), sink)
    p = jnp.exp(f - m)
    es = jnp.exp(sink - m)
    inv = 1.0 / (jnp.sum(p, axis=0, keepdims=True) + es)
    return p * inv, es * inv


def _a_fwd(x, g_a, wa_in, ln_g, ln_b, ws, bs_t, wa_out, g_kv, w_kv, b_kv, rc, rs1, rs2, later):
    T, D = x.shape
    AW = wa_out.shape[0]
    G = ws.shape[0]
    SH = wa_in.shape[2]
    TM = min(256, T)
    nT = T // TM
    nC = TM // CHUNK
    nl = len(later)

    def body(x_ref, ga_ref, wain_ref, lng_ref, lnb_ref, ws_ref, bst_ref, waout_ref, gkv_ref, wkv_ref, bkv_ref,
             rc_ref, rs1_ref, rs2_ref, *rest):
        shards, rest = rest[:nl], rest[nl:]
        (h1_ref, u_ref, gt_ref, sv_ref, vhat_ref, rstd_ref, k4_ref, v4_ref, kt_ref, vt_ref), rest = rest[:10], rest[10:]
        gathered, rest = rest[:nl], rest[nl:]
        z_scr, sv_scr = rest[:2]
        stages, (ssem, rsem, lsem) = rest[2:2 + nl], rest[2 + nl:]
        i = pl.program_id(0)
        gathers = [_Direct(stages[k], gathered[k], ssem.at[k], rsem.at[k], lsem.at[k], scatter=False) for k in range(nl)]

        @pl.when(i == 0)
        def _():
            for k in range(nl):
                stages[k][...] = shards[k][...].astype(BF)
                gathers[k].start()

        xv = x_ref[...]
        r1 = lax.rsqrt(jnp.mean(xv * xv, axis=-1, keepdims=True) + EPS)
        n1 = (xv * r1 * ga_ref[...]).astype(BF)
        for j in range(N_DEV):
            z_scr[:, j * SH:(j + 1) * SH] = _dot(n1, wain_ref[j])
        u = z_scr[:, :AW]
        v = z_scr[:, AW:2 * AW]
        gt = z_scr[:, 2 * AW:]
        mu = jnp.mean(v, axis=-1, keepdims=True)
        xc = v - mu
        rstd = lax.rsqrt(jnp.mean(xc * xc, axis=-1, keepdims=True) + EPS)
        vhat = xc * rstd
        vln = (vhat * lng_ref[...] + lnb_ref[...]).astype(BF)
        tri = lax.broadcasted_iota(jnp.int32, (CHUNK, CHUNK), 0) >= lax.broadcasted_iota(jnp.int32, (CHUNK, CHUNK), 1)
        for g in range(G):
            wsm = jnp.where(tri, ws_ref[g], 0.0).astype(BF)
            bias = bst_ref[:, g:g + 1]
            for c in range(nC):
                blk = vln[c * CHUNK:(c + 1) * CHUNK, g * CHUNK:(g + 1) * CHUNK]
                sv_scr[c * CHUNK:(c + 1) * CHUNK, g * CHUNK:(g + 1) * CHUNK] = _dot(wsm, blk) + bias
        sv = sv_scr[...]
        silu, _ = _silu_parts(gt)
        y = (u * sv * silu).astype(BF)
        h1 = xv + _dot(y, waout_ref[...])
        h1_ref[...] = h1
        u_ref[...] = u.astype(BF)
        gt_ref[...] = gt.astype(BF)
        sv_ref[...] = sv.astype(BF)
        vhat_ref[...] = vhat.astype(BF)
        rstd_ref[...] = jnp.broadcast_to(rstd, rstd_ref.shape)
        rkv = lax.rsqrt(jnp.mean(h1 * h1, axis=-1, keepdims=True) + EPS)
        nkv = (h1 * rkv * gkv_ref[...]).astype(BF)
        kv = _dot(nkv, wkv_ref[...]) + bkv_ref[...]
        k_rot = _rot(kv[:, :LANES], rc_ref[...], rs1_ref[...], rs2_ref[...])
        for src, ref, tref in ((k_rot, k4_ref, kt_ref), (kv[:, LANES:], v4_ref, vt_ref)):
            t4 = _split4(src)
            ref[...] = t4.astype(BF)
            for c in range(nC):
                for b in range(4):
                    blk = t4[c * CHUNK:(c + 1) * CHUNK, b * LANES:(b + 1) * LANES]
                    tref[c, b * LANES:(b + 1) * LANES, :] = blk.T.astype(BF)

        @pl.when(i == nT - 1)
        def _():
            for gth in gathers:
                gth.finish()

    row = functools.partial(_row_spec, TM)
    tr = pl.BlockSpec((nC, 4 * LANES, CHUNK), lambda i: (i, 0, 0))
    hbm = pl.BlockSpec(memory_space=pl.ANY)
    S = jax.ShapeDtypeStruct
    return pl.pallas_call(
        body, name="a_fwd", grid=(nT,),
        in_specs=[row(D), _const_spec((1, D)), _const_spec(wa_in.shape), _const_spec((1, AW)), _const_spec((1, AW)),
                  _const_spec(ws.shape), _const_spec(bs_t.shape), _const_spec(wa_out.shape), _const_spec((1, D)),
                  _const_spec(w_kv.shape), _const_spec((1, 2 * LANES)), row(LANES), row(LANES), row(LANES)]
        + [_const_spec(w.shape) for w in later],
        out_specs=[row(D), row(AW), row(AW), row(AW), row(AW), row(LANES), row(4 * LANES), row(4 * LANES), tr, tr]
        + [hbm] * nl,
        out_shape=(S((T, D), F32), S((T, AW), BF), S((T, AW), BF), S((T, AW), BF), S((T, AW), BF), S((T, LANES), F32),
                   S((T, 4 * LANES), BF), S((T, 4 * LANES), BF),
                   S((T // CHUNK, 4 * LANES, CHUNK), BF), S((T // CHUNK, 4 * LANES, CHUNK), BF))
        + tuple(S((N_DEV,) + w.shape, BF) for w in later),
        scratch_shapes=[pltpu.VMEM((TM, 3 * AW), F32), pltpu.VMEM((TM, AW), F32)]
        + [pltpu.VMEM(w.shape, BF) for w in later] + _direct_sems(nl),
        compiler_params=_params(("arbitrary",)),
    )(x, g_a, wa_in, ln_g, ln_b, ws, bs_t, wa_out, g_kv, w_kv, b_kv, rc, rs1, rs2, *later)


def _b_fwd(h1, g_b, wb_in, bq, rc, rs1, rs2, k4, vt, sinks, wb_out, g_f, target):
    T, D = h1.shape
    BW = wb_out.shape[0]
    SH = wb_in.shape[2]
    TM = min(256, T)
    nC = TM // CHUNK
    nP = BW // LANES

    def body(h1_ref, gb_ref, wbin_ref, bq_ref, rc_ref, rs1_ref, rs2_ref, k4_ref, vt_ref, sink_ref, wbout_ref, gf_ref,
             tgt_ref, q_ref, g2_ref, o_ref, dh2_ref, dh2b_ref, loss_ref, dgf_ref, z_scr, o_scr):
        i = pl.program_id(0)
        h1v = h1_ref[...]
        r2 = lax.rsqrt(jnp.mean(h1v * h1v, axis=-1, keepdims=True) + EPS)
        n2 = (h1v * r2 * gb_ref[...]).astype(BF)
        for j in range(N_DEV):
            z_scr[:, j * SH:(j + 1) * SH] = _dot(n2, wbin_ref[j])
        c_t, s1_t, s2_t = rc_ref[...], rs1_ref[...], rs2_ref[...]
        for p in range(nP):
            cols = slice(p * LANES, (p + 1) * LANES)
            qp = _rot(z_scr[:, cols] + bq_ref[:, cols], c_t, s1_t, s2_t) * (HEAD_DIM ** -0.5)
            q_ref[:, cols] = qp.astype(BF)
        g2 = z_scr[:, BW:]
        g2_ref[...] = g2.astype(BF)
        upper = _upper()
        for c in range(nC):
            ci = i * nC + c
            rows = slice(c * CHUNK, (c + 1) * CHUNK)
            pci = jnp.maximum(ci - 1, 0)
            prev = pl.multiple_of(pci * CHUNK, CHUNK)
            cur = pl.multiple_of(ci * CHUNK, CHUNK)
            qc = q_ref[rows, :]
            for h in range(2):
                st = _dot_nt(_band_rows(k4_ref, prev, cur, h), _stack_pairs(qc, h))
                fa, fb = _fold(st, upper, ci > 0)
                pa, _ = _softmax_sink(fa, sink_ref[2 * h:2 * h + 1, :])
                pb, _ = _softmax_sink(fb, sink_ref[2 * h + 1:2 * h + 2, :])
                ot = _dot(_band_cols(vt_ref, pci, ci, h), _unfold(pa, pb, upper).astype(BF))
                for j in range(4):
                    o_scr[rows, (h * 4 + j) * LANES:(h * 4 + j + 1) * LANES] = ot[:, j * CHUNK:(j + 1) * CHUNK].T
        o = o_scr[...]
        o_ref[...] = o.astype(BF)
        silu, _ = _silu_parts(g2)
        h2 = h1v + _dot((o * silu).astype(BF), wbout_ref[...])
        rf = lax.rsqrt(jnp.mean(h2 * h2, axis=-1, keepdims=True) + EPS)
        xh = h2 * rf
        gf = gf_ref[...]
        err = xh * gf - tgt_ref[...]
        dyf = err * (1.0 / D)
        dh2 = _rms_bwd(dyf, xh, rf, gf)
        dh2_ref[...] = dh2
        dh2b_ref[...] = dh2.astype(BF)

        @pl.when(i == 0)
        def _():
            loss_ref[...] = jnp.zeros_like(loss_ref)
            dgf_ref[...] = jnp.zeros_like(dgf_ref)

        loss_ref[...] += 0.5 * jnp.sum(jnp.mean(err * err, axis=-1, keepdims=True), axis=0, keepdims=True)
        dgf_ref[...] += jnp.sum(dyf * xh, axis=0, keepdims=True)

    row = functools.partial(_row_spec, TM)
    S = jax.ShapeDtypeStruct
    return pl.pallas_call(
        body, name="b_fwd", grid=(T // TM,),
        in_specs=[row(D), _const_spec((1, D)), _const_spec(wb_in.shape), _const_spec((1, BW)), row(LANES), row(LANES),
                  row(LANES), _const_spec(k4.shape), _const_spec(vt.shape), _const_spec(sinks.shape),
                  _const_spec(wb_out.shape), _const_spec((1, D)), row(D)],
        out_specs=[row(BW), row(BW), row(BW), row(D), row(D), _acc_spec((1, 1)), _acc_spec((1, D))],
        out_shape=(S((T, BW), BF), S((T, BW), BF), S((T, BW), BF), S((T, D), F32), S((T, D), BF), S((1, 1), F32),
                   S((1, D), F32)),
        scratch_shapes=[pltpu.VMEM((TM, 2 * BW), F32), pltpu.VMEM((TM, BW), F32)],
        compiler_params=_params(("arbitrary",)),
    )(h1, g_b, wb_in, bq, rc, rs1, rs2, k4, vt, sinks, wb_out, g_f, target)


def _b_bwd(dh2, h1, q, g2, o, k4, v4, kt, sinks, wb_out, wb_in, g_b, rc, rs1, rs2):
    T, D = h1.shape
    BW = wb_out.shape[0]
    SH = wb_in.shape[2]
    TM = min(256, T)
    nT = T // TM
    nC = TM // CHUNK
    nP = BW // LANES

    def body(dh2_ref, h1_ref, q_ref, g2_ref, o_ref, k4_ref, v4_ref, kt_ref, sink_ref, wbout_ref, wbin_ref, gb_ref,
             rc_ref, rs1_ref, rs2_ref,
             dh1_ref, dz2_ref, n2_ref, y2_ref, dk_ref, dv_ref, dbq_ref, dgb_ref, dsink_ref, do_scr, dq_scr, dsacc_scr):
        i = pl.program_id(0)

        @pl.when(i == 0)
        def _():
            dk_ref[...] = jnp.zeros_like(dk_ref)
            dv_ref[...] = jnp.zeros_like(dv_ref)
            dbq_ref[...] = jnp.zeros_like(dbq_ref)
            dgb_ref[...] = jnp.zeros_like(dgb_ref)
            dsacc_scr[...] = jnp.zeros_like(dsacc_scr)

        dh2 = dh2_ref[...]
        dy2 = _dot_nt(dh2.astype(BF), wbout_ref[...])
        g2v = g2_ref[...].astype(F32)
        ov = o_ref[...].astype(F32)
        silu, dsilu = _silu_parts(g2v)
        y2_ref[...] = (ov * silu).astype(BF).T
        do_scr[...] = (dy2 * silu).astype(BF)
        dz2_ref[:, BW:] = (dy2 * ov * dsilu).astype(BF)
        upper = _upper()
        lo = _lane_lo((2 * CHUNK, LANES))
        for c in range(nC):
            ci = i * nC + c
            rows = slice(c * CHUNK, (c + 1) * CHUNK)
            pci = jnp.maximum(ci - 1, 0)
            prev = pl.multiple_of(pci * CHUNK, CHUNK)
            cur = pl.multiple_of(ci * CHUNK, CHUNK)
            qc = q_ref[rows, :]
            doc = do_scr[rows, :]
            dkb = jnp.zeros((2 * CHUNK, LANES), F32)
            dvb = jnp.zeros((2 * CHUNK, LANES), F32)
            for h in range(2):
                qs = _stack_pairs(qc, h)
                dos = _stack_pairs(doc, h)
                fa, fb = _fold(_dot_nt(_band_rows(k4_ref, prev, cur, h), qs), upper, ci > 0)
                dfa, dfb = _fold(_dot_nt(_band_rows(v4_ref, prev, cur, h), dos), upper)
                folded = []
                for k, (f, df) in enumerate(((fa, dfa), (fb, dfb))):
                    p, ps = _softmax_sink(f, sink_ref[2 * h + k:2 * h + k + 1, :])
                    delta = jnp.sum(p * df, axis=0, keepdims=True)
                    dsacc_scr[2 * h + k:2 * h + k + 1, :] -= ps * delta
                    folded.append((p, p * (df - delta)))
                pt = _unfold(folded[0][0], folded[1][0], upper).astype(BF)
                dst = _unfold(folded[0][1], folded[1][1], upper).astype(BF)
                dqt = _dot(_band_cols(kt_ref, pci, ci, h), dst)
                for j in range(4):
                    dq_scr[rows, (h * 4 + j) * LANES:(h * 4 + j + 1) * LANES] = dqt[:, j * CHUNK:(j + 1) * CHUNK].T
                for acc_name, g in (("k", _dot(dst, qs)), ("v", _dot(pt, dos))):
                    a, b = g[:2 * CHUNK], g[2 * CHUNK:]
                    if h == 0:
                        part = jnp.where(lo, a + pltpu.roll(b, HEAD_DIM, 1), 0.0)
                    else:
                        part = jnp.where(lo, 0.0, pltpu.roll(a, HEAD_DIM, 1) + b)
                    if acc_name == "k":
                        dkb += part
                    else:
                        dvb += part
            dk_ref[pl.ds(prev, CHUNK), :] += dkb[:CHUNK]
            dk_ref[pl.ds(cur, CHUNK), :] += dkb[CHUNK:]
            dv_ref[pl.ds(prev, CHUNK), :] += dvb[:CHUNK]
            dv_ref[pl.ds(cur, CHUNK), :] += dvb[CHUNK:]

        @pl.when(i == nT - 1)
        def _():
            lane = lax.broadcasted_iota(jnp.int32, dsink_ref.shape, 1)
            tot = jnp.zeros(dsink_ref.shape, F32)
            for j in range(4):
                tot += jnp.where(lane == j, jnp.sum(dsacc_scr[:, j * CHUNK:(j + 1) * CHUNK], axis=1, keepdims=True), 0.0)
            dsink_ref[...] = tot
        c_t, s1_t, s2_t = rc_ref[...], rs1_ref[...], rs2_ref[...]
        for p in range(nP):
            cols = slice(p * LANES, (p + 1) * LANES)
            dqp = _rot_bwd(dq_scr[:, cols] * (HEAD_DIM ** -0.5), c_t, s1_t, s2_t)
            dbq_ref[:, cols] += jnp.sum(dqp, axis=0, keepdims=True)
            dz2_ref[:, cols] = dqp.astype(BF)
        h1v = h1_ref[...]
        r2 = lax.rsqrt(jnp.mean(h1v * h1v, axis=-1, keepdims=True) + EPS)
        xh = h1v * r2
        gb = gb_ref[...]
        n2_ref[...] = (xh * gb).astype(BF).T
        dn2 = None
        for j in range(N_DEV):
            part = _dot_nt(dz2_ref[:, j * SH:(j + 1) * SH], wbin_ref[j])
            dn2 = part if dn2 is None else dn2 + part
        dgb_ref[...] += jnp.sum(dn2 * xh, axis=0, keepdims=True)
        dh1_ref[...] = dh2 + _rms_bwd(dn2, xh, r2, gb)

    row = functools.partial(_row_spec, TM)
    S = jax.ShapeDtypeStruct
    return pl.pallas_call(
        body, name="b_bwd", grid=(T // TM,),
        in_specs=[row(D), row(D), row(BW), row(BW), row(BW), _const_spec(k4.shape), _const_spec(v4.shape),
                  _const_spec(kt.shape), _const_spec(sinks.shape), _const_spec(wb_out.shape), _const_spec(wb_in.shape),
                  _const_spec((1, D)), row(LANES), row(LANES), row(LANES)],
        out_specs=[row(D), row(2 * BW), _col_spec(TM, D), _col_spec(TM, BW), _acc_spec((T, LANES)),
                   _acc_spec((T, LANES)), _acc_spec((1, BW)), _acc_spec((1, D)), _acc_spec((4, LANES))],
        out_shape=(S((T, D), F32), S((T, 2 * BW), BF), S((D, T), BF), S((BW, T), BF), S((T, LANES), F32),
                   S((T, LANES), F32), S((1, BW), F32), S((1, D), F32), S((4, LANES), F32)),
        scratch_shapes=[pltpu.VMEM((TM, BW), BF), pltpu.VMEM((TM, BW), F32), pltpu.VMEM((4, 4 * CHUNK), F32)],
        compiler_params=_params(("arbitrary",)),
    )(dh2, h1, q, g2, o, k4, v4, kt, sinks, wb_out, wb_in, g_b, rc, rs1, rs2)


def _a_bwd(dh1p, dk, dv, h1, x, g_kv, w_kv, g_a, wa_out, wa_in, ws, ln_g, ln_b, u, gt, sv, vhat, rstd, rc, rs1, rs2,
           ready):
    T, D = x.shape
    AW = wa_out.shape[0]
    G = ws.shape[0]
    SH = wa_in.shape[2]
    TM = min(128, T)
    nT = T // TM
    nC = TM // CHUNK
    nr = len(ready)

    def body(dh1p_ref, dk_ref, dv_ref, h1_ref, x_ref, gkv_ref, wkv_ref, ga_ref, waout_ref, wain_ref, ws_ref, lng_ref,
             lnb_ref, u_ref, gt_ref, sv_ref, vhat_ref, rstd_ref, rc_ref, rs1_ref, rs2_ref, *rest):
        ready_refs, rest = rest[:nr], rest[nr:]
        (dx_ref, dz_ref, y_ref, n1_ref, nkv_ref, dkv_ref, dh1_ref, dga_ref, dgkv_ref, dbkv_ref, dlng_ref, dlnb_ref,
         dws_ref, dbs_ref), rest = rest[:14], rest[14:]
        recv_refs, (dsv_scr, dvln_scr, ssem, rsem, lsem) = rest[:nr], rest[nr:]
        i = pl.program_id(0)
        exchanges = [_Direct(ready_refs[k], recv_refs[k], ssem.at[k], rsem.at[k], lsem.at[k], scatter=True)
                     for k in range(nr)]

        @pl.when(i == 0)
        def _():
            for e in exchanges:
                e.start()
            for r in (dga_ref, dgkv_ref, dbkv_ref, dlng_ref, dlnb_ref, dws_ref, dbs_ref):
                r[...] = jnp.zeros_like(r)

        dk_pre = _rot_bwd(dk_ref[...], rc_ref[...], rs1_ref[...], rs2_ref[...])
        dkv = jnp.concatenate([dk_pre, dv_ref[...]], axis=1)
        dbkv_ref[...] += jnp.sum(dkv, axis=0, keepdims=True)
        dkv_b = dkv.astype(BF)
        dkv_ref[...] = dkv_b
        h1v = h1_ref[...]
        rkv = lax.rsqrt(jnp.mean(h1v * h1v, axis=-1, keepdims=True) + EPS)
        xh_kv = h1v * rkv
        gkv = gkv_ref[...]
        nkv_ref[...] = (xh_kv * gkv).astype(BF).T
        dnkv = _dot_nt(dkv_b, wkv_ref[...])
        dgkv_ref[...] += jnp.sum(dnkv * xh_kv, axis=0, keepdims=True)
        dh1 = dh1p_ref[...] + _rms_bwd(dnkv, xh_kv, rkv, gkv)
        dh1_b = dh1.astype(BF)
        dh1_ref[...] = dh1_b
        dy = _dot_nt(dh1_b, waout_ref[...])
        uv = u_ref[...].astype(F32)
        gtv = gt_ref[...].astype(F32)
        svv = sv_ref[...].astype(F32)
        silu, dsilu = _silu_parts(gtv)
        us = uv * silu
        y_ref[...] = (us * svv).astype(BF).T
        dz_ref[:, :AW] = (dy * svv * silu).astype(BF)
        dz_ref[:, 2 * AW:] = (dy * uv * svv * dsilu).astype(BF)
        dsv_scr[...] = (dy * us).astype(BF)
        vhat_v = vhat_ref[...].astype(F32)
        lng = lng_ref[...]
        vln_b = (vhat_v * lng + lnb_ref[...]).astype(BF)
        tri = lax.broadcasted_iota(jnp.int32, (CHUNK, CHUNK), 0) >= lax.broadcasted_iota(jnp.int32, (CHUNK, CHUNK), 1)
        lane = lax.broadcasted_iota(jnp.int32, (CHUNK, LANES), 1)
        dbs = jnp.zeros((CHUNK, LANES), F32)
        for g in range(G):
            wsm = jnp.where(tri, ws_ref[g], 0.0).astype(BF)
            cols = slice(g * CHUNK, (g + 1) * CHUNK)
            dws_g = None
            for c in range(nC):
                rows = slice(c * CHUNK, (c + 1) * CHUNK)
                dsv_cg = dsv_scr[rows, cols]
                dvln_scr[rows, cols] = _dot_tn(wsm, dsv_cg)
                part = _dot_nt(dsv_cg, vln_b[rows, cols])
                dws_g = part if dws_g is None else dws_g + part
                dbs += jnp.where(lane == g, jnp.sum(dsv_cg.astype(F32), axis=-1, keepdims=True), 0.0)
            dws_ref[g] += jnp.where(tri, dws_g, 0.0)
        dbs_ref[...] += dbs
        dvln = dvln_scr[...]
        dlng_ref[...] += jnp.sum(dvln * vhat_v, axis=0, keepdims=True)
        dlnb_ref[...] += jnp.sum(dvln, axis=0, keepdims=True)
        a = dvln * lng
        dvv = rstd_ref[:, 0:1] * (a - jnp.mean(a, axis=-1, keepdims=True)
                                  - vhat_v * jnp.mean(a * vhat_v, axis=-1, keepdims=True))
        dz_ref[:, AW:2 * AW] = dvv.astype(BF)
        xv = x_ref[...]
        r1 = lax.rsqrt(jnp.mean(xv * xv, axis=-1, keepdims=True) + EPS)
        xh = xv * r1
        ga = ga_ref[...]
        n1_ref[...] = (xh * ga).astype(BF).T
        dn1 = None
        for j in range(N_DEV):
            part = _dot_nt(dz_ref[:, j * SH:(j + 1) * SH], wain_ref[j])
            dn1 = part if dn1 is None else dn1 + part
        dga_ref[...] += jnp.sum(dn1 * xh, axis=0, keepdims=True)
        dx_ref[...] = dh1 + _rms_bwd(dn1, xh, r1, ga)

        @pl.when(i == nT - 1)
        def _():
            for e in exchanges:
                e.finish()

    row = functools.partial(_row_spec, TM)
    col = functools.partial(_col_spec, TM)
    hbm = pl.BlockSpec(memory_space=pl.ANY)
    S = jax.ShapeDtypeStruct
    return pl.pallas_call(
        body, name="a_bwd", grid=(nT,),
        in_specs=[row(D), row(LANES), row(LANES), row(D), row(D), _const_spec((1, D)), _const_spec(w_kv.shape),
                  _const_spec((1, D)), _const_spec(wa_out.shape), _const_spec(wa_in.shape), _const_spec(ws.shape),
                  _const_spec((1, AW)), _const_spec((1, AW)), row(AW), row(AW), row(AW), row(AW), row(LANES),
                  row(LANES), row(LANES), row(LANES)] + [hbm] * nr,
        out_specs=[row(D), row(3 * AW), col(AW), col(D), col(D), row(2 * LANES), row(D),
                   _acc_spec((1, D)), _acc_spec((1, D)), _acc_spec((1, 2 * LANES)), _acc_spec((1, AW)),
                   _acc_spec((1, AW)), _acc_spec(ws.shape), _acc_spec((CHUNK, LANES))] + [hbm] * nr,
        out_shape=(S((T, D), F32), S((T, 3 * AW), BF), S((AW, T), BF), S((D, T), BF), S((D, T), BF),
                   S((T, 2 * LANES), BF), S((T, D), BF),
                   S((1, D), F32), S((1, D), F32), S((1, 2 * LANES), F32), S((1, AW), F32), S((1, AW), F32),
                   S(ws.shape, F32), S((CHUNK, LANES), F32)) + tuple(S(r.shape, r.dtype) for r in ready),
        scratch_shapes=[pltpu.VMEM((TM, AW), BF), pltpu.VMEM((TM, AW), F32)] + _direct_sems(nr),
        compiler_params=_params(("arbitrary",)),
    )(dh1p, dk, dv, h1, x, g_kv, w_kv, g_a, wa_out, wa_in, ws, ln_g, ln_b, u, gt, sv, vhat, rstd, rc, rs1, rs2, *ready)


def _wgrad(at, b, nblk, name):
    K, T = at.shape
    N = b.shape[1] // nblk
    BT = min(512, T)
    nt = T // BT

    def body(a_ref, b_ref, o_ref, acc):
        t = pl.program_id(1)

        @pl.when(t == 0)
        def _():
            acc[...] = jnp.zeros_like(acc)

        acc[...] += _dot(a_ref[...], b_ref[...])

        @pl.when(t == nt - 1)
        def _():
            o_ref[0] = acc[...].astype(BF)

    return pl.pallas_call(
        body, name=name, grid=(nblk, nt),
        in_specs=[pl.BlockSpec((K, BT), lambda j, t: (0, t)), pl.BlockSpec((BT, N), lambda j, t: (t, j))],
        out_specs=pl.BlockSpec((1, K, N), lambda j, t: (j, 0, 0)),
        out_shape=jax.ShapeDtypeStruct((nblk, K, N), BF),
        scratch_shapes=[pltpu.VMEM((K, N), F32)],
        compiler_params=_params(("arbitrary", "arbitrary")),
    )(at, b)


def _wgrad_exchange(a, b, me, extras, name):
    K, T = a.shape
    N = b.shape[1] // N_DEV
    BT = min(512, T)
    nt = T // BT
    ne = len(extras)
    last = N_DEV - 1
    n_chip = N_DEV // 2

    def body(me_ref, a_ref, b_ref, *rest):
        ex_in, recv_ref, ex_out = rest[:ne], rest[ne], rest[ne + 1:2 * ne + 1]
        acc, dstage, istage, half, d_s, d_r, i_s, i_r, lsem, ex_ssem, ex_rsem, ex_lsem = rest[2 * ne + 1:]
        s, t = pl.program_id(0), pl.program_id(1)
        x, y, c = (lax.axis_index(ax) for ax in AXES)
        ex = [_Direct(ex_in[k], ex_out[k], ex_ssem.at[k], ex_rsem.at[k], ex_lsem.at[k], scatter=True) for k in range(ne)]

        def to_sibling(k, slot):
            return pltpu.make_async_remote_copy(src_ref=dstage.at[slot], dst_ref=half.at[k], send_sem=d_s.at[k],
                                                recv_sem=d_r.at[k], device_id=(x, y, 1 - c), device_id_type=MESH)

        def to_chip(k, slot, sender):
            far = n_chip - 1 - k
            px, py = x ^ ((far >> 1) & 1), y ^ (far & 1)
            dst = recv_ref.at[2 * x + y] if sender else recv_ref.at[2 * px + py]
            return pltpu.make_async_remote_copy(src_ref=istage.at[slot], dst_ref=dst, send_sem=i_s.at[k],
                                                recv_sem=i_r.at[k], device_id=(px, py, c), device_id_type=MESH)

        @pl.when((s == 0) & (t == 0))
        def _():
            for e in ex:
                e.start()

        @pl.when(t == 0)
        def _():
            acc[...] = jnp.zeros_like(acc)

        acc[...] += _dot(a_ref[...], b_ref[...])

        @pl.when(t == nt - 1)
        def _():
            k = lax.div(s, 2)
            slot = lax.rem(k, 2)

            @pl.when(lax.rem(s, 2) == 0)
            def _():
                @pl.when(k >= 2)
                def _():
                    to_sibling(k - 2, slot).wait_send()

                dstage[slot] = acc[...].astype(BF)
                to_sibling(k, slot).start()

            @pl.when(lax.rem(s, 2) == 1)
            def _():
                to_sibling(k, slot).wait_recv()

                @pl.when(k >= 2)
                def _():
                    to_chip(k - 2, slot, True).wait_send()

                istage[slot] = (acc[...] + half[k].astype(F32)).astype(BF)

                @pl.when(k < n_chip - 1)
                def _():
                    to_chip(k, slot, True).start()

            @pl.when(s == last)
            def _():
                own = pltpu.make_async_copy(istage.at[slot], recv_ref.at[2 * x + y], lsem)
                own.start()
                to_chip(n_chip - 2, 0, True).wait_send()
                to_sibling(n_chip - 2, 0).wait_send()
                to_sibling(n_chip - 1, 1).wait_send()
                for kk in range(n_chip - 1):
                    to_chip(kk, 0, False).wait_recv()
                own.wait()
                for e in ex:
                    e.finish()

    hbm = pl.BlockSpec(memory_space=pl.ANY)
    dma = pltpu.SemaphoreType.DMA
    grid_spec = pltpu.PrefetchScalarGridSpec(
        num_scalar_prefetch=1, grid=(N_DEV, nt),
        in_specs=[pl.BlockSpec((K, BT), lambda s, t, me_ref: (0, t)),
                  pl.BlockSpec((BT, N), lambda s, t, me_ref: (t, me_ref[0] ^ (last - s)))] + [hbm] * ne,
        out_specs=[hbm] * (ne + 1),
        scratch_shapes=[pltpu.VMEM((K, N), F32), pltpu.VMEM((2, K, N), BF), pltpu.VMEM((2, K, N), BF),
                        pltpu.VMEM((n_chip, K, N), BF), dma((n_chip,)), dma((n_chip,)), dma((n_chip - 1,)),
                        dma((n_chip - 1,)), dma] + _direct_sems(ne))
    return pl.pallas_call(
        body, name=name, grid_spec=grid_spec,
        out_shape=[jax.ShapeDtypeStruct((n_chip, K, N), BF)] + [jax.ShapeDtypeStruct(e.shape, e.dtype) for e in extras],
        compiler_params=_params(("arbitrary", "arbitrary")),
    )(me, a, b, *extras)


def _my_index():
    return 4 * lax.axis_index("x") + 2 * lax.axis_index("y") + lax.axis_index("c")


def _all_gather(arrs, dtypes, name):
    n = len(arrs)

    def body(*refs):
        ins, outs = refs[:n], refs[n:2 * n]
        stages = refs[2 * n:3 * n]
        send_sems, recv_sems, local_sems = refs[3 * n:]
        x, y, c = lax.axis_index("x"), lax.axis_index("y"), lax.axis_index("c")
        me, sibling = (x, y, c), (x, y, 1 - c)
        chips = [(1 - x, y), (x, 1 - y), (1 - x, 1 - y)]

        def idx(p):
            return 4 * p[0] + 2 * p[1] + p[2]

        def copy(a, k, block, to, src=None):
            dst = outs[a].at[idx(block)]
            return pltpu.make_async_remote_copy(src_ref=dst if src is None else src, dst_ref=dst,
                                                send_sem=send_sems.at[a, k], recv_sem=recv_sems.at[a, k],
                                                device_id=to, device_id_type=MESH)

        owns, firsts, passed = [], [], []
        for a in range(n):
            stages[a][...] = ins[a][...].astype(stages[a].dtype)
            own = pltpu.make_async_copy(stages[a], outs[a].at[idx(me)], local_sems.at[a])
            own.start()
            owns.append(own)
            first = [copy(a, 0, me, sibling, src=stages[a])]
            first += [copy(a, 1 + j, me, (*chip, c), src=stages[a]) for j, chip in enumerate(chips)]
            for cp in first:
                cp.start()
            firsts += first
        for a in range(n):
            for j, chip in enumerate(chips):
                copy(a, 1 + j, (*chip, c), me).wait_recv()
                fwd = copy(a, 4 + j, (*chip, c), sibling)
                fwd.start()
                passed.append(fwd)
        for a in range(n):
            copy(a, 0, sibling, me).wait_recv()
            for j, chip in enumerate(chips):
                copy(a, 4 + j, (*chip, 1 - c), me).wait_recv()
        for cp in firsts + passed:
            cp.wait_send()
        for own in owns:
            own.wait()

    vm = pl.BlockSpec(memory_space=pltpu.VMEM)
    hbm = pl.BlockSpec(memory_space=pl.ANY)
    return pl.pallas_call(
        body, name=name,
        in_specs=[vm] * n, out_specs=[hbm] * n,
        out_shape=[jax.ShapeDtypeStruct((N_DEV,) + a.shape, dt) for a, dt in zip(arrs, dtypes)],
        scratch_shapes=[pltpu.VMEM(a.shape, dt) for a, dt in zip(arrs, dtypes)]
        + [pltpu.SemaphoreType.DMA((n, 7)), pltpu.SemaphoreType.DMA((n, 7)), pltpu.SemaphoreType.DMA((n,))],
        compiler_params=pltpu.CompilerParams(vmem_limit_bytes=VMEM_LIMIT),
    )(*arrs)


def _peer(mask):
    x, y, c = (lax.axis_index(a) for a in AXES)
    return (x ^ ((mask >> 2) & 1), y ^ ((mask >> 1) & 1), c ^ (mask & 1))


def _dev_index(p):
    return 4 * p[0] + 2 * p[1] + p[2]


class _Direct:
    def __init__(self, src, dst, send_sems, recv_sems, local_sem, scatter):
        me = _my_index()
        self.own = pltpu.make_async_copy(src.at[me] if scatter else src, dst.at[me], local_sem)
        self.sends, self.recvs = [], []
        for k in range(1, N_DEV):
            p = _peer(k)
            pi = _dev_index(p)
            sems = dict(send_sem=send_sems.at[k - 1], recv_sem=recv_sems.at[k - 1], device_id=p, device_id_type=MESH)
            self.sends.append(pltpu.make_async_remote_copy(src_ref=src.at[pi] if scatter else src, dst_ref=dst.at[me],
                                                           **sems))
            self.recvs.append(pltpu.make_async_remote_copy(src_ref=src.at[me] if scatter else src, dst_ref=dst.at[pi],
                                                           **sems))

    def start(self):
        self.own.start()
        for cp in self.sends:
            cp.start()

    def finish(self):
        for cp in self.sends:
            cp.wait_send()
        for cp in self.recvs:
            cp.wait_recv()
        self.own.wait()


def _direct_sems(n):
    return [pltpu.SemaphoreType.DMA((n, 7)), pltpu.SemaphoreType.DMA((n, 7)), pltpu.SemaphoreType.DMA((n,))]


def _adam_math(w, g, m, v):
    m = ADAM_B1 * m + (1.0 - ADAM_B1) * g
    v = ADAM_B2 * v + (1.0 - ADAM_B2) * (g * g)
    m_hat = m / (1.0 - ADAM_B1 ** ADAM_STEP)
    v_hat = v / (1.0 - ADAM_B2 ** ADAM_STEP)
    delta = -ADAM_LR * (m_hat / (jnp.sqrt(v_hat) + ADAM_EPS) + ADAM_WD * w)
    return delta, m, v


def _sum_adam(parts, w, m, v, name):
    R, C = w.shape
    NP = parts.shape[0]
    BR = CHUNK if R % CHUNK == 0 else R

    def body(p_ref, w_ref, m_ref, v_ref, g_ref, d_ref, nm_ref, nv_ref):
        g = p_ref[0].astype(F32)
        for i in range(1, NP):
            g = g + p_ref[i].astype(F32)
        g_ref[...] = g
        d_ref[...], nm_ref[...], nv_ref[...] = _adam_math(w_ref[...], g, m_ref[...], v_ref[...])

    blk = pl.BlockSpec((BR, C), lambda i: (i, 0))
    S = jax.ShapeDtypeStruct((R, C), F32)
    return pl.pallas_call(
        body, name=name, grid=(R // BR,),
        in_specs=[pl.BlockSpec((NP, BR, C), lambda i: (0, i, 0)), blk, blk, blk],
        out_specs=[blk] * 4, out_shape=(S,) * 4,
        compiler_params=_params(("arbitrary",)),
    )(parts, w, m, v)


def _sum8(parts, name):
    _, R, C = parts.shape

    def body(p_ref, o_ref):
        g = p_ref[0]
        for i in range(1, N_DEV):
            g = g + p_ref[i]
        o_ref[...] = g

    return pl.pallas_call(body, name=name, out_shape=jax.ShapeDtypeStruct((R, C), F32))(parts)


def _adam_only(g, w, m, v, name):
    def body(g_ref, w_ref, m_ref, v_ref, d_ref, nm_ref, nv_ref):
        d_ref[...], nm_ref[...], nv_ref[...] = _adam_math(w_ref[...], g_ref[...], m_ref[...], v_ref[...])

    S = jax.ShapeDtypeStruct(w.shape, F32)
    return pl.pallas_call(body, name=name, out_shape=(S,) * 3)(g, w, m, v)


def _rope_tables(T):
    pos = jnp.arange(T, dtype=F32)
    inv_freq = ROPE_THETA ** (-jnp.arange(0, HEAD_DIM, 2, dtype=F32) / HEAD_DIM)
    ang = pos[:, None] * inv_freq[None, :]
    cos, sin, zero = jnp.cos(ang), jnp.sin(ang), jnp.zeros_like(ang)
    c = jnp.concatenate([cos, cos, cos, cos], axis=1)
    s1 = jnp.concatenate([-sin, zero, -sin, zero], axis=1)
    s2 = jnp.concatenate([zero, sin, zero, sin], axis=1)
    return c, s1, s2


SUBLANES = 8


def _nrows(size):
    return -(-size // (SUBLANES * LANES)) * SUBLANES


def _rows(a):
    flat = a.reshape(-1)
    pad = _nrows(flat.shape[0]) * LANES - flat.shape[0]
    if pad:
        flat = jnp.concatenate([flat, jnp.zeros((pad,), flat.dtype)])
    return flat.reshape(-1, LANES)


def _pack(arrs, total_rows):
    rows = [_rows(a) for a in arrs]
    used = sum(r.shape[0] for r in rows)
    if total_rows > used:
        rows.append(jnp.zeros((total_rows - used, LANES), F32))
    return jnp.concatenate(rows, axis=0)


def _unpack(packed, shapes):
    out, at = [], 0
    for shp in shapes:
        size = math.prod(shp)
        nrow = _nrows(size)
        out.append(packed[at:at + nrow].reshape(-1)[:size].reshape(shp))
        at += nrow
    return out


def kernel(x, a_norm_g, a_w_in, a_ln_g, a_ln_b, a_ws, a_bs, a_w_out, kv_norm_g, w_kv, b_kv, b_norm_g, b_w_in, b_bq, b_sinks, b_w_out, final_norm_g, loss_target, m_a_norm_g, m_a_w_in, m_a_ln_g, m_a_ln_b, m_a_ws, m_a_bs, m_a_w_out, m_kv_norm_g, m_w_kv, m_b_kv, m_b_norm_g, m_b_w_in, m_b_bq, m_b_sinks, m_b_w_out, m_final_norm_g, v_a_norm_g, v_a_w_in, v_a_ln_g, v_a_ln_b, v_a_ws, v_a_bs, v_a_w_out, v_kv_norm_g, v_w_kv, v_b_kv, v_b_norm_g, v_b_w_in, v_b_bq, v_b_sinks, v_b_w_out, v_final_norm_g):
    T, D = x.shape[1], x.shape[2]
    AW = a_ln_g.shape[1] * N_DEV
    G = a_ws.shape[1]
    assert w_kv.shape[1] == 2 * LANES and a_ws.shape[2] == CHUNK and T % CHUNK == 0
    me = _my_index()

    vec = jnp.concatenate([a_norm_g, a_ln_g, a_ln_b], axis=1)
    vec = jnp.broadcast_to(vec, (8, vec.shape[1]))
    wa_in, wa_out, wkv, vecs = _all_gather([a_w_in[0], a_w_out[0], w_kv, vec], [BF, BF, BF, F32], "gather_weights")
    wa_out = wa_out.reshape(AW, D)
    wkv = wkv.reshape(D, 2 * LANES)
    vecs = vecs[:, 0, :]
    ds = D // N_DEV
    g_a = vecs[:, :ds].reshape(1, D)
    ln_g = vecs[:, ds:ds + AW // N_DEV].reshape(1, AW)
    ln_b = vecs[:, ds + AW // N_DEV:].reshape(1, AW)

    rc, rs1, rs2 = _rope_tables(T)
    ws = a_ws[0]
    bs_t = a_bs[0].T
    g_kv = kv_norm_g.reshape(1, D)
    bkv = b_kv.reshape(1, -1)
    g_f = final_norm_g.reshape(1, D)
    sinks = jnp.repeat(b_sinks.reshape(2, 4, 2).transpose(0, 2, 1).reshape(4, 4), CHUNK, axis=1)
    xs, tgt = x[0], loss_target[0]

    h1, u, gt, sv, vhat, rstd, k4, v4, kt, vt, wb_in, wb_out = _a_fwd(
        xs, g_a, wa_in, ln_g, ln_b, ws, bs_t, wa_out, g_kv, wkv, bkv, rc, rs1, rs2, [b_w_in[0], b_w_out[0]])
    wb_out = wb_out.reshape(-1, D)
    q, g2, o, dh2, dh2_b, loss, d_gf = _b_fwd(h1, b_norm_g, wb_in, b_bq, rc, rs1, rs2, k4, vt, sinks, wb_out, g_f, tgt)
    dh1p, dz2, n2, y2, dk, dv, d_bq, d_gb, d_sink = _b_bwd(dh2, h1, q, g2, o, k4, v4, kt, sinks, wb_out, wb_in,
                                                           b_norm_g, rc, rs1, rs2)
    d_sink = d_sink[:, :4].reshape(2, 2, 4).transpose(0, 2, 1).reshape(1, 16)
    gw_b_in = _wgrad(n2, dz2, N_DEV, "wgrad_b_in")
    gw_b_out = _wgrad(y2, dh2_b, 1, "wgrad_b_out").reshape(N_DEV, -1, D)
    (dx, dz, y, n1, nkv, dkv, dh1, d_ga, d_gkv, d_bkv, d_lng, d_lnb, d_ws, d_bst, r_b_in, r_b_out) = _a_bwd(
        dh1p, dk, dv, h1, xs, g_kv, wkv, g_a, wa_out, wa_in, ws, ln_g, ln_b, u, gt, sv, vhat, rstd, rc, rs1, rs2,
        [gw_b_in, gw_b_out])
    gw_a_out = _wgrad(y, dh1, 1, "wgrad_a_out").reshape(N_DEV, AW // N_DEV, D)
    gw_kv = _wgrad(nkv, dkv, 1, "wgrad_kv").reshape(N_DEV, D // N_DEV, 2 * LANES)
    small = [d_ws, d_bst[:, :G].T, d_gkv, d_bkv, d_gb, d_bq, d_sink, d_gf, d_ga, d_lng, d_lnb]
    used = sum(_nrows(a.size) for a in small)
    per = -(-used // (SUBLANES * N_DEV)) * SUBLANES
    small_pack = _pack(small, per * N_DEV).reshape(N_DEV, per, LANES)
    r_a_in, r_a_out, r_kv, r_small = _wgrad_exchange(n1, dz, me.reshape(1), [gw_a_out, gw_kv, small_pack],
                                                     "wgrad_a_in")

    g_a_in, d_a_in, nm_a_in, nv_a_in = _sum_adam(r_a_in, a_w_in[0], m_a_w_in[0], v_a_w_in[0], "adam_a_in")
    g_a_out, d_a_out, nm_a_out, nv_a_out = _sum_adam(r_a_out, a_w_out[0], m_a_w_out[0], v_a_w_out[0], "adam_a_out")
    g_kvw, d_kvw, nm_kvw, nv_kvw = _sum_adam(r_kv, w_kv, m_w_kv, v_w_kv, "adam_kv")
    g_b_in, d_b_in, nm_b_in, nv_b_in = _sum_adam(r_b_in, b_w_in[0], m_b_w_in[0], v_b_w_in[0], "adam_b_in")
    g_b_out, d_b_out, nm_b_out, nv_b_out = _sum_adam(r_b_out, b_w_out[0], m_b_w_out[0], v_b_w_out[0], "adam_b_out")

    red = _sum8(r_small, "sum_small")
    (full_small,) = _all_gather([red], [F32], "gather_small")
    full_small = full_small.reshape(N_DEV * per, LANES)
    rep_shapes = [a_ws.shape, a_bs.shape, kv_norm_g.shape, b_kv.shape, b_norm_g.shape, b_bq.shape, b_sinks.shape,
                  final_norm_g.shape]
    gs = _unpack(full_small, rep_shapes + [(N_DEV, a_norm_g.shape[1]), (N_DEV, a_ln_g.shape[1]), (N_DEV, a_ln_b.shape[1])])
    g_ang = lax.dynamic_slice_in_dim(gs[8], me, 1, axis=0)
    g_alng = lax.dynamic_slice_in_dim(gs[9], me, 1, axis=0)
    g_alnb = lax.dynamic_slice_in_dim(gs[10], me, 1, axis=0)
    sm_g = gs[:8] + [g_ang, g_alng, g_alnb]
    sm_shapes = [a.shape for a in sm_g]
    tot = sum(_nrows(a.size) for a in sm_g)
    pw = _pack([a_ws, a_bs, kv_norm_g, b_kv, b_norm_g, b_bq, b_sinks, final_norm_g, a_norm_g, a_ln_g, a_ln_b], tot)
    pm = _pack([m_a_ws, m_a_bs, m_kv_norm_g, m_b_kv, m_b_norm_g, m_b_bq, m_b_sinks, m_final_norm_g, m_a_norm_g,
                m_a_ln_g, m_a_ln_b], tot)
    pv = _pack([v_a_ws, v_a_bs, v_kv_norm_g, v_b_kv, v_b_norm_g, v_b_bq, v_b_sinks, v_final_norm_g, v_a_norm_g,
                v_a_ln_g, v_a_ln_b], tot)
    pg = _pack(sm_g, tot)
    pd, pnm, pnv = _adam_only(pg, pw, pm, pv, "adam_small")
    sd, snm, snv = _unpack(pd, sm_shapes), _unpack(pnm, sm_shapes), _unpack(pnv, sm_shapes)

    loss = lax.psum(loss[0, 0], AXES)

    def order(big, sm):
        a_in, a_out, kvw, b_in, b_out = big
        ws_, bs_, kvg, bkv_, bng, bq_, snk, fng, ang, alng, alnb = sm
        return (ang, a_in[None], alng, alnb, ws_, bs_, a_out[None], kvg, kvw, bkv_, bng, b_in[None], bq_, snk,
                b_out[None], fng)

    grads = order((g_a_in, g_a_out, g_kvw, g_b_in, g_b_out), sm_g)
    deltas = order((d_a_in, d_a_out, d_kvw, d_b_in, d_b_out), sd)
    new_m = order((nm_a_in, nm_a_out, nm_kvw, nm_b_in, nm_b_out), snm)
    new_v = order((nv_a_in, nv_a_out, nv_kvw, nv_b_in, nv_b_out), snv)
    return (loss, dx[None], *grads, *deltas, *new_m, *new_v)
```

```python
import functools
import math

import jax
import jax.numpy as jnp
from jax import lax
from jax.experimental import pallas as pl
from jax.experimental.pallas import tpu as pltpu

CHUNK = 128
HEAD_DIM = 64
ROPE_THETA = 10000.0
EPS = 1e-5
ADAM_LR = 0.001
ADAM_B1 = 0.9
ADAM_B2 = 0.999
ADAM_EPS = 1e-08
ADAM_WD = 0.01
ADAM_STEP = 10
N_DEV = 8
LANES = 128
NEG = -1e30

BF = jnp.bfloat16
F32 = jnp.float32
MESH = pl.DeviceIdType.MESH
AXES = ("x", "y", "c")
VMEM_LIMIT = 56 * 1024 * 1024


def _dot(a, b):
    return jnp.dot(a, b, preferred_element_type=F32)


def _dot_nt(a, b):
    return lax.dot_general(a, b, (((1,), (1,)), ((), ())), preferred_element_type=F32)


def _dot_tn(a, b):
    return lax.dot_general(a, b, (((0,), (0,)), ((), ())), preferred_element_type=F32)


def _const_spec(shape):
    nd = len(shape)
    return pl.BlockSpec(shape, lambda *_: (0,) * nd, pipeline_mode=pl.Buffered(1))


def _acc_spec(shape):
    nd = len(shape)
    return pl.BlockSpec(shape, lambda *_: (0,) * nd)


def _row_spec(tm, width):
    return pl.BlockSpec((tm, width), lambda i: (i, 0))


def _col_spec(tm, height):
    return pl.BlockSpec((height, tm), lambda i: (0, i))


def _params(sem):
    return pltpu.CompilerParams(dimension_semantics=sem, vmem_limit_bytes=VMEM_LIMIT)


def _rot(x, c, s1, s2):
    return x * c + pltpu.roll(x, 96, 1) * s1 + pltpu.roll(x, 32, 1) * s2


def _rot_bwd(d, c, s1, s2):
    return d * c + pltpu.roll(d * s1, 32, 1) + pltpu.roll(d * s2, 96, 1)


def _silu_parts(g):
    sg = jax.nn.sigmoid(g)
    return g * sg, sg * (1.0 + g * (1.0 - sg))


def _rms_bwd(dn, xh, r, g):
    a = dn * g
    return r * (a - xh * jnp.mean(a * xh, axis=-1, keepdims=True))


def _lane_lo(shape):
    return lax.broadcasted_iota(jnp.int32, shape, 1) < HEAD_DIM


def _split4(t):
    lo = _lane_lo(t.shape)
    tr = pltpu.roll(t, HEAD_DIM, 1)
    z = jnp.zeros_like(t)
    return jnp.concatenate([jnp.where(lo, t, z), jnp.where(lo, z, tr), jnp.where(lo, tr, z), jnp.where(lo, z, t)], axis=1)


def _stack_pairs(t, h):
    return jnp.concatenate([t[:, (h * 4 + j) * LANES:(h * 4 + j + 1) * LANES] for j in range(4)], axis=0)


def _upper():
    shape = (CHUNK, 4 * CHUNK)
    return lax.broadcasted_iota(jnp.int32, shape, 0) > (lax.broadcasted_iota(jnp.int32, shape, 1) & (CHUNK - 1))


def _band_rows(ref, prev, cur, h):
    a = slice(2 * h * LANES, (2 * h + 1) * LANES)
    b = slice((2 * h + 1) * LANES, (2 * h + 2) * LANES)
    return jnp.concatenate([ref[pl.ds(prev, CHUNK), a], ref[pl.ds(cur, CHUNK), a],
                            ref[pl.ds(prev, CHUNK), b], ref[pl.ds(cur, CHUNK), b]], axis=0)


def _band_cols(ref, pci, ci, h):
    a = slice(2 * h * LANES, (2 * h + 1) * LANES)
    b = slice((2 * h + 1) * LANES, (2 * h + 2) * LANES)
    return jnp.concatenate([ref[pci, a, :], ref[ci, a, :], ref[pci, b, :], ref[ci, b, :]], axis=1)


def _fold(t, upper, has_prev=None):
    out = []
    for k in range(2):
        prev = t[2 * k * CHUNK:(2 * k + 1) * CHUNK]
        if has_prev is not None:
            prev = jnp.where(has_prev, prev, NEG)
        out.append(jnp.where(upper, prev, t[(2 * k + 1) * CHUNK:(2 * k + 2) * CHUNK]))
    return out


def _unfold(fa, fb, upper):
    z = jnp.zeros_like(fa)
    return jnp.concatenate([jnp.where(upper, fa, z), jnp.where(upper, z, fa),
                            jnp.where(upper, fb, z), jnp.where(upper, z, fb)], axis=0)


def _softmax_sink(f, sink):
    m = jnp.maximum(jnp.max(f, axis=0, keepdims=True), sink)
    p = jnp.exp(f - m)
    es = jnp.exp(sink - m)
    inv = 1.0 / (jnp.sum(p, axis=0, keepdims=True) + es)
    return p * inv, es * inv


def _a_fwd(x, g_a, wa_in, ln_g, ln_b, ws, bs_t, wa_out, g_kv, w_kv, b_kv, rc, rs1, rs2, later):
    T, D = x.shape
    AW = wa_out.shape[0]
    G = ws.shape[0]
    SH = wa_in.shape[2]
    TM = min(256, T)
    nT = T // TM
    nC = TM // CHUNK
    nl = len(later)

    def body(x_ref, ga_ref, wain_ref, lng_ref, lnb_ref, ws_ref, bst_ref, waout_ref, gkv_ref, wkv_ref, bkv_ref,
             rc_ref, rs1_ref, rs2_ref, *rest):
        shards, rest = rest[:nl], rest[nl:]
        (h1_ref, u_ref, gt_ref, sv_ref, vhat_ref, rstd_ref, k4_ref, v4_ref, kt_ref, vt_ref), rest = rest[:10], rest[10:]
        gathered, rest = rest[:nl], rest[nl:]
        z_scr, sv_scr = rest[:2]
        stages, (ssem, rsem, lsem) = rest[2:2 + nl], rest[2 + nl:]
        i = pl.program_id(0)
        gathers = [_Direct(stages[k], gathered[k], ssem.at[k], rsem.at[k], lsem.at[k], scatter=False) for k in range(nl)]

        @pl.when(i == 0)
        def _():
            for k in range(nl):
                stages[k][...] = shards[k][...].astype(BF)
                gathers[k].start()

        xv = x_ref[...]
        r1 = lax.rsqrt(jnp.mean(xv * xv, axis=-1, keepdims=True) + EPS)
        n1 = (xv * r1 * ga_ref[...]).astype(BF)
        for j in range(N_DEV):
            z_scr[:, j * SH:(j + 1) * SH] = _dot(n1, wain_ref[j])
        u = z_scr[:, :AW]
        v = z_scr[:, AW:2 * AW]
        gt = z_scr[:, 2 * AW:]
        mu = jnp.mean(v, axis=-1, keepdims=True)
        xc = v - mu
        rstd = lax.rsqrt(jnp.mean(xc * xc, axis=-1, keepdims=True) + EPS)
        vhat = xc * rstd
        vln = (vhat * lng_ref[...] + lnb_ref[...]).astype(BF)
        tri = lax.broadcasted_iota(jnp.int32, (CHUNK, CHUNK), 0) >= lax.broadcasted_iota(jnp.int32, (CHUNK, CHUNK), 1)
        for g in range(G):
            wsm = jnp.where(tri, ws_ref[g], 0.0).astype(BF)
            bias = bst_ref[:, g:g + 1]
            for c in range(nC):
                blk = vln[c * CHUNK:(c + 1) * CHUNK, g * CHUNK:(g + 1) * CHUNK]
                sv_scr[c * CHUNK:(c + 1) * CHUNK, g * CHUNK:(g + 1) * CHUNK] = _dot(wsm, blk) + bias
        sv = sv_scr[...]
        silu, _ = _silu_parts(gt)
        y = (u * sv * silu).astype(BF)
        h1 = xv + _dot(y, waout_ref[...])
        h1_ref[...] = h1
        u_ref[...] = u.astype(BF)
        gt_ref[...] = gt.astype(BF)
        sv_ref[...] = sv.astype(BF)
        vhat_ref[...] = vhat.astype(BF)
        rstd_ref[...] = jnp.broadcast_to(rstd, rstd_ref.shape)
        rkv = lax.rsqrt(jnp.mean(h1 * h1, axis=-1, keepdims=True) + EPS)
        nkv = (h1 * rkv * gkv_ref[...]).astype(BF)
        kv = _dot(nkv, wkv_ref[...]) + bkv_ref[...]
        k_rot = _rot(kv[:, :LANES], rc_ref[...], rs1_ref[...], rs2_ref[...])
        for src, ref, tref in ((k_rot, k4_ref, kt_ref), (kv[:, LANES:], v4_ref, vt_ref)):
            t4 = _split4(src)
            ref[...] = t4.astype(BF)
            for c in range(nC):
                for b in range(4):
                    blk = t4[c * CHUNK:(c + 1) * CHUNK, b * LANES:(b + 1) * LANES]
                    tref[c, b * LANES:(b + 1) * LANES, :] = blk.T.astype(BF)

        @pl.when(i == nT - 1)
        def _():
            for gth in gathers:
                gth.finish()

    row = functools.partial(_row_spec, TM)
    tr = pl.BlockSpec((nC, 4 * LANES, CHUNK), lambda i: (i, 0, 0))
    hbm = pl.BlockSpec(memory_space=pl.ANY)
    S = jax.ShapeDtypeStruct
    return pl.pallas_call(
        body, name="a_fwd", grid=(nT,),
        in_specs=[row(D), _const_spec((1, D)), _const_spec(wa_in.shape), _const_spec((1, AW)), _const_spec((1, AW)),
                  _const_spec(ws.shape), _const_spec(bs_t.shape), _const_spec(wa_out.shape), _const_spec((1, D)),
                  _const_spec(w_kv.shape), _const_spec((1, 2 * LANES)), row(LANES), row(LANES), row(LANES)]
        + [_const_spec(w.shape) for w in later],
        out_specs=[row(D), row(AW), row(AW), row(AW), row(AW), row(LANES), row(4 * LANES), row(4 * LANES), tr, tr]
        + [hbm] * nl,
        out_shape=(S((T, D), F32), S((T, AW), BF), S((T, AW), BF), S((T, AW), BF), S((T, AW), BF), S((T, LANES), F32),
                   S((T, 4 * LANES), BF), S((T, 4 * LANES), BF),
                   S((T // CHUNK, 4 * LANES, CHUNK), BF), S((T // CHUNK, 4 * LANES, CHUNK), BF))
        + tuple(S((N_DEV,) + w.shape, BF) for w in later),
        scratch_shapes=[pltpu.VMEM((TM, 3 * AW), F32), pltpu.VMEM((TM, AW), F32)]
        + [pltpu.VMEM(w.shape, BF) for w in later] + _direct_sems(nl),
        compiler_params=_params(("arbitrary",)),
    )(x, g_a, wa_in, ln_g, ln_b, ws, bs_t, wa_out, g_kv, w_kv, b_kv, rc, rs1, rs2, *later)


def _b_fwd(h1, g_b, wb_in, bq, rc, rs1, rs2, k4, vt, sinks, wb_out, g_f, target):
    T, D = h1.shape
    BW = wb_out.shape[0]
    SH = wb_in.shape[2]
    TM = min(256, T)
    nC = TM // CHUNK
    nP = BW // LANES

    def body(h1_ref, gb_ref, wbin_ref, bq_ref, rc_ref, rs1_ref, rs2_ref, k4_ref, vt_ref, sink_ref, wbout_ref, gf_ref,
             tgt_ref, q_ref, g2_ref, o_ref, dh2_ref, dh2b_ref, loss_ref, dgf_ref, z_scr, o_scr):
        i = pl.program_id(0)
        h1v = h1_ref[...]
        r2 = lax.rsqrt(jnp.mean(h1v * h1v, axis=-1, keepdims=True) + EPS)
        n2 = (h1v * r2 * gb_ref[...]).astype(BF)
        for j in range(N_DEV):
            z_scr[:, j * SH:(j + 1) * SH] = _dot(n2, wbin_ref[j])
        c_t, s1_t, s2_t = rc_ref[...], rs1_ref[...], rs2_ref[...]
        for p in range(nP):
            cols = slice(p * LANES, (p + 1) * LANES)
            qp = _rot(z_scr[:, cols] + bq_ref[:, cols], c_t, s1_t, s2_t) * (HEAD_DIM ** -0.5)
            q_ref[:, cols] = qp.astype(BF)
        g2 = z_scr[:, BW:]
        g2_ref[...] = g2.astype(BF)
        upper = _upper()
        for c in range(nC):
            ci = i * nC + c
            rows = slice(c * CHUNK, (c + 1) * CHUNK)
            pci = jnp.maximum(ci - 1, 0)
            prev = pl.multiple_of(pci * CHUNK, CHUNK)
            cur = pl.multiple_of(ci * CHUNK, CHUNK)
            qc = q_ref[rows, :]
            for h in range(2):
                st = _dot_nt(_band_rows(k4_ref, prev, cur, h), _stack_pairs(qc, h))
                fa, fb = _fold(st, upper, ci > 0)
                pa, _ = _softmax_sink(fa, sink_ref[2 * h:2 * h + 1, :])
                pb, _ = _softmax_sink(fb, sink_ref[2 * h + 1:2 * h + 2, :])
                ot = _dot(_band_cols(vt_ref, pci, ci, h), _unfold(pa, pb, upper).astype(BF))
                for j in range(4):
                    o_scr[rows, (h * 4 + j) * LANES:(h * 4 + j + 1) * LANES] = ot[:, j * CHUNK:(j + 1) * CHUNK].T
        o = o_scr[...]
        o_ref[...] = o.astype(BF)
        silu, _ = _silu_parts(g2)
        h2 = h1v + _dot((o * silu).astype(BF), wbout_ref[...])
        rf = lax.rsqrt(jnp.mean(h2 * h2, axis=-1, keepdims=True) + EPS)
        xh = h2 * rf
        gf = gf_ref[...]
        err = xh * gf - tgt_ref[...]
        dyf = err * (1.0 / D)
        dh2 = _rms_bwd(dyf, xh, rf, gf)
        dh2_ref[...] = dh2
        dh2b_ref[...] = dh2.astype(BF)

        @pl.when(i == 0)
        def _():
            loss_ref[...] = jnp.zeros_like(loss_ref)
            dgf_ref[...] = jnp.zeros_like(dgf_ref)

        loss_ref[...] += 0.5 * jnp.sum(jnp.mean(err * err, axis=-1, keepdims=True), axis=0, keepdims=True)
        dgf_ref[...] += jnp.sum(dyf * xh, axis=0, keepdims=True)

    row = functools.partial(_row_spec, TM)
    S = jax.ShapeDtypeStruct
    return pl.pallas_call(
        body, name="b_fwd", grid=(T // TM,),
        in_specs=[row(D), _const_spec((1, D)), _const_spec(wb_in.shape), _const_spec((1, BW)), row(LANES), row(LANES),
                  row(LANES), _const_spec(k4.shape), _const_spec(vt.shape), _const_spec(sinks.shape),
                  _const_spec(wb_out.shape), _const_spec((1, D)), row(D)],
        out_specs=[row(BW), row(BW), row(BW), row(D), row(D), _acc_spec((1, 1)), _acc_spec((1, D))],
        out_shape=(S((T, BW), BF), S((T, BW), BF), S((T, BW), BF), S((T, D), F32), S((T, D), BF), S((1, 1), F32),
                   S((1, D), F32)),
        scratch_shapes=[pltpu.VMEM((TM, 2 * BW), F32), pltpu.VMEM((TM, BW), F32)],
        compiler_params=_params(("arbitrary",)),
    )(h1, g_b, wb_in, bq, rc, rs1, rs2, k4, vt, sinks, wb_out, g_f, target)


def _b_bwd(dh2, h1, q, g2, o, k4, v4, kt, sinks, wb_out, wb_in, g_b, rc, rs1, rs2):
    T, D = h1.shape
    BW = wb_out.shape[0]
    SH = wb_in.shape[2]
    TM = min(256, T)
    nT = T // TM
    nC = TM // CHUNK
    nP = BW // LANES

    def body(dh2_ref, h1_ref, q_ref, g2_ref, o_ref, k4_ref, v4_ref, kt_ref, sink_ref, wbout_ref, wbin_ref, gb_ref,
             rc_ref, rs1_ref, rs2_ref,
             dh1_ref, dz2_ref, n2_ref, y2_ref, dk_ref, dv_ref, dbq_ref, dgb_ref, dsink_ref, do_scr, dq_scr, dsacc_scr):
        i = pl.program_id(0)

        @pl.when(i == 0)
        def _():
            dk_ref[...] = jnp.zeros_like(dk_ref)
            dv_ref[...] = jnp.zeros_like(dv_ref)
            dbq_ref[...] = jnp.zeros_like(dbq_ref)
            dgb_ref[...] = jnp.zeros_like(dgb_ref)
            dsacc_scr[...] = jnp.zeros_like(dsacc_scr)

        dh2 = dh2_ref[...]
        dy2 = _dot_nt(dh2.astype(BF), wbout_ref[...])
        g2v = g2_ref[...].astype(F32)
        ov = o_ref[...].astype(F32)
        silu, dsilu = _silu_parts(g2v)
        y2_ref[...] = (ov * silu).astype(BF).T
        do_scr[...] = (dy2 * silu).astype(BF)
        dz2_ref[:, BW:] = (dy2 * ov * dsilu).astype(BF)
        upper = _upper()
        lo = _lane_lo((2 * CHUNK, LANES))
        for c in range(nC):
            ci = i * nC + c
            rows = slice(c * CHUNK, (c + 1) * CHUNK)
            pci = jnp.maximum(ci - 1, 0)
            prev = pl.multiple_of(pci * CHUNK, CHUNK)
            cur = pl.multiple_of(ci * CHUNK, CHUNK)
            qc = q_ref[rows, :]
            doc = do_scr[rows, :]
            dkb = jnp.zeros((2 * CHUNK, LANES), F32)
            dvb = jnp.zeros((2 * CHUNK, LANES), F32)
            for h in range(2):
                qs = _stack_pairs(qc, h)
                dos = _stack_pairs(doc, h)
                fa, fb = _fold(_dot_nt(_band_rows(k4_ref, prev, cur, h), qs), upper, ci > 0)
                dfa, dfb = _fold(_dot_nt(_band_rows(v4_ref, prev, cur, h), dos), upper)
                folded = []
                for k, (f, df) in enumerate(((fa, dfa), (fb, dfb))):
                    p, ps = _softmax_sink(f, sink_ref[2 * h + k:2 * h + k + 1, :])
                    delta = jnp.sum(p * df, axis=0, keepdims=True)
                    dsacc_scr[2 * h + k:2 * h + k + 1, :] -= ps * delta
                    folded.append((p, p * (df - delta)))
                pt = _unfold(folded[0][0], folded[1][0], upper).astype(BF)
                dst = _unfold(folded[0][1], folded[1][1], upper).astype(BF)
                dqt = _dot(_band_cols(kt_ref, pci, ci, h), dst)
                for j in range(4):
                    dq_scr[rows, (h * 4 + j) * LANES:(h * 4 + j + 1) * LANES] = dqt[:, j * CHUNK:(j + 1) * CHUNK].T
                for acc_name, g in (("k", _dot(dst, qs)), ("v", _dot(pt, dos))):
                    a, b = g[:2 * CHUNK], g[2 * CHUNK:]
                    if h == 0:
                        part = jnp.where(lo, a + pltpu.roll(b, HEAD_DIM, 1), 0.0)
                    else:
                        part = jnp.where(lo, 0.0, pltpu.roll(a, HEAD_DIM, 1) + b)
                    if acc_name == "k":
                        dkb += part
                    else:
                        dvb += part
            dk_ref[pl.ds(prev, CHUNK), :] += dkb[:CHUNK]
            dk_ref[pl.ds(cur, CHUNK), :] += dkb[CHUNK:]
            dv_ref[pl.ds(prev, CHUNK), :] += dvb[:CHUNK]
            dv_ref[pl.ds(cur, CHUNK), :] += dvb[CHUNK:]

        @pl.when(i == nT - 1)
        def _():
            lane = lax.broadcasted_iota(jnp.int32, dsink_ref.shape, 1)
            tot = jnp.zeros(dsink_ref.shape, F32)
            for j in range(4):
                tot += jnp.where(lane == j, jnp.sum(dsacc_scr[:, j * CHUNK:(j + 1) * CHUNK], axis=1, keepdims=True), 0.0)
            dsink_ref[...] = tot
        c_t, s1_t, s2_t = rc_ref[...], rs1_ref[...], rs2_ref[...]
        for p in range(nP):
            cols = slice(p * LANES, (p + 1) * LANES)
            dqp = _rot_bwd(dq_scr[:, cols] * (HEAD_DIM ** -0.5), c_t, s1_t, s2_t)
            dbq_ref[:, cols] += jnp.sum(dqp, axis=0, keepdims=True)
            dz2_ref[:, cols] = dqp.astype(BF)
        h1v = h1_ref[...]
        r2 = lax.rsqrt(jnp.mean(h1v * h1v, axis=-1, keepdims=True) + EPS)
        xh = h1v * r2
        gb = gb_ref[...]
        n2_ref[...] = (xh * gb).astype(BF).T
        dn2 = None
        for j in range(N_DEV):
            part = _dot_nt(dz2_ref[:, j * SH:(j + 1) * SH], wbin_ref[j])
            dn2 = part if dn2 is None else dn2 + part
        dgb_ref[...] += jnp.sum(dn2 * xh, axis=0, keepdims=True)
        dh1_ref[...] = dh2 + _rms_bwd(dn2, xh, r2, gb)

    row = functools.partial(_row_spec, TM)
    S = jax.ShapeDtypeStruct
    return pl.pallas_call(
        body, name="b_bwd", grid=(T // TM,),
        in_specs=[row(D), row(D), row(BW), row(BW), row(BW), _const_spec(k4.shape), _const_spec(v4.shape),
                  _const_spec(kt.shape), _const_spec(sinks.shape), _const_spec(wb_out.shape), _const_spec(wb_in.shape),
                  _const_spec((1, D)), row(LANES), row(LANES), row(LANES)],
        out_specs=[row(D), row(2 * BW), _col_spec(TM, D), _col_spec(TM, BW), _acc_spec((T, LANES)),
                   _acc_spec((T, LANES)), _acc_spec((1, BW)), _acc_spec((1, D)), _acc_spec((4, LANES))],
        out_shape=(S((T, D), F32), S((T, 2 * BW), BF), S((D, T), BF), S((BW, T), BF), S((T, LANES), F32),
                   S((T, LANES), F32), S((1, BW), F32), S((1, D), F32), S((4, LANES), F32)),
        scratch_shapes=[pltpu.VMEM((TM, BW), BF), pltpu.VMEM((TM, BW), F32), pltpu.VMEM((4, 4 * CHUNK), F32)],
        compiler_params=_params(("arbitrary",)),
    )(dh2, h1, q, g2, o, k4, v4, kt, sinks, wb_out, wb_in, g_b, rc, rs1, rs2)


def _a_bwd(dh1p, dk, dv, h1, g_kv, w_kv, wa_out, ws, ln_g, ln_b, u, gt, sv, vhat, rstd, rc, rs1, rs2, ready):
    T, D = h1.shape
    AW = wa_out.shape[0]
    G = ws.shape[0]
    TM = min(256, T)
    nT = T // TM
    nC = TM // CHUNK
    nr = len(ready)

    def body(dh1p_ref, dk_ref, dv_ref, h1_ref, gkv_ref, wkv_ref, waout_ref, ws_ref, lng_ref,
             lnb_ref, u_ref, gt_ref, sv_ref, vhat_ref, rstd_ref, rc_ref, rs1_ref, rs2_ref, *rest):
        ready_refs, rest = rest[:nr], rest[nr:]
        (dz_ref, y_ref, nkv_ref, dkv_ref, dh1_ref, dh1f_ref, dgkv_ref, dbkv_ref, dlng_ref, dlnb_ref,
         dws_ref, dbs_ref), rest = rest[:12], rest[12:]
        recv_refs, (dsv_scr, dvln_scr, ssem, rsem, lsem) = rest[:nr], rest[nr:]
        i = pl.program_id(0)
        exchanges = [_Direct(ready_refs[k], recv_refs[k], ssem.at[k], rsem.at[k], lsem.at[k], scatter=True)
                     for k in range(nr)]

        @pl.when(i == 0)
        def _():
            for e in exchanges:
                e.start()
            for r in (dgkv_ref, dbkv_ref, dlng_ref, dlnb_ref, dws_ref, dbs_ref):
                r[...] = jnp.zeros_like(r)

        dk_pre = _rot_bwd(dk_ref[...], rc_ref[...], rs1_ref[...], rs2_ref[...])
        dkv = jnp.concatenate([dk_pre, dv_ref[...]], axis=1)
        dbkv_ref[...] += jnp.sum(dkv, axis=0, keepdims=True)
        dkv_b = dkv.astype(BF)
        dkv_ref[...] = dkv_b
        h1v = h1_ref[...]
        rkv = lax.rsqrt(jnp.mean(h1v * h1v, axis=-1, keepdims=True) + EPS)
        xh_kv = h1v * rkv
        gkv = gkv_ref[...]
        nkv_ref[...] = (xh_kv * gkv).astype(BF).T
        dnkv = _dot_nt(dkv_b, wkv_ref[...])
        dgkv_ref[...] += jnp.sum(dnkv * xh_kv, axis=0, keepdims=True)
        dh1 = dh1p_ref[...] + _rms_bwd(dnkv, xh_kv, rkv, gkv)
        dh1_b = dh1.astype(BF)
        dh1_ref[...] = dh1_b
        dh1f_ref[...] = dh1
        dy = _dot_nt(dh1_b, waout_ref[...])
        uv = u_ref[...].astype(F32)
        gtv = gt_ref[...].astype(F32)
        svv = sv_ref[...].astype(F32)
        silu, dsilu = _silu_parts(gtv)
        us = uv * silu
        y_ref[...] = (us * svv).astype(BF).T
        dz_ref[:, :AW] = (dy * svv * silu).astype(BF)
        dz_ref[:, 2 * AW:] = (dy * uv * svv * dsilu).astype(BF)
        dsv_scr[...] = (dy * us).astype(BF)
        vhat_v = vhat_ref[...].astype(F32)
        lng = lng_ref[...]
        vln_b = (vhat_v * lng + lnb_ref[...]).astype(BF)
        tri = lax.broadcasted_iota(jnp.int32, (CHUNK, CHUNK), 0) >= lax.broadcasted_iota(jnp.int32, (CHUNK, CHUNK), 1)
        lane = lax.broadcasted_iota(jnp.int32, (CHUNK, LANES), 1)
        dbs = jnp.zeros((CHUNK, LANES), F32)
        for g in range(G):
            wsm = jnp.where(tri, ws_ref[g], 0.0).astype(BF)
            cols = slice(g * CHUNK, (g + 1) * CHUNK)
            dws_g = None
            for c in range(nC):
                rows = slice(c * CHUNK, (c + 1) * CHUNK)
                dsv_cg = dsv_scr[rows, cols]
                dvln_scr[rows, cols] = _dot_tn(wsm, dsv_cg)
                part = _dot_nt(dsv_cg, vln_b[rows, cols])
                dws_g = part if dws_g is None else dws_g + part
                dbs += jnp.where(lane == g, jnp.sum(dsv_cg.astype(F32), axis=-1, keepdims=True), 0.0)
            dws_ref[g] += jnp.where(tri, dws_g, 0.0)
        dbs_ref[...] += dbs
        dvln = dvln_scr[...]
        dlng_ref[...] += jnp.sum(dvln * vhat_v, axis=0, keepdims=True)
        dlnb_ref[...] += jnp.sum(dvln, axis=0, keepdims=True)
        a = dvln * lng
        dvv = rstd_ref[:, 0:1] * (a - jnp.mean(a, axis=-1, keepdims=True)
                                  - vhat_v * jnp.mean(a * vhat_v, axis=-1, keepdims=True))
        dz_ref[:, AW:2 * AW] = dvv.astype(BF)

        @pl.when(i == nT - 1)
        def _():
            for e in exchanges:
                e.finish()

    row = functools.partial(_row_spec, TM)
    col = functools.partial(_col_spec, TM)
    hbm = pl.BlockSpec(memory_space=pl.ANY)
    S = jax.ShapeDtypeStruct
    return pl.pallas_call(
        body, name="a_bwd", grid=(nT,),
        in_specs=[row(D), row(LANES), row(LANES), row(D), _const_spec((1, D)), _const_spec(w_kv.shape),
                  _const_spec(wa_out.shape), _const_spec(ws.shape),
                  _const_spec((1, AW)), _const_spec((1, AW)), row(AW), row(AW), row(AW), row(AW), row(LANES),
                  row(LANES), row(LANES), row(LANES)] + [hbm] * nr,
        out_specs=[row(3 * AW), col(AW), col(D), row(2 * LANES), row(D), row(D),
                   _acc_spec((1, D)), _acc_spec((1, 2 * LANES)), _acc_spec((1, AW)),
                   _acc_spec((1, AW)), _acc_spec(ws.shape), _acc_spec((CHUNK, LANES))] + [hbm] * nr,
        out_shape=(S((T, 3 * AW), BF), S((AW, T), BF), S((D, T), BF), S((T, 2 * LANES), BF), S((T, D), BF),
                   S((T, D), F32),
                   S((1, D), F32), S((1, 2 * LANES), F32), S((1, AW), F32), S((1, AW), F32),
                   S(ws.shape, F32), S((CHUNK, LANES), F32)) + tuple(S(r.shape, r.dtype) for r in ready),
        scratch_shapes=[pltpu.VMEM((TM, AW), BF), pltpu.VMEM((TM, AW), F32)] + _direct_sems(nr),
        compiler_params=_params(("arbitrary",)),
    )(dh1p, dk, dv, h1, g_kv, w_kv, wa_out, ws, ln_g, ln_b, u, gt, sv, vhat, rstd, rc, rs1, rs2, *ready)


def _a_in_bwd(dz, wa_in, x, dh1, g_a):
    T, D = x.shape
    SH = wa_in.shape[2]
    TM = min(512, T)

    def body(dz_ref, wain_ref, x_ref, dh1_ref, ga_ref, dx_ref, n1_ref, dga_ref):
        @pl.when(pl.program_id(0) == 0)
        def _():
            dga_ref[...] = jnp.zeros_like(dga_ref)

        xv = x_ref[...]
        r1 = lax.rsqrt(jnp.mean(xv * xv, axis=-1, keepdims=True) + EPS)
        xh = xv * r1
        ga = ga_ref[...]
        n1_ref[...] = (xh * ga).astype(BF).T
        dn1 = None
        for j in range(N_DEV):
            part = _dot_nt(dz_ref[:, j * SH:(j + 1) * SH], wain_ref[j])
            dn1 = part if dn1 is None else dn1 + part
        dga_ref[...] += jnp.sum(dn1 * xh, axis=0, keepdims=True)
        dx_ref[...] = dh1_ref[...] + _rms_bwd(dn1, xh, r1, ga)

    row = functools.partial(_row_spec, TM)
    S = jax.ShapeDtypeStruct
    return pl.pallas_call(
        body, name="a_in_bwd", grid=(T // TM,),
        in_specs=[row(dz.shape[1]), _const_spec(wa_in.shape), row(D), row(D), _const_spec((1, D))],
        out_specs=[row(D), _col_spec(TM, D), _acc_spec((1, D))],
        out_shape=(S((T, D), F32), S((D, T), BF), S((1, D), F32)),
        compiler_params=_params(("arbitrary",)),
    )(dz, wa_in, x, dh1, g_a)


def _wgrad(at, b, nblk, name, bt=512):
    K, T = at.shape
    N = b.shape[1] // nblk
    BT = min(bt, T)
    nt = T // BT

    def body(a_ref, b_ref, o_ref, acc):
        t = pl.program_id(1)

        @pl.when(t == 0)
        def _():
            acc[...] = jnp.zeros_like(acc)

        acc[...] += _dot(a_ref[...], b_ref[...])

        @pl.when(t == nt - 1)
        def _():
            o_ref[0] = acc[...].astype(BF)

    return pl.pallas_call(
        body, name=name, grid=(nblk, nt),
        in_specs=[pl.BlockSpec((K, BT), lambda j, t: (0, t)), pl.BlockSpec((BT, N), lambda j, t: (t, j))],
        out_specs=pl.BlockSpec((1, K, N), lambda j, t: (j, 0, 0)),
        out_shape=jax.ShapeDtypeStruct((nblk, K, N), BF),
        scratch_shapes=[pltpu.VMEM((K, N), F32)],
        compiler_params=_params(("arbitrary", "arbitrary")),
    )(at, b)


def _wgrad_exchange(a, b, me, extras, name):
    K, T = a.shape
    N = b.shape[1] // N_DEV
    BT = min(1024, T)
    nt = T // BT
    ne = len(extras)
    last = N_DEV - 1
    n_chip = N_DEV // 2

    def body(me_ref, a_ref, b_ref, *rest):
        ex_in, recv_ref, ex_out = rest[:ne], rest[ne], rest[ne + 1:2 * ne + 1]
        acc, dstage, istage, half, d_s, d_r, i_s, i_r, lsem, ex_ssem, ex_rsem, ex_lsem = rest[2 * ne + 1:]
        s, t = pl.program_id(0), pl.program_id(1)
        x, y, c = (lax.axis_index(ax) for ax in AXES)
        ex = [_Direct(ex_in[k], ex_out[k], ex_ssem.at[k], ex_rsem.at[k], ex_lsem.at[k], scatter=True) for k in range(ne)]

        def to_sibling(k, slot):
            return pltpu.make_async_remote_copy(src_ref=dstage.at[slot], dst_ref=half.at[k], send_sem=d_s.at[k],
                                                recv_sem=d_r.at[k], device_id=(x, y, 1 - c), device_id_type=MESH)

        def to_chip(k, slot, sender):
            far = n_chip - 1 - k
            px, py = x ^ ((far >> 1) & 1), y ^ (far & 1)
            dst = recv_ref.at[2 * x + y] if sender else recv_ref.at[2 * px + py]
            return pltpu.make_async_remote_copy(src_ref=istage.at[slot], dst_ref=dst, send_sem=i_s.at[k],
                                                recv_sem=i_r.at[k], device_id=(px, py, c), device_id_type=MESH)

        @pl.when((s == 0) & (t == 0))
        def _():
            for e in ex:
                e.start()

        @pl.when(t == 0)
        def _():
            acc[...] = jnp.zeros_like(acc)

        acc[...] += _dot(a_ref[...], b_ref[...])

        @pl.when(t == nt - 1)
        def _():
            k = lax.div(s, 2)
            slot = lax.rem(k, 2)

            @pl.when(lax.rem(s, 2) == 0)
            def _():
                @pl.when(k >= 2)
                def _():
                    to_sibling(k - 2, slot).wait_send()

                dstage[slot] = acc[...].astype(BF)
                to_sibling(k, slot).start()

            @pl.when(lax.rem(s, 2) == 1)
            def _():
                to_sibling(k, slot).wait_recv()

                @pl.when(k >= 2)
                def _():
                    to_chip(k - 2, slot, True).wait_send()

                istage[slot] = (acc[...] + half[k].astype(F32)).astype(BF)

                @pl.when(k < n_chip - 1)
                def _():
                    to_chip(k, slot, True).start()

            @pl.when(s == last)
            def _():
                own = pltpu.make_async_copy(istage.at[slot], recv_ref.at[2 * x + y], lsem)
                own.start()
                to_chip(n_chip - 2, 0, True).wait_send()
                to_sibling(n_chip - 2, 0).wait_send()
                to_sibling(n_chip - 1, 1).wait_send()
                for kk in range(n_chip - 1):
                    to_chip(kk, 0, False).wait_recv()
                own.wait()
                for e in ex:
                    e.finish()

    hbm = pl.BlockSpec(memory_space=pl.ANY)
    dma = pltpu.SemaphoreType.DMA
    grid_spec = pltpu.PrefetchScalarGridSpec(
        num_scalar_prefetch=1, grid=(N_DEV, nt),
        in_specs=[pl.BlockSpec((K, BT), lambda s, t, me_ref: (0, t)),
                  pl.BlockSpec((BT, N), lambda s, t, me_ref: (t, me_ref[0] ^ (last - s)))] + [hbm] * ne,
        out_specs=[hbm] * (ne + 1),
        scratch_shapes=[pltpu.VMEM((K, N), F32), pltpu.VMEM((2, K, N), BF), pltpu.VMEM((2, K, N), BF),
                        pltpu.VMEM((n_chip, K, N), BF), dma((n_chip,)), dma((n_chip,)), dma((n_chip - 1,)),
                        dma((n_chip - 1,)), dma] + _direct_sems(ne))
    return pl.pallas_call(
        body, name=name, grid_spec=grid_spec,
        out_shape=[jax.ShapeDtypeStruct((n_chip, K, N), BF)] + [jax.ShapeDtypeStruct(e.shape, e.dtype) for e in extras],
        compiler_params=_params(("arbitrary", "arbitrary")),
    )(me, a, b, *extras)


def _my_index():
    return 4 * lax.axis_index("x") + 2 * lax.axis_index("y") + lax.axis_index("c")


def _all_gather(arrs, dtypes, name):
    n = len(arrs)

    def body(*refs):
        ins, outs = refs[:n], refs[n:2 * n]
        stages = refs[2 * n:3 * n]
        send_sems, recv_sems, local_sems = refs[3 * n:]
        x, y, c = lax.axis_index("x"), lax.axis_index("y"), lax.axis_index("c")
        me, sibling = (x, y, c), (x, y, 1 - c)
        chips = [(1 - x, y), (x, 1 - y), (1 - x, 1 - y)]

        def idx(p):
            return 4 * p[0] + 2 * p[1] + p[2]

        def copy(a, k, block, to, src=None):
            dst = outs[a].at[idx(block)]
            return pltpu.make_async_remote_copy(src_ref=dst if src is None else src, dst_ref=dst,
                                                send_sem=send_sems.at[a, k], recv_sem=recv_sems.at[a, k],
                                                device_id=to, device_id_type=MESH)

        owns, firsts, passed = [], [], []
        for a in range(n):
            stages[a][...] = ins[a][...].astype(stages[a].dtype)
            own = pltpu.make_async_copy(stages[a], outs[a].at[idx(me)], local_sems.at[a])
            own.start()
            owns.append(own)
            first = [copy(a, 0, me, sibling, src=stages[a])]
            first += [copy(a, 1 + j, me, (*chip, c), src=stages[a]) for j, chip in enumerate(chips)]
            for cp in first:
                cp.start()
            firsts += first
        for a in range(n):
            for j, chip in enumerate(chips):
                copy(a, 1 + j, (*chip, c), me).wait_recv()
                fwd = copy(a, 4 + j, (*chip, c), sibling)
                fwd.start()
                passed.append(fwd)
        for a in range(n):
            copy(a, 0, sibling, me).wait_recv()
            for j, chip in enumerate(chips):
                copy(a, 4 + j, (*chip, 1 - c), me).wait_recv()
        for cp in firsts + passed:
            cp.wait_send()
        for own in owns:
            own.wait()

    vm = pl.BlockSpec(memory_space=pltpu.VMEM)
    hbm = pl.BlockSpec(memory_space=pl.ANY)
    return pl.pallas_call(
        body, name=name,
        in_specs=[vm] * n, out_specs=[hbm] * n,
        out_shape=[jax.ShapeDtypeStruct((N_DEV,) + a.shape, dt) for a, dt in zip(arrs, dtypes)],
        scratch_shapes=[pltpu.VMEM(a.shape, dt) for a, dt in zip(arrs, dtypes)]
        + [pltpu.SemaphoreType.DMA((n, 7)), pltpu.SemaphoreType.DMA((n, 7)), pltpu.SemaphoreType.DMA((n,))],
        compiler_params=pltpu.CompilerParams(vmem_limit_bytes=VMEM_LIMIT),
    )(*arrs)


def _peer(mask):
    x, y, c = (lax.axis_index(a) for a in AXES)
    return (x ^ ((mask >> 2) & 1), y ^ ((mask >> 1) & 1), c ^ (mask & 1))


def _dev_index(p):
    return 4 * p[0] + 2 * p[1] + p[2]


class _Direct:
    def __init__(self, src, dst, send_sems, recv_sems, local_sem, scatter):
        me = _my_index()
        self.own = pltpu.make_async_copy(src.at[me] if scatter else src, dst.at[me], local_sem)
        self.sends, self.recvs = [], []
        for k in range(1, N_DEV):
            p = _peer(k)
            pi = _dev_index(p)
            sems = dict(send_sem=send_sems.at[k - 1], recv_sem=recv_sems.at[k - 1], device_id=p, device_id_type=MESH)
            self.sends.append(pltpu.make_async_remote_copy(src_ref=src.at[pi] if scatter else src, dst_ref=dst.at[me],
                                                           **sems))
            self.recvs.append(pltpu.make_async_remote_copy(src_ref=src.at[me] if scatter else src, dst_ref=dst.at[pi],
                                                           **sems))

    def start(self):
        self.own.start()
        for cp in self.sends:
            cp.start()

    def finish(self):
        for cp in self.sends:
            cp.wait_send()
        for cp in self.recvs:
            cp.wait_recv()
        self.own.wait()


def _direct_sems(n):
    return [pltpu.SemaphoreType.DMA((n, 7)), pltpu.SemaphoreType.DMA((n, 7)), pltpu.SemaphoreType.DMA((n,))]


def _adam_math(w, g, m, v):
    m = ADAM_B1 * m + (1.0 - ADAM_B1) * g
    v = ADAM_B2 * v + (1.0 - ADAM_B2) * (g * g)
    m_hat = m / (1.0 - ADAM_B1 ** ADAM_STEP)
    v_hat = v / (1.0 - ADAM_B2 ** ADAM_STEP)
    delta = -ADAM_LR * (m_hat / (jnp.sqrt(v_hat) + ADAM_EPS) + ADAM_WD * w)
    return delta, m, v


def _sum_adam(parts, w, m, v, name):
    R, C = w.shape
    NP = parts.shape[0]
    BR = CHUNK if R % CHUNK == 0 else R

    def body(p_ref, w_ref, m_ref, v_ref, g_ref, d_ref, nm_ref, nv_ref):
        g = p_ref[0].astype(F32)
        for i in range(1, NP):
            g = g + p_ref[i].astype(F32)
        g_ref[...] = g
        d_ref[...], nm_ref[...], nv_ref[...] = _adam_math(w_ref[...], g, m_ref[...], v_ref[...])

    blk = pl.BlockSpec((BR, C), lambda i: (i, 0))
    S = jax.ShapeDtypeStruct((R, C), F32)
    return pl.pallas_call(
        body, name=name, grid=(R // BR,),
        in_specs=[pl.BlockSpec((NP, BR, C), lambda i: (0, i, 0)), blk, blk, blk],
        out_specs=[blk] * 4, out_shape=(S,) * 4,
        compiler_params=_params(("arbitrary",)),
    )(parts, w, m, v)


def _sum8(parts, name):
    _, R, C = parts.shape

    def body(p_ref, o_ref):
        g = p_ref[0]
        for i in range(1, N_DEV):
            g = g + p_ref[i]
        o_ref[...] = g

    return pl.pallas_call(body, name=name, out_shape=jax.ShapeDtypeStruct((R, C), F32))(parts)


def _adam_only(g, w, m, v, name):
    def body(g_ref, w_ref, m_ref, v_ref, d_ref, nm_ref, nv_ref):
        d_ref[...], nm_ref[...], nv_ref[...] = _adam_math(w_ref[...], g_ref[...], m_ref[...], v_ref[...])

    S = jax.ShapeDtypeStruct(w.shape, F32)
    return pl.pallas_call(body, name=name, out_shape=(S,) * 3)(g, w, m, v)


def _rope_tables(T):
    pos = jnp.arange(T, dtype=F32)
    inv_freq = ROPE_THETA ** (-jnp.arange(0, HEAD_DIM, 2, dtype=F32) / HEAD_DIM)
    ang = pos[:, None] * inv_freq[None, :]
    cos, sin, zero = jnp.cos(ang), jnp.sin(ang), jnp.zeros_like(ang)
    c = jnp.concatenate([cos, cos, cos, cos], axis=1)
    s1 = jnp.concatenate([-sin, zero, -sin, zero], axis=1)
    s2 = jnp.concatenate([zero, sin, zero, sin], axis=1)
    return c, s1, s2


SUBLANES = 8


def _nrows(size):
    return -(-size // (SUBLANES * LANES)) * SUBLANES


def _rows(a):
    flat = a.reshape(-1)
    pad = _nrows(flat.shape[0]) * LANES - flat.shape[0]
    if pad:
        flat = jnp.concatenate([flat, jnp.zeros((pad,), flat.dtype)])
    return flat.reshape(-1, LANES)


def _pack(arrs, total_rows):
    rows = [_rows(a) for a in arrs]
    used = sum(r.shape[0] for r in rows)
    if total_rows > used:
        rows.append(jnp.zeros((total_rows - used, LANES), F32))
    return jnp.concatenate(rows, axis=0)


def _unpack(packed, shapes):
    out, at = [], 0
    for shp in shapes:
        size = math.prod(shp)
        nrow = _nrows(size)
        out.append(packed[at:at + nrow].reshape(-1)[:size].reshape(shp))
        at += nrow
    return out


def kernel(x, a_norm_g, a_w_in, a_ln_g, a_ln_b, a_ws, a_bs, a_w_out, kv_norm_g, w_kv, b_kv, b_norm_g, b_w_in, b_bq, b_sinks, b_w_out, final_norm_g, loss_target, m_a_norm_g, m_a_w_in, m_a_ln_g, m_a_ln_b, m_a_ws, m_a_bs, m_a_w_out, m_kv_norm_g, m_w_kv, m_b_kv, m_b_norm_g, m_b_w_in, m_b_bq, m_b_sinks, m_b_w_out, m_final_norm_g, v_a_norm_g, v_a_w_in, v_a_ln_g, v_a_ln_b, v_a_ws, v_a_bs, v_a_w_out, v_kv_norm_g, v_w_kv, v_b_kv, v_b_norm_g, v_b_w_in, v_b_bq, v_b_sinks, v_b_w_out, v_final_norm_g):
    T, D = x.shape[1], x.shape[2]
    AW = a_ln_g.shape[1] * N_DEV
    G = a_ws.shape[1]
    assert w_kv.shape[1] == 2 * LANES and a_ws.shape[2] == CHUNK and T % CHUNK == 0
    me = _my_index()

    vec = jnp.concatenate([a_norm_g, a_ln_g, a_ln_b], axis=1)
    vec = jnp.broadcast_to(vec, (8, vec.shape[1]))
    wa_in, wa_out, wkv, vecs = _all_gather([a_w_in[0], a_w_out[0], w_kv, vec], [BF, BF, BF, F32], "gather_weights")
    wa_out = wa_out.reshape(AW, D)
    wkv = wkv.reshape(D, 2 * LANES)
    vecs = vecs[:, 0, :]
    ds = D // N_DEV
    g_a = vecs[:, :ds].reshape(1, D)
    ln_g = vecs[:, ds:ds + AW // N_DEV].reshape(1, AW)
    ln_b = vecs[:, ds + AW // N_DEV:].reshape(1, AW)

    rc, rs1, rs2 = _rope_tables(T)
    ws = a_ws[0]
    bs_t = a_bs[0].T
    g_kv = kv_norm_g.reshape(1, D)
    bkv = b_kv.reshape(1, -1)
    g_f = final_norm_g.reshape(1, D)
    sinks = jnp.repeat(b_sinks.reshape(2, 4, 2).transpose(0, 2, 1).reshape(4, 4), CHUNK, axis=1)
    xs, tgt = x[0], loss_target[0]

    h1, u, gt, sv, vhat, rstd, k4, v4, kt, vt, wb_in, wb_out = _a_fwd(
        xs, g_a, wa_in, ln_g, ln_b, ws, bs_t, wa_out, g_kv, wkv, bkv, rc, rs1, rs2, [b_w_in[0], b_w_out[0]])
    wb_out = wb_out.reshape(-1, D)
    q, g2, o, dh2, dh2_b, loss, d_gf = _b_fwd(h1, b_norm_g, wb_in, b_bq, rc, rs1, rs2, k4, vt, sinks, wb_out, g_f, tgt)
    dh1p, dz2, n2, y2, dk, dv, d_bq, d_gb, d_sink = _b_bwd(dh2, h1, q, g2, o, k4, v4, kt, sinks, wb_out, wb_in,
                                                           b_norm_g, rc, rs1, rs2)
    d_sink = d_sink[:, :4].reshape(2, 2, 4).transpose(0, 2, 1).reshape(1, 16)
    gw_b_in = _wgrad(n2, dz2, N_DEV, "wgrad_b_in", bt=2048)
    gw_b_out = _wgrad(y2, dh2_b, 1, "wgrad_b_out").reshape(N_DEV, -1, D)
    (dz, y, nkv, dkv, dh1, dh1_f, d_gkv, d_bkv, d_lng, d_lnb, d_ws, d_bst, r_b_in, r_b_out) = _a_bwd(
        dh1p, dk, dv, h1, g_kv, wkv, wa_out, ws, ln_g, ln_b, u, gt, sv, vhat, rstd, rc, rs1, rs2, [gw_b_in, gw_b_out])
    dx, n1, d_ga = _a_in_bwd(dz, wa_in, xs, dh1_f, g_a)
    gw_a_out = _wgrad(y, dh1, 1, "wgrad_a_out").reshape(N_DEV, AW // N_DEV, D)
    gw_kv = _wgrad(nkv, dkv, 1, "wgrad_kv").reshape(N_DEV, D // N_DEV, 2 * LANES)
    small = [d_ws, d_bst[:, :G].T, d_gkv, d_bkv, d_gb, d_bq, d_sink, d_gf, d_ga, d_lng, d_lnb]
    used = sum(_nrows(a.size) for a in small)
    per = -(-used // (SUBLANES * N_DEV)) * SUBLANES
    small_pack = _pack(small, per * N_DEV).reshape(N_DEV, per, LANES)
    r_a_in, r_a_out, r_kv, r_small = _wgrad_exchange(n1, dz, me.reshape(1), [gw_a_out, gw_kv, small_pack],
                                                     "wgrad_a_in")

    g_a_in, d_a_in, nm_a_in, nv_a_in = _sum_adam(r_a_in, a_w_in[0], m_a_w_in[0], v_a_w_in[0], "adam_a_in")
    g_a_out, d_a_out, nm_a_out, nv_a_out = _sum_adam(r_a_out, a_w_out[0], m_a_w_out[0], v_a_w_out[0], "adam_a_out")
    g_kvw, d_kvw, nm_kvw, nv_kvw = _sum_adam(r_kv, w_kv, m_w_kv, v_w_kv, "adam_kv")
    g_b_in, d_b_in, nm_b_in, nv_b_in = _sum_adam(r_b_in, b_w_in[0], m_b_w_in[0], v_b_w_in[0], "adam_b_in")
    g_b_out, d_b_out, nm_b_out, nv_b_out = _sum_adam(r_b_out, b_w_out[0], m_b_w_out[0], v_b_w_out[0], "adam_b_out")

    red = _sum8(r_small, "sum_small")
    (full_small,) = _all_gather([red], [F32], "gather_small")
    full_small = full_small.reshape(N_DEV * per, LANES)
    rep_shapes = [a_ws.shape, a_bs.shape, kv_norm_g.shape, b_kv.shape, b_norm_g.shape, b_bq.shape, b_sinks.shape,
                  final_norm_g.shape]
    gs = _unpack(full_small, rep_shapes + [(N_DEV, a_norm_g.shape[1]), (N_DEV, a_ln_g.shape[1]), (N_DEV, a_ln_b.shape[1])])
    g_ang = lax.dynamic_slice_in_dim(gs[8], me, 1, axis=0)
    g_alng = lax.dynamic_slice_in_dim(gs[9], me, 1, axis=0)
    g_alnb = lax.dynamic_slice_in_dim(gs[10], me, 1, axis=0)
    sm_g = gs[:8] + [g_ang, g_alng, g_alnb]
    sm_shapes = [a.shape for a in sm_g]
    tot = sum(_nrows(a.size) for a in sm_g)
    pw = _pack([a_ws, a_bs, kv_norm_g, b_kv, b_norm_g, b_bq, b_sinks, final_norm_g, a_norm_g, a_ln_g, a_ln_b], tot)
    pm = _pack([m_a_ws, m_a_bs, m_kv_norm_g, m_b_kv, m_b_norm_g, m_b_bq, m_b_sinks, m_final_norm_g, m_a_norm_g,
                m_a_ln_g, m_a_ln_b], tot)
    pv = _pack([v_a_ws, v_a_bs, v_kv_norm_g, v_b_kv, v_b_norm_g, v_b_bq, v_b_sinks, v_final_norm_g, v_a_norm_g,
                v_a_ln_g, v_a_ln_b], tot)
    pg = _pack(sm_g, tot)
    pd, pnm, pnv = _adam_only(pg, pw, pm, pv, "adam_small")
    sd, snm, snv = _unpack(pd, sm_shapes), _unpack(pnm, sm_shapes), _unpack(pnv, sm_shapes)

    loss = lax.psum(loss[0, 0], AXES)

    def order(big, sm):
        a_in, a_out, kvw, b_in, b_out = big
        ws_, bs_, kvg, bkv_, bng, bq_, snk, fng, ang, alng, alnb = sm
        return (ang, a_in[None], alng, alnb, ws_, bs_, a_out[None], kvg, kvw, bkv_, bng, b_in[None], bq_, snk,
                b_out[None], fng)

    grads = order((g_a_in, g_a_out, g_kvw, g_b_in, g_b_out), sm_g)
    deltas = order((d_a_in, d_a_out, d_kvw, d_b_in, d_b_out), sd)
    new_m = order((nm_a_in, nm_a_out, nm_kvw, nm_b_in, nm_b_out), snm)
    new_v = order((nv_a_in, nv_a_out, nv_kvw, nv_b_in, nv_b_out), snv)
    return (loss, dx[None], *grads, *deltas, *new_m, *new_v)
```

```python
import functools
import math

import jax
import jax.numpy as jnp
import numpy as np
from jax import lax
from jax.experimental import pallas as pl
from jax.experimental.pallas import tpu as pltpu

CHUNK = 128
HEAD_DIM = 64
ROPE_THETA = 10000.0
EPS = 1e-5
ADAM_LR = 0.001
ADAM_B1 = 0.9
ADAM_B2 = 0.999
ADAM_EPS = 1e-08
ADAM_WD = 0.01
ADAM_STEP = 10
N_DEV = 8
LANES = 128
NEG = -1e30

BF = jnp.bfloat16
F32 = jnp.float32
MESH = pl.DeviceIdType.MESH
AXES = ("x", "y", "c")
VMEM_LIMIT = 56 * 1024 * 1024


def _dot(a, b):
    return jnp.dot(a, b, preferred_element_type=F32)


def _dot_nt(a, b):
    return lax.dot_general(a, b, (((1,), (1,)), ((), ())), preferred_element_type=F32)


def _dot_tn(a, b):
    return lax.dot_general(a, b, (((0,), (0,)), ((), ())), preferred_element_type=F32)


def _const_spec(shape):
    nd = len(shape)
    return pl.BlockSpec(shape, lambda *_: (0,) * nd, pipeline_mode=pl.Buffered(1))


def _acc_spec(shape):
    nd = len(shape)
    return pl.BlockSpec(shape, lambda *_: (0,) * nd)


def _row_spec(tm, width):
    return pl.BlockSpec((tm, width), lambda i: (i, 0))


def _col_spec(tm, height):
    return pl.BlockSpec((height, tm), lambda i: (0, i))


def _params(sem):
    return pltpu.CompilerParams(dimension_semantics=sem, vmem_limit_bytes=VMEM_LIMIT)


def _rot(x, c, s1, s2):
    return x * c + pltpu.roll(x, 96, 1) * s1 + pltpu.roll(x, 32, 1) * s2


def _rot_bwd(d, c, s1, s2):
    return d * c + pltpu.roll(d * s1, 32, 1) + pltpu.roll(d * s2, 96, 1)


def _silu_parts(g):
    sg = jax.nn.sigmoid(g)
    return g * sg, sg * (1.0 + g * (1.0 - sg))


def _rms_bwd(dn, xh, r, g):
    a = dn * g
    return r * (a - xh * jnp.mean(a * xh, axis=-1, keepdims=True))


def _lane_lo(shape):
    return lax.broadcasted_iota(jnp.int32, shape, 1) < HEAD_DIM


def _split4(t):
    lo = _lane_lo(t.shape)
    tr = pltpu.roll(t, HEAD_DIM, 1)
    z = jnp.zeros_like(t)
    return jnp.concatenate([jnp.where(lo, t, z), jnp.where(lo, z, tr), jnp.where(lo, tr, z), jnp.where(lo, z, t)], axis=1)


def _stack_pairs(t, h):
    return jnp.concatenate([t[:, (h * 4 + j) * LANES:(h * 4 + j + 1) * LANES] for j in range(4)], axis=0)


def _upper():
    shape = (CHUNK, 4 * CHUNK)
    return lax.broadcasted_iota(jnp.int32, shape, 0) > (lax.broadcasted_iota(jnp.int32, shape, 1) & (CHUNK - 1))


def _band_rows(ref, prev, cur, h):
    a = slice(2 * h * LANES, (2 * h + 1) * LANES)
    b = slice((2 * h + 1) * LANES, (2 * h + 2) * LANES)
    return jnp.concatenate([ref[pl.ds(prev, CHUNK), a], ref[pl.ds(cur, CHUNK), a],
                            ref[pl.ds(prev, CHUNK), b], ref[pl.ds(cur, CHUNK), b]], axis=0)


def _band_cols(ref, pci, ci, h):
    a = slice(2 * h * LANES, (2 * h + 1) * LANES)
    b = slice((2 * h + 1) * LANES, (2 * h + 2) * LANES)
    return jnp.concatenate([ref[pci, a, :], ref[ci, a, :], ref[pci, b, :], ref[ci, b, :]], axis=1)


def _fold(t, upper, has_prev=None):
    out = []
    for k in range(2):
        prev = t[2 * k * CHUNK:(2 * k + 1) * CHUNK]
        if has_prev is not None:
            prev = jnp.where(has_prev, prev, NEG)
        out.append(jnp.where(upper, prev, t[(2 * k + 1) * CHUNK:(2 * k + 2) * CHUNK]))
    return out


def _unfold(fa, fb, upper):
    z = jnp.zeros_like(fa)
    return jnp.concatenate([jnp.where(upper, fa, z), jnp.where(upper, z, fa),
                            jnp.where(upper, fb, z), jnp.where(upper, z, fb)], axis=0)


def _softmax_sink(f, sink):
    m = jnp.maximum(jnp.max(f, axis=0, keepdims=True), sink)
    p = jnp.exp(f - m)
    es = jnp.exp(sink - m)
    inv = 1.0 / (jnp.sum(p, axis=0, keepdims=True) + es)
    return p * inv, es * inv


def _a_fwd(x, g_a, wa_in, ln_g, ln_b, ws, bs_t, wa_out, g_kv, w_kv, b_kv, rc, rs1, rs2, later):
    T, D = x.shape
    AW = wa_out.shape[0]
    G = ws.shape[0]
    SH = wa_in.shape[2]
    TM = min(256, T)
    nT = T // TM
    nC = TM // CHUNK
    nl = len(later)

    def body(x_ref, ga_ref, wain_ref, lng_ref, lnb_ref, ws_ref, bst_ref, waout_ref, gkv_ref, wkv_ref, bkv_ref,
             rc_ref, rs1_ref, rs2_ref, *rest):
        shards, rest = rest[:nl], rest[nl:]
        (h1_ref, u_ref, gt_ref, sv_ref, vhat_ref, rstd_ref, k4_ref, v4_ref, kt_ref, vt_ref), rest = rest[:10], rest[10:]
        gathered, rest = rest[:nl], rest[nl:]
        z_scr, sv_scr = rest[:2]
        stages, (ssem, rsem, lsem) = rest[2:2 + nl], rest[2 + nl:]
        i = pl.program_id(0)
        gathers = [_Direct(stages[k], gathered[k], ssem.at[k], rsem.at[k], lsem.at[k], scatter=False) for k in range(nl)]

        @pl.when(i == 0)
        def _():
            for k in range(nl):
                stages[k][...] = shards[k][...].astype(BF)
                gathers[k].start()

        xv = x_ref[...]
        r1 = lax.rsqrt(jnp.mean(xv * xv, axis=-1, keepdims=True) + EPS)
        n1 = (xv * r1 * ga_ref[...]).astype(BF)
        for j in range(N_DEV):
            z_scr[:, j * SH:(j + 1) * SH] = _dot(n1, wain_ref[j])
        u = z_scr[:, :AW]
        v = z_scr[:, AW:2 * AW]
        gt = z_scr[:, 2 * AW:]
        mu = jnp.mean(v, axis=-1, keepdims=True)
        xc = v - mu
        rstd = lax.rsqrt(jnp.mean(xc * xc, axis=-1, keepdims=True) + EPS)
        vhat = xc * rstd
        vln = (vhat * lng_ref[...] + lnb_ref[...]).astype(BF)
        tri = lax.broadcasted_iota(jnp.int32, (CHUNK, CHUNK), 0) >= lax.broadcasted_iota(jnp.int32, (CHUNK, CHUNK), 1)
        for g in range(G):
            wsm = jnp.where(tri, ws_ref[g], 0.0).astype(BF)
            bias = bst_ref[:, g:g + 1]
            for c in range(nC):
                blk = vln[c * CHUNK:(c + 1) * CHUNK, g * CHUNK:(g + 1) * CHUNK]
                sv_scr[c * CHUNK:(c + 1) * CHUNK, g * CHUNK:(g + 1) * CHUNK] = _dot(wsm, blk) + bias
        sv = sv_scr[...]
        silu, _ = _silu_parts(gt)
        y = (u * sv * silu).astype(BF)
        h1 = xv + _dot(y, waout_ref[...])
        h1_ref[...] = h1
        u_ref[...] = u.astype(BF)
        gt_ref[...] = gt.astype(BF)
        sv_ref[...] = sv.astype(BF)
        vhat_ref[...] = vhat.astype(BF)
        rstd_ref[...] = jnp.broadcast_to(rstd, rstd_ref.shape)
        rkv = lax.rsqrt(jnp.mean(h1 * h1, axis=-1, keepdims=True) + EPS)
        nkv = (h1 * rkv * gkv_ref[...]).astype(BF)
        kv = _dot(nkv, wkv_ref[...]) + bkv_ref[...]
        k_rot = _rot(kv[:, :LANES], rc_ref[...], rs1_ref[...], rs2_ref[...])
        for src, ref, tref in ((k_rot, k4_ref, kt_ref), (kv[:, LANES:], v4_ref, vt_ref)):
            t4 = _split4(src)
            ref[...] = t4.astype(BF)
            for c in range(nC):
                for b in range(4):
                    blk = t4[c * CHUNK:(c + 1) * CHUNK, b * LANES:(b + 1) * LANES]
                    tref[c, b * LANES:(b + 1) * LANES, :] = blk.T.astype(BF)

        @pl.when(i == nT - 1)
        def _():
            for gth in gathers:
                gth.finish()

    row = functools.partial(_row_spec, TM)
    tr = pl.BlockSpec((nC, 4 * LANES, CHUNK), lambda i: (i, 0, 0))
    hbm = pl.BlockSpec(memory_space=pl.ANY)
    S = jax.ShapeDtypeStruct
    return pl.pallas_call(
        body, name="a_fwd", grid=(nT,),
        in_specs=[row(D), _const_spec((1, D)), _const_spec(wa_in.shape), _const_spec((1, AW)), _const_spec((1, AW)),
                  _const_spec(ws.shape), _const_spec(bs_t.shape), _const_spec(wa_out.shape), _const_spec((1, D)),
                  _const_spec(w_kv.shape), _const_spec((1, 2 * LANES)), row(LANES), row(LANES), row(LANES)]
        + [_const_spec(w.shape) for w in later],
        out_specs=[row(D), row(AW), row(AW), row(AW), row(AW), row(LANES), row(4 * LANES), row(4 * LANES), tr, tr]
        + [hbm] * nl,
        out_shape=(S((T, D), F32), S((T, AW), BF), S((T, AW), BF), S((T, AW), BF), S((T, AW), BF), S((T, LANES), F32),
                   S((T, 4 * LANES), BF), S((T, 4 * LANES), BF),
                   S((T // CHUNK, 4 * LANES, CHUNK), BF), S((T // CHUNK, 4 * LANES, CHUNK), BF))
        + tuple(S((N_DEV,) + w.shape, BF) for w in later),
        scratch_shapes=[pltpu.VMEM((TM, 3 * AW), F32), pltpu.VMEM((TM, AW), F32)]
        + [pltpu.VMEM(w.shape, BF) for w in later] + _direct_sems(nl),
        compiler_params=_params(("arbitrary",)),
    )(x, g_a, wa_in, ln_g, ln_b, ws, bs_t, wa_out, g_kv, w_kv, b_kv, rc, rs1, rs2, *later)


def _b_fwd(h1, g_b, wb_in, bq, rc, rs1, rs2, k4, vt, sinks, wb_out, g_f, target):
    T, D = h1.shape
    BW = wb_out.shape[0]
    SH = wb_in.shape[2]
    TM = min(256, T)
    nC = TM // CHUNK
    nP = BW // LANES

    def body(h1_ref, gb_ref, wbin_ref, bq_ref, rc_ref, rs1_ref, rs2_ref, k4_ref, vt_ref, sink_ref, wbout_ref, gf_ref,
             tgt_ref, q_ref, g2_ref, o_ref, dh2_ref, dh2b_ref, loss_ref, dgf_ref, z_scr, o_scr):
        i = pl.program_id(0)
        h1v = h1_ref[...]
        r2 = lax.rsqrt(jnp.mean(h1v * h1v, axis=-1, keepdims=True) + EPS)
        n2 = (h1v * r2 * gb_ref[...]).astype(BF)
        for j in range(N_DEV):
            z_scr[:, j * SH:(j + 1) * SH] = _dot(n2, wbin_ref[j])
        c_t, s1_t, s2_t = rc_ref[...], rs1_ref[...], rs2_ref[...]
        for p in range(nP):
            cols = slice(p * LANES, (p + 1) * LANES)
            qp = _rot(z_scr[:, cols] + bq_ref[:, cols], c_t, s1_t, s2_t) * (HEAD_DIM ** -0.5)
            q_ref[:, cols] = qp.astype(BF)
        g2 = z_scr[:, BW:]
        g2_ref[...] = g2.astype(BF)
        upper = _upper()
        for c in range(nC):
            ci = i * nC + c
            rows = slice(c * CHUNK, (c + 1) * CHUNK)
            pci = jnp.maximum(ci - 1, 0)
            prev = pl.multiple_of(pci * CHUNK, CHUNK)
            cur = pl.multiple_of(ci * CHUNK, CHUNK)
            qc = q_ref[rows, :]
            for h in range(2):
                st = _dot_nt(_band_rows(k4_ref, prev, cur, h), _stack_pairs(qc, h))
                fa, fb = _fold(st, upper, ci > 0)
                pa, _ = _softmax_sink(fa, sink_ref[2 * h:2 * h + 1, :])
                pb, _ = _softmax_sink(fb, sink_ref[2 * h + 1:2 * h + 2, :])
                ot = _dot(_band_cols(vt_ref, pci, ci, h), _unfold(pa, pb, upper).astype(BF))
                for j in range(4):
                    o_scr[rows, (h * 4 + j) * LANES:(h * 4 + j + 1) * LANES] = ot[:, j * CHUNK:(j + 1) * CHUNK].T
        o = o_scr[...]
        o_ref[...] = o.astype(BF)
        silu, _ = _silu_parts(g2)
        h2 = h1v + _dot((o * silu).astype(BF), wbout_ref[...])
        rf = lax.rsqrt(jnp.mean(h2 * h2, axis=-1, keepdims=True) + EPS)
        xh = h2 * rf
        gf = gf_ref[...]
        err = xh * gf - tgt_ref[...]
        dyf = err * (1.0 / D)
        dh2 = _rms_bwd(dyf, xh, rf, gf)
        dh2_ref[...] = dh2
        dh2b_ref[...] = dh2.astype(BF)

        @pl.when(i == 0)
        def _():
            loss_ref[...] = jnp.zeros_like(loss_ref)
            dgf_ref[...] = jnp.zeros_like(dgf_ref)

        loss_ref[...] += 0.5 * jnp.sum(jnp.mean(err * err, axis=-1, keepdims=True), axis=0, keepdims=True)
        dgf_ref[...] += jnp.sum(dyf * xh, axis=0, keepdims=True)

    row = functools.partial(_row_spec, TM)
    S = jax.ShapeDtypeStruct
    return pl.pallas_call(
        body, name="b_fwd", grid=(T // TM,),
        in_specs=[row(D), _const_spec((1, D)), _const_spec(wb_in.shape), _const_spec((1, BW)), row(LANES), row(LANES),
                  row(LANES), _const_spec(k4.shape), _const_spec(vt.shape), _const_spec(sinks.shape),
                  _const_spec(wb_out.shape), _const_spec((1, D)), row(D)],
        out_specs=[row(BW), row(BW), row(BW), row(D), row(D), _acc_spec((1, 1)), _acc_spec((1, D))],
        out_shape=(S((T, BW), BF), S((T, BW), BF), S((T, BW), BF), S((T, D), F32), S((T, D), BF), S((1, 1), F32),
                   S((1, D), F32)),
        scratch_shapes=[pltpu.VMEM((TM, 2 * BW), F32), pltpu.VMEM((TM, BW), F32)],
        compiler_params=_params(("arbitrary",)),
    )(h1, g_b, wb_in, bq, rc, rs1, rs2, k4, vt, sinks, wb_out, g_f, target)


def _b_bwd(dh2, h1, q, g2, o, k4, v4, kt, sinks, wb_out, wb_in, g_b, rc, rs1, rs2):
    T, D = h1.shape
    BW = wb_out.shape[0]
    SH = wb_in.shape[2]
    TM = min(256, T)
    nT = T // TM
    nC = TM // CHUNK
    nP = BW // LANES

    def body(dh2_ref, h1_ref, q_ref, g2_ref, o_ref, k4_ref, v4_ref, kt_ref, sink_ref, wbout_ref, wbin_ref, gb_ref,
             rc_ref, rs1_ref, rs2_ref,
             dh1_ref, dz2_ref, n2_ref, y2_ref, dk_ref, dv_ref, dbq_ref, dgb_ref, dsink_ref, do_scr, dq_scr, dsacc_scr):
        i = pl.program_id(0)

        @pl.when(i == 0)
        def _():
            dk_ref[...] = jnp.zeros_like(dk_ref)
            dv_ref[...] = jnp.zeros_like(dv_ref)
            dbq_ref[...] = jnp.zeros_like(dbq_ref)
            dgb_ref[...] = jnp.zeros_like(dgb_ref)
            dsacc_scr[...] = jnp.zeros_like(dsacc_scr)

        dh2 = dh2_ref[...]
        dy2 = _dot_nt(dh2.astype(BF), wbout_ref[...])
        g2v = g2_ref[...].astype(F32)
        ov = o_ref[...].astype(F32)
        silu, dsilu = _silu_parts(g2v)
        y2_ref[...] = (ov * silu).astype(BF).T
        do_scr[...] = (dy2 * silu).astype(BF)
        dz2_ref[:, BW:] = (dy2 * ov * dsilu).astype(BF)
        upper = _upper()
        lo = _lane_lo((2 * CHUNK, LANES))
        for c in range(nC):
            ci = i * nC + c
            rows = slice(c * CHUNK, (c + 1) * CHUNK)
            pci = jnp.maximum(ci - 1, 0)
            prev = pl.multiple_of(pci * CHUNK, CHUNK)
            cur = pl.multiple_of(ci * CHUNK, CHUNK)
            qc = q_ref[rows, :]
            doc = do_scr[rows, :]
            dkb = jnp.zeros((2 * CHUNK, LANES), F32)
            dvb = jnp.zeros((2 * CHUNK, LANES), F32)
            for h in range(2):
                qs = _stack_pairs(qc, h)
                dos = _stack_pairs(doc, h)
                fa, fb = _fold(_dot_nt(_band_rows(k4_ref, prev, cur, h), qs), upper, ci > 0)
                dfa, dfb = _fold(_dot_nt(_band_rows(v4_ref, prev, cur, h), dos), upper)
                folded = []
                for k, (f, df) in enumerate(((fa, dfa), (fb, dfb))):
                    p, ps = _softmax_sink(f, sink_ref[2 * h + k:2 * h + k + 1, :])
                    delta = jnp.sum(p * df, axis=0, keepdims=True)
                    dsacc_scr[2 * h + k:2 * h + k + 1, :] -= ps * delta
                    folded.append((p, p * (df - delta)))
                pt = _unfold(folded[0][0], folded[1][0], upper).astype(BF)
                dst = _unfold(folded[0][1], folded[1][1], upper).astype(BF)
                dqt = _dot(_band_cols(kt_ref, pci, ci, h), dst)
                for j in range(4):
                    dq_scr[rows, (h * 4 + j) * LANES:(h * 4 + j + 1) * LANES] = dqt[:, j * CHUNK:(j + 1) * CHUNK].T
                for acc_name, g in (("k", _dot(dst, qs)), ("v", _dot(pt, dos))):
                    a, b = g[:2 * CHUNK], g[2 * CHUNK:]
                    if h == 0:
                        part = jnp.where(lo, a + pltpu.roll(b, HEAD_DIM, 1), 0.0)
                    else:
                        part = jnp.where(lo, 0.0, pltpu.roll(a, HEAD_DIM, 1) + b)
                    if acc_name == "k":
                        dkb += part
                    else:
                        dvb += part
            dk_ref[pl.ds(prev, CHUNK), :] += dkb[:CHUNK]
            dk_ref[pl.ds(cur, CHUNK), :] += dkb[CHUNK:]
            dv_ref[pl.ds(prev, CHUNK), :] += dvb[:CHUNK]
            dv_ref[pl.ds(cur, CHUNK), :] += dvb[CHUNK:]

        @pl.when(i == nT - 1)
        def _():
            lane = lax.broadcasted_iota(jnp.int32, dsink_ref.shape, 1)
            tot = jnp.zeros(dsink_ref.shape, F32)
            for j in range(4):
                tot += jnp.where(lane == j, jnp.sum(dsacc_scr[:, j * CHUNK:(j + 1) * CHUNK], axis=1, keepdims=True), 0.0)
            dsink_ref[...] = tot
        c_t, s1_t, s2_t = rc_ref[...], rs1_ref[...], rs2_ref[...]
        for p in range(nP):
            cols = slice(p * LANES, (p + 1) * LANES)
            dqp = _rot_bwd(dq_scr[:, cols] * (HEAD_DIM ** -0.5), c_t, s1_t, s2_t)
            dbq_ref[:, cols] += jnp.sum(dqp, axis=0, keepdims=True)
            dz2_ref[:, cols] = dqp.astype(BF)
        h1v = h1_ref[...]
        r2 = lax.rsqrt(jnp.mean(h1v * h1v, axis=-1, keepdims=True) + EPS)
        xh = h1v * r2
        gb = gb_ref[...]
        n2_ref[...] = (xh * gb).astype(BF).T
        dn2 = None
        for j in range(N_DEV):
            part = _dot_nt(dz2_ref[:, j * SH:(j + 1) * SH], wbin_ref[j])
            dn2 = part if dn2 is None else dn2 + part
        dgb_ref[...] += jnp.sum(dn2 * xh, axis=0, keepdims=True)
        dh1_ref[...] = dh2 + _rms_bwd(dn2, xh, r2, gb)

    row = functools.partial(_row_spec, TM)
    S = jax.ShapeDtypeStruct
    return pl.pallas_call(
        body, name="b_bwd", grid=(T // TM,),
        in_specs=[row(D), row(D), row(BW), row(BW), row(BW), _const_spec(k4.shape), _const_spec(v4.shape),
                  _const_spec(kt.shape), _const_spec(sinks.shape), _const_spec(wb_out.shape), _const_spec(wb_in.shape),
                  _const_spec((1, D)), row(LANES), row(LANES), row(LANES)],
        out_specs=[row(D), row(2 * BW), _col_spec(TM, D), _col_spec(TM, BW), _acc_spec((T, LANES)),
                   _acc_spec((T, LANES)), _acc_spec((1, BW)), _acc_spec((1, D)), _acc_spec((4, LANES))],
        out_shape=(S((T, D), F32), S((T, 2 * BW), BF), S((D, T), BF), S((BW, T), BF), S((T, LANES), F32),
                   S((T, LANES), F32), S((1, BW), F32), S((1, D), F32), S((4, LANES), F32)),
        scratch_shapes=[pltpu.VMEM((TM, BW), BF), pltpu.VMEM((TM, BW), F32), pltpu.VMEM((4, 4 * CHUNK), F32)],
        compiler_params=_params(("arbitrary",)),
    )(dh2, h1, q, g2, o, k4, v4, kt, sinks, wb_out, wb_in, g_b, rc, rs1, rs2)


def _a_bwd(dh1p, dk, dv, h1, g_kv, w_kv, wa_out, ws, ln_g, ln_b, u, gt, sv, vhat, rstd, rc, rs1, rs2, ready):
    T, D = h1.shape
    AW = wa_out.shape[0]
    G = ws.shape[0]
    TM = min(256, T)
    nT = T // TM
    nC = TM // CHUNK
    nr = len(ready)

    def body(dh1p_ref, dk_ref, dv_ref, h1_ref, gkv_ref, wkv_ref, waout_ref, ws_ref, lng_ref,
             lnb_ref, u_ref, gt_ref, sv_ref, vhat_ref, rstd_ref, rc_ref, rs1_ref, rs2_ref, *rest):
        ready_refs, rest = rest[:nr], rest[nr:]
        (dz_ref, y_ref, nkv_ref, dkv_ref, dh1_ref, dh1f_ref, dgkv_ref, dbkv_ref, dlng_ref, dlnb_ref,
         dws_ref, dbs_ref), rest = rest[:12], rest[12:]
        recv_refs, (dsv_scr, dvln_scr, ssem, rsem, lsem) = rest[:nr], rest[nr:]
        i = pl.program_id(0)
        exchanges = [_Direct(ready_refs[k], recv_refs[k], ssem.at[k], rsem.at[k], lsem.at[k], scatter=True)
                     for k in range(nr)]

        @pl.when(i == 0)
        def _():
            for e in exchanges:
                e.start()
            for r in (dgkv_ref, dbkv_ref, dlng_ref, dlnb_ref, dws_ref, dbs_ref):
                r[...] = jnp.zeros_like(r)

        dk_pre = _rot_bwd(dk_ref[...], rc_ref[...], rs1_ref[...], rs2_ref[...])
        dkv = jnp.concatenate([dk_pre, dv_ref[...]], axis=1)
        dbkv_ref[...] += jnp.sum(dkv, axis=0, keepdims=True)
        dkv_b = dkv.astype(BF)
        dkv_ref[...] = dkv_b
        h1v = h1_ref[...]
        rkv = lax.rsqrt(jnp.mean(h1v * h1v, axis=-1, keepdims=True) + EPS)
        xh_kv = h1v * rkv
        gkv = gkv_ref[...]
        nkv_ref[...] = (xh_kv * gkv).astype(BF).T
        dnkv = _dot_nt(dkv_b, wkv_ref[...])
        dgkv_ref[...] += jnp.sum(dnkv * xh_kv, axis=0, keepdims=True)
        dh1 = dh1p_ref[...] + _rms_bwd(dnkv, xh_kv, rkv, gkv)
        dh1_b = dh1.astype(BF)
        dh1_ref[...] = dh1_b
        dh1f_ref[...] = dh1
        dy = _dot_nt(dh1_b, waout_ref[...])
        uv = u_ref[...].astype(F32)
        gtv = gt_ref[...].astype(F32)
        svv = sv_ref[...].astype(F32)
        silu, dsilu = _silu_parts(gtv)
        us = uv * silu
        y_ref[...] = (us * svv).astype(BF).T
        dz_ref[:, :AW] = (dy * svv * silu).astype(BF)
        dz_ref[:, 2 * AW:] = (dy * uv * svv * dsilu).astype(BF)
        dsv_scr[...] = (dy * us).astype(BF)
        vhat_v = vhat_ref[...].astype(F32)
        lng = lng_ref[...]
        vln_b = (vhat_v * lng + lnb_ref[...]).astype(BF)
        tri = lax.broadcasted_iota(jnp.int32, (CHUNK, CHUNK), 0) >= lax.broadcasted_iota(jnp.int32, (CHUNK, CHUNK), 1)
        lane = lax.broadcasted_iota(jnp.int32, (CHUNK, LANES), 1)
        dbs = jnp.zeros((CHUNK, LANES), F32)
        for g in range(G):
            wsm = jnp.where(tri, ws_ref[g], 0.0).astype(BF)
            cols = slice(g * CHUNK, (g + 1) * CHUNK)
            dws_g = None
            for c in range(nC):
                rows = slice(c * CHUNK, (c + 1) * CHUNK)
                dsv_cg = dsv_scr[rows, cols]
                dvln_scr[rows, cols] = _dot_tn(wsm, dsv_cg)
                part = _dot_nt(dsv_cg, vln_b[rows, cols])
                dws_g = part if dws_g is None else dws_g + part
                dbs += jnp.where(lane == g, jnp.sum(dsv_cg.astype(F32), axis=-1, keepdims=True), 0.0)
            dws_ref[g] += jnp.where(tri, dws_g, 0.0)
        dbs_ref[...] += dbs
        dvln = dvln_scr[...]
        dlng_ref[...] += jnp.sum(dvln * vhat_v, axis=0, keepdims=True)
        dlnb_ref[...] += jnp.sum(dvln, axis=0, keepdims=True)
        a = dvln * lng
        dvv = rstd_ref[:, 0:1] * (a - jnp.mean(a, axis=-1, keepdims=True)
                                  - vhat_v * jnp.mean(a * vhat_v, axis=-1, keepdims=True))
        dz_ref[:, AW:2 * AW] = dvv.astype(BF)

        @pl.when(i == nT - 1)
        def _():
            for e in exchanges:
                e.finish()

    row = functools.partial(_row_spec, TM)
    col = functools.partial(_col_spec, TM)
    hbm = pl.BlockSpec(memory_space=pl.ANY)
    S = jax.ShapeDtypeStruct
    return pl.pallas_call(
        body, name="a_bwd", grid=(nT,),
        in_specs=[row(D), row(LANES), row(LANES), row(D), _const_spec((1, D)), _const_spec(w_kv.shape),
                  _const_spec(wa_out.shape), _const_spec(ws.shape),
                  _const_spec((1, AW)), _const_spec((1, AW)), row(AW), row(AW), row(AW), row(AW), row(LANES),
                  row(LANES), row(LANES), row(LANES)] + [hbm] * nr,
        out_specs=[row(3 * AW), col(AW), col(D), row(2 * LANES), row(D), row(D),
                   _acc_spec((1, D)), _acc_spec((1, 2 * LANES)), _acc_spec((1, AW)),
                   _acc_spec((1, AW)), _acc_spec(ws.shape), _acc_spec((CHUNK, LANES))] + [hbm] * nr,
        out_shape=(S((T, 3 * AW), BF), S((AW, T), BF), S((D, T), BF), S((T, 2 * LANES), BF), S((T, D), BF),
                   S((T, D), F32),
                   S((1, D), F32), S((1, 2 * LANES), F32), S((1, AW), F32), S((1, AW), F32),
                   S(ws.shape, F32), S((CHUNK, LANES), F32)) + tuple(S(r.shape, r.dtype) for r in ready),
        scratch_shapes=[pltpu.VMEM((TM, AW), BF), pltpu.VMEM((TM, AW), F32)] + _direct_sems(nr),
        compiler_params=_params(("arbitrary",)),
    )(dh1p, dk, dv, h1, g_kv, w_kv, wa_out, ws, ln_g, ln_b, u, gt, sv, vhat, rstd, rc, rs1, rs2, *ready)


def _a_in_bwd(dz, wa_in, x, dh1, g_a, ready):
    T, D = x.shape
    SH = wa_in.shape[2]
    TM = min(512, T)
    nT = T // TM
    nr = len(ready)

    def body(dz_ref, wain_ref, x_ref, dh1_ref, ga_ref, *rest):
        ready_refs, (dx_ref, n1_ref, dga_ref), rest = rest[:nr], rest[nr:nr + 3], rest[nr + 3:]
        recv_refs, (ssem, rsem, lsem) = rest[:nr], rest[nr:]
        i = pl.program_id(0)
        exchanges = [_Direct(ready_refs[k], recv_refs[k], ssem.at[k], rsem.at[k], lsem.at[k], scatter=True)
                     for k in range(nr)]

        @pl.when(i == 0)
        def _():
            for e in exchanges:
                e.start()
            dga_ref[...] = jnp.zeros_like(dga_ref)

        xv = x_ref[...]
        r1 = lax.rsqrt(jnp.mean(xv * xv, axis=-1, keepdims=True) + EPS)
        xh = xv * r1
        ga = ga_ref[...]
        n1_ref[...] = (xh * ga).astype(BF).T
        dn1 = None
        for j in range(N_DEV):
            part = _dot_nt(dz_ref[:, j * SH:(j + 1) * SH], wain_ref[j])
            dn1 = part if dn1 is None else dn1 + part
        dga_ref[...] += jnp.sum(dn1 * xh, axis=0, keepdims=True)
        dx_ref[...] = dh1_ref[...] + _rms_bwd(dn1, xh, r1, ga)

        @pl.when(i == nT - 1)
        def _():
            for e in exchanges:
                e.finish()

    row = functools.partial(_row_spec, TM)
    hbm = pl.BlockSpec(memory_space=pl.ANY)
    S = jax.ShapeDtypeStruct
    return pl.pallas_call(
        body, name="a_in_bwd", grid=(nT,),
        in_specs=[row(dz.shape[1]), _const_spec(wa_in.shape), row(D), row(D), _const_spec((1, D))] + [hbm] * nr,
        out_specs=[row(D), _col_spec(TM, D), _acc_spec((1, D))] + [hbm] * nr,
        out_shape=(S((T, D), F32), S((D, T), BF), S((1, D), F32)) + tuple(S(r.shape, r.dtype) for r in ready),
        scratch_shapes=_direct_sems(nr),
        compiler_params=_params(("arbitrary",)),
    )(dz, wa_in, x, dh1, g_a, *ready)


def _wgrad(at, b, nblk, name, bt=512):
    K, T = at.shape
    N = b.shape[1] // nblk
    BT = min(bt, T)
    nt = T // BT

    def body(a_ref, b_ref, o_ref, acc):
        t = pl.program_id(1)

        @pl.when(t == 0)
        def _():
            acc[...] = jnp.zeros_like(acc)

        acc[...] += _dot(a_ref[...], b_ref[...])

        @pl.when(t == nt - 1)
        def _():
            o_ref[0] = acc[...].astype(BF)

    return pl.pallas_call(
        body, name=name, grid=(nblk, nt),
        in_specs=[pl.BlockSpec((K, BT), lambda j, t: (0, t)), pl.BlockSpec((BT, N), lambda j, t: (t, j))],
        out_specs=pl.BlockSpec((1, K, N), lambda j, t: (j, 0, 0)),
        out_shape=jax.ShapeDtypeStruct((nblk, K, N), BF),
        scratch_shapes=[pltpu.VMEM((K, N), F32)],
        compiler_params=_params(("arbitrary", "arbitrary")),
    )(at, b)


def _wgrad_exchange(a, b, me, extras, name):
    K, T = a.shape
    N = b.shape[1] // N_DEV
    BT = min(1024, T)
    nt = T // BT
    ne = len(extras)
    last = N_DEV - 1
    n_chip = N_DEV // 2

    def body(me_ref, a_ref, b_ref, *rest):
        ex_in, recv_ref, ex_out = rest[:ne], rest[ne], rest[ne + 1:2 * ne + 1]
        acc, dstage, istage, half, d_s, d_r, i_s, i_r, lsem, ex_ssem, ex_rsem, ex_lsem = rest[2 * ne + 1:]
        s, t = pl.program_id(0), pl.program_id(1)
        x, y, c = (lax.axis_index(ax) for ax in AXES)
        ex = [_Direct(ex_in[k], ex_out[k], ex_ssem.at[k], ex_rsem.at[k], ex_lsem.at[k], scatter=True) for k in range(ne)]

        def to_sibling(k, slot):
            return pltpu.make_async_remote_copy(src_ref=dstage.at[slot], dst_ref=half.at[k], send_sem=d_s.at[k],
                                                recv_sem=d_r.at[k], device_id=(x, y, 1 - c), device_id_type=MESH)

        def to_chip(k, slot, sender):
            far = n_chip - 1 - k
            px, py = x ^ ((far >> 1) & 1), y ^ (far & 1)
            dst = recv_ref.at[2 * x + y] if sender else recv_ref.at[2 * px + py]
            return pltpu.make_async_remote_copy(src_ref=istage.at[slot], dst_ref=dst, send_sem=i_s.at[k],
                                                recv_sem=i_r.at[k], device_id=(px, py, c), device_id_type=MESH)

        @pl.when((s == 0) & (t == 0))
        def _():
            for e in ex:
                e.start()

        @pl.when(t == 0)
        def _():
            acc[...] = jnp.zeros_like(acc)

        acc[...] += _dot(a_ref[...], b_ref[...])

        @pl.when(t == nt - 1)
        def _():
            k = lax.div(s, 2)
            slot = lax.rem(k, 2)

            @pl.when(lax.rem(s, 2) == 0)
            def _():
                @pl.when(k >= 2)
                def _():
                    to_sibling(k - 2, slot).wait_send()

                dstage[slot] = acc[...].astype(BF)
                to_sibling(k, slot).start()

            @pl.when(lax.rem(s, 2) == 1)
            def _():
                to_sibling(k, slot).wait_recv()

                @pl.when(k >= 2)
                def _():
                    to_chip(k - 2, slot, True).wait_send()

                istage[slot] = (acc[...] + half[k].astype(F32)).astype(BF)

                @pl.when(k < n_chip - 1)
                def _():
                    to_chip(k, slot, True).start()

            @pl.when(s == last)
            def _():
                own = pltpu.make_async_copy(istage.at[slot], recv_ref.at[2 * x + y], lsem)
                own.start()
                to_chip(n_chip - 2, 0, True).wait_send()
                to_sibling(n_chip - 2, 0).wait_send()
                to_sibling(n_chip - 1, 1).wait_send()
                for kk in range(n_chip - 1):
                    to_chip(kk, 0, False).wait_recv()
                own.wait()
                for e in ex:
                    e.finish()

    hbm = pl.BlockSpec(memory_space=pl.ANY)
    dma = pltpu.SemaphoreType.DMA
    grid_spec = pltpu.PrefetchScalarGridSpec(
        num_scalar_prefetch=1, grid=(N_DEV, nt),
        in_specs=[pl.BlockSpec((K, BT), lambda s, t, me_ref: (0, t)),
                  pl.BlockSpec((BT, N), lambda s, t, me_ref: (t, me_ref[0] ^ (last - s)))] + [hbm] * ne,
        out_specs=[hbm] * (ne + 1),
        scratch_shapes=[pltpu.VMEM((K, N), F32), pltpu.VMEM((2, K, N), BF), pltpu.VMEM((2, K, N), BF),
                        pltpu.VMEM((n_chip, K, N), BF), dma((n_chip,)), dma((n_chip,)), dma((n_chip - 1,)),
                        dma((n_chip - 1,)), dma] + _direct_sems(ne))
    return pl.pallas_call(
        body, name=name, grid_spec=grid_spec,
        out_shape=[jax.ShapeDtypeStruct((n_chip, K, N), BF)] + [jax.ShapeDtypeStruct(e.shape, e.dtype) for e in extras],
        compiler_params=_params(("arbitrary", "arbitrary")),
    )(me, a, b, *extras)


def _my_index():
    return 4 * lax.axis_index("x") + 2 * lax.axis_index("y") + lax.axis_index("c")


def _all_gather(arrs, dtypes, name):
    n = len(arrs)

    def body(*refs):
        ins, outs = refs[:n], refs[n:2 * n]
        stages = refs[2 * n:3 * n]
        send_sems, recv_sems, local_sems = refs[3 * n:]
        x, y, c = lax.axis_index("x"), lax.axis_index("y"), lax.axis_index("c")
        me, sibling = (x, y, c), (x, y, 1 - c)
        chips = [(1 - x, y), (x, 1 - y), (1 - x, 1 - y)]

        def idx(p):
            return 4 * p[0] + 2 * p[1] + p[2]

        def copy(a, k, block, to, src=None):
            dst = outs[a].at[idx(block)]
            return pltpu.make_async_remote_copy(src_ref=dst if src is None else src, dst_ref=dst,
                                                send_sem=send_sems.at[a, k], recv_sem=recv_sems.at[a, k],
                                                device_id=to, device_id_type=MESH)

        owns, firsts, passed = [], [], []
        for a in range(n):
            stages[a][...] = ins[a][...].astype(stages[a].dtype)
            own = pltpu.make_async_copy(stages[a], outs[a].at[idx(me)], local_sems.at[a])
            own.start()
            owns.append(own)
            first = [copy(a, 0, me, sibling, src=stages[a])]
            first += [copy(a, 1 + j, me, (*chip, c), src=stages[a]) for j, chip in enumerate(chips)]
            for cp in first:
                cp.start()
            firsts += first
        for a in range(n):
            for j, chip in enumerate(chips):
                copy(a, 1 + j, (*chip, c), me).wait_recv()
                fwd = copy(a, 4 + j, (*chip, c), sibling)
                fwd.start()
                passed.append(fwd)
        for a in range(n):
            copy(a, 0, sibling, me).wait_recv()
            for j, chip in enumerate(chips):
                copy(a, 4 + j, (*chip, 1 - c), me).wait_recv()
        for cp in firsts + passed:
            cp.wait_send()
        for own in owns:
            own.wait()

    vm = pl.BlockSpec(memory_space=pltpu.VMEM)
    hbm = pl.BlockSpec(memory_space=pl.ANY)
    return pl.pallas_call(
        body, name=name,
        in_specs=[vm] * n, out_specs=[hbm] * n,
        out_shape=[jax.ShapeDtypeStruct((N_DEV,) + a.shape, dt) for a, dt in zip(arrs, dtypes)],
        scratch_shapes=[pltpu.VMEM(a.shape, dt) for a, dt in zip(arrs, dtypes)]
        + [pltpu.SemaphoreType.DMA((n, 7)), pltpu.SemaphoreType.DMA((n, 7)), pltpu.SemaphoreType.DMA((n,))],
        compiler_params=pltpu.CompilerParams(vmem_limit_bytes=VMEM_LIMIT),
    )(*arrs)


def _peer(mask):
    x, y, c = (lax.axis_index(a) for a in AXES)
    return (x ^ ((mask >> 2) & 1), y ^ ((mask >> 1) & 1), c ^ (mask & 1))


def _dev_index(p):
    return 4 * p[0] + 2 * p[1] + p[2]


class _Direct:
    def __init__(self, src, dst, send_sems, recv_sems, local_sem, scatter):
        me = _my_index()
        self.own = pltpu.make_async_copy(src.at[me] if scatter else src, dst.at[me], local_sem)
        self.sends, self.recvs = [], []
        for k in range(1, N_DEV):
            p = _peer(k)
            pi = _dev_index(p)
            sems = dict(send_sem=send_sems.at[k - 1], recv_sem=recv_sems.at[k - 1], device_id=p, device_id_type=MESH)
            self.sends.append(pltpu.make_async_remote_copy(src_ref=src.at[pi] if scatter else src, dst_ref=dst.at[me],
                                                           **sems))
            self.recvs.append(pltpu.make_async_remote_copy(src_ref=src.at[me] if scatter else src, dst_ref=dst.at[pi],
                                                           **sems))

    def start(self):
        self.own.start()
        for cp in self.sends:
            cp.start()

    def finish(self):
        for cp in self.sends:
            cp.wait_send()
        for cp in self.recvs:
            cp.wait_recv()
        self.own.wait()


def _direct_sems(n):
    return [pltpu.SemaphoreType.DMA((n, 7)), pltpu.SemaphoreType.DMA((n, 7)), pltpu.SemaphoreType.DMA((n,))]


def _adam_math(w, g, m, v):
    m = ADAM_B1 * m + (1.0 - ADAM_B1) * g
    v = ADAM_B2 * v + (1.0 - ADAM_B2) * (g * g)
    m_hat = m / (1.0 - ADAM_B1 ** ADAM_STEP)
    v_hat = v / (1.0 - ADAM_B2 ** ADAM_STEP)
    delta = -ADAM_LR * (m_hat / (jnp.sqrt(v_hat) + ADAM_EPS) + ADAM_WD * w)
    return delta, m, v


def _sum_adam(parts, w, m, v, name):
    R, C = w.shape
    NP = parts.shape[0]
    BR = CHUNK if R % CHUNK == 0 else R

    def body(p_ref, w_ref, m_ref, v_ref, g_ref, d_ref, nm_ref, nv_ref):
        g = p_ref[0].astype(F32)
        for i in range(1, NP):
            g = g + p_ref[i].astype(F32)
        g_ref[...] = g
        d_ref[...], nm_ref[...], nv_ref[...] = _adam_math(w_ref[...], g, m_ref[...], v_ref[...])

    blk = pl.BlockSpec((BR, C), lambda i: (i, 0))
    S = jax.ShapeDtypeStruct((R, C), F32)
    return pl.pallas_call(
        body, name=name, grid=(R // BR,),
        in_specs=[pl.BlockSpec((NP, BR, C), lambda i: (0, i, 0)), blk, blk, blk],
        out_specs=[blk] * 4, out_shape=(S,) * 4,
        compiler_params=_params(("arbitrary",)),
    )(parts, w, m, v)


def _sum8(parts, name):
    _, R, C = parts.shape

    def body(p_ref, o_ref):
        g = p_ref[0]
        for i in range(1, N_DEV):
            g = g + p_ref[i]
        o_ref[...] = g

    return pl.pallas_call(body, name=name, out_shape=jax.ShapeDtypeStruct((R, C), F32))(parts)


def _adam_only(g, w, m, v, name):
    def body(g_ref, w_ref, m_ref, v_ref, d_ref, nm_ref, nv_ref):
        d_ref[...], nm_ref[...], nv_ref[...] = _adam_math(w_ref[...], g_ref[...], m_ref[...], v_ref[...])

    S = jax.ShapeDtypeStruct(w.shape, F32)
    return pl.pallas_call(body, name=name, out_shape=(S,) * 3)(g, w, m, v)


def _rope_tables(T):
    pos = np.arange(T, dtype=np.float32)
    inv_freq = (np.float64(ROPE_THETA) ** (-np.arange(0, HEAD_DIM, 2, dtype=np.float64) / HEAD_DIM)).astype(np.float32)
    ang = (pos[:, None] * inv_freq[None, :]).astype(np.float64)
    cos, sin, zero = np.cos(ang).astype(np.float32), np.sin(ang).astype(np.float32), np.zeros(ang.shape, np.float32)
    c = np.concatenate([cos, cos, cos, cos], axis=1)
    s1 = np.concatenate([-sin, zero, -sin, zero], axis=1)
    s2 = np.concatenate([zero, sin, zero, sin], axis=1)
    return jnp.asarray(c), jnp.asarray(s1), jnp.asarray(s2)


SUBLANES = 8


def _nrows(size):
    return -(-size // (SUBLANES * LANES)) * SUBLANES


def _rows(a):
    flat = a.reshape(-1)
    pad = _nrows(flat.shape[0]) * LANES - flat.shape[0]
    if pad:
        flat = jnp.concatenate([flat, jnp.zeros((pad,), flat.dtype)])
    return flat.reshape(-1, LANES)


def _pack(arrs, total_rows):
    rows = [_rows(a) for a in arrs]
    used = sum(r.shape[0] for r in rows)
    if total_rows > used:
        rows.append(jnp.zeros((total_rows - used, LANES), F32))
    return jnp.concatenate(rows, axis=0)


def _unpack(packed, shapes):
    out, at = [], 0
    for shp in shapes:
        size = math.prod(shp)
        nrow = _nrows(size)
        out.append(packed[at:at + nrow].reshape(-1)[:size].reshape(shp))
        at += nrow
    return out


def kernel(x, a_norm_g, a_w_in, a_ln_g, a_ln_b, a_ws, a_bs, a_w_out, kv_norm_g, w_kv, b_kv, b_norm_g, b_w_in, b_bq, b_sinks, b_w_out, final_norm_g, loss_target, m_a_norm_g, m_a_w_in, m_a_ln_g, m_a_ln_b, m_a_ws, m_a_bs, m_a_w_out, m_kv_norm_g, m_w_kv, m_b_kv, m_b_norm_g, m_b_w_in, m_b_bq, m_b_sinks, m_b_w_out, m_final_norm_g, v_a_norm_g, v_a_w_in, v_a_ln_g, v_a_ln_b, v_a_ws, v_a_bs, v_a_w_out, v_kv_norm_g, v_w_kv, v_b_kv, v_b_norm_g, v_b_w_in, v_b_bq, v_b_sinks, v_b_w_out, v_final_norm_g):
    T, D = x.shape[1], x.shape[2]
    AW = a_ln_g.shape[1] * N_DEV
    G = a_ws.shape[1]
    assert w_kv.shape[1] == 2 * LANES and a_ws.shape[2] == CHUNK and T % CHUNK == 0
    me = _my_index()

    vec = jnp.concatenate([a_norm_g, a_ln_g, a_ln_b], axis=1)
    vec = jnp.broadcast_to(vec, (8, vec.shape[1]))
    wa_in, wa_out, wkv, vecs = _all_gather([a_w_in[0], a_w_out[0], w_kv, vec], [BF, BF, BF, F32], "gather_weights")
    wa_out = wa_out.reshape(AW, D)
    wkv = wkv.reshape(D, 2 * LANES)
    vecs = vecs[:, 0, :]
    ds = D // N_DEV
    g_a = vecs[:, :ds].reshape(1, D)
    ln_g = vecs[:, ds:ds + AW // N_DEV].reshape(1, AW)
    ln_b = vecs[:, ds + AW // N_DEV:].reshape(1, AW)

    rc, rs1, rs2 = _rope_tables(T)
    ws = a_ws[0]
    bs_t = a_bs[0].T
    g_kv = kv_norm_g.reshape(1, D)
    bkv = b_kv.reshape(1, -1)
    g_f = final_norm_g.reshape(1, D)
    sinks = jnp.repeat(b_sinks.reshape(2, 4, 2).transpose(0, 2, 1).reshape(4, 4), CHUNK, axis=1)
    xs, tgt = x[0], loss_target[0]

    h1, u, gt, sv, vhat, rstd, k4, v4, kt, vt, wb_in, wb_out = _a_fwd(
        xs, g_a, wa_in, ln_g, ln_b, ws, bs_t, wa_out, g_kv, wkv, bkv, rc, rs1, rs2, [b_w_in[0], b_w_out[0]])
    wb_out = wb_out.reshape(-1, D)
    q, g2, o, dh2, dh2_b, loss, d_gf = _b_fwd(h1, b_norm_g, wb_in, b_bq, rc, rs1, rs2, k4, vt, sinks, wb_out, g_f, tgt)
    dh1p, dz2, n2, y2, dk, dv, d_bq, d_gb, d_sink = _b_bwd(dh2, h1, q, g2, o, k4, v4, kt, sinks, wb_out, wb_in,
                                                           b_norm_g, rc, rs1, rs2)
    d_sink = d_sink[:, :4].reshape(2, 2, 4).transpose(0, 2, 1).reshape(1, 16)
    gw_b_in = _wgrad(n2, dz2, N_DEV, "wgrad_b_in", bt=2048)
    gw_b_out = _wgrad(y2, dh2_b, 1, "wgrad_b_out").reshape(N_DEV, -1, D)
    (dz, y, nkv, dkv, dh1, dh1_f, d_gkv, d_bkv, d_lng, d_lnb, d_ws, d_bst, r_b_in, r_b_out) = _a_bwd(
        dh1p, dk, dv, h1, g_kv, wkv, wa_out, ws, ln_g, ln_b, u, gt, sv, vhat, rstd, rc, rs1, rs2, [gw_b_in, gw_b_out])
    gw_a_out = _wgrad(y, dh1, 1, "wgrad_a_out").reshape(N_DEV, AW // N_DEV, D)
    gw_kv = _wgrad(nkv, dkv, 1, "wgrad_kv").reshape(N_DEV, D // N_DEV, 2 * LANES)
    dx, n1, d_ga, r_a_out, r_kv = _a_in_bwd(dz, wa_in, xs, dh1_f, g_a, [gw_a_out, gw_kv])
    small = [d_ws, d_bst[:, :G].T, d_gkv, d_bkv, d_gb, d_bq, d_sink, d_gf, d_ga, d_lng, d_lnb, loss]
    used = sum(_nrows(a.size) for a in small)
    per = -(-used // (SUBLANES * N_DEV)) * SUBLANES
    small_pack = _pack(small, per * N_DEV).reshape(N_DEV, per, LANES)
    r_a_in, r_small = _wgrad_exchange(n1, dz, me.reshape(1), [small_pack], "wgrad_a_in")

    g_a_in, d_a_in, nm_a_in, nv_a_in = _sum_adam(r_a_in, a_w_in[0], m_a_w_in[0], v_a_w_in[0], "adam_a_in")
    g_a_out, d_a_out, nm_a_out, nv_a_out = _sum_adam(r_a_out, a_w_out[0], m_a_w_out[0], v_a_w_out[0], "adam_a_out")
    g_kvw, d_kvw, nm_kvw, nv_kvw = _sum_adam(r_kv, w_kv, m_w_kv, v_w_kv, "adam_kv")
    g_b_in, d_b_in, nm_b_in, nv_b_in = _sum_adam(r_b_in, b_w_in[0], m_b_w_in[0], v_b_w_in[0], "adam_b_in")
    g_b_out, d_b_out, nm_b_out, nv_b_out = _sum_adam(r_b_out, b_w_out[0], m_b_w_out[0], v_b_w_out[0], "adam_b_out")

    red = _sum8(r_small, "sum_small")
    (full_small,) = _all_gather([red], [F32], "gather_small")
    full_small = full_small.reshape(N_DEV * per, LANES)
    rep_shapes = [a_ws.shape, a_bs.shape, kv_norm_g.shape, b_kv.shape, b_norm_g.shape, b_bq.shape, b_sinks.shape,
                  final_norm_g.shape]
    gs = _unpack(full_small, rep_shapes + [(N_DEV, a_norm_g.shape[1]), (N_DEV, a_ln_g.shape[1]),
                                           (N_DEV, a_ln_b.shape[1]), (1, 1)])
    loss = gs.pop()[0, 0]
    g_ang = lax.dynamic_slice_in_dim(gs[8], me, 1, axis=0)
    g_alng = lax.dynamic_slice_in_dim(gs[9], me, 1, axis=0)
    g_alnb = lax.dynamic_slice_in_dim(gs[10], me, 1, axis=0)
    sm_g = gs[:8] + [g_ang, g_alng, g_alnb]
    sm_shapes = [a.shape for a in sm_g]
    tot = sum(_nrows(a.size) for a in sm_g)
    pw = _pack([a_ws, a_bs, kv_norm_g, b_kv, b_norm_g, b_bq, b_sinks, final_norm_g, a_norm_g, a_ln_g, a_ln_b], tot)
    pm = _pack([m_a_ws, m_a_bs, m_kv_norm_g, m_b_kv, m_b_norm_g, m_b_bq, m_b_sinks, m_final_norm_g, m_a_norm_g,
                m_a_ln_g, m_a_ln_b], tot)
    pv = _pack([v_a_ws, v_a_bs, v_kv_norm_g, v_b_kv, v_b_norm_g, v_b_bq, v_b_sinks, v_final_norm_g, v_a_norm_g,
                v_a_ln_g, v_a_ln_b], tot)
    pg = _pack(sm_g, tot)
    pd, pnm, pnv = _adam_only(pg, pw, pm, pv, "adam_small")
    sd, snm, snv = _unpack(pd, sm_shapes), _unpack(pnm, sm_shapes), _unpack(pnv, sm_shapes)

    def order(big, sm):
        a_in, a_out, kvw, b_in, b_out = big
        ws_, bs_, kvg, bkv_, bng, bq_, snk, fng, ang, alng, alnb = sm
        return (ang, a_in[None], alng, alnb, ws_, bs_, a_out[None], kvg, kvw, bkv_, bng, b_in[None], bq_, snk,
                b_out[None], fng)

    grads = order((g_a_in, g_a_out, g_kvw, g_b_in, g_b_out), sm_g)
    deltas = order((d_a_in, d_a_out, d_kvw, d_b_in, d_b_out), sd)
    new_m = order((nm_a_in, nm_a_out, nm_kvw, nm_b_in, nm_b_out), snm)
    new_v = order((nv_a_in, nv_a_out, nv_kvw, nv_b_in, nv_b_out), snv)
    return (loss, dx[None], *grads, *deltas, *new_m, *new_v)
```

```python
import functools
import math

import jax
import jax.numpy as jnp
import numpy as np
from jax import lax
from jax.experimental import pallas as pl
from jax.experimental.pallas import tpu as pltpu

CHUNK = 128
HEAD_DIM = 64
ROPE_THETA = 10000.0
EPS = 1e-5
ADAM_LR = 0.001
ADAM_B1 = 0.9
ADAM_B2 = 0.999
ADAM_EPS = 1e-08
ADAM_WD = 0.01
ADAM_STEP = 10
N_DEV = 8
LANES = 128
NEG = -1e30

BF = jnp.bfloat16
F32 = jnp.float32
MESH = pl.DeviceIdType.MESH
AXES = ("x", "y", "c")
VMEM_LIMIT = 56 * 1024 * 1024


def _dot(a, b):
    return jnp.dot(a, b, preferred_element_type=F32)


def _dot_nt(a, b):
    return lax.dot_general(a, b, (((1,), (1,)), ((), ())), preferred_element_type=F32)


def _dot_tn(a, b):
    return lax.dot_general(a, b, (((0,), (0,)), ((), ())), preferred_element_type=F32)


def _const_spec(shape):
    nd = len(shape)
    return pl.BlockSpec(shape, lambda *_: (0,) * nd, pipeline_mode=pl.Buffered(1))


def _acc_spec(shape):
    nd = len(shape)
    return pl.BlockSpec(shape, lambda *_: (0,) * nd)


def _row_spec(tm, width):
    return pl.BlockSpec((tm, width), lambda i: (i, 0))


def _col_spec(tm, height):
    return pl.BlockSpec((height, tm), lambda i: (0, i))


def _params(sem):
    return pltpu.CompilerParams(dimension_semantics=sem, vmem_limit_bytes=VMEM_LIMIT)


def _rot(x, c, s1, s2):
    return x * c + pltpu.roll(x, 96, 1) * s1 + pltpu.roll(x, 32, 1) * s2


def _rot_bwd(d, c, s1, s2):
    return d * c + pltpu.roll(d * s1, 32, 1) + pltpu.roll(d * s2, 96, 1)


def _silu_parts(g):
    sg = jax.nn.sigmoid(g)
    return g * sg, sg * (1.0 + g * (1.0 - sg))


def _rms_bwd(dn, xh, r, g):
    a = dn * g
    return r * (a - xh * jnp.mean(a * xh, axis=-1, keepdims=True))


def _lane_lo(shape):
    return lax.broadcasted_iota(jnp.int32, shape, 1) < HEAD_DIM


def _split4(t):
    lo = _lane_lo(t.shape)
    tr = pltpu.roll(t, HEAD_DIM, 1)
    z = jnp.zeros_like(t)
    return jnp.concatenate([jnp.where(lo, t, z), jnp.where(lo, z, tr), jnp.where(lo, tr, z), jnp.where(lo, z, t)], axis=1)


def _stack_pairs(t, h):
    return jnp.concatenate([t[:, (h * 4 + j) * LANES:(h * 4 + j + 1) * LANES] for j in range(4)], axis=0)


def _upper():
    shape = (CHUNK, 4 * CHUNK)
    return lax.broadcasted_iota(jnp.int32, shape, 0) > (lax.broadcasted_iota(jnp.int32, shape, 1) & (CHUNK - 1))


def _band_rows(ref, prev, cur, h):
    a = slice(2 * h * LANES, (2 * h + 1) * LANES)
    b = slice((2 * h + 1) * LANES, (2 * h + 2) * LANES)
    return jnp.concatenate([ref[pl.ds(prev, CHUNK), a], ref[pl.ds(cur, CHUNK), a],
                            ref[pl.ds(prev, CHUNK), b], ref[pl.ds(cur, CHUNK), b]], axis=0)


def _band_cols(ref, pci, ci, h):
    a = slice(2 * h * LANES, (2 * h + 1) * LANES)
    b = slice((2 * h + 1) * LANES, (2 * h + 2) * LANES)
    return jnp.concatenate([ref[pci, a, :], ref[ci, a, :], ref[pci, b, :], ref[ci, b, :]], axis=1)


def _fold(t, upper, has_prev=None):
    out = []
    for k in range(2):
        prev = t[2 * k * CHUNK:(2 * k + 1) * CHUNK]
        if has_prev is not None:
            prev = jnp.where(has_prev, prev, NEG)
        out.append(jnp.where(upper, prev, t[(2 * k + 1) * CHUNK:(2 * k + 2) * CHUNK]))
    return out


def _unfold(fa, fb, upper):
    z = jnp.zeros_like(fa)
    return jnp.concatenate([jnp.where(upper, fa, z), jnp.where(upper, z, fa),
                            jnp.where(upper, fb, z), jnp.where(upper, z, fb)], axis=0)


def _softmax_sink(f, sink):
    m = jnp.maximum(jnp.max(f, axis=0, keepdims=True), sink)
    p = jnp.exp(f - m)
    es = jnp.exp(sink - m)
    inv = 1.0 / (jnp.sum(p, axis=0, keepdims=True) + es)
    return p * inv, es * inv


class _Riding:
    def __init__(self, shards, gathered, stages, sems, n_steps):
        self.shards, self.stages, self.n_steps = shards, stages, n_steps
        ssem, rsem, lsem = sems
        self.gathers = [_TwoLevel(stages[k], gathered[k], ssem.at[k], rsem.at[k], lsem.at[k])
                        for k in range(len(shards))]

    def begin(self, i):
        @pl.when(i == 0)
        def _():
            for shard, stage, g in zip(self.shards, self.stages, self.gathers):
                stage[...] = shard[...].astype(stage.dtype)
                g.start()

    def end(self, i):
        @pl.when(i == self.n_steps // 2)
        def _():
            for g in self.gathers:
                g.forward()

        @pl.when(i == self.n_steps - 1)
        def _():
            for g in self.gathers:
                g.finish()

    @staticmethod
    def specs(later):
        nl = len(later)
        hbm = pl.BlockSpec(memory_space=pl.ANY)
        return ([_const_spec(w.shape) for w in later], [hbm] * nl,
                tuple(jax.ShapeDtypeStruct((N_DEV,) + w.shape, BF) for w in later),
                [pltpu.VMEM(w.shape, BF) for w in later] + _direct_sems(nl))


def _in_proj(x, g_a, wa_in, later):
    T, D = x.shape
    SH = wa_in.shape[2]
    TM = min(512, T)
    nT = T // TM
    nl = len(later)

    def body(x_ref, ga_ref, wain_ref, *rest):
        shards, z_ref, gathered, stages, sems = rest[:nl], rest[nl], rest[nl + 1:2 * nl + 1], rest[2 * nl + 1:3 * nl + 1], rest[3 * nl + 1:]
        i = pl.program_id(0)
        riding = _Riding(shards, gathered, stages, sems, nT)
        riding.begin(i)
        xv = x_ref[...]
        r1 = lax.rsqrt(jnp.mean(xv * xv, axis=-1, keepdims=True) + EPS)
        n1 = (xv * r1 * ga_ref[...]).astype(BF)
        for j in range(N_DEV):
            z_ref[:, j * SH:(j + 1) * SH] = _dot(n1, wain_ref[j]).astype(BF)
        riding.end(i)

    row = functools.partial(_row_spec, TM)
    r_in, r_out, r_shape, r_scratch = _Riding.specs(later)
    return pl.pallas_call(
        body, name="a_in_proj", grid=(nT,),
        in_specs=[row(D), _const_spec((1, D)), _const_spec(wa_in.shape)] + r_in,
        out_specs=[row(N_DEV * SH)] + r_out,
        out_shape=(jax.ShapeDtypeStruct((T, N_DEV * SH), BF),) + r_shape,
        scratch_shapes=r_scratch,
        compiler_params=_params(("arbitrary",)),
    )(x, g_a, wa_in, *later)


def _a_fwd(x, z, ln_g, ln_b, ws, bs_t, wa_out, g_kv, w_kv, b_kv, rc, rs1, rs2, later):
    T, D = x.shape
    AW = wa_out.shape[0]
    G = ws.shape[0]
    TM = min(256, T)
    nT = T // TM
    nC = TM // CHUNK
    nl = len(later)

    def body(x_ref, u_ref, v_ref, gt_ref, lng_ref, lnb_ref, ws_ref, bst_ref, waout_ref, gkv_ref, wkv_ref, bkv_ref,
             rc_ref, rs1_ref, rs2_ref, *rest):
        shards, rest = rest[:nl], rest[nl:]
        (h1_ref, sv_ref, vhat_ref, rstd_ref, k4_ref, v4_ref, kt_ref, vt_ref), rest = rest[:8], rest[8:]
        gathered, sv_scr, stages, sems = rest[:nl], rest[nl], rest[nl + 1:2 * nl + 1], rest[2 * nl + 1:]
        i = pl.program_id(0)
        riding = _Riding(shards, gathered, stages, sems, nT)
        riding.begin(i)
        xv = x_ref[...]
        u = u_ref[...].astype(F32)
        v = v_ref[...].astype(F32)
        gt = gt_ref[...].astype(F32)
        mu = jnp.mean(v, axis=-1, keepdims=True)
        xc = v - mu
        rstd = lax.rsqrt(jnp.mean(xc * xc, axis=-1, keepdims=True) + EPS)
        vhat = xc * rstd
        vln = (vhat * lng_ref[...] + lnb_ref[...]).astype(BF)
        tri = lax.broadcasted_iota(jnp.int32, (CHUNK, CHUNK), 0) >= lax.broadcasted_iota(jnp.int32, (CHUNK, CHUNK), 1)
        for g in range(G):
            wsm = jnp.where(tri, ws_ref[g], 0.0).astype(BF)
            bias = bst_ref[:, g:g + 1]
            for c in range(nC):
                blk = vln[c * CHUNK:(c + 1) * CHUNK, g * CHUNK:(g + 1) * CHUNK]
                sv_scr[c * CHUNK:(c + 1) * CHUNK, g * CHUNK:(g + 1) * CHUNK] = _dot(wsm, blk) + bias
        sv = sv_scr[...]
        silu, _ = _silu_parts(gt)
        y = (u * sv * silu).astype(BF)
        h1 = xv + _dot(y, waout_ref[...])
        h1_ref[...] = h1
        sv_ref[...] = sv.astype(BF)
        vhat_ref[...] = vhat.astype(BF)
        rstd_ref[...] = jnp.broadcast_to(rstd, rstd_ref.shape)
        rkv = lax.rsqrt(jnp.mean(h1 * h1, axis=-1, keepdims=True) + EPS)
        nkv = (h1 * rkv * gkv_ref[...]).astype(BF)
        kv = _dot(nkv, wkv_ref[...]) + bkv_ref[...]
        k_rot = _rot(kv[:, :LANES], rc_ref[...], rs1_ref[...], rs2_ref[...])
        for src, ref, tref in ((k_rot, k4_ref, kt_ref), (kv[:, LANES:], v4_ref, vt_ref)):
            t4 = _split4(src)
            ref[...] = t4.astype(BF)
            for c in range(nC):
                for b in range(4):
                    blk = t4[c * CHUNK:(c + 1) * CHUNK, b * LANES:(b + 1) * LANES]
                    tref[c, b * LANES:(b + 1) * LANES, :] = blk.T.astype(BF)
        riding.end(i)

    row = functools.partial(_row_spec, TM)
    zcol = [pl.BlockSpec((TM, AW), functools.partial(lambda k, i: (i, k), k)) for k in range(3)]
    tr = pl.BlockSpec((nC, 4 * LANES, CHUNK), lambda i: (i, 0, 0))
    r_in, r_out, r_shape, r_scratch = _Riding.specs(later)
    S = jax.ShapeDtypeStruct
    return pl.pallas_call(
        body, name="a_fwd", grid=(nT,),
        in_specs=[row(D)] + zcol + [_const_spec((1, AW)), _const_spec((1, AW)),
                  _const_spec(ws.shape), _const_spec(bs_t.shape), _const_spec(wa_out.shape), _const_spec((1, D)),
                  _const_spec(w_kv.shape), _const_spec((1, 2 * LANES)), row(LANES), row(LANES), row(LANES)] + r_in,
        out_specs=[row(D), row(AW), row(AW), row(LANES), row(4 * LANES), row(4 * LANES), tr, tr] + r_out,
        out_shape=(S((T, D), F32), S((T, AW), BF), S((T, AW), BF), S((T, LANES), F32),
                   S((T, 4 * LANES), BF), S((T, 4 * LANES), BF),
                   S((T // CHUNK, 4 * LANES, CHUNK), BF), S((T // CHUNK, 4 * LANES, CHUNK), BF)) + r_shape,
        scratch_shapes=[pltpu.VMEM((TM, AW), F32)] + r_scratch,
        compiler_params=_params(("arbitrary",)),
    )(x, z, z, z, ln_g, ln_b, ws, bs_t, wa_out, g_kv, w_kv, b_kv, rc, rs1, rs2, *later)


def _b_fwd(h1, g_b, wb_in, bq, rc, rs1, rs2, k4, vt, sinks, wb_out, g_f, target):
    T, D = h1.shape
    BW = wb_out.shape[0]
    SH = wb_in.shape[2]
    TM = min(256, T)
    nC = TM // CHUNK
    nP = BW // LANES

    def body(h1_ref, gb_ref, wbin_ref, bq_ref, rc_ref, rs1_ref, rs2_ref, k4_ref, vt_ref, sink_ref, wbout_ref, gf_ref,
             tgt_ref, q_ref, g2_ref, o_ref, dh2_ref, dh2b_ref, loss_ref, dgf_ref, z_scr, o_scr):
        i = pl.program_id(0)
        h1v = h1_ref[...]
        r2 = lax.rsqrt(jnp.mean(h1v * h1v, axis=-1, keepdims=True) + EPS)
        n2 = (h1v * r2 * gb_ref[...]).astype(BF)
        for j in range(N_DEV):
            z_scr[:, j * SH:(j + 1) * SH] = _dot(n2, wbin_ref[j])
        c_t, s1_t, s2_t = rc_ref[...], rs1_ref[...], rs2_ref[...]
        for p in range(nP):
            cols = slice(p * LANES, (p + 1) * LANES)
            qp = _rot(z_scr[:, cols] + bq_ref[:, cols], c_t, s1_t, s2_t) * (HEAD_DIM ** -0.5)
            q_ref[:, cols] = qp.astype(BF)
        g2 = z_scr[:, BW:]
        g2_ref[...] = g2.astype(BF)
        upper = _upper()
        for c in range(nC):
            ci = i * nC + c
            rows = slice(c * CHUNK, (c + 1) * CHUNK)
            pci = jnp.maximum(ci - 1, 0)
            prev = pl.multiple_of(pci * CHUNK, CHUNK)
            cur = pl.multiple_of(ci * CHUNK, CHUNK)
            qc = q_ref[rows, :]
            for h in range(2):
                st = _dot_nt(_band_rows(k4_ref, prev, cur, h), _stack_pairs(qc, h))
                fa, fb = _fold(st, upper, ci > 0)
                pa, _ = _softmax_sink(fa, sink_ref[2 * h:2 * h + 1, :])
                pb, _ = _softmax_sink(fb, sink_ref[2 * h + 1:2 * h + 2, :])
                ot = _dot(_band_cols(vt_ref, pci, ci, h), _unfold(pa, pb, upper).astype(BF))
                for j in range(4):
                    o_scr[rows, (h * 4 + j) * LANES:(h * 4 + j + 1) * LANES] = ot[:, j * CHUNK:(j + 1) * CHUNK].T
        o = o_scr[...]
        o_ref[...] = o.astype(BF)
        silu, _ = _silu_parts(g2)
        h2 = h1v + _dot((o * silu).astype(BF), wbout_ref[...])
        rf = lax.rsqrt(jnp.mean(h2 * h2, axis=-1, keepdims=True) + EPS)
        xh = h2 * rf
        gf = gf_ref[...]
        err = xh * gf - tgt_ref[...]
        dyf = err * (1.0 / D)
        dh2 = _rms_bwd(dyf, xh, rf, gf)
        dh2_ref[...] = dh2
        dh2b_ref[...] = dh2.astype(BF)

        @pl.when(i == 0)
        def _():
            loss_ref[...] = jnp.zeros_like(loss_ref)
            dgf_ref[...] = jnp.zeros_like(dgf_ref)

        loss_ref[...] += 0.5 * jnp.sum(jnp.mean(err * err, axis=-1, keepdims=True), axis=0, keepdims=True)
        dgf_ref[...] += jnp.sum(dyf * xh, axis=0, keepdims=True)

    row = functools.partial(_row_spec, TM)
    S = jax.ShapeDtypeStruct
    return pl.pallas_call(
        body, name="b_fwd", grid=(T // TM,),
        in_specs=[row(D), _const_spec((1, D)), _const_spec(wb_in.shape), _const_spec((1, BW)), row(LANES), row(LANES),
                  row(LANES), _const_spec(k4.shape), _const_spec(vt.shape), _const_spec(sinks.shape),
                  _const_spec(wb_out.shape), _const_spec((1, D)), row(D)],
        out_specs=[row(BW), row(BW), row(BW), row(D), row(D), _acc_spec((1, 1)), _acc_spec((1, D))],
        out_shape=(S((T, BW), BF), S((T, BW), BF), S((T, BW), BF), S((T, D), F32), S((T, D), BF), S((1, 1), F32),
                   S((1, D), F32)),
        scratch_shapes=[pltpu.VMEM((TM, 2 * BW), F32), pltpu.VMEM((TM, BW), F32)],
        compiler_params=_params(("arbitrary",)),
    )(h1, g_b, wb_in, bq, rc, rs1, rs2, k4, vt, sinks, wb_out, g_f, target)


def _b_bwd(dh2, h1, q, g2, o, k4, v4, kt, sinks, wb_out, wb_in, g_b, rc, rs1, rs2):
    T, D = h1.shape
    BW = wb_out.shape[0]
    SH = wb_in.shape[2]
    TM = min(256, T)
    nT = T // TM
    nC = TM // CHUNK
    nP = BW // LANES

    def body(dh2_ref, h1_ref, q_ref, g2_ref, o_ref, k4_ref, v4_ref, kt_ref, sink_ref, wbout_ref, wbin_ref, gb_ref,
             rc_ref, rs1_ref, rs2_ref,
             dh1_ref, dz2_ref, n2_ref, y2_ref, dk_ref, dv_ref, dbq_ref, dgb_ref, dsink_ref, do_scr, dq_scr, dsacc_scr):
        i = pl.program_id(0)

        @pl.when(i == 0)
        def _():
            dk_ref[...] = jnp.zeros_like(dk_ref)
            dv_ref[...] = jnp.zeros_like(dv_ref)
            dbq_ref[...] = jnp.zeros_like(dbq_ref)
            dgb_ref[...] = jnp.zeros_like(dgb_ref)
            dsacc_scr[...] = jnp.zeros_like(dsacc_scr)

        dh2 = dh2_ref[...]
        dy2 = _dot_nt(dh2.astype(BF), wbout_ref[...])
        g2v = g2_ref[...].astype(F32)
        ov = o_ref[...].astype(F32)
        silu, dsilu = _silu_parts(g2v)
        y2_ref[...] = (ov * silu).astype(BF).T
        do_scr[...] = (dy2 * silu).astype(BF)
        dz2_ref[:, BW:] = (dy2 * ov * dsilu).astype(BF)
        upper = _upper()
        lo = _lane_lo((2 * CHUNK, LANES))
        for c in range(nC):
            ci = i * nC + c
            rows = slice(c * CHUNK, (c + 1) * CHUNK)
            pci = jnp.maximum(ci - 1, 0)
            prev = pl.multiple_of(pci * CHUNK, CHUNK)
            cur = pl.multiple_of(ci * CHUNK, CHUNK)
            qc = q_ref[rows, :]
            doc = do_scr[rows, :]
            dkb = jnp.zeros((2 * CHUNK, LANES), F32)
            dvb = jnp.zeros((2 * CHUNK, LANES), F32)
            for h in range(2):
                qs = _stack_pairs(qc, h)
                dos = _stack_pairs(doc, h)
                fa, fb = _fold(_dot_nt(_band_rows(k4_ref, prev, cur, h), qs), upper, ci > 0)
                dfa, dfb = _fold(_dot_nt(_band_rows(v4_ref, prev, cur, h), dos), upper)
                folded = []
                for k, (f, df) in enumerate(((fa, dfa), (fb, dfb))):
                    p, ps = _softmax_sink(f, sink_ref[2 * h + k:2 * h + k + 1, :])
                    delta = jnp.sum(p * df, axis=0, keepdims=True)
                    dsacc_scr[2 * h + k:2 * h + k + 1, :] -= ps * delta
                    folded.append((p, p * (df - delta)))
                pt = _unfold(folded[0][0], folded[1][0], upper).astype(BF)
                dst = _unfold(folded[0][1], folded[1][1], upper).astype(BF)
                dqt = _dot(_band_cols(kt_ref, pci, ci, h), dst)
                for j in range(4):
                    dq_scr[rows, (h * 4 + j) * LANES:(h * 4 + j + 1) * LANES] = dqt[:, j * CHUNK:(j + 1) * CHUNK].T
                for acc_name, g in (("k", _dot(dst, qs)), ("v", _dot(pt, dos))):
                    a, b = g[:2 * CHUNK], g[2 * CHUNK:]
                    if h == 0:
                        part = jnp.where(lo, a + pltpu.roll(b, HEAD_DIM, 1), 0.0)
                    else:
                        part = jnp.where(lo, 0.0, pltpu.roll(a, HEAD_DIM, 1) + b)
                    if acc_name == "k":
                        dkb += part
                    else:
                        dvb += part
            dk_ref[pl.ds(prev, CHUNK), :] += dkb[:CHUNK]
            dk_ref[pl.ds(cur, CHUNK), :] += dkb[CHUNK:]
            dv_ref[pl.ds(prev, CHUNK), :] += dvb[:CHUNK]
            dv_ref[pl.ds(cur, CHUNK), :] += dvb[CHUNK:]

        @pl.when(i == nT - 1)
        def _():
            lane = lax.broadcasted_iota(jnp.int32, dsink_ref.shape, 1)
            tot = jnp.zeros(dsink_ref.shape, F32)
            for j in range(4):
                tot += jnp.where(lane == j, jnp.sum(dsacc_scr[:, j * CHUNK:(j + 1) * CHUNK], axis=1, keepdims=True), 0.0)
            dsink_ref[...] = tot
        c_t, s1_t, s2_t = rc_ref[...], rs1_ref[...], rs2_ref[...]
        for p in range(nP):
            cols = slice(p * LANES, (p + 1) * LANES)
            dqp = _rot_bwd(dq_scr[:, cols] * (HEAD_DIM ** -0.5), c_t, s1_t, s2_t)
            dbq_ref[:, cols] += jnp.sum(dqp, axis=0, keepdims=True)
            dz2_ref[:, cols] = dqp.astype(BF)
        h1v = h1_ref[...]
        r2 = lax.rsqrt(jnp.mean(h1v * h1v, axis=-1, keepdims=True) + EPS)
        xh = h1v * r2
        gb = gb_ref[...]
        n2_ref[...] = (xh * gb).astype(BF).T
        dn2 = None
        for j in range(N_DEV):
            part = _dot_nt(dz2_ref[:, j * SH:(j + 1) * SH], wbin_ref[j])
            dn2 = part if dn2 is None else dn2 + part
        dgb_ref[...] += jnp.sum(dn2 * xh, axis=0, keepdims=True)
        dh1_ref[...] = dh2 + _rms_bwd(dn2, xh, r2, gb)

    row = functools.partial(_row_spec, TM)
    S = jax.ShapeDtypeStruct
    return pl.pallas_call(
        body, name="b_bwd", grid=(T // TM,),
        in_specs=[row(D), row(D), row(BW), row(BW), row(BW), _const_spec(k4.shape), _const_spec(v4.shape),
                  _const_spec(kt.shape), _const_spec(sinks.shape), _const_spec(wb_out.shape), _const_spec(wb_in.shape),
                  _const_spec((1, D)), row(LANES), row(LANES), row(LANES)],
        out_specs=[row(D), row(2 * BW), _col_spec(TM, D), _col_spec(TM, BW), _acc_spec((T, LANES)),
                   _acc_spec((T, LANES)), _acc_spec((1, BW)), _acc_spec((1, D)), _acc_spec((4, LANES))],
        out_shape=(S((T, D), F32), S((T, 2 * BW), BF), S((D, T), BF), S((BW, T), BF), S((T, LANES), F32),
                   S((T, LANES), F32), S((1, BW), F32), S((1, D), F32), S((4, LANES), F32)),
        scratch_shapes=[pltpu.VMEM((TM, BW), BF), pltpu.VMEM((TM, BW), F32), pltpu.VMEM((4, 4 * CHUNK), F32)],
        compiler_params=_params(("arbitrary",)),
    )(dh2, h1, q, g2, o, k4, v4, kt, sinks, wb_out, wb_in, g_b, rc, rs1, rs2)


def _a_bwd(dh1p, dk, dv, h1, g_kv, w_kv, wa_out, ws, ln_g, ln_b, z, sv, vhat, rstd, rc, rs1, rs2, ready):
    T, D = h1.shape
    AW = wa_out.shape[0]
    G = ws.shape[0]
    TM = min(256, T)
    nT = T // TM
    nC = TM // CHUNK
    nr = len(ready)

    def body(dh1p_ref, dk_ref, dv_ref, h1_ref, gkv_ref, wkv_ref, waout_ref, ws_ref, lng_ref,
             lnb_ref, u_ref, gt_ref, sv_ref, vhat_ref, rstd_ref, rc_ref, rs1_ref, rs2_ref, *rest):
        ready_refs, rest = rest[:nr], rest[nr:]
        (dz_ref, y_ref, nkv_ref, dkv_ref, dh1_ref, dh1f_ref, dgkv_ref, dbkv_ref, dlng_ref, dlnb_ref,
         dws_ref, dbs_ref), rest = rest[:12], rest[12:]
        recv_refs, (dsv_scr, dvln_scr, ssem, rsem, lsem) = rest[:nr], rest[nr:]
        i = pl.program_id(0)
        exchanges = [_Direct(ready_refs[k], recv_refs[k], ssem.at[k], rsem.at[k], lsem.at[k], scatter=True)
                     for k in range(nr)]

        @pl.when(i == 0)
        def _():
            for e in exchanges:
                e.start()
            for r in (dgkv_ref, dbkv_ref, dlng_ref, dlnb_ref, dws_ref, dbs_ref):
                r[...] = jnp.zeros_like(r)

        dk_pre = _rot_bwd(dk_ref[...], rc_ref[...], rs1_ref[...], rs2_ref[...])
        dkv = jnp.concatenate([dk_pre, dv_ref[...]], axis=1)
        dbkv_ref[...] += jnp.sum(dkv, axis=0, keepdims=True)
        dkv_b = dkv.astype(BF)
        dkv_ref[...] = dkv_b
        h1v = h1_ref[...]
        rkv = lax.rsqrt(jnp.mean(h1v * h1v, axis=-1, keepdims=True) + EPS)
        xh_kv = h1v * rkv
        gkv = gkv_ref[...]
        nkv_ref[...] = (xh_kv * gkv).astype(BF).T
        dnkv = _dot_nt(dkv_b, wkv_ref[...])
        dgkv_ref[...] += jnp.sum(dnkv * xh_kv, axis=0, keepdims=True)
        dh1 = dh1p_ref[...] + _rms_bwd(dnkv, xh_kv, rkv, gkv)
        dh1_b = dh1.astype(BF)
        dh1_ref[...] = dh1_b
        dh1f_ref[...] = dh1
        dy = _dot_nt(dh1_b, waout_ref[...])
        uv = u_ref[...].astype(F32)
        gtv = gt_ref[...].astype(F32)
        svv = sv_ref[...].astype(F32)
        silu, dsilu = _silu_parts(gtv)
        us = uv * silu
        y_ref[...] = (us * svv).astype(BF).T
        dz_ref[:, :AW] = (dy * svv * silu).astype(BF)
        dz_ref[:, 2 * AW:] = (dy * uv * svv * dsilu).astype(BF)
        dsv_scr[...] = (dy * us).astype(BF)
        vhat_v = vhat_ref[...].astype(F32)
        lng = lng_ref[...]
        vln_b = (vhat_v * lng + lnb_ref[...]).astype(BF)
        tri = lax.broadcasted_iota(jnp.int32, (CHUNK, CHUNK), 0) >= lax.broadcasted_iota(jnp.int32, (CHUNK, CHUNK), 1)
        lane = lax.broadcasted_iota(jnp.int32, (CHUNK, LANES), 1)
        dbs = jnp.zeros((CHUNK, LANES), F32)
        for g in range(G):
            wsm = jnp.where(tri, ws_ref[g], 0.0).astype(BF)
            cols = slice(g * CHUNK, (g + 1) * CHUNK)
            dws_g = None
            for c in range(nC):
                rows = slice(c * CHUNK, (c + 1) * CHUNK)
                dsv_cg = dsv_scr[rows, cols]
                dvln_scr[rows, cols] = _dot_tn(wsm, dsv_cg)
                part = _dot_nt(dsv_cg, vln_b[rows, cols])
                dws_g = part if dws_g is None else dws_g + part
                dbs += jnp.where(lane == g, jnp.sum(dsv_cg.astype(F32), axis=-1, keepdims=True), 0.0)
            dws_ref[g] += jnp.where(tri, dws_g, 0.0)
        dbs_ref[...] += dbs
        dvln = dvln_scr[...]
        dlng_ref[...] += jnp.sum(dvln * vhat_v, axis=0, keepdims=True)
        dlnb_ref[...] += jnp.sum(dvln, axis=0, keepdims=True)
        a = dvln * lng
        dvv = rstd_ref[:, 0:1] * (a - jnp.mean(a, axis=-1, keepdims=True)
                                  - vhat_v * jnp.mean(a * vhat_v, axis=-1, keepdims=True))
        dz_ref[:, AW:2 * AW] = dvv.astype(BF)

        @pl.when(i == nT - 1)
        def _():
            for e in exchanges:
                e.finish()

    row = functools.partial(_row_spec, TM)
    col = functools.partial(_col_spec, TM)
    hbm = pl.BlockSpec(memory_space=pl.ANY)
    S = jax.ShapeDtypeStruct
    return pl.pallas_call(
        body, name="a_bwd", grid=(nT,),
        in_specs=[row(D), row(LANES), row(LANES), row(D), _const_spec((1, D)), _const_spec(w_kv.shape),
                  _const_spec(wa_out.shape), _const_spec(ws.shape),
                  _const_spec((1, AW)), _const_spec((1, AW)), pl.BlockSpec((TM, AW), lambda i: (i, 0)),
                  pl.BlockSpec((TM, AW), lambda i: (i, 2)), row(AW), row(AW), row(LANES),
                  row(LANES), row(LANES), row(LANES)] + [hbm] * nr,
        out_specs=[row(3 * AW), col(AW), col(D), row(2 * LANES), row(D), row(D),
                   _acc_spec((1, D)), _acc_spec((1, 2 * LANES)), _acc_spec((1, AW)),
                   _acc_spec((1, AW)), _acc_spec(ws.shape), _acc_spec((CHUNK, LANES))] + [hbm] * nr,
        out_shape=(S((T, 3 * AW), BF), S((AW, T), BF), S((D, T), BF), S((T, 2 * LANES), BF), S((T, D), BF),
                   S((T, D), F32),
                   S((1, D), F32), S((1, 2 * LANES), F32), S((1, AW), F32), S((1, AW), F32),
                   S(ws.shape, F32), S((CHUNK, LANES), F32)) + tuple(S(r.shape, r.dtype) for r in ready),
        scratch_shapes=[pltpu.VMEM((TM, AW), BF), pltpu.VMEM((TM, AW), F32)] + _direct_sems(nr),
        compiler_params=_params(("arbitrary",)),
    )(dh1p, dk, dv, h1, g_kv, w_kv, wa_out, ws, ln_g, ln_b, z, z, sv, vhat, rstd, rc, rs1, rs2, *ready)


def _a_in_bwd(dz, wa_in, x, dh1, g_a, ready):
    T, D = x.shape
    SH = wa_in.shape[2]
    TM = min(512, T)
    nT = T // TM
    nr = len(ready)

    def body(dz_ref, wain_ref, x_ref, dh1_ref, ga_ref, *rest):
        ready_refs, (dx_ref, n1_ref, dga_ref), rest = rest[:nr], rest[nr:nr + 3], rest[nr + 3:]
        recv_refs, (ssem, rsem, lsem) = rest[:nr], rest[nr:]
        i = pl.program_id(0)
        exchanges = [_Direct(ready_refs[k], recv_refs[k], ssem.at[k], rsem.at[k], lsem.at[k], scatter=True)
                     for k in range(nr)]

        @pl.when(i == 0)
        def _():
            for e in exchanges:
                e.start()
            dga_ref[...] = jnp.zeros_like(dga_ref)

        xv = x_ref[...]
        r1 = lax.rsqrt(jnp.mean(xv * xv, axis=-1, keepdims=True) + EPS)
        xh = xv * r1
        ga = ga_ref[...]
        n1_ref[...] = (xh * ga).astype(BF).T
        dn1 = None
        for j in range(N_DEV):
            part = _dot_nt(dz_ref[:, j * SH:(j + 1) * SH], wain_ref[j])
            dn1 = part if dn1 is None else dn1 + part
        dga_ref[...] += jnp.sum(dn1 * xh, axis=0, keepdims=True)
        dx_ref[...] = dh1_ref[...] + _rms_bwd(dn1, xh, r1, ga)

        @pl.when(i == nT - 1)
        def _():
            for e in exchanges:
                e.finish()

    row = functools.partial(_row_spec, TM)
    hbm = pl.BlockSpec(memory_space=pl.ANY)
    S = jax.ShapeDtypeStruct
    return pl.pallas_call(
        body, name="a_in_bwd", grid=(nT,),
        in_specs=[row(dz.shape[1]), _const_spec(wa_in.shape), row(D), row(D), _const_spec((1, D))] + [hbm] * nr,
        out_specs=[row(D), _col_spec(TM, D), _acc_spec((1, D))] + [hbm] * nr,
        out_shape=(S((T, D), F32), S((D, T), BF), S((1, D), F32)) + tuple(S(r.shape, r.dtype) for r in ready),
        scratch_shapes=_direct_sems(nr),
        compiler_params=_params(("arbitrary",)),
    )(dz, wa_in, x, dh1, g_a, *ready)


def _wgrad(at, b, nblk, name, bt=512):
    K, T = at.shape
    N = b.shape[1] // nblk
    BT = min(bt, T)
    nt = T // BT

    def body(a_ref, b_ref, o_ref, acc):
        t = pl.program_id(1)

        @pl.when(t == 0)
        def _():
            acc[...] = jnp.zeros_like(acc)

        acc[...] += _dot(a_ref[...], b_ref[...])

        @pl.when(t == nt - 1)
        def _():
            o_ref[0] = acc[...].astype(BF)

    return pl.pallas_call(
        body, name=name, grid=(nblk, nt),
        in_specs=[pl.BlockSpec((K, BT), lambda j, t: (0, t)), pl.BlockSpec((BT, N), lambda j, t: (t, j))],
        out_specs=pl.BlockSpec((1, K, N), lambda j, t: (j, 0, 0)),
        out_shape=jax.ShapeDtypeStruct((nblk, K, N), BF),
        scratch_shapes=[pltpu.VMEM((K, N), F32)],
        compiler_params=_params(("arbitrary", "arbitrary")),
    )(at, b)


def _wgrad_exchange(a, b, me, extras, name):
    K, T = a.shape
    N = b.shape[1] // N_DEV
    BT = min(1024, T)
    nt = T // BT
    ne = len(extras)
    last = N_DEV - 1
    n_chip = N_DEV // 2

    def body(me_ref, a_ref, b_ref, *rest):
        ex_in, recv_ref, ex_out = rest[:ne], rest[ne], rest[ne + 1:2 * ne + 1]
        acc, dstage, istage, half, d_s, d_r, i_s, i_r, lsem, ex_ssem, ex_rsem, ex_lsem = rest[2 * ne + 1:]
        s, t = pl.program_id(0), pl.program_id(1)
        x, y, c = (lax.axis_index(ax) for ax in AXES)
        ex = [_Direct(ex_in[k], ex_out[k], ex_ssem.at[k], ex_rsem.at[k], ex_lsem.at[k], scatter=True) for k in range(ne)]

        def to_sibling(k, slot):
            return pltpu.make_async_remote_copy(src_ref=dstage.at[slot], dst_ref=half.at[k], send_sem=d_s.at[k],
                                                recv_sem=d_r.at[k], device_id=(x, y, 1 - c), device_id_type=MESH)

        def to_chip(k, slot, sender):
            far = n_chip - 1 - k
            px, py = x ^ ((far >> 1) & 1), y ^ (far & 1)
            dst = recv_ref.at[2 * x + y] if sender else recv_ref.at[2 * px + py]
            return pltpu.make_async_remote_copy(src_ref=istage.at[slot], dst_ref=dst, send_sem=i_s.at[k],
                                                recv_sem=i_r.at[k], device_id=(px, py, c), device_id_type=MESH)

        @pl.when((s == 0) & (t == 0))
        def _():
            for e in ex:
                e.start()

        @pl.when(t == 0)
        def _():
            acc[...] = jnp.zeros_like(acc)

        acc[...] += _dot(a_ref[...], b_ref[...])

        @pl.when(t == nt - 1)
        def _():
            k = lax.div(s, 2)
            slot = lax.rem(k, 2)

            @pl.when(lax.rem(s, 2) == 0)
            def _():
                @pl.when(k >= 2)
                def _():
                    to_sibling(k - 2, slot).wait_send()

                dstage[slot] = acc[...].astype(BF)
                to_sibling(k, slot).start()

            @pl.when(lax.rem(s, 2) == 1)
            def _():
                to_sibling(k, slot).wait_recv()

                @pl.when(k >= 2)
                def _():
                    to_chip(k - 2, slot, True).wait_send()

                istage[slot] = (acc[...] + half[k].astype(F32)).astype(BF)

                @pl.when(k < n_chip - 1)
                def _():
                    to_chip(k, slot, True).start()

            @pl.when(s == last)
            def _():
                own = pltpu.make_async_copy(istage.at[slot], recv_ref.at[2 * x + y], lsem)
                own.start()
                to_chip(n_chip - 2, 0, True).wait_send()
                to_sibling(n_chip - 2, 0).wait_send()
                to_sibling(n_chip - 1, 1).wait_send()
                for kk in range(n_chip - 1):
                    to_chip(kk, 0, False).wait_recv()
                own.wait()
                for e in ex:
                    e.finish()

    hbm = pl.BlockSpec(memory_space=pl.ANY)
    dma = pltpu.SemaphoreType.DMA
    grid_spec = pltpu.PrefetchScalarGridSpec(
        num_scalar_prefetch=1, grid=(N_DEV, nt),
        in_specs=[pl.BlockSpec((K, BT), lambda s, t, me_ref: (0, t)),
                  pl.BlockSpec((BT, N), lambda s, t, me_ref: (t, me_ref[0] ^ (last - s)))] + [hbm] * ne,
        out_specs=[hbm] * (ne + 1),
        scratch_shapes=[pltpu.VMEM((K, N), F32), pltpu.VMEM((2, K, N), BF), pltpu.VMEM((2, K, N), BF),
                        pltpu.VMEM((n_chip, K, N), BF), dma((n_chip,)), dma((n_chip,)), dma((n_chip - 1,)),
                        dma((n_chip - 1,)), dma] + _direct_sems(ne))
    return pl.pallas_call(
        body, name=name, grid_spec=grid_spec,
        out_shape=[jax.ShapeDtypeStruct((n_chip, K, N), BF)] + [jax.ShapeDtypeStruct(e.shape, e.dtype) for e in extras],
        compiler_params=_params(("arbitrary", "arbitrary")),
    )(me, a, b, *extras)


def _my_index():
    return 4 * lax.axis_index("x") + 2 * lax.axis_index("y") + lax.axis_index("c")


def _all_gather(arrs, dtypes, name):
    n = len(arrs)

    def body(*refs):
        ins, outs = refs[:n], refs[n:2 * n]
        stages = refs[2 * n:3 * n]
        send_sems, recv_sems, local_sems = refs[3 * n:]
        gathers = [_TwoLevel(stages[a], outs[a], send_sems.at[a], recv_sems.at[a], local_sems.at[a]) for a in range(n)]
        for a in range(n):
            stages[a][...] = ins[a][...].astype(stages[a].dtype)
            gathers[a].start()
        for g in gathers:
            g.forward()
        for g in gathers:
            g.finish()

    vm = pl.BlockSpec(memory_space=pltpu.VMEM)
    hbm = pl.BlockSpec(memory_space=pl.ANY)
    return pl.pallas_call(
        body, name=name,
        in_specs=[vm] * n, out_specs=[hbm] * n,
        out_shape=[jax.ShapeDtypeStruct((N_DEV,) + a.shape, dt) for a, dt in zip(arrs, dtypes)],
        scratch_shapes=[pltpu.VMEM(a.shape, dt) for a, dt in zip(arrs, dtypes)]
        + [pltpu.SemaphoreType.DMA((n, 7)), pltpu.SemaphoreType.DMA((n, 7)), pltpu.SemaphoreType.DMA((n,))],
        compiler_params=pltpu.CompilerParams(vmem_limit_bytes=VMEM_LIMIT),
    )(*arrs)


def _peer(mask):
    x, y, c = (lax.axis_index(a) for a in AXES)
    return (x ^ ((mask >> 2) & 1), y ^ ((mask >> 1) & 1), c ^ (mask & 1))


def _dev_index(p):
    return 4 * p[0] + 2 * p[1] + p[2]


class _Direct:
    def __init__(self, src, dst, send_sems, recv_sems, local_sem, scatter):
        me = _my_index()
        self.own = pltpu.make_async_copy(src.at[me] if scatter else src, dst.at[me], local_sem)
        self.sends, self.recvs = [], []
        for k in range(1, N_DEV):
            p = _peer(k)
            pi = _dev_index(p)
            sems = dict(send_sem=send_sems.at[k - 1], recv_sem=recv_sems.at[k - 1], device_id=p, device_id_type=MESH)
            self.sends.append(pltpu.make_async_remote_copy(src_ref=src.at[pi] if scatter else src, dst_ref=dst.at[me],
                                                           **sems))
            self.recvs.append(pltpu.make_async_remote_copy(src_ref=src.at[me] if scatter else src, dst_ref=dst.at[pi],
                                                           **sems))

    def start(self):
        self.own.start()
        for cp in self.sends:
            cp.start()

    def finish(self):
        for cp in self.sends:
            cp.wait_send()
        for cp in self.recvs:
            cp.wait_recv()
        self.own.wait()


class _TwoLevel:
    def __init__(self, src, dst, send_sems, recv_sems, local_sem):
        x, y, c = (lax.axis_index(a) for a in AXES)
        self.me, self.sibling = (x, y, c), (x, y, 1 - c)
        self.chips = [(1 - x, y), (x, 1 - y), (1 - x, 1 - y)]
        self.src, self.dst, self.send_sems, self.recv_sems = src, dst, send_sems, recv_sems
        self.own = pltpu.make_async_copy(src, dst.at[_dev_index(self.me)], local_sem)

    def _copy(self, k, block, to, from_src=False):
        slot = self.dst.at[_dev_index(block)]
        return pltpu.make_async_remote_copy(src_ref=self.src if from_src else slot, dst_ref=slot,
                                            send_sem=self.send_sems.at[k], recv_sem=self.recv_sems.at[k],
                                            device_id=to, device_id_type=MESH)

    def _firsts(self):
        c = self.me[2]
        return [self._copy(0, self.me, self.sibling, True)] + [self._copy(1 + j, self.me, (*chip, c), True)
                                                               for j, chip in enumerate(self.chips)]

    def _passed(self):
        c = self.me[2]
        return [self._copy(4 + j, (*chip, c), self.sibling) for j, chip in enumerate(self.chips)]

    def start(self):
        self.own.start()
        for cp in self._firsts():
            cp.start()

    def forward(self):
        c = self.me[2]
        for j, (chip, fwd) in enumerate(zip(self.chips, self._passed())):
            self._copy(1 + j, (*chip, c), self.me).wait_recv()
            fwd.start()

    def finish(self):
        c = self.me[2]
        self._copy(0, self.sibling, self.me).wait_recv()
        for j, chip in enumerate(self.chips):
            self._copy(4 + j, (*chip, 1 - c), self.me).wait_recv()
        for cp in self._firsts() + self._passed():
            cp.wait_send()
        self.own.wait()


def _direct_sems(n):
    return [pltpu.SemaphoreType.DMA((n, 7)), pltpu.SemaphoreType.DMA((n, 7)), pltpu.SemaphoreType.DMA((n,))]


def _adam_math(w, g, m, v):
    m = ADAM_B1 * m + (1.0 - ADAM_B1) * g
    v = ADAM_B2 * v + (1.0 - ADAM_B2) * (g * g)
    m_hat = m / (1.0 - ADAM_B1 ** ADAM_STEP)
    v_hat = v / (1.0 - ADAM_B2 ** ADAM_STEP)
    delta = -ADAM_LR * (m_hat / (jnp.sqrt(v_hat) + ADAM_EPS) + ADAM_WD * w)
    return delta, m, v


def _sum_adam(parts, w, m, v, name):
    R, C = w.shape
    NP = parts.shape[0]
    BR = CHUNK if R % CHUNK == 0 else R

    def body(p_ref, w_ref, m_ref, v_ref, g_ref, d_ref, nm_ref, nv_ref):
        g = p_ref[0].astype(F32)
        for i in range(1, NP):
            g = g + p_ref[i].astype(F32)
        g_ref[...] = g
        d_ref[...], nm_ref[...], nv_ref[...] = _adam_math(w_ref[...], g, m_ref[...], v_ref[...])

    blk = pl.BlockSpec((BR, C), lambda i: (i, 0))
    S = jax.ShapeDtypeStruct((R, C), F32)
    return pl.pallas_call(
        body, name=name, grid=(R // BR,),
        in_specs=[pl.BlockSpec((NP, BR, C), lambda i: (0, i, 0)), blk, blk, blk],
        out_specs=[blk] * 4, out_shape=(S,) * 4,
        compiler_params=_params(("arbitrary",)),
    )(parts, w, m, v)


def _sum8(parts, name):
    _, R, C = parts.shape

    def body(p_ref, o_ref):
        g = p_ref[0]
        for i in range(1, N_DEV):
            g = g + p_ref[i]
        o_ref[...] = g

    return pl.pallas_call(body, name=name, out_shape=jax.ShapeDtypeStruct((R, C), F32))(parts)


def _adam_only(g, w, m, v, name):
    def body(g_ref, w_ref, m_ref, v_ref, d_ref, nm_ref, nv_ref):
        d_ref[...], nm_ref[...], nv_ref[...] = _adam_math(w_ref[...], g_ref[...], m_ref[...], v_ref[...])

    S = jax.ShapeDtypeStruct(w.shape, F32)
    return pl.pallas_call(body, name=name, out_shape=(S,) * 3)(g, w, m, v)


def _rope_tables(T):
    pos = np.arange(T, dtype=np.float32)
    inv_freq = (np.float64(ROPE_THETA) ** (-np.arange(0, HEAD_DIM, 2, dtype=np.float64) / HEAD_DIM)).astype(np.float32)
    ang = (pos[:, None] * inv_freq[None, :]).astype(np.float64)
    cos, sin, zero = np.cos(ang).astype(np.float32), np.sin(ang).astype(np.float32), np.zeros(ang.shape, np.float32)
    c = np.concatenate([cos, cos, cos, cos], axis=1)
    s1 = np.concatenate([-sin, zero, -sin, zero], axis=1)
    s2 = np.concatenate([zero, sin, zero, sin], axis=1)
    return jnp.asarray(c), jnp.asarray(s1), jnp.asarray(s2)


SUBLANES = 8


def _nrows(size):
    return -(-size // (SUBLANES * LANES)) * SUBLANES


def _rows(a):
    flat = a.reshape(-1)
    pad = _nrows(flat.shape[0]) * LANES - flat.shape[0]
    if pad:
        flat = jnp.concatenate([flat, jnp.zeros((pad,), flat.dtype)])
    return flat.reshape(-1, LANES)


def _pack(arrs, total_rows):
    rows = [_rows(a) for a in arrs]
    used = sum(r.shape[0] for r in rows)
    if total_rows > used:
        rows.append(jnp.zeros((total_rows - used, LANES), F32))
    return jnp.concatenate(rows, axis=0)


def _unpack(packed, shapes):
    out, at = [], 0
    for shp in shapes:
        size = math.prod(shp)
        nrow = _nrows(size)
        out.append(packed[at:at + nrow].reshape(-1)[:size].reshape(shp))
        at += nrow
    return out


def kernel(x, a_norm_g, a_w_in, a_ln_g, a_ln_b, a_ws, a_bs, a_w_out, kv_norm_g, w_kv, b_kv, b_norm_g, b_w_in, b_bq, b_sinks, b_w_out, final_norm_g, loss_target, m_a_norm_g, m_a_w_in, m_a_ln_g, m_a_ln_b, m_a_ws, m_a_bs, m_a_w_out, m_kv_norm_g, m_w_kv, m_b_kv, m_b_norm_g, m_b_w_in, m_b_bq, m_b_sinks, m_b_w_out, m_final_norm_g, v_a_norm_g, v_a_w_in, v_a_ln_g, v_a_ln_b, v_a_ws, v_a_bs, v_a_w_out, v_kv_norm_g, v_w_kv, v_b_kv, v_b_norm_g, v_b_w_in, v_b_bq, v_b_sinks, v_b_w_out, v_final_norm_g):
    T, D = x.shape[1], x.shape[2]
    AW = a_ln_g.shape[1] * N_DEV
    G = a_ws.shape[1]
    assert w_kv.shape[1] == 2 * LANES and a_ws.shape[2] == CHUNK and T % CHUNK == 0
    me = _my_index()

    vec = jnp.concatenate([a_norm_g, a_ln_g, a_ln_b], axis=1)
    vec = jnp.broadcast_to(vec, (8, vec.shape[1]))
    wa_in, vecs = _all_gather([a_w_in[0], vec], [BF, F32], "gather_weights")
    vecs = vecs[:, 0, :]
    ds = D // N_DEV
    g_a = vecs[:, :ds].reshape(1, D)
    ln_g = vecs[:, ds:ds + AW // N_DEV].reshape(1, AW)
    ln_b = vecs[:, ds + AW // N_DEV:].reshape(1, AW)

    rc, rs1, rs2 = _rope_tables(T)
    ws = a_ws[0]
    bs_t = a_bs[0].T
    g_kv = kv_norm_g.reshape(1, D)
    bkv = b_kv.reshape(1, -1)
    g_f = final_norm_g.reshape(1, D)
    sinks = jnp.repeat(b_sinks.reshape(2, 4, 2).transpose(0, 2, 1).reshape(4, 4), CHUNK, axis=1)
    xs, tgt = x[0], loss_target[0]

    z, wa_out, wkv = _in_proj(xs, g_a, wa_in, [a_w_out[0], w_kv])
    wa_out = wa_out.reshape(AW, D)
    wkv = wkv.reshape(D, 2 * LANES)
    h1, sv, vhat, rstd, k4, v4, kt, vt, wb_in, wb_out = _a_fwd(
        xs, z, ln_g, ln_b, ws, bs_t, wa_out, g_kv, wkv, bkv, rc, rs1, rs2, [b_w_in[0], b_w_out[0]])
    wb_out = wb_out.reshape(-1, D)
    q, g2, o, dh2, dh2_b, loss, d_gf = _b_fwd(h1, b_norm_g, wb_in, b_bq, rc, rs1, rs2, k4, vt, sinks, wb_out, g_f, tgt)
    dh1p, dz2, n2, y2, dk, dv, d_bq, d_gb, d_sink = _b_bwd(dh2, h1, q, g2, o, k4, v4, kt, sinks, wb_out, wb_in,
                                                           b_norm_g, rc, rs1, rs2)
    d_sink = d_sink[:, :4].reshape(2, 2, 4).transpose(0, 2, 1).reshape(1, 16)
    gw_b_in = _wgrad(n2, dz2, N_DEV, "wgrad_b_in", bt=2048)
    gw_b_out = _wgrad(y2, dh2_b, 1, "wgrad_b_out").reshape(N_DEV, -1, D)
    (dz, y, nkv, dkv, dh1, dh1_f, d_gkv, d_bkv, d_lng, d_lnb, d_ws, d_bst, r_b_in, r_b_out) = _a_bwd(
        dh1p, dk, dv, h1, g_kv, wkv, wa_out, ws, ln_g, ln_b, z, sv, vhat, rstd, rc, rs1, rs2, [gw_b_in, gw_b_out])
    gw_a_out = _wgrad(y, dh1, 1, "wgrad_a_out").reshape(N_DEV, AW // N_DEV, D)
    gw_kv = _wgrad(nkv, dkv, 1, "wgrad_kv").reshape(N_DEV, D // N_DEV, 2 * LANES)
    dx, n1, d_ga, r_a_out, r_kv = _a_in_bwd(dz, wa_in, xs, dh1_f, g_a, [gw_a_out, gw_kv])
    small = [d_ws, d_bst[:, :G].T, d_gkv, d_bkv, d_gb, d_bq, d_sink, d_gf, d_ga, d_lng, d_lnb, loss]
    used = sum(_nrows(a.size) for a in small)
    per = -(-used // (SUBLANES * N_DEV)) * SUBLANES
    small_pack = _pack(small, per * N_DEV).reshape(N_DEV, per, LANES)
    r_a_in, r_small = _wgrad_exchange(n1, dz, me.reshape(1), [small_pack], "wgrad_a_in")

    g_a_in, d_a_in, nm_a_in, nv_a_in = _sum_adam(r_a_in, a_w_in[0], m_a_w_in[0], v_a_w_in[0], "adam_a_in")
    g_a_out, d_a_out, nm_a_out, nv_a_out = _sum_adam(r_a_out, a_w_out[0], m_a_w_out[0], v_a_w_out[0], "adam_a_out")
    g_kvw, d_kvw, nm_kvw, nv_kvw = _sum_adam(r_kv, w_kv, m_w_kv, v_w_kv, "adam_kv")
    g_b_in, d_b_in, nm_b_in, nv_b_in = _sum_adam(r_b_in, b_w_in[0], m_b_w_in[0], v_b_w_in[0], "adam_b_in")
    g_b_out, d_b_out, nm_b_out, nv_b_out = _sum_adam(r_b_out, b_w_out[0], m_b_w_out[0], v_b_w_out[0], "adam_b_out")

    red = _sum8(r_small, "sum_small")
    (full_small,) = _all_gather([red], [F32], "gather_small")
    full_small = full_small.reshape(N_DEV * per, LANES)
    rep_shapes = [a_ws.shape, a_bs.shape, kv_norm_g.shape, b_kv.shape, b_norm_g.shape, b_bq.shape, b_sinks.shape,
                  final_norm_g.shape]
    gs = _unpack(full_small, rep_shapes + [(N_DEV, a_norm_g.shape[1]), (N_DEV, a_ln_g.shape[1]),
                                           (N_DEV, a_ln_b.shape[1]), (1, 1)])
    loss = gs.pop()[0, 0]
    g_ang = lax.dynamic_slice_in_dim(gs[8], me, 1, axis=0)
    g_alng = lax.dynamic_slice_in_dim(gs[9], me, 1, axis=0)
    g_alnb = lax.dynamic_slice_in_dim(gs[10], me, 1, axis=0)
    sm_g = gs[:8] + [g_ang, g_alng, g_alnb]
    sm_shapes = [a.shape for a in sm_g]
    tot = sum(_nrows(a.size) for a in sm_g)
    pw = _pack([a_ws, a_bs, kv_norm_g, b_kv, b_norm_g, b_bq, b_sinks, final_norm_g, a_norm_g, a_ln_g, a_ln_b], tot)
    pm = _pack([m_a_ws, m_a_bs, m_kv_norm_g, m_b_kv, m_b_norm_g, m_b_bq, m_b_sinks, m_final_norm_g, m_a_norm_g,
                m_a_ln_g, m_a_ln_b], tot)
    pv = _pack([v_a_ws, v_a_bs, v_kv_norm_g, v_b_kv, v_b_norm_g, v_b_bq, v_b_sinks, v_final_norm_g, v_a_norm_g,
                v_a_ln_g, v_a_ln_b], tot)
    pg = _pack(sm_g, tot)
    pd, pnm, pnv = _adam_only(pg, pw, pm, pv, "adam_small")
    sd, snm, snv = _unpack(pd, sm_shapes), _unpack(pnm, sm_shapes), _unpack(pnv, sm_shapes)

    def order(big, sm):
        a_in, a_out, kvw, b_in, b_out = big
        ws_, bs_, kvg, bkv_, bng, bq_, snk, fng, ang, alng, alnb = sm
        return (ang, a_in[None], alng, alnb, ws_, bs_, a_out[None], kvg, kvw, bkv_, bng, b_in[None], bq_, snk,
                b_out[None], fng)

    grads = order((g_a_in, g_a_out, g_kvw, g_b_in, g_b_out), sm_g)
    deltas = order((d_a_in, d_a_out, d_kvw, d_b_in, d_b_out), sd)
    new_m = order((nm_a_in, nm_a_out, nm_kvw, nm_b_in, nm_b_out), snm)
    new_v = order((nv_a_in, nv_a_out, nv_kvw, nv_b_in, nv_b_out), snv)
    return (loss, dx[None], *grads, *deltas, *new_m, *new_v)
```

```python
import functools
import math

import jax
import jax.numpy as jnp
import numpy as np
from jax import lax
from jax.experimental import pallas as pl
from jax.experimental.pallas import tpu as pltpu

CHUNK = 128
HEAD_DIM = 64
ROPE_THETA = 10000.0
EPS = 1e-5
ADAM_LR = 0.001
ADAM_B1 = 0.9
ADAM_B2 = 0.999
ADAM_EPS = 1e-08
ADAM_WD = 0.01
ADAM_STEP = 10
N_DEV = 8
LANES = 128
NEG = -1e30

BF = jnp.bfloat16
F32 = jnp.float32
MESH = pl.DeviceIdType.MESH
AXES = ("x", "y", "c")
VMEM_LIMIT = 56 * 1024 * 1024


def _dot(a, b):
    return jnp.dot(a, b, preferred_element_type=F32)


def _dot_nt(a, b):
    return lax.dot_general(a, b, (((1,), (1,)), ((), ())), preferred_element_type=F32)


def _dot_tn(a, b):
    return lax.dot_general(a, b, (((0,), (0,)), ((), ())), preferred_element_type=F32)


def _const_spec(shape):
    nd = len(shape)
    return pl.BlockSpec(shape, lambda *_: (0,) * nd, pipeline_mode=pl.Buffered(1))


def _acc_spec(shape):
    nd = len(shape)
    return pl.BlockSpec(shape, lambda *_: (0,) * nd)


def _row_spec(tm, width):
    return pl.BlockSpec((tm, width), lambda i: (i, 0))


def _col_spec(tm, height):
    return pl.BlockSpec((height, tm), lambda i: (0, i))


def _params(sem):
    return pltpu.CompilerParams(dimension_semantics=sem, vmem_limit_bytes=VMEM_LIMIT)


def _rot(x, c, s1, s2):
    return x * c + pltpu.roll(x, 96, 1) * s1 + pltpu.roll(x, 32, 1) * s2


def _rot_bwd(d, c, s1, s2):
    return d * c + pltpu.roll(d * s1, 32, 1) + pltpu.roll(d * s2, 96, 1)


def _silu_parts(g):
    sg = jax.nn.sigmoid(g)
    return g * sg, sg * (1.0 + g * (1.0 - sg))


def _rms_bwd(dn, xh, r, g):
    a = dn * g
    return r * (a - xh * jnp.mean(a * xh, axis=-1, keepdims=True))


def _lane_lo(shape):
    return lax.broadcasted_iota(jnp.int32, shape, 1) < HEAD_DIM


def _split4(t):
    lo = _lane_lo(t.shape)
    tr = pltpu.roll(t, HEAD_DIM, 1)
    z = jnp.zeros_like(t)
    return jnp.concatenate([jnp.where(lo, t, z), jnp.where(lo, z, tr), jnp.where(lo, tr, z), jnp.where(lo, z, t)], axis=1)


def _stack_pairs(t, h):
    return jnp.concatenate([t[:, (h * 4 + j) * LANES:(h * 4 + j + 1) * LANES] for j in range(4)], axis=0)


def _upper():
    shape = (CHUNK, 4 * CHUNK)
    return lax.broadcasted_iota(jnp.int32, shape, 0) > (lax.broadcasted_iota(jnp.int32, shape, 1) & (CHUNK - 1))


def _band_rows(ref, prev, cur, h):
    a = slice(2 * h * LANES, (2 * h + 1) * LANES)
    b = slice((2 * h + 1) * LANES, (2 * h + 2) * LANES)
    return jnp.concatenate([ref[pl.ds(prev, CHUNK), a], ref[pl.ds(cur, CHUNK), a],
                            ref[pl.ds(prev, CHUNK), b], ref[pl.ds(cur, CHUNK), b]], axis=0)


def _band_cols(ref, pci, ci, h):
    a = slice(2 * h * LANES, (2 * h + 1) * LANES)
    b = slice((2 * h + 1) * LANES, (2 * h + 2) * LANES)
    return jnp.concatenate([ref[pci, a, :], ref[ci, a, :], ref[pci, b, :], ref[ci, b, :]], axis=1)


def _fold(t, upper, has_prev=None):
    out = []
    for k in range(2):
        prev = t[2 * k * CHUNK:(2 * k + 1) * CHUNK]
        if has_prev is not None:
            prev = jnp.where(has_prev, prev, NEG)
        out.append(jnp.where(upper, prev, t[(2 * k + 1) * CHUNK:(2 * k + 2) * CHUNK]))
    return out


def _unfold(fa, fb, upper):
    z = jnp.zeros_like(fa)
    return jnp.concatenate([jnp.where(upper, fa, z), jnp.where(upper, z, fa),
                            jnp.where(upper, fb, z), jnp.where(upper, z, fb)], axis=0)


def _softmax_sink(f, sink):
    m = jnp.maximum(jnp.max(f, axis=0, keepdims=True), sink)
    p = jnp.exp(f - m)
    es = jnp.exp(sink - m)
    inv = 1.0 / (jnp.sum(p, axis=0, keepdims=True) + es)
    return p * inv, es * inv


class _Riding:
    def __init__(self, shards, gathered, stages, sems, n_steps):
        self.shards, self.stages, self.n_steps = shards, stages, n_steps
        ssem, rsem, lsem = sems
        self.gathers = [_TwoLevel(stages[k], gathered[k], ssem.at[k], rsem.at[k], lsem.at[k])
                        for k in range(len(shards))]

    def begin(self, i):
        @pl.when(i == 0)
        def _():
            for shard, stage, g in zip(self.shards, self.stages, self.gathers):
                stage[...] = shard[...].astype(stage.dtype)
                g.start()

    def end(self, i):
        @pl.when(i == self.n_steps // 2)
        def _():
            for g in self.gathers:
                g.forward()

        @pl.when(i == self.n_steps - 1)
        def _():
            for g in self.gathers:
                g.finish()

    @staticmethod
    def specs(later):
        nl = len(later)
        hbm = pl.BlockSpec(memory_space=pl.ANY)
        return ([_const_spec(w.shape) for w in later], [hbm] * nl,
                tuple(jax.ShapeDtypeStruct((N_DEV,) + w.shape, BF) for w in later),
                [pltpu.VMEM(w.shape, BF) for w in later] + _direct_sems(nl))


def _in_proj(x, g_a, wa_in, ln_g, ln_b, later):
    T, D = x.shape
    SH = wa_in.shape[2]
    AW = N_DEV * SH // 3
    TM = min(256, T)
    nT = T // TM
    nl = len(later)
    order = sorted(range(N_DEV), key=lambda j: -(((j + 1) * SH - 1) // AW))

    def body(x_ref, ga_ref, wain_ref, lng_ref, lnb_ref, *rest):
        shards, rest = rest[:nl], rest[nl:]
        (ug_ref, us_ref, vln_ref, vhat_ref, rstd_ref), rest = rest[:5], rest[5:]
        gathered, z_scr, stages, sems = rest[:nl], rest[nl], rest[nl + 1:2 * nl + 1], rest[2 * nl + 1:]
        i = pl.program_id(0)
        riding = _Riding(shards, gathered, stages, sems, nT)
        riding.begin(i)
        xv = x_ref[...]
        r1 = lax.rsqrt(jnp.mean(xv * xv, axis=-1, keepdims=True) + EPS)
        n1 = (xv * r1 * ga_ref[...]).astype(BF)
        for j in order:
            z_scr[:, j * SH:(j + 1) * SH] = _dot(n1, wain_ref[j])
        gt = z_scr[:, 2 * AW:]
        ug_ref[:, AW:] = gt.astype(BF)
        silu, _ = _silu_parts(gt)
        v = z_scr[:, AW:2 * AW]
        mu = jnp.mean(v, axis=-1, keepdims=True)
        xc = v - mu
        rstd = lax.rsqrt(jnp.mean(xc * xc, axis=-1, keepdims=True) + EPS)
        vhat = xc * rstd
        vhat_ref[...] = vhat.astype(BF)
        vln_ref[...] = (vhat * lng_ref[...] + lnb_ref[...]).astype(BF)
        rstd_ref[...] = jnp.broadcast_to(rstd, rstd_ref.shape)
        u = z_scr[:, :AW]
        ug_ref[:, :AW] = u.astype(BF)
        us_ref[...] = (u * silu).astype(BF)
        riding.end(i)

    row = functools.partial(_row_spec, TM)
    r_in, r_out, r_shape, r_scratch = _Riding.specs(later)
    S = jax.ShapeDtypeStruct
    return pl.pallas_call(
        body, name="a_in_proj", grid=(nT,),
        in_specs=[row(D), _const_spec((1, D)), _const_spec(wa_in.shape), _const_spec((1, AW)), _const_spec((1, AW))]
        + r_in,
        out_specs=[row(2 * AW), row(AW), row(AW), row(AW), row(LANES)] + r_out,
        out_shape=(S((T, 2 * AW), BF), S((T, AW), BF), S((T, AW), BF), S((T, AW), BF), S((T, LANES), F32)) + r_shape,
        scratch_shapes=[pltpu.VMEM((TM, 3 * AW), F32)] + r_scratch,
        compiler_params=_params(("arbitrary",)),
    )(x, g_a, wa_in, ln_g, ln_b, *later)


def _a_fwd(x, us, vln, ws, bs_t, wa_out, g_kv, w_kv, b_kv, rc, rs1, rs2, later):
    T, D = x.shape
    AW = wa_out.shape[0]
    G = ws.shape[0]
    TM = min(256, T)
    nT = T // TM
    nC = TM // CHUNK
    nl = len(later)

    def body(x_ref, us_ref, vln_ref, ws_ref, bst_ref, waout_ref, gkv_ref, wkv_ref, bkv_ref,
             rc_ref, rs1_ref, rs2_ref, *rest):
        shards, rest = rest[:nl], rest[nl:]
        (h1_ref, sv_ref, k4_ref, v4_ref, kt_ref, vt_ref), rest = rest[:6], rest[6:]
        gathered, sv_scr, stages, sems = rest[:nl], rest[nl], rest[nl + 1:2 * nl + 1], rest[2 * nl + 1:]
        i = pl.program_id(0)
        riding = _Riding(shards, gathered, stages, sems, nT)
        riding.begin(i)
        xv = x_ref[...]
        tri = lax.broadcasted_iota(jnp.int32, (CHUNK, CHUNK), 0) >= lax.broadcasted_iota(jnp.int32, (CHUNK, CHUNK), 1)
        for g in range(G):
            wsm = jnp.where(tri, ws_ref[g], 0.0).astype(BF)
            bias = bst_ref[:, g:g + 1]
            for c in range(nC):
                blk = vln_ref[c * CHUNK:(c + 1) * CHUNK, g * CHUNK:(g + 1) * CHUNK]
                sv_scr[c * CHUNK:(c + 1) * CHUNK, g * CHUNK:(g + 1) * CHUNK] = _dot(wsm, blk) + bias
        sv = sv_scr[...]
        y = (us_ref[...].astype(F32) * sv).astype(BF)
        h1 = xv + _dot(y, waout_ref[...])
        h1_ref[...] = h1
        sv_ref[...] = sv.astype(BF)
        rkv = lax.rsqrt(jnp.mean(h1 * h1, axis=-1, keepdims=True) + EPS)
        nkv = (h1 * rkv * gkv_ref[...]).astype(BF)
        kv = _dot(nkv, wkv_ref[...]) + bkv_ref[...]
        k_rot = _rot(kv[:, :LANES], rc_ref[...], rs1_ref[...], rs2_ref[...])
        for src, ref, tref in ((k_rot, k4_ref, kt_ref), (kv[:, LANES:], v4_ref, vt_ref)):
            t4 = _split4(src)
            ref[...] = t4.astype(BF)
            for c in range(nC):
                for b in range(4):
                    blk = t4[c * CHUNK:(c + 1) * CHUNK, b * LANES:(b + 1) * LANES]
                    tref[c, b * LANES:(b + 1) * LANES, :] = blk.T.astype(BF)
        riding.end(i)

    row = functools.partial(_row_spec, TM)
    tr = pl.BlockSpec((nC, 4 * LANES, CHUNK), lambda i: (i, 0, 0))
    r_in, r_out, r_shape, r_scratch = _Riding.specs(later)
    S = jax.ShapeDtypeStruct
    return pl.pallas_call(
        body, name="a_fwd", grid=(nT,),
        in_specs=[row(D), row(AW), row(AW), _const_spec(ws.shape), _const_spec(bs_t.shape), _const_spec(wa_out.shape),
                  _const_spec((1, D)), _const_spec(w_kv.shape), _const_spec((1, 2 * LANES)), row(LANES), row(LANES),
                  row(LANES)] + r_in,
        out_specs=[row(D), row(AW), row(4 * LANES), row(4 * LANES), tr, tr] + r_out,
        out_shape=(S((T, D), F32), S((T, AW), BF), S((T, 4 * LANES), BF), S((T, 4 * LANES), BF),
                   S((T // CHUNK, 4 * LANES, CHUNK), BF), S((T // CHUNK, 4 * LANES, CHUNK), BF)) + r_shape,
        scratch_shapes=[pltpu.VMEM((TM, AW), F32)] + r_scratch,
        compiler_params=_params(("arbitrary",)),
    )(x, us, vln, ws, bs_t, wa_out, g_kv, w_kv, b_kv, rc, rs1, rs2, *later)


def _b_fwd(h1, g_b, wb_in, bq, rc, rs1, rs2, k4, vt, sinks, wb_out, g_f, target):
    T, D = h1.shape
    BW = wb_out.shape[0]
    SH = wb_in.shape[2]
    TM = min(256, T)
    nC = TM // CHUNK
    nP = BW // LANES

    def body(h1_ref, gb_ref, wbin_ref, bq_ref, rc_ref, rs1_ref, rs2_ref, k4_ref, vt_ref, sink_ref, wbout_ref, gf_ref,
             tgt_ref, q_ref, g2_ref, o_ref, dh2_ref, dh2b_ref, loss_ref, dgf_ref, z_scr, o_scr):
        i = pl.program_id(0)
        h1v = h1_ref[...]
        r2 = lax.rsqrt(jnp.mean(h1v * h1v, axis=-1, keepdims=True) + EPS)
        n2 = (h1v * r2 * gb_ref[...]).astype(BF)
        for j in range(N_DEV):
            z_scr[:, j * SH:(j + 1) * SH] = _dot(n2, wbin_ref[j])
        c_t, s1_t, s2_t = rc_ref[...], rs1_ref[...], rs2_ref[...]
        for p in range(nP):
            cols = slice(p * LANES, (p + 1) * LANES)
            qp = _rot(z_scr[:, cols] + bq_ref[:, cols], c_t, s1_t, s2_t) * (HEAD_DIM ** -0.5)
            q_ref[:, cols] = qp.astype(BF)
        g2 = z_scr[:, BW:]
        g2_ref[...] = g2.astype(BF)
        upper = _upper()
        for c in range(nC):
            ci = i * nC + c
            rows = slice(c * CHUNK, (c + 1) * CHUNK)
            pci = jnp.maximum(ci - 1, 0)
            prev = pl.multiple_of(pci * CHUNK, CHUNK)
            cur = pl.multiple_of(ci * CHUNK, CHUNK)
            qc = q_ref[rows, :]
            for h in range(2):
                st = _dot_nt(_band_rows(k4_ref, prev, cur, h), _stack_pairs(qc, h))
                fa, fb = _fold(st, upper, ci > 0)
                pa, _ = _softmax_sink(fa, sink_ref[2 * h:2 * h + 1, :])
                pb, _ = _softmax_sink(fb, sink_ref[2 * h + 1:2 * h + 2, :])
                ot = _dot(_band_cols(vt_ref, pci, ci, h), _unfold(pa, pb, upper).astype(BF))
                for j in range(4):
                    o_scr[rows, (h * 4 + j) * LANES:(h * 4 + j + 1) * LANES] = ot[:, j * CHUNK:(j + 1) * CHUNK].T
        o = o_scr[...]
        o_ref[...] = o.astype(BF)
        silu, _ = _silu_parts(g2)
        h2 = h1v + _dot((o * silu).astype(BF), wbout_ref[...])
        rf = lax.rsqrt(jnp.mean(h2 * h2, axis=-1, keepdims=True) + EPS)
        xh = h2 * rf
        gf = gf_ref[...]
        err = xh * gf - tgt_ref[...]
        dyf = err * (1.0 / D)
        dh2 = _rms_bwd(dyf, xh, rf, gf)
        dh2_ref[...] = dh2
        dh2b_ref[...] = dh2.astype(BF)

        @pl.when(i == 0)
        def _():
            loss_ref[...] = jnp.zeros_like(loss_ref)
            dgf_ref[...] = jnp.zeros_like(dgf_ref)

        loss_ref[...] += 0.5 * jnp.sum(jnp.mean(err * err, axis=-1, keepdims=True), axis=0, keepdims=True)
        dgf_ref[...] += jnp.sum(dyf * xh, axis=0, keepdims=True)

    row = functools.partial(_row_spec, TM)
    S = jax.ShapeDtypeStruct
    return pl.pallas_call(
        body, name="b_fwd", grid=(T // TM,),
        in_specs=[row(D), _const_spec((1, D)), _const_spec(wb_in.shape), _const_spec((1, BW)), row(LANES), row(LANES),
                  row(LANES), _const_spec(k4.shape), _const_spec(vt.shape), _const_spec(sinks.shape),
                  _const_spec(wb_out.shape), _const_spec((1, D)), row(D)],
        out_specs=[row(BW), row(BW), row(BW), row(D), row(D), _acc_spec((1, 1)), _acc_spec((1, D))],
        out_shape=(S((T, BW), BF), S((T, BW), BF), S((T, BW), BF), S((T, D), F32), S((T, D), BF), S((1, 1), F32),
                   S((1, D), F32)),
        scratch_shapes=[pltpu.VMEM((TM, 2 * BW), F32), pltpu.VMEM((TM, BW), F32)],
        compiler_params=_params(("arbitrary",)),
    )(h1, g_b, wb_in, bq, rc, rs1, rs2, k4, vt, sinks, wb_out, g_f, target)


def _b_bwd(dh2, h1, q, g2, o, k4, v4, kt, sinks, wb_out, wb_in, g_b, rc, rs1, rs2):
    T, D = h1.shape
    BW = wb_out.shape[0]
    SH = wb_in.shape[2]
    TM = min(256, T)
    nT = T // TM
    nC = TM // CHUNK
    nP = BW // LANES

    def body(dh2_ref, h1_ref, q_ref, g2_ref, o_ref, k4_ref, v4_ref, kt_ref, sink_ref, wbout_ref, wbin_ref, gb_ref,
             rc_ref, rs1_ref, rs2_ref,
             dh1_ref, dz2_ref, n2_ref, y2_ref, dk_ref, dv_ref, dbq_ref, dgb_ref, dsink_ref, do_scr, dq_scr, dsacc_scr):
        i = pl.program_id(0)

        @pl.when(i == 0)
        def _():
            dk_ref[...] = jnp.zeros_like(dk_ref)
            dv_ref[...] = jnp.zeros_like(dv_ref)
            dbq_ref[...] = jnp.zeros_like(dbq_ref)
            dgb_ref[...] = jnp.zeros_like(dgb_ref)
            dsacc_scr[...] = jnp.zeros_like(dsacc_scr)

        dh2 = dh2_ref[...]
        dy2 = _dot_nt(dh2.astype(BF), wbout_ref[...])
        g2v = g2_ref[...].astype(F32)
        ov = o_ref[...].astype(F32)
        silu, dsilu = _silu_parts(g2v)
        y2_ref[...] = (ov * silu).astype(BF).T
        do_scr[...] = (dy2 * silu).astype(BF)
        dz2_ref[:, BW:] = (dy2 * ov * dsilu).astype(BF)
        upper = _upper()
        lo = _lane_lo((2 * CHUNK, LANES))
        for c in range(nC):
            ci = i * nC + c
            rows = slice(c * CHUNK, (c + 1) * CHUNK)
            pci = jnp.maximum(ci - 1, 0)
            prev = pl.multiple_of(pci * CHUNK, CHUNK)
            cur = pl.multiple_of(ci * CHUNK, CHUNK)
            qc = q_ref[rows, :]
            doc = do_scr[rows, :]
            dkb = jnp.zeros((2 * CHUNK, LANES), F32)
            dvb = jnp.zeros((2 * CHUNK, LANES), F32)
            for h in range(2):
                qs = _stack_pairs(qc, h)
                dos = _stack_pairs(doc, h)
                fa, fb = _fold(_dot_nt(_band_rows(k4_ref, prev, cur, h), qs), upper, ci > 0)
                dfa, dfb = _fold(_dot_nt(_band_rows(v4_ref, prev, cur, h), dos), upper)
                folded = []
                for k, (f, df) in enumerate(((fa, dfa), (fb, dfb))):
                    p, ps = _softmax_sink(f, sink_ref[2 * h + k:2 * h + k + 1, :])
                    delta = jnp.sum(p * df, axis=0, keepdims=True)
                    dsacc_scr[2 * h + k:2 * h + k + 1, :] -= ps * delta
                    folded.append((p, p * (df - delta)))
                pt = _unfold(folded[0][0], folded[1][0], upper).astype(BF)
                dst = _unfold(folded[0][1], folded[1][1], upper).astype(BF)
                dqt = _dot(_band_cols(kt_ref, pci, ci, h), dst)
                for j in range(4):
                    dq_scr[rows, (h * 4 + j) * LANES:(h * 4 + j + 1) * LANES] = dqt[:, j * CHUNK:(j + 1) * CHUNK].T
                for acc_name, g in (("k", _dot(dst, qs)), ("v", _dot(pt, dos))):
                    a, b = g[:2 * CHUNK], g[2 * CHUNK:]
                    if h == 0:
                        part = jnp.where(lo, a + pltpu.roll(b, HEAD_DIM, 1), 0.0)
                    else:
                        part = jnp.where(lo, 0.0, pltpu.roll(a, HEAD_DIM, 1) + b)
                    if acc_name == "k":
                        dkb += part
                    else:
                        dvb += part
            dk_ref[pl.ds(prev, CHUNK), :] += dkb[:CHUNK]
            dk_ref[pl.ds(cur, CHUNK), :] += dkb[CHUNK:]
            dv_ref[pl.ds(prev, CHUNK), :] += dvb[:CHUNK]
            dv_ref[pl.ds(cur, CHUNK), :] += dvb[CHUNK:]

        @pl.when(i == nT - 1)
        def _():
            lane = lax.broadcasted_iota(jnp.int32, dsink_ref.shape, 1)
            tot = jnp.zeros(dsink_ref.shape, F32)
            for j in range(4):
                tot += jnp.where(lane == j, jnp.sum(dsacc_scr[:, j * CHUNK:(j + 1) * CHUNK], axis=1, keepdims=True), 0.0)
            dsink_ref[...] = tot
        c_t, s1_t, s2_t = rc_ref[...], rs1_ref[...], rs2_ref[...]
        for p in range(nP):
            cols = slice(p * LANES, (p + 1) * LANES)
            dqp = _rot_bwd(dq_scr[:, cols] * (HEAD_DIM ** -0.5), c_t, s1_t, s2_t)
            dbq_ref[:, cols] += jnp.sum(dqp, axis=0, keepdims=True)
            dz2_ref[:, cols] = dqp.astype(BF)
        h1v = h1_ref[...]
        r2 = lax.rsqrt(jnp.mean(h1v * h1v, axis=-1, keepdims=True) + EPS)
        xh = h1v * r2
        gb = gb_ref[...]
        n2_ref[...] = (xh * gb).astype(BF).T
        dn2 = None
        for j in range(N_DEV):
            part = _dot_nt(dz2_ref[:, j * SH:(j + 1) * SH], wbin_ref[j])
            dn2 = part if dn2 is None else dn2 + part
        dgb_ref[...] += jnp.sum(dn2 * xh, axis=0, keepdims=True)
        dh1_ref[...] = dh2 + _rms_bwd(dn2, xh, r2, gb)

    row = functools.partial(_row_spec, TM)
    S = jax.ShapeDtypeStruct
    return pl.pallas_call(
        body, name="b_bwd", grid=(T // TM,),
        in_specs=[row(D), row(D), row(BW), row(BW), row(BW), _const_spec(k4.shape), _const_spec(v4.shape),
                  _const_spec(kt.shape), _const_spec(sinks.shape), _const_spec(wb_out.shape), _const_spec(wb_in.shape),
                  _const_spec((1, D)), row(LANES), row(LANES), row(LANES)],
        out_specs=[row(D), row(2 * BW), _col_spec(TM, D), _col_spec(TM, BW), _acc_spec((T, LANES)),
                   _acc_spec((T, LANES)), _acc_spec((1, BW)), _acc_spec((1, D)), _acc_spec((4, LANES))],
        out_shape=(S((T, D), F32), S((T, 2 * BW), BF), S((D, T), BF), S((BW, T), BF), S((T, LANES), F32),
                   S((T, LANES), F32), S((1, BW), F32), S((1, D), F32), S((4, LANES), F32)),
        scratch_shapes=[pltpu.VMEM((TM, BW), BF), pltpu.VMEM((TM, BW), F32), pltpu.VMEM((4, 4 * CHUNK), F32)],
        compiler_params=_params(("arbitrary",)),
    )(dh2, h1, q, g2, o, k4, v4, kt, sinks, wb_out, wb_in, g_b, rc, rs1, rs2)


def _a_bwd(dh1p, dk, dv, h1, g_kv, w_kv, wa_out, ws, ln_g, ln_b, z, sv, vhat, rstd, rc, rs1, rs2, ready):
    T, D = h1.shape
    AW = wa_out.shape[0]
    G = ws.shape[0]
    TM = min(256, T)
    nT = T // TM
    nC = TM // CHUNK
    nr = len(ready)

    def body(dh1p_ref, dk_ref, dv_ref, h1_ref, gkv_ref, wkv_ref, waout_ref, ws_ref, lng_ref,
             lnb_ref, u_ref, gt_ref, sv_ref, vhat_ref, rstd_ref, rc_ref, rs1_ref, rs2_ref, *rest):
        ready_refs, rest = rest[:nr], rest[nr:]
        (dz_ref, y_ref, nkv_ref, dkv_ref, dh1_ref, dh1f_ref, dgkv_ref, dbkv_ref, dlng_ref, dlnb_ref,
         dws_ref, dbs_ref), rest = rest[:12], rest[12:]
        recv_refs, (dsv_scr, dvln_scr, ssem, rsem, lsem) = rest[:nr], rest[nr:]
        i = pl.program_id(0)
        exchanges = [_Direct(ready_refs[k], recv_refs[k], ssem.at[k], rsem.at[k], lsem.at[k], scatter=True)
                     for k in range(nr)]

        @pl.when(i == 0)
        def _():
            for e in exchanges:
                e.start()
            for r in (dgkv_ref, dbkv_ref, dlng_ref, dlnb_ref, dws_ref, dbs_ref):
                r[...] = jnp.zeros_like(r)

        dk_pre = _rot_bwd(dk_ref[...], rc_ref[...], rs1_ref[...], rs2_ref[...])
        dkv = jnp.concatenate([dk_pre, dv_ref[...]], axis=1)
        dbkv_ref[...] += jnp.sum(dkv, axis=0, keepdims=True)
        dkv_b = dkv.astype(BF)
        dkv_ref[...] = dkv_b
        h1v = h1_ref[...]
        rkv = lax.rsqrt(jnp.mean(h1v * h1v, axis=-1, keepdims=True) + EPS)
        xh_kv = h1v * rkv
        gkv = gkv_ref[...]
        nkv_ref[...] = (xh_kv * gkv).astype(BF).T
        dnkv = _dot_nt(dkv_b, wkv_ref[...])
        dgkv_ref[...] += jnp.sum(dnkv * xh_kv, axis=0, keepdims=True)
        dh1 = dh1p_ref[...] + _rms_bwd(dnkv, xh_kv, rkv, gkv)
        dh1_b = dh1.astype(BF)
        dh1_ref[...] = dh1_b
        dh1f_ref[...] = dh1
        dy = _dot_nt(dh1_b, waout_ref[...])
        uv = u_ref[...].astype(F32)
        gtv = gt_ref[...].astype(F32)
        svv = sv_ref[...].astype(F32)
        silu, dsilu = _silu_parts(gtv)
        us = uv * silu
        y_ref[...] = (us * svv).astype(BF).T
        dz_ref[:, :AW] = (dy * svv * silu).astype(BF)
        dz_ref[:, 2 * AW:] = (dy * uv * svv * dsilu).astype(BF)
        dsv_scr[...] = (dy * us).astype(BF)
        vhat_v = vhat_ref[...].astype(F32)
        lng = lng_ref[...]
        vln_b = (vhat_v * lng + lnb_ref[...]).astype(BF)
        tri = lax.broadcasted_iota(jnp.int32, (CHUNK, CHUNK), 0) >= lax.broadcasted_iota(jnp.int32, (CHUNK, CHUNK), 1)
        lane = lax.broadcasted_iota(jnp.int32, (CHUNK, LANES), 1)
        dbs = jnp.zeros((CHUNK, LANES), F32)
        for g in range(G):
            wsm = jnp.where(tri, ws_ref[g], 0.0).astype(BF)
            cols = slice(g * CHUNK, (g + 1) * CHUNK)
            dws_g = None
            for c in range(nC):
                rows = slice(c * CHUNK, (c + 1) * CHUNK)
                dsv_cg = dsv_scr[rows, cols]
                dvln_scr[rows, cols] = _dot_tn(wsm, dsv_cg)
                part = _dot_nt(dsv_cg, vln_b[rows, cols])
                dws_g = part if dws_g is None else dws_g + part
                dbs += jnp.where(lane == g, jnp.sum(dsv_cg.astype(F32), axis=-1, keepdims=True), 0.0)
            dws_ref[g] += jnp.where(tri, dws_g, 0.0)
        dbs_ref[...] += dbs
        dvln = dvln_scr[...]
        dlng_ref[...] += jnp.sum(dvln * vhat_v, axis=0, keepdims=True)
        dlnb_ref[...] += jnp.sum(dvln, axis=0, keepdims=True)
        a = dvln * lng
        dvv = rstd_ref[:, 0:1] * (a - jnp.mean(a, axis=-1, keepdims=True)
                                  - vhat_v * jnp.mean(a * vhat_v, axis=-1, keepdims=True))
        dz_ref[:, AW:2 * AW] = dvv.astype(BF)

        @pl.when(i == nT - 1)
        def _():
            for e in exchanges:
                e.finish()

    row = functools.partial(_row_spec, TM)
    col = functools.partial(_col_spec, TM)
    hbm = pl.BlockSpec(memory_space=pl.ANY)
    S = jax.ShapeDtypeStruct
    return pl.pallas_call(
        body, name="a_bwd", grid=(nT,),
        in_specs=[row(D), row(LANES), row(LANES), row(D), _const_spec((1, D)), _const_spec(w_kv.shape),
                  _const_spec(wa_out.shape), _const_spec(ws.shape),
                  _const_spec((1, AW)), _const_spec((1, AW)), pl.BlockSpec((TM, AW), lambda i: (i, 0)),
                  pl.BlockSpec((TM, AW), lambda i: (i, 1)), row(AW), row(AW), row(LANES),
                  row(LANES), row(LANES), row(LANES)] + [hbm] * nr,
        out_specs=[row(3 * AW), col(AW), col(D), row(2 * LANES), row(D), row(D),
                   _acc_spec((1, D)), _acc_spec((1, 2 * LANES)), _acc_spec((1, AW)),
                   _acc_spec((1, AW)), _acc_spec(ws.shape), _acc_spec((CHUNK, LANES))] + [hbm] * nr,
        out_shape=(S((T, 3 * AW), BF), S((AW, T), BF), S((D, T), BF), S((T, 2 * LANES), BF), S((T, D), BF),
                   S((T, D), F32),
                   S((1, D), F32), S((1, 2 * LANES), F32), S((1, AW), F32), S((1, AW), F32),
                   S(ws.shape, F32), S((CHUNK, LANES), F32)) + tuple(S(r.shape, r.dtype) for r in ready),
        scratch_shapes=[pltpu.VMEM((TM, AW), BF), pltpu.VMEM((TM, AW), F32)] + _direct_sems(nr),
        compiler_params=_params(("arbitrary",)),
    )(dh1p, dk, dv, h1, g_kv, w_kv, wa_out, ws, ln_g, ln_b, z, z, sv, vhat, rstd, rc, rs1, rs2, *ready)


def _a_in_bwd(dz, wa_in, x, dh1, g_a, ready):
    T, D = x.shape
    SH = wa_in.shape[2]
    TM = min(512, T)
    nT = T // TM
    nr = len(ready)

    def body(dz_ref, wain_ref, x_ref, dh1_ref, ga_ref, *rest):
        ready_refs, (dx_ref, n1_ref, dga_ref), rest = rest[:nr], rest[nr:nr + 3], rest[nr + 3:]
        recv_refs, (ssem, rsem, lsem) = rest[:nr], rest[nr:]
        i = pl.program_id(0)
        exchanges = [_Direct(ready_refs[k], recv_refs[k], ssem.at[k], rsem.at[k], lsem.at[k], scatter=True)
                     for k in range(nr)]

        @pl.when(i == 0)
        def _():
            for e in exchanges:
                e.start()
            dga_ref[...] = jnp.zeros_like(dga_ref)

        xv = x_ref[...]
        r1 = lax.rsqrt(jnp.mean(xv * xv, axis=-1, keepdims=True) + EPS)
        xh = xv * r1
        ga = ga_ref[...]
        n1_ref[...] = (xh * ga).astype(BF).T
        dn1 = None
        for j in range(N_DEV):
            part = _dot_nt(dz_ref[:, j * SH:(j + 1) * SH], wain_ref[j])
            dn1 = part if dn1 is None else dn1 + part
        dga_ref[...] += jnp.sum(dn1 * xh, axis=0, keepdims=True)
        dx_ref[...] = dh1_ref[...] + _rms_bwd(dn1, xh, r1, ga)

        @pl.when(i == nT - 1)
        def _():
            for e in exchanges:
                e.finish()

    row = functools.partial(_row_spec, TM)
    hbm = pl.BlockSpec(memory_space=pl.ANY)
    S = jax.ShapeDtypeStruct
    return pl.pallas_call(
        body, name="a_in_bwd", grid=(nT,),
        in_specs=[row(dz.shape[1]), _const_spec(wa_in.shape), row(D), row(D), _const_spec((1, D))] + [hbm] * nr,
        out_specs=[row(D), _col_spec(TM, D), _acc_spec((1, D))] + [hbm] * nr,
        out_shape=(S((T, D), F32), S((D, T), BF), S((1, D), F32)) + tuple(S(r.shape, r.dtype) for r in ready),
        scratch_shapes=_direct_sems(nr),
        compiler_params=_params(("arbitrary",)),
    )(dz, wa_in, x, dh1, g_a, *ready)


def _wgrad(at, b, nblk, name, bt=512):
    K, T = at.shape
    N = b.shape[1] // nblk
    BT = min(bt, T)
    nt = T // BT

    def body(a_ref, b_ref, o_ref, acc):
        t = pl.program_id(1)

        @pl.when(t == 0)
        def _():
            acc[...] = jnp.zeros_like(acc)

        acc[...] += _dot(a_ref[...], b_ref[...])

        @pl.when(t == nt - 1)
        def _():
            o_ref[0] = acc[...].astype(BF)

    return pl.pallas_call(
        body, name=name, grid=(nblk, nt),
        in_specs=[pl.BlockSpec((K, BT), lambda j, t: (0, t)), pl.BlockSpec((BT, N), lambda j, t: (t, j))],
        out_specs=pl.BlockSpec((1, K, N), lambda j, t: (j, 0, 0)),
        out_shape=jax.ShapeDtypeStruct((nblk, K, N), BF),
        scratch_shapes=[pltpu.VMEM((K, N), F32)],
        compiler_params=_params(("arbitrary", "arbitrary")),
    )(at, b)


def _wgrad_exchange(a, b, me, extras, name):
    K, T = a.shape
    N = b.shape[1] // N_DEV
    BT = min(1024, T)
    nt = T // BT
    ne = len(extras)
    last = N_DEV - 1
    n_chip = N_DEV // 2

    def body(me_ref, a_ref, b_ref, *rest):
        ex_in, recv_ref, ex_out = rest[:ne], rest[ne], rest[ne + 1:2 * ne + 1]
        acc, dstage, istage, half, d_s, d_r, i_s, i_r, lsem, ex_ssem, ex_rsem, ex_lsem = rest[2 * ne + 1:]
        s, t = pl.program_id(0), pl.program_id(1)
        x, y, c = (lax.axis_index(ax) for ax in AXES)
        ex = [_Direct(ex_in[k], ex_out[k], ex_ssem.at[k], ex_rsem.at[k], ex_lsem.at[k], scatter=True) for k in range(ne)]

        def to_sibling(k, slot):
            return pltpu.make_async_remote_copy(src_ref=dstage.at[slot], dst_ref=half.at[k], send_sem=d_s.at[k],
                                                recv_sem=d_r.at[k], device_id=(x, y, 1 - c), device_id_type=MESH)

        def to_chip(k, slot, sender):
            far = n_chip - 1 - k
            px, py = x ^ ((far >> 1) & 1), y ^ (far & 1)
            dst = recv_ref.at[2 * x + y] if sender else recv_ref.at[2 * px + py]
            return pltpu.make_async_remote_copy(src_ref=istage.at[slot], dst_ref=dst, send_sem=i_s.at[k],
                                                recv_sem=i_r.at[k], device_id=(px, py, c), device_id_type=MESH)

        @pl.when((s == 0) & (t == 0))
        def _():
            for e in ex:
                e.start()

        @pl.when(t == 0)
        def _():
            acc[...] = jnp.zeros_like(acc)

        acc[...] += _dot(a_ref[...], b_ref[...])

        @pl.when(t == nt - 1)
        def _():
            k = lax.div(s, 2)
            slot = lax.rem(k, 2)

            @pl.when(lax.rem(s, 2) == 0)
            def _():
                @pl.when(k >= 2)
                def _():
                    to_sibling(k - 2, slot).wait_send()

                dstage[slot] = acc[...].astype(BF)
                to_sibling(k, slot).start()

            @pl.when(lax.rem(s, 2) == 1)
            def _():
                to_sibling(k, slot).wait_recv()

                @pl.when(k >= 2)
                def _():
                    to_chip(k - 2, slot, True).wait_send()

                istage[slot] = (acc[...] + half[k].astype(F32)).astype(BF)

                @pl.when(k < n_chip - 1)
                def _():
                    to_chip(k, slot, True).start()

            @pl.when(s == last)
            def _():
                own = pltpu.make_async_copy(istage.at[slot], recv_ref.at[2 * x + y], lsem)
                own.start()
                to_chip(n_chip - 2, 0, True).wait_send()
                to_sibling(n_chip - 2, 0).wait_send()
                to_sibling(n_chip - 1, 1).wait_send()
                for kk in range(n_chip - 1):
                    to_chip(kk, 0, False).wait_recv()
                own.wait()
                for e in ex:
                    e.finish()

    hbm = pl.BlockSpec(memory_space=pl.ANY)
    dma = pltpu.SemaphoreType.DMA
    grid_spec = pltpu.PrefetchScalarGridSpec(
        num_scalar_prefetch=1, grid=(N_DEV, nt),
        in_specs=[pl.BlockSpec((K, BT), lambda s, t, me_ref: (0, t)),
                  pl.BlockSpec((BT, N), lambda s, t, me_ref: (t, me_ref[0] ^ (last - s)))] + [hbm] * ne,
        out_specs=[hbm] * (ne + 1),
        scratch_shapes=[pltpu.VMEM((K, N), F32), pltpu.VMEM((2, K, N), BF), pltpu.VMEM((2, K, N), BF),
                        pltpu.VMEM((n_chip, K, N), BF), dma((n_chip,)), dma((n_chip,)), dma((n_chip - 1,)),
                        dma((n_chip - 1,)), dma] + _direct_sems(ne))
    return pl.pallas_call(
        body, name=name, grid_spec=grid_spec,
        out_shape=[jax.ShapeDtypeStruct((n_chip, K, N), BF)] + [jax.ShapeDtypeStruct(e.shape, e.dtype) for e in extras],
        compiler_params=_params(("arbitrary", "arbitrary")),
    )(me, a, b, *extras)


def _my_index():
    return 4 * lax.axis_index("x") + 2 * lax.axis_index("y") + lax.axis_index("c")


def _all_gather(arrs, dtypes, name):
    n = len(arrs)

    def body(*refs):
        ins, outs = refs[:n], refs[n:2 * n]
        stages = refs[2 * n:3 * n]
        send_sems, recv_sems, local_sems = refs[3 * n:]
        gathers = [_TwoLevel(stages[a], outs[a], send_sems.at[a], recv_sems.at[a], local_sems.at[a]) for a in range(n)]
        for a in range(n):
            stages[a][...] = ins[a][...].astype(stages[a].dtype)
            gathers[a].start()
        for g in gathers:
            g.forward()
        for g in gathers:
            g.finish()

    vm = pl.BlockSpec(memory_space=pltpu.VMEM)
    hbm = pl.BlockSpec(memory_space=pl.ANY)
    return pl.pallas_call(
        body, name=name,
        in_specs=[vm] * n, out_specs=[hbm] * n,
        out_shape=[jax.ShapeDtypeStruct((N_DEV,) + a.shape, dt) for a, dt in zip(arrs, dtypes)],
        scratch_shapes=[pltpu.VMEM(a.shape, dt) for a, dt in zip(arrs, dtypes)]
        + [pltpu.SemaphoreType.DMA((n, 7)), pltpu.SemaphoreType.DMA((n, 7)), pltpu.SemaphoreType.DMA((n,))],
        compiler_params=pltpu.CompilerParams(vmem_limit_bytes=VMEM_LIMIT),
    )(*arrs)


def _peer(mask):
    x, y, c = (lax.axis_index(a) for a in AXES)
    return (x ^ ((mask >> 2) & 1), y ^ ((mask >> 1) & 1), c ^ (mask & 1))


def _dev_index(p):
    return 4 * p[0] + 2 * p[1] + p[2]


class _Direct:
    def __init__(self, src, dst, send_sems, recv_sems, local_sem, scatter):
        me = _my_index()
        self.own = pltpu.make_async_copy(src.at[me] if scatter else src, dst.at[me], local_sem)
        self.sends, self.recvs = [], []
        for k in range(1, N_DEV):
            p = _peer(k)
            pi = _dev_index(p)
            sems = dict(send_sem=send_sems.at[k - 1], recv_sem=recv_sems.at[k - 1], device_id=p, device_id_type=MESH)
            self.sends.append(pltpu.make_async_remote_copy(src_ref=src.at[pi] if scatter else src, dst_ref=dst.at[me],
                                                           **sems))
            self.recvs.append(pltpu.make_async_remote_copy(src_ref=src.at[me] if scatter else src, dst_ref=dst.at[pi],
                                                           **sems))

    def start(self):
        self.own.start()
        for cp in self.sends:
            cp.start()

    def finish(self):
        for cp in self.sends:
            cp.wait_send()
        for cp in self.recvs:
            cp.wait_recv()
        self.own.wait()


class _TwoLevel:
    def __init__(self, src, dst, send_sems, recv_sems, local_sem):
        x, y, c = (lax.axis_index(a) for a in AXES)
        self.me, self.sibling = (x, y, c), (x, y, 1 - c)
        self.chips = [(1 - x, y), (x, 1 - y), (1 - x, 1 - y)]
        self.src, self.dst, self.send_sems, self.recv_sems = src, dst, send_sems, recv_sems
        self.own = pltpu.make_async_copy(src, dst.at[_dev_index(self.me)], local_sem)

    def _copy(self, k, block, to, from_src=False):
        slot = self.dst.at[_dev_index(block)]
        return pltpu.make_async_remote_copy(src_ref=self.src if from_src else slot, dst_ref=slot,
                                            send_sem=self.send_sems.at[k], recv_sem=self.recv_sems.at[k],
                                            device_id=to, device_id_type=MESH)

    def _firsts(self):
        c = self.me[2]
        return [self._copy(0, self.me, self.sibling, True)] + [self._copy(1 + j, self.me, (*chip, c), True)
                                                               for j, chip in enumerate(self.chips)]

    def _passed(self):
        c = self.me[2]
        return [self._copy(4 + j, (*chip, c), self.sibling) for j, chip in enumerate(self.chips)]

    def start(self):
        self.own.start()
        for cp in self._firsts():
            cp.start()

    def forward(self):
        c = self.me[2]
        for j, (chip, fwd) in enumerate(zip(self.chips, self._passed())):
            self._copy(1 + j, (*chip, c), self.me).wait_recv()
            fwd.start()

    def finish(self):
        c = self.me[2]
        self._copy(0, self.sibling, self.me).wait_recv()
        for j, chip in enumerate(self.chips):
            self._copy(4 + j, (*chip, 1 - c), self.me).wait_recv()
        for cp in self._firsts() + self._passed():
            cp.wait_send()
        self.own.wait()


def _direct_sems(n):
    return [pltpu.SemaphoreType.DMA((n, 7)), pltpu.SemaphoreType.DMA((n, 7)), pltpu.SemaphoreType.DMA((n,))]


def _adam_math(w, g, m, v):
    m = ADAM_B1 * m + (1.0 - ADAM_B1) * g
    v = ADAM_B2 * v + (1.0 - ADAM_B2) * (g * g)
    m_hat = m / (1.0 - ADAM_B1 ** ADAM_STEP)
    v_hat = v / (1.0 - ADAM_B2 ** ADAM_STEP)
    delta = -ADAM_LR * (m_hat / (jnp.sqrt(v_hat) + ADAM_EPS) + ADAM_WD * w)
    return delta, m, v


def _sum_adam(parts, w, m, v, name):
    R, C = w.shape
    NP = parts.shape[0]
    BR = CHUNK if R % CHUNK == 0 else R

    def body(p_ref, w_ref, m_ref, v_ref, g_ref, d_ref, nm_ref, nv_ref):
        g = p_ref[0].astype(F32)
        for i in range(1, NP):
            g = g + p_ref[i].astype(F32)
        g_ref[...] = g
        d_ref[...], nm_ref[...], nv_ref[...] = _adam_math(w_ref[...], g, m_ref[...], v_ref[...])

    blk = pl.BlockSpec((BR, C), lambda i: (i, 0))
    S = jax.ShapeDtypeStruct((R, C), F32)
    return pl.pallas_call(
        body, name=name, grid=(R // BR,),
        in_specs=[pl.BlockSpec((NP, BR, C), lambda i: (0, i, 0)), blk, blk, blk],
        out_specs=[blk] * 4, out_shape=(S,) * 4,
        compiler_params=_params(("arbitrary",)),
    )(parts, w, m, v)


def _sum8(parts, name):
    _, R, C = parts.shape

    def body(p_ref, o_ref):
        g = p_ref[0]
        for i in range(1, N_DEV):
            g = g + p_ref[i]
        o_ref[...] = g

    return pl.pallas_call(body, name=name, out_shape=jax.ShapeDtypeStruct((R, C), F32))(parts)


def _adam_only(g, w, m, v, name):
    def body(g_ref, w_ref, m_ref, v_ref, d_ref, nm_ref, nv_ref):
        d_ref[...], nm_ref[...], nv_ref[...] = _adam_math(w_ref[...], g_ref[...], m_ref[...], v_ref[...])

    S = jax.ShapeDtypeStruct(w.shape, F32)
    return pl.pallas_call(body, name=name, out_shape=(S,) * 3)(g, w, m, v)


def _rope_tables(T):
    pos = np.arange(T, dtype=np.float32)
    inv_freq = (np.float64(ROPE_THETA) ** (-np.arange(0, HEAD_DIM, 2, dtype=np.float64) / HEAD_DIM)).astype(np.float32)
    ang = (pos[:, None] * inv_freq[None, :]).astype(np.float64)
    cos, sin, zero = np.cos(ang).astype(np.float32), np.sin(ang).astype(np.float32), np.zeros(ang.shape, np.float32)
    c = np.concatenate([cos, cos, cos, cos], axis=1)
    s1 = np.concatenate([-sin, zero, -sin, zero], axis=1)
    s2 = np.concatenate([zero, sin, zero, sin], axis=1)
    return jnp.asarray(c), jnp.asarray(s1), jnp.asarray(s2)


SUBLANES = 8


def _nrows(size):
    return -(-size // (SUBLANES * LANES)) * SUBLANES


def _rows(a):
    flat = a.reshape(-1)
    pad = _nrows(flat.shape[0]) * LANES - flat.shape[0]
    if pad:
        flat = jnp.concatenate([flat, jnp.zeros((pad,), flat.dtype)])
    return flat.reshape(-1, LANES)


def _pack(arrs, total_rows):
    rows = [_rows(a) for a in arrs]
    used = sum(r.shape[0] for r in rows)
    if total_rows > used:
        rows.append(jnp.zeros((total_rows - used, LANES), F32))
    return jnp.concatenate(rows, axis=0)


def _unpack(packed, shapes):
    out, at = [], 0
    for shp in shapes:
        size = math.prod(shp)
        nrow = _nrows(size)
        out.append(packed[at:at + nrow].reshape(-1)[:size].reshape(shp))
        at += nrow
    return out


def kernel(x, a_norm_g, a_w_in, a_ln_g, a_ln_b, a_ws, a_bs, a_w_out, kv_norm_g, w_kv, b_kv, b_norm_g, b_w_in, b_bq, b_sinks, b_w_out, final_norm_g, loss_target, m_a_norm_g, m_a_w_in, m_a_ln_g, m_a_ln_b, m_a_ws, m_a_bs, m_a_w_out, m_kv_norm_g, m_w_kv, m_b_kv, m_b_norm_g, m_b_w_in, m_b_bq, m_b_sinks, m_b_w_out, m_final_norm_g, v_a_norm_g, v_a_w_in, v_a_ln_g, v_a_ln_b, v_a_ws, v_a_bs, v_a_w_out, v_kv_norm_g, v_w_kv, v_b_kv, v_b_norm_g, v_b_w_in, v_b_bq, v_b_sinks, v_b_w_out, v_final_norm_g):
    T, D = x.shape[1], x.shape[2]
    AW = a_ln_g.shape[1] * N_DEV
    G = a_ws.shape[1]
    assert w_kv.shape[1] == 2 * LANES and a_ws.shape[2] == CHUNK and T % CHUNK == 0
    me = _my_index()

    vec = jnp.concatenate([a_norm_g, a_ln_g, a_ln_b], axis=1)
    vec = jnp.broadcast_to(vec, (8, vec.shape[1]))
    wa_in, vecs = _all_gather([a_w_in[0], vec], [BF, F32], "gather_weights")
    vecs = vecs[:, 0, :]
    ds = D // N_DEV
    g_a = vecs[:, :ds].reshape(1, D)
    ln_g = vecs[:, ds:ds + AW // N_DEV].reshape(1, AW)
    ln_b = vecs[:, ds + AW // N_DEV:].reshape(1, AW)

    rc, rs1, rs2 = _rope_tables(T)
    ws = a_ws[0]
    bs_t = a_bs[0].T
    g_kv = kv_norm_g.reshape(1, D)
    bkv = b_kv.reshape(1, -1)
    g_f = final_norm_g.reshape(1, D)
    sinks = jnp.repeat(b_sinks.reshape(2, 4, 2).transpose(0, 2, 1).reshape(4, 4), CHUNK, axis=1)
    xs, tgt = x[0], loss_target[0]

    z, us, vln, vhat, rstd, wa_out, wkv = _in_proj(xs, g_a, wa_in, ln_g, ln_b, [a_w_out[0], w_kv])
    wa_out = wa_out.reshape(AW, D)
    wkv = wkv.reshape(D, 2 * LANES)
    h1, sv, k4, v4, kt, vt, wb_in, wb_out = _a_fwd(
        xs, us, vln, ws, bs_t, wa_out, g_kv, wkv, bkv, rc, rs1, rs2, [b_w_in[0], b_w_out[0]])
    wb_out = wb_out.reshape(-1, D)
    q, g2, o, dh2, dh2_b, loss, d_gf = _b_fwd(h1, b_norm_g, wb_in, b_bq, rc, rs1, rs2, k4, vt, sinks, wb_out, g_f, tgt)
    dh1p, dz2, n2, y2, dk, dv, d_bq, d_gb, d_sink = _b_bwd(dh2, h1, q, g2, o, k4, v4, kt, sinks, wb_out, wb_in,
                                                           b_norm_g, rc, rs1, rs2)
    d_sink = d_sink[:, :4].reshape(2, 2, 4).transpose(0, 2, 1).reshape(1, 16)
    gw_b_in = _wgrad(n2, dz2, N_DEV, "wgrad_b_in", bt=2048)
    gw_b_out = _wgrad(y2, dh2_b, 1, "wgrad_b_out").reshape(N_DEV, -1, D)
    (dz, y, nkv, dkv, dh1, dh1_f, d_gkv, d_bkv, d_lng, d_lnb, d_ws, d_bst, r_b_in, r_b_out) = _a_bwd(
        dh1p, dk, dv, h1, g_kv, wkv, wa_out, ws, ln_g, ln_b, z, sv, vhat, rstd, rc, rs1, rs2, [gw_b_in, gw_b_out])
    gw_a_out = _wgrad(y, dh1, 1, "wgrad_a_out").reshape(N_DEV, AW // N_DEV, D)
    gw_kv = _wgrad(nkv, dkv, 1, "wgrad_kv").reshape(N_DEV, D // N_DEV, 2 * LANES)
    dx, n1, d_ga, r_a_out, r_kv = _a_in_bwd(dz, wa_in, xs, dh1_f, g_a, [gw_a_out, gw_kv])
    small = [d_ws, d_bst[:, :G].T, d_gkv, d_bkv, d_gb, d_bq, d_sink, d_gf, d_ga, d_lng, d_lnb, loss]
    used = sum(_nrows(a.size) for a in small)
    per = -(-used // (SUBLANES * N_DEV)) * SUBLANES
    small_pack = _pack(small, per * N_DEV).reshape(N_DEV, per, LANES)
    r_a_in, r_small = _wgrad_exchange(n1, dz, me.reshape(1), [small_pack], "wgrad_a_in")

    g_a_in, d_a_in, nm_a_in, nv_a_in = _sum_adam(r_a_in, a_w_in[0], m_a_w_in[0], v_a_w_in[0], "adam_a_in")
    g_a_out, d_a_out, nm_a_out, nv_a_out = _sum_adam(r_a_out, a_w_out[0], m_a_w_out[0], v_a_w_out[0], "adam_a_out")
    g_kvw, d_kvw, nm_kvw, nv_kvw = _sum_adam(r_kv, w_kv, m_w_kv, v_w_kv, "adam_kv")
    g_b_in, d_b_in, nm_b_in, nv_b_in = _sum_adam(r_b_in, b_w_in[0], m_b_w_in[0], v_b_w_in[0], "adam_b_in")
    g_b_out, d_b_out, nm_b_out, nv_b_out = _sum_adam(r_b_out, b_w_out[0], m_b_w_out[0], v_b_w_out[0], "adam_b_out")

    red = _sum8(r_small, "sum_small")
    (full_small,) = _all_gather([red], [F32], "gather_small")
    full_small = full_small.reshape(N_DEV * per, LANES)
    rep_shapes = [a_ws.shape, a_bs.shape, kv_norm_g.shape, b_kv.shape, b_norm_g.shape, b_bq.shape, b_sinks.shape,
                  final_norm_g.shape]
    gs = _unpack(full_small, rep_shapes + [(N_DEV, a_norm_g.shape[1]), (N_DEV, a_ln_g.shape[1]),
                                           (N_DEV, a_ln_b.shape[1]), (1, 1)])
    loss = gs.pop()[0, 0]
    g_ang = lax.dynamic_slice_in_dim(gs[8], me, 1, axis=0)
    g_alng = lax.dynamic_slice_in_dim(gs[9], me, 1, axis=0)
    g_alnb = lax.dynamic_slice_in_dim(gs[10], me, 1, axis=0)
    sm_g = gs[:8] + [g_ang, g_alng, g_alnb]
    sm_shapes = [a.shape for a in sm_g]
    tot = sum(_nrows(a.size) for a in sm_g)
    pw = _pack([a_ws, a_bs, kv_norm_g, b_kv, b_norm_g, b_bq, b_sinks, final_norm_g, a_norm_g, a_ln_g, a_ln_b], tot)
    pm = _pack([m_a_ws, m_a_bs, m_kv_norm_g, m_b_kv, m_b_norm_g, m_b_bq, m_b_sinks, m_final_norm_g, m_a_norm_g,
                m_a_ln_g, m_a_ln_b], tot)
    pv = _pack([v_a_ws, v_a_bs, v_kv_norm_g, v_b_kv, v_b_norm_g, v_b_bq, v_b_sinks, v_final_norm_g, v_a_norm_g,
                v_a_ln_g, v_a_ln_b], tot)
    pg = _pack(sm_g, tot)
    pd, pnm, pnv = _adam_only(pg, pw, pm, pv, "adam_small")
    sd, snm, snv = _unpack(pd, sm_shapes), _unpack(pnm, sm_shapes), _unpack(pnv, sm_shapes)

    def order(big, sm):
        a_in, a_out, kvw, b_in, b_out = big
        ws_, bs_, kvg, bkv_, bng, bq_, snk, fng, ang, alng, alnb = sm
        return (ang, a_in[None], alng, alnb, ws_, bs_, a_out[None], kvg, kvw, bkv_, bng, b_in[None], bq_, snk,
                b_out[None], fng)

    grads = order((g_a_in, g_a_out, g_kvw, g_b_in, g_b_out), sm_g)
    deltas = order((d_a_in, d_a_out, d_kvw, d_b_in, d_b_out), sd)
    new_m = order((nm_a_in, nm_a_out, nm_kvw, nm_b_in, nm_b_out), snm)
    new_v = order((nv_a_in, nv_a_out, nv_kvw, nv_b_in, nv_b_out), snv)
    return (loss, dx[None], *grads, *deltas, *new_m, *new_v)
```

```python
import functools
import math

import jax
import jax.numpy as jnp
import numpy as np
from jax import lax
from jax.experimental import pallas as pl
from jax.experimental.pallas import tpu as pltpu

CHUNK = 128
HEAD_DIM = 64
ROPE_THETA = 10000.0
EPS = 1e-5
ADAM_LR = 0.001
ADAM_B1 = 0.9
ADAM_B2 = 0.999
ADAM_EPS = 1e-08
ADAM_WD = 0.01
ADAM_STEP = 10
N_DEV = 8
LANES = 128
NEG = -1e30

BF = jnp.bfloat16
F32 = jnp.float32
MESH = pl.DeviceIdType.MESH
AXES = ("x", "y", "c")
VMEM_LIMIT = 56 * 1024 * 1024


def _dot(a, b):
    return jnp.dot(a, b, preferred_element_type=F32)


def _dot_nt(a, b):
    return lax.dot_general(a, b, (((1,), (1,)), ((), ())), preferred_element_type=F32)


def _dot_tn(a, b):
    return lax.dot_general(a, b, (((0,), (0,)), ((), ())), preferred_element_type=F32)


def _const_spec(shape):
    nd = len(shape)
    return pl.BlockSpec(shape, lambda *_: (0,) * nd, pipeline_mode=pl.Buffered(1))


def _acc_spec(shape):
    nd = len(shape)
    return pl.BlockSpec(shape, lambda *_: (0,) * nd)


def _row_spec(tm, width):
    return pl.BlockSpec((tm, width), lambda i: (i, 0))


def _col_spec(tm, height):
    return pl.BlockSpec((height, tm), lambda i: (0, i))


def _params(sem):
    return pltpu.CompilerParams(dimension_semantics=sem, vmem_limit_bytes=VMEM_LIMIT)


def _rot(x, c, s1, s2):
    return x * c + pltpu.roll(x, 96, 1) * s1 + pltpu.roll(x, 32, 1) * s2


def _rot_bwd(d, c, s1, s2):
    return d * c + pltpu.roll(d * s1, 32, 1) + pltpu.roll(d * s2, 96, 1)


def _silu_parts(g):
    sg = jax.nn.sigmoid(g)
    return g * sg, sg * (1.0 + g * (1.0 - sg))


def _rms_bwd(dn, xh, r, g):
    a = dn * g
    return r * (a - xh * jnp.mean(a * xh, axis=-1, keepdims=True))


def _lane_lo(shape):
    return lax.broadcasted_iota(jnp.int32, shape, 1) < HEAD_DIM


def _split4(t):
    lo = _lane_lo(t.shape)
    tr = pltpu.roll(t, HEAD_DIM, 1)
    z = jnp.zeros_like(t)
    return jnp.concatenate([jnp.where(lo, t, z), jnp.where(lo, z, tr), jnp.where(lo, tr, z), jnp.where(lo, z, t)], axis=1)


def _stack_pairs(t, h):
    return jnp.concatenate([t[:, (h * 4 + j) * LANES:(h * 4 + j + 1) * LANES] for j in range(4)], axis=0)


def _upper():
    shape = (CHUNK, 4 * CHUNK)
    return lax.broadcasted_iota(jnp.int32, shape, 0) > (lax.broadcasted_iota(jnp.int32, shape, 1) & (CHUNK - 1))


def _band_rows(ref, prev, cur, h):
    a = slice(2 * h * LANES, (2 * h + 1) * LANES)
    b = slice((2 * h + 1) * LANES, (2 * h + 2) * LANES)
    return jnp.concatenate([ref[pl.ds(prev, CHUNK), a], ref[pl.ds(cur, CHUNK), a],
                            ref[pl.ds(prev, CHUNK), b], ref[pl.ds(cur, CHUNK), b]], axis=0)


def _band_cols(ref, pci, ci, h):
    a = slice(2 * h * LANES, (2 * h + 1) * LANES)
    b = slice((2 * h + 1) * LANES, (2 * h + 2) * LANES)
    return jnp.concatenate([ref[pci, a, :], ref[ci, a, :], ref[pci, b, :], ref[ci, b, :]], axis=1)


def _fold(t, upper, has_prev=None):
    out = []
    for k in range(2):
        prev = t[2 * k * CHUNK:(2 * k + 1) * CHUNK]
        if has_prev is not None:
            prev = jnp.where(has_prev, prev, NEG)
        out.append(jnp.where(upper, prev, t[(2 * k + 1) * CHUNK:(2 * k + 2) * CHUNK]))
    return out


def _unfold(fa, fb, upper):
    z = jnp.zeros_like(fa)
    return jnp.concatenate([jnp.where(upper, fa, z), jnp.where(upper, z, fa),
                            jnp.where(upper, fb, z), jnp.where(upper, z, fb)], axis=0)


def _softmax_sink(f, sink):
    m = jnp.maximum(jnp.max(f, axis=0, keepdims=True), sink)
    p = jnp.exp(f - m)
    es = jnp.exp(sink - m)
    inv = 1.0 / (jnp.sum(p, axis=0, keepdims=True) + es)
    return p * inv, es * inv


class _Riding:
    def __init__(self, shards, gathered, stages, sems, n_steps):
        self.shards, self.stages, self.n_steps = shards, stages, n_steps
        ssem, rsem, lsem = sems
        self.gathers = [_TwoLevel(stages[k], gathered[k], ssem.at[k], rsem.at[k], lsem.at[k])
                        for k in range(len(shards))]

    def begin(self, i):
        @pl.when(i == 0)
        def _():
            for shard, stage, g in zip(self.shards, self.stages, self.gathers):
                stage[...] = shard[...].astype(stage.dtype)
                g.start()

    def end(self, i):
        @pl.when(i == self.n_steps // 2)
        def _():
            for g in self.gathers:
                g.forward()

        @pl.when(i == self.n_steps - 1)
        def _():
            for g in self.gathers:
                g.finish()

    @staticmethod
    def specs(later):
        nl = len(later)
        hbm = pl.BlockSpec(memory_space=pl.ANY)
        return ([_const_spec(w.shape) for w in later], [hbm] * nl,
                tuple(jax.ShapeDtypeStruct((N_DEV,) + w.shape, BF) for w in later),
                [pltpu.VMEM(w.shape, BF) for w in later] + _direct_sems(nl))


PASS_MASKS = (0, 1, 4, 2, 5, 3, 6, 7)


def _in_proj(x, w_shard, vec_shard, slots, later):
    T, D = x.shape
    SH = w_shard.shape[1]
    VW = vec_shard.shape[1]
    TM = min(512, T)
    nT = T // TM
    nl = len(later)
    ds = D // N_DEV
    last = N_DEV - 1

    def body(slots_ref, x_ref, wsh_ref, vsh_ref, *rest):
        shards, rest = rest[:nl], rest[nl:]
        (z_ref, wout_ref, vout_ref), rest = rest[:3], rest[3:]
        gathered, rest = rest[:nl], rest[nl:]
        (w_scr, vec_scr, vstage, n1_scr, ga_scr, w_s, w_r, w_l, v_s, v_r, v_l), rest = rest[:11], rest[11:]
        stages, (l_s, l_r, l_l) = rest[:nl], rest[nl:]
        p, i = pl.program_id(0), pl.program_id(1)
        me = _my_index()
        wg = _TwoLevel(w_scr.at[me], w_scr, w_s, w_r, w_l, own=False)
        vg = _TwoLevel(vstage, vec_scr, v_s, v_r, v_l)
        lg = [_TwoLevel(stages[k], gathered[k], l_s.at[k], l_r.at[k], l_l.at[k]) for k in range(nl)]
        w_copy = pltpu.make_async_copy(w_scr, wout_ref, w_l)

        def at_pass(k):
            return (p == k) & (i == 0)

        @pl.when(at_pass(0))
        def _():
            vstage[...] = vsh_ref[...]
            vg.start()
            w_scr[me] = wsh_ref[...].astype(BF)
            wg.start()
            for k in range(nl):
                stages[k][...] = shards[k][...].astype(BF)
                lg[k].start()
            vg.forward()
            vg.finish()
            for j in range(N_DEV):
                ga_scr[:, j * ds:(j + 1) * ds] = vec_scr[j, 0:1, 0:ds]
            vout_ref[...] = vec_scr[...]

        @pl.when(at_pass(1))
        def _():
            wg.wait_sibling()

        for k, j in ((2, 0), (3, 1), (6, 2)):
            @pl.when(at_pass(k))
            def _(j=j):
                wg.wait_chip_and_forward(j)

        for k, j in ((4, 0), (5, 1), (7, 2)):
            @pl.when(at_pass(k))
            def _(j=j):
                wg.wait_passed(j)

        @pl.when(at_pass(last))
        def _():
            w_copy.start()

        @pl.when(p == 0)
        def _():
            xv = x_ref[...]
            r1 = lax.rsqrt(jnp.mean(xv * xv, axis=-1, keepdims=True) + EPS)
            n1_scr[i] = (xv * r1 * ga_scr[...]).astype(BF)

        z_ref[...] = _dot(n1_scr[i], w_scr[slots_ref[p]]).astype(BF)

        @pl.when((p == last) & (i == nT - 1))
        def _():
            wg.wait_sends()
            for g in lg:
                g.forward()
            for g in lg:
                g.finish()
            w_copy.wait()

    hbm = pl.BlockSpec(memory_space=pl.ANY)
    dma = pltpu.SemaphoreType.DMA
    S = jax.ShapeDtypeStruct
    grid_spec = pltpu.PrefetchScalarGridSpec(
        num_scalar_prefetch=1, grid=(N_DEV, nT),
        in_specs=[pl.BlockSpec((TM, D), lambda p, i, s: (jnp.where(p == 0, i, nT - 1), 0)),
                  pl.BlockSpec(w_shard.shape, lambda p, i, s: (0, 0), pipeline_mode=pl.Buffered(1)),
                  pl.BlockSpec(vec_shard.shape, lambda p, i, s: (0, 0), pipeline_mode=pl.Buffered(1))]
        + [pl.BlockSpec(w.shape, lambda p, i, s: (0, 0), pipeline_mode=pl.Buffered(1)) for w in later],
        out_specs=[pl.BlockSpec((TM, SH), lambda p, i, s: (i, s[p])), hbm,
                   pl.BlockSpec((N_DEV,) + vec_shard.shape, lambda p, i, s: (0, 0, 0))] + [hbm] * nl,
        scratch_shapes=[pltpu.VMEM((N_DEV, D, SH), BF), pltpu.VMEM((N_DEV,) + vec_shard.shape, F32),
                        pltpu.VMEM(vec_shard.shape, F32), pltpu.VMEM((nT, TM, D), BF), pltpu.VMEM((1, D), F32),
                        dma((7,)), dma((7,)), dma, dma((7,)), dma((7,)), dma]
        + [pltpu.VMEM(w.shape, BF) for w in later] + _direct_sems(nl))
    return pl.pallas_call(
        body, name="a_in_proj", grid_spec=grid_spec,
        out_shape=(S((T, N_DEV * SH), BF), S((N_DEV, D, SH), BF), S((N_DEV,) + vec_shard.shape, F32))
        + tuple(S((N_DEV,) + w.shape, BF) for w in later),
        compiler_params=_params(("arbitrary", "arbitrary")),
    )(slots, x, w_shard, vec_shard, *later)


def _a_fwd(x, z, ln_g, ln_b, ws, bs_t, wa_out, g_kv, w_kv, b_kv, rc, rs1, rs2, later):
    T, D = x.shape
    AW = wa_out.shape[0]
    G = ws.shape[0]
    TM = min(256, T)
    nT = T // TM
    nC = TM // CHUNK
    nl = len(later)

    def body(x_ref, u_ref, v_ref, gt_ref, lng_ref, lnb_ref, ws_ref, bst_ref, waout_ref, gkv_ref, wkv_ref, bkv_ref,
             rc_ref, rs1_ref, rs2_ref, *rest):
        shards, rest = rest[:nl], rest[nl:]
        (h1_ref, sv_ref, vhat_ref, rstd_ref, k4_ref, v4_ref, kt_ref, vt_ref), rest = rest[:8], rest[8:]
        gathered, sv_scr, stages, sems = rest[:nl], rest[nl], rest[nl + 1:2 * nl + 1], rest[2 * nl + 1:]
        i = pl.program_id(0)
        riding = _Riding(shards, gathered, stages, sems, nT)
        riding.begin(i)
        xv = x_ref[...]
        u = u_ref[...].astype(F32)
        v = v_ref[...].astype(F32)
        gt = gt_ref[...].astype(F32)
        mu = jnp.mean(v, axis=-1, keepdims=True)
        xc = v - mu
        rstd = lax.rsqrt(jnp.mean(xc * xc, axis=-1, keepdims=True) + EPS)
        vhat = xc * rstd
        vln = (vhat * lng_ref[...] + lnb_ref[...]).astype(BF)
        tri = lax.broadcasted_iota(jnp.int32, (CHUNK, CHUNK), 0) >= lax.broadcasted_iota(jnp.int32, (CHUNK, CHUNK), 1)
        for g in range(G):
            wsm = jnp.where(tri, ws_ref[g], 0.0).astype(BF)
            bias = bst_ref[:, g:g + 1]
            for c in range(nC):
                blk = vln[c * CHUNK:(c + 1) * CHUNK, g * CHUNK:(g + 1) * CHUNK]
                sv_scr[c * CHUNK:(c + 1) * CHUNK, g * CHUNK:(g + 1) * CHUNK] = _dot(wsm, blk) + bias
        sv = sv_scr[...]
        silu, _ = _silu_parts(gt)
        y = (u * sv * silu).astype(BF)
        h1 = xv + _dot(y, waout_ref[...])
        h1_ref[...] = h1
        sv_ref[...] = sv.astype(BF)
        vhat_ref[...] = vhat.astype(BF)
        rstd_ref[...] = jnp.broadcast_to(rstd, rstd_ref.shape)
        rkv = lax.rsqrt(jnp.mean(h1 * h1, axis=-1, keepdims=True) + EPS)
        nkv = (h1 * rkv * gkv_ref[...]).astype(BF)
        kv = _dot(nkv, wkv_ref[...]) + bkv_ref[...]
        k_rot = _rot(kv[:, :LANES], rc_ref[...], rs1_ref[...], rs2_ref[...])
        for src, ref, tref in ((k_rot, k4_ref, kt_ref), (kv[:, LANES:], v4_ref, vt_ref)):
            t4 = _split4(src)
            ref[...] = t4.astype(BF)
            for c in range(nC):
                for b in range(4):
                    blk = t4[c * CHUNK:(c + 1) * CHUNK, b * LANES:(b + 1) * LANES]
                    tref[c, b * LANES:(b + 1) * LANES, :] = blk.T.astype(BF)
        riding.end(i)

    row = functools.partial(_row_spec, TM)
    zcol = [pl.BlockSpec((TM, AW), functools.partial(lambda k, i: (i, k), k)) for k in range(3)]
    tr = pl.BlockSpec((nC, 4 * LANES, CHUNK), lambda i: (i, 0, 0))
    r_in, r_out, r_shape, r_scratch = _Riding.specs(later)
    S = jax.ShapeDtypeStruct
    return pl.pallas_call(
        body, name="a_fwd", grid=(nT,),
        in_specs=[row(D)] + zcol + [_const_spec((1, AW)), _const_spec((1, AW)),
                  _const_spec(ws.shape), _const_spec(bs_t.shape), _const_spec(wa_out.shape), _const_spec((1, D)),
                  _const_spec(w_kv.shape), _const_spec((1, 2 * LANES)), row(LANES), row(LANES), row(LANES)] + r_in,
        out_specs=[row(D), row(AW), row(AW), row(LANES), row(4 * LANES), row(4 * LANES), tr, tr] + r_out,
        out_shape=(S((T, D), F32), S((T, AW), BF), S((T, AW), BF), S((T, LANES), F32),
                   S((T, 4 * LANES), BF), S((T, 4 * LANES), BF),
                   S((T // CHUNK, 4 * LANES, CHUNK), BF), S((T // CHUNK, 4 * LANES, CHUNK), BF)) + r_shape,
        scratch_shapes=[pltpu.VMEM((TM, AW), F32)] + r_scratch,
        compiler_params=_params(("arbitrary",)),
    )(x, z, z, z, ln_g, ln_b, ws, bs_t, wa_out, g_kv, w_kv, b_kv, rc, rs1, rs2, *later)


def _b_fwd(h1, g_b, wb_in, bq, rc, rs1, rs2, k4, vt, sinks, wb_out, g_f, target):
    T, D = h1.shape
    BW = wb_out.shape[0]
    SH = wb_in.shape[2]
    TM = min(256, T)
    nC = TM // CHUNK
    nP = BW // LANES

    def body(h1_ref, gb_ref, wbin_ref, bq_ref, rc_ref, rs1_ref, rs2_ref, k4_ref, vt_ref, sink_ref, wbout_ref, gf_ref,
             tgt_ref, q_ref, g2_ref, o_ref, dh2_ref, dh2b_ref, loss_ref, dgf_ref, z_scr, o_scr):
        i = pl.program_id(0)
        h1v = h1_ref[...]
        r2 = lax.rsqrt(jnp.mean(h1v * h1v, axis=-1, keepdims=True) + EPS)
        n2 = (h1v * r2 * gb_ref[...]).astype(BF)
        for j in range(N_DEV):
            z_scr[:, j * SH:(j + 1) * SH] = _dot(n2, wbin_ref[j])
        c_t, s1_t, s2_t = rc_ref[...], rs1_ref[...], rs2_ref[...]
        for p in range(nP):
            cols = slice(p * LANES, (p + 1) * LANES)
            qp = _rot(z_scr[:, cols] + bq_ref[:, cols], c_t, s1_t, s2_t) * (HEAD_DIM ** -0.5)
            q_ref[:, cols] = qp.astype(BF)
        g2 = z_scr[:, BW:]
        g2_ref[...] = g2.astype(BF)
        upper = _upper()
        for c in range(nC):
            ci = i * nC + c
            rows = slice(c * CHUNK, (c + 1) * CHUNK)
            pci = jnp.maximum(ci - 1, 0)
            prev = pl.multiple_of(pci * CHUNK, CHUNK)
            cur = pl.multiple_of(ci * CHUNK, CHUNK)
            qc = q_ref[rows, :]
            for h in range(2):
                st = _dot_nt(_band_rows(k4_ref, prev, cur, h), _stack_pairs(qc, h))
                fa, fb = _fold(st, upper, ci > 0)
                pa, _ = _softmax_sink(fa, sink_ref[2 * h:2 * h + 1, :])
                pb, _ = _softmax_sink(fb, sink_ref[2 * h + 1:2 * h + 2, :])
                ot = _dot(_band_cols(vt_ref, pci, ci, h), _unfold(pa, pb, upper).astype(BF))
                for j in range(4):
                    o_scr[rows, (h * 4 + j) * LANES:(h * 4 + j + 1) * LANES] = ot[:, j * CHUNK:(j + 1) * CHUNK].T
        o = o_scr[...]
        o_ref[...] = o.astype(BF)
        silu, _ = _silu_parts(g2)
        h2 = h1v + _dot((o * silu).astype(BF), wbout_ref[...])
        rf = lax.rsqrt(jnp.mean(h2 * h2, axis=-1, keepdims=True) + EPS)
        xh = h2 * rf
        gf = gf_ref[...]
        err = xh * gf - tgt_ref[...]
        dyf = err * (1.0 / D)
        dh2 = _rms_bwd(dyf, xh, rf, gf)
        dh2_ref[...] = dh2
        dh2b_ref[...] = dh2.astype(BF)

        @pl.when(i == 0)
        def _():
            loss_ref[...] = jnp.zeros_like(loss_ref)
            dgf_ref[...] = jnp.zeros_like(dgf_ref)

        loss_ref[...] += 0.5 * jnp.sum(jnp.mean(err * err, axis=-1, keepdims=True), axis=0, keepdims=True)
        dgf_ref[...] += jnp.sum(dyf * xh, axis=0, keepdims=True)

    row = functools.partial(_row_spec, TM)
    S = jax.ShapeDtypeStruct
    return pl.pallas_call(
        body, name="b_fwd", grid=(T // TM,),
        in_specs=[row(D), _const_spec((1, D)), _const_spec(wb_in.shape), _const_spec((1, BW)), row(LANES), row(LANES),
                  row(LANES), _const_spec(k4.shape), _const_spec(vt.shape), _const_spec(sinks.shape),
                  _const_spec(wb_out.shape), _const_spec((1, D)), row(D)],
        out_specs=[row(BW), row(BW), row(BW), row(D), row(D), _acc_spec((1, 1)), _acc_spec((1, D))],
        out_shape=(S((T, BW), BF), S((T, BW), BF), S((T, BW), BF), S((T, D), F32), S((T, D), BF), S((1, 1), F32),
                   S((1, D), F32)),
        scratch_shapes=[pltpu.VMEM((TM, 2 * BW), F32), pltpu.VMEM((TM, BW), F32)],
        compiler_params=_params(("arbitrary",)),
    )(h1, g_b, wb_in, bq, rc, rs1, rs2, k4, vt, sinks, wb_out, g_f, target)


def _b_bwd(dh2, h1, q, g2, o, k4, v4, kt, sinks, wb_out, wb_in, g_b, rc, rs1, rs2):
    T, D = h1.shape
    BW = wb_out.shape[0]
    SH = wb_in.shape[2]
    TM = min(256, T)
    nT = T // TM
    nC = TM // CHUNK
    nP = BW // LANES

    def body(dh2_ref, h1_ref, q_ref, g2_ref, o_ref, k4_ref, v4_ref, kt_ref, sink_ref, wbout_ref, wbin_ref, gb_ref,
             rc_ref, rs1_ref, rs2_ref,
             dh1_ref, dz2_ref, n2_ref, y2_ref, dk_ref, dv_ref, dbq_ref, dgb_ref, dsink_ref, do_scr, dq_scr, dsacc_scr):
        i = pl.program_id(0)

        @pl.when(i == 0)
        def _():
            dk_ref[...] = jnp.zeros_like(dk_ref)
            dv_ref[...] = jnp.zeros_like(dv_ref)
            dbq_ref[...] = jnp.zeros_like(dbq_ref)
            dgb_ref[...] = jnp.zeros_like(dgb_ref)
            dsacc_scr[...] = jnp.zeros_like(dsacc_scr)

        dh2 = dh2_ref[...]
        dy2 = _dot_nt(dh2.astype(BF), wbout_ref[...])
        g2v = g2_ref[...].astype(F32)
        ov = o_ref[...].astype(F32)
        silu, dsilu = _silu_parts(g2v)
        y2_ref[...] = (ov * silu).astype(BF).T
        do_scr[...] = (dy2 * silu).astype(BF)
        dz2_ref[:, BW:] = (dy2 * ov * dsilu).astype(BF)
        upper = _upper()
        lo = _lane_lo((2 * CHUNK, LANES))
        for c in range(nC):
            ci = i * nC + c
            rows = slice(c * CHUNK, (c + 1) * CHUNK)
            pci = jnp.maximum(ci - 1, 0)
            prev = pl.multiple_of(pci * CHUNK, CHUNK)
            cur = pl.multiple_of(ci * CHUNK, CHUNK)
            qc = q_ref[rows, :]
            doc = do_scr[rows, :]
            dkb = jnp.zeros((2 * CHUNK, LANES), F32)
            dvb = jnp.zeros((2 * CHUNK, LANES), F32)
            for h in range(2):
                qs = _stack_pairs(qc, h)
                dos = _stack_pairs(doc, h)
                fa, fb = _fold(_dot_nt(_band_rows(k4_ref, prev, cur, h), qs), upper, ci > 0)
                dfa, dfb = _fold(_dot_nt(_band_rows(v4_ref, prev, cur, h), dos), upper)
                folded = []
                for k, (f, df) in enumerate(((fa, dfa), (fb, dfb))):
                    p, ps = _softmax_sink(f, sink_ref[2 * h + k:2 * h + k + 1, :])
                    delta = jnp.sum(p * df, axis=0, keepdims=True)
                    dsacc_scr[2 * h + k:2 * h + k + 1, :] -= ps * delta
                    folded.append((p, p * (df - delta)))
                pt = _unfold(folded[0][0], folded[1][0], upper).astype(BF)
                dst = _unfold(folded[0][1], folded[1][1], upper).astype(BF)
                dqt = _dot(_band_cols(kt_ref, pci, ci, h), dst)
                for j in range(4):
                    dq_scr[rows, (h * 4 + j) * LANES:(h * 4 + j + 1) * LANES] = dqt[:, j * CHUNK:(j + 1) * CHUNK].T
                for acc_name, g in (("k", _dot(dst, qs)), ("v", _dot(pt, dos))):
                    a, b = g[:2 * CHUNK], g[2 * CHUNK:]
                    if h == 0:
                        part = jnp.where(lo, a + pltpu.roll(b, HEAD_DIM, 1), 0.0)
                    else:
                        part = jnp.where(lo, 0.0, pltpu.roll(a, HEAD_DIM, 1) + b)
                    if acc_name == "k":
                        dkb += part
                    else:
                        dvb += part
            dk_ref[pl.ds(prev, CHUNK), :] += dkb[:CHUNK]
            dk_ref[pl.ds(cur, CHUNK), :] += dkb[CHUNK:]
            dv_ref[pl.ds(prev, CHUNK), :] += dvb[:CHUNK]
            dv_ref[pl.ds(cur, CHUNK), :] += dvb[CHUNK:]

        @pl.when(i == nT - 1)
        def _():
            lane = lax.broadcasted_iota(jnp.int32, dsink_ref.shape, 1)
            tot = jnp.zeros(dsink_ref.shape, F32)
            for j in range(4):
                tot += jnp.where(lane == j, jnp.sum(dsacc_scr[:, j * CHUNK:(j + 1) * CHUNK], axis=1, keepdims=True), 0.0)
            dsink_ref[...] = tot
        c_t, s1_t, s2_t = rc_ref[...], rs1_ref[...], rs2_ref[...]
        for p in range(nP):
            cols = slice(p * LANES, (p + 1) * LANES)
            dqp = _rot_bwd(dq_scr[:, cols] * (HEAD_DIM ** -0.5), c_t, s1_t, s2_t)
            dbq_ref[:, cols] += jnp.sum(dqp, axis=0, keepdims=True)
            dz2_ref[:, cols] = dqp.astype(BF)
        h1v = h1_ref[...]
        r2 = lax.rsqrt(jnp.mean(h1v * h1v, axis=-1, keepdims=True) + EPS)
        xh = h1v * r2
        gb = gb_ref[...]
        n2_ref[...] = (xh * gb).astype(BF).T
        dn2 = None
        for j in range(N_DEV):
            part = _dot_nt(dz2_ref[:, j * SH:(j + 1) * SH], wbin_ref[j])
            dn2 = part if dn2 is None else dn2 + part
        dgb_ref[...] += jnp.sum(dn2 * xh, axis=0, keepdims=True)
        dh1_ref[...] = dh2 + _rms_bwd(dn2, xh, r2, gb)

    row = functools.partial(_row_spec, TM)
    S = jax.ShapeDtypeStruct
    return pl.pallas_call(
        body, name="b_bwd", grid=(T // TM,),
        in_specs=[row(D), row(D), row(BW), row(BW), row(BW), _const_spec(k4.shape), _const_spec(v4.shape),
                  _const_spec(kt.shape), _const_spec(sinks.shape), _const_spec(wb_out.shape), _const_spec(wb_in.shape),
                  _const_spec((1, D)), row(LANES), row(LANES), row(LANES)],
        out_specs=[row(D), row(2 * BW), _col_spec(TM, D), _col_spec(TM, BW), _acc_spec((T, LANES)),
                   _acc_spec((T, LANES)), _acc_spec((1, BW)), _acc_spec((1, D)), _acc_spec((4, LANES))],
        out_shape=(S((T, D), F32), S((T, 2 * BW), BF), S((D, T), BF), S((BW, T), BF), S((T, LANES), F32),
                   S((T, LANES), F32), S((1, BW), F32), S((1, D), F32), S((4, LANES), F32)),
        scratch_shapes=[pltpu.VMEM((TM, BW), BF), pltpu.VMEM((TM, BW), F32), pltpu.VMEM((4, 4 * CHUNK), F32)],
        compiler_params=_params(("arbitrary",)),
    )(dh2, h1, q, g2, o, k4, v4, kt, sinks, wb_out, wb_in, g_b, rc, rs1, rs2)


def _a_bwd(dh1p, dk, dv, h1, g_kv, w_kv, wa_out, ws, ln_g, ln_b, z, sv, vhat, rstd, rc, rs1, rs2, ready):
    T, D = h1.shape
    AW = wa_out.shape[0]
    G = ws.shape[0]
    TM = min(256, T)
    nT = T // TM
    nC = TM // CHUNK
    nr = len(ready)

    def body(dh1p_ref, dk_ref, dv_ref, h1_ref, gkv_ref, wkv_ref, waout_ref, ws_ref, lng_ref,
             lnb_ref, u_ref, gt_ref, sv_ref, vhat_ref, rstd_ref, rc_ref, rs1_ref, rs2_ref, *rest):
        ready_refs, rest = rest[:nr], rest[nr:]
        (dz_ref, y_ref, nkv_ref, dkv_ref, dh1_ref, dh1f_ref, dgkv_ref, dbkv_ref, dlng_ref, dlnb_ref,
         dws_ref, dbs_ref), rest = rest[:12], rest[12:]
        recv_refs, (dsv_scr, dvln_scr, ssem, rsem, lsem) = rest[:nr], rest[nr:]
        i = pl.program_id(0)
        exchanges = [_Direct(ready_refs[k], recv_refs[k], ssem.at[k], rsem.at[k], lsem.at[k], scatter=True)
                     for k in range(nr)]

        @pl.when(i == 0)
        def _():
            for e in exchanges:
                e.start()
            for r in (dgkv_ref, dbkv_ref, dlng_ref, dlnb_ref, dws_ref, dbs_ref):
                r[...] = jnp.zeros_like(r)

        dk_pre = _rot_bwd(dk_ref[...], rc_ref[...], rs1_ref[...], rs2_ref[...])
        dkv = jnp.concatenate([dk_pre, dv_ref[...]], axis=1)
        dbkv_ref[...] += jnp.sum(dkv, axis=0, keepdims=True)
        dkv_b = dkv.astype(BF)
        dkv_ref[...] = dkv_b
        h1v = h1_ref[...]
        rkv = lax.rsqrt(jnp.mean(h1v * h1v, axis=-1, keepdims=True) + EPS)
        xh_kv = h1v * rkv
        gkv = gkv_ref[...]
        nkv_ref[...] = (xh_kv * gkv).astype(BF).T
        dnkv = _dot_nt(dkv_b, wkv_ref[...])
        dgkv_ref[...] += jnp.sum(dnkv * xh_kv, axis=0, keepdims=True)
        dh1 = dh1p_ref[...] + _rms_bwd(dnkv, xh_kv, rkv, gkv)
        dh1_b = dh1.astype(BF)
        dh1_ref[...] = dh1_b
        dh1f_ref[...] = dh1
        dy = _dot_nt(dh1_b, waout_ref[...])
        uv = u_ref[...].astype(F32)
        gtv = gt_ref[...].astype(F32)
        svv = sv_ref[...].astype(F32)
        silu, dsilu = _silu_parts(gtv)
        us = uv * silu
        y_ref[...] = (us * svv).astype(BF).T
        dz_ref[:, :AW] = (dy * svv * silu).astype(BF)
        dz_ref[:, 2 * AW:] = (dy * uv * svv * dsilu).astype(BF)
        dsv_scr[...] = (dy * us).astype(BF)
        vhat_v = vhat_ref[...].astype(F32)
        lng = lng_ref[...]
        vln_b = (vhat_v * lng + lnb_ref[...]).astype(BF)
        tri = lax.broadcasted_iota(jnp.int32, (CHUNK, CHUNK), 0) >= lax.broadcasted_iota(jnp.int32, (CHUNK, CHUNK), 1)
        lane = lax.broadcasted_iota(jnp.int32, (CHUNK, LANES), 1)
        dbs = jnp.zeros((CHUNK, LANES), F32)
        for g in range(G):
            wsm = jnp.where(tri, ws_ref[g], 0.0).astype(BF)
            cols = slice(g * CHUNK, (g + 1) * CHUNK)
            dws_g = None
            for c in range(nC):
                rows = slice(c * CHUNK, (c + 1) * CHUNK)
                dsv_cg = dsv_scr[rows, cols]
                dvln_scr[rows, cols] = _dot_tn(wsm, dsv_cg)
                part = _dot_nt(dsv_cg, vln_b[rows, cols])
                dws_g = part if dws_g is None else dws_g + part
                dbs += jnp.where(lane == g, jnp.sum(dsv_cg.astype(F32), axis=-1, keepdims=True), 0.0)
            dws_ref[g] += jnp.where(tri, dws_g, 0.0)
        dbs_ref[...] += dbs
        dvln = dvln_scr[...]
        dlng_ref[...] += jnp.sum(dvln * vhat_v, axis=0, keepdims=True)
        dlnb_ref[...] += jnp.sum(dvln, axis=0, keepdims=True)
        a = dvln * lng
        dvv = rstd_ref[:, 0:1] * (a - jnp.mean(a, axis=-1, keepdims=True)
                                  - vhat_v * jnp.mean(a * vhat_v, axis=-1, keepdims=True))
        dz_ref[:, AW:2 * AW] = dvv.astype(BF)

        @pl.when(i == nT - 1)
        def _():
            for e in exchanges:
                e.finish()

    row = functools.partial(_row_spec, TM)
    col = functools.partial(_col_spec, TM)
    hbm = pl.BlockSpec(memory_space=pl.ANY)
    S = jax.ShapeDtypeStruct
    return pl.pallas_call(
        body, name="a_bwd", grid=(nT,),
        in_specs=[row(D), row(LANES), row(LANES), row(D), _const_spec((1, D)), _const_spec(w_kv.shape),
                  _const_spec(wa_out.shape), _const_spec(ws.shape),
                  _const_spec((1, AW)), _const_spec((1, AW)), pl.BlockSpec((TM, AW), lambda i: (i, 0)),
                  pl.BlockSpec((TM, AW), lambda i: (i, 2)), row(AW), row(AW), row(LANES),
                  row(LANES), row(LANES), row(LANES)] + [hbm] * nr,
        out_specs=[row(3 * AW), col(AW), col(D), row(2 * LANES), row(D), row(D),
                   _acc_spec((1, D)), _acc_spec((1, 2 * LANES)), _acc_spec((1, AW)),
                   _acc_spec((1, AW)), _acc_spec(ws.shape), _acc_spec((CHUNK, LANES))] + [hbm] * nr,
        out_shape=(S((T, 3 * AW), BF), S((AW, T), BF), S((D, T), BF), S((T, 2 * LANES), BF), S((T, D), BF),
                   S((T, D), F32),
                   S((1, D), F32), S((1, 2 * LANES), F32), S((1, AW), F32), S((1, AW), F32),
                   S(ws.shape, F32), S((CHUNK, LANES), F32)) + tuple(S(r.shape, r.dtype) for r in ready),
        scratch_shapes=[pltpu.VMEM((TM, AW), BF), pltpu.VMEM((TM, AW), F32)] + _direct_sems(nr),
        compiler_params=_params(("arbitrary",)),
    )(dh1p, dk, dv, h1, g_kv, w_kv, wa_out, ws, ln_g, ln_b, z, z, sv, vhat, rstd, rc, rs1, rs2, *ready)


def _a_in_bwd(dz, wa_in, x, dh1, g_a, ready):
    T, D = x.shape
    SH = wa_in.shape[2]
    TM = min(512, T)
    nT = T // TM
    nr = len(ready)

    def body(dz_ref, wain_ref, x_ref, dh1_ref, ga_ref, *rest):
        ready_refs, (dx_ref, n1_ref, dga_ref), rest = rest[:nr], rest[nr:nr + 3], rest[nr + 3:]
        recv_refs, (ssem, rsem, lsem) = rest[:nr], rest[nr:]
        i = pl.program_id(0)
        exchanges = [_Direct(ready_refs[k], recv_refs[k], ssem.at[k], rsem.at[k], lsem.at[k], scatter=True)
                     for k in range(nr)]

        @pl.when(i == 0)
        def _():
            for e in exchanges:
                e.start()
            dga_ref[...] = jnp.zeros_like(dga_ref)

        xv = x_ref[...]
        r1 = lax.rsqrt(jnp.mean(xv * xv, axis=-1, keepdims=True) + EPS)
        xh = xv * r1
        ga = ga_ref[...]
        n1_ref[...] = (xh * ga).astype(BF).T
        dn1 = None
        for j in range(N_DEV):
            part = _dot_nt(dz_ref[:, j * SH:(j + 1) * SH], wain_ref[j])
            dn1 = part if dn1 is None else dn1 + part
        dga_ref[...] += jnp.sum(dn1 * xh, axis=0, keepdims=True)
        dx_ref[...] = dh1_ref[...] + _rms_bwd(dn1, xh, r1, ga)

        @pl.when(i == nT - 1)
        def _():
            for e in exchanges:
                e.finish()

    row = functools.partial(_row_spec, TM)
    hbm = pl.BlockSpec(memory_space=pl.ANY)
    S = jax.ShapeDtypeStruct
    return pl.pallas_call(
        body, name="a_in_bwd", grid=(nT,),
        in_specs=[row(dz.shape[1]), _const_spec(wa_in.shape), row(D), row(D), _const_spec((1, D))] + [hbm] * nr,
        out_specs=[row(D), _col_spec(TM, D), _acc_spec((1, D))] + [hbm] * nr,
        out_shape=(S((T, D), F32), S((D, T), BF), S((1, D), F32)) + tuple(S(r.shape, r.dtype) for r in ready),
        scratch_shapes=_direct_sems(nr),
        compiler_params=_params(("arbitrary",)),
    )(dz, wa_in, x, dh1, g_a, *ready)


def _wgrad(at, b, nblk, name, bt=512):
    K, T = at.shape
    N = b.shape[1] // nblk
    BT = min(bt, T)
    nt = T // BT

    def body(a_ref, b_ref, o_ref, acc):
        t = pl.program_id(1)

        @pl.when(t == 0)
        def _():
            acc[...] = jnp.zeros_like(acc)

        acc[...] += _dot(a_ref[...], b_ref[...])

        @pl.when(t == nt - 1)
        def _():
            o_ref[0] = acc[...].astype(BF)

    return pl.pallas_call(
        body, name=name, grid=(nblk, nt),
        in_specs=[pl.BlockSpec((K, BT), lambda j, t: (0, t)), pl.BlockSpec((BT, N), lambda j, t: (t, j))],
        out_specs=pl.BlockSpec((1, K, N), lambda j, t: (j, 0, 0)),
        out_shape=jax.ShapeDtypeStruct((nblk, K, N), BF),
        scratch_shapes=[pltpu.VMEM((K, N), F32)],
        compiler_params=_params(("arbitrary", "arbitrary")),
    )(at, b)


def _wgrad_exchange(a, b, me, extras, name):
    K, T = a.shape
    N = b.shape[1] // N_DEV
    BT = min(1024, T)
    nt = T // BT
    ne = len(extras)
    last = N_DEV - 1
    n_chip = N_DEV // 2

    def body(me_ref, a_ref, b_ref, *rest):
        ex_in, recv_ref, ex_out = rest[:ne], rest[ne], rest[ne + 1:2 * ne + 1]
        acc, dstage, istage, half, d_s, d_r, i_s, i_r, lsem, ex_ssem, ex_rsem, ex_lsem = rest[2 * ne + 1:]
        s, t = pl.program_id(0), pl.program_id(1)
        x, y, c = (lax.axis_index(ax) for ax in AXES)
        ex = [_Direct(ex_in[k], ex_out[k], ex_ssem.at[k], ex_rsem.at[k], ex_lsem.at[k], scatter=True) for k in range(ne)]

        def to_sibling(k, slot):
            return pltpu.make_async_remote_copy(src_ref=dstage.at[slot], dst_ref=half.at[k], send_sem=d_s.at[k],
                                                recv_sem=d_r.at[k], device_id=(x, y, 1 - c), device_id_type=MESH)

        def to_chip(k, slot, sender):
            far = n_chip - 1 - k
            px, py = x ^ ((far >> 1) & 1), y ^ (far & 1)
            dst = recv_ref.at[2 * x + y] if sender else recv_ref.at[2 * px + py]
            return pltpu.make_async_remote_copy(src_ref=istage.at[slot], dst_ref=dst, send_sem=i_s.at[k],
                                                recv_sem=i_r.at[k], device_id=(px, py, c), device_id_type=MESH)

        @pl.when((s == 0) & (t == 0))
        def _():
            for e in ex:
                e.start()

        @pl.when(t == 0)
        def _():
            acc[...] = jnp.zeros_like(acc)

        acc[...] += _dot(a_ref[...], b_ref[...])

        @pl.when(t == nt - 1)
        def _():
            k = lax.div(s, 2)
            slot = lax.rem(k, 2)

            @pl.when(lax.rem(s, 2) == 0)
            def _():
                @pl.when(k >= 2)
                def _():
                    to_sibling(k - 2, slot).wait_send()

                dstage[slot] = acc[...].astype(BF)
                to_sibling(k, slot).start()

            @pl.when(lax.rem(s, 2) == 1)
            def _():
                to_sibling(k, slot).wait_recv()

                @pl.when(k >= 2)
                def _():
                    to_chip(k - 2, slot, True).wait_send()

                istage[slot] = (acc[...] + half[k].astype(F32)).astype(BF)

                @pl.when(k < n_chip - 1)
                def _():
                    to_chip(k, slot, True).start()

            @pl.when(s == last)
            def _():
                own = pltpu.make_async_copy(istage.at[slot], recv_ref.at[2 * x + y], lsem)
                own.start()
                to_chip(n_chip - 2, 0, True).wait_send()
                to_sibling(n_chip - 2, 0).wait_send()
                to_sibling(n_chip - 1, 1).wait_send()
                for kk in range(n_chip - 1):
                    to_chip(kk, 0, False).wait_recv()
                own.wait()
                for e in ex:
                    e.finish()

    hbm = pl.BlockSpec(memory_space=pl.ANY)
    dma = pltpu.SemaphoreType.DMA
    grid_spec = pltpu.PrefetchScalarGridSpec(
        num_scalar_prefetch=1, grid=(N_DEV, nt),
        in_specs=[pl.BlockSpec((K, BT), lambda s, t, me_ref: (0, t)),
                  pl.BlockSpec((BT, N), lambda s, t, me_ref: (t, me_ref[0] ^ (last - s)))] + [hbm] * ne,
        out_specs=[hbm] * (ne + 1),
        scratch_shapes=[pltpu.VMEM((K, N), F32), pltpu.VMEM((2, K, N), BF), pltpu.VMEM((2, K, N), BF),
                        pltpu.VMEM((n_chip, K, N), BF), dma((n_chip,)), dma((n_chip,)), dma((n_chip - 1,)),
                        dma((n_chip - 1,)), dma] + _direct_sems(ne))
    return pl.pallas_call(
        body, name=name, grid_spec=grid_spec,
        out_shape=[jax.ShapeDtypeStruct((n_chip, K, N), BF)] + [jax.ShapeDtypeStruct(e.shape, e.dtype) for e in extras],
        compiler_params=_params(("arbitrary", "arbitrary")),
    )(me, a, b, *extras)


def _my_index():
    return 4 * lax.axis_index("x") + 2 * lax.axis_index("y") + lax.axis_index("c")


def _all_gather(arrs, dtypes, name):
    n = len(arrs)

    def body(*refs):
        ins, outs = refs[:n], refs[n:2 * n]
        stages = refs[2 * n:3 * n]
        send_sems, recv_sems, local_sems = refs[3 * n:]
        gathers = [_TwoLevel(stages[a], outs[a], send_sems.at[a], recv_sems.at[a], local_sems.at[a]) for a in range(n)]
        for a in range(n):
            stages[a][...] = ins[a][...].astype(stages[a].dtype)
            gathers[a].start()
        for g in gathers:
            g.forward()
        for g in gathers:
            g.finish()

    vm = pl.BlockSpec(memory_space=pltpu.VMEM)
    hbm = pl.BlockSpec(memory_space=pl.ANY)
    return pl.pallas_call(
        body, name=name,
        in_specs=[vm] * n, out_specs=[hbm] * n,
        out_shape=[jax.ShapeDtypeStruct((N_DEV,) + a.shape, dt) for a, dt in zip(arrs, dtypes)],
        scratch_shapes=[pltpu.VMEM(a.shape, dt) for a, dt in zip(arrs, dtypes)]
        + [pltpu.SemaphoreType.DMA((n, 7)), pltpu.SemaphoreType.DMA((n, 7)), pltpu.SemaphoreType.DMA((n,))],
        compiler_params=pltpu.CompilerParams(vmem_limit_bytes=VMEM_LIMIT),
    )(*arrs)


def _peer(mask):
    x, y, c = (lax.axis_index(a) for a in AXES)
    return (x ^ ((mask >> 2) & 1), y ^ ((mask >> 1) & 1), c ^ (mask & 1))


def _dev_index(p):
    return 4 * p[0] + 2 * p[1] + p[2]


class _Direct:
    def __init__(self, src, dst, send_sems, recv_sems, local_sem, scatter):
        me = _my_index()
        self.own = pltpu.make_async_copy(src.at[me] if scatter else src, dst.at[me], local_sem)
        self.sends, self.recvs = [], []
        for k in range(1, N_DEV):
            p = _peer(k)
            pi = _dev_index(p)
            sems = dict(send_sem=send_sems.at[k - 1], recv_sem=recv_sems.at[k - 1], device_id=p, device_id_type=MESH)
            self.sends.append(pltpu.make_async_remote_copy(src_ref=src.at[pi] if scatter else src, dst_ref=dst.at[me],
                                                           **sems))
            self.recvs.append(pltpu.make_async_remote_copy(src_ref=src.at[me] if scatter else src, dst_ref=dst.at[pi],
                                                           **sems))

    def start(self):
        self.own.start()
        for cp in self.sends:
            cp.start()

    def finish(self):
        for cp in self.sends:
            cp.wait_send()
        for cp in self.recvs:
            cp.wait_recv()
        self.own.wait()


class _TwoLevel:
    def __init__(self, src, dst, send_sems, recv_sems, local_sem, own=True):
        x, y, c = (lax.axis_index(a) for a in AXES)
        self.me, self.sibling = (x, y, c), (x, y, 1 - c)
        self.chips = [(1 - x, y), (x, 1 - y), (1 - x, 1 - y)]
        self.src, self.dst, self.send_sems, self.recv_sems = src, dst, send_sems, recv_sems
        self.own = pltpu.make_async_copy(src, dst.at[_dev_index(self.me)], local_sem) if own else None

    def _copy(self, k, block, to, from_src=False):
        slot = self.dst.at[_dev_index(block)]
        return pltpu.make_async_remote_copy(src_ref=self.src if from_src else slot, dst_ref=slot,
                                            send_sem=self.send_sems.at[k], recv_sem=self.recv_sems.at[k],
                                            device_id=to, device_id_type=MESH)

    def _firsts(self):
        c = self.me[2]
        return [self._copy(0, self.me, self.sibling, True)] + [self._copy(1 + j, self.me, (*chip, c), True)
                                                               for j, chip in enumerate(self.chips)]

    def _passed(self):
        c = self.me[2]
        return [self._copy(4 + j, (*chip, c), self.sibling) for j, chip in enumerate(self.chips)]

    def start(self):
        if self.own is not None:
            self.own.start()
        for cp in self._firsts():
            cp.start()

    def wait_sibling(self):
        self._copy(0, self.sibling, self.me).wait_recv()

    def wait_chip_and_forward(self, j):
        self._copy(1 + j, (*self.chips[j], self.me[2]), self.me).wait_recv()
        self._passed()[j].start()

    def wait_passed(self, j):
        self._copy(4 + j, (*self.chips[j], 1 - self.me[2]), self.me).wait_recv()

    def wait_sends(self):
        for cp in self._firsts() + self._passed():
            cp.wait_send()
        if self.own is not None:
            self.own.wait()

    def forward(self):
        for j in range(3):
            self.wait_chip_and_forward(j)

    def finish(self):
        self.wait_sibling()
        for j in range(3):
            self.wait_passed(j)
        self.wait_sends()


def _direct_sems(n):
    return [pltpu.SemaphoreType.DMA((n, 7)), pltpu.SemaphoreType.DMA((n, 7)), pltpu.SemaphoreType.DMA((n,))]


def _adam_math(w, g, m, v):
    m = ADAM_B1 * m + (1.0 - ADAM_B1) * g
    v = ADAM_B2 * v + (1.0 - ADAM_B2) * (g * g)
    m_hat = m / (1.0 - ADAM_B1 ** ADAM_STEP)
    v_hat = v / (1.0 - ADAM_B2 ** ADAM_STEP)
    delta = -ADAM_LR * (m_hat / (jnp.sqrt(v_hat) + ADAM_EPS) + ADAM_WD * w)
    return delta, m, v


def _sum_adam(parts, w, m, v, name):
    R, C = w.shape
    NP = parts.shape[0]
    BR = CHUNK if R % CHUNK == 0 else R

    def body(p_ref, w_ref, m_ref, v_ref, g_ref, d_ref, nm_ref, nv_ref):
        g = p_ref[0].astype(F32)
        for i in range(1, NP):
            g = g + p_ref[i].astype(F32)
        g_ref[...] = g
        d_ref[...], nm_ref[...], nv_ref[...] = _adam_math(w_ref[...], g, m_ref[...], v_ref[...])

    blk = pl.BlockSpec((BR, C), lambda i: (i, 0))
    S = jax.ShapeDtypeStruct((R, C), F32)
    return pl.pallas_call(
        body, name=name, grid=(R // BR,),
        in_specs=[pl.BlockSpec((NP, BR, C), lambda i: (0, i, 0)), blk, blk, blk],
        out_specs=[blk] * 4, out_shape=(S,) * 4,
        compiler_params=_params(("arbitrary",)),
    )(parts, w, m, v)


def _sum8(parts, name):
    _, R, C = parts.shape

    def body(p_ref, o_ref):
        g = p_ref[0]
        for i in range(1, N_DEV):
            g = g + p_ref[i]
        o_ref[...] = g

    return pl.pallas_call(body, name=name, out_shape=jax.ShapeDtypeStruct((R, C), F32))(parts)


def _adam_only(g, w, m, v, name):
    def body(g_ref, w_ref, m_ref, v_ref, d_ref, nm_ref, nv_ref):
        d_ref[...], nm_ref[...], nv_ref[...] = _adam_math(w_ref[...], g_ref[...], m_ref[...], v_ref[...])

    S = jax.ShapeDtypeStruct(w.shape, F32)
    return pl.pallas_call(body, name=name, out_shape=(S,) * 3)(g, w, m, v)


def _rope_tables(T):
    pos = np.arange(T, dtype=np.float32)
    inv_freq = (np.float64(ROPE_THETA) ** (-np.arange(0, HEAD_DIM, 2, dtype=np.float64) / HEAD_DIM)).astype(np.float32)
    ang = (pos[:, None] * inv_freq[None, :]).astype(np.float64)
    cos, sin, zero = np.cos(ang).astype(np.float32), np.sin(ang).astype(np.float32), np.zeros(ang.shape, np.float32)
    c = np.concatenate([cos, cos, cos, cos], axis=1)
    s1 = np.concatenate([-sin, zero, -sin, zero], axis=1)
    s2 = np.concatenate([zero, sin, zero, sin], axis=1)
    return jnp.asarray(c), jnp.asarray(s1), jnp.asarray(s2)


SUBLANES = 8


def _nrows(size):
    return -(-size // (SUBLANES * LANES)) * SUBLANES


def _rows(a):
    flat = a.reshape(-1)
    pad = _nrows(flat.shape[0]) * LANES - flat.shape[0]
    if pad:
        flat = jnp.concatenate([flat, jnp.zeros((pad,), flat.dtype)])
    return flat.reshape(-1, LANES)


def _pack(arrs, total_rows):
    rows = [_rows(a) for a in arrs]
    used = sum(r.shape[0] for r in rows)
    if total_rows > used:
        rows.append(jnp.zeros((total_rows - used, LANES), F32))
    return jnp.concatenate(rows, axis=0)


def _unpack(packed, shapes):
    out, at = [], 0
    for shp in shapes:
        size = math.prod(shp)
        nrow = _nrows(size)
        out.append(packed[at:at + nrow].reshape(-1)[:size].reshape(shp))
        at += nrow
    return out


def kernel(x, a_norm_g, a_w_in, a_ln_g, a_ln_b, a_ws, a_bs, a_w_out, kv_norm_g, w_kv, b_kv, b_norm_g, b_w_in, b_bq, b_sinks, b_w_out, final_norm_g, loss_target, m_a_norm_g, m_a_w_in, m_a_ln_g, m_a_ln_b, m_a_ws, m_a_bs, m_a_w_out, m_kv_norm_g, m_w_kv, m_b_kv, m_b_norm_g, m_b_w_in, m_b_bq, m_b_sinks, m_b_w_out, m_final_norm_g, v_a_norm_g, v_a_w_in, v_a_ln_g, v_a_ln_b, v_a_ws, v_a_bs, v_a_w_out, v_kv_norm_g, v_w_kv, v_b_kv, v_b_norm_g, v_b_w_in, v_b_bq, v_b_sinks, v_b_w_out, v_final_norm_g):
    T, D = x.shape[1], x.shape[2]
    AW = a_ln_g.shape[1] * N_DEV
    G = a_ws.shape[1]
    assert w_kv.shape[1] == 2 * LANES and a_ws.shape[2] == CHUNK and T % CHUNK == 0
    me = _my_index()

    xs, tgt = x[0], loss_target[0]
    vec = jnp.concatenate([a_norm_g, a_ln_g, a_ln_b], axis=1)
    vec = jnp.broadcast_to(vec, (SUBLANES, vec.shape[1]))
    slots = me ^ jnp.array(PASS_MASKS, jnp.int32)
    z, wa_in, vecs, wa_out, wkv = _in_proj(xs, a_w_in[0], vec, slots, [a_w_out[0], w_kv])
    wa_out = wa_out.reshape(AW, D)
    wkv = wkv.reshape(D, 2 * LANES)
    vecs = vecs[:, 0, :]
    ds = D // N_DEV
    g_a = vecs[:, :ds].reshape(1, D)
    ln_g = vecs[:, ds:ds + AW // N_DEV].reshape(1, AW)
    ln_b = vecs[:, ds + AW // N_DEV:].reshape(1, AW)

    rc, rs1, rs2 = _rope_tables(T)
    ws = a_ws[0]
    bs_t = a_bs[0].T
    g_kv = kv_norm_g.reshape(1, D)
    bkv = b_kv.reshape(1, -1)
    g_f = final_norm_g.reshape(1, D)
    sinks = jnp.repeat(b_sinks.reshape(2, 4, 2).transpose(0, 2, 1).reshape(4, 4), CHUNK, axis=1)
    h1, sv, vhat, rstd, k4, v4, kt, vt, wb_in, wb_out = _a_fwd(
        xs, z, ln_g, ln_b, ws, bs_t, wa_out, g_kv, wkv, bkv, rc, rs1, rs2, [b_w_in[0], b_w_out[0]])
    wb_out = wb_out.reshape(-1, D)
    q, g2, o, dh2, dh2_b, loss, d_gf = _b_fwd(h1, b_norm_g, wb_in, b_bq, rc, rs1, rs2, k4, vt, sinks, wb_out, g_f, tgt)
    dh1p, dz2, n2, y2, dk, dv, d_bq, d_gb, d_sink = _b_bwd(dh2, h1, q, g2, o, k4, v4, kt, sinks, wb_out, wb_in,
                                                           b_norm_g, rc, rs1, rs2)
    d_sink = d_sink[:, :4].reshape(2, 2, 4).transpose(0, 2, 1).reshape(1, 16)
    gw_b_in = _wgrad(n2, dz2, N_DEV, "wgrad_b_in", bt=2048)
    gw_b_out = _wgrad(y2, dh2_b, 1, "wgrad_b_out").reshape(N_DEV, -1, D)
    (dz, y, nkv, dkv, dh1, dh1_f, d_gkv, d_bkv, d_lng, d_lnb, d_ws, d_bst, r_b_in, r_b_out) = _a_bwd(
        dh1p, dk, dv, h1, g_kv, wkv, wa_out, ws, ln_g, ln_b, z, sv, vhat, rstd, rc, rs1, rs2, [gw_b_in, gw_b_out])
    gw_a_out = _wgrad(y, dh1, 1, "wgrad_a_out").reshape(N_DEV, AW // N_DEV, D)
    gw_kv = _wgrad(nkv, dkv, 1, "wgrad_kv").reshape(N_DEV, D // N_DEV, 2 * LANES)
    dx, n1, d_ga, r_a_out, r_kv = _a_in_bwd(dz, wa_in, xs, dh1_f, g_a, [gw_a_out, gw_kv])
    small = [d_ws, d_bst[:, :G].T, d_gkv, d_bkv, d_gb, d_bq, d_sink, d_gf, d_ga, d_lng, d_lnb, loss]
    used = sum(_nrows(a.size) for a in small)
    per = -(-used // (SUBLANES * N_DEV)) * SUBLANES
    small_pack = _pack(small, per * N_DEV).reshape(N_DEV, per, LANES)
    r_a_in, r_small = _wgrad_exchange(n1, dz, me.reshape(1), [small_pack], "wgrad_a_in")

    g_a_in, d_a_in, nm_a_in, nv_a_in = _sum_adam(r_a_in, a_w_in[0], m_a_w_in[0], v_a_w_in[0], "adam_a_in")
    g_a_out, d_a_out, nm_a_out, nv_a_out = _sum_adam(r_a_out, a_w_out[0], m_a_w_out[0], v_a_w_out[0], "adam_a_out")
    g_kvw, d_kvw, nm_kvw, nv_kvw = _sum_adam(r_kv, w_kv, m_w_kv, v_w_kv, "adam_kv")
    g_b_in, d_b_in, nm_b_in, nv_b_in = _sum_adam(r_b_in, b_w_in[0], m_b_w_in[0], v_b_w_in[0], "adam_b_in")
    g_b_out, d_b_out, nm_b_out, nv_b_out = _sum_adam(r_b_out, b_w_out[0], m_b_w_out[0], v_b_w_out[0], "adam_b_out")

    red = _sum8(r_small, "sum_small")
    (full_small,) = _all_gather([red], [F32], "gather_small")
    full_small = full_small.reshape(N_DEV * per, LANES)
    rep_shapes = [a_ws.shape, a_bs.shape, kv_norm_g.shape, b_kv.shape, b_norm_g.shape, b_bq.shape, b_sinks.shape,
                  final_norm_g.shape]
    gs = _unpack(full_small, rep_shapes + [(N_DEV, a_norm_g.shape[1]), (N_DEV, a_ln_g.shape[1]),
                                           (N_DEV, a_ln_b.shape[1]), (1, 1)])
    loss = gs.pop()[0, 0]
    g_ang = lax.dynamic_slice_in_dim(gs[8], me, 1, axis=0)
    g_alng = lax.dynamic_slice_in_dim(gs[9], me, 1, axis=0)
    g_alnb = lax.dynamic_slice_in_dim(gs[10], me, 1, axis=0)
    sm_g = gs[:8] + [g_ang, g_alng, g_alnb]
    sm_shapes = [a.shape for a in sm_g]
    tot = sum(_nrows(a.size) for a in sm_g)
    pw = _pack([a_ws, a_bs, kv_norm_g, b_kv, b_norm_g, b_bq, b_sinks, final_norm_g, a_norm_g, a_ln_g, a_ln_b], tot)
    pm = _pack([m_a_ws, m_a_bs, m_kv_norm_g, m_b_kv, m_b_norm_g, m_b_bq, m_b_sinks, m_final_norm_g, m_a_norm_g,
                m_a_ln_g, m_a_ln_b], tot)
    pv = _pack([v_a_ws, v_a_bs, v_kv_norm_g, v_b_kv, v_b_norm_g, v_b_bq, v_b_sinks, v_final_norm_g, v_a_norm_g,
                v_a_ln_g, v_a_ln_b], tot)
    pg = _pack(sm_g, tot)
    pd, pnm, pnv = _adam_only(pg, pw, pm, pv, "adam_small")
    sd, snm, snv = _unpack(pd, sm_shapes), _unpack(pnm, sm_shapes), _unpack(pnv, sm_shapes)

    def order(big, sm):
        a_in, a_out, kvw, b_in, b_out = big
        ws_, bs_, kvg, bkv_, bng, bq_, snk, fng, ang, alng, alnb = sm
        return (ang, a_in[None], alng, alnb, ws_, bs_, a_out[None], kvg, kvw, bkv_, bng, b_in[None], bq_, snk,
                b_out[None], fng)

    grads = order((g_a_in, g_a_out, g_kvw, g_b_in, g_b_out), sm_g)
    deltas = order((d_a_in, d_a_out, d_kvw, d_b_in, d_b_out), sd)
    new_m = order((nm_a_in, nm_a_out, nm_kvw, nm_b_in, nm_b_out), snm)
    new_v = order((nv_a_in, nv_a_out, nv_kvw, nv_b_in, nv_b_out), snv)
    return (loss, dx[None], *grads, *deltas, *new_m, *new_v)
```

```python
import functools
import math

import jax
import jax.numpy as jnp
import numpy as np
from jax import lax
from jax.experimental import pallas as pl
from jax.experimental.pallas import tpu as pltpu

CHUNK = 128
HEAD_DIM = 64
ROPE_THETA = 10000.0
EPS = 1e-5
ADAM_LR = 0.001
ADAM_B1 = 0.9
ADAM_B2 = 0.999
ADAM_EPS = 1e-08
ADAM_WD = 0.01
ADAM_STEP = 10
N_DEV = 8
LANES = 128
NEG = -1e30

BF = jnp.bfloat16
F32 = jnp.float32
MESH = pl.DeviceIdType.MESH
AXES = ("x", "y", "c")
VMEM_LIMIT = 56 * 1024 * 1024


def _dot(a, b):
    return jnp.dot(a, b, preferred_element_type=F32)


def _dot_nt(a, b):
    return lax.dot_general(a, b, (((1,), (1,)), ((), ())), preferred_element_type=F32)


def _dot_tn(a, b):
    return lax.dot_general(a, b, (((0,), (0,)), ((), ())), preferred_element_type=F32)


def _const_spec(shape):
    nd = len(shape)
    return pl.BlockSpec(shape, lambda *_: (0,) * nd, pipeline_mode=pl.Buffered(1))


def _acc_spec(shape):
    nd = len(shape)
    return pl.BlockSpec(shape, lambda *_: (0,) * nd)


def _row_spec(tm, width):
    return pl.BlockSpec((tm, width), lambda i: (i, 0))


def _col_spec(tm, height):
    return pl.BlockSpec((height, tm), lambda i: (0, i))


def _params(sem):
    return pltpu.CompilerParams(dimension_semantics=sem, vmem_limit_bytes=VMEM_LIMIT)


def _rot(x, c, s1, s2):
    return x * c + pltpu.roll(x, 96, 1) * s1 + pltpu.roll(x, 32, 1) * s2


def _rot_bwd(d, c, s1, s2):
    return d * c + pltpu.roll(d * s1, 32, 1) + pltpu.roll(d * s2, 96, 1)


def _silu_parts(g):
    sg = jax.nn.sigmoid(g)
    return g * sg, sg * (1.0 + g * (1.0 - sg))


def _rms_bwd(dn, xh, r, g):
    a = dn * g
    return r * (a - xh * jnp.mean(a * xh, axis=-1, keepdims=True))


def _lane_lo(shape):
    return lax.broadcasted_iota(jnp.int32, shape, 1) < HEAD_DIM


def _split4(t):
    lo = _lane_lo(t.shape)
    tr = pltpu.roll(t, HEAD_DIM, 1)
    z = jnp.zeros_like(t)
    return jnp.concatenate([jnp.where(lo, t, z), jnp.where(lo, z, tr), jnp.where(lo, tr, z), jnp.where(lo, z, t)], axis=1)


def _stack_pairs(t, h):
    return jnp.concatenate([t[:, (h * 4 + j) * LANES:(h * 4 + j + 1) * LANES] for j in range(4)], axis=0)


def _upper():
    shape = (CHUNK, 4 * CHUNK)
    return lax.broadcasted_iota(jnp.int32, shape, 0) > (lax.broadcasted_iota(jnp.int32, shape, 1) & (CHUNK - 1))


def _band_rows(ref, prev, cur, h):
    a = slice(2 * h * LANES, (2 * h + 1) * LANES)
    b = slice((2 * h + 1) * LANES, (2 * h + 2) * LANES)
    return jnp.concatenate([ref[pl.ds(prev, CHUNK), a], ref[pl.ds(cur, CHUNK), a],
                            ref[pl.ds(prev, CHUNK), b], ref[pl.ds(cur, CHUNK), b]], axis=0)


def _band_cols(ref, pci, ci, h):
    a = slice(2 * h * LANES, (2 * h + 1) * LANES)
    b = slice((2 * h + 1) * LANES, (2 * h + 2) * LANES)
    return jnp.concatenate([ref[pci, a, :], ref[ci, a, :], ref[pci, b, :], ref[ci, b, :]], axis=1)


def _fold(t, upper, has_prev=None):
    out = []
    for k in range(2):
        prev = t[2 * k * CHUNK:(2 * k + 1) * CHUNK]
        if has_prev is not None:
            prev = jnp.where(has_prev, prev, NEG)
        out.append(jnp.where(upper, prev, t[(2 * k + 1) * CHUNK:(2 * k + 2) * CHUNK]))
    return out


def _unfold(fa, fb, upper):
    z = jnp.zeros_like(fa)
    return jnp.concatenate([jnp.where(upper, fa, z), jnp.where(upper, z, fa),
                            jnp.where(upper, fb, z), jnp.where(upper, z, fb)], axis=0)


def _softmax_sink(f, sink):
    m = jnp.maximum(jnp.max(f, axis=0, keepdims=True), sink)
    p = jnp.exp(f - m)
    es = jnp.exp(sink - m)
    inv = 1.0 / (jnp.sum(p, axis=0, keepdims=True) + es)
    return p * inv, es * inv


class _Riding:
    def __init__(self, shards, gathered, stages, sems, n_steps):
        self.shards, self.stages, self.n_steps = shards, stages, n_steps
        ssem, rsem, lsem = sems
        self.gathers = [_TwoLevel(stages[k], gathered[k], ssem.at[k], rsem.at[k], lsem.at[k])
                        for k in range(len(shards))]

    def begin(self, i):
        @pl.when(i == 0)
        def _():
            for shard, stage, g in zip(self.shards, self.stages, self.gathers):
                stage[...] = shard[...].astype(stage.dtype)
                g.start()

    def end(self, i):
        @pl.when(i == self.n_steps // 2)
        def _():
            for g in self.gathers:
                g.forward()

        @pl.when(i == self.n_steps - 1)
        def _():
            for g in self.gathers:
                g.finish()

    @staticmethod
    def specs(later):
        nl = len(later)
        hbm = pl.BlockSpec(memory_space=pl.ANY)
        return ([_const_spec(w.shape) for w in later], [hbm] * nl,
                tuple(jax.ShapeDtypeStruct((N_DEV,) + w.shape, BF) for w in later),
                [pltpu.VMEM(w.shape, BF) for w in later] + _direct_sems(nl))


PASS_MASKS = (0, 1, 4, 2, 5, 3, 6, 7)


def _in_proj(x, w_shard, vec_shard, slots, later):
    T, D = x.shape
    SH = w_shard.shape[1]
    VW = vec_shard.shape[1]
    TM = min(512, T)
    nT = T // TM
    nl = len(later)
    ds = D // N_DEV
    last = N_DEV - 1

    def body(slots_ref, x_ref, wsh_ref, vsh_ref, *rest):
        shards, rest = rest[:nl], rest[nl:]
        (z_ref, wout_ref, vout_ref), rest = rest[:3], rest[3:]
        gathered, rest = rest[:nl], rest[nl:]
        (w_scr, vec_scr, vstage, n1_scr, ga_scr, w_s, w_r, w_l, v_s, v_r, v_l), rest = rest[:11], rest[11:]
        stages, sems = rest[:nl], rest[nl:]
        p, i = pl.program_id(0), pl.program_id(1)
        me = _my_index()
        wg = _TwoLevel(w_scr.at[me], w_scr, w_s, w_r, w_l, own=False)
        vg = _TwoLevel(vstage, vec_scr, v_s, v_r, v_l)
        lg = [_TwoLevel(stages[k], gathered[k], sems[0].at[k], sems[1].at[k], sems[2].at[k]) for k in range(nl)]
        w_copy = pltpu.make_async_copy(w_scr, wout_ref, w_l)

        def at_pass(k):
            return (p == k) & (i == 0)

        @pl.when(at_pass(0))
        def _():
            vstage[...] = vsh_ref[...]
            vg.start()
            w_scr[me] = wsh_ref[...].astype(BF)
            wg.start()
            for k in range(nl):
                stages[k][...] = shards[k][...].astype(BF)
                lg[k].start()
            vg.forward()
            vg.finish()
            for j in range(N_DEV):
                ga_scr[:, j * ds:(j + 1) * ds] = vec_scr[j, 0:1, 0:ds]
            vout_ref[...] = vec_scr[...]

        @pl.when(at_pass(1))
        def _():
            wg.wait_sibling()

        for k, j in ((2, 0), (3, 1), (6, 2)):
            @pl.when(at_pass(k))
            def _(j=j):
                wg.wait_chip_and_forward(j)

        for k, j in ((4, 0), (5, 1), (7, 2)):
            @pl.when(at_pass(k))
            def _(j=j):
                wg.wait_passed(j)

        @pl.when(at_pass(last))
        def _():
            w_copy.start()

        @pl.when(p == 0)
        def _():
            xv = x_ref[...]
            r1 = lax.rsqrt(jnp.mean(xv * xv, axis=-1, keepdims=True) + EPS)
            n1_scr[i] = (xv * r1 * ga_scr[...]).astype(BF)

        z_ref[...] = _dot(n1_scr[i], w_scr[slots_ref[p]]).astype(BF)

        @pl.when((p == last) & (i == nT - 1))
        def _():
            wg.wait_sends()
            for g in lg:
                g.forward()
            for g in lg:
                g.finish()
            w_copy.wait()

    hbm = pl.BlockSpec(memory_space=pl.ANY)
    dma = pltpu.SemaphoreType.DMA
    S = jax.ShapeDtypeStruct
    grid_spec = pltpu.PrefetchScalarGridSpec(
        num_scalar_prefetch=1, grid=(N_DEV, nT),
        in_specs=[pl.BlockSpec((TM, D), lambda p, i, s: (jnp.where(p == 0, i, nT - 1), 0)),
                  pl.BlockSpec(w_shard.shape, lambda p, i, s: (0, 0), pipeline_mode=pl.Buffered(1)),
                  pl.BlockSpec(vec_shard.shape, lambda p, i, s: (0, 0), pipeline_mode=pl.Buffered(1))]
        + [pl.BlockSpec(w.shape, lambda p, i, s: (0, 0), pipeline_mode=pl.Buffered(1)) for w in later],
        out_specs=[pl.BlockSpec((TM, SH), lambda p, i, s: (i, s[p])), hbm,
                   pl.BlockSpec((N_DEV,) + vec_shard.shape, lambda p, i, s: (0, 0, 0))] + [hbm] * nl,
        scratch_shapes=[pltpu.VMEM((N_DEV, D, SH), BF), pltpu.VMEM((N_DEV,) + vec_shard.shape, F32),
                        pltpu.VMEM(vec_shard.shape, F32), pltpu.VMEM((nT, TM, D), BF), pltpu.VMEM((1, D), F32),
                        dma((7,)), dma((7,)), dma, dma((7,)), dma((7,)), dma]
        + [pltpu.VMEM(w.shape, BF) for w in later] + _direct_sems(nl))
    return pl.pallas_call(
        body, name="a_in_proj", grid_spec=grid_spec,
        out_shape=(S((T, N_DEV * SH), BF), S((N_DEV, D, SH), BF), S((N_DEV,) + vec_shard.shape, F32))
        + tuple(S((N_DEV,) + w.shape, BF) for w in later),
        compiler_params=_params(("arbitrary", "arbitrary")),
    )(slots, x, w_shard, vec_shard, *later)


def _a_fwd(x, z, ln_g, ln_b, ws, bs_t, g_kv, b_kv, rc, rs1, rs2, wout_shard, wkv_shard, later):
    T, D = x.shape
    AW = z.shape[1] // 3
    G = ws.shape[0]
    TM = min(256, T)
    nT = T // TM
    nC = TM // CHUNK
    nl = len(later)
    RO, RK = wout_shard.shape[0], wkv_shard.shape[0]

    def body(x_ref, u_ref, v_ref, gt_ref, lng_ref, lnb_ref, ws_ref, bst_ref, gkv_ref, bkv_ref,
             rc_ref, rs1_ref, rs2_ref, wosh_ref, wksh_ref, *rest):
        shards, rest = rest[:nl], rest[nl:]
        (h1_ref, sv_ref, vhat_ref, rstd_ref, k4_ref, v4_ref, kt_ref, vt_ref, wo_out, wk_out), rest = rest[:10], rest[10:]
        gathered, rest = rest[:nl], rest[nl:]
        (sv_scr, y_scr, wo_scr, wk_scr, f_s, f_r, f_l), rest = rest[:7], rest[7:]
        stages, sems = rest[:nl], rest[nl:]
        p, i = pl.program_id(0), pl.program_id(1)
        me = _my_index()
        first = [_TwoLevel(wo_scr.at[me], wo_scr, f_s.at[0], f_r.at[0], f_l.at[0], own=False),
                 _TwoLevel(wk_scr.at[me], wk_scr, f_s.at[1], f_r.at[1], f_l.at[1], own=False)]
        keep = [pltpu.make_async_copy(wo_scr, wo_out, f_l.at[0]), pltpu.make_async_copy(wk_scr, wk_out, f_l.at[1])]
        lg = [_TwoLevel(stages[k], gathered[k], sems[0].at[k], sems[1].at[k], sems[2].at[k]) for k in range(nl)]

        @pl.when((p == 0) & (i == 0))
        def _():
            wo_scr[me] = wosh_ref[...].astype(BF)
            wk_scr[me] = wksh_ref[...].astype(BF)
            for g in first:
                g.start()
            for k in range(nl):
                stages[k][...] = shards[k][...].astype(BF)
                lg[k].start()

        @pl.when((p == 0) & (i == nT // 2))
        def _():
            for g in first:
                g.forward()

        @pl.when((p == 1) & (i == 0))
        def _():
            for g in first:
                g.finish()
            for cp in keep:
                cp.start()
            for g in lg:
                g.forward()

        @pl.when(p == 0)
        def _():
            u = u_ref[...].astype(F32)
            v = v_ref[...].astype(F32)
            gt = gt_ref[...].astype(F32)
            mu = jnp.mean(v, axis=-1, keepdims=True)
            xc = v - mu
            rstd = lax.rsqrt(jnp.mean(xc * xc, axis=-1, keepdims=True) + EPS)
            vhat = xc * rstd
            vln = (vhat * lng_ref[...] + lnb_ref[...]).astype(BF)
            tri = (lax.broadcasted_iota(jnp.int32, (CHUNK, CHUNK), 0)
                   >= lax.broadcasted_iota(jnp.int32, (CHUNK, CHUNK), 1))
            for g in range(G):
                wsm = jnp.where(tri, ws_ref[g], 0.0).astype(BF)
                bias = bst_ref[:, g:g + 1]
                for c in range(nC):
                    blk = vln[c * CHUNK:(c + 1) * CHUNK, g * CHUNK:(g + 1) * CHUNK]
                    sv_scr[c * CHUNK:(c + 1) * CHUNK, g * CHUNK:(g + 1) * CHUNK] = _dot(wsm, blk) + bias
            sv = sv_scr[...]
            silu, _ = _silu_parts(gt)
            y_scr[i] = (u * sv * silu).astype(BF)
            sv_ref[...] = sv.astype(BF)
            vhat_ref[...] = vhat.astype(BF)
            rstd_ref[...] = jnp.broadcast_to(rstd, rstd_ref.shape)

        @pl.when(p == 1)
        def _():
            y = y_scr[i]
            h1 = x_ref[...]
            for j in range(N_DEV):
                h1 += _dot(y[:, j * RO:(j + 1) * RO], wo_scr[j])
            h1_ref[...] = h1
            rkv = lax.rsqrt(jnp.mean(h1 * h1, axis=-1, keepdims=True) + EPS)
            nkv = (h1 * rkv * gkv_ref[...]).astype(BF)
            kv = bkv_ref[...]
            for j in range(N_DEV):
                kv = kv + _dot(nkv[:, j * RK:(j + 1) * RK], wk_scr[j])
            k_rot = _rot(kv[:, :LANES], rc_ref[...], rs1_ref[...], rs2_ref[...])
            for src, ref, tref in ((k_rot, k4_ref, kt_ref), (kv[:, LANES:], v4_ref, vt_ref)):
                t4 = _split4(src)
                ref[...] = t4.astype(BF)
                for c in range(nC):
                    for b in range(4):
                        blk = t4[c * CHUNK:(c + 1) * CHUNK, b * LANES:(b + 1) * LANES]
                        tref[c, b * LANES:(b + 1) * LANES, :] = blk.T.astype(BF)

        @pl.when((p == 1) & (i == nT - 1))
        def _():
            for g in lg:
                g.finish()
            for cp in keep:
                cp.wait()

    def sweep0(width):
        return pl.BlockSpec((TM, width), lambda p, i: (jnp.where(p == 0, i, nT - 1), 0))

    def sweep1(width):
        return pl.BlockSpec((TM, width), lambda p, i: (i * p, 0))

    def const(shape):
        nd = len(shape)
        return pl.BlockSpec(shape, lambda p, i: (0,) * nd, pipeline_mode=pl.Buffered(1))

    zcol = [pl.BlockSpec((TM, AW), functools.partial(lambda k, p, i: (jnp.where(p == 0, i, nT - 1), k), k))
            for k in range(3)]
    tr = pl.BlockSpec((nC, 4 * LANES, CHUNK), lambda p, i: (i * p, 0, 0))
    hbm = pl.BlockSpec(memory_space=pl.ANY)
    dma = pltpu.SemaphoreType.DMA
    S = jax.ShapeDtypeStruct
    return pl.pallas_call(
        body, name="a_fwd", grid=(2, nT),
        in_specs=[sweep1(D)] + zcol + [const((1, AW)), const((1, AW)), const(ws.shape), const(bs_t.shape),
                  const((1, D)), const((1, 2 * LANES)), sweep1(LANES), sweep1(LANES), sweep1(LANES),
                  const(wout_shard.shape), const(wkv_shard.shape)] + [const(w.shape) for w in later],
        out_specs=[sweep1(D), sweep0(AW), sweep0(AW), sweep0(LANES), sweep1(4 * LANES), sweep1(4 * LANES), tr, tr,
                   hbm, hbm] + [hbm] * nl,
        out_shape=(S((T, D), F32), S((T, AW), BF), S((T, AW), BF), S((T, LANES), F32),
                   S((T, 4 * LANES), BF), S((T, 4 * LANES), BF),
                   S((T // CHUNK, 4 * LANES, CHUNK), BF), S((T // CHUNK, 4 * LANES, CHUNK), BF),
                   S((N_DEV,) + wout_shard.shape, BF), S((N_DEV,) + wkv_shard.shape, BF))
        + tuple(S((N_DEV,) + w.shape, BF) for w in later),
        scratch_shapes=[pltpu.VMEM((TM, AW), F32), pltpu.VMEM((nT, TM, AW), BF),
                        pltpu.VMEM((N_DEV,) + wout_shard.shape, BF), pltpu.VMEM((N_DEV,) + wkv_shard.shape, BF),
                        dma((2, 7)), dma((2, 7)), dma((2,))]
        + [pltpu.VMEM(w.shape, BF) for w in later] + _direct_sems(nl),
        compiler_params=_params(("arbitrary", "arbitrary")),
    )(x, z, z, z, ln_g, ln_b, ws, bs_t, g_kv, b_kv, rc, rs1, rs2, wout_shard, wkv_shard, *later)


def _b_fwd(h1, g_b, wb_in, bq, rc, rs1, rs2, k4, vt, sinks, wb_out, g_f, target):
    T, D = h1.shape
    BW = wb_out.shape[0]
    SH = wb_in.shape[2]
    TM = min(256, T)
    nC = TM // CHUNK
    nP = BW // LANES

    def body(h1_ref, gb_ref, wbin_ref, bq_ref, rc_ref, rs1_ref, rs2_ref, k4_ref, vt_ref, sink_ref, wbout_ref, gf_ref,
             tgt_ref, q_ref, g2_ref, o_ref, dh2_ref, dh2b_ref, loss_ref, dgf_ref, z_scr, o_scr):
        i = pl.program_id(0)
        h1v = h1_ref[...]
        r2 = lax.rsqrt(jnp.mean(h1v * h1v, axis=-1, keepdims=True) + EPS)
        n2 = (h1v * r2 * gb_ref[...]).astype(BF)
        for j in range(N_DEV):
            z_scr[:, j * SH:(j + 1) * SH] = _dot(n2, wbin_ref[j])
        c_t, s1_t, s2_t = rc_ref[...], rs1_ref[...], rs2_ref[...]
        for p in range(nP):
            cols = slice(p * LANES, (p + 1) * LANES)
            qp = _rot(z_scr[:, cols] + bq_ref[:, cols], c_t, s1_t, s2_t) * (HEAD_DIM ** -0.5)
            q_ref[:, cols] = qp.astype(BF)
        g2 = z_scr[:, BW:]
        g2_ref[...] = g2.astype(BF)
        upper = _upper()
        for c in range(nC):
            ci = i * nC + c
            rows = slice(c * CHUNK, (c + 1) * CHUNK)
            pci = jnp.maximum(ci - 1, 0)
            prev = pl.multiple_of(pci * CHUNK, CHUNK)
            cur = pl.multiple_of(ci * CHUNK, CHUNK)
            qc = q_ref[rows, :]
            for h in range(2):
                st = _dot_nt(_band_rows(k4_ref, prev, cur, h), _stack_pairs(qc, h))
                fa, fb = _fold(st, upper, ci > 0)
                pa, _ = _softmax_sink(fa, sink_ref[2 * h:2 * h + 1, :])
                pb, _ = _softmax_sink(fb, sink_ref[2 * h + 1:2 * h + 2, :])
                ot = _dot(_band_cols(vt_ref, pci, ci, h), _unfold(pa, pb, upper).astype(BF))
                for j in range(4):
                    o_scr[rows, (h * 4 + j) * LANES:(h * 4 + j + 1) * LANES] = ot[:, j * CHUNK:(j + 1) * CHUNK].T
        o = o_scr[...]
        o_ref[...] = o.astype(BF)
        silu, _ = _silu_parts(g2)
        h2 = h1v + _dot((o * silu).astype(BF), wbout_ref[...])
        rf = lax.rsqrt(jnp.mean(h2 * h2, axis=-1, keepdims=True) + EPS)
        xh = h2 * rf
        gf = gf_ref[...]
        err = xh * gf - tgt_ref[...]
        dyf = err * (1.0 / D)
        dh2 = _rms_bwd(dyf, xh, rf, gf)
        dh2_ref[...] = dh2
        dh2b_ref[...] = dh2.astype(BF)

        @pl.when(i == 0)
        def _():
            loss_ref[...] = jnp.zeros_like(loss_ref)
            dgf_ref[...] = jnp.zeros_like(dgf_ref)

        loss_ref[...] += 0.5 * jnp.sum(jnp.mean(err * err, axis=-1, keepdims=True), axis=0, keepdims=True)
        dgf_ref[...] += jnp.sum(dyf * xh, axis=0, keepdims=True)

    row = functools.partial(_row_spec, TM)
    S = jax.ShapeDtypeStruct
    return pl.pallas_call(
        body, name="b_fwd", grid=(T // TM,),
        in_specs=[row(D), _const_spec((1, D)), _const_spec(wb_in.shape), _const_spec((1, BW)), row(LANES), row(LANES),
                  row(LANES), _const_spec(k4.shape), _const_spec(vt.shape), _const_spec(sinks.shape),
                  _const_spec(wb_out.shape), _const_spec((1, D)), row(D)],
        out_specs=[row(BW), row(BW), row(BW), row(D), row(D), _acc_spec((1, 1)), _acc_spec((1, D))],
        out_shape=(S((T, BW), BF), S((T, BW), BF), S((T, BW), BF), S((T, D), F32), S((T, D), BF), S((1, 1), F32),
                   S((1, D), F32)),
        scratch_shapes=[pltpu.VMEM((TM, 2 * BW), F32), pltpu.VMEM((TM, BW), F32)],
        compiler_params=_params(("arbitrary",)),
    )(h1, g_b, wb_in, bq, rc, rs1, rs2, k4, vt, sinks, wb_out, g_f, target)


def _b_bwd(dh2, h1, q, g2, o, k4, v4, kt, sinks, wb_out, wb_in, g_b, rc, rs1, rs2):
    T, D = h1.shape
    BW = wb_out.shape[0]
    SH = wb_in.shape[2]
    TM = min(256, T)
    nT = T // TM
    nC = TM // CHUNK
    nP = BW // LANES

    def body(dh2_ref, h1_ref, q_ref, g2_ref, o_ref, k4_ref, v4_ref, kt_ref, sink_ref, wbout_ref, wbin_ref, gb_ref,
             rc_ref, rs1_ref, rs2_ref,
             dh1_ref, dz2_ref, n2_ref, y2_ref, dk_ref, dv_ref, dbq_ref, dgb_ref, dsink_ref, do_scr, dq_scr, dsacc_scr):
        i = pl.program_id(0)

        @pl.when(i == 0)
        def _():
            dk_ref[...] = jnp.zeros_like(dk_ref)
            dv_ref[...] = jnp.zeros_like(dv_ref)
            dbq_ref[...] = jnp.zeros_like(dbq_ref)
            dgb_ref[...] = jnp.zeros_like(dgb_ref)
            dsacc_scr[...] = jnp.zeros_like(dsacc_scr)

        dh2 = dh2_ref[...]
        dy2 = _dot_nt(dh2.astype(BF), wbout_ref[...])
        g2v = g2_ref[...].astype(F32)
        ov = o_ref[...].astype(F32)
        silu, dsilu = _silu_parts(g2v)
        y2_ref[...] = (ov * silu).astype(BF).T
        do_scr[...] = (dy2 * silu).astype(BF)
        dz2_ref[:, BW:] = (dy2 * ov * dsilu).astype(BF)
        upper = _upper()
        lo = _lane_lo((2 * CHUNK, LANES))
        for c in range(nC):
            ci = i * nC + c
            rows = slice(c * CHUNK, (c + 1) * CHUNK)
            pci = jnp.maximum(ci - 1, 0)
            prev = pl.multiple_of(pci * CHUNK, CHUNK)
            cur = pl.multiple_of(ci * CHUNK, CHUNK)
            qc = q_ref[rows, :]
            doc = do_scr[rows, :]
            dkb = jnp.zeros((2 * CHUNK, LANES), F32)
            dvb = jnp.zeros((2 * CHUNK, LANES), F32)
            for h in range(2):
                qs = _stack_pairs(qc, h)
                dos = _stack_pairs(doc, h)
                fa, fb = _fold(_dot_nt(_band_rows(k4_ref, prev, cur, h), qs), upper, ci > 0)
                dfa, dfb = _fold(_dot_nt(_band_rows(v4_ref, prev, cur, h), dos), upper)
                folded = []
                for k, (f, df) in enumerate(((fa, dfa), (fb, dfb))):
                    p, ps = _softmax_sink(f, sink_ref[2 * h + k:2 * h + k + 1, :])
                    delta = jnp.sum(p * df, axis=0, keepdims=True)
                    dsacc_scr[2 * h + k:2 * h + k + 1, :] -= ps * delta
                    folded.append((p, p * (df - delta)))
                pt = _unfold(folded[0][0], folded[1][0], upper).astype(BF)
                dst = _unfold(folded[0][1], folded[1][1], upper).astype(BF)
                dqt = _dot(_band_cols(kt_ref, pci, ci, h), dst)
                for j in range(4):
                    dq_scr[rows, (h * 4 + j) * LANES:(h * 4 + j + 1) * LANES] = dqt[:, j * CHUNK:(j + 1) * CHUNK].T
                for acc_name, g in (("k", _dot(dst, qs)), ("v", _dot(pt, dos))):
                    a, b = g[:2 * CHUNK], g[2 * CHUNK:]
                    if h == 0:
                        part = jnp.where(lo, a + pltpu.roll(b, HEAD_DIM, 1), 0.0)
                    else:
                        part = jnp.where(lo, 0.0, pltpu.roll(a, HEAD_DIM, 1) + b)
                    if acc_name == "k":
                        dkb += part
                    else:
                        dvb += part
            dk_ref[pl.ds(prev, CHUNK), :] += dkb[:CHUNK]
            dk_ref[pl.ds(cur, CHUNK), :] += dkb[CHUNK:]
            dv_ref[pl.ds(prev, CHUNK), :] += dvb[:CHUNK]
            dv_ref[pl.ds(cur, CHUNK), :] += dvb[CHUNK:]

        @pl.when(i == nT - 1)
        def _():
            lane = lax.broadcasted_iota(jnp.int32, dsink_ref.shape, 1)
            tot = jnp.zeros(dsink_ref.shape, F32)
            for j in range(4):
                tot += jnp.where(lane == j, jnp.sum(dsacc_scr[:, j * CHUNK:(j + 1) * CHUNK], axis=1, keepdims=True), 0.0)
            dsink_ref[...] = tot
        c_t, s1_t, s2_t = rc_ref[...], rs1_ref[...], rs2_ref[...]
        for p in range(nP):
            cols = slice(p * LANES, (p + 1) * LANES)
            dqp = _rot_bwd(dq_scr[:, cols] * (HEAD_DIM ** -0.5), c_t, s1_t, s2_t)
            dbq_ref[:, cols] += jnp.sum(dqp, axis=0, keepdims=True)
            dz2_ref[:, cols] = dqp.astype(BF)
        h1v = h1_ref[...]
        r2 = lax.rsqrt(jnp.mean(h1v * h1v, axis=-1, keepdims=True) + EPS)
        xh = h1v * r2
        gb = gb_ref[...]
        n2_ref[...] = (xh * gb).astype(BF).T
        dn2 = None
        for j in range(N_DEV):
            part = _dot_nt(dz2_ref[:, j * SH:(j + 1) * SH], wbin_ref[j])
            dn2 = part if dn2 is None else dn2 + part
        dgb_ref[...] += jnp.sum(dn2 * xh, axis=0, keepdims=True)
        dh1_ref[...] = dh2 + _rms_bwd(dn2, xh, r2, gb)

    row = functools.partial(_row_spec, TM)
    S = jax.ShapeDtypeStruct
    return pl.pallas_call(
        body, name="b_bwd", grid=(T // TM,),
        in_specs=[row(D), row(D), row(BW), row(BW), row(BW), _const_spec(k4.shape), _const_spec(v4.shape),
                  _const_spec(kt.shape), _const_spec(sinks.shape), _const_spec(wb_out.shape), _const_spec(wb_in.shape),
                  _const_spec((1, D)), row(LANES), row(LANES), row(LANES)],
        out_specs=[row(D), row(2 * BW), _col_spec(TM, D), _col_spec(TM, BW), _acc_spec((T, LANES)),
                   _acc_spec((T, LANES)), _acc_spec((1, BW)), _acc_spec((1, D)), _acc_spec((4, LANES))],
        out_shape=(S((T, D), F32), S((T, 2 * BW), BF), S((D, T), BF), S((BW, T), BF), S((T, LANES), F32),
                   S((T, LANES), F32), S((1, BW), F32), S((1, D), F32), S((4, LANES), F32)),
        scratch_shapes=[pltpu.VMEM((TM, BW), BF), pltpu.VMEM((TM, BW), F32), pltpu.VMEM((4, 4 * CHUNK), F32)],
        compiler_params=_params(("arbitrary",)),
    )(dh2, h1, q, g2, o, k4, v4, kt, sinks, wb_out, wb_in, g_b, rc, rs1, rs2)


def _a_bwd(dh1p, dk, dv, h1, g_kv, w_kv, wa_out, ws, ln_g, ln_b, z, sv, vhat, rstd, rc, rs1, rs2, ready):
    T, D = h1.shape
    AW = wa_out.shape[0]
    G = ws.shape[0]
    TM = min(256, T)
    nT = T // TM
    nC = TM // CHUNK
    nr = len(ready)

    def body(dh1p_ref, dk_ref, dv_ref, h1_ref, gkv_ref, wkv_ref, waout_ref, ws_ref, lng_ref,
             lnb_ref, u_ref, gt_ref, sv_ref, vhat_ref, rstd_ref, rc_ref, rs1_ref, rs2_ref, *rest):
        ready_refs, rest = rest[:nr], rest[nr:]
        (dz_ref, y_ref, nkv_ref, dkv_ref, dh1_ref, dh1f_ref, dgkv_ref, dbkv_ref, dlng_ref, dlnb_ref,
         dws_ref, dbs_ref), rest = rest[:12], rest[12:]
        recv_refs, (dsv_scr, dvln_scr, ssem, rsem, lsem) = rest[:nr], rest[nr:]
        i = pl.program_id(0)
        exchanges = [_Direct(ready_refs[k], recv_refs[k], ssem.at[k], rsem.at[k], lsem.at[k], scatter=True)
                     for k in range(nr)]

        @pl.when(i == 0)
        def _():
            for e in exchanges:
                e.start()
            for r in (dgkv_ref, dbkv_ref, dlng_ref, dlnb_ref, dws_ref, dbs_ref):
                r[...] = jnp.zeros_like(r)

        dk_pre = _rot_bwd(dk_ref[...], rc_ref[...], rs1_ref[...], rs2_ref[...])
        dkv = jnp.concatenate([dk_pre, dv_ref[...]], axis=1)
        dbkv_ref[...] += jnp.sum(dkv, axis=0, keepdims=True)
        dkv_b = dkv.astype(BF)
        dkv_ref[...] = dkv_b
        h1v = h1_ref[...]
        rkv = lax.rsqrt(jnp.mean(h1v * h1v, axis=-1, keepdims=True) + EPS)
        xh_kv = h1v * rkv
        gkv = gkv_ref[...]
        nkv_ref[...] = (xh_kv * gkv).astype(BF).T
        dnkv = _dot_nt(dkv_b, wkv_ref[...])
        dgkv_ref[...] += jnp.sum(dnkv * xh_kv, axis=0, keepdims=True)
        dh1 = dh1p_ref[...] + _rms_bwd(dnkv, xh_kv, rkv, gkv)
        dh1_b = dh1.astype(BF)
        dh1_ref[...] = dh1_b
        dh1f_ref[...] = dh1
        dy = _dot_nt(dh1_b, waout_ref[...]).astype(BF)
        silu, dsilu = _silu_parts(gt_ref[...].astype(F32))
        silu, dsilu = silu.astype(BF), dsilu.astype(BF)
        ub, svb = u_ref[...], sv_ref[...]
        us = ub * silu
        dys = dy * svb
        y_ref[...] = (us * svb).T
        dz_ref[:, :AW] = dys * silu
        dz_ref[:, 2 * AW:] = dys * ub * dsilu
        dsv_scr[...] = dy * us
        vhat_v = vhat_ref[...].astype(F32)
        lng = lng_ref[...]
        vln_b = (vhat_v * lng + lnb_ref[...]).astype(BF)
        tri = lax.broadcasted_iota(jnp.int32, (CHUNK, CHUNK), 0) >= lax.broadcasted_iota(jnp.int32, (CHUNK, CHUNK), 1)
        lane = lax.broadcasted_iota(jnp.int32, (CHUNK, LANES), 1)
        dbs = jnp.zeros((CHUNK, LANES), F32)
        for g in range(G):
            wsm = jnp.where(tri, ws_ref[g], 0.0).astype(BF)
            cols = slice(g * CHUNK, (g + 1) * CHUNK)
            dws_g = None
            for c in range(nC):
                rows = slice(c * CHUNK, (c + 1) * CHUNK)
                dsv_cg = dsv_scr[rows, cols]
                dvln_scr[rows, cols] = _dot_tn(wsm, dsv_cg)
                part = _dot_nt(dsv_cg, vln_b[rows, cols])
                dws_g = part if dws_g is None else dws_g + part
                dbs += jnp.where(lane == g, jnp.sum(dsv_cg.astype(F32), axis=-1, keepdims=True), 0.0)
            dws_ref[g] += jnp.where(tri, dws_g, 0.0)
        dbs_ref[...] += dbs
        dvln = dvln_scr[...]
        dlng_ref[...] += jnp.sum(dvln * vhat_v, axis=0, keepdims=True)
        dlnb_ref[...] += jnp.sum(dvln, axis=0, keepdims=True)
        a = dvln * lng
        dvv = rstd_ref[:, 0:1] * (a - jnp.mean(a, axis=-1, keepdims=True)
                                  - vhat_v * jnp.mean(a * vhat_v, axis=-1, keepdims=True))
        dz_ref[:, AW:2 * AW] = dvv.astype(BF)

        @pl.when(i == nT - 1)
        def _():
            for e in exchanges:
                e.finish()

    row = functools.partial(_row_spec, TM)
    col = functools.partial(_col_spec, TM)
    hbm = pl.BlockSpec(memory_space=pl.ANY)
    S = jax.ShapeDtypeStruct
    return pl.pallas_call(
        body, name="a_bwd", grid=(nT,),
        in_specs=[row(D), row(LANES), row(LANES), row(D), _const_spec((1, D)), _const_spec(w_kv.shape),
                  _const_spec(wa_out.shape), _const_spec(ws.shape),
                  _const_spec((1, AW)), _const_spec((1, AW)), pl.BlockSpec((TM, AW), lambda i: (i, 0)),
                  pl.BlockSpec((TM, AW), lambda i: (i, 2)), row(AW), row(AW), row(LANES),
                  row(LANES), row(LANES), row(LANES)] + [hbm] * nr,
        out_specs=[row(3 * AW), col(AW), col(D), row(2 * LANES), row(D), row(D),
                   _acc_spec((1, D)), _acc_spec((1, 2 * LANES)), _acc_spec((1, AW)),
                   _acc_spec((1, AW)), _acc_spec(ws.shape), _acc_spec((CHUNK, LANES))] + [hbm] * nr,
        out_shape=(S((T, 3 * AW), BF), S((AW, T), BF), S((D, T), BF), S((T, 2 * LANES), BF), S((T, D), BF),
                   S((T, D), F32),
                   S((1, D), F32), S((1, 2 * LANES), F32), S((1, AW), F32), S((1, AW), F32),
                   S(ws.shape, F32), S((CHUNK, LANES), F32)) + tuple(S(r.shape, r.dtype) for r in ready),
        scratch_shapes=[pltpu.VMEM((TM, AW), BF), pltpu.VMEM((TM, AW), F32)] + _direct_sems(nr),
        compiler_params=_params(("arbitrary",)),
    )(dh1p, dk, dv, h1, g_kv, w_kv, wa_out, ws, ln_g, ln_b, z, z, sv, vhat, rstd, rc, rs1, rs2, *ready)


def _a_in_bwd(dz, wa_in, x, dh1, g_a, ready):
    T, D = x.shape
    SH = wa_in.shape[2]
    TM = min(512, T)
    nT = T // TM
    nr = len(ready)

    def body(dz_ref, wain_ref, x_ref, dh1_ref, ga_ref, *rest):
        ready_refs, (dx_ref, n1_ref, dga_ref), rest = rest[:nr], rest[nr:nr + 3], rest[nr + 3:]
        recv_refs, (ssem, rsem, lsem) = rest[:nr], rest[nr:]
        i = pl.program_id(0)
        exchanges = [_Direct(ready_refs[k], recv_refs[k], ssem.at[k], rsem.at[k], lsem.at[k], scatter=True)
                     for k in range(nr)]

        @pl.when(i == 0)
        def _():
            for e in exchanges:
                e.start()
            dga_ref[...] = jnp.zeros_like(dga_ref)

        xv = x_ref[...]
        r1 = lax.rsqrt(jnp.mean(xv * xv, axis=-1, keepdims=True) + EPS)
        xh = xv * r1
        ga = ga_ref[...]
        n1_ref[...] = (xh * ga).astype(BF).T
        dn1 = None
        for j in range(N_DEV):
            part = _dot_nt(dz_ref[:, j * SH:(j + 1) * SH], wain_ref[j])
            dn1 = part if dn1 is None else dn1 + part
        dga_ref[...] += jnp.sum(dn1 * xh, axis=0, keepdims=True)
        dx_ref[...] = dh1_ref[...] + _rms_bwd(dn1, xh, r1, ga)

        @pl.when(i == nT - 1)
        def _():
            for e in exchanges:
                e.finish()

    row = functools.partial(_row_spec, TM)
    hbm = pl.BlockSpec(memory_space=pl.ANY)
    S = jax.ShapeDtypeStruct
    return pl.pallas_call(
        body, name="a_in_bwd", grid=(nT,),
        in_specs=[row(dz.shape[1]), _const_spec(wa_in.shape), row(D), row(D), _const_spec((1, D))] + [hbm] * nr,
        out_specs=[row(D), _col_spec(TM, D), _acc_spec((1, D))] + [hbm] * nr,
        out_shape=(S((T, D), F32), S((D, T), BF), S((1, D), F32)) + tuple(S(r.shape, r.dtype) for r in ready),
        scratch_shapes=_direct_sems(nr),
        compiler_params=_params(("arbitrary",)),
    )(dz, wa_in, x, dh1, g_a, *ready)


def _wgrad(at, b, nblk, name, bt=512):
    K, T = at.shape
    N = b.shape[1] // nblk
    BT = min(bt, T)
    nt = T // BT

    def body(a_ref, b_ref, o_ref, acc):
        t = pl.program_id(1)

        @pl.when(t == 0)
        def _():
            acc[...] = jnp.zeros_like(acc)

        acc[...] += _dot(a_ref[...], b_ref[...])

        @pl.when(t == nt - 1)
        def _():
            o_ref[0] = acc[...].astype(BF)

    return pl.pallas_call(
        body, name=name, grid=(nblk, nt),
        in_specs=[pl.BlockSpec((K, BT), lambda j, t: (0, t)), pl.BlockSpec((BT, N), lambda j, t: (t, j))],
        out_specs=pl.BlockSpec((1, K, N), lambda j, t: (j, 0, 0)),
        out_shape=jax.ShapeDtypeStruct((nblk, K, N), BF),
        scratch_shapes=[pltpu.VMEM((K, N), F32)],
        compiler_params=_params(("arbitrary", "arbitrary")),
    )(at, b)


def _wgrad_exchange(a, b, me, extras, name):
    K, T = a.shape
    N = b.shape[1] // N_DEV
    BT = min(1024, T)
    nt = T // BT
    ne = len(extras)
    last = N_DEV - 1
    n_chip = N_DEV // 2

    def body(me_ref, a_ref, b_ref, *rest):
        ex_in, recv_ref, ex_out = rest[:ne], rest[ne], rest[ne + 1:2 * ne + 1]
        acc, dstage, istage, half, d_s, d_r, i_s, i_r, lsem, ex_ssem, ex_rsem, ex_lsem = rest[2 * ne + 1:]
        s, t = pl.program_id(0), pl.program_id(1)
        x, y, c = (lax.axis_index(ax) for ax in AXES)
        ex = [_Direct(ex_in[k], ex_out[k], ex_ssem.at[k], ex_rsem.at[k], ex_lsem.at[k], scatter=True) for k in range(ne)]

        def to_sibling(k, slot):
            return pltpu.make_async_remote_copy(src_ref=dstage.at[slot], dst_ref=half.at[k], send_sem=d_s.at[k],
                                                recv_sem=d_r.at[k], device_id=(x, y, 1 - c), device_id_type=MESH)

        def to_chip(k, slot, sender):
            far = n_chip - 1 - k
            px, py = x ^ ((far >> 1) & 1), y ^ (far & 1)
            dst = recv_ref.at[2 * x + y] if sender else recv_ref.at[2 * px + py]
            return pltpu.make_async_remote_copy(src_ref=istage.at[slot], dst_ref=dst, send_sem=i_s.at[k],
                                                recv_sem=i_r.at[k], device_id=(px, py, c), device_id_type=MESH)

        @pl.when((s == 0) & (t == 0))
        def _():
            for e in ex:
                e.start()

        @pl.when(t == 0)
        def _():
            acc[...] = jnp.zeros_like(acc)

        acc[...] += _dot(a_ref[...], b_ref[...])

        @pl.when(t == nt - 1)
        def _():
            k = lax.div(s, 2)
            slot = lax.rem(k, 2)

            @pl.when(lax.rem(s, 2) == 0)
            def _():
                @pl.when(k >= 2)
                def _():
                    to_sibling(k - 2, slot).wait_send()

                dstage[slot] = acc[...].astype(BF)
                to_sibling(k, slot).start()

            @pl.when(lax.rem(s, 2) == 1)
            def _():
                to_sibling(k, slot).wait_recv()

                @pl.when(k >= 2)
                def _():
                    to_chip(k - 2, slot, True).wait_send()

                istage[slot] = (acc[...] + half[k].astype(F32)).astype(BF)

                @pl.when(k < n_chip - 1)
                def _():
                    to_chip(k, slot, True).start()

            @pl.when(s == last)
            def _():
                own = pltpu.make_async_copy(istage.at[slot], recv_ref.at[2 * x + y], lsem)
                own.start()
                to_chip(n_chip - 2, 0, True).wait_send()
                to_sibling(n_chip - 2, 0).wait_send()
                to_sibling(n_chip - 1, 1).wait_send()
                for kk in range(n_chip - 1):
                    to_chip(kk, 0, False).wait_recv()
                own.wait()
                for e in ex:
                    e.finish()

    hbm = pl.BlockSpec(memory_space=pl.ANY)
    dma = pltpu.SemaphoreType.DMA
    grid_spec = pltpu.PrefetchScalarGridSpec(
        num_scalar_prefetch=1, grid=(N_DEV, nt),
        in_specs=[pl.BlockSpec((K, BT), lambda s, t, me_ref: (0, t)),
                  pl.BlockSpec((BT, N), lambda s, t, me_ref: (t, me_ref[0] ^ (last - s)))] + [hbm] * ne,
        out_specs=[hbm] * (ne + 1),
        scratch_shapes=[pltpu.VMEM((K, N), F32), pltpu.VMEM((2, K, N), BF), pltpu.VMEM((2, K, N), BF),
                        pltpu.VMEM((n_chip, K, N), BF), dma((n_chip,)), dma((n_chip,)), dma((n_chip - 1,)),
                        dma((n_chip - 1,)), dma] + _direct_sems(ne))
    return pl.pallas_call(
        body, name=name, grid_spec=grid_spec,
        out_shape=[jax.ShapeDtypeStruct((n_chip, K, N), BF)] + [jax.ShapeDtypeStruct(e.shape, e.dtype) for e in extras],
        compiler_params=_params(("arbitrary", "arbitrary")),
    )(me, a, b, *extras)


def _my_index():
    return 4 * lax.axis_index("x") + 2 * lax.axis_index("y") + lax.axis_index("c")


def _all_gather(arrs, dtypes, name):
    n = len(arrs)

    def body(*refs):
        ins, outs = refs[:n], refs[n:2 * n]
        stages = refs[2 * n:3 * n]
        send_sems, recv_sems, local_sems = refs[3 * n:]
        gathers = [_TwoLevel(stages[a], outs[a], send_sems.at[a], recv_sems.at[a], local_sems.at[a]) for a in range(n)]
        for a in range(n):
            stages[a][...] = ins[a][...].astype(stages[a].dtype)
            gathers[a].start()
        for g in gathers:
            g.forward()
        for g in gathers:
            g.finish()

    vm = pl.BlockSpec(memory_space=pltpu.VMEM)
    hbm = pl.BlockSpec(memory_space=pl.ANY)
    return pl.pallas_call(
        body, name=name,
        in_specs=[vm] * n, out_specs=[hbm] * n,
        out_shape=[jax.ShapeDtypeStruct((N_DEV,) + a.shape, dt) for a, dt in zip(arrs, dtypes)],
        scratch_shapes=[pltpu.VMEM(a.shape, dt) for a, dt in zip(arrs, dtypes)]
        + [pltpu.SemaphoreType.DMA((n, 7)), pltpu.SemaphoreType.DMA((n, 7)), pltpu.SemaphoreType.DMA((n,))],
        compiler_params=pltpu.CompilerParams(vmem_limit_bytes=VMEM_LIMIT),
    )(*arrs)


def _peer(mask):
    x, y, c = (lax.axis_index(a) for a in AXES)
    return (x ^ ((mask >> 2) & 1), y ^ ((mask >> 1) & 1), c ^ (mask & 1))


def _dev_index(p):
    return 4 * p[0] + 2 * p[1] + p[2]


class _Direct:
    def __init__(self, src, dst, send_sems, recv_sems, local_sem, scatter):
        me = _my_index()
        self.own = pltpu.make_async_copy(src.at[me] if scatter else src, dst.at[me], local_sem)
        self.sends, self.recvs = [], []
        for k in range(1, N_DEV):
            p = _peer(k)
            pi = _dev_index(p)
            sems = dict(send_sem=send_sems.at[k - 1], recv_sem=recv_sems.at[k - 1], device_id=p, device_id_type=MESH)
            self.sends.append(pltpu.make_async_remote_copy(src_ref=src.at[pi] if scatter else src, dst_ref=dst.at[me],
                                                           **sems))
            self.recvs.append(pltpu.make_async_remote_copy(src_ref=src.at[me] if scatter else src, dst_ref=dst.at[pi],
                                                           **sems))

    def start(self):
        self.own.start()
        for cp in self.sends:
            cp.start()

    def finish(self):
        for cp in self.sends:
            cp.wait_send()
        for cp in self.recvs:
            cp.wait_recv()
        self.own.wait()


class _TwoLevel:
    def __init__(self, src, dst, send_sems, recv_sems, local_sem, own=True):
        x, y, c = (lax.axis_index(a) for a in AXES)
        self.me, self.sibling = (x, y, c), (x, y, 1 - c)
        self.chips = [(1 - x, y), (x, 1 - y), (1 - x, 1 - y)]
        self.src, self.dst, self.send_sems, self.recv_sems = src, dst, send_sems, recv_sems
        self.own = pltpu.make_async_copy(src, dst.at[_dev_index(self.me)], local_sem) if own else None

    def _copy(self, k, block, to, from_src=False):
        slot = self.dst.at[_dev_index(block)]
        return pltpu.make_async_remote_copy(src_ref=self.src if from_src else slot, dst_ref=slot,
                                            send_sem=self.send_sems.at[k], recv_sem=self.recv_sems.at[k],
                                            device_id=to, device_id_type=MESH)

    def _firsts(self):
        c = self.me[2]
        return [self._copy(0, self.me, self.sibling, True)] + [self._copy(1 + j, self.me, (*chip, c), True)
                                                               for j, chip in enumerate(self.chips)]

    def _passed(self):
        c = self.me[2]
        return [self._copy(4 + j, (*chip, c), self.sibling) for j, chip in enumerate(self.chips)]

    def start(self):
        if self.own is not None:
            self.own.start()
        for cp in self._firsts():
            cp.start()

    def wait_sibling(self):
        self._copy(0, self.sibling, self.me).wait_recv()

    def wait_chip_and_forward(self, j):
        self._copy(1 + j, (*self.chips[j], self.me[2]), self.me).wait_recv()
        self._passed()[j].start()

    def wait_passed(self, j):
        self._copy(4 + j, (*self.chips[j], 1 - self.me[2]), self.me).wait_recv()

    def wait_sends(self):
        for cp in self._firsts() + self._passed():
            cp.wait_send()
        if self.own is not None:
            self.own.wait()

    def forward(self):
        for j in range(3):
            self.wait_chip_and_forward(j)

    def finish(self):
        self.wait_sibling()
        for j in range(3):
            self.wait_passed(j)
        self.wait_sends()


def _direct_sems(n):
    if n == 0:
        return []
    return [pltpu.SemaphoreType.DMA((n, 7)), pltpu.SemaphoreType.DMA((n, 7)), pltpu.SemaphoreType.DMA((n,))]


def _adam_math(w, g, m, v):
    m = ADAM_B1 * m + (1.0 - ADAM_B1) * g
    v = ADAM_B2 * v + (1.0 - ADAM_B2) * (g * g)
    m_hat = m / (1.0 - ADAM_B1 ** ADAM_STEP)
    v_hat = v / (1.0 - ADAM_B2 ** ADAM_STEP)
    delta = -ADAM_LR * (m_hat / (jnp.sqrt(v_hat) + ADAM_EPS) + ADAM_WD * w)
    return delta, m, v


def _sum_adam(parts, w, m, v, name):
    R, C = w.shape
    NP = parts.shape[0]
    BR = CHUNK if R % CHUNK == 0 else R

    def body(p_ref, w_ref, m_ref, v_ref, g_ref, d_ref, nm_ref, nv_ref):
        g = p_ref[0].astype(F32)
        for i in range(1, NP):
            g = g + p_ref[i].astype(F32)
        g_ref[...] = g
        d_ref[...], nm_ref[...], nv_ref[...] = _adam_math(w_ref[...], g, m_ref[...], v_ref[...])

    blk = pl.BlockSpec((BR, C), lambda i: (i, 0))
    S = jax.ShapeDtypeStruct((R, C), F32)
    return pl.pallas_call(
        body, name=name, grid=(R // BR,),
        in_specs=[pl.BlockSpec((NP, BR, C), lambda i: (0, i, 0)), blk, blk, blk],
        out_specs=[blk] * 4, out_shape=(S,) * 4,
        compiler_params=_params(("arbitrary",)),
    )(parts, w, m, v)


def _sum8(parts, name):
    _, R, C = parts.shape

    def body(p_ref, o_ref):
        g = p_ref[0]
        for i in range(1, N_DEV):
            g = g + p_ref[i]
        o_ref[...] = g

    return pl.pallas_call(body, name=name, out_shape=jax.ShapeDtypeStruct((R, C), F32))(parts)


def _adam_only(g, w, m, v, name):
    def body(g_ref, w_ref, m_ref, v_ref, d_ref, nm_ref, nv_ref):
        d_ref[...], nm_ref[...], nv_ref[...] = _adam_math(w_ref[...], g_ref[...], m_ref[...], v_ref[...])

    S = jax.ShapeDtypeStruct(w.shape, F32)
    return pl.pallas_call(body, name=name, out_shape=(S,) * 3)(g, w, m, v)


def _rope_tables(T):
    pos = np.arange(T, dtype=np.float32)
    inv_freq = (np.float64(ROPE_THETA) ** (-np.arange(0, HEAD_DIM, 2, dtype=np.float64) / HEAD_DIM)).astype(np.float32)
    ang = (pos[:, None] * inv_freq[None, :]).astype(np.float64)
    cos, sin, zero = np.cos(ang).astype(np.float32), np.sin(ang).astype(np.float32), np.zeros(ang.shape, np.float32)
    c = np.concatenate([cos, cos, cos, cos], axis=1)
    s1 = np.concatenate([-sin, zero, -sin, zero], axis=1)
    s2 = np.concatenate([zero, sin, zero, sin], axis=1)
    return jnp.asarray(c), jnp.asarray(s1), jnp.asarray(s2)


SUBLANES = 8


def _nrows(size):
    return -(-size // (SUBLANES * LANES)) * SUBLANES


def _rows(a):
    flat = a.reshape(-1)
    pad = _nrows(flat.shape[0]) * LANES - flat.shape[0]
    if pad:
        flat = jnp.concatenate([flat, jnp.zeros((pad,), flat.dtype)])
    return flat.reshape(-1, LANES)


def _pack(arrs, total_rows):
    rows = [_rows(a) for a in arrs]
    used = sum(r.shape[0] for r in rows)
    if total_rows > used:
        rows.append(jnp.zeros((total_rows - used, LANES), F32))
    return jnp.concatenate(rows, axis=0)


def _unpack(packed, shapes):
    out, at = [], 0
    for shp in shapes:
        size = math.prod(shp)
        nrow = _nrows(size)
        out.append(packed[at:at + nrow].reshape(-1)[:size].reshape(shp))
        at += nrow
    return out


def kernel(x, a_norm_g, a_w_in, a_ln_g, a_ln_b, a_ws, a_bs, a_w_out, kv_norm_g, w_kv, b_kv, b_norm_g, b_w_in, b_bq, b_sinks, b_w_out, final_norm_g, loss_target, m_a_norm_g, m_a_w_in, m_a_ln_g, m_a_ln_b, m_a_ws, m_a_bs, m_a_w_out, m_kv_norm_g, m_w_kv, m_b_kv, m_b_norm_g, m_b_w_in, m_b_bq, m_b_sinks, m_b_w_out, m_final_norm_g, v_a_norm_g, v_a_w_in, v_a_ln_g, v_a_ln_b, v_a_ws, v_a_bs, v_a_w_out, v_kv_norm_g, v_w_kv, v_b_kv, v_b_norm_g, v_b_w_in, v_b_bq, v_b_sinks, v_b_w_out, v_final_norm_g):
    T, D = x.shape[1], x.shape[2]
    AW = a_ln_g.shape[1] * N_DEV
    G = a_ws.shape[1]
    assert w_kv.shape[1] == 2 * LANES and a_ws.shape[2] == CHUNK and T % CHUNK == 0
    me = _my_index()

    xs, tgt = x[0], loss_target[0]
    vec = jnp.concatenate([a_norm_g, a_ln_g, a_ln_b], axis=1)
    vec = jnp.broadcast_to(vec, (SUBLANES, vec.shape[1]))
    slots = me ^ jnp.array(PASS_MASKS, jnp.int32)
    z, wa_in, vecs = _in_proj(xs, a_w_in[0], vec, slots, [])
    vecs = vecs[:, 0, :]
    ds = D // N_DEV
    g_a = vecs[:, :ds].reshape(1, D)
    ln_g = vecs[:, ds:ds + AW // N_DEV].reshape(1, AW)
    ln_b = vecs[:, ds + AW // N_DEV:].reshape(1, AW)

    rc, rs1, rs2 = _rope_tables(T)
    ws = a_ws[0]
    bs_t = a_bs[0].T
    g_kv = kv_norm_g.reshape(1, D)
    bkv = b_kv.reshape(1, -1)
    g_f = final_norm_g.reshape(1, D)
    sinks = jnp.repeat(b_sinks.reshape(2, 4, 2).transpose(0, 2, 1).reshape(4, 4), CHUNK, axis=1)
    h1, sv, vhat, rstd, k4, v4, kt, vt, wa_out, wkv, wb_in, wb_out = _a_fwd(
        xs, z, ln_g, ln_b, ws, bs_t, g_kv, bkv, rc, rs1, rs2, a_w_out[0], w_kv, [b_w_in[0], b_w_out[0]])
    wa_out = wa_out.reshape(AW, D)
    wkv = wkv.reshape(D, 2 * LANES)
    wb_out = wb_out.reshape(-1, D)
    q, g2, o, dh2, dh2_b, loss, d_gf = _b_fwd(h1, b_norm_g, wb_in, b_bq, rc, rs1, rs2, k4, vt, sinks, wb_out, g_f, tgt)
    dh1p, dz2, n2, y2, dk, dv, d_bq, d_gb, d_sink = _b_bwd(dh2, h1, q, g2, o, k4, v4, kt, sinks, wb_out, wb_in,
                                                           b_norm_g, rc, rs1, rs2)
    d_sink = d_sink[:, :4].reshape(2, 2, 4).transpose(0, 2, 1).reshape(1, 16)
    gw_b_in = _wgrad(n2, dz2, N_DEV, "wgrad_b_in", bt=2048)
    gw_b_out = _wgrad(y2, dh2_b, 1, "wgrad_b_out").reshape(N_DEV, -1, D)
    (dz, y, nkv, dkv, dh1, dh1_f, d_gkv, d_bkv, d_lng, d_lnb, d_ws, d_bst, r_b_in, r_b_out) = _a_bwd(
        dh1p, dk, dv, h1, g_kv, wkv, wa_out, ws, ln_g, ln_b, z, sv, vhat, rstd, rc, rs1, rs2, [gw_b_in, gw_b_out])
    gw_a_out = _wgrad(y, dh1, 1, "wgrad_a_out").reshape(N_DEV, AW // N_DEV, D)
    gw_kv = _wgrad(nkv, dkv, 1, "wgrad_kv").reshape(N_DEV, D // N_DEV, 2 * LANES)
    dx, n1, d_ga, r_a_out, r_kv = _a_in_bwd(dz, wa_in, xs, dh1_f, g_a, [gw_a_out, gw_kv])
    small = [d_ws, d_bst[:, :G].T, d_gkv, d_bkv, d_gb, d_bq, d_sink, d_gf, d_ga, d_lng, d_lnb, loss]
    used = sum(_nrows(a.size) for a in small)
    per = -(-used // (SUBLANES * N_DEV)) * SUBLANES
    small_pack = _pack(small, per * N_DEV).reshape(N_DEV, per, LANES)
    r_a_in, r_small = _wgrad_exchange(n1, dz, me.reshape(1), [small_pack], "wgrad_a_in")

    g_a_in, d_a_in, nm_a_in, nv_a_in = _sum_adam(r_a_in, a_w_in[0], m_a_w_in[0], v_a_w_in[0], "adam_a_in")
    g_a_out, d_a_out, nm_a_out, nv_a_out = _sum_adam(r_a_out, a_w_out[0], m_a_w_out[0], v_a_w_out[0], "adam_a_out")
    g_kvw, d_kvw, nm_kvw, nv_kvw = _sum_adam(r_kv, w_kv, m_w_kv, v_w_kv, "adam_kv")
    g_b_in, d_b_in, nm_b_in, nv_b_in = _sum_adam(r_b_in, b_w_in[0], m_b_w_in[0], v_b_w_in[0], "adam_b_in")
    g_b_out, d_b_out, nm_b_out, nv_b_out = _sum_adam(r_b_out, b_w_out[0], m_b_w_out[0], v_b_w_out[0], "adam_b_out")

    red = _sum8(r_small, "sum_small")
    (full_small,) = _all_gather([red], [F32], "gather_small")
    full_small = full_small.reshape(N_DEV * per, LANES)
    rep_shapes = [a_ws.shape, a_bs.shape, kv_norm_g.shape, b_kv.shape, b_norm_g.shape, b_bq.shape, b_sinks.shape,
                  final_norm_g.shape]
    gs = _unpack(full_small, rep_shapes + [(N_DEV, a_norm_g.shape[1]), (N_DEV, a_ln_g.shape[1]),
                                           (N_DEV, a_ln_b.shape[1]), (1, 1)])
    loss = gs.pop()[0, 0]
    g_ang = lax.dynamic_slice_in_dim(gs[8], me, 1, axis=0)
    g_alng = lax.dynamic_slice_in_dim(gs[9], me, 1, axis=0)
    g_alnb = lax.dynamic_slice_in_dim(gs[10], me, 1, axis=0)
    sm_g = gs[:8] + [g_ang, g_alng, g_alnb]
    sm_shapes = [a.shape for a in sm_g]
    tot = sum(_nrows(a.size) for a in sm_g)
    pw = _pack([a_ws, a_bs, kv_norm_g, b_kv, b_norm_g, b_bq, b_sinks, final_norm_g, a_norm_g, a_ln_g, a_ln_b], tot)
    pm = _pack([m_a_ws, m_a_bs, m_kv_norm_g, m_b_kv, m_b_norm_g, m_b_bq, m_b_sinks, m_final_norm_g, m_a_norm_g,
                m_a_ln_g, m_a_ln_b], tot)
    pv = _pack([v_a_ws, v_a_bs, v_kv_norm_g, v_b_kv, v_b_norm_g, v_b_bq, v_b_sinks, v_final_norm_g, v_a_norm_g,
                v_a_ln_g, v_a_ln_b], tot)
    pg = _pack(sm_g, tot)
    pd, pnm, pnv = _adam_only(pg, pw, pm, pv, "adam_small")
    sd, snm, snv = _unpack(pd, sm_shapes), _unpack(pnm, sm_shapes), _unpack(pnv, sm_shapes)

    def order(big, sm):
        a_in, a_out, kvw, b_in, b_out = big
        ws_, bs_, kvg, bkv_, bng, bq_, snk, fng, ang, alng, alnb = sm
        return (ang, a_in[None], alng, alnb, ws_, bs_, a_out[None], kvg, kvw, bkv_, bng, b_in[None], bq_, snk,
                b_out[None], fng)

    grads = order((g_a_in, g_a_out, g_kvw, g_b_in, g_b_out), sm_g)
    deltas = order((d_a_in, d_a_out, d_kvw, d_b_in, d_b_out), sd)
    new_m = order((nm_a_in, nm_a_out, nm_kvw, nm_b_in, nm_b_out), snm)
    new_v = order((nv_a_in, nv_a_out, nv_kvw, nv_b_in, nv_b_out), snv)
    return (loss, dx[None], *grads, *deltas, *new_m, *new_v)
```

```python
import functools
import math

import jax
import jax.numpy as jnp
import numpy as np
from jax import lax
from jax.experimental import pallas as pl
from jax.experimental.pallas import tpu as pltpu

CHUNK = 128
HEAD_DIM = 64
ROPE_THETA = 10000.0
EPS = 1e-5
ADAM_LR = 0.001
ADAM_B1 = 0.9
ADAM_B2 = 0.999
ADAM_EPS = 1e-08
ADAM_WD = 0.01
ADAM_STEP = 10
N_DEV = 8
LANES = 128
NEG = -1e30

BF = jnp.bfloat16
F32 = jnp.float32
MESH = pl.DeviceIdType.MESH
AXES = ("x", "y", "c")
VMEM_LIMIT = 56 * 1024 * 1024


def _dot(a, b):
    return jnp.dot(a, b, preferred_element_type=F32)


def _dot_nt(a, b):
    return lax.dot_general(a, b, (((1,), (1,)), ((), ())), preferred_element_type=F32)


def _dot_tn(a, b):
    return lax.dot_general(a, b, (((0,), (0,)), ((), ())), preferred_element_type=F32)


def _const_spec(shape):
    nd = len(shape)
    return pl.BlockSpec(shape, lambda *_: (0,) * nd, pipeline_mode=pl.Buffered(1))


def _acc_spec(shape):
    nd = len(shape)
    return pl.BlockSpec(shape, lambda *_: (0,) * nd)


def _row_spec(tm, width):
    return pl.BlockSpec((tm, width), lambda i: (i, 0))


def _col_spec(tm, height):
    return pl.BlockSpec((height, tm), lambda i: (0, i))


def _params(sem):
    return pltpu.CompilerParams(dimension_semantics=sem, vmem_limit_bytes=VMEM_LIMIT)


def _rot(x, c, s1, s2):
    return x * c + pltpu.roll(x, 96, 1) * s1 + pltpu.roll(x, 32, 1) * s2


def _rot_bwd(d, c, s1, s2):
    return d * c + pltpu.roll(d * s1, 32, 1) + pltpu.roll(d * s2, 96, 1)


def _silu_parts(g):
    sg = jax.nn.sigmoid(g)
    return g * sg, sg * (1.0 + g * (1.0 - sg))


def _rms_bwd(dn, xh, r, g):
    a = dn * g
    return r * (a - xh * jnp.mean(a * xh, axis=-1, keepdims=True))


def _lane_lo(shape):
    return lax.broadcasted_iota(jnp.int32, shape, 1) < HEAD_DIM


def _split4(t):
    lo = _lane_lo(t.shape)
    tr = pltpu.roll(t, HEAD_DIM, 1)
    z = jnp.zeros_like(t)
    return jnp.concatenate([jnp.where(lo, t, z), jnp.where(lo, z, tr), jnp.where(lo, tr, z), jnp.where(lo, z, t)], axis=1)


def _stack_pairs(t, h):
    return jnp.concatenate([t[:, (h * 4 + j) * LANES:(h * 4 + j + 1) * LANES] for j in range(4)], axis=0)


def _upper():
    shape = (CHUNK, 4 * CHUNK)
    return lax.broadcasted_iota(jnp.int32, shape, 0) > (lax.broadcasted_iota(jnp.int32, shape, 1) & (CHUNK - 1))


def _band_rows(ref, prev, cur, h):
    a = slice(2 * h * LANES, (2 * h + 1) * LANES)
    b = slice((2 * h + 1) * LANES, (2 * h + 2) * LANES)
    return jnp.concatenate([ref[pl.ds(prev, CHUNK), a], ref[pl.ds(cur, CHUNK), a],
                            ref[pl.ds(prev, CHUNK), b], ref[pl.ds(cur, CHUNK), b]], axis=0)


def _band_cols(ref, pci, ci, h):
    a = slice(2 * h * LANES, (2 * h + 1) * LANES)
    b = slice((2 * h + 1) * LANES, (2 * h + 2) * LANES)
    return jnp.concatenate([ref[pci, a, :], ref[ci, a, :], ref[pci, b, :], ref[ci, b, :]], axis=1)


def _fold(t, upper, has_prev=None):
    out = []
    for k in range(2):
        prev = t[2 * k * CHUNK:(2 * k + 1) * CHUNK]
        if has_prev is not None:
            prev = jnp.where(has_prev, prev, NEG)
        out.append(jnp.where(upper, prev, t[(2 * k + 1) * CHUNK:(2 * k + 2) * CHUNK]))
    return out


def _unfold(fa, fb, upper):
    z = jnp.zeros_like(fa)
    return jnp.concatenate([jnp.where(upper, fa, z), jnp.where(upper, z, fa),
                            jnp.where(upper, fb, z), jnp.where(upper, z, fb)], axis=0)


def _softmax_sink(f, sink):
    m = jnp.maximum(jnp.max(f, axis=0, keepdims=True), sink)
    p = jnp.exp(f - m)
    es = jnp.exp(sink - m)
    inv = 1.0 / (jnp.sum(p, axis=0, keepdims=True) + es)
    return p * inv, es * inv


class _Riding:
    def __init__(self, shards, gathered, stages, sems, n_steps):
        self.shards, self.stages, self.n_steps = shards, stages, n_steps
        ssem, rsem, lsem = sems
        self.gathers = [_TwoLevel(stages[k], gathered[k], ssem.at[k], rsem.at[k], lsem.at[k])
                        for k in range(len(shards))]

    def begin(self, i):
        @pl.when(i == 0)
        def _():
            for shard, stage, g in zip(self.shards, self.stages, self.gathers):
                stage[...] = shard[...].astype(stage.dtype)
                g.start()

    def end(self, i):
        @pl.when(i == self.n_steps // 2)
        def _():
            for g in self.gathers:
                g.forward()

        @pl.when(i == self.n_steps - 1)
        def _():
            for g in self.gathers:
                g.finish()

    @staticmethod
    def specs(later):
        nl = len(later)
        hbm = pl.BlockSpec(memory_space=pl.ANY)
        return ([_const_spec(w.shape) for w in later], [hbm] * nl,
                tuple(jax.ShapeDtypeStruct((N_DEV,) + w.shape, BF) for w in later),
                [pltpu.VMEM(w.shape, BF) for w in later] + _direct_sems(nl))


PASS_MASKS = (0, 1, 4, 2, 5, 3, 6, 7)


def _in_proj(x, w_shard, vec_shard, slots, later):
    T, D = x.shape
    SH = w_shard.shape[1]
    VW = vec_shard.shape[1]
    TM = min(512, T)
    nT = T // TM
    nl = len(later)
    ds = D // N_DEV
    last = N_DEV - 1

    def body(slots_ref, x_ref, wsh_ref, vsh_ref, *rest):
        shards, rest = rest[:nl], rest[nl:]
        (z_ref, wout_ref, vout_ref), rest = rest[:3], rest[3:]
        gathered, rest = rest[:nl], rest[nl:]
        (w_scr, vec_scr, vstage, n1_scr, ga_scr, w_s, w_r, w_l, v_s, v_r, v_l), rest = rest[:11], rest[11:]
        stages, sems = rest[:nl], rest[nl:]
        p, i = pl.program_id(0), pl.program_id(1)
        me = _my_index()
        wg = _TwoLevel(w_scr.at[me], w_scr, w_s, w_r, w_l, own=False)
        vg = _TwoLevel(vstage, vec_scr, v_s, v_r, v_l)
        lg = [_TwoLevel(stages[k], gathered[k], sems[0].at[k], sems[1].at[k], sems[2].at[k]) for k in range(nl)]
        w_copy = pltpu.make_async_copy(w_scr, wout_ref, w_l)

        def at_pass(k):
            return (p == k) & (i == 0)

        @pl.when(at_pass(0))
        def _():
            vstage[...] = vsh_ref[...]
            vg.start()
            w_scr[me] = wsh_ref[...].astype(BF)
            wg.start()
            for k in range(nl):
                stages[k][...] = shards[k][...].astype(BF)
                lg[k].start()
            vg.forward()
            vg.finish()
            for j in range(N_DEV):
                ga_scr[:, j * ds:(j + 1) * ds] = vec_scr[j, 0:1, 0:ds]
            vout_ref[...] = vec_scr[...]

        @pl.when(at_pass(1))
        def _():
            wg.wait_sibling()

        for k, j in ((2, 0), (3, 1), (6, 2)):
            @pl.when(at_pass(k))
            def _(j=j):
                wg.wait_chip_and_forward(j)

        for k, j in ((4, 0), (5, 1), (7, 2)):
            @pl.when(at_pass(k))
            def _(j=j):
                wg.wait_passed(j)

        @pl.when(at_pass(last))
        def _():
            w_copy.start()

        @pl.when(p == 0)
        def _():
            xv = x_ref[...]
            r1 = lax.rsqrt(jnp.mean(xv * xv, axis=-1, keepdims=True) + EPS)
            n1_scr[i] = (xv * r1 * ga_scr[...]).astype(BF)

        z_ref[...] = _dot(n1_scr[i], w_scr[slots_ref[p]]).astype(BF)

        @pl.when((p == last) & (i == nT - 1))
        def _():
            wg.wait_sends()
            for g in lg:
                g.forward()
            for g in lg:
                g.finish()
            w_copy.wait()

    hbm = pl.BlockSpec(memory_space=pl.ANY)
    dma = pltpu.SemaphoreType.DMA
    S = jax.ShapeDtypeStruct
    grid_spec = pltpu.PrefetchScalarGridSpec(
        num_scalar_prefetch=1, grid=(N_DEV, nT),
        in_specs=[pl.BlockSpec((TM, D), lambda p, i, s: (jnp.where(p == 0, i, nT - 1), 0)),
                  pl.BlockSpec(w_shard.shape, lambda p, i, s: (0, 0), pipeline_mode=pl.Buffered(1)),
                  pl.BlockSpec(vec_shard.shape, lambda p, i, s: (0, 0), pipeline_mode=pl.Buffered(1))]
        + [pl.BlockSpec(w.shape, lambda p, i, s: (0, 0), pipeline_mode=pl.Buffered(1)) for w in later],
        out_specs=[pl.BlockSpec((TM, SH), lambda p, i, s: (i, s[p])), hbm,
                   pl.BlockSpec((N_DEV,) + vec_shard.shape, lambda p, i, s: (0, 0, 0))] + [hbm] * nl,
        scratch_shapes=[pltpu.VMEM((N_DEV, D, SH), BF), pltpu.VMEM((N_DEV,) + vec_shard.shape, F32),
                        pltpu.VMEM(vec_shard.shape, F32), pltpu.VMEM((nT, TM, D), BF), pltpu.VMEM((1, D), F32),
                        dma((7,)), dma((7,)), dma, dma((7,)), dma((7,)), dma]
        + [pltpu.VMEM(w.shape, BF) for w in later] + _direct_sems(nl))
    return pl.pallas_call(
        body, name="a_in_proj", grid_spec=grid_spec,
        out_shape=(S((T, N_DEV * SH), BF), S((N_DEV, D, SH), BF), S((N_DEV,) + vec_shard.shape, F32))
        + tuple(S((N_DEV,) + w.shape, BF) for w in later),
        compiler_params=_params(("arbitrary", "arbitrary")),
    )(slots, x, w_shard, vec_shard, *later)


def _a_fwd(x, z, ln_g, ln_b, ws, bs_t, wa_out, g_kv, w_kv, b_kv, rc, rs1, rs2, later):
    T, D = x.shape
    AW = wa_out.shape[0]
    G = ws.shape[0]
    TM = min(256, T)
    nT = T // TM
    nC = TM // CHUNK
    nl = len(later)

    def body(x_ref, u_ref, v_ref, gt_ref, lng_ref, lnb_ref, ws_ref, bst_ref, waout_ref, gkv_ref, wkv_ref, bkv_ref,
             rc_ref, rs1_ref, rs2_ref, *rest):
        shards, rest = rest[:nl], rest[nl:]
        (h1_ref, sv_ref, vhat_ref, rstd_ref, k4_ref, v4_ref, kt_ref, vt_ref), rest = rest[:8], rest[8:]
        gathered, sv_scr, stages, sems = rest[:nl], rest[nl], rest[nl + 1:2 * nl + 1], rest[2 * nl + 1:]
        i = pl.program_id(0)
        riding = _Riding(shards, gathered, stages, sems, nT)
        riding.begin(i)
        xv = x_ref[...]
        u = u_ref[...].astype(F32)
        v = v_ref[...].astype(F32)
        gt = gt_ref[...].astype(F32)
        mu = jnp.mean(v, axis=-1, keepdims=True)
        xc = v - mu
        rstd = lax.rsqrt(jnp.mean(xc * xc, axis=-1, keepdims=True) + EPS)
        vhat = xc * rstd
        vln = (vhat * lng_ref[...] + lnb_ref[...]).astype(BF)
        tri = lax.broadcasted_iota(jnp.int32, (CHUNK, CHUNK), 0) >= lax.broadcasted_iota(jnp.int32, (CHUNK, CHUNK), 1)
        for g in range(G):
            wsm = jnp.where(tri, ws_ref[g], 0.0).astype(BF)
            bias = bst_ref[:, g:g + 1]
            for c in range(nC):
                blk = vln[c * CHUNK:(c + 1) * CHUNK, g * CHUNK:(g + 1) * CHUNK]
                sv_scr[c * CHUNK:(c + 1) * CHUNK, g * CHUNK:(g + 1) * CHUNK] = _dot(wsm, blk) + bias
        sv = sv_scr[...]
        silu, _ = _silu_parts(gt)
        y = (u * sv * silu).astype(BF)
        h1 = xv + _dot(y, waout_ref[...])
        h1_ref[...] = h1
        sv_ref[...] = sv.astype(BF)
        vhat_ref[...] = vhat.astype(BF)
        rstd_ref[...] = jnp.broadcast_to(rstd, rstd_ref.shape)
        rkv = lax.rsqrt(jnp.mean(h1 * h1, axis=-1, keepdims=True) + EPS)
        nkv = (h1 * rkv * gkv_ref[...]).astype(BF)
        kv = _dot(nkv, wkv_ref[...]) + bkv_ref[...]
        k_rot = _rot(kv[:, :LANES], rc_ref[...], rs1_ref[...], rs2_ref[...])
        for src, ref, tref in ((k_rot, k4_ref, kt_ref), (kv[:, LANES:], v4_ref, vt_ref)):
            t4 = _split4(src)
            ref[...] = t4.astype(BF)
            for c in range(nC):
                for b in range(4):
                    blk = t4[c * CHUNK:(c + 1) * CHUNK, b * LANES:(b + 1) * LANES]
                    tref[c, b * LANES:(b + 1) * LANES, :] = blk.T.astype(BF)
        riding.end(i)

    row = functools.partial(_row_spec, TM)
    zcol = [pl.BlockSpec((TM, AW), functools.partial(lambda k, i: (i, k), k)) for k in range(3)]
    tr = pl.BlockSpec((nC, 4 * LANES, CHUNK), lambda i: (i, 0, 0))
    r_in, r_out, r_shape, r_scratch = _Riding.specs(later)
    S = jax.ShapeDtypeStruct
    return pl.pallas_call(
        body, name="a_fwd", grid=(nT,),
        in_specs=[row(D)] + zcol + [_const_spec((1, AW)), _const_spec((1, AW)),
                  _const_spec(ws.shape), _const_spec(bs_t.shape), _const_spec(wa_out.shape), _const_spec((1, D)),
                  _const_spec(w_kv.shape), _const_spec((1, 2 * LANES)), row(LANES), row(LANES), row(LANES)] + r_in,
        out_specs=[row(D), row(AW), row(AW), row(LANES), row(4 * LANES), row(4 * LANES), tr, tr] + r_out,
        out_shape=(S((T, D), F32), S((T, AW), BF), S((T, AW), BF), S((T, LANES), F32),
                   S((T, 4 * LANES), BF), S((T, 4 * LANES), BF),
                   S((T // CHUNK, 4 * LANES, CHUNK), BF), S((T // CHUNK, 4 * LANES, CHUNK), BF)) + r_shape,
        scratch_shapes=[pltpu.VMEM((TM, AW), F32)] + r_scratch,
        compiler_params=_params(("arbitrary",)),
    )(x, z, z, z, ln_g, ln_b, ws, bs_t, wa_out, g_kv, w_kv, b_kv, rc, rs1, rs2, *later)


def _b_fwd(h1, g_b, wb_in, bq, rc, rs1, rs2, k4, vt, sinks, wb_out, g_f, target):
    T, D = h1.shape
    BW = wb_out.shape[0]
    SH = wb_in.shape[2]
    TM = min(256, T)
    nC = TM // CHUNK
    nP = BW // LANES

    def body(h1_ref, gb_ref, wbin_ref, bq_ref, rc_ref, rs1_ref, rs2_ref, k4_ref, vt_ref, sink_ref, wbout_ref, gf_ref,
             tgt_ref, q_ref, g2_ref, o_ref, dh2_ref, dh2b_ref, loss_ref, dgf_ref, z_scr, o_scr):
        i = pl.program_id(0)
        h1v = h1_ref[...]
        r2 = lax.rsqrt(jnp.mean(h1v * h1v, axis=-1, keepdims=True) + EPS)
        n2 = (h1v * r2 * gb_ref[...]).astype(BF)
        for j in range(N_DEV):
            z_scr[:, j * SH:(j + 1) * SH] = _dot(n2, wbin_ref[j])
        c_t, s1_t, s2_t = rc_ref[...], rs1_ref[...], rs2_ref[...]
        for p in range(nP):
            cols = slice(p * LANES, (p + 1) * LANES)
            qp = _rot(z_scr[:, cols] + bq_ref[:, cols], c_t, s1_t, s2_t) * (HEAD_DIM ** -0.5)
            q_ref[:, cols] = qp.astype(BF)
        g2 = z_scr[:, BW:]
        g2_ref[...] = g2.astype(BF)
        upper = _upper()
        for c in range(nC):
            ci = i * nC + c
            rows = slice(c * CHUNK, (c + 1) * CHUNK)
            pci = jnp.maximum(ci - 1, 0)
            prev = pl.multiple_of(pci * CHUNK, CHUNK)
            cur = pl.multiple_of(ci * CHUNK, CHUNK)
            qc = q_ref[rows, :]
            for h in range(2):
                st = _dot_nt(_band_rows(k4_ref, prev, cur, h), _stack_pairs(qc, h))
                fa, fb = _fold(st, upper, ci > 0)
                pa, _ = _softmax_sink(fa, sink_ref[2 * h:2 * h + 1, :])
                pb, _ = _softmax_sink(fb, sink_ref[2 * h + 1:2 * h + 2, :])
                ot = _dot(_band_cols(vt_ref, pci, ci, h), _unfold(pa, pb, upper).astype(BF))
                for j in range(4):
                    o_scr[rows, (h * 4 + j) * LANES:(h * 4 + j + 1) * LANES] = ot[:, j * CHUNK:(j + 1) * CHUNK].T
        o = o_scr[...]
        o_ref[...] = o.astype(BF)
        silu, _ = _silu_parts(g2)
        h2 = h1v + _dot((o * silu).astype(BF), wbout_ref[...])
        rf = lax.rsqrt(jnp.mean(h2 * h2, axis=-1, keepdims=True) + EPS)
        xh = h2 * rf
        gf = gf_ref[...]
        err = xh * gf - tgt_ref[...]
        dyf = err * (1.0 / D)
        dh2 = _rms_bwd(dyf, xh, rf, gf)
        dh2_ref[...] = dh2
        dh2b_ref[...] = dh2.astype(BF)

        @pl.when(i == 0)
        def _():
            loss_ref[...] = jnp.zeros_like(loss_ref)
            dgf_ref[...] = jnp.zeros_like(dgf_ref)

        loss_ref[...] += 0.5 * jnp.sum(jnp.mean(err * err, axis=-1, keepdims=True), axis=0, keepdims=True)
        dgf_ref[...] += jnp.sum(dyf * xh, axis=0, keepdims=True)

    row = functools.partial(_row_spec, TM)
    S = jax.ShapeDtypeStruct
    return pl.pallas_call(
        body, name="b_fwd", grid=(T // TM,),
        in_specs=[row(D), _const_spec((1, D)), _const_spec(wb_in.shape), _const_spec((1, BW)), row(LANES), row(LANES),
                  row(LANES), _const_spec(k4.shape), _const_spec(vt.shape), _const_spec(sinks.shape),
                  _const_spec(wb_out.shape), _const_spec((1, D)), row(D)],
        out_specs=[row(BW), row(BW), row(BW), row(D), row(D), _acc_spec((1, 1)), _acc_spec((1, D))],
        out_shape=(S((T, BW), BF), S((T, BW), BF), S((T, BW), BF), S((T, D), F32), S((T, D), BF), S((1, 1), F32),
                   S((1, D), F32)),
        scratch_shapes=[pltpu.VMEM((TM, 2 * BW), F32), pltpu.VMEM((TM, BW), F32)],
        compiler_params=_params(("arbitrary",)),
    )(h1, g_b, wb_in, bq, rc, rs1, rs2, k4, vt, sinks, wb_out, g_f, target)


def _b_bwd(dh2, h1, q, g2, o, k4, v4, kt, sinks, wb_out, wb_in, g_b, rc, rs1, rs2):
    T, D = h1.shape
    BW = wb_out.shape[0]
    SH = wb_in.shape[2]
    TM = min(256, T)
    nT = T // TM
    nC = TM // CHUNK
    nP = BW // LANES

    def body(dh2_ref, h1_ref, q_ref, g2_ref, o_ref, k4_ref, v4_ref, kt_ref, sink_ref, wbout_ref, wbin_ref, gb_ref,
             rc_ref, rs1_ref, rs2_ref,
             dh1_ref, dz2_ref, n2_ref, y2_ref, dk_ref, dv_ref, dbq_ref, dgb_ref, dsink_ref, do_scr, dq_scr, dsacc_scr):
        i = pl.program_id(0)

        @pl.when(i == 0)
        def _():
            dk_ref[...] = jnp.zeros_like(dk_ref)
            dv_ref[...] = jnp.zeros_like(dv_ref)
            dbq_ref[...] = jnp.zeros_like(dbq_ref)
            dgb_ref[...] = jnp.zeros_like(dgb_ref)
            dsacc_scr[...] = jnp.zeros_like(dsacc_scr)

        dh2 = dh2_ref[...]
        dy2 = _dot_nt(dh2.astype(BF), wbout_ref[...]).astype(BF)
        silu, dsilu = _silu_parts(g2_ref[...].astype(F32))
        silu, dsilu = silu.astype(BF), dsilu.astype(BF)
        ob = o_ref[...]
        y2_ref[...] = (ob * silu).T
        do_scr[...] = dy2 * silu
        dz2_ref[:, BW:] = dy2 * ob * dsilu
        upper = _upper()
        lo = _lane_lo((2 * CHUNK, LANES))
        for c in range(nC):
            ci = i * nC + c
            rows = slice(c * CHUNK, (c + 1) * CHUNK)
            pci = jnp.maximum(ci - 1, 0)
            prev = pl.multiple_of(pci * CHUNK, CHUNK)
            cur = pl.multiple_of(ci * CHUNK, CHUNK)
            qc = q_ref[rows, :]
            doc = do_scr[rows, :]
            dkb = jnp.zeros((2 * CHUNK, LANES), F32)
            dvb = jnp.zeros((2 * CHUNK, LANES), F32)
            for h in range(2):
                qs = _stack_pairs(qc, h)
                dos = _stack_pairs(doc, h)
                fa, fb = _fold(_dot_nt(_band_rows(k4_ref, prev, cur, h), qs), upper, ci > 0)
                dfa, dfb = _fold(_dot_nt(_band_rows(v4_ref, prev, cur, h), dos), upper)
                folded = []
                for k, (f, df) in enumerate(((fa, dfa), (fb, dfb))):
                    p, ps = _softmax_sink(f, sink_ref[2 * h + k:2 * h + k + 1, :])
                    delta = jnp.sum(p * df, axis=0, keepdims=True)
                    dsacc_scr[2 * h + k:2 * h + k + 1, :] -= ps * delta
                    folded.append((p, p * (df - delta)))
                pt = _unfold(folded[0][0], folded[1][0], upper).astype(BF)
                dst = _unfold(folded[0][1], folded[1][1], upper).astype(BF)
                dqt = _dot(_band_cols(kt_ref, pci, ci, h), dst)
                for j in range(4):
                    dq_scr[rows, (h * 4 + j) * LANES:(h * 4 + j + 1) * LANES] = dqt[:, j * CHUNK:(j + 1) * CHUNK].T
                for acc_name, g in (("k", _dot(dst, qs)), ("v", _dot(pt, dos))):
                    a, b = g[:2 * CHUNK], g[2 * CHUNK:]
                    if h == 0:
                        part = jnp.where(lo, a + pltpu.roll(b, HEAD_DIM, 1), 0.0)
                    else:
                        part = jnp.where(lo, 0.0, pltpu.roll(a, HEAD_DIM, 1) + b)
                    if acc_name == "k":
                        dkb += part
                    else:
                        dvb += part
            dk_ref[pl.ds(prev, CHUNK), :] += dkb[:CHUNK]
            dk_ref[pl.ds(cur, CHUNK), :] += dkb[CHUNK:]
            dv_ref[pl.ds(prev, CHUNK), :] += dvb[:CHUNK]
            dv_ref[pl.ds(cur, CHUNK), :] += dvb[CHUNK:]

        @pl.when(i == nT - 1)
        def _():
            lane = lax.broadcasted_iota(jnp.int32, dsink_ref.shape, 1)
            tot = jnp.zeros(dsink_ref.shape, F32)
            for j in range(4):
                tot += jnp.where(lane == j, jnp.sum(dsacc_scr[:, j * CHUNK:(j + 1) * CHUNK], axis=1, keepdims=True), 0.0)
            dsink_ref[...] = tot
        c_t, s1_t, s2_t = rc_ref[...], rs1_ref[...], rs2_ref[...]
        for p in range(nP):
            cols = slice(p * LANES, (p + 1) * LANES)
            dqp = _rot_bwd(dq_scr[:, cols] * (HEAD_DIM ** -0.5), c_t, s1_t, s2_t)
            dbq_ref[:, cols] += jnp.sum(dqp, axis=0, keepdims=True)
            dz2_ref[:, cols] = dqp.astype(BF)
        h1v = h1_ref[...]
        r2 = lax.rsqrt(jnp.mean(h1v * h1v, axis=-1, keepdims=True) + EPS)
        xh = h1v * r2
        gb = gb_ref[...]
        n2_ref[...] = (xh * gb).astype(BF).T
        dn2 = None
        for j in range(N_DEV):
            part = _dot_nt(dz2_ref[:, j * SH:(j + 1) * SH], wbin_ref[j])
            dn2 = part if dn2 is None else dn2 + part
        dgb_ref[...] += jnp.sum(dn2 * xh, axis=0, keepdims=True)
        dh1_ref[...] = dh2 + _rms_bwd(dn2, xh, r2, gb)

    row = functools.partial(_row_spec, TM)
    S = jax.ShapeDtypeStruct
    return pl.pallas_call(
        body, name="b_bwd", grid=(T // TM,),
        in_specs=[row(D), row(D), row(BW), row(BW), row(BW), _const_spec(k4.shape), _const_spec(v4.shape),
                  _const_spec(kt.shape), _const_spec(sinks.shape), _const_spec(wb_out.shape), _const_spec(wb_in.shape),
                  _const_spec((1, D)), row(LANES), row(LANES), row(LANES)],
        out_specs=[row(D), row(2 * BW), _col_spec(TM, D), _col_spec(TM, BW), _acc_spec((T, LANES)),
                   _acc_spec((T, LANES)), _acc_spec((1, BW)), _acc_spec((1, D)), _acc_spec((4, LANES))],
        out_shape=(S((T, D), F32), S((T, 2 * BW), BF), S((D, T), BF), S((BW, T), BF), S((T, LANES), F32),
                   S((T, LANES), F32), S((1, BW), F32), S((1, D), F32), S((4, LANES), F32)),
        scratch_shapes=[pltpu.VMEM((TM, BW), BF), pltpu.VMEM((TM, BW), F32), pltpu.VMEM((4, 4 * CHUNK), F32)],
        compiler_params=_params(("arbitrary",)),
    )(dh2, h1, q, g2, o, k4, v4, kt, sinks, wb_out, wb_in, g_b, rc, rs1, rs2)


def _a_bwd(dh1p, dk, dv, h1, g_kv, w_kv, wa_out, ws, ln_g, ln_b, z, sv, vhat, rstd, rc, rs1, rs2, ready):
    T, D = h1.shape
    AW = wa_out.shape[0]
    G = ws.shape[0]
    TM = min(256, T)
    nT = T // TM
    nC = TM // CHUNK
    nr = len(ready)

    def body(dh1p_ref, dk_ref, dv_ref, h1_ref, gkv_ref, wkv_ref, waout_ref, ws_ref, lng_ref,
             lnb_ref, u_ref, gt_ref, sv_ref, vhat_ref, rstd_ref, rc_ref, rs1_ref, rs2_ref, *rest):
        ready_refs, rest = rest[:nr], rest[nr:]
        (dz_ref, y_ref, nkv_ref, dkv_ref, dh1_ref, dh1f_ref, dgkv_ref, dbkv_ref, dlng_ref, dlnb_ref,
         dws_ref, dbs_ref), rest = rest[:12], rest[12:]
        recv_refs, (dsv_scr, dvln_scr, ssem, rsem, lsem) = rest[:nr], rest[nr:]
        i = pl.program_id(0)
        exchanges = [_Direct(ready_refs[k], recv_refs[k], ssem.at[k], rsem.at[k], lsem.at[k], scatter=True)
                     for k in range(nr)]

        @pl.when(i == 0)
        def _():
            for e in exchanges:
                e.start()
            for r in (dgkv_ref, dbkv_ref, dlng_ref, dlnb_ref, dws_ref, dbs_ref):
                r[...] = jnp.zeros_like(r)

        dk_pre = _rot_bwd(dk_ref[...], rc_ref[...], rs1_ref[...], rs2_ref[...])
        dkv = jnp.concatenate([dk_pre, dv_ref[...]], axis=1)
        dbkv_ref[...] += jnp.sum(dkv, axis=0, keepdims=True)
        dkv_b = dkv.astype(BF)
        dkv_ref[...] = dkv_b
        h1v = h1_ref[...]
        rkv = lax.rsqrt(jnp.mean(h1v * h1v, axis=-1, keepdims=True) + EPS)
        xh_kv = h1v * rkv
        gkv = gkv_ref[...]
        nkv_ref[...] = (xh_kv * gkv).astype(BF).T
        dnkv = _dot_nt(dkv_b, wkv_ref[...])
        dgkv_ref[...] += jnp.sum(dnkv * xh_kv, axis=0, keepdims=True)
        dh1 = dh1p_ref[...] + _rms_bwd(dnkv, xh_kv, rkv, gkv)
        dh1_b = dh1.astype(BF)
        dh1_ref[...] = dh1_b
        dh1f_ref[...] = dh1
        dy = _dot_nt(dh1_b, waout_ref[...]).astype(BF)
        silu, dsilu = _silu_parts(gt_ref[...].astype(F32))
        silu, dsilu = silu.astype(BF), dsilu.astype(BF)
        ub, svb = u_ref[...], sv_ref[...]
        us = ub * silu
        dys = dy * svb
        y_ref[...] = (us * svb).T
        dz_ref[:, :AW] = dys * silu
        dz_ref[:, 2 * AW:] = dys * ub * dsilu
        dsv_scr[...] = dy * us
        vhat_v = vhat_ref[...].astype(F32)
        lng = lng_ref[...]
        vln_b = (vhat_v * lng + lnb_ref[...]).astype(BF)
        tri = lax.broadcasted_iota(jnp.int32, (CHUNK, CHUNK), 0) >= lax.broadcasted_iota(jnp.int32, (CHUNK, CHUNK), 1)
        lane = lax.broadcasted_iota(jnp.int32, (CHUNK, LANES), 1)
        dbs = jnp.zeros((CHUNK, LANES), F32)
        for g in range(G):
            wsm = jnp.where(tri, ws_ref[g], 0.0).astype(BF)
            cols = slice(g * CHUNK, (g + 1) * CHUNK)
            dws_g = None
            for c in range(nC):
                rows = slice(c * CHUNK, (c + 1) * CHUNK)
                dsv_cg = dsv_scr[rows, cols]
                dvln_scr[rows, cols] = _dot_tn(wsm, dsv_cg)
                part = _dot_nt(dsv_cg, vln_b[rows, cols])
                dws_g = part if dws_g is None else dws_g + part
                dbs += jnp.where(lane == g, jnp.sum(dsv_cg.astype(F32), axis=-1, keepdims=True), 0.0)
            dws_ref[g] += jnp.where(tri, dws_g, 0.0)
        dbs_ref[...] += dbs
        dvln = dvln_scr[...]
        dlng_ref[...] += jnp.sum(dvln * vhat_v, axis=0, keepdims=True)
        dlnb_ref[...] += jnp.sum(dvln, axis=0, keepdims=True)
        a = dvln * lng
        dvv = rstd_ref[:, 0:1] * (a - jnp.mean(a, axis=-1, keepdims=True)
                                  - vhat_v * jnp.mean(a * vhat_v, axis=-1, keepdims=True))
        dz_ref[:, AW:2 * AW] = dvv.astype(BF)

        @pl.when(i == nT - 1)
        def _():
            for e in exchanges:
                e.finish()

    row = functools.partial(_row_spec, TM)
    col = functools.partial(_col_spec, TM)
    hbm = pl.BlockSpec(memory_space=pl.ANY)
    S = jax.ShapeDtypeStruct
    return pl.pallas_call(
        body, name="a_bwd", grid=(nT,),
        in_specs=[row(D), row(LANES), row(LANES), row(D), _const_spec((1, D)), _const_spec(w_kv.shape),
                  _const_spec(wa_out.shape), _const_spec(ws.shape),
                  _const_spec((1, AW)), _const_spec((1, AW)), pl.BlockSpec((TM, AW), lambda i: (i, 0)),
                  pl.BlockSpec((TM, AW), lambda i: (i, 2)), row(AW), row(AW), row(LANES),
                  row(LANES), row(LANES), row(LANES)] + [hbm] * nr,
        out_specs=[row(3 * AW), col(AW), col(D), row(2 * LANES), row(D), row(D),
                   _acc_spec((1, D)), _acc_spec((1, 2 * LANES)), _acc_spec((1, AW)),
                   _acc_spec((1, AW)), _acc_spec(ws.shape), _acc_spec((CHUNK, LANES))] + [hbm] * nr,
        out_shape=(S((T, 3 * AW), BF), S((AW, T), BF), S((D, T), BF), S((T, 2 * LANES), BF), S((T, D), BF),
                   S((T, D), F32),
                   S((1, D), F32), S((1, 2 * LANES), F32), S((1, AW), F32), S((1, AW), F32),
                   S(ws.shape, F32), S((CHUNK, LANES), F32)) + tuple(S(r.shape, r.dtype) for r in ready),
        scratch_shapes=[pltpu.VMEM((TM, AW), BF), pltpu.VMEM((TM, AW), F32)] + _direct_sems(nr),
        compiler_params=_params(("arbitrary",)),
    )(dh1p, dk, dv, h1, g_kv, w_kv, wa_out, ws, ln_g, ln_b, z, z, sv, vhat, rstd, rc, rs1, rs2, *ready)


def _a_in_bwd(dz, wa_in, x, dh1, g_a, ready):
    T, D = x.shape
    SH = wa_in.shape[2]
    TM = min(512, T)
    nT = T // TM
    nr = len(ready)

    def body(dz_ref, wain_ref, x_ref, dh1_ref, ga_ref, *rest):
        ready_refs, (dx_ref, n1_ref, dga_ref), rest = rest[:nr], rest[nr:nr + 3], rest[nr + 3:]
        recv_refs, (ssem, rsem, lsem) = rest[:nr], rest[nr:]
        i = pl.program_id(0)
        exchanges = [_Direct(ready_refs[k], recv_refs[k], ssem.at[k], rsem.at[k], lsem.at[k], scatter=True)
                     for k in range(nr)]

        @pl.when(i == 0)
        def _():
            for e in exchanges:
                e.start()
            dga_ref[...] = jnp.zeros_like(dga_ref)

        xv = x_ref[...]
        r1 = lax.rsqrt(jnp.mean(xv * xv, axis=-1, keepdims=True) + EPS)
        xh = xv * r1
        ga = ga_ref[...]
        n1_ref[...] = (xh * ga).astype(BF).T
        dn1 = None
        for j in range(N_DEV):
            part = _dot_nt(dz_ref[:, j * SH:(j + 1) * SH], wain_ref[j])
            dn1 = part if dn1 is None else dn1 + part
        dga_ref[...] += jnp.sum(dn1 * xh, axis=0, keepdims=True)
        dx_ref[...] = dh1_ref[...] + _rms_bwd(dn1, xh, r1, ga)

        @pl.when(i == nT - 1)
        def _():
            for e in exchanges:
                e.finish()

    row = functools.partial(_row_spec, TM)
    hbm = pl.BlockSpec(memory_space=pl.ANY)
    S = jax.ShapeDtypeStruct
    return pl.pallas_call(
        body, name="a_in_bwd", grid=(nT,),
        in_specs=[row(dz.shape[1]), _const_spec(wa_in.shape), row(D), row(D), _const_spec((1, D))] + [hbm] * nr,
        out_specs=[row(D), _col_spec(TM, D), _acc_spec((1, D))] + [hbm] * nr,
        out_shape=(S((T, D), F32), S((D, T), BF), S((1, D), F32)) + tuple(S(r.shape, r.dtype) for r in ready),
        scratch_shapes=_direct_sems(nr),
        compiler_params=_params(("arbitrary",)),
    )(dz, wa_in, x, dh1, g_a, *ready)


def _wgrad(at, b, nblk, name, bt=512):
    K, T = at.shape
    N = b.shape[1] // nblk
    BT = min(bt, T)
    nt = T // BT

    def body(a_ref, b_ref, o_ref, acc):
        t = pl.program_id(1)

        @pl.when(t == 0)
        def _():
            acc[...] = jnp.zeros_like(acc)

        acc[...] += _dot(a_ref[...], b_ref[...])

        @pl.when(t == nt - 1)
        def _():
            o_ref[0] = acc[...].astype(BF)

    return pl.pallas_call(
        body, name=name, grid=(nblk, nt),
        in_specs=[pl.BlockSpec((K, BT), lambda j, t: (0, t)), pl.BlockSpec((BT, N), lambda j, t: (t, j))],
        out_specs=pl.BlockSpec((1, K, N), lambda j, t: (j, 0, 0)),
        out_shape=jax.ShapeDtypeStruct((nblk, K, N), BF),
        scratch_shapes=[pltpu.VMEM((K, N), F32)],
        compiler_params=_params(("arbitrary", "arbitrary")),
    )(at, b)


def _wgrad_exchange(a, b, me, extras, name):
    K, T = a.shape
    N = b.shape[1] // N_DEV
    BT = min(1024, T)
    nt = T // BT
    ne = len(extras)
    last = N_DEV - 1
    n_chip = N_DEV // 2

    def body(me_ref, a_ref, b_ref, *rest):
        ex_in, recv_ref, ex_out = rest[:ne], rest[ne], rest[ne + 1:2 * ne + 1]
        acc, dstage, istage, half, d_s, d_r, i_s, i_r, lsem, ex_ssem, ex_rsem, ex_lsem = rest[2 * ne + 1:]
        s, t = pl.program_id(0), pl.program_id(1)
        x, y, c = (lax.axis_index(ax) for ax in AXES)
        ex = [_Direct(ex_in[k], ex_out[k], ex_ssem.at[k], ex_rsem.at[k], ex_lsem.at[k], scatter=True) for k in range(ne)]

        def to_sibling(k, slot):
            return pltpu.make_async_remote_copy(src_ref=dstage.at[slot], dst_ref=half.at[k], send_sem=d_s.at[k],
                                                recv_sem=d_r.at[k], device_id=(x, y, 1 - c), device_id_type=MESH)

        def to_chip(k, slot, sender):
            far = n_chip - 1 - k
            px, py = x ^ ((far >> 1) & 1), y ^ (far & 1)
            dst = recv_ref.at[2 * x + y] if sender else recv_ref.at[2 * px + py]
            return pltpu.make_async_remote_copy(src_ref=istage.at[slot], dst_ref=dst, send_sem=i_s.at[k],
                                                recv_sem=i_r.at[k], device_id=(px, py, c), device_id_type=MESH)

        @pl.when((s == 0) & (t == 0))
        def _():
            for e in ex:
                e.start()

        @pl.when(t == 0)
        def _():
            acc[...] = jnp.zeros_like(acc)

        acc[...] += _dot(a_ref[...], b_ref[...])

        @pl.when(t == nt - 1)
        def _():
            k = lax.div(s, 2)
            slot = lax.rem(k, 2)

            @pl.when(lax.rem(s, 2) == 0)
            def _():
                @pl.when(k >= 2)
                def _():
                    to_sibling(k - 2, slot).wait_send()

                dstage[slot] = acc[...].astype(BF)
                to_sibling(k, slot).start()

            @pl.when(lax.rem(s, 2) == 1)
            def _():
                to_sibling(k, slot).wait_recv()

                @pl.when(k >= 2)
                def _():
                    to_chip(k - 2, slot, True).wait_send()

                istage[slot] = (acc[...] + half[k].astype(F32)).astype(BF)

                @pl.when(k < n_chip - 1)
                def _():
                    to_chip(k, slot, True).start()

            @pl.when(s == last)
            def _():
                own = pltpu.make_async_copy(istage.at[slot], recv_ref.at[2 * x + y], lsem)
                own.start()
                to_chip(n_chip - 2, 0, True).wait_send()
                to_sibling(n_chip - 2, 0).wait_send()
                to_sibling(n_chip - 1, 1).wait_send()
                for kk in range(n_chip - 1):
                    to_chip(kk, 0, False).wait_recv()
                own.wait()
                for e in ex:
                    e.finish()

    hbm = pl.BlockSpec(memory_space=pl.ANY)
    dma = pltpu.SemaphoreType.DMA
    grid_spec = pltpu.PrefetchScalarGridSpec(
        num_scalar_prefetch=1, grid=(N_DEV, nt),
        in_specs=[pl.BlockSpec((K, BT), lambda s, t, me_ref: (0, t)),
                  pl.BlockSpec((BT, N), lambda s, t, me_ref: (t, me_ref[0] ^ (last - s)))] + [hbm] * ne,
        out_specs=[hbm] * (ne + 1),
        scratch_shapes=[pltpu.VMEM((K, N), F32), pltpu.VMEM((2, K, N), BF), pltpu.VMEM((2, K, N), BF),
                        pltpu.VMEM((n_chip, K, N), BF), dma((n_chip,)), dma((n_chip,)), dma((n_chip - 1,)),
                        dma((n_chip - 1,)), dma] + _direct_sems(ne))
    return pl.pallas_call(
        body, name=name, grid_spec=grid_spec,
        out_shape=[jax.ShapeDtypeStruct((n_chip, K, N), BF)] + [jax.ShapeDtypeStruct(e.shape, e.dtype) for e in extras],
        compiler_params=_params(("arbitrary", "arbitrary")),
    )(me, a, b, *extras)


def _my_index():
    return 4 * lax.axis_index("x") + 2 * lax.axis_index("y") + lax.axis_index("c")


def _all_gather(arrs, dtypes, name):
    n = len(arrs)

    def body(*refs):
        ins, outs = refs[:n], refs[n:2 * n]
        stages = refs[2 * n:3 * n]
        send_sems, recv_sems, local_sems = refs[3 * n:]
        gathers = [_TwoLevel(stages[a], outs[a], send_sems.at[a], recv_sems.at[a], local_sems.at[a]) for a in range(n)]
        for a in range(n):
            stages[a][...] = ins[a][...].astype(stages[a].dtype)
            gathers[a].start()
        for g in gathers:
            g.forward()
        for g in gathers:
            g.finish()

    vm = pl.BlockSpec(memory_space=pltpu.VMEM)
    hbm = pl.BlockSpec(memory_space=pl.ANY)
    return pl.pallas_call(
        body, name=name,
        in_specs=[vm] * n, out_specs=[hbm] * n,
        out_shape=[jax.ShapeDtypeStruct((N_DEV,) + a.shape, dt) for a, dt in zip(arrs, dtypes)],
        scratch_shapes=[pltpu.VMEM(a.shape, dt) for a, dt in zip(arrs, dtypes)]
        + [pltpu.SemaphoreType.DMA((n, 7)), pltpu.SemaphoreType.DMA((n, 7)), pltpu.SemaphoreType.DMA((n,))],
        compiler_params=pltpu.CompilerParams(vmem_limit_bytes=VMEM_LIMIT),
    )(*arrs)


def _peer(mask):
    x, y, c = (lax.axis_index(a) for a in AXES)
    return (x ^ ((mask >> 2) & 1), y ^ ((mask >> 1) & 1), c ^ (mask & 1))


def _dev_index(p):
    return 4 * p[0] + 2 * p[1] + p[2]


class _Direct:
    def __init__(self, src, dst, send_sems, recv_sems, local_sem, scatter):
        me = _my_index()
        self.own = pltpu.make_async_copy(src.at[me] if scatter else src, dst.at[me], local_sem)
        self.sends, self.recvs = [], []
        for k in range(1, N_DEV):
            p = _peer(k)
            pi = _dev_index(p)
            sems = dict(send_sem=send_sems.at[k - 1], recv_sem=recv_sems.at[k - 1], device_id=p, device_id_type=MESH)
            self.sends.append(pltpu.make_async_remote_copy(src_ref=src.at[pi] if scatter else src, dst_ref=dst.at[me],
                                                           **sems))
            self.recvs.append(pltpu.make_async_remote_copy(src_ref=src.at[me] if scatter else src, dst_ref=dst.at[pi],
                                                           **sems))

    def start(self):
        self.own.start()
        for cp in self.sends:
            cp.start()

    def finish(self):
        for cp in self.sends:
            cp.wait_send()
        for cp in self.recvs:
            cp.wait_recv()
        self.own.wait()


class _TwoLevel:
    def __init__(self, src, dst, send_sems, recv_sems, local_sem, own=True):
        x, y, c = (lax.axis_index(a) for a in AXES)
        self.me, self.sibling = (x, y, c), (x, y, 1 - c)
        self.chips = [(1 - x, y), (x, 1 - y), (1 - x, 1 - y)]
        self.src, self.dst, self.send_sems, self.recv_sems = src, dst, send_sems, recv_sems
        self.own = pltpu.make_async_copy(src, dst.at[_dev_index(self.me)], local_sem) if own else None

    def _copy(self, k, block, to, from_src=False):
        slot = self.dst.at[_dev_index(block)]
        return pltpu.make_async_remote_copy(src_ref=self.src if from_src else slot, dst_ref=slot,
                                            send_sem=self.send_sems.at[k], recv_sem=self.recv_sems.at[k],
                                            device_id=to, device_id_type=MESH)

    def _firsts(self):
        c = self.me[2]
        return [self._copy(0, self.me, self.sibling, True)] + [self._copy(1 + j, self.me, (*chip, c), True)
                                                               for j, chip in enumerate(self.chips)]

    def _passed(self):
        c = self.me[2]
        return [self._copy(4 + j, (*chip, c), self.sibling) for j, chip in enumerate(self.chips)]

    def start(self):
        if self.own is not None:
            self.own.start()
        for cp in self._firsts():
            cp.start()

    def wait_sibling(self):
        self._copy(0, self.sibling, self.me).wait_recv()

    def wait_chip_and_forward(self, j):
        self._copy(1 + j, (*self.chips[j], self.me[2]), self.me).wait_recv()
        self._passed()[j].start()

    def wait_passed(self, j):
        self._copy(4 + j, (*self.chips[j], 1 - self.me[2]), self.me).wait_recv()

    def wait_sends(self):
        for cp in self._firsts() + self._passed():
            cp.wait_send()
        if self.own is not None:
            self.own.wait()

    def forward(self):
        for j in range(3):
            self.wait_chip_and_forward(j)

    def finish(self):
        self.wait_sibling()
        for j in range(3):
            self.wait_passed(j)
        self.wait_sends()


def _direct_sems(n):
    if n == 0:
        return []
    return [pltpu.SemaphoreType.DMA((n, 7)), pltpu.SemaphoreType.DMA((n, 7)), pltpu.SemaphoreType.DMA((n,))]


def _adam_math(w, g, m, v):
    m = ADAM_B1 * m + (1.0 - ADAM_B1) * g
    v = ADAM_B2 * v + (1.0 - ADAM_B2) * (g * g)
    m_hat = m / (1.0 - ADAM_B1 ** ADAM_STEP)
    v_hat = v / (1.0 - ADAM_B2 ** ADAM_STEP)
    delta = -ADAM_LR * (m_hat / (jnp.sqrt(v_hat) + ADAM_EPS) + ADAM_WD * w)
    return delta, m, v


def _sum_adam(parts, w, m, v, name):
    R, C = w.shape
    NP = parts.shape[0]
    BR = CHUNK if R % CHUNK == 0 else R

    def body(p_ref, w_ref, m_ref, v_ref, g_ref, d_ref, nm_ref, nv_ref):
        g = p_ref[0].astype(F32)
        for i in range(1, NP):
            g = g + p_ref[i].astype(F32)
        g_ref[...] = g
        d_ref[...], nm_ref[...], nv_ref[...] = _adam_math(w_ref[...], g, m_ref[...], v_ref[...])

    blk = pl.BlockSpec((BR, C), lambda i: (i, 0))
    S = jax.ShapeDtypeStruct((R, C), F32)
    return pl.pallas_call(
        body, name=name, grid=(R // BR,),
        in_specs=[pl.BlockSpec((NP, BR, C), lambda i: (0, i, 0)), blk, blk, blk],
        out_specs=[blk] * 4, out_shape=(S,) * 4,
        compiler_params=_params(("arbitrary",)),
    )(parts, w, m, v)


def _sum8(parts, name):
    _, R, C = parts.shape

    def body(p_ref, o_ref):
        g = p_ref[0]
        for i in range(1, N_DEV):
            g = g + p_ref[i]
        o_ref[...] = g

    return pl.pallas_call(body, name=name, out_shape=jax.ShapeDtypeStruct((R, C), F32))(parts)


def _adam_only(g, w, m, v, name):
    def body(g_ref, w_ref, m_ref, v_ref, d_ref, nm_ref, nv_ref):
        d_ref[...], nm_ref[...], nv_ref[...] = _adam_math(w_ref[...], g_ref[...], m_ref[...], v_ref[...])

    S = jax.ShapeDtypeStruct(w.shape, F32)
    return pl.pallas_call(body, name=name, out_shape=(S,) * 3)(g, w, m, v)


def _rope_tables(T):
    pos = np.arange(T, dtype=np.float32)
    inv_freq = (np.float64(ROPE_THETA) ** (-np.arange(0, HEAD_DIM, 2, dtype=np.float64) / HEAD_DIM)).astype(np.float32)
    ang = (pos[:, None] * inv_freq[None, :]).astype(np.float64)
    cos, sin, zero = np.cos(ang).astype(np.float32), np.sin(ang).astype(np.float32), np.zeros(ang.shape, np.float32)
    c = np.concatenate([cos, cos, cos, cos], axis=1)
    s1 = np.concatenate([-sin, zero, -sin, zero], axis=1)
    s2 = np.concatenate([zero, sin, zero, sin], axis=1)
    return jnp.asarray(c), jnp.asarray(s1), jnp.asarray(s2)


SUBLANES = 8


def _nrows(size):
    return -(-size // (SUBLANES * LANES)) * SUBLANES


def _rows(a):
    flat = a.reshape(-1)
    pad = _nrows(flat.shape[0]) * LANES - flat.shape[0]
    if pad:
        flat = jnp.concatenate([flat, jnp.zeros((pad,), flat.dtype)])
    return flat.reshape(-1, LANES)


def _pack(arrs, total_rows):
    rows = [_rows(a) for a in arrs]
    used = sum(r.shape[0] for r in rows)
    if total_rows > used:
        rows.append(jnp.zeros((total_rows - used, LANES), F32))
    return jnp.concatenate(rows, axis=0)


def _unpack(packed, shapes):
    out, at = [], 0
    for shp in shapes:
        size = math.prod(shp)
        nrow = _nrows(size)
        out.append(packed[at:at + nrow].reshape(-1)[:size].reshape(shp))
        at += nrow
    return out


def kernel(x, a_norm_g, a_w_in, a_ln_g, a_ln_b, a_ws, a_bs, a_w_out, kv_norm_g, w_kv, b_kv, b_norm_g, b_w_in, b_bq, b_sinks, b_w_out, final_norm_g, loss_target, m_a_norm_g, m_a_w_in, m_a_ln_g, m_a_ln_b, m_a_ws, m_a_bs, m_a_w_out, m_kv_norm_g, m_w_kv, m_b_kv, m_b_norm_g, m_b_w_in, m_b_bq, m_b_sinks, m_b_w_out, m_final_norm_g, v_a_norm_g, v_a_w_in, v_a_ln_g, v_a_ln_b, v_a_ws, v_a_bs, v_a_w_out, v_kv_norm_g, v_w_kv, v_b_kv, v_b_norm_g, v_b_w_in, v_b_bq, v_b_sinks, v_b_w_out, v_final_norm_g):
    T, D = x.shape[1], x.shape[2]
    AW = a_ln_g.shape[1] * N_DEV
    G = a_ws.shape[1]
    assert w_kv.shape[1] == 2 * LANES and a_ws.shape[2] == CHUNK and T % CHUNK == 0
    me = _my_index()

    xs, tgt = x[0], loss_target[0]
    vec = jnp.concatenate([a_norm_g, a_ln_g, a_ln_b], axis=1)
    vec = jnp.broadcast_to(vec, (SUBLANES, vec.shape[1]))
    slots = me ^ jnp.array(PASS_MASKS, jnp.int32)
    z, wa_in, vecs, wa_out, wkv = _in_proj(xs, a_w_in[0], vec, slots, [a_w_out[0], w_kv])
    wa_out = wa_out.reshape(AW, D)
    wkv = wkv.reshape(D, 2 * LANES)
    vecs = vecs[:, 0, :]
    ds = D // N_DEV
    g_a = vecs[:, :ds].reshape(1, D)
    ln_g = vecs[:, ds:ds + AW // N_DEV].reshape(1, AW)
    ln_b = vecs[:, ds + AW // N_DEV:].reshape(1, AW)

    rc, rs1, rs2 = _rope_tables(T)
    ws = a_ws[0]
    bs_t = a_bs[0].T
    g_kv = kv_norm_g.reshape(1, D)
    bkv = b_kv.reshape(1, -1)
    g_f = final_norm_g.reshape(1, D)
    sinks = jnp.repeat(b_sinks.reshape(2, 4, 2).transpose(0, 2, 1).reshape(4, 4), CHUNK, axis=1)
    h1, sv, vhat, rstd, k4, v4, kt, vt, wb_in, wb_out = _a_fwd(
        xs, z, ln_g, ln_b, ws, bs_t, wa_out, g_kv, wkv, bkv, rc, rs1, rs2, [b_w_in[0], b_w_out[0]])
    wb_out = wb_out.reshape(-1, D)
    q, g2, o, dh2, dh2_b, loss, d_gf = _b_fwd(h1, b_norm_g, wb_in, b_bq, rc, rs1, rs2, k4, vt, sinks, wb_out, g_f, tgt)
    dh1p, dz2, n2, y2, dk, dv, d_bq, d_gb, d_sink = _b_bwd(dh2, h1, q, g2, o, k4, v4, kt, sinks, wb_out, wb_in,
                                                           b_norm_g, rc, rs1, rs2)
    d_sink = d_sink[:, :4].reshape(2, 2, 4).transpose(0, 2, 1).reshape(1, 16)
    gw_b_in = _wgrad(n2, dz2, N_DEV, "wgrad_b_in", bt=2048)
    gw_b_out = _wgrad(y2, dh2_b, 1, "wgrad_b_out", bt=1024).reshape(N_DEV, -1, D)
    (dz, y, nkv, dkv, dh1, dh1_f, d_gkv, d_bkv, d_lng, d_lnb, d_ws, d_bst, r_b_in, r_b_out) = _a_bwd(
        dh1p, dk, dv, h1, g_kv, wkv, wa_out, ws, ln_g, ln_b, z, sv, vhat, rstd, rc, rs1, rs2, [gw_b_in, gw_b_out])
    gw_a_out = _wgrad(y, dh1, 1, "wgrad_a_out", bt=1024).reshape(N_DEV, AW // N_DEV, D)
    gw_kv = _wgrad(nkv, dkv, 1, "wgrad_kv", bt=2048).reshape(N_DEV, D // N_DEV, 2 * LANES)
    dx, n1, d_ga, r_a_out, r_kv = _a_in_bwd(dz, wa_in, xs, dh1_f, g_a, [gw_a_out, gw_kv])
    small = [d_ws, d_bst[:, :G].T, d_gkv, d_bkv, d_gb, d_bq, d_sink, d_gf, d_ga, d_lng, d_lnb, loss]
    used = sum(_nrows(a.size) for a in small)
    per = -(-used // (SUBLANES * N_DEV)) * SUBLANES
    small_pack = _pack(small, per * N_DEV).reshape(N_DEV, per, LANES)
    r_a_in, r_small = _wgrad_exchange(n1, dz, me.reshape(1), [small_pack], "wgrad_a_in")

    g_a_in, d_a_in, nm_a_in, nv_a_in = _sum_adam(r_a_in, a_w_in[0], m_a_w_in[0], v_a_w_in[0], "adam_a_in")
    g_a_out, d_a_out, nm_a_out, nv_a_out = _sum_adam(r_a_out, a_w_out[0], m_a_w_out[0], v_a_w_out[0], "adam_a_out")
    g_kvw, d_kvw, nm_kvw, nv_kvw = _sum_adam(r_kv, w_kv, m_w_kv, v_w_kv, "adam_kv")
    g_b_in, d_b_in, nm_b_in, nv_b_in = _sum_adam(r_b_in, b_w_in[0], m_b_w_in[0], v_b_w_in[0], "adam_b_in")
    g_b_out, d_b_out, nm_b_out, nv_b_out = _sum_adam(r_b_out, b_w_out[0], m_b_w_out[0], v_b_w_out[0], "adam_b_out")

    red = _sum8(r_small, "sum_small")
    (full_small,) = _all_gather([red], [F32], "gather_small")
    full_small = full_small.reshape(N_DEV * per, LANES)
    rep_shapes = [a_ws.shape, a_bs.shape, kv_norm_g.shape, b_kv.shape, b_norm_g.shape, b_bq.shape, b_sinks.shape,
                  final_norm_g.shape]
    gs = _unpack(full_small, rep_shapes + [(N_DEV, a_norm_g.shape[1]), (N_DEV, a_ln_g.shape[1]),
                                           (N_DEV, a_ln_b.shape[1]), (1, 1)])
    loss = gs.pop()[0, 0]
    g_ang = lax.dynamic_slice_in_dim(gs[8], me, 1, axis=0)
    g_alng = lax.dynamic_slice_in_dim(gs[9], me, 1, axis=0)
    g_alnb = lax.dynamic_slice_in_dim(gs[10], me, 1, axis=0)
    sm_g = gs[:8] + [g_ang, g_alng, g_alnb]
    sm_shapes = [a.shape for a in sm_g]
    tot = sum(_nrows(a.size) for a in sm_g)
    pw = _pack([a_ws, a_bs, kv_norm_g, b_kv, b_norm_g, b_bq, b_sinks, final_norm_g, a_norm_g, a_ln_g, a_ln_b], tot)
    pm = _pack([m_a_ws, m_a_bs, m_kv_norm_g, m_b_kv, m_b_norm_g, m_b_bq, m_b_sinks, m_final_norm_g, m_a_norm_g,
                m_a_ln_g, m_a_ln_b], tot)
    pv = _pack([v_a_ws, v_a_bs, v_kv_norm_g, v_b_kv, v_b_norm_g, v_b_bq, v_b_sinks, v_final_norm_g, v_a_norm_g,
                v_a_ln_g, v_a_ln_b], tot)
    pg = _pack(sm_g, tot)
    pd, pnm, pnv = _adam_only(pg, pw, pm, pv, "adam_small")
    sd, snm, snv = _unpack(pd, sm_shapes), _unpack(pnm, sm_shapes), _unpack(pnv, sm_shapes)

    def order(big, sm):
        a_in, a_out, kvw, b_in, b_out = big
        ws_, bs_, kvg, bkv_, bng, bq_, snk, fng, ang, alng, alnb = sm
        return (ang, a_in[None], alng, alnb, ws_, bs_, a_out[None], kvg, kvw, bkv_, bng, b_in[None], bq_, snk,
                b_out[None], fng)

    grads = order((g_a_in, g_a_out, g_kvw, g_b_in, g_b_out), sm_g)
    deltas = order((d_a_in, d_a_out, d_kvw, d_b_in, d_b_out), sd)
    new_m = order((nm_a_in, nm_a_out, nm_kvw, nm_b_in, nm_b_out), snm)
    new_v = order((nv_a_in, nv_a_out, nv_kvw, nv_b_in, nv_b_out), snv)
    return (loss, dx[None], *grads, *deltas, *new_m, *new_v)
```

```python
import functools

import jax
import jax.numpy as jnp
import numpy as np
from jax import lax
from jax.experimental import pallas as pl
from jax.experimental.pallas import tpu as pltpu

CHUNK = 128
HEAD_DIM = 64
ROPE_THETA = 10000.0
EPS = 1e-5
ADAM_LR = 0.001
ADAM_B1 = 0.9
ADAM_B2 = 0.999
ADAM_EPS = 1e-08
ADAM_WD = 0.01
ADAM_STEP = 10
N_DEV = 8
LANES = 128
NEG = -1e30

BF = jnp.bfloat16
F32 = jnp.float32
MESH = pl.DeviceIdType.MESH
AXES = ("x", "y", "c")
VMEM_LIMIT = 56 * 1024 * 1024


def _dot(a, b):
    return jnp.dot(a, b, preferred_element_type=F32)


def _dot_nt(a, b):
    return lax.dot_general(a, b, (((1,), (1,)), ((), ())), preferred_element_type=F32)


def _dot_tn(a, b):
    return lax.dot_general(a, b, (((0,), (0,)), ((), ())), preferred_element_type=F32)


def _const_spec(shape):
    nd = len(shape)
    return pl.BlockSpec(shape, lambda *_: (0,) * nd, pipeline_mode=pl.Buffered(1))


def _acc_spec(shape):
    nd = len(shape)
    return pl.BlockSpec(shape, lambda *_: (0,) * nd)


def _row_spec(tm, width):
    return pl.BlockSpec((tm, width), lambda i: (i, 0))


def _col_spec(tm, height):
    return pl.BlockSpec((height, tm), lambda i: (0, i))


def _params(sem):
    return pltpu.CompilerParams(dimension_semantics=sem, vmem_limit_bytes=VMEM_LIMIT)


def _rot(x, c, s1, s2):
    return x * c + pltpu.roll(x, 96, 1) * s1 + pltpu.roll(x, 32, 1) * s2


def _rot_bwd(d, c, s1, s2):
    return d * c + pltpu.roll(d * s1, 32, 1) + pltpu.roll(d * s2, 96, 1)


def _silu_parts(g):
    sg = jax.nn.sigmoid(g)
    return g * sg, sg * (1.0 + g * (1.0 - sg))


def _rms_bwd(dn, xh, r, g):
    a = dn * g
    return r * (a - xh * jnp.mean(a * xh, axis=-1, keepdims=True))


def _lane_lo(shape):
    return lax.broadcasted_iota(jnp.int32, shape, 1) < HEAD_DIM


def _split4(t):
    lo = _lane_lo(t.shape)
    tr = pltpu.roll(t, HEAD_DIM, 1)
    z = jnp.zeros_like(t)
    return jnp.concatenate([jnp.where(lo, t, z), jnp.where(lo, z, tr), jnp.where(lo, tr, z), jnp.where(lo, z, t)], axis=1)


def _stack_pairs(t, h):
    return jnp.concatenate([t[:, (h * 4 + j) * LANES:(h * 4 + j + 1) * LANES] for j in range(4)], axis=0)


def _upper():
    shape = (CHUNK, 4 * CHUNK)
    return lax.broadcasted_iota(jnp.int32, shape, 0) > (lax.broadcasted_iota(jnp.int32, shape, 1) & (CHUNK - 1))


def _band_rows(ref, prev, cur, h):
    a = slice(2 * h * LANES, (2 * h + 1) * LANES)
    b = slice((2 * h + 1) * LANES, (2 * h + 2) * LANES)
    return jnp.concatenate([ref[pl.ds(prev, CHUNK), a], ref[pl.ds(cur, CHUNK), a],
                            ref[pl.ds(prev, CHUNK), b], ref[pl.ds(cur, CHUNK), b]], axis=0)


def _band_cols(ref, pci, ci, h):
    a = slice(2 * h * LANES, (2 * h + 1) * LANES)
    b = slice((2 * h + 1) * LANES, (2 * h + 2) * LANES)
    return jnp.concatenate([ref[pci, a, :], ref[ci, a, :], ref[pci, b, :], ref[ci, b, :]], axis=1)


def _fold(t, upper, has_prev=None):
    out = []
    for k in range(2):
        prev = t[2 * k * CHUNK:(2 * k + 1) * CHUNK]
        if has_prev is not None:
            prev = jnp.where(has_prev, prev, NEG)
        out.append(jnp.where(upper, prev, t[(2 * k + 1) * CHUNK:(2 * k + 2) * CHUNK]))
    return out


def _unfold(fa, fb, upper):
    z = jnp.zeros_like(fa)
    return jnp.concatenate([jnp.where(upper, fa, z), jnp.where(upper, z, fa),
                            jnp.where(upper, fb, z), jnp.where(upper, z, fb)], axis=0)


def _softmax_sink(f, sink):
    m = jnp.maximum(jnp.max(f, axis=0, keepdims=True), sink)
    p = jnp.exp(f - m)
    es = jnp.exp(sink - m)
    inv = 1.0 / (jnp.sum(p, axis=0, keepdims=True) + es)
    return p * inv, es * inv


class _Riding:
    def __init__(self, shards, gathered, stages, sems, n_steps):
        self.shards, self.stages, self.n_steps = shards, stages, n_steps
        ssem, rsem, lsem = sems
        self.gathers = [_TwoLevel(stages[k], gathered[k], ssem.at[k], rsem.at[k], lsem.at[k])
                        for k in range(len(shards))]

    def begin(self, i):
        @pl.when(i == 0)
        def _():
            for shard, stage, g in zip(self.shards, self.stages, self.gathers):
                stage[...] = shard[...].astype(stage.dtype)
                g.start()

    def end(self, i):
        @pl.when(i == self.n_steps // 2)
        def _():
            for g in self.gathers:
                g.forward()

        @pl.when(i == self.n_steps - 1)
        def _():
            for g in self.gathers:
                g.finish()

    @staticmethod
    def specs(later):
        nl = len(later)
        hbm = pl.BlockSpec(memory_space=pl.ANY)
        return ([_const_spec(w.shape) for w in later], [hbm] * nl,
                tuple(jax.ShapeDtypeStruct((N_DEV,) + w.shape, BF) for w in later),
                [pltpu.VMEM(w.shape, BF) for w in later] + _direct_sems(nl))


PASS_MASKS = (0, 1, 4, 2, 5, 3, 6, 7)


def _in_proj(x, w_shard, vec_shard, slots, later):
    T, D = x.shape
    SH = w_shard.shape[1]
    VW = vec_shard.shape[1]
    TM = min(512, T)
    nT = T // TM
    nl = len(later)
    ds = D // N_DEV
    last = N_DEV - 1

    def body(slots_ref, x_ref, wsh_ref, vsh_ref, *rest):
        shards, rest = rest[:nl], rest[nl:]
        (z_ref, wout_ref, vout_ref), rest = rest[:3], rest[3:]
        gathered, rest = rest[:nl], rest[nl:]
        (w_scr, vec_scr, vstage, n1_scr, ga_scr, w_s, w_r, w_l, v_s, v_r, v_l), rest = rest[:11], rest[11:]
        stages, sems = rest[:nl], rest[nl:]
        p, i = pl.program_id(0), pl.program_id(1)
        me = _my_index()
        wg = _TwoLevel(w_scr.at[me], w_scr, w_s, w_r, w_l, own=False)
        vg = _TwoLevel(vstage, vec_scr, v_s, v_r, v_l)
        lg = [_TwoLevel(stages[k], gathered[k], sems[0].at[k], sems[1].at[k], sems[2].at[k]) for k in range(nl)]
        w_copy = pltpu.make_async_copy(w_scr, wout_ref, w_l)

        def at_pass(k):
            return (p == k) & (i == 0)

        @pl.when(at_pass(0))
        def _():
            vstage[...] = vsh_ref[...]
            vg.start()
            w_scr[me] = wsh_ref[...].astype(BF)
            wg.start()
            for k in range(nl):
                stages[k][...] = shards[k][...].astype(BF)
                lg[k].start()
            vg.forward()
            vg.finish()
            for j in range(N_DEV):
                ga_scr[:, j * ds:(j + 1) * ds] = vec_scr[j, 0:1, 0:ds]
            vout_ref[...] = vec_scr[...]

        @pl.when(at_pass(1))
        def _():
            wg.wait_sibling()

        for k, j in ((2, 0), (3, 1), (6, 2)):
            @pl.when(at_pass(k))
            def _(j=j):
                wg.wait_chip_and_forward(j)

        for k, j in ((4, 0), (5, 1), (7, 2)):
            @pl.when(at_pass(k))
            def _(j=j):
                wg.wait_passed(j)

        @pl.when(at_pass(last))
        def _():
            w_copy.start()

        @pl.when(p == 0)
        def _():
            xv = x_ref[...]
            r1 = lax.rsqrt(jnp.mean(xv * xv, axis=-1, keepdims=True) + EPS)
            n1_scr[i] = (xv * r1 * ga_scr[...]).astype(BF)

        z_ref[...] = _dot(n1_scr[i], w_scr[slots_ref[p]]).astype(BF)

        @pl.when((p == last) & (i == nT - 1))
        def _():
            wg.wait_sends()
            for g in lg:
                g.forward()
            for g in lg:
                g.finish()
            w_copy.wait()

    hbm = pl.BlockSpec(memory_space=pl.ANY)
    dma = pltpu.SemaphoreType.DMA
    S = jax.ShapeDtypeStruct
    grid_spec = pltpu.PrefetchScalarGridSpec(
        num_scalar_prefetch=1, grid=(N_DEV, nT),
        in_specs=[pl.BlockSpec((TM, D), lambda p, i, s: (jnp.where(p == 0, i, nT - 1), 0)),
                  pl.BlockSpec(w_shard.shape, lambda p, i, s: (0, 0), pipeline_mode=pl.Buffered(1)),
                  pl.BlockSpec(vec_shard.shape, lambda p, i, s: (0, 0), pipeline_mode=pl.Buffered(1))]
        + [pl.BlockSpec(w.shape, lambda p, i, s: (0, 0), pipeline_mode=pl.Buffered(1)) for w in later],
        out_specs=[pl.BlockSpec((TM, SH), lambda p, i, s: (i, s[p])), hbm,
                   pl.BlockSpec((N_DEV,) + vec_shard.shape, lambda p, i, s: (0, 0, 0))] + [hbm] * nl,
        scratch_shapes=[pltpu.VMEM((N_DEV, D, SH), BF), pltpu.VMEM((N_DEV,) + vec_shard.shape, F32),
                        pltpu.VMEM(vec_shard.shape, F32), pltpu.VMEM((nT, TM, D), BF), pltpu.VMEM((1, D), F32),
                        dma((7,)), dma((7,)), dma, dma((7,)), dma((7,)), dma]
        + [pltpu.VMEM(w.shape, BF) for w in later] + _direct_sems(nl))
    return pl.pallas_call(
        body, name="a_in_proj", grid_spec=grid_spec,
        out_shape=(S((T, N_DEV * SH), BF), S((N_DEV, D, SH), BF), S((N_DEV,) + vec_shard.shape, F32))
        + tuple(S((N_DEV,) + w.shape, BF) for w in later),
        compiler_params=_params(("arbitrary", "arbitrary")),
    )(slots, x, w_shard, vec_shard, *later)


def _a_fwd(x, z, ln_g, ln_b, ws, bs_t, wa_out, g_kv, w_kv, b_kv, rc, rs1, rs2, later):
    T, D = x.shape
    AW = wa_out.shape[0]
    G = ws.shape[0]
    TM = min(256, T)
    nT = T // TM
    nC = TM // CHUNK
    nl = len(later)

    def body(x_ref, u_ref, v_ref, gt_ref, lng_ref, lnb_ref, ws_ref, bst_ref, waout_ref, gkv_ref, wkv_ref, bkv_ref,
             rc_ref, rs1_ref, rs2_ref, *rest):
        shards, rest = rest[:nl], rest[nl:]
        (h1_ref, sv_ref, vhat_ref, rstd_ref, k4_ref, v4_ref, kt_ref, vt_ref), rest = rest[:8], rest[8:]
        gathered, sv_scr, stages, sems = rest[:nl], rest[nl], rest[nl + 1:2 * nl + 1], rest[2 * nl + 1:]
        i = pl.program_id(0)
        riding = _Riding(shards, gathered, stages, sems, nT)
        riding.begin(i)
        xv = x_ref[...]
        u = u_ref[...].astype(F32)
        v = v_ref[...].astype(F32)
        gt = gt_ref[...].astype(F32)
        mu = jnp.mean(v, axis=-1, keepdims=True)
        xc = v - mu
        rstd = lax.rsqrt(jnp.mean(xc * xc, axis=-1, keepdims=True) + EPS)
        vhat = xc * rstd
        vln = (vhat * lng_ref[...] + lnb_ref[...]).astype(BF)
        tri = lax.broadcasted_iota(jnp.int32, (CHUNK, CHUNK), 0) >= lax.broadcasted_iota(jnp.int32, (CHUNK, CHUNK), 1)
        for g in range(G):
            wsm = jnp.where(tri, ws_ref[g], 0.0).astype(BF)
            bias = bst_ref[:, g:g + 1]
            for c in range(nC):
                blk = vln[c * CHUNK:(c + 1) * CHUNK, g * CHUNK:(g + 1) * CHUNK]
                sv_scr[c * CHUNK:(c + 1) * CHUNK, g * CHUNK:(g + 1) * CHUNK] = _dot(wsm, blk) + bias
        sv = sv_scr[...]
        silu, _ = _silu_parts(gt)
        y = (u * sv * silu).astype(BF)
        h1 = xv + _dot(y, waout_ref[...])
        h1_ref[...] = h1
        sv_ref[...] = sv.astype(BF)
        vhat_ref[...] = vhat.astype(BF)
        rstd_ref[...] = jnp.broadcast_to(rstd, rstd_ref.shape)
        rkv = lax.rsqrt(jnp.mean(h1 * h1, axis=-1, keepdims=True) + EPS)
        nkv = (h1 * rkv * gkv_ref[...]).astype(BF)
        kv = _dot(nkv, wkv_ref[...]) + bkv_ref[...]
        k_rot = _rot(kv[:, :LANES], rc_ref[...], rs1_ref[...], rs2_ref[...])
        for src, ref, tref in ((k_rot, k4_ref, kt_ref), (kv[:, LANES:], v4_ref, vt_ref)):
            t4 = _split4(src)
            ref[...] = t4.astype(BF)
            for c in range(nC):
                for b in range(4):
                    blk = t4[c * CHUNK:(c + 1) * CHUNK, b * LANES:(b + 1) * LANES]
                    tref[c, b * LANES:(b + 1) * LANES, :] = blk.T.astype(BF)
        riding.end(i)

    row = functools.partial(_row_spec, TM)
    zcol = [pl.BlockSpec((TM, AW), functools.partial(lambda k, i: (i, k), k)) for k in range(3)]
    tr = pl.BlockSpec((nC, 4 * LANES, CHUNK), lambda i: (i, 0, 0))
    r_in, r_out, r_shape, r_scratch = _Riding.specs(later)
    S = jax.ShapeDtypeStruct
    return pl.pallas_call(
        body, name="a_fwd", grid=(nT,),
        in_specs=[row(D)] + zcol + [_const_spec((1, AW)), _const_spec((1, AW)),
                  _const_spec(ws.shape), _const_spec(bs_t.shape), _const_spec(wa_out.shape), _const_spec((1, D)),
                  _const_spec(w_kv.shape), _const_spec((1, 2 * LANES)), row(LANES), row(LANES), row(LANES)] + r_in,
        out_specs=[row(D), row(AW), row(AW), row(LANES), row(4 * LANES), row(4 * LANES), tr, tr] + r_out,
        out_shape=(S((T, D), F32), S((T, AW), BF), S((T, AW), BF), S((T, LANES), F32),
                   S((T, 4 * LANES), BF), S((T, 4 * LANES), BF),
                   S((T // CHUNK, 4 * LANES, CHUNK), BF), S((T // CHUNK, 4 * LANES, CHUNK), BF)) + r_shape,
        scratch_shapes=[pltpu.VMEM((TM, AW), F32)] + r_scratch,
        compiler_params=_params(("arbitrary",)),
    )(x, z, z, z, ln_g, ln_b, ws, bs_t, wa_out, g_kv, w_kv, b_kv, rc, rs1, rs2, *later)


def _b_fwd(h1, g_b, wb_in, bq, rc, rs1, rs2, k4, vt, sinks, wb_out, g_f, target):
    T, D = h1.shape
    BW = wb_out.shape[0]
    SH = wb_in.shape[2]
    TM = min(256, T)
    nC = TM // CHUNK
    nP = BW // LANES

    def body(h1_ref, gb_ref, wbin_ref, bq_ref, rc_ref, rs1_ref, rs2_ref, k4_ref, vt_ref, sink_ref, wbout_ref, gf_ref,
             tgt_ref, q_ref, g2_ref, o_ref, dh2_ref, dh2b_ref, loss_ref, dgf_ref, z_scr, o_scr):
        i = pl.program_id(0)
        h1v = h1_ref[...]
        r2 = lax.rsqrt(jnp.mean(h1v * h1v, axis=-1, keepdims=True) + EPS)
        n2 = (h1v * r2 * gb_ref[...]).astype(BF)
        for j in range(N_DEV):
            z_scr[:, j * SH:(j + 1) * SH] = _dot(n2, wbin_ref[j])
        c_t, s1_t, s2_t = rc_ref[...], rs1_ref[...], rs2_ref[...]
        for p in range(nP):
            cols = slice(p * LANES, (p + 1) * LANES)
            qp = _rot(z_scr[:, cols] + bq_ref[:, cols], c_t, s1_t, s2_t) * (HEAD_DIM ** -0.5)
            q_ref[:, cols] = qp.astype(BF)
        g2 = z_scr[:, BW:]
        g2_ref[...] = g2.astype(BF)
        upper = _upper()
        for c in range(nC):
            ci = i * nC + c
            rows = slice(c * CHUNK, (c + 1) * CHUNK)
            pci = jnp.maximum(ci - 1, 0)
            prev = pl.multiple_of(pci * CHUNK, CHUNK)
            cur = pl.multiple_of(ci * CHUNK, CHUNK)
            qc = q_ref[rows, :]
            for h in range(2):
                st = _dot_nt(_band_rows(k4_ref, prev, cur, h), _stack_pairs(qc, h))
                fa, fb = _fold(st, upper, ci > 0)
                pa, _ = _softmax_sink(fa, sink_ref[2 * h:2 * h + 1, :])
                pb, _ = _softmax_sink(fb, sink_ref[2 * h + 1:2 * h + 2, :])
                ot = _dot(_band_cols(vt_ref, pci, ci, h), _unfold(pa, pb, upper).astype(BF))
                for j in range(4):
                    o_scr[rows, (h * 4 + j) * LANES:(h * 4 + j + 1) * LANES] = ot[:, j * CHUNK:(j + 1) * CHUNK].T
        o = o_scr[...]
        o_ref[...] = o.astype(BF)
        silu, _ = _silu_parts(g2)
        h2 = h1v + _dot((o * silu).astype(BF), wbout_ref[...])
        rf = lax.rsqrt(jnp.mean(h2 * h2, axis=-1, keepdims=True) + EPS)
        xh = h2 * rf
        gf = gf_ref[...]
        err = xh * gf - tgt_ref[...]
        dyf = err * (1.0 / D)
        dh2 = _rms_bwd(dyf, xh, rf, gf)
        dh2_ref[...] = dh2
        dh2b_ref[...] = dh2.astype(BF)

        @pl.when(i == 0)
        def _():
            loss_ref[...] = jnp.zeros_like(loss_ref)
            dgf_ref[...] = jnp.zeros_like(dgf_ref)

        loss_ref[...] += 0.5 * jnp.sum(jnp.mean(err * err, axis=-1, keepdims=True), axis=0, keepdims=True)
        dgf_ref[...] += jnp.sum(dyf * xh, axis=0, keepdims=True)

    row = functools.partial(_row_spec, TM)
    S = jax.ShapeDtypeStruct
    return pl.pallas_call(
        body, name="b_fwd", grid=(T // TM,),
        in_specs=[row(D), _const_spec((1, D)), _const_spec(wb_in.shape), _const_spec((1, BW)), row(LANES), row(LANES),
                  row(LANES), _const_spec(k4.shape), _const_spec(vt.shape), _const_spec(sinks.shape),
                  _const_spec(wb_out.shape), _const_spec((1, D)), row(D)],
        out_specs=[row(BW), row(BW), row(BW), row(D), row(D), _acc_spec((1, 1)), _acc_spec((1, D))],
        out_shape=(S((T, BW), BF), S((T, BW), BF), S((T, BW), BF), S((T, D), F32), S((T, D), BF), S((1, 1), F32),
                   S((1, D), F32)),
        scratch_shapes=[pltpu.VMEM((TM, 2 * BW), F32), pltpu.VMEM((TM, BW), F32)],
        compiler_params=_params(("arbitrary",)),
    )(h1, g_b, wb_in, bq, rc, rs1, rs2, k4, vt, sinks, wb_out, g_f, target)


def _b_bwd(dh2, h1, q, g2, o, k4, v4, kt, sinks, wb_out, wb_in, g_b, rc, rs1, rs2):
    T, D = h1.shape
    BW = wb_out.shape[0]
    SH = wb_in.shape[2]
    TM = min(256, T)
    nT = T // TM
    nC = TM // CHUNK
    nP = BW // LANES

    def body(dh2_ref, h1_ref, q_ref, g2_ref, o_ref, k4_ref, v4_ref, kt_ref, sink_ref, wbout_ref, wbin_ref, gb_ref,
             rc_ref, rs1_ref, rs2_ref,
             dh1_ref, dz2_ref, n2_ref, y2_ref, dk_ref, dv_ref, dbq_ref, dgb_ref, dsink_ref, do_scr, dq_scr, dsacc_scr):
        i = pl.program_id(0)

        @pl.when(i == 0)
        def _():
            dk_ref[...] = jnp.zeros_like(dk_ref)
            dv_ref[...] = jnp.zeros_like(dv_ref)
            dbq_ref[...] = jnp.zeros_like(dbq_ref)
            dgb_ref[...] = jnp.zeros_like(dgb_ref)
            dsacc_scr[...] = jnp.zeros_like(dsacc_scr)

        dh2 = dh2_ref[...]
        dy2 = _dot_nt(dh2.astype(BF), wbout_ref[...]).astype(BF)
        silu, dsilu = _silu_parts(g2_ref[...].astype(F32))
        silu, dsilu = silu.astype(BF), dsilu.astype(BF)
        ob = o_ref[...]
        y2_ref[...] = (ob * silu).T
        do_scr[...] = dy2 * silu
        dz2_ref[:, BW:] = dy2 * ob * dsilu
        upper = _upper()
        lo = _lane_lo((2 * CHUNK, LANES))
        for c in range(nC):
            ci = i * nC + c
            rows = slice(c * CHUNK, (c + 1) * CHUNK)
            pci = jnp.maximum(ci - 1, 0)
            prev = pl.multiple_of(pci * CHUNK, CHUNK)
            cur = pl.multiple_of(ci * CHUNK, CHUNK)
            qc = q_ref[rows, :]
            doc = do_scr[rows, :]
            dkb = jnp.zeros((2 * CHUNK, LANES), F32)
            dvb = jnp.zeros((2 * CHUNK, LANES), F32)
            for h in range(2):
                qs = _stack_pairs(qc, h)
                dos = _stack_pairs(doc, h)
                fa, fb = _fold(_dot_nt(_band_rows(k4_ref, prev, cur, h), qs), upper, ci > 0)
                dfa, dfb = _fold(_dot_nt(_band_rows(v4_ref, prev, cur, h), dos), upper)
                folded = []
                for k, (f, df) in enumerate(((fa, dfa), (fb, dfb))):
                    p, ps = _softmax_sink(f, sink_ref[2 * h + k:2 * h + k + 1, :])
                    delta = jnp.sum(p * df, axis=0, keepdims=True)
                    dsacc_scr[2 * h + k:2 * h + k + 1, :] -= ps * delta
                    folded.append((p, p * (df - delta)))
                pt = _unfold(folded[0][0], folded[1][0], upper).astype(BF)
                dst = _unfold(folded[0][1], folded[1][1], upper).astype(BF)
                dqt = _dot(_band_cols(kt_ref, pci, ci, h), dst)
                for j in range(4):
                    dq_scr[rows, (h * 4 + j) * LANES:(h * 4 + j + 1) * LANES] = dqt[:, j * CHUNK:(j + 1) * CHUNK].T
                for acc_name, g in (("k", _dot(dst, qs)), ("v", _dot(pt, dos))):
                    a, b = g[:2 * CHUNK], g[2 * CHUNK:]
                    if h == 0:
                        part = jnp.where(lo, a + pltpu.roll(b, HEAD_DIM, 1), 0.0)
                    else:
                        part = jnp.where(lo, 0.0, pltpu.roll(a, HEAD_DIM, 1) + b)
                    if acc_name == "k":
                        dkb += part
                    else:
                        dvb += part
            dk_ref[pl.ds(prev, CHUNK), :] += dkb[:CHUNK]
            dk_ref[pl.ds(cur, CHUNK), :] += dkb[CHUNK:]
            dv_ref[pl.ds(prev, CHUNK), :] += dvb[:CHUNK]
            dv_ref[pl.ds(cur, CHUNK), :] += dvb[CHUNK:]

        @pl.when(i == nT - 1)
        def _():
            lane = lax.broadcasted_iota(jnp.int32, dsink_ref.shape, 1)
            tot = jnp.zeros(dsink_ref.shape, F32)
            for j in range(4):
                tot += jnp.where(lane == j, jnp.sum(dsacc_scr[:, j * CHUNK:(j + 1) * CHUNK], axis=1, keepdims=True), 0.0)
            dsink_ref[...] = tot
        c_t, s1_t, s2_t = rc_ref[...], rs1_ref[...], rs2_ref[...]
        for p in range(nP):
            cols = slice(p * LANES, (p + 1) * LANES)
            dqp = _rot_bwd(dq_scr[:, cols] * (HEAD_DIM ** -0.5), c_t, s1_t, s2_t)
            dbq_ref[:, cols] += jnp.sum(dqp, axis=0, keepdims=True)
            dz2_ref[:, cols] = dqp.astype(BF)
        h1v = h1_ref[...]
        r2 = lax.rsqrt(jnp.mean(h1v * h1v, axis=-1, keepdims=True) + EPS)
        xh = h1v * r2
        gb = gb_ref[...]
        n2_ref[...] = (xh * gb).astype(BF).T
        dn2 = None
        for j in range(N_DEV):
            part = _dot_nt(dz2_ref[:, j * SH:(j + 1) * SH], wbin_ref[j])
            dn2 = part if dn2 is None else dn2 + part
        dgb_ref[...] += jnp.sum(dn2 * xh, axis=0, keepdims=True)
        dh1_ref[...] = dh2 + _rms_bwd(dn2, xh, r2, gb)

    row = functools.partial(_row_spec, TM)
    S = jax.ShapeDtypeStruct
    return pl.pallas_call(
        body, name="b_bwd", grid=(T // TM,),
        in_specs=[row(D), row(D), row(BW), row(BW), row(BW), _const_spec(k4.shape), _const_spec(v4.shape),
                  _const_spec(kt.shape), _const_spec(sinks.shape), _const_spec(wb_out.shape), _const_spec(wb_in.shape),
                  _const_spec((1, D)), row(LANES), row(LANES), row(LANES)],
        out_specs=[row(D), row(2 * BW), _col_spec(TM, D), _col_spec(TM, BW), _acc_spec((T, LANES)),
                   _acc_spec((T, LANES)), _acc_spec((1, BW)), _acc_spec((1, D)), _acc_spec((4, LANES))],
        out_shape=(S((T, D), F32), S((T, 2 * BW), BF), S((D, T), BF), S((BW, T), BF), S((T, LANES), F32),
                   S((T, LANES), F32), S((1, BW), F32), S((1, D), F32), S((4, LANES), F32)),
        scratch_shapes=[pltpu.VMEM((TM, BW), BF), pltpu.VMEM((TM, BW), F32), pltpu.VMEM((4, 4 * CHUNK), F32)],
        compiler_params=_params(("arbitrary",)),
    )(dh2, h1, q, g2, o, k4, v4, kt, sinks, wb_out, wb_in, g_b, rc, rs1, rs2)


def _a_bwd(dh1p, dk, dv, h1, g_kv, w_kv, wa_out, ws, ln_g, ln_b, z, sv, vhat, rstd, rc, rs1, rs2, ready):
    T, D = h1.shape
    AW = wa_out.shape[0]
    G = ws.shape[0]
    TM = min(256, T)
    nT = T // TM
    nC = TM // CHUNK
    nr = len(ready)

    def body(dh1p_ref, dk_ref, dv_ref, h1_ref, gkv_ref, wkv_ref, waout_ref, ws_ref, lng_ref,
             lnb_ref, u_ref, gt_ref, sv_ref, vhat_ref, rstd_ref, rc_ref, rs1_ref, rs2_ref, *rest):
        ready_refs, rest = rest[:nr], rest[nr:]
        (dz_ref, y_ref, nkv_ref, dkv_ref, dh1_ref, dh1f_ref, dgkv_ref, dbkv_ref, dlng_ref, dlnb_ref,
         dws_ref, dbs_ref), rest = rest[:12], rest[12:]
        recv_refs, (dsv_scr, dvln_scr, ssem, rsem, lsem) = rest[:nr], rest[nr:]
        i = pl.program_id(0)
        exchanges = [_Direct(ready_refs[k], recv_refs[k], ssem.at[k], rsem.at[k], lsem.at[k], scatter=True)
                     for k in range(nr)]

        @pl.when(i == 0)
        def _():
            for e in exchanges:
                e.start()
            for r in (dgkv_ref, dbkv_ref, dlng_ref, dlnb_ref, dws_ref, dbs_ref):
                r[...] = jnp.zeros_like(r)

        dk_pre = _rot_bwd(dk_ref[...], rc_ref[...], rs1_ref[...], rs2_ref[...])
        dkv = jnp.concatenate([dk_pre, dv_ref[...]], axis=1)
        dbkv_ref[...] += jnp.sum(dkv, axis=0, keepdims=True)
        dkv_b = dkv.astype(BF)
        dkv_ref[...] = dkv_b
        h1v = h1_ref[...]
        rkv = lax.rsqrt(jnp.mean(h1v * h1v, axis=-1, keepdims=True) + EPS)
        xh_kv = h1v * rkv
        gkv = gkv_ref[...]
        nkv_ref[...] = (xh_kv * gkv).astype(BF).T
        dnkv = _dot_nt(dkv_b, wkv_ref[...])
        dgkv_ref[...] += jnp.sum(dnkv * xh_kv, axis=0, keepdims=True)
        dh1 = dh1p_ref[...] + _rms_bwd(dnkv, xh_kv, rkv, gkv)
        dh1_b = dh1.astype(BF)
        dh1_ref[...] = dh1_b
        dh1f_ref[...] = dh1
        dy = _dot_nt(dh1_b, waout_ref[...]).astype(BF)
        silu, dsilu = _silu_parts(gt_ref[...].astype(F32))
        silu, dsilu = silu.astype(BF), dsilu.astype(BF)
        ub, svb = u_ref[...], sv_ref[...]
        us = ub * silu
        dys = dy * svb
        y_ref[...] = (us * svb).T
        dz_ref[:, :AW] = dys * silu
        dz_ref[:, 2 * AW:] = dys * ub * dsilu
        dsv_scr[...] = dy * us
        vhat_v = vhat_ref[...].astype(F32)
        lng = lng_ref[...]
        vln_b = (vhat_v * lng + lnb_ref[...]).astype(BF)
        tri = lax.broadcasted_iota(jnp.int32, (CHUNK, CHUNK), 0) >= lax.broadcasted_iota(jnp.int32, (CHUNK, CHUNK), 1)
        lane = lax.broadcasted_iota(jnp.int32, (CHUNK, LANES), 1)
        dbs = jnp.zeros((CHUNK, LANES), F32)
        for g in range(G):
            wsm = jnp.where(tri, ws_ref[g], 0.0).astype(BF)
            cols = slice(g * CHUNK, (g + 1) * CHUNK)
            dws_g = None
            for c in range(nC):
                rows = slice(c * CHUNK, (c + 1) * CHUNK)
                dsv_cg = dsv_scr[rows, cols]
                dvln_scr[rows, cols] = _dot_tn(wsm, dsv_cg)
                part = _dot_nt(dsv_cg, vln_b[rows, cols])
                dws_g = part if dws_g is None else dws_g + part
                dbs += jnp.where(lane == g, jnp.sum(dsv_cg.astype(F32), axis=-1, keepdims=True), 0.0)
            dws_ref[g] += jnp.where(tri, dws_g, 0.0)
        dbs_ref[...] += dbs
        dvln = dvln_scr[...]
        dlng_ref[...] += jnp.sum(dvln * vhat_v, axis=0, keepdims=True)
        dlnb_ref[...] += jnp.sum(dvln, axis=0, keepdims=True)
        a = dvln * lng
        dvv = rstd_ref[:, 0:1] * (a - jnp.mean(a, axis=-1, keepdims=True)
                                  - vhat_v * jnp.mean(a * vhat_v, axis=-1, keepdims=True))
        dz_ref[:, AW:2 * AW] = dvv.astype(BF)

        @pl.when(i == nT - 1)
        def _():
            for e in exchanges:
                e.finish()

    row = functools.partial(_row_spec, TM)
    col = functools.partial(_col_spec, TM)
    hbm = pl.BlockSpec(memory_space=pl.ANY)
    S = jax.ShapeDtypeStruct
    return pl.pallas_call(
        body, name="a_bwd", grid=(nT,),
        in_specs=[row(D), row(LANES), row(LANES), row(D), _const_spec((1, D)), _const_spec(w_kv.shape),
                  _const_spec(wa_out.shape), _const_spec(ws.shape),
                  _const_spec((1, AW)), _const_spec((1, AW)), pl.BlockSpec((TM, AW), lambda i: (i, 0)),
                  pl.BlockSpec((TM, AW), lambda i: (i, 2)), row(AW), row(AW), row(LANES),
                  row(LANES), row(LANES), row(LANES)] + [hbm] * nr,
        out_specs=[row(3 * AW), col(AW), col(D), row(2 * LANES), row(D), row(D),
                   _acc_spec((1, D)), _acc_spec((1, 2 * LANES)), _acc_spec((1, AW)),
                   _acc_spec((1, AW)), _acc_spec(ws.shape), _acc_spec((CHUNK, LANES))] + [hbm] * nr,
        out_shape=(S((T, 3 * AW), BF), S((AW, T), BF), S((D, T), BF), S((T, 2 * LANES), BF), S((T, D), BF),
                   S((T, D), F32),
                   S((1, D), F32), S((1, 2 * LANES), F32), S((1, AW), F32), S((1, AW), F32),
                   S(ws.shape, F32), S((CHUNK, LANES), F32)) + tuple(S(r.shape, r.dtype) for r in ready),
        scratch_shapes=[pltpu.VMEM((TM, AW), BF), pltpu.VMEM((TM, AW), F32)] + _direct_sems(nr),
        compiler_params=_params(("arbitrary",)),
    )(dh1p, dk, dv, h1, g_kv, w_kv, wa_out, ws, ln_g, ln_b, z, z, sv, vhat, rstd, rc, rs1, rs2, *ready)


def _a_in_bwd(dz, wa_in, x, dh1, g_a, ready):
    T, D = x.shape
    SH = wa_in.shape[2]
    TM = min(512, T)
    nT = T // TM
    nr = len(ready)

    def body(dz_ref, wain_ref, x_ref, dh1_ref, ga_ref, *rest):
        ready_refs, (dx_ref, n1_ref, dga_ref), rest = rest[:nr], rest[nr:nr + 3], rest[nr + 3:]
        recv_refs, (ssem, rsem, lsem) = rest[:nr], rest[nr:]
        i = pl.program_id(0)
        exchanges = [_Direct(ready_refs[k], recv_refs[k], ssem.at[k], rsem.at[k], lsem.at[k], scatter=True)
                     for k in range(nr)]

        @pl.when(i == 0)
        def _():
            for e in exchanges:
                e.start()
            dga_ref[...] = jnp.zeros_like(dga_ref)

        xv = x_ref[...]
        r1 = lax.rsqrt(jnp.mean(xv * xv, axis=-1, keepdims=True) + EPS)
        xh = xv * r1
        ga = ga_ref[...]
        n1_ref[...] = (xh * ga).astype(BF).T
        dn1 = None
        for j in range(N_DEV):
            part = _dot_nt(dz_ref[:, j * SH:(j + 1) * SH], wain_ref[j])
            dn1 = part if dn1 is None else dn1 + part
        dga_ref[...] += jnp.sum(dn1 * xh, axis=0, keepdims=True)
        dx_ref[...] = dh1_ref[...] + _rms_bwd(dn1, xh, r1, ga)

        @pl.when(i == nT - 1)
        def _():
            for e in exchanges:
                e.finish()

    row = functools.partial(_row_spec, TM)
    hbm = pl.BlockSpec(memory_space=pl.ANY)
    S = jax.ShapeDtypeStruct
    return pl.pallas_call(
        body, name="a_in_bwd", grid=(nT,),
        in_specs=[row(dz.shape[1]), _const_spec(wa_in.shape), row(D), row(D), _const_spec((1, D))] + [hbm] * nr,
        out_specs=[row(D), _col_spec(TM, D), _acc_spec((1, D))] + [hbm] * nr,
        out_shape=(S((T, D), F32), S((D, T), BF), S((1, D), F32)) + tuple(S(r.shape, r.dtype) for r in ready),
        scratch_shapes=_direct_sems(nr),
        compiler_params=_params(("arbitrary",)),
    )(dz, wa_in, x, dh1, g_a, *ready)


def _wgrad(at, b, nblk, name, bt=512):
    K, T = at.shape
    N = b.shape[1] // nblk
    BT = min(bt, T)
    nt = T // BT

    def body(a_ref, b_ref, o_ref, acc):
        t = pl.program_id(1)

        @pl.when(t == 0)
        def _():
            acc[...] = jnp.zeros_like(acc)

        acc[...] += _dot(a_ref[...], b_ref[...])

        @pl.when(t == nt - 1)
        def _():
            o_ref[0] = acc[...].astype(BF)

    return pl.pallas_call(
        body, name=name, grid=(nblk, nt),
        in_specs=[pl.BlockSpec((K, BT), lambda j, t: (0, t)), pl.BlockSpec((BT, N), lambda j, t: (t, j))],
        out_specs=pl.BlockSpec((1, K, N), lambda j, t: (j, 0, 0)),
        out_shape=jax.ShapeDtypeStruct((nblk, K, N), BF),
        scratch_shapes=[pltpu.VMEM((K, N), F32)],
        compiler_params=_params(("arbitrary", "arbitrary")),
    )(at, b)


def _wgrad_exchange(a, b, me, extras, name):
    K, T = a.shape
    N = b.shape[1] // N_DEV
    BT = min(1024, T)
    nt = T // BT
    ne = len(extras)
    last = N_DEV - 1
    n_chip = N_DEV // 2

    def body(me_ref, a_ref, b_ref, *rest):
        ex_in, recv_ref, ex_out = rest[:ne], rest[ne], rest[ne + 1:2 * ne + 1]
        acc, dstage, istage, half, d_s, d_r, i_s, i_r, lsem, ex_ssem, ex_rsem, ex_lsem = rest[2 * ne + 1:]
        s, t = pl.program_id(0), pl.program_id(1)
        x, y, c = (lax.axis_index(ax) for ax in AXES)
        ex = [_Direct(ex_in[k], ex_out[k], ex_ssem.at[k], ex_rsem.at[k], ex_lsem.at[k], scatter=True) for k in range(ne)]

        def to_sibling(k, slot):
            return pltpu.make_async_remote_copy(src_ref=dstage.at[slot], dst_ref=half.at[k], send_sem=d_s.at[k],
                                                recv_sem=d_r.at[k], device_id=(x, y, 1 - c), device_id_type=MESH)

        def to_chip(k, slot, sender):
            far = n_chip - 1 - k
            px, py = x ^ ((far >> 1) & 1), y ^ (far & 1)
            dst = recv_ref.at[2 * x + y] if sender else recv_ref.at[2 * px + py]
            return pltpu.make_async_remote_copy(src_ref=istage.at[slot], dst_ref=dst, send_sem=i_s.at[k],
                                                recv_sem=i_r.at[k], device_id=(px, py, c), device_id_type=MESH)

        @pl.when((s == 0) & (t == 0))
        def _():
            for e in ex:
                e.start()

        @pl.when(t == 0)
        def _():
            acc[...] = jnp.zeros_like(acc)

        acc[...] += _dot(a_ref[...], b_ref[...])

        @pl.when(t == nt - 1)
        def _():
            k = lax.div(s, 2)
            slot = lax.rem(k, 2)

            @pl.when(lax.rem(s, 2) == 0)
            def _():
                @pl.when(k >= 2)
                def _():
                    to_sibling(k - 2, slot).wait_send()

                dstage[slot] = acc[...].astype(BF)
                to_sibling(k, slot).start()

            @pl.when(lax.rem(s, 2) == 1)
            def _():
                to_sibling(k, slot).wait_recv()

                @pl.when(k >= 2)
                def _():
                    to_chip(k - 2, slot, True).wait_send()

                istage[slot] = (acc[...] + half[k].astype(F32)).astype(BF)

                @pl.when(k < n_chip - 1)
                def _():
                    to_chip(k, slot, True).start()

            @pl.when(s == last)
            def _():
                own = pltpu.make_async_copy(istage.at[slot], recv_ref.at[2 * x + y], lsem)
                own.start()
                to_chip(n_chip - 2, 0, True).wait_send()
                to_sibling(n_chip - 2, 0).wait_send()
                to_sibling(n_chip - 1, 1).wait_send()
                for kk in range(n_chip - 1):
                    to_chip(kk, 0, False).wait_recv()
                own.wait()
                for e in ex:
                    e.finish()

    hbm = pl.BlockSpec(memory_space=pl.ANY)
    dma = pltpu.SemaphoreType.DMA
    grid_spec = pltpu.PrefetchScalarGridSpec(
        num_scalar_prefetch=1, grid=(N_DEV, nt),
        in_specs=[pl.BlockSpec((K, BT), lambda s, t, me_ref: (0, t)),
                  pl.BlockSpec((BT, N), lambda s, t, me_ref: (t, me_ref[0] ^ (last - s)))] + [hbm] * ne,
        out_specs=[hbm] * (ne + 1),
        scratch_shapes=[pltpu.VMEM((K, N), F32), pltpu.VMEM((2, K, N), BF), pltpu.VMEM((2, K, N), BF),
                        pltpu.VMEM((n_chip, K, N), BF), dma((n_chip,)), dma((n_chip,)), dma((n_chip - 1,)),
                        dma((n_chip - 1,)), dma] + _direct_sems(ne))
    return pl.pallas_call(
        body, name=name, grid_spec=grid_spec,
        out_shape=[jax.ShapeDtypeStruct((n_chip, K, N), BF)] + [jax.ShapeDtypeStruct(e.shape, e.dtype) for e in extras],
        compiler_params=_params(("arbitrary", "arbitrary")),
    )(me, a, b, *extras)


def _my_index():
    return 4 * lax.axis_index("x") + 2 * lax.axis_index("y") + lax.axis_index("c")


def _all_gather(arrs, dtypes, name):
    n = len(arrs)

    def body(*refs):
        ins, outs = refs[:n], refs[n:2 * n]
        stages = refs[2 * n:3 * n]
        send_sems, recv_sems, local_sems = refs[3 * n:]
        gathers = [_TwoLevel(stages[a], outs[a], send_sems.at[a], recv_sems.at[a], local_sems.at[a]) for a in range(n)]
        for a in range(n):
            stages[a][...] = ins[a][...].astype(stages[a].dtype)
            gathers[a].start()
        for g in gathers:
            g.forward()
        for g in gathers:
            g.finish()

    vm = pl.BlockSpec(memory_space=pltpu.VMEM)
    hbm = pl.BlockSpec(memory_space=pl.ANY)
    return pl.pallas_call(
        body, name=name,
        in_specs=[vm] * n, out_specs=[hbm] * n,
        out_shape=[jax.ShapeDtypeStruct((N_DEV,) + a.shape, dt) for a, dt in zip(arrs, dtypes)],
        scratch_shapes=[pltpu.VMEM(a.shape, dt) for a, dt in zip(arrs, dtypes)]
        + [pltpu.SemaphoreType.DMA((n, 7)), pltpu.SemaphoreType.DMA((n, 7)), pltpu.SemaphoreType.DMA((n,))],
        compiler_params=pltpu.CompilerParams(vmem_limit_bytes=VMEM_LIMIT),
    )(*arrs)


def _peer(mask):
    x, y, c = (lax.axis_index(a) for a in AXES)
    return (x ^ ((mask >> 2) & 1), y ^ ((mask >> 1) & 1), c ^ (mask & 1))


def _dev_index(p):
    return 4 * p[0] + 2 * p[1] + p[2]


class _Direct:
    def __init__(self, src, dst, send_sems, recv_sems, local_sem, scatter):
        me = _my_index()
        self.own = pltpu.make_async_copy(src.at[me] if scatter else src, dst.at[me], local_sem)
        self.sends, self.recvs = [], []
        for k in range(1, N_DEV):
            p = _peer(k)
            pi = _dev_index(p)
            sems = dict(send_sem=send_sems.at[k - 1], recv_sem=recv_sems.at[k - 1], device_id=p, device_id_type=MESH)
            self.sends.append(pltpu.make_async_remote_copy(src_ref=src.at[pi] if scatter else src, dst_ref=dst.at[me],
                                                           **sems))
            self.recvs.append(pltpu.make_async_remote_copy(src_ref=src.at[me] if scatter else src, dst_ref=dst.at[pi],
                                                           **sems))

    def start(self):
        self.own.start()
        for cp in self.sends:
            cp.start()

    def finish(self):
        for cp in self.sends:
            cp.wait_send()
        for cp in self.recvs:
            cp.wait_recv()
        self.own.wait()


class _TwoLevel:
    def __init__(self, src, dst, send_sems, recv_sems, local_sem, own=True):
        x, y, c = (lax.axis_index(a) for a in AXES)
        self.me, self.sibling = (x, y, c), (x, y, 1 - c)
        self.chips = [(1 - x, y), (x, 1 - y), (1 - x, 1 - y)]
        self.src, self.dst, self.send_sems, self.recv_sems = src, dst, send_sems, recv_sems
        self.own = pltpu.make_async_copy(src, dst.at[_dev_index(self.me)], local_sem) if own else None

    def _copy(self, k, block, to, from_src=False):
        slot = self.dst.at[_dev_index(block)]
        return pltpu.make_async_remote_copy(src_ref=self.src if from_src else slot, dst_ref=slot,
                                            send_sem=self.send_sems.at[k], recv_sem=self.recv_sems.at[k],
                                            device_id=to, device_id_type=MESH)

    def _firsts(self):
        c = self.me[2]
        return [self._copy(0, self.me, self.sibling, True)] + [self._copy(1 + j, self.me, (*chip, c), True)
                                                               for j, chip in enumerate(self.chips)]

    def _passed(self):
        c = self.me[2]
        return [self._copy(4 + j, (*chip, c), self.sibling) for j, chip in enumerate(self.chips)]

    def start(self):
        if self.own is not None:
            self.own.start()
        for cp in self._firsts():
            cp.start()

    def wait_sibling(self):
        self._copy(0, self.sibling, self.me).wait_recv()

    def wait_chip_and_forward(self, j):
        self._copy(1 + j, (*self.chips[j], self.me[2]), self.me).wait_recv()
        self._passed()[j].start()

    def wait_passed(self, j):
        self._copy(4 + j, (*self.chips[j], 1 - self.me[2]), self.me).wait_recv()

    def wait_sends(self):
        for cp in self._firsts() + self._passed():
            cp.wait_send()
        if self.own is not None:
            self.own.wait()

    def forward(self):
        for j in range(3):
            self.wait_chip_and_forward(j)

    def finish(self):
        self.wait_sibling()
        for j in range(3):
            self.wait_passed(j)
        self.wait_sends()


def _direct_sems(n):
    if n == 0:
        return []
    return [pltpu.SemaphoreType.DMA((n, 7)), pltpu.SemaphoreType.DMA((n, 7)), pltpu.SemaphoreType.DMA((n,))]


def _adam_math(w, g, m, v):
    m = ADAM_B1 * m + (1.0 - ADAM_B1) * g
    v = ADAM_B2 * v + (1.0 - ADAM_B2) * (g * g)
    m_hat = m / (1.0 - ADAM_B1 ** ADAM_STEP)
    v_hat = v / (1.0 - ADAM_B2 ** ADAM_STEP)
    delta = -ADAM_LR * (m_hat / (jnp.sqrt(v_hat) + ADAM_EPS) + ADAM_WD * w)
    return delta, m, v


def _sum_adam(parts, w, m, v, name):
    R, C = w.shape
    NP = parts.shape[0]
    BR = CHUNK if R % CHUNK == 0 else R

    def body(p_ref, w_ref, m_ref, v_ref, g_ref, d_ref, nm_ref, nv_ref):
        g = p_ref[0].astype(F32)
        for i in range(1, NP):
            g = g + p_ref[i].astype(F32)
        g_ref[...] = g
        d_ref[...], nm_ref[...], nv_ref[...] = _adam_math(w_ref[...], g, m_ref[...], v_ref[...])

    blk = pl.BlockSpec((BR, C), lambda i: (i, 0))
    S = jax.ShapeDtypeStruct((R, C), F32)
    return pl.pallas_call(
        body, name=name, grid=(R // BR,),
        in_specs=[pl.BlockSpec((NP, BR, C), lambda i: (0, i, 0)), blk, blk, blk],
        out_specs=[blk] * 4, out_shape=(S,) * 4,
        compiler_params=_params(("arbitrary",)),
    )(parts, w, m, v)


def _sum8(parts, name):
    _, R, C = parts.shape

    def body(p_ref, o_ref):
        g = p_ref[0]
        for i in range(1, N_DEV):
            g = g + p_ref[i]
        o_ref[...] = g

    return pl.pallas_call(body, name=name, out_shape=jax.ShapeDtypeStruct((R, C), F32))(parts)


SUBLANES = 8


def _nrows(size):
    return -(-size // (SUBLANES * LANES)) * SUBLANES


def _view2d(a):
    return a.reshape(-1, LANES) if a.size % LANES == 0 else a.reshape(1, -1)


def _pack_small(parts, total_rows, name):
    arrs = [p[0] for p in parts]

    def body(*refs):
        out = refs[-1]
        out[...] = jnp.zeros_like(out)
        at = 0
        for ref, (a, rows, flag) in zip(refs[:-1], parts):
            val = ref[...].T if flag == "T" else ref[...]
            r, c = (rows, val.shape[1]) if flag == "T" else val.shape
            out[at:at + r, 0:c] = val[:r]
            at += _nrows(r * c)

    return pl.pallas_call(body, name=name, out_shape=jax.ShapeDtypeStruct((total_rows, LANES), F32))(*arrs)


def _small_update(full, me, reps, shards, name):
    n = len(reps) + len(shards)

    def body(me_ref, full_ref, *refs):
        ins, outs = refs[:3 * n], refs[3 * n:]
        at = 0
        for k in range(n):
            w_ref, m_ref, v_ref = ins[3 * k:3 * k + 3]
            r, c = w_ref.shape
            if k < len(reps):
                g = full_ref[at:at + r, 0:c]
                at += _nrows(r * c)
            else:
                seg = full_ref[at:at + N_DEV * r, :]
                row = lax.broadcasted_iota(jnp.int32, seg.shape, 0)
                pick = [jnp.sum(jnp.where(row == r * me_ref[0] + t, seg, 0.0), axis=0, keepdims=True) for t in range(r)]
                g = pick[0] if r == 1 else jnp.concatenate(pick, axis=0)
                at += N_DEV * r
            g_ref, d_ref, nm_ref, nv_ref = outs[4 * k:4 * k + 4]
            g_ref[...] = g
            d_ref[...], nm_ref[...], nv_ref[...] = _adam_math(w_ref[...], g, m_ref[...], v_ref[...])
        outs[4 * n][...] = full_ref[at:at + 1, 0:1]

    flat = [t for p in reps + shards for t in p]
    S = jax.ShapeDtypeStruct
    res = pl.pallas_call(
        body, name=name,
        in_specs=[pl.BlockSpec(memory_space=pltpu.SMEM)] + [pl.BlockSpec(memory_space=pltpu.VMEM)] * (1 + len(flat)),
        out_shape=[S(p[0].shape, F32) for p in reps + shards for _ in range(4)] + [S((1, 1), F32)],
    )(me, full, *flat)
    return [tuple(res[4 * k:4 * k + 4]) for k in range(n)], res[4 * n]


def _rope_tables(T):
    pos = np.arange(T, dtype=np.float32)
    inv_freq = (np.float64(ROPE_THETA) ** (-np.arange(0, HEAD_DIM, 2, dtype=np.float64) / HEAD_DIM)).astype(np.float32)
    ang = (pos[:, None] * inv_freq[None, :]).astype(np.float64)
    cos, sin, zero = np.cos(ang).astype(np.float32), np.sin(ang).astype(np.float32), np.zeros(ang.shape, np.float32)
    c = np.concatenate([cos, cos, cos, cos], axis=1)
    s1 = np.concatenate([-sin, zero, -sin, zero], axis=1)
    s2 = np.concatenate([zero, sin, zero, sin], axis=1)
    return jnp.asarray(c), jnp.asarray(s1), jnp.asarray(s2)


def kernel(x, a_norm_g, a_w_in, a_ln_g, a_ln_b, a_ws, a_bs, a_w_out, kv_norm_g, w_kv, b_kv, b_norm_g, b_w_in, b_bq, b_sinks, b_w_out, final_norm_g, loss_target, m_a_norm_g, m_a_w_in, m_a_ln_g, m_a_ln_b, m_a_ws, m_a_bs, m_a_w_out, m_kv_norm_g, m_w_kv, m_b_kv, m_b_norm_g, m_b_w_in, m_b_bq, m_b_sinks, m_b_w_out, m_final_norm_g, v_a_norm_g, v_a_w_in, v_a_ln_g, v_a_ln_b, v_a_ws, v_a_bs, v_a_w_out, v_kv_norm_g, v_w_kv, v_b_kv, v_b_norm_g, v_b_w_in, v_b_bq, v_b_sinks, v_b_w_out, v_final_norm_g):
    T, D = x.shape[1], x.shape[2]
    AW = a_ln_g.shape[1] * N_DEV
    G = a_ws.shape[1]
    assert w_kv.shape[1] == 2 * LANES and a_ws.shape[2] == CHUNK and T % CHUNK == 0
    me = _my_index()

    xs, tgt = x[0], loss_target[0]
    vec = jnp.concatenate([a_norm_g, a_ln_g, a_ln_b], axis=1)
    vec = jnp.broadcast_to(vec, (SUBLANES, vec.shape[1]))
    slots = me ^ jnp.array(PASS_MASKS, jnp.int32)
    z, wa_in, vecs, wa_out, wkv = _in_proj(xs, a_w_in[0], vec, slots, [a_w_out[0], w_kv])
    wa_out = wa_out.reshape(AW, D)
    wkv = wkv.reshape(D, 2 * LANES)
    vecs = vecs[:, 0, :]
    ds = D // N_DEV
    g_a = vecs[:, :ds].reshape(1, D)
    ln_g = vecs[:, ds:ds + AW // N_DEV].reshape(1, AW)
    ln_b = vecs[:, ds + AW // N_DEV:].reshape(1, AW)

    rc, rs1, rs2 = _rope_tables(T)
    ws = a_ws[0]
    bs_t = a_bs[0].T
    g_kv = kv_norm_g.reshape(1, D)
    bkv = b_kv.reshape(1, -1)
    g_f = final_norm_g.reshape(1, D)
    sinks = jnp.repeat(b_sinks.reshape(2, 4, 2).transpose(0, 2, 1).reshape(4, 4), CHUNK, axis=1)
    h1, sv, vhat, rstd, k4, v4, kt, vt, wb_in, wb_out = _a_fwd(
        xs, z, ln_g, ln_b, ws, bs_t, wa_out, g_kv, wkv, bkv, rc, rs1, rs2, [b_w_in[0], b_w_out[0]])
    wb_out = wb_out.reshape(-1, D)
    q, g2, o, dh2, dh2_b, loss, d_gf = _b_fwd(h1, b_norm_g, wb_in, b_bq, rc, rs1, rs2, k4, vt, sinks, wb_out, g_f, tgt)
    dh1p, dz2, n2, y2, dk, dv, d_bq, d_gb, d_sink = _b_bwd(dh2, h1, q, g2, o, k4, v4, kt, sinks, wb_out, wb_in,
                                                           b_norm_g, rc, rs1, rs2)
    d_sink = d_sink[:, :4].reshape(2, 2, 4).transpose(0, 2, 1).reshape(1, 16)
    gw_b_in = _wgrad(n2, dz2, N_DEV, "wgrad_b_in", bt=2048)
    gw_b_out = _wgrad(y2, dh2_b, 1, "wgrad_b_out", bt=1024).reshape(N_DEV, -1, D)
    (dz, y, nkv, dkv, dh1, dh1_f, d_gkv, d_bkv, d_lng, d_lnb, d_ws, d_bst, r_b_in, r_b_out) = _a_bwd(
        dh1p, dk, dv, h1, g_kv, wkv, wa_out, ws, ln_g, ln_b, z, sv, vhat, rstd, rc, rs1, rs2, [gw_b_in, gw_b_out])
    gw_a_out = _wgrad(y, dh1, 1, "wgrad_a_out", bt=1024).reshape(N_DEV, AW // N_DEV, D)
    gw_kv = _wgrad(nkv, dkv, 1, "wgrad_kv", bt=2048).reshape(N_DEV, D // N_DEV, 2 * LANES)
    dx, n1, d_ga, r_a_out, r_kv = _a_in_bwd(dz, wa_in, xs, dh1_f, g_a, [gw_a_out, gw_kv])
    small = [(_view2d(d_ws), None, None), (d_bst, G, "T")] + [(_view2d(a), None, None) for a in (
        d_gkv, d_bkv, d_gb, d_bq, d_sink, d_gf, d_ga, d_lng, d_lnb, loss)]
    used = sum(_nrows(G * CHUNK if flag else a.size) for a, _, flag in small)
    per = -(-used // (SUBLANES * N_DEV)) * SUBLANES
    small_pack = _pack_small(small, per * N_DEV, "pack_small").reshape(N_DEV, per, LANES)
    r_a_in, r_small = _wgrad_exchange(n1, dz, me.reshape(1), [small_pack], "wgrad_a_in")

    g_a_in, d_a_in, nm_a_in, nv_a_in = _sum_adam(r_a_in, a_w_in[0], m_a_w_in[0], v_a_w_in[0], "adam_a_in")
    g_a_out, d_a_out, nm_a_out, nv_a_out = _sum_adam(r_a_out, a_w_out[0], m_a_w_out[0], v_a_w_out[0], "adam_a_out")
    g_kvw, d_kvw, nm_kvw, nv_kvw = _sum_adam(r_kv, w_kv, m_w_kv, v_w_kv, "adam_kv")
    g_b_in, d_b_in, nm_b_in, nv_b_in = _sum_adam(r_b_in, b_w_in[0], m_b_w_in[0], v_b_w_in[0], "adam_b_in")
    g_b_out, d_b_out, nm_b_out, nv_b_out = _sum_adam(r_b_out, b_w_out[0], m_b_w_out[0], v_b_w_out[0], "adam_b_out")

    red = _sum8(r_small, "sum_small")
    (full_small,) = _all_gather([red], [F32], "gather_small")
    full_small = full_small.reshape(N_DEV * per, LANES)
    reps = [(a_ws, m_a_ws, v_a_ws), (a_bs, m_a_bs, v_a_bs), (kv_norm_g, m_kv_norm_g, v_kv_norm_g),
            (b_kv, m_b_kv, v_b_kv), (b_norm_g, m_b_norm_g, v_b_norm_g), (b_bq, m_b_bq, v_b_bq),
            (b_sinks, m_b_sinks, v_b_sinks), (final_norm_g, m_final_norm_g, v_final_norm_g)]
    shards = [(a_norm_g, m_a_norm_g, v_a_norm_g), (a_ln_g, m_a_ln_g, v_a_ln_g), (a_ln_b, m_a_ln_b, v_a_ln_b)]
    upd, loss = _small_update(full_small, me.reshape(1), [tuple(_view2d(t) for t in p) for p in reps],
                              [tuple(_view2d(t) for t in p) for p in shards], "adam_small")
    loss = loss[0, 0]
    sm_g, sd, snm, snv = ([upd[k][j].reshape(p[0].shape) for k, p in enumerate(reps + shards)] for j in range(4))

    def order(big, sm):
        a_in, a_out, kvw, b_in, b_out = big
        ws_, bs_, kvg, bkv_, bng, bq_, snk, fng, ang, alng, alnb = sm
        return (ang, a_in[None], alng, alnb, ws_, bs_, a_out[None], kvg, kvw, bkv_, bng, b_in[None], bq_, snk,
                b_out[None], fng)

    grads = order((g_a_in, g_a_out, g_kvw, g_b_in, g_b_out), sm_g)
    deltas = order((d_a_in, d_a_out, d_kvw, d_b_in, d_b_out), sd)
    new_m = order((nm_a_in, nm_a_out, nm_kvw, nm_b_in, nm_b_out), snm)
    new_v = order((nv_a_in, nv_a_out, nv_kvw, nv_b_in, nv_b_out), snv)
    return (loss, dx[None], *grads, *deltas, *new_m, *new_v)
```

```python
import functools

import jax
import jax.numpy as jnp
import numpy as np
from jax import lax
from jax.experimental import pallas as pl
from jax.experimental.pallas import tpu as pltpu

CHUNK = 128
HEAD_DIM = 64
ROPE_THETA = 10000.0
EPS = 1e-5
ADAM_LR = 0.001
ADAM_B1 = 0.9
ADAM_B2 = 0.999
ADAM_EPS = 1e-08
ADAM_WD = 0.01
ADAM_STEP = 10
N_DEV = 8
LANES = 128
NEG = -1e30

BF = jnp.bfloat16
F32 = jnp.float32
MESH = pl.DeviceIdType.MESH
AXES = ("x", "y", "c")
VMEM_LIMIT = 56 * 1024 * 1024


def _dot(a, b):
    return jnp.dot(a, b, preferred_element_type=F32)


def _dot_nt(a, b):
    return lax.dot_general(a, b, (((1,), (1,)), ((), ())), preferred_element_type=F32)


def _dot_tn(a, b):
    return lax.dot_general(a, b, (((0,), (0,)), ((), ())), preferred_element_type=F32)


def _const_spec(shape):
    nd = len(shape)
    return pl.BlockSpec(shape, lambda *_: (0,) * nd, pipeline_mode=pl.Buffered(1))


def _acc_spec(shape):
    nd = len(shape)
    return pl.BlockSpec(shape, lambda *_: (0,) * nd)


def _row_spec(tm, width):
    return pl.BlockSpec((tm, width), lambda i: (i, 0))


def _col_spec(tm, height):
    return pl.BlockSpec((height, tm), lambda i: (0, i))


def _params(sem):
    return pltpu.CompilerParams(dimension_semantics=sem, vmem_limit_bytes=VMEM_LIMIT)


def _rot(x, c, s1, s2):
    return x * c + pltpu.roll(x, 96, 1) * s1 + pltpu.roll(x, 32, 1) * s2


def _rot_bwd(d, c, s1, s2):
    return d * c + pltpu.roll(d * s1, 32, 1) + pltpu.roll(d * s2, 96, 1)


def _silu_parts(g):
    sg = jax.nn.sigmoid(g)
    return g * sg, sg * (1.0 + g * (1.0 - sg))


def _rms_bwd(dn, xh, r, g):
    a = dn * g
    return r * (a - xh * jnp.mean(a * xh, axis=-1, keepdims=True))


def _lane_lo(shape):
    return lax.broadcasted_iota(jnp.int32, shape, 1) < HEAD_DIM


def _split4(t):
    lo = _lane_lo(t.shape)
    tr = pltpu.roll(t, HEAD_DIM, 1)
    z = jnp.zeros_like(t)
    return jnp.concatenate([jnp.where(lo, t, z), jnp.where(lo, z, tr), jnp.where(lo, tr, z), jnp.where(lo, z, t)], axis=1)


def _stack_pairs(t, h):
    return jnp.concatenate([t[:, (h * 4 + j) * LANES:(h * 4 + j + 1) * LANES] for j in range(4)], axis=0)


def _upper():
    shape = (CHUNK, 4 * CHUNK)
    return lax.broadcasted_iota(jnp.int32, shape, 0) > (lax.broadcasted_iota(jnp.int32, shape, 1) & (CHUNK - 1))


def _band_rows(ref, prev, cur, h):
    a = slice(2 * h * LANES, (2 * h + 1) * LANES)
    b = slice((2 * h + 1) * LANES, (2 * h + 2) * LANES)
    return jnp.concatenate([ref[pl.ds(prev, CHUNK), a], ref[pl.ds(cur, CHUNK), a],
                            ref[pl.ds(prev, CHUNK), b], ref[pl.ds(cur, CHUNK), b]], axis=0)


def _band_cols(ref, pci, ci, h):
    a = slice(2 * h * LANES, (2 * h + 1) * LANES)
    b = slice((2 * h + 1) * LANES, (2 * h + 2) * LANES)
    return jnp.concatenate([ref[pci, a, :], ref[ci, a, :], ref[pci, b, :], ref[ci, b, :]], axis=1)


def _fold(t, upper, has_prev=None):
    out = []
    for k in range(2):
        prev = t[2 * k * CHUNK:(2 * k + 1) * CHUNK]
        if has_prev is not None:
            prev = jnp.where(has_prev, prev, NEG)
        out.append(jnp.where(upper, prev, t[(2 * k + 1) * CHUNK:(2 * k + 2) * CHUNK]))
    return out


def _unfold(fa, fb, upper):
    z = jnp.zeros_like(fa)
    return jnp.concatenate([jnp.where(upper, fa, z), jnp.where(upper, z, fa),
                            jnp.where(upper, fb, z), jnp.where(upper, z, fb)], axis=0)


def _softmax_sink(f, sink):
    m = jnp.maximum(jnp.max(f, axis=0, keepdims=True), sink)
    p = jnp.exp(f - m)
    es = jnp.exp(sink - m)
    inv = 1.0 / (jnp.sum(p, axis=0, keepdims=True) + es)
    return p * inv, es * inv


class _Riding:
    def __init__(self, shards, gathered, stages, sems, n_steps):
        self.shards, self.stages, self.n_steps = shards, stages, n_steps
        ssem, rsem, lsem = sems
        self.gathers = [_TwoLevel(stages[k], gathered[k], ssem.at[k], rsem.at[k], lsem.at[k])
                        for k in range(len(shards))]

    def begin(self, i):
        @pl.when(i == 0)
        def _():
            for shard, stage, g in zip(self.shards, self.stages, self.gathers):
                stage[...] = shard[...].astype(stage.dtype)
                g.start()

    def end(self, i):
        @pl.when(i == self.n_steps // 2)
        def _():
            for g in self.gathers:
                g.forward()

        @pl.when(i == self.n_steps - 1)
        def _():
            for g in self.gathers:
                g.finish()

    @staticmethod
    def specs(later):
        nl = len(later)
        hbm = pl.BlockSpec(memory_space=pl.ANY)
        return ([_const_spec(w.shape) for w in later], [hbm] * nl,
                tuple(jax.ShapeDtypeStruct((N_DEV,) + w.shape, BF) for w in later),
                [pltpu.VMEM(w.shape, BF) for w in later] + _direct_sems(nl))


PASS_MASKS = (0, 1, 4, 2, 5, 3, 6, 7)


def _in_proj(x, w_shard, vec_shard, slots, later):
    T, D = x.shape
    SH = w_shard.shape[1]
    VW = vec_shard.shape[1]
    TM = min(512, T)
    nT = T // TM
    nl = len(later)
    ds = D // N_DEV
    last = N_DEV - 1

    def body(slots_ref, x_ref, wsh_ref, vsh_ref, *rest):
        shards, rest = rest[:nl], rest[nl:]
        (z_ref, wout_ref, vout_ref), rest = rest[:3], rest[3:]
        gathered, rest = rest[:nl], rest[nl:]
        (w_scr, vec_scr, vstage, n1_scr, ga_scr, w_s, w_r, w_l, v_s, v_r, v_l), rest = rest[:11], rest[11:]
        stages, sems = rest[:nl], rest[nl:]
        p, i = pl.program_id(0), pl.program_id(1)
        me = _my_index()
        wg = _RelayGather(w_scr, w_s, w_r)
        vg = _TwoLevel(vstage, vec_scr, v_s, v_r, v_l)
        lg = [_TwoLevel(stages[k], gathered[k], sems[0].at[k], sems[1].at[k], sems[2].at[k]) for k in range(nl)]
        w_copy = pltpu.make_async_copy(w_scr, wout_ref, w_l)

        def at_pass(k):
            return (p == k) & (i == 0)

        @pl.when(at_pass(0))
        def _():
            vstage[...] = vsh_ref[...]
            vg.start()
            w_scr[me] = wsh_ref[...].astype(BF)
            wg.start()
            for k in range(nl):
                stages[k][...] = shards[k][...].astype(BF)
                lg[k].start()
            vg.forward()
            vg.finish()
            for j in range(N_DEV):
                ga_scr[:, j * ds:(j + 1) * ds] = vec_scr[j, 0:1, 0:ds]
            vout_ref[...] = vec_scr[...]

        @pl.when(at_pass(1))
        def _():
            wg.wait_sibling()

        for k, landed in ((2, wg.on_x), (3, wg.on_y), (6, wg.on_diag)):
            @pl.when(at_pass(k))
            def _(landed=landed):
                landed()

        for k, j in ((4, 0), (5, 1), (7, 2)):
            @pl.when(at_pass(k))
            def _(j=j):
                wg.wait_passed(j)

        @pl.when(at_pass(last))
        def _():
            w_copy.start()

        @pl.when(p == 0)
        def _():
            xv = x_ref[...]
            r1 = lax.rsqrt(jnp.mean(xv * xv, axis=-1, keepdims=True) + EPS)
            n1_scr[i] = (xv * r1 * ga_scr[...]).astype(BF)

        z_ref[...] = _dot(n1_scr[i], w_scr[slots_ref[p]]).astype(BF)

        @pl.when((p == last) & (i == nT - 1))
        def _():
            wg.wait_sends()
            for g in lg:
                g.forward()
            for g in lg:
                g.finish()
            w_copy.wait()

    hbm = pl.BlockSpec(memory_space=pl.ANY)
    dma = pltpu.SemaphoreType.DMA
    S = jax.ShapeDtypeStruct
    grid_spec = pltpu.PrefetchScalarGridSpec(
        num_scalar_prefetch=1, grid=(N_DEV, nT),
        in_specs=[pl.BlockSpec((TM, D), lambda p, i, s: (jnp.where(p == 0, i, nT - 1), 0)),
                  pl.BlockSpec(w_shard.shape, lambda p, i, s: (0, 0), pipeline_mode=pl.Buffered(1)),
                  pl.BlockSpec(vec_shard.shape, lambda p, i, s: (0, 0), pipeline_mode=pl.Buffered(1))]
        + [pl.BlockSpec(w.shape, lambda p, i, s: (0, 0), pipeline_mode=pl.Buffered(1)) for w in later],
        out_specs=[pl.BlockSpec((TM, SH), lambda p, i, s: (i, s[p])), hbm,
                   pl.BlockSpec((N_DEV,) + vec_shard.shape, lambda p, i, s: (0, 0, 0))] + [hbm] * nl,
        scratch_shapes=[pltpu.VMEM((N_DEV, D, SH), BF), pltpu.VMEM((N_DEV,) + vec_shard.shape, F32),
                        pltpu.VMEM(vec_shard.shape, F32), pltpu.VMEM((nT, TM, D), BF), pltpu.VMEM((1, D), F32),
                        dma((8,)), dma((8,)), dma, dma((7,)), dma((7,)), dma]
        + [pltpu.VMEM(w.shape, BF) for w in later] + _direct_sems(nl))
    return pl.pallas_call(
        body, name="a_in_proj", grid_spec=grid_spec,
        out_shape=(S((T, N_DEV * SH), BF), S((N_DEV, D, SH), BF), S((N_DEV,) + vec_shard.shape, F32))
        + tuple(S((N_DEV,) + w.shape, BF) for w in later),
        compiler_params=_params(("arbitrary", "arbitrary")),
    )(slots, x, w_shard, vec_shard, *later)


def _a_fwd(x, z, ln_g, ln_b, ws, bs_t, wa_out, g_kv, w_kv, b_kv, rc, rs1, rs2, later):
    T, D = x.shape
    AW = wa_out.shape[0]
    G = ws.shape[0]
    TM = min(256, T)
    nT = T // TM
    nC = TM // CHUNK
    nl = len(later)

    def body(x_ref, u_ref, v_ref, gt_ref, lng_ref, lnb_ref, ws_ref, bst_ref, waout_ref, gkv_ref, wkv_ref, bkv_ref,
             rc_ref, rs1_ref, rs2_ref, *rest):
        shards, rest = rest[:nl], rest[nl:]
        (h1_ref, sv_ref, vhat_ref, rstd_ref, k4_ref, v4_ref, kt_ref, vt_ref), rest = rest[:8], rest[8:]
        gathered, sv_scr, stages, sems = rest[:nl], rest[nl], rest[nl + 1:2 * nl + 1], rest[2 * nl + 1:]
        i = pl.program_id(0)
        riding = _Riding(shards, gathered, stages, sems, nT)
        riding.begin(i)
        xv = x_ref[...]
        u = u_ref[...].astype(F32)
        v = v_ref[...].astype(F32)
        gt = gt_ref[...].astype(F32)
        mu = jnp.mean(v, axis=-1, keepdims=True)
        xc = v - mu
        rstd = lax.rsqrt(jnp.mean(xc * xc, axis=-1, keepdims=True) + EPS)
        vhat = xc * rstd
        vln = (vhat * lng_ref[...] + lnb_ref[...]).astype(BF)
        tri = lax.broadcasted_iota(jnp.int32, (CHUNK, CHUNK), 0) >= lax.broadcasted_iota(jnp.int32, (CHUNK, CHUNK), 1)
        for g in range(G):
            wsm = jnp.where(tri, ws_ref[g], 0.0).astype(BF)
            bias = bst_ref[:, g:g + 1]
            for c in range(nC):
                blk = vln[c * CHUNK:(c + 1) * CHUNK, g * CHUNK:(g + 1) * CHUNK]
                sv_scr[c * CHUNK:(c + 1) * CHUNK, g * CHUNK:(g + 1) * CHUNK] = _dot(wsm, blk) + bias
        sv = sv_scr[...]
        silu, _ = _silu_parts(gt)
        y = (u * sv * silu).astype(BF)
        h1 = xv + _dot(y, waout_ref[...])
        h1_ref[...] = h1
        sv_ref[...] = sv.astype(BF)
        vhat_ref[...] = vhat.astype(BF)
        rstd_ref[...] = jnp.broadcast_to(rstd, rstd_ref.shape)
        rkv = lax.rsqrt(jnp.mean(h1 * h1, axis=-1, keepdims=True) + EPS)
        nkv = (h1 * rkv * gkv_ref[...]).astype(BF)
        kv = _dot(nkv, wkv_ref[...]) + bkv_ref[...]
        k_rot = _rot(kv[:, :LANES], rc_ref[...], rs1_ref[...], rs2_ref[...])
        for src, ref, tref in ((k_rot, k4_ref, kt_ref), (kv[:, LANES:], v4_ref, vt_ref)):
            t4 = _split4(src)
            ref[...] = t4.astype(BF)
            for c in range(nC):
                for b in range(4):
                    blk = t4[c * CHUNK:(c + 1) * CHUNK, b * LANES:(b + 1) * LANES]
                    tref[c, b * LANES:(b + 1) * LANES, :] = blk.T.astype(BF)
        riding.end(i)

    row = functools.partial(_row_spec, TM)
    zcol = [pl.BlockSpec((TM, AW), functools.partial(lambda k, i: (i, k), k)) for k in range(3)]
    tr = pl.BlockSpec((nC, 4 * LANES, CHUNK), lambda i: (i, 0, 0))
    r_in, r_out, r_shape, r_scratch = _Riding.specs(later)
    S = jax.ShapeDtypeStruct
    return pl.pallas_call(
        body, name="a_fwd", grid=(nT,),
        in_specs=[row(D)] + zcol + [_const_spec((1, AW)), _const_spec((1, AW)),
                  _const_spec(ws.shape), _const_spec(bs_t.shape), _const_spec(wa_out.shape), _const_spec((1, D)),
                  _const_spec(w_kv.shape), _const_spec((1, 2 * LANES)), row(LANES), row(LANES), row(LANES)] + r_in,
        out_specs=[row(D), row(AW), row(AW), row(LANES), row(4 * LANES), row(4 * LANES), tr, tr] + r_out,
        out_shape=(S((T, D), F32), S((T, AW), BF), S((T, AW), BF), S((T, LANES), F32),
                   S((T, 4 * LANES), BF), S((T, 4 * LANES), BF),
                   S((T // CHUNK, 4 * LANES, CHUNK), BF), S((T // CHUNK, 4 * LANES, CHUNK), BF)) + r_shape,
        scratch_shapes=[pltpu.VMEM((TM, AW), F32)] + r_scratch,
        compiler_params=_params(("arbitrary",)),
    )(x, z, z, z, ln_g, ln_b, ws, bs_t, wa_out, g_kv, w_kv, b_kv, rc, rs1, rs2, *later)


def _b_fwd(h1, g_b, wb_in, bq, rc, rs1, rs2, k4, vt, sinks, wb_out, g_f, target):
    T, D = h1.shape
    BW = wb_out.shape[0]
    SH = wb_in.shape[2]
    TM = min(256, T)
    nC = TM // CHUNK
    nP = BW // LANES

    def body(h1_ref, gb_ref, wbin_ref, bq_ref, rc_ref, rs1_ref, rs2_ref, k4_ref, vt_ref, sink_ref, wbout_ref, gf_ref,
             tgt_ref, q_ref, g2_ref, o_ref, dh2_ref, dh2b_ref, loss_ref, dgf_ref, z_scr, o_scr):
        i = pl.program_id(0)
        h1v = h1_ref[...]
        r2 = lax.rsqrt(jnp.mean(h1v * h1v, axis=-1, keepdims=True) + EPS)
        n2 = (h1v * r2 * gb_ref[...]).astype(BF)
        for j in range(N_DEV):
            z_scr[:, j * SH:(j + 1) * SH] = _dot(n2, wbin_ref[j])
        c_t, s1_t, s2_t = rc_ref[...], rs1_ref[...], rs2_ref[...]
        for p in range(nP):
            cols = slice(p * LANES, (p + 1) * LANES)
            qp = _rot(z_scr[:, cols] + bq_ref[:, cols], c_t, s1_t, s2_t) * (HEAD_DIM ** -0.5)
            q_ref[:, cols] = qp.astype(BF)
        g2 = z_scr[:, BW:]
        g2_ref[...] = g2.astype(BF)
        upper = _upper()
        for c in range(nC):
            ci = i * nC + c
            rows = slice(c * CHUNK, (c + 1) * CHUNK)
            pci = jnp.maximum(ci - 1, 0)
            prev = pl.multiple_of(pci * CHUNK, CHUNK)
            cur = pl.multiple_of(ci * CHUNK, CHUNK)
            qc = q_ref[rows, :]
            for h in range(2):
                st = _dot_nt(_band_rows(k4_ref, prev, cur, h), _stack_pairs(qc, h))
                fa, fb = _fold(st, upper, ci > 0)
                pa, _ = _softmax_sink(fa, sink_ref[2 * h:2 * h + 1, :])
                pb, _ = _softmax_sink(fb, sink_ref[2 * h + 1:2 * h + 2, :])
                ot = _dot(_band_cols(vt_ref, pci, ci, h), _unfold(pa, pb, upper).astype(BF))
                for j in range(4):
                    o_scr[rows, (h * 4 + j) * LANES:(h * 4 + j + 1) * LANES] = ot[:, j * CHUNK:(j + 1) * CHUNK].T
        o = o_scr[...]
        o_ref[...] = o.astype(BF)
        silu, _ = _silu_parts(g2)
        h2 = h1v + _dot((o * silu).astype(BF), wbout_ref[...])
        rf = lax.rsqrt(jnp.mean(h2 * h2, axis=-1, keepdims=True) + EPS)
        xh = h2 * rf
        gf = gf_ref[...]
        err = xh * gf - tgt_ref[...]
        dyf = err * (1.0 / D)
        dh2 = _rms_bwd(dyf, xh, rf, gf)
        dh2_ref[...] = dh2
        dh2b_ref[...] = dh2.astype(BF)

        @pl.when(i == 0)
        def _():
            loss_ref[...] = jnp.zeros_like(loss_ref)
            dgf_ref[...] = jnp.zeros_like(dgf_ref)

        loss_ref[...] += 0.5 * jnp.sum(jnp.mean(err * err, axis=-1, keepdims=True), axis=0, keepdims=True)
        dgf_ref[...] += jnp.sum(dyf * xh, axis=0, keepdims=True)

    row = functools.partial(_row_spec, TM)
    S = jax.ShapeDtypeStruct
    return pl.pallas_call(
        body, name="b_fwd", grid=(T // TM,),
        in_specs=[row(D), _const_spec((1, D)), _const_spec(wb_in.shape), _const_spec((1, BW)), row(LANES), row(LANES),
                  row(LANES), _const_spec(k4.shape), _const_spec(vt.shape), _const_spec(sinks.shape),
                  _const_spec(wb_out.shape), _const_spec((1, D)), row(D)],
        out_specs=[row(BW), row(BW), row(BW), row(D), row(D), _acc_spec((1, 1)), _acc_spec((1, D))],
        out_shape=(S((T, BW), BF), S((T, BW), BF), S((T, BW), BF), S((T, D), F32), S((T, D), BF), S((1, 1), F32),
                   S((1, D), F32)),
        scratch_shapes=[pltpu.VMEM((TM, 2 * BW), F32), pltpu.VMEM((TM, BW), F32)],
        compiler_params=_params(("arbitrary",)),
    )(h1, g_b, wb_in, bq, rc, rs1, rs2, k4, vt, sinks, wb_out, g_f, target)


def _b_bwd(dh2, h1, q, g2, o, k4, v4, kt, sinks, wb_out, wb_in, g_b, rc, rs1, rs2):
    T, D = h1.shape
    BW = wb_out.shape[0]
    SH = wb_in.shape[2]
    TM = min(256, T)
    nT = T // TM
    nC = TM // CHUNK
    nP = BW // LANES

    def body(dh2_ref, h1_ref, q_ref, g2_ref, o_ref, k4_ref, v4_ref, kt_ref, sink_ref, wbout_ref, wbin_ref, gb_ref,
             rc_ref, rs1_ref, rs2_ref,
             dh1_ref, dz2_ref, n2_ref, y2_ref, dk_ref, dv_ref, dbq_ref, dgb_ref, dsink_ref, do_scr, dq_scr, dsacc_scr):
        i = pl.program_id(0)

        @pl.when(i == 0)
        def _():
            dk_ref[...] = jnp.zeros_like(dk_ref)
            dv_ref[...] = jnp.zeros_like(dv_ref)
            dbq_ref[...] = jnp.zeros_like(dbq_ref)
            dgb_ref[...] = jnp.zeros_like(dgb_ref)
            dsacc_scr[...] = jnp.zeros_like(dsacc_scr)

        dh2 = dh2_ref[...]
        dy2 = _dot_nt(dh2.astype(BF), wbout_ref[...]).astype(BF)
        silu, dsilu = _silu_parts(g2_ref[...].astype(F32))
        silu, dsilu = silu.astype(BF), dsilu.astype(BF)
        ob = o_ref[...]
        y2_ref[...] = (ob * silu).T
        do_scr[...] = dy2 * silu
        dz2_ref[:, BW:] = dy2 * ob * dsilu
        upper = _upper()
        lo = _lane_lo((2 * CHUNK, LANES))
        for c in range(nC):
            ci = i * nC + c
            rows = slice(c * CHUNK, (c + 1) * CHUNK)
            pci = jnp.maximum(ci - 1, 0)
            prev = pl.multiple_of(pci * CHUNK, CHUNK)
            cur = pl.multiple_of(ci * CHUNK, CHUNK)
            qc = q_ref[rows, :]
            doc = do_scr[rows, :]
            dkb = jnp.zeros((2 * CHUNK, LANES), F32)
            dvb = jnp.zeros((2 * CHUNK, LANES), F32)
            for h in range(2):
                qs = _stack_pairs(qc, h)
                dos = _stack_pairs(doc, h)
                fa, fb = _fold(_dot_nt(_band_rows(k4_ref, prev, cur, h), qs), upper, ci > 0)
                dfa, dfb = _fold(_dot_nt(_band_rows(v4_ref, prev, cur, h), dos), upper)
                folded = []
                for k, (f, df) in enumerate(((fa, dfa), (fb, dfb))):
                    p, ps = _softmax_sink(f, sink_ref[2 * h + k:2 * h + k + 1, :])
                    delta = jnp.sum(p * df, axis=0, keepdims=True)
                    dsacc_scr[2 * h + k:2 * h + k + 1, :] -= ps * delta
                    folded.append((p, p * (df - delta)))
                pt = _unfold(folded[0][0], folded[1][0], upper).astype(BF)
                dst = _unfold(folded[0][1], folded[1][1], upper).astype(BF)
                dqt = _dot(_band_cols(kt_ref, pci, ci, h), dst)
                for j in range(4):
                    dq_scr[rows, (h * 4 + j) * LANES:(h * 4 + j + 1) * LANES] = dqt[:, j * CHUNK:(j + 1) * CHUNK].T
                for acc_name, g in (("k", _dot(dst, qs)), ("v", _dot(pt, dos))):
                    a, b = g[:2 * CHUNK], g[2 * CHUNK:]
                    if h == 0:
                        part = jnp.where(lo, a + pltpu.roll(b, HEAD_DIM, 1), 0.0)
                    else:
                        part = jnp.where(lo, 0.0, pltpu.roll(a, HEAD_DIM, 1) + b)
                    if acc_name == "k":
                        dkb += part
                    else:
                        dvb += part
            dk_ref[pl.ds(prev, CHUNK), :] += dkb[:CHUNK]
            dk_ref[pl.ds(cur, CHUNK), :] += dkb[CHUNK:]
            dv_ref[pl.ds(prev, CHUNK), :] += dvb[:CHUNK]
            dv_ref[pl.ds(cur, CHUNK), :] += dvb[CHUNK:]

        @pl.when(i == nT - 1)
        def _():
            lane = lax.broadcasted_iota(jnp.int32, dsink_ref.shape, 1)
            tot = jnp.zeros(dsink_ref.shape, F32)
            for j in range(4):
                tot += jnp.where(lane == j, jnp.sum(dsacc_scr[:, j * CHUNK:(j + 1) * CHUNK], axis=1, keepdims=True), 0.0)
            dsink_ref[...] = tot
        c_t, s1_t, s2_t = rc_ref[...], rs1_ref[...], rs2_ref[...]
        for p in range(nP):
            cols = slice(p * LANES, (p + 1) * LANES)
            dqp = _rot_bwd(dq_scr[:, cols] * (HEAD_DIM ** -0.5), c_t, s1_t, s2_t)
            dbq_ref[:, cols] += jnp.sum(dqp, axis=0, keepdims=True)
            dz2_ref[:, cols] = dqp.astype(BF)
        h1v = h1_ref[...]
        r2 = lax.rsqrt(jnp.mean(h1v * h1v, axis=-1, keepdims=True) + EPS)
        xh = h1v * r2
        gb = gb_ref[...]
        n2_ref[...] = (xh * gb).astype(BF).T
        dn2 = None
        for j in range(N_DEV):
            part = _dot_nt(dz2_ref[:, j * SH:(j + 1) * SH], wbin_ref[j])
            dn2 = part if dn2 is None else dn2 + part
        dgb_ref[...] += jnp.sum(dn2 * xh, axis=0, keepdims=True)
        dh1_ref[...] = dh2 + _rms_bwd(dn2, xh, r2, gb)

    row = functools.partial(_row_spec, TM)
    S = jax.ShapeDtypeStruct
    return pl.pallas_call(
        body, name="b_bwd", grid=(T // TM,),
        in_specs=[row(D), row(D), row(BW), row(BW), row(BW), _const_spec(k4.shape), _const_spec(v4.shape),
                  _const_spec(kt.shape), _const_spec(sinks.shape), _const_spec(wb_out.shape), _const_spec(wb_in.shape),
                  _const_spec((1, D)), row(LANES), row(LANES), row(LANES)],
        out_specs=[row(D), row(2 * BW), _col_spec(TM, D), _col_spec(TM, BW), _acc_spec((T, LANES)),
                   _acc_spec((T, LANES)), _acc_spec((1, BW)), _acc_spec((1, D)), _acc_spec((4, LANES))],
        out_shape=(S((T, D), F32), S((T, 2 * BW), BF), S((D, T), BF), S((BW, T), BF), S((T, LANES), F32),
                   S((T, LANES), F32), S((1, BW), F32), S((1, D), F32), S((4, LANES), F32)),
        scratch_shapes=[pltpu.VMEM((TM, BW), BF), pltpu.VMEM((TM, BW), F32), pltpu.VMEM((4, 4 * CHUNK), F32)],
        compiler_params=_params(("arbitrary",)),
    )(dh2, h1, q, g2, o, k4, v4, kt, sinks, wb_out, wb_in, g_b, rc, rs1, rs2)


def _a_bwd(dh1p, dk, dv, h1, g_kv, w_kv, wa_out, ws, ln_g, ln_b, z, sv, vhat, rstd, rc, rs1, rs2, ready):
    T, D = h1.shape
    AW = wa_out.shape[0]
    G = ws.shape[0]
    TM = min(256, T)
    nT = T // TM
    nC = TM // CHUNK
    nr = len(ready)

    def body(dh1p_ref, dk_ref, dv_ref, h1_ref, gkv_ref, wkv_ref, waout_ref, ws_ref, lng_ref,
             lnb_ref, u_ref, gt_ref, sv_ref, vhat_ref, rstd_ref, rc_ref, rs1_ref, rs2_ref, *rest):
        ready_refs, rest = rest[:nr], rest[nr:]
        (dz_ref, y_ref, nkv_ref, dkv_ref, dh1_ref, dh1f_ref, dgkv_ref, dbkv_ref, dlng_ref, dlnb_ref,
         dws_ref, dbs_ref), rest = rest[:12], rest[12:]
        recv_refs, (dsv_scr, dvln_scr, ssem, rsem, lsem) = rest[:nr], rest[nr:]
        i = pl.program_id(0)
        exchanges = [_Direct(ready_refs[k], recv_refs[k], ssem.at[k], rsem.at[k], lsem.at[k], scatter=True)
                     for k in range(nr)]

        @pl.when(i == 0)
        def _():
            for e in exchanges:
                e.start()
            for r in (dgkv_ref, dbkv_ref, dlng_ref, dlnb_ref, dws_ref, dbs_ref):
                r[...] = jnp.zeros_like(r)

        dk_pre = _rot_bwd(dk_ref[...], rc_ref[...], rs1_ref[...], rs2_ref[...])
        dkv = jnp.concatenate([dk_pre, dv_ref[...]], axis=1)
        dbkv_ref[...] += jnp.sum(dkv, axis=0, keepdims=True)
        dkv_b = dkv.astype(BF)
        dkv_ref[...] = dkv_b
        h1v = h1_ref[...]
        rkv = lax.rsqrt(jnp.mean(h1v * h1v, axis=-1, keepdims=True) + EPS)
        xh_kv = h1v * rkv
        gkv = gkv_ref[...]
        nkv_ref[...] = (xh_kv * gkv).astype(BF).T
        dnkv = _dot_nt(dkv_b, wkv_ref[...])
        dgkv_ref[...] += jnp.sum(dnkv * xh_kv, axis=0, keepdims=True)
        dh1 = dh1p_ref[...] + _rms_bwd(dnkv, xh_kv, rkv, gkv)
        dh1_b = dh1.astype(BF)
        dh1_ref[...] = dh1_b
        dh1f_ref[...] = dh1
        dy = _dot_nt(dh1_b, waout_ref[...]).astype(BF)
        silu, dsilu = _silu_parts(gt_ref[...].astype(F32))
        silu, dsilu = silu.astype(BF), dsilu.astype(BF)
        ub, svb = u_ref[...], sv_ref[...]
        us = ub * silu
        dys = dy * svb
        y_ref[...] = (us * svb).T
        dz_ref[:, :AW] = dys * silu
        dz_ref[:, 2 * AW:] = dys * ub * dsilu
        dsv_scr[...] = dy * us
        vhat_v = vhat_ref[...].astype(F32)
        lng = lng_ref[...]
        vln_b = (vhat_v * lng + lnb_ref[...]).astype(BF)
        tri = lax.broadcasted_iota(jnp.int32, (CHUNK, CHUNK), 0) >= lax.broadcasted_iota(jnp.int32, (CHUNK, CHUNK), 1)
        lane = lax.broadcasted_iota(jnp.int32, (CHUNK, LANES), 1)
        dbs = jnp.zeros((CHUNK, LANES), F32)
        for g in range(G):
            wsm = jnp.where(tri, ws_ref[g], 0.0).astype(BF)
            cols = slice(g * CHUNK, (g + 1) * CHUNK)
            dws_g = None
            for c in range(nC):
                rows = slice(c * CHUNK, (c + 1) * CHUNK)
                dsv_cg = dsv_scr[rows, cols]
                dvln_scr[rows, cols] = _dot_tn(wsm, dsv_cg)
                part = _dot_nt(dsv_cg, vln_b[rows, cols])
                dws_g = part if dws_g is None else dws_g + part
                dbs += jnp.where(lane == g, jnp.sum(dsv_cg.astype(F32), axis=-1, keepdims=True), 0.0)
            dws_ref[g] += jnp.where(tri, dws_g, 0.0)
        dbs_ref[...] += dbs
        dvln = dvln_scr[...]
        dlng_ref[...] += jnp.sum(dvln * vhat_v, axis=0, keepdims=True)
        dlnb_ref[...] += jnp.sum(dvln, axis=0, keepdims=True)
        a = dvln * lng
        dvv = rstd_ref[:, 0:1] * (a - jnp.mean(a, axis=-1, keepdims=True)
                                  - vhat_v * jnp.mean(a * vhat_v, axis=-1, keepdims=True))
        dz_ref[:, AW:2 * AW] = dvv.astype(BF)

        @pl.when(i == nT - 1)
        def _():
            for e in exchanges:
                e.finish()

    row = functools.partial(_row_spec, TM)
    col = functools.partial(_col_spec, TM)
    hbm = pl.BlockSpec(memory_space=pl.ANY)
    S = jax.ShapeDtypeStruct
    return pl.pallas_call(
        body, name="a_bwd", grid=(nT,),
        in_specs=[row(D), row(LANES), row(LANES), row(D), _const_spec((1, D)), _const_spec(w_kv.shape),
                  _const_spec(wa_out.shape), _const_spec(ws.shape),
                  _const_spec((1, AW)), _const_spec((1, AW)), pl.BlockSpec((TM, AW), lambda i: (i, 0)),
                  pl.BlockSpec((TM, AW), lambda i: (i, 2)), row(AW), row(AW), row(LANES),
                  row(LANES), row(LANES), row(LANES)] + [hbm] * nr,
        out_specs=[row(3 * AW), col(AW), col(D), row(2 * LANES), row(D), row(D),
                   _acc_spec((1, D)), _acc_spec((1, 2 * LANES)), _acc_spec((1, AW)),
                   _acc_spec((1, AW)), _acc_spec(ws.shape), _acc_spec((CHUNK, LANES))] + [hbm] * nr,
        out_shape=(S((T, 3 * AW), BF), S((AW, T), BF), S((D, T), BF), S((T, 2 * LANES), BF), S((T, D), BF),
                   S((T, D), F32),
                   S((1, D), F32), S((1, 2 * LANES), F32), S((1, AW), F32), S((1, AW), F32),
                   S(ws.shape, F32), S((CHUNK, LANES), F32)) + tuple(S(r.shape, r.dtype) for r in ready),
        scratch_shapes=[pltpu.VMEM((TM, AW), BF), pltpu.VMEM((TM, AW), F32)] + _direct_sems(nr),
        compiler_params=_params(("arbitrary",)),
    )(dh1p, dk, dv, h1, g_kv, w_kv, wa_out, ws, ln_g, ln_b, z, z, sv, vhat, rstd, rc, rs1, rs2, *ready)


def _a_in_bwd(dz, wa_in, x, dh1, g_a, ready):
    T, D = x.shape
    SH = wa_in.shape[2]
    TM = min(512, T)
    nT = T // TM
    nr = len(ready)

    def body(dz_ref, wain_ref, x_ref, dh1_ref, ga_ref, *rest):
        ready_refs, (dx_ref, n1_ref, dga_ref), rest = rest[:nr], rest[nr:nr + 3], rest[nr + 3:]
        recv_refs, (ssem, rsem, lsem) = rest[:nr], rest[nr:]
        i = pl.program_id(0)
        exchanges = [_Direct(ready_refs[k], recv_refs[k], ssem.at[k], rsem.at[k], lsem.at[k], scatter=True)
                     for k in range(nr)]

        @pl.when(i == 0)
        def _():
            for e in exchanges:
                e.start()
            dga_ref[...] = jnp.zeros_like(dga_ref)

        xv = x_ref[...]
        r1 = lax.rsqrt(jnp.mean(xv * xv, axis=-1, keepdims=True) + EPS)
        xh = xv * r1
        ga = ga_ref[...]
        n1_ref[...] = (xh * ga).astype(BF).T
        dn1 = None
        for j in range(N_DEV):
            part = _dot_nt(dz_ref[:, j * SH:(j + 1) * SH], wain_ref[j])
            dn1 = part if dn1 is None else dn1 + part
        dga_ref[...] += jnp.sum(dn1 * xh, axis=0, keepdims=True)
        dx_ref[...] = dh1_ref[...] + _rms_bwd(dn1, xh, r1, ga)

        @pl.when(i == nT - 1)
        def _():
            for e in exchanges:
                e.finish()

    row = functools.partial(_row_spec, TM)
    hbm = pl.BlockSpec(memory_space=pl.ANY)
    S = jax.ShapeDtypeStruct
    return pl.pallas_call(
        body, name="a_in_bwd", grid=(nT,),
        in_specs=[row(dz.shape[1]), _const_spec(wa_in.shape), row(D), row(D), _const_spec((1, D))] + [hbm] * nr,
        out_specs=[row(D), _col_spec(TM, D), _acc_spec((1, D))] + [hbm] * nr,
        out_shape=(S((T, D), F32), S((D, T), BF), S((1, D), F32)) + tuple(S(r.shape, r.dtype) for r in ready),
        scratch_shapes=_direct_sems(nr),
        compiler_params=_params(("arbitrary",)),
    )(dz, wa_in, x, dh1, g_a, *ready)


def _wgrad(at, b, nblk, name, bt=512):
    K, T = at.shape
    N = b.shape[1] // nblk
    BT = min(bt, T)
    nt = T // BT

    def body(a_ref, b_ref, o_ref, acc):
        t = pl.program_id(0)

        @pl.when(t == 0)
        def _():
            acc[...] = jnp.zeros_like(acc)

        acc[...] += _dot(a_ref[...], b_ref[...])

        @pl.when(t == nt - 1)
        def _():
            for j in range(nblk):
                o_ref[j] = acc[:, j * N:(j + 1) * N].astype(BF)

    return pl.pallas_call(
        body, name=name, grid=(nt,),
        in_specs=[pl.BlockSpec((K, BT), lambda t: (0, t)), pl.BlockSpec((BT, nblk * N), lambda t: (t, 0))],
        out_specs=pl.BlockSpec((nblk, K, N), lambda t: (0, 0, 0)),
        out_shape=jax.ShapeDtypeStruct((nblk, K, N), BF),
        scratch_shapes=[pltpu.VMEM((K, nblk * N), F32)],
        compiler_params=_params(("arbitrary",)),
    )(at, b)


def _wgrad_exchange(a, b, me, extras, name):
    K, T = a.shape
    N = b.shape[1] // N_DEV
    BT = min(1024, T)
    nt = T // BT
    ne = len(extras)
    last = N_DEV - 1
    n_chip = N_DEV // 2

    def body(me_ref, a_ref, b_ref, *rest):
        ex_in, recv_ref, ex_out = rest[:ne], rest[ne], rest[ne + 1:2 * ne + 1]
        acc, dstage, istage, half, d_s, d_r, i_s, i_r, lsem, ex_ssem, ex_rsem, ex_lsem = rest[2 * ne + 1:]
        s, t = pl.program_id(0), pl.program_id(1)
        x, y, c = (lax.axis_index(ax) for ax in AXES)
        ex = [_Direct(ex_in[k], ex_out[k], ex_ssem.at[k], ex_rsem.at[k], ex_lsem.at[k], scatter=True) for k in range(ne)]

        def to_sibling(k, slot):
            return pltpu.make_async_remote_copy(src_ref=dstage.at[slot], dst_ref=half.at[k], send_sem=d_s.at[k],
                                                recv_sem=d_r.at[k], device_id=(x, y, 1 - c), device_id_type=MESH)

        def to_chip(k, slot, sender):
            far = n_chip - 1 - k
            px, py = x ^ ((far >> 1) & 1), y ^ (far & 1)
            dst = recv_ref.at[2 * x + y] if sender else recv_ref.at[2 * px + py]
            return pltpu.make_async_remote_copy(src_ref=istage.at[slot], dst_ref=dst, send_sem=i_s.at[k],
                                                recv_sem=i_r.at[k], device_id=(px, py, c), device_id_type=MESH)

        @pl.when((s == 0) & (t == 0))
        def _():
            for e in ex:
                e.start()

        @pl.when(t == 0)
        def _():
            acc[...] = jnp.zeros_like(acc)

        acc[...] += _dot(a_ref[...], b_ref[...])

        @pl.when(t == nt - 1)
        def _():
            k = lax.div(s, 2)
            slot = lax.rem(k, 2)

            @pl.when(lax.rem(s, 2) == 0)
            def _():
                @pl.when(k >= 2)
                def _():
                    to_sibling(k - 2, slot).wait_send()

                dstage[slot] = acc[...].astype(BF)
                to_sibling(k, slot).start()

            @pl.when(lax.rem(s, 2) == 1)
            def _():
                to_sibling(k, slot).wait_recv()

                @pl.when(k >= 2)
                def _():
                    to_chip(k - 2, slot, True).wait_send()

                istage[slot] = (acc[...] + half[k].astype(F32)).astype(BF)

                @pl.when(k < n_chip - 1)
                def _():
                    to_chip(k, slot, True).start()

            @pl.when(s == last)
            def _():
                own = pltpu.make_async_copy(istage.at[slot], recv_ref.at[2 * x + y], lsem)
                own.start()
                to_chip(n_chip - 2, 0, True).wait_send()
                to_sibling(n_chip - 2, 0).wait_send()
                to_sibling(n_chip - 1, 1).wait_send()
                for kk in range(n_chip - 1):
                    to_chip(kk, 0, False).wait_recv()
                own.wait()
                for e in ex:
                    e.finish()

    hbm = pl.BlockSpec(memory_space=pl.ANY)
    dma = pltpu.SemaphoreType.DMA
    grid_spec = pltpu.PrefetchScalarGridSpec(
        num_scalar_prefetch=1, grid=(N_DEV, nt),
        in_specs=[pl.BlockSpec((K, BT), lambda s, t, me_ref: (0, t)),
                  pl.BlockSpec((BT, N), lambda s, t, me_ref: (t, me_ref[0] ^ (last - s)))] + [hbm] * ne,
        out_specs=[hbm] * (ne + 1),
        scratch_shapes=[pltpu.VMEM((K, N), F32), pltpu.VMEM((2, K, N), BF), pltpu.VMEM((2, K, N), BF),
                        pltpu.VMEM((n_chip, K, N), BF), dma((n_chip,)), dma((n_chip,)), dma((n_chip - 1,)),
                        dma((n_chip - 1,)), dma] + _direct_sems(ne))
    return pl.pallas_call(
        body, name=name, grid_spec=grid_spec,
        out_shape=[jax.ShapeDtypeStruct((n_chip, K, N), BF)] + [jax.ShapeDtypeStruct(e.shape, e.dtype) for e in extras],
        compiler_params=_params(("arbitrary", "arbitrary")),
    )(me, a, b, *extras)


def _my_index():
    return 4 * lax.axis_index("x") + 2 * lax.axis_index("y") + lax.axis_index("c")


def _all_gather(arrs, dtypes, name):
    n = len(arrs)

    def body(*refs):
        ins, outs = refs[:n], refs[n:2 * n]
        stages = refs[2 * n:3 * n]
        send_sems, recv_sems, local_sems = refs[3 * n:]
        gathers = [_TwoLevel(stages[a], outs[a], send_sems.at[a], recv_sems.at[a], local_sems.at[a]) for a in range(n)]
        for a in range(n):
            stages[a][...] = ins[a][...].astype(stages[a].dtype)
            gathers[a].start()
        for g in gathers:
            g.forward()
        for g in gathers:
            g.finish()

    vm = pl.BlockSpec(memory_space=pltpu.VMEM)
    hbm = pl.BlockSpec(memory_space=pl.ANY)
    return pl.pallas_call(
        body, name=name,
        in_specs=[vm] * n, out_specs=[hbm] * n,
        out_shape=[jax.ShapeDtypeStruct((N_DEV,) + a.shape, dt) for a, dt in zip(arrs, dtypes)],
        scratch_shapes=[pltpu.VMEM(a.shape, dt) for a, dt in zip(arrs, dtypes)]
        + [pltpu.SemaphoreType.DMA((n, 7)), pltpu.SemaphoreType.DMA((n, 7)), pltpu.SemaphoreType.DMA((n,))],
        compiler_params=pltpu.CompilerParams(vmem_limit_bytes=VMEM_LIMIT),
    )(*arrs)


def _peer(mask):
    x, y, c = (lax.axis_index(a) for a in AXES)
    return (x ^ ((mask >> 2) & 1), y ^ ((mask >> 1) & 1), c ^ (mask & 1))


def _dev_index(p):
    return 4 * p[0] + 2 * p[1] + p[2]


class _Direct:
    def __init__(self, src, dst, send_sems, recv_sems, local_sem, scatter):
        me = _my_index()
        self.own = pltpu.make_async_copy(src.at[me] if scatter else src, dst.at[me], local_sem)
        self.sends, self.recvs = [], []
        for k in range(1, N_DEV):
            p = _peer(k)
            pi = _dev_index(p)
            sems = dict(send_sem=send_sems.at[k - 1], recv_sem=recv_sems.at[k - 1], device_id=p, device_id_type=MESH)
            self.sends.append(pltpu.make_async_remote_copy(src_ref=src.at[pi] if scatter else src, dst_ref=dst.at[me],
                                                           **sems))
            self.recvs.append(pltpu.make_async_remote_copy(src_ref=src.at[me] if scatter else src, dst_ref=dst.at[pi],
                                                           **sems))

    def start(self):
        self.own.start()
        for cp in self.sends:
            cp.start()

    def finish(self):
        for cp in self.sends:
            cp.wait_send()
        for cp in self.recvs:
            cp.wait_recv()
        self.own.wait()


class _TwoLevel:
    def __init__(self, src, dst, send_sems, recv_sems, local_sem, own=True):
        x, y, c = (lax.axis_index(a) for a in AXES)
        self.me, self.sibling = (x, y, c), (x, y, 1 - c)
        self.chips = [(1 - x, y), (x, 1 - y), (1 - x, 1 - y)]
        self.src, self.dst, self.send_sems, self.recv_sems = src, dst, send_sems, recv_sems
        self.own = pltpu.make_async_copy(src, dst.at[_dev_index(self.me)], local_sem) if own else None

    def _copy(self, k, block, to, from_src=False):
        slot = self.dst.at[_dev_index(block)]
        return pltpu.make_async_remote_copy(src_ref=self.src if from_src else slot, dst_ref=slot,
                                            send_sem=self.send_sems.at[k], recv_sem=self.recv_sems.at[k],
                                            device_id=to, device_id_type=MESH)

    def _firsts(self):
        c = self.me[2]
        return [self._copy(0, self.me, self.sibling, True)] + [self._copy(1 + j, self.me, (*chip, c), True)
                                                               for j, chip in enumerate(self.chips)]

    def _passed(self):
        c = self.me[2]
        return [self._copy(4 + j, (*chip, c), self.sibling) for j, chip in enumerate(self.chips)]

    def start(self):
        if self.own is not None:
            self.own.start()
        for cp in self._firsts():
            cp.start()

    def wait_sibling(self):
        self._copy(0, self.sibling, self.me).wait_recv()

    def wait_chip_and_forward(self, j):
        self._copy(1 + j, (*self.chips[j], self.me[2]), self.me).wait_recv()
        self._passed()[j].start()

    def wait_passed(self, j):
        self._copy(4 + j, (*self.chips[j], 1 - self.me[2]), self.me).wait_recv()

    def wait_sends(self):
        for cp in self._firsts() + self._passed():
            cp.wait_send()
        if self.own is not None:
            self.own.wait()

    def forward(self):
        for j in range(3):
            self.wait_chip_and_forward(j)

    def finish(self):
        self.wait_sibling()
        for j in range(3):
            self.wait_passed(j)
        self.wait_sends()


class _RelayGather:
    def __init__(self, dst, send_sems, recv_sems):
        x, y, c = (lax.axis_index(a) for a in AXES)
        self.c = c
        self.sib, self.xn, self.yn, self.dg = (x, y, 1 - c), (1 - x, y, c), (x, 1 - y, c), (1 - x, 1 - y, c)
        self.me = (x, y, c)
        self.dst, self.send_sems, self.recv_sems = dst, send_sems, recv_sems
        self.half = dst.shape[1] // 2

    def _slot(self, dev, part=None):
        i = _dev_index(dev)
        if part is None:
            return self.dst.at[i]
        return self.dst.at[i, pl.ds(part * self.half, self.half)]

    def _copy(self, k, dev, to, part=None):
        ref = self._slot(dev, part)
        return pltpu.make_async_remote_copy(src_ref=ref, dst_ref=ref, send_sem=self.send_sems.at[k],
                                            recv_sem=self.recv_sems.at[k], device_id=to, device_id_type=MESH)

    def _other(self, dev):
        return (dev[0], dev[1], 1 - self.c)

    def start(self):
        for k, to in enumerate((self.sib, self.xn, self.yn)):
            self._copy(k, self.me, to).start()

    def wait_sibling(self):
        self._copy(0, self.sib, self.me).wait_recv()

    def on_x(self):
        self._copy(1, self.xn, self.me).wait_recv()
        self._copy(3, self.xn, self.yn, part=0).start()
        self._copy(5, self.xn, self.sib).start()

    def on_y(self):
        self._copy(2, self.yn, self.me).wait_recv()
        self._copy(4, self.yn, self.xn, part=1).start()
        self._copy(6, self.yn, self.sib).start()

    def on_diag(self):
        self._copy(3, self.dg, self.me, part=0).wait_recv()
        self._copy(4, self.dg, self.me, part=1).wait_recv()
        self._copy(7, self.dg, self.sib).start()

    def wait_passed(self, j):
        self._copy(5 + j, self._other((self.xn, self.yn, self.dg)[j]), self.me).wait_recv()

    def wait_sends(self):
        for k, to in enumerate((self.sib, self.xn, self.yn)):
            self._copy(k, self.me, to).wait_send()
        self._copy(3, self.xn, self.yn, part=0).wait_send()
        self._copy(4, self.yn, self.xn, part=1).wait_send()
        for j, dev in enumerate((self.xn, self.yn, self.dg)):
            self._copy(5 + j, dev, self.sib).wait_send()


def _direct_sems(n):
    if n == 0:
        return []
    return [pltpu.SemaphoreType.DMA((n, 7)), pltpu.SemaphoreType.DMA((n, 7)), pltpu.SemaphoreType.DMA((n,))]


def _adam_math(w, g, m, v):
    m = ADAM_B1 * m + (1.0 - ADAM_B1) * g
    v = ADAM_B2 * v + (1.0 - ADAM_B2) * (g * g)
    m_hat = m / (1.0 - ADAM_B1 ** ADAM_STEP)
    v_hat = v / (1.0 - ADAM_B2 ** ADAM_STEP)
    delta = -ADAM_LR * (m_hat / (jnp.sqrt(v_hat) + ADAM_EPS) + ADAM_WD * w)
    return delta, m, v


def _sum_adam(parts, w, m, v, name):
    R, C = w.shape
    NP = parts.shape[0]
    BR = CHUNK if R % CHUNK == 0 else R

    def body(p_ref, w_ref, m_ref, v_ref, g_ref, d_ref, nm_ref, nv_ref):
        g = p_ref[0].astype(F32)
        for i in range(1, NP):
            g = g + p_ref[i].astype(F32)
        g_ref[...] = g
        d_ref[...], nm_ref[...], nv_ref[...] = _adam_math(w_ref[...], g, m_ref[...], v_ref[...])

    blk = pl.BlockSpec((BR, C), lambda i: (i, 0))
    S = jax.ShapeDtypeStruct((R, C), F32)
    return pl.pallas_call(
        body, name=name, grid=(R // BR,),
        in_specs=[pl.BlockSpec((NP, BR, C), lambda i: (0, i, 0)), blk, blk, blk],
        out_specs=[blk] * 4, out_shape=(S,) * 4,
        compiler_params=_params(("arbitrary",)),
    )(parts, w, m, v)


def _sum8(parts, name):
    _, R, C = parts.shape

    def body(p_ref, o_ref):
        g = p_ref[0]
        for i in range(1, N_DEV):
            g = g + p_ref[i]
        o_ref[...] = g

    return pl.pallas_call(body, name=name, out_shape=jax.ShapeDtypeStruct((R, C), F32))(parts)


SUBLANES = 8


def _nrows(size):
    return -(-size // (SUBLANES * LANES)) * SUBLANES


def _view2d(a):
    return a.reshape(-1, LANES) if a.size % LANES == 0 else a.reshape(1, -1)


def _pack_small(parts, total_rows, name):
    arrs = [p[0] for p in parts]

    def body(*refs):
        out = refs[-1]
        out[...] = jnp.zeros_like(out)
        at = 0
        for ref, (a, rows, flag) in zip(refs[:-1], parts):
            val = ref[...].T if flag == "T" else ref[...]
            r, c = (rows, val.shape[1]) if flag == "T" else val.shape
            out[at:at + r, 0:c] = val[:r]
            at += _nrows(r * c)

    return pl.pallas_call(body, name=name, out_shape=jax.ShapeDtypeStruct((total_rows, LANES), F32))(*arrs)


def _small_update(full, me, reps, shards, name):
    n = len(reps) + len(shards)

    def body(me_ref, full_ref, *refs):
        ins, outs = refs[:3 * n], refs[3 * n:]
        at = 0
        for k in range(n):
            w_ref, m_ref, v_ref = ins[3 * k:3 * k + 3]
            r, c = w_ref.shape
            if k < len(reps):
                g = full_ref[at:at + r, 0:c]
                at += _nrows(r * c)
            else:
                seg = full_ref[at:at + N_DEV * r, :]
                row = lax.broadcasted_iota(jnp.int32, seg.shape, 0)
                pick = [jnp.sum(jnp.where(row == r * me_ref[0] + t, seg, 0.0), axis=0, keepdims=True) for t in range(r)]
                g = pick[0] if r == 1 else jnp.concatenate(pick, axis=0)
                at += N_DEV * r
            g_ref, d_ref, nm_ref, nv_ref = outs[4 * k:4 * k + 4]
            g_ref[...] = g
            d_ref[...], nm_ref[...], nv_ref[...] = _adam_math(w_ref[...], g, m_ref[...], v_ref[...])
        outs[4 * n][...] = full_ref[at:at + 1, 0:1]

    flat = [t for p in reps + shards for t in p]
    S = jax.ShapeDtypeStruct
    res = pl.pallas_call(
        body, name=name,
        in_specs=[pl.BlockSpec(memory_space=pltpu.SMEM)] + [pl.BlockSpec(memory_space=pltpu.VMEM)] * (1 + len(flat)),
        out_shape=[S(p[0].shape, F32) for p in reps + shards for _ in range(4)] + [S((1, 1), F32)],
    )(me, full, *flat)
    return [tuple(res[4 * k:4 * k + 4]) for k in range(n)], res[4 * n]


def _rope_tables(T):
    pos = np.arange(T, dtype=np.float32)
    inv_freq = (np.float64(ROPE_THETA) ** (-np.arange(0, HEAD_DIM, 2, dtype=np.float64) / HEAD_DIM)).astype(np.float32)
    ang = (pos[:, None] * inv_freq[None, :]).astype(np.float64)
    cos, sin, zero = np.cos(ang).astype(np.float32), np.sin(ang).astype(np.float32), np.zeros(ang.shape, np.float32)
    c = np.concatenate([cos, cos, cos, cos], axis=1)
    s1 = np.concatenate([-sin, zero, -sin, zero], axis=1)
    s2 = np.concatenate([zero, sin, zero, sin], axis=1)
    return jnp.asarray(c), jnp.asarray(s1), jnp.asarray(s2)


def kernel(x, a_norm_g, a_w_in, a_ln_g, a_ln_b, a_ws, a_bs, a_w_out, kv_norm_g, w_kv, b_kv, b_norm_g, b_w_in, b_bq, b_sinks, b_w_out, final_norm_g, loss_target, m_a_norm_g, m_a_w_in, m_a_ln_g, m_a_ln_b, m_a_ws, m_a_bs, m_a_w_out, m_kv_norm_g, m_w_kv, m_b_kv, m_b_norm_g, m_b_w_in, m_b_bq, m_b_sinks, m_b_w_out, m_final_norm_g, v_a_norm_g, v_a_w_in, v_a_ln_g, v_a_ln_b, v_a_ws, v_a_bs, v_a_w_out, v_kv_norm_g, v_w_kv, v_b_kv, v_b_norm_g, v_b_w_in, v_b_bq, v_b_sinks, v_b_w_out, v_final_norm_g):
    T, D = x.shape[1], x.shape[2]
    AW = a_ln_g.shape[1] * N_DEV
    G = a_ws.shape[1]
    assert w_kv.shape[1] == 2 * LANES and a_ws.shape[2] == CHUNK and T % CHUNK == 0
    me = _my_index()

    xs, tgt = x[0], loss_target[0]
    vec = jnp.concatenate([a_norm_g, a_ln_g, a_ln_b], axis=1)
    vec = jnp.broadcast_to(vec, (SUBLANES, vec.shape[1]))
    slots = me ^ jnp.array(PASS_MASKS, jnp.int32)
    z, wa_in, vecs, wa_out, wkv = _in_proj(xs, a_w_in[0], vec, slots, [a_w_out[0], w_kv])
    wa_out = wa_out.reshape(AW, D)
    wkv = wkv.reshape(D, 2 * LANES)
    vecs = vecs[:, 0, :]
    ds = D // N_DEV
    g_a = vecs[:, :ds].reshape(1, D)
    ln_g = vecs[:, ds:ds + AW // N_DEV].reshape(1, AW)
    ln_b = vecs[:, ds + AW // N_DEV:].reshape(1, AW)

    rc, rs1, rs2 = _rope_tables(T)
    ws = a_ws[0]
    bs_t = a_bs[0].T
    g_kv = kv_norm_g.reshape(1, D)
    bkv = b_kv.reshape(1, -1)
    g_f = final_norm_g.reshape(1, D)
    sinks = jnp.repeat(b_sinks.reshape(2, 4, 2).transpose(0, 2, 1).reshape(4, 4), CHUNK, axis=1)
    h1, sv, vhat, rstd, k4, v4, kt, vt, wb_in, wb_out = _a_fwd(
        xs, z, ln_g, ln_b, ws, bs_t, wa_out, g_kv, wkv, bkv, rc, rs1, rs2, [b_w_in[0], b_w_out[0]])
    wb_out = wb_out.reshape(-1, D)
    q, g2, o, dh2, dh2_b, loss, d_gf = _b_fwd(h1, b_norm_g, wb_in, b_bq, rc, rs1, rs2, k4, vt, sinks, wb_out, g_f, tgt)
    dh1p, dz2, n2, y2, dk, dv, d_bq, d_gb, d_sink = _b_bwd(dh2, h1, q, g2, o, k4, v4, kt, sinks, wb_out, wb_in,
                                                           b_norm_g, rc, rs1, rs2)
    d_sink = d_sink[:, :4].reshape(2, 2, 4).transpose(0, 2, 1).reshape(1, 16)
    gw_b_in = _wgrad(n2, dz2, N_DEV, "wgrad_b_in", bt=1024)
    gw_b_out = _wgrad(y2, dh2_b, 1, "wgrad_b_out", bt=1024).reshape(N_DEV, -1, D)
    (dz, y, nkv, dkv, dh1, dh1_f, d_gkv, d_bkv, d_lng, d_lnb, d_ws, d_bst, r_b_in, r_b_out) = _a_bwd(
        dh1p, dk, dv, h1, g_kv, wkv, wa_out, ws, ln_g, ln_b, z, sv, vhat, rstd, rc, rs1, rs2, [gw_b_in, gw_b_out])
    gw_a_out = _wgrad(y, dh1, 1, "wgrad_a_out", bt=1024).reshape(N_DEV, AW // N_DEV, D)
    gw_kv = _wgrad(nkv, dkv, 1, "wgrad_kv", bt=2048).reshape(N_DEV, D // N_DEV, 2 * LANES)
    dx, n1, d_ga, r_a_out, r_kv = _a_in_bwd(dz, wa_in, xs, dh1_f, g_a, [gw_a_out, gw_kv])
    small = [(_view2d(d_ws), None, None), (d_bst, G, "T")] + [(_view2d(a), None, None) for a in (
        d_gkv, d_bkv, d_gb, d_bq, d_sink, d_gf, d_ga, d_lng, d_lnb, loss)]
    used = sum(_nrows(G * CHUNK if flag else a.size) for a, _, flag in small)
    per = -(-used // (SUBLANES * N_DEV)) * SUBLANES
    small_pack = _pack_small(small, per * N_DEV, "pack_small").reshape(N_DEV, per, LANES)
    r_a_in, r_small = _wgrad_exchange(n1, dz, me.reshape(1), [small_pack], "wgrad_a_in")

    g_a_in, d_a_in, nm_a_in, nv_a_in = _sum_adam(r_a_in, a_w_in[0], m_a_w_in[0], v_a_w_in[0], "adam_a_in")
    g_a_out, d_a_out, nm_a_out, nv_a_out = _sum_adam(r_a_out, a_w_out[0], m_a_w_out[0], v_a_w_out[0], "adam_a_out")
    g_kvw, d_kvw, nm_kvw, nv_kvw = _sum_adam(r_kv, w_kv, m_w_kv, v_w_kv, "adam_kv")
    g_b_in, d_b_in, nm_b_in, nv_b_in = _sum_adam(r_b_in, b_w_in[0], m_b_w_in[0], v_b_w_in[0], "adam_b_in")
    g_b_out, d_b_out, nm_b_out, nv_b_out = _sum_adam(r_b_out, b_w_out[0], m_b_w_out[0], v_b_w_out[0], "adam_b_out")

    red = _sum8(r_small, "sum_small")
    (full_small,) = _all_gather([red], [F32], "gather_small")
    full_small = full_small.reshape(N_DEV * per, LANES)
    reps = [(a_ws, m_a_ws, v_a_ws), (a_bs, m_a_bs, v_a_bs), (kv_norm_g, m_kv_norm_g, v_kv_norm_g),
            (b_kv, m_b_kv, v_b_kv), (b_norm_g, m_b_norm_g, v_b_norm_g), (b_bq, m_b_bq, v_b_bq),
            (b_sinks, m_b_sinks, v_b_sinks), (final_norm_g, m_final_norm_g, v_final_norm_g)]
    shards = [(a_norm_g, m_a_norm_g, v_a_norm_g), (a_ln_g, m_a_ln_g, v_a_ln_g), (a_ln_b, m_a_ln_b, v_a_ln_b)]
    upd, loss = _small_update(full_small, me.reshape(1), [tuple(_view2d(t) for t in p) for p in reps],
                              [tuple(_view2d(t) for t in p) for p in shards], "adam_small")
    loss = loss[0, 0]
    sm_g, sd, snm, snv = ([upd[k][j].reshape(p[0].shape) for k, p in enumerate(reps + shards)] for j in range(4))

    def order(big, sm):
        a_in, a_out, kvw, b_in, b_out = big
        ws_, bs_, kvg, bkv_, bng, bq_, snk, fng, ang, alng, alnb = sm
        return (ang, a_in[None], alng, alnb, ws_, bs_, a_out[None], kvg, kvw, bkv_, bng, b_in[None], bq_, snk,
                b_out[None], fng)

    grads = order((g_a_in, g_a_out, g_kvw, g_b_in, g_b_out), sm_g)
    deltas = order((d_a_in, d_a_out, d_kvw, d_b_in, d_b_out), sd)
    new_m = order((nm_a_in, nm_a_out, nm_kvw, nm_b_in, nm_b_out), snm)
    new_v = order((nv_a_in, nv_a_out, nv_kvw, nv_b_in, nv_b_out), snv)
    return (loss, dx[None], *grads, *deltas, *new_m, *new_v)
```

```python
import functools

import jax
import jax.numpy as jnp
import numpy as np
from jax import lax
from jax.experimental import pallas as pl
from jax.experimental.pallas import tpu as pltpu

CHUNK = 128
HEAD_DIM = 64
ROPE_THETA = 10000.0
EPS = 1e-5
ADAM_LR = 0.001
ADAM_B1 = 0.9
ADAM_B2 = 0.999
ADAM_EPS = 1e-08
ADAM_WD = 0.01
ADAM_STEP = 10
N_DEV = 8
LANES = 128
NEG = -1e30

BF = jnp.bfloat16
F32 = jnp.float32
MESH = pl.DeviceIdType.MESH
AXES = ("x", "y", "c")
VMEM_LIMIT = 56 * 1024 * 1024


def _dot(a, b):
    return jnp.dot(a, b, preferred_element_type=F32)


def _dot_nt(a, b):
    return lax.dot_general(a, b, (((1,), (1,)), ((), ())), preferred_element_type=F32)


def _dot_tn(a, b):
    return lax.dot_general(a, b, (((0,), (0,)), ((), ())), preferred_element_type=F32)


def _const_spec(shape):
    nd = len(shape)
    return pl.BlockSpec(shape, lambda *_: (0,) * nd, pipeline_mode=pl.Buffered(1))


def _acc_spec(shape):
    nd = len(shape)
    return pl.BlockSpec(shape, lambda *_: (0,) * nd)


def _row_spec(tm, width):
    return pl.BlockSpec((tm, width), lambda i: (i, 0))


def _col_spec(tm, height):
    return pl.BlockSpec((height, tm), lambda i: (0, i))


def _params(sem):
    return pltpu.CompilerParams(dimension_semantics=sem, vmem_limit_bytes=VMEM_LIMIT)


def _rot(x, c, s1, s2):
    return x * c + pltpu.roll(x, 96, 1) * s1 + pltpu.roll(x, 32, 1) * s2


def _rot_bwd(d, c, s1, s2):
    return d * c + pltpu.roll(d * s1, 32, 1) + pltpu.roll(d * s2, 96, 1)


def _silu_parts(g):
    sg = jax.nn.sigmoid(g)
    return g * sg, sg * (1.0 + g * (1.0 - sg))


def _rms_bwd(dn, xh, r, g):
    a = dn * g
    return r * (a - xh * jnp.mean(a * xh, axis=-1, keepdims=True))


def _lane_lo(shape):
    return lax.broadcasted_iota(jnp.int32, shape, 1) < HEAD_DIM


def _split4(t):
    lo = _lane_lo(t.shape)
    tr = pltpu.roll(t, HEAD_DIM, 1)
    z = jnp.zeros_like(t)
    return jnp.concatenate([jnp.where(lo, t, z), jnp.where(lo, z, tr), jnp.where(lo, tr, z), jnp.where(lo, z, t)], axis=1)


def _stack_pairs(t, h):
    return jnp.concatenate([t[:, (h * 4 + j) * LANES:(h * 4 + j + 1) * LANES] for j in range(4)], axis=0)


def _upper():
    shape = (CHUNK, 4 * CHUNK)
    return lax.broadcasted_iota(jnp.int32, shape, 0) > (lax.broadcasted_iota(jnp.int32, shape, 1) & (CHUNK - 1))


def _band_rows(ref, prev, cur, h):
    a = slice(2 * h * LANES, (2 * h + 1) * LANES)
    b = slice((2 * h + 1) * LANES, (2 * h + 2) * LANES)
    return jnp.concatenate([ref[pl.ds(prev, CHUNK), a], ref[pl.ds(cur, CHUNK), a],
                            ref[pl.ds(prev, CHUNK), b], ref[pl.ds(cur, CHUNK), b]], axis=0)


def _band_cols(ref, pci, ci, h):
    a = slice(2 * h * LANES, (2 * h + 1) * LANES)
    b = slice((2 * h + 1) * LANES, (2 * h + 2) * LANES)
    return jnp.concatenate([ref[pci, a, :], ref[ci, a, :], ref[pci, b, :], ref[ci, b, :]], axis=1)


def _fold(t, upper, has_prev=None):
    out = []
    for k in range(2):
        prev = t[2 * k * CHUNK:(2 * k + 1) * CHUNK]
        if has_prev is not None:
            prev = jnp.where(has_prev, prev, NEG)
        out.append(jnp.where(upper, prev, t[(2 * k + 1) * CHUNK:(2 * k + 2) * CHUNK]))
    return out


def _unfold(fa, fb, upper):
    z = jnp.zeros_like(fa)
    return jnp.concatenate([jnp.where(upper, fa, z), jnp.where(upper, z, fa),
                            jnp.where(upper, fb, z), jnp.where(upper, z, fb)], axis=0)


def _softmax_sink(f, sink):
    m = jnp.maximum(jnp.max(f, axis=0, keepdims=True), sink)
    p = jnp.exp(f - m)
    es = jnp.exp(sink - m)
    inv = 1.0 / (jnp.sum(p, axis=0, keepdims=True) + es)
    return p * inv, es * inv


class _Riding:
    def __init__(self, shards, gathered, stages, sems, n_steps):
        self.shards, self.stages, self.n_steps = shards, stages, n_steps
        ssem, rsem, lsem = sems
        self.gathers = [_TwoLevel(stages[k], gathered[k], ssem.at[k], rsem.at[k], lsem.at[k])
                        for k in range(len(shards))]

    def begin(self, i):
        @pl.when(i == 0)
        def _():
            for shard, stage, g in zip(self.shards, self.stages, self.gathers):
                stage[...] = shard[...].astype(stage.dtype)
                g.start()

    def end(self, i):
        @pl.when(i == self.n_steps // 2)
        def _():
            for g in self.gathers:
                g.forward()

        @pl.when(i == self.n_steps - 1)
        def _():
            for g in self.gathers:
                g.finish()

    @staticmethod
    def specs(later):
        nl = len(later)
        hbm = pl.BlockSpec(memory_space=pl.ANY)
        return ([_const_spec(w.shape) for w in later], [hbm] * nl,
                tuple(jax.ShapeDtypeStruct((N_DEV,) + w.shape, BF) for w in later),
                [pltpu.VMEM(w.shape, BF) for w in later] + _direct_sems(nl))


PASS_MASKS = (0, 1, 4, 2, 5, 3, 6, 7)


def _in_proj(x, w_shard, vec_shard, slots, later):
    T, D = x.shape
    SH = w_shard.shape[1]
    VW = vec_shard.shape[1]
    TM = min(512, T)
    nT = T // TM
    nl = len(later)
    ds = D // N_DEV
    last = N_DEV - 1

    def body(slots_ref, x_ref, wsh_ref, vsh_ref, *rest):
        shards, rest = rest[:nl], rest[nl:]
        (z_ref, wout_ref, vout_ref), rest = rest[:3], rest[3:]
        gathered, rest = rest[:nl], rest[nl:]
        (w_scr, vec_scr, vstage, n1_scr, ga_scr, w_s, w_r, w_l, v_s, v_r, v_l), rest = rest[:11], rest[11:]
        stages, sems = rest[:nl], rest[nl:]
        p, i = pl.program_id(0), pl.program_id(1)
        me = _my_index()
        wg = _RelayGather(w_scr, w_s, w_r)
        vg = _TwoLevel(vstage, vec_scr, v_s, v_r, v_l)
        lg = [_TwoLevel(stages[k], gathered[k], sems[0].at[k], sems[1].at[k], sems[2].at[k]) for k in range(nl)]
        w_copy = pltpu.make_async_copy(w_scr, wout_ref, w_l)

        def at_pass(k):
            return (p == k) & (i == 0)

        @pl.when(at_pass(0))
        def _():
            vstage[...] = vsh_ref[...]
            vg.start()
            w_scr[me] = wsh_ref[...].astype(BF)
            wg.start()
            for k in range(nl):
                stages[k][...] = shards[k][...].astype(BF)
                lg[k].start()
            vg.forward()
            vg.finish()
            for j in range(N_DEV):
                ga_scr[:, j * ds:(j + 1) * ds] = vec_scr[j, 0:1, 0:ds]
            vout_ref[...] = vec_scr[...]

        @pl.when(at_pass(1))
        def _():
            wg.wait_sibling()

        for k, landed in ((2, wg.on_x), (3, wg.on_y), (6, wg.on_diag)):
            @pl.when(at_pass(k))
            def _(landed=landed):
                landed()

        for k, j in ((4, 0), (5, 1), (7, 2)):
            @pl.when(at_pass(k))
            def _(j=j):
                wg.wait_passed(j)

        @pl.when(at_pass(last))
        def _():
            w_copy.start()

        @pl.when(p == 0)
        def _():
            xv = x_ref[...]
            r1 = lax.rsqrt(jnp.mean(xv * xv, axis=-1, keepdims=True) + EPS)
            n1_scr[i] = (xv * r1 * ga_scr[...]).astype(BF)

        z_ref[...] = _dot(n1_scr[i], w_scr[slots_ref[p]]).astype(BF)

        @pl.when((p == last) & (i == nT - 1))
        def _():
            wg.wait_sends()
            for g in lg:
                g.forward()
            for g in lg:
                g.finish()
            w_copy.wait()

    hbm = pl.BlockSpec(memory_space=pl.ANY)
    dma = pltpu.SemaphoreType.DMA
    S = jax.ShapeDtypeStruct
    grid_spec = pltpu.PrefetchScalarGridSpec(
        num_scalar_prefetch=1, grid=(N_DEV, nT),
        in_specs=[pl.BlockSpec((TM, D), lambda p, i, s: (jnp.where(p == 0, i, nT - 1), 0)),
                  pl.BlockSpec(w_shard.shape, lambda p, i, s: (0, 0), pipeline_mode=pl.Buffered(1)),
                  pl.BlockSpec(vec_shard.shape, lambda p, i, s: (0, 0), pipeline_mode=pl.Buffered(1))]
        + [pl.BlockSpec(w.shape, lambda p, i, s: (0, 0), pipeline_mode=pl.Buffered(1)) for w in later],
        out_specs=[pl.BlockSpec((TM, SH), lambda p, i, s: (i, s[p])), hbm,
                   pl.BlockSpec((N_DEV,) + vec_shard.shape, lambda p, i, s: (0, 0, 0))] + [hbm] * nl,
        scratch_shapes=[pltpu.VMEM((N_DEV, D, SH), BF), pltpu.VMEM((N_DEV,) + vec_shard.shape, F32),
                        pltpu.VMEM(vec_shard.shape, F32), pltpu.VMEM((nT, TM, D), BF), pltpu.VMEM((1, D), F32),
                        dma((8,)), dma((8,)), dma, dma((7,)), dma((7,)), dma]
        + [pltpu.VMEM(w.shape, BF) for w in later] + _direct_sems(nl))
    return pl.pallas_call(
        body, name="a_in_proj", grid_spec=grid_spec,
        out_shape=(S((T, N_DEV * SH), BF), S((N_DEV, D, SH), BF), S((N_DEV,) + vec_shard.shape, F32))
        + tuple(S((N_DEV,) + w.shape, BF) for w in later),
        compiler_params=_params(("arbitrary", "arbitrary")),
    )(slots, x, w_shard, vec_shard, *later)


def _a_fwd(x, z, ln_g, ln_b, ws, bs_t, wa_out, g_kv, w_kv, b_kv, rc, rs1, rs2, later):
    T, D = x.shape
    AW = wa_out.shape[0]
    G = ws.shape[0]
    TM = min(256, T)
    nT = T // TM
    nC = TM // CHUNK
    nl = len(later)

    def body(x_ref, u_ref, v_ref, gt_ref, lng_ref, lnb_ref, ws_ref, bst_ref, waout_ref, gkv_ref, wkv_ref, bkv_ref,
             rc_ref, rs1_ref, rs2_ref, *rest):
        shards, rest = rest[:nl], rest[nl:]
        (h1_ref, sv_ref, vhat_ref, rstd_ref, k4_ref, v4_ref, kt_ref, vt_ref), rest = rest[:8], rest[8:]
        gathered, sv_scr, stages, sems = rest[:nl], rest[nl], rest[nl + 1:2 * nl + 1], rest[2 * nl + 1:]
        i = pl.program_id(0)
        riding = _Riding(shards, gathered, stages, sems, nT)
        riding.begin(i)
        xv = x_ref[...]
        u = u_ref[...].astype(F32)
        v = v_ref[...].astype(F32)
        gt = gt_ref[...].astype(F32)
        mu = jnp.mean(v, axis=-1, keepdims=True)
        xc = v - mu
        rstd = lax.rsqrt(jnp.mean(xc * xc, axis=-1, keepdims=True) + EPS)
        vhat = xc * rstd
        vln = (vhat * lng_ref[...] + lnb_ref[...]).astype(BF)
        tri = lax.broadcasted_iota(jnp.int32, (CHUNK, CHUNK), 0) >= lax.broadcasted_iota(jnp.int32, (CHUNK, CHUNK), 1)
        for g in range(G):
            wsm = jnp.where(tri, ws_ref[g], 0.0).astype(BF)
            bias = bst_ref[:, g:g + 1]
            for c in range(nC):
                blk = vln[c * CHUNK:(c + 1) * CHUNK, g * CHUNK:(g + 1) * CHUNK]
                sv_scr[c * CHUNK:(c + 1) * CHUNK, g * CHUNK:(g + 1) * CHUNK] = _dot(wsm, blk) + bias
        sv = sv_scr[...]
        silu, _ = _silu_parts(gt)
        y = (u * sv * silu).astype(BF)
        h1 = xv + _dot(y, waout_ref[...])
        h1_ref[...] = h1
        sv_ref[...] = sv.astype(BF)
        vhat_ref[...] = vhat.astype(BF)
        rstd_ref[...] = jnp.broadcast_to(rstd, rstd_ref.shape)
        rkv = lax.rsqrt(jnp.mean(h1 * h1, axis=-1, keepdims=True) + EPS)
        nkv = (h1 * rkv * gkv_ref[...]).astype(BF)
        kv = _dot(nkv, wkv_ref[...]) + bkv_ref[...]
        k_rot = _rot(kv[:, :LANES], rc_ref[...], rs1_ref[...], rs2_ref[...])
        for src, ref, tref in ((k_rot, k4_ref, kt_ref), (kv[:, LANES:], v4_ref, vt_ref)):
            t4 = _split4(src)
            ref[...] = t4.astype(BF)
            for c in range(nC):
                for b in range(4):
                    blk = t4[c * CHUNK:(c + 1) * CHUNK, b * LANES:(b + 1) * LANES]
                    tref[c, b * LANES:(b + 1) * LANES, :] = blk.T.astype(BF)
        riding.end(i)

    row = functools.partial(_row_spec, TM)
    zcol = [pl.BlockSpec((TM, AW), functools.partial(lambda k, i: (i, k), k)) for k in range(3)]
    tr = pl.BlockSpec((nC, 4 * LANES, CHUNK), lambda i: (i, 0, 0))
    r_in, r_out, r_shape, r_scratch = _Riding.specs(later)
    S = jax.ShapeDtypeStruct
    return pl.pallas_call(
        body, name="a_fwd", grid=(nT,),
        in_specs=[row(D)] + zcol + [_const_spec((1, AW)), _const_spec((1, AW)),
                  _const_spec(ws.shape), _const_spec(bs_t.shape), _const_spec(wa_out.shape), _const_spec((1, D)),
                  _const_spec(w_kv.shape), _const_spec((1, 2 * LANES)), row(LANES), row(LANES), row(LANES)] + r_in,
        out_specs=[row(D), row(AW), row(AW), row(LANES), row(4 * LANES), row(4 * LANES), tr, tr] + r_out,
        out_shape=(S((T, D), F32), S((T, AW), BF), S((T, AW), BF), S((T, LANES), F32),
                   S((T, 4 * LANES), BF), S((T, 4 * LANES), BF),
                   S((T // CHUNK, 4 * LANES, CHUNK), BF), S((T // CHUNK, 4 * LANES, CHUNK), BF)) + r_shape,
        scratch_shapes=[pltpu.VMEM((TM, AW), F32)] + r_scratch,
        compiler_params=_params(("arbitrary",)),
    )(x, z, z, z, ln_g, ln_b, ws, bs_t, wa_out, g_kv, w_kv, b_kv, rc, rs1, rs2, *later)


def _b_fwd(h1, g_b, wb_in, bq, rc, rs1, rs2, k4, vt, sinks, wb_out, g_f, target):
    T, D = h1.shape
    BW = wb_out.shape[0]
    SH = wb_in.shape[2]
    TM = min(256, T)
    nC = TM // CHUNK
    nP = BW // LANES

    def body(h1_ref, gb_ref, wbin_ref, bq_ref, rc_ref, rs1_ref, rs2_ref, k4_ref, vt_ref, sink_ref, wbout_ref, gf_ref,
             tgt_ref, q_ref, g2_ref, o_ref, dh2_ref, dh2b_ref, loss_ref, dgf_ref, z_scr, o_scr):
        i = pl.program_id(0)
        h1v = h1_ref[...]
        r2 = lax.rsqrt(jnp.mean(h1v * h1v, axis=-1, keepdims=True) + EPS)
        n2 = (h1v * r2 * gb_ref[...]).astype(BF)
        for j in range(N_DEV):
            z_scr[:, j * SH:(j + 1) * SH] = _dot(n2, wbin_ref[j])
        c_t, s1_t, s2_t = rc_ref[...], rs1_ref[...], rs2_ref[...]
        for p in range(nP):
            cols = slice(p * LANES, (p + 1) * LANES)
            qp = _rot(z_scr[:, cols] + bq_ref[:, cols], c_t, s1_t, s2_t) * (HEAD_DIM ** -0.5)
            q_ref[:, cols] = qp.astype(BF)
        g2 = z_scr[:, BW:]
        g2_ref[...] = g2.astype(BF)
        upper = _upper()
        for c in range(nC):
            ci = i * nC + c
            rows = slice(c * CHUNK, (c + 1) * CHUNK)
            pci = jnp.maximum(ci - 1, 0)
            prev = pl.multiple_of(pci * CHUNK, CHUNK)
            cur = pl.multiple_of(ci * CHUNK, CHUNK)
            qc = q_ref[rows, :]
            for h in range(2):
                st = _dot_nt(_band_rows(k4_ref, prev, cur, h), _stack_pairs(qc, h))
                fa, fb = _fold(st, upper, ci > 0)
                pa, _ = _softmax_sink(fa, sink_ref[2 * h:2 * h + 1, :])
                pb, _ = _softmax_sink(fb, sink_ref[2 * h + 1:2 * h + 2, :])
                ot = _dot(_band_cols(vt_ref, pci, ci, h), _unfold(pa, pb, upper).astype(BF))
                for j in range(4):
                    o_scr[rows, (h * 4 + j) * LANES:(h * 4 + j + 1) * LANES] = ot[:, j * CHUNK:(j + 1) * CHUNK].T
        o = o_scr[...]
        o_ref[...] = o.astype(BF)
        silu, _ = _silu_parts(g2)
        h2 = h1v + _dot((o * silu).astype(BF), wbout_ref[...])
        rf = lax.rsqrt(jnp.mean(h2 * h2, axis=-1, keepdims=True) + EPS)
        xh = h2 * rf
        gf = gf_ref[...]
        err = xh * gf - tgt_ref[...]
        dyf = err * (1.0 / D)
        dh2 = _rms_bwd(dyf, xh, rf, gf)
        dh2_ref[...] = dh2
        dh2b_ref[...] = dh2.astype(BF)

        @pl.when(i == 0)
        def _():
            loss_ref[...] = jnp.zeros_like(loss_ref)
            dgf_ref[...] = jnp.zeros_like(dgf_ref)

        loss_ref[...] += 0.5 * jnp.sum(jnp.mean(err * err, axis=-1, keepdims=True), axis=0, keepdims=True)
        dgf_ref[...] += jnp.sum(dyf * xh, axis=0, keepdims=True)

    row = functools.partial(_row_spec, TM)
    S = jax.ShapeDtypeStruct
    return pl.pallas_call(
        body, name="b_fwd", grid=(T // TM,),
        in_specs=[row(D), _const_spec((1, D)), _const_spec(wb_in.shape), _const_spec((1, BW)), row(LANES), row(LANES),
                  row(LANES), _const_spec(k4.shape), _const_spec(vt.shape), _const_spec(sinks.shape),
                  _const_spec(wb_out.shape), _const_spec((1, D)), row(D)],
        out_specs=[row(BW), row(BW), row(BW), row(D), row(D), _acc_spec((1, 1)), _acc_spec((1, D))],
        out_shape=(S((T, BW), BF), S((T, BW), BF), S((T, BW), BF), S((T, D), F32), S((T, D), BF), S((1, 1), F32),
                   S((1, D), F32)),
        scratch_shapes=[pltpu.VMEM((TM, 2 * BW), F32), pltpu.VMEM((TM, BW), F32)],
        compiler_params=_params(("arbitrary",)),
    )(h1, g_b, wb_in, bq, rc, rs1, rs2, k4, vt, sinks, wb_out, g_f, target)


def _b_bwd(dh2, h1, q, g2, o, k4, v4, kt, sinks, wb_out, wb_in, g_b, rc, rs1, rs2):
    T, D = h1.shape
    BW = wb_out.shape[0]
    SH = wb_in.shape[2]
    TM = min(256, T)
    nT = T // TM
    nC = TM // CHUNK
    nP = BW // LANES

    def body(dh2_ref, h1_ref, q_ref, g2_ref, o_ref, k4_ref, v4_ref, kt_ref, sink_ref, wbout_ref, wbin_ref, gb_ref,
             rc_ref, rs1_ref, rs2_ref,
             dh1_ref, dz2_ref, n2_ref, y2_ref, dk_ref, dv_ref, dbq_ref, dgb_ref, dsink_ref, do_scr, dq_scr, dsacc_scr):
        i = pl.program_id(0)

        @pl.when(i == 0)
        def _():
            dk_ref[...] = jnp.zeros_like(dk_ref)
            dv_ref[...] = jnp.zeros_like(dv_ref)
            dbq_ref[...] = jnp.zeros_like(dbq_ref)
            dgb_ref[...] = jnp.zeros_like(dgb_ref)
            dsacc_scr[...] = jnp.zeros_like(dsacc_scr)

        dh2 = dh2_ref[...]
        dy2 = _dot_nt(dh2.astype(BF), wbout_ref[...]).astype(BF)
        silu, dsilu = _silu_parts(g2_ref[...].astype(F32))
        silu, dsilu = silu.astype(BF), dsilu.astype(BF)
        ob = o_ref[...]
        y2_ref[...] = (ob * silu).T
        do_scr[...] = dy2 * silu
        dz2_ref[:, BW:] = dy2 * ob * dsilu
        upper = _upper()
        lo = _lane_lo((2 * CHUNK, LANES))
        for c in range(nC):
            ci = i * nC + c
            rows = slice(c * CHUNK, (c + 1) * CHUNK)
            pci = jnp.maximum(ci - 1, 0)
            prev = pl.multiple_of(pci * CHUNK, CHUNK)
            cur = pl.multiple_of(ci * CHUNK, CHUNK)
            qc = q_ref[rows, :]
            doc = do_scr[rows, :]
            dkb = jnp.zeros((2 * CHUNK, LANES), F32)
            dvb = jnp.zeros((2 * CHUNK, LANES), F32)
            for h in range(2):
                qs = _stack_pairs(qc, h)
                dos = _stack_pairs(doc, h)
                fa, fb = _fold(_dot_nt(_band_rows(k4_ref, prev, cur, h), qs), upper, ci > 0)
                dfa, dfb = _fold(_dot_nt(_band_rows(v4_ref, prev, cur, h), dos), upper)
                folded = []
                for k, (f, df) in enumerate(((fa, dfa), (fb, dfb))):
                    p, ps = _softmax_sink(f, sink_ref[2 * h + k:2 * h + k + 1, :])
                    delta = jnp.sum(p * df, axis=0, keepdims=True)
                    dsacc_scr[2 * h + k:2 * h + k + 1, :] -= ps * delta
                    folded.append((p, p * (df - delta)))
                pt = _unfold(folded[0][0], folded[1][0], upper).astype(BF)
                dst = _unfold(folded[0][1], folded[1][1], upper).astype(BF)
                dqt = _dot(_band_cols(kt_ref, pci, ci, h), dst)
                for j in range(4):
                    dq_scr[rows, (h * 4 + j) * LANES:(h * 4 + j + 1) * LANES] = dqt[:, j * CHUNK:(j + 1) * CHUNK].T
                for acc_name, g in (("k", _dot(dst, qs)), ("v", _dot(pt, dos))):
                    a, b = g[:2 * CHUNK], g[2 * CHUNK:]
                    if h == 0:
                        part = jnp.where(lo, a + pltpu.roll(b, HEAD_DIM, 1), 0.0)
                    else:
                        part = jnp.where(lo, 0.0, pltpu.roll(a, HEAD_DIM, 1) + b)
                    if acc_name == "k":
                        dkb += part
                    else:
                        dvb += part
            dk_ref[pl.ds(prev, CHUNK), :] += dkb[:CHUNK]
            dk_ref[pl.ds(cur, CHUNK), :] += dkb[CHUNK:]
            dv_ref[pl.ds(prev, CHUNK), :] += dvb[:CHUNK]
            dv_ref[pl.ds(cur, CHUNK), :] += dvb[CHUNK:]

        @pl.when(i == nT - 1)
        def _():
            lane = lax.broadcasted_iota(jnp.int32, dsink_ref.shape, 1)
            tot = jnp.zeros(dsink_ref.shape, F32)
            for j in range(4):
                tot += jnp.where(lane == j, jnp.sum(dsacc_scr[:, j * CHUNK:(j + 1) * CHUNK], axis=1, keepdims=True), 0.0)
            dsink_ref[...] = tot
        c_t, s1_t, s2_t = rc_ref[...], rs1_ref[...], rs2_ref[...]
        for p in range(nP):
            cols = slice(p * LANES, (p + 1) * LANES)
            dqp = _rot_bwd(dq_scr[:, cols] * (HEAD_DIM ** -0.5), c_t, s1_t, s2_t)
            dbq_ref[:, cols] += jnp.sum(dqp, axis=0, keepdims=True)
            dz2_ref[:, cols] = dqp.astype(BF)
        h1v = h1_ref[...]
        r2 = lax.rsqrt(jnp.mean(h1v * h1v, axis=-1, keepdims=True) + EPS)
        xh = h1v * r2
        gb = gb_ref[...]
        n2_ref[...] = (xh * gb).astype(BF).T
        dn2 = None
        for j in range(N_DEV):
            part = _dot_nt(dz2_ref[:, j * SH:(j + 1) * SH], wbin_ref[j])
            dn2 = part if dn2 is None else dn2 + part
        dgb_ref[...] += jnp.sum(dn2 * xh, axis=0, keepdims=True)
        dh1_ref[...] = dh2 + _rms_bwd(dn2, xh, r2, gb)

    row = functools.partial(_row_spec, TM)
    S = jax.ShapeDtypeStruct
    return pl.pallas_call(
        body, name="b_bwd", grid=(T // TM,),
        in_specs=[row(D), row(D), row(BW), row(BW), row(BW), _const_spec(k4.shape), _const_spec(v4.shape),
                  _const_spec(kt.shape), _const_spec(sinks.shape), _const_spec(wb_out.shape), _const_spec(wb_in.shape),
                  _const_spec((1, D)), row(LANES), row(LANES), row(LANES)],
        out_specs=[row(D), row(2 * BW), _col_spec(TM, D), _col_spec(TM, BW), _acc_spec((T, LANES)),
                   _acc_spec((T, LANES)), _acc_spec((1, BW)), _acc_spec((1, D)), _acc_spec((4, LANES))],
        out_shape=(S((T, D), F32), S((T, 2 * BW), BF), S((D, T), BF), S((BW, T), BF), S((T, LANES), F32),
                   S((T, LANES), F32), S((1, BW), F32), S((1, D), F32), S((4, LANES), F32)),
        scratch_shapes=[pltpu.VMEM((TM, BW), BF), pltpu.VMEM((TM, BW), F32), pltpu.VMEM((4, 4 * CHUNK), F32)],
        compiler_params=_params(("arbitrary",)),
    )(dh2, h1, q, g2, o, k4, v4, kt, sinks, wb_out, wb_in, g_b, rc, rs1, rs2)


def _a_bwd(dh1p, dk, dv, h1, g_kv, w_kv, wa_out, ws, ln_g, ln_b, z, sv, vhat, rstd, rc, rs1, rs2, ready):
    T, D = h1.shape
    AW = wa_out.shape[0]
    G = ws.shape[0]
    TM = min(256, T)
    nT = T // TM
    nC = TM // CHUNK
    nr = len(ready)

    def body(dh1p_ref, dk_ref, dv_ref, h1_ref, gkv_ref, wkv_ref, waout_ref, ws_ref, lng_ref,
             lnb_ref, u_ref, gt_ref, sv_ref, vhat_ref, rstd_ref, rc_ref, rs1_ref, rs2_ref, *rest):
        ready_refs, rest = rest[:nr], rest[nr:]
        (dz_ref, gwo_ref, gwk_ref, dh1f_ref, dgkv_ref, dbkv_ref, dlng_ref, dlnb_ref,
         dws_ref, dbs_ref), rest = rest[:10], rest[10:]
        recv_refs, (dsv_scr, dvln_scr, acco_scr, acck_scr, ssem, rsem, lsem) = rest[:nr], rest[nr:]
        i = pl.program_id(0)
        exchanges = [_Direct(ready_refs[k], recv_refs[k], ssem.at[k], rsem.at[k], lsem.at[k], scatter=True)
                     for k in range(nr)]

        @pl.when(i == 0)
        def _():
            for e in exchanges:
                e.start()
            for r in (dgkv_ref, dbkv_ref, dlng_ref, dlnb_ref, dws_ref, dbs_ref, acco_scr, acck_scr):
                r[...] = jnp.zeros_like(r)

        dk_pre = _rot_bwd(dk_ref[...], rc_ref[...], rs1_ref[...], rs2_ref[...])
        dkv = jnp.concatenate([dk_pre, dv_ref[...]], axis=1)
        dbkv_ref[...] += jnp.sum(dkv, axis=0, keepdims=True)
        dkv_b = dkv.astype(BF)
        h1v = h1_ref[...]
        rkv = lax.rsqrt(jnp.mean(h1v * h1v, axis=-1, keepdims=True) + EPS)
        xh_kv = h1v * rkv
        gkv = gkv_ref[...]
        acck_scr[...] += _dot((xh_kv * gkv).astype(BF).T, dkv_b)
        dnkv = _dot_nt(dkv_b, wkv_ref[...])
        dgkv_ref[...] += jnp.sum(dnkv * xh_kv, axis=0, keepdims=True)
        dh1 = dh1p_ref[...] + _rms_bwd(dnkv, xh_kv, rkv, gkv)
        dh1_b = dh1.astype(BF)
        dh1f_ref[...] = dh1
        dy = _dot_nt(dh1_b, waout_ref[...]).astype(BF)
        silu, dsilu = _silu_parts(gt_ref[...].astype(F32))
        silu, dsilu = silu.astype(BF), dsilu.astype(BF)
        ub, svb = u_ref[...], sv_ref[...]
        us = ub * silu
        dys = dy * svb
        acco_scr[...] += _dot((us * svb).T, dh1_b)
        dz_ref[:, :AW] = dys * silu
        dz_ref[:, 2 * AW:] = dys * ub * dsilu
        dsv_scr[...] = dy * us
        vhat_v = vhat_ref[...].astype(F32)
        lng = lng_ref[...]
        vln_b = (vhat_v * lng + lnb_ref[...]).astype(BF)
        tri = lax.broadcasted_iota(jnp.int32, (CHUNK, CHUNK), 0) >= lax.broadcasted_iota(jnp.int32, (CHUNK, CHUNK), 1)
        lane = lax.broadcasted_iota(jnp.int32, (CHUNK, LANES), 1)
        dbs = jnp.zeros((CHUNK, LANES), F32)
        for g in range(G):
            wsm = jnp.where(tri, ws_ref[g], 0.0).astype(BF)
            cols = slice(g * CHUNK, (g + 1) * CHUNK)
            dws_g = None
            for c in range(nC):
                rows = slice(c * CHUNK, (c + 1) * CHUNK)
                dsv_cg = dsv_scr[rows, cols]
                dvln_scr[rows, cols] = _dot_tn(wsm, dsv_cg)
                part = _dot_nt(dsv_cg, vln_b[rows, cols])
                dws_g = part if dws_g is None else dws_g + part
                dbs += jnp.where(lane == g, jnp.sum(dsv_cg.astype(F32), axis=-1, keepdims=True), 0.0)
            dws_ref[g] += jnp.where(tri, dws_g, 0.0)
        dbs_ref[...] += dbs
        dvln = dvln_scr[...]
        dlng_ref[...] += jnp.sum(dvln * vhat_v, axis=0, keepdims=True)
        dlnb_ref[...] += jnp.sum(dvln, axis=0, keepdims=True)
        a = dvln * lng
        dvv = rstd_ref[:, 0:1] * (a - jnp.mean(a, axis=-1, keepdims=True)
                                  - vhat_v * jnp.mean(a * vhat_v, axis=-1, keepdims=True))
        dz_ref[:, AW:2 * AW] = dvv.astype(BF)

        @pl.when(i == nT - 1)
        def _():
            for j in range(N_DEV):
                gwo_ref[j] = acco_scr[j * (AW // N_DEV):(j + 1) * (AW // N_DEV)].astype(BF)
                gwk_ref[j] = acck_scr[j * (D // N_DEV):(j + 1) * (D // N_DEV)].astype(BF)
            for e in exchanges:
                e.finish()

    row = functools.partial(_row_spec, TM)
    hbm = pl.BlockSpec(memory_space=pl.ANY)
    S = jax.ShapeDtypeStruct
    gwo_shape, gwk_shape = (N_DEV, AW // N_DEV, D), (N_DEV, D // N_DEV, 2 * LANES)
    return pl.pallas_call(
        body, name="a_bwd", grid=(nT,),
        in_specs=[row(D), row(LANES), row(LANES), row(D), _const_spec((1, D)), _const_spec(w_kv.shape),
                  _const_spec(wa_out.shape), _const_spec(ws.shape),
                  _const_spec((1, AW)), _const_spec((1, AW)), pl.BlockSpec((TM, AW), lambda i: (i, 0)),
                  pl.BlockSpec((TM, AW), lambda i: (i, 2)), row(AW), row(AW), row(LANES),
                  row(LANES), row(LANES), row(LANES)] + [hbm] * nr,
        out_specs=[row(3 * AW), _const_spec(gwo_shape), _const_spec(gwk_shape), row(D),
                   _acc_spec((1, D)), _acc_spec((1, 2 * LANES)), _acc_spec((1, AW)),
                   _acc_spec((1, AW)), _acc_spec(ws.shape), _acc_spec((CHUNK, LANES))] + [hbm] * nr,
        out_shape=(S((T, 3 * AW), BF), S(gwo_shape, BF), S(gwk_shape, BF), S((T, D), F32),
                   S((1, D), F32), S((1, 2 * LANES), F32), S((1, AW), F32), S((1, AW), F32),
                   S(ws.shape, F32), S((CHUNK, LANES), F32)) + tuple(S(r.shape, r.dtype) for r in ready),
        scratch_shapes=[pltpu.VMEM((TM, AW), BF), pltpu.VMEM((TM, AW), F32), pltpu.VMEM((AW, D), F32),
                        pltpu.VMEM((D, 2 * LANES), F32)] + _direct_sems(nr),
        compiler_params=_params(("arbitrary",)),
    )(dh1p, dk, dv, h1, g_kv, w_kv, wa_out, ws, ln_g, ln_b, z, z, sv, vhat, rstd, rc, rs1, rs2, *ready)


def _a_in_bwd(dz, wa_in, x, dh1, g_a, ready):
    T, D = x.shape
    SH = wa_in.shape[2]
    TM = min(512, T)
    nT = T // TM
    nr = len(ready)

    def body(dz_ref, wain_ref, x_ref, dh1_ref, ga_ref, *rest):
        ready_refs, (dx_ref, n1_ref, dga_ref), rest = rest[:nr], rest[nr:nr + 3], rest[nr + 3:]
        recv_refs, (ssem, rsem, lsem) = rest[:nr], rest[nr:]
        i = pl.program_id(0)
        exchanges = [_Direct(ready_refs[k], recv_refs[k], ssem.at[k], rsem.at[k], lsem.at[k], scatter=True)
                     for k in range(nr)]

        @pl.when(i == 0)
        def _():
            for e in exchanges:
                e.start()
            dga_ref[...] = jnp.zeros_like(dga_ref)

        xv = x_ref[...]
        r1 = lax.rsqrt(jnp.mean(xv * xv, axis=-1, keepdims=True) + EPS)
        xh = xv * r1
        ga = ga_ref[...]
        n1_ref[...] = (xh * ga).astype(BF).T
        dn1 = None
        for j in range(N_DEV):
            part = _dot_nt(dz_ref[:, j * SH:(j + 1) * SH], wain_ref[j])
            dn1 = part if dn1 is None else dn1 + part
        dga_ref[...] += jnp.sum(dn1 * xh, axis=0, keepdims=True)
        dx_ref[...] = dh1_ref[...] + _rms_bwd(dn1, xh, r1, ga)

        @pl.when(i == nT - 1)
        def _():
            for e in exchanges:
                e.finish()

    row = functools.partial(_row_spec, TM)
    hbm = pl.BlockSpec(memory_space=pl.ANY)
    S = jax.ShapeDtypeStruct
    return pl.pallas_call(
        body, name="a_in_bwd", grid=(nT,),
        in_specs=[row(dz.shape[1]), _const_spec(wa_in.shape), row(D), row(D), _const_spec((1, D))] + [hbm] * nr,
        out_specs=[row(D), _col_spec(TM, D), _acc_spec((1, D))] + [hbm] * nr,
        out_shape=(S((T, D), F32), S((D, T), BF), S((1, D), F32)) + tuple(S(r.shape, r.dtype) for r in ready),
        scratch_shapes=_direct_sems(nr),
        compiler_params=_params(("arbitrary",)),
    )(dz, wa_in, x, dh1, g_a, *ready)


def _wgrad(at, b, nblk, name, bt=512):
    K, T = at.shape
    N = b.shape[1] // nblk
    BT = min(bt, T)
    nt = T // BT

    def body(a_ref, b_ref, o_ref, acc):
        t = pl.program_id(0)

        @pl.when(t == 0)
        def _():
            acc[...] = jnp.zeros_like(acc)

        acc[...] += _dot(a_ref[...], b_ref[...])

        @pl.when(t == nt - 1)
        def _():
            for j in range(nblk):
                o_ref[j] = acc[:, j * N:(j + 1) * N].astype(BF)

    return pl.pallas_call(
        body, name=name, grid=(nt,),
        in_specs=[pl.BlockSpec((K, BT), lambda t: (0, t)), pl.BlockSpec((BT, nblk * N), lambda t: (t, 0))],
        out_specs=pl.BlockSpec((nblk, K, N), lambda t: (0, 0, 0)),
        out_shape=jax.ShapeDtypeStruct((nblk, K, N), BF),
        scratch_shapes=[pltpu.VMEM((K, nblk * N), F32)],
        compiler_params=_params(("arbitrary",)),
    )(at, b)


def _wgrad_exchange(a, b, me, extras, name):
    K, T = a.shape
    N = b.shape[1] // N_DEV
    BT = min(1024, T)
    nt = T // BT
    ne = len(extras)
    last = N_DEV - 1
    n_chip = N_DEV // 2

    def body(me_ref, a_ref, b_ref, *rest):
        ex_in, recv_ref, ex_out = rest[:ne], rest[ne], rest[ne + 1:2 * ne + 1]
        acc, dstage, istage, half, d_s, d_r, i_s, i_r, lsem, ex_ssem, ex_rsem, ex_lsem = rest[2 * ne + 1:]
        s, t = pl.program_id(0), pl.program_id(1)
        x, y, c = (lax.axis_index(ax) for ax in AXES)
        ex = [_Direct(ex_in[k], ex_out[k], ex_ssem.at[k], ex_rsem.at[k], ex_lsem.at[k], scatter=True) for k in range(ne)]

        def to_sibling(k, slot):
            return pltpu.make_async_remote_copy(src_ref=dstage.at[slot], dst_ref=half.at[k], send_sem=d_s.at[k],
                                                recv_sem=d_r.at[k], device_id=(x, y, 1 - c), device_id_type=MESH)

        def to_chip(k, slot, sender):
            far = n_chip - 1 - k
            px, py = x ^ ((far >> 1) & 1), y ^ (far & 1)
            dst = recv_ref.at[2 * x + y] if sender else recv_ref.at[2 * px + py]
            return pltpu.make_async_remote_copy(src_ref=istage.at[slot], dst_ref=dst, send_sem=i_s.at[k],
                                                recv_sem=i_r.at[k], device_id=(px, py, c), device_id_type=MESH)

        @pl.when((s == 0) & (t == 0))
        def _():
            for e in ex:
                e.start()

        @pl.when(t == 0)
        def _():
            acc[...] = jnp.zeros_like(acc)

        acc[...] += _dot(a_ref[...], b_ref[...])

        @pl.when(t == nt - 1)
        def _():
            k = lax.div(s, 2)
            slot = lax.rem(k, 2)

            @pl.when(lax.rem(s, 2) == 0)
            def _():
                @pl.when(k >= 2)
                def _():
                    to_sibling(k - 2, slot).wait_send()

                dstage[slot] = acc[...].astype(BF)
                to_sibling(k, slot).start()

            @pl.when(lax.rem(s, 2) == 1)
            def _():
                to_sibling(k, slot).wait_recv()

                @pl.when(k >= 2)
                def _():
                    to_chip(k - 2, slot, True).wait_send()

                istage[slot] = (acc[...] + half[k].astype(F32)).astype(BF)

                @pl.when(k < n_chip - 1)
                def _():
                    to_chip(k, slot, True).start()

            @pl.when(s == last)
            def _():
                own = pltpu.make_async_copy(istage.at[slot], recv_ref.at[2 * x + y], lsem)
                own.start()
                to_chip(n_chip - 2, 0, True).wait_send()
                to_sibling(n_chip - 2, 0).wait_send()
                to_sibling(n_chip - 1, 1).wait_send()
                for kk in range(n_chip - 1):
                    to_chip(kk, 0, False).wait_recv()
                own.wait()
                for e in ex:
                    e.finish()

    hbm = pl.BlockSpec(memory_space=pl.ANY)
    dma = pltpu.SemaphoreType.DMA
    grid_spec = pltpu.PrefetchScalarGridSpec(
        num_scalar_prefetch=1, grid=(N_DEV, nt),
        in_specs=[pl.BlockSpec((K, BT), lambda s, t, me_ref: (0, t)),
                  pl.BlockSpec((BT, N), lambda s, t, me_ref: (t, me_ref[0] ^ (last - s)))] + [hbm] * ne,
        out_specs=[hbm] * (ne + 1),
        scratch_shapes=[pltpu.VMEM((K, N), F32), pltpu.VMEM((2, K, N), BF), pltpu.VMEM((2, K, N), BF),
                        pltpu.VMEM((n_chip, K, N), BF), dma((n_chip,)), dma((n_chip,)), dma((n_chip - 1,)),
                        dma((n_chip - 1,)), dma] + _direct_sems(ne))
    return pl.pallas_call(
        body, name=name, grid_spec=grid_spec,
        out_shape=[jax.ShapeDtypeStruct((n_chip, K, N), BF)] + [jax.ShapeDtypeStruct(e.shape, e.dtype) for e in extras],
        compiler_params=_params(("arbitrary", "arbitrary")),
    )(me, a, b, *extras)


def _my_index():
    return 4 * lax.axis_index("x") + 2 * lax.axis_index("y") + lax.axis_index("c")


def _all_gather(arrs, dtypes, name):
    n = len(arrs)

    def body(*refs):
        ins, outs = refs[:n], refs[n:2 * n]
        stages = refs[2 * n:3 * n]
        send_sems, recv_sems, local_sems = refs[3 * n:]
        gathers = [_TwoLevel(stages[a], outs[a], send_sems.at[a], recv_sems.at[a], local_sems.at[a]) for a in range(n)]
        for a in range(n):
            stages[a][...] = ins[a][...].astype(stages[a].dtype)
            gathers[a].start()
        for g in gathers:
            g.forward()
        for g in gathers:
            g.finish()

    vm = pl.BlockSpec(memory_space=pltpu.VMEM)
    hbm = pl.BlockSpec(memory_space=pl.ANY)
    return pl.pallas_call(
        body, name=name,
        in_specs=[vm] * n, out_specs=[hbm] * n,
        out_shape=[jax.ShapeDtypeStruct((N_DEV,) + a.shape, dt) for a, dt in zip(arrs, dtypes)],
        scratch_shapes=[pltpu.VMEM(a.shape, dt) for a, dt in zip(arrs, dtypes)]
        + [pltpu.SemaphoreType.DMA((n, 7)), pltpu.SemaphoreType.DMA((n, 7)), pltpu.SemaphoreType.DMA((n,))],
        compiler_params=pltpu.CompilerParams(vmem_limit_bytes=VMEM_LIMIT),
    )(*arrs)


def _peer(mask):
    x, y, c = (lax.axis_index(a) for a in AXES)
    return (x ^ ((mask >> 2) & 1), y ^ ((mask >> 1) & 1), c ^ (mask & 1))


def _dev_index(p):
    return 4 * p[0] + 2 * p[1] + p[2]


class _Direct:
    def __init__(self, src, dst, send_sems, recv_sems, local_sem, scatter):
        me = _my_index()
        self.own = pltpu.make_async_copy(src.at[me] if scatter else src, dst.at[me], local_sem)
        self.sends, self.recvs = [], []
        for k in range(1, N_DEV):
            p = _peer(k)
            pi = _dev_index(p)
            sems = dict(send_sem=send_sems.at[k - 1], recv_sem=recv_sems.at[k - 1], device_id=p, device_id_type=MESH)
            self.sends.append(pltpu.make_async_remote_copy(src_ref=src.at[pi] if scatter else src, dst_ref=dst.at[me],
                                                           **sems))
            self.recvs.append(pltpu.make_async_remote_copy(src_ref=src.at[me] if scatter else src, dst_ref=dst.at[pi],
                                                           **sems))

    def start(self):
        self.own.start()
        for cp in self.sends:
            cp.start()

    def finish(self):
        for cp in self.sends:
            cp.wait_send()
        for cp in self.recvs:
            cp.wait_recv()
        self.own.wait()


class _TwoLevel:
    def __init__(self, src, dst, send_sems, recv_sems, local_sem, own=True):
        x, y, c = (lax.axis_index(a) for a in AXES)
        self.me, self.sibling = (x, y, c), (x, y, 1 - c)
        self.chips = [(1 - x, y), (x, 1 - y), (1 - x, 1 - y)]
        self.src, self.dst, self.send_sems, self.recv_sems = src, dst, send_sems, recv_sems
        self.own = pltpu.make_async_copy(src, dst.at[_dev_index(self.me)], local_sem) if own else None

    def _copy(self, k, block, to, from_src=False):
        slot = self.dst.at[_dev_index(block)]
        return pltpu.make_async_remote_copy(src_ref=self.src if from_src else slot, dst_ref=slot,
                                            send_sem=self.send_sems.at[k], recv_sem=self.recv_sems.at[k],
                                            device_id=to, device_id_type=MESH)

    def _firsts(self):
        c = self.me[2]
        return [self._copy(0, self.me, self.sibling, True)] + [self._copy(1 + j, self.me, (*chip, c), True)
                                                               for j, chip in enumerate(self.chips)]

    def _passed(self):
        c = self.me[2]
        return [self._copy(4 + j, (*chip, c), self.sibling) for j, chip in enumerate(self.chips)]

    def start(self):
        if self.own is not None:
            self.own.start()
        for cp in self._firsts():
            cp.start()

    def wait_sibling(self):
        self._copy(0, self.sibling, self.me).wait_recv()

    def wait_chip_and_forward(self, j):
        self._copy(1 + j, (*self.chips[j], self.me[2]), self.me).wait_recv()
        self._passed()[j].start()

    def wait_passed(self, j):
        self._copy(4 + j, (*self.chips[j], 1 - self.me[2]), self.me).wait_recv()

    def wait_sends(self):
        for cp in self._firsts() + self._passed():
            cp.wait_send()
        if self.own is not None:
            self.own.wait()

    def forward(self):
        for j in range(3):
            self.wait_chip_and_forward(j)

    def finish(self):
        self.wait_sibling()
        for j in range(3):
            self.wait_passed(j)
        self.wait_sends()


class _RelayGather:
    def __init__(self, dst, send_sems, recv_sems):
        x, y, c = (lax.axis_index(a) for a in AXES)
        self.c = c
        self.sib, self.xn, self.yn, self.dg = (x, y, 1 - c), (1 - x, y, c), (x, 1 - y, c), (1 - x, 1 - y, c)
        self.me = (x, y, c)
        self.dst, self.send_sems, self.recv_sems = dst, send_sems, recv_sems
        self.half = dst.shape[1] // 2

    def _slot(self, dev, part=None):
        i = _dev_index(dev)
        if part is None:
            return self.dst.at[i]
        return self.dst.at[i, pl.ds(part * self.half, self.half)]

    def _copy(self, k, dev, to, part=None):
        ref = self._slot(dev, part)
        return pltpu.make_async_remote_copy(src_ref=ref, dst_ref=ref, send_sem=self.send_sems.at[k],
                                            recv_sem=self.recv_sems.at[k], device_id=to, device_id_type=MESH)

    def _other(self, dev):
        return (dev[0], dev[1], 1 - self.c)

    def start(self):
        for k, to in enumerate((self.sib, self.xn, self.yn)):
            self._copy(k, self.me, to).start()

    def wait_sibling(self):
        self._copy(0, self.sib, self.me).wait_recv()

    def on_x(self):
        self._copy(1, self.xn, self.me).wait_recv()
        self._copy(3, self.xn, self.yn, part=0).start()
        self._copy(5, self.xn, self.sib).start()

    def on_y(self):
        self._copy(2, self.yn, self.me).wait_recv()
        self._copy(4, self.yn, self.xn, part=1).start()
        self._copy(6, self.yn, self.sib).start()

    def on_diag(self):
        self._copy(3, self.dg, self.me, part=0).wait_recv()
        self._copy(4, self.dg, self.me, part=1).wait_recv()
        self._copy(7, self.dg, self.sib).start()

    def wait_passed(self, j):
        self._copy(5 + j, self._other((self.xn, self.yn, self.dg)[j]), self.me).wait_recv()

    def wait_sends(self):
        for k, to in enumerate((self.sib, self.xn, self.yn)):
            self._copy(k, self.me, to).wait_send()
        self._copy(3, self.xn, self.yn, part=0).wait_send()
        self._copy(4, self.yn, self.xn, part=1).wait_send()
        for j, dev in enumerate((self.xn, self.yn, self.dg)):
            self._copy(5 + j, dev, self.sib).wait_send()


def _direct_sems(n):
    if n == 0:
        return []
    return [pltpu.SemaphoreType.DMA((n, 7)), pltpu.SemaphoreType.DMA((n, 7)), pltpu.SemaphoreType.DMA((n,))]


def _adam_math(w, g, m, v):
    m = ADAM_B1 * m + (1.0 - ADAM_B1) * g
    v = ADAM_B2 * v + (1.0 - ADAM_B2) * (g * g)
    m_hat = m / (1.0 - ADAM_B1 ** ADAM_STEP)
    v_hat = v / (1.0 - ADAM_B2 ** ADAM_STEP)
    delta = -ADAM_LR * (m_hat / (jnp.sqrt(v_hat) + ADAM_EPS) + ADAM_WD * w)
    return delta, m, v


def _sum_adam(parts, w, m, v, name):
    R, C = w.shape
    NP = parts.shape[0]
    BR = CHUNK if R % CHUNK == 0 else R

    def body(p_ref, w_ref, m_ref, v_ref, g_ref, d_ref, nm_ref, nv_ref):
        g = p_ref[0].astype(F32)
        for i in range(1, NP):
            g = g + p_ref[i].astype(F32)
        g_ref[...] = g
        d_ref[...], nm_ref[...], nv_ref[...] = _adam_math(w_ref[...], g, m_ref[...], v_ref[...])

    blk = pl.BlockSpec((BR, C), lambda i: (i, 0))
    S = jax.ShapeDtypeStruct((R, C), F32)
    return pl.pallas_call(
        body, name=name, grid=(R // BR,),
        in_specs=[pl.BlockSpec((NP, BR, C), lambda i: (0, i, 0)), blk, blk, blk],
        out_specs=[blk] * 4, out_shape=(S,) * 4,
        compiler_params=_params(("arbitrary",)),
    )(parts, w, m, v)


def _sum8(parts, name):
    _, R, C = parts.shape

    def body(p_ref, o_ref):
        g = p_ref[0]
        for i in range(1, N_DEV):
            g = g + p_ref[i]
        o_ref[...] = g

    return pl.pallas_call(body, name=name, out_shape=jax.ShapeDtypeStruct((R, C), F32))(parts)


SUBLANES = 8


def _nrows(size):
    return -(-size // (SUBLANES * LANES)) * SUBLANES


def _view2d(a):
    return a.reshape(-1, LANES) if a.size % LANES == 0 else a.reshape(1, -1)


def _pack_small(parts, total_rows, name):
    arrs = [p[0] for p in parts]

    def body(*refs):
        out = refs[-1]
        out[...] = jnp.zeros_like(out)
        at = 0
        for ref, (a, rows, flag) in zip(refs[:-1], parts):
            val = ref[...].T if flag == "T" else ref[...]
            r, c = (rows, val.shape[1]) if flag == "T" else val.shape
            out[at:at + r, 0:c] = val[:r]
            at += _nrows(r * c)

    return pl.pallas_call(body, name=name, out_shape=jax.ShapeDtypeStruct((total_rows, LANES), F32))(*arrs)


def _small_update(full, me, reps, shards, name):
    n = len(reps) + len(shards)

    def body(me_ref, full_ref, *refs):
        ins, outs = refs[:3 * n], refs[3 * n:]
        at = 0
        for k in range(n):
            w_ref, m_ref, v_ref = ins[3 * k:3 * k + 3]
            r, c = w_ref.shape
            if k < len(reps):
                g = full_ref[at:at + r, 0:c]
                at += _nrows(r * c)
            else:
                seg = full_ref[at:at + N_DEV * r, :]
                row = lax.broadcasted_iota(jnp.int32, seg.shape, 0)
                pick = [jnp.sum(jnp.where(row == r * me_ref[0] + t, seg, 0.0), axis=0, keepdims=True) for t in range(r)]
                g = pick[0] if r == 1 else jnp.concatenate(pick, axis=0)
                at += N_DEV * r
            g_ref, d_ref, nm_ref, nv_ref = outs[4 * k:4 * k + 4]
            g_ref[...] = g
            d_ref[...], nm_ref[...], nv_ref[...] = _adam_math(w_ref[...], g, m_ref[...], v_ref[...])
        outs[4 * n][...] = full_ref[at:at + 1, 0:1]

    flat = [t for p in reps + shards for t in p]
    S = jax.ShapeDtypeStruct
    res = pl.pallas_call(
        body, name=name,
        in_specs=[pl.BlockSpec(memory_space=pltpu.SMEM)] + [pl.BlockSpec(memory_space=pltpu.VMEM)] * (1 + len(flat)),
        out_shape=[S(p[0].shape, F32) for p in reps + shards for _ in range(4)] + [S((1, 1), F32)],
    )(me, full, *flat)
    return [tuple(res[4 * k:4 * k + 4]) for k in range(n)], res[4 * n]


def _rope_tables(T):
    pos = np.arange(T, dtype=np.float32)
    inv_freq = (np.float64(ROPE_THETA) ** (-np.arange(0, HEAD_DIM, 2, dtype=np.float64) / HEAD_DIM)).astype(np.float32)
    ang = (pos[:, None] * inv_freq[None, :]).astype(np.float64)
    cos, sin, zero = np.cos(ang).astype(np.float32), np.sin(ang).astype(np.float32), np.zeros(ang.shape, np.float32)
    c = np.concatenate([cos, cos, cos, cos], axis=1)
    s1 = np.concatenate([-sin, zero, -sin, zero], axis=1)
    s2 = np.concatenate([zero, sin, zero, sin], axis=1)
    return jnp.asarray(c), jnp.asarray(s1), jnp.asarray(s2)


def kernel(x, a_norm_g, a_w_in, a_ln_g, a_ln_b, a_ws, a_bs, a_w_out, kv_norm_g, w_kv, b_kv, b_norm_g, b_w_in, b_bq, b_sinks, b_w_out, final_norm_g, loss_target, m_a_norm_g, m_a_w_in, m_a_ln_g, m_a_ln_b, m_a_ws, m_a_bs, m_a_w_out, m_kv_norm_g, m_w_kv, m_b_kv, m_b_norm_g, m_b_w_in, m_b_bq, m_b_sinks, m_b_w_out, m_final_norm_g, v_a_norm_g, v_a_w_in, v_a_ln_g, v_a_ln_b, v_a_ws, v_a_bs, v_a_w_out, v_kv_norm_g, v_w_kv, v_b_kv, v_b_norm_g, v_b_w_in, v_b_bq, v_b_sinks, v_b_w_out, v_final_norm_g):
    T, D = x.shape[1], x.shape[2]
    AW = a_ln_g.shape[1] * N_DEV
    G = a_ws.shape[1]
    assert w_kv.shape[1] == 2 * LANES and a_ws.shape[2] == CHUNK and T % CHUNK == 0
    me = _my_index()

    xs, tgt = x[0], loss_target[0]
    vec = jnp.concatenate([a_norm_g, a_ln_g, a_ln_b], axis=1)
    vec = jnp.broadcast_to(vec, (SUBLANES, vec.shape[1]))
    slots = me ^ jnp.array(PASS_MASKS, jnp.int32)
    z, wa_in, vecs, wa_out, wkv = _in_proj(xs, a_w_in[0], vec, slots, [a_w_out[0], w_kv])
    wa_out = wa_out.reshape(AW, D)
    wkv = wkv.reshape(D, 2 * LANES)
    vecs = vecs[:, 0, :]
    ds = D // N_DEV
    g_a = vecs[:, :ds].reshape(1, D)
    ln_g = vecs[:, ds:ds + AW // N_DEV].reshape(1, AW)
    ln_b = vecs[:, ds + AW // N_DEV:].reshape(1, AW)

    rc, rs1, rs2 = _rope_tables(T)
    ws = a_ws[0]
    bs_t = a_bs[0].T
    g_kv = kv_norm_g.reshape(1, D)
    bkv = b_kv.reshape(1, -1)
    g_f = final_norm_g.reshape(1, D)
    sinks = jnp.repeat(b_sinks.reshape(2, 4, 2).transpose(0, 2, 1).reshape(4, 4), CHUNK, axis=1)
    h1, sv, vhat, rstd, k4, v4, kt, vt, wb_in, wb_out = _a_fwd(
        xs, z, ln_g, ln_b, ws, bs_t, wa_out, g_kv, wkv, bkv, rc, rs1, rs2, [b_w_in[0], b_w_out[0]])
    wb_out = wb_out.reshape(-1, D)
    q, g2, o, dh2, dh2_b, loss, d_gf = _b_fwd(h1, b_norm_g, wb_in, b_bq, rc, rs1, rs2, k4, vt, sinks, wb_out, g_f, tgt)
    dh1p, dz2, n2, y2, dk, dv, d_bq, d_gb, d_sink = _b_bwd(dh2, h1, q, g2, o, k4, v4, kt, sinks, wb_out, wb_in,
                                                           b_norm_g, rc, rs1, rs2)
    d_sink = d_sink[:, :4].reshape(2, 2, 4).transpose(0, 2, 1).reshape(1, 16)
    gw_b_in = _wgrad(n2, dz2, N_DEV, "wgrad_b_in", bt=1024)
    gw_b_out = _wgrad(y2, dh2_b, 1, "wgrad_b_out", bt=1024).reshape(N_DEV, -1, D)
    (dz, gw_a_out, gw_kv, dh1_f, d_gkv, d_bkv, d_lng, d_lnb, d_ws, d_bst, r_b_in, r_b_out) = _a_bwd(
        dh1p, dk, dv, h1, g_kv, wkv, wa_out, ws, ln_g, ln_b, z, sv, vhat, rstd, rc, rs1, rs2, [gw_b_in, gw_b_out])
    dx, n1, d_ga, r_a_out, r_kv = _a_in_bwd(dz, wa_in, xs, dh1_f, g_a, [gw_a_out, gw_kv])
    small = [(_view2d(d_ws), None, None), (d_bst, G, "T")] + [(_view2d(a), None, None) for a in (
        d_gkv, d_bkv, d_gb, d_bq, d_sink, d_gf, d_ga, d_lng, d_lnb, loss)]
    used = sum(_nrows(G * CHUNK if flag else a.size) for a, _, flag in small)
    per = -(-used // (SUBLANES * N_DEV)) * SUBLANES
    small_pack = _pack_small(small, per * N_DEV, "pack_small").reshape(N_DEV, per, LANES)
    r_a_in, r_small = _wgrad_exchange(n1, dz, me.reshape(1), [small_pack], "wgrad_a_in")

    g_a_in, d_a_in, nm_a_in, nv_a_in = _sum_adam(r_a_in, a_w_in[0], m_a_w_in[0], v_a_w_in[0], "adam_a_in")
    g_a_out, d_a_out, nm_a_out, nv_a_out = _sum_adam(r_a_out, a_w_out[0], m_a_w_out[0], v_a_w_out[0], "adam_a_out")
    g_kvw, d_kvw, nm_kvw, nv_kvw = _sum_adam(r_kv, w_kv, m_w_kv, v_w_kv, "adam_kv")
    g_b_in, d_b_in, nm_b_in, nv_b_in = _sum_adam(r_b_in, b_w_in[0], m_b_w_in[0], v_b_w_in[0], "adam_b_in")
    g_b_out, d_b_out, nm_b_out, nv_b_out = _sum_adam(r_b_out, b_w_out[0], m_b_w_out[0], v_b_w_out[0], "adam_b_out")

    red = _sum8(r_small, "sum_small")
    (full_small,) = _all_gather([red], [F32], "gather_small")
    full_small = full_small.reshape(N_DEV * per, LANES)
    reps = [(a_ws, m_a_ws, v_a_ws), (a_bs, m_a_bs, v_a_bs), (kv_norm_g, m_kv_norm_g, v_kv_norm_g),
            (b_kv, m_b_kv, v_b_kv), (b_norm_g, m_b_norm_g, v_b_norm_g), (b_bq, m_b_bq, v_b_bq),
            (b_sinks, m_b_sinks, v_b_sinks), (final_norm_g, m_final_norm_g, v_final_norm_g)]
    shards = [(a_norm_g, m_a_norm_g, v_a_norm_g), (a_ln_g, m_a_ln_g, v_a_ln_g), (a_ln_b, m_a_ln_b, v_a_ln_b)]
    upd, loss = _small_update(full_small, me.reshape(1), [tuple(_view2d(t) for t in p) for p in reps],
                              [tuple(_view2d(t) for t in p) for p in shards], "adam_small")
    loss = loss[0, 0]
    sm_g, sd, snm, snv = ([upd[k][j].reshape(p[0].shape) for k, p in enumerate(reps + shards)] for j in range(4))

    def order(big, sm):
        a_in, a_out, kvw, b_in, b_out = big
        ws_, bs_, kvg, bkv_, bng, bq_, snk, fng, ang, alng, alnb = sm
        return (ang, a_in[None], alng, alnb, ws_, bs_, a_out[None], kvg, kvw, bkv_, bng, b_in[None], bq_, snk,
                b_out[None], fng)

    grads = order((g_a_in, g_a_out, g_kvw, g_b_in, g_b_out), sm_g)
    deltas = order((d_a_in, d_a_out, d_kvw, d_b_in, d_b_out), sd)
    new_m = order((nm_a_in, nm_a_out, nm_kvw, nm_b_in, nm_b_out), snm)
    new_v = order((nv_a_in, nv_a_out, nv_kvw, nv_b_in, nv_b_out), snv)
    return (loss, dx[None], *grads, *deltas, *new_m, *new_v)
```

```python
import functools

import jax
import jax.numpy as jnp
import numpy as np
from jax import lax
from jax.experimental import pallas as pl
from jax.experimental.pallas import tpu as pltpu

CHUNK = 128
HEAD_DIM = 64
ROPE_THETA = 10000.0
EPS = 1e-5
ADAM_LR = 0.001
ADAM_B1 = 0.9
ADAM_B2 = 0.999
ADAM_EPS = 1e-08
ADAM_WD = 0.01
ADAM_STEP = 10
N_DEV = 8
LANES = 128
NEG = -1e30

BF = jnp.bfloat16
F32 = jnp.float32
MESH = pl.DeviceIdType.MESH
AXES = ("x", "y", "c")
VMEM_LIMIT = 56 * 1024 * 1024


def _dot(a, b):
    return jnp.dot(a, b, preferred_element_type=F32)


def _dot_nt(a, b):
    return lax.dot_general(a, b, (((1,), (1,)), ((), ())), preferred_element_type=F32)


def _dot_tn(a, b):
    return lax.dot_general(a, b, (((0,), (0,)), ((), ())), preferred_element_type=F32)


def _const_spec(shape):
    nd = len(shape)
    return pl.BlockSpec(shape, lambda *_: (0,) * nd, pipeline_mode=pl.Buffered(1))


def _acc_spec(shape):
    nd = len(shape)
    return pl.BlockSpec(shape, lambda *_: (0,) * nd)


def _row_spec(tm, width):
    return pl.BlockSpec((tm, width), lambda i: (i, 0))


def _col_spec(tm, height):
    return pl.BlockSpec((height, tm), lambda i: (0, i))


def _params(sem):
    return pltpu.CompilerParams(dimension_semantics=sem, vmem_limit_bytes=VMEM_LIMIT)


def _rot(x, c, s1, s2):
    return x * c + pltpu.roll(x, 96, 1) * s1 + pltpu.roll(x, 32, 1) * s2


def _rot_bwd(d, c, s1, s2):
    return d * c + pltpu.roll(d * s1, 32, 1) + pltpu.roll(d * s2, 96, 1)


def _silu_parts(g):
    sg = jax.nn.sigmoid(g)
    return g * sg, sg * (1.0 + g * (1.0 - sg))


def _rms_bwd(dn, xh, r, g):
    a = dn * g
    return r * (a - xh * jnp.mean(a * xh, axis=-1, keepdims=True))


def _lane_lo(shape):
    return lax.broadcasted_iota(jnp.int32, shape, 1) < HEAD_DIM


def _split4(t):
    lo = _lane_lo(t.shape)
    tr = pltpu.roll(t, HEAD_DIM, 1)
    z = jnp.zeros_like(t)
    return jnp.concatenate([jnp.where(lo, t, z), jnp.where(lo, z, tr), jnp.where(lo, tr, z), jnp.where(lo, z, t)], axis=1)


def _stack_pairs(t, h):
    return jnp.concatenate([t[:, (h * 4 + j) * LANES:(h * 4 + j + 1) * LANES] for j in range(4)], axis=0)


def _upper():
    shape = (CHUNK, 4 * CHUNK)
    return lax.broadcasted_iota(jnp.int32, shape, 0) > (lax.broadcasted_iota(jnp.int32, shape, 1) & (CHUNK - 1))


def _band_rows(ref, prev, cur, h):
    a = slice(2 * h * LANES, (2 * h + 1) * LANES)
    b = slice((2 * h + 1) * LANES, (2 * h + 2) * LANES)
    return jnp.concatenate([ref[pl.ds(prev, CHUNK), a], ref[pl.ds(cur, CHUNK), a],
                            ref[pl.ds(prev, CHUNK), b], ref[pl.ds(cur, CHUNK), b]], axis=0)


def _band_cols(ref, pci, ci, h):
    a = slice(2 * h * LANES, (2 * h + 1) * LANES)
    b = slice((2 * h + 1) * LANES, (2 * h + 2) * LANES)
    return jnp.concatenate([ref[pci, a, :], ref[ci, a, :], ref[pci, b, :], ref[ci, b, :]], axis=1)


def _fold(t, upper, has_prev=None):
    out = []
    for k in range(2):
        prev = t[2 * k * CHUNK:(2 * k + 1) * CHUNK]
        if has_prev is not None:
            prev = jnp.where(has_prev, prev, NEG)
        out.append(jnp.where(upper, prev, t[(2 * k + 1) * CHUNK:(2 * k + 2) * CHUNK]))
    return out


def _unfold(fa, fb, upper):
    z = jnp.zeros_like(fa)
    return jnp.concatenate([jnp.where(upper, fa, z), jnp.where(upper, z, fa),
                            jnp.where(upper, fb, z), jnp.where(upper, z, fb)], axis=0)


def _softmax_sink(f, sink):
    m = jnp.maximum(jnp.max(f, axis=0, keepdims=True), sink)
    p = jnp.exp(f - m)
    es = jnp.exp(sink - m)
    inv = 1.0 / (jnp.sum(p, axis=0, keepdims=True) + es)
    return p * inv, es * inv


class _Riding:
    def __init__(self, shards, gathered, stages, sems, n_steps):
        self.shards, self.stages, self.n_steps = shards, stages, n_steps
        ssem, rsem, lsem = sems
        self.gathers = [_TwoLevel(stages[k], gathered[k], ssem.at[k], rsem.at[k], lsem.at[k])
                        for k in range(len(shards))]

    def begin(self, i):
        @pl.when(i == 0)
        def _():
            for shard, stage, g in zip(self.shards, self.stages, self.gathers):
                stage[...] = shard[...].astype(stage.dtype)
                g.start()

    def end(self, i):
        @pl.when(i == self.n_steps // 2)
        def _():
            for g in self.gathers:
                g.forward()

        @pl.when(i == self.n_steps - 1)
        def _():
            for g in self.gathers:
                g.finish()

    @staticmethod
    def specs(later):
        nl = len(later)
        hbm = pl.BlockSpec(memory_space=pl.ANY)
        return ([_const_spec(w.shape) for w in later], [hbm] * nl,
                tuple(jax.ShapeDtypeStruct((N_DEV,) + w.shape, BF) for w in later),
                [pltpu.VMEM(w.shape, BF) for w in later] + _direct_sems(nl))


PASS_MASKS = (0, 1, 4, 2, 5, 3, 6, 7)


def _in_proj(x, w_shard, vec_shard, slots, later):
    T, D = x.shape
    SH = w_shard.shape[1]
    VW = vec_shard.shape[1]
    TM = min(512, T)
    nT = T // TM
    nl = len(later)
    ds = D // N_DEV
    last = N_DEV - 1

    def body(slots_ref, x_ref, wsh_ref, vsh_ref, *rest):
        shards, rest = rest[:nl], rest[nl:]
        (z_ref, wout_ref, vout_ref), rest = rest[:3], rest[3:]
        gathered, rest = rest[:nl], rest[nl:]
        (w_scr, vec_scr, vstage, n1_scr, ga_scr, w_s, w_r, w_l, v_s, v_r, v_l), rest = rest[:11], rest[11:]
        stages, sems = rest[:nl], rest[nl:]
        p, i = pl.program_id(0), pl.program_id(1)
        me = _my_index()
        wg = _RelayGather(w_scr, w_s, w_r)
        vg = _Direct(vstage, vec_scr, v_s, v_r, v_l, scatter=False)
        lg = [_TwoLevel(stages[k], gathered[k], sems[0].at[k], sems[1].at[k], sems[2].at[k]) for k in range(nl)]
        w_copy = pltpu.make_async_copy(w_scr, wout_ref, w_l)

        def at_pass(k):
            return (p == k) & (i == 0)

        @pl.when(at_pass(0))
        def _():
            vstage[...] = vsh_ref[...]
            vg.start()
            w_scr[me] = wsh_ref[...].astype(BF)
            wg.start()
            vg.finish()
            for j in range(N_DEV):
                ga_scr[:, j * ds:(j + 1) * ds] = vec_scr[j, 0:1, 0:ds]
            vout_ref[...] = vec_scr[...]

        @pl.when(at_pass(4))
        def _():
            for k in range(nl):
                stages[k][...] = shards[k][...].astype(BF)
                lg[k].start()

        @pl.when(at_pass(1))
        def _():
            wg.wait_sibling()

        for k, landed in ((2, wg.on_x), (3, wg.on_y), (6, wg.on_diag)):
            @pl.when(at_pass(k))
            def _(landed=landed):
                landed()

        for k, j in ((4, 0), (5, 1), (7, 2)):
            @pl.when(at_pass(k))
            def _(j=j):
                wg.wait_passed(j)

        @pl.when(at_pass(last))
        def _():
            w_copy.start()

        @pl.when(p == 0)
        def _():
            xv = x_ref[...]
            r1 = lax.rsqrt(jnp.mean(xv * xv, axis=-1, keepdims=True) + EPS)
            n1_scr[i] = (xv * r1 * ga_scr[...]).astype(BF)

        z_ref[...] = _dot(n1_scr[i], w_scr[slots_ref[p]]).astype(BF)

        @pl.when((p == last) & (i == nT - 1))
        def _():
            wg.wait_sends()
            for g in lg:
                g.forward()
            for g in lg:
                g.finish()
            w_copy.wait()

    hbm = pl.BlockSpec(memory_space=pl.ANY)
    dma = pltpu.SemaphoreType.DMA
    S = jax.ShapeDtypeStruct
    grid_spec = pltpu.PrefetchScalarGridSpec(
        num_scalar_prefetch=1, grid=(N_DEV, nT),
        in_specs=[pl.BlockSpec((TM, D), lambda p, i, s: (jnp.where(p == 0, i, nT - 1), 0)),
                  pl.BlockSpec(w_shard.shape, lambda p, i, s: (0, 0), pipeline_mode=pl.Buffered(1)),
                  pl.BlockSpec(vec_shard.shape, lambda p, i, s: (0, 0), pipeline_mode=pl.Buffered(1))]
        + [pl.BlockSpec(w.shape, lambda p, i, s: (0, 0), pipeline_mode=pl.Buffered(1)) for w in later],
        out_specs=[pl.BlockSpec((TM, SH), lambda p, i, s: (i, s[p])), hbm,
                   pl.BlockSpec((N_DEV,) + vec_shard.shape, lambda p, i, s: (0, 0, 0))] + [hbm] * nl,
        scratch_shapes=[pltpu.VMEM((N_DEV, D, SH), BF), pltpu.VMEM((N_DEV,) + vec_shard.shape, F32),
                        pltpu.VMEM(vec_shard.shape, F32), pltpu.VMEM((nT, TM, D), BF), pltpu.VMEM((1, D), F32),
                        dma((8,)), dma((8,)), dma, dma((7,)), dma((7,)), dma]
        + [pltpu.VMEM(w.shape, BF) for w in later] + _direct_sems(nl))
    return pl.pallas_call(
        body, name="a_in_proj", grid_spec=grid_spec,
        out_shape=(S((T, N_DEV * SH), BF), S((N_DEV, D, SH), BF), S((N_DEV,) + vec_shard.shape, F32))
        + tuple(S((N_DEV,) + w.shape, BF) for w in later),
        compiler_params=_params(("arbitrary", "arbitrary")),
    )(slots, x, w_shard, vec_shard, *later)


def _a_fwd(x, z, ln_g, ln_b, ws, bs_t, wa_out, g_kv, w_kv, b_kv, rc, rs1, rs2, later):
    T, D = x.shape
    AW = wa_out.shape[0]
    G = ws.shape[0]
    TM = min(256, T)
    nT = T // TM
    nC = TM // CHUNK
    nl = len(later)

    def body(x_ref, u_ref, v_ref, gt_ref, lng_ref, lnb_ref, ws_ref, bst_ref, waout_ref, gkv_ref, wkv_ref, bkv_ref,
             rc_ref, rs1_ref, rs2_ref, *rest):
        shards, rest = rest[:nl], rest[nl:]
        (h1_ref, sv_ref, vhat_ref, rstd_ref, k4_ref, v4_ref, kt_ref, vt_ref), rest = rest[:8], rest[8:]
        gathered, sv_scr, stages, sems = rest[:nl], rest[nl], rest[nl + 1:2 * nl + 1], rest[2 * nl + 1:]
        i = pl.program_id(0)
        riding = _Riding(shards, gathered, stages, sems, nT)
        riding.begin(i)
        xv = x_ref[...]
        u = u_ref[...].astype(F32)
        v = v_ref[...].astype(F32)
        gt = gt_ref[...].astype(F32)
        mu = jnp.mean(v, axis=-1, keepdims=True)
        xc = v - mu
        rstd = lax.rsqrt(jnp.mean(xc * xc, axis=-1, keepdims=True) + EPS)
        vhat = xc * rstd
        vln = (vhat * lng_ref[...] + lnb_ref[...]).astype(BF)
        tri = lax.broadcasted_iota(jnp.int32, (CHUNK, CHUNK), 0) >= lax.broadcasted_iota(jnp.int32, (CHUNK, CHUNK), 1)
        for g in range(G):
            wsm = jnp.where(tri, ws_ref[g], 0.0).astype(BF)
            bias = bst_ref[:, g:g + 1]
            for c in range(nC):
                blk = vln[c * CHUNK:(c + 1) * CHUNK, g * CHUNK:(g + 1) * CHUNK]
                sv_scr[c * CHUNK:(c + 1) * CHUNK, g * CHUNK:(g + 1) * CHUNK] = _dot(wsm, blk) + bias
        sv = sv_scr[...]
        silu, _ = _silu_parts(gt)
        y = (u * sv * silu).astype(BF)
        h1 = xv + _dot(y, waout_ref[...])
        h1_ref[...] = h1
        sv_ref[...] = sv.astype(BF)
        vhat_ref[...] = vhat.astype(BF)
        rstd_ref[...] = jnp.broadcast_to(rstd, rstd_ref.shape)
        rkv = lax.rsqrt(jnp.mean(h1 * h1, axis=-1, keepdims=True) + EPS)
        nkv = (h1 * rkv * gkv_ref[...]).astype(BF)
        kv = _dot(nkv, wkv_ref[...]) + bkv_ref[...]
        k_rot = _rot(kv[:, :LANES], rc_ref[...], rs1_ref[...], rs2_ref[...])
        for src, ref, tref in ((k_rot, k4_ref, kt_ref), (kv[:, LANES:], v4_ref, vt_ref)):
            t4 = _split4(src)
            ref[...] = t4.astype(BF)
            for c in range(nC):
                for b in range(4):
                    blk = t4[c * CHUNK:(c + 1) * CHUNK, b * LANES:(b + 1) * LANES]
                    tref[c, b * LANES:(b + 1) * LANES, :] = blk.T.astype(BF)
        riding.end(i)

    row = functools.partial(_row_spec, TM)
    zcol = [pl.BlockSpec((TM, AW), functools.partial(lambda k, i: (i, k), k)) for k in range(3)]
    tr = pl.BlockSpec((nC, 4 * LANES, CHUNK), lambda i: (i, 0, 0))
    r_in, r_out, r_shape, r_scratch = _Riding.specs(later)
    S = jax.ShapeDtypeStruct
    return pl.pallas_call(
        body, name="a_fwd", grid=(nT,),
        in_specs=[row(D)] + zcol + [_const_spec((1, AW)), _const_spec((1, AW)),
                  _const_spec(ws.shape), _const_spec(bs_t.shape), _const_spec(wa_out.shape), _const_spec((1, D)),
                  _const_spec(w_kv.shape), _const_spec((1, 2 * LANES)), row(LANES), row(LANES), row(LANES)] + r_in,
        out_specs=[row(D), row(AW), row(AW), row(LANES), row(4 * LANES), row(4 * LANES), tr, tr] + r_out,
        out_shape=(S((T, D), F32), S((T, AW), BF), S((T, AW), BF), S((T, LANES), F32),
                   S((T, 4 * LANES), BF), S((T, 4 * LANES), BF),
                   S((T // CHUNK, 4 * LANES, CHUNK), BF), S((T // CHUNK, 4 * LANES, CHUNK), BF)) + r_shape,
        scratch_shapes=[pltpu.VMEM((TM, AW), F32)] + r_scratch,
        compiler_params=_params(("arbitrary",)),
    )(x, z, z, z, ln_g, ln_b, ws, bs_t, wa_out, g_kv, w_kv, b_kv, rc, rs1, rs2, *later)


def _b_fwd(h1, g_b, wb_in, bq, rc, rs1, rs2, k4, vt, sinks, wb_out, g_f, target):
    T, D = h1.shape
    BW = wb_out.shape[0]
    SH = wb_in.shape[2]
    TM = min(512, T)
    nC = TM // CHUNK
    nP = BW // LANES

    def body(h1_ref, gb_ref, wbin_ref, bq_ref, rc_ref, rs1_ref, rs2_ref, k4_ref, vt_ref, sink_ref, wbout_ref, gf_ref,
             tgt_ref, q_ref, g2_ref, o_ref, dh2_ref, dh2b_ref, loss_ref, dgf_ref, z_scr, o_scr):
        i = pl.program_id(0)
        h1v = h1_ref[...]
        r2 = lax.rsqrt(jnp.mean(h1v * h1v, axis=-1, keepdims=True) + EPS)
        n2 = (h1v * r2 * gb_ref[...]).astype(BF)
        for j in range(N_DEV):
            z_scr[:, j * SH:(j + 1) * SH] = _dot(n2, wbin_ref[j])
        c_t, s1_t, s2_t = rc_ref[...], rs1_ref[...], rs2_ref[...]
        for p in range(nP):
            cols = slice(p * LANES, (p + 1) * LANES)
            qp = _rot(z_scr[:, cols] + bq_ref[:, cols], c_t, s1_t, s2_t) * (HEAD_DIM ** -0.5)
            q_ref[:, cols] = qp.astype(BF)
        g2 = z_scr[:, BW:]
        g2_ref[...] = g2.astype(BF)
        upper = _upper()
        for c in range(nC):
            ci = i * nC + c
            rows = slice(c * CHUNK, (c + 1) * CHUNK)
            pci = jnp.maximum(ci - 1, 0)
            prev = pl.multiple_of(pci * CHUNK, CHUNK)
            cur = pl.multiple_of(ci * CHUNK, CHUNK)
            qc = q_ref[rows, :]
            for h in range(2):
                st = _dot_nt(_band_rows(k4_ref, prev, cur, h), _stack_pairs(qc, h))
                fa, fb = _fold(st, upper, ci > 0)
                pa, _ = _softmax_sink(fa, sink_ref[2 * h:2 * h + 1, :])
                pb, _ = _softmax_sink(fb, sink_ref[2 * h + 1:2 * h + 2, :])
                ot = _dot(_band_cols(vt_ref, pci, ci, h), _unfold(pa, pb, upper).astype(BF))
                for j in range(4):
                    o_scr[rows, (h * 4 + j) * LANES:(h * 4 + j + 1) * LANES] = ot[:, j * CHUNK:(j + 1) * CHUNK].T
        o = o_scr[...]
        o_ref[...] = o.astype(BF)
        silu, _ = _silu_parts(g2)
        h2 = h1v + _dot((o * silu).astype(BF), wbout_ref[...])
        rf = lax.rsqrt(jnp.mean(h2 * h2, axis=-1, keepdims=True) + EPS)
        xh = h2 * rf
        gf = gf_ref[...]
        err = xh * gf - tgt_ref[...]
        dyf = err * (1.0 / D)
        dh2 = _rms_bwd(dyf, xh, rf, gf)
        dh2_ref[...] = dh2
        dh2b_ref[...] = dh2.astype(BF)

        @pl.when(i == 0)
        def _():
            loss_ref[...] = jnp.zeros_like(loss_ref)
            dgf_ref[...] = jnp.zeros_like(dgf_ref)

        loss_ref[...] += 0.5 * jnp.sum(jnp.mean(err * err, axis=-1, keepdims=True), axis=0, keepdims=True)
        dgf_ref[...] += jnp.sum(dyf * xh, axis=0, keepdims=True)

    row = functools.partial(_row_spec, TM)
    S = jax.ShapeDtypeStruct
    return pl.pallas_call(
        body, name="b_fwd", grid=(T // TM,),
        in_specs=[row(D), _const_spec((1, D)), _const_spec(wb_in.shape), _const_spec((1, BW)), row(LANES), row(LANES),
                  row(LANES), _const_spec(k4.shape), _const_spec(vt.shape), _const_spec(sinks.shape),
                  _const_spec(wb_out.shape), _const_spec((1, D)), row(D)],
        out_specs=[row(BW), row(BW), row(BW), row(D), row(D), _acc_spec((1, 1)), _acc_spec((1, D))],
        out_shape=(S((T, BW), BF), S((T, BW), BF), S((T, BW), BF), S((T, D), F32), S((T, D), BF), S((1, 1), F32),
                   S((1, D), F32)),
        scratch_shapes=[pltpu.VMEM((TM, 2 * BW), F32), pltpu.VMEM((TM, BW), F32)],
        compiler_params=_params(("arbitrary",)),
    )(h1, g_b, wb_in, bq, rc, rs1, rs2, k4, vt, sinks, wb_out, g_f, target)


def _b_bwd(dh2, h1, q, g2, o, k4, v4, kt, sinks, wb_out, wb_in, g_b, rc, rs1, rs2):
    T, D = h1.shape
    BW = wb_out.shape[0]
    SH = wb_in.shape[2]
    TM = min(256, T)
    nT = T // TM
    nC = TM // CHUNK
    nP = BW // LANES

    def body(dh2_ref, h1_ref, q_ref, g2_ref, o_ref, k4_ref, v4_ref, kt_ref, sink_ref, wbout_ref, wbin_ref, gb_ref,
             rc_ref, rs1_ref, rs2_ref,
             dh1_ref, dz2_ref, n2_ref, y2_ref, dk_ref, dv_ref, dbq_ref, dgb_ref, dsink_ref, do_scr, dq_scr, dsacc_scr):
        i = pl.program_id(0)

        @pl.when(i == 0)
        def _():
            dk_ref[...] = jnp.zeros_like(dk_ref)
            dv_ref[...] = jnp.zeros_like(dv_ref)
            dbq_ref[...] = jnp.zeros_like(dbq_ref)
            dgb_ref[...] = jnp.zeros_like(dgb_ref)
            dsacc_scr[...] = jnp.zeros_like(dsacc_scr)

        dh2 = dh2_ref[...]
        dy2 = _dot_nt(dh2.astype(BF), wbout_ref[...])
        silu, dsilu = _silu_parts(g2_ref[...].astype(F32))
        do_scr[...] = (dy2 * silu).astype(BF)
        dy2, silu, dsilu = dy2.astype(BF), silu.astype(BF), dsilu.astype(BF)
        ob = o_ref[...]
        y2_ref[...] = (ob * silu).T
        dz2_ref[:, BW:] = dy2 * ob * dsilu
        upper = _upper()
        lo = _lane_lo((2 * CHUNK, LANES))
        for c in range(nC):
            ci = i * nC + c
            rows = slice(c * CHUNK, (c + 1) * CHUNK)
            pci = jnp.maximum(ci - 1, 0)
            prev = pl.multiple_of(pci * CHUNK, CHUNK)
            cur = pl.multiple_of(ci * CHUNK, CHUNK)
            qc = q_ref[rows, :]
            doc = do_scr[rows, :]
            dkb = jnp.zeros((2 * CHUNK, LANES), F32)
            dvb = jnp.zeros((2 * CHUNK, LANES), F32)
            for h in range(2):
                qs = _stack_pairs(qc, h)
                dos = _stack_pairs(doc, h)
                fa, fb = _fold(_dot_nt(_band_rows(k4_ref, prev, cur, h), qs), upper, ci > 0)
                dfa, dfb = _fold(_dot_nt(_band_rows(v4_ref, prev, cur, h), dos), upper)
                folded = []
                for k, (f, df) in enumerate(((fa, dfa), (fb, dfb))):
                    p, ps = _softmax_sink(f, sink_ref[2 * h + k:2 * h + k + 1, :])
                    delta = jnp.sum(p * df, axis=0, keepdims=True)
                    dsacc_scr[2 * h + k:2 * h + k + 1, :] -= ps * delta
                    folded.append((p, p * (df - delta)))
                pt = _unfold(folded[0][0], folded[1][0], upper).astype(BF)
                dst = _unfold(folded[0][1], folded[1][1], upper).astype(BF)
                dqt = _dot(_band_cols(kt_ref, pci, ci, h), dst)
                for j in range(4):
                    dq_scr[rows, (h * 4 + j) * LANES:(h * 4 + j + 1) * LANES] = dqt[:, j * CHUNK:(j + 1) * CHUNK].T
                for acc_name, g in (("k", _dot(dst, qs)), ("v", _dot(pt, dos))):
                    a, b = g[:2 * CHUNK], g[2 * CHUNK:]
                    if h == 0:
                        part = jnp.where(lo, a + pltpu.roll(b, HEAD_DIM, 1), 0.0)
                    else:
                        part = jnp.where(lo, 0.0, pltpu.roll(a, HEAD_DIM, 1) + b)
                    if acc_name == "k":
                        dkb += part
                    else:
                        dvb += part
            dk_ref[pl.ds(prev, CHUNK), :] += dkb[:CHUNK]
            dk_ref[pl.ds(cur, CHUNK), :] += dkb[CHUNK:]
            dv_ref[pl.ds(prev, CHUNK), :] += dvb[:CHUNK]
            dv_ref[pl.ds(cur, CHUNK), :] += dvb[CHUNK:]

        @pl.when(i == nT - 1)
        def _():
            lane = lax.broadcasted_iota(jnp.int32, dsink_ref.shape, 1)
            tot = jnp.zeros(dsink_ref.shape, F32)
            for j in range(4):
                tot += jnp.where(lane == j, jnp.sum(dsacc_scr[:, j * CHUNK:(j + 1) * CHUNK], axis=1, keepdims=True), 0.0)
            dsink_ref[...] = tot
        c_t, s1_t, s2_t = rc_ref[...], rs1_ref[...], rs2_ref[...]
        for p in range(nP):
            cols = slice(p * LANES, (p + 1) * LANES)
            dqp = _rot_bwd(dq_scr[:, cols] * (HEAD_DIM ** -0.5), c_t, s1_t, s2_t)
            dbq_ref[:, cols] += jnp.sum(dqp, axis=0, keepdims=True)
            dz2_ref[:, cols] = dqp.astype(BF)
        h1v = h1_ref[...]
        r2 = lax.rsqrt(jnp.mean(h1v * h1v, axis=-1, keepdims=True) + EPS)
        xh = h1v * r2
        gb = gb_ref[...]
        n2_ref[...] = (xh * gb).astype(BF).T
        dn2 = None
        for j in range(N_DEV):
            part = _dot_nt(dz2_ref[:, j * SH:(j + 1) * SH], wbin_ref[j])
            dn2 = part if dn2 is None else dn2 + part
        dgb_ref[...] += jnp.sum(dn2 * xh, axis=0, keepdims=True)
        dh1_ref[...] = dh2 + _rms_bwd(dn2, xh, r2, gb)

    row = functools.partial(_row_spec, TM)
    S = jax.ShapeDtypeStruct
    return pl.pallas_call(
        body, name="b_bwd", grid=(T // TM,),
        in_specs=[row(D), row(D), row(BW), row(BW), row(BW), _const_spec(k4.shape), _const_spec(v4.shape),
                  _const_spec(kt.shape), _const_spec(sinks.shape), _const_spec(wb_out.shape), _const_spec(wb_in.shape),
                  _const_spec((1, D)), row(LANES), row(LANES), row(LANES)],
        out_specs=[row(D), row(2 * BW), _col_spec(TM, D), _col_spec(TM, BW), _acc_spec((T, LANES)),
                   _acc_spec((T, LANES)), _acc_spec((1, BW)), _acc_spec((1, D)), _acc_spec((4, LANES))],
        out_shape=(S((T, D), F32), S((T, 2 * BW), BF), S((D, T), BF), S((BW, T), BF), S((T, LANES), F32),
                   S((T, LANES), F32), S((1, BW), F32), S((1, D), F32), S((4, LANES), F32)),
        scratch_shapes=[pltpu.VMEM((TM, BW), BF), pltpu.VMEM((TM, BW), F32), pltpu.VMEM((4, 4 * CHUNK), F32)],
        compiler_params=_params(("arbitrary",)),
    )(dh2, h1, q, g2, o, k4, v4, kt, sinks, wb_out, wb_in, g_b, rc, rs1, rs2)


def _a_bwd(dh1p, dk, dv, h1, g_kv, w_kv, wa_out, ws, ln_g, ln_b, z, sv, vhat, rstd, rc, rs1, rs2, ready):
    T, D = h1.shape
    AW = wa_out.shape[0]
    G = ws.shape[0]
    TM = min(256, T)
    nT = T // TM
    nC = TM // CHUNK
    nr = len(ready)

    def body(dh1p_ref, dk_ref, dv_ref, h1_ref, gkv_ref, wkv_ref, waout_ref, ws_ref, lng_ref,
             lnb_ref, u_ref, gt_ref, sv_ref, vhat_ref, rstd_ref, rc_ref, rs1_ref, rs2_ref, *rest):
        ready_refs, rest = rest[:nr], rest[nr:]
        (dz_ref, gwo_ref, gwk_ref, dh1f_ref, dgkv_ref, dbkv_ref, dlng_ref, dlnb_ref,
         dws_ref, dbs_ref), rest = rest[:10], rest[10:]
        recv_refs, (dsv_scr, dvln_scr, acco_scr, acck_scr, ssem, rsem, lsem) = rest[:nr], rest[nr:]
        i = pl.program_id(0)
        exchanges = [_Direct(ready_refs[k], recv_refs[k], ssem.at[k], rsem.at[k], lsem.at[k], scatter=True)
                     for k in range(nr)]

        @pl.when(i == 0)
        def _():
            for e in exchanges:
                e.start()
            for r in (dgkv_ref, dbkv_ref, dlng_ref, dlnb_ref, dws_ref, dbs_ref, acco_scr, acck_scr):
                r[...] = jnp.zeros_like(r)

        dk_pre = _rot_bwd(dk_ref[...], rc_ref[...], rs1_ref[...], rs2_ref[...])
        dkv = jnp.concatenate([dk_pre, dv_ref[...]], axis=1)
        dbkv_ref[...] += jnp.sum(dkv, axis=0, keepdims=True)
        dkv_b = dkv.astype(BF)
        h1v = h1_ref[...]
        rkv = lax.rsqrt(jnp.mean(h1v * h1v, axis=-1, keepdims=True) + EPS)
        xh_kv = h1v * rkv
        gkv = gkv_ref[...]
        acck_scr[...] += _dot((xh_kv * gkv).astype(BF).T, dkv_b)
        dnkv = _dot_nt(dkv_b, wkv_ref[...])
        dgkv_ref[...] += jnp.sum(dnkv * xh_kv, axis=0, keepdims=True)
        dh1 = dh1p_ref[...] + _rms_bwd(dnkv, xh_kv, rkv, gkv)
        dh1_b = dh1.astype(BF)
        dh1f_ref[...] = dh1
        dy = _dot_nt(dh1_b, waout_ref[...]).astype(BF)
        silu, dsilu = _silu_parts(gt_ref[...].astype(F32))
        silu, dsilu = silu.astype(BF), dsilu.astype(BF)
        ub, svb = u_ref[...], sv_ref[...]
        us = ub * silu
        dys = dy * svb
        acco_scr[...] += _dot((us * svb).T, dh1_b)
        dz_ref[:, :AW] = dys * silu
        dz_ref[:, 2 * AW:] = dys * ub * dsilu
        dsv_scr[...] = dy * us
        vhat_v = vhat_ref[...].astype(F32)
        lng = lng_ref[...]
        vln_b = (vhat_v * lng + lnb_ref[...]).astype(BF)
        tri = lax.broadcasted_iota(jnp.int32, (CHUNK, CHUNK), 0) >= lax.broadcasted_iota(jnp.int32, (CHUNK, CHUNK), 1)
        lane = lax.broadcasted_iota(jnp.int32, (CHUNK, LANES), 1)
        dbs = jnp.zeros((CHUNK, LANES), F32)
        for g in range(G):
            wsm = jnp.where(tri, ws_ref[g], 0.0).astype(BF)
            cols = slice(g * CHUNK, (g + 1) * CHUNK)
            dws_g = None
            for c in range(nC):
                rows = slice(c * CHUNK, (c + 1) * CHUNK)
                dsv_cg = dsv_scr[rows, cols]
                dvln_scr[rows, cols] = _dot_tn(wsm, dsv_cg)
                part = _dot_nt(dsv_cg, vln_b[rows, cols])
                dws_g = part if dws_g is None else dws_g + part
                dbs += jnp.where(lane == g, jnp.sum(dsv_cg.astype(F32), axis=-1, keepdims=True), 0.0)
            dws_ref[g] += jnp.where(tri, dws_g, 0.0)
        dbs_ref[...] += dbs
        dvln = dvln_scr[...]
        dlng_ref[...] += jnp.sum(dvln * vhat_v, axis=0, keepdims=True)
        dlnb_ref[...] += jnp.sum(dvln, axis=0, keepdims=True)
        a = dvln * lng
        dvv = rstd_ref[:, 0:1] * (a - jnp.mean(a, axis=-1, keepdims=True)
                                  - vhat_v * jnp.mean(a * vhat_v, axis=-1, keepdims=True))
        dz_ref[:, AW:2 * AW] = dvv.astype(BF)

        @pl.when(i == nT - 1)
        def _():
            for j in range(N_DEV):
                gwo_ref[j] = acco_scr[j * (AW // N_DEV):(j + 1) * (AW // N_DEV)].astype(BF)
                gwk_ref[j] = acck_scr[j * (D // N_DEV):(j + 1) * (D // N_DEV)].astype(BF)
            for e in exchanges:
                e.finish()

    row = functools.partial(_row_spec, TM)
    hbm = pl.BlockSpec(memory_space=pl.ANY)
    S = jax.ShapeDtypeStruct
    gwo_shape, gwk_shape = (N_DEV, AW // N_DEV, D), (N_DEV, D // N_DEV, 2 * LANES)
    return pl.pallas_call(
        body, name="a_bwd", grid=(nT,),
        in_specs=[row(D), row(LANES), row(LANES), row(D), _const_spec((1, D)), _const_spec(w_kv.shape),
                  _const_spec(wa_out.shape), _const_spec(ws.shape),
                  _const_spec((1, AW)), _const_spec((1, AW)), pl.BlockSpec((TM, AW), lambda i: (i, 0)),
                  pl.BlockSpec((TM, AW), lambda i: (i, 2)), row(AW), row(AW), row(LANES),
                  row(LANES), row(LANES), row(LANES)] + [hbm] * nr,
        out_specs=[row(3 * AW), _const_spec(gwo_shape), _const_spec(gwk_shape), row(D),
                   _acc_spec((1, D)), _acc_spec((1, 2 * LANES)), _acc_spec((1, AW)),
                   _acc_spec((1, AW)), _acc_spec(ws.shape), _acc_spec((CHUNK, LANES))] + [hbm] * nr,
        out_shape=(S((T, 3 * AW), BF), S(gwo_shape, BF), S(gwk_shape, BF), S((T, D), F32),
                   S((1, D), F32), S((1, 2 * LANES), F32), S((1, AW), F32), S((1, AW), F32),
                   S(ws.shape, F32), S((CHUNK, LANES), F32)) + tuple(S(r.shape, r.dtype) for r in ready),
        scratch_shapes=[pltpu.VMEM((TM, AW), BF), pltpu.VMEM((TM, AW), F32), pltpu.VMEM((AW, D), F32),
                        pltpu.VMEM((D, 2 * LANES), F32)] + _direct_sems(nr),
        compiler_params=_params(("arbitrary",)),
    )(dh1p, dk, dv, h1, g_kv, w_kv, wa_out, ws, ln_g, ln_b, z, z, sv, vhat, rstd, rc, rs1, rs2, *ready)


def _a_in_bwd(dz, wa_in, x, dh1, g_a, ready):
    T, D = x.shape
    SH = wa_in.shape[2]
    TM = min(512, T)
    nT = T // TM
    nr = len(ready)

    def body(dz_ref, wain_ref, x_ref, dh1_ref, ga_ref, *rest):
        ready_refs, (dx_ref, n1_ref, dga_ref), rest = rest[:nr], rest[nr:nr + 3], rest[nr + 3:]
        recv_refs, (ssem, rsem, lsem) = rest[:nr], rest[nr:]
        i = pl.program_id(0)
        exchanges = [_Direct(ready_refs[k], recv_refs[k], ssem.at[k], rsem.at[k], lsem.at[k], scatter=True)
                     for k in range(nr)]

        @pl.when(i == 0)
        def _():
            for e in exchanges:
                e.start()
            dga_ref[...] = jnp.zeros_like(dga_ref)

        xv = x_ref[...]
        r1 = lax.rsqrt(jnp.mean(xv * xv, axis=-1, keepdims=True) + EPS)
        xh = xv * r1
        ga = ga_ref[...]
        n1_ref[...] = (xh * ga).astype(BF).T
        dn1 = None
        for j in range(N_DEV):
            part = _dot_nt(dz_ref[:, j * SH:(j + 1) * SH], wain_ref[j])
            dn1 = part if dn1 is None else dn1 + part
        dga_ref[...] += jnp.sum(dn1 * xh, axis=0, keepdims=True)
        dx_ref[...] = dh1_ref[...] + _rms_bwd(dn1, xh, r1, ga)

        @pl.when(i == nT - 1)
        def _():
            for e in exchanges:
                e.finish()

    row = functools.partial(_row_spec, TM)
    hbm = pl.BlockSpec(memory_space=pl.ANY)
    S = jax.ShapeDtypeStruct
    return pl.pallas_call(
        body, name="a_in_bwd", grid=(nT,),
        in_specs=[row(dz.shape[1]), _const_spec(wa_in.shape), row(D), row(D), _const_spec((1, D))] + [hbm] * nr,
        out_specs=[row(D), _col_spec(TM, D), _acc_spec((1, D))] + [hbm] * nr,
        out_shape=(S((T, D), F32), S((D, T), BF), S((1, D), F32)) + tuple(S(r.shape, r.dtype) for r in ready),
        scratch_shapes=_direct_sems(nr),
        compiler_params=_params(("arbitrary",)),
    )(dz, wa_in, x, dh1, g_a, *ready)


def _wgrad(at, b, nblk, name, bt=512):
    K, T = at.shape
    N = b.shape[1] // nblk
    BT = min(bt, T)
    nt = T // BT

    def body(a_ref, b_ref, o_ref, acc):
        t = pl.program_id(0)

        @pl.when(t == 0)
        def _():
            acc[...] = jnp.zeros_like(acc)

        acc[...] += _dot(a_ref[...], b_ref[...])

        @pl.when(t == nt - 1)
        def _():
            for j in range(nblk):
                o_ref[j] = acc[:, j * N:(j + 1) * N].astype(BF)

    return pl.pallas_call(
        body, name=name, grid=(nt,),
        in_specs=[pl.BlockSpec((K, BT), lambda t: (0, t)), pl.BlockSpec((BT, nblk * N), lambda t: (t, 0))],
        out_specs=pl.BlockSpec((nblk, K, N), lambda t: (0, 0, 0)),
        out_shape=jax.ShapeDtypeStruct((nblk, K, N), BF),
        scratch_shapes=[pltpu.VMEM((K, nblk * N), F32)],
        compiler_params=_params(("arbitrary",)),
    )(at, b)


def _wgrad_exchange(a, b, me, small, name):
    K, T = a.shape
    N = b.shape[1] // N_DEV
    BT = min(1024, T)
    nt = T // BT
    last = N_DEV - 1
    n_chip = N_DEV // 2

    def body(me_ref, a_ref, b_ref, small_ref, recv_ref, full_ref, *scratch):
        (acc, dstage, istage, half, d_s, d_r, i_s, i_r, lsem, parts_scr, red_scr, e_s, e_r, e_l, g_s, g_r,
         g_l) = scratch
        s, t = pl.program_id(0), pl.program_id(1)
        x, y, c = (lax.axis_index(ax) for ax in AXES)
        ex = [_Direct(small_ref, parts_scr, e_s, e_r, e_l, scatter=True)]
        regather = _TwoLevel(red_scr, full_ref, g_s, g_r, g_l)

        def to_sibling(k, slot):
            return pltpu.make_async_remote_copy(src_ref=dstage.at[slot], dst_ref=half.at[k], send_sem=d_s.at[k],
                                                recv_sem=d_r.at[k], device_id=(x, y, 1 - c), device_id_type=MESH)

        def to_chip(k, slot, sender):
            far = n_chip - 1 - k
            px, py = x ^ ((far >> 1) & 1), y ^ (far & 1)
            dst = recv_ref.at[2 * x + y] if sender else recv_ref.at[2 * px + py]
            return pltpu.make_async_remote_copy(src_ref=istage.at[slot], dst_ref=dst, send_sem=i_s.at[k],
                                                recv_sem=i_r.at[k], device_id=(px, py, c), device_id_type=MESH)

        @pl.when((s == 0) & (t == 0))
        def _():
            for e in ex:
                e.start()

        @pl.when(t == 0)
        def _():
            acc[...] = jnp.zeros_like(acc)

        acc[...] += _dot(a_ref[...], b_ref[...])

        @pl.when(t == nt - 1)
        def _():
            k = lax.div(s, 2)
            slot = lax.rem(k, 2)

            @pl.when(lax.rem(s, 2) == 0)
            def _():
                @pl.when(k >= 2)
                def _():
                    to_sibling(k - 2, slot).wait_send()

                dstage[slot] = acc[...].astype(BF)
                to_sibling(k, slot).start()

            @pl.when(lax.rem(s, 2) == 1)
            def _():
                to_sibling(k, slot).wait_recv()

                @pl.when(k >= 2)
                def _():
                    to_chip(k - 2, slot, True).wait_send()

                istage[slot] = (acc[...] + half[k].astype(F32)).astype(BF)

                @pl.when(k < n_chip - 1)
                def _():
                    to_chip(k, slot, True).start()

            @pl.when(s == last)
            def _():
                own = pltpu.make_async_copy(istage.at[slot], recv_ref.at[2 * x + y], lsem)
                own.start()
                to_chip(n_chip - 2, 0, True).wait_send()
                to_sibling(n_chip - 2, 0).wait_send()
                to_sibling(n_chip - 1, 1).wait_send()
                for kk in range(n_chip - 1):
                    to_chip(kk, 0, False).wait_recv()
                own.wait()
                for e in ex:
                    e.finish()
                total = parts_scr[0]
                for dev in range(1, N_DEV):
                    total = total + parts_scr[dev]
                red_scr[...] = total
                regather.start()
                regather.forward()
                regather.finish()

    hbm = pl.BlockSpec(memory_space=pl.ANY)
    dma = pltpu.SemaphoreType.DMA
    grid_spec = pltpu.PrefetchScalarGridSpec(
        num_scalar_prefetch=1, grid=(N_DEV, nt),
        in_specs=[pl.BlockSpec((K, BT), lambda s, t, me_ref: (0, t)),
                  pl.BlockSpec((BT, N), lambda s, t, me_ref: (t, me_ref[0] ^ (last - s))), hbm],
        out_specs=[hbm, hbm],
        scratch_shapes=[pltpu.VMEM((K, N), F32), pltpu.VMEM((2, K, N), BF), pltpu.VMEM((2, K, N), BF),
                        pltpu.VMEM((n_chip, K, N), BF), dma((n_chip,)), dma((n_chip,)), dma((n_chip - 1,)),
                        dma((n_chip - 1,)), dma, pltpu.VMEM(small.shape, F32), pltpu.VMEM(small.shape[1:], F32),
                        dma((last,)), dma((last,)), dma, dma((last,)), dma((last,)), dma])
    return pl.pallas_call(
        body, name=name, grid_spec=grid_spec,
        out_shape=[jax.ShapeDtypeStruct((n_chip, K, N), BF), jax.ShapeDtypeStruct(small.shape, F32)],
        compiler_params=_params(("arbitrary", "arbitrary")),
    )(me, a, b, small)


def _my_index():
    return 4 * lax.axis_index("x") + 2 * lax.axis_index("y") + lax.axis_index("c")


def _peer(mask):
    x, y, c = (lax.axis_index(a) for a in AXES)
    return (x ^ ((mask >> 2) & 1), y ^ ((mask >> 1) & 1), c ^ (mask & 1))


def _dev_index(p):
    return 4 * p[0] + 2 * p[1] + p[2]


class _Direct:
    def __init__(self, src, dst, send_sems, recv_sems, local_sem, scatter):
        me = _my_index()
        self.own = pltpu.make_async_copy(src.at[me] if scatter else src, dst.at[me], local_sem)
        self.sends, self.recvs = [], []
        for k in range(1, N_DEV):
            p = _peer(k)
            pi = _dev_index(p)
            sems = dict(send_sem=send_sems.at[k - 1], recv_sem=recv_sems.at[k - 1], device_id=p, device_id_type=MESH)
            self.sends.append(pltpu.make_async_remote_copy(src_ref=src.at[pi] if scatter else src, dst_ref=dst.at[me],
                                                           **sems))
            self.recvs.append(pltpu.make_async_remote_copy(src_ref=src.at[me] if scatter else src, dst_ref=dst.at[pi],
                                                           **sems))

    def start(self):
        self.own.start()
        for cp in self.sends:
            cp.start()

    def finish(self):
        for cp in self.sends:
            cp.wait_send()
        for cp in self.recvs:
            cp.wait_recv()
        self.own.wait()


class _TwoLevel:
    def __init__(self, src, dst, send_sems, recv_sems, local_sem, own=True):
        x, y, c = (lax.axis_index(a) for a in AXES)
        self.me, self.sibling = (x, y, c), (x, y, 1 - c)
        self.chips = [(1 - x, y), (x, 1 - y), (1 - x, 1 - y)]
        self.src, self.dst, self.send_sems, self.recv_sems = src, dst, send_sems, recv_sems
        self.own = pltpu.make_async_copy(src, dst.at[_dev_index(self.me)], local_sem) if own else None

    def _copy(self, k, block, to, from_src=False):
        slot = self.dst.at[_dev_index(block)]
        return pltpu.make_async_remote_copy(src_ref=self.src if from_src else slot, dst_ref=slot,
                                            send_sem=self.send_sems.at[k], recv_sem=self.recv_sems.at[k],
                                            device_id=to, device_id_type=MESH)

    def _firsts(self):
        c = self.me[2]
        return [self._copy(0, self.me, self.sibling, True)] + [self._copy(1 + j, self.me, (*chip, c), True)
                                                               for j, chip in enumerate(self.chips)]

    def _passed(self):
        c = self.me[2]
        return [self._copy(4 + j, (*chip, c), self.sibling) for j, chip in enumerate(self.chips)]

    def start(self):
        if self.own is not None:
            self.own.start()
        for cp in self._firsts():
            cp.start()

    def wait_sibling(self):
        self._copy(0, self.sibling, self.me).wait_recv()

    def wait_chip_and_forward(self, j):
        self._copy(1 + j, (*self.chips[j], self.me[2]), self.me).wait_recv()
        self._passed()[j].start()

    def wait_passed(self, j):
        self._copy(4 + j, (*self.chips[j], 1 - self.me[2]), self.me).wait_recv()

    def wait_sends(self):
        for cp in self._firsts() + self._passed():
            cp.wait_send()
        if self.own is not None:
            self.own.wait()

    def forward(self):
        for j in range(3):
            self.wait_chip_and_forward(j)

    def finish(self):
        self.wait_sibling()
        for j in range(3):
            self.wait_passed(j)
        self.wait_sends()


class _RelayGather:
    def __init__(self, dst, send_sems, recv_sems):
        x, y, c = (lax.axis_index(a) for a in AXES)
        self.c = c
        self.sib, self.xn, self.yn, self.dg = (x, y, 1 - c), (1 - x, y, c), (x, 1 - y, c), (1 - x, 1 - y, c)
        self.me = (x, y, c)
        self.dst, self.send_sems, self.recv_sems = dst, send_sems, recv_sems
        self.half = dst.shape[1] // 2

    def _slot(self, dev, part=None):
        i = _dev_index(dev)
        if part is None:
            return self.dst.at[i]
        return self.dst.at[i, pl.ds(part * self.half, self.half)]

    def _copy(self, k, dev, to, part=None):
        ref = self._slot(dev, part)
        return pltpu.make_async_remote_copy(src_ref=ref, dst_ref=ref, send_sem=self.send_sems.at[k],
                                            recv_sem=self.recv_sems.at[k], device_id=to, device_id_type=MESH)

    def _other(self, dev):
        return (dev[0], dev[1], 1 - self.c)

    def start(self):
        for k, to in enumerate((self.sib, self.xn, self.yn)):
            self._copy(k, self.me, to).start()

    def wait_sibling(self):
        self._copy(0, self.sib, self.me).wait_recv()

    def on_x(self):
        self._copy(1, self.xn, self.me).wait_recv()
        self._copy(3, self.xn, self.yn, part=0).start()
        self._copy(5, self.xn, self.sib).start()

    def on_y(self):
        self._copy(2, self.yn, self.me).wait_recv()
        self._copy(4, self.yn, self.xn, part=1).start()
        self._copy(6, self.yn, self.sib).start()

    def on_diag(self):
        self._copy(3, self.dg, self.me, part=0).wait_recv()
        self._copy(4, self.dg, self.me, part=1).wait_recv()
        self._copy(7, self.dg, self.sib).start()

    def wait_passed(self, j):
        self._copy(5 + j, self._other((self.xn, self.yn, self.dg)[j]), self.me).wait_recv()

    def wait_sends(self):
        for k, to in enumerate((self.sib, self.xn, self.yn)):
            self._copy(k, self.me, to).wait_send()
        self._copy(3, self.xn, self.yn, part=0).wait_send()
        self._copy(4, self.yn, self.xn, part=1).wait_send()
        for j, dev in enumerate((self.xn, self.yn, self.dg)):
            self._copy(5 + j, dev, self.sib).wait_send()


def _direct_sems(n):
    if n == 0:
        return []
    return [pltpu.SemaphoreType.DMA((n, 7)), pltpu.SemaphoreType.DMA((n, 7)), pltpu.SemaphoreType.DMA((n,))]


def _adam_math(w, g, m, v):
    m = ADAM_B1 * m + (1.0 - ADAM_B1) * g
    v = ADAM_B2 * v + (1.0 - ADAM_B2) * (g * g)
    m_hat = m / (1.0 - ADAM_B1 ** ADAM_STEP)
    v_hat = v / (1.0 - ADAM_B2 ** ADAM_STEP)
    delta = -ADAM_LR * (m_hat / (jnp.sqrt(v_hat) + ADAM_EPS) + ADAM_WD * w)
    return delta, m, v


def _sum_adam(parts, w, m, v, name):
    R, C = w.shape
    NP = parts.shape[0]
    BR = CHUNK if R % CHUNK == 0 else R

    def body(p_ref, w_ref, m_ref, v_ref, g_ref, d_ref, nm_ref, nv_ref):
        g = p_ref[0].astype(F32)
        for i in range(1, NP):
            g = g + p_ref[i].astype(F32)
        g_ref[...] = g
        d_ref[...], nm_ref[...], nv_ref[...] = _adam_math(w_ref[...], g, m_ref[...], v_ref[...])

    blk = pl.BlockSpec((BR, C), lambda i: (i, 0))
    S = jax.ShapeDtypeStruct((R, C), F32)
    return pl.pallas_call(
        body, name=name, grid=(R // BR,),
        in_specs=[pl.BlockSpec((NP, BR, C), lambda i: (0, i, 0)), blk, blk, blk],
        out_specs=[blk] * 4, out_shape=(S,) * 4,
        compiler_params=_params(("arbitrary",)),
    )(parts, w, m, v)


SUBLANES = 8


def _nrows(size):
    return -(-size // (SUBLANES * LANES)) * SUBLANES


def _view2d(a):
    return a.reshape(-1, LANES) if a.size % LANES == 0 else a.reshape(1, -1)


def _pack_small(parts, total_rows, name):
    arrs = [p[0] for p in parts]

    def body(*refs):
        out = refs[-1]
        out[...] = jnp.zeros_like(out)
        at = 0
        for ref, (a, rows, flag) in zip(refs[:-1], parts):
            val = ref[...].T if flag == "T" else ref[...]
            r, c = (rows, val.shape[1]) if flag == "T" else val.shape
            out[at:at + r, 0:c] = val[:r]
            at += _nrows(r * c)

    return pl.pallas_call(body, name=name, out_shape=jax.ShapeDtypeStruct((total_rows, LANES), F32))(*arrs)


def _small_update(full, me, reps, shards, name):
    n = len(reps) + len(shards)

    def body(me_ref, full_ref, *refs):
        ins, outs = refs[:3 * n], refs[3 * n:]
        at = 0
        for k in range(n):
            w_ref, m_ref, v_ref = ins[3 * k:3 * k + 3]
            r, c = w_ref.shape
            if k < len(reps):
                g = full_ref[at:at + r, 0:c]
                at += _nrows(r * c)
            else:
                seg = full_ref[at:at + N_DEV * r, :]
                row = lax.broadcasted_iota(jnp.int32, seg.shape, 0)
                pick = [jnp.sum(jnp.where(row == r * me_ref[0] + t, seg, 0.0), axis=0, keepdims=True) for t in range(r)]
                g = pick[0] if r == 1 else jnp.concatenate(pick, axis=0)
                at += N_DEV * r
            g_ref, d_ref, nm_ref, nv_ref = outs[4 * k:4 * k + 4]
            g_ref[...] = g
            d_ref[...], nm_ref[...], nv_ref[...] = _adam_math(w_ref[...], g, m_ref[...], v_ref[...])
        outs[4 * n][...] = full_ref[at:at + 1, 0:1]

    flat = [t for p in reps + shards for t in p]
    S = jax.ShapeDtypeStruct
    res = pl.pallas_call(
        body, name=name,
        in_specs=[pl.BlockSpec(memory_space=pltpu.SMEM)] + [pl.BlockSpec(memory_space=pltpu.VMEM)] * (1 + len(flat)),
        out_shape=[S(p[0].shape, F32) for p in reps + shards for _ in range(4)] + [S((1, 1), F32)],
    )(me, full, *flat)
    return [tuple(res[4 * k:4 * k + 4]) for k in range(n)], res[4 * n]


def _rope_tables(T):
    pos = np.arange(T, dtype=np.float32)
    inv_freq = (np.float64(ROPE_THETA) ** (-np.arange(0, HEAD_DIM, 2, dtype=np.float64) / HEAD_DIM)).astype(np.float32)
    ang = (pos[:, None] * inv_freq[None, :]).astype(np.float64)
    cos, sin, zero = np.cos(ang).astype(np.float32), np.sin(ang).astype(np.float32), np.zeros(ang.shape, np.float32)
    c = np.concatenate([cos, cos, cos, cos], axis=1)
    s1 = np.concatenate([-sin, zero, -sin, zero], axis=1)
    s2 = np.concatenate([zero, sin, zero, sin], axis=1)
    return jnp.asarray(c), jnp.asarray(s1), jnp.asarray(s2)


def kernel(x, a_norm_g, a_w_in, a_ln_g, a_ln_b, a_ws, a_bs, a_w_out, kv_norm_g, w_kv, b_kv, b_norm_g, b_w_in, b_bq, b_sinks, b_w_out, final_norm_g, loss_target, m_a_norm_g, m_a_w_in, m_a_ln_g, m_a_ln_b, m_a_ws, m_a_bs, m_a_w_out, m_kv_norm_g, m_w_kv, m_b_kv, m_b_norm_g, m_b_w_in, m_b_bq, m_b_sinks, m_b_w_out, m_final_norm_g, v_a_norm_g, v_a_w_in, v_a_ln_g, v_a_ln_b, v_a_ws, v_a_bs, v_a_w_out, v_kv_norm_g, v_w_kv, v_b_kv, v_b_norm_g, v_b_w_in, v_b_bq, v_b_sinks, v_b_w_out, v_final_norm_g):
    T, D = x.shape[1], x.shape[2]
    AW = a_ln_g.shape[1] * N_DEV
    G = a_ws.shape[1]
    assert w_kv.shape[1] == 2 * LANES and a_ws.shape[2] == CHUNK and T % CHUNK == 0
    me = _my_index()

    xs, tgt = x[0], loss_target[0]
    vec = jnp.concatenate([a_norm_g, a_ln_g, a_ln_b], axis=1)
    vec = jnp.broadcast_to(vec, (SUBLANES, vec.shape[1]))
    slots = me ^ jnp.array(PASS_MASKS, jnp.int32)
    z, wa_in, vecs, wa_out, wkv = _in_proj(xs, a_w_in[0], vec, slots, [a_w_out[0], w_kv])
    wa_out = wa_out.reshape(AW, D)
    wkv = wkv.reshape(D, 2 * LANES)
    vecs = vecs[:, 0, :]
    ds = D // N_DEV
    g_a = vecs[:, :ds].reshape(1, D)
    ln_g = vecs[:, ds:ds + AW // N_DEV].reshape(1, AW)
    ln_b = vecs[:, ds + AW // N_DEV:].reshape(1, AW)

    rc, rs1, rs2 = _rope_tables(T)
    ws = a_ws[0]
    bs_t = a_bs[0].T
    g_kv = kv_norm_g.reshape(1, D)
    bkv = b_kv.reshape(1, -1)
    g_f = final_norm_g.reshape(1, D)
    sinks = jnp.repeat(b_sinks.reshape(2, 4, 2).transpose(0, 2, 1).reshape(4, 4), CHUNK, axis=1)
    h1, sv, vhat, rstd, k4, v4, kt, vt, wb_in, wb_out = _a_fwd(
        xs, z, ln_g, ln_b, ws, bs_t, wa_out, g_kv, wkv, bkv, rc, rs1, rs2, [b_w_in[0], b_w_out[0]])
    wb_out = wb_out.reshape(-1, D)
    q, g2, o, dh2, dh2_b, loss, d_gf = _b_fwd(h1, b_norm_g, wb_in, b_bq, rc, rs1, rs2, k4, vt, sinks, wb_out, g_f, tgt)
    dh1p, dz2, n2, y2, dk, dv, d_bq, d_gb, d_sink = _b_bwd(dh2, h1, q, g2, o, k4, v4, kt, sinks, wb_out, wb_in,
                                                           b_norm_g, rc, rs1, rs2)
    d_sink = d_sink[:, :4].reshape(2, 2, 4).transpose(0, 2, 1).reshape(1, 16)
    gw_b_in = _wgrad(n2, dz2, N_DEV, "wgrad_b_in", bt=1024)
    gw_b_out = _wgrad(y2, dh2_b, 1, "wgrad_b_out", bt=1024).reshape(N_DEV, -1, D)
    (dz, gw_a_out, gw_kv, dh1_f, d_gkv, d_bkv, d_lng, d_lnb, d_ws, d_bst, r_b_in, r_b_out) = _a_bwd(
        dh1p, dk, dv, h1, g_kv, wkv, wa_out, ws, ln_g, ln_b, z, sv, vhat, rstd, rc, rs1, rs2, [gw_b_in, gw_b_out])
    dx, n1, d_ga, r_a_out, r_kv = _a_in_bwd(dz, wa_in, xs, dh1_f, g_a, [gw_a_out, gw_kv])
    small = [(_view2d(d_ws), None, None), (d_bst, G, "T")] + [(_view2d(a), None, None) for a in (
        d_gkv, d_bkv, d_gb, d_bq, d_sink, d_gf, d_ga, d_lng, d_lnb, loss)]
    used = sum(_nrows(G * CHUNK if flag else a.size) for a, _, flag in small)
    per = -(-used // (SUBLANES * N_DEV)) * SUBLANES
    small_pack = _pack_small(small, per * N_DEV, "pack_small").reshape(N_DEV, per, LANES)
    r_a_in, full_small = _wgrad_exchange(n1, dz, me.reshape(1), small_pack, "wgrad_a_in")

    g_a_in, d_a_in, nm_a_in, nv_a_in = _sum_adam(r_a_in, a_w_in[0], m_a_w_in[0], v_a_w_in[0], "adam_a_in")
    g_a_out, d_a_out, nm_a_out, nv_a_out = _sum_adam(r_a_out, a_w_out[0], m_a_w_out[0], v_a_w_out[0], "adam_a_out")
    g_kvw, d_kvw, nm_kvw, nv_kvw = _sum_adam(r_kv, w_kv, m_w_kv, v_w_kv, "adam_kv")
    g_b_in, d_b_in, nm_b_in, nv_b_in = _sum_adam(r_b_in, b_w_in[0], m_b_w_in[0], v_b_w_in[0], "adam_b_in")
    g_b_out, d_b_out, nm_b_out, nv_b_out = _sum_adam(r_b_out, b_w_out[0], m_b_w_out[0], v_b_w_out[0], "adam_b_out")

    full_small = full_small.reshape(N_DEV * per, LANES)
    reps = [(a_ws, m_a_ws, v_a_ws), (a_bs, m_a_bs, v_a_bs), (kv_norm_g, m_kv_norm_g, v_kv_norm_g),
            (b_kv, m_b_kv, v_b_kv), (b_norm_g, m_b_norm_g, v_b_norm_g), (b_bq, m_b_bq, v_b_bq),
            (b_sinks, m_b_sinks, v_b_sinks), (final_norm_g, m_final_norm_g, v_final_norm_g)]
    shards = [(a_norm_g, m_a_norm_g, v_a_norm_g), (a_ln_g, m_a_ln_g, v_a_ln_g), (a_ln_b, m_a_ln_b, v_a_ln_b)]
    upd, loss = _small_update(full_small, me.reshape(1), [tuple(_view2d(t) for t in p) for p in reps],
                              [tuple(_view2d(t) for t in p) for p in shards], "adam_small")
    loss = loss[0, 0]
    sm_g, sd, snm, snv = ([upd[k][j].reshape(p[0].shape) for k, p in enumerate(reps + shards)] for j in range(4))

    def order(big, sm):
        a_in, a_out, kvw, b_in, b_out = big
        ws_, bs_, kvg, bkv_, bng, bq_, snk, fng, ang, alng, alnb = sm
        return (ang, a_in[None], alng, alnb, ws_, bs_, a_out[None], kvg, kvw, bkv_, bng, b_in[None], bq_, snk,
                b_out[None], fng)

    grads = order((g_a_in, g_a_out, g_kvw, g_b_in, g_b_out), sm_g)
    deltas = order((d_a_in, d_a_out, d_kvw, d_b_in, d_b_out), sd)
    new_m = order((nm_a_in, nm_a_out, nm_kvw, nm_b_in, nm_b_out), snm)
    new_v = order((nv_a_in, nv_a_out, nv_kvw, nv_b_in, nv_b_out), snv)
    return (loss, dx[None], *grads, *deltas, *new_m, *new_v)
```

```python
import functools

import jax
import jax.numpy as jnp
import numpy as np
from jax import lax
from jax.experimental import pallas as pl
from jax.experimental.pallas import tpu as pltpu

CHUNK = 128
HEAD_DIM = 64
ROPE_THETA = 10000.0
EPS = 1e-5
ADAM_LR = 0.001
ADAM_B1 = 0.9
ADAM_B2 = 0.999
ADAM_EPS = 1e-08
ADAM_WD = 0.01
ADAM_STEP = 10
N_DEV = 8
LANES = 128
NEG = -1e30

BF = jnp.bfloat16
F32 = jnp.float32
MESH = pl.DeviceIdType.MESH
AXES = ("x", "y", "c")
VMEM_LIMIT = 56 * 1024 * 1024


def _dot(a, b):
    return jnp.dot(a, b, preferred_element_type=F32)


def _dot_nt(a, b):
    return lax.dot_general(a, b, (((1,), (1,)), ((), ())), preferred_element_type=F32)


def _dot_tn(a, b):
    return lax.dot_general(a, b, (((0,), (0,)), ((), ())), preferred_element_type=F32)


def _const_spec(shape):
    nd = len(shape)
    return pl.BlockSpec(shape, lambda *_: (0,) * nd, pipeline_mode=pl.Buffered(1))


def _acc_spec(shape):
    nd = len(shape)
    return pl.BlockSpec(shape, lambda *_: (0,) * nd)


def _row_spec(tm, width):
    return pl.BlockSpec((tm, width), lambda i: (i, 0))


def _col_spec(tm, height):
    return pl.BlockSpec((height, tm), lambda i: (0, i))


def _params(sem):
    return pltpu.CompilerParams(dimension_semantics=sem, vmem_limit_bytes=VMEM_LIMIT)


def _rot(x, c, s1, s2):
    return x * c + pltpu.roll(x, 96, 1) * s1 + pltpu.roll(x, 32, 1) * s2


def _rot_bwd(d, c, s1, s2):
    return d * c + pltpu.roll(d * s1, 32, 1) + pltpu.roll(d * s2, 96, 1)


def _silu_parts(g):
    sg = jax.nn.sigmoid(g)
    return g * sg, sg * (1.0 + g * (1.0 - sg))


def _rms_bwd(dn, xh, r, g):
    a = dn * g
    return r * (a - xh * jnp.mean(a * xh, axis=-1, keepdims=True))


def _lane_lo(shape):
    return lax.broadcasted_iota(jnp.int32, shape, 1) < HEAD_DIM


def _split4(t):
    lo = _lane_lo(t.shape)
    tr = pltpu.roll(t, HEAD_DIM, 1)
    z = jnp.zeros_like(t)
    return jnp.concatenate([jnp.where(lo, t, z), jnp.where(lo, z, tr), jnp.where(lo, tr, z), jnp.where(lo, z, t)], axis=1)


def _stack_pairs(t, h):
    return jnp.concatenate([t[:, (h * 4 + j) * LANES:(h * 4 + j + 1) * LANES] for j in range(4)], axis=0)


def _upper():
    shape = (CHUNK, 4 * CHUNK)
    return lax.broadcasted_iota(jnp.int32, shape, 0) > (lax.broadcasted_iota(jnp.int32, shape, 1) & (CHUNK - 1))


def _band_rows(ref, prev, cur, h):
    a = slice(2 * h * LANES, (2 * h + 1) * LANES)
    b = slice((2 * h + 1) * LANES, (2 * h + 2) * LANES)
    return jnp.concatenate([ref[pl.ds(prev, CHUNK), a], ref[pl.ds(cur, CHUNK), a],
                            ref[pl.ds(prev, CHUNK), b], ref[pl.ds(cur, CHUNK), b]], axis=0)


def _band_cols(ref, pci, ci, h):
    a = slice(2 * h * LANES, (2 * h + 1) * LANES)
    b = slice((2 * h + 1) * LANES, (2 * h + 2) * LANES)
    return jnp.concatenate([ref[pci, a, :], ref[ci, a, :], ref[pci, b, :], ref[ci, b, :]], axis=1)


def _fold(t, upper, has_prev=None):
    out = []
    for k in range(2):
        prev = t[2 * k * CHUNK:(2 * k + 1) * CHUNK]
        if has_prev is not None:
            prev = jnp.where(has_prev, prev, NEG)
        out.append(jnp.where(upper, prev, t[(2 * k + 1) * CHUNK:(2 * k + 2) * CHUNK]))
    return out


def _unfold(fa, fb, upper):
    z = jnp.zeros_like(fa)
    return jnp.concatenate([jnp.where(upper, fa, z), jnp.where(upper, z, fa),
                            jnp.where(upper, fb, z), jnp.where(upper, z, fb)], axis=0)


def _softmax_sink(f, sink):
    m = jnp.maximum(jnp.max(f, axis=0, keepdims=True), sink)
    p = jnp.exp(f - m)
    es = jnp.exp(sink - m)
    inv = 1.0 / (jnp.sum(p, axis=0, keepdims=True) + es)
    return p * inv, es * inv


class _Riding:
    def __init__(self, shards, gathered, stages, sems, n_steps):
        self.shards, self.stages, self.n_steps = shards, stages, n_steps
        ssem, rsem, lsem = sems
        self.gathers = [_TwoLevel(stages[k], gathered[k], ssem.at[k], rsem.at[k], lsem.at[k])
                        for k in range(len(shards))]

    def begin(self, i):
        @pl.when(i == 0)
        def _():
            for shard, stage, g in zip(self.shards, self.stages, self.gathers):
                stage[...] = shard[...].astype(stage.dtype)
                g.start()

    def end(self, i):
        @pl.when(i == self.n_steps // 2)
        def _():
            for g in self.gathers:
                g.forward()

        @pl.when(i == self.n_steps - 1)
        def _():
            for g in self.gathers:
                g.finish()

    @staticmethod
    def specs(later):
        nl = len(later)
        hbm = pl.BlockSpec(memory_space=pl.ANY)
        return ([_const_spec(w.shape) for w in later], [hbm] * nl,
                tuple(jax.ShapeDtypeStruct((N_DEV,) + w.shape, BF) for w in later),
                [pltpu.VMEM(w.shape, BF) for w in later] + _direct_sems(nl))


PASS_MASKS = ((0, 1, 2, 5, 4, 3, 6, 7), (0, 1, 4, 3, 2, 5, 6, 7))


def _in_proj(x, w_shard, vec_shard, slots, later):
    T, D = x.shape
    SH = w_shard.shape[1]
    TM = min(1024, T)
    nT = T // TM
    nl = len(later)
    ds = D // N_DEV
    last = N_DEV - 1

    def body(slots_ref, x_ref, wsh_ref, vsh_ref, *rest):
        shards, rest = rest[:nl], rest[nl:]
        (z_ref, wout_ref, vout_ref), rest = rest[:3], rest[3:]
        gathered, rest = rest[:nl], rest[nl:]
        (w_scr, vec_scr, vstage, n1_scr, ga_scr, w_s, w_r, w_l, v_s, v_r, v_l), rest = rest[:11], rest[11:]
        stages, sems = rest[:nl], rest[nl:]
        p, i = pl.program_id(0), pl.program_id(1)
        me = _my_index()
        wg = _RelayGather(w_scr, w_s, w_r)
        vg = _Direct(vstage, vec_scr, v_s, v_r, v_l, scatter=False)
        lg = [_TwoLevel(stages[k], gathered[k], sems[0].at[k], sems[1].at[k], sems[2].at[k]) for k in range(nl)]
        w_copy = pltpu.make_async_copy(w_scr, wout_ref, w_l)

        def at_pass(k):
            return (p == k) & (i == 0)

        c = lax.axis_index("c")

        @pl.when(at_pass(0))
        def _():
            vstage[...] = vsh_ref[...]
            vg.start()
            w_scr[me] = wsh_ref[...].astype(BF)
            wg.send_own(0).start()

            @pl.when(c == 1)
            def _():
                wg.send_own(1).start()

            @pl.when(c == 0)
            def _():
                wg.send_own(2).start()

            vg.finish()
            for j in range(N_DEV):
                ga_scr[:, j * ds:(j + 1) * ds] = vec_scr[j, 0:1, 0:ds]
            vout_ref[...] = vec_scr[...]

        @pl.when(at_pass(1))
        def _():
            wg.wait_sibling()

        for first, second, landed_first, landed_second in ((1, 2, wg.on_x, wg.on_y), (2, 1, wg.on_y, wg.on_x)):
            mine = c == (1 if first == 1 else 0)

            @pl.when(at_pass(2) & mine)
            def _(second=second, landed_first=landed_first):
                wg.send_own(second).start()
                landed_first()

            @pl.when(at_pass(3) & mine)
            def _(second=second):
                wg.wait_passed(second - 1)

            @pl.when(at_pass(4) & mine)
            def _(landed_second=landed_second):
                landed_second()

            @pl.when(at_pass(5) & mine)
            def _(first=first):
                wg.wait_passed(first - 1)

        @pl.when(at_pass(4))
        def _():
            for k in range(nl):
                stages[k][...] = shards[k][...].astype(BF)
                lg[k].start()

        @pl.when(at_pass(6))
        def _():
            wg.on_diag()

        @pl.when(at_pass(7))
        def _():
            wg.wait_passed(2)

        @pl.when(at_pass(last))
        def _():
            w_copy.start()

        @pl.when(p == 0)
        def _():
            xv = x_ref[...]
            r1 = lax.rsqrt(jnp.mean(xv * xv, axis=-1, keepdims=True) + EPS)
            n1_scr[i] = (xv * r1 * ga_scr[...]).astype(BF)

        z_ref[...] = _dot(n1_scr[i], w_scr[slots_ref[p]]).astype(BF)

        @pl.when((p == last) & (i == nT - 1))
        def _():
            wg.wait_sends()
            for g in lg:
                g.forward()
            for g in lg:
                g.finish()
            w_copy.wait()

    hbm = pl.BlockSpec(memory_space=pl.ANY)
    dma = pltpu.SemaphoreType.DMA
    S = jax.ShapeDtypeStruct
    grid_spec = pltpu.PrefetchScalarGridSpec(
        num_scalar_prefetch=1, grid=(N_DEV, nT),
        in_specs=[pl.BlockSpec((TM, D), lambda p, i, s: (jnp.where(p == 0, i, nT - 1), 0)),
                  pl.BlockSpec(w_shard.shape, lambda p, i, s: (0, 0), pipeline_mode=pl.Buffered(1)),
                  pl.BlockSpec(vec_shard.shape, lambda p, i, s: (0, 0), pipeline_mode=pl.Buffered(1))]
        + [pl.BlockSpec(w.shape, lambda p, i, s: (0, 0), pipeline_mode=pl.Buffered(1)) for w in later],
        out_specs=[pl.BlockSpec((TM, SH), lambda p, i, s: (i, s[p])), hbm,
                   pl.BlockSpec((N_DEV,) + vec_shard.shape, lambda p, i, s: (0, 0, 0))] + [hbm] * nl,
        scratch_shapes=[pltpu.VMEM((N_DEV, D, SH), BF), pltpu.VMEM((N_DEV,) + vec_shard.shape, F32),
                        pltpu.VMEM(vec_shard.shape, F32), pltpu.VMEM((nT, TM, D), BF), pltpu.VMEM((1, D), F32),
                        dma((8,)), dma((8,)), dma, dma((7,)), dma((7,)), dma]
        + [pltpu.VMEM(w.shape, BF) for w in later] + _direct_sems(nl))
    return pl.pallas_call(
        body, name="a_in_proj", grid_spec=grid_spec,
        out_shape=(S((T, N_DEV * SH), BF), S((N_DEV, D, SH), BF), S((N_DEV,) + vec_shard.shape, F32))
        + tuple(S((N_DEV,) + w.shape, BF) for w in later),
        compiler_params=_params(("arbitrary", "arbitrary")),
    )(slots, x, w_shard, vec_shard, *later)


def _a_fwd(x, z, ln_g, ln_b, ws, bs_t, wa_out, g_kv, w_kv, b_kv, rc, rs1, rs2, later):
    T, D = x.shape
    AW = wa_out.shape[0]
    G = ws.shape[0]
    TM = min(256, T)
    nT = T // TM
    nC = TM // CHUNK
    nl = len(later)

    def body(x_ref, u_ref, v_ref, gt_ref, lng_ref, lnb_ref, ws_ref, bst_ref, waout_ref, gkv_ref, wkv_ref, bkv_ref,
             rc_ref, rs1_ref, rs2_ref, *rest):
        shards, rest = rest[:nl], rest[nl:]
        (h1_ref, sv_ref, vhat_ref, rstd_ref, k4_ref, v4_ref, kt_ref, vt_ref), rest = rest[:8], rest[8:]
        gathered, sv_scr, stages, sems = rest[:nl], rest[nl], rest[nl + 1:2 * nl + 1], rest[2 * nl + 1:]
        i = pl.program_id(0)
        riding = _Riding(shards, gathered, stages, sems, nT)
        riding.begin(i)
        xv = x_ref[...]
        u = u_ref[...].astype(F32)
        v = v_ref[...].astype(F32)
        gt = gt_ref[...].astype(F32)
        mu = jnp.mean(v, axis=-1, keepdims=True)
        xc = v - mu
        rstd = lax.rsqrt(jnp.mean(xc * xc, axis=-1, keepdims=True) + EPS)
        vhat = xc * rstd
        vln = (vhat * lng_ref[...] + lnb_ref[...]).astype(BF)
        tri = lax.broadcasted_iota(jnp.int32, (CHUNK, CHUNK), 0) >= lax.broadcasted_iota(jnp.int32, (CHUNK, CHUNK), 1)
        for g in range(G):
            wsm = jnp.where(tri, ws_ref[g], 0.0).astype(BF)
            bias = bst_ref[:, g:g + 1]
            for c in range(nC):
                blk = vln[c * CHUNK:(c + 1) * CHUNK, g * CHUNK:(g + 1) * CHUNK]
                sv_scr[c * CHUNK:(c + 1) * CHUNK, g * CHUNK:(g + 1) * CHUNK] = _dot(wsm, blk) + bias
        sv = sv_scr[...]
        silu, _ = _silu_parts(gt)
        y = (u * sv * silu).astype(BF)
        h1 = xv + _dot(y, waout_ref[...])
        h1_ref[...] = h1
        sv_ref[...] = sv.astype(BF)
        vhat_ref[...] = vhat.astype(BF)
        rstd_ref[...] = jnp.broadcast_to(rstd, rstd_ref.shape)
        rkv = lax.rsqrt(jnp.mean(h1 * h1, axis=-1, keepdims=True) + EPS)
        nkv = (h1 * rkv * gkv_ref[...]).astype(BF)
        kv = _dot(nkv, wkv_ref[...]) + bkv_ref[...]
        k_rot = _rot(kv[:, :LANES], rc_ref[...], rs1_ref[...], rs2_ref[...])
        for src, ref, tref in ((k_rot, k4_ref, kt_ref), (kv[:, LANES:], v4_ref, vt_ref)):
            t4 = _split4(src)
            ref[...] = t4.astype(BF)
            for c in range(nC):
                for b in range(4):
                    blk = t4[c * CHUNK:(c + 1) * CHUNK, b * LANES:(b + 1) * LANES]
                    tref[c, b * LANES:(b + 1) * LANES, :] = blk.T.astype(BF)
        riding.end(i)

    row = functools.partial(_row_spec, TM)
    zcol = [pl.BlockSpec((TM, AW), functools.partial(lambda k, i: (i, k), k)) for k in range(3)]
    tr = pl.BlockSpec((nC, 4 * LANES, CHUNK), lambda i: (i, 0, 0))
    r_in, r_out, r_shape, r_scratch = _Riding.specs(later)
    S = jax.ShapeDtypeStruct
    return pl.pallas_call(
        body, name="a_fwd", grid=(nT,),
        in_specs=[row(D)] + zcol + [_const_spec((1, AW)), _const_spec((1, AW)),
                  _const_spec(ws.shape), _const_spec(bs_t.shape), _const_spec(wa_out.shape), _const_spec((1, D)),
                  _const_spec(w_kv.shape), _const_spec((1, 2 * LANES)), row(LANES), row(LANES), row(LANES)] + r_in,
        out_specs=[row(D), row(AW), row(AW), row(LANES), row(4 * LANES), row(4 * LANES), tr, tr] + r_out,
        out_shape=(S((T, D), F32), S((T, AW), BF), S((T, AW), BF), S((T, LANES), F32),
                   S((T, 4 * LANES), BF), S((T, 4 * LANES), BF),
                   S((T // CHUNK, 4 * LANES, CHUNK), BF), S((T // CHUNK, 4 * LANES, CHUNK), BF)) + r_shape,
        scratch_shapes=[pltpu.VMEM((TM, AW), F32)] + r_scratch,
        compiler_params=_params(("arbitrary",)),
    )(x, z, z, z, ln_g, ln_b, ws, bs_t, wa_out, g_kv, w_kv, b_kv, rc, rs1, rs2, *later)


def _b_fwd(h1, g_b, wb_in, bq, rc, rs1, rs2, k4, vt, sinks, wb_out, g_f, target):
    T, D = h1.shape
    BW = wb_out.shape[0]
    SH = wb_in.shape[2]
    TM = min(512, T)
    nC = TM // CHUNK
    nP = BW // LANES

    def body(h1_ref, gb_ref, wbin_ref, bq_ref, rc_ref, rs1_ref, rs2_ref, k4_ref, vt_ref, sink_ref, wbout_ref, gf_ref,
             tgt_ref, q_ref, g2_ref, o_ref, dh2_ref, dh2b_ref, loss_ref, dgf_ref, z_scr, o_scr):
        i = pl.program_id(0)
        h1v = h1_ref[...]
        r2 = lax.rsqrt(jnp.mean(h1v * h1v, axis=-1, keepdims=True) + EPS)
        n2 = (h1v * r2 * gb_ref[...]).astype(BF)
        for j in range(N_DEV):
            z_scr[:, j * SH:(j + 1) * SH] = _dot(n2, wbin_ref[j])
        c_t, s1_t, s2_t = rc_ref[...], rs1_ref[...], rs2_ref[...]
        for p in range(nP):
            cols = slice(p * LANES, (p + 1) * LANES)
            qp = _rot(z_scr[:, cols] + bq_ref[:, cols], c_t, s1_t, s2_t) * (HEAD_DIM ** -0.5)
            q_ref[:, cols] = qp.astype(BF)
        g2 = z_scr[:, BW:]
        g2_ref[...] = g2.astype(BF)
        upper = _upper()
        for c in range(nC):
            ci = i * nC + c
            rows = slice(c * CHUNK, (c + 1) * CHUNK)
            pci = jnp.maximum(ci - 1, 0)
            prev = pl.multiple_of(pci * CHUNK, CHUNK)
            cur = pl.multiple_of(ci * CHUNK, CHUNK)
            qc = q_ref[rows, :]
            for h in range(2):
                st = _dot_nt(_band_rows(k4_ref, prev, cur, h), _stack_pairs(qc, h))
                fa, fb = _fold(st, upper, ci > 0)
                pa, _ = _softmax_sink(fa, sink_ref[2 * h:2 * h + 1, :])
                pb, _ = _softmax_sink(fb, sink_ref[2 * h + 1:2 * h + 2, :])
                ot = _dot(_band_cols(vt_ref, pci, ci, h), _unfold(pa, pb, upper).astype(BF))
                for j in range(4):
                    o_scr[rows, (h * 4 + j) * LANES:(h * 4 + j + 1) * LANES] = ot[:, j * CHUNK:(j + 1) * CHUNK].T
        o = o_scr[...]
        o_ref[...] = o.astype(BF)
        silu, _ = _silu_parts(g2)
        h2 = h1v + _dot((o * silu).astype(BF), wbout_ref[...])
        rf = lax.rsqrt(jnp.mean(h2 * h2, axis=-1, keepdims=True) + EPS)
        xh = h2 * rf
        gf = gf_ref[...]
        err = xh * gf - tgt_ref[...]
        dyf = err * (1.0 / D)
        dh2 = _rms_bwd(dyf, xh, rf, gf)
        dh2_ref[...] = dh2
        dh2b_ref[...] = dh2.astype(BF)

        @pl.when(i == 0)
        def _():
            loss_ref[...] = jnp.zeros_like(loss_ref)
            dgf_ref[...] = jnp.zeros_like(dgf_ref)

        loss_ref[...] += 0.5 * jnp.sum(jnp.mean(err * err, axis=-1, keepdims=True), axis=0, keepdims=True)
        dgf_ref[...] += jnp.sum(dyf * xh, axis=0, keepdims=True)

    row = functools.partial(_row_spec, TM)
    S = jax.ShapeDtypeStruct
    return pl.pallas_call(
        body, name="b_fwd", grid=(T // TM,),
        in_specs=[row(D), _const_spec((1, D)), _const_spec(wb_in.shape), _const_spec((1, BW)), row(LANES), row(LANES),
                  row(LANES), _const_spec(k4.shape), _const_spec(vt.shape), _const_spec(sinks.shape),
                  _const_spec(wb_out.shape), _const_spec((1, D)), row(D)],
        out_specs=[row(BW), row(BW), row(BW), row(D), row(D), _acc_spec((1, 1)), _acc_spec((1, D))],
        out_shape=(S((T, BW), BF), S((T, BW), BF), S((T, BW), BF), S((T, D), F32), S((T, D), BF), S((1, 1), F32),
                   S((1, D), F32)),
        scratch_shapes=[pltpu.VMEM((TM, 2 * BW), F32), pltpu.VMEM((TM, BW), F32)],
        compiler_params=_params(("arbitrary",)),
    )(h1, g_b, wb_in, bq, rc, rs1, rs2, k4, vt, sinks, wb_out, g_f, target)


def _b_bwd(dh2, h1, q, g2, o, k4, v4, kt, sinks, wb_out, wb_in, g_b, rc, rs1, rs2):
    T, D = h1.shape
    BW = wb_out.shape[0]
    SH = wb_in.shape[2]
    TM = min(256, T)
    nT = T // TM
    nC = TM // CHUNK
    nP = BW // LANES

    def body(dh2_ref, h1_ref, q_ref, g2_ref, o_ref, k4_ref, v4_ref, kt_ref, sink_ref, wbout_ref, wbin_ref, gb_ref,
             rc_ref, rs1_ref, rs2_ref,
             dh1_ref, dz2_ref, n2_ref, y2_ref, dk_ref, dv_ref, dbq_ref, dgb_ref, dsink_ref, do_scr, dq_scr, dsacc_scr):
        i = pl.program_id(0)

        @pl.when(i == 0)
        def _():
            dk_ref[...] = jnp.zeros_like(dk_ref)
            dv_ref[...] = jnp.zeros_like(dv_ref)
            dbq_ref[...] = jnp.zeros_like(dbq_ref)
            dgb_ref[...] = jnp.zeros_like(dgb_ref)
            dsacc_scr[...] = jnp.zeros_like(dsacc_scr)

        dh2 = dh2_ref[...]
        dy2 = _dot_nt(dh2.astype(BF), wbout_ref[...])
        silu, dsilu = _silu_parts(g2_ref[...].astype(F32))
        do_scr[...] = (dy2 * silu).astype(BF)
        dy2, silu, dsilu = dy2.astype(BF), silu.astype(BF), dsilu.astype(BF)
        ob = o_ref[...]
        y2_ref[...] = (ob * silu).T
        dz2_ref[:, BW:] = dy2 * ob * dsilu
        upper = _upper()
        lo = _lane_lo((2 * CHUNK, LANES))
        for c in range(nC):
            ci = i * nC + c
            rows = slice(c * CHUNK, (c + 1) * CHUNK)
            pci = jnp.maximum(ci - 1, 0)
            prev = pl.multiple_of(pci * CHUNK, CHUNK)
            cur = pl.multiple_of(ci * CHUNK, CHUNK)
            qc = q_ref[rows, :]
            doc = do_scr[rows, :]
            dkb = jnp.zeros((2 * CHUNK, LANES), F32)
            dvb = jnp.zeros((2 * CHUNK, LANES), F32)
            for h in range(2):
                qs = _stack_pairs(qc, h)
                dos = _stack_pairs(doc, h)
                fa, fb = _fold(_dot_nt(_band_rows(k4_ref, prev, cur, h), qs), upper, ci > 0)
                dfa, dfb = _fold(_dot_nt(_band_rows(v4_ref, prev, cur, h), dos), upper)
                folded = []
                for k, (f, df) in enumerate(((fa, dfa), (fb, dfb))):
                    p, ps = _softmax_sink(f, sink_ref[2 * h + k:2 * h + k + 1, :])
                    delta = jnp.sum(p * df, axis=0, keepdims=True)
                    dsacc_scr[2 * h + k:2 * h + k + 1, :] -= ps * delta
                    folded.append((p, p * (df - delta)))
                pt = _unfold(folded[0][0], folded[1][0], upper).astype(BF)
                dst = _unfold(folded[0][1], folded[1][1], upper).astype(BF)
                dqt = _dot(_band_cols(kt_ref, pci, ci, h), dst)
                for j in range(4):
                    dq_scr[rows, (h * 4 + j) * LANES:(h * 4 + j + 1) * LANES] = dqt[:, j * CHUNK:(j + 1) * CHUNK].T
                for acc_name, g in (("k", _dot(dst, qs)), ("v", _dot(pt, dos))):
                    a, b = g[:2 * CHUNK], g[2 * CHUNK:]
                    if h == 0:
                        part = jnp.where(lo, a + pltpu.roll(b, HEAD_DIM, 1), 0.0)
                    else:
                        part = jnp.where(lo, 0.0, pltpu.roll(a, HEAD_DIM, 1) + b)
                    if acc_name == "k":
                        dkb += part
                    else:
                        dvb += part
            dk_ref[pl.ds(prev, CHUNK), :] += dkb[:CHUNK]
            dk_ref[pl.ds(cur, CHUNK), :] += dkb[CHUNK:]
            dv_ref[pl.ds(prev, CHUNK), :] += dvb[:CHUNK]
            dv_ref[pl.ds(cur, CHUNK), :] += dvb[CHUNK:]

        @pl.when(i == nT - 1)
        def _():
            lane = lax.broadcasted_iota(jnp.int32, dsink_ref.shape, 1)
            tot = jnp.zeros(dsink_ref.shape, F32)
            for j in range(4):
                tot += jnp.where(lane == j, jnp.sum(dsacc_scr[:, j * CHUNK:(j + 1) * CHUNK], axis=1, keepdims=True), 0.0)
            dsink_ref[...] = tot
        c_t, s1_t, s2_t = rc_ref[...], rs1_ref[...], rs2_ref[...]
        for p in range(nP):
            cols = slice(p * LANES, (p + 1) * LANES)
            dqp = _rot_bwd(dq_scr[:, cols] * (HEAD_DIM ** -0.5), c_t, s1_t, s2_t)
            dbq_ref[:, cols] += jnp.sum(dqp, axis=0, keepdims=True)
            dz2_ref[:, cols] = dqp.astype(BF)
        h1v = h1_ref[...]
        r2 = lax.rsqrt(jnp.mean(h1v * h1v, axis=-1, keepdims=True) + EPS)
        xh = h1v * r2
        gb = gb_ref[...]
        n2_ref[...] = (xh * gb).astype(BF).T
        dn2 = None
        for j in range(N_DEV):
            part = _dot_nt(dz2_ref[:, j * SH:(j + 1) * SH], wbin_ref[j])
            dn2 = part if dn2 is None else dn2 + part
        dgb_ref[...] += jnp.sum(dn2 * xh, axis=0, keepdims=True)
        dh1_ref[...] = dh2 + _rms_bwd(dn2, xh, r2, gb)

    row = functools.partial(_row_spec, TM)
    S = jax.ShapeDtypeStruct
    return pl.pallas_call(
        body, name="b_bwd", grid=(T // TM,),
        in_specs=[row(D), row(D), row(BW), row(BW), row(BW), _const_spec(k4.shape), _const_spec(v4.shape),
                  _const_spec(kt.shape), _const_spec(sinks.shape), _const_spec(wb_out.shape), _const_spec(wb_in.shape),
                  _const_spec((1, D)), row(LANES), row(LANES), row(LANES)],
        out_specs=[row(D), row(2 * BW), _col_spec(TM, D), _col_spec(TM, BW), _acc_spec((T, LANES)),
                   _acc_spec((T, LANES)), _acc_spec((1, BW)), _acc_spec((1, D)), _acc_spec((4, LANES))],
        out_shape=(S((T, D), F32), S((T, 2 * BW), BF), S((D, T), BF), S((BW, T), BF), S((T, LANES), F32),
                   S((T, LANES), F32), S((1, BW), F32), S((1, D), F32), S((4, LANES), F32)),
        scratch_shapes=[pltpu.VMEM((TM, BW), BF), pltpu.VMEM((TM, BW), F32), pltpu.VMEM((4, 4 * CHUNK), F32)],
        compiler_params=_params(("arbitrary",)),
    )(dh2, h1, q, g2, o, k4, v4, kt, sinks, wb_out, wb_in, g_b, rc, rs1, rs2)


def _a_bwd(dh1p, dk, dv, h1, g_kv, w_kv, wa_out, ws, ln_g, ln_b, z, sv, vhat, rstd, rc, rs1, rs2, ready):
    T, D = h1.shape
    AW = wa_out.shape[0]
    G = ws.shape[0]
    TM = min(256, T)
    nT = T // TM
    nC = TM // CHUNK
    nr = len(ready)

    def body(dh1p_ref, dk_ref, dv_ref, h1_ref, gkv_ref, wkv_ref, waout_ref, ws_ref, lng_ref,
             lnb_ref, u_ref, gt_ref, sv_ref, vhat_ref, rstd_ref, rc_ref, rs1_ref, rs2_ref, *rest):
        ready_refs, rest = rest[:nr], rest[nr:]
        (dz_ref, gwo_ref, gwk_ref, dh1f_ref, dgkv_ref, dbkv_ref, dlng_ref, dlnb_ref,
         dws_ref, dbs_ref), rest = rest[:10], rest[10:]
        recv_refs, (dsv_scr, dvln_scr, acco_scr, acck_scr, ssem, rsem, lsem) = rest[:nr], rest[nr:]
        i = pl.program_id(0)
        exchanges = [_Direct(ready_refs[k], recv_refs[k], ssem.at[k], rsem.at[k], lsem.at[k], scatter=True)
                     for k in range(nr)]

        @pl.when(i == 0)
        def _():
            for e in exchanges:
                e.start()
            for r in (dgkv_ref, dbkv_ref, dlng_ref, dlnb_ref, dws_ref, dbs_ref, acco_scr, acck_scr):
                r[...] = jnp.zeros_like(r)

        dk_pre = _rot_bwd(dk_ref[...], rc_ref[...], rs1_ref[...], rs2_ref[...])
        dkv = jnp.concatenate([dk_pre, dv_ref[...]], axis=1)
        dbkv_ref[...] += jnp.sum(dkv, axis=0, keepdims=True)
        dkv_b = dkv.astype(BF)
        h1v = h1_ref[...]
        rkv = lax.rsqrt(jnp.mean(h1v * h1v, axis=-1, keepdims=True) + EPS)
        xh_kv = h1v * rkv
        gkv = gkv_ref[...]
        acck_scr[...] += _dot((xh_kv * gkv).astype(BF).T, dkv_b)
        dnkv = _dot_nt(dkv_b, wkv_ref[...])
        dgkv_ref[...] += jnp.sum(dnkv * xh_kv, axis=0, keepdims=True)
        dh1 = dh1p_ref[...] + _rms_bwd(dnkv, xh_kv, rkv, gkv)
        dh1_b = dh1.astype(BF)
        dh1f_ref[...] = dh1
        dy = _dot_nt(dh1_b, waout_ref[...]).astype(BF)
        silu, dsilu = _silu_parts(gt_ref[...].astype(F32))
        silu, dsilu = silu.astype(BF), dsilu.astype(BF)
        ub, svb = u_ref[...], sv_ref[...]
        us = ub * silu
        dys = dy * svb
        acco_scr[...] += _dot((us * svb).T, dh1_b)
        dz_ref[:, :AW] = dys * silu
        dz_ref[:, 2 * AW:] = dys * ub * dsilu
        dsv_scr[...] = dy * us
        vhat_v = vhat_ref[...].astype(F32)
        lng = lng_ref[...]
        vln_b = (vhat_v * lng + lnb_ref[...]).astype(BF)
        tri = lax.broadcasted_iota(jnp.int32, (CHUNK, CHUNK), 0) >= lax.broadcasted_iota(jnp.int32, (CHUNK, CHUNK), 1)
        lane = lax.broadcasted_iota(jnp.int32, (CHUNK, LANES), 1)
        dbs = jnp.zeros((CHUNK, LANES), F32)
        for g in range(G):
            wsm = jnp.where(tri, ws_ref[g], 0.0).astype(BF)
            cols = slice(g * CHUNK, (g + 1) * CHUNK)
            dws_g = None
            for c in range(nC):
                rows = slice(c * CHUNK, (c + 1) * CHUNK)
                dsv_cg = dsv_scr[rows, cols]
                dvln_scr[rows, cols] = _dot_tn(wsm, dsv_cg)
                part = _dot_nt(dsv_cg, vln_b[rows, cols])
                dws_g = part if dws_g is None else dws_g + part
                dbs += jnp.where(lane == g, jnp.sum(dsv_cg.astype(F32), axis=-1, keepdims=True), 0.0)
            dws_ref[g] += jnp.where(tri, dws_g, 0.0)
        dbs_ref[...] += dbs
        dvln = dvln_scr[...]
        dlng_ref[...] += jnp.sum(dvln * vhat_v, axis=0, keepdims=True)
        dlnb_ref[...] += jnp.sum(dvln, axis=0, keepdims=True)
        a = dvln * lng
        dvv = rstd_ref[:, 0:1] * (a - jnp.mean(a, axis=-1, keepdims=True)
                                  - vhat_v * jnp.mean(a * vhat_v, axis=-1, keepdims=True))
        dz_ref[:, AW:2 * AW] = dvv.astype(BF)

        @pl.when(i == nT - 1)
        def _():
            for j in range(N_DEV):
                gwo_ref[j] = acco_scr[j * (AW // N_DEV):(j + 1) * (AW // N_DEV)].astype(BF)
                gwk_ref[j] = acck_scr[j * (D // N_DEV):(j + 1) * (D // N_DEV)].astype(BF)
            for e in exchanges:
                e.finish()

    row = functools.partial(_row_spec, TM)
    hbm = pl.BlockSpec(memory_space=pl.ANY)
    S = jax.ShapeDtypeStruct
    gwo_shape, gwk_shape = (N_DEV, AW // N_DEV, D), (N_DEV, D // N_DEV, 2 * LANES)
    return pl.pallas_call(
        body, name="a_bwd", grid=(nT,),
        in_specs=[row(D), row(LANES), row(LANES), row(D), _const_spec((1, D)), _const_spec(w_kv.shape),
                  _const_spec(wa_out.shape), _const_spec(ws.shape),
                  _const_spec((1, AW)), _const_spec((1, AW)), pl.BlockSpec((TM, AW), lambda i: (i, 0)),
                  pl.BlockSpec((TM, AW), lambda i: (i, 2)), row(AW), row(AW), row(LANES),
                  row(LANES), row(LANES), row(LANES)] + [hbm] * nr,
        out_specs=[row(3 * AW), _const_spec(gwo_shape), _const_spec(gwk_shape), row(D),
                   _acc_spec((1, D)), _acc_spec((1, 2 * LANES)), _acc_spec((1, AW)),
                   _acc_spec((1, AW)), _acc_spec(ws.shape), _acc_spec((CHUNK, LANES))] + [hbm] * nr,
        out_shape=(S((T, 3 * AW), BF), S(gwo_shape, BF), S(gwk_shape, BF), S((T, D), F32),
                   S((1, D), F32), S((1, 2 * LANES), F32), S((1, AW), F32), S((1, AW), F32),
                   S(ws.shape, F32), S((CHUNK, LANES), F32)) + tuple(S(r.shape, r.dtype) for r in ready),
        scratch_shapes=[pltpu.VMEM((TM, AW), BF), pltpu.VMEM((TM, AW), F32), pltpu.VMEM((AW, D), F32),
                        pltpu.VMEM((D, 2 * LANES), F32)] + _direct_sems(nr),
        compiler_params=_params(("arbitrary",)),
    )(dh1p, dk, dv, h1, g_kv, w_kv, wa_out, ws, ln_g, ln_b, z, z, sv, vhat, rstd, rc, rs1, rs2, *ready)


def _a_in_bwd(dz, wa_in, x, dh1, g_a, ready):
    T, D = x.shape
    SH = wa_in.shape[2]
    TM = min(512, T)
    nT = T // TM
    nr = len(ready)

    def body(dz_ref, wain_ref, x_ref, dh1_ref, ga_ref, *rest):
        ready_refs, (dx_ref, n1_ref, dga_ref), rest = rest[:nr], rest[nr:nr + 3], rest[nr + 3:]
        recv_refs, (ssem, rsem, lsem) = rest[:nr], rest[nr:]
        i = pl.program_id(0)
        exchanges = [_Direct(ready_refs[k], recv_refs[k], ssem.at[k], rsem.at[k], lsem.at[k], scatter=True)
                     for k in range(nr)]

        @pl.when(i == 0)
        def _():
            for e in exchanges:
                e.start()
            dga_ref[...] = jnp.zeros_like(dga_ref)

        xv = x_ref[...]
        r1 = lax.rsqrt(jnp.mean(xv * xv, axis=-1, keepdims=True) + EPS)
        xh = xv * r1
        ga = ga_ref[...]
        n1_ref[...] = (xh * ga).astype(BF).T
        dn1 = None
        for j in range(N_DEV):
            part = _dot_nt(dz_ref[:, j * SH:(j + 1) * SH], wain_ref[j])
            dn1 = part if dn1 is None else dn1 + part
        dga_ref[...] += jnp.sum(dn1 * xh, axis=0, keepdims=True)
        dx_ref[...] = dh1_ref[...] + _rms_bwd(dn1, xh, r1, ga)

        @pl.when(i == nT - 1)
        def _():
            for e in exchanges:
                e.finish()

    row = functools.partial(_row_spec, TM)
    hbm = pl.BlockSpec(memory_space=pl.ANY)
    S = jax.ShapeDtypeStruct
    return pl.pallas_call(
        body, name="a_in_bwd", grid=(nT,),
        in_specs=[row(dz.shape[1]), _const_spec(wa_in.shape), row(D), row(D), _const_spec((1, D))] + [hbm] * nr,
        out_specs=[row(D), _col_spec(TM, D), _acc_spec((1, D))] + [hbm] * nr,
        out_shape=(S((T, D), F32), S((D, T), BF), S((1, D), F32)) + tuple(S(r.shape, r.dtype) for r in ready),
        scratch_shapes=_direct_sems(nr),
        compiler_params=_params(("arbitrary",)),
    )(dz, wa_in, x, dh1, g_a, *ready)


def _wgrad(at, b, nblk, name, bt=512):
    K, T = at.shape
    N = b.shape[1] // nblk
    BT = min(bt, T)
    nt = T // BT

    def body(a_ref, b_ref, o_ref, acc):
        t = pl.program_id(0)

        @pl.when(t == 0)
        def _():
            acc[...] = jnp.zeros_like(acc)

        acc[...] += _dot(a_ref[...], b_ref[...])

        @pl.when(t == nt - 1)
        def _():
            for j in range(nblk):
                o_ref[j] = acc[:, j * N:(j + 1) * N].astype(BF)

    return pl.pallas_call(
        body, name=name, grid=(nt,),
        in_specs=[pl.BlockSpec((K, BT), lambda t: (0, t)), pl.BlockSpec((BT, nblk * N), lambda t: (t, 0))],
        out_specs=pl.BlockSpec((nblk, K, N), lambda t: (0, 0, 0)),
        out_shape=jax.ShapeDtypeStruct((nblk, K, N), BF),
        scratch_shapes=[pltpu.VMEM((K, nblk * N), F32)],
        compiler_params=_params(("arbitrary",)),
    )(at, b)


def _wgrad_exchange(a, b, me, small, name):
    K, T = a.shape
    N = b.shape[1] // N_DEV
    BT = min(1024, T)
    nt = T // BT
    last = N_DEV - 1
    n_chip = N_DEV // 2

    def body(me_ref, a_ref, b_ref, small_ref, recv_ref, full_ref, *scratch):
        (acc, dstage, istage, half, d_s, d_r, i_s, i_r, lsem, parts_scr, red_scr, e_s, e_r, e_l, g_s, g_r,
         g_l) = scratch
        s, t = pl.program_id(0), pl.program_id(1)
        x, y, c = (lax.axis_index(ax) for ax in AXES)
        ex = [_Direct(small_ref, parts_scr, e_s, e_r, e_l, scatter=True)]
        regather = _TwoLevel(red_scr, full_ref, g_s, g_r, g_l)

        def to_sibling(k, slot):
            return pltpu.make_async_remote_copy(src_ref=dstage.at[slot], dst_ref=half.at[k], send_sem=d_s.at[k],
                                                recv_sem=d_r.at[k], device_id=(x, y, 1 - c), device_id_type=MESH)

        def to_chip(k, slot, sender):
            far = n_chip - 1 - k
            px, py = x ^ ((far >> 1) & 1), y ^ (far & 1)
            dst = recv_ref.at[2 * x + y] if sender else recv_ref.at[2 * px + py]
            return pltpu.make_async_remote_copy(src_ref=istage.at[slot], dst_ref=dst, send_sem=i_s.at[k],
                                                recv_sem=i_r.at[k], device_id=(px, py, c), device_id_type=MESH)

        @pl.when((s == 0) & (t == 0))
        def _():
            for e in ex:
                e.start()

        @pl.when(t == 0)
        def _():
            acc[...] = jnp.zeros_like(acc)

        acc[...] += _dot(a_ref[...], b_ref[...])

        @pl.when(t == nt - 1)
        def _():
            k = lax.div(s, 2)
            slot = lax.rem(k, 2)

            @pl.when(lax.rem(s, 2) == 0)
            def _():
                @pl.when(k >= 2)
                def _():
                    to_sibling(k - 2, slot).wait_send()

                dstage[slot] = acc[...].astype(BF)
                to_sibling(k, slot).start()

            @pl.when(lax.rem(s, 2) == 1)
            def _():
                to_sibling(k, slot).wait_recv()

                @pl.when(k >= 2)
                def _():
                    to_chip(k - 2, slot, True).wait_send()

                istage[slot] = (acc[...] + half[k].astype(F32)).astype(BF)

                @pl.when(k < n_chip - 1)
                def _():
                    to_chip(k, slot, True).start()

            @pl.when(s == last)
            def _():
                own = pltpu.make_async_copy(istage.at[slot], recv_ref.at[2 * x + y], lsem)
                own.start()
                to_chip(n_chip - 2, 0, True).wait_send()
                to_sibling(n_chip - 2, 0).wait_send()
                to_sibling(n_chip - 1, 1).wait_send()
                for kk in range(n_chip - 1):
                    to_chip(kk, 0, False).wait_recv()
                own.wait()
                for e in ex:
                    e.finish()
                total = parts_scr[0]
                for dev in range(1, N_DEV):
                    total = total + parts_scr[dev]
                red_scr[...] = total
                regather.start()
                regather.forward()
                regather.finish()

    hbm = pl.BlockSpec(memory_space=pl.ANY)
    dma = pltpu.SemaphoreType.DMA
    grid_spec = pltpu.PrefetchScalarGridSpec(
        num_scalar_prefetch=1, grid=(N_DEV, nt),
        in_specs=[pl.BlockSpec((K, BT), lambda s, t, me_ref: (0, t)),
                  pl.BlockSpec((BT, N), lambda s, t, me_ref: (t, me_ref[0] ^ (last - s))), hbm],
        out_specs=[hbm, hbm],
        scratch_shapes=[pltpu.VMEM((K, N), F32), pltpu.VMEM((2, K, N), BF), pltpu.VMEM((2, K, N), BF),
                        pltpu.VMEM((n_chip, K, N), BF), dma((n_chip,)), dma((n_chip,)), dma((n_chip - 1,)),
                        dma((n_chip - 1,)), dma, pltpu.VMEM(small.shape, F32), pltpu.VMEM(small.shape[1:], F32),
                        dma((last,)), dma((last,)), dma, dma((last,)), dma((last,)), dma])
    return pl.pallas_call(
        body, name=name, grid_spec=grid_spec,
        out_shape=[jax.ShapeDtypeStruct((n_chip, K, N), BF), jax.ShapeDtypeStruct(small.shape, F32)],
        compiler_params=_params(("arbitrary", "arbitrary")),
    )(me, a, b, small)


def _my_index():
    return 4 * lax.axis_index("x") + 2 * lax.axis_index("y") + lax.axis_index("c")


def _peer(mask):
    x, y, c = (lax.axis_index(a) for a in AXES)
    return (x ^ ((mask >> 2) & 1), y ^ ((mask >> 1) & 1), c ^ (mask & 1))


def _dev_index(p):
    return 4 * p[0] + 2 * p[1] + p[2]


class _Direct:
    def __init__(self, src, dst, send_sems, recv_sems, local_sem, scatter):
        me = _my_index()
        self.own = pltpu.make_async_copy(src.at[me] if scatter else src, dst.at[me], local_sem)
        self.sends, self.recvs = [], []
        for k in range(1, N_DEV):
            p = _peer(k)
            pi = _dev_index(p)
            sems = dict(send_sem=send_sems.at[k - 1], recv_sem=recv_sems.at[k - 1], device_id=p, device_id_type=MESH)
            self.sends.append(pltpu.make_async_remote_copy(src_ref=src.at[pi] if scatter else src, dst_ref=dst.at[me],
                                                           **sems))
            self.recvs.append(pltpu.make_async_remote_copy(src_ref=src.at[me] if scatter else src, dst_ref=dst.at[pi],
                                                           **sems))

    def start(self):
        self.own.start()
        for cp in self.sends:
            cp.start()

    def finish(self):
        for cp in self.sends:
            cp.wait_send()
        for cp in self.recvs:
            cp.wait_recv()
        self.own.wait()


class _TwoLevel:
    def __init__(self, src, dst, send_sems, recv_sems, local_sem, own=True):
        x, y, c = (lax.axis_index(a) for a in AXES)
        self.me, self.sibling = (x, y, c), (x, y, 1 - c)
        self.chips = [(1 - x, y), (x, 1 - y), (1 - x, 1 - y)]
        self.src, self.dst, self.send_sems, self.recv_sems = src, dst, send_sems, recv_sems
        self.own = pltpu.make_async_copy(src, dst.at[_dev_index(self.me)], local_sem) if own else None

    def _copy(self, k, block, to, from_src=False):
        slot = self.dst.at[_dev_index(block)]
        return pltpu.make_async_remote_copy(src_ref=self.src if from_src else slot, dst_ref=slot,
                                            send_sem=self.send_sems.at[k], recv_sem=self.recv_sems.at[k],
                                            device_id=to, device_id_type=MESH)

    def _firsts(self):
        c = self.me[2]
        return [self._copy(0, self.me, self.sibling, True)] + [self._copy(1 + j, self.me, (*chip, c), True)
                                                               for j, chip in enumerate(self.chips)]

    def _passed(self):
        c = self.me[2]
        return [self._copy(4 + j, (*chip, c), self.sibling) for j, chip in enumerate(self.chips)]

    def start(self):
        if self.own is not None:
            self.own.start()
        for cp in self._firsts():
            cp.start()

    def wait_sibling(self):
        self._copy(0, self.sibling, self.me).wait_recv()

    def wait_chip_and_forward(self, j):
        self._copy(1 + j, (*self.chips[j], self.me[2]), self.me).wait_recv()
        self._passed()[j].start()

    def wait_passed(self, j):
        self._copy(4 + j, (*self.chips[j], 1 - self.me[2]), self.me).wait_recv()

    def wait_sends(self):
        for cp in self._firsts() + self._passed():
            cp.wait_send()
        if self.own is not None:
            self.own.wait()

    def forward(self):
        for j in range(3):
            self.wait_chip_and_forward(j)

    def finish(self):
        self.wait_sibling()
        for j in range(3):
            self.wait_passed(j)
        self.wait_sends()


class _RelayGather:
    def __init__(self, dst, send_sems, recv_sems):
        x, y, c = (lax.axis_index(a) for a in AXES)
        self.c = c
        self.sib, self.xn, self.yn, self.dg = (x, y, 1 - c), (1 - x, y, c), (x, 1 - y, c), (1 - x, 1 - y, c)
        self.me = (x, y, c)
        self.dst, self.send_sems, self.recv_sems = dst, send_sems, recv_sems
        self.half = dst.shape[1] // 2

    def _slot(self, dev, part=None):
        i = _dev_index(dev)
        if part is None:
            return self.dst.at[i]
        return self.dst.at[i, pl.ds(part * self.half, self.half)]

    def _copy(self, k, dev, to, part=None):
        ref = self._slot(dev, part)
        return pltpu.make_async_remote_copy(src_ref=ref, dst_ref=ref, send_sem=self.send_sems.at[k],
                                            recv_sem=self.recv_sems.at[k], device_id=to, device_id_type=MESH)

    def _other(self, dev):
        return (dev[0], dev[1], 1 - self.c)

    def start(self):
        for k, to in enumerate((self.sib, self.xn, self.yn)):
            self._copy(k, self.me, to).start()

    def send_own(self, k):
        return self._copy(k, self.me, (self.sib, self.xn, self.yn)[k])

    def wait_sibling(self):
        self._copy(0, self.sib, self.me).wait_recv()

    def on_x(self):
        self._copy(1, self.xn, self.me).wait_recv()
        self._copy(3, self.xn, self.yn, part=0).start()
        self._copy(5, self.xn, self.sib).start()

    def on_y(self):
        self._copy(2, self.yn, self.me).wait_recv()
        self._copy(4, self.yn, self.xn, part=1).start()
        self._copy(6, self.yn, self.sib).start()

    def on_diag(self):
        self._copy(3, self.dg, self.me, part=0).wait_recv()
        self._copy(4, self.dg, self.me, part=1).wait_recv()
        self._copy(7, self.dg, self.sib).start()

    def wait_passed(self, j):
        self._copy(5 + j, self._other((self.xn, self.yn, self.dg)[j]), self.me).wait_recv()

    def wait_sends(self):
        for k, to in enumerate((self.sib, self.xn, self.yn)):
            self._copy(k, self.me, to).wait_send()
        self._copy(3, self.xn, self.yn, part=0).wait_send()
        self._copy(4, self.yn, self.xn, part=1).wait_send()
        for j, dev in enumerate((self.xn, self.yn, self.dg)):
            self._copy(5 + j, dev, self.sib).wait_send()


def _direct_sems(n):
    if n == 0:
        return []
    return [pltpu.SemaphoreType.DMA((n, 7)), pltpu.SemaphoreType.DMA((n, 7)), pltpu.SemaphoreType.DMA((n,))]


def _adam_math(w, g, m, v):
    m = ADAM_B1 * m + (1.0 - ADAM_B1) * g
    v = ADAM_B2 * v + (1.0 - ADAM_B2) * (g * g)
    m_hat = m / (1.0 - ADAM_B1 ** ADAM_STEP)
    v_hat = v / (1.0 - ADAM_B2 ** ADAM_STEP)
    delta = -ADAM_LR * (m_hat / (jnp.sqrt(v_hat) + ADAM_EPS) + ADAM_WD * w)
    return delta, m, v


def _sum_adam(parts, w, m, v, name):
    R, C = w.shape
    NP = parts.shape[0]
    BR = CHUNK if R % CHUNK == 0 else R

    def body(p_ref, w_ref, m_ref, v_ref, g_ref, d_ref, nm_ref, nv_ref):
        g = p_ref[0].astype(F32)
        for i in range(1, NP):
            g = g + p_ref[i].astype(F32)
        g_ref[...] = g
        d_ref[...], nm_ref[...], nv_ref[...] = _adam_math(w_ref[...], g, m_ref[...], v_ref[...])

    blk = pl.BlockSpec((BR, C), lambda i: (i, 0))
    S = jax.ShapeDtypeStruct((R, C), F32)
    return pl.pallas_call(
        body, name=name, grid=(R // BR,),
        in_specs=[pl.BlockSpec((NP, BR, C), lambda i: (0, i, 0)), blk, blk, blk],
        out_specs=[blk] * 4, out_shape=(S,) * 4,
        compiler_params=_params(("arbitrary",)),
    )(parts, w, m, v)


SUBLANES = 8


def _nrows(size):
    return -(-size // (SUBLANES * LANES)) * SUBLANES


def _view2d(a):
    return a.reshape(-1, LANES) if a.size % LANES == 0 else a.reshape(1, -1)


def _pack_small(parts, total_rows, name):
    arrs = [p[0] for p in parts]

    def body(*refs):
        out = refs[-1]
        out[...] = jnp.zeros_like(out)
        at = 0
        for ref, (a, rows, flag) in zip(refs[:-1], parts):
            val = ref[...].T if flag == "T" else ref[...]
            r, c = (rows, val.shape[1]) if flag == "T" else val.shape
            out[at:at + r, 0:c] = val[:r]
            at += _nrows(r * c)

    return pl.pallas_call(body, name=name, out_shape=jax.ShapeDtypeStruct((total_rows, LANES), F32))(*arrs)


def _small_update(full, me, reps, shards, name):
    n = len(reps) + len(shards)

    def body(me_ref, full_ref, *refs):
        ins, outs = refs[:3 * n], refs[3 * n:]
        at = 0
        for k in range(n):
            w_ref, m_ref, v_ref = ins[3 * k:3 * k + 3]
            r, c = w_ref.shape
            if k < len(reps):
                g = full_ref[at:at + r, 0:c]
                at += _nrows(r * c)
            else:
                seg = full_ref[at:at + N_DEV * r, :]
                row = lax.broadcasted_iota(jnp.int32, seg.shape, 0)
                pick = [jnp.sum(jnp.where(row == r * me_ref[0] + t, seg, 0.0), axis=0, keepdims=True) for t in range(r)]
                g = pick[0] if r == 1 else jnp.concatenate(pick, axis=0)
                at += N_DEV * r
            g_ref, d_ref, nm_ref, nv_ref = outs[4 * k:4 * k + 4]
            g_ref[...] = g
            d_ref[...], nm_ref[...], nv_ref[...] = _adam_math(w_ref[...], g, m_ref[...], v_ref[...])
        outs[4 * n][...] = full_ref[at:at + 1, 0:1]

    flat = [t for p in reps + shards for t in p]
    S = jax.ShapeDtypeStruct
    res = pl.pallas_call(
        body, name=name,
        in_specs=[pl.BlockSpec(memory_space=pltpu.SMEM)] + [pl.BlockSpec(memory_space=pltpu.VMEM)] * (1 + len(flat)),
        out_shape=[S(p[0].shape, F32) for p in reps + shards for _ in range(4)] + [S((1, 1), F32)],
    )(me, full, *flat)
    return [tuple(res[4 * k:4 * k + 4]) for k in range(n)], res[4 * n]


def _rope_tables(T):
    pos = np.arange(T, dtype=np.float32)
    inv_freq = (np.float64(ROPE_THETA) ** (-np.arange(0, HEAD_DIM, 2, dtype=np.float64) / HEAD_DIM)).astype(np.float32)
    ang = (pos[:, None] * inv_freq[None, :]).astype(np.float64)
    cos, sin, zero = np.cos(ang).astype(np.float32), np.sin(ang).astype(np.float32), np.zeros(ang.shape, np.float32)
    c = np.concatenate([cos, cos, cos, cos], axis=1)
    s1 = np.concatenate([-sin, zero, -sin, zero], axis=1)
    s2 = np.concatenate([zero, sin, zero, sin], axis=1)
    return jnp.asarray(c), jnp.asarray(s1), jnp.asarray(s2)


def kernel(x, a_norm_g, a_w_in, a_ln_g, a_ln_b, a_ws, a_bs, a_w_out, kv_norm_g, w_kv, b_kv, b_norm_g, b_w_in, b_bq, b_sinks, b_w_out, final_norm_g, loss_target, m_a_norm_g, m_a_w_in, m_a_ln_g, m_a_ln_b, m_a_ws, m_a_bs, m_a_w_out, m_kv_norm_g, m_w_kv, m_b_kv, m_b_norm_g, m_b_w_in, m_b_bq, m_b_sinks, m_b_w_out, m_final_norm_g, v_a_norm_g, v_a_w_in, v_a_ln_g, v_a_ln_b, v_a_ws, v_a_bs, v_a_w_out, v_kv_norm_g, v_w_kv, v_b_kv, v_b_norm_g, v_b_w_in, v_b_bq, v_b_sinks, v_b_w_out, v_final_norm_g):
    T, D = x.shape[1], x.shape[2]
    AW = a_ln_g.shape[1] * N_DEV
    G = a_ws.shape[1]
    assert w_kv.shape[1] == 2 * LANES and a_ws.shape[2] == CHUNK and T % CHUNK == 0
    me = _my_index()

    xs, tgt = x[0], loss_target[0]
    vec = jnp.concatenate([a_norm_g, a_ln_g, a_ln_b], axis=1)
    vec = jnp.broadcast_to(vec, (SUBLANES, vec.shape[1]))
    north = lax.axis_index("c") == 1
    slots = me ^ jnp.where(north, jnp.array(PASS_MASKS[1], jnp.int32), jnp.array(PASS_MASKS[0], jnp.int32))
    z, wa_in, vecs, wa_out, wkv = _in_proj(xs, a_w_in[0], vec, slots, [a_w_out[0], w_kv])
    wa_out = wa_out.reshape(AW, D)
    wkv = wkv.reshape(D, 2 * LANES)
    vecs = vecs[:, 0, :]
    ds = D // N_DEV
    g_a = vecs[:, :ds].reshape(1, D)
    ln_g = vecs[:, ds:ds + AW // N_DEV].reshape(1, AW)
    ln_b = vecs[:, ds + AW // N_DEV:].reshape(1, AW)

    rc, rs1, rs2 = _rope_tables(T)
    ws = a_ws[0]
    bs_t = a_bs[0].T
    g_kv = kv_norm_g.reshape(1, D)
    bkv = b_kv.reshape(1, -1)
    g_f = final_norm_g.reshape(1, D)
    sinks = jnp.repeat(b_sinks.reshape(2, 4, 2).transpose(0, 2, 1).reshape(4, 4), CHUNK, axis=1)
    h1, sv, vhat, rstd, k4, v4, kt, vt, wb_in, wb_out = _a_fwd(
        xs, z, ln_g, ln_b, ws, bs_t, wa_out, g_kv, wkv, bkv, rc, rs1, rs2, [b_w_in[0], b_w_out[0]])
    wb_out = wb_out.reshape(-1, D)
    q, g2, o, dh2, dh2_b, loss, d_gf = _b_fwd(h1, b_norm_g, wb_in, b_bq, rc, rs1, rs2, k4, vt, sinks, wb_out, g_f, tgt)
    dh1p, dz2, n2, y2, dk, dv, d_bq, d_gb, d_sink = _b_bwd(dh2, h1, q, g2, o, k4, v4, kt, sinks, wb_out, wb_in,
                                                           b_norm_g, rc, rs1, rs2)
    d_sink = d_sink[:, :4].reshape(2, 2, 4).transpose(0, 2, 1).reshape(1, 16)
    gw_b_in = _wgrad(n2, dz2, N_DEV, "wgrad_b_in", bt=1024)
    gw_b_out = _wgrad(y2, dh2_b, 1, "wgrad_b_out", bt=1024).reshape(N_DEV, -1, D)
    (dz, gw_a_out, gw_kv, dh1_f, d_gkv, d_bkv, d_lng, d_lnb, d_ws, d_bst, r_b_in, r_b_out) = _a_bwd(
        dh1p, dk, dv, h1, g_kv, wkv, wa_out, ws, ln_g, ln_b, z, sv, vhat, rstd, rc, rs1, rs2, [gw_b_in, gw_b_out])
    dx, n1, d_ga, r_a_out, r_kv = _a_in_bwd(dz, wa_in, xs, dh1_f, g_a, [gw_a_out, gw_kv])
    small = [(_view2d(d_ws), None, None), (d_bst, G, "T")] + [(_view2d(a), None, None) for a in (
        d_gkv, d_bkv, d_gb, d_bq, d_sink, d_gf, d_ga, d_lng, d_lnb, loss)]
    used = sum(_nrows(G * CHUNK if flag else a.size) for a, _, flag in small)
    per = -(-used // (SUBLANES * N_DEV)) * SUBLANES
    small_pack = _pack_small(small, per * N_DEV, "pack_small").reshape(N_DEV, per, LANES)
    r_a_in, full_small = _wgrad_exchange(n1, dz, me.reshape(1), small_pack, "wgrad_a_in")

    g_a_in, d_a_in, nm_a_in, nv_a_in = _sum_adam(r_a_in, a_w_in[0], m_a_w_in[0], v_a_w_in[0], "adam_a_in")
    g_a_out, d_a_out, nm_a_out, nv_a_out = _sum_adam(r_a_out, a_w_out[0], m_a_w_out[0], v_a_w_out[0], "adam_a_out")
    g_kvw, d_kvw, nm_kvw, nv_kvw = _sum_adam(r_kv, w_kv, m_w_kv, v_w_kv, "adam_kv")
    g_b_in, d_b_in, nm_b_in, nv_b_in = _sum_adam(r_b_in, b_w_in[0], m_b_w_in[0], v_b_w_in[0], "adam_b_in")
    g_b_out, d_b_out, nm_b_out, nv_b_out = _sum_adam(r_b_out, b_w_out[0], m_b_w_out[0], v_b_w_out[0], "adam_b_out")

    full_small = full_small.reshape(N_DEV * per, LANES)
    reps = [(a_ws, m_a_ws, v_a_ws), (a_bs, m_a_bs, v_a_bs), (kv_norm_g, m_kv_norm_g, v_kv_norm_g),
            (b_kv, m_b_kv, v_b_kv), (b_norm_g, m_b_norm_g, v_b_norm_g), (b_bq, m_b_bq, v_b_bq),
            (b_sinks, m_b_sinks, v_b_sinks), (final_norm_g, m_final_norm_g, v_final_norm_g)]
    shards = [(a_norm_g, m_a_norm_g, v_a_norm_g), (a_ln_g, m_a_ln_g, v_a_ln_g), (a_ln_b, m_a_ln_b, v_a_ln_b)]
    upd, loss = _small_update(full_small, me.reshape(1), [tuple(_view2d(t) for t in p) for p in reps],
                              [tuple(_view2d(t) for t in p) for p in shards], "adam_small")
    loss = loss[0, 0]
    sm_g, sd, snm, snv = ([upd[k][j].reshape(p[0].shape) for k, p in enumerate(reps + shards)] for j in range(4))

    def order(big, sm):
        a_in, a_out, kvw, b_in, b_out = big
        ws_, bs_, kvg, bkv_, bng, bq_, snk, fng, ang, alng, alnb = sm
        return (ang, a_in[None], alng, alnb, ws_, bs_, a_out[None], kvg, kvw, bkv_, bng, b_in[None], bq_, snk,
                b_out[None], fng)

    grads = order((g_a_in, g_a_out, g_kvw, g_b_in, g_b_out), sm_g)
    deltas = order((d_a_in, d_a_out, d_kvw, d_b_in, d_b_out), sd)
    new_m = order((nm_a_in, nm_a_out, nm_kvw, nm_b_in, nm_b_out), snm)
    new_v = order((nv_a_in, nv_a_out, nv_kvw, nv_b_in, nv_b_out), snv)
    return (loss, dx[None], *grads, *deltas, *new_m, *new_v)
```

```python
import functools

import jax
import jax.numpy as jnp
import numpy as np
from jax import lax
from jax.experimental import pallas as pl
from jax.experimental.pallas import tpu as pltpu

CHUNK = 128
HEAD_DIM = 64
ROPE_THETA = 10000.0
EPS = 1e-5
ADAM_LR = 0.001
ADAM_B1 = 0.9
ADAM_B2 = 0.999
ADAM_EPS = 1e-08
ADAM_WD = 0.01
ADAM_STEP = 10
N_DEV = 8
LANES = 128
NEG = -1e30

BF = jnp.bfloat16
F32 = jnp.float32
MESH = pl.DeviceIdType.MESH
AXES = ("x", "y", "c")
VMEM_LIMIT = 56 * 1024 * 1024


def _dot(a, b):
    return jnp.dot(a, b, preferred_element_type=F32)


def _dot_nt(a, b):
    return lax.dot_general(a, b, (((1,), (1,)), ((), ())), preferred_element_type=F32)


def _dot_tn(a, b):
    return lax.dot_general(a, b, (((0,), (0,)), ((), ())), preferred_element_type=F32)


def _const_spec(shape):
    nd = len(shape)
    return pl.BlockSpec(shape, lambda *_: (0,) * nd, pipeline_mode=pl.Buffered(1))


def _acc_spec(shape):
    nd = len(shape)
    return pl.BlockSpec(shape, lambda *_: (0,) * nd)


def _row_spec(tm, width):
    return pl.BlockSpec((tm, width), lambda i: (i, 0))


def _col_spec(tm, height):
    return pl.BlockSpec((height, tm), lambda i: (0, i))


def _params(sem):
    return pltpu.CompilerParams(dimension_semantics=sem, vmem_limit_bytes=VMEM_LIMIT)


def _rot(x, c, s1, s2):
    return x * c + pltpu.roll(x, 96, 1) * s1 + pltpu.roll(x, 32, 1) * s2


def _rot_bwd(d, c, s1, s2):
    return d * c + pltpu.roll(d * s1, 32, 1) + pltpu.roll(d * s2, 96, 1)


def _silu_parts(g):
    sg = jax.nn.sigmoid(g)
    return g * sg, sg * (1.0 + g * (1.0 - sg))


def _rms_bwd(dn, xh, r, g):
    a = dn * g
    return r * (a - xh * jnp.mean(a * xh, axis=-1, keepdims=True))


def _lane_lo(shape):
    return lax.broadcasted_iota(jnp.int32, shape, 1) < HEAD_DIM


def _split4(t):
    lo = _lane_lo(t.shape)
    tr = pltpu.roll(t, HEAD_DIM, 1)
    z = jnp.zeros_like(t)
    return jnp.concatenate([jnp.where(lo, t, z), jnp.where(lo, z, tr), jnp.where(lo, tr, z), jnp.where(lo, z, t)], axis=1)


def _stack_pairs(t, h):
    return jnp.concatenate([t[:, (h * 4 + j) * LANES:(h * 4 + j + 1) * LANES] for j in range(4)], axis=0)


def _upper():
    shape = (CHUNK, 4 * CHUNK)
    return lax.broadcasted_iota(jnp.int32, shape, 0) > (lax.broadcasted_iota(jnp.int32, shape, 1) & (CHUNK - 1))


def _band_rows(ref, prev, cur, h):
    a = slice(2 * h * LANES, (2 * h + 1) * LANES)
    b = slice((2 * h + 1) * LANES, (2 * h + 2) * LANES)
    return jnp.concatenate([ref[pl.ds(prev, CHUNK), a], ref[pl.ds(cur, CHUNK), a],
                            ref[pl.ds(prev, CHUNK), b], ref[pl.ds(cur, CHUNK), b]], axis=0)


def _band_cols(ref, pci, ci, h):
    a = slice(2 * h * LANES, (2 * h + 1) * LANES)
    b = slice((2 * h + 1) * LANES, (2 * h + 2) * LANES)
    return jnp.concatenate([ref[pci, a, :], ref[ci, a, :], ref[pci, b, :], ref[ci, b, :]], axis=1)


def _fold(t, upper, has_prev=None):
    out = []
    for k in range(2):
        prev = t[2 * k * CHUNK:(2 * k + 1) * CHUNK]
        if has_prev is not None:
            prev = jnp.where(has_prev, prev, NEG)
        out.append(jnp.where(upper, prev, t[(2 * k + 1) * CHUNK:(2 * k + 2) * CHUNK]))
    return out


def _unfold(fa, fb, upper):
    z = jnp.zeros_like(fa)
    return jnp.concatenate([jnp.where(upper, fa, z), jnp.where(upper, z, fa),
                            jnp.where(upper, fb, z), jnp.where(upper, z, fb)], axis=0)


def _softmax_sink(f, sink):
    m = jnp.maximum(jnp.max(f, axis=0, keepdims=True), sink)
    p = jnp.exp(f - m)
    es = jnp.exp(sink - m)
    inv = 1.0 / (jnp.sum(p, axis=0, keepdims=True) + es)
    return p * inv, es * inv


class _Riding:
    def __init__(self, shards, gathered, stages, sems, n_steps):
        self.shards, self.stages, self.n_steps = shards, stages, n_steps
        ssem, rsem, lsem = sems
        self.gathers = [_TwoLevel(stages[k], gathered[k], ssem.at[k], rsem.at[k], lsem.at[k])
                        for k in range(len(shards))]

    def begin(self, i):
        @pl.when(i == 0)
        def _():
            for shard, stage, g in zip(self.shards, self.stages, self.gathers):
                stage[...] = shard[...].astype(stage.dtype)
                g.start()

    def end(self, i):
        @pl.when(i == self.n_steps // 2)
        def _():
            for g in self.gathers:
                g.forward()

        @pl.when(i == self.n_steps - 1)
        def _():
            for g in self.gathers:
                g.finish()

    @staticmethod
    def specs(later):
        nl = len(later)
        hbm = pl.BlockSpec(memory_space=pl.ANY)
        return ([_const_spec(w.shape) for w in later], [hbm] * nl,
                tuple(jax.ShapeDtypeStruct((N_DEV,) + w.shape, BF) for w in later),
                [pltpu.VMEM(w.shape, BF) for w in later] + _direct_sems(nl))


PASS_MASKS = ((0, 1, 2, 5, 4, 3, 6, 7), (0, 1, 4, 3, 2, 5, 6, 7))


def _in_proj(x, w_shard, vec_shard, slots, later):
    T, D = x.shape
    SH = w_shard.shape[1]
    TM = min(1024, T)
    nT = T // TM
    nl = len(later)
    ds = D // N_DEV
    last = N_DEV - 1

    def body(slots_ref, x_ref, wsh_ref, vsh_ref, *rest):
        shards, rest = rest[:nl], rest[nl:]
        (z_ref, wout_ref, vout_ref), rest = rest[:3], rest[3:]
        gathered, rest = rest[:nl], rest[nl:]
        (w_scr, vec_scr, vstage, n1_scr, ga_scr, w_s, w_r, w_l, v_s, v_r, v_l), rest = rest[:11], rest[11:]
        stages, sems = rest[:nl], rest[nl:]
        p, i = pl.program_id(0), pl.program_id(1)
        me = _my_index()
        wg = _RelayGather(w_scr, w_s, w_r)
        vg = _Direct(vstage, vec_scr, v_s, v_r, v_l, scatter=False)
        lg = [_TwoLevel(stages[k], gathered[k], sems[0].at[k], sems[1].at[k], sems[2].at[k]) for k in range(nl)]
        w_copy = pltpu.make_async_copy(w_scr, wout_ref, w_l)

        def at_pass(k):
            return (p == k) & (i == 0)

        c = lax.axis_index("c")

        @pl.when(at_pass(0))
        def _():
            vstage[...] = vsh_ref[...]
            vg.start()
            w_scr[me] = wsh_ref[...].astype(BF)
            wg.send_own(0).start()

            @pl.when(c == 1)
            def _():
                wg.send_own(1).start()

            @pl.when(c == 0)
            def _():
                wg.send_own(2).start()

            vg.finish()
            for j in range(N_DEV):
                ga_scr[:, j * ds:(j + 1) * ds] = vec_scr[j, 0:1, 0:ds]
            vout_ref[...] = vec_scr[...]

        @pl.when(at_pass(1))
        def _():
            wg.wait_sibling()

        for first, second, landed_first, landed_second in ((1, 2, wg.on_x, wg.on_y), (2, 1, wg.on_y, wg.on_x)):
            mine = c == (1 if first == 1 else 0)

            @pl.when(at_pass(2) & mine)
            def _(second=second, landed_first=landed_first):
                wg.send_own(second).start()
                landed_first()

            @pl.when(at_pass(3) & mine)
            def _(second=second):
                wg.wait_passed(second - 1)

            @pl.when(at_pass(4) & mine)
            def _(landed_second=landed_second):
                landed_second()

            @pl.when(at_pass(5) & mine)
            def _(first=first):
                wg.wait_passed(first - 1)

        @pl.when(at_pass(4))
        def _():
            for k in range(nl):
                stages[k][...] = shards[k][...].astype(BF)
                lg[k].start()

        @pl.when(at_pass(6))
        def _():
            wg.on_diag()

        @pl.when(at_pass(7))
        def _():
            wg.wait_passed(2)

        @pl.when(at_pass(last))
        def _():
            w_copy.start()

        @pl.when(p == 0)
        def _():
            xv = x_ref[...]
            r1 = lax.rsqrt(jnp.mean(xv * xv, axis=-1, keepdims=True) + EPS)
            n1_scr[i] = (xv * r1 * ga_scr[...]).astype(BF)

        z_ref[...] = _dot(n1_scr[i], w_scr[slots_ref[p]]).astype(BF)

        @pl.when((p == last) & (i == nT - 1))
        def _():
            wg.wait_sends()
            for g in lg:
                g.forward()
            for g in lg:
                g.finish()
            w_copy.wait()

    hbm = pl.BlockSpec(memory_space=pl.ANY)
    dma = pltpu.SemaphoreType.DMA
    S = jax.ShapeDtypeStruct
    grid_spec = pltpu.PrefetchScalarGridSpec(
        num_scalar_prefetch=1, grid=(N_DEV, nT),
        in_specs=[pl.BlockSpec((TM, D), lambda p, i, s: (jnp.where(p == 0, i, nT - 1), 0)),
                  pl.BlockSpec(w_shard.shape, lambda p, i, s: (0, 0), pipeline_mode=pl.Buffered(1)),
                  pl.BlockSpec(vec_shard.shape, lambda p, i, s: (0, 0), pipeline_mode=pl.Buffered(1))]
        + [pl.BlockSpec(w.shape, lambda p, i, s: (0, 0), pipeline_mode=pl.Buffered(1)) for w in later],
        out_specs=[pl.BlockSpec((TM, SH), lambda p, i, s: (i, s[p])), hbm,
                   pl.BlockSpec((N_DEV,) + vec_shard.shape, lambda p, i, s: (0, 0, 0))] + [hbm] * nl,
        scratch_shapes=[pltpu.VMEM((N_DEV, D, SH), BF), pltpu.VMEM((N_DEV,) + vec_shard.shape, F32),
                        pltpu.VMEM(vec_shard.shape, F32), pltpu.VMEM((nT, TM, D), BF), pltpu.VMEM((1, D), F32),
                        dma((8,)), dma((8,)), dma, dma((7,)), dma((7,)), dma]
        + [pltpu.VMEM(w.shape, BF) for w in later] + _direct_sems(nl))
    return pl.pallas_call(
        body, name="a_in_proj", grid_spec=grid_spec,
        out_shape=(S((T, N_DEV * SH), BF), S((N_DEV, D, SH), BF), S((N_DEV,) + vec_shard.shape, F32))
        + tuple(S((N_DEV,) + w.shape, BF) for w in later),
        compiler_params=_params(("arbitrary", "arbitrary")),
    )(slots, x, w_shard, vec_shard, *later)


def _a_fwd(x, z, ln_g, ln_b, ws, bs_t, wa_out, g_kv, w_kv, b_kv, rc, rs1, rs2, later):
    T, D = x.shape
    AW = wa_out.shape[0]
    G = ws.shape[0]
    TM = min(256, T)
    nT = T // TM
    nC = TM // CHUNK
    nl = len(later)

    def body(x_ref, u_ref, v_ref, gt_ref, lng_ref, lnb_ref, ws_ref, bst_ref, waout_ref, gkv_ref, wkv_ref, bkv_ref,
             rc_ref, rs1_ref, rs2_ref, *rest):
        shards, rest = rest[:nl], rest[nl:]
        (h1_ref, sv_ref, vhat_ref, rstd_ref, k4_ref, v4_ref, kt_ref, vt_ref), rest = rest[:8], rest[8:]
        gathered, sv_scr, stages, sems = rest[:nl], rest[nl], rest[nl + 1:2 * nl + 1], rest[2 * nl + 1:]
        i = pl.program_id(0)
        riding = _Riding(shards, gathered, stages, sems, nT)
        riding.begin(i)
        xv = x_ref[...]
        u = u_ref[...].astype(F32)
        v = v_ref[...].astype(F32)
        gt = gt_ref[...].astype(F32)
        mu = jnp.mean(v, axis=-1, keepdims=True)
        xc = v - mu
        rstd = lax.rsqrt(jnp.mean(xc * xc, axis=-1, keepdims=True) + EPS)
        vhat = xc * rstd
        vln = (vhat * lng_ref[...] + lnb_ref[...]).astype(BF)
        tri = lax.broadcasted_iota(jnp.int32, (CHUNK, CHUNK), 0) >= lax.broadcasted_iota(jnp.int32, (CHUNK, CHUNK), 1)
        for g in range(G):
            wsm = jnp.where(tri, ws_ref[g], 0.0).astype(BF)
            bias = bst_ref[:, g:g + 1]
            for c in range(nC):
                blk = vln[c * CHUNK:(c + 1) * CHUNK, g * CHUNK:(g + 1) * CHUNK]
                sv_scr[c * CHUNK:(c + 1) * CHUNK, g * CHUNK:(g + 1) * CHUNK] = _dot(wsm, blk) + bias
        sv = sv_scr[...]
        silu, _ = _silu_parts(gt)
        y = (u * sv * silu).astype(BF)
        h1 = xv + _dot(y, waout_ref[...])
        h1_ref[...] = h1
        sv_ref[...] = sv.astype(BF)
        vhat_ref[...] = vhat.astype(BF)
        rstd_ref[...] = jnp.broadcast_to(rstd, rstd_ref.shape)
        rkv = lax.rsqrt(jnp.mean(h1 * h1, axis=-1, keepdims=True) + EPS)
        nkv = (h1 * rkv * gkv_ref[...]).astype(BF)
        kv = _dot(nkv, wkv_ref[...]) + bkv_ref[...]
        k_rot = _rot(kv[:, :LANES], rc_ref[...], rs1_ref[...], rs2_ref[...])
        for src, ref, tref in ((k_rot, k4_ref, kt_ref), (kv[:, LANES:], v4_ref, vt_ref)):
            t4 = _split4(src)
            ref[...] = t4.astype(BF)
            for c in range(nC):
                for b in range(4):
                    blk = t4[c * CHUNK:(c + 1) * CHUNK, b * LANES:(b + 1) * LANES]
                    tref[c, b * LANES:(b + 1) * LANES, :] = blk.T.astype(BF)
        riding.end(i)

    row = functools.partial(_row_spec, TM)
    zcol = [pl.BlockSpec((TM, AW), functools.partial(lambda k, i: (i, k), k)) for k in range(3)]
    tr = pl.BlockSpec((nC, 4 * LANES, CHUNK), lambda i: (i, 0, 0))
    r_in, r_out, r_shape, r_scratch = _Riding.specs(later)
    S = jax.ShapeDtypeStruct
    return pl.pallas_call(
        body, name="a_fwd", grid=(nT,),
        in_specs=[row(D)] + zcol + [_const_spec((1, AW)), _const_spec((1, AW)),
                  _const_spec(ws.shape), _const_spec(bs_t.shape), _const_spec(wa_out.shape), _const_spec((1, D)),
                  _const_spec(w_kv.shape), _const_spec((1, 2 * LANES)), row(LANES), row(LANES), row(LANES)] + r_in,
        out_specs=[row(D), row(AW), row(AW), row(LANES), row(4 * LANES), row(4 * LANES), tr, tr] + r_out,
        out_shape=(S((T, D), F32), S((T, AW), BF), S((T, AW), BF), S((T, LANES), F32),
                   S((T, 4 * LANES), BF), S((T, 4 * LANES), BF),
                   S((T // CHUNK, 4 * LANES, CHUNK), BF), S((T // CHUNK, 4 * LANES, CHUNK), BF)) + r_shape,
        scratch_shapes=[pltpu.VMEM((TM, AW), F32)] + r_scratch,
        compiler_params=_params(("arbitrary",)),
    )(x, z, z, z, ln_g, ln_b, ws, bs_t, wa_out, g_kv, w_kv, b_kv, rc, rs1, rs2, *later)


def _b_fwd(h1, g_b, wb_in, bq, rc, rs1, rs2, k4, vt, sinks, wb_out, g_f, target):
    T, D = h1.shape
    BW = wb_out.shape[0]
    SH = wb_in.shape[2]
    TM = min(512, T)
    nC = TM // CHUNK
    nP = BW // LANES

    def body(h1_ref, gb_ref, wbin_ref, bq_ref, rc_ref, rs1_ref, rs2_ref, k4_ref, vt_ref, sink_ref, wbout_ref, gf_ref,
             tgt_ref, q_ref, g2_ref, o_ref, dh2_ref, dh2b_ref, loss_ref, dgf_ref, z_scr, o_scr):
        i = pl.program_id(0)
        h1v = h1_ref[...]
        r2 = lax.rsqrt(jnp.mean(h1v * h1v, axis=-1, keepdims=True) + EPS)
        n2 = (h1v * r2 * gb_ref[...]).astype(BF)
        for j in range(N_DEV):
            z_scr[:, j * SH:(j + 1) * SH] = _dot(n2, wbin_ref[j])
        c_t, s1_t, s2_t = rc_ref[...], rs1_ref[...], rs2_ref[...]
        for p in range(nP):
            cols = slice(p * LANES, (p + 1) * LANES)
            qp = _rot(z_scr[:, cols] + bq_ref[:, cols], c_t, s1_t, s2_t) * (HEAD_DIM ** -0.5)
            q_ref[:, cols] = qp.astype(BF)
        g2 = z_scr[:, BW:]
        g2_ref[...] = g2.astype(BF)
        upper = _upper()
        for c in range(nC):
            ci = i * nC + c
            rows = slice(c * CHUNK, (c + 1) * CHUNK)
            pci = jnp.maximum(ci - 1, 0)
            prev = pl.multiple_of(pci * CHUNK, CHUNK)
            cur = pl.multiple_of(ci * CHUNK, CHUNK)
            qc = q_ref[rows, :]
            for h in range(2):
                st = _dot_nt(_band_rows(k4_ref, prev, cur, h), _stack_pairs(qc, h))
                fa, fb = _fold(st, upper, ci > 0)
                pa, _ = _softmax_sink(fa, sink_ref[2 * h:2 * h + 1, :])
                pb, _ = _softmax_sink(fb, sink_ref[2 * h + 1:2 * h + 2, :])
                ot = _dot(_band_cols(vt_ref, pci, ci, h), _unfold(pa, pb, upper).astype(BF))
                for j in range(4):
                    o_scr[rows, (h * 4 + j) * LANES:(h * 4 + j + 1) * LANES] = ot[:, j * CHUNK:(j + 1) * CHUNK].T
        o = o_scr[...]
        o_ref[...] = o.astype(BF)
        silu, _ = _silu_parts(g2)
        h2 = h1v + _dot((o * silu).astype(BF), wbout_ref[...])
        rf = lax.rsqrt(jnp.mean(h2 * h2, axis=-1, keepdims=True) + EPS)
        xh = h2 * rf
        gf = gf_ref[...]
        err = xh * gf - tgt_ref[...]
        dyf = err * (1.0 / D)
        dh2 = _rms_bwd(dyf, xh, rf, gf)
        dh2_ref[...] = dh2
        dh2b_ref[...] = dh2.astype(BF)

        @pl.when(i == 0)
        def _():
            loss_ref[...] = jnp.zeros_like(loss_ref)
            dgf_ref[...] = jnp.zeros_like(dgf_ref)

        loss_ref[...] += 0.5 * jnp.sum(jnp.mean(err * err, axis=-1, keepdims=True), axis=0, keepdims=True)
        dgf_ref[...] += jnp.sum(dyf * xh, axis=0, keepdims=True)

    row = functools.partial(_row_spec, TM)
    S = jax.ShapeDtypeStruct
    return pl.pallas_call(
        body, name="b_fwd", grid=(T // TM,),
        in_specs=[row(D), _const_spec((1, D)), _const_spec(wb_in.shape), _const_spec((1, BW)), row(LANES), row(LANES),
                  row(LANES), _const_spec(k4.shape), _const_spec(vt.shape), _const_spec(sinks.shape),
                  _const_spec(wb_out.shape), _const_spec((1, D)), row(D)],
        out_specs=[row(BW), row(BW), row(BW), row(D), row(D), _acc_spec((1, 1)), _acc_spec((1, D))],
        out_shape=(S((T, BW), BF), S((T, BW), BF), S((T, BW), BF), S((T, D), F32), S((T, D), BF), S((1, 1), F32),
                   S((1, D), F32)),
        scratch_shapes=[pltpu.VMEM((TM, 2 * BW), F32), pltpu.VMEM((TM, BW), F32)],
        compiler_params=_params(("arbitrary",)),
    )(h1, g_b, wb_in, bq, rc, rs1, rs2, k4, vt, sinks, wb_out, g_f, target)


def _b_bwd(dh2, h1, q, g2, o, k4, v4, kt, sinks, wb_out, wb_in, g_b, rc, rs1, rs2):
    T, D = h1.shape
    BW = wb_out.shape[0]
    SH = wb_in.shape[2]
    TM = min(256, T)
    nT = T // TM
    nC = TM // CHUNK
    nP = BW // LANES

    def body(dh2_ref, h1_ref, q_ref, g2_ref, o_ref, k4_ref, v4_ref, kt_ref, sink_ref, wbout_ref, wbin_ref, gb_ref,
             rc_ref, rs1_ref, rs2_ref,
             dh1_ref, dz2_ref, n2_ref, y2_ref, dk_ref, dv_ref, dbq_ref, dgb_ref, dsink_ref, do_scr, dq_scr, dsacc_scr):
        i = pl.program_id(0)

        @pl.when(i == 0)
        def _():
            dk_ref[...] = jnp.zeros_like(dk_ref)
            dv_ref[...] = jnp.zeros_like(dv_ref)
            dbq_ref[...] = jnp.zeros_like(dbq_ref)
            dgb_ref[...] = jnp.zeros_like(dgb_ref)
            dsacc_scr[...] = jnp.zeros_like(dsacc_scr)

        dh2 = dh2_ref[...]
        dy2 = _dot_nt(dh2.astype(BF), wbout_ref[...])
        silu, dsilu = _silu_parts(g2_ref[...].astype(F32))
        do_scr[...] = (dy2 * silu).astype(BF)
        dy2, silu, dsilu = dy2.astype(BF), silu.astype(BF), dsilu.astype(BF)
        ob = o_ref[...]
        y2_ref[...] = (ob * silu).T
        dz2_ref[:, BW:] = dy2 * ob * dsilu
        upper = _upper()
        lo = _lane_lo((2 * CHUNK, LANES))
        for c in range(nC):
            ci = i * nC + c
            rows = slice(c * CHUNK, (c + 1) * CHUNK)
            pci = jnp.maximum(ci - 1, 0)
            prev = pl.multiple_of(pci * CHUNK, CHUNK)
            cur = pl.multiple_of(ci * CHUNK, CHUNK)
            qc = q_ref[rows, :]
            doc = do_scr[rows, :]
            dkb = jnp.zeros((2 * CHUNK, LANES), F32)
            dvb = jnp.zeros((2 * CHUNK, LANES), F32)
            for h in range(2):
                qs = _stack_pairs(qc, h)
                dos = _stack_pairs(doc, h)
                fa, fb = _fold(_dot_nt(_band_rows(k4_ref, prev, cur, h), qs), upper, ci > 0)
                dfa, dfb = _fold(_dot_nt(_band_rows(v4_ref, prev, cur, h), dos), upper)
                folded = []
                for k, (f, df) in enumerate(((fa, dfa), (fb, dfb))):
                    p, ps = _softmax_sink(f, sink_ref[2 * h + k:2 * h + k + 1, :])
                    delta = jnp.sum(p * df, axis=0, keepdims=True)
                    dsacc_scr[2 * h + k:2 * h + k + 1, :] -= ps * delta
                    folded.append((p, p * (df - delta)))
                pt = _unfold(folded[0][0], folded[1][0], upper).astype(BF)
                dst = _unfold(folded[0][1], folded[1][1], upper).astype(BF)
                dqt = _dot(_band_cols(kt_ref, pci, ci, h), dst)
                for j in range(4):
                    dq_scr[rows, (h * 4 + j) * LANES:(h * 4 + j + 1) * LANES] = dqt[:, j * CHUNK:(j + 1) * CHUNK].T
                for acc_name, g in (("k", _dot(dst, qs)), ("v", _dot(pt, dos))):
                    a, b = g[:2 * CHUNK], g[2 * CHUNK:]
                    if h == 0:
                        part = jnp.where(lo, a + pltpu.roll(b, HEAD_DIM, 1), 0.0)
                    else:
                        part = jnp.where(lo, 0.0, pltpu.roll(a, HEAD_DIM, 1) + b)
                    if acc_name == "k":
                        dkb += part
                    else:
                        dvb += part
            dk_ref[pl.ds(prev, CHUNK), :] += dkb[:CHUNK]
            dk_ref[pl.ds(cur, CHUNK), :] += dkb[CHUNK:]
            dv_ref[pl.ds(prev, CHUNK), :] += dvb[:CHUNK]
            dv_ref[pl.ds(cur, CHUNK), :] += dvb[CHUNK:]

        @pl.when(i == nT - 1)
        def _():
            lane = lax.broadcasted_iota(jnp.int32, dsink_ref.shape, 1)
            tot = jnp.zeros(dsink_ref.shape, F32)
            for j in range(4):
                tot += jnp.where(lane == j, jnp.sum(dsacc_scr[:, j * CHUNK:(j + 1) * CHUNK], axis=1, keepdims=True), 0.0)
            dsink_ref[...] = tot
        c_t, s1_t, s2_t = rc_ref[...], rs1_ref[...], rs2_ref[...]
        for p in range(nP):
            cols = slice(p * LANES, (p + 1) * LANES)
            dqp = _rot_bwd(dq_scr[:, cols] * (HEAD_DIM ** -0.5), c_t, s1_t, s2_t)
            dbq_ref[:, cols] += jnp.sum(dqp, axis=0, keepdims=True)
            dz2_ref[:, cols] = dqp.astype(BF)
        h1v = h1_ref[...]
        r2 = lax.rsqrt(jnp.mean(h1v * h1v, axis=-1, keepdims=True) + EPS)
        xh = h1v * r2
        gb = gb_ref[...]
        n2_ref[...] = (xh * gb).astype(BF).T
        dn2 = None
        for j in range(N_DEV):
            part = _dot_nt(dz2_ref[:, j * SH:(j + 1) * SH], wbin_ref[j])
            dn2 = part if dn2 is None else dn2 + part
        dgb_ref[...] += jnp.sum(dn2 * xh, axis=0, keepdims=True)
        dh1_ref[...] = dh2 + _rms_bwd(dn2, xh, r2, gb)

    row = functools.partial(_row_spec, TM)
    S = jax.ShapeDtypeStruct
    return pl.pallas_call(
        body, name="b_bwd", grid=(T // TM,),
        in_specs=[row(D), row(D), row(BW), row(BW), row(BW), _const_spec(k4.shape), _const_spec(v4.shape),
                  _const_spec(kt.shape), _const_spec(sinks.shape), _const_spec(wb_out.shape), _const_spec(wb_in.shape),
                  _const_spec((1, D)), row(LANES), row(LANES), row(LANES)],
        out_specs=[row(D), row(2 * BW), _col_spec(TM, D), _col_spec(TM, BW), _acc_spec((T, LANES)),
                   _acc_spec((T, LANES)), _acc_spec((1, BW)), _acc_spec((1, D)), _acc_spec((4, LANES))],
        out_shape=(S((T, D), F32), S((T, 2 * BW), BF), S((D, T), BF), S((BW, T), BF), S((T, LANES), F32),
                   S((T, LANES), F32), S((1, BW), F32), S((1, D), F32), S((4, LANES), F32)),
        scratch_shapes=[pltpu.VMEM((TM, BW), BF), pltpu.VMEM((TM, BW), F32), pltpu.VMEM((4, 4 * CHUNK), F32)],
        compiler_params=_params(("arbitrary",)),
    )(dh2, h1, q, g2, o, k4, v4, kt, sinks, wb_out, wb_in, g_b, rc, rs1, rs2)


def _a_bwd(dh1p, dk, dv, h1, g_kv, w_kv, wa_out, ws, ln_g, ln_b, z, sv, vhat, rstd, rc, rs1, rs2, ready):
    T, D = h1.shape
    AW = wa_out.shape[0]
    G = ws.shape[0]
    TM = min(256, T)
    nT = T // TM
    nC = TM // CHUNK
    nr = len(ready)

    def body(dh1p_ref, dk_ref, dv_ref, h1_ref, gkv_ref, wkv_ref, waout_ref, ws_ref, lng_ref,
             lnb_ref, u_ref, gt_ref, sv_ref, vhat_ref, rstd_ref, rc_ref, rs1_ref, rs2_ref, *rest):
        ready_refs, rest = rest[:nr], rest[nr:]
        (dz_ref, gwo_ref, gwk_ref, dh1f_ref, dgkv_ref, dbkv_ref, dlng_ref, dlnb_ref,
         dws_ref, dbs_ref), rest = rest[:10], rest[10:]
        recv_refs, (dsv_scr, dvln_scr, acco_scr, acck_scr, ssem, rsem, lsem) = rest[:nr], rest[nr:]
        i = pl.program_id(0)
        exchanges = [_Direct(ready_refs[k], recv_refs[k], ssem.at[k], rsem.at[k], lsem.at[k], scatter=True)
                     for k in range(nr)]

        @pl.when(i == 0)
        def _():
            for e in exchanges:
                e.start()
            for r in (dgkv_ref, dbkv_ref, dlng_ref, dlnb_ref, dws_ref, dbs_ref, acco_scr, acck_scr):
                r[...] = jnp.zeros_like(r)

        dk_pre = _rot_bwd(dk_ref[...], rc_ref[...], rs1_ref[...], rs2_ref[...])
        dkv = jnp.concatenate([dk_pre, dv_ref[...]], axis=1)
        dbkv_ref[...] += jnp.sum(dkv, axis=0, keepdims=True)
        dkv_b = dkv.astype(BF)
        h1v = h1_ref[...]
        rkv = lax.rsqrt(jnp.mean(h1v * h1v, axis=-1, keepdims=True) + EPS)
        xh_kv = h1v * rkv
        gkv = gkv_ref[...]
        acck_scr[...] += _dot((xh_kv * gkv).astype(BF).T, dkv_b)
        dnkv = _dot_nt(dkv_b, wkv_ref[...])
        dgkv_ref[...] += jnp.sum(dnkv * xh_kv, axis=0, keepdims=True)
        dh1 = dh1p_ref[...] + _rms_bwd(dnkv, xh_kv, rkv, gkv)
        dh1_b = dh1.astype(BF)
        dh1f_ref[...] = dh1
        dy = _dot_nt(dh1_b, waout_ref[...]).astype(BF)
        silu, dsilu = _silu_parts(gt_ref[...].astype(F32))
        silu, dsilu = silu.astype(BF), dsilu.astype(BF)
        ub, svb = u_ref[...], sv_ref[...]
        us = ub * silu
        dys = dy * svb
        acco_scr[...] += _dot((us * svb).T, dh1_b)
        dz_ref[:, :AW] = dys * silu
        dz_ref[:, 2 * AW:] = dys * ub * dsilu
        dsv_scr[...] = dy * us
        vhat_v = vhat_ref[...].astype(F32)
        lng = lng_ref[...]
        vln_b = (vhat_v * lng + lnb_ref[...]).astype(BF)
        tri = lax.broadcasted_iota(jnp.int32, (CHUNK, CHUNK), 0) >= lax.broadcasted_iota(jnp.int32, (CHUNK, CHUNK), 1)
        lane = lax.broadcasted_iota(jnp.int32, (CHUNK, LANES), 1)
        dbs = jnp.zeros((CHUNK, LANES), F32)
        for g in range(G):
            wsm = jnp.where(tri, ws_ref[g], 0.0).astype(BF)
            cols = slice(g * CHUNK, (g + 1) * CHUNK)
            dws_g = None
            for c in range(nC):
                rows = slice(c * CHUNK, (c + 1) * CHUNK)
                dsv_cg = dsv_scr[rows, cols]
                dvln_scr[rows, cols] = _dot_tn(wsm, dsv_cg)
                part = _dot_nt(dsv_cg, vln_b[rows, cols])
                dws_g = part if dws_g is None else dws_g + part
                dbs += jnp.where(lane == g, jnp.sum(dsv_cg.astype(F32), axis=-1, keepdims=True), 0.0)
            dws_ref[g] += jnp.where(tri, dws_g, 0.0)
        dbs_ref[...] += dbs
        dvln = dvln_scr[...]
        dlng_ref[...] += jnp.sum(dvln * vhat_v, axis=0, keepdims=True)
        dlnb_ref[...] += jnp.sum(dvln, axis=0, keepdims=True)
        a = dvln * lng
        dvv = rstd_ref[:, 0:1] * (a - jnp.mean(a, axis=-1, keepdims=True)
                                  - vhat_v * jnp.mean(a * vhat_v, axis=-1, keepdims=True))
        dz_ref[:, AW:2 * AW] = dvv.astype(BF)

        @pl.when(i == nT - 1)
        def _():
            for j in range(N_DEV):
                gwo_ref[j] = acco_scr[j * (AW // N_DEV):(j + 1) * (AW // N_DEV)].astype(BF)
                gwk_ref[j] = acck_scr[j * (D // N_DEV):(j + 1) * (D // N_DEV)].astype(BF)
            for e in exchanges:
                e.finish()

    row = functools.partial(_row_spec, TM)
    hbm = pl.BlockSpec(memory_space=pl.ANY)
    S = jax.ShapeDtypeStruct
    gwo_shape, gwk_shape = (N_DEV, AW // N_DEV, D), (N_DEV, D // N_DEV, 2 * LANES)
    return pl.pallas_call(
        body, name="a_bwd", grid=(nT,),
        in_specs=[row(D), row(LANES), row(LANES), row(D), _const_spec((1, D)), _const_spec(w_kv.shape),
                  _const_spec(wa_out.shape), _const_spec(ws.shape),
                  _const_spec((1, AW)), _const_spec((1, AW)), pl.BlockSpec((TM, AW), lambda i: (i, 0)),
                  pl.BlockSpec((TM, AW), lambda i: (i, 2)), row(AW), row(AW), row(LANES),
                  row(LANES), row(LANES), row(LANES)] + [hbm] * nr,
        out_specs=[row(3 * AW), _const_spec(gwo_shape), _const_spec(gwk_shape), row(D),
                   _acc_spec((1, D)), _acc_spec((1, 2 * LANES)), _acc_spec((1, AW)),
                   _acc_spec((1, AW)), _acc_spec(ws.shape), _acc_spec((CHUNK, LANES))] + [hbm] * nr,
        out_shape=(S((T, 3 * AW), BF), S(gwo_shape, BF), S(gwk_shape, BF), S((T, D), F32),
                   S((1, D), F32), S((1, 2 * LANES), F32), S((1, AW), F32), S((1, AW), F32),
                   S(ws.shape, F32), S((CHUNK, LANES), F32)) + tuple(S(r.shape, r.dtype) for r in ready),
        scratch_shapes=[pltpu.VMEM((TM, AW), BF), pltpu.VMEM((TM, AW), F32), pltpu.VMEM((AW, D), F32),
                        pltpu.VMEM((D, 2 * LANES), F32)] + _direct_sems(nr),
        compiler_params=_params(("arbitrary",)),
    )(dh1p, dk, dv, h1, g_kv, w_kv, wa_out, ws, ln_g, ln_b, z, z, sv, vhat, rstd, rc, rs1, rs2, *ready)


def _a_in_bwd(dz, wa_in, x, dh1, g_a, ready):
    T, D = x.shape
    SH = wa_in.shape[2]
    TM = min(512, T)
    nT = T // TM
    nr = len(ready)

    def body(dz_ref, wain_ref, x_ref, dh1_ref, ga_ref, *rest):
        ready_refs, (dx_ref, n1_ref, dga_ref), rest = rest[:nr], rest[nr:nr + 3], rest[nr + 3:]
        recv_refs, (ssem, rsem, lsem) = rest[:nr], rest[nr:]
        i = pl.program_id(0)
        exchanges = [_Direct(ready_refs[k], recv_refs[k], ssem.at[k], rsem.at[k], lsem.at[k], scatter=True)
                     for k in range(nr)]

        @pl.when(i == 0)
        def _():
            for e in exchanges:
                e.start()
            dga_ref[...] = jnp.zeros_like(dga_ref)

        xv = x_ref[...]
        r1 = lax.rsqrt(jnp.mean(xv * xv, axis=-1, keepdims=True) + EPS)
        xh = xv * r1
        ga = ga_ref[...]
        n1_ref[...] = (xh * ga).astype(BF).T
        dn1 = None
        for j in range(N_DEV):
            part = _dot_nt(dz_ref[:, j * SH:(j + 1) * SH], wain_ref[j])
            dn1 = part if dn1 is None else dn1 + part
        dga_ref[...] += jnp.sum(dn1 * xh, axis=0, keepdims=True)
        dx_ref[...] = dh1_ref[...] + _rms_bwd(dn1, xh, r1, ga)

        @pl.when(i == nT - 1)
        def _():
            for e in exchanges:
                e.finish()

    row = functools.partial(_row_spec, TM)
    hbm = pl.BlockSpec(memory_space=pl.ANY)
    S = jax.ShapeDtypeStruct
    return pl.pallas_call(
        body, name="a_in_bwd", grid=(nT,),
        in_specs=[row(dz.shape[1]), _const_spec(wa_in.shape), row(D), row(D), _const_spec((1, D))] + [hbm] * nr,
        out_specs=[row(D), _col_spec(TM, D), _acc_spec((1, D))] + [hbm] * nr,
        out_shape=(S((T, D), F32), S((D, T), BF), S((1, D), F32)) + tuple(S(r.shape, r.dtype) for r in ready),
        scratch_shapes=_direct_sems(nr),
        compiler_params=_params(("arbitrary",)),
    )(dz, wa_in, x, dh1, g_a, *ready)


def _wgrad(at, b, nblk, name, bt=512):
    K, T = at.shape
    N = b.shape[1] // nblk
    BT = min(bt, T)
    nt = T // BT

    def body(a_ref, b_ref, o_ref, acc):
        t = pl.program_id(0)

        @pl.when(t == 0)
        def _():
            acc[...] = jnp.zeros_like(acc)

        acc[...] += _dot(a_ref[...], b_ref[...])

        @pl.when(t == nt - 1)
        def _():
            for j in range(nblk):
                o_ref[j] = acc[:, j * N:(j + 1) * N].astype(BF)

    return pl.pallas_call(
        body, name=name, grid=(nt,),
        in_specs=[pl.BlockSpec((K, BT), lambda t: (0, t)), pl.BlockSpec((BT, nblk * N), lambda t: (t, 0))],
        out_specs=pl.BlockSpec((nblk, K, N), lambda t: (0, 0, 0)),
        out_shape=jax.ShapeDtypeStruct((nblk, K, N), BF),
        scratch_shapes=[pltpu.VMEM((K, nblk * N), F32)],
        compiler_params=_params(("arbitrary",)),
    )(at, b)


def _wgrad_exchange(a, b, me, small, name):
    K, T = a.shape
    N = b.shape[1] // N_DEV
    BT = min(1024, T)
    nt = T // BT
    last = N_DEV - 1
    n_chip = N_DEV // 2

    def far_of(k, core):
        return jnp.where((core == 0) & ((k == 1) | (k == 2)), k, n_chip - 1 - k)

    def block_of(s, me_i):
        k, odd = s // 2, s % 2
        core = me_i & 1
        return me_i ^ ((far_of(k, jnp.where(odd == 1, core, 1 - core)) << 1) | (1 - odd))

    def body(me_ref, a_ref, b_ref, small_ref, recv_ref, full_ref, *scratch):
        (acc, dstage, istage, half, d_s, d_r, i_s, i_r, lsem, parts_scr, red_scr, e_s, e_r, e_l, g_s, g_r,
         g_l) = scratch
        s, t = pl.program_id(0), pl.program_id(1)
        x, y, c = (lax.axis_index(ax) for ax in AXES)
        ex = [_Direct(small_ref, parts_scr, e_s, e_r, e_l, scatter=True)]
        regather = _TwoLevel(red_scr, full_ref, g_s, g_r, g_l)

        def to_sibling(k, slot):
            return pltpu.make_async_remote_copy(src_ref=dstage.at[slot], dst_ref=half.at[k], send_sem=d_s.at[k],
                                                recv_sem=d_r.at[k], device_id=(x, y, 1 - c), device_id_type=MESH)

        def to_chip(k, slot, sender):
            far = far_of(k, c)
            px, py = x ^ ((far >> 1) & 1), y ^ (far & 1)
            dst = recv_ref.at[2 * x + y] if sender else recv_ref.at[2 * px + py]
            return pltpu.make_async_remote_copy(src_ref=istage.at[slot], dst_ref=dst, send_sem=i_s.at[k],
                                                recv_sem=i_r.at[k], device_id=(px, py, c), device_id_type=MESH)

        @pl.when((s == 0) & (t == 0))
        def _():
            for e in ex:
                e.start()

        @pl.when(t == 0)
        def _():
            acc[...] = jnp.zeros_like(acc)

        acc[...] += _dot(a_ref[...], b_ref[...])

        @pl.when(t == nt - 1)
        def _():
            k = lax.div(s, 2)
            slot = lax.rem(k, 2)

            @pl.when(lax.rem(s, 2) == 0)
            def _():
                @pl.when(k >= 2)
                def _():
                    to_sibling(k - 2, slot).wait_send()

                dstage[slot] = acc[...].astype(BF)
                to_sibling(k, slot).start()

            @pl.when(lax.rem(s, 2) == 1)
            def _():
                to_sibling(k, slot).wait_recv()

                @pl.when(k >= 2)
                def _():
                    to_chip(k - 2, slot, True).wait_send()

                istage[slot] = (acc[...] + half[k].astype(F32)).astype(BF)

                @pl.when(k < n_chip - 1)
                def _():
                    to_chip(k, slot, True).start()

            @pl.when(s == last)
            def _():
                own = pltpu.make_async_copy(istage.at[slot], recv_ref.at[2 * x + y], lsem)
                own.start()
                to_chip(n_chip - 2, 0, True).wait_send()
                to_sibling(n_chip - 2, 0).wait_send()
                to_sibling(n_chip - 1, 1).wait_send()
                for kk in range(n_chip - 1):
                    to_chip(kk, 0, False).wait_recv()
                own.wait()
                for e in ex:
                    e.finish()
                total = parts_scr[0]
                for dev in range(1, N_DEV):
                    total = total + parts_scr[dev]
                red_scr[...] = total
                regather.start()
                regather.forward()
                regather.finish()

    hbm = pl.BlockSpec(memory_space=pl.ANY)
    dma = pltpu.SemaphoreType.DMA
    grid_spec = pltpu.PrefetchScalarGridSpec(
        num_scalar_prefetch=1, grid=(N_DEV, nt),
        in_specs=[pl.BlockSpec((K, BT), lambda s, t, me_ref: (0, t)),
                  pl.BlockSpec((BT, N), lambda s, t, me_ref: (t, block_of(s, me_ref[0]))), hbm],
        out_specs=[hbm, hbm],
        scratch_shapes=[pltpu.VMEM((K, N), F32), pltpu.VMEM((2, K, N), BF), pltpu.VMEM((2, K, N), BF),
                        pltpu.VMEM((n_chip, K, N), BF), dma((n_chip,)), dma((n_chip,)), dma((n_chip - 1,)),
                        dma((n_chip - 1,)), dma, pltpu.VMEM(small.shape, F32), pltpu.VMEM(small.shape[1:], F32),
                        dma((last,)), dma((last,)), dma, dma((last,)), dma((last,)), dma])
    return pl.pallas_call(
        body, name=name, grid_spec=grid_spec,
        out_shape=[jax.ShapeDtypeStruct((n_chip, K, N), BF), jax.ShapeDtypeStruct(small.shape, F32)],
        compiler_params=_params(("arbitrary", "arbitrary")),
    )(me, a, b, small)


def _my_index():
    return 4 * lax.axis_index("x") + 2 * lax.axis_index("y") + lax.axis_index("c")


def _peer(mask):
    x, y, c = (lax.axis_index(a) for a in AXES)
    return (x ^ ((mask >> 2) & 1), y ^ ((mask >> 1) & 1), c ^ (mask & 1))


def _dev_index(p):
    return 4 * p[0] + 2 * p[1] + p[2]


class _Direct:
    def __init__(self, src, dst, send_sems, recv_sems, local_sem, scatter):
        me = _my_index()
        self.own = pltpu.make_async_copy(src.at[me] if scatter else src, dst.at[me], local_sem)
        self.sends, self.recvs = [], []
        for k in range(1, N_DEV):
            p = _peer(k)
            pi = _dev_index(p)
            sems = dict(send_sem=send_sems.at[k - 1], recv_sem=recv_sems.at[k - 1], device_id=p, device_id_type=MESH)
            self.sends.append(pltpu.make_async_remote_copy(src_ref=src.at[pi] if scatter else src, dst_ref=dst.at[me],
                                                           **sems))
            self.recvs.append(pltpu.make_async_remote_copy(src_ref=src.at[me] if scatter else src, dst_ref=dst.at[pi],
                                                           **sems))

    def start(self):
        self.own.start()
        for cp in self.sends:
            cp.start()

    def finish(self):
        for cp in self.sends:
            cp.wait_send()
        for cp in self.recvs:
            cp.wait_recv()
        self.own.wait()


class _TwoLevel:
    def __init__(self, src, dst, send_sems, recv_sems, local_sem, own=True):
        x, y, c = (lax.axis_index(a) for a in AXES)
        self.me, self.sibling = (x, y, c), (x, y, 1 - c)
        self.chips = [(1 - x, y), (x, 1 - y), (1 - x, 1 - y)]
        self.src, self.dst, self.send_sems, self.recv_sems = src, dst, send_sems, recv_sems
        self.own = pltpu.make_async_copy(src, dst.at[_dev_index(self.me)], local_sem) if own else None

    def _copy(self, k, block, to, from_src=False):
        slot = self.dst.at[_dev_index(block)]
        return pltpu.make_async_remote_copy(src_ref=self.src if from_src else slot, dst_ref=slot,
                                            send_sem=self.send_sems.at[k], recv_sem=self.recv_sems.at[k],
                                            device_id=to, device_id_type=MESH)

    def _firsts(self):
        c = self.me[2]
        return [self._copy(0, self.me, self.sibling, True)] + [self._copy(1 + j, self.me, (*chip, c), True)
                                                               for j, chip in enumerate(self.chips)]

    def _passed(self):
        c = self.me[2]
        return [self._copy(4 + j, (*chip, c), self.sibling) for j, chip in enumerate(self.chips)]

    def start(self):
        if self.own is not None:
            self.own.start()
        for cp in self._firsts():
            cp.start()

    def wait_sibling(self):
        self._copy(0, self.sibling, self.me).wait_recv()

    def wait_chip_and_forward(self, j):
        self._copy(1 + j, (*self.chips[j], self.me[2]), self.me).wait_recv()
        self._passed()[j].start()

    def wait_passed(self, j):
        self._copy(4 + j, (*self.chips[j], 1 - self.me[2]), self.me).wait_recv()

    def wait_sends(self):
        for cp in self._firsts() + self._passed():
            cp.wait_send()
        if self.own is not None:
            self.own.wait()

    def forward(self):
        for j in range(3):
            self.wait_chip_and_forward(j)

    def finish(self):
        self.wait_sibling()
        for j in range(3):
            self.wait_passed(j)
        self.wait_sends()


class _RelayGather:
    def __init__(self, dst, send_sems, recv_sems):
        x, y, c = (lax.axis_index(a) for a in AXES)
        self.c = c
        self.sib, self.xn, self.yn, self.dg = (x, y, 1 - c), (1 - x, y, c), (x, 1 - y, c), (1 - x, 1 - y, c)
        self.me = (x, y, c)
        self.dst, self.send_sems, self.recv_sems = dst, send_sems, recv_sems
        self.half = dst.shape[1] // 2

    def _slot(self, dev, part=None):
        i = _dev_index(dev)
        if part is None:
            return self.dst.at[i]
        return self.dst.at[i, pl.ds(part * self.half, self.half)]

    def _copy(self, k, dev, to, part=None):
        ref = self._slot(dev, part)
        return pltpu.make_async_remote_copy(src_ref=ref, dst_ref=ref, send_sem=self.send_sems.at[k],
                                            recv_sem=self.recv_sems.at[k], device_id=to, device_id_type=MESH)

    def _other(self, dev):
        return (dev[0], dev[1], 1 - self.c)

    def start(self):
        for k, to in enumerate((self.sib, self.xn, self.yn)):
            self._copy(k, self.me, to).start()

    def send_own(self, k):
        return self._copy(k, self.me, (self.sib, self.xn, self.yn)[k])

    def wait_sibling(self):
        self._copy(0, self.sib, self.me).wait_recv()

    def on_x(self):
        self._copy(1, self.xn, self.me).wait_recv()
        self._copy(3, self.xn, self.yn, part=0).start()
        self._copy(5, self.xn, self.sib).start()

    def on_y(self):
        self._copy(2, self.yn, self.me).wait_recv()
        self._copy(4, self.yn, self.xn, part=1).start()
        self._copy(6, self.yn, self.sib).start()

    def on_diag(self):
        self._copy(3, self.dg, self.me, part=0).wait_recv()
        self._copy(4, self.dg, self.me, part=1).wait_recv()
        self._copy(7, self.dg, self.sib).start()

    def wait_passed(self, j):
        self._copy(5 + j, self._other((self.xn, self.yn, self.dg)[j]), self.me).wait_recv()

    def wait_sends(self):
        for k, to in enumerate((self.sib, self.xn, self.yn)):
            self._copy(k, self.me, to).wait_send()
        self._copy(3, self.xn, self.yn, part=0).wait_send()
        self._copy(4, self.yn, self.xn, part=1).wait_send()
        for j, dev in enumerate((self.xn, self.yn, self.dg)):
            self._copy(5 + j, dev, self.sib).wait_send()


def _direct_sems(n):
    if n == 0:
        return []
    return [pltpu.SemaphoreType.DMA((n, 7)), pltpu.SemaphoreType.DMA((n, 7)), pltpu.SemaphoreType.DMA((n,))]


def _adam_math(w, g, m, v):
    m = ADAM_B1 * m + (1.0 - ADAM_B1) * g
    v = ADAM_B2 * v + (1.0 - ADAM_B2) * (g * g)
    m_hat = m / (1.0 - ADAM_B1 ** ADAM_STEP)
    v_hat = v / (1.0 - ADAM_B2 ** ADAM_STEP)
    delta = -ADAM_LR * (m_hat / (jnp.sqrt(v_hat) + ADAM_EPS) + ADAM_WD * w)
    return delta, m, v


def _sum_adam(parts, w, m, v, name):
    R, C = w.shape
    NP = parts.shape[0]
    BR = CHUNK if R % CHUNK == 0 else R

    def body(p_ref, w_ref, m_ref, v_ref, g_ref, d_ref, nm_ref, nv_ref):
        g = p_ref[0].astype(F32)
        for i in range(1, NP):
            g = g + p_ref[i].astype(F32)
        g_ref[...] = g
        d_ref[...], nm_ref[...], nv_ref[...] = _adam_math(w_ref[...], g, m_ref[...], v_ref[...])

    blk = pl.BlockSpec((BR, C), lambda i: (i, 0))
    S = jax.ShapeDtypeStruct((R, C), F32)
    return pl.pallas_call(
        body, name=name, grid=(R // BR,),
        in_specs=[pl.BlockSpec((NP, BR, C), lambda i: (0, i, 0)), blk, blk, blk],
        out_specs=[blk] * 4, out_shape=(S,) * 4,
        compiler_params=_params(("arbitrary",)),
    )(parts, w, m, v)


SUBLANES = 8


def _nrows(size):
    return -(-size // (SUBLANES * LANES)) * SUBLANES


def _view2d(a):
    return a.reshape(-1, LANES) if a.size % LANES == 0 else a.reshape(1, -1)


def _pack_small(parts, total_rows, name):
    arrs = [p[0] for p in parts]

    def body(*refs):
        out = refs[-1]
        out[...] = jnp.zeros_like(out)
        at = 0
        for ref, (a, rows, flag) in zip(refs[:-1], parts):
            val = ref[...].T if flag == "T" else ref[...]
            r, c = (rows, val.shape[1]) if flag == "T" else val.shape
            out[at:at + r, 0:c] = val[:r]
            at += _nrows(r * c)

    return pl.pallas_call(body, name=name, out_shape=jax.ShapeDtypeStruct((total_rows, LANES), F32))(*arrs)


def _small_update(full, me, reps, shards, name):
    n = len(reps) + len(shards)

    def body(me_ref, full_ref, *refs):
        ins, outs = refs[:3 * n], refs[3 * n:]
        at = 0
        for k in range(n):
            w_ref, m_ref, v_ref = ins[3 * k:3 * k + 3]
            r, c = w_ref.shape
            if k < len(reps):
                g = full_ref[at:at + r, 0:c]
                at += _nrows(r * c)
            else:
                seg = full_ref[at:at + N_DEV * r, :]
                row = lax.broadcasted_iota(jnp.int32, seg.shape, 0)
                pick = [jnp.sum(jnp.where(row == r * me_ref[0] + t, seg, 0.0), axis=0, keepdims=True) for t in range(r)]
                g = pick[0] if r == 1 else jnp.concatenate(pick, axis=0)
                at += N_DEV * r
            g_ref, d_ref, nm_ref, nv_ref = outs[4 * k:4 * k + 4]
            g_ref[...] = g
            d_ref[...], nm_ref[...], nv_ref[...] = _adam_math(w_ref[...], g, m_ref[...], v_ref[...])
        outs[4 * n][...] = full_ref[at:at + 1, 0:1]

    flat = [t for p in reps + shards for t in p]
    S = jax.ShapeDtypeStruct
    res = pl.pallas_call(
        body, name=name,
        in_specs=[pl.BlockSpec(memory_space=pltpu.SMEM)] + [pl.BlockSpec(memory_space=pltpu.VMEM)] * (1 + len(flat)),
        out_shape=[S(p[0].shape, F32) for p in reps + shards for _ in range(4)] + [S((1, 1), F32)],
    )(me, full, *flat)
    return [tuple(res[4 * k:4 * k + 4]) for k in range(n)], res[4 * n]


def _rope_tables(T):
    pos = np.arange(T, dtype=np.float32)
    inv_freq = (np.float64(ROPE_THETA) ** (-np.arange(0, HEAD_DIM, 2, dtype=np.float64) / HEAD_DIM)).astype(np.float32)
    ang = (pos[:, None] * inv_freq[None, :]).astype(np.float64)
    cos, sin, zero = np.cos(ang).astype(np.float32), np.sin(ang).astype(np.float32), np.zeros(ang.shape, np.float32)
    c = np.concatenate([cos, cos, cos, cos], axis=1)
    s1 = np.concatenate([-sin, zero, -sin, zero], axis=1)
    s2 = np.concatenate([zero, sin, zero, sin], axis=1)
    return jnp.asarray(c), jnp.asarray(s1), jnp.asarray(s2)


def kernel(x, a_norm_g, a_w_in, a_ln_g, a_ln_b, a_ws, a_bs, a_w_out, kv_norm_g, w_kv, b_kv, b_norm_g, b_w_in, b_bq, b_sinks, b_w_out, final_norm_g, loss_target, m_a_norm_g, m_a_w_in, m_a_ln_g, m_a_ln_b, m_a_ws, m_a_bs, m_a_w_out, m_kv_norm_g, m_w_kv, m_b_kv, m_b_norm_g, m_b_w_in, m_b_bq, m_b_sinks, m_b_w_out, m_final_norm_g, v_a_norm_g, v_a_w_in, v_a_ln_g, v_a_ln_b, v_a_ws, v_a_bs, v_a_w_out, v_kv_norm_g, v_w_kv, v_b_kv, v_b_norm_g, v_b_w_in, v_b_bq, v_b_sinks, v_b_w_out, v_final_norm_g):
    T, D = x.shape[1], x.shape[2]
    AW = a_ln_g.shape[1] * N_DEV
    G = a_ws.shape[1]
    assert w_kv.shape[1] == 2 * LANES and a_ws.shape[2] == CHUNK and T % CHUNK == 0
    me = _my_index()

    xs, tgt = x[0], loss_target[0]
    vec = jnp.concatenate([a_norm_g, a_ln_g, a_ln_b], axis=1)
    vec = jnp.broadcast_to(vec, (SUBLANES, vec.shape[1]))
    north = lax.axis_index("c") == 1
    slots = me ^ jnp.where(north, jnp.array(PASS_MASKS[1], jnp.int32), jnp.array(PASS_MASKS[0], jnp.int32))
    z, wa_in, vecs, wa_out, wkv = _in_proj(xs, a_w_in[0], vec, slots, [a_w_out[0], w_kv])
    wa_out = wa_out.reshape(AW, D)
    wkv = wkv.reshape(D, 2 * LANES)
    vecs = vecs[:, 0, :]
    ds = D // N_DEV
    g_a = vecs[:, :ds].reshape(1, D)
    ln_g = vecs[:, ds:ds + AW // N_DEV].reshape(1, AW)
    ln_b = vecs[:, ds + AW // N_DEV:].reshape(1, AW)

    rc, rs1, rs2 = _rope_tables(T)
    ws = a_ws[0]
    bs_t = a_bs[0].T
    g_kv = kv_norm_g.reshape(1, D)
    bkv = b_kv.reshape(1, -1)
    g_f = final_norm_g.reshape(1, D)
    sinks = jnp.repeat(b_sinks.reshape(2, 4, 2).transpose(0, 2, 1).reshape(4, 4), CHUNK, axis=1)
    h1, sv, vhat, rstd, k4, v4, kt, vt, wb_in, wb_out = _a_fwd(
        xs, z, ln_g, ln_b, ws, bs_t, wa_out, g_kv, wkv, bkv, rc, rs1, rs2, [b_w_in[0], b_w_out[0]])
    wb_out = wb_out.reshape(-1, D)
    q, g2, o, dh2, dh2_b, loss, d_gf = _b_fwd(h1, b_norm_g, wb_in, b_bq, rc, rs1, rs2, k4, vt, sinks, wb_out, g_f, tgt)
    dh1p, dz2, n2, y2, dk, dv, d_bq, d_gb, d_sink = _b_bwd(dh2, h1, q, g2, o, k4, v4, kt, sinks, wb_out, wb_in,
                                                           b_norm_g, rc, rs1, rs2)
    d_sink = d_sink[:, :4].reshape(2, 2, 4).transpose(0, 2, 1).reshape(1, 16)
    gw_b_in = _wgrad(n2, dz2, N_DEV, "wgrad_b_in", bt=1024)
    gw_b_out = _wgrad(y2, dh2_b, 1, "wgrad_b_out", bt=1024).reshape(N_DEV, -1, D)
    (dz, gw_a_out, gw_kv, dh1_f, d_gkv, d_bkv, d_lng, d_lnb, d_ws, d_bst, r_b_in, r_b_out) = _a_bwd(
        dh1p, dk, dv, h1, g_kv, wkv, wa_out, ws, ln_g, ln_b, z, sv, vhat, rstd, rc, rs1, rs2, [gw_b_in, gw_b_out])
    dx, n1, d_ga, r_a_out, r_kv = _a_in_bwd(dz, wa_in, xs, dh1_f, g_a, [gw_a_out, gw_kv])
    small = [(_view2d(d_ws), None, None), (d_bst, G, "T")] + [(_view2d(a), None, None) for a in (
        d_gkv, d_bkv, d_gb, d_bq, d_sink, d_gf, d_ga, d_lng, d_lnb, loss)]
    used = sum(_nrows(G * CHUNK if flag else a.size) for a, _, flag in small)
    per = -(-used // (SUBLANES * N_DEV)) * SUBLANES
    small_pack = _pack_small(small, per * N_DEV, "pack_small").reshape(N_DEV, per, LANES)
    r_a_in, full_small = _wgrad_exchange(n1, dz, me.reshape(1), small_pack, "wgrad_a_in")

    g_a_in, d_a_in, nm_a_in, nv_a_in = _sum_adam(r_a_in, a_w_in[0], m_a_w_in[0], v_a_w_in[0], "adam_a_in")
    g_a_out, d_a_out, nm_a_out, nv_a_out = _sum_adam(r_a_out, a_w_out[0], m_a_w_out[0], v_a_w_out[0], "adam_a_out")
    g_kvw, d_kvw, nm_kvw, nv_kvw = _sum_adam(r_kv, w_kv, m_w_kv, v_w_kv, "adam_kv")
    g_b_in, d_b_in, nm_b_in, nv_b_in = _sum_adam(r_b_in, b_w_in[0], m_b_w_in[0], v_b_w_in[0], "adam_b_in")
    g_b_out, d_b_out, nm_b_out, nv_b_out = _sum_adam(r_b_out, b_w_out[0], m_b_w_out[0], v_b_w_out[0], "adam_b_out")

    full_small = full_small.reshape(N_DEV * per, LANES)
    reps = [(a_ws, m_a_ws, v_a_ws), (a_bs, m_a_bs, v_a_bs), (kv_norm_g, m_kv_norm_g, v_kv_norm_g),
            (b_kv, m_b_kv, v_b_kv), (b_norm_g, m_b_norm_g, v_b_norm_g), (b_bq, m_b_bq, v_b_bq),
            (b_sinks, m_b_sinks, v_b_sinks), (final_norm_g, m_final_norm_g, v_final_norm_g)]
    shards = [(a_norm_g, m_a_norm_g, v_a_norm_g), (a_ln_g, m_a_ln_g, v_a_ln_g), (a_ln_b, m_a_ln_b, v_a_ln_b)]
    upd, loss = _small_update(full_small, me.reshape(1), [tuple(_view2d(t) for t in p) for p in reps],
                              [tuple(_view2d(t) for t in p) for p in shards], "adam_small")
    loss = loss[0, 0]
    sm_g, sd, snm, snv = ([upd[k][j].reshape(p[0].shape) for k, p in enumerate(reps + shards)] for j in range(4))

    def order(big, sm):
        a_in, a_out, kvw, b_in, b_out = big
        ws_, bs_, kvg, bkv_, bng, bq_, snk, fng, ang, alng, alnb = sm
        return (ang, a_in[None], alng, alnb, ws_, bs_, a_out[None], kvg, kvw, bkv_, bng, b_in[None], bq_, snk,
                b_out[None], fng)

    grads = order((g_a_in, g_a_out, g_kvw, g_b_in, g_b_out), sm_g)
    deltas = order((d_a_in, d_a_out, d_kvw, d_b_in, d_b_out), sd)
    new_m = order((nm_a_in, nm_a_out, nm_kvw, nm_b_in, nm_b_out), snm)
    new_v = order((nv_a_in, nv_a_out, nv_kvw, nv_b_in, nv_b_out), snv)
    return (loss, dx[None], *grads, *deltas, *new_m, *new_v)
```

```python
import functools

import jax
import jax.numpy as jnp
import numpy as np
from jax import lax
from jax.experimental import pallas as pl
from jax.experimental.pallas import tpu as pltpu

CHUNK = 128
HEAD_DIM = 64
ROPE_THETA = 10000.0
EPS = 1e-5
ADAM_LR = 0.001
ADAM_B1 = 0.9
ADAM_B2 = 0.999
ADAM_EPS = 1e-08
ADAM_WD = 0.01
ADAM_STEP = 10
N_DEV = 8
LANES = 128
NEG = -1e30

BF = jnp.bfloat16
F32 = jnp.float32
MESH = pl.DeviceIdType.MESH
AXES = ("x", "y", "c")
VMEM_LIMIT = 56 * 1024 * 1024


def _dot(a, b):
    return jnp.dot(a, b, preferred_element_type=F32)


def _dot_nt(a, b):
    return lax.dot_general(a, b, (((1,), (1,)), ((), ())), preferred_element_type=F32)


def _dot_tn(a, b):
    return lax.dot_general(a, b, (((0,), (0,)), ((), ())), preferred_element_type=F32)


def _const_spec(shape):
    nd = len(shape)
    return pl.BlockSpec(shape, lambda *_: (0,) * nd, pipeline_mode=pl.Buffered(1))


def _acc_spec(shape):
    nd = len(shape)
    return pl.BlockSpec(shape, lambda *_: (0,) * nd)


def _row_spec(tm, width):
    return pl.BlockSpec((tm, width), lambda i: (i, 0))


def _col_spec(tm, height):
    return pl.BlockSpec((height, tm), lambda i: (0, i))


def _params(sem):
    return pltpu.CompilerParams(dimension_semantics=sem, vmem_limit_bytes=VMEM_LIMIT)


def _rot(x, c, s1, s2):
    return x * c + pltpu.roll(x, 96, 1) * s1 + pltpu.roll(x, 32, 1) * s2


def _rot_bwd(d, c, s1, s2):
    return d * c + pltpu.roll(d * s1, 32, 1) + pltpu.roll(d * s2, 96, 1)


def _silu_parts(g):
    sg = jax.nn.sigmoid(g)
    return g * sg, sg * (1.0 + g * (1.0 - sg))


def _rms_bwd(dn, xh, r, g):
    a = dn * g
    return r * (a - xh * jnp.mean(a * xh, axis=-1, keepdims=True))


def _lane_lo(shape):
    return lax.broadcasted_iota(jnp.int32, shape, 1) < HEAD_DIM


def _split4(t):
    lo = _lane_lo(t.shape)
    tr = pltpu.roll(t, HEAD_DIM, 1)
    z = jnp.zeros_like(t)
    return jnp.concatenate([jnp.where(lo, t, z), jnp.where(lo, z, tr), jnp.where(lo, tr, z), jnp.where(lo, z, t)], axis=1)


def _stack_pairs(t, h):
    return jnp.concatenate([t[:, (h * 4 + j) * LANES:(h * 4 + j + 1) * LANES] for j in range(4)], axis=0)


def _upper():
    shape = (CHUNK, 4 * CHUNK)
    return lax.broadcasted_iota(jnp.int32, shape, 0) > (lax.broadcasted_iota(jnp.int32, shape, 1) & (CHUNK - 1))


def _band_rows(ref, prev, cur, h):
    a = slice(2 * h * LANES, (2 * h + 1) * LANES)
    b = slice((2 * h + 1) * LANES, (2 * h + 2) * LANES)
    return jnp.concatenate([ref[pl.ds(prev, CHUNK), a], ref[pl.ds(cur, CHUNK), a],
                            ref[pl.ds(prev, CHUNK), b], ref[pl.ds(cur, CHUNK), b]], axis=0)


def _band_cols(ref, pci, ci, h):
    a = slice(2 * h * LANES, (2 * h + 1) * LANES)
    b = slice((2 * h + 1) * LANES, (2 * h + 2) * LANES)
    return jnp.concatenate([ref[pci, a, :], ref[ci, a, :], ref[pci, b, :], ref[ci, b, :]], axis=1)


def _fold(t, upper, has_prev=None):
    out = []
    for k in range(2):
        prev = t[2 * k * CHUNK:(2 * k + 1) * CHUNK]
        if has_prev is not None:
            prev = jnp.where(has_prev, prev, NEG)
        out.append(jnp.where(upper, prev, t[(2 * k + 1) * CHUNK:(2 * k + 2) * CHUNK]))
    return out


def _unfold(fa, fb, upper):
    z = jnp.zeros_like(fa)
    return jnp.concatenate([jnp.where(upper, fa, z), jnp.where(upper, z, fa),
                            jnp.where(upper, fb, z), jnp.where(upper, z, fb)], axis=0)


def _softmax_sink(f, sink):
    m = jnp.maximum(jnp.max(f, axis=0, keepdims=True), sink)
    p = jnp.exp(f - m)
    es = jnp.exp(sink - m)
    inv = 1.0 / (jnp.sum(p, axis=0, keepdims=True) + es)
    return p * inv, es * inv


class _Riding:
    def __init__(self, shards, gathered, stages, sems, n_steps):
        self.shards, self.stages, self.n_steps = shards, stages, n_steps
        ssem, rsem, lsem = sems
        self.gathers = [_TwoLevel(stages[k], gathered[k], ssem.at[k], rsem.at[k], lsem.at[k])
                        for k in range(len(shards))]

    def begin(self, i):
        @pl.when(i == 0)
        def _():
            for shard, stage, g in zip(self.shards, self.stages, self.gathers):
                stage[...] = shard[...].astype(stage.dtype)
                g.start()

    def end(self, i):
        @pl.when(i == self.n_steps // 2)
        def _():
            for g in self.gathers:
                g.forward()

        @pl.when(i == self.n_steps - 1)
        def _():
            for g in self.gathers:
                g.finish()

    @staticmethod
    def specs(later):
        nl = len(later)
        hbm = pl.BlockSpec(memory_space=pl.ANY)
        return ([_const_spec(w.shape) for w in later], [hbm] * nl,
                tuple(jax.ShapeDtypeStruct((N_DEV,) + w.shape, BF) for w in later),
                [pltpu.VMEM(w.shape, BF) for w in later] + _direct_sems(nl))


PASS_MASKS = ((0, 1, 2, 5, 4, 3, 6, 7), (0, 1, 4, 3, 2, 5, 6, 7))


def _in_proj(x, w_shard, vec_shard, slots, later):
    T, D = x.shape
    SH = w_shard.shape[1]
    TM = min(1024, T)
    nT = T // TM
    nl = len(later)
    ds = D // N_DEV
    last = N_DEV - 1

    def body(slots_ref, x_ref, wsh_ref, vsh_ref, *rest):
        shards, rest = rest[:nl], rest[nl:]
        (z_ref, wout_ref, vout_ref), rest = rest[:3], rest[3:]
        gathered, rest = rest[:nl], rest[nl:]
        (w_scr, vec_scr, vstage, n1_scr, ga_scr, w_s, w_r, w_l, v_s, v_r, v_l), rest = rest[:11], rest[11:]
        stages, sems = rest[:nl], rest[nl:]
        p, i = pl.program_id(0), pl.program_id(1)
        me = _my_index()
        wg = _RelayGather(w_scr, w_s, w_r)
        vg = _Direct(vstage, vec_scr, v_s, v_r, v_l, scatter=False)
        lg = [_TwoLevel(stages[k], gathered[k], sems[0].at[k], sems[1].at[k], sems[2].at[k]) for k in range(nl)]
        w_copy = pltpu.make_async_copy(w_scr, wout_ref, w_l)

        def at_pass(k):
            return (p == k) & (i == 0)

        c = lax.axis_index("c")

        @pl.when(at_pass(0))
        def _():
            vstage[...] = vsh_ref[...]
            vg.start()
            w_scr[me] = wsh_ref[...].astype(BF)
            wg.send_own(0).start()

            @pl.when(c == 1)
            def _():
                wg.send_own(1).start()

            @pl.when(c == 0)
            def _():
                wg.send_own(2).start()

            vg.finish()
            for j in range(N_DEV):
                ga_scr[:, j * ds:(j + 1) * ds] = vec_scr[j, 0:1, 0:ds]
            vout_ref[...] = vec_scr[...]

        @pl.when(at_pass(1))
        def _():
            wg.wait_sibling()

        for first, second, landed_first, landed_second in ((1, 2, wg.on_x, wg.on_y), (2, 1, wg.on_y, wg.on_x)):
            mine = c == (1 if first == 1 else 0)

            @pl.when(at_pass(2) & mine)
            def _(second=second, landed_first=landed_first):
                wg.send_own(second).start()
                landed_first()

            @pl.when(at_pass(3) & mine)
            def _(second=second):
                wg.wait_passed(second - 1)

            @pl.when(at_pass(4) & mine)
            def _(landed_second=landed_second):
                landed_second()

            @pl.when(at_pass(5) & mine)
            def _(first=first):
                wg.wait_passed(first - 1)

        @pl.when(at_pass(4))
        def _():
            for k in range(nl):
                stages[k][...] = shards[k][...].astype(BF)
                lg[k].start()

        @pl.when(at_pass(6))
        def _():
            wg.on_diag()

        @pl.when(at_pass(7))
        def _():
            wg.wait_passed(2)

        @pl.when(at_pass(last))
        def _():
            w_copy.start()

        @pl.when(p == 0)
        def _():
            xv = x_ref[...]
            r1 = lax.rsqrt(jnp.mean(xv * xv, axis=-1, keepdims=True) + EPS)
            n1_scr[i] = (xv * r1 * ga_scr[...]).astype(BF)

        z_ref[...] = _dot(n1_scr[i], w_scr[slots_ref[p]]).astype(BF)

        @pl.when((p == last) & (i == nT - 1))
        def _():
            wg.wait_sends()
            for g in lg:
                g.forward()
            for g in lg:
                g.finish()
            w_copy.wait()

    hbm = pl.BlockSpec(memory_space=pl.ANY)
    dma = pltpu.SemaphoreType.DMA
    S = jax.ShapeDtypeStruct
    grid_spec = pltpu.PrefetchScalarGridSpec(
        num_scalar_prefetch=1, grid=(N_DEV, nT),
        in_specs=[pl.BlockSpec((TM, D), lambda p, i, s: (jnp.where(p == 0, i, nT - 1), 0)),
                  pl.BlockSpec(w_shard.shape, lambda p, i, s: (0, 0), pipeline_mode=pl.Buffered(1)),
                  pl.BlockSpec(vec_shard.shape, lambda p, i, s: (0, 0), pipeline_mode=pl.Buffered(1))]
        + [pl.BlockSpec(w.shape, lambda p, i, s: (0, 0), pipeline_mode=pl.Buffered(1)) for w in later],
        out_specs=[pl.BlockSpec((TM, SH), lambda p, i, s: (i, s[p])), hbm,
                   pl.BlockSpec((N_DEV,) + vec_shard.shape, lambda p, i, s: (0, 0, 0))] + [hbm] * nl,
        scratch_shapes=[pltpu.VMEM((N_DEV, D, SH), BF), pltpu.VMEM((N_DEV,) + vec_shard.shape, F32),
                        pltpu.VMEM(vec_shard.shape, F32), pltpu.VMEM((nT, TM, D), BF), pltpu.VMEM((1, D), F32),
                        dma((8,)), dma((8,)), dma, dma((7,)), dma((7,)), dma]
        + [pltpu.VMEM(w.shape, BF) for w in later] + _direct_sems(nl))
    return pl.pallas_call(
        body, name="a_in_proj", grid_spec=grid_spec,
        out_shape=(S((T, N_DEV * SH), BF), S((N_DEV, D, SH), BF), S((N_DEV,) + vec_shard.shape, F32))
        + tuple(S((N_DEV,) + w.shape, BF) for w in later),
        compiler_params=_params(("arbitrary", "arbitrary")),
    )(slots, x, w_shard, vec_shard, *later)


def _a_fwd(x, z, ln_g, ln_b, ws, bs_t, wa_out, g_kv, w_kv, b_kv, rc, rs1, rs2, later):
    T, D = x.shape
    AW = wa_out.shape[0]
    G = ws.shape[0]
    TM = min(256, T)
    nT = T // TM
    nC = TM // CHUNK
    nl = len(later)

    def body(x_ref, u_ref, v_ref, gt_ref, lng_ref, lnb_ref, ws_ref, bst_ref, waout_ref, gkv_ref, wkv_ref, bkv_ref,
             rc_ref, rs1_ref, rs2_ref, *rest):
        shards, rest = rest[:nl], rest[nl:]
        (h1_ref, sv_ref, vhat_ref, rstd_ref, k4_ref, v4_ref, kt_ref, vt_ref), rest = rest[:8], rest[8:]
        gathered, sv_scr, stages, sems = rest[:nl], rest[nl], rest[nl + 1:2 * nl + 1], rest[2 * nl + 1:]
        i = pl.program_id(0)
        riding = _Riding(shards, gathered, stages, sems, nT)
        riding.begin(i)
        xv = x_ref[...]
        u = u_ref[...].astype(F32)
        v = v_ref[...].astype(F32)
        gt = gt_ref[...].astype(F32)
        mu = jnp.mean(v, axis=-1, keepdims=True)
        xc = v - mu
        rstd = lax.rsqrt(jnp.mean(xc * xc, axis=-1, keepdims=True) + EPS)
        vhat = xc * rstd
        vln = (vhat * lng_ref[...] + lnb_ref[...]).astype(BF)
        tri = lax.broadcasted_iota(jnp.int32, (CHUNK, CHUNK), 0) >= lax.broadcasted_iota(jnp.int32, (CHUNK, CHUNK), 1)
        for g in range(G):
            wsm = jnp.where(tri, ws_ref[g], 0.0).astype(BF)
            bias = bst_ref[:, g:g + 1]
            for c in range(nC):
                blk = vln[c * CHUNK:(c + 1) * CHUNK, g * CHUNK:(g + 1) * CHUNK]
                sv_scr[c * CHUNK:(c + 1) * CHUNK, g * CHUNK:(g + 1) * CHUNK] = _dot(wsm, blk) + bias
        sv = sv_scr[...]
        silu, _ = _silu_parts(gt)
        y = (u * sv * silu).astype(BF)
        h1 = xv + _dot(y, waout_ref[...])
        h1_ref[...] = h1
        sv_ref[...] = sv.astype(BF)
        vhat_ref[...] = vhat.astype(BF)
        rstd_ref[...] = jnp.broadcast_to(rstd, rstd_ref.shape)
        rkv = lax.rsqrt(jnp.mean(h1 * h1, axis=-1, keepdims=True) + EPS)
        nkv = (h1 * rkv * gkv_ref[...]).astype(BF)
        kv = _dot(nkv, wkv_ref[...]) + bkv_ref[...]
        k_rot = _rot(kv[:, :LANES], rc_ref[...], rs1_ref[...], rs2_ref[...])
        for src, ref, tref in ((k_rot, k4_ref, kt_ref), (kv[:, LANES:], v4_ref, vt_ref)):
            t4 = _split4(src)
            ref[...] = t4.astype(BF)
            for c in range(nC):
                for b in range(4):
                    blk = t4[c * CHUNK:(c + 1) * CHUNK, b * LANES:(b + 1) * LANES]
                    tref[c, b * LANES:(b + 1) * LANES, :] = blk.T.astype(BF)
        riding.end(i)

    row = functools.partial(_row_spec, TM)
    zcol = [pl.BlockSpec((TM, AW), functools.partial(lambda k, i: (i, k), k)) for k in range(3)]
    tr = pl.BlockSpec((nC, 4 * LANES, CHUNK), lambda i: (i, 0, 0))
    r_in, r_out, r_shape, r_scratch = _Riding.specs(later)
    S = jax.ShapeDtypeStruct
    return pl.pallas_call(
        body, name="a_fwd", grid=(nT,),
        in_specs=[row(D)] + zcol + [_const_spec((1, AW)), _const_spec((1, AW)),
                  _const_spec(ws.shape), _const_spec(bs_t.shape), _const_spec(wa_out.shape), _const_spec((1, D)),
                  _const_spec(w_kv.shape), _const_spec((1, 2 * LANES)), row(LANES), row(LANES), row(LANES)] + r_in,
        out_specs=[row(D), row(AW), row(AW), row(LANES), row(4 * LANES), row(4 * LANES), tr, tr] + r_out,
        out_shape=(S((T, D), F32), S((T, AW), BF), S((T, AW), BF), S((T, LANES), F32),
                   S((T, 4 * LANES), BF), S((T, 4 * LANES), BF),
                   S((T // CHUNK, 4 * LANES, CHUNK), BF), S((T // CHUNK, 4 * LANES, CHUNK), BF)) + r_shape,
        scratch_shapes=[pltpu.VMEM((TM, AW), F32)] + r_scratch,
        compiler_params=_params(("arbitrary",)),
    )(x, z, z, z, ln_g, ln_b, ws, bs_t, wa_out, g_kv, w_kv, b_kv, rc, rs1, rs2, *later)


def _b_fwd(h1, g_b, wb_in, bq, rc, rs1, rs2, k4, vt, sinks, wb_out, g_f, target):
    T, D = h1.shape
    BW = wb_out.shape[0]
    SH = wb_in.shape[2]
    TM = min(512, T)
    nC = TM // CHUNK
    nP = BW // LANES

    def body(h1_ref, gb_ref, wbin_ref, bq_ref, rc_ref, rs1_ref, rs2_ref, k4_ref, vt_ref, sink_ref, wbout_ref, gf_ref,
             tgt_ref, q_ref, g2_ref, o_ref, dh2_ref, dh2b_ref, loss_ref, dgf_ref, z_scr, o_scr):
        i = pl.program_id(0)
        h1v = h1_ref[...]
        r2 = lax.rsqrt(jnp.mean(h1v * h1v, axis=-1, keepdims=True) + EPS)
        n2 = (h1v * r2 * gb_ref[...]).astype(BF)
        for j in range(N_DEV):
            z_scr[:, j * SH:(j + 1) * SH] = _dot(n2, wbin_ref[j])
        c_t, s1_t, s2_t = rc_ref[...], rs1_ref[...], rs2_ref[...]
        for p in range(nP):
            cols = slice(p * LANES, (p + 1) * LANES)
            qp = _rot(z_scr[:, cols] + bq_ref[:, cols], c_t, s1_t, s2_t) * (HEAD_DIM ** -0.5)
            q_ref[:, cols] = qp.astype(BF)
        g2 = z_scr[:, BW:]
        g2_ref[...] = g2.astype(BF)
        upper = _upper()
        for c in range(nC):
            ci = i * nC + c
            rows = slice(c * CHUNK, (c + 1) * CHUNK)
            pci = jnp.maximum(ci - 1, 0)
            prev = pl.multiple_of(pci * CHUNK, CHUNK)
            cur = pl.multiple_of(ci * CHUNK, CHUNK)
            qc = q_ref[rows, :]
            for h in range(2):
                st = _dot_nt(_band_rows(k4_ref, prev, cur, h), _stack_pairs(qc, h))
                fa, fb = _fold(st, upper, ci > 0)
                pa, _ = _softmax_sink(fa, sink_ref[2 * h:2 * h + 1, :])
                pb, _ = _softmax_sink(fb, sink_ref[2 * h + 1:2 * h + 2, :])
                ot = _dot(_band_cols(vt_ref, pci, ci, h), _unfold(pa, pb, upper).astype(BF))
                for j in range(4):
                    o_scr[rows, (h * 4 + j) * LANES:(h * 4 + j + 1) * LANES] = ot[:, j * CHUNK:(j + 1) * CHUNK].T
        o = o_scr[...]
        o_ref[...] = o.astype(BF)
        silu, _ = _silu_parts(g2)
        h2 = h1v + _dot((o * silu).astype(BF), wbout_ref[...])
        rf = lax.rsqrt(jnp.mean(h2 * h2, axis=-1, keepdims=True) + EPS)
        xh = h2 * rf
        gf = gf_ref[...]
        err = xh * gf - tgt_ref[...]
        dyf = err * (1.0 / D)
        dh2 = _rms_bwd(dyf, xh, rf, gf)
        dh2_ref[...] = dh2
        dh2b_ref[...] = dh2.astype(BF)

        @pl.when(i == 0)
        def _():
            loss_ref[...] = jnp.zeros_like(loss_ref)
            dgf_ref[...] = jnp.zeros_like(dgf_ref)

        loss_ref[...] += 0.5 * jnp.sum(jnp.mean(err * err, axis=-1, keepdims=True), axis=0, keepdims=True)
        dgf_ref[...] += jnp.sum(dyf * xh, axis=0, keepdims=True)

    row = functools.partial(_row_spec, TM)
    S = jax.ShapeDtypeStruct
    return pl.pallas_call(
        body, name="b_fwd", grid=(T // TM,),
        in_specs=[row(D), _const_spec((1, D)), _const_spec(wb_in.shape), _const_spec((1, BW)), row(LANES), row(LANES),
                  row(LANES), _const_spec(k4.shape), _const_spec(vt.shape), _const_spec(sinks.shape),
                  _const_spec(wb_out.shape), _const_spec((1, D)), row(D)],
        out_specs=[row(BW), row(BW), row(BW), row(D), row(D), _acc_spec((1, 1)), _acc_spec((1, D))],
        out_shape=(S((T, BW), BF), S((T, BW), BF), S((T, BW), BF), S((T, D), F32), S((T, D), BF), S((1, 1), F32),
                   S((1, D), F32)),
        scratch_shapes=[pltpu.VMEM((TM, 2 * BW), F32), pltpu.VMEM((TM, BW), F32)],
        compiler_params=_params(("arbitrary",)),
    )(h1, g_b, wb_in, bq, rc, rs1, rs2, k4, vt, sinks, wb_out, g_f, target)


def _b_bwd(dh2, h1, q, g2, o, k4, v4, kt, sinks, wb_out, wb_in, g_b, rc, rs1, rs2):
    T, D = h1.shape
    BW = wb_out.shape[0]
    SH = wb_in.shape[2]
    TM = min(256, T)
    nT = T // TM
    nC = TM // CHUNK
    nP = BW // LANES

    def body(dh2_ref, h1_ref, q_ref, g2_ref, o_ref, k4_ref, v4_ref, kt_ref, sink_ref, wbout_ref, wbin_ref, gb_ref,
             rc_ref, rs1_ref, rs2_ref,
             dh1_ref, dz2_ref, n2_ref, y2_ref, dk_ref, dv_ref, dbq_ref, dgb_ref, dsink_ref, do_scr, dq_scr, dsacc_scr):
        i = pl.program_id(0)

        @pl.when(i == 0)
        def _():
            dk_ref[...] = jnp.zeros_like(dk_ref)
            dv_ref[...] = jnp.zeros_like(dv_ref)
            dbq_ref[...] = jnp.zeros_like(dbq_ref)
            dgb_ref[...] = jnp.zeros_like(dgb_ref)
            dsacc_scr[...] = jnp.zeros_like(dsacc_scr)

        dh2 = dh2_ref[...]
        dy2 = _dot_nt(dh2.astype(BF), wbout_ref[...])
        silu, dsilu = _silu_parts(g2_ref[...].astype(F32))
        do_scr[...] = (dy2 * silu).astype(BF)
        dy2, silu, dsilu = dy2.astype(BF), silu.astype(BF), dsilu.astype(BF)
        ob = o_ref[...]
        y2_ref[...] = (ob * silu).T
        dz2_ref[:, BW:] = dy2 * ob * dsilu
        upper = _upper()
        lo = _lane_lo((2 * CHUNK, LANES))
        for c in range(nC):
            ci = i * nC + c
            rows = slice(c * CHUNK, (c + 1) * CHUNK)
            pci = jnp.maximum(ci - 1, 0)
            prev = pl.multiple_of(pci * CHUNK, CHUNK)
            cur = pl.multiple_of(ci * CHUNK, CHUNK)
            qc = q_ref[rows, :]
            doc = do_scr[rows, :]
            dkb = jnp.zeros((2 * CHUNK, LANES), F32)
            dvb = jnp.zeros((2 * CHUNK, LANES), F32)
            for h in range(2):
                qs = _stack_pairs(qc, h)
                dos = _stack_pairs(doc, h)
                fa, fb = _fold(_dot_nt(_band_rows(k4_ref, prev, cur, h), qs), upper, ci > 0)
                dfa, dfb = _fold(_dot_nt(_band_rows(v4_ref, prev, cur, h), dos), upper)
                folded = []
                for k, (f, df) in enumerate(((fa, dfa), (fb, dfb))):
                    p, ps = _softmax_sink(f, sink_ref[2 * h + k:2 * h + k + 1, :])
                    delta = jnp.sum(p * df, axis=0, keepdims=True)
                    dsacc_scr[2 * h + k:2 * h + k + 1, :] -= ps * delta
                    folded.append((p, p * (df - delta)))
                pt = _unfold(folded[0][0], folded[1][0], upper).astype(BF)
                dst = _unfold(folded[0][1], folded[1][1], upper).astype(BF)
                dqt = _dot(_band_cols(kt_ref, pci, ci, h), dst)
                for j in range(4):
                    dq_scr[rows, (h * 4 + j) * LANES:(h * 4 + j + 1) * LANES] = dqt[:, j * CHUNK:(j + 1) * CHUNK].T
                for acc_name, g in (("k", _dot(dst, qs)), ("v", _dot(pt, dos))):
                    a, b = g[:2 * CHUNK], g[2 * CHUNK:]
                    if h == 0:
                        part = jnp.where(lo, a + pltpu.roll(b, HEAD_DIM, 1), 0.0)
                    else:
                        part = jnp.where(lo, 0.0, pltpu.roll(a, HEAD_DIM, 1) + b)
                    if acc_name == "k":
                        dkb += part
                    else:
                        dvb += part
            dk_ref[pl.ds(prev, CHUNK), :] += dkb[:CHUNK]
            dk_ref[pl.ds(cur, CHUNK), :] += dkb[CHUNK:]
            dv_ref[pl.ds(prev, CHUNK), :] += dvb[:CHUNK]
            dv_ref[pl.ds(cur, CHUNK), :] += dvb[CHUNK:]

        @pl.when(i == nT - 1)
        def _():
            lane = lax.broadcasted_iota(jnp.int32, dsink_ref.shape, 1)
            tot = jnp.zeros(dsink_ref.shape, F32)
            for j in range(4):
                tot += jnp.where(lane == j, jnp.sum(dsacc_scr[:, j * CHUNK:(j + 1) * CHUNK], axis=1, keepdims=True), 0.0)
            dsink_ref[...] = tot
        c_t, s1_t, s2_t = rc_ref[...], rs1_ref[...], rs2_ref[...]
        for p in range(nP):
            cols = slice(p * LANES, (p + 1) * LANES)
            dqp = _rot_bwd(dq_scr[:, cols] * (HEAD_DIM ** -0.5), c_t, s1_t, s2_t)
            dbq_ref[:, cols] += jnp.sum(dqp, axis=0, keepdims=True)
            dz2_ref[:, cols] = dqp.astype(BF)
        h1v = h1_ref[...]
        r2 = lax.rsqrt(jnp.mean(h1v * h1v, axis=-1, keepdims=True) + EPS)
        xh = h1v * r2
        gb = gb_ref[...]
        n2_ref[...] = (xh * gb).astype(BF).T
        dn2 = None
        for j in range(N_DEV):
            part = _dot_nt(dz2_ref[:, j * SH:(j + 1) * SH], wbin_ref[j])
            dn2 = part if dn2 is None else dn2 + part
        dgb_ref[...] += jnp.sum(dn2 * xh, axis=0, keepdims=True)
        dh1_ref[...] = dh2 + _rms_bwd(dn2, xh, r2, gb)

    row = functools.partial(_row_spec, TM)
    S = jax.ShapeDtypeStruct
    return pl.pallas_call(
        body, name="b_bwd", grid=(T // TM,),
        in_specs=[row(D), row(D), row(BW), row(BW), row(BW), _const_spec(k4.shape), _const_spec(v4.shape),
                  _const_spec(kt.shape), _const_spec(sinks.shape), _const_spec(wb_out.shape), _const_spec(wb_in.shape),
                  _const_spec((1, D)), row(LANES), row(LANES), row(LANES)],
        out_specs=[row(D), row(2 * BW), _col_spec(TM, D), _col_spec(TM, BW), _acc_spec((T, LANES)),
                   _acc_spec((T, LANES)), _acc_spec((1, BW)), _acc_spec((1, D)), _acc_spec((4, LANES))],
        out_shape=(S((T, D), F32), S((T, 2 * BW), BF), S((D, T), BF), S((BW, T), BF), S((T, LANES), F32),
                   S((T, LANES), F32), S((1, BW), F32), S((1, D), F32), S((4, LANES), F32)),
        scratch_shapes=[pltpu.VMEM((TM, BW), BF), pltpu.VMEM((TM, BW), F32), pltpu.VMEM((4, 4 * CHUNK), F32)],
        compiler_params=_params(("arbitrary",)),
    )(dh2, h1, q, g2, o, k4, v4, kt, sinks, wb_out, wb_in, g_b, rc, rs1, rs2)


def _a_bwd(dh1p, dk, dv, h1, g_kv, w_kv, wa_out, ws, ln_g, ln_b, z, sv, vhat, rstd, rc, rs1, rs2, ready):
    T, D = h1.shape
    AW = wa_out.shape[0]
    G = ws.shape[0]
    TM = min(256, T)
    nT = T // TM
    nC = TM // CHUNK
    nr = len(ready)

    def body(dh1p_ref, dk_ref, dv_ref, h1_ref, gkv_ref, wkv_ref, waout_ref, ws_ref, lng_ref,
             lnb_ref, u_ref, gt_ref, sv_ref, vhat_ref, rstd_ref, rc_ref, rs1_ref, rs2_ref, *rest):
        ready_refs, rest = rest[:nr], rest[nr:]
        (dz_ref, gwo_ref, gwk_ref, dh1f_ref, dgkv_ref, dbkv_ref, dlng_ref, dlnb_ref,
         dws_ref, dbs_ref), rest = rest[:10], rest[10:]
        recv_refs, (dsv_scr, dvln_scr, acco_scr, acck_scr, ssem, rsem, lsem) = rest[:nr], rest[nr:]
        i = pl.program_id(0)
        exchanges = [_Direct(ready_refs[k], recv_refs[k], ssem.at[k], rsem.at[k], lsem.at[k], scatter=True)
                     for k in range(nr)]

        @pl.when(i == 0)
        def _():
            for e in exchanges:
                e.start()
            for r in (dgkv_ref, dbkv_ref, dlng_ref, dlnb_ref, dws_ref, dbs_ref, acco_scr, acck_scr):
                r[...] = jnp.zeros_like(r)

        dk_pre = _rot_bwd(dk_ref[...], rc_ref[...], rs1_ref[...], rs2_ref[...])
        dkv = jnp.concatenate([dk_pre, dv_ref[...]], axis=1)
        dbkv_ref[...] += jnp.sum(dkv, axis=0, keepdims=True)
        dkv_b = dkv.astype(BF)
        h1v = h1_ref[...]
        rkv = lax.rsqrt(jnp.mean(h1v * h1v, axis=-1, keepdims=True) + EPS)
        xh_kv = h1v * rkv
        gkv = gkv_ref[...]
        acck_scr[...] += _dot((xh_kv * gkv).astype(BF).T, dkv_b)
        dnkv = _dot_nt(dkv_b, wkv_ref[...])
        dgkv_ref[...] += jnp.sum(dnkv * xh_kv, axis=0, keepdims=True)
        dh1 = dh1p_ref[...] + _rms_bwd(dnkv, xh_kv, rkv, gkv)
        dh1_b = dh1.astype(BF)
        dh1f_ref[...] = dh1
        dy = _dot_nt(dh1_b, waout_ref[...]).astype(BF)
        silu, dsilu = _silu_parts(gt_ref[...].astype(F32))
        silu, dsilu = silu.astype(BF), dsilu.astype(BF)
        ub, svb = u_ref[...], sv_ref[...]
        us = ub * silu
        dys = dy * svb
        acco_scr[...] += _dot((us * svb).T, dh1_b)
        dz_ref[:, :AW] = dys * silu
        dz_ref[:, 2 * AW:] = dys * ub * dsilu
        dsv_scr[...] = dy * us
        vhat_v = vhat_ref[...].astype(F32)
        lng = lng_ref[...]
        vln_b = (vhat_v * lng + lnb_ref[...]).astype(BF)
        tri = lax.broadcasted_iota(jnp.int32, (CHUNK, CHUNK), 0) >= lax.broadcasted_iota(jnp.int32, (CHUNK, CHUNK), 1)
        lane = lax.broadcasted_iota(jnp.int32, (CHUNK, LANES), 1)
        dbs = jnp.zeros((CHUNK, LANES), F32)
        for g in range(G):
            wsm = jnp.where(tri, ws_ref[g], 0.0).astype(BF)
            cols = slice(g * CHUNK, (g + 1) * CHUNK)
            dws_g = None
            for c in range(nC):
                rows = slice(c * CHUNK, (c + 1) * CHUNK)
                dsv_cg = dsv_scr[rows, cols]
                dvln_scr[rows, cols] = _dot_tn(wsm, dsv_cg)
                part = _dot_nt(dsv_cg, vln_b[rows, cols])
                dws_g = part if dws_g is None else dws_g + part
                dbs += jnp.where(lane == g, jnp.sum(dsv_cg.astype(F32), axis=-1, keepdims=True), 0.0)
            dws_ref[g] += jnp.where(tri, dws_g, 0.0)
        dbs_ref[...] += dbs
        dvln = dvln_scr[...]
        dlng_ref[...] += jnp.sum(dvln * vhat_v, axis=0, keepdims=True)
        dlnb_ref[...] += jnp.sum(dvln, axis=0, keepdims=True)
        a = dvln * lng
        dvv = rstd_ref[:, 0:1] * (a - jnp.mean(a, axis=-1, keepdims=True)
                                  - vhat_v * jnp.mean(a * vhat_v, axis=-1, keepdims=True))
        dz_ref[:, AW:2 * AW] = dvv.astype(BF)

        @pl.when(i == nT - 1)
        def _():
            for j in range(N_DEV):
                gwo_ref[j] = acco_scr[j * (AW // N_DEV):(j + 1) * (AW // N_DEV)].astype(BF)
                gwk_ref[j] = acck_scr[j * (D // N_DEV):(j + 1) * (D // N_DEV)].astype(BF)
            for e in exchanges:
                e.finish()

    row = functools.partial(_row_spec, TM)
    hbm = pl.BlockSpec(memory_space=pl.ANY)
    S = jax.ShapeDtypeStruct
    gwo_shape, gwk_shape = (N_DEV, AW // N_DEV, D), (N_DEV, D // N_DEV, 2 * LANES)
    return pl.pallas_call(
        body, name="a_bwd", grid=(nT,),
        in_specs=[row(D), row(LANES), row(LANES), row(D), _const_spec((1, D)), _const_spec(w_kv.shape),
                  _const_spec(wa_out.shape), _const_spec(ws.shape),
                  _const_spec((1, AW)), _const_spec((1, AW)), pl.BlockSpec((TM, AW), lambda i: (i, 0)),
                  pl.BlockSpec((TM, AW), lambda i: (i, 2)), row(AW), row(AW), row(LANES),
                  row(LANES), row(LANES), row(LANES)] + [hbm] * nr,
        out_specs=[row(3 * AW), _const_spec(gwo_shape), _const_spec(gwk_shape), row(D),
                   _acc_spec((1, D)), _acc_spec((1, 2 * LANES)), _acc_spec((1, AW)),
                   _acc_spec((1, AW)), _acc_spec(ws.shape), _acc_spec((CHUNK, LANES))] + [hbm] * nr,
        out_shape=(S((T, 3 * AW), BF), S(gwo_shape, BF), S(gwk_shape, BF), S((T, D), F32),
                   S((1, D), F32), S((1, 2 * LANES), F32), S((1, AW), F32), S((1, AW), F32),
                   S(ws.shape, F32), S((CHUNK, LANES), F32)) + tuple(S(r.shape, r.dtype) for r in ready),
        scratch_shapes=[pltpu.VMEM((TM, AW), BF), pltpu.VMEM((TM, AW), F32), pltpu.VMEM((AW, D), F32),
                        pltpu.VMEM((D, 2 * LANES), F32)] + _direct_sems(nr),
        compiler_params=_params(("arbitrary",)),
    )(dh1p, dk, dv, h1, g_kv, w_kv, wa_out, ws, ln_g, ln_b, z, z, sv, vhat, rstd, rc, rs1, rs2, *ready)


def _a_in_bwd(dz, wa_in, x, dh1, g_a, ready):
    T, D = x.shape
    SH = wa_in.shape[2]
    TM = min(512, T)
    nT = T // TM
    nr = len(ready)

    def body(dz_ref, wain_ref, x_ref, dh1_ref, ga_ref, *rest):
        ready_refs, (dx_ref, n1_ref, dga_ref), rest = rest[:nr], rest[nr:nr + 3], rest[nr + 3:]
        recv_refs, (ssem, rsem, lsem) = rest[:nr], rest[nr:]
        i = pl.program_id(0)
        exchanges = [_Direct(ready_refs[k], recv_refs[k], ssem.at[k], rsem.at[k], lsem.at[k], scatter=True)
                     for k in range(nr)]

        @pl.when(i == 0)
        def _():
            for e in exchanges:
                e.start()
            dga_ref[...] = jnp.zeros_like(dga_ref)

        xv = x_ref[...]
        r1 = lax.rsqrt(jnp.mean(xv * xv, axis=-1, keepdims=True) + EPS)
        xh = xv * r1
        ga = ga_ref[...]
        n1_ref[...] = (xh * ga).astype(BF).T
        dn1 = None
        for j in range(N_DEV):
            part = _dot_nt(dz_ref[:, j * SH:(j + 1) * SH], wain_ref[j])
            dn1 = part if dn1 is None else dn1 + part
        dga_ref[...] += jnp.sum(dn1 * xh, axis=0, keepdims=True)
        dx_ref[...] = dh1_ref[...] + _rms_bwd(dn1, xh, r1, ga)

        @pl.when(i == nT - 1)
        def _():
            for e in exchanges:
                e.finish()

    row = functools.partial(_row_spec, TM)
    hbm = pl.BlockSpec(memory_space=pl.ANY)
    S = jax.ShapeDtypeStruct
    return pl.pallas_call(
        body, name="a_in_bwd", grid=(nT,),
        in_specs=[row(dz.shape[1]), _const_spec(wa_in.shape), row(D), row(D), _const_spec((1, D))] + [hbm] * nr,
        out_specs=[row(D), _col_spec(TM, D), _acc_spec((1, D))] + [hbm] * nr,
        out_shape=(S((T, D), F32), S((D, T), BF), S((1, D), F32)) + tuple(S(r.shape, r.dtype) for r in ready),
        scratch_shapes=_direct_sems(nr),
        compiler_params=_params(("arbitrary",)),
    )(dz, wa_in, x, dh1, g_a, *ready)


def _wgrad(at, b, nblk, name, bt=512):
    K, T = at.shape
    N = b.shape[1] // nblk
    BT = min(bt, T)
    nt = T // BT

    def body(a_ref, b_ref, o_ref, acc):
        t = pl.program_id(0)

        @pl.when(t == 0)
        def _():
            acc[...] = jnp.zeros_like(acc)

        acc[...] += _dot(a_ref[...], b_ref[...])

        @pl.when(t == nt - 1)
        def _():
            for j in range(nblk):
                o_ref[j] = acc[:, j * N:(j + 1) * N].astype(BF)

    return pl.pallas_call(
        body, name=name, grid=(nt,),
        in_specs=[pl.BlockSpec((K, BT), lambda t: (0, t)), pl.BlockSpec((BT, nblk * N), lambda t: (t, 0))],
        out_specs=pl.BlockSpec((nblk, K, N), lambda t: (0, 0, 0)),
        out_shape=jax.ShapeDtypeStruct((nblk, K, N), BF),
        scratch_shapes=[pltpu.VMEM((K, nblk * N), F32)],
        compiler_params=_params(("arbitrary",)),
    )(at, b)


def _wgrad_exchange(a, b, me, small, name):
    K, T = a.shape
    N = b.shape[1] // N_DEV
    BT = min(2048, T)
    nt = T // BT
    last = N_DEV - 1
    n_chip = N_DEV // 2

    def far_of(k, core):
        return jnp.where((core == 0) & ((k == 1) | (k == 2)), k, n_chip - 1 - k)

    def block_of(s, me_i):
        k, odd = s // 2, s % 2
        core = me_i & 1
        return me_i ^ ((far_of(k, jnp.where(odd == 1, core, 1 - core)) << 1) | (1 - odd))

    def body(me_ref, a_ref, b_ref, small_ref, recv_ref, full_ref, *scratch):
        (acc, dstage, istage, half, d_s, d_r, i_s, i_r, lsem, parts_scr, red_scr, e_s, e_r, e_l, g_s, g_r,
         g_l) = scratch
        s, t = pl.program_id(0), pl.program_id(1)
        x, y, c = (lax.axis_index(ax) for ax in AXES)
        ex = [_Direct(small_ref, parts_scr, e_s, e_r, e_l, scatter=True)]
        regather = _TwoLevel(red_scr, full_ref, g_s, g_r, g_l)

        def to_sibling(k, slot):
            return pltpu.make_async_remote_copy(src_ref=dstage.at[slot], dst_ref=half.at[k], send_sem=d_s.at[k],
                                                recv_sem=d_r.at[k], device_id=(x, y, 1 - c), device_id_type=MESH)

        def to_chip(k, slot, sender):
            far = far_of(k, c)
            px, py = x ^ ((far >> 1) & 1), y ^ (far & 1)
            dst = recv_ref.at[2 * x + y] if sender else recv_ref.at[2 * px + py]
            return pltpu.make_async_remote_copy(src_ref=istage.at[slot], dst_ref=dst, send_sem=i_s.at[k],
                                                recv_sem=i_r.at[k], device_id=(px, py, c), device_id_type=MESH)

        @pl.when((s == 0) & (t == 0))
        def _():
            for e in ex:
                e.start()

        @pl.when(t == 0)
        def _():
            acc[...] = jnp.zeros_like(acc)

        acc[...] += _dot(a_ref[...], b_ref[...])

        @pl.when(t == nt - 1)
        def _():
            k = lax.div(s, 2)
            slot = lax.rem(k, 2)

            @pl.when(lax.rem(s, 2) == 0)
            def _():
                @pl.when(k >= 2)
                def _():
                    to_sibling(k - 2, slot).wait_send()

                dstage[slot] = acc[...].astype(BF)
                to_sibling(k, slot).start()

            @pl.when(lax.rem(s, 2) == 1)
            def _():
                to_sibling(k, slot).wait_recv()

                @pl.when(k >= 2)
                def _():
                    to_chip(k - 2, slot, True).wait_send()

                istage[slot] = (acc[...] + half[k].astype(F32)).astype(BF)

                @pl.when(k < n_chip - 1)
                def _():
                    to_chip(k, slot, True).start()

            @pl.when(s == last)
            def _():
                own = pltpu.make_async_copy(istage.at[slot], recv_ref.at[2 * x + y], lsem)
                own.start()
                to_chip(n_chip - 2, 0, True).wait_send()
                to_sibling(n_chip - 2, 0).wait_send()
                to_sibling(n_chip - 1, 1).wait_send()
                for kk in range(n_chip - 1):
                    to_chip(kk, 0, False).wait_recv()
                own.wait()
                for e in ex:
                    e.finish()
                total = parts_scr[0]
                for dev in range(1, N_DEV):
                    total = total + parts_scr[dev]
                red_scr[...] = total
                regather.start()
                regather.forward()
                regather.finish()

    hbm = pl.BlockSpec(memory_space=pl.ANY)
    dma = pltpu.SemaphoreType.DMA
    grid_spec = pltpu.PrefetchScalarGridSpec(
        num_scalar_prefetch=1, grid=(N_DEV, nt),
        in_specs=[pl.BlockSpec((K, BT), lambda s, t, me_ref: (0, t)),
                  pl.BlockSpec((BT, N), lambda s, t, me_ref: (t, block_of(s, me_ref[0]))), hbm],
        out_specs=[hbm, hbm],
        scratch_shapes=[pltpu.VMEM((K, N), F32), pltpu.VMEM((2, K, N), BF), pltpu.VMEM((2, K, N), BF),
                        pltpu.VMEM((n_chip, K, N), BF), dma((n_chip,)), dma((n_chip,)), dma((n_chip - 1,)),
                        dma((n_chip - 1,)), dma, pltpu.VMEM(small.shape, F32), pltpu.VMEM(small.shape[1:], F32),
                        dma((last,)), dma((last,)), dma, dma((last,)), dma((last,)), dma])
    return pl.pallas_call(
        body, name=name, grid_spec=grid_spec,
        out_shape=[jax.ShapeDtypeStruct((n_chip, K, N), BF), jax.ShapeDtypeStruct(small.shape, F32)],
        compiler_params=_params(("arbitrary", "arbitrary")),
    )(me, a, b, small)


def _my_index():
    return 4 * lax.axis_index("x") + 2 * lax.axis_index("y") + lax.axis_index("c")


def _peer(mask):
    x, y, c = (lax.axis_index(a) for a in AXES)
    return (x ^ ((mask >> 2) & 1), y ^ ((mask >> 1) & 1), c ^ (mask & 1))


def _dev_index(p):
    return 4 * p[0] + 2 * p[1] + p[2]


class _Direct:
    def __init__(self, src, dst, send_sems, recv_sems, local_sem, scatter):
        me = _my_index()
        self.own = pltpu.make_async_copy(src.at[me] if scatter else src, dst.at[me], local_sem)
        self.sends, self.recvs = [], []
        for k in range(1, N_DEV):
            p = _peer(k)
            pi = _dev_index(p)
            sems = dict(send_sem=send_sems.at[k - 1], recv_sem=recv_sems.at[k - 1], device_id=p, device_id_type=MESH)
            self.sends.append(pltpu.make_async_remote_copy(src_ref=src.at[pi] if scatter else src, dst_ref=dst.at[me],
                                                           **sems))
            self.recvs.append(pltpu.make_async_remote_copy(src_ref=src.at[me] if scatter else src, dst_ref=dst.at[pi],
                                                           **sems))

    def start(self):
        self.own.start()
        for cp in self.sends:
            cp.start()

    def finish(self):
        for cp in self.sends:
            cp.wait_send()
        for cp in self.recvs:
            cp.wait_recv()
        self.own.wait()


class _TwoLevel:
    def __init__(self, src, dst, send_sems, recv_sems, local_sem, own=True):
        x, y, c = (lax.axis_index(a) for a in AXES)
        self.me, self.sibling = (x, y, c), (x, y, 1 - c)
        self.chips = [(1 - x, y), (x, 1 - y), (1 - x, 1 - y)]
        self.src, self.dst, self.send_sems, self.recv_sems = src, dst, send_sems, recv_sems
        self.own = pltpu.make_async_copy(src, dst.at[_dev_index(self.me)], local_sem) if own else None

    def _copy(self, k, block, to, from_src=False):
        slot = self.dst.at[_dev_index(block)]
        return pltpu.make_async_remote_copy(src_ref=self.src if from_src else slot, dst_ref=slot,
                                            send_sem=self.send_sems.at[k], recv_sem=self.recv_sems.at[k],
                                            device_id=to, device_id_type=MESH)

    def _firsts(self):
        c = self.me[2]
        return [self._copy(0, self.me, self.sibling, True)] + [self._copy(1 + j, self.me, (*chip, c), True)
                                                               for j, chip in enumerate(self.chips)]

    def _passed(self):
        c = self.me[2]
        return [self._copy(4 + j, (*chip, c), self.sibling) for j, chip in enumerate(self.chips)]

    def start(self):
        if self.own is not None:
            self.own.start()
        for cp in self._firsts():
            cp.start()

    def wait_sibling(self):
        self._copy(0, self.sibling, self.me).wait_recv()

    def wait_chip_and_forward(self, j):
        self._copy(1 + j, (*self.chips[j], self.me[2]), self.me).wait_recv()
        self._passed()[j].start()

    def wait_passed(self, j):
        self._copy(4 + j, (*self.chips[j], 1 - self.me[2]), self.me).wait_recv()

    def wait_sends(self):
        for cp in self._firsts() + self._passed():
            cp.wait_send()
        if self.own is not None:
            self.own.wait()

    def forward(self):
        for j in range(3):
            self.wait_chip_and_forward(j)

    def finish(self):
        self.wait_sibling()
        for j in range(3):
            self.wait_passed(j)
        self.wait_sends()


class _RelayGather:
    def __init__(self, dst, send_sems, recv_sems):
        x, y, c = (lax.axis_index(a) for a in AXES)
        self.c = c
        self.sib, self.xn, self.yn, self.dg = (x, y, 1 - c), (1 - x, y, c), (x, 1 - y, c), (1 - x, 1 - y, c)
        self.me = (x, y, c)
        self.dst, self.send_sems, self.recv_sems = dst, send_sems, recv_sems
        self.half = dst.shape[1] // 2

    def _slot(self, dev, part=None):
        i = _dev_index(dev)
        if part is None:
            return self.dst.at[i]
        return self.dst.at[i, pl.ds(part * self.half, self.half)]

    def _copy(self, k, dev, to, part=None):
        ref = self._slot(dev, part)
        return pltpu.make_async_remote_copy(src_ref=ref, dst_ref=ref, send_sem=self.send_sems.at[k],
                                            recv_sem=self.recv_sems.at[k], device_id=to, device_id_type=MESH)

    def _other(self, dev):
        return (dev[0], dev[1], 1 - self.c)

    def start(self):
        for k, to in enumerate((self.sib, self.xn, self.yn)):
            self._copy(k, self.me, to).start()

    def send_own(self, k):
        return self._copy(k, self.me, (self.sib, self.xn, self.yn)[k])

    def wait_sibling(self):
        self._copy(0, self.sib, self.me).wait_recv()

    def on_x(self):
        self._copy(1, self.xn, self.me).wait_recv()
        self._copy(3, self.xn, self.yn, part=0).start()
        self._copy(5, self.xn, self.sib).start()

    def on_y(self):
        self._copy(2, self.yn, self.me).wait_recv()
        self._copy(4, self.yn, self.xn, part=1).start()
        self._copy(6, self.yn, self.sib).start()

    def on_diag(self):
        self._copy(3, self.dg, self.me, part=0).wait_recv()
        self._copy(4, self.dg, self.me, part=1).wait_recv()
        self._copy(7, self.dg, self.sib).start()

    def wait_passed(self, j):
        self._copy(5 + j, self._other((self.xn, self.yn, self.dg)[j]), self.me).wait_recv()

    def wait_sends(self):
        for k, to in enumerate((self.sib, self.xn, self.yn)):
            self._copy(k, self.me, to).wait_send()
        self._copy(3, self.xn, self.yn, part=0).wait_send()
        self._copy(4, self.yn, self.xn, part=1).wait_send()
        for j, dev in enumerate((self.xn, self.yn, self.dg)):
            self._copy(5 + j, dev, self.sib).wait_send()


def _direct_sems(n):
    if n == 0:
        return []
    return [pltpu.SemaphoreType.DMA((n, 7)), pltpu.SemaphoreType.DMA((n, 7)), pltpu.SemaphoreType.DMA((n,))]


def _adam_math(w, g, m, v):
    m = ADAM_B1 * m + (1.0 - ADAM_B1) * g
    v = ADAM_B2 * v + (1.0 - ADAM_B2) * (g * g)
    m_hat = m / (1.0 - ADAM_B1 ** ADAM_STEP)
    v_hat = v / (1.0 - ADAM_B2 ** ADAM_STEP)
    delta = -ADAM_LR * (m_hat / (jnp.sqrt(v_hat) + ADAM_EPS) + ADAM_WD * w)
    return delta, m, v


def _sum_adam(parts, w, m, v, name):
    R, C = w.shape
    NP = parts.shape[0]
    BR = CHUNK if R % CHUNK == 0 else R

    def body(p_ref, w_ref, m_ref, v_ref, g_ref, d_ref, nm_ref, nv_ref):
        g = p_ref[0].astype(F32)
        for i in range(1, NP):
            g = g + p_ref[i].astype(F32)
        g_ref[...] = g
        d_ref[...], nm_ref[...], nv_ref[...] = _adam_math(w_ref[...], g, m_ref[...], v_ref[...])

    blk = pl.BlockSpec((BR, C), lambda i: (i, 0))
    S = jax.ShapeDtypeStruct((R, C), F32)
    return pl.pallas_call(
        body, name=name, grid=(R // BR,),
        in_specs=[pl.BlockSpec((NP, BR, C), lambda i: (0, i, 0)), blk, blk, blk],
        out_specs=[blk] * 4, out_shape=(S,) * 4,
        compiler_params=_params(("arbitrary",)),
    )(parts, w, m, v)


SUBLANES = 8


def _nrows(size):
    return -(-size // (SUBLANES * LANES)) * SUBLANES


def _view2d(a):
    return a.reshape(-1, LANES) if a.size % LANES == 0 else a.reshape(1, -1)


def _pack_small(parts, total_rows, name):
    arrs = [p[0] for p in parts]

    def body(*refs):
        out = refs[-1]
        out[...] = jnp.zeros_like(out)
        at = 0
        for ref, (a, rows, flag) in zip(refs[:-1], parts):
            val = ref[...].T if flag == "T" else ref[...]
            r, c = (rows, val.shape[1]) if flag == "T" else val.shape
            out[at:at + r, 0:c] = val[:r]
            at += _nrows(r * c)

    return pl.pallas_call(body, name=name, out_shape=jax.ShapeDtypeStruct((total_rows, LANES), F32))(*arrs)


def _small_update(full, me, reps, shards, name):
    n = len(reps) + len(shards)

    def body(me_ref, full_ref, *refs):
        ins, outs = refs[:3 * n], refs[3 * n:]
        at = 0
        for k in range(n):
            w_ref, m_ref, v_ref = ins[3 * k:3 * k + 3]
            r, c = w_ref.shape
            if k < len(reps):
                g = full_ref[at:at + r, 0:c]
                at += _nrows(r * c)
            else:
                seg = full_ref[at:at + N_DEV * r, :]
                row = lax.broadcasted_iota(jnp.int32, seg.shape, 0)
                pick = [jnp.sum(jnp.where(row == r * me_ref[0] + t, seg, 0.0), axis=0, keepdims=True) for t in range(r)]
                g = pick[0] if r == 1 else jnp.concatenate(pick, axis=0)
                at += N_DEV * r
            g_ref, d_ref, nm_ref, nv_ref = outs[4 * k:4 * k + 4]
            g_ref[...] = g
            d_ref[...], nm_ref[...], nv_ref[...] = _adam_math(w_ref[...], g, m_ref[...], v_ref[...])
        outs[4 * n][...] = full_ref[at:at + 1, 0:1]

    flat = [t for p in reps + shards for t in p]
    S = jax.ShapeDtypeStruct
    res = pl.pallas_call(
        body, name=name,
        in_specs=[pl.BlockSpec(memory_space=pltpu.SMEM)] + [pl.BlockSpec(memory_space=pltpu.VMEM)] * (1 + len(flat)),
        out_shape=[S(p[0].shape, F32) for p in reps + shards for _ in range(4)] + [S((1, 1), F32)],
    )(me, full, *flat)
    return [tuple(res[4 * k:4 * k + 4]) for k in range(n)], res[4 * n]


def _rope_tables(T):
    pos = np.arange(T, dtype=np.float32)
    inv_freq = (np.float64(ROPE_THETA) ** (-np.arange(0, HEAD_DIM, 2, dtype=np.float64) / HEAD_DIM)).astype(np.float32)
    ang = (pos[:, None] * inv_freq[None, :]).astype(np.float64)
    cos, sin, zero = np.cos(ang).astype(np.float32), np.sin(ang).astype(np.float32), np.zeros(ang.shape, np.float32)
    c = np.concatenate([cos, cos, cos, cos], axis=1)
    s1 = np.concatenate([-sin, zero, -sin, zero], axis=1)
    s2 = np.concatenate([zero, sin, zero, sin], axis=1)
    return jnp.asarray(c), jnp.asarray(s1), jnp.asarray(s2)


def kernel(x, a_norm_g, a_w_in, a_ln_g, a_ln_b, a_ws, a_bs, a_w_out, kv_norm_g, w_kv, b_kv, b_norm_g, b_w_in, b_bq, b_sinks, b_w_out, final_norm_g, loss_target, m_a_norm_g, m_a_w_in, m_a_ln_g, m_a_ln_b, m_a_ws, m_a_bs, m_a_w_out, m_kv_norm_g, m_w_kv, m_b_kv, m_b_norm_g, m_b_w_in, m_b_bq, m_b_sinks, m_b_w_out, m_final_norm_g, v_a_norm_g, v_a_w_in, v_a_ln_g, v_a_ln_b, v_a_ws, v_a_bs, v_a_w_out, v_kv_norm_g, v_w_kv, v_b_kv, v_b_norm_g, v_b_w_in, v_b_bq, v_b_sinks, v_b_w_out, v_final_norm_g):
    T, D = x.shape[1], x.shape[2]
    AW = a_ln_g.shape[1] * N_DEV
    G = a_ws.shape[1]
    assert w_kv.shape[1] == 2 * LANES and a_ws.shape[2] == CHUNK and T % CHUNK == 0
    me = _my_index()

    xs, tgt = x[0], loss_target[0]
    vec = jnp.concatenate([a_norm_g, a_ln_g, a_ln_b], axis=1)
    vec = jnp.broadcast_to(vec, (SUBLANES, vec.shape[1]))
    north = lax.axis_index("c") == 1
    slots = me ^ jnp.where(north, jnp.array(PASS_MASKS[1], jnp.int32), jnp.array(PASS_MASKS[0], jnp.int32))
    z, wa_in, vecs, wa_out, wkv = _in_proj(xs, a_w_in[0], vec, slots, [a_w_out[0], w_kv])
    wa_out = wa_out.reshape(AW, D)
    wkv = wkv.reshape(D, 2 * LANES)
    vecs = vecs[:, 0, :]
    ds = D // N_DEV
    g_a = vecs[:, :ds].reshape(1, D)
    ln_g = vecs[:, ds:ds + AW // N_DEV].reshape(1, AW)
    ln_b = vecs[:, ds + AW // N_DEV:].reshape(1, AW)

    rc, rs1, rs2 = _rope_tables(T)
    ws = a_ws[0]
    bs_t = a_bs[0].T
    g_kv = kv_norm_g.reshape(1, D)
    bkv = b_kv.reshape(1, -1)
    g_f = final_norm_g.reshape(1, D)
    sinks = jnp.repeat(b_sinks.reshape(2, 4, 2).transpose(0, 2, 1).reshape(4, 4), CHUNK, axis=1)
    h1, sv, vhat, rstd, k4, v4, kt, vt, wb_in, wb_out = _a_fwd(
        xs, z, ln_g, ln_b, ws, bs_t, wa_out, g_kv, wkv, bkv, rc, rs1, rs2, [b_w_in[0], b_w_out[0]])
    wb_out = wb_out.reshape(-1, D)
    q, g2, o, dh2, dh2_b, loss, d_gf = _b_fwd(h1, b_norm_g, wb_in, b_bq, rc, rs1, rs2, k4, vt, sinks, wb_out, g_f, tgt)
    dh1p, dz2, n2, y2, dk, dv, d_bq, d_gb, d_sink = _b_bwd(dh2, h1, q, g2, o, k4, v4, kt, sinks, wb_out, wb_in,
                                                           b_norm_g, rc, rs1, rs2)
    d_sink = d_sink[:, :4].reshape(2, 2, 4).transpose(0, 2, 1).reshape(1, 16)
    gw_b_in = _wgrad(n2, dz2, N_DEV, "wgrad_b_in", bt=1024)
    gw_b_out = _wgrad(y2, dh2_b, 1, "wgrad_b_out", bt=1024).reshape(N_DEV, -1, D)
    (dz, gw_a_out, gw_kv, dh1_f, d_gkv, d_bkv, d_lng, d_lnb, d_ws, d_bst, r_b_in, r_b_out) = _a_bwd(
        dh1p, dk, dv, h1, g_kv, wkv, wa_out, ws, ln_g, ln_b, z, sv, vhat, rstd, rc, rs1, rs2, [gw_b_in, gw_b_out])
    dx, n1, d_ga, r_a_out, r_kv = _a_in_bwd(dz, wa_in, xs, dh1_f, g_a, [gw_a_out, gw_kv])
    small = [(_view2d(d_ws), None, None), (d_bst, G, "T")] + [(_view2d(a), None, None) for a in (
        d_gkv, d_bkv, d_gb, d_bq, d_sink, d_gf, d_ga, d_lng, d_lnb, loss)]
    used = sum(_nrows(G * CHUNK if flag else a.size) for a, _, flag in small)
    per = -(-used // (SUBLANES * N_DEV)) * SUBLANES
    small_pack = _pack_small(small, per * N_DEV, "pack_small").reshape(N_DEV, per, LANES)
    r_a_in, full_small = _wgrad_exchange(n1, dz, me.reshape(1), small_pack, "wgrad_a_in")

    g_a_in, d_a_in, nm_a_in, nv_a_in = _sum_adam(r_a_in, a_w_in[0], m_a_w_in[0], v_a_w_in[0], "adam_a_in")
    g_a_out, d_a_out, nm_a_out, nv_a_out = _sum_adam(r_a_out, a_w_out[0], m_a_w_out[0], v_a_w_out[0], "adam_a_out")
    g_kvw, d_kvw, nm_kvw, nv_kvw = _sum_adam(r_kv, w_kv, m_w_kv, v_w_kv, "adam_kv")
    g_b_in, d_b_in, nm_b_in, nv_b_in = _sum_adam(r_b_in, b_w_in[0], m_b_w_in[0], v_b_w_in[0], "adam_b_in")
    g_b_out, d_b_out, nm_b_out, nv_b_out = _sum_adam(r_b_out, b_w_out[0], m_b_w_out[0], v_b_w_out[0], "adam_b_out")

    full_small = full_small.reshape(N_DEV * per, LANES)
    reps = [(a_ws, m_a_ws, v_a_ws), (a_bs, m_a_bs, v_a_bs), (kv_norm_g, m_kv_norm_g, v_kv_norm_g),
            (b_kv, m_b_kv, v_b_kv), (b_norm_g, m_b_norm_g, v_b_norm_g), (b_bq, m_b_bq, v_b_bq),
            (b_sinks, m_b_sinks, v_b_sinks), (final_norm_g, m_final_norm_g, v_final_norm_g)]
    shards = [(a_norm_g, m_a_norm_g, v_a_norm_g), (a_ln_g, m_a_ln_g, v_a_ln_g), (a_ln_b, m_a_ln_b, v_a_ln_b)]
    upd, loss = _small_update(full_small, me.reshape(1), [tuple(_view2d(t) for t in p) for p in reps],
                              [tuple(_view2d(t) for t in p) for p in shards], "adam_small")
    loss = loss[0, 0]
    sm_g, sd, snm, snv = ([upd[k][j].reshape(p[0].shape) for k, p in enumerate(reps + shards)] for j in range(4))

    def order(big, sm):
        a_in, a_out, kvw, b_in, b_out = big
        ws_, bs_, kvg, bkv_, bng, bq_, snk, fng, ang, alng, alnb = sm
        return (ang, a_in[None], alng, alnb, ws_, bs_, a_out[None], kvg, kvw, bkv_, bng, b_in[None], bq_, snk,
                b_out[None], fng)

    grads = order((g_a_in, g_a_out, g_kvw, g_b_in, g_b_out), sm_g)
    deltas = order((d_a_in, d_a_out, d_kvw, d_b_in, d_b_out), sd)
    new_m = order((nm_a_in, nm_a_out, nm_kvw, nm_b_in, nm_b_out), snm)
    new_v = order((nv_a_in, nv_a_out, nv_kvw, nv_b_in, nv_b_out), snv)
    return (loss, dx[None], *grads, *deltas, *new_m, *new_v)
```

```python
import functools

import jax
import jax.numpy as jnp
import numpy as np
from jax import lax
from jax.experimental import pallas as pl
from jax.experimental.pallas import tpu as pltpu

CHUNK = 128
HEAD_DIM = 64
ROPE_THETA = 10000.0
EPS = 1e-5
ADAM_LR = 0.001
ADAM_B1 = 0.9
ADAM_B2 = 0.999
ADAM_EPS = 1e-08
ADAM_WD = 0.01
ADAM_STEP = 10
N_DEV = 8
LANES = 128
NEG = -1e30

BF = jnp.bfloat16
F32 = jnp.float32
MESH = pl.DeviceIdType.MESH
AXES = ("x", "y", "c")
VMEM_LIMIT = 56 * 1024 * 1024


def _dot(a, b):
    return jnp.dot(a, b, preferred_element_type=F32)


def _dot_nt(a, b):
    return lax.dot_general(a, b, (((1,), (1,)), ((), ())), preferred_element_type=F32)


def _dot_tn(a, b):
    return lax.dot_general(a, b, (((0,), (0,)), ((), ())), preferred_element_type=F32)


def _const_spec(shape):
    nd = len(shape)
    return pl.BlockSpec(shape, lambda *_: (0,) * nd, pipeline_mode=pl.Buffered(1))


def _acc_spec(shape):
    nd = len(shape)
    return pl.BlockSpec(shape, lambda *_: (0,) * nd)


def _row_spec(tm, width):
    return pl.BlockSpec((tm, width), lambda i: (i, 0))


def _col_spec(tm, height):
    return pl.BlockSpec((height, tm), lambda i: (0, i))


def _params(sem):
    return pltpu.CompilerParams(dimension_semantics=sem, vmem_limit_bytes=VMEM_LIMIT)


def _rot(x, c, s1, s2):
    return x * c + pltpu.roll(x, 96, 1) * s1 + pltpu.roll(x, 32, 1) * s2


def _rot_bwd(d, c, s1, s2):
    return d * c + pltpu.roll(d * s1, 32, 1) + pltpu.roll(d * s2, 96, 1)


def _silu_parts(g):
    sg = jax.nn.sigmoid(g)
    return g * sg, sg * (1.0 + g * (1.0 - sg))


def _rms_bwd(dn, xh, r, g):
    a = dn * g
    return r * (a - xh * jnp.mean(a * xh, axis=-1, keepdims=True))


def _lane_lo(shape):
    return lax.broadcasted_iota(jnp.int32, shape, 1) < HEAD_DIM


def _split4(t):
    lo = _lane_lo(t.shape)
    tr = pltpu.roll(t, HEAD_DIM, 1)
    z = jnp.zeros_like(t)
    return jnp.concatenate([jnp.where(lo, t, z), jnp.where(lo, z, tr), jnp.where(lo, tr, z), jnp.where(lo, z, t)], axis=1)


def _stack_pairs(t, h):
    return jnp.concatenate([t[:, (h * 4 + j) * LANES:(h * 4 + j + 1) * LANES] for j in range(4)], axis=0)


def _upper():
    shape = (CHUNK, 4 * CHUNK)
    return lax.broadcasted_iota(jnp.int32, shape, 0) > (lax.broadcasted_iota(jnp.int32, shape, 1) & (CHUNK - 1))


def _band_rows(ref, prev, cur, h):
    a = slice(2 * h * LANES, (2 * h + 1) * LANES)
    b = slice((2 * h + 1) * LANES, (2 * h + 2) * LANES)
    return jnp.concatenate([ref[pl.ds(prev, CHUNK), a], ref[pl.ds(cur, CHUNK), a],
                            ref[pl.ds(prev, CHUNK), b], ref[pl.ds(cur, CHUNK), b]], axis=0)


def _band_cols(ref, pci, ci, h):
    a = slice(2 * h * LANES, (2 * h + 1) * LANES)
    b = slice((2 * h + 1) * LANES, (2 * h + 2) * LANES)
    return jnp.concatenate([ref[pci, a, :], ref[ci, a, :], ref[pci, b, :], ref[ci, b, :]], axis=1)


def _fold(t, upper, has_prev=None):
    out = []
    for k in range(2):
        prev = t[2 * k * CHUNK:(2 * k + 1) * CHUNK]
        if has_prev is not None:
            prev = jnp.where(has_prev, prev, NEG)
        out.append(jnp.where(upper, prev, t[(2 * k + 1) * CHUNK:(2 * k + 2) * CHUNK]))
    return out


def _unfold(fa, fb, upper):
    z = jnp.zeros_like(fa)
    return jnp.concatenate([jnp.where(upper, fa, z), jnp.where(upper, z, fa),
                            jnp.where(upper, fb, z), jnp.where(upper, z, fb)], axis=0)


def _softmax_sink(f, sink):
    m = jnp.maximum(jnp.max(f, axis=0, keepdims=True), sink)
    p = jnp.exp(f - m)
    es = jnp.exp(sink - m)
    inv = 1.0 / (jnp.sum(p, axis=0, keepdims=True) + es)
    return p * inv, es * inv


class _Riding:
    def __init__(self, shards, gathered, stages, sems, n_steps):
        self.shards, self.stages, self.n_steps = shards, stages, n_steps
        ssem, rsem, lsem = sems
        self.gathers = [_TwoLevel(stages[k], gathered[k], ssem.at[k], rsem.at[k], lsem.at[k])
                        for k in range(len(shards))]

    def begin(self, i):
        @pl.when(i == 0)
        def _():
            for shard, stage, g in zip(self.shards, self.stages, self.gathers):
                stage[...] = shard[...].astype(stage.dtype)
                g.start()

    def end(self, i):
        @pl.when(i == self.n_steps // 2)
        def _():
            for g in self.gathers:
                g.forward()

        @pl.when(i == self.n_steps - 1)
        def _():
            for g in self.gathers:
                g.finish()

    @staticmethod
    def specs(later):
        nl = len(later)
        hbm = pl.BlockSpec(memory_space=pl.ANY)
        return ([_const_spec(w.shape) for w in later], [hbm] * nl,
                tuple(jax.ShapeDtypeStruct((N_DEV,) + w.shape, BF) for w in later),
                [pltpu.VMEM(w.shape, BF) for w in later] + _direct_sems(nl))


PASS_MASKS = ((0, 1, 2, 5, 4, 3, 6, 7), (0, 1, 4, 3, 2, 5, 6, 7))


def _in_proj(x, w_shard, vec_shard, slots, later):
    T, D = x.shape
    SH = w_shard.shape[1]
    TM = min(1024, T)
    nT = T // TM
    nl = len(later)
    ds = D // N_DEV
    last = N_DEV - 1

    def body(slots_ref, x_ref, wsh_ref, vsh_ref, *rest):
        shards, rest = rest[:nl], rest[nl:]
        (z_ref, wout_ref, vout_ref), rest = rest[:3], rest[3:]
        gathered, rest = rest[:nl], rest[nl:]
        (w_scr, vec_scr, vstage, n1_scr, ga_scr, w_s, w_r, w_l, v_s, v_r, v_l), rest = rest[:11], rest[11:]
        stages, sems = rest[:nl], rest[nl:]
        p, i = pl.program_id(0), pl.program_id(1)
        me = _my_index()
        wg = _RelayGather(w_scr, w_s, w_r)
        vg = _Direct(vstage, vec_scr, v_s, v_r, v_l, scatter=False)
        lg = [_TwoLevel(stages[k], gathered[k], sems[0].at[k], sems[1].at[k], sems[2].at[k]) for k in range(nl)]
        w_copy = pltpu.make_async_copy(w_scr, wout_ref, w_l)

        def at_pass(k):
            return (p == k) & (i == 0)

        c = lax.axis_index("c")

        @pl.when(at_pass(0))
        def _():
            vstage[...] = vsh_ref[...]
            vg.start()
            w_scr[me] = wsh_ref[...].astype(BF)
            wg.send_own(0).start()

            @pl.when(c == 1)
            def _():
                wg.send_own(1).start()

            @pl.when(c == 0)
            def _():
                wg.send_own(2).start()

            vg.finish()
            for j in range(N_DEV):
                ga_scr[:, j * ds:(j + 1) * ds] = vec_scr[j, 0:1, 0:ds]
            vout_ref[...] = vec_scr[...]

        @pl.when(at_pass(1))
        def _():
            wg.wait_sibling()

        for first, second, landed_first, landed_second in ((1, 2, wg.on_x, wg.on_y), (2, 1, wg.on_y, wg.on_x)):
            mine = c == (1 if first == 1 else 0)

            @pl.when(at_pass(2) & mine)
            def _(second=second, landed_first=landed_first):
                wg.send_own(second).start()
                landed_first()

            @pl.when(at_pass(3) & mine)
            def _(second=second):
                wg.wait_passed(second - 1)

            @pl.when(at_pass(4) & mine)
            def _(landed_second=landed_second):
                landed_second()

            @pl.when(at_pass(5) & mine)
            def _(first=first):
                wg.wait_passed(first - 1)

        @pl.when(at_pass(4))
        def _():
            for k in range(nl):
                stages[k][...] = shards[k][...].astype(BF)
                lg[k].start()

        @pl.when(at_pass(6))
        def _():
            wg.on_diag()

        @pl.when(at_pass(7))
        def _():
            wg.wait_passed(2)

        @pl.when(at_pass(last))
        def _():
            w_copy.start()

        @pl.when(p == 0)
        def _():
            xv = x_ref[...]
            r1 = lax.rsqrt(jnp.mean(xv * xv, axis=-1, keepdims=True) + EPS)
            n1_scr[i] = (xv * r1 * ga_scr[...]).astype(BF)

        z_ref[...] = _dot(n1_scr[i], w_scr[slots_ref[p]]).astype(BF)

        @pl.when((p == last) & (i == nT - 1))
        def _():
            wg.wait_sends()
            for g in lg:
                g.forward()
            for g in lg:
                g.finish()
            w_copy.wait()

    hbm = pl.BlockSpec(memory_space=pl.ANY)
    dma = pltpu.SemaphoreType.DMA
    S = jax.ShapeDtypeStruct
    grid_spec = pltpu.PrefetchScalarGridSpec(
        num_scalar_prefetch=1, grid=(N_DEV, nT),
        in_specs=[pl.BlockSpec((TM, D), lambda p, i, s: (jnp.where(p == 0, i, nT - 1), 0)),
                  pl.BlockSpec(w_shard.shape, lambda p, i, s: (0, 0), pipeline_mode=pl.Buffered(1)),
                  pl.BlockSpec(vec_shard.shape, lambda p, i, s: (0, 0), pipeline_mode=pl.Buffered(1))]
        + [pl.BlockSpec(w.shape, lambda p, i, s: (0, 0), pipeline_mode=pl.Buffered(1)) for w in later],
        out_specs=[pl.BlockSpec((TM, SH), lambda p, i, s: (i, s[p])), hbm,
                   pl.BlockSpec((N_DEV,) + vec_shard.shape, lambda p, i, s: (0, 0, 0))] + [hbm] * nl,
        scratch_shapes=[pltpu.VMEM((N_DEV, D, SH), BF), pltpu.VMEM((N_DEV,) + vec_shard.shape, F32),
                        pltpu.VMEM(vec_shard.shape, F32), pltpu.VMEM((nT, TM, D), BF), pltpu.VMEM((1, D), F32),
                        dma((8,)), dma((8,)), dma, dma((7,)), dma((7,)), dma]
        + [pltpu.VMEM(w.shape, BF) for w in later] + _direct_sems(nl))
    return pl.pallas_call(
        body, name="a_in_proj", grid_spec=grid_spec,
        out_shape=(S((T, N_DEV * SH), BF), S((N_DEV, D, SH), BF), S((N_DEV,) + vec_shard.shape, F32))
        + tuple(S((N_DEV,) + w.shape, BF) for w in later),
        compiler_params=_params(("arbitrary", "arbitrary")),
    )(slots, x, w_shard, vec_shard, *later)


def _a_fwd(x, z, ln_g, ln_b, ws, bs_t, wa_out, g_kv, w_kv, b_kv, rc, rs1, rs2, later):
    T, D = x.shape
    AW = wa_out.shape[0]
    G = ws.shape[0]
    TM = min(256, T)
    nT = T // TM
    nC = TM // CHUNK
    nl = len(later)

    def body(x_ref, u_ref, v_ref, gt_ref, lng_ref, lnb_ref, ws_ref, bst_ref, waout_ref, gkv_ref, wkv_ref, bkv_ref,
             rc_ref, rs1_ref, rs2_ref, *rest):
        shards, rest = rest[:nl], rest[nl:]
        (h1_ref, sv_ref, vhat_ref, rstd_ref, k4_ref, v4_ref, kt_ref, vt_ref), rest = rest[:8], rest[8:]
        gathered, sv_scr, stages, sems = rest[:nl], rest[nl], rest[nl + 1:2 * nl + 1], rest[2 * nl + 1:]
        i = pl.program_id(0)
        riding = _Riding(shards, gathered, stages, sems, nT)
        riding.begin(i)
        xv = x_ref[...]
        u = u_ref[...].astype(F32)
        v = v_ref[...].astype(F32)
        gt = gt_ref[...].astype(F32)
        mu = jnp.mean(v, axis=-1, keepdims=True)
        xc = v - mu
        rstd = lax.rsqrt(jnp.mean(xc * xc, axis=-1, keepdims=True) + EPS)
        vhat = xc * rstd
        vln = (vhat * lng_ref[...] + lnb_ref[...]).astype(BF)
        tri = lax.broadcasted_iota(jnp.int32, (CHUNK, CHUNK), 0) >= lax.broadcasted_iota(jnp.int32, (CHUNK, CHUNK), 1)
        for g in range(G):
            wsm = jnp.where(tri, ws_ref[g], 0.0).astype(BF)
            bias = bst_ref[:, g:g + 1]
            for c in range(nC):
                blk = vln[c * CHUNK:(c + 1) * CHUNK, g * CHUNK:(g + 1) * CHUNK]
                sv_scr[c * CHUNK:(c + 1) * CHUNK, g * CHUNK:(g + 1) * CHUNK] = _dot(wsm, blk) + bias
        sv = sv_scr[...]
        silu, _ = _silu_parts(gt)
        y = (u * sv * silu).astype(BF)
        h1 = xv + _dot(y, waout_ref[...])
        h1_ref[...] = h1
        sv_ref[...] = sv.astype(BF)
        vhat_ref[...] = vhat.astype(BF)
        rstd_ref[...] = jnp.broadcast_to(rstd, rstd_ref.shape)
        rkv = lax.rsqrt(jnp.mean(h1 * h1, axis=-1, keepdims=True) + EPS)
        nkv = (h1 * rkv * gkv_ref[...]).astype(BF)
        kv = _dot(nkv, wkv_ref[...]) + bkv_ref[...]
        k_rot = _rot(kv[:, :LANES], rc_ref[...], rs1_ref[...], rs2_ref[...])
        for src, ref, tref in ((k_rot, k4_ref, kt_ref), (kv[:, LANES:], v4_ref, vt_ref)):
            t4 = _split4(src)
            ref[...] = t4.astype(BF)
            for c in range(nC):
                for b in range(4):
                    blk = t4[c * CHUNK:(c + 1) * CHUNK, b * LANES:(b + 1) * LANES]
                    tref[c, b * LANES:(b + 1) * LANES, :] = blk.T.astype(BF)
        riding.end(i)

    row = functools.partial(_row_spec, TM)
    zcol = [pl.BlockSpec((TM, AW), functools.partial(lambda k, i: (i, k), k)) for k in range(3)]
    tr = pl.BlockSpec((nC, 4 * LANES, CHUNK), lambda i: (i, 0, 0))
    r_in, r_out, r_shape, r_scratch = _Riding.specs(later)
    S = jax.ShapeDtypeStruct
    return pl.pallas_call(
        body, name="a_fwd", grid=(nT,),
        in_specs=[row(D)] + zcol + [_const_spec((1, AW)), _const_spec((1, AW)),
                  _const_spec(ws.shape), _const_spec(bs_t.shape), _const_spec(wa_out.shape), _const_spec((1, D)),
                  _const_spec(w_kv.shape), _const_spec((1, 2 * LANES)), row(LANES), row(LANES), row(LANES)] + r_in,
        out_specs=[row(D), row(AW), row(AW), row(LANES), row(4 * LANES), row(4 * LANES), tr, tr] + r_out,
        out_shape=(S((T, D), F32), S((T, AW), BF), S((T, AW), BF), S((T, LANES), F32),
                   S((T, 4 * LANES), BF), S((T, 4 * LANES), BF),
                   S((T // CHUNK, 4 * LANES, CHUNK), BF), S((T // CHUNK, 4 * LANES, CHUNK), BF)) + r_shape,
        scratch_shapes=[pltpu.VMEM((TM, AW), F32)] + r_scratch,
        compiler_params=_params(("arbitrary",)),
    )(x, z, z, z, ln_g, ln_b, ws, bs_t, wa_out, g_kv, w_kv, b_kv, rc, rs1, rs2, *later)


def _b_fwd(h1, g_b, wb_in, bq, rc, rs1, rs2, k4, vt, sinks, wb_out, g_f, target):
    T, D = h1.shape
    BW = wb_out.shape[0]
    SH = wb_in.shape[2]
    TM = min(512, T)
    nC = TM // CHUNK
    nP = BW // LANES

    def body(h1_ref, gb_ref, wbin_ref, bq_ref, rc_ref, rs1_ref, rs2_ref, k4_ref, vt_ref, sink_ref, wbout_ref, gf_ref,
             tgt_ref, q_ref, g2_ref, o_ref, dh2_ref, dh2b_ref, loss_ref, dgf_ref, z_scr, o_scr):
        i = pl.program_id(0)

        @pl.when(i == 0)
        def _():
            loss_ref[...] = jnp.zeros_like(loss_ref)
            dgf_ref[...] = jnp.zeros_like(dgf_ref)

        h1v = h1_ref[...]
        r2 = lax.rsqrt(jnp.mean(h1v * h1v, axis=-1, keepdims=True) + EPS)
        n2 = (h1v * r2 * gb_ref[...]).astype(BF)
        for j in range(N_DEV):
            z_scr[:, j * SH:(j + 1) * SH] = _dot(n2, wbin_ref[j])
        c_t, s1_t, s2_t = rc_ref[...], rs1_ref[...], rs2_ref[...]
        for p in range(nP):
            cols = slice(p * LANES, (p + 1) * LANES)
            qp = _rot(z_scr[:, cols] + bq_ref[:, cols], c_t, s1_t, s2_t) * (HEAD_DIM ** -0.5)
            q_ref[:, cols] = qp.astype(BF)
        g2 = z_scr[:, BW:]
        g2_ref[...] = g2.astype(BF)
        upper = _upper()
        for c in range(nC):
            ci = i * nC + c
            rows = slice(c * CHUNK, (c + 1) * CHUNK)
            pci = jnp.maximum(ci - 1, 0)
            prev = pl.multiple_of(pci * CHUNK, CHUNK)
            cur = pl.multiple_of(ci * CHUNK, CHUNK)
            qc = q_ref[rows, :]
            for h in range(2):
                st = _dot_nt(_band_rows(k4_ref, prev, cur, h), _stack_pairs(qc, h))
                fa, fb = _fold(st, upper, ci > 0)
                pa, _ = _softmax_sink(fa, sink_ref[2 * h:2 * h + 1, :])
                pb, _ = _softmax_sink(fb, sink_ref[2 * h + 1:2 * h + 2, :])
                ot = _dot(_band_cols(vt_ref, pci, ci, h), _unfold(pa, pb, upper).astype(BF))
                for j in range(4):
                    o_scr[rows, (h * 4 + j) * LANES:(h * 4 + j + 1) * LANES] = ot[:, j * CHUNK:(j + 1) * CHUNK].T
        o = o_scr[...]
        o_ref[...] = o.astype(BF)
        silu, _ = _silu_parts(g2)
        h2 = h1v + _dot((o * silu).astype(BF), wbout_ref[...])
        rf = lax.rsqrt(jnp.mean(h2 * h2, axis=-1, keepdims=True) + EPS)
        xh = h2 * rf
        gf = gf_ref[...]
        err = xh * gf - tgt_ref[...]
        dyf = err * (1.0 / D)
        dh2 = _rms_bwd(dyf, xh, rf, gf)
        dh2_ref[...] = dh2
        dh2b_ref[...] = dh2.astype(BF)
        loss_ref[...] += 0.5 * jnp.sum(jnp.mean(err * err, axis=-1, keepdims=True), axis=0, keepdims=True)
        dgf_ref[...] += jnp.sum(dyf * xh, axis=0, keepdims=True)

    row = functools.partial(_row_spec, TM)
    S = jax.ShapeDtypeStruct
    return pl.pallas_call(
        body, name="b_fwd", grid=(T // TM,),
        in_specs=[row(D), _const_spec((1, D)), _const_spec(wb_in.shape), _const_spec((1, BW)), row(LANES), row(LANES),
                  row(LANES), _const_spec(k4.shape), _const_spec(vt.shape), _const_spec(sinks.shape),
                  _const_spec(wb_out.shape), _const_spec((1, D)), row(D)],
        out_specs=[row(BW), row(BW), row(BW), row(D), row(D), _acc_spec((1, 1)), _acc_spec((1, D))],
        out_shape=(S((T, BW), BF), S((T, BW), BF), S((T, BW), BF), S((T, D), F32), S((T, D), BF), S((1, 1), F32),
                   S((1, D), F32)),
        scratch_shapes=[pltpu.VMEM((TM, 2 * BW), F32), pltpu.VMEM((TM, BW), F32)],
        compiler_params=_params(("arbitrary",)),
    )(h1, g_b, wb_in, bq, rc, rs1, rs2, k4, vt, sinks, wb_out, g_f, target)


def _b_bwd(dh2, h1, q, g2, o, k4, v4, kt, sinks, wb_out, wb_in, g_b, rc, rs1, rs2):
    T, D = h1.shape
    BW = wb_out.shape[0]
    SH = wb_in.shape[2]
    TM = min(256, T)
    nT = T // TM
    nC = TM // CHUNK
    nP = BW // LANES

    def body(dh2_ref, h1_ref, q_ref, g2_ref, o_ref, k4_ref, v4_ref, kt_ref, sink_ref, wbout_ref, wbin_ref, gb_ref,
             rc_ref, rs1_ref, rs2_ref,
             dh1_ref, dz2_ref, n2_ref, y2_ref, dk_ref, dv_ref, dbq_ref, dgb_ref, dsink_ref, do_scr, dq_scr, dsacc_scr):
        i = pl.program_id(0)

        @pl.when(i == 0)
        def _():
            dk_ref[...] = jnp.zeros_like(dk_ref)
            dv_ref[...] = jnp.zeros_like(dv_ref)
            dbq_ref[...] = jnp.zeros_like(dbq_ref)
            dgb_ref[...] = jnp.zeros_like(dgb_ref)
            dsacc_scr[...] = jnp.zeros_like(dsacc_scr)

        dh2 = dh2_ref[...]
        dy2 = _dot_nt(dh2.astype(BF), wbout_ref[...])
        silu, dsilu = _silu_parts(g2_ref[...].astype(F32))
        do_scr[...] = (dy2 * silu).astype(BF)
        dy2, silu, dsilu = dy2.astype(BF), silu.astype(BF), dsilu.astype(BF)
        ob = o_ref[...]
        y2_ref[...] = (ob * silu).T
        dz2_ref[:, BW:] = dy2 * ob * dsilu
        upper = _upper()
        lo = _lane_lo((2 * CHUNK, LANES))
        for c in range(nC):
            ci = i * nC + c
            rows = slice(c * CHUNK, (c + 1) * CHUNK)
            pci = jnp.maximum(ci - 1, 0)
            prev = pl.multiple_of(pci * CHUNK, CHUNK)
            cur = pl.multiple_of(ci * CHUNK, CHUNK)
            qc = q_ref[rows, :]
            doc = do_scr[rows, :]
            dkb = jnp.zeros((2 * CHUNK, LANES), F32)
            dvb = jnp.zeros((2 * CHUNK, LANES), F32)
            for h in range(2):
                qs = _stack_pairs(qc, h)
                dos = _stack_pairs(doc, h)
                fa, fb = _fold(_dot_nt(_band_rows(k4_ref, prev, cur, h), qs), upper, ci > 0)
                dfa, dfb = _fold(_dot_nt(_band_rows(v4_ref, prev, cur, h), dos), upper)
                folded = []
                for k, (f, df) in enumerate(((fa, dfa), (fb, dfb))):
                    p, ps = _softmax_sink(f, sink_ref[2 * h + k:2 * h + k + 1, :])
                    delta = jnp.sum(p * df, axis=0, keepdims=True)
                    dsacc_scr[2 * h + k:2 * h + k + 1, :] -= ps * delta
                    folded.append((p, p * (df - delta)))
                pt = _unfold(folded[0][0], folded[1][0], upper).astype(BF)
                dst = _unfold(folded[0][1], folded[1][1], upper).astype(BF)
                dqt = _dot(_band_cols(kt_ref, pci, ci, h), dst)
                for j in range(4):
                    dq_scr[rows, (h * 4 + j) * LANES:(h * 4 + j + 1) * LANES] = dqt[:, j * CHUNK:(j + 1) * CHUNK].T
                for acc_name, g in (("k", _dot(dst, qs)), ("v", _dot(pt, dos))):
                    a, b = g[:2 * CHUNK], g[2 * CHUNK:]
                    if h == 0:
                        part = jnp.where(lo, a + pltpu.roll(b, HEAD_DIM, 1), 0.0)
                    else:
                        part = jnp.where(lo, 0.0, pltpu.roll(a, HEAD_DIM, 1) + b)
                    if acc_name == "k":
                        dkb += part
                    else:
                        dvb += part
            dk_ref[pl.ds(prev, CHUNK), :] += dkb[:CHUNK]
            dk_ref[pl.ds(cur, CHUNK), :] += dkb[CHUNK:]
            dv_ref[pl.ds(prev, CHUNK), :] += dvb[:CHUNK]
            dv_ref[pl.ds(cur, CHUNK), :] += dvb[CHUNK:]
        c_t, s1_t, s2_t = rc_ref[...], rs1_ref[...], rs2_ref[...]
        for p in range(nP):
            cols = slice(p * LANES, (p + 1) * LANES)
            dqp = _rot_bwd(dq_scr[:, cols] * (HEAD_DIM ** -0.5), c_t, s1_t, s2_t)
            dbq_ref[:, cols] += jnp.sum(dqp, axis=0, keepdims=True)
            dz2_ref[:, cols] = dqp.astype(BF)
        h1v = h1_ref[...]
        r2 = lax.rsqrt(jnp.mean(h1v * h1v, axis=-1, keepdims=True) + EPS)
        xh = h1v * r2
        gb = gb_ref[...]
        n2_ref[...] = (xh * gb).astype(BF).T
        dn2 = None
        for j in range(N_DEV):
            part = _dot_nt(dz2_ref[:, j * SH:(j + 1) * SH], wbin_ref[j])
            dn2 = part if dn2 is None else dn2 + part
        dgb_ref[...] += jnp.sum(dn2 * xh, axis=0, keepdims=True)
        dh1_ref[...] = dh2 + _rms_bwd(dn2, xh, r2, gb)

        @pl.when(i == nT - 1)
        def _():
            lane = lax.broadcasted_iota(jnp.int32, dsink_ref.shape, 1)
            tot = jnp.zeros(dsink_ref.shape, F32)
            for j in range(4):
                tot += jnp.where(lane == j, jnp.sum(dsacc_scr[:, j * CHUNK:(j + 1) * CHUNK], axis=1, keepdims=True), 0.0)
            dsink_ref[...] = tot

    row = functools.partial(_row_spec, TM)
    S = jax.ShapeDtypeStruct
    return pl.pallas_call(
        body, name="b_bwd", grid=(T // TM,),
        in_specs=[row(D), row(D), row(BW), row(BW), row(BW), _const_spec(k4.shape), _const_spec(v4.shape),
                  _const_spec(kt.shape), _const_spec(sinks.shape), _const_spec(wb_out.shape), _const_spec(wb_in.shape),
                  _const_spec((1, D)), row(LANES), row(LANES), row(LANES)],
        out_specs=[row(D), row(2 * BW), _col_spec(TM, D), _col_spec(TM, BW), _acc_spec((T, LANES)),
                   _acc_spec((T, LANES)), _acc_spec((1, BW)), _acc_spec((1, D)), _acc_spec((4, LANES))],
        out_shape=(S((T, D), F32), S((T, 2 * BW), BF), S((D, T), BF), S((BW, T), BF), S((T, LANES), F32),
                   S((T, LANES), F32), S((1, BW), F32), S((1, D), F32), S((4, LANES), F32)),
        scratch_shapes=[pltpu.VMEM((TM, BW), BF), pltpu.VMEM((TM, BW), F32), pltpu.VMEM((4, 4 * CHUNK), F32)],
        compiler_params=_params(("arbitrary",)),
    )(dh2, h1, q, g2, o, k4, v4, kt, sinks, wb_out, wb_in, g_b, rc, rs1, rs2)


def _a_bwd(dh1p, dk, dv, h1, g_kv, w_kv, wa_out, ws, ln_g, ln_b, z, sv, vhat, rstd, rc, rs1, rs2, ready):
    T, D = h1.shape
    AW = wa_out.shape[0]
    G = ws.shape[0]
    TM = min(256, T)
    nT = T // TM
    nC = TM // CHUNK
    nr = len(ready)

    def body(dh1p_ref, dk_ref, dv_ref, h1_ref, gkv_ref, wkv_ref, waout_ref, ws_ref, lng_ref,
             lnb_ref, u_ref, gt_ref, sv_ref, vhat_ref, rstd_ref, rc_ref, rs1_ref, rs2_ref, *rest):
        ready_refs, rest = rest[:nr], rest[nr:]
        (dz_ref, gwo_ref, gwk_ref, dh1f_ref, dgkv_ref, dbkv_ref, dlng_ref, dlnb_ref,
         dws_ref, dbs_ref), rest = rest[:10], rest[10:]
        recv_refs, (dsv_scr, dvln_scr, acco_scr, acck_scr, ssem, rsem, lsem) = rest[:nr], rest[nr:]
        i = pl.program_id(0)
        exchanges = [_Direct(ready_refs[k], recv_refs[k], ssem.at[k], rsem.at[k], lsem.at[k], scatter=True)
                     for k in range(nr)]

        @pl.when(i == 0)
        def _():
            for e in exchanges:
                e.start()
            for r in (dgkv_ref, dbkv_ref, dlng_ref, dlnb_ref, dws_ref, dbs_ref, acco_scr, acck_scr):
                r[...] = jnp.zeros_like(r)

        dk_pre = _rot_bwd(dk_ref[...], rc_ref[...], rs1_ref[...], rs2_ref[...])
        dkv = jnp.concatenate([dk_pre, dv_ref[...]], axis=1)
        dbkv_ref[...] += jnp.sum(dkv, axis=0, keepdims=True)
        dkv_b = dkv.astype(BF)
        h1v = h1_ref[...]
        rkv = lax.rsqrt(jnp.mean(h1v * h1v, axis=-1, keepdims=True) + EPS)
        xh_kv = h1v * rkv
        gkv = gkv_ref[...]
        acck_scr[...] += _dot((xh_kv * gkv).astype(BF).T, dkv_b)
        dnkv = _dot_nt(dkv_b, wkv_ref[...])
        dgkv_ref[...] += jnp.sum(dnkv * xh_kv, axis=0, keepdims=True)
        dh1 = dh1p_ref[...] + _rms_bwd(dnkv, xh_kv, rkv, gkv)
        dh1_b = dh1.astype(BF)
        dh1f_ref[...] = dh1
        dy = _dot_nt(dh1_b, waout_ref[...]).astype(BF)
        silu, dsilu = _silu_parts(gt_ref[...].astype(F32))
        silu, dsilu = silu.astype(BF), dsilu.astype(BF)
        ub, svb = u_ref[...], sv_ref[...]
        us = ub * silu
        dys = dy * svb
        acco_scr[...] += _dot((us * svb).T, dh1_b)
        dz_ref[:, :AW] = dys * silu
        dz_ref[:, 2 * AW:] = dys * ub * dsilu
        dsv_scr[...] = dy * us
        vhat_v = vhat_ref[...].astype(F32)
        lng = lng_ref[...]
        vln_b = (vhat_v * lng + lnb_ref[...]).astype(BF)
        tri = lax.broadcasted_iota(jnp.int32, (CHUNK, CHUNK), 0) >= lax.broadcasted_iota(jnp.int32, (CHUNK, CHUNK), 1)
        lane = lax.broadcasted_iota(jnp.int32, (CHUNK, LANES), 1)
        dbs = jnp.zeros((CHUNK, LANES), F32)
        for g in range(G):
            wsm = jnp.where(tri, ws_ref[g], 0.0).astype(BF)
            cols = slice(g * CHUNK, (g + 1) * CHUNK)
            dws_g = None
            for c in range(nC):
                rows = slice(c * CHUNK, (c + 1) * CHUNK)
                dsv_cg = dsv_scr[rows, cols]
                dvln_scr[rows, cols] = _dot_tn(wsm, dsv_cg)
                part = _dot_nt(dsv_cg, vln_b[rows, cols])
                dws_g = part if dws_g is None else dws_g + part
                dbs += jnp.where(lane == g, jnp.sum(dsv_cg.astype(F32), axis=-1, keepdims=True), 0.0)
            dws_ref[g] += jnp.where(tri, dws_g, 0.0)
        dbs_ref[...] += dbs
        dvln = dvln_scr[...]
        dlng_ref[...] += jnp.sum(dvln * vhat_v, axis=0, keepdims=True)
        dlnb_ref[...] += jnp.sum(dvln, axis=0, keepdims=True)
        a = dvln * lng
        dvv = rstd_ref[:, 0:1] * (a - jnp.mean(a, axis=-1, keepdims=True)
                                  - vhat_v * jnp.mean(a * vhat_v, axis=-1, keepdims=True))
        dz_ref[:, AW:2 * AW] = dvv.astype(BF)

        @pl.when(i == nT - 1)
        def _():
            for j in range(N_DEV):
                gwo_ref[j] = acco_scr[j * (AW // N_DEV):(j + 1) * (AW // N_DEV)].astype(BF)
                gwk_ref[j] = acck_scr[j * (D // N_DEV):(j + 1) * (D // N_DEV)].astype(BF)
            for e in exchanges:
                e.finish()

    row = functools.partial(_row_spec, TM)
    hbm = pl.BlockSpec(memory_space=pl.ANY)
    S = jax.ShapeDtypeStruct
    gwo_shape, gwk_shape = (N_DEV, AW // N_DEV, D), (N_DEV, D // N_DEV, 2 * LANES)
    return pl.pallas_call(
        body, name="a_bwd", grid=(nT,),
        in_specs=[row(D), row(LANES), row(LANES), row(D), _const_spec((1, D)), _const_spec(w_kv.shape),
                  _const_spec(wa_out.shape), _const_spec(ws.shape),
                  _const_spec((1, AW)), _const_spec((1, AW)), pl.BlockSpec((TM, AW), lambda i: (i, 0)),
                  pl.BlockSpec((TM, AW), lambda i: (i, 2)), row(AW), row(AW), row(LANES),
                  row(LANES), row(LANES), row(LANES)] + [hbm] * nr,
        out_specs=[row(3 * AW), _const_spec(gwo_shape), _const_spec(gwk_shape), row(D),
                   _acc_spec((1, D)), _acc_spec((1, 2 * LANES)), _acc_spec((1, AW)),
                   _acc_spec((1, AW)), _acc_spec(ws.shape), _acc_spec((CHUNK, LANES))] + [hbm] * nr,
        out_shape=(S((T, 3 * AW), BF), S(gwo_shape, BF), S(gwk_shape, BF), S((T, D), F32),
                   S((1, D), F32), S((1, 2 * LANES), F32), S((1, AW), F32), S((1, AW), F32),
                   S(ws.shape, F32), S((CHUNK, LANES), F32)) + tuple(S(r.shape, r.dtype) for r in ready),
        scratch_shapes=[pltpu.VMEM((TM, AW), BF), pltpu.VMEM((TM, AW), F32), pltpu.VMEM((AW, D), F32),
                        pltpu.VMEM((D, 2 * LANES), F32)] + _direct_sems(nr),
        compiler_params=_params(("arbitrary",)),
    )(dh1p, dk, dv, h1, g_kv, w_kv, wa_out, ws, ln_g, ln_b, z, z, sv, vhat, rstd, rc, rs1, rs2, *ready)


def _a_in_bwd(dz, wa_in, x, dh1, g_a, ready):
    T, D = x.shape
    SH = wa_in.shape[2]
    TM = min(512, T)
    nT = T // TM
    nr = len(ready)

    def body(dz_ref, wain_ref, x_ref, dh1_ref, ga_ref, *rest):
        ready_refs, (dx_ref, n1_ref, dga_ref), rest = rest[:nr], rest[nr:nr + 3], rest[nr + 3:]
        recv_refs, (ssem, rsem, lsem) = rest[:nr], rest[nr:]
        i = pl.program_id(0)
        exchanges = [_Direct(ready_refs[k], recv_refs[k], ssem.at[k], rsem.at[k], lsem.at[k], scatter=True)
                     for k in range(nr)]

        @pl.when(i == 0)
        def _():
            for e in exchanges:
                e.start()
            dga_ref[...] = jnp.zeros_like(dga_ref)

        xv = x_ref[...]
        r1 = lax.rsqrt(jnp.mean(xv * xv, axis=-1, keepdims=True) + EPS)
        xh = xv * r1
        ga = ga_ref[...]
        n1_ref[...] = (xh * ga).astype(BF).T
        dn1 = None
        for j in range(N_DEV):
            part = _dot_nt(dz_ref[:, j * SH:(j + 1) * SH], wain_ref[j])
            dn1 = part if dn1 is None else dn1 + part
        dga_ref[...] += jnp.sum(dn1 * xh, axis=0, keepdims=True)
        dx_ref[...] = dh1_ref[...] + _rms_bwd(dn1, xh, r1, ga)

        @pl.when(i == nT - 1)
        def _():
            for e in exchanges:
                e.finish()

    row = functools.partial(_row_spec, TM)
    hbm = pl.BlockSpec(memory_space=pl.ANY)
    S = jax.ShapeDtypeStruct
    return pl.pallas_call(
        body, name="a_in_bwd", grid=(nT,),
        in_specs=[row(dz.shape[1]), _const_spec(wa_in.shape), row(D), row(D), _const_spec((1, D))] + [hbm] * nr,
        out_specs=[row(D), _col_spec(TM, D), _acc_spec((1, D))] + [hbm] * nr,
        out_shape=(S((T, D), F32), S((D, T), BF), S((1, D), F32)) + tuple(S(r.shape, r.dtype) for r in ready),
        scratch_shapes=_direct_sems(nr),
        compiler_params=_params(("arbitrary",)),
    )(dz, wa_in, x, dh1, g_a, *ready)


def _wgrad(at, b, nblk, name, bt=512):
    K, T = at.shape
    N = b.shape[1] // nblk
    BT = min(bt, T)
    nt = T // BT

    def body(a_ref, b_ref, o_ref, acc):
        t = pl.program_id(0)

        @pl.when(t == 0)
        def _():
            acc[...] = jnp.zeros_like(acc)

        acc[...] += _dot(a_ref[...], b_ref[...])

        @pl.when(t == nt - 1)
        def _():
            for j in range(nblk):
                o_ref[j] = acc[:, j * N:(j + 1) * N].astype(BF)

    return pl.pallas_call(
        body, name=name, grid=(nt,),
        in_specs=[pl.BlockSpec((K, BT), lambda t: (0, t)), pl.BlockSpec((BT, nblk * N), lambda t: (t, 0))],
        out_specs=pl.BlockSpec((nblk, K, N), lambda t: (0, 0, 0)),
        out_shape=jax.ShapeDtypeStruct((nblk, K, N), BF),
        scratch_shapes=[pltpu.VMEM((K, nblk * N), F32)],
        compiler_params=_params(("arbitrary",)),
    )(at, b)


def _wgrad_exchange(a, b, me, small, name):
    K, T = a.shape
    N = b.shape[1] // N_DEV
    BT = T
    nt = T // BT
    last = N_DEV - 1
    n_chip = N_DEV // 2

    def far_of(k, core):
        return jnp.where((core == 0) & ((k == 1) | (k == 2)), k, n_chip - 1 - k)

    def block_of(s, me_i):
        k, odd = s // 2, s % 2
        core = me_i & 1
        return me_i ^ ((far_of(k, jnp.where(odd == 1, core, 1 - core)) << 1) | (1 - odd))

    def body(me_ref, a_ref, b_ref, small_ref, recv_ref, full_ref, *scratch):
        (acc, dstage, istage, half, d_s, d_r, i_s, i_r, lsem, parts_scr, red_scr, e_s, e_r, e_l, g_s, g_r,
         g_l) = scratch
        s, t = pl.program_id(0), pl.program_id(1)
        x, y, c = (lax.axis_index(ax) for ax in AXES)
        ex = [_Direct(small_ref, parts_scr, e_s, e_r, e_l, scatter=True)]
        regather = _TwoLevel(red_scr, full_ref, g_s, g_r, g_l)

        def to_sibling(k, slot):
            return pltpu.make_async_remote_copy(src_ref=dstage.at[slot], dst_ref=half.at[k], send_sem=d_s.at[k],
                                                recv_sem=d_r.at[k], device_id=(x, y, 1 - c), device_id_type=MESH)

        def to_chip(k, slot, sender):
            far = far_of(k, c)
            px, py = x ^ ((far >> 1) & 1), y ^ (far & 1)
            dst = recv_ref.at[2 * x + y] if sender else recv_ref.at[2 * px + py]
            return pltpu.make_async_remote_copy(src_ref=istage.at[slot], dst_ref=dst, send_sem=i_s.at[k],
                                                recv_sem=i_r.at[k], device_id=(px, py, c), device_id_type=MESH)

        @pl.when((s == 0) & (t == 0))
        def _():
            for e in ex:
                e.start()

        acc[...] = _dot(a_ref[...], b_ref[...])

        @pl.when(t == nt - 1)
        def _():
            k = lax.div(s, 2)
            slot = lax.rem(k, 2)

            @pl.when(lax.rem(s, 2) == 0)
            def _():
                @pl.when(k >= 2)
                def _():
                    to_sibling(k - 2, slot).wait_send()

                dstage[slot] = acc[...].astype(BF)
                to_sibling(k, slot).start()

            @pl.when(lax.rem(s, 2) == 1)
            def _():
                to_sibling(k, slot).wait_recv()

                @pl.when(k >= 2)
                def _():
                    to_chip(k - 2, slot, True).wait_send()

                istage[slot] = (acc[...] + half[k].astype(F32)).astype(BF)

                @pl.when(k < n_chip - 1)
                def _():
                    to_chip(k, slot, True).start()

            @pl.when(s == last)
            def _():
                own = pltpu.make_async_copy(istage.at[slot], recv_ref.at[2 * x + y], lsem)
                own.start()
                to_chip(n_chip - 2, 0, True).wait_send()
                to_sibling(n_chip - 2, 0).wait_send()
                to_sibling(n_chip - 1, 1).wait_send()
                for kk in range(n_chip - 1):
                    to_chip(kk, 0, False).wait_recv()
                own.wait()
                for e in ex:
                    e.finish()
                total = parts_scr[0]
                for dev in range(1, N_DEV):
                    total = total + parts_scr[dev]
                red_scr[...] = total
                regather.start()
                regather.forward()
                regather.finish()

    hbm = pl.BlockSpec(memory_space=pl.ANY)
    dma = pltpu.SemaphoreType.DMA
    grid_spec = pltpu.PrefetchScalarGridSpec(
        num_scalar_prefetch=1, grid=(N_DEV, nt),
        in_specs=[pl.BlockSpec((K, BT), lambda s, t, me_ref: (0, t), pipeline_mode=pl.Buffered(1)),
                  pl.BlockSpec((BT, N), lambda s, t, me_ref: (t, block_of(s, me_ref[0]))), hbm],
        out_specs=[hbm, hbm],
        scratch_shapes=[pltpu.VMEM((K, N), F32), pltpu.VMEM((2, K, N), BF), pltpu.VMEM((2, K, N), BF),
                        pltpu.VMEM((n_chip, K, N), BF), dma((n_chip,)), dma((n_chip,)), dma((n_chip - 1,)),
                        dma((n_chip - 1,)), dma, pltpu.VMEM(small.shape, F32), pltpu.VMEM(small.shape[1:], F32),
                        dma((last,)), dma((last,)), dma, dma((last,)), dma((last,)), dma])
    return pl.pallas_call(
        body, name=name, grid_spec=grid_spec,
        out_shape=[jax.ShapeDtypeStruct((n_chip, K, N), BF), jax.ShapeDtypeStruct(small.shape, F32)],
        compiler_params=_params(("arbitrary", "arbitrary")),
    )(me, a, b, small)


def _my_index():
    return 4 * lax.axis_index("x") + 2 * lax.axis_index("y") + lax.axis_index("c")


def _peer(mask):
    x, y, c = (lax.axis_index(a) for a in AXES)
    return (x ^ ((mask >> 2) & 1), y ^ ((mask >> 1) & 1), c ^ (mask & 1))


def _dev_index(p):
    return 4 * p[0] + 2 * p[1] + p[2]


class _Direct:
    def __init__(self, src, dst, send_sems, recv_sems, local_sem, scatter):
        me = _my_index()
        self.own = pltpu.make_async_copy(src.at[me] if scatter else src, dst.at[me], local_sem)
        self.sends, self.recvs = [], []
        for k in range(1, N_DEV):
            p = _peer(k)
            pi = _dev_index(p)
            sems = dict(send_sem=send_sems.at[k - 1], recv_sem=recv_sems.at[k - 1], device_id=p, device_id_type=MESH)
            self.sends.append(pltpu.make_async_remote_copy(src_ref=src.at[pi] if scatter else src, dst_ref=dst.at[me],
                                                           **sems))
            self.recvs.append(pltpu.make_async_remote_copy(src_ref=src.at[me] if scatter else src, dst_ref=dst.at[pi],
                                                           **sems))

    def start(self):
        self.own.start()
        for cp in self.sends:
            cp.start()

    def finish(self):
        for cp in self.sends:
            cp.wait_send()
        for cp in self.recvs:
            cp.wait_recv()
        self.own.wait()


class _TwoLevel:
    def __init__(self, src, dst, send_sems, recv_sems, local_sem, own=True):
        x, y, c = (lax.axis_index(a) for a in AXES)
        self.me, self.sibling = (x, y, c), (x, y, 1 - c)
        self.chips = [(1 - x, y), (x, 1 - y), (1 - x, 1 - y)]
        self.src, self.dst, self.send_sems, self.recv_sems = src, dst, send_sems, recv_sems
        self.own = pltpu.make_async_copy(src, dst.at[_dev_index(self.me)], local_sem) if own else None

    def _copy(self, k, block, to, from_src=False):
        slot = self.dst.at[_dev_index(block)]
        return pltpu.make_async_remote_copy(src_ref=self.src if from_src else slot, dst_ref=slot,
                                            send_sem=self.send_sems.at[k], recv_sem=self.recv_sems.at[k],
                                            device_id=to, device_id_type=MESH)

    def _firsts(self):
        c = self.me[2]
        return [self._copy(0, self.me, self.sibling, True)] + [self._copy(1 + j, self.me, (*chip, c), True)
                                                               for j, chip in enumerate(self.chips)]

    def _passed(self):
        c = self.me[2]
        return [self._copy(4 + j, (*chip, c), self.sibling) for j, chip in enumerate(self.chips)]

    def start(self):
        if self.own is not None:
            self.own.start()
        for cp in self._firsts():
            cp.start()

    def wait_sibling(self):
        self._copy(0, self.sibling, self.me).wait_recv()

    def wait_chip_and_forward(self, j):
        self._copy(1 + j, (*self.chips[j], self.me[2]), self.me).wait_recv()
        self._passed()[j].start()

    def wait_passed(self, j):
        self._copy(4 + j, (*self.chips[j], 1 - self.me[2]), self.me).wait_recv()

    def wait_sends(self):
        for cp in self._firsts() + self._passed():
            cp.wait_send()
        if self.own is not None:
            self.own.wait()

    def forward(self):
        for j in range(3):
            self.wait_chip_and_forward(j)

    def finish(self):
        self.wait_sibling()
        for j in range(3):
            self.wait_passed(j)
        self.wait_sends()


class _RelayGather:
    def __init__(self, dst, send_sems, recv_sems):
        x, y, c = (lax.axis_index(a) for a in AXES)
        self.c = c
        self.sib, self.xn, self.yn, self.dg = (x, y, 1 - c), (1 - x, y, c), (x, 1 - y, c), (1 - x, 1 - y, c)
        self.me = (x, y, c)
        self.dst, self.send_sems, self.recv_sems = dst, send_sems, recv_sems
        self.half = dst.shape[1] // 2

    def _slot(self, dev, part=None):
        i = _dev_index(dev)
        if part is None:
            return self.dst.at[i]
        return self.dst.at[i, pl.ds(part * self.half, self.half)]

    def _copy(self, k, dev, to, part=None):
        ref = self._slot(dev, part)
        return pltpu.make_async_remote_copy(src_ref=ref, dst_ref=ref, send_sem=self.send_sems.at[k],
                                            recv_sem=self.recv_sems.at[k], device_id=to, device_id_type=MESH)

    def _other(self, dev):
        return (dev[0], dev[1], 1 - self.c)

    def start(self):
        for k, to in enumerate((self.sib, self.xn, self.yn)):
            self._copy(k, self.me, to).start()

    def send_own(self, k):
        return self._copy(k, self.me, (self.sib, self.xn, self.yn)[k])

    def wait_sibling(self):
        self._copy(0, self.sib, self.me).wait_recv()

    def on_x(self):
        self._copy(1, self.xn, self.me).wait_recv()
        self._copy(3, self.xn, self.yn, part=0).start()
        self._copy(5, self.xn, self.sib).start()

    def on_y(self):
        self._copy(2, self.yn, self.me).wait_recv()
        self._copy(4, self.yn, self.xn, part=1).start()
        self._copy(6, self.yn, self.sib).start()

    def on_diag(self):
        self._copy(3, self.dg, self.me, part=0).wait_recv()
        self._copy(4, self.dg, self.me, part=1).wait_recv()
        self._copy(7, self.dg, self.sib).start()

    def wait_passed(self, j):
        self._copy(5 + j, self._other((self.xn, self.yn, self.dg)[j]), self.me).wait_recv()

    def wait_sends(self):
        for k, to in enumerate((self.sib, self.xn, self.yn)):
            self._copy(k, self.me, to).wait_send()
        self._copy(3, self.xn, self.yn, part=0).wait_send()
        self._copy(4, self.yn, self.xn, part=1).wait_send()
        for j, dev in enumerate((self.xn, self.yn, self.dg)):
            self._copy(5 + j, dev, self.sib).wait_send()


def _direct_sems(n):
    if n == 0:
        return []
    return [pltpu.SemaphoreType.DMA((n, 7)), pltpu.SemaphoreType.DMA((n, 7)), pltpu.SemaphoreType.DMA((n,))]


def _adam_math(w, g, m, v):
    m = ADAM_B1 * m + (1.0 - ADAM_B1) * g
    v = ADAM_B2 * v + (1.0 - ADAM_B2) * (g * g)
    m_hat = m / (1.0 - ADAM_B1 ** ADAM_STEP)
    v_hat = v / (1.0 - ADAM_B2 ** ADAM_STEP)
    delta = -ADAM_LR * (m_hat / (jnp.sqrt(v_hat) + ADAM_EPS) + ADAM_WD * w)
    return delta, m, v


def _sum_adam(parts, w, m, v, name):
    R, C = w.shape
    NP = parts.shape[0]
    BR = CHUNK if R % CHUNK == 0 else R

    def body(p_ref, w_ref, m_ref, v_ref, g_ref, d_ref, nm_ref, nv_ref):
        g = p_ref[0].astype(F32)
        for i in range(1, NP):
            g = g + p_ref[i].astype(F32)
        g_ref[...] = g
        d_ref[...], nm_ref[...], nv_ref[...] = _adam_math(w_ref[...], g, m_ref[...], v_ref[...])

    blk = pl.BlockSpec((BR, C), lambda i: (i, 0))
    S = jax.ShapeDtypeStruct((R, C), F32)
    return pl.pallas_call(
        body, name=name, grid=(R // BR,),
        in_specs=[pl.BlockSpec((NP, BR, C), lambda i: (0, i, 0)), blk, blk, blk],
        out_specs=[blk] * 4, out_shape=(S,) * 4,
        compiler_params=_params(("arbitrary",)),
    )(parts, w, m, v)


SUBLANES = 8


def _nrows(size):
    return -(-size // (SUBLANES * LANES)) * SUBLANES


def _view2d(a):
    return a.reshape(-1, LANES) if a.size % LANES == 0 else a.reshape(1, -1)


def _pack_small(parts, total_rows, name):
    arrs = [p[0] for p in parts]

    def body(*refs):
        out = refs[-1]
        out[...] = jnp.zeros_like(out)
        at = 0
        for ref, (a, rows, flag) in zip(refs[:-1], parts):
            val = ref[...].T if flag == "T" else ref[...]
            r, c = (rows, val.shape[1]) if flag == "T" else val.shape
            out[at:at + r, 0:c] = val[:r]
            at += _nrows(r * c)

    return pl.pallas_call(body, name=name, out_shape=jax.ShapeDtypeStruct((total_rows, LANES), F32))(*arrs)


def _small_update(full, me, reps, shards, name):
    n = len(reps) + len(shards)

    def body(me_ref, full_ref, *refs):
        ins, outs = refs[:3 * n], refs[3 * n:]
        at = 0
        for k in range(n):
            w_ref, m_ref, v_ref = ins[3 * k:3 * k + 3]
            r, c = w_ref.shape
            if k < len(reps):
                g = full_ref[at:at + r, 0:c]
                at += _nrows(r * c)
            else:
                seg = full_ref[at:at + N_DEV * r, :]
                row = lax.broadcasted_iota(jnp.int32, seg.shape, 0)
                pick = [jnp.sum(jnp.where(row == r * me_ref[0] + t, seg, 0.0), axis=0, keepdims=True) for t in range(r)]
                g = pick[0] if r == 1 else jnp.concatenate(pick, axis=0)
                at += N_DEV * r
            g_ref, d_ref, nm_ref, nv_ref = outs[4 * k:4 * k + 4]
            g_ref[...] = g
            d_ref[...], nm_ref[...], nv_ref[...] = _adam_math(w_ref[...], g, m_ref[...], v_ref[...])
        outs[4 * n][...] = full_ref[at:at + 1, 0:1]

    flat = [t for p in reps + shards for t in p]
    S = jax.ShapeDtypeStruct
    res = pl.pallas_call(
        body, name=name,
        in_specs=[pl.BlockSpec(memory_space=pltpu.SMEM)] + [pl.BlockSpec(memory_space=pltpu.VMEM)] * (1 + len(flat)),
        out_shape=[S(p[0].shape, F32) for p in reps + shards for _ in range(4)] + [S((1, 1), F32)],
    )(me, full, *flat)
    return [tuple(res[4 * k:4 * k + 4]) for k in range(n)], res[4 * n]


def _rope_tables(T):
    pos = np.arange(T, dtype=np.float32)
    inv_freq = (np.float64(ROPE_THETA) ** (-np.arange(0, HEAD_DIM, 2, dtype=np.float64) / HEAD_DIM)).astype(np.float32)
    ang = (pos[:, None] * inv_freq[None, :]).astype(np.float64)
    cos, sin, zero = np.cos(ang).astype(np.float32), np.sin(ang).astype(np.float32), np.zeros(ang.shape, np.float32)
    c = np.concatenate([cos, cos, cos, cos], axis=1)
    s1 = np.concatenate([-sin, zero, -sin, zero], axis=1)
    s2 = np.concatenate([zero, sin, zero, sin], axis=1)
    return jnp.asarray(c), jnp.asarray(s1), jnp.asarray(s2)


def kernel(x, a_norm_g, a_w_in, a_ln_g, a_ln_b, a_ws, a_bs, a_w_out, kv_norm_g, w_kv, b_kv, b_norm_g, b_w_in, b_bq, b_sinks, b_w_out, final_norm_g, loss_target, m_a_norm_g, m_a_w_in, m_a_ln_g, m_a_ln_b, m_a_ws, m_a_bs, m_a_w_out, m_kv_norm_g, m_w_kv, m_b_kv, m_b_norm_g, m_b_w_in, m_b_bq, m_b_sinks, m_b_w_out, m_final_norm_g, v_a_norm_g, v_a_w_in, v_a_ln_g, v_a_ln_b, v_a_ws, v_a_bs, v_a_w_out, v_kv_norm_g, v_w_kv, v_b_kv, v_b_norm_g, v_b_w_in, v_b_bq, v_b_sinks, v_b_w_out, v_final_norm_g):
    T, D = x.shape[1], x.shape[2]
    AW = a_ln_g.shape[1] * N_DEV
    G = a_ws.shape[1]
    assert w_kv.shape[1] == 2 * LANES and a_ws.shape[2] == CHUNK and T % CHUNK == 0
    me = _my_index()

    xs, tgt = x[0], loss_target[0]
    vec = jnp.concatenate([a_norm_g, a_ln_g, a_ln_b], axis=1)
    vec = jnp.broadcast_to(vec, (SUBLANES, vec.shape[1]))
    north = lax.axis_index("c") == 1
    slots = me ^ jnp.where(north, jnp.array(PASS_MASKS[1], jnp.int32), jnp.array(PASS_MASKS[0], jnp.int32))
    z, wa_in, vecs, wa_out, wkv = _in_proj(xs, a_w_in[0], vec, slots, [a_w_out[0], w_kv])
    wa_out = wa_out.reshape(AW, D)
    wkv = wkv.reshape(D, 2 * LANES)
    vecs = vecs[:, 0, :]
    ds = D // N_DEV
    g_a = vecs[:, :ds].reshape(1, D)
    ln_g = vecs[:, ds:ds + AW // N_DEV].reshape(1, AW)
    ln_b = vecs[:, ds + AW // N_DEV:].reshape(1, AW)

    rc, rs1, rs2 = _rope_tables(T)
    ws = a_ws[0]
    bs_t = a_bs[0].T
    g_kv = kv_norm_g.reshape(1, D)
    bkv = b_kv.reshape(1, -1)
    g_f = final_norm_g.reshape(1, D)
    sinks = jnp.repeat(b_sinks.reshape(2, 4, 2).transpose(0, 2, 1).reshape(4, 4), CHUNK, axis=1)
    h1, sv, vhat, rstd, k4, v4, kt, vt, wb_in, wb_out = _a_fwd(
        xs, z, ln_g, ln_b, ws, bs_t, wa_out, g_kv, wkv, bkv, rc, rs1, rs2, [b_w_in[0], b_w_out[0]])
    wb_out = wb_out.reshape(-1, D)
    q, g2, o, dh2, dh2_b, loss, d_gf = _b_fwd(h1, b_norm_g, wb_in, b_bq, rc, rs1, rs2, k4, vt, sinks, wb_out, g_f, tgt)
    dh1p, dz2, n2, y2, dk, dv, d_bq, d_gb, d_sink = _b_bwd(dh2, h1, q, g2, o, k4, v4, kt, sinks, wb_out, wb_in,
                                                           b_norm_g, rc, rs1, rs2)
    d_sink = d_sink[:, :4].reshape(2, 2, 4).transpose(0, 2, 1).reshape(1, 16)
    gw_b_in = _wgrad(n2, dz2, N_DEV, "wgrad_b_in", bt=1024)
    gw_b_out = _wgrad(y2, dh2_b, 1, "wgrad_b_out", bt=1024).reshape(N_DEV, -1, D)
    (dz, gw_a_out, gw_kv, dh1_f, d_gkv, d_bkv, d_lng, d_lnb, d_ws, d_bst, r_b_in, r_b_out) = _a_bwd(
        dh1p, dk, dv, h1, g_kv, wkv, wa_out, ws, ln_g, ln_b, z, sv, vhat, rstd, rc, rs1, rs2, [gw_b_in, gw_b_out])
    dx, n1, d_ga, r_a_out, r_kv = _a_in_bwd(dz, wa_in, xs, dh1_f, g_a, [gw_a_out, gw_kv])
    small = [(_view2d(d_ws), None, None), (d_bst, G, "T")] + [(_view2d(a), None, None) for a in (
        d_gkv, d_bkv, d_gb, d_bq, d_sink, d_gf, d_ga, d_lng, d_lnb, loss)]
    used = sum(_nrows(G * CHUNK if flag else a.size) for a, _, flag in small)
    per = -(-used // (SUBLANES * N_DEV)) * SUBLANES
    small_pack = _pack_small(small, per * N_DEV, "pack_small").reshape(N_DEV, per, LANES)
    r_a_in, full_small = _wgrad_exchange(n1, dz, me.reshape(1), small_pack, "wgrad_a_in")

    g_a_in, d_a_in, nm_a_in, nv_a_in = _sum_adam(r_a_in, a_w_in[0], m_a_w_in[0], v_a_w_in[0], "adam_a_in")
    g_a_out, d_a_out, nm_a_out, nv_a_out = _sum_adam(r_a_out, a_w_out[0], m_a_w_out[0], v_a_w_out[0], "adam_a_out")
    g_kvw, d_kvw, nm_kvw, nv_kvw = _sum_adam(r_kv, w_kv, m_w_kv, v_w_kv, "adam_kv")
    g_b_in, d_b_in, nm_b_in, nv_b_in = _sum_adam(r_b_in, b_w_in[0], m_b_w_in[0], v_b_w_in[0], "adam_b_in")
    g_b_out, d_b_out, nm_b_out, nv_b_out = _sum_adam(r_b_out, b_w_out[0], m_b_w_out[0], v_b_w_out[0], "adam_b_out")

    full_small = full_small.reshape(N_DEV * per, LANES)
    reps = [(a_ws, m_a_ws, v_a_ws), (a_bs, m_a_bs, v_a_bs), (kv_norm_g, m_kv_norm_g, v_kv_norm_g),
            (b_kv, m_b_kv, v_b_kv), (b_norm_g, m_b_norm_g, v_b_norm_g), (b_bq, m_b_bq, v_b_bq),
            (b_sinks, m_b_sinks, v_b_sinks), (final_norm_g, m_final_norm_g, v_final_norm_g)]
    shards = [(a_norm_g, m_a_norm_g, v_a_norm_g), (a_ln_g, m_a_ln_g, v_a_ln_g), (a_ln_b, m_a_ln_b, v_a_ln_b)]
    upd, loss = _small_update(full_small, me.reshape(1), [tuple(_view2d(t) for t in p) for p in reps],
                              [tuple(_view2d(t) for t in p) for p in shards], "adam_small")
    loss = loss[0, 0]
    sm_g, sd, snm, snv = ([upd[k][j].reshape(p[0].shape) for k, p in enumerate(reps + shards)] for j in range(4))

    def order(big, sm):
        a_in, a_out, kvw, b_in, b_out = big
        ws_, bs_, kvg, bkv_, bng, bq_, snk, fng, ang, alng, alnb = sm
        return (ang, a_in[None], alng, alnb, ws_, bs_, a_out[None], kvg, kvw, bkv_, bng, b_in[None], bq_, snk,
                b_out[None], fng)

    grads = order((g_a_in, g_a_out, g_kvw, g_b_in, g_b_out), sm_g)
    deltas = order((d_a_in, d_a_out, d_kvw, d_b_in, d_b_out), sd)
    new_m = order((nm_a_in, nm_a_out, nm_kvw, nm_b_in, nm_b_out), snm)
    new_v = order((nv_a_in, nv_a_out, nv_kvw, nv_b_in, nv_b_out), snv)
    return (loss, dx[None], *grads, *deltas, *new_m, *new_v)
```

```python
import functools

import jax
import jax.numpy as jnp
import numpy as np
from jax import lax
from jax.experimental import pallas as pl
from jax.experimental.pallas import tpu as pltpu

CHUNK = 128
HEAD_DIM = 64
ROPE_THETA = 10000.0
EPS = 1e-5
ADAM_LR = 0.001
ADAM_B1 = 0.9
ADAM_B2 = 0.999
ADAM_EPS = 1e-08
ADAM_WD = 0.01
ADAM_STEP = 10
N_DEV = 8
LANES = 128
NEG = -1e30

BF = jnp.bfloat16
F32 = jnp.float32
MESH = pl.DeviceIdType.MESH
AXES = ("x", "y", "c")
VMEM_LIMIT = 56 * 1024 * 1024


def _dot(a, b):
    return jnp.dot(a, b, preferred_element_type=F32)


def _dot_nt(a, b):
    return lax.dot_general(a, b, (((1,), (1,)), ((), ())), preferred_element_type=F32)


def _dot_tn(a, b):
    return lax.dot_general(a, b, (((0,), (0,)), ((), ())), preferred_element_type=F32)


def _const_spec(shape):
    nd = len(shape)
    return pl.BlockSpec(shape, lambda *_: (0,) * nd, pipeline_mode=pl.Buffered(1))


def _acc_spec(shape):
    nd = len(shape)
    return pl.BlockSpec(shape, lambda *_: (0,) * nd)


def _row_spec(tm, width):
    return pl.BlockSpec((tm, width), lambda i: (i, 0))


def _col_spec(tm, height):
    return pl.BlockSpec((height, tm), lambda i: (0, i))


def _params(sem):
    return pltpu.CompilerParams(dimension_semantics=sem, vmem_limit_bytes=VMEM_LIMIT)


def _rot(x, c, s1, s2):
    return x * c + pltpu.roll(x, 96, 1) * s1 + pltpu.roll(x, 32, 1) * s2


def _rot_bwd(d, c, s1, s2):
    return d * c + pltpu.roll(d * s1, 32, 1) + pltpu.roll(d * s2, 96, 1)


def _silu_parts(g):
    sg = jax.nn.sigmoid(g)
    return g * sg, sg * (1.0 + g * (1.0 - sg))


def _rms_bwd(dn, xh, r, g):
    a = dn * g
    return r * (a - xh * jnp.mean(a * xh, axis=-1, keepdims=True))


def _lane_lo(shape):
    return lax.broadcasted_iota(jnp.int32, shape, 1) < HEAD_DIM


def _split4(t):
    lo = _lane_lo(t.shape)
    tr = pltpu.roll(t, HEAD_DIM, 1)
    z = jnp.zeros_like(t)
    return jnp.concatenate([jnp.where(lo, t, z), jnp.where(lo, z, tr), jnp.where(lo, tr, z), jnp.where(lo, z, t)], axis=1)


def _stack_pairs(t, h):
    return jnp.concatenate([t[:, (h * 4 + j) * LANES:(h * 4 + j + 1) * LANES] for j in range(4)], axis=0)


def _upper():
    shape = (CHUNK, 4 * CHUNK)
    return lax.broadcasted_iota(jnp.int32, shape, 0) > (lax.broadcasted_iota(jnp.int32, shape, 1) & (CHUNK - 1))


def _band_rows(ref, prev, cur, h):
    a = slice(2 * h * LANES, (2 * h + 1) * LANES)
    b = slice((2 * h + 1) * LANES, (2 * h + 2) * LANES)
    return jnp.concatenate([ref[pl.ds(prev, CHUNK), a], ref[pl.ds(cur, CHUNK), a],
                            ref[pl.ds(prev, CHUNK), b], ref[pl.ds(cur, CHUNK), b]], axis=0)


def _band_cols(ref, pci, ci, h):
    a = slice(2 * h * LANES, (2 * h + 1) * LANES)
    b = slice((2 * h + 1) * LANES, (2 * h + 2) * LANES)
    return jnp.concatenate([ref[pci, a, :], ref[ci, a, :], ref[pci, b, :], ref[ci, b, :]], axis=1)


def _fold(t, upper, has_prev=None):
    out = []
    for k in range(2):
        prev = t[2 * k * CHUNK:(2 * k + 1) * CHUNK]
        if has_prev is not None:
            prev = jnp.where(has_prev, prev, NEG)
        out.append(jnp.where(upper, prev, t[(2 * k + 1) * CHUNK:(2 * k + 2) * CHUNK]))
    return out


def _unfold(fa, fb, upper):
    z = jnp.zeros_like(fa)
    return jnp.concatenate([jnp.where(upper, fa, z), jnp.where(upper, z, fa),
                            jnp.where(upper, fb, z), jnp.where(upper, z, fb)], axis=0)


def _softmax_sink(f, sink):
    m = jnp.maximum(jnp.max(f, axis=0, keepdims=True), sink)
    p = jnp.exp(f - m)
    es = jnp.exp(sink - m)
    inv = 1.0 / (jnp.sum(p, axis=0, keepdims=True) + es)
    return p * inv, es * inv


class _Riding:
    def __init__(self, shards, gathered, stages, sems, n_steps):
        self.shards, self.stages, self.n_steps = shards, stages, n_steps
        ssem, rsem, lsem = sems
        self.gathers = [_TwoLevel(stages[k], gathered[k], ssem.at[k], rsem.at[k], lsem.at[k])
                        for k in range(len(shards))]

    def begin(self, i):
        @pl.when(i == 0)
        def _():
            for shard, stage, g in zip(self.shards, self.stages, self.gathers):
                stage[...] = shard[...].astype(stage.dtype)
                g.start()

    def end(self, i):
        @pl.when(i == self.n_steps // 2)
        def _():
            for g in self.gathers:
                g.forward()

        @pl.when(i == self.n_steps - 1)
        def _():
            for g in self.gathers:
                g.finish()

    @staticmethod
    def specs(later):
        nl = len(later)
        hbm = pl.BlockSpec(memory_space=pl.ANY)
        return ([_const_spec(w.shape) for w in later], [hbm] * nl,
                tuple(jax.ShapeDtypeStruct((N_DEV,) + w.shape, BF) for w in later),
                [pltpu.VMEM(w.shape, BF) for w in later] + _direct_sems(nl))


PASS_MASKS = ((0, 1, 2, 5, 4, 3, 6, 7), (0, 1, 4, 3, 2, 5, 6, 7))


def _in_proj(x, w_shard, vec_shard, slots, later):
    T, D = x.shape
    SH = w_shard.shape[1]
    TM = min(1024, T)
    nT = T // TM
    nl = len(later)
    ds = D // N_DEV
    last = N_DEV - 1

    def body(slots_ref, x_ref, wsh_ref, vsh_ref, *rest):
        shards, rest = rest[:nl], rest[nl:]
        (z_ref, wout_ref, vout_ref), rest = rest[:3], rest[3:]
        gathered, rest = rest[:nl], rest[nl:]
        (w_scr, vec_scr, vstage, n1_scr, ga_scr, w_s, w_r, w_l, v_s, v_r, v_l), rest = rest[:11], rest[11:]
        stages, sems = rest[:nl], rest[nl:]
        p, i = pl.program_id(0), pl.program_id(1)
        me = _my_index()
        wg = _RelayGather(w_scr, w_s, w_r)
        vg = _Direct(vstage, vec_scr, v_s, v_r, v_l, scatter=False)
        lg = [_TwoLevel(stages[k], gathered[k], sems[0].at[k], sems[1].at[k], sems[2].at[k]) for k in range(nl)]
        w_copy = pltpu.make_async_copy(w_scr, wout_ref, w_l)

        def at_pass(k):
            return (p == k) & (i == 0)

        c = lax.axis_index("c")

        @pl.when(at_pass(0))
        def _():
            vstage[...] = vsh_ref[...]
            vg.start()
            w_scr[me] = wsh_ref[...].astype(BF)
            wg.send_own(0).start()

            @pl.when(c == 1)
            def _():
                wg.send_own(1).start()

            @pl.when(c == 0)
            def _():
                wg.send_own(2).start()

            vg.finish()
            for j in range(N_DEV):
                ga_scr[:, j * ds:(j + 1) * ds] = vec_scr[j, 0:1, 0:ds]
            vout_ref[...] = vec_scr[...]

        @pl.when(at_pass(1))
        def _():
            wg.wait_sibling()

        for first, second, landed_first, landed_second in ((1, 2, wg.on_x, wg.on_y), (2, 1, wg.on_y, wg.on_x)):
            mine = c == (1 if first == 1 else 0)

            @pl.when(at_pass(2) & mine)
            def _(second=second, landed_first=landed_first):
                wg.send_own(second).start()
                landed_first()

            @pl.when(at_pass(3) & mine)
            def _(second=second):
                wg.wait_passed(second - 1)

            @pl.when(at_pass(4) & mine)
            def _(landed_second=landed_second):
                landed_second()

            @pl.when(at_pass(5) & mine)
            def _(first=first):
                wg.wait_passed(first - 1)

        @pl.when(at_pass(4))
        def _():
            for k in range(nl):
                stages[k][...] = shards[k][...].astype(BF)
                lg[k].start()

        @pl.when(at_pass(6))
        def _():
            wg.on_diag()

        @pl.when(at_pass(7))
        def _():
            wg.wait_passed(2)

        @pl.when(at_pass(last))
        def _():
            w_copy.start()

        @pl.when(p == 0)
        def _():
            xv = x_ref[...]
            r1 = lax.rsqrt(jnp.mean(xv * xv, axis=-1, keepdims=True) + EPS)
            n1_scr[i] = (xv * r1 * ga_scr[...]).astype(BF)

        z_ref[...] = _dot(n1_scr[i], w_scr[slots_ref[p]]).astype(BF)

        @pl.when((p == last) & (i == nT - 1))
        def _():
            wg.wait_sends()
            for g in lg:
                g.forward()
            for g in lg:
                g.finish()
            w_copy.wait()

    hbm = pl.BlockSpec(memory_space=pl.ANY)
    dma = pltpu.SemaphoreType.DMA
    S = jax.ShapeDtypeStruct
    grid_spec = pltpu.PrefetchScalarGridSpec(
        num_scalar_prefetch=1, grid=(N_DEV, nT),
        in_specs=[pl.BlockSpec((TM, D), lambda p, i, s: (jnp.where(p == 0, i, nT - 1), 0)),
                  pl.BlockSpec(w_shard.shape, lambda p, i, s: (0, 0), pipeline_mode=pl.Buffered(1)),
                  pl.BlockSpec(vec_shard.shape, lambda p, i, s: (0, 0), pipeline_mode=pl.Buffered(1))]
        + [pl.BlockSpec(w.shape, lambda p, i, s: (0, 0), pipeline_mode=pl.Buffered(1)) for w in later],
        out_specs=[pl.BlockSpec((TM, SH), lambda p, i, s: (i, s[p])), hbm,
                   pl.BlockSpec((N_DEV,) + vec_shard.shape, lambda p, i, s: (0, 0, 0))] + [hbm] * nl,
        scratch_shapes=[pltpu.VMEM((N_DEV, D, SH), BF), pltpu.VMEM((N_DEV,) + vec_shard.shape, F32),
                        pltpu.VMEM(vec_shard.shape, F32), pltpu.VMEM((nT, TM, D), BF), pltpu.VMEM((1, D), F32),
                        dma((8,)), dma((8,)), dma, dma((7,)), dma((7,)), dma]
        + [pltpu.VMEM(w.shape, BF) for w in later] + _direct_sems(nl))
    return pl.pallas_call(
        body, name="a_in_proj", grid_spec=grid_spec,
        out_shape=(S((T, N_DEV * SH), BF), S((N_DEV, D, SH), BF), S((N_DEV,) + vec_shard.shape, F32))
        + tuple(S((N_DEV,) + w.shape, BF) for w in later),
        compiler_params=_params(("arbitrary", "arbitrary")),
    )(slots, x, w_shard, vec_shard, *later)


def _a_fwd(x, z, ln_g, ln_b, ws, bs_t, wa_out, g_kv, w_kv, b_kv, rc, rs1, rs2, later):
    T, D = x.shape
    AW = wa_out.shape[0]
    G = ws.shape[0]
    TM = min(256, T)
    nT = T // TM
    nC = TM // CHUNK
    nl = len(later)

    def body(x_ref, u_ref, v_ref, gt_ref, lng_ref, lnb_ref, ws_ref, bst_ref, waout_ref, gkv_ref, wkv_ref, bkv_ref,
             rc_ref, rs1_ref, rs2_ref, *rest):
        shards, rest = rest[:nl], rest[nl:]
        (h1_ref, sv_ref, vhat_ref, rstd_ref, k4_ref, v4_ref, kt_ref, vt_ref), rest = rest[:8], rest[8:]
        gathered, sv_scr, stages, sems = rest[:nl], rest[nl], rest[nl + 1:2 * nl + 1], rest[2 * nl + 1:]
        i = pl.program_id(0)
        riding = _Riding(shards, gathered, stages, sems, nT)
        riding.begin(i)
        xv = x_ref[...]
        u = u_ref[...].astype(F32)
        v = v_ref[...].astype(F32)
        gt = gt_ref[...].astype(F32)
        mu = jnp.mean(v, axis=-1, keepdims=True)
        xc = v - mu
        rstd = lax.rsqrt(jnp.mean(xc * xc, axis=-1, keepdims=True) + EPS)
        vhat = xc * rstd
        vln = (vhat * lng_ref[...] + lnb_ref[...]).astype(BF)
        tri = lax.broadcasted_iota(jnp.int32, (CHUNK, CHUNK), 0) >= lax.broadcasted_iota(jnp.int32, (CHUNK, CHUNK), 1)
        for g in range(G):
            wsm = jnp.where(tri, ws_ref[g], 0.0).astype(BF)
            bias = bst_ref[:, g:g + 1]
            for c in range(nC):
                blk = vln[c * CHUNK:(c + 1) * CHUNK, g * CHUNK:(g + 1) * CHUNK]
                sv_scr[c * CHUNK:(c + 1) * CHUNK, g * CHUNK:(g + 1) * CHUNK] = _dot(wsm, blk) + bias
        sv = sv_scr[...]
        silu, _ = _silu_parts(gt)
        y = (u * sv * silu).astype(BF)
        h1 = xv + _dot(y, waout_ref[...])
        h1_ref[...] = h1
        sv_ref[...] = sv.astype(BF)
        vhat_ref[...] = vhat.astype(BF)
        rstd_ref[...] = jnp.broadcast_to(rstd, rstd_ref.shape)
        rkv = lax.rsqrt(jnp.mean(h1 * h1, axis=-1, keepdims=True) + EPS)
        nkv = (h1 * rkv * gkv_ref[...]).astype(BF)
        kv = _dot(nkv, wkv_ref[...]) + bkv_ref[...]
        k_rot = _rot(kv[:, :LANES], rc_ref[...], rs1_ref[...], rs2_ref[...])
        for src, ref, tref in ((k_rot, k4_ref, kt_ref), (kv[:, LANES:], v4_ref, vt_ref)):
            t4 = _split4(src)
            ref[...] = t4.astype(BF)
            for c in range(nC):
                for b in range(4):
                    blk = t4[c * CHUNK:(c + 1) * CHUNK, b * LANES:(b + 1) * LANES]
                    tref[c, b * LANES:(b + 1) * LANES, :] = blk.T.astype(BF)
        riding.end(i)

    row = functools.partial(_row_spec, TM)
    zcol = [pl.BlockSpec((TM, AW), functools.partial(lambda k, i: (i, k), k)) for k in range(3)]
    tr = pl.BlockSpec((nC, 4 * LANES, CHUNK), lambda i: (i, 0, 0))
    r_in, r_out, r_shape, r_scratch = _Riding.specs(later)
    S = jax.ShapeDtypeStruct
    return pl.pallas_call(
        body, name="a_fwd", grid=(nT,),
        in_specs=[row(D)] + zcol + [_const_spec((1, AW)), _const_spec((1, AW)),
                  _const_spec(ws.shape), _const_spec(bs_t.shape), _const_spec(wa_out.shape), _const_spec((1, D)),
                  _const_spec(w_kv.shape), _const_spec((1, 2 * LANES)), row(LANES), row(LANES), row(LANES)] + r_in,
        out_specs=[row(D), row(AW), row(AW), row(LANES), row(4 * LANES), row(4 * LANES), tr, tr] + r_out,
        out_shape=(S((T, D), F32), S((T, AW), BF), S((T, AW), BF), S((T, LANES), F32),
                   S((T, 4 * LANES), BF), S((T, 4 * LANES), BF),
                   S((T // CHUNK, 4 * LANES, CHUNK), BF), S((T // CHUNK, 4 * LANES, CHUNK), BF)) + r_shape,
        scratch_shapes=[pltpu.VMEM((TM, AW), F32)] + r_scratch,
        compiler_params=_params(("arbitrary",)),
    )(x, z, z, z, ln_g, ln_b, ws, bs_t, wa_out, g_kv, w_kv, b_kv, rc, rs1, rs2, *later)


def _b_fwd(h1, g_b, wb_in, bq, rc, rs1, rs2, k4, vt, sinks, wb_out, g_f, target):
    T, D = h1.shape
    BW = wb_out.shape[0]
    SH = wb_in.shape[2]
    TM = min(512, T)
    nC = TM // CHUNK
    nP = BW // LANES

    def body(h1_ref, gb_ref, wbin_ref, bq_ref, rc_ref, rs1_ref, rs2_ref, k4_ref, vt_ref, sink_ref, wbout_ref, gf_ref,
             tgt_ref, q_ref, g2_ref, o_ref, dh2_ref, dh2b_ref, loss_ref, dgf_ref, z_scr, o_scr):
        i = pl.program_id(0)

        @pl.when(i == 0)
        def _():
            loss_ref[...] = jnp.zeros_like(loss_ref)
            dgf_ref[...] = jnp.zeros_like(dgf_ref)

        h1v = h1_ref[...]
        r2 = lax.rsqrt(jnp.mean(h1v * h1v, axis=-1, keepdims=True) + EPS)
        n2 = (h1v * r2 * gb_ref[...]).astype(BF)
        for j in range(N_DEV):
            z_scr[:, j * SH:(j + 1) * SH] = _dot(n2, wbin_ref[j])
        c_t, s1_t, s2_t = rc_ref[...], rs1_ref[...], rs2_ref[...]
        for p in range(nP):
            cols = slice(p * LANES, (p + 1) * LANES)
            qp = _rot(z_scr[:, cols] + bq_ref[:, cols], c_t, s1_t, s2_t) * (HEAD_DIM ** -0.5)
            q_ref[:, cols] = qp.astype(BF)
        g2 = z_scr[:, BW:]
        g2_ref[...] = g2.astype(BF)
        upper = _upper()
        for c in range(nC):
            ci = i * nC + c
            rows = slice(c * CHUNK, (c + 1) * CHUNK)
            pci = jnp.maximum(ci - 1, 0)
            prev = pl.multiple_of(pci * CHUNK, CHUNK)
            cur = pl.multiple_of(ci * CHUNK, CHUNK)
            qc = q_ref[rows, :]
            for h in range(2):
                st = _dot_nt(_band_rows(k4_ref, prev, cur, h), _stack_pairs(qc, h))
                fa, fb = _fold(st, upper, ci > 0)
                pa, _ = _softmax_sink(fa, sink_ref[2 * h:2 * h + 1, :])
                pb, _ = _softmax_sink(fb, sink_ref[2 * h + 1:2 * h + 2, :])
                ot = _dot(_band_cols(vt_ref, pci, ci, h), _unfold(pa, pb, upper).astype(BF))
                for j in range(4):
                    o_scr[rows, (h * 4 + j) * LANES:(h * 4 + j + 1) * LANES] = ot[:, j * CHUNK:(j + 1) * CHUNK].T
        o = o_scr[...]
        o_ref[...] = o.astype(BF)
        silu, _ = _silu_parts(g2)
        h2 = h1v + _dot((o * silu).astype(BF), wbout_ref[...])
        rf = lax.rsqrt(jnp.mean(h2 * h2, axis=-1, keepdims=True) + EPS)
        xh = h2 * rf
        gf = gf_ref[...]
        err = xh * gf - tgt_ref[...]
        dyf = err * (1.0 / D)
        dh2 = _rms_bwd(dyf, xh, rf, gf)
        dh2_ref[...] = dh2
        dh2b_ref[...] = dh2.astype(BF)
        loss_ref[...] += 0.5 * jnp.sum(jnp.mean(err * err, axis=-1, keepdims=True), axis=0, keepdims=True)
        dgf_ref[...] += jnp.sum(dyf * xh, axis=0, keepdims=True)

    row = functools.partial(_row_spec, TM)
    S = jax.ShapeDtypeStruct
    return pl.pallas_call(
        body, name="b_fwd", grid=(T // TM,),
        in_specs=[row(D), _const_spec((1, D)), _const_spec(wb_in.shape), _const_spec((1, BW)), row(LANES), row(LANES),
                  row(LANES), _const_spec(k4.shape), _const_spec(vt.shape), _const_spec(sinks.shape),
                  _const_spec(wb_out.shape), _const_spec((1, D)), row(D)],
        out_specs=[row(BW), row(BW), row(BW), row(D), row(D), _acc_spec((1, 1)), _acc_spec((1, D))],
        out_shape=(S((T, BW), BF), S((T, BW), BF), S((T, BW), BF), S((T, D), F32), S((T, D), BF), S((1, 1), F32),
                   S((1, D), F32)),
        scratch_shapes=[pltpu.VMEM((TM, 2 * BW), F32), pltpu.VMEM((TM, BW), F32)],
        compiler_params=_params(("arbitrary",)),
    )(h1, g_b, wb_in, bq, rc, rs1, rs2, k4, vt, sinks, wb_out, g_f, target)


def _b_bwd(dh2, h1, q, g2, o, k4, v4, kt, sinks, wb_out, wb_in, g_b, rc, rs1, rs2):
    T, D = h1.shape
    BW = wb_out.shape[0]
    SH = wb_in.shape[2]
    TM = min(256, T)
    nT = T // TM
    nC = TM // CHUNK
    nP = BW // LANES

    def body(dh2_ref, h1_ref, q_ref, g2_ref, o_ref, k4_ref, v4_ref, kt_ref, sink_ref, wbout_ref, wbin_ref, gb_ref,
             rc_ref, rs1_ref, rs2_ref,
             dh1_ref, dz2_ref, n2_ref, y2_ref, dk_ref, dv_ref, dbq_ref, dgb_ref, dsink_ref, do_scr, dq_scr, dsacc_scr):
        i = pl.program_id(0)

        @pl.when(i == 0)
        def _():
            dk_ref[...] = jnp.zeros_like(dk_ref)
            dv_ref[...] = jnp.zeros_like(dv_ref)
            dbq_ref[...] = jnp.zeros_like(dbq_ref)
            dgb_ref[...] = jnp.zeros_like(dgb_ref)
            dsacc_scr[...] = jnp.zeros_like(dsacc_scr)

        dh2 = dh2_ref[...]
        dy2 = _dot_nt(dh2.astype(BF), wbout_ref[...])
        silu, dsilu = _silu_parts(g2_ref[...].astype(F32))
        do_scr[...] = (dy2 * silu).astype(BF)
        dy2, silu, dsilu = dy2.astype(BF), silu.astype(BF), dsilu.astype(BF)
        ob = o_ref[...]
        y2_ref[...] = (ob * silu).T
        dz2_ref[:, BW:] = dy2 * ob * dsilu
        upper = _upper()
        lo = _lane_lo((2 * CHUNK, LANES))
        for c in range(nC):
            ci = i * nC + c
            rows = slice(c * CHUNK, (c + 1) * CHUNK)
            pci = jnp.maximum(ci - 1, 0)
            prev = pl.multiple_of(pci * CHUNK, CHUNK)
            cur = pl.multiple_of(ci * CHUNK, CHUNK)
            qc = q_ref[rows, :]
            doc = do_scr[rows, :]
            dkb = jnp.zeros((2 * CHUNK, LANES), F32)
            dvb = jnp.zeros((2 * CHUNK, LANES), F32)
            for h in range(2):
                qs = _stack_pairs(qc, h)
                dos = _stack_pairs(doc, h)
                fa, fb = _fold(_dot_nt(_band_rows(k4_ref, prev, cur, h), qs), upper, ci > 0)
                dfa, dfb = _fold(_dot_nt(_band_rows(v4_ref, prev, cur, h), dos), upper)
                folded = []
                for k, (f, df) in enumerate(((fa, dfa), (fb, dfb))):
                    p, ps = _softmax_sink(f, sink_ref[2 * h + k:2 * h + k + 1, :])
                    delta = jnp.sum(p * df, axis=0, keepdims=True)
                    dsacc_scr[2 * h + k:2 * h + k + 1, :] -= ps * delta
                    folded.append((p, p * (df - delta)))
                pt = _unfold(folded[0][0], folded[1][0], upper).astype(BF)
                dst = _unfold(folded[0][1], folded[1][1], upper).astype(BF)
                dqt = _dot(_band_cols(kt_ref, pci, ci, h), dst)
                for j in range(4):
                    dq_scr[rows, (h * 4 + j) * LANES:(h * 4 + j + 1) * LANES] = dqt[:, j * CHUNK:(j + 1) * CHUNK].T
                for acc_name, g in (("k", _dot(dst, qs)), ("v", _dot(pt, dos))):
                    a, b = g[:2 * CHUNK], g[2 * CHUNK:]
                    if h == 0:
                        part = jnp.where(lo, a + pltpu.roll(b, HEAD_DIM, 1), 0.0)
                    else:
                        part = jnp.where(lo, 0.0, pltpu.roll(a, HEAD_DIM, 1) + b)
                    if acc_name == "k":
                        dkb += part
                    else:
                        dvb += part
            dk_ref[pl.ds(prev, CHUNK), :] += dkb[:CHUNK]
            dk_ref[pl.ds(cur, CHUNK), :] += dkb[CHUNK:]
            dv_ref[pl.ds(prev, CHUNK), :] += dvb[:CHUNK]
            dv_ref[pl.ds(cur, CHUNK), :] += dvb[CHUNK:]
        c_t, s1_t, s2_t = rc_ref[...], rs1_ref[...], rs2_ref[...]
        for p in range(nP):
            cols = slice(p * LANES, (p + 1) * LANES)
            dqp = _rot_bwd(dq_scr[:, cols] * (HEAD_DIM ** -0.5), c_t, s1_t, s2_t)
            dbq_ref[:, cols] += jnp.sum(dqp, axis=0, keepdims=True)
            dz2_ref[:, cols] = dqp.astype(BF)
        h1v = h1_ref[...]
        r2 = lax.rsqrt(jnp.mean(h1v * h1v, axis=-1, keepdims=True) + EPS)
        xh = h1v * r2
        gb = gb_ref[...]
        n2_ref[...] = (xh * gb).astype(BF).T
        dn2 = None
        for j in range(N_DEV):
            part = _dot_nt(dz2_ref[:, j * SH:(j + 1) * SH], wbin_ref[j])
            dn2 = part if dn2 is None else dn2 + part
        dgb_ref[...] += jnp.sum(dn2 * xh, axis=0, keepdims=True)
        dh1_ref[...] = dh2 + _rms_bwd(dn2, xh, r2, gb)

        @pl.when(i == nT - 1)
        def _():
            lane = lax.broadcasted_iota(jnp.int32, dsink_ref.shape, 1)
            tot = jnp.zeros(dsink_ref.shape, F32)
            for j in range(4):
                tot += jnp.where(lane == j, jnp.sum(dsacc_scr[:, j * CHUNK:(j + 1) * CHUNK], axis=1, keepdims=True), 0.0)
            dsink_ref[...] = tot

    row = functools.partial(_row_spec, TM)
    S = jax.ShapeDtypeStruct
    return pl.pallas_call(
        body, name="b_bwd", grid=(T // TM,),
        in_specs=[row(D), row(D), row(BW), row(BW), row(BW), _const_spec(k4.shape), _const_spec(v4.shape),
                  _const_spec(kt.shape), _const_spec(sinks.shape), _const_spec(wb_out.shape), _const_spec(wb_in.shape),
                  _const_spec((1, D)), row(LANES), row(LANES), row(LANES)],
        out_specs=[row(D), row(2 * BW), _col_spec(TM, D), _col_spec(TM, BW), _acc_spec((T, LANES)),
                   _acc_spec((T, LANES)), _acc_spec((1, BW)), _acc_spec((1, D)), _acc_spec((4, LANES))],
        out_shape=(S((T, D), F32), S((T, 2 * BW), BF), S((D, T), BF), S((BW, T), BF), S((T, LANES), F32),
                   S((T, LANES), F32), S((1, BW), F32), S((1, D), F32), S((4, LANES), F32)),
        scratch_shapes=[pltpu.VMEM((TM, BW), BF), pltpu.VMEM((TM, BW), F32), pltpu.VMEM((4, 4 * CHUNK), F32)],
        compiler_params=_params(("arbitrary",)),
    )(dh2, h1, q, g2, o, k4, v4, kt, sinks, wb_out, wb_in, g_b, rc, rs1, rs2)


def _a_bwd(dh1p, dk, dv, h1, g_kv, w_kv, wa_out, ws, ln_g, ln_b, z, sv, vhat, rstd, rc, rs1, rs2, ready):
    T, D = h1.shape
    AW = wa_out.shape[0]
    G = ws.shape[0]
    TM = min(256, T)
    nT = T // TM
    nC = TM // CHUNK
    nr = len(ready)

    def body(dh1p_ref, dk_ref, dv_ref, h1_ref, gkv_ref, wkv_ref, waout_ref, ws_ref, lng_ref,
             lnb_ref, u_ref, gt_ref, sv_ref, vhat_ref, rstd_ref, rc_ref, rs1_ref, rs2_ref, *rest):
        ready_refs, rest = rest[:nr], rest[nr:]
        (dz_ref, gwo_ref, gwk_ref, dh1f_ref, dgkv_ref, dbkv_ref, dlng_ref, dlnb_ref,
         dws_ref, dbs_ref), rest = rest[:10], rest[10:]
        recv_refs, (dsv_scr, dvln_scr, acco_scr, acck_scr, ssem, rsem, lsem) = rest[:nr], rest[nr:]
        i = pl.program_id(0)
        exchanges = [_Direct(ready_refs[k], recv_refs[k], ssem.at[k], rsem.at[k], lsem.at[k], scatter=True)
                     for k in range(nr)]

        @pl.when(i == 0)
        def _():
            for e in exchanges:
                e.start()
            for r in (dgkv_ref, dbkv_ref, dlng_ref, dlnb_ref, dws_ref, dbs_ref, acco_scr, acck_scr):
                r[...] = jnp.zeros_like(r)

        dk_pre = _rot_bwd(dk_ref[...], rc_ref[...], rs1_ref[...], rs2_ref[...])
        dkv = jnp.concatenate([dk_pre, dv_ref[...]], axis=1)
        dbkv_ref[...] += jnp.sum(dkv, axis=0, keepdims=True)
        dkv_b = dkv.astype(BF)
        h1v = h1_ref[...]
        rkv = lax.rsqrt(jnp.mean(h1v * h1v, axis=-1, keepdims=True) + EPS)
        xh_kv = h1v * rkv
        gkv = gkv_ref[...]
        acck_scr[...] += _dot((xh_kv * gkv).astype(BF).T, dkv_b)
        dnkv = _dot_nt(dkv_b, wkv_ref[...])
        dgkv_ref[...] += jnp.sum(dnkv * xh_kv, axis=0, keepdims=True)
        dh1 = dh1p_ref[...] + _rms_bwd(dnkv, xh_kv, rkv, gkv)
        dh1_b = dh1.astype(BF)
        dh1f_ref[...] = dh1
        dy = _dot_nt(dh1_b, waout_ref[...]).astype(BF)
        silu, dsilu = _silu_parts(gt_ref[...].astype(F32))
        silu, dsilu = silu.astype(BF), dsilu.astype(BF)
        ub, svb = u_ref[...], sv_ref[...]
        us = ub * silu
        dys = dy * svb
        acco_scr[...] += _dot((us * svb).T, dh1_b)
        dz_ref[:, :AW] = dys * silu
        dz_ref[:, 2 * AW:] = dys * ub * dsilu
        dsv_scr[...] = dy * us
        vhat_v = vhat_ref[...].astype(F32)
        lng = lng_ref[...]
        vln_b = (vhat_v * lng + lnb_ref[...]).astype(BF)
        tri = lax.broadcasted_iota(jnp.int32, (CHUNK, CHUNK), 0) >= lax.broadcasted_iota(jnp.int32, (CHUNK, CHUNK), 1)
        lane = lax.broadcasted_iota(jnp.int32, (CHUNK, LANES), 1)
        dbs = jnp.zeros((CHUNK, LANES), F32)
        for g in range(G):
            wsm = jnp.where(tri, ws_ref[g], 0.0).astype(BF)
            cols = slice(g * CHUNK, (g + 1) * CHUNK)
            dws_g = None
            for c in range(nC):
                rows = slice(c * CHUNK, (c + 1) * CHUNK)
                dsv_cg = dsv_scr[rows, cols]
                dvln_scr[rows, cols] = _dot_tn(wsm, dsv_cg)
                part = _dot_nt(dsv_cg, vln_b[rows, cols])
                dws_g = part if dws_g is None else dws_g + part
                dbs += jnp.where(lane == g, jnp.sum(dsv_cg.astype(F32), axis=-1, keepdims=True), 0.0)
            dws_ref[g] += jnp.where(tri, dws_g, 0.0)
        dbs_ref[...] += dbs
        dvln = dvln_scr[...]
        dlng_ref[...] += jnp.sum(dvln * vhat_v, axis=0, keepdims=True)
        dlnb_ref[...] += jnp.sum(dvln, axis=0, keepdims=True)
        a = dvln * lng
        dvv = rstd_ref[:, 0:1] * (a - jnp.mean(a, axis=-1, keepdims=True)
                                  - vhat_v * jnp.mean(a * vhat_v, axis=-1, keepdims=True))
        dz_ref[:, AW:2 * AW] = dvv.astype(BF)

        @pl.when(i == nT - 1)
        def _():
            for j in range(N_DEV):
                gwo_ref[j] = acco_scr[j * (AW // N_DEV):(j + 1) * (AW // N_DEV)].astype(BF)
                gwk_ref[j] = acck_scr[j * (D // N_DEV):(j + 1) * (D // N_DEV)].astype(BF)
            for e in exchanges:
                e.finish()

    row = functools.partial(_row_spec, TM)
    hbm = pl.BlockSpec(memory_space=pl.ANY)
    S = jax.ShapeDtypeStruct
    gwo_shape, gwk_shape = (N_DEV, AW // N_DEV, D), (N_DEV, D // N_DEV, 2 * LANES)
    return pl.pallas_call(
        body, name="a_bwd", grid=(nT,),
        in_specs=[row(D), row(LANES), row(LANES), row(D), _const_spec((1, D)), _const_spec(w_kv.shape),
                  _const_spec(wa_out.shape), _const_spec(ws.shape),
                  _const_spec((1, AW)), _const_spec((1, AW)), pl.BlockSpec((TM, AW), lambda i: (i, 0)),
                  pl.BlockSpec((TM, AW), lambda i: (i, 2)), row(AW), row(AW), row(LANES),
                  row(LANES), row(LANES), row(LANES)] + [hbm] * nr,
        out_specs=[row(3 * AW), _const_spec(gwo_shape), _const_spec(gwk_shape), row(D),
                   _acc_spec((1, D)), _acc_spec((1, 2 * LANES)), _acc_spec((1, AW)),
                   _acc_spec((1, AW)), _acc_spec(ws.shape), _acc_spec((CHUNK, LANES))] + [hbm] * nr,
        out_shape=(S((T, 3 * AW), BF), S(gwo_shape, BF), S(gwk_shape, BF), S((T, D), F32),
                   S((1, D), F32), S((1, 2 * LANES), F32), S((1, AW), F32), S((1, AW), F32),
                   S(ws.shape, F32), S((CHUNK, LANES), F32)) + tuple(S(r.shape, r.dtype) for r in ready),
        scratch_shapes=[pltpu.VMEM((TM, AW), BF), pltpu.VMEM((TM, AW), F32), pltpu.VMEM((AW, D), F32),
                        pltpu.VMEM((D, 2 * LANES), F32)] + _direct_sems(nr),
        compiler_params=_params(("arbitrary",)),
    )(dh1p, dk, dv, h1, g_kv, w_kv, wa_out, ws, ln_g, ln_b, z, z, sv, vhat, rstd, rc, rs1, rs2, *ready)


def _a_in_bwd(dz, wa_in, x, dh1, g_a, ready):
    T, D = x.shape
    SH = wa_in.shape[2]
    TM = min(512, T)
    nT = T // TM
    nr = len(ready)

    def body(dz_ref, wain_ref, x_ref, dh1_ref, ga_ref, *rest):
        ready_refs, (dx_ref, n1_ref, dga_ref), rest = rest[:nr], rest[nr:nr + 3], rest[nr + 3:]
        recv_refs, (ssem, rsem, lsem) = rest[:nr], rest[nr:]
        i = pl.program_id(0)
        exchanges = [_Direct(ready_refs[k], recv_refs[k], ssem.at[k], rsem.at[k], lsem.at[k], scatter=True)
                     for k in range(nr)]

        @pl.when(i == 0)
        def _():
            for e in exchanges:
                e.start()
            dga_ref[...] = jnp.zeros_like(dga_ref)

        xv = x_ref[...]
        r1 = lax.rsqrt(jnp.mean(xv * xv, axis=-1, keepdims=True) + EPS)
        xh = xv * r1
        ga = ga_ref[...]
        n1_ref[...] = (xh * ga).astype(BF).T
        dn1 = None
        for j in range(N_DEV):
            part = _dot_nt(dz_ref[:, j * SH:(j + 1) * SH], wain_ref[j])
            dn1 = part if dn1 is None else dn1 + part
        dga_ref[...] += jnp.sum(dn1 * xh, axis=0, keepdims=True)
        dx_ref[...] = dh1_ref[...] + _rms_bwd(dn1, xh, r1, ga)

        @pl.when(i == nT - 1)
        def _():
            for e in exchanges:
                e.finish()

    row = functools.partial(_row_spec, TM)
    hbm = pl.BlockSpec(memory_space=pl.ANY)
    S = jax.ShapeDtypeStruct
    return pl.pallas_call(
        body, name="a_in_bwd", grid=(nT,),
        in_specs=[row(dz.shape[1]), _const_spec(wa_in.shape), row(D), row(D), _const_spec((1, D))] + [hbm] * nr,
        out_specs=[row(D), _col_spec(TM, D), _acc_spec((1, D))] + [hbm] * nr,
        out_shape=(S((T, D), F32), S((D, T), BF), S((1, D), F32)) + tuple(S(r.shape, r.dtype) for r in ready),
        scratch_shapes=_direct_sems(nr),
        compiler_params=_params(("arbitrary",)),
    )(dz, wa_in, x, dh1, g_a, *ready)


def _wgrad(at, b, nblk, name, bt=512):
    K, T = at.shape
    N = b.shape[1] // nblk
    BT = min(bt, T)
    nt = T // BT

    def body(a_ref, b_ref, o_ref, acc):
        t = pl.program_id(0)

        @pl.when(t == 0)
        def _():
            acc[...] = jnp.zeros_like(acc)

        acc[...] += _dot(a_ref[...], b_ref[...])

        @pl.when(t == nt - 1)
        def _():
            for j in range(nblk):
                o_ref[j] = acc[:, j * N:(j + 1) * N].astype(BF)

    return pl.pallas_call(
        body, name=name, grid=(nt,),
        in_specs=[pl.BlockSpec((K, BT), lambda t: (0, t)), pl.BlockSpec((BT, nblk * N), lambda t: (t, 0))],
        out_specs=pl.BlockSpec((nblk, K, N), lambda t: (0, 0, 0)),
        out_shape=jax.ShapeDtypeStruct((nblk, K, N), BF),
        scratch_shapes=[pltpu.VMEM((K, nblk * N), F32)],
        compiler_params=_params(("arbitrary",)),
    )(at, b)


def _wgrad_exchange(a, b, me, small, name):
    K, T = a.shape
    N = b.shape[1] // N_DEV
    BT = T
    nt = T // BT
    last = N_DEV - 1
    n_chip = N_DEV // 2

    def far_of(k, core):
        return jnp.where((core == 0) & ((k == 1) | (k == 2)), k, n_chip - 1 - k)

    def block_of(s, me_i):
        k, odd = s // 2, s % 2
        core = me_i & 1
        return me_i ^ ((far_of(k, jnp.where(odd == 1, core, 1 - core)) << 1) | (1 - odd))

    H = K // 2

    def body(me_ref, a_ref, b_ref, small_ref, recv_ref, full_ref, *scratch):
        (acc, dstage, istage, half, relay, d_s, d_r, i_s, i_r, r_s, r_r, lsem, parts_scr, red_scr, e_s, e_r, e_l, g_s,
         g_r, g_l) = scratch
        s, t = pl.program_id(0), pl.program_id(1)
        x, y, c = (lax.axis_index(ax) for ax in AXES)
        ex = [_Direct(small_ref, parts_scr, e_s, e_r, e_l, scatter=True)]
        regather = _TwoLevel(red_scr, full_ref, g_s, g_r, g_l)

        def to_sibling(k, slot):
            return pltpu.make_async_remote_copy(src_ref=dstage.at[slot], dst_ref=half.at[k], send_sem=d_s.at[k],
                                                recv_sem=d_r.at[k], device_id=(x, y, 1 - c), device_id_type=MESH)

        def to_chip(k, slot):
            over_x = far_of(k, c) == 2
            px, py = jnp.where(over_x, 1 - x, x), jnp.where(over_x, y, 1 - y)
            return pltpu.make_async_remote_copy(src_ref=istage.at[slot], dst_ref=recv_ref.at[jnp.where(over_x, 1, 2)],
                                                send_sem=i_s.at[k], recv_sem=i_r.at[k], device_id=(px, py, c),
                                                device_id_type=MESH)

        def to_relay(j, slot):
            to = (1 - x, y, c) if j == 0 else (x, 1 - y, c)
            return pltpu.make_async_remote_copy(src_ref=istage.at[slot, pl.ds(j * H, H)], dst_ref=relay.at[j],
                                                send_sem=r_s.at[j], recv_sem=r_r.at[j], device_id=to,
                                                device_id_type=MESH)

        @pl.when((s == 0) & (t == 0))
        def _():
            for e in ex:
                e.start()

        acc[...] = _dot(a_ref[...], b_ref[...])

        @pl.when(t == nt - 1)
        def _():
            k = lax.div(s, 2)
            slot = lax.rem(k, 2)

            @pl.when(lax.rem(s, 2) == 0)
            def _():
                @pl.when(k >= 2)
                def _():
                    to_sibling(k - 2, slot).wait_send()

                dstage[slot] = acc[...].astype(BF)
                to_sibling(k, slot).start()

            @pl.when(lax.rem(s, 2) == 1)
            def _():
                to_sibling(k, slot).wait_recv()
                pair = acc[...] + half[k].astype(F32)

                @pl.when(k == 0)
                def _():
                    istage[slot] = pair.astype(BF)
                    for j in range(2):
                        to_relay(j, slot).start()

                @pl.when(k == 1)
                def _():
                    for j in range(2):
                        to_relay(j, slot).wait_recv()

                @pl.when(k == 2)
                def _():
                    for j in range(2):
                        to_relay(j, slot).wait_send()

                @pl.when(k == n_chip - 1)
                def _():
                    to_chip(1, slot).wait_send()
                    istage[slot] = pair.astype(BF)

                @pl.when((k == 1) | (k == 2))
                def _():
                    over_x = far_of(k, c) == 2
                    istage[slot, 0:H] = (pair[:H] + jnp.where(over_x, 0.0, relay[0].astype(F32))).astype(BF)
                    istage[slot, H:K] = (pair[H:] + jnp.where(over_x, relay[1].astype(F32), 0.0)).astype(BF)
                    to_chip(k, slot).start()

            @pl.when(s == last)
            def _():
                own = pltpu.make_async_copy(istage.at[slot], recv_ref.at[0], lsem)
                own.start()
                to_chip(2, 0).wait_send()
                to_sibling(n_chip - 2, 0).wait_send()
                to_sibling(n_chip - 1, 1).wait_send()
                for kk in (1, 2):
                    to_chip(kk, 0).wait_recv()
                own.wait()
                for e in ex:
                    e.finish()
                total = parts_scr[0]
                for dev in range(1, N_DEV):
                    total = total + parts_scr[dev]
                red_scr[...] = total
                regather.start()
                regather.forward()
                regather.finish()

    hbm = pl.BlockSpec(memory_space=pl.ANY)
    dma = pltpu.SemaphoreType.DMA
    grid_spec = pltpu.PrefetchScalarGridSpec(
        num_scalar_prefetch=1, grid=(N_DEV, nt),
        in_specs=[pl.BlockSpec((K, BT), lambda s, t, me_ref: (0, t), pipeline_mode=pl.Buffered(1)),
                  pl.BlockSpec((BT, N), lambda s, t, me_ref: (t, block_of(s, me_ref[0]))), hbm],
        out_specs=[hbm, hbm],
        scratch_shapes=[pltpu.VMEM((K, N), F32), pltpu.VMEM((2, K, N), BF), pltpu.VMEM((2, K, N), BF),
                        pltpu.VMEM((n_chip, K, N), BF), pltpu.VMEM((2, H, N), BF), dma((n_chip,)), dma((n_chip,)),
                        dma((n_chip - 1,)), dma((n_chip - 1,)), dma((2,)), dma((2,)), dma,
                        pltpu.VMEM(small.shape, F32), pltpu.VMEM(small.shape[1:], F32),
                        dma((last,)), dma((last,)), dma, dma((last,)), dma((last,)), dma])
    return pl.pallas_call(
        body, name=name, grid_spec=grid_spec,
        out_shape=[jax.ShapeDtypeStruct((n_chip - 1, K, N), BF), jax.ShapeDtypeStruct(small.shape, F32)],
        compiler_params=_params(("arbitrary", "arbitrary")),
    )(me, a, b, small)


def _my_index():
    return 4 * lax.axis_index("x") + 2 * lax.axis_index("y") + lax.axis_index("c")


def _peer(mask):
    x, y, c = (lax.axis_index(a) for a in AXES)
    return (x ^ ((mask >> 2) & 1), y ^ ((mask >> 1) & 1), c ^ (mask & 1))


def _dev_index(p):
    return 4 * p[0] + 2 * p[1] + p[2]


class _Direct:
    def __init__(self, src, dst, send_sems, recv_sems, local_sem, scatter):
        me = _my_index()
        self.own = pltpu.make_async_copy(src.at[me] if scatter else src, dst.at[me], local_sem)
        self.sends, self.recvs = [], []
        for k in range(1, N_DEV):
            p = _peer(k)
            pi = _dev_index(p)
            sems = dict(send_sem=send_sems.at[k - 1], recv_sem=recv_sems.at[k - 1], device_id=p, device_id_type=MESH)
            self.sends.append(pltpu.make_async_remote_copy(src_ref=src.at[pi] if scatter else src, dst_ref=dst.at[me],
                                                           **sems))
            self.recvs.append(pltpu.make_async_remote_copy(src_ref=src.at[me] if scatter else src, dst_ref=dst.at[pi],
                                                           **sems))

    def start(self):
        self.own.start()
        for cp in self.sends:
            cp.start()

    def finish(self):
        for cp in self.sends:
            cp.wait_send()
        for cp in self.recvs:
            cp.wait_recv()
        self.own.wait()


class _TwoLevel:
    def __init__(self, src, dst, send_sems, recv_sems, local_sem, own=True):
        x, y, c = (lax.axis_index(a) for a in AXES)
        self.me, self.sibling = (x, y, c), (x, y, 1 - c)
        self.chips = [(1 - x, y), (x, 1 - y), (1 - x, 1 - y)]
        self.src, self.dst, self.send_sems, self.recv_sems = src, dst, send_sems, recv_sems
        self.own = pltpu.make_async_copy(src, dst.at[_dev_index(self.me)], local_sem) if own else None

    def _copy(self, k, block, to, from_src=False):
        slot = self.dst.at[_dev_index(block)]
        return pltpu.make_async_remote_copy(src_ref=self.src if from_src else slot, dst_ref=slot,
                                            send_sem=self.send_sems.at[k], recv_sem=self.recv_sems.at[k],
                                            device_id=to, device_id_type=MESH)

    def _firsts(self):
        c = self.me[2]
        return [self._copy(0, self.me, self.sibling, True)] + [self._copy(1 + j, self.me, (*chip, c), True)
                                                               for j, chip in enumerate(self.chips)]

    def _passed(self):
        c = self.me[2]
        return [self._copy(4 + j, (*chip, c), self.sibling) for j, chip in enumerate(self.chips)]

    def start(self):
        if self.own is not None:
            self.own.start()
        for cp in self._firsts():
            cp.start()

    def wait_sibling(self):
        self._copy(0, self.sibling, self.me).wait_recv()

    def wait_chip_and_forward(self, j):
        self._copy(1 + j, (*self.chips[j], self.me[2]), self.me).wait_recv()
        self._passed()[j].start()

    def wait_passed(self, j):
        self._copy(4 + j, (*self.chips[j], 1 - self.me[2]), self.me).wait_recv()

    def wait_sends(self):
        for cp in self._firsts() + self._passed():
            cp.wait_send()
        if self.own is not None:
            self.own.wait()

    def forward(self):
        for j in range(3):
            self.wait_chip_and_forward(j)

    def finish(self):
        self.wait_sibling()
        for j in range(3):
            self.wait_passed(j)
        self.wait_sends()


class _RelayGather:
    def __init__(self, dst, send_sems, recv_sems):
        x, y, c = (lax.axis_index(a) for a in AXES)
        self.c = c
        self.sib, self.xn, self.yn, self.dg = (x, y, 1 - c), (1 - x, y, c), (x, 1 - y, c), (1 - x, 1 - y, c)
        self.me = (x, y, c)
        self.dst, self.send_sems, self.recv_sems = dst, send_sems, recv_sems
        self.half = dst.shape[1] // 2

    def _slot(self, dev, part=None):
        i = _dev_index(dev)
        if part is None:
            return self.dst.at[i]
        return self.dst.at[i, pl.ds(part * self.half, self.half)]

    def _copy(self, k, dev, to, part=None):
        ref = self._slot(dev, part)
        return pltpu.make_async_remote_copy(src_ref=ref, dst_ref=ref, send_sem=self.send_sems.at[k],
                                            recv_sem=self.recv_sems.at[k], device_id=to, device_id_type=MESH)

    def _other(self, dev):
        return (dev[0], dev[1], 1 - self.c)

    def start(self):
        for k, to in enumerate((self.sib, self.xn, self.yn)):
            self._copy(k, self.me, to).start()

    def send_own(self, k):
        return self._copy(k, self.me, (self.sib, self.xn, self.yn)[k])

    def wait_sibling(self):
        self._copy(0, self.sib, self.me).wait_recv()

    def on_x(self):
        self._copy(1, self.xn, self.me).wait_recv()
        self._copy(3, self.xn, self.yn, part=0).start()
        self._copy(5, self.xn, self.sib).start()

    def on_y(self):
        self._copy(2, self.yn, self.me).wait_recv()
        self._copy(4, self.yn, self.xn, part=1).start()
        self._copy(6, self.yn, self.sib).start()

    def on_diag(self):
        self._copy(3, self.dg, self.me, part=0).wait_recv()
        self._copy(4, self.dg, self.me, part=1).wait_recv()
        self._copy(7, self.dg, self.sib).start()

    def wait_passed(self, j):
        self._copy(5 + j, self._other((self.xn, self.yn, self.dg)[j]), self.me).wait_recv()

    def wait_sends(self):
        for k, to in enumerate((self.sib, self.xn, self.yn)):
            self._copy(k, self.me, to).wait_send()
        self._copy(3, self.xn, self.yn, part=0).wait_send()
        self._copy(4, self.yn, self.xn, part=1).wait_send()
        for j, dev in enumerate((self.xn, self.yn, self.dg)):
            self._copy(5 + j, dev, self.sib).wait_send()


def _direct_sems(n):
    if n == 0:
        return []
    return [pltpu.SemaphoreType.DMA((n, 7)), pltpu.SemaphoreType.DMA((n, 7)), pltpu.SemaphoreType.DMA((n,))]


def _adam_math(w, g, m, v):
    m = ADAM_B1 * m + (1.0 - ADAM_B1) * g
    v = ADAM_B2 * v + (1.0 - ADAM_B2) * (g * g)
    m_hat = m / (1.0 - ADAM_B1 ** ADAM_STEP)
    v_hat = v / (1.0 - ADAM_B2 ** ADAM_STEP)
    delta = -ADAM_LR * (m_hat / (jnp.sqrt(v_hat) + ADAM_EPS) + ADAM_WD * w)
    return delta, m, v


def _sum_adam(parts, w, m, v, name):
    R, C = w.shape
    NP = parts.shape[0]
    BR = CHUNK if R % CHUNK == 0 else R

    def body(p_ref, w_ref, m_ref, v_ref, g_ref, d_ref, nm_ref, nv_ref):
        g = p_ref[0].astype(F32)
        for i in range(1, NP):
            g = g + p_ref[i].astype(F32)
        g_ref[...] = g
        d_ref[...], nm_ref[...], nv_ref[...] = _adam_math(w_ref[...], g, m_ref[...], v_ref[...])

    blk = pl.BlockSpec((BR, C), lambda i: (i, 0))
    S = jax.ShapeDtypeStruct((R, C), F32)
    return pl.pallas_call(
        body, name=name, grid=(R // BR,),
        in_specs=[pl.BlockSpec((NP, BR, C), lambda i: (0, i, 0)), blk, blk, blk],
        out_specs=[blk] * 4, out_shape=(S,) * 4,
        compiler_params=_params(("arbitrary",)),
    )(parts, w, m, v)


SUBLANES = 8


def _nrows(size):
    return -(-size // (SUBLANES * LANES)) * SUBLANES


def _view2d(a):
    return a.reshape(-1, LANES) if a.size % LANES == 0 else a.reshape(1, -1)


def _pack_small(parts, total_rows, name):
    arrs = [p[0] for p in parts]

    def body(*refs):
        out = refs[-1]
        out[...] = jnp.zeros_like(out)
        at = 0
        for ref, (a, rows, flag) in zip(refs[:-1], parts):
            val = ref[...].T if flag == "T" else ref[...]
            r, c = (rows, val.shape[1]) if flag == "T" else val.shape
            out[at:at + r, 0:c] = val[:r]
            at += _nrows(r * c)

    return pl.pallas_call(body, name=name, out_shape=jax.ShapeDtypeStruct((total_rows, LANES), F32))(*arrs)


def _small_update(full, me, reps, shards, name):
    n = len(reps) + len(shards)

    def body(me_ref, full_ref, *refs):
        ins, outs = refs[:3 * n], refs[3 * n:]
        at = 0
        for k in range(n):
            w_ref, m_ref, v_ref = ins[3 * k:3 * k + 3]
            r, c = w_ref.shape
            if k < len(reps):
                g = full_ref[at:at + r, 0:c]
                at += _nrows(r * c)
            else:
                seg = full_ref[at:at + N_DEV * r, :]
                row = lax.broadcasted_iota(jnp.int32, seg.shape, 0)
                pick = [jnp.sum(jnp.where(row == r * me_ref[0] + t, seg, 0.0), axis=0, keepdims=True) for t in range(r)]
                g = pick[0] if r == 1 else jnp.concatenate(pick, axis=0)
                at += N_DEV * r
            g_ref, d_ref, nm_ref, nv_ref = outs[4 * k:4 * k + 4]
            g_ref[...] = g
            d_ref[...], nm_ref[...], nv_ref[...] = _adam_math(w_ref[...], g, m_ref[...], v_ref[...])
        outs[4 * n][...] = full_ref[at:at + 1, 0:1]

    flat = [t for p in reps + shards for t in p]
    S = jax.ShapeDtypeStruct
    res = pl.pallas_call(
        body, name=name,
        in_specs=[pl.BlockSpec(memory_space=pltpu.SMEM)] + [pl.BlockSpec(memory_space=pltpu.VMEM)] * (1 + len(flat)),
        out_shape=[S(p[0].shape, F32) for p in reps + shards for _ in range(4)] + [S((1, 1), F32)],
    )(me, full, *flat)
    return [tuple(res[4 * k:4 * k + 4]) for k in range(n)], res[4 * n]


def _rope_tables(T):
    pos = np.arange(T, dtype=np.float32)
    inv_freq = (np.float64(ROPE_THETA) ** (-np.arange(0, HEAD_DIM, 2, dtype=np.float64) / HEAD_DIM)).astype(np.float32)
    ang = (pos[:, None] * inv_freq[None, :]).astype(np.float64)
    cos, sin, zero = np.cos(ang).astype(np.float32), np.sin(ang).astype(np.float32), np.zeros(ang.shape, np.float32)
    c = np.concatenate([cos, cos, cos, cos], axis=1)
    s1 = np.concatenate([-sin, zero, -sin, zero], axis=1)
    s2 = np.concatenate([zero, sin, zero, sin], axis=1)
    return jnp.asarray(c), jnp.asarray(s1), jnp.asarray(s2)


def kernel(x, a_norm_g, a_w_in, a_ln_g, a_ln_b, a_ws, a_bs, a_w_out, kv_norm_g, w_kv, b_kv, b_norm_g, b_w_in, b_bq, b_sinks, b_w_out, final_norm_g, loss_target, m_a_norm_g, m_a_w_in, m_a_ln_g, m_a_ln_b, m_a_ws, m_a_bs, m_a_w_out, m_kv_norm_g, m_w_kv, m_b_kv, m_b_norm_g, m_b_w_in, m_b_bq, m_b_sinks, m_b_w_out, m_final_norm_g, v_a_norm_g, v_a_w_in, v_a_ln_g, v_a_ln_b, v_a_ws, v_a_bs, v_a_w_out, v_kv_norm_g, v_w_kv, v_b_kv, v_b_norm_g, v_b_w_in, v_b_bq, v_b_sinks, v_b_w_out, v_final_norm_g):
    T, D = x.shape[1], x.shape[2]
    AW = a_ln_g.shape[1] * N_DEV
    G = a_ws.shape[1]
    assert w_kv.shape[1] == 2 * LANES and a_ws.shape[2] == CHUNK and T % CHUNK == 0
    me = _my_index()

    xs, tgt = x[0], loss_target[0]
    vec = jnp.concatenate([a_norm_g, a_ln_g, a_ln_b], axis=1)
    vec = jnp.broadcast_to(vec, (SUBLANES, vec.shape[1]))
    north = lax.axis_index("c") == 1
    slots = me ^ jnp.where(north, jnp.array(PASS_MASKS[1], jnp.int32), jnp.array(PASS_MASKS[0], jnp.int32))
    z, wa_in, vecs, wa_out, wkv = _in_proj(xs, a_w_in[0], vec, slots, [a_w_out[0], w_kv])
    wa_out = wa_out.reshape(AW, D)
    wkv = wkv.reshape(D, 2 * LANES)
    vecs = vecs[:, 0, :]
    ds = D // N_DEV
    g_a = vecs[:, :ds].reshape(1, D)
    ln_g = vecs[:, ds:ds + AW // N_DEV].reshape(1, AW)
    ln_b = vecs[:, ds + AW // N_DEV:].reshape(1, AW)

    rc, rs1, rs2 = _rope_tables(T)
    ws = a_ws[0]
    bs_t = a_bs[0].T
    g_kv = kv_norm_g.reshape(1, D)
    bkv = b_kv.reshape(1, -1)
    g_f = final_norm_g.reshape(1, D)
    sinks = jnp.repeat(b_sinks.reshape(2, 4, 2).transpose(0, 2, 1).reshape(4, 4), CHUNK, axis=1)
    h1, sv, vhat, rstd, k4, v4, kt, vt, wb_in, wb_out = _a_fwd(
        xs, z, ln_g, ln_b, ws, bs_t, wa_out, g_kv, wkv, bkv, rc, rs1, rs2, [b_w_in[0], b_w_out[0]])
    wb_out = wb_out.reshape(-1, D)
    q, g2, o, dh2, dh2_b, loss, d_gf = _b_fwd(h1, b_norm_g, wb_in, b_bq, rc, rs1, rs2, k4, vt, sinks, wb_out, g_f, tgt)
    dh1p, dz2, n2, y2, dk, dv, d_bq, d_gb, d_sink = _b_bwd(dh2, h1, q, g2, o, k4, v4, kt, sinks, wb_out, wb_in,
                                                           b_norm_g, rc, rs1, rs2)
    d_sink = d_sink[:, :4].reshape(2, 2, 4).transpose(0, 2, 1).reshape(1, 16)
    gw_b_in = _wgrad(n2, dz2, N_DEV, "wgrad_b_in", bt=1024)
    gw_b_out = _wgrad(y2, dh2_b, 1, "wgrad_b_out", bt=1024).reshape(N_DEV, -1, D)
    (dz, gw_a_out, gw_kv, dh1_f, d_gkv, d_bkv, d_lng, d_lnb, d_ws, d_bst, r_b_in, r_b_out) = _a_bwd(
        dh1p, dk, dv, h1, g_kv, wkv, wa_out, ws, ln_g, ln_b, z, sv, vhat, rstd, rc, rs1, rs2, [gw_b_in, gw_b_out])
    dx, n1, d_ga, r_a_out, r_kv = _a_in_bwd(dz, wa_in, xs, dh1_f, g_a, [gw_a_out, gw_kv])
    small = [(_view2d(d_ws), None, None), (d_bst, G, "T")] + [(_view2d(a), None, None) for a in (
        d_gkv, d_bkv, d_gb, d_bq, d_sink, d_gf, d_ga, d_lng, d_lnb, loss)]
    used = sum(_nrows(G * CHUNK if flag else a.size) for a, _, flag in small)
    per = -(-used // (SUBLANES * N_DEV)) * SUBLANES
    small_pack = _pack_small(small, per * N_DEV, "pack_small").reshape(N_DEV, per, LANES)
    r_a_in, full_small = _wgrad_exchange(n1, dz, me.reshape(1), small_pack, "wgrad_a_in")

    g_a_in, d_a_in, nm_a_in, nv_a_in = _sum_adam(r_a_in, a_w_in[0], m_a_w_in[0], v_a_w_in[0], "adam_a_in")
    g_a_out, d_a_out, nm_a_out, nv_a_out = _sum_adam(r_a_out, a_w_out[0], m_a_w_out[0], v_a_w_out[0], "adam_a_out")
    g_kvw, d_kvw, nm_kvw, nv_kvw = _sum_adam(r_kv, w_kv, m_w_kv, v_w_kv, "adam_kv")
    g_b_in, d_b_in, nm_b_in, nv_b_in = _sum_adam(r_b_in, b_w_in[0], m_b_w_in[0], v_b_w_in[0], "adam_b_in")
    g_b_out, d_b_out, nm_b_out, nv_b_out = _sum_adam(r_b_out, b_w_out[0], m_b_w_out[0], v_b_w_out[0], "adam_b_out")

    full_small = full_small.reshape(N_DEV * per, LANES)
    reps = [(a_ws, m_a_ws, v_a_ws), (a_bs, m_a_bs, v_a_bs), (kv_norm_g, m_kv_norm_g, v_kv_norm_g),
            (b_kv, m_b_kv, v_b_kv), (b_norm_g, m_b_norm_g, v_b_norm_g), (b_bq, m_b_bq, v_b_bq),
            (b_sinks, m_b_sinks, v_b_sinks), (final_norm_g, m_final_norm_g, v_final_norm_g)]
    shards = [(a_norm_g, m_a_norm_g, v_a_norm_g), (a_ln_g, m_a_ln_g, v_a_ln_g), (a_ln_b, m_a_ln_b, v_a_ln_b)]
    upd, loss = _small_update(full_small, me.reshape(1), [tuple(_view2d(t) for t in p) for p in reps],
                              [tuple(_view2d(t) for t in p) for p in shards], "adam_small")
    loss = loss[0, 0]
    sm_g, sd, snm, snv = ([upd[k][j].reshape(p[0].shape) for k, p in enumerate(reps + shards)] for j in range(4))

    def order(big, sm):
        a_in, a_out, kvw, b_in, b_out = big
        ws_, bs_, kvg, bkv_, bng, bq_, snk, fng, ang, alng, alnb = sm
        return (ang, a_in[None], alng, alnb, ws_, bs_, a_out[None], kvg, kvw, bkv_, bng, b_in[None], bq_, snk,
                b_out[None], fng)

    grads = order((g_a_in, g_a_out, g_kvw, g_b_in, g_b_out), sm_g)
    deltas = order((d_a_in, d_a_out, d_kvw, d_b_in, d_b_out), sd)
    new_m = order((nm_a_in, nm_a_out, nm_kvw, nm_b_in, nm_b_out), snm)
    new_v = order((nv_a_in, nv_a_out, nv_kvw, nv_b_in, nv_b_out), snv)
    return (loss, dx[None], *grads, *deltas, *new_m, *new_v)
```

```python
import functools

import jax
import jax.numpy as jnp
import numpy as np
from jax import lax
from jax.experimental import pallas as pl
from jax.experimental.pallas import tpu as pltpu

CHUNK = 128
HEAD_DIM = 64
ROPE_THETA = 10000.0
EPS = 1e-5
ADAM_LR = 0.001
ADAM_B1 = 0.9
ADAM_B2 = 0.999
ADAM_EPS = 1e-08
ADAM_WD = 0.01
ADAM_STEP = 10
N_DEV = 8
LANES = 128
NEG = -1e30

BF = jnp.bfloat16
F32 = jnp.float32
MESH = pl.DeviceIdType.MESH
AXES = ("x", "y", "c")
VMEM_LIMIT = 56 * 1024 * 1024


def _dot(a, b):
    return jnp.dot(a, b, preferred_element_type=F32)


def _dot_nt(a, b):
    return lax.dot_general(a, b, (((1,), (1,)), ((), ())), preferred_element_type=F32)


def _dot_tn(a, b):
    return lax.dot_general(a, b, (((0,), (0,)), ((), ())), preferred_element_type=F32)


def _const_spec(shape):
    nd = len(shape)
    return pl.BlockSpec(shape, lambda *_: (0,) * nd, pipeline_mode=pl.Buffered(1))


def _acc_spec(shape):
    nd = len(shape)
    return pl.BlockSpec(shape, lambda *_: (0,) * nd)


def _row_spec(tm, width):
    return pl.BlockSpec((tm, width), lambda i: (i, 0))


def _col_spec(tm, height):
    return pl.BlockSpec((height, tm), lambda i: (0, i))


def _params(sem):
    return pltpu.CompilerParams(dimension_semantics=sem, vmem_limit_bytes=VMEM_LIMIT)


def _rot(x, c, s1, s2):
    return x * c + pltpu.roll(x, 96, 1) * s1 + pltpu.roll(x, 32, 1) * s2


def _rot_bwd(d, c, s1, s2):
    return d * c + pltpu.roll(d * s1, 32, 1) + pltpu.roll(d * s2, 96, 1)


def _silu_parts(g):
    sg = jax.nn.sigmoid(g)
    return g * sg, sg * (1.0 + g * (1.0 - sg))


def _rms_bwd(dn, xh, r, g):
    a = dn * g
    return r * (a - xh * jnp.mean(a * xh, axis=-1, keepdims=True))


def _lane_lo(shape):
    return lax.broadcasted_iota(jnp.int32, shape, 1) < HEAD_DIM


def _split4(t):
    lo = _lane_lo(t.shape)
    tr = pltpu.roll(t, HEAD_DIM, 1)
    z = jnp.zeros_like(t)
    return jnp.concatenate([jnp.where(lo, t, z), jnp.where(lo, z, tr), jnp.where(lo, tr, z), jnp.where(lo, z, t)], axis=1)


def _stack_pairs(t, h):
    return jnp.concatenate([t[:, (h * 4 + j) * LANES:(h * 4 + j + 1) * LANES] for j in range(4)], axis=0)


def _upper():
    shape = (CHUNK, 4 * CHUNK)
    return lax.broadcasted_iota(jnp.int32, shape, 0) > (lax.broadcasted_iota(jnp.int32, shape, 1) & (CHUNK - 1))


def _band_rows(ref, prev, cur, h):
    a = slice(2 * h * LANES, (2 * h + 1) * LANES)
    b = slice((2 * h + 1) * LANES, (2 * h + 2) * LANES)
    return jnp.concatenate([ref[pl.ds(prev, CHUNK), a], ref[pl.ds(cur, CHUNK), a],
                            ref[pl.ds(prev, CHUNK), b], ref[pl.ds(cur, CHUNK), b]], axis=0)


def _band_cols(ref, pci, ci, h):
    a = slice(2 * h * LANES, (2 * h + 1) * LANES)
    b = slice((2 * h + 1) * LANES, (2 * h + 2) * LANES)
    return jnp.concatenate([ref[pci, a, :], ref[ci, a, :], ref[pci, b, :], ref[ci, b, :]], axis=1)


def _fold(t, upper, has_prev=None):
    out = []
    for k in range(2):
        prev = t[2 * k * CHUNK:(2 * k + 1) * CHUNK]
        if has_prev is not None:
            prev = jnp.where(has_prev, prev, NEG)
        out.append(jnp.where(upper, prev, t[(2 * k + 1) * CHUNK:(2 * k + 2) * CHUNK]))
    return out


def _unfold(fa, fb, upper):
    z = jnp.zeros_like(fa)
    return jnp.concatenate([jnp.where(upper, fa, z), jnp.where(upper, z, fa),
                            jnp.where(upper, fb, z), jnp.where(upper, z, fb)], axis=0)


def _softmax_sink(f, sink):
    m = jnp.maximum(jnp.max(f, axis=0, keepdims=True), sink)
    p = jnp.exp(f - m)
    es = jnp.exp(sink - m)
    inv = 1.0 / (jnp.sum(p, axis=0, keepdims=True) + es)
    return p * inv, es * inv


class _Riding:
    def __init__(self, shards, gathered, stages, sems, n_steps):
        self.shards, self.stages, self.n_steps = shards, stages, n_steps
        ssem, rsem, lsem = sems
        self.gathers = [_TwoLevel(stages[k], gathered[k], ssem.at[k], rsem.at[k], lsem.at[k])
                        for k in range(len(shards))]

    def begin(self, i):
        @pl.when(i == 0)
        def _():
            for shard, stage, g in zip(self.shards, self.stages, self.gathers):
                stage[...] = shard[...].astype(stage.dtype)
                g.start()

    def end(self, i):
        @pl.when(i == self.n_steps // 2)
        def _():
            for g in self.gathers:
                g.forward()

        @pl.when(i == self.n_steps - 1)
        def _():
            for g in self.gathers:
                g.finish()

    @staticmethod
    def specs(later):
        nl = len(later)
        hbm = pl.BlockSpec(memory_space=pl.ANY)
        return ([_const_spec(w.shape) for w in later], [hbm] * nl,
                tuple(jax.ShapeDtypeStruct((N_DEV,) + w.shape, BF) for w in later),
                [pltpu.VMEM(w.shape, BF) for w in later] + _direct_sems(nl))


PASS_MASKS = ((0, 1, 2, 5, 4, 3, 6, 7), (0, 1, 4, 3, 2, 5, 6, 7))


def _in_proj(x, w_shard, vec_shard, slots, later):
    T, D = x.shape
    SH = w_shard.shape[1]
    TM = min(1024, T)
    nT = T // TM
    nl = len(later)
    ds = D // N_DEV
    last = N_DEV - 1

    def body(slots_ref, x_ref, wsh_ref, vsh_ref, *rest):
        shards, rest = rest[:nl], rest[nl:]
        (z_ref, wout_ref, vout_ref), rest = rest[:3], rest[3:]
        gathered, rest = rest[:nl], rest[nl:]
        (w_scr, vec_scr, vstage, n1_scr, ga_scr, w_s, w_r, w_l, v_s, v_r, v_l), rest = rest[:11], rest[11:]
        stages, sems = rest[:nl], rest[nl:]
        p, i = pl.program_id(0), pl.program_id(1)
        me = _my_index()
        wg = _RelayGather(w_scr, w_s, w_r)
        vg = _Direct(vstage, vec_scr, v_s, v_r, v_l, scatter=False)
        lg = [_TwoLevel(stages[k], gathered[k], sems[0].at[k], sems[1].at[k], sems[2].at[k]) for k in range(nl)]
        w_copy = pltpu.make_async_copy(w_scr, wout_ref, w_l)

        def at_pass(k):
            return (p == k) & (i == 0)

        c = lax.axis_index("c")

        @pl.when(at_pass(0))
        def _():
            vstage[...] = vsh_ref[...]
            vg.start()
            w_scr[me] = wsh_ref[...].astype(BF)
            wg.send_own(0).start()

            @pl.when(c == 1)
            def _():
                wg.send_own(1).start()

            @pl.when(c == 0)
            def _():
                wg.send_own(2).start()

            vg.finish()
            for j in range(N_DEV):
                ga_scr[:, j * ds:(j + 1) * ds] = vec_scr[j, 0:1, 0:ds]
            vout_ref[...] = vec_scr[...]

        @pl.when(at_pass(1))
        def _():
            wg.wait_sibling()

        for first, second, landed_first, landed_second in ((1, 2, wg.on_x, wg.on_y), (2, 1, wg.on_y, wg.on_x)):
            mine = c == (1 if first == 1 else 0)

            @pl.when(at_pass(2) & mine)
            def _(second=second, landed_first=landed_first):
                wg.send_own(second).start()
                landed_first()

            @pl.when(at_pass(3) & mine)
            def _(second=second):
                wg.wait_passed(second - 1)

            @pl.when(at_pass(4) & mine)
            def _(landed_second=landed_second):
                landed_second()

            @pl.when(at_pass(5) & mine)
            def _(first=first):
                wg.wait_passed(first - 1)

        @pl.when(at_pass(4))
        def _():
            for k in range(nl):
                stages[k][...] = shards[k][...].astype(BF)
                lg[k].start()

        @pl.when(at_pass(6))
        def _():
            wg.on_diag()

        @pl.when(at_pass(7))
        def _():
            wg.wait_passed(2)
            for g in lg:
                g.forward()

        @pl.when(at_pass(last))
        def _():
            w_copy.start()

        @pl.when(p == 0)
        def _():
            xv = x_ref[...]
            r1 = lax.rsqrt(jnp.mean(xv * xv, axis=-1, keepdims=True) + EPS)
            n1_scr[i] = (xv * r1 * ga_scr[...]).astype(BF)

        z_ref[...] = _dot(n1_scr[i], w_scr[slots_ref[p]]).astype(BF)

        @pl.when((p == last) & (i == nT - 1))
        def _():
            wg.wait_sends()
            for g in lg:
                g.finish()
            w_copy.wait()

    hbm = pl.BlockSpec(memory_space=pl.ANY)
    dma = pltpu.SemaphoreType.DMA
    S = jax.ShapeDtypeStruct
    grid_spec = pltpu.PrefetchScalarGridSpec(
        num_scalar_prefetch=1, grid=(N_DEV, nT),
        in_specs=[pl.BlockSpec((TM, D), lambda p, i, s: (jnp.where(p == 0, i, nT - 1), 0)),
                  pl.BlockSpec(w_shard.shape, lambda p, i, s: (0, 0), pipeline_mode=pl.Buffered(1)),
                  pl.BlockSpec(vec_shard.shape, lambda p, i, s: (0, 0), pipeline_mode=pl.Buffered(1))]
        + [pl.BlockSpec(w.shape, lambda p, i, s: (0, 0), pipeline_mode=pl.Buffered(1)) for w in later],
        out_specs=[pl.BlockSpec((TM, SH), lambda p, i, s: (i, s[p])), hbm,
                   pl.BlockSpec((N_DEV,) + vec_shard.shape, lambda p, i, s: (0, 0, 0))] + [hbm] * nl,
        scratch_shapes=[pltpu.VMEM((N_DEV, D, SH), BF), pltpu.VMEM((N_DEV,) + vec_shard.shape, F32),
                        pltpu.VMEM(vec_shard.shape, F32), pltpu.VMEM((nT, TM, D), BF), pltpu.VMEM((1, D), F32),
                        dma((8,)), dma((8,)), dma, dma((7,)), dma((7,)), dma]
        + [pltpu.VMEM(w.shape, BF) for w in later] + _direct_sems(nl))
    return pl.pallas_call(
        body, name="a_in_proj", grid_spec=grid_spec,
        out_shape=(S((T, N_DEV * SH), BF), S((N_DEV, D, SH), BF), S((N_DEV,) + vec_shard.shape, F32))
        + tuple(S((N_DEV,) + w.shape, BF) for w in later),
        compiler_params=_params(("arbitrary", "arbitrary")),
    )(slots, x, w_shard, vec_shard, *later)


def _a_fwd(x, z, ln_g, ln_b, ws, bs_t, wa_out, g_kv, w_kv, b_kv, rc, rs1, rs2, later):
    T, D = x.shape
    AW = wa_out.shape[0]
    G = ws.shape[0]
    TM = min(256, T)
    nT = T // TM
    nC = TM // CHUNK
    nl = len(later)

    def body(x_ref, u_ref, v_ref, gt_ref, lng_ref, lnb_ref, ws_ref, bst_ref, waout_ref, gkv_ref, wkv_ref, bkv_ref,
             rc_ref, rs1_ref, rs2_ref, *rest):
        shards, rest = rest[:nl], rest[nl:]
        (h1_ref, sv_ref, vhat_ref, rstd_ref, k4_ref, v4_ref, kt_ref, vt_ref), rest = rest[:8], rest[8:]
        gathered, sv_scr, stages, sems = rest[:nl], rest[nl], rest[nl + 1:2 * nl + 1], rest[2 * nl + 1:]
        i = pl.program_id(0)
        riding = _Riding(shards, gathered, stages, sems, nT)
        riding.begin(i)
        xv = x_ref[...]
        u = u_ref[...].astype(F32)
        v = v_ref[...].astype(F32)
        gt = gt_ref[...].astype(F32)
        mu = jnp.mean(v, axis=-1, keepdims=True)
        xc = v - mu
        rstd = lax.rsqrt(jnp.mean(xc * xc, axis=-1, keepdims=True) + EPS)
        vhat = xc * rstd
        vln = (vhat * lng_ref[...] + lnb_ref[...]).astype(BF)
        tri = lax.broadcasted_iota(jnp.int32, (CHUNK, CHUNK), 0) >= lax.broadcasted_iota(jnp.int32, (CHUNK, CHUNK), 1)
        for g in range(G):
            wsm = jnp.where(tri, ws_ref[g], 0.0).astype(BF)
            bias = bst_ref[:, g:g + 1]
            for c in range(nC):
                blk = vln[c * CHUNK:(c + 1) * CHUNK, g * CHUNK:(g + 1) * CHUNK]
                sv_scr[c * CHUNK:(c + 1) * CHUNK, g * CHUNK:(g + 1) * CHUNK] = _dot(wsm, blk) + bias
        sv = sv_scr[...]
        silu, _ = _silu_parts(gt)
        y = (u * sv * silu).astype(BF)
        h1 = xv + _dot(y, waout_ref[...])
        h1_ref[...] = h1
        sv_ref[...] = sv.astype(BF)
        vhat_ref[...] = vhat.astype(BF)
        rstd_ref[...] = jnp.broadcast_to(rstd, rstd_ref.shape)
        rkv = lax.rsqrt(jnp.mean(h1 * h1, axis=-1, keepdims=True) + EPS)
        nkv = (h1 * rkv * gkv_ref[...]).astype(BF)
        kv = _dot(nkv, wkv_ref[...]) + bkv_ref[...]
        k_rot = _rot(kv[:, :LANES], rc_ref[...], rs1_ref[...], rs2_ref[...])
        for src, ref, tref in ((k_rot, k4_ref, kt_ref), (kv[:, LANES:], v4_ref, vt_ref)):
            t4 = _split4(src)
            ref[...] = t4.astype(BF)
            for c in range(nC):
                for b in range(4):
                    blk = t4[c * CHUNK:(c + 1) * CHUNK, b * LANES:(b + 1) * LANES]
                    tref[c, b * LANES:(b + 1) * LANES, :] = blk.T.astype(BF)
        riding.end(i)

    row = functools.partial(_row_spec, TM)
    zcol = [pl.BlockSpec((TM, AW), functools.partial(lambda k, i: (i, k), k)) for k in range(3)]
    tr = pl.BlockSpec((nC, 4 * LANES, CHUNK), lambda i: (i, 0, 0))
    r_in, r_out, r_shape, r_scratch = _Riding.specs(later)
    S = jax.ShapeDtypeStruct
    return pl.pallas_call(
        body, name="a_fwd", grid=(nT,),
        in_specs=[row(D)] + zcol + [_const_spec((1, AW)), _const_spec((1, AW)),
                  _const_spec(ws.shape), _const_spec(bs_t.shape), _const_spec(wa_out.shape), _const_spec((1, D)),
                  _const_spec(w_kv.shape), _const_spec((1, 2 * LANES)), row(LANES), row(LANES), row(LANES)] + r_in,
        out_specs=[row(D), row(AW), row(AW), row(LANES), row(4 * LANES), row(4 * LANES), tr, tr] + r_out,
        out_shape=(S((T, D), F32), S((T, AW), BF), S((T, AW), BF), S((T, LANES), F32),
                   S((T, 4 * LANES), BF), S((T, 4 * LANES), BF),
                   S((T // CHUNK, 4 * LANES, CHUNK), BF), S((T // CHUNK, 4 * LANES, CHUNK), BF)) + r_shape,
        scratch_shapes=[pltpu.VMEM((TM, AW), F32)] + r_scratch,
        compiler_params=_params(("arbitrary",)),
    )(x, z, z, z, ln_g, ln_b, ws, bs_t, wa_out, g_kv, w_kv, b_kv, rc, rs1, rs2, *later)


def _b_fwd(h1, g_b, wb_in, bq, rc, rs1, rs2, k4, vt, sinks, wb_out, g_f, target):
    T, D = h1.shape
    BW = wb_out.shape[0]
    SH = wb_in.shape[2]
    TM = min(512, T)
    nC = TM // CHUNK
    nP = BW // LANES

    def body(h1_ref, gb_ref, wbin_ref, bq_ref, rc_ref, rs1_ref, rs2_ref, k4_ref, vt_ref, sink_ref, wbout_ref, gf_ref,
             tgt_ref, q_ref, g2_ref, o_ref, dh2_ref, dh2b_ref, loss_ref, dgf_ref, z_scr, o_scr):
        i = pl.program_id(0)

        @pl.when(i == 0)
        def _():
            loss_ref[...] = jnp.zeros_like(loss_ref)
            dgf_ref[...] = jnp.zeros_like(dgf_ref)

        h1v = h1_ref[...]
        r2 = lax.rsqrt(jnp.mean(h1v * h1v, axis=-1, keepdims=True) + EPS)
        n2 = (h1v * r2 * gb_ref[...]).astype(BF)
        for j in range(N_DEV):
            z_scr[:, j * SH:(j + 1) * SH] = _dot(n2, wbin_ref[j])
        c_t, s1_t, s2_t = rc_ref[...], rs1_ref[...], rs2_ref[...]
        for p in range(nP):
            cols = slice(p * LANES, (p + 1) * LANES)
            qp = _rot(z_scr[:, cols] + bq_ref[:, cols], c_t, s1_t, s2_t) * (HEAD_DIM ** -0.5)
            q_ref[:, cols] = qp.astype(BF)
        g2 = z_scr[:, BW:]
        g2_ref[...] = g2.astype(BF)
        upper = _upper()
        for c in range(nC):
            ci = i * nC + c
            rows = slice(c * CHUNK, (c + 1) * CHUNK)
            pci = jnp.maximum(ci - 1, 0)
            prev = pl.multiple_of(pci * CHUNK, CHUNK)
            cur = pl.multiple_of(ci * CHUNK, CHUNK)
            qc = q_ref[rows, :]
            for h in range(2):
                st = _dot_nt(_band_rows(k4_ref, prev, cur, h), _stack_pairs(qc, h))
                fa, fb = _fold(st, upper, ci > 0)
                pa, _ = _softmax_sink(fa, sink_ref[2 * h:2 * h + 1, :])
                pb, _ = _softmax_sink(fb, sink_ref[2 * h + 1:2 * h + 2, :])
                ot = _dot(_band_cols(vt_ref, pci, ci, h), _unfold(pa, pb, upper).astype(BF))
                for j in range(4):
                    o_scr[rows, (h * 4 + j) * LANES:(h * 4 + j + 1) * LANES] = ot[:, j * CHUNK:(j + 1) * CHUNK].T
        o = o_scr[...]
        o_ref[...] = o.astype(BF)
        silu, _ = _silu_parts(g2)
        h2 = h1v + _dot((o * silu).astype(BF), wbout_ref[...])
        rf = lax.rsqrt(jnp.mean(h2 * h2, axis=-1, keepdims=True) + EPS)
        xh = h2 * rf
        gf = gf_ref[...]
        err = xh * gf - tgt_ref[...]
        dyf = err * (1.0 / D)
        dh2 = _rms_bwd(dyf, xh, rf, gf)
        dh2_ref[...] = dh2
        dh2b_ref[...] = dh2.astype(BF)
        loss_ref[...] += 0.5 * jnp.sum(jnp.mean(err * err, axis=-1, keepdims=True), axis=0, keepdims=True)
        dgf_ref[...] += jnp.sum(dyf * xh, axis=0, keepdims=True)

    row = functools.partial(_row_spec, TM)
    S = jax.ShapeDtypeStruct
    return pl.pallas_call(
        body, name="b_fwd", grid=(T // TM,),
        in_specs=[row(D), _const_spec((1, D)), _const_spec(wb_in.shape), _const_spec((1, BW)), row(LANES), row(LANES),
                  row(LANES), _const_spec(k4.shape), _const_spec(vt.shape), _const_spec(sinks.shape),
                  _const_spec(wb_out.shape), _const_spec((1, D)), row(D)],
        out_specs=[row(BW), row(BW), row(BW), row(D), row(D), _acc_spec((1, 1)), _acc_spec((1, D))],
        out_shape=(S((T, BW), BF), S((T, BW), BF), S((T, BW), BF), S((T, D), F32), S((T, D), BF), S((1, 1), F32),
                   S((1, D), F32)),
        scratch_shapes=[pltpu.VMEM((TM, 2 * BW), F32), pltpu.VMEM((TM, BW), F32)],
        compiler_params=_params(("arbitrary",)),
    )(h1, g_b, wb_in, bq, rc, rs1, rs2, k4, vt, sinks, wb_out, g_f, target)


def _b_bwd(dh2, h1, q, g2, o, k4, v4, kt, sinks, wb_out, wb_in, g_b, rc, rs1, rs2):
    T, D = h1.shape
    BW = wb_out.shape[0]
    SH = wb_in.shape[2]
    TM = min(256, T)
    nT = T // TM
    nC = TM // CHUNK
    nP = BW // LANES

    def body(dh2_ref, h1_ref, q_ref, g2_ref, o_ref, k4_ref, v4_ref, kt_ref, sink_ref, wbout_ref, wbin_ref, gb_ref,
             rc_ref, rs1_ref, rs2_ref,
             dh1_ref, dz2_ref, n2_ref, y2_ref, dk_ref, dv_ref, dbq_ref, dgb_ref, dsink_ref, do_scr, dq_scr, dsacc_scr):
        i = pl.program_id(0)

        @pl.when(i == 0)
        def _():
            dk_ref[...] = jnp.zeros_like(dk_ref)
            dv_ref[...] = jnp.zeros_like(dv_ref)
            dbq_ref[...] = jnp.zeros_like(dbq_ref)
            dgb_ref[...] = jnp.zeros_like(dgb_ref)
            dsacc_scr[...] = jnp.zeros_like(dsacc_scr)

        dh2 = dh2_ref[...]
        dy2 = _dot_nt(dh2.astype(BF), wbout_ref[...])
        silu, dsilu = _silu_parts(g2_ref[...].astype(F32))
        do_scr[...] = (dy2 * silu).astype(BF)
        dy2, silu, dsilu = dy2.astype(BF), silu.astype(BF), dsilu.astype(BF)
        ob = o_ref[...]
        y2_ref[...] = (ob * silu).T
        dz2_ref[:, BW:] = dy2 * ob * dsilu
        upper = _upper()
        lo = _lane_lo((2 * CHUNK, LANES))
        for c in range(nC):
            ci = i * nC + c
            rows = slice(c * CHUNK, (c + 1) * CHUNK)
            pci = jnp.maximum(ci - 1, 0)
            prev = pl.multiple_of(pci * CHUNK, CHUNK)
            cur = pl.multiple_of(ci * CHUNK, CHUNK)
            qc = q_ref[rows, :]
            doc = do_scr[rows, :]
            dkb = jnp.zeros((2 * CHUNK, LANES), F32)
            dvb = jnp.zeros((2 * CHUNK, LANES), F32)
            for h in range(2):
                qs = _stack_pairs(qc, h)
                dos = _stack_pairs(doc, h)
                fa, fb = _fold(_dot_nt(_band_rows(k4_ref, prev, cur, h), qs), upper, ci > 0)
                dfa, dfb = _fold(_dot_nt(_band_rows(v4_ref, prev, cur, h), dos), upper)
                folded = []
                for k, (f, df) in enumerate(((fa, dfa), (fb, dfb))):
                    p, ps = _softmax_sink(f, sink_ref[2 * h + k:2 * h + k + 1, :])
                    delta = jnp.sum(p * df, axis=0, keepdims=True)
                    dsacc_scr[2 * h + k:2 * h + k + 1, :] -= ps * delta
                    folded.append((p, p * (df - delta)))
                pt = _unfold(folded[0][0], folded[1][0], upper).astype(BF)
                dst = _unfold(folded[0][1], folded[1][1], upper).astype(BF)
                dqt = _dot(_band_cols(kt_ref, pci, ci, h), dst)
                for j in range(4):
                    dq_scr[rows, (h * 4 + j) * LANES:(h * 4 + j + 1) * LANES] = dqt[:, j * CHUNK:(j + 1) * CHUNK].T
                for acc_name, g in (("k", _dot(dst, qs)), ("v", _dot(pt, dos))):
                    a, b = g[:2 * CHUNK], g[2 * CHUNK:]
                    if h == 0:
                        part = jnp.where(lo, a + pltpu.roll(b, HEAD_DIM, 1), 0.0)
                    else:
                        part = jnp.where(lo, 0.0, pltpu.roll(a, HEAD_DIM, 1) + b)
                    if acc_name == "k":
                        dkb += part
                    else:
                        dvb += part
            dk_ref[pl.ds(prev, CHUNK), :] += dkb[:CHUNK]
            dk_ref[pl.ds(cur, CHUNK), :] += dkb[CHUNK:]
            dv_ref[pl.ds(prev, CHUNK), :] += dvb[:CHUNK]
            dv_ref[pl.ds(cur, CHUNK), :] += dvb[CHUNK:]
        c_t, s1_t, s2_t = rc_ref[...], rs1_ref[...], rs2_ref[...]
        for p in range(nP):
            cols = slice(p * LANES, (p + 1) * LANES)
            dqp = _rot_bwd(dq_scr[:, cols] * (HEAD_DIM ** -0.5), c_t, s1_t, s2_t)
            dbq_ref[:, cols] += jnp.sum(dqp, axis=0, keepdims=True)
            dz2_ref[:, cols] = dqp.astype(BF)
        h1v = h1_ref[...]
        r2 = lax.rsqrt(jnp.mean(h1v * h1v, axis=-1, keepdims=True) + EPS)
        xh = h1v * r2
        gb = gb_ref[...]
        n2_ref[...] = (xh * gb).astype(BF).T
        dn2 = None
        for j in range(N_DEV):
            part = _dot_nt(dz2_ref[:, j * SH:(j + 1) * SH], wbin_ref[j])
            dn2 = part if dn2 is None else dn2 + part
        dgb_ref[...] += jnp.sum(dn2 * xh, axis=0, keepdims=True)
        dh1_ref[...] = dh2 + _rms_bwd(dn2, xh, r2, gb)

        @pl.when(i == nT - 1)
        def _():
            lane = lax.broadcasted_iota(jnp.int32, dsink_ref.shape, 1)
            tot = jnp.zeros(dsink_ref.shape, F32)
            for j in range(4):
                tot += jnp.where(lane == j, jnp.sum(dsacc_scr[:, j * CHUNK:(j + 1) * CHUNK], axis=1, keepdims=True), 0.0)
            dsink_ref[...] = tot

    row = functools.partial(_row_spec, TM)
    S = jax.ShapeDtypeStruct
    return pl.pallas_call(
        body, name="b_bwd", grid=(T // TM,),
        in_specs=[row(D), row(D), row(BW), row(BW), row(BW), _const_spec(k4.shape), _const_spec(v4.shape),
                  _const_spec(kt.shape), _const_spec(sinks.shape), _const_spec(wb_out.shape), _const_spec(wb_in.shape),
                  _const_spec((1, D)), row(LANES), row(LANES), row(LANES)],
        out_specs=[row(D), row(2 * BW), _col_spec(TM, D), _col_spec(TM, BW), _acc_spec((T, LANES)),
                   _acc_spec((T, LANES)), _acc_spec((1, BW)), _acc_spec((1, D)), _acc_spec((4, LANES))],
        out_shape=(S((T, D), F32), S((T, 2 * BW), BF), S((D, T), BF), S((BW, T), BF), S((T, LANES), F32),
                   S((T, LANES), F32), S((1, BW), F32), S((1, D), F32), S((4, LANES), F32)),
        scratch_shapes=[pltpu.VMEM((TM, BW), BF), pltpu.VMEM((TM, BW), F32), pltpu.VMEM((4, 4 * CHUNK), F32)],
        compiler_params=_params(("arbitrary",)),
    )(dh2, h1, q, g2, o, k4, v4, kt, sinks, wb_out, wb_in, g_b, rc, rs1, rs2)


def _a_bwd(dh1p, dk, dv, h1, g_kv, w_kv, wa_out, ws, ln_g, ln_b, z, sv, vhat, rstd, rc, rs1, rs2, ready):
    T, D = h1.shape
    AW = wa_out.shape[0]
    G = ws.shape[0]
    TM = min(256, T)
    nT = T // TM
    nC = TM // CHUNK
    nr = len(ready)

    def body(dh1p_ref, dk_ref, dv_ref, h1_ref, gkv_ref, wkv_ref, waout_ref, ws_ref, lng_ref,
             lnb_ref, u_ref, gt_ref, sv_ref, vhat_ref, rstd_ref, rc_ref, rs1_ref, rs2_ref, *rest):
        ready_refs, rest = rest[:nr], rest[nr:]
        (dz_ref, gwo_ref, gwk_ref, dh1f_ref, dgkv_ref, dbkv_ref, dlng_ref, dlnb_ref,
         dws_ref, dbs_ref), rest = rest[:10], rest[10:]
        recv_refs, (dsv_scr, dvln_scr, acco_scr, acck_scr, ssem, rsem, lsem) = rest[:nr], rest[nr:]
        i = pl.program_id(0)
        exchanges = [_Direct(ready_refs[k], recv_refs[k], ssem.at[k], rsem.at[k], lsem.at[k], scatter=True)
                     for k in range(nr)]

        @pl.when(i == 0)
        def _():
            for e in exchanges:
                e.start()
            for r in (dgkv_ref, dbkv_ref, dlng_ref, dlnb_ref, dws_ref, dbs_ref, acco_scr, acck_scr):
                r[...] = jnp.zeros_like(r)

        dk_pre = _rot_bwd(dk_ref[...], rc_ref[...], rs1_ref[...], rs2_ref[...])
        dkv = jnp.concatenate([dk_pre, dv_ref[...]], axis=1)
        dbkv_ref[...] += jnp.sum(dkv, axis=0, keepdims=True)
        dkv_b = dkv.astype(BF)
        h1v = h1_ref[...]
        rkv = lax.rsqrt(jnp.mean(h1v * h1v, axis=-1, keepdims=True) + EPS)
        xh_kv = h1v * rkv
        gkv = gkv_ref[...]
        acck_scr[...] += _dot((xh_kv * gkv).astype(BF).T, dkv_b)
        dnkv = _dot_nt(dkv_b, wkv_ref[...])
        dgkv_ref[...] += jnp.sum(dnkv * xh_kv, axis=0, keepdims=True)
        dh1 = dh1p_ref[...] + _rms_bwd(dnkv, xh_kv, rkv, gkv)
        dh1_b = dh1.astype(BF)
        dh1f_ref[...] = dh1
        dy = _dot_nt(dh1_b, waout_ref[...]).astype(BF)
        silu, dsilu = _silu_parts(gt_ref[...].astype(F32))
        silu, dsilu = silu.astype(BF), dsilu.astype(BF)
        ub, svb = u_ref[...], sv_ref[...]
        us = ub * silu
        dys = dy * svb
        acco_scr[...] += _dot((us * svb).T, dh1_b)
        dz_ref[:, :AW] = dys * silu
        dz_ref[:, 2 * AW:] = dys * ub * dsilu
        dsv_scr[...] = dy * us
        vhat_v = vhat_ref[...].astype(F32)
        lng = lng_ref[...]
        vln_b = (vhat_v * lng + lnb_ref[...]).astype(BF)
        tri = lax.broadcasted_iota(jnp.int32, (CHUNK, CHUNK), 0) >= lax.broadcasted_iota(jnp.int32, (CHUNK, CHUNK), 1)
        lane = lax.broadcasted_iota(jnp.int32, (CHUNK, LANES), 1)
        dbs = jnp.zeros((CHUNK, LANES), F32)
        for g in range(G):
            wsm = jnp.where(tri, ws_ref[g], 0.0).astype(BF)
            cols = slice(g * CHUNK, (g + 1) * CHUNK)
            dws_g = None
            for c in range(nC):
                rows = slice(c * CHUNK, (c + 1) * CHUNK)
                dsv_cg = dsv_scr[rows, cols]
                dvln_scr[rows, cols] = _dot_tn(wsm, dsv_cg)
                part = _dot_nt(dsv_cg, vln_b[rows, cols])
                dws_g = part if dws_g is None else dws_g + part
                dbs += jnp.where(lane == g, jnp.sum(dsv_cg.astype(F32), axis=-1, keepdims=True), 0.0)
            dws_ref[g] += jnp.where(tri, dws_g, 0.0)
        dbs_ref[...] += dbs
        dvln = dvln_scr[...]
        dlng_ref[...] += jnp.sum(dvln * vhat_v, axis=0, keepdims=True)
        dlnb_ref[...] += jnp.sum(dvln, axis=0, keepdims=True)
        a = dvln * lng
        dvv = rstd_ref[:, 0:1] * (a - jnp.mean(a, axis=-1, keepdims=True)
                                  - vhat_v * jnp.mean(a * vhat_v, axis=-1, keepdims=True))
        dz_ref[:, AW:2 * AW] = dvv.astype(BF)

        @pl.when(i == nT - 1)
        def _():
            for j in range(N_DEV):
                gwo_ref[j] = acco_scr[j * (AW // N_DEV):(j + 1) * (AW // N_DEV)].astype(BF)
                gwk_ref[j] = acck_scr[j * (D // N_DEV):(j + 1) * (D // N_DEV)].astype(BF)
            for e in exchanges:
                e.finish()

    row = functools.partial(_row_spec, TM)
    hbm = pl.BlockSpec(memory_space=pl.ANY)
    S = jax.ShapeDtypeStruct
    gwo_shape, gwk_shape = (N_DEV, AW // N_DEV, D), (N_DEV, D // N_DEV, 2 * LANES)
    return pl.pallas_call(
        body, name="a_bwd", grid=(nT,),
        in_specs=[row(D), row(LANES), row(LANES), row(D), _const_spec((1, D)), _const_spec(w_kv.shape),
                  _const_spec(wa_out.shape), _const_spec(ws.shape),
                  _const_spec((1, AW)), _const_spec((1, AW)), pl.BlockSpec((TM, AW), lambda i: (i, 0)),
                  pl.BlockSpec((TM, AW), lambda i: (i, 2)), row(AW), row(AW), row(LANES),
                  row(LANES), row(LANES), row(LANES)] + [hbm] * nr,
        out_specs=[row(3 * AW), _const_spec(gwo_shape), _const_spec(gwk_shape), row(D),
                   _acc_spec((1, D)), _acc_spec((1, 2 * LANES)), _acc_spec((1, AW)),
                   _acc_spec((1, AW)), _acc_spec(ws.shape), _acc_spec((CHUNK, LANES))] + [hbm] * nr,
        out_shape=(S((T, 3 * AW), BF), S(gwo_shape, BF), S(gwk_shape, BF), S((T, D), F32),
                   S((1, D), F32), S((1, 2 * LANES), F32), S((1, AW), F32), S((1, AW), F32),
                   S(ws.shape, F32), S((CHUNK, LANES), F32)) + tuple(S(r.shape, r.dtype) for r in ready),
        scratch_shapes=[pltpu.VMEM((TM, AW), BF), pltpu.VMEM((TM, AW), F32), pltpu.VMEM((AW, D), F32),
                        pltpu.VMEM((D, 2 * LANES), F32)] + _direct_sems(nr),
        compiler_params=_params(("arbitrary",)),
    )(dh1p, dk, dv, h1, g_kv, w_kv, wa_out, ws, ln_g, ln_b, z, z, sv, vhat, rstd, rc, rs1, rs2, *ready)


def _a_in_bwd(dz, wa_in, x, dh1, g_a, ready):
    T, D = x.shape
    SH = wa_in.shape[2]
    TM = min(512, T)
    nT = T // TM
    nr = len(ready)

    def body(dz_ref, wain_ref, x_ref, dh1_ref, ga_ref, *rest):
        ready_refs, (dx_ref, n1_ref, dga_ref), rest = rest[:nr], rest[nr:nr + 3], rest[nr + 3:]
        recv_refs, (ssem, rsem, lsem) = rest[:nr], rest[nr:]
        i = pl.program_id(0)
        exchanges = [_Direct(ready_refs[k], recv_refs[k], ssem.at[k], rsem.at[k], lsem.at[k], scatter=True)
                     for k in range(nr)]

        @pl.when(i == 0)
        def _():
            for e in exchanges:
                e.start()
            dga_ref[...] = jnp.zeros_like(dga_ref)

        xv = x_ref[...]
        r1 = lax.rsqrt(jnp.mean(xv * xv, axis=-1, keepdims=True) + EPS)
        xh = xv * r1
        ga = ga_ref[...]
        n1_ref[...] = (xh * ga).astype(BF).T
        dn1 = None
        for j in range(N_DEV):
            part = _dot_nt(dz_ref[:, j * SH:(j + 1) * SH], wain_ref[j])
            dn1 = part if dn1 is None else dn1 + part
        dga_ref[...] += jnp.sum(dn1 * xh, axis=0, keepdims=True)
        dx_ref[...] = dh1_ref[...] + _rms_bwd(dn1, xh, r1, ga)

        @pl.when(i == nT - 1)
        def _():
            for e in exchanges:
                e.finish()

    row = functools.partial(_row_spec, TM)
    hbm = pl.BlockSpec(memory_space=pl.ANY)
    S = jax.ShapeDtypeStruct
    return pl.pallas_call(
        body, name="a_in_bwd", grid=(nT,),
        in_specs=[row(dz.shape[1]), _const_spec(wa_in.shape), row(D), row(D), _const_spec((1, D))] + [hbm] * nr,
        out_specs=[row(D), _col_spec(TM, D), _acc_spec((1, D))] + [hbm] * nr,
        out_shape=(S((T, D), F32), S((D, T), BF), S((1, D), F32)) + tuple(S(r.shape, r.dtype) for r in ready),
        scratch_shapes=_direct_sems(nr),
        compiler_params=_params(("arbitrary",)),
    )(dz, wa_in, x, dh1, g_a, *ready)


def _wgrad(at, b, nblk, name, bt=512):
    K, T = at.shape
    N = b.shape[1] // nblk
    BT = min(bt, T)
    nt = T // BT

    def body(a_ref, b_ref, o_ref, acc):
        t = pl.program_id(0)

        @pl.when(t == 0)
        def _():
            acc[...] = jnp.zeros_like(acc)

        acc[...] += _dot(a_ref[...], b_ref[...])

        @pl.when(t == nt - 1)
        def _():
            for j in range(nblk):
                o_ref[j] = acc[:, j * N:(j + 1) * N].astype(BF)

    return pl.pallas_call(
        body, name=name, grid=(nt,),
        in_specs=[pl.BlockSpec((K, BT), lambda t: (0, t)), pl.BlockSpec((BT, nblk * N), lambda t: (t, 0))],
        out_specs=pl.BlockSpec((nblk, K, N), lambda t: (0, 0, 0)),
        out_shape=jax.ShapeDtypeStruct((nblk, K, N), BF),
        scratch_shapes=[pltpu.VMEM((K, nblk * N), F32)],
        compiler_params=_params(("arbitrary",)),
    )(at, b)


def _wgrad_exchange(a, b, me, small, name):
    K, T = a.shape
    N = b.shape[1] // N_DEV
    BT = T
    nt = T // BT
    last = N_DEV - 1
    n_chip = N_DEV // 2

    def far_of(k, core):
        return jnp.where((core == 0) & ((k == 1) | (k == 2)), k, n_chip - 1 - k)

    def block_of(s, me_i):
        k, odd = s // 2, s % 2
        core = me_i & 1
        return me_i ^ ((far_of(k, jnp.where(odd == 1, core, 1 - core)) << 1) | (1 - odd))

    H = K // 2

    def body(me_ref, a_ref, b_ref, small_ref, recv_ref, full_ref, *scratch):
        (acc, dstage, istage, half, relay, d_s, d_r, i_s, i_r, r_s, r_r, lsem, parts_scr, red_scr, e_s, e_r, e_l, g_s,
         g_r, g_l) = scratch
        s, t = pl.program_id(0), pl.program_id(1)
        x, y, c = (lax.axis_index(ax) for ax in AXES)
        ex = [_Direct(small_ref, parts_scr, e_s, e_r, e_l, scatter=True)]
        regather = _TwoLevel(red_scr, full_ref, g_s, g_r, g_l)

        def to_sibling(k, slot):
            return pltpu.make_async_remote_copy(src_ref=dstage.at[slot], dst_ref=half.at[k], send_sem=d_s.at[k],
                                                recv_sem=d_r.at[k], device_id=(x, y, 1 - c), device_id_type=MESH)

        def to_chip(k, slot):
            over_x = far_of(k, c) == 2
            px, py = jnp.where(over_x, 1 - x, x), jnp.where(over_x, y, 1 - y)
            return pltpu.make_async_remote_copy(src_ref=istage.at[slot], dst_ref=recv_ref.at[jnp.where(over_x, 1, 2)],
                                                send_sem=i_s.at[k], recv_sem=i_r.at[k], device_id=(px, py, c),
                                                device_id_type=MESH)

        def to_relay(j, slot):
            to = (1 - x, y, c) if j == 0 else (x, 1 - y, c)
            return pltpu.make_async_remote_copy(src_ref=istage.at[slot, pl.ds(j * H, H)], dst_ref=relay.at[j],
                                                send_sem=r_s.at[j], recv_sem=r_r.at[j], device_id=to,
                                                device_id_type=MESH)

        @pl.when((s == 0) & (t == 0))
        def _():
            for e in ex:
                e.start()

        acc[...] = _dot(a_ref[...], b_ref[...])

        @pl.when(t == nt - 1)
        def _():
            k = lax.div(s, 2)
            slot = lax.rem(k, 2)

            @pl.when(lax.rem(s, 2) == 0)
            def _():
                @pl.when(k >= 2)
                def _():
                    to_sibling(k - 2, slot).wait_send()

                dstage[slot] = acc[...].astype(BF)
                to_sibling(k, slot).start()

            @pl.when(lax.rem(s, 2) == 1)
            def _():
                to_sibling(k, slot).wait_recv()
                pair = acc[...] + half[k].astype(F32)

                @pl.when(k == 0)
                def _():
                    istage[slot] = pair.astype(BF)
                    for j in range(2):
                        to_relay(j, slot).start()

                @pl.when(k == 1)
                def _():
                    for j in range(2):
                        to_relay(j, slot).wait_recv()

                @pl.when(k == 2)
                def _():
                    for j in range(2):
                        to_relay(j, slot).wait_send()

                @pl.when(k == n_chip - 1)
                def _():
                    to_chip(1, slot).wait_send()
                    istage[slot] = pair.astype(BF)

                @pl.when((k == 1) | (k == 2))
                def _():
                    over_x = far_of(k, c) == 2
                    istage[slot, 0:H] = (pair[:H] + jnp.where(over_x, 0.0, relay[0].astype(F32))).astype(BF)
                    istage[slot, H:K] = (pair[H:] + jnp.where(over_x, relay[1].astype(F32), 0.0)).astype(BF)
                    to_chip(k, slot).start()

            @pl.when(s == last)
            def _():
                own = pltpu.make_async_copy(istage.at[slot], recv_ref.at[0], lsem)
                own.start()
                to_chip(2, 0).wait_send()
                to_sibling(n_chip - 2, 0).wait_send()
                to_sibling(n_chip - 1, 1).wait_send()
                for kk in (1, 2):
                    to_chip(kk, 0).wait_recv()
                own.wait()
                for e in ex:
                    e.finish()
                total = parts_scr[0]
                for dev in range(1, N_DEV):
                    total = total + parts_scr[dev]
                red_scr[...] = total
                regather.start()
                regather.forward()
                regather.finish()

    hbm = pl.BlockSpec(memory_space=pl.ANY)
    dma = pltpu.SemaphoreType.DMA
    grid_spec = pltpu.PrefetchScalarGridSpec(
        num_scalar_prefetch=1, grid=(N_DEV, nt),
        in_specs=[pl.BlockSpec((K, BT), lambda s, t, me_ref: (0, t), pipeline_mode=pl.Buffered(1)),
                  pl.BlockSpec((BT, N), lambda s, t, me_ref: (t, block_of(s, me_ref[0]))), hbm],
        out_specs=[hbm, hbm],
        scratch_shapes=[pltpu.VMEM((K, N), F32), pltpu.VMEM((2, K, N), BF), pltpu.VMEM((2, K, N), BF),
                        pltpu.VMEM((n_chip, K, N), BF), pltpu.VMEM((2, H, N), BF), dma((n_chip,)), dma((n_chip,)),
                        dma((n_chip - 1,)), dma((n_chip - 1,)), dma((2,)), dma((2,)), dma,
                        pltpu.VMEM(small.shape, F32), pltpu.VMEM(small.shape[1:], F32),
                        dma((last,)), dma((last,)), dma, dma((last,)), dma((last,)), dma])
    return pl.pallas_call(
        body, name=name, grid_spec=grid_spec,
        out_shape=[jax.ShapeDtypeStruct((n_chip - 1, K, N), BF), jax.ShapeDtypeStruct(small.shape, F32)],
        compiler_params=_params(("arbitrary", "arbitrary")),
    )(me, a, b, small)


def _my_index():
    return 4 * lax.axis_index("x") + 2 * lax.axis_index("y") + lax.axis_index("c")


def _peer(mask):
    x, y, c = (lax.axis_index(a) for a in AXES)
    return (x ^ ((mask >> 2) & 1), y ^ ((mask >> 1) & 1), c ^ (mask & 1))


def _dev_index(p):
    return 4 * p[0] + 2 * p[1] + p[2]


class _Direct:
    def __init__(self, src, dst, send_sems, recv_sems, local_sem, scatter):
        me = _my_index()
        self.own = pltpu.make_async_copy(src.at[me] if scatter else src, dst.at[me], local_sem)
        self.sends, self.recvs = [], []
        for k in range(1, N_DEV):
            p = _peer(k)
            pi = _dev_index(p)
            sems = dict(send_sem=send_sems.at[k - 1], recv_sem=recv_sems.at[k - 1], device_id=p, device_id_type=MESH)
            self.sends.append(pltpu.make_async_remote_copy(src_ref=src.at[pi] if scatter else src, dst_ref=dst.at[me],
                                                           **sems))
            self.recvs.append(pltpu.make_async_remote_copy(src_ref=src.at[me] if scatter else src, dst_ref=dst.at[pi],
                                                           **sems))

    def start(self):
        self.own.start()
        for cp in self.sends:
            cp.start()

    def finish(self):
        for cp in self.sends:
            cp.wait_send()
        for cp in self.recvs:
            cp.wait_recv()
        self.own.wait()


class _TwoLevel:
    def __init__(self, src, dst, send_sems, recv_sems, local_sem, own=True):
        x, y, c = (lax.axis_index(a) for a in AXES)
        self.me, self.sibling = (x, y, c), (x, y, 1 - c)
        self.chips = [(1 - x, y), (x, 1 - y), (1 - x, 1 - y)]
        self.src, self.dst, self.send_sems, self.recv_sems = src, dst, send_sems, recv_sems
        self.own = pltpu.make_async_copy(src, dst.at[_dev_index(self.me)], local_sem) if own else None

    def _copy(self, k, block, to, from_src=False):
        slot = self.dst.at[_dev_index(block)]
        return pltpu.make_async_remote_copy(src_ref=self.src if from_src else slot, dst_ref=slot,
                                            send_sem=self.send_sems.at[k], recv_sem=self.recv_sems.at[k],
                                            device_id=to, device_id_type=MESH)

    def _firsts(self):
        c = self.me[2]
        return [self._copy(0, self.me, self.sibling, True)] + [self._copy(1 + j, self.me, (*chip, c), True)
                                                               for j, chip in enumerate(self.chips)]

    def _passed(self):
        c = self.me[2]
        return [self._copy(4 + j, (*chip, c), self.sibling) for j, chip in enumerate(self.chips)]

    def start(self):
        if self.own is not None:
            self.own.start()
        for cp in self._firsts():
            cp.start()

    def wait_sibling(self):
        self._copy(0, self.sibling, self.me).wait_recv()

    def wait_chip_and_forward(self, j):
        self._copy(1 + j, (*self.chips[j], self.me[2]), self.me).wait_recv()
        self._passed()[j].start()

    def wait_passed(self, j):
        self._copy(4 + j, (*self.chips[j], 1 - self.me[2]), self.me).wait_recv()

    def wait_sends(self):
        for cp in self._firsts() + self._passed():
            cp.wait_send()
        if self.own is not None:
            self.own.wait()

    def forward(self):
        for j in range(3):
            self.wait_chip_and_forward(j)

    def finish(self):
        self.wait_sibling()
        for j in range(3):
            self.wait_passed(j)
        self.wait_sends()


class _RelayGather:
    def __init__(self, dst, send_sems, recv_sems):
        x, y, c = (lax.axis_index(a) for a in AXES)
        self.c = c
        self.sib, self.xn, self.yn, self.dg = (x, y, 1 - c), (1 - x, y, c), (x, 1 - y, c), (1 - x, 1 - y, c)
        self.me = (x, y, c)
        self.dst, self.send_sems, self.recv_sems = dst, send_sems, recv_sems
        self.half = dst.shape[1] // 2

    def _slot(self, dev, part=None):
        i = _dev_index(dev)
        if part is None:
            return self.dst.at[i]
        return self.dst.at[i, pl.ds(part * self.half, self.half)]

    def _copy(self, k, dev, to, part=None):
        ref = self._slot(dev, part)
        return pltpu.make_async_remote_copy(src_ref=ref, dst_ref=ref, send_sem=self.send_sems.at[k],
                                            recv_sem=self.recv_sems.at[k], device_id=to, device_id_type=MESH)

    def _other(self, dev):
        return (dev[0], dev[1], 1 - self.c)

    def start(self):
        for k, to in enumerate((self.sib, self.xn, self.yn)):
            self._copy(k, self.me, to).start()

    def send_own(self, k):
        return self._copy(k, self.me, (self.sib, self.xn, self.yn)[k])

    def wait_sibling(self):
        self._copy(0, self.sib, self.me).wait_recv()

    def on_x(self):
        self._copy(1, self.xn, self.me).wait_recv()
        self._copy(3, self.xn, self.yn, part=0).start()
        self._copy(5, self.xn, self.sib).start()

    def on_y(self):
        self._copy(2, self.yn, self.me).wait_recv()
        self._copy(4, self.yn, self.xn, part=1).start()
        self._copy(6, self.yn, self.sib).start()

    def on_diag(self):
        self._copy(3, self.dg, self.me, part=0).wait_recv()
        self._copy(4, self.dg, self.me, part=1).wait_recv()
        self._copy(7, self.dg, self.sib).start()

    def wait_passed(self, j):
        self._copy(5 + j, self._other((self.xn, self.yn, self.dg)[j]), self.me).wait_recv()

    def wait_sends(self):
        for k, to in enumerate((self.sib, self.xn, self.yn)):
            self._copy(k, self.me, to).wait_send()
        self._copy(3, self.xn, self.yn, part=0).wait_send()
        self._copy(4, self.yn, self.xn, part=1).wait_send()
        for j, dev in enumerate((self.xn, self.yn, self.dg)):
            self._copy(5 + j, dev, self.sib).wait_send()


def _direct_sems(n):
    if n == 0:
        return []
    return [pltpu.SemaphoreType.DMA((n, 7)), pltpu.SemaphoreType.DMA((n, 7)), pltpu.SemaphoreType.DMA((n,))]


def _adam_math(w, g, m, v):
    m = ADAM_B1 * m + (1.0 - ADAM_B1) * g
    v = ADAM_B2 * v + (1.0 - ADAM_B2) * (g * g)
    m_hat = m / (1.0 - ADAM_B1 ** ADAM_STEP)
    v_hat = v / (1.0 - ADAM_B2 ** ADAM_STEP)
    delta = -ADAM_LR * (m_hat / (jnp.sqrt(v_hat) + ADAM_EPS) + ADAM_WD * w)
    return delta, m, v


def _sum_adam(parts, w, m, v, name):
    R, C = w.shape
    NP = parts.shape[0]
    BR = CHUNK if R % CHUNK == 0 else R

    def body(p_ref, w_ref, m_ref, v_ref, g_ref, d_ref, nm_ref, nv_ref):
        g = p_ref[0].astype(F32)
        for i in range(1, NP):
            g = g + p_ref[i].astype(F32)
        g_ref[...] = g
        d_ref[...], nm_ref[...], nv_ref[...] = _adam_math(w_ref[...], g, m_ref[...], v_ref[...])

    blk = pl.BlockSpec((BR, C), lambda i: (i, 0))
    S = jax.ShapeDtypeStruct((R, C), F32)
    return pl.pallas_call(
        body, name=name, grid=(R // BR,),
        in_specs=[pl.BlockSpec((NP, BR, C), lambda i: (0, i, 0)), blk, blk, blk],
        out_specs=[blk] * 4, out_shape=(S,) * 4,
        compiler_params=_params(("arbitrary",)),
    )(parts, w, m, v)


SUBLANES = 8


def _nrows(size):
    return -(-size // (SUBLANES * LANES)) * SUBLANES


def _view2d(a):
    return a.reshape(-1, LANES) if a.size % LANES == 0 else a.reshape(1, -1)


def _pack_small(parts, total_rows, name):
    arrs = [p[0] for p in parts]

    def body(*refs):
        out = refs[-1]
        out[...] = jnp.zeros_like(out)
        at = 0
        for ref, (a, rows, flag) in zip(refs[:-1], parts):
            val = ref[...].T if flag == "T" else ref[...]
            r, c = (rows, val.shape[1]) if flag == "T" else val.shape
            out[at:at + r, 0:c] = val[:r]
            at += _nrows(r * c)

    return pl.pallas_call(body, name=name, out_shape=jax.ShapeDtypeStruct((total_rows, LANES), F32))(*arrs)


def _small_update(full, me, reps, shards, name):
    n = len(reps) + len(shards)

    def body(me_ref, full_ref, *refs):
        ins, outs = refs[:3 * n], refs[3 * n:]
        at = 0
        for k in range(n):
            w_ref, m_ref, v_ref = ins[3 * k:3 * k + 3]
            r, c = w_ref.shape
            if k < len(reps):
                g = full_ref[at:at + r, 0:c]
                at += _nrows(r * c)
            else:
                seg = full_ref[at:at + N_DEV * r, :]
                row = lax.broadcasted_iota(jnp.int32, seg.shape, 0)
                pick = [jnp.sum(jnp.where(row == r * me_ref[0] + t, seg, 0.0), axis=0, keepdims=True) for t in range(r)]
                g = pick[0] if r == 1 else jnp.concatenate(pick, axis=0)
                at += N_DEV * r
            g_ref, d_ref, nm_ref, nv_ref = outs[4 * k:4 * k + 4]
            g_ref[...] = g
            d_ref[...], nm_ref[...], nv_ref[...] = _adam_math(w_ref[...], g, m_ref[...], v_ref[...])
        outs[4 * n][...] = full_ref[at:at + 1, 0:1]

    flat = [t for p in reps + shards for t in p]
    S = jax.ShapeDtypeStruct
    res = pl.pallas_call(
        body, name=name,
        in_specs=[pl.BlockSpec(memory_space=pltpu.SMEM)] + [pl.BlockSpec(memory_space=pltpu.VMEM)] * (1 + len(flat)),
        out_shape=[S(p[0].shape, F32) for p in reps + shards for _ in range(4)] + [S((1, 1), F32)],
    )(me, full, *flat)
    return [tuple(res[4 * k:4 * k + 4]) for k in range(n)], res[4 * n]


def _rope_tables(T):
    pos = np.arange(T, dtype=np.float32)
    inv_freq = (np.float64(ROPE_THETA) ** (-np.arange(0, HEAD_DIM, 2, dtype=np.float64) / HEAD_DIM)).astype(np.float32)
    ang = (pos[:, None] * inv_freq[None, :]).astype(np.float64)
    cos, sin, zero = np.cos(ang).astype(np.float32), np.sin(ang).astype(np.float32), np.zeros(ang.shape, np.float32)
    c = np.concatenate([cos, cos, cos, cos], axis=1)
    s1 = np.concatenate([-sin, zero, -sin, zero], axis=1)
    s2 = np.concatenate([zero, sin, zero, sin], axis=1)
    return jnp.asarray(c), jnp.asarray(s1), jnp.asarray(s2)


def kernel(x, a_norm_g, a_w_in, a_ln_g, a_ln_b, a_ws, a_bs, a_w_out, kv_norm_g, w_kv, b_kv, b_norm_g, b_w_in, b_bq, b_sinks, b_w_out, final_norm_g, loss_target, m_a_norm_g, m_a_w_in, m_a_ln_g, m_a_ln_b, m_a_ws, m_a_bs, m_a_w_out, m_kv_norm_g, m_w_kv, m_b_kv, m_b_norm_g, m_b_w_in, m_b_bq, m_b_sinks, m_b_w_out, m_final_norm_g, v_a_norm_g, v_a_w_in, v_a_ln_g, v_a_ln_b, v_a_ws, v_a_bs, v_a_w_out, v_kv_norm_g, v_w_kv, v_b_kv, v_b_norm_g, v_b_w_in, v_b_bq, v_b_sinks, v_b_w_out, v_final_norm_g):
    T, D = x.shape[1], x.shape[2]
    AW = a_ln_g.shape[1] * N_DEV
    G = a_ws.shape[1]
    assert w_kv.shape[1] == 2 * LANES and a_ws.shape[2] == CHUNK and T % CHUNK == 0
    me = _my_index()

    xs, tgt = x[0], loss_target[0]
    vec = jnp.concatenate([a_norm_g, a_ln_g, a_ln_b], axis=1)
    vec = jnp.broadcast_to(vec, (SUBLANES, vec.shape[1]))
    north = lax.axis_index("c") == 1
    slots = me ^ jnp.where(north, jnp.array(PASS_MASKS[1], jnp.int32), jnp.array(PASS_MASKS[0], jnp.int32))
    z, wa_in, vecs, wa_out, wkv = _in_proj(xs, a_w_in[0], vec, slots, [a_w_out[0], w_kv])
    wa_out = wa_out.reshape(AW, D)
    wkv = wkv.reshape(D, 2 * LANES)
    vecs = vecs[:, 0, :]
    ds = D // N_DEV
    g_a = vecs[:, :ds].reshape(1, D)
    ln_g = vecs[:, ds:ds + AW // N_DEV].reshape(1, AW)
    ln_b = vecs[:, ds + AW // N_DEV:].reshape(1, AW)

    rc, rs1, rs2 = _rope_tables(T)
    ws = a_ws[0]
    bs_t = a_bs[0].T
    g_kv = kv_norm_g.reshape(1, D)
    bkv = b_kv.reshape(1, -1)
    g_f = final_norm_g.reshape(1, D)
    sinks = jnp.repeat(b_sinks.reshape(2, 4, 2).transpose(0, 2, 1).reshape(4, 4), CHUNK, axis=1)
    h1, sv, vhat, rstd, k4, v4, kt, vt, wb_in, wb_out = _a_fwd(
        xs, z, ln_g, ln_b, ws, bs_t, wa_out, g_kv, wkv, bkv, rc, rs1, rs2, [b_w_in[0], b_w_out[0]])
    wb_out = wb_out.reshape(-1, D)
    q, g2, o, dh2, dh2_b, loss, d_gf = _b_fwd(h1, b_norm_g, wb_in, b_bq, rc, rs1, rs2, k4, vt, sinks, wb_out, g_f, tgt)
    dh1p, dz2, n2, y2, dk, dv, d_bq, d_gb, d_sink = _b_bwd(dh2, h1, q, g2, o, k4, v4, kt, sinks, wb_out, wb_in,
                                                           b_norm_g, rc, rs1, rs2)
    d_sink = d_sink[:, :4].reshape(2, 2, 4).transpose(0, 2, 1).reshape(1, 16)
    gw_b_in = _wgrad(n2, dz2, N_DEV, "wgrad_b_in", bt=1024)
    gw_b_out = _wgrad(y2, dh2_b, 1, "wgrad_b_out", bt=1024).reshape(N_DEV, -1, D)
    (dz, gw_a_out, gw_kv, dh1_f, d_gkv, d_bkv, d_lng, d_lnb, d_ws, d_bst, r_b_in, r_b_out) = _a_bwd(
        dh1p, dk, dv, h1, g_kv, wkv, wa_out, ws, ln_g, ln_b, z, sv, vhat, rstd, rc, rs1, rs2, [gw_b_in, gw_b_out])
    dx, n1, d_ga, r_a_out, r_kv = _a_in_bwd(dz, wa_in, xs, dh1_f, g_a, [gw_a_out, gw_kv])
    small = [(_view2d(d_ws), None, None), (d_bst, G, "T")] + [(_view2d(a), None, None) for a in (
        d_gkv, d_bkv, d_gb, d_bq, d_sink, d_gf, d_ga, d_lng, d_lnb, loss)]
    used = sum(_nrows(G * CHUNK if flag else a.size) for a, _, flag in small)
    per = -(-used // (SUBLANES * N_DEV)) * SUBLANES
    small_pack = _pack_small(small, per * N_DEV, "pack_small").reshape(N_DEV, per, LANES)
    r_a_in, full_small = _wgrad_exchange(n1, dz, me.reshape(1), small_pack, "wgrad_a_in")

    g_a_in, d_a_in, nm_a_in, nv_a_in = _sum_adam(r_a_in, a_w_in[0], m_a_w_in[0], v_a_w_in[0], "adam_a_in")
    g_a_out, d_a_out, nm_a_out, nv_a_out = _sum_adam(r_a_out, a_w_out[0], m_a_w_out[0], v_a_w_out[0], "adam_a_out")
    g_kvw, d_kvw, nm_kvw, nv_kvw = _sum_adam(r_kv, w_kv, m_w_kv, v_w_kv, "adam_kv")
    g_b_in, d_b_in, nm_b_in, nv_b_in = _sum_adam(r_b_in, b_w_in[0], m_b_w_in[0], v_b_w_in[0], "adam_b_in")
    g_b_out, d_b_out, nm_b_out, nv_b_out = _sum_adam(r_b_out, b_w_out[0], m_b_w_out[0], v_b_w_out[0], "adam_b_out")

    full_small = full_small.reshape(N_DEV * per, LANES)
    reps = [(a_ws, m_a_ws, v_a_ws), (a_bs, m_a_bs, v_a_bs), (kv_norm_g, m_kv_norm_g, v_kv_norm_g),
            (b_kv, m_b_kv, v_b_kv), (b_norm_g, m_b_norm_g, v_b_norm_g), (b_bq, m_b_bq, v_b_bq),
            (b_sinks, m_b_sinks, v_b_sinks), (final_norm_g, m_final_norm_g, v_final_norm_g)]
    shards = [(a_norm_g, m_a_norm_g, v_a_norm_g), (a_ln_g, m_a_ln_g, v_a_ln_g), (a_ln_b, m_a_ln_b, v_a_ln_b)]
    upd, loss = _small_update(full_small, me.reshape(1), [tuple(_view2d(t) for t in p) for p in reps],
                              [tuple(_view2d(t) for t in p) for p in shards], "adam_small")
    loss = loss[0, 0]
    sm_g, sd, snm, snv = ([upd[k][j].reshape(p[0].shape) for k, p in enumerate(reps + shards)] for j in range(4))

    def order(big, sm):
        a_in, a_out, kvw, b_in, b_out = big
        ws_, bs_, kvg, bkv_, bng, bq_, snk, fng, ang, alng, alnb = sm
        return (ang, a_in[None], alng, alnb, ws_, bs_, a_out[None], kvg, kvw, bkv_, bng, b_in[None], bq_, snk,
                b_out[None], fng)

    grads = order((g_a_in, g_a_out, g_kvw, g_b_in, g_b_out), sm_g)
    deltas = order((d_a_in, d_a_out, d_kvw, d_b_in, d_b_out), sd)
    new_m = order((nm_a_in, nm_a_out, nm_kvw, nm_b_in, nm_b_out), snm)
    new_v = order((nv_a_in, nv_a_out, nv_kvw, nv_b_in, nv_b_out), snv)
    return (loss, dx[None], *grads, *deltas, *new_m, *new_v)
```

```python
import functools

import jax
import jax.numpy as jnp
import numpy as np
from jax import lax
from jax.experimental import pallas as pl
from jax.experimental.pallas import tpu as pltpu

CHUNK = 128
HEAD_DIM = 64
ROPE_THETA = 10000.0
EPS = 1e-5
ADAM_LR = 0.001
ADAM_B1 = 0.9
ADAM_B2 = 0.999
ADAM_EPS = 1e-08
ADAM_WD = 0.01
ADAM_STEP = 10
N_DEV = 8
LANES = 128
NEG = -1e30

BF = jnp.bfloat16
F32 = jnp.float32
MESH = pl.DeviceIdType.MESH
AXES = ("x", "y", "c")
VMEM_LIMIT = 56 * 1024 * 1024


def _dot(a, b):
    return jnp.dot(a, b, preferred_element_type=F32)


def _dot_nt(a, b):
    return lax.dot_general(a, b, (((1,), (1,)), ((), ())), preferred_element_type=F32)


def _dot_tn(a, b):
    return lax.dot_general(a, b, (((0,), (0,)), ((), ())), preferred_element_type=F32)


def _const_spec(shape):
    nd = len(shape)
    return pl.BlockSpec(shape, lambda *_: (0,) * nd, pipeline_mode=pl.Buffered(1))


def _acc_spec(shape):
    nd = len(shape)
    return pl.BlockSpec(shape, lambda *_: (0,) * nd)


def _row_spec(tm, width):
    return pl.BlockSpec((tm, width), lambda i: (i, 0))


def _col_spec(tm, height):
    return pl.BlockSpec((height, tm), lambda i: (0, i))


def _params(sem):
    return pltpu.CompilerParams(dimension_semantics=sem, vmem_limit_bytes=VMEM_LIMIT)


def _rot(x, c, s1, s2):
    return x * c + pltpu.roll(x, 96, 1) * s1 + pltpu.roll(x, 32, 1) * s2


def _rot_bwd(d, c, s1, s2):
    return d * c + pltpu.roll(d * s1, 32, 1) + pltpu.roll(d * s2, 96, 1)


def _silu_parts(g):
    sg = jax.nn.sigmoid(g)
    return g * sg, sg * (1.0 + g * (1.0 - sg))


def _rms_bwd(dn, xh, r, g):
    a = dn * g
    return r * (a - xh * jnp.mean(a * xh, axis=-1, keepdims=True))


def _lane_lo(shape):
    return lax.broadcasted_iota(jnp.int32, shape, 1) < HEAD_DIM


def _split4(t):
    lo = _lane_lo(t.shape)
    tr = pltpu.roll(t, HEAD_DIM, 1)
    z = jnp.zeros_like(t)
    return jnp.concatenate([jnp.where(lo, t, z), jnp.where(lo, z, tr), jnp.where(lo, tr, z), jnp.where(lo, z, t)], axis=1)


def _stack_pairs(t, h):
    return jnp.concatenate([t[:, (h * 4 + j) * LANES:(h * 4 + j + 1) * LANES] for j in range(4)], axis=0)


def _upper():
    shape = (CHUNK, 4 * CHUNK)
    return lax.broadcasted_iota(jnp.int32, shape, 0) > (lax.broadcasted_iota(jnp.int32, shape, 1) & (CHUNK - 1))


def _band_rows(tile_ref, before_ref, c, h):
    a = slice(2 * h * LANES, (2 * h + 1) * LANES)
    b = slice((2 * h + 1) * LANES, (2 * h + 2) * LANES)
    cur = slice(c * CHUNK, (c + 1) * CHUNK)

    def prev(cols):
        return before_ref[:, cols] if c == 0 else tile_ref[(c - 1) * CHUNK:c * CHUNK, cols]

    return jnp.concatenate([prev(a), tile_ref[cur, a], prev(b), tile_ref[cur, b]], axis=0)


def _band_cols(tile_ref, before_ref, c, h):
    a = slice(2 * h * LANES, (2 * h + 1) * LANES)
    b = slice((2 * h + 1) * LANES, (2 * h + 2) * LANES)

    def prev(rows):
        return before_ref[0, rows, :] if c == 0 else tile_ref[c - 1, rows, :]

    return jnp.concatenate([prev(a), tile_ref[c, a, :], prev(b), tile_ref[c, b, :]], axis=1)


def _band_specs(tm):
    nc = tm // CHUNK

    def before(i):
        return jnp.maximum(i * nc - 1, 0)

    return (pl.BlockSpec((tm, 4 * LANES), lambda i: (i, 0)),
            pl.BlockSpec((CHUNK, 4 * LANES), lambda i: (before(i), 0)),
            pl.BlockSpec((nc, 4 * LANES, CHUNK), lambda i: (i, 0, 0)),
            pl.BlockSpec((1, 4 * LANES, CHUNK), lambda i: (before(i), 0, 0)))


def _fold(t, upper, has_prev=None):
    out = []
    for k in range(2):
        prev = t[2 * k * CHUNK:(2 * k + 1) * CHUNK]
        if has_prev is not None:
            prev = jnp.where(has_prev, prev, NEG)
        out.append(jnp.where(upper, prev, t[(2 * k + 1) * CHUNK:(2 * k + 2) * CHUNK]))
    return out


def _unfold(fa, fb, upper):
    z = jnp.zeros_like(fa)
    return jnp.concatenate([jnp.where(upper, fa, z), jnp.where(upper, z, fa),
                            jnp.where(upper, fb, z), jnp.where(upper, z, fb)], axis=0)


def _softmax_sink(f, sink):
    m = jnp.maximum(jnp.max(f, axis=0, keepdims=True), sink)
    p = jnp.exp(f - m)
    es = jnp.exp(sink - m)
    inv = 1.0 / (jnp.sum(p, axis=0, keepdims=True) + es)
    return p * inv, es * inv


class _Riding:
    def __init__(self, shards, gathered, stages, sems, n_steps):
        self.shards, self.stages, self.n_steps = shards, stages, n_steps
        ssem, rsem, lsem = sems
        self.gathers = [_TwoLevel(stages[k], gathered[k], ssem.at[k], rsem.at[k], lsem.at[k])
                        for k in range(len(shards))]

    def begin(self, i):
        @pl.when(i == 0)
        def _():
            for shard, stage, g in zip(self.shards, self.stages, self.gathers):
                stage[...] = shard[...].astype(stage.dtype)
                g.start()

    def end(self, i):
        @pl.when(i == self.n_steps // 2)
        def _():
            for g in self.gathers:
                g.forward()

        @pl.when(i == self.n_steps - 1)
        def _():
            for g in self.gathers:
                g.finish()

    @staticmethod
    def specs(later):
        nl = len(later)
        hbm = pl.BlockSpec(memory_space=pl.ANY)
        return ([_const_spec(w.shape) for w in later], [hbm] * nl,
                tuple(jax.ShapeDtypeStruct((N_DEV,) + w.shape, BF) for w in later),
                [pltpu.VMEM(w.shape, BF) for w in later] + _direct_sems(nl))


PASS_MASKS = ((0, 1, 2, 5, 4, 3, 6, 7), (0, 1, 4, 3, 2, 5, 6, 7))


def _in_proj(x, w_shard, vec_shard, slots, later):
    T, D = x.shape
    SH = w_shard.shape[1]
    TM = min(1024, T)
    nT = T // TM
    nl = len(later)
    ds = D // N_DEV
    last = N_DEV - 1

    def body(slots_ref, x_ref, wsh_ref, vsh_ref, *rest):
        shards, rest = rest[:nl], rest[nl:]
        (z_ref, wout_ref, vout_ref), rest = rest[:3], rest[3:]
        gathered, rest = rest[:nl], rest[nl:]
        (w_scr, vec_scr, vstage, n1_scr, ga_scr, w_s, w_r, w_l, v_s, v_r, v_l), rest = rest[:11], rest[11:]
        stages, sems = rest[:nl], rest[nl:]
        p, i = pl.program_id(0), pl.program_id(1)
        me = _my_index()
        wg = _RelayGather(w_scr, w_s, w_r)
        vg = _Direct(vstage, vec_scr, v_s, v_r, v_l, scatter=False)
        lg = [_TwoLevel(stages[k], gathered[k], sems[0].at[k], sems[1].at[k], sems[2].at[k]) for k in range(nl)]
        w_copy = pltpu.make_async_copy(w_scr, wout_ref, w_l)

        def at_pass(k):
            return (p == k) & (i == 0)

        c = lax.axis_index("c")

        @pl.when(at_pass(0))
        def _():
            vstage[...] = vsh_ref[...]
            vg.start()
            w_scr[me] = wsh_ref[...].astype(BF)
            wg.send_own(0).start()

            @pl.when(c == 1)
            def _():
                wg.send_own(1).start()

            @pl.when(c == 0)
            def _():
                wg.send_own(2).start()

            vg.finish()
            for j in range(N_DEV):
                ga_scr[:, j * ds:(j + 1) * ds] = vec_scr[j, 0:1, 0:ds]
            vout_ref[...] = vec_scr[...]

        @pl.when(at_pass(1))
        def _():
            wg.wait_sibling()

        for first, second, landed_first, landed_second in ((1, 2, wg.on_x, wg.on_y), (2, 1, wg.on_y, wg.on_x)):
            mine = c == (1 if first == 1 else 0)

            @pl.when(at_pass(2) & mine)
            def _(second=second, landed_first=landed_first):
                wg.send_own(second).start()
                landed_first()

            @pl.when(at_pass(3) & mine)
            def _(second=second):
                wg.wait_passed(second - 1)

            @pl.when(at_pass(4) & mine)
            def _(landed_second=landed_second):
                landed_second()

            @pl.when(at_pass(5) & mine)
            def _(first=first):
                wg.wait_passed(first - 1)

        @pl.when(at_pass(4))
        def _():
            for k in range(nl):
                stages[k][...] = shards[k][...].astype(BF)
                lg[k].start()

        @pl.when(at_pass(6))
        def _():
            wg.on_diag()

        @pl.when(at_pass(7))
        def _():
            wg.wait_passed(2)

        @pl.when(at_pass(last))
        def _():
            w_copy.start()

        @pl.when(p == 0)
        def _():
            xv = x_ref[...]
            r1 = lax.rsqrt(jnp.mean(xv * xv, axis=-1, keepdims=True) + EPS)
            n1_scr[i] = (xv * r1 * ga_scr[...]).astype(BF)

        z_ref[...] = _dot(n1_scr[i], w_scr[slots_ref[p]]).astype(BF)

        @pl.when((p == last) & (i == nT - 1))
        def _():
            wg.wait_sends()
            for g in lg:
                g.forward()
            for g in lg:
                g.finish()
            w_copy.wait()

    hbm = pl.BlockSpec(memory_space=pl.ANY)
    dma = pltpu.SemaphoreType.DMA
    S = jax.ShapeDtypeStruct
    grid_spec = pltpu.PrefetchScalarGridSpec(
        num_scalar_prefetch=1, grid=(N_DEV, nT),
        in_specs=[pl.BlockSpec((TM, D), lambda p, i, s: (jnp.where(p == 0, i, nT - 1), 0)),
                  pl.BlockSpec(w_shard.shape, lambda p, i, s: (0, 0), pipeline_mode=pl.Buffered(1)),
                  pl.BlockSpec(vec_shard.shape, lambda p, i, s: (0, 0), pipeline_mode=pl.Buffered(1))]
        + [pl.BlockSpec(w.shape, lambda p, i, s: (0, 0), pipeline_mode=pl.Buffered(1)) for w in later],
        out_specs=[pl.BlockSpec((TM, SH), lambda p, i, s: (i, s[p])), hbm,
                   pl.BlockSpec((N_DEV,) + vec_shard.shape, lambda p, i, s: (0, 0, 0))] + [hbm] * nl,
        scratch_shapes=[pltpu.VMEM((N_DEV, D, SH), BF), pltpu.VMEM((N_DEV,) + vec_shard.shape, F32),
                        pltpu.VMEM(vec_shard.shape, F32), pltpu.VMEM((nT, TM, D), BF), pltpu.VMEM((1, D), F32),
                        dma((8,)), dma((8,)), dma, dma((7,)), dma((7,)), dma]
        + [pltpu.VMEM(w.shape, BF) for w in later] + _direct_sems(nl))
    return pl.pallas_call(
        body, name="a_in_proj", grid_spec=grid_spec,
        out_shape=(S((T, N_DEV * SH), BF), S((N_DEV, D, SH), BF), S((N_DEV,) + vec_shard.shape, F32))
        + tuple(S((N_DEV,) + w.shape, BF) for w in later),
        compiler_params=_params(("arbitrary", "arbitrary")),
    )(slots, x, w_shard, vec_shard, *later)


def _a_fwd(x, z, ln_g, ln_b, ws, bs_t, wa_out, g_kv, w_kv, b_kv, rc, rs1, rs2, later):
    T, D = x.shape
    AW = wa_out.shape[0]
    G = ws.shape[0]
    TM = min(256, T)
    nT = T // TM
    nC = TM // CHUNK
    nl = len(later)

    def body(x_ref, u_ref, v_ref, gt_ref, lng_ref, lnb_ref, ws_ref, bst_ref, waout_ref, gkv_ref, wkv_ref, bkv_ref,
             rc_ref, rs1_ref, rs2_ref, *rest):
        shards, rest = rest[:nl], rest[nl:]
        (h1_ref, sv_ref, vhat_ref, rstd_ref, k4_ref, v4_ref, kt_ref, vt_ref), rest = rest[:8], rest[8:]
        gathered, sv_scr, stages, sems = rest[:nl], rest[nl], rest[nl + 1:2 * nl + 1], rest[2 * nl + 1:]
        i = pl.program_id(0)
        riding = _Riding(shards, gathered, stages, sems, nT)
        riding.begin(i)
        xv = x_ref[...]
        u = u_ref[...].astype(F32)
        v = v_ref[...].astype(F32)
        gt = gt_ref[...].astype(F32)
        mu = jnp.mean(v, axis=-1, keepdims=True)
        xc = v - mu
        rstd = lax.rsqrt(jnp.mean(xc * xc, axis=-1, keepdims=True) + EPS)
        vhat = xc * rstd
        vln = (vhat * lng_ref[...] + lnb_ref[...]).astype(BF)
        tri = lax.broadcasted_iota(jnp.int32, (CHUNK, CHUNK), 0) >= lax.broadcasted_iota(jnp.int32, (CHUNK, CHUNK), 1)
        for g in range(G):
            wsm = jnp.where(tri, ws_ref[g], 0.0).astype(BF)
            bias = bst_ref[:, g:g + 1]
            for c in range(nC):
                blk = vln[c * CHUNK:(c + 1) * CHUNK, g * CHUNK:(g + 1) * CHUNK]
                sv_scr[c * CHUNK:(c + 1) * CHUNK, g * CHUNK:(g + 1) * CHUNK] = _dot(wsm, blk) + bias
        sv = sv_scr[...]
        silu, _ = _silu_parts(gt)
        y = (u * sv * silu).astype(BF)
        h1 = xv + _dot(y, waout_ref[...])
        h1_ref[...] = h1
        sv_ref[...] = sv.astype(BF)
        vhat_ref[...] = vhat.astype(BF)
        rstd_ref[...] = jnp.broadcast_to(rstd, rstd_ref.shape)
        rkv = lax.rsqrt(jnp.mean(h1 * h1, axis=-1, keepdims=True) + EPS)
        nkv = (h1 * rkv * gkv_ref[...]).astype(BF)
        kv = _dot(nkv, wkv_ref[...]) + bkv_ref[...]
        k_rot = _rot(kv[:, :LANES], rc_ref[...], rs1_ref[...], rs2_ref[...])
        for src, ref, tref in ((k_rot, k4_ref, kt_ref), (kv[:, LANES:], v4_ref, vt_ref)):
            t4 = _split4(src)
            ref[...] = t4.astype(BF)
            for c in range(nC):
                for b in range(4):
                    blk = t4[c * CHUNK:(c + 1) * CHUNK, b * LANES:(b + 1) * LANES]
                    tref[c, b * LANES:(b + 1) * LANES, :] = blk.T.astype(BF)
        riding.end(i)

    row = functools.partial(_row_spec, TM)
    zcol = [pl.BlockSpec((TM, AW), functools.partial(lambda k, i: (i, k), k)) for k in range(3)]
    tr = pl.BlockSpec((nC, 4 * LANES, CHUNK), lambda i: (i, 0, 0))
    r_in, r_out, r_shape, r_scratch = _Riding.specs(later)
    S = jax.ShapeDtypeStruct
    return pl.pallas_call(
        body, name="a_fwd", grid=(nT,),
        in_specs=[row(D)] + zcol + [_const_spec((1, AW)), _const_spec((1, AW)),
                  _const_spec(ws.shape), _const_spec(bs_t.shape), _const_spec(wa_out.shape), _const_spec((1, D)),
                  _const_spec(w_kv.shape), _const_spec((1, 2 * LANES)), row(LANES), row(LANES), row(LANES)] + r_in,
        out_specs=[row(D), row(AW), row(AW), row(LANES), row(4 * LANES), row(4 * LANES), tr, tr] + r_out,
        out_shape=(S((T, D), F32), S((T, AW), BF), S((T, AW), BF), S((T, LANES), F32),
                   S((T, 4 * LANES), BF), S((T, 4 * LANES), BF),
                   S((T // CHUNK, 4 * LANES, CHUNK), BF), S((T // CHUNK, 4 * LANES, CHUNK), BF)) + r_shape,
        scratch_shapes=[pltpu.VMEM((TM, AW), F32)] + r_scratch,
        compiler_params=_params(("arbitrary",)),
    )(x, z, z, z, ln_g, ln_b, ws, bs_t, wa_out, g_kv, w_kv, b_kv, rc, rs1, rs2, *later)


def _b_fwd(h1, g_b, wb_in, bq, rc, rs1, rs2, k4, vt, sinks, wb_out, g_f, target):
    T, D = h1.shape
    BW = wb_out.shape[0]
    SH = wb_in.shape[2]
    TM = min(512, T)
    nC = TM // CHUNK
    nP = BW // LANES

    def body(h1_ref, gb_ref, wbin_ref, bq_ref, rc_ref, rs1_ref, rs2_ref, k4_ref, k4p_ref, vt_ref, vtp_ref, sink_ref,
             wbout_ref, gf_ref, tgt_ref, q_ref, g2_ref, o_ref, dh2_ref, dh2b_ref, loss_ref, dgf_ref, z_scr, o_scr):
        i = pl.program_id(0)

        @pl.when(i == 0)
        def _():
            loss_ref[...] = jnp.zeros_like(loss_ref)
            dgf_ref[...] = jnp.zeros_like(dgf_ref)

        h1v = h1_ref[...]
        r2 = lax.rsqrt(jnp.mean(h1v * h1v, axis=-1, keepdims=True) + EPS)
        n2 = (h1v * r2 * gb_ref[...]).astype(BF)
        for j in range(N_DEV):
            z_scr[:, j * SH:(j + 1) * SH] = _dot(n2, wbin_ref[j])
        c_t, s1_t, s2_t = rc_ref[...], rs1_ref[...], rs2_ref[...]
        for p in range(nP):
            cols = slice(p * LANES, (p + 1) * LANES)
            qp = _rot(z_scr[:, cols] + bq_ref[:, cols], c_t, s1_t, s2_t) * (HEAD_DIM ** -0.5)
            q_ref[:, cols] = qp.astype(BF)
        g2 = z_scr[:, BW:]
        g2_ref[...] = g2.astype(BF)
        upper = _upper()
        for c in range(nC):
            ci = i * nC + c
            rows = slice(c * CHUNK, (c + 1) * CHUNK)
            qc = q_ref[rows, :]
            for h in range(2):
                st = _dot_nt(_band_rows(k4_ref, k4p_ref, c, h), _stack_pairs(qc, h))
                fa, fb = _fold(st, upper, ci > 0)
                pa, _ = _softmax_sink(fa, sink_ref[2 * h:2 * h + 1, :])
                pb, _ = _softmax_sink(fb, sink_ref[2 * h + 1:2 * h + 2, :])
                ot = _dot(_band_cols(vt_ref, vtp_ref, c, h), _unfold(pa, pb, upper).astype(BF))
                for j in range(4):
                    o_scr[rows, (h * 4 + j) * LANES:(h * 4 + j + 1) * LANES] = ot[:, j * CHUNK:(j + 1) * CHUNK].T
        o = o_scr[...]
        o_ref[...] = o.astype(BF)
        silu, _ = _silu_parts(g2)
        h2 = h1v + _dot((o * silu).astype(BF), wbout_ref[...])
        rf = lax.rsqrt(jnp.mean(h2 * h2, axis=-1, keepdims=True) + EPS)
        xh = h2 * rf
        gf = gf_ref[...]
        err = xh * gf - tgt_ref[...]
        dyf = err * (1.0 / D)
        dh2 = _rms_bwd(dyf, xh, rf, gf)
        dh2_ref[...] = dh2
        dh2b_ref[...] = dh2.astype(BF)
        loss_ref[...] += 0.5 * jnp.sum(jnp.mean(err * err, axis=-1, keepdims=True), axis=0, keepdims=True)
        dgf_ref[...] += jnp.sum(dyf * xh, axis=0, keepdims=True)

    row = functools.partial(_row_spec, TM)
    rows_tile, rows_before, cols_tile, cols_before = _band_specs(TM)
    S = jax.ShapeDtypeStruct
    return pl.pallas_call(
        body, name="b_fwd", grid=(T // TM,),
        in_specs=[row(D), _const_spec((1, D)), _const_spec(wb_in.shape), _const_spec((1, BW)), row(LANES), row(LANES),
                  row(LANES), rows_tile, rows_before, cols_tile, cols_before, _const_spec(sinks.shape),
                  _const_spec(wb_out.shape), _const_spec((1, D)), row(D)],
        out_specs=[row(BW), row(BW), row(BW), row(D), row(D), _acc_spec((1, 1)), _acc_spec((1, D))],
        out_shape=(S((T, BW), BF), S((T, BW), BF), S((T, BW), BF), S((T, D), F32), S((T, D), BF), S((1, 1), F32),
                   S((1, D), F32)),
        scratch_shapes=[pltpu.VMEM((TM, 2 * BW), F32), pltpu.VMEM((TM, BW), F32)],
        compiler_params=_params(("arbitrary",)),
    )(h1, g_b, wb_in, bq, rc, rs1, rs2, k4, k4, vt, vt, sinks, wb_out, g_f, target)


def _b_bwd(dh2, h1, q, g2, o, k4, v4, kt, sinks, wb_out, wb_in, g_b, rc, rs1, rs2):
    T, D = h1.shape
    BW = wb_out.shape[0]
    SH = wb_in.shape[2]
    TM = min(256, T)
    nT = T // TM
    nC = TM // CHUNK
    nP = BW // LANES

    def body(dh2_ref, h1_ref, q_ref, g2_ref, o_ref, k4_ref, k4p_ref, v4_ref, v4p_ref, kt_ref, ktp_ref, sink_ref,
             wbout_ref, wbin_ref, gb_ref, rc_ref, rs1_ref, rs2_ref,
             dh1_ref, dz2_ref, n2_ref, y2_ref, dk_ref, dv_ref, dbq_ref, dgb_ref, dsink_ref, do_scr, dq_scr, dsacc_scr):
        i = pl.program_id(0)

        @pl.when(i == 0)
        def _():
            dk_ref[...] = jnp.zeros_like(dk_ref)
            dv_ref[...] = jnp.zeros_like(dv_ref)
            dbq_ref[...] = jnp.zeros_like(dbq_ref)
            dgb_ref[...] = jnp.zeros_like(dgb_ref)
            dsacc_scr[...] = jnp.zeros_like(dsacc_scr)

        dh2 = dh2_ref[...]
        dy2 = _dot_nt(dh2.astype(BF), wbout_ref[...])
        silu, dsilu = _silu_parts(g2_ref[...].astype(F32))
        do_scr[...] = (dy2 * silu).astype(BF)
        dy2, silu, dsilu = dy2.astype(BF), silu.astype(BF), dsilu.astype(BF)
        ob = o_ref[...]
        y2_ref[...] = (ob * silu).T
        dz2_ref[:, BW:] = dy2 * ob * dsilu
        upper = _upper()
        lo = _lane_lo((2 * CHUNK, LANES))
        for c in range(nC):
            ci = i * nC + c
            rows = slice(c * CHUNK, (c + 1) * CHUNK)
            pci = jnp.maximum(ci - 1, 0)
            prev = pl.multiple_of(pci * CHUNK, CHUNK)
            cur = pl.multiple_of(ci * CHUNK, CHUNK)
            qc = q_ref[rows, :]
            doc = do_scr[rows, :]
            dkb = jnp.zeros((2 * CHUNK, LANES), F32)
            dvb = jnp.zeros((2 * CHUNK, LANES), F32)
            for h in range(2):
                qs = _stack_pairs(qc, h)
                dos = _stack_pairs(doc, h)
                fa, fb = _fold(_dot_nt(_band_rows(k4_ref, k4p_ref, c, h), qs), upper, ci > 0)
                dfa, dfb = _fold(_dot_nt(_band_rows(v4_ref, v4p_ref, c, h), dos), upper)
                folded = []
                for k, (f, df) in enumerate(((fa, dfa), (fb, dfb))):
                    p, ps = _softmax_sink(f, sink_ref[2 * h + k:2 * h + k + 1, :])
                    delta = jnp.sum(p * df, axis=0, keepdims=True)
                    dsacc_scr[2 * h + k:2 * h + k + 1, :] -= ps * delta
                    folded.append((p, p * (df - delta)))
                pt = _unfold(folded[0][0], folded[1][0], upper).astype(BF)
                dst = _unfold(folded[0][1], folded[1][1], upper).astype(BF)
                dqt = _dot(_band_cols(kt_ref, ktp_ref, c, h), dst)
                for j in range(4):
                    dq_scr[rows, (h * 4 + j) * LANES:(h * 4 + j + 1) * LANES] = dqt[:, j * CHUNK:(j + 1) * CHUNK].T
                for acc_name, g in (("k", _dot(dst, qs)), ("v", _dot(pt, dos))):
                    a, b = g[:2 * CHUNK], g[2 * CHUNK:]
                    if h == 0:
                        part = jnp.where(lo, a + pltpu.roll(b, HEAD_DIM, 1), 0.0)
                    else:
                        part = jnp.where(lo, 0.0, pltpu.roll(a, HEAD_DIM, 1) + b)
                    if acc_name == "k":
                        dkb += part
                    else:
                        dvb += part
            dk_ref[pl.ds(prev, CHUNK), :] += dkb[:CHUNK]
            dk_ref[pl.ds(cur, CHUNK), :] += dkb[CHUNK:]
            dv_ref[pl.ds(prev, CHUNK), :] += dvb[:CHUNK]
            dv_ref[pl.ds(cur, CHUNK), :] += dvb[CHUNK:]
        c_t, s1_t, s2_t = rc_ref[...], rs1_ref[...], rs2_ref[...]
        for p in range(nP):
            cols = slice(p * LANES, (p + 1) * LANES)
            dqp = _rot_bwd(dq_scr[:, cols] * (HEAD_DIM ** -0.5), c_t, s1_t, s2_t)
            dbq_ref[:, cols] += jnp.sum(dqp, axis=0, keepdims=True)
            dz2_ref[:, cols] = dqp.astype(BF)
        h1v = h1_ref[...]
        r2 = lax.rsqrt(jnp.mean(h1v * h1v, axis=-1, keepdims=True) + EPS)
        xh = h1v * r2
        gb = gb_ref[...]
        n2_ref[...] = (xh * gb).astype(BF).T
        dn2 = None
        for j in range(N_DEV):
            part = _dot_nt(dz2_ref[:, j * SH:(j + 1) * SH], wbin_ref[j])
            dn2 = part if dn2 is None else dn2 + part
        dgb_ref[...] += jnp.sum(dn2 * xh, axis=0, keepdims=True)
        dh1_ref[...] = dh2 + _rms_bwd(dn2, xh, r2, gb)

        @pl.when(i == nT - 1)
        def _():
            lane = lax.broadcasted_iota(jnp.int32, dsink_ref.shape, 1)
            tot = jnp.zeros(dsink_ref.shape, F32)
            for j in range(4):
                tot += jnp.where(lane == j, jnp.sum(dsacc_scr[:, j * CHUNK:(j + 1) * CHUNK], axis=1, keepdims=True), 0.0)
            dsink_ref[...] = tot

    row = functools.partial(_row_spec, TM)
    rows_tile, rows_before, cols_tile, cols_before = _band_specs(TM)
    S = jax.ShapeDtypeStruct
    return pl.pallas_call(
        body, name="b_bwd", grid=(T // TM,),
        in_specs=[row(D), row(D), row(BW), row(BW), row(BW), rows_tile, rows_before, rows_tile, rows_before,
                  cols_tile, cols_before, _const_spec(sinks.shape), _const_spec(wb_out.shape), _const_spec(wb_in.shape),
                  _const_spec((1, D)), row(LANES), row(LANES), row(LANES)],
        out_specs=[row(D), row(2 * BW), _col_spec(TM, D), _col_spec(TM, BW), _acc_spec((T, LANES)),
                   _acc_spec((T, LANES)), _acc_spec((1, BW)), _acc_spec((1, D)), _acc_spec((4, LANES))],
        out_shape=(S((T, D), F32), S((T, 2 * BW), BF), S((D, T), BF), S((BW, T), BF), S((T, LANES), F32),
                   S((T, LANES), F32), S((1, BW), F32), S((1, D), F32), S((4, LANES), F32)),
        scratch_shapes=[pltpu.VMEM((TM, BW), BF), pltpu.VMEM((TM, BW), F32), pltpu.VMEM((4, 4 * CHUNK), F32)],
        compiler_params=_params(("arbitrary",)),
    )(dh2, h1, q, g2, o, k4, k4, v4, v4, kt, kt, sinks, wb_out, wb_in, g_b, rc, rs1, rs2)


def _a_bwd(dh1p, dk, dv, h1, g_kv, w_kv, wa_out, ws, ln_g, ln_b, z, sv, vhat, rstd, rc, rs1, rs2, ready):
    T, D = h1.shape
    AW = wa_out.shape[0]
    G = ws.shape[0]
    TM = min(256, T)
    nT = T // TM
    nC = TM // CHUNK
    nr = len(ready)

    def body(dh1p_ref, dk_ref, dv_ref, h1_ref, gkv_ref, wkv_ref, waout_ref, ws_ref, lng_ref,
             lnb_ref, u_ref, gt_ref, sv_ref, vhat_ref, rstd_ref, rc_ref, rs1_ref, rs2_ref, *rest):
        ready_refs, rest = rest[:nr], rest[nr:]
        (dz_ref, gwo_ref, gwk_ref, dh1f_ref, dgkv_ref, dbkv_ref, dlng_ref, dlnb_ref,
         dws_ref, dbs_ref), rest = rest[:10], rest[10:]
        recv_refs, (dsv_scr, dvln_scr, acco_scr, acck_scr, ssem, rsem, lsem) = rest[:nr], rest[nr:]
        i = pl.program_id(0)
        exchanges = [_Direct(ready_refs[k], recv_refs[k], ssem.at[k], rsem.at[k], lsem.at[k], scatter=True)
                     for k in range(nr)]

        @pl.when(i == 0)
        def _():
            for e in exchanges:
                e.start()
            for r in (dgkv_ref, dbkv_ref, dlng_ref, dlnb_ref, dws_ref, dbs_ref, acco_scr, acck_scr):
                r[...] = jnp.zeros_like(r)

        dk_pre = _rot_bwd(dk_ref[...], rc_ref[...], rs1_ref[...], rs2_ref[...])
        dkv = jnp.concatenate([dk_pre, dv_ref[...]], axis=1)
        dbkv_ref[...] += jnp.sum(dkv, axis=0, keepdims=True)
        dkv_b = dkv.astype(BF)
        h1v = h1_ref[...]
        rkv = lax.rsqrt(jnp.mean(h1v * h1v, axis=-1, keepdims=True) + EPS)
        xh_kv = h1v * rkv
        gkv = gkv_ref[...]
        acck_scr[...] += _dot((xh_kv * gkv).astype(BF).T, dkv_b)
        dnkv = _dot_nt(dkv_b, wkv_ref[...])
        dgkv_ref[...] += jnp.sum(dnkv * xh_kv, axis=0, keepdims=True)
        dh1 = dh1p_ref[...] + _rms_bwd(dnkv, xh_kv, rkv, gkv)
        dh1_b = dh1.astype(BF)
        dh1f_ref[...] = dh1
        dy = _dot_nt(dh1_b, waout_ref[...]).astype(BF)
        silu, dsilu = _silu_parts(gt_ref[...].astype(F32))
        silu, dsilu = silu.astype(BF), dsilu.astype(BF)
        ub, svb = u_ref[...], sv_ref[...]
        us = ub * silu
        dys = dy * svb
        acco_scr[...] += _dot((us * svb).T, dh1_b)
        dz_ref[:, :AW] = dys * silu
        dz_ref[:, 2 * AW:] = dys * ub * dsilu
        dsv_scr[...] = dy * us
        vhat_v = vhat_ref[...].astype(F32)
        lng = lng_ref[...]
        vln_b = (vhat_v * lng + lnb_ref[...]).astype(BF)
        tri = lax.broadcasted_iota(jnp.int32, (CHUNK, CHUNK), 0) >= lax.broadcasted_iota(jnp.int32, (CHUNK, CHUNK), 1)
        lane = lax.broadcasted_iota(jnp.int32, (CHUNK, LANES), 1)
        dbs = jnp.zeros((CHUNK, LANES), F32)
        for g in range(G):
            wsm = jnp.where(tri, ws_ref[g], 0.0).astype(BF)
            cols = slice(g * CHUNK, (g + 1) * CHUNK)
            dws_g = None
            for c in range(nC):
                rows = slice(c * CHUNK, (c + 1) * CHUNK)
                dsv_cg = dsv_scr[rows, cols]
                dvln_scr[rows, cols] = _dot_tn(wsm, dsv_cg)
                part = _dot_nt(dsv_cg, vln_b[rows, cols])
                dws_g = part if dws_g is None else dws_g + part
                dbs += jnp.where(lane == g, jnp.sum(dsv_cg.astype(F32), axis=-1, keepdims=True), 0.0)
            dws_ref[g] += jnp.where(tri, dws_g, 0.0)
        dbs_ref[...] += dbs
        dvln = dvln_scr[...]
        dlng_ref[...] += jnp.sum(dvln * vhat_v, axis=0, keepdims=True)
        dlnb_ref[...] += jnp.sum(dvln, axis=0, keepdims=True)
        a = dvln * lng
        dvv = rstd_ref[:, 0:1] * (a - jnp.mean(a, axis=-1, keepdims=True)
                                  - vhat_v * jnp.mean(a * vhat_v, axis=-1, keepdims=True))
        dz_ref[:, AW:2 * AW] = dvv.astype(BF)

        @pl.when(i == nT - 1)
        def _():
            for j in range(N_DEV):
                gwo_ref[j] = acco_scr[j * (AW // N_DEV):(j + 1) * (AW // N_DEV)].astype(BF)
                gwk_ref[j] = acck_scr[j * (D // N_DEV):(j + 1) * (D // N_DEV)].astype(BF)
            for e in exchanges:
                e.finish()

    row = functools.partial(_row_spec, TM)
    hbm = pl.BlockSpec(memory_space=pl.ANY)
    S = jax.ShapeDtypeStruct
    gwo_shape, gwk_shape = (N_DEV, AW // N_DEV, D), (N_DEV, D // N_DEV, 2 * LANES)
    return pl.pallas_call(
        body, name="a_bwd", grid=(nT,),
        in_specs=[row(D), row(LANES), row(LANES), row(D), _const_spec((1, D)), _const_spec(w_kv.shape),
                  _const_spec(wa_out.shape), _const_spec(ws.shape),
                  _const_spec((1, AW)), _const_spec((1, AW)), pl.BlockSpec((TM, AW), lambda i: (i, 0)),
                  pl.BlockSpec((TM, AW), lambda i: (i, 2)), row(AW), row(AW), row(LANES),
                  row(LANES), row(LANES), row(LANES)] + [hbm] * nr,
        out_specs=[row(3 * AW), _const_spec(gwo_shape), _const_spec(gwk_shape), row(D),
                   _acc_spec((1, D)), _acc_spec((1, 2 * LANES)), _acc_spec((1, AW)),
                   _acc_spec((1, AW)), _acc_spec(ws.shape), _acc_spec((CHUNK, LANES))] + [hbm] * nr,
        out_shape=(S((T, 3 * AW), BF), S(gwo_shape, BF), S(gwk_shape, BF), S((T, D), F32),
                   S((1, D), F32), S((1, 2 * LANES), F32), S((1, AW), F32), S((1, AW), F32),
                   S(ws.shape, F32), S((CHUNK, LANES), F32)) + tuple(S(r.shape, r.dtype) for r in ready),
        scratch_shapes=[pltpu.VMEM((TM, AW), BF), pltpu.VMEM((TM, AW), F32), pltpu.VMEM((AW, D), F32),
                        pltpu.VMEM((D, 2 * LANES), F32)] + _direct_sems(nr),
        compiler_params=_params(("arbitrary",)),
    )(dh1p, dk, dv, h1, g_kv, w_kv, wa_out, ws, ln_g, ln_b, z, z, sv, vhat, rstd, rc, rs1, rs2, *ready)


def _a_in_bwd(dz, wa_in, x, dh1, g_a, ready):
    T, D = x.shape
    SH = wa_in.shape[2]
    TM = min(512, T)
    nT = T // TM
    nr = len(ready)

    def body(dz_ref, wain_ref, x_ref, dh1_ref, ga_ref, *rest):
        ready_refs, (dx_ref, n1_ref, dga_ref), rest = rest[:nr], rest[nr:nr + 3], rest[nr + 3:]
        recv_refs, (ssem, rsem, lsem) = rest[:nr], rest[nr:]
        i = pl.program_id(0)
        exchanges = [_Direct(ready_refs[k], recv_refs[k], ssem.at[k], rsem.at[k], lsem.at[k], scatter=True)
                     for k in range(nr)]

        @pl.when(i == 0)
        def _():
            for e in exchanges:
                e.start()
            dga_ref[...] = jnp.zeros_like(dga_ref)

        xv = x_ref[...]
        r1 = lax.rsqrt(jnp.mean(xv * xv, axis=-1, keepdims=True) + EPS)
        xh = xv * r1
        ga = ga_ref[...]
        n1_ref[...] = (xh * ga).astype(BF).T
        dn1 = None
        for j in range(N_DEV):
            part = _dot_nt(dz_ref[:, j * SH:(j + 1) * SH], wain_ref[j])
            dn1 = part if dn1 is None else dn1 + part
        dga_ref[...] += jnp.sum(dn1 * xh, axis=0, keepdims=True)
        dx_ref[...] = dh1_ref[...] + _rms_bwd(dn1, xh, r1, ga)

        @pl.when(i == nT - 1)
        def _():
            for e in exchanges:
                e.finish()

    row = functools.partial(_row_spec, TM)
    hbm = pl.BlockSpec(memory_space=pl.ANY)
    S = jax.ShapeDtypeStruct
    return pl.pallas_call(
        body, name="a_in_bwd", grid=(nT,),
        in_specs=[row(dz.shape[1]), _const_spec(wa_in.shape), row(D), row(D), _const_spec((1, D))] + [hbm] * nr,
        out_specs=[row(D), _col_spec(TM, D), _acc_spec((1, D))] + [hbm] * nr,
        out_shape=(S((T, D), F32), S((D, T), BF), S((1, D), F32)) + tuple(S(r.shape, r.dtype) for r in ready),
        scratch_shapes=_direct_sems(nr),
        compiler_params=_params(("arbitrary",)),
    )(dz, wa_in, x, dh1, g_a, *ready)


def _wgrad(at, b, nblk, name, bt=512):
    K, T = at.shape
    N = b.shape[1] // nblk
    BT = min(bt, T)
    nt = T // BT

    def body(a_ref, b_ref, o_ref, acc):
        t = pl.program_id(0)

        @pl.when(t == 0)
        def _():
            acc[...] = jnp.zeros_like(acc)

        acc[...] += _dot(a_ref[...], b_ref[...])

        @pl.when(t == nt - 1)
        def _():
            for j in range(nblk):
                o_ref[j] = acc[:, j * N:(j + 1) * N].astype(BF)

    return pl.pallas_call(
        body, name=name, grid=(nt,),
        in_specs=[pl.BlockSpec((K, BT), lambda t: (0, t)), pl.BlockSpec((BT, nblk * N), lambda t: (t, 0))],
        out_specs=pl.BlockSpec((nblk, K, N), lambda t: (0, 0, 0)),
        out_shape=jax.ShapeDtypeStruct((nblk, K, N), BF),
        scratch_shapes=[pltpu.VMEM((K, nblk * N), F32)],
        compiler_params=_params(("arbitrary",)),
    )(at, b)


def _wgrad_exchange(a, b, me, small, name):
    K, T = a.shape
    N = b.shape[1] // N_DEV
    BT = T
    nt = T // BT
    last = N_DEV - 1
    n_chip = N_DEV // 2

    def far_of(k, core):
        return jnp.where((core == 0) & ((k == 1) | (k == 2)), k, n_chip - 1 - k)

    def block_of(s, me_i):
        k, odd = s // 2, s % 2
        core = me_i & 1
        return me_i ^ ((far_of(k, jnp.where(odd == 1, core, 1 - core)) << 1) | (1 - odd))

    H = K // 2

    def body(me_ref, a_ref, b_ref, small_ref, recv_ref, full_ref, *scratch):
        (acc, dstage, istage, half, relay, d_s, d_r, i_s, i_r, r_s, r_r, lsem, parts_scr, red_scr, e_s, e_r, e_l, g_s,
         g_r, g_l) = scratch
        s, t = pl.program_id(0), pl.program_id(1)
        x, y, c = (lax.axis_index(ax) for ax in AXES)
        ex = [_Direct(small_ref, parts_scr, e_s, e_r, e_l, scatter=True)]
        regather = _TwoLevel(red_scr, full_ref, g_s, g_r, g_l)

        def to_sibling(k, slot):
            return pltpu.make_async_remote_copy(src_ref=dstage.at[slot], dst_ref=half.at[k], send_sem=d_s.at[k],
                                                recv_sem=d_r.at[k], device_id=(x, y, 1 - c), device_id_type=MESH)

        def to_chip(k, slot):
            over_x = far_of(k, c) == 2
            px, py = jnp.where(over_x, 1 - x, x), jnp.where(over_x, y, 1 - y)
            return pltpu.make_async_remote_copy(src_ref=istage.at[slot], dst_ref=recv_ref.at[jnp.where(over_x, 1, 2)],
                                                send_sem=i_s.at[k], recv_sem=i_r.at[k], device_id=(px, py, c),
                                                device_id_type=MESH)

        def to_relay(j, slot):
            to = (1 - x, y, c) if j == 0 else (x, 1 - y, c)
            return pltpu.make_async_remote_copy(src_ref=istage.at[slot, pl.ds(j * H, H)], dst_ref=relay.at[j],
                                                send_sem=r_s.at[j], recv_sem=r_r.at[j], device_id=to,
                                                device_id_type=MESH)

        @pl.when((s == 0) & (t == 0))
        def _():
            for e in ex:
                e.start()

        acc[...] = _dot(a_ref[...], b_ref[...])

        @pl.when(t == nt - 1)
        def _():
            k = lax.div(s, 2)
            slot = lax.rem(k, 2)

            @pl.when(lax.rem(s, 2) == 0)
            def _():
                @pl.when(k >= 2)
                def _():
                    to_sibling(k - 2, slot).wait_send()

                dstage[slot] = acc[...].astype(BF)
                to_sibling(k, slot).start()

            @pl.when(lax.rem(s, 2) == 1)
            def _():
                to_sibling(k, slot).wait_recv()
                pair = acc[...] + half[k].astype(F32)

                @pl.when(k == 0)
                def _():
                    istage[slot] = pair.astype(BF)
                    for j in range(2):
                        to_relay(j, slot).start()

                @pl.when(k == 1)
                def _():
                    for j in range(2):
                        to_relay(j, slot).wait_recv()

                @pl.when(k == 2)
                def _():
                    for j in range(2):
                        to_relay(j, slot).wait_send()

                @pl.when(k == n_chip - 1)
                def _():
                    to_chip(1, slot).wait_send()
                    istage[slot] = pair.astype(BF)

                @pl.when((k == 1) | (k == 2))
                def _():
                    over_x = far_of(k, c) == 2
                    istage[slot, 0:H] = (pair[:H] + jnp.where(over_x, 0.0, relay[0].astype(F32))).astype(BF)
                    istage[slot, H:K] = (pair[H:] + jnp.where(over_x, relay[1].astype(F32), 0.0)).astype(BF)
                    to_chip(k, slot).start()

            @pl.when(s == last)
            def _():
                own = pltpu.make_async_copy(istage.at[slot], recv_ref.at[0], lsem)
                own.start()
                to_chip(2, 0).wait_send()
                to_sibling(n_chip - 2, 0).wait_send()
                to_sibling(n_chip - 1, 1).wait_send()
                for kk in (1, 2):
                    to_chip(kk, 0).wait_recv()
                own.wait()
                for e in ex:
                    e.finish()
                total = parts_scr[0]
                for dev in range(1, N_DEV):
                    total = total + parts_scr[dev]
                red_scr[...] = total
                regather.start()
                regather.forward()
                regather.finish()

    hbm = pl.BlockSpec(memory_space=pl.ANY)
    dma = pltpu.SemaphoreType.DMA
    grid_spec = pltpu.PrefetchScalarGridSpec(
        num_scalar_prefetch=1, grid=(N_DEV, nt),
        in_specs=[pl.BlockSpec((K, BT), lambda s, t, me_ref: (0, t), pipeline_mode=pl.Buffered(1)),
                  pl.BlockSpec((BT, N), lambda s, t, me_ref: (t, block_of(s, me_ref[0]))), hbm],
        out_specs=[hbm, hbm],
        scratch_shapes=[pltpu.VMEM((K, N), F32), pltpu.VMEM((2, K, N), BF), pltpu.VMEM((2, K, N), BF),
                        pltpu.VMEM((n_chip, K, N), BF), pltpu.VMEM((2, H, N), BF), dma((n_chip,)), dma((n_chip,)),
                        dma((n_chip - 1,)), dma((n_chip - 1,)), dma((2,)), dma((2,)), dma,
                        pltpu.VMEM(small.shape, F32), pltpu.VMEM(small.shape[1:], F32),
                        dma((last,)), dma((last,)), dma, dma((last,)), dma((last,)), dma])
    return pl.pallas_call(
        body, name=name, grid_spec=grid_spec,
        out_shape=[jax.ShapeDtypeStruct((n_chip - 1, K, N), BF), jax.ShapeDtypeStruct(small.shape, F32)],
        compiler_params=_params(("arbitrary", "arbitrary")),
    )(me, a, b, small)


def _my_index():
    return 4 * lax.axis_index("x") + 2 * lax.axis_index("y") + lax.axis_index("c")


def _peer(mask):
    x, y, c = (lax.axis_index(a) for a in AXES)
    return (x ^ ((mask >> 2) & 1), y ^ ((mask >> 1) & 1), c ^ (mask & 1))


def _dev_index(p):
    return 4 * p[0] + 2 * p[1] + p[2]


class _Direct:
    def __init__(self, src, dst, send_sems, recv_sems, local_sem, scatter):
        me = _my_index()
        self.own = pltpu.make_async_copy(src.at[me] if scatter else src, dst.at[me], local_sem)
        self.sends, self.recvs = [], []
        for k in range(1, N_DEV):
            p = _peer(k)
            pi = _dev_index(p)
            sems = dict(send_sem=send_sems.at[k - 1], recv_sem=recv_sems.at[k - 1], device_id=p, device_id_type=MESH)
            self.sends.append(pltpu.make_async_remote_copy(src_ref=src.at[pi] if scatter else src, dst_ref=dst.at[me],
                                                           **sems))
            self.recvs.append(pltpu.make_async_remote_copy(src_ref=src.at[me] if scatter else src, dst_ref=dst.at[pi],
                                                           **sems))

    def start(self):
        self.own.start()
        for cp in self.sends:
            cp.start()

    def finish(self):
        for cp in self.sends:
            cp.wait_send()
        for cp in self.recvs:
            cp.wait_recv()
        self.own.wait()


class _TwoLevel:
    def __init__(self, src, dst, send_sems, recv_sems, local_sem, own=True):
        x, y, c = (lax.axis_index(a) for a in AXES)
        self.me, self.sibling = (x, y, c), (x, y, 1 - c)
        self.chips = [(1 - x, y), (x, 1 - y), (1 - x, 1 - y)]
        self.src, self.dst, self.send_sems, self.recv_sems = src, dst, send_sems, recv_sems
        self.own = pltpu.make_async_copy(src, dst.at[_dev_index(self.me)], local_sem) if own else None

    def _copy(self, k, block, to, from_src=False):
        slot = self.dst.at[_dev_index(block)]
        return pltpu.make_async_remote_copy(src_ref=self.src if from_src else slot, dst_ref=slot,
                                            send_sem=self.send_sems.at[k], recv_sem=self.recv_sems.at[k],
                                            device_id=to, device_id_type=MESH)

    def _firsts(self):
        c = self.me[2]
        return [self._copy(0, self.me, self.sibling, True)] + [self._copy(1 + j, self.me, (*chip, c), True)
                                                               for j, chip in enumerate(self.chips)]

    def _passed(self):
        c = self.me[2]
        return [self._copy(4 + j, (*chip, c), self.sibling) for j, chip in enumerate(self.chips)]

    def start(self):
        if self.own is not None:
            self.own.start()
        for cp in self._firsts():
            cp.start()

    def wait_sibling(self):
        self._copy(0, self.sibling, self.me).wait_recv()

    def wait_chip_and_forward(self, j):
        self._copy(1 + j, (*self.chips[j], self.me[2]), self.me).wait_recv()
        self._passed()[j].start()

    def wait_passed(self, j):
        self._copy(4 + j, (*self.chips[j], 1 - self.me[2]), self.me).wait_recv()

    def wait_sends(self):
        for cp in self._firsts() + self._passed():
            cp.wait_send()
        if self.own is not None:
            self.own.wait()

    def forward(self):
        for j in range(3):
            self.wait_chip_and_forward(j)

    def finish(self):
        self.wait_sibling()
        for j in range(3):
            self.wait_passed(j)
        self.wait_sends()


class _RelayGather:
    def __init__(self, dst, send_sems, recv_sems):
        x, y, c = (lax.axis_index(a) for a in AXES)
        self.c = c
        self.sib, self.xn, self.yn, self.dg = (x, y, 1 - c), (1 - x, y, c), (x, 1 - y, c), (1 - x, 1 - y, c)
        self.me = (x, y, c)
        self.dst, self.send_sems, self.recv_sems = dst, send_sems, recv_sems
        self.half = dst.shape[1] // 2

    def _slot(self, dev, part=None):
        i = _dev_index(dev)
        if part is None:
            return self.dst.at[i]
        return self.dst.at[i, pl.ds(part * self.half, self.half)]

    def _copy(self, k, dev, to, part=None):
        ref = self._slot(dev, part)
        return pltpu.make_async_remote_copy(src_ref=ref, dst_ref=ref, send_sem=self.send_sems.at[k],
                                            recv_sem=self.recv_sems.at[k], device_id=to, device_id_type=MESH)

    def _other(self, dev):
        return (dev[0], dev[1], 1 - self.c)

    def start(self):
        for k, to in enumerate((self.sib, self.xn, self.yn)):
            self._copy(k, self.me, to).start()

    def send_own(self, k):
        return self._copy(k, self.me, (self.sib, self.xn, self.yn)[k])

    def wait_sibling(self):
        self._copy(0, self.sib, self.me).wait_recv()

    def on_x(self):
        self._copy(1, self.xn, self.me).wait_recv()
        self._copy(3, self.xn, self.yn, part=0).start()
        self._copy(5, self.xn, self.sib).start()

    def on_y(self):
        self._copy(2, self.yn, self.me).wait_recv()
        self._copy(4, self.yn, self.xn, part=1).start()
        self._copy(6, self.yn, self.sib).start()

    def on_diag(self):
        self._copy(3, self.dg, self.me, part=0).wait_recv()
        self._copy(4, self.dg, self.me, part=1).wait_recv()
        self._copy(7, self.dg, self.sib).start()

    def wait_passed(self, j):
        self._copy(5 + j, self._other((self.xn, self.yn, self.dg)[j]), self.me).wait_recv()

    def wait_sends(self):
        for k, to in enumerate((self.sib, self.xn, self.yn)):
            self._copy(k, self.me, to).wait_send()
        self._copy(3, self.xn, self.yn, part=0).wait_send()
        self._copy(4, self.yn, self.xn, part=1).wait_send()
        for j, dev in enumerate((self.xn, self.yn, self.dg)):
            self._copy(5 + j, dev, self.sib).wait_send()


def _direct_sems(n):
    if n == 0:
        return []
    return [pltpu.SemaphoreType.DMA((n, 7)), pltpu.SemaphoreType.DMA((n, 7)), pltpu.SemaphoreType.DMA((n,))]


def _adam_math(w, g, m, v):
    m = ADAM_B1 * m + (1.0 - ADAM_B1) * g
    v = ADAM_B2 * v + (1.0 - ADAM_B2) * (g * g)
    m_hat = m / (1.0 - ADAM_B1 ** ADAM_STEP)
    v_hat = v / (1.0 - ADAM_B2 ** ADAM_STEP)
    delta = -ADAM_LR * (m_hat / (jnp.sqrt(v_hat) + ADAM_EPS) + ADAM_WD * w)
    return delta, m, v


def _sum_adam(parts, w, m, v, name):
    R, C = w.shape
    NP = parts.shape[0]
    BR = 4 * CHUNK if R % (4 * CHUNK) == 0 else R

    def body(p_ref, w_ref, m_ref, v_ref, g_ref, d_ref, nm_ref, nv_ref):
        g = p_ref[0].astype(F32)
        for i in range(1, NP):
            g = g + p_ref[i].astype(F32)
        g_ref[...] = g
        d_ref[...], nm_ref[...], nv_ref[...] = _adam_math(w_ref[...], g, m_ref[...], v_ref[...])

    blk = pl.BlockSpec((BR, C), lambda i: (i, 0))
    S = jax.ShapeDtypeStruct((R, C), F32)
    return pl.pallas_call(
        body, name=name, grid=(R // BR,),
        in_specs=[pl.BlockSpec((NP, BR, C), lambda i: (0, i, 0)), blk, blk, blk],
        out_specs=[blk] * 4, out_shape=(S,) * 4,
        compiler_params=_params(("arbitrary",)),
    )(parts, w, m, v)


SUBLANES = 8


def _nrows(size):
    return -(-size // (SUBLANES * LANES)) * SUBLANES


def _view2d(a):
    return a.reshape(-1, LANES) if a.size % LANES == 0 else a.reshape(1, -1)


def _pack_small(parts, total_rows, name):
    arrs = [p[0] for p in parts]

    def body(*refs):
        out = refs[-1]
        out[...] = jnp.zeros_like(out)
        at = 0
        for ref, (a, rows, flag) in zip(refs[:-1], parts):
            val = ref[...].T if flag == "T" else ref[...]
            r, c = (rows, val.shape[1]) if flag == "T" else val.shape
            out[at:at + r, 0:c] = val[:r]
            at += _nrows(r * c)

    return pl.pallas_call(body, name=name, out_shape=jax.ShapeDtypeStruct((total_rows, LANES), F32))(*arrs)


def _small_update(full, me, reps, shards, name):
    n = len(reps) + len(shards)

    def body(me_ref, full_ref, *refs):
        ins, outs = refs[:3 * n], refs[3 * n:]
        at = 0
        for k in range(n):
            w_ref, m_ref, v_ref = ins[3 * k:3 * k + 3]
            r, c = w_ref.shape
            if k < len(reps):
                g = full_ref[at:at + r, 0:c]
                at += _nrows(r * c)
            else:
                seg = full_ref[at:at + N_DEV * r, :]
                row = lax.broadcasted_iota(jnp.int32, seg.shape, 0)
                pick = [jnp.sum(jnp.where(row == r * me_ref[0] + t, seg, 0.0), axis=0, keepdims=True) for t in range(r)]
                g = pick[0] if r == 1 else jnp.concatenate(pick, axis=0)
                at += N_DEV * r
            g_ref, d_ref, nm_ref, nv_ref = outs[4 * k:4 * k + 4]
            g_ref[...] = g
            d_ref[...], nm_ref[...], nv_ref[...] = _adam_math(w_ref[...], g, m_ref[...], v_ref[...])
        outs[4 * n][...] = full_ref[at:at + 1, 0:1]

    flat = [t for p in reps + shards for t in p]
    S = jax.ShapeDtypeStruct
    res = pl.pallas_call(
        body, name=name,
        in_specs=[pl.BlockSpec(memory_space=pltpu.SMEM)] + [pl.BlockSpec(memory_space=pltpu.VMEM)] * (1 + len(flat)),
        out_shape=[S(p[0].shape, F32) for p in reps + shards for _ in range(4)] + [S((1, 1), F32)],
    )(me, full, *flat)
    return [tuple(res[4 * k:4 * k + 4]) for k in range(n)], res[4 * n]


def _rope_tables(T):
    pos = np.arange(T, dtype=np.float32)
    inv_freq = (np.float64(ROPE_THETA) ** (-np.arange(0, HEAD_DIM, 2, dtype=np.float64) / HEAD_DIM)).astype(np.float32)
    ang = (pos[:, None] * inv_freq[None, :]).astype(np.float64)
    cos, sin, zero = np.cos(ang).astype(np.float32), np.sin(ang).astype(np.float32), np.zeros(ang.shape, np.float32)
    c = np.concatenate([cos, cos, cos, cos], axis=1)
    s1 = np.concatenate([-sin, zero, -sin, zero], axis=1)
    s2 = np.concatenate([zero, sin, zero, sin], axis=1)
    return jnp.asarray(c), jnp.asarray(s1), jnp.asarray(s2)


def kernel(x, a_norm_g, a_w_in, a_ln_g, a_ln_b, a_ws, a_bs, a_w_out, kv_norm_g, w_kv, b_kv, b_norm_g, b_w_in, b_bq, b_sinks, b_w_out, final_norm_g, loss_target, m_a_norm_g, m_a_w_in, m_a_ln_g, m_a_ln_b, m_a_ws, m_a_bs, m_a_w_out, m_kv_norm_g, m_w_kv, m_b_kv, m_b_norm_g, m_b_w_in, m_b_bq, m_b_sinks, m_b_w_out, m_final_norm_g, v_a_norm_g, v_a_w_in, v_a_ln_g, v_a_ln_b, v_a_ws, v_a_bs, v_a_w_out, v_kv_norm_g, v_w_kv, v_b_kv, v_b_norm_g, v_b_w_in, v_b_bq, v_b_sinks, v_b_w_out, v_final_norm_g):
    T, D = x.shape[1], x.shape[2]
    AW = a_ln_g.shape[1] * N_DEV
    G = a_ws.shape[1]
    assert w_kv.shape[1] == 2 * LANES and a_ws.shape[2] == CHUNK and T % CHUNK == 0
    me = _my_index()

    xs, tgt = x[0], loss_target[0]
    vec = jnp.concatenate([a_norm_g, a_ln_g, a_ln_b], axis=1)
    vec = jnp.broadcast_to(vec, (SUBLANES, vec.shape[1]))
    north = lax.axis_index("c") == 1
    slots = me ^ jnp.where(north, jnp.array(PASS_MASKS[1], jnp.int32), jnp.array(PASS_MASKS[0], jnp.int32))
    z, wa_in, vecs, wa_out, wkv = _in_proj(xs, a_w_in[0], vec, slots, [a_w_out[0], w_kv])
    wa_out = wa_out.reshape(AW, D)
    wkv = wkv.reshape(D, 2 * LANES)
    vecs = vecs[:, 0, :]
    ds = D // N_DEV
    g_a = vecs[:, :ds].reshape(1, D)
    ln_g = vecs[:, ds:ds + AW // N_DEV].reshape(1, AW)
    ln_b = vecs[:, ds + AW // N_DEV:].reshape(1, AW)

    rc, rs1, rs2 = _rope_tables(T)
    ws = a_ws[0]
    bs_t = a_bs[0].T
    g_kv = kv_norm_g.reshape(1, D)
    bkv = b_kv.reshape(1, -1)
    g_f = final_norm_g.reshape(1, D)
    sinks = jnp.repeat(b_sinks.reshape(2, 4, 2).transpose(0, 2, 1).reshape(4, 4), CHUNK, axis=1)
    h1, sv, vhat, rstd, k4, v4, kt, vt, wb_in, wb_out = _a_fwd(
        xs, z, ln_g, ln_b, ws, bs_t, wa_out, g_kv, wkv, bkv, rc, rs1, rs2, [b_w_in[0], b_w_out[0]])
    wb_out = wb_out.reshape(-1, D)
    q, g2, o, dh2, dh2_b, loss, d_gf = _b_fwd(h1, b_norm_g, wb_in, b_bq, rc, rs1, rs2, k4, vt, sinks, wb_out, g_f, tgt)
    dh1p, dz2, n2, y2, dk, dv, d_bq, d_gb, d_sink = _b_bwd(dh2, h1, q, g2, o, k4, v4, kt, sinks, wb_out, wb_in,
                                                           b_norm_g, rc, rs1, rs2)
    d_sink = d_sink[:, :4].reshape(2, 2, 4).transpose(0, 2, 1).reshape(1, 16)
    gw_b_in = _wgrad(n2, dz2, N_DEV, "wgrad_b_in", bt=1024)
    gw_b_out = _wgrad(y2, dh2_b, 1, "wgrad_b_out", bt=1024).reshape(N_DEV, -1, D)
    (dz, gw_a_out, gw_kv, dh1_f, d_gkv, d_bkv, d_lng, d_lnb, d_ws, d_bst, r_b_in, r_b_out) = _a_bwd(
        dh1p, dk, dv, h1, g_kv, wkv, wa_out, ws, ln_g, ln_b, z, sv, vhat, rstd, rc, rs1, rs2, [gw_b_in, gw_b_out])
    dx, n1, d_ga, r_a_out, r_kv = _a_in_bwd(dz, wa_in, xs, dh1_f, g_a, [gw_a_out, gw_kv])
    small = [(_view2d(d_ws), None, None), (d_bst, G, "T")] + [(_view2d(a), None, None) for a in (
        d_gkv, d_bkv, d_gb, d_bq, d_sink, d_gf, d_ga, d_lng, d_lnb, loss)]
    used = sum(_nrows(G * CHUNK if flag else a.size) for a, _, flag in small)
    per = -(-used // (SUBLANES * N_DEV)) * SUBLANES
    small_pack = _pack_small(small, per * N_DEV, "pack_small").reshape(N_DEV, per, LANES)
    r_a_in, full_small = _wgrad_exchange(n1, dz, me.reshape(1), small_pack, "wgrad_a_in")

    g_a_in, d_a_in, nm_a_in, nv_a_in = _sum_adam(r_a_in, a_w_in[0], m_a_w_in[0], v_a_w_in[0], "adam_a_in")
    g_a_out, d_a_out, nm_a_out, nv_a_out = _sum_adam(r_a_out, a_w_out[0], m_a_w_out[0], v_a_w_out[0], "adam_a_out")
    g_kvw, d_kvw, nm_kvw, nv_kvw = _sum_adam(r_kv, w_kv, m_w_kv, v_w_kv, "adam_kv")
    g_b_in, d_b_in, nm_b_in, nv_b_in = _sum_adam(r_b_in, b_w_in[0], m_b_w_in[0], v_b_w_in[0], "adam_b_in")
    g_b_out, d_b_out, nm_b_out, nv_b_out = _sum_adam(r_b_out, b_w_out[0], m_b_w_out[0], v_b_w_out[0], "adam_b_out")

    full_small = full_small.reshape(N_DEV * per, LANES)
    reps = [(a_ws, m_a_ws, v_a_ws), (a_bs, m_a_bs, v_a_bs), (kv_norm_g, m_kv_norm_g, v_kv_norm_g),
            (b_kv, m_b_kv, v_b_kv), (b_norm_g, m_b_norm_g, v_b_norm_g), (b_bq, m_b_bq, v_b_bq),
            (b_sinks, m_b_sinks, v_b_sinks), (final_norm_g, m_final_norm_g, v_final_norm_g)]
    shards = [(a_norm_g, m_a_norm_g, v_a_norm_g), (a_ln_g, m_a_ln_g, v_a_ln_g), (a_ln_b, m_a_ln_b, v_a_ln_b)]
    upd, loss = _small_update(full_small, me.reshape(1), [tuple(_view2d(t) for t in p) for p in reps],
                              [tuple(_view2d(t) for t in p) for p in shards], "adam_small")
    loss = loss[0, 0]
    sm_g, sd, snm, snv = ([upd[k][j].reshape(p[0].shape) for k, p in enumerate(reps + shards)] for j in range(4))

    def order(big, sm):
        a_in, a_out, kvw, b_in, b_out = big
        ws_, bs_, kvg, bkv_, bng, bq_, snk, fng, ang, alng, alnb = sm
        return (ang, a_in[None], alng, alnb, ws_, bs_, a_out[None], kvg, kvw, bkv_, bng, b_in[None], bq_, snk,
                b_out[None], fng)

    grads = order((g_a_in, g_a_out, g_kvw, g_b_in, g_b_out), sm_g)
    deltas = order((d_a_in, d_a_out, d_kvw, d_b_in, d_b_out), sd)
    new_m = order((nm_a_in, nm_a_out, nm_kvw, nm_b_in, nm_b_out), snm)
    new_v = order((nv_a_in, nv_a_out, nv_kvw, nv_b_in, nv_b_out), snv)
    return (loss, dx[None], *grads, *deltas, *new_m, *new_v)
```

```python
import functools

import jax
import jax.numpy as jnp
import numpy as np
from jax import lax
from jax.experimental import pallas as pl
from jax.experimental.pallas import tpu as pltpu

CHUNK = 128
HEAD_DIM = 64
ROPE_THETA = 10000.0
EPS = 1e-5
ADAM_LR = 0.001
ADAM_B1 = 0.9
ADAM_B2 = 0.999
ADAM_EPS = 1e-08
ADAM_WD = 0.01
ADAM_STEP = 10
N_DEV = 8
LANES = 128
NEG = -1e30

BF = jnp.bfloat16
F32 = jnp.float32
MESH = pl.DeviceIdType.MESH
AXES = ("x", "y", "c")
VMEM_LIMIT = 56 * 1024 * 1024


def _dot(a, b):
    return jnp.dot(a, b, preferred_element_type=F32)


def _dot_nt(a, b):
    return lax.dot_general(a, b, (((1,), (1,)), ((), ())), preferred_element_type=F32)


def _dot_tn(a, b):
    return lax.dot_general(a, b, (((0,), (0,)), ((), ())), preferred_element_type=F32)


def _const_spec(shape):
    nd = len(shape)
    return pl.BlockSpec(shape, lambda *_: (0,) * nd, pipeline_mode=pl.Buffered(1))


def _acc_spec(shape):
    nd = len(shape)
    return pl.BlockSpec(shape, lambda *_: (0,) * nd)


def _row_spec(tm, width):
    return pl.BlockSpec((tm, width), lambda i: (i, 0))


def _col_spec(tm, height):
    return pl.BlockSpec((height, tm), lambda i: (0, i))


def _params(sem):
    return pltpu.CompilerParams(dimension_semantics=sem, vmem_limit_bytes=VMEM_LIMIT)


def _rot(x, c, s1, s2):
    return x * c + pltpu.roll(x, 96, 1) * s1 + pltpu.roll(x, 32, 1) * s2


def _rot_bwd(d, c, s1, s2):
    return d * c + pltpu.roll(d * s1, 32, 1) + pltpu.roll(d * s2, 96, 1)


def _silu_parts(g):
    sg = jax.nn.sigmoid(g)
    return g * sg, sg * (1.0 + g * (1.0 - sg))


def _rms_bwd(dn, xh, r, g):
    a = dn * g
    return r * (a - xh * jnp.mean(a * xh, axis=-1, keepdims=True))


def _lane_lo(shape):
    return lax.broadcasted_iota(jnp.int32, shape, 1) < HEAD_DIM


def _split4(t):
    lo = _lane_lo(t.shape)
    tr = pltpu.roll(t, HEAD_DIM, 1)
    z = jnp.zeros_like(t)
    return jnp.concatenate([jnp.where(lo, t, z), jnp.where(lo, z, tr), jnp.where(lo, tr, z), jnp.where(lo, z, t)], axis=1)


def _stack_pairs(t, h):
    return jnp.concatenate([t[:, (h * 4 + j) * LANES:(h * 4 + j + 1) * LANES] for j in range(4)], axis=0)


def _upper():
    shape = (CHUNK, 4 * CHUNK)
    return lax.broadcasted_iota(jnp.int32, shape, 0) > (lax.broadcasted_iota(jnp.int32, shape, 1) & (CHUNK - 1))


def _band_rows(tile_ref, before_ref, c, h):
    a = slice(2 * h * LANES, (2 * h + 1) * LANES)
    b = slice((2 * h + 1) * LANES, (2 * h + 2) * LANES)
    cur = slice(c * CHUNK, (c + 1) * CHUNK)

    def prev(cols):
        return before_ref[:, cols] if c == 0 else tile_ref[(c - 1) * CHUNK:c * CHUNK, cols]

    return jnp.concatenate([prev(a), tile_ref[cur, a], prev(b), tile_ref[cur, b]], axis=0)


def _band_cols(tile_ref, before_ref, c, h):
    a = slice(2 * h * LANES, (2 * h + 1) * LANES)
    b = slice((2 * h + 1) * LANES, (2 * h + 2) * LANES)

    def prev(rows):
        return before_ref[0, rows, :] if c == 0 else tile_ref[c - 1, rows, :]

    return jnp.concatenate([prev(a), tile_ref[c, a, :], prev(b), tile_ref[c, b, :]], axis=1)


def _band_specs(tm):
    nc = tm // CHUNK

    def before(i):
        return jnp.maximum(i * nc - 1, 0)

    return (pl.BlockSpec((tm, 4 * LANES), lambda i: (i, 0)),
            pl.BlockSpec((CHUNK, 4 * LANES), lambda i: (before(i), 0)),
            pl.BlockSpec((nc, 4 * LANES, CHUNK), lambda i: (i, 0, 0)),
            pl.BlockSpec((1, 4 * LANES, CHUNK), lambda i: (before(i), 0, 0)))


def _fold(t, upper, has_prev=None):
    out = []
    for k in range(2):
        prev = t[2 * k * CHUNK:(2 * k + 1) * CHUNK]
        if has_prev is not None:
            prev = jnp.where(has_prev, prev, NEG)
        out.append(jnp.where(upper, prev, t[(2 * k + 1) * CHUNK:(2 * k + 2) * CHUNK]))
    return out


def _unfold(fa, fb, upper):
    z = jnp.zeros_like(fa)
    return jnp.concatenate([jnp.where(upper, fa, z), jnp.where(upper, z, fa),
                            jnp.where(upper, fb, z), jnp.where(upper, z, fb)], axis=0)


def _softmax_sink(f, sink):
    m = jnp.maximum(jnp.max(f, axis=0, keepdims=True), sink)
    p = jnp.exp(f - m)
    es = jnp.exp(sink - m)
    inv = 1.0 / (jnp.sum(p, axis=0, keepdims=True) + es)
    return p * inv, es * inv


class _Riding:
    def __init__(self, shards, gathered, stages, sems, n_steps):
        self.shards, self.stages, self.n_steps = shards, stages, n_steps
        ssem, rsem, lsem = sems
        self.gathers = [_TwoLevel(stages[k], gathered[k], ssem.at[k], rsem.at[k], lsem.at[k])
                        for k in range(len(shards))]

    def begin(self, i):
        @pl.when(i == 0)
        def _():
            for shard, stage, g in zip(self.shards, self.stages, self.gathers):
                stage[...] = shard[...].astype(stage.dtype)
                g.start()

    def end(self, i):
        @pl.when(i == self.n_steps // 2)
        def _():
            for g in self.gathers:
                g.forward()

        @pl.when(i == self.n_steps - 1)
        def _():
            for g in self.gathers:
                g.finish()

    @staticmethod
    def specs(later):
        nl = len(later)
        hbm = pl.BlockSpec(memory_space=pl.ANY)
        return ([_const_spec(w.shape) for w in later], [hbm] * nl,
                tuple(jax.ShapeDtypeStruct((N_DEV,) + w.shape, BF) for w in later),
                [pltpu.VMEM(w.shape, BF) for w in later] + _direct_sems(nl))


PASS_MASKS = ((0, 1, 2, 5, 4, 3, 6, 7), (0, 1, 4, 3, 2, 5, 6, 7))


def _in_proj(x, w_shard, vec_shard, slots, later):
    T, D = x.shape
    SH = w_shard.shape[1]
    TM = min(1024, T)
    nT = T // TM
    nl = len(later)
    ds = D // N_DEV
    last = N_DEV - 1

    def body(slots_ref, x_ref, wsh_ref, vsh_ref, *rest):
        shards, rest = rest[:nl], rest[nl:]
        (z_ref, wt_ref, vout_ref), rest = rest[:3], rest[3:]
        gathered, rest = rest[:nl], rest[nl:]
        (w_scr, vec_scr, vstage, n1_scr, ga_scr, w_s, w_r, v_s, v_r, v_l), rest = rest[:10], rest[10:]
        stages, sems = rest[:nl], rest[nl:]
        p, i = pl.program_id(0), pl.program_id(1)
        me = _my_index()
        wg = _RelayGather(w_scr, w_s, w_r)
        vg = _Direct(vstage, vec_scr, v_s, v_r, v_l, scatter=False)
        lg = [_TwoLevel(stages[k], gathered[k], sems[0].at[k], sems[1].at[k], sems[2].at[k]) for k in range(nl)]

        def at_pass(k):
            return (p == k) & (i == 0)

        c = lax.axis_index("c")

        @pl.when(at_pass(0))
        def _():
            vstage[...] = vsh_ref[...]
            vg.start()
            w_scr[me] = wsh_ref[...].astype(BF)
            wg.send_own(0).start()

            @pl.when(c == 1)
            def _():
                wg.send_own(1).start()

            @pl.when(c == 0)
            def _():
                wg.send_own(2).start()

            vg.finish()
            for j in range(N_DEV):
                ga_scr[:, j * ds:(j + 1) * ds] = vec_scr[j, 0:1, 0:ds]
            vout_ref[...] = vec_scr[...]

        @pl.when(at_pass(1))
        def _():
            wg.wait_sibling()

        for first, second, landed_first, landed_second in ((1, 2, wg.on_x, wg.on_y), (2, 1, wg.on_y, wg.on_x)):
            mine = c == (1 if first == 1 else 0)

            @pl.when(at_pass(2) & mine)
            def _(second=second, landed_first=landed_first):
                wg.send_own(second).start()
                landed_first()

            @pl.when(at_pass(3) & mine)
            def _(second=second):
                wg.wait_passed(second - 1)

            @pl.when(at_pass(4) & mine)
            def _(landed_second=landed_second):
                landed_second()

            @pl.when(at_pass(5) & mine)
            def _(first=first):
                wg.wait_passed(first - 1)

        @pl.when(at_pass(4))
        def _():
            for k in range(nl):
                stages[k][...] = shards[k][...].astype(BF)
                lg[k].start()

        @pl.when(at_pass(6))
        def _():
            wg.on_diag()

        @pl.when(at_pass(7))
        def _():
            wg.wait_passed(2)

        @pl.when(p == 0)
        def _():
            xv = x_ref[...]
            r1 = lax.rsqrt(jnp.mean(xv * xv, axis=-1, keepdims=True) + EPS)
            n1_scr[i] = (xv * r1 * ga_scr[...]).astype(BF)

        z_ref[...] = _dot(n1_scr[i], w_scr[slots_ref[p]]).astype(BF)

        @pl.when(i == 0)
        def _():
            wt_ref[0] = w_scr[slots_ref[p]].T

        @pl.when((p == last) & (i == nT - 1))
        def _():
            wg.wait_sends()
            for g in lg:
                g.forward()
            for g in lg:
                g.finish()

    hbm = pl.BlockSpec(memory_space=pl.ANY)
    dma = pltpu.SemaphoreType.DMA
    S = jax.ShapeDtypeStruct
    grid_spec = pltpu.PrefetchScalarGridSpec(
        num_scalar_prefetch=1, grid=(N_DEV, nT),
        in_specs=[pl.BlockSpec((TM, D), lambda p, i, s: (jnp.where(p == 0, i, nT - 1), 0)),
                  pl.BlockSpec(w_shard.shape, lambda p, i, s: (0, 0), pipeline_mode=pl.Buffered(1)),
                  pl.BlockSpec(vec_shard.shape, lambda p, i, s: (0, 0), pipeline_mode=pl.Buffered(1))]
        + [pl.BlockSpec(w.shape, lambda p, i, s: (0, 0), pipeline_mode=pl.Buffered(1)) for w in later],
        out_specs=[pl.BlockSpec((TM, SH), lambda p, i, s: (i, s[p])),
                   pl.BlockSpec((1, SH, D), lambda p, i, s: (s[p], 0, 0)),
                   pl.BlockSpec((N_DEV,) + vec_shard.shape, lambda p, i, s: (0, 0, 0))] + [hbm] * nl,
        scratch_shapes=[pltpu.VMEM((N_DEV, D, SH), BF), pltpu.VMEM((N_DEV,) + vec_shard.shape, F32),
                        pltpu.VMEM(vec_shard.shape, F32), pltpu.VMEM((nT, TM, D), BF), pltpu.VMEM((1, D), F32),
                        dma((8,)), dma((8,)), dma((7,)), dma((7,)), dma]
        + [pltpu.VMEM(w.shape, BF) for w in later] + _direct_sems(nl))
    return pl.pallas_call(
        body, name="a_in_proj", grid_spec=grid_spec,
        out_shape=(S((T, N_DEV * SH), BF), S((N_DEV, SH, D), BF), S((N_DEV,) + vec_shard.shape, F32))
        + tuple(S((N_DEV,) + w.shape, BF) for w in later),
        compiler_params=_params(("arbitrary", "arbitrary")),
    )(slots, x, w_shard, vec_shard, *later)


def _a_fwd(x, z, ln_g, ln_b, ws, bs_t, wa_out, g_kv, w_kv, b_kv, rc, rs1, rs2, later):
    T, D = x.shape
    AW = wa_out.shape[0]
    G = ws.shape[0]
    TM = min(256, T)
    nT = T // TM
    nC = TM // CHUNK
    nl = len(later)

    def body(x_ref, u_ref, v_ref, gt_ref, lng_ref, lnb_ref, ws_ref, bst_ref, waout_ref, gkv_ref, wkv_ref, bkv_ref,
             rc_ref, rs1_ref, rs2_ref, *rest):
        shards, rest = rest[:nl], rest[nl:]
        (h1_ref, sv_ref, vhat_ref, rstd_ref, k4_ref, v4_ref, kt_ref, vt_ref), rest = rest[:8], rest[8:]
        gathered, sv_scr, stages, sems = rest[:nl], rest[nl], rest[nl + 1:2 * nl + 1], rest[2 * nl + 1:]
        i = pl.program_id(0)
        riding = _Riding(shards, gathered, stages, sems, nT)
        riding.begin(i)
        xv = x_ref[...]
        u = u_ref[...].astype(F32)
        v = v_ref[...].astype(F32)
        gt = gt_ref[...].astype(F32)
        mu = jnp.mean(v, axis=-1, keepdims=True)
        xc = v - mu
        rstd = lax.rsqrt(jnp.mean(xc * xc, axis=-1, keepdims=True) + EPS)
        vhat = xc * rstd
        vln = (vhat * lng_ref[...] + lnb_ref[...]).astype(BF)
        tri = lax.broadcasted_iota(jnp.int32, (CHUNK, CHUNK), 0) >= lax.broadcasted_iota(jnp.int32, (CHUNK, CHUNK), 1)
        for g in range(G):
            wsm = jnp.where(tri, ws_ref[g], 0.0).astype(BF)
            bias = bst_ref[:, g:g + 1]
            for c in range(nC):
                blk = vln[c * CHUNK:(c + 1) * CHUNK, g * CHUNK:(g + 1) * CHUNK]
                sv_scr[c * CHUNK:(c + 1) * CHUNK, g * CHUNK:(g + 1) * CHUNK] = _dot(wsm, blk) + bias
        sv = sv_scr[...]
        silu, _ = _silu_parts(gt)
        y = (u * sv * silu).astype(BF)
        h1 = xv + _dot(y, waout_ref[...])
        h1_ref[...] = h1
        sv_ref[...] = sv.astype(BF)
        vhat_ref[...] = vhat.astype(BF)
        rstd_ref[...] = jnp.broadcast_to(rstd, rstd_ref.shape)
        rkv = lax.rsqrt(jnp.mean(h1 * h1, axis=-1, keepdims=True) + EPS)
        nkv = (h1 * rkv * gkv_ref[...]).astype(BF)
        kv = _dot(nkv, wkv_ref[...]) + bkv_ref[...]
        k_rot = _rot(kv[:, :LANES], rc_ref[...], rs1_ref[...], rs2_ref[...])
        for src, ref, tref in ((k_rot, k4_ref, kt_ref), (kv[:, LANES:], v4_ref, vt_ref)):
            t4 = _split4(src)
            ref[...] = t4.astype(BF)
            for c in range(nC):
                for b in range(4):
                    blk = t4[c * CHUNK:(c + 1) * CHUNK, b * LANES:(b + 1) * LANES]
                    tref[c, b * LANES:(b + 1) * LANES, :] = blk.T.astype(BF)
        riding.end(i)

    row = functools.partial(_row_spec, TM)
    zcol = [pl.BlockSpec((TM, AW), functools.partial(lambda k, i: (i, k), k)) for k in range(3)]
    tr = pl.BlockSpec((nC, 4 * LANES, CHUNK), lambda i: (i, 0, 0))
    r_in, r_out, r_shape, r_scratch = _Riding.specs(later)
    S = jax.ShapeDtypeStruct
    return pl.pallas_call(
        body, name="a_fwd", grid=(nT,),
        in_specs=[row(D)] + zcol + [_const_spec((1, AW)), _const_spec((1, AW)),
                  _const_spec(ws.shape), _const_spec(bs_t.shape), _const_spec(wa_out.shape), _const_spec((1, D)),
                  _const_spec(w_kv.shape), _const_spec((1, 2 * LANES)), row(LANES), row(LANES), row(LANES)] + r_in,
        out_specs=[row(D), row(AW), row(AW), row(LANES), row(4 * LANES), row(4 * LANES), tr, tr] + r_out,
        out_shape=(S((T, D), F32), S((T, AW), BF), S((T, AW), BF), S((T, LANES), F32),
                   S((T, 4 * LANES), BF), S((T, 4 * LANES), BF),
                   S((T // CHUNK, 4 * LANES, CHUNK), BF), S((T // CHUNK, 4 * LANES, CHUNK), BF)) + r_shape,
        scratch_shapes=[pltpu.VMEM((TM, AW), F32)] + r_scratch,
        compiler_params=_params(("arbitrary",)),
    )(x, z, z, z, ln_g, ln_b, ws, bs_t, wa_out, g_kv, w_kv, b_kv, rc, rs1, rs2, *later)


def _b_fwd(h1, g_b, wb_in, bq, rc, rs1, rs2, k4, vt, sinks, wb_out, g_f, target):
    T, D = h1.shape
    BW = wb_out.shape[0]
    SH = wb_in.shape[2]
    TM = min(512, T)
    nC = TM // CHUNK
    nP = BW // LANES

    def body(h1_ref, gb_ref, wbin_ref, bq_ref, rc_ref, rs1_ref, rs2_ref, k4_ref, k4p_ref, vt_ref, vtp_ref, sink_ref,
             wbout_ref, gf_ref, tgt_ref, q_ref, g2_ref, o_ref, dh2_ref, dh2b_ref, loss_ref, dgf_ref, z_scr, o_scr):
        i = pl.program_id(0)

        @pl.when(i == 0)
        def _():
            loss_ref[...] = jnp.zeros_like(loss_ref)
            dgf_ref[...] = jnp.zeros_like(dgf_ref)

        h1v = h1_ref[...]
        r2 = lax.rsqrt(jnp.mean(h1v * h1v, axis=-1, keepdims=True) + EPS)
        n2 = (h1v * r2 * gb_ref[...]).astype(BF)
        for j in range(N_DEV):
            z_scr[:, j * SH:(j + 1) * SH] = _dot(n2, wbin_ref[j])
        c_t, s1_t, s2_t = rc_ref[...], rs1_ref[...], rs2_ref[...]
        for p in range(nP):
            cols = slice(p * LANES, (p + 1) * LANES)
            qp = _rot(z_scr[:, cols] + bq_ref[:, cols], c_t, s1_t, s2_t) * (HEAD_DIM ** -0.5)
            q_ref[:, cols] = qp.astype(BF)
        g2 = z_scr[:, BW:]
        g2_ref[...] = g2.astype(BF)
        upper = _upper()
        for c in range(nC):
            ci = i * nC + c
            rows = slice(c * CHUNK, (c + 1) * CHUNK)
            qc = q_ref[rows, :]
            for h in range(2):
                st = _dot_nt(_band_rows(k4_ref, k4p_ref, c, h), _stack_pairs(qc, h))
                fa, fb = _fold(st, upper, ci > 0)
                pa, _ = _softmax_sink(fa, sink_ref[2 * h:2 * h + 1, :])
                pb, _ = _softmax_sink(fb, sink_ref[2 * h + 1:2 * h + 2, :])
                ot = _dot(_band_cols(vt_ref, vtp_ref, c, h), _unfold(pa, pb, upper).astype(BF))
                for j in range(4):
                    o_scr[rows, (h * 4 + j) * LANES:(h * 4 + j + 1) * LANES] = ot[:, j * CHUNK:(j + 1) * CHUNK].T
        o = o_scr[...]
        o_ref[...] = o.astype(BF)
        silu, _ = _silu_parts(g2)
        h2 = h1v + _dot((o * silu).astype(BF), wbout_ref[...])
        rf = lax.rsqrt(jnp.mean(h2 * h2, axis=-1, keepdims=True) + EPS)
        xh = h2 * rf
        gf = gf_ref[...]
        err = xh * gf - tgt_ref[...]
        dyf = err * (1.0 / D)
        dh2 = _rms_bwd(dyf, xh, rf, gf)
        dh2_ref[...] = dh2
        dh2b_ref[...] = dh2.astype(BF)
        loss_ref[...] += 0.5 * jnp.sum(jnp.mean(err * err, axis=-1, keepdims=True), axis=0, keepdims=True)
        dgf_ref[...] += jnp.sum(dyf * xh, axis=0, keepdims=True)

    row = functools.partial(_row_spec, TM)
    rows_tile, rows_before, cols_tile, cols_before = _band_specs(TM)
    S = jax.ShapeDtypeStruct
    return pl.pallas_call(
        body, name="b_fwd", grid=(T // TM,),
        in_specs=[row(D), _const_spec((1, D)), _const_spec(wb_in.shape), _const_spec((1, BW)), row(LANES), row(LANES),
                  row(LANES), rows_tile, rows_before, cols_tile, cols_before, _const_spec(sinks.shape),
                  _const_spec(wb_out.shape), _const_spec((1, D)), row(D)],
        out_specs=[row(BW), row(BW), row(BW), row(D), row(D), _acc_spec((1, 1)), _acc_spec((1, D))],
        out_shape=(S((T, BW), BF), S((T, BW), BF), S((T, BW), BF), S((T, D), F32), S((T, D), BF), S((1, 1), F32),
                   S((1, D), F32)),
        scratch_shapes=[pltpu.VMEM((TM, 2 * BW), F32), pltpu.VMEM((TM, BW), F32)],
        compiler_params=_params(("arbitrary",)),
    )(h1, g_b, wb_in, bq, rc, rs1, rs2, k4, k4, vt, vt, sinks, wb_out, g_f, target)


def _b_bwd(dh2, h1, q, g2, o, k4, v4, kt, sinks, wb_out, wb_in, g_b, rc, rs1, rs2):
    T, D = h1.shape
    BW = wb_out.shape[0]
    SH = wb_in.shape[2]
    TM = min(256, T)
    nT = T // TM
    nC = TM // CHUNK
    nP = BW // LANES

    def body(dh2_ref, h1_ref, q_ref, g2_ref, o_ref, k4_ref, k4p_ref, v4_ref, v4p_ref, kt_ref, ktp_ref, sink_ref,
             wbout_ref, wbin_ref, gb_ref, rc_ref, rs1_ref, rs2_ref,
             dh1_ref, dz2_ref, n2_ref, y2_ref, dk_ref, dv_ref, dbq_ref, dgb_ref, dsink_ref, do_scr, dq_scr, dsacc_scr):
        i = pl.program_id(0)

        @pl.when(i == 0)
        def _():
            dk_ref[...] = jnp.zeros_like(dk_ref)
            dv_ref[...] = jnp.zeros_like(dv_ref)
            dbq_ref[...] = jnp.zeros_like(dbq_ref)
            dgb_ref[...] = jnp.zeros_like(dgb_ref)
            dsacc_scr[...] = jnp.zeros_like(dsacc_scr)

        dh2 = dh2_ref[...]
        dy2 = _dot_nt(dh2.astype(BF), wbout_ref[...])
        silu, dsilu = _silu_parts(g2_ref[...].astype(F32))
        do_scr[...] = (dy2 * silu).astype(BF)
        dy2, silu, dsilu = dy2.astype(BF), silu.astype(BF), dsilu.astype(BF)
        ob = o_ref[...]
        y2_ref[...] = (ob * silu).T
        dz2_ref[:, BW:] = dy2 * ob * dsilu
        upper = _upper()
        lo = _lane_lo((2 * CHUNK, LANES))
        for c in range(nC):
            ci = i * nC + c
            rows = slice(c * CHUNK, (c + 1) * CHUNK)
            pci = jnp.maximum(ci - 1, 0)
            prev = pl.multiple_of(pci * CHUNK, CHUNK)
            cur = pl.multiple_of(ci * CHUNK, CHUNK)
            qc = q_ref[rows, :]
            doc = do_scr[rows, :]
            dkb = jnp.zeros((2 * CHUNK, LANES), F32)
            dvb = jnp.zeros((2 * CHUNK, LANES), F32)
            for h in range(2):
                qs = _stack_pairs(qc, h)
                dos = _stack_pairs(doc, h)
                fa, fb = _fold(_dot_nt(_band_rows(k4_ref, k4p_ref, c, h), qs), upper, ci > 0)
                dfa, dfb = _fold(_dot_nt(_band_rows(v4_ref, v4p_ref, c, h), dos), upper)
                folded = []
                for k, (f, df) in enumerate(((fa, dfa), (fb, dfb))):
                    p, ps = _softmax_sink(f, sink_ref[2 * h + k:2 * h + k + 1, :])
                    delta = jnp.sum(p * df, axis=0, keepdims=True)
                    dsacc_scr[2 * h + k:2 * h + k + 1, :] -= ps * delta
                    folded.append((p, p * (df - delta)))
                pt = _unfold(folded[0][0], folded[1][0], upper).astype(BF)
                dst = _unfold(folded[0][1], folded[1][1], upper).astype(BF)
                dqt = _dot(_band_cols(kt_ref, ktp_ref, c, h), dst)
                for j in range(4):
                    dq_scr[rows, (h * 4 + j) * LANES:(h * 4 + j + 1) * LANES] = dqt[:, j * CHUNK:(j + 1) * CHUNK].T
                for acc_name, g in (("k", _dot(dst, qs)), ("v", _dot(pt, dos))):
                    a, b = g[:2 * CHUNK], g[2 * CHUNK:]
                    if h == 0:
                        part = jnp.where(lo, a + pltpu.roll(b, HEAD_DIM, 1), 0.0)
                    else:
                        part = jnp.where(lo, 0.0, pltpu.roll(a, HEAD_DIM, 1) + b)
                    if acc_name == "k":
                        dkb += part
                    else:
                        dvb += part
            dk_ref[pl.ds(prev, CHUNK), :] += dkb[:CHUNK]
            dk_ref[pl.ds(cur, CHUNK), :] += dkb[CHUNK:]
            dv_ref[pl.ds(prev, CHUNK), :] += dvb[:CHUNK]
            dv_ref[pl.ds(cur, CHUNK), :] += dvb[CHUNK:]
        c_t, s1_t, s2_t = rc_ref[...], rs1_ref[...], rs2_ref[...]
        for p in range(nP):
            cols = slice(p * LANES, (p + 1) * LANES)
            dqp = _rot_bwd(dq_scr[:, cols] * (HEAD_DIM ** -0.5), c_t, s1_t, s2_t)
            dbq_ref[:, cols] += jnp.sum(dqp, axis=0, keepdims=True)
            dz2_ref[:, cols] = dqp.astype(BF)
        h1v = h1_ref[...]
        r2 = lax.rsqrt(jnp.mean(h1v * h1v, axis=-1, keepdims=True) + EPS)
        xh = h1v * r2
        gb = gb_ref[...]
        n2_ref[...] = (xh * gb).astype(BF).T
        dn2 = None
        for j in range(N_DEV):
            part = _dot_nt(dz2_ref[:, j * SH:(j + 1) * SH], wbin_ref[j])
            dn2 = part if dn2 is None else dn2 + part
        dgb_ref[...] += jnp.sum(dn2 * xh, axis=0, keepdims=True)
        dh1_ref[...] = dh2 + _rms_bwd(dn2, xh, r2, gb)

        @pl.when(i == nT - 1)
        def _():
            lane = lax.broadcasted_iota(jnp.int32, dsink_ref.shape, 1)
            tot = jnp.zeros(dsink_ref.shape, F32)
            for j in range(4):
                tot += jnp.where(lane == j, jnp.sum(dsacc_scr[:, j * CHUNK:(j + 1) * CHUNK], axis=1, keepdims=True), 0.0)
            dsink_ref[...] = tot

    row = functools.partial(_row_spec, TM)
    rows_tile, rows_before, cols_tile, cols_before = _band_specs(TM)
    S = jax.ShapeDtypeStruct
    return pl.pallas_call(
        body, name="b_bwd", grid=(T // TM,),
        in_specs=[row(D), row(D), row(BW), row(BW), row(BW), rows_tile, rows_before, rows_tile, rows_before,
                  cols_tile, cols_before, _const_spec(sinks.shape), _const_spec(wb_out.shape), _const_spec(wb_in.shape),
                  _const_spec((1, D)), row(LANES), row(LANES), row(LANES)],
        out_specs=[row(D), row(2 * BW), _col_spec(TM, D), _col_spec(TM, BW), _acc_spec((T, LANES)),
                   _acc_spec((T, LANES)), _acc_spec((1, BW)), _acc_spec((1, D)), _acc_spec((4, LANES))],
        out_shape=(S((T, D), F32), S((T, 2 * BW), BF), S((D, T), BF), S((BW, T), BF), S((T, LANES), F32),
                   S((T, LANES), F32), S((1, BW), F32), S((1, D), F32), S((4, LANES), F32)),
        scratch_shapes=[pltpu.VMEM((TM, BW), BF), pltpu.VMEM((TM, BW), F32), pltpu.VMEM((4, 4 * CHUNK), F32)],
        compiler_params=_params(("arbitrary",)),
    )(dh2, h1, q, g2, o, k4, k4, v4, v4, kt, kt, sinks, wb_out, wb_in, g_b, rc, rs1, rs2)


def _a_bwd(dh1p, dk, dv, h1, g_kv, w_kv, wa_out, ws, ln_g, ln_b, z, sv, vhat, rstd, rc, rs1, rs2, ready):
    T, D = h1.shape
    AW = wa_out.shape[0]
    G = ws.shape[0]
    TM = min(256, T)
    nT = T // TM
    nC = TM // CHUNK
    nr = len(ready)

    def body(dh1p_ref, dk_ref, dv_ref, h1_ref, gkv_ref, wkv_ref, waout_ref, ws_ref, lng_ref,
             lnb_ref, u_ref, gt_ref, sv_ref, vhat_ref, rstd_ref, rc_ref, rs1_ref, rs2_ref, *rest):
        ready_refs, rest = rest[:nr], rest[nr:]
        (dz_ref, gwo_ref, gwk_ref, dh1f_ref, dgkv_ref, dbkv_ref, dlng_ref, dlnb_ref,
         dws_ref, dbs_ref), rest = rest[:10], rest[10:]
        recv_refs, (dsv_scr, dvln_scr, acco_scr, acck_scr, ssem, rsem, lsem) = rest[:nr], rest[nr:]
        i = pl.program_id(0)
        exchanges = [_Direct(ready_refs[k], recv_refs[k], ssem.at[k], rsem.at[k], lsem.at[k], scatter=True)
                     for k in range(nr)]

        @pl.when(i == 0)
        def _():
            for e in exchanges:
                e.start()
            for r in (dgkv_ref, dbkv_ref, dlng_ref, dlnb_ref, dws_ref, dbs_ref, acco_scr, acck_scr):
                r[...] = jnp.zeros_like(r)

        dk_pre = _rot_bwd(dk_ref[...], rc_ref[...], rs1_ref[...], rs2_ref[...])
        dkv = jnp.concatenate([dk_pre, dv_ref[...]], axis=1)
        dbkv_ref[...] += jnp.sum(dkv, axis=0, keepdims=True)
        dkv_b = dkv.astype(BF)
        h1v = h1_ref[...]
        rkv = lax.rsqrt(jnp.mean(h1v * h1v, axis=-1, keepdims=True) + EPS)
        xh_kv = h1v * rkv
        gkv = gkv_ref[...]
        acck_scr[...] += _dot((xh_kv * gkv).astype(BF).T, dkv_b)
        dnkv = _dot_nt(dkv_b, wkv_ref[...])
        dgkv_ref[...] += jnp.sum(dnkv * xh_kv, axis=0, keepdims=True)
        dh1 = dh1p_ref[...] + _rms_bwd(dnkv, xh_kv, rkv, gkv)
        dh1_b = dh1.astype(BF)
        dh1f_ref[...] = dh1
        dy = _dot_nt(dh1_b, waout_ref[...]).astype(BF)
        silu, dsilu = _silu_parts(gt_ref[...].astype(F32))
        silu, dsilu = silu.astype(BF), dsilu.astype(BF)
        ub, svb = u_ref[...], sv_ref[...]
        us = ub * silu
        dys = dy * svb
        acco_scr[...] += _dot((us * svb).T, dh1_b)
        dz_ref[:, :AW] = dys * silu
        dz_ref[:, 2 * AW:] = dys * ub * dsilu
        dsv_scr[...] = dy * us
        vhat_v = vhat_ref[...].astype(F32)
        lng = lng_ref[...]
        vln_b = (vhat_v * lng + lnb_ref[...]).astype(BF)
        tri = lax.broadcasted_iota(jnp.int32, (CHUNK, CHUNK), 0) >= lax.broadcasted_iota(jnp.int32, (CHUNK, CHUNK), 1)
        lane = lax.broadcasted_iota(jnp.int32, (CHUNK, LANES), 1)
        dbs = jnp.zeros((CHUNK, LANES), F32)
        for g in range(G):
            wsm = jnp.where(tri, ws_ref[g], 0.0).astype(BF)
            cols = slice(g * CHUNK, (g + 1) * CHUNK)
            dws_g = None
            for c in range(nC):
                rows = slice(c * CHUNK, (c + 1) * CHUNK)
                dsv_cg = dsv_scr[rows, cols]
                dvln_scr[rows, cols] = _dot_tn(wsm, dsv_cg)
                part = _dot_nt(dsv_cg, vln_b[rows, cols])
                dws_g = part if dws_g is None else dws_g + part
                dbs += jnp.where(lane == g, jnp.sum(dsv_cg.astype(F32), axis=-1, keepdims=True), 0.0)
            dws_ref[g] += jnp.where(tri, dws_g, 0.0)
        dbs_ref[...] += dbs
        dvln = dvln_scr[...]
        dlng_ref[...] += jnp.sum(dvln * vhat_v, axis=0, keepdims=True)
        dlnb_ref[...] += jnp.sum(dvln, axis=0, keepdims=True)
        a = dvln * lng
        dvv = rstd_ref[:, 0:1] * (a - jnp.mean(a, axis=-1, keepdims=True)
                                  - vhat_v * jnp.mean(a * vhat_v, axis=-1, keepdims=True))
        dz_ref[:, AW:2 * AW] = dvv.astype(BF)

        @pl.when(i == nT - 1)
        def _():
            for j in range(N_DEV):
                gwo_ref[j] = acco_scr[j * (AW // N_DEV):(j + 1) * (AW // N_DEV)].astype(BF)
                gwk_ref[j] = acck_scr[j * (D // N_DEV):(j + 1) * (D // N_DEV)].astype(BF)
            for e in exchanges:
                e.finish()

    row = functools.partial(_row_spec, TM)
    hbm = pl.BlockSpec(memory_space=pl.ANY)
    S = jax.ShapeDtypeStruct
    gwo_shape, gwk_shape = (N_DEV, AW // N_DEV, D), (N_DEV, D // N_DEV, 2 * LANES)
    return pl.pallas_call(
        body, name="a_bwd", grid=(nT,),
        in_specs=[row(D), row(LANES), row(LANES), row(D), _const_spec((1, D)), _const_spec(w_kv.shape),
                  _const_spec(wa_out.shape), _const_spec(ws.shape),
                  _const_spec((1, AW)), _const_spec((1, AW)), pl.BlockSpec((TM, AW), lambda i: (i, 0)),
                  pl.BlockSpec((TM, AW), lambda i: (i, 2)), row(AW), row(AW), row(LANES),
                  row(LANES), row(LANES), row(LANES)] + [hbm] * nr,
        out_specs=[row(3 * AW), _const_spec(gwo_shape), _const_spec(gwk_shape), row(D),
                   _acc_spec((1, D)), _acc_spec((1, 2 * LANES)), _acc_spec((1, AW)),
                   _acc_spec((1, AW)), _acc_spec(ws.shape), _acc_spec((CHUNK, LANES))] + [hbm] * nr,
        out_shape=(S((T, 3 * AW), BF), S(gwo_shape, BF), S(gwk_shape, BF), S((T, D), F32),
                   S((1, D), F32), S((1, 2 * LANES), F32), S((1, AW), F32), S((1, AW), F32),
                   S(ws.shape, F32), S((CHUNK, LANES), F32)) + tuple(S(r.shape, r.dtype) for r in ready),
        scratch_shapes=[pltpu.VMEM((TM, AW), BF), pltpu.VMEM((TM, AW), F32), pltpu.VMEM((AW, D), F32),
                        pltpu.VMEM((D, 2 * LANES), F32)] + _direct_sems(nr),
        compiler_params=_params(("arbitrary",)),
    )(dh1p, dk, dv, h1, g_kv, w_kv, wa_out, ws, ln_g, ln_b, z, z, sv, vhat, rstd, rc, rs1, rs2, *ready)


def _a_in_bwd(dz, wa_in_t, x, dh1, g_a, ready):
    T, D = x.shape
    TM = min(512, T)
    nT = T // TM
    nr = len(ready)

    def body(dz_ref, wain_ref, x_ref, dh1_ref, ga_ref, *rest):
        ready_refs, (dx_ref, n1_ref, dga_ref), rest = rest[:nr], rest[nr:nr + 3], rest[nr + 3:]
        recv_refs, (ssem, rsem, lsem) = rest[:nr], rest[nr:]
        i = pl.program_id(0)
        exchanges = [_Direct(ready_refs[k], recv_refs[k], ssem.at[k], rsem.at[k], lsem.at[k], scatter=True)
                     for k in range(nr)]

        @pl.when(i == 0)
        def _():
            for e in exchanges:
                e.start()
            dga_ref[...] = jnp.zeros_like(dga_ref)

        xv = x_ref[...]
        r1 = lax.rsqrt(jnp.mean(xv * xv, axis=-1, keepdims=True) + EPS)
        xh = xv * r1
        ga = ga_ref[...]
        n1_ref[...] = (xh * ga).astype(BF).T
        dn1 = _dot(dz_ref[...], wain_ref[...])
        dga_ref[...] += jnp.sum(dn1 * xh, axis=0, keepdims=True)
        dx_ref[...] = dh1_ref[...] + _rms_bwd(dn1, xh, r1, ga)

        @pl.when(i == nT - 1)
        def _():
            for e in exchanges:
                e.finish()

    row = functools.partial(_row_spec, TM)
    hbm = pl.BlockSpec(memory_space=pl.ANY)
    S = jax.ShapeDtypeStruct
    return pl.pallas_call(
        body, name="a_in_bwd", grid=(nT,),
        in_specs=[row(dz.shape[1]), _const_spec(wa_in_t.shape), row(D), row(D), _const_spec((1, D))] + [hbm] * nr,
        out_specs=[row(D), _col_spec(TM, D), _acc_spec((1, D))] + [hbm] * nr,
        out_shape=(S((T, D), F32), S((D, T), BF), S((1, D), F32)) + tuple(S(r.shape, r.dtype) for r in ready),
        scratch_shapes=_direct_sems(nr),
        compiler_params=_params(("arbitrary",)),
    )(dz, wa_in_t, x, dh1, g_a, *ready)


def _wgrad(at, b, nblk, name, bt=512):
    K, T = at.shape
    N = b.shape[1] // nblk
    BT = min(bt, T)
    nt = T // BT

    def body(a_ref, b_ref, o_ref, acc):
        t = pl.program_id(0)

        @pl.when(t == 0)
        def _():
            acc[...] = jnp.zeros_like(acc)

        acc[...] += _dot(a_ref[...], b_ref[...])

        @pl.when(t == nt - 1)
        def _():
            for j in range(nblk):
                o_ref[j] = acc[:, j * N:(j + 1) * N].astype(BF)

    return pl.pallas_call(
        body, name=name, grid=(nt,),
        in_specs=[pl.BlockSpec((K, BT), lambda t: (0, t)), pl.BlockSpec((BT, nblk * N), lambda t: (t, 0))],
        out_specs=pl.BlockSpec((nblk, K, N), lambda t: (0, 0, 0)),
        out_shape=jax.ShapeDtypeStruct((nblk, K, N), BF),
        scratch_shapes=[pltpu.VMEM((K, nblk * N), F32)],
        compiler_params=_params(("arbitrary",)),
    )(at, b)


def _wgrad_exchange(a, b, me, small, name):
    K, T = a.shape
    N = b.shape[1] // N_DEV
    BT = T
    nt = T // BT
    last = N_DEV - 1
    n_chip = N_DEV // 2

    def far_of(k, core):
        return jnp.where((core == 0) & ((k == 1) | (k == 2)), k, n_chip - 1 - k)

    def block_of(s, me_i):
        k, odd = s // 2, s % 2
        core = me_i & 1
        return me_i ^ ((far_of(k, jnp.where(odd == 1, core, 1 - core)) << 1) | (1 - odd))

    H = K // 2

    def body(me_ref, a_ref, b_ref, small_ref, recv_ref, full_ref, *scratch):
        (acc, dstage, istage, half, relay, d_s, d_r, i_s, i_r, r_s, r_r, lsem, parts_scr, red_scr, e_s, e_r, e_l, g_s,
         g_r, g_l) = scratch
        s, t = pl.program_id(0), pl.program_id(1)
        x, y, c = (lax.axis_index(ax) for ax in AXES)
        ex = [_Direct(small_ref, parts_scr, e_s, e_r, e_l, scatter=True)]
        regather = _TwoLevel(red_scr, full_ref, g_s, g_r, g_l)

        def to_sibling(k, slot):
            return pltpu.make_async_remote_copy(src_ref=dstage.at[slot], dst_ref=half.at[k], send_sem=d_s.at[k],
                                                recv_sem=d_r.at[k], device_id=(x, y, 1 - c), device_id_type=MESH)

        def to_chip(k, slot):
            over_x = far_of(k, c) == 2
            px, py = jnp.where(over_x, 1 - x, x), jnp.where(over_x, y, 1 - y)
            return pltpu.make_async_remote_copy(src_ref=istage.at[slot], dst_ref=recv_ref.at[jnp.where(over_x, 1, 2)],
                                                send_sem=i_s.at[k], recv_sem=i_r.at[k], device_id=(px, py, c),
                                                device_id_type=MESH)

        def to_relay(j, slot):
            to = (1 - x, y, c) if j == 0 else (x, 1 - y, c)
            return pltpu.make_async_remote_copy(src_ref=istage.at[slot, pl.ds(j * H, H)], dst_ref=relay.at[j],
                                                send_sem=r_s.at[j], recv_sem=r_r.at[j], device_id=to,
                                                device_id_type=MESH)

        @pl.when((s == 0) & (t == 0))
        def _():
            for e in ex:
                e.start()

        acc[...] = _dot(a_ref[...], b_ref[...])

        @pl.when(t == nt - 1)
        def _():
            k = lax.div(s, 2)
            slot = lax.rem(k, 2)

            @pl.when(lax.rem(s, 2) == 0)
            def _():
                @pl.when(k >= 2)
                def _():
                    to_sibling(k - 2, slot).wait_send()

                dstage[slot] = acc[...].astype(BF)
                to_sibling(k, slot).start()

            @pl.when(lax.rem(s, 2) == 1)
            def _():
                to_sibling(k, slot).wait_recv()
                pair = acc[...] + half[k].astype(F32)

                @pl.when(k == 0)
                def _():
                    istage[slot] = pair.astype(BF)
                    for j in range(2):
                        to_relay(j, slot).start()

                @pl.when(k == 1)
                def _():
                    for j in range(2):
                        to_relay(j, slot).wait_recv()

                @pl.when(k == 2)
                def _():
                    for j in range(2):
                        to_relay(j, slot).wait_send()

                @pl.when(k == n_chip - 1)
                def _():
                    to_chip(1, slot).wait_send()
                    istage[slot] = pair.astype(BF)

                @pl.when((k == 1) | (k == 2))
                def _():
                    over_x = far_of(k, c) == 2
                    istage[slot, 0:H] = (pair[:H] + jnp.where(over_x, 0.0, relay[0].astype(F32))).astype(BF)
                    istage[slot, H:K] = (pair[H:] + jnp.where(over_x, relay[1].astype(F32), 0.0)).astype(BF)
                    to_chip(k, slot).start()

            @pl.when(s == last)
            def _():
                own = pltpu.make_async_copy(istage.at[slot], recv_ref.at[0], lsem)
                own.start()
                to_chip(2, 0).wait_send()
                to_sibling(n_chip - 2, 0).wait_send()
                to_sibling(n_chip - 1, 1).wait_send()
                for kk in (1, 2):
                    to_chip(kk, 0).wait_recv()
                own.wait()
                for e in ex:
                    e.finish()
                total = parts_scr[0]
                for dev in range(1, N_DEV):
                    total = total + parts_scr[dev]
                red_scr[...] = total
                regather.start()
                regather.forward()
                regather.finish()

    hbm = pl.BlockSpec(memory_space=pl.ANY)
    dma = pltpu.SemaphoreType.DMA
    grid_spec = pltpu.PrefetchScalarGridSpec(
        num_scalar_prefetch=1, grid=(N_DEV, nt),
        in_specs=[pl.BlockSpec((K, BT), lambda s, t, me_ref: (0, t), pipeline_mode=pl.Buffered(1)),
                  pl.BlockSpec((BT, N), lambda s, t, me_ref: (t, block_of(s, me_ref[0]))), hbm],
        out_specs=[hbm, hbm],
        scratch_shapes=[pltpu.VMEM((K, N), F32), pltpu.VMEM((2, K, N), BF), pltpu.VMEM((2, K, N), BF),
                        pltpu.VMEM((n_chip, K, N), BF), pltpu.VMEM((2, H, N), BF), dma((n_chip,)), dma((n_chip,)),
                        dma((n_chip - 1,)), dma((n_chip - 1,)), dma((2,)), dma((2,)), dma,
                        pltpu.VMEM(small.shape, F32), pltpu.VMEM(small.shape[1:], F32),
                        dma((last,)), dma((last,)), dma, dma((last,)), dma((last,)), dma])
    return pl.pallas_call(
        body, name=name, grid_spec=grid_spec,
        out_shape=[jax.ShapeDtypeStruct((n_chip - 1, K, N), BF), jax.ShapeDtypeStruct(small.shape, F32)],
        compiler_params=_params(("arbitrary", "arbitrary")),
    )(me, a, b, small)


def _my_index():
    return 4 * lax.axis_index("x") + 2 * lax.axis_index("y") + lax.axis_index("c")


def _peer(mask):
    x, y, c = (lax.axis_index(a) for a in AXES)
    return (x ^ ((mask >> 2) & 1), y ^ ((mask >> 1) & 1), c ^ (mask & 1))


def _dev_index(p):
    return 4 * p[0] + 2 * p[1] + p[2]


class _Direct:
    def __init__(self, src, dst, send_sems, recv_sems, local_sem, scatter):
        me = _my_index()
        self.own = pltpu.make_async_copy(src.at[me] if scatter else src, dst.at[me], local_sem)
        self.sends, self.recvs = [], []
        for k in range(1, N_DEV):
            p = _peer(k)
            pi = _dev_index(p)
            sems = dict(send_sem=send_sems.at[k - 1], recv_sem=recv_sems.at[k - 1], device_id=p, device_id_type=MESH)
            self.sends.append(pltpu.make_async_remote_copy(src_ref=src.at[pi] if scatter else src, dst_ref=dst.at[me],
                                                           **sems))
            self.recvs.append(pltpu.make_async_remote_copy(src_ref=src.at[me] if scatter else src, dst_ref=dst.at[pi],
                                                           **sems))

    def start(self):
        self.own.start()
        for cp in self.sends:
            cp.start()

    def finish(self):
        for cp in self.sends:
            cp.wait_send()
        for cp in self.recvs:
            cp.wait_recv()
        self.own.wait()


class _TwoLevel:
    def __init__(self, src, dst, send_sems, recv_sems, local_sem, own=True):
        x, y, c = (lax.axis_index(a) for a in AXES)
        self.me, self.sibling = (x, y, c), (x, y, 1 - c)
        self.chips = [(1 - x, y), (x, 1 - y), (1 - x, 1 - y)]
        self.src, self.dst, self.send_sems, self.recv_sems = src, dst, send_sems, recv_sems
        self.own = pltpu.make_async_copy(src, dst.at[_dev_index(self.me)], local_sem) if own else None

    def _copy(self, k, block, to, from_src=False):
        slot = self.dst.at[_dev_index(block)]
        return pltpu.make_async_remote_copy(src_ref=self.src if from_src else slot, dst_ref=slot,
                                            send_sem=self.send_sems.at[k], recv_sem=self.recv_sems.at[k],
                                            device_id=to, device_id_type=MESH)

    def _firsts(self):
        c = self.me[2]
        return [self._copy(0, self.me, self.sibling, True)] + [self._copy(1 + j, self.me, (*chip, c), True)
                                                               for j, chip in enumerate(self.chips)]

    def _passed(self):
        c = self.me[2]
        return [self._copy(4 + j, (*chip, c), self.sibling) for j, chip in enumerate(self.chips)]

    def start(self):
        if self.own is not None:
            self.own.start()
        for cp in self._firsts():
            cp.start()

    def wait_sibling(self):
        self._copy(0, self.sibling, self.me).wait_recv()

    def wait_chip_and_forward(self, j):
        self._copy(1 + j, (*self.chips[j], self.me[2]), self.me).wait_recv()
        self._passed()[j].start()

    def wait_passed(self, j):
        self._copy(4 + j, (*self.chips[j], 1 - self.me[2]), self.me).wait_recv()

    def wait_sends(self):
        for cp in self._firsts() + self._passed():
            cp.wait_send()
        if self.own is not None:
            self.own.wait()

    def forward(self):
        for j in range(3):
            self.wait_chip_and_forward(j)

    def finish(self):
        self.wait_sibling()
        for j in range(3):
            self.wait_passed(j)
        self.wait_sends()


class _RelayGather:
    def __init__(self, dst, send_sems, recv_sems):
        x, y, c = (lax.axis_index(a) for a in AXES)
        self.c = c
        self.sib, self.xn, self.yn, self.dg = (x, y, 1 - c), (1 - x, y, c), (x, 1 - y, c), (1 - x, 1 - y, c)
        self.me = (x, y, c)
        self.dst, self.send_sems, self.recv_sems = dst, send_sems, recv_sems
        self.half = dst.shape[1] // 2

    def _slot(self, dev, part=None):
        i = _dev_index(dev)
        if part is None:
            return self.dst.at[i]
        return self.dst.at[i, pl.ds(part * self.half, self.half)]

    def _copy(self, k, dev, to, part=None):
        ref = self._slot(dev, part)
        return pltpu.make_async_remote_copy(src_ref=ref, dst_ref=ref, send_sem=self.send_sems.at[k],
                                            recv_sem=self.recv_sems.at[k], device_id=to, device_id_type=MESH)

    def _other(self, dev):
        return (dev[0], dev[1], 1 - self.c)

    def start(self):
        for k, to in enumerate((self.sib, self.xn, self.yn)):
            self._copy(k, self.me, to).start()

    def send_own(self, k):
        return self._copy(k, self.me, (self.sib, self.xn, self.yn)[k])

    def wait_sibling(self):
        self._copy(0, self.sib, self.me).wait_recv()

    def on_x(self):
        self._copy(1, self.xn, self.me).wait_recv()
        self._copy(3, self.xn, self.yn, part=0).start()
        self._copy(5, self.xn, self.sib).start()

    def on_y(self):
        self._copy(2, self.yn, self.me).wait_recv()
        self._copy(4, self.yn, self.xn, part=1).start()
        self._copy(6, self.yn, self.sib).start()

    def on_diag(self):
        self._copy(3, self.dg, self.me, part=0).wait_recv()
        self._copy(4, self.dg, self.me, part=1).wait_recv()
        self._copy(7, self.dg, self.sib).start()

    def wait_passed(self, j):
        self._copy(5 + j, self._other((self.xn, self.yn, self.dg)[j]), self.me).wait_recv()

    def wait_sends(self):
        for k, to in enumerate((self.sib, self.xn, self.yn)):
            self._copy(k, self.me, to).wait_send()
        self._copy(3, self.xn, self.yn, part=0).wait_send()
        self._copy(4, self.yn, self.xn, part=1).wait_send()
        for j, dev in enumerate((self.xn, self.yn, self.dg)):
            self._copy(5 + j, dev, self.sib).wait_send()


def _direct_sems(n):
    if n == 0:
        return []
    return [pltpu.SemaphoreType.DMA((n, 7)), pltpu.SemaphoreType.DMA((n, 7)), pltpu.SemaphoreType.DMA((n,))]


def _adam_math(w, g, m, v):
    m = ADAM_B1 * m + (1.0 - ADAM_B1) * g
    v = ADAM_B2 * v + (1.0 - ADAM_B2) * (g * g)
    m_hat = m / (1.0 - ADAM_B1 ** ADAM_STEP)
    v_hat = v / (1.0 - ADAM_B2 ** ADAM_STEP)
    delta = -ADAM_LR * (m_hat / (jnp.sqrt(v_hat) + ADAM_EPS) + ADAM_WD * w)
    return delta, m, v


def _sum_adam(parts, w, m, v, name):
    R, C = w.shape
    NP = parts.shape[0]
    BR = 4 * CHUNK if R % (4 * CHUNK) == 0 else R

    def body(p_ref, w_ref, m_ref, v_ref, g_ref, d_ref, nm_ref, nv_ref):
        g = p_ref[0].astype(F32)
        for i in range(1, NP):
            g = g + p_ref[i].astype(F32)
        g_ref[...] = g
        d_ref[...], nm_ref[...], nv_ref[...] = _adam_math(w_ref[...], g, m_ref[...], v_ref[...])

    blk = pl.BlockSpec((BR, C), lambda i: (i, 0))
    S = jax.ShapeDtypeStruct((R, C), F32)
    return pl.pallas_call(
        body, name=name, grid=(R // BR,),
        in_specs=[pl.BlockSpec((NP, BR, C), lambda i: (0, i, 0)), blk, blk, blk],
        out_specs=[blk] * 4, out_shape=(S,) * 4,
        compiler_params=_params(("arbitrary",)),
    )(parts, w, m, v)


SUBLANES = 8


def _nrows(size):
    return -(-size // (SUBLANES * LANES)) * SUBLANES


def _view2d(a):
    return a.reshape(-1, LANES) if a.size % LANES == 0 else a.reshape(1, -1)


def _pack_small(parts, total_rows, name):
    arrs = [p[0] for p in parts]

    def body(*refs):
        out = refs[-1]
        out[...] = jnp.zeros_like(out)
        at = 0
        for ref, (a, rows, flag) in zip(refs[:-1], parts):
            val = ref[...].T if flag == "T" else ref[...]
            r, c = (rows, val.shape[1]) if flag == "T" else val.shape
            out[at:at + r, 0:c] = val[:r]
            at += _nrows(r * c)

    return pl.pallas_call(body, name=name, out_shape=jax.ShapeDtypeStruct((total_rows, LANES), F32))(*arrs)


def _small_update(full, me, reps, shards, name):
    n = len(reps) + len(shards)

    def body(me_ref, full_ref, *refs):
        ins, outs = refs[:3 * n], refs[3 * n:]
        at = 0
        for k in range(n):
            w_ref, m_ref, v_ref = ins[3 * k:3 * k + 3]
            r, c = w_ref.shape
            if k < len(reps):
                g = full_ref[at:at + r, 0:c]
                at += _nrows(r * c)
            else:
                seg = full_ref[at:at + N_DEV * r, :]
                row = lax.broadcasted_iota(jnp.int32, seg.shape, 0)
                pick = [jnp.sum(jnp.where(row == r * me_ref[0] + t, seg, 0.0), axis=0, keepdims=True) for t in range(r)]
                g = pick[0] if r == 1 else jnp.concatenate(pick, axis=0)
                at += N_DEV * r
            g_ref, d_ref, nm_ref, nv_ref = outs[4 * k:4 * k + 4]
            g_ref[...] = g
            d_ref[...], nm_ref[...], nv_ref[...] = _adam_math(w_ref[...], g, m_ref[...], v_ref[...])
        outs[4 * n][...] = full_ref[at:at + 1, 0:1]

    flat = [t for p in reps + shards for t in p]
    S = jax.ShapeDtypeStruct
    res = pl.pallas_call(
        body, name=name,
        in_specs=[pl.BlockSpec(memory_space=pltpu.SMEM)] + [pl.BlockSpec(memory_space=pltpu.VMEM)] * (1 + len(flat)),
        out_shape=[S(p[0].shape, F32) for p in reps + shards for _ in range(4)] + [S((1, 1), F32)],
    )(me, full, *flat)
    return [tuple(res[4 * k:4 * k + 4]) for k in range(n)], res[4 * n]


def _rope_tables(T):
    pos = np.arange(T, dtype=np.float32)
    inv_freq = (np.float64(ROPE_THETA) ** (-np.arange(0, HEAD_DIM, 2, dtype=np.float64) / HEAD_DIM)).astype(np.float32)
    ang = (pos[:, None] * inv_freq[None, :]).astype(np.float64)
    cos, sin, zero = np.cos(ang).astype(np.float32), np.sin(ang).astype(np.float32), np.zeros(ang.shape, np.float32)
    c = np.concatenate([cos, cos, cos, cos], axis=1)
    s1 = np.concatenate([-sin, zero, -sin, zero], axis=1)
    s2 = np.concatenate([zero, sin, zero, sin], axis=1)
    return jnp.asarray(c), jnp.asarray(s1), jnp.asarray(s2)


def kernel(x, a_norm_g, a_w_in, a_ln_g, a_ln_b, a_ws, a_bs, a_w_out, kv_norm_g, w_kv, b_kv, b_norm_g, b_w_in, b_bq, b_sinks, b_w_out, final_norm_g, loss_target, m_a_norm_g, m_a_w_in, m_a_ln_g, m_a_ln_b, m_a_ws, m_a_bs, m_a_w_out, m_kv_norm_g, m_w_kv, m_b_kv, m_b_norm_g, m_b_w_in, m_b_bq, m_b_sinks, m_b_w_out, m_final_norm_g, v_a_norm_g, v_a_w_in, v_a_ln_g, v_a_ln_b, v_a_ws, v_a_bs, v_a_w_out, v_kv_norm_g, v_w_kv, v_b_kv, v_b_norm_g, v_b_w_in, v_b_bq, v_b_sinks, v_b_w_out, v_final_norm_g):
    T, D = x.shape[1], x.shape[2]
    AW = a_ln_g.shape[1] * N_DEV
    G = a_ws.shape[1]
    assert w_kv.shape[1] == 2 * LANES and a_ws.shape[2] == CHUNK and T % CHUNK == 0
    me = _my_index()

    xs, tgt = x[0], loss_target[0]
    vec = jnp.concatenate([a_norm_g, a_ln_g, a_ln_b], axis=1)
    vec = jnp.broadcast_to(vec, (SUBLANES, vec.shape[1]))
    north = lax.axis_index("c") == 1
    slots = me ^ jnp.where(north, jnp.array(PASS_MASKS[1], jnp.int32), jnp.array(PASS_MASKS[0], jnp.int32))
    z, wa_in_t, vecs, wa_out, wkv = _in_proj(xs, a_w_in[0], vec, slots, [a_w_out[0], w_kv])
    wa_in_t = wa_in_t.reshape(-1, D)
    wa_out = wa_out.reshape(AW, D)
    wkv = wkv.reshape(D, 2 * LANES)
    vecs = vecs[:, 0, :]
    ds = D // N_DEV
    g_a = vecs[:, :ds].reshape(1, D)
    ln_g = vecs[:, ds:ds + AW // N_DEV].reshape(1, AW)
    ln_b = vecs[:, ds + AW // N_DEV:].reshape(1, AW)

    rc, rs1, rs2 = _rope_tables(T)
    ws = a_ws[0]
    bs_t = a_bs[0].T
    g_kv = kv_norm_g.reshape(1, D)
    bkv = b_kv.reshape(1, -1)
    g_f = final_norm_g.reshape(1, D)
    sinks = jnp.repeat(b_sinks.reshape(2, 4, 2).transpose(0, 2, 1).reshape(4, 4), CHUNK, axis=1)
    h1, sv, vhat, rstd, k4, v4, kt, vt, wb_in, wb_out = _a_fwd(
        xs, z, ln_g, ln_b, ws, bs_t, wa_out, g_kv, wkv, bkv, rc, rs1, rs2, [b_w_in[0], b_w_out[0]])
    wb_out = wb_out.reshape(-1, D)
    q, g2, o, dh2, dh2_b, loss, d_gf = _b_fwd(h1, b_norm_g, wb_in, b_bq, rc, rs1, rs2, k4, vt, sinks, wb_out, g_f, tgt)
    dh1p, dz2, n2, y2, dk, dv, d_bq, d_gb, d_sink = _b_bwd(dh2, h1, q, g2, o, k4, v4, kt, sinks, wb_out, wb_in,
                                                           b_norm_g, rc, rs1, rs2)
    d_sink = d_sink[:, :4].reshape(2, 2, 4).transpose(0, 2, 1).reshape(1, 16)
    gw_b_in = _wgrad(n2, dz2, N_DEV, "wgrad_b_in", bt=1024)
    gw_b_out = _wgrad(y2, dh2_b, 1, "wgrad_b_out", bt=1024).reshape(N_DEV, -1, D)
    (dz, gw_a_out, gw_kv, dh1_f, d_gkv, d_bkv, d_lng, d_lnb, d_ws, d_bst, r_b_in, r_b_out) = _a_bwd(
        dh1p, dk, dv, h1, g_kv, wkv, wa_out, ws, ln_g, ln_b, z, sv, vhat, rstd, rc, rs1, rs2, [gw_b_in, gw_b_out])
    dx, n1, d_ga, r_a_out, r_kv = _a_in_bwd(dz, wa_in_t, xs, dh1_f, g_a, [gw_a_out, gw_kv])
    small = [(_view2d(d_ws), None, None), (d_bst, G, "T")] + [(_view2d(a), None, None) for a in (
        d_gkv, d_bkv, d_gb, d_bq, d_sink, d_gf, d_ga, d_lng, d_lnb, loss)]
    used = sum(_nrows(G * CHUNK if flag else a.size) for a, _, flag in small)
    per = -(-used // (SUBLANES * N_DEV)) * SUBLANES
    small_pack = _pack_small(small, per * N_DEV, "pack_small").reshape(N_DEV, per, LANES)
    r_a_in, full_small = _wgrad_exchange(n1, dz, me.reshape(1), small_pack, "wgrad_a_in")

    g_a_in, d_a_in, nm_a_in, nv_a_in = _sum_adam(r_a_in, a_w_in[0], m_a_w_in[0], v_a_w_in[0], "adam_a_in")
    g_a_out, d_a_out, nm_a_out, nv_a_out = _sum_adam(r_a_out, a_w_out[0], m_a_w_out[0], v_a_w_out[0], "adam_a_out")
    g_kvw, d_kvw, nm_kvw, nv_kvw = _sum_adam(r_kv, w_kv, m_w_kv, v_w_kv, "adam_kv")
    g_b_in, d_b_in, nm_b_in, nv_b_in = _sum_adam(r_b_in, b_w_in[0], m_b_w_in[0], v_b_w_in[0], "adam_b_in")
    g_b_out, d_b_out, nm_b_out, nv_b_out = _sum_adam(r_b_out, b_w_out[0], m_b_w_out[0], v_b_w_out[0], "adam_b_out")

    full_small = full_small.reshape(N_DEV * per, LANES)
    reps = [(a_ws, m_a_ws, v_a_ws), (a_bs, m_a_bs, v_a_bs), (kv_norm_g, m_kv_norm_g, v_kv_norm_g),
            (b_kv, m_b_kv, v_b_kv), (b_norm_g, m_b_norm_g, v_b_norm_g), (b_bq, m_b_bq, v_b_bq),
            (b_sinks, m_b_sinks, v_b_sinks), (final_norm_g, m_final_norm_g, v_final_norm_g)]
    shards = [(a_norm_g, m_a_norm_g, v_a_norm_g), (a_ln_g, m_a_ln_g, v_a_ln_g), (a_ln_b, m_a_ln_b, v_a_ln_b)]
    upd, loss = _small_update(full_small, me.reshape(1), [tuple(_view2d(t) for t in p) for p in reps],
                              [tuple(_view2d(t) for t in p) for p in shards], "adam_small")
    loss = loss[0, 0]
    sm_g, sd, snm, snv = ([upd[k][j].reshape(p[0].shape) for k, p in enumerate(reps + shards)] for j in range(4))

    def order(big, sm):
        a_in, a_out, kvw, b_in, b_out = big
        ws_, bs_, kvg, bkv_, bng, bq_, snk, fng, ang, alng, alnb = sm
        return (ang, a_in[None], alng, alnb, ws_, bs_, a_out[None], kvg, kvw, bkv_, bng, b_in[None], bq_, snk,
                b_out[None], fng)

    grads = order((g_a_in, g_a_out, g_kvw, g_b_in, g_b_out), sm_g)
    deltas = order((d_a_in, d_a_out, d_kvw, d_b_in, d_b_out), sd)
    new_m = order((nm_a_in, nm_a_out, nm_kvw, nm_b_in, nm_b_out), snm)
    new_v = order((nv_a_in, nv_a_out, nv_kvw, nv_b_in, nv_b_out), snv)
    return (loss, dx[None], *grads, *deltas, *new_m, *new_v)
```

```python
import functools

import jax
import jax.numpy as jnp
import numpy as np
from jax import lax
from jax.experimental import pallas as pl
from jax.experimental.pallas import tpu as pltpu

CHUNK = 128
HEAD_DIM = 64
ROPE_THETA = 10000.0
EPS = 1e-5
ADAM_LR = 0.001
ADAM_B1 = 0.9
ADAM_B2 = 0.999
ADAM_EPS = 1e-08
ADAM_WD = 0.01
ADAM_STEP = 10
N_DEV = 8
LANES = 128
NEG = -1e30

BF = jnp.bfloat16
F32 = jnp.float32
MESH = pl.DeviceIdType.MESH
AXES = ("x", "y", "c")
VMEM_LIMIT = 56 * 1024 * 1024


def _dot(a, b):
    return jnp.dot(a, b, preferred_element_type=F32)


def _dot_nt(a, b):
    return lax.dot_general(a, b, (((1,), (1,)), ((), ())), preferred_element_type=F32)


def _dot_tn(a, b):
    return lax.dot_general(a, b, (((0,), (0,)), ((), ())), preferred_element_type=F32)


def _const_spec(shape):
    nd = len(shape)
    return pl.BlockSpec(shape, lambda *_: (0,) * nd, pipeline_mode=pl.Buffered(1))


def _acc_spec(shape):
    nd = len(shape)
    return pl.BlockSpec(shape, lambda *_: (0,) * nd)


def _row_spec(tm, width):
    return pl.BlockSpec((tm, width), lambda i: (i, 0))


def _col_spec(tm, height):
    return pl.BlockSpec((height, tm), lambda i: (0, i))


def _params(sem):
    return pltpu.CompilerParams(dimension_semantics=sem, vmem_limit_bytes=VMEM_LIMIT)


def _rot(x, c, s1, s2):
    return x * c + pltpu.roll(x, 96, 1) * s1 + pltpu.roll(x, 32, 1) * s2


def _rot_bwd(d, c, s1, s2):
    return d * c + pltpu.roll(d * s1, 32, 1) + pltpu.roll(d * s2, 96, 1)


def _silu_parts(g):
    sg = jax.nn.sigmoid(g)
    return g * sg, sg * (1.0 + g * (1.0 - sg))


def _rms_bwd(dn, xh, r, g):
    a = dn * g
    return r * (a - xh * jnp.mean(a * xh, axis=-1, keepdims=True))


def _lane_lo(shape):
    return lax.broadcasted_iota(jnp.int32, shape, 1) < HEAD_DIM


def _split4(t):
    lo = _lane_lo(t.shape)
    tr = pltpu.roll(t, HEAD_DIM, 1)
    z = jnp.zeros_like(t)
    return jnp.concatenate([jnp.where(lo, t, z), jnp.where(lo, z, tr), jnp.where(lo, tr, z), jnp.where(lo, z, t)], axis=1)


def _stack_pairs(t, h):
    return jnp.concatenate([t[:, (h * 4 + j) * LANES:(h * 4 + j + 1) * LANES] for j in range(4)], axis=0)


def _upper():
    shape = (CHUNK, 4 * CHUNK)
    return lax.broadcasted_iota(jnp.int32, shape, 0) > (lax.broadcasted_iota(jnp.int32, shape, 1) & (CHUNK - 1))


def _band_rows(tile_ref, before_ref, c, h):
    a = slice(2 * h * LANES, (2 * h + 1) * LANES)
    b = slice((2 * h + 1) * LANES, (2 * h + 2) * LANES)
    cur = slice(c * CHUNK, (c + 1) * CHUNK)

    def prev(cols):
        return before_ref[:, cols] if c == 0 else tile_ref[(c - 1) * CHUNK:c * CHUNK, cols]

    return jnp.concatenate([prev(a), tile_ref[cur, a], prev(b), tile_ref[cur, b]], axis=0)


def _band_cols(tile_ref, before_ref, c, h):
    a = slice(2 * h * LANES, (2 * h + 1) * LANES)
    b = slice((2 * h + 1) * LANES, (2 * h + 2) * LANES)

    def prev(rows):
        return before_ref[0, rows, :] if c == 0 else tile_ref[c - 1, rows, :]

    return jnp.concatenate([prev(a), tile_ref[c, a, :], prev(b), tile_ref[c, b, :]], axis=1)


def _band_specs(tm):
    nc = tm // CHUNK

    def before(i):
        return jnp.maximum(i * nc - 1, 0)

    return (pl.BlockSpec((tm, 4 * LANES), lambda i: (i, 0)),
            pl.BlockSpec((CHUNK, 4 * LANES), lambda i: (before(i), 0)),
            pl.BlockSpec((nc, 4 * LANES, CHUNK), lambda i: (i, 0, 0)),
            pl.BlockSpec((1, 4 * LANES, CHUNK), lambda i: (before(i), 0, 0)))


def _fold(t, upper, has_prev=None):
    out = []
    for k in range(2):
        prev = t[2 * k * CHUNK:(2 * k + 1) * CHUNK]
        if has_prev is not None:
            prev = jnp.where(has_prev, prev, NEG)
        out.append(jnp.where(upper, prev, t[(2 * k + 1) * CHUNK:(2 * k + 2) * CHUNK]))
    return out


def _unfold(fa, fb, upper):
    z = jnp.zeros_like(fa)
    return jnp.concatenate([jnp.where(upper, fa, z), jnp.where(upper, z, fa),
                            jnp.where(upper, fb, z), jnp.where(upper, z, fb)], axis=0)


def _softmax_sink(f, sink):
    m = jnp.maximum(jnp.max(f, axis=0, keepdims=True), sink)
    p = jnp.exp(f - m)
    es = jnp.exp(sink - m)
    inv = 1.0 / (jnp.sum(p, axis=0, keepdims=True) + es)
    return p * inv, es * inv


class _Riding:
    def __init__(self, shards, gathered, stages, sems, n_steps):
        self.shards, self.stages, self.n_steps = shards, stages, n_steps
        ssem, rsem, lsem = sems
        self.gathers = [_TwoLevel(stages[k], gathered[k], ssem.at[k], rsem.at[k], lsem.at[k])
                        for k in range(len(shards))]

    def begin(self, i):
        @pl.when(i == 0)
        def _():
            for shard, stage, g in zip(self.shards, self.stages, self.gathers):
                stage[...] = shard[...].astype(stage.dtype)
                g.start()

    def end(self, i):
        @pl.when(i == self.n_steps // 2)
        def _():
            for g in self.gathers:
                g.forward()

        @pl.when(i == self.n_steps - 1)
        def _():
            for g in self.gathers:
                g.finish()

    @staticmethod
    def specs(later):
        nl = len(later)
        hbm = pl.BlockSpec(memory_space=pl.ANY)
        return ([_const_spec(w.shape) for w in later], [hbm] * nl,
                tuple(jax.ShapeDtypeStruct((N_DEV,) + w.shape, BF) for w in later),
                [pltpu.VMEM(w.shape, BF) for w in later] + _direct_sems(nl))


PASS_MASKS = ((0, 1, 2, 5, 4, 3, 6, 7), (0, 1, 4, 3, 2, 5, 6, 7))


def _in_proj(x, w_shard, vec_shards, me, later):
    T, D = x.shape
    SH = w_shard.shape[1]
    TM = min(1024, T)
    nT = T // TM
    nl = len(later)
    nv = len(vec_shards)
    widths = [v.shape[1] for v in vec_shards]
    offsets = [sum(widths[:k]) for k in range(nv)]
    vec_shape = (SUBLANES, sum(widths))
    ds = widths[0]
    last = N_DEV - 1
    masks = jnp.asarray(np.array(PASS_MASKS, np.int32).reshape(-1))

    def slot(p, me_ref, masks_ref):
        return me_ref[0] ^ masks_ref[(me_ref[0] & 1) * N_DEV + p]

    def body(me_ref, masks_ref, x_ref, wsh_ref, *rest):
        vsh_refs, rest = rest[:nv], rest[nv:]
        shards, rest = rest[:nl], rest[nl:]
        (z_ref, wt_ref), rest = rest[:2], rest[2:]
        vout_refs, rest = rest[:nv], rest[nv:]
        gathered, rest = rest[:nl], rest[nl:]
        (w_scr, vec_scr, vstage, n1_scr, ga_scr, w_s, w_r, v_s, v_r, v_l), rest = rest[:10], rest[10:]
        stages, sems = rest[:nl], rest[nl:]
        p, i = pl.program_id(0), pl.program_id(1)
        me = _my_index()
        wg = _RelayGather(w_scr, w_s, w_r)
        vg = _Direct(vstage, vec_scr, v_s, v_r, v_l, scatter=False)
        lg = [_TwoLevel(stages[k], gathered[k], sems[0].at[k], sems[1].at[k], sems[2].at[k]) for k in range(nl)]

        def at_pass(k):
            return (p == k) & (i == 0)

        c = lax.axis_index("c")

        @pl.when(at_pass(0))
        def _():
            for ref, off, wd in zip(vsh_refs, offsets, widths):
                vstage[:, off:off + wd] = jnp.broadcast_to(ref[...], (SUBLANES, wd))
            vg.start()
            w_scr[me] = wsh_ref[...].astype(BF)
            wg.send_own(0).start()

            @pl.when(c == 1)
            def _():
                wg.send_own(1).start()

            @pl.when(c == 0)
            def _():
                wg.send_own(2).start()

            vg.finish()
            for j in range(N_DEV):
                ga_scr[:, j * ds:(j + 1) * ds] = vec_scr[j, 0:1, 0:ds]
                for ref, off, wd in zip(vout_refs, offsets, widths):
                    ref[:, j * wd:(j + 1) * wd] = vec_scr[j, 0:1, off:off + wd]

        @pl.when(at_pass(1))
        def _():
            wg.wait_sibling()

        for first, second, landed_first, landed_second in ((1, 2, wg.on_x, wg.on_y), (2, 1, wg.on_y, wg.on_x)):
            mine = c == (1 if first == 1 else 0)

            @pl.when(at_pass(2) & mine)
            def _(second=second, landed_first=landed_first):
                wg.send_own(second).start()
                landed_first()

            @pl.when(at_pass(3) & mine)
            def _(second=second):
                wg.wait_passed(second - 1)

            @pl.when(at_pass(4) & mine)
            def _(landed_second=landed_second):
                landed_second()

            @pl.when(at_pass(5) & mine)
            def _(first=first):
                wg.wait_passed(first - 1)

        @pl.when(at_pass(4))
        def _():
            for k in range(nl):
                stages[k][...] = shards[k][...].astype(BF)
                lg[k].start()

        @pl.when(at_pass(6))
        def _():
            wg.on_diag()

        @pl.when(at_pass(7))
        def _():
            wg.wait_passed(2)

        @pl.when(p == 0)
        def _():
            xv = x_ref[...]
            r1 = lax.rsqrt(jnp.mean(xv * xv, axis=-1, keepdims=True) + EPS)
            n1_scr[i] = (xv * r1 * ga_scr[...]).astype(BF)

        z_ref[...] = _dot(n1_scr[i], w_scr[slot(p, me_ref, masks_ref)]).astype(BF)

        @pl.when(i == 0)
        def _():
            wt_ref[0] = w_scr[slot(p, me_ref, masks_ref)].T

        @pl.when((p == last) & (i == nT - 1))
        def _():
            wg.wait_sends()
            for g in lg:
                g.forward()
            for g in lg:
                g.finish()

    hbm = pl.BlockSpec(memory_space=pl.ANY)
    dma = pltpu.SemaphoreType.DMA
    S = jax.ShapeDtypeStruct
    def whole(shape):
        return pl.BlockSpec(shape, lambda p, i, m, t: (0, 0))

    def once(shape):
        return pl.BlockSpec(shape, lambda p, i, m, t: (0, 0), pipeline_mode=pl.Buffered(1))

    grid_spec = pltpu.PrefetchScalarGridSpec(
        num_scalar_prefetch=2, grid=(N_DEV, nT),
        in_specs=[pl.BlockSpec((TM, D), lambda p, i, m, t: (jnp.where(p == 0, i, nT - 1), 0)), once(w_shard.shape)]
        + [once(v.shape) for v in vec_shards] + [once(w.shape) for w in later],
        out_specs=[pl.BlockSpec((TM, SH), lambda p, i, m, t: (i, slot(p, m, t))),
                   pl.BlockSpec((1, SH, D), lambda p, i, m, t: (slot(p, m, t), 0, 0))]
        + [whole((1, N_DEV * wd)) for wd in widths] + [hbm] * nl,
        scratch_shapes=[pltpu.VMEM((N_DEV, D, SH), BF), pltpu.VMEM((N_DEV,) + vec_shape, F32),
                        pltpu.VMEM(vec_shape, F32), pltpu.VMEM((nT, TM, D), BF), pltpu.VMEM((1, D), F32),
                        dma((8,)), dma((8,)), dma((7,)), dma((7,)), dma]
        + [pltpu.VMEM(w.shape, BF) for w in later] + _direct_sems(nl))
    return pl.pallas_call(
        body, name="a_in_proj", grid_spec=grid_spec,
        out_shape=(S((T, N_DEV * SH), BF), S((N_DEV, SH, D), BF)) + tuple(S((1, N_DEV * wd), F32) for wd in widths)
        + tuple(S((N_DEV,) + w.shape, BF) for w in later),
        compiler_params=_params(("arbitrary", "arbitrary")),
    )(me, masks, x, w_shard, *vec_shards, *later)


def _a_fwd(x, z, ln_g, ln_b, ws, bs_t, wa_out, g_kv, w_kv, b_kv, rc, rs1, rs2, later):
    T, D = x.shape
    AW = wa_out.shape[0]
    G = ws.shape[0]
    TM = min(256, T)
    nT = T // TM
    nC = TM // CHUNK
    nl = len(later)

    def body(x_ref, u_ref, v_ref, gt_ref, lng_ref, lnb_ref, ws_ref, bst_ref, waout_ref, gkv_ref, wkv_ref, bkv_ref,
             rc_ref, rs1_ref, rs2_ref, *rest):
        shards, rest = rest[:nl], rest[nl:]
        (h1_ref, sv_ref, vhat_ref, rstd_ref, k4_ref, v4_ref, kt_ref, vt_ref), rest = rest[:8], rest[8:]
        gathered, sv_scr, stages, sems = rest[:nl], rest[nl], rest[nl + 1:2 * nl + 1], rest[2 * nl + 1:]
        i = pl.program_id(0)
        riding = _Riding(shards, gathered, stages, sems, nT)
        riding.begin(i)
        xv = x_ref[...]
        u = u_ref[...].astype(F32)
        v = v_ref[...].astype(F32)
        gt = gt_ref[...].astype(F32)
        mu = jnp.mean(v, axis=-1, keepdims=True)
        xc = v - mu
        rstd = lax.rsqrt(jnp.mean(xc * xc, axis=-1, keepdims=True) + EPS)
        vhat = xc * rstd
        vln = (vhat * lng_ref[...] + lnb_ref[...]).astype(BF)
        tri = lax.broadcasted_iota(jnp.int32, (CHUNK, CHUNK), 0) >= lax.broadcasted_iota(jnp.int32, (CHUNK, CHUNK), 1)
        for g in range(G):
            wsm = jnp.where(tri, ws_ref[g], 0.0).astype(BF)
            bias = bst_ref[:, g:g + 1]
            for c in range(nC):
                blk = vln[c * CHUNK:(c + 1) * CHUNK, g * CHUNK:(g + 1) * CHUNK]
                sv_scr[c * CHUNK:(c + 1) * CHUNK, g * CHUNK:(g + 1) * CHUNK] = _dot(wsm, blk) + bias
        sv = sv_scr[...]
        silu, _ = _silu_parts(gt)
        y = (u * sv * silu).astype(BF)
        h1 = xv + _dot(y, waout_ref[...])
        h1_ref[...] = h1
        sv_ref[...] = sv.astype(BF)
        vhat_ref[...] = vhat.astype(BF)
        rstd_ref[...] = jnp.broadcast_to(rstd, rstd_ref.shape)
        rkv = lax.rsqrt(jnp.mean(h1 * h1, axis=-1, keepdims=True) + EPS)
        nkv = (h1 * rkv * gkv_ref[...]).astype(BF)
        kv = _dot(nkv, wkv_ref[...]) + bkv_ref[...]
        k_rot = _rot(kv[:, :LANES], rc_ref[...], rs1_ref[...], rs2_ref[...])
        for src, ref, tref in ((k_rot, k4_ref, kt_ref), (kv[:, LANES:], v4_ref, vt_ref)):
            t4 = _split4(src)
            ref[...] = t4.astype(BF)
            for c in range(nC):
                for b in range(4):
                    blk = t4[c * CHUNK:(c + 1) * CHUNK, b * LANES:(b + 1) * LANES]
                    tref[c, b * LANES:(b + 1) * LANES, :] = blk.T.astype(BF)
        riding.end(i)

    row = functools.partial(_row_spec, TM)
    zcol = [pl.BlockSpec((TM, AW), functools.partial(lambda k, i: (i, k), k)) for k in range(3)]
    tr = pl.BlockSpec((nC, 4 * LANES, CHUNK), lambda i: (i, 0, 0))
    r_in, r_out, r_shape, r_scratch = _Riding.specs(later)
    S = jax.ShapeDtypeStruct
    return pl.pallas_call(
        body, name="a_fwd", grid=(nT,),
        in_specs=[row(D)] + zcol + [_const_spec((1, AW)), _const_spec((1, AW)),
                  _const_spec(ws.shape), _const_spec(bs_t.shape), _const_spec(wa_out.shape), _const_spec((1, D)),
                  _const_spec(w_kv.shape), _const_spec((1, 2 * LANES)), row(LANES), row(LANES), row(LANES)] + r_in,
        out_specs=[row(D), row(AW), row(AW), row(LANES), row(4 * LANES), row(4 * LANES), tr, tr] + r_out,
        out_shape=(S((T, D), F32), S((T, AW), BF), S((T, AW), BF), S((T, LANES), F32),
                   S((T, 4 * LANES), BF), S((T, 4 * LANES), BF),
                   S((T // CHUNK, 4 * LANES, CHUNK), BF), S((T // CHUNK, 4 * LANES, CHUNK), BF)) + r_shape,
        scratch_shapes=[pltpu.VMEM((TM, AW), F32)] + r_scratch,
        compiler_params=_params(("arbitrary",)),
    )(x, z, z, z, ln_g, ln_b, ws, bs_t, wa_out, g_kv, w_kv, b_kv, rc, rs1, rs2, *later)


def _b_fwd(h1, g_b, wb_in, bq, rc, rs1, rs2, k4, vt, sinks, wb_out, g_f, target):
    T, D = h1.shape
    BW = wb_out.shape[0]
    SH = wb_in.shape[2]
    TM = min(512, T)
    nC = TM // CHUNK
    nP = BW // LANES

    def body(h1_ref, gb_ref, wbin_ref, bq_ref, rc_ref, rs1_ref, rs2_ref, k4_ref, k4p_ref, vt_ref, vtp_ref, sink_ref,
             wbout_ref, gf_ref, tgt_ref, q_ref, g2_ref, o_ref, dh2_ref, dh2b_ref, loss_ref, dgf_ref, z_scr, o_scr):
        i = pl.program_id(0)

        @pl.when(i == 0)
        def _():
            loss_ref[...] = jnp.zeros_like(loss_ref)
            dgf_ref[...] = jnp.zeros_like(dgf_ref)

        h1v = h1_ref[...]
        r2 = lax.rsqrt(jnp.mean(h1v * h1v, axis=-1, keepdims=True) + EPS)
        n2 = (h1v * r2 * gb_ref[...]).astype(BF)
        for j in range(N_DEV):
            z_scr[:, j * SH:(j + 1) * SH] = _dot(n2, wbin_ref[j])
        c_t, s1_t, s2_t = rc_ref[...], rs1_ref[...], rs2_ref[...]
        for p in range(nP):
            cols = slice(p * LANES, (p + 1) * LANES)
            qp = _rot(z_scr[:, cols] + bq_ref[:, cols], c_t, s1_t, s2_t) * (HEAD_DIM ** -0.5)
            q_ref[:, cols] = qp.astype(BF)
        g2 = z_scr[:, BW:]
        g2_ref[...] = g2.astype(BF)
        upper = _upper()
        for c in range(nC):
            ci = i * nC + c
            rows = slice(c * CHUNK, (c + 1) * CHUNK)
            qc = q_ref[rows, :]
            for h in range(2):
                st = _dot_nt(_band_rows(k4_ref, k4p_ref, c, h), _stack_pairs(qc, h))
                fa, fb = _fold(st, upper, ci > 0)
                pa, _ = _softmax_sink(fa, sink_ref[2 * h:2 * h + 1, :])
                pb, _ = _softmax_sink(fb, sink_ref[2 * h + 1:2 * h + 2, :])
                ot = _dot(_band_cols(vt_ref, vtp_ref, c, h), _unfold(pa, pb, upper).astype(BF))
                for j in range(4):
                    o_scr[rows, (h * 4 + j) * LANES:(h * 4 + j + 1) * LANES] = ot[:, j * CHUNK:(j + 1) * CHUNK].T
        o = o_scr[...]
        o_ref[...] = o.astype(BF)
        silu, _ = _silu_parts(g2)
        h2 = h1v + _dot((o * silu).astype(BF), wbout_ref[...])
        rf = lax.rsqrt(jnp.mean(h2 * h2, axis=-1, keepdims=True) + EPS)
        xh = h2 * rf
        gf = gf_ref[...]
        err = xh * gf - tgt_ref[...]
        dyf = err * (1.0 / D)
        dh2 = _rms_bwd(dyf, xh, rf, gf)
        dh2_ref[...] = dh2
        dh2b_ref[...] = dh2.astype(BF)
        loss_ref[...] += 0.5 * jnp.sum(jnp.mean(err * err, axis=-1, keepdims=True), axis=0, keepdims=True)
        dgf_ref[...] += jnp.sum(dyf * xh, axis=0, keepdims=True)

    row = functools.partial(_row_spec, TM)
    rows_tile, rows_before, cols_tile, cols_before = _band_specs(TM)
    S = jax.ShapeDtypeStruct
    return pl.pallas_call(
        body, name="b_fwd", grid=(T // TM,),
        in_specs=[row(D), _const_spec((1, D)), _const_spec(wb_in.shape), _const_spec((1, BW)), row(LANES), row(LANES),
                  row(LANES), rows_tile, rows_before, cols_tile, cols_before, _const_spec(sinks.shape),
                  _const_spec(wb_out.shape), _const_spec((1, D)), row(D)],
        out_specs=[row(BW), row(BW), row(BW), row(D), row(D), _acc_spec((1, 1)), _acc_spec((1, D))],
        out_shape=(S((T, BW), BF), S((T, BW), BF), S((T, BW), BF), S((T, D), F32), S((T, D), BF), S((1, 1), F32),
                   S((1, D), F32)),
        scratch_shapes=[pltpu.VMEM((TM, 2 * BW), F32), pltpu.VMEM((TM, BW), F32)],
        compiler_params=_params(("arbitrary",)),
    )(h1, g_b, wb_in, bq, rc, rs1, rs2, k4, k4, vt, vt, sinks, wb_out, g_f, target)


def _b_bwd(dh2, h1, q, g2, o, k4, v4, kt, sinks, wb_out, wb_in, g_b, rc, rs1, rs2):
    T, D = h1.shape
    BW = wb_out.shape[0]
    SH = wb_in.shape[2]
    TM = min(256, T)
    nT = T // TM
    nC = TM // CHUNK
    nP = BW // LANES

    def body(dh2_ref, h1_ref, q_ref, g2_ref, o_ref, k4_ref, k4p_ref, v4_ref, v4p_ref, kt_ref, ktp_ref, sink_ref,
             wbout_ref, wbin_ref, gb_ref, rc_ref, rs1_ref, rs2_ref,
             dh1_ref, dz2_ref, n2_ref, y2_ref, dk_ref, dv_ref, dbq_ref, dgb_ref, dsink_ref, do_scr, dq_scr, dsacc_scr):
        i = pl.program_id(0)

        @pl.when(i == 0)
        def _():
            dk_ref[...] = jnp.zeros_like(dk_ref)
            dv_ref[...] = jnp.zeros_like(dv_ref)
            dbq_ref[...] = jnp.zeros_like(dbq_ref)
            dgb_ref[...] = jnp.zeros_like(dgb_ref)
            dsacc_scr[...] = jnp.zeros_like(dsacc_scr)

        dh2 = dh2_ref[...]
        dy2 = _dot_nt(dh2.astype(BF), wbout_ref[...])
        silu, dsilu = _silu_parts(g2_ref[...].astype(F32))
        do_scr[...] = (dy2 * silu).astype(BF)
        dy2, silu, dsilu = dy2.astype(BF), silu.astype(BF), dsilu.astype(BF)
        ob = o_ref[...]
        y2_ref[...] = (ob * silu).T
        dz2_ref[:, BW:] = dy2 * ob * dsilu
        upper = _upper()
        lo = _lane_lo((2 * CHUNK, LANES))
        for c in range(nC):
            ci = i * nC + c
            rows = slice(c * CHUNK, (c + 1) * CHUNK)
            pci = jnp.maximum(ci - 1, 0)
            prev = pl.multiple_of(pci * CHUNK, CHUNK)
            cur = pl.multiple_of(ci * CHUNK, CHUNK)
            qc = q_ref[rows, :]
            doc = do_scr[rows, :]
            dkb = jnp.zeros((2 * CHUNK, LANES), F32)
            dvb = jnp.zeros((2 * CHUNK, LANES), F32)
            for h in range(2):
                qs = _stack_pairs(qc, h)
                dos = _stack_pairs(doc, h)
                fa, fb = _fold(_dot_nt(_band_rows(k4_ref, k4p_ref, c, h), qs), upper, ci > 0)
                dfa, dfb = _fold(_dot_nt(_band_rows(v4_ref, v4p_ref, c, h), dos), upper)
                folded = []
                for k, (f, df) in enumerate(((fa, dfa), (fb, dfb))):
                    p, ps = _softmax_sink(f, sink_ref[2 * h + k:2 * h + k + 1, :])
                    delta = jnp.sum(p * df, axis=0, keepdims=True)
                    dsacc_scr[2 * h + k:2 * h + k + 1, :] -= ps * delta
                    folded.append((p, p * (df - delta)))
                pt = _unfold(folded[0][0], folded[1][0], upper).astype(BF)
                dst = _unfold(folded[0][1], folded[1][1], upper).astype(BF)
                dqt = _dot(_band_cols(kt_ref, ktp_ref, c, h), dst)
                for j in range(4):
                    dq_scr[rows, (h * 4 + j) * LANES:(h * 4 + j + 1) * LANES] = dqt[:, j * CHUNK:(j + 1) * CHUNK].T
                for acc_name, g in (("k", _dot(dst, qs)), ("v", _dot(pt, dos))):
                    a, b = g[:2 * CHUNK], g[2 * CHUNK:]
                    if h == 0:
                        part = jnp.where(lo, a + pltpu.roll(b, HEAD_DIM, 1), 0.0)
                    else:
                        part = jnp.where(lo, 0.0, pltpu.roll(a, HEAD_DIM, 1) + b)
                    if acc_name == "k":
                        dkb += part
                    else:
                        dvb += part
            dk_ref[pl.ds(prev, CHUNK), :] += dkb[:CHUNK]
            dk_ref[pl.ds(cur, CHUNK), :] += dkb[CHUNK:]
            dv_ref[pl.ds(prev, CHUNK), :] += dvb[:CHUNK]
            dv_ref[pl.ds(cur, CHUNK), :] += dvb[CHUNK:]
        c_t, s1_t, s2_t = rc_ref[...], rs1_ref[...], rs2_ref[...]
        for p in range(nP):
            cols = slice(p * LANES, (p + 1) * LANES)
            dqp = _rot_bwd(dq_scr[:, cols] * (HEAD_DIM ** -0.5), c_t, s1_t, s2_t)
            dbq_ref[:, cols] += jnp.sum(dqp, axis=0, keepdims=True)
            dz2_ref[:, cols] = dqp.astype(BF)
        h1v = h1_ref[...]
        r2 = lax.rsqrt(jnp.mean(h1v * h1v, axis=-1, keepdims=True) + EPS)
        xh = h1v * r2
        gb = gb_ref[...]
        n2_ref[...] = (xh * gb).astype(BF).T
        dn2 = None
        for j in range(N_DEV):
            part = _dot_nt(dz2_ref[:, j * SH:(j + 1) * SH], wbin_ref[j])
            dn2 = part if dn2 is None else dn2 + part
        dgb_ref[...] += jnp.sum(dn2 * xh, axis=0, keepdims=True)
        dh1_ref[...] = dh2 + _rms_bwd(dn2, xh, r2, gb)

        @pl.when(i == nT - 1)
        def _():
            lane = lax.broadcasted_iota(jnp.int32, dsink_ref.shape, 1)
            tot = jnp.zeros(dsink_ref.shape, F32)
            for j in range(4):
                tot += jnp.where(lane == j, jnp.sum(dsacc_scr[:, j * CHUNK:(j + 1) * CHUNK], axis=1, keepdims=True), 0.0)
            dsink_ref[...] = tot

    row = functools.partial(_row_spec, TM)
    rows_tile, rows_before, cols_tile, cols_before = _band_specs(TM)
    S = jax.ShapeDtypeStruct
    return pl.pallas_call(
        body, name="b_bwd", grid=(T // TM,),
        in_specs=[row(D), row(D), row(BW), row(BW), row(BW), rows_tile, rows_before, rows_tile, rows_before,
                  cols_tile, cols_before, _const_spec(sinks.shape), _const_spec(wb_out.shape), _const_spec(wb_in.shape),
                  _const_spec((1, D)), row(LANES), row(LANES), row(LANES)],
        out_specs=[row(D), row(2 * BW), _col_spec(TM, D), _col_spec(TM, BW), _acc_spec((T, LANES)),
                   _acc_spec((T, LANES)), _acc_spec((1, BW)), _acc_spec((1, D)), _acc_spec((4, LANES))],
        out_shape=(S((T, D), F32), S((T, 2 * BW), BF), S((D, T), BF), S((BW, T), BF), S((T, LANES), F32),
                   S((T, LANES), F32), S((1, BW), F32), S((1, D), F32), S((4, LANES), F32)),
        scratch_shapes=[pltpu.VMEM((TM, BW), BF), pltpu.VMEM((TM, BW), F32), pltpu.VMEM((4, 4 * CHUNK), F32)],
        compiler_params=_params(("arbitrary",)),
    )(dh2, h1, q, g2, o, k4, k4, v4, v4, kt, kt, sinks, wb_out, wb_in, g_b, rc, rs1, rs2)


def _a_bwd(dh1p, dk, dv, h1, g_kv, w_kv, wa_out, ws, ln_g, ln_b, z, sv, vhat, rstd, rc, rs1, rs2, ready):
    T, D = h1.shape
    AW = wa_out.shape[0]
    G = ws.shape[0]
    TM = min(256, T)
    nT = T // TM
    nC = TM // CHUNK
    nr = len(ready)

    def body(dh1p_ref, dk_ref, dv_ref, h1_ref, gkv_ref, wkv_ref, waout_ref, ws_ref, lng_ref,
             lnb_ref, u_ref, gt_ref, sv_ref, vhat_ref, rstd_ref, rc_ref, rs1_ref, rs2_ref, *rest):
        ready_refs, rest = rest[:nr], rest[nr:]
        (dz_ref, gwo_ref, gwk_ref, dh1f_ref, dgkv_ref, dbkv_ref, dlng_ref, dlnb_ref,
         dws_ref, dbs_ref), rest = rest[:10], rest[10:]
        recv_refs, (dsv_scr, dvln_scr, acco_scr, acck_scr, ssem, rsem, lsem) = rest[:nr], rest[nr:]
        i = pl.program_id(0)
        exchanges = [_Direct(ready_refs[k], recv_refs[k], ssem.at[k], rsem.at[k], lsem.at[k], scatter=True)
                     for k in range(nr)]

        @pl.when(i == 0)
        def _():
            for e in exchanges:
                e.start()
            for r in (dgkv_ref, dbkv_ref, dlng_ref, dlnb_ref, dws_ref, dbs_ref, acco_scr, acck_scr):
                r[...] = jnp.zeros_like(r)

        dk_pre = _rot_bwd(dk_ref[...], rc_ref[...], rs1_ref[...], rs2_ref[...])
        dkv = jnp.concatenate([dk_pre, dv_ref[...]], axis=1)
        dbkv_ref[...] += jnp.sum(dkv, axis=0, keepdims=True)
        dkv_b = dkv.astype(BF)
        h1v = h1_ref[...]
        rkv = lax.rsqrt(jnp.mean(h1v * h1v, axis=-1, keepdims=True) + EPS)
        xh_kv = h1v * rkv
        gkv = gkv_ref[...]
        acck_scr[...] += _dot((xh_kv * gkv).astype(BF).T, dkv_b)
        dnkv = _dot_nt(dkv_b, wkv_ref[...])
        dgkv_ref[...] += jnp.sum(dnkv * xh_kv, axis=0, keepdims=True)
        dh1 = dh1p_ref[...] + _rms_bwd(dnkv, xh_kv, rkv, gkv)
        dh1_b = dh1.astype(BF)
        dh1f_ref[...] = dh1
        dy = _dot_nt(dh1_b, waout_ref[...]).astype(BF)
        silu, dsilu = _silu_parts(gt_ref[...].astype(F32))
        silu, dsilu = silu.astype(BF), dsilu.astype(BF)
        ub, svb = u_ref[...], sv_ref[...]
        us = ub * silu
        dys = dy * svb
        acco_scr[...] += _dot((us * svb).T, dh1_b)
        dz_ref[:, :AW] = dys * silu
        dz_ref[:, 2 * AW:] = dys * ub * dsilu
        dsv_scr[...] = dy * us
        vhat_v = vhat_ref[...].astype(F32)
        lng = lng_ref[...]
        vln_b = (vhat_v * lng + lnb_ref[...]).astype(BF)
        tri = lax.broadcasted_iota(jnp.int32, (CHUNK, CHUNK), 0) >= lax.broadcasted_iota(jnp.int32, (CHUNK, CHUNK), 1)
        lane = lax.broadcasted_iota(jnp.int32, (CHUNK, LANES), 1)
        dbs = jnp.zeros((CHUNK, LANES), F32)
        for g in range(G):
            wsm = jnp.where(tri, ws_ref[g], 0.0).astype(BF)
            cols = slice(g * CHUNK, (g + 1) * CHUNK)
            dws_g = None
            for c in range(nC):
                rows = slice(c * CHUNK, (c + 1) * CHUNK)
                dsv_cg = dsv_scr[rows, cols]
                dvln_scr[rows, cols] = _dot_tn(wsm, dsv_cg)
                part = _dot_nt(dsv_cg, vln_b[rows, cols])
                dws_g = part if dws_g is None else dws_g + part
                dbs += jnp.where(lane == g, jnp.sum(dsv_cg.astype(F32), axis=-1, keepdims=True), 0.0)
            dws_ref[g] += jnp.where(tri, dws_g, 0.0)
        dbs_ref[...] += dbs
        dvln = dvln_scr[...]
        dlng_ref[...] += jnp.sum(dvln * vhat_v, axis=0, keepdims=True)
        dlnb_ref[...] += jnp.sum(dvln, axis=0, keepdims=True)
        a = dvln * lng
        dvv = rstd_ref[:, 0:1] * (a - jnp.mean(a, axis=-1, keepdims=True)
                                  - vhat_v * jnp.mean(a * vhat_v, axis=-1, keepdims=True))
        dz_ref[:, AW:2 * AW] = dvv.astype(BF)

        @pl.when(i == nT - 1)
        def _():
            for j in range(N_DEV):
                gwo_ref[j] = acco_scr[j * (AW // N_DEV):(j + 1) * (AW // N_DEV)].astype(BF)
                gwk_ref[j] = acck_scr[j * (D // N_DEV):(j + 1) * (D // N_DEV)].astype(BF)
            for e in exchanges:
                e.finish()

    row = functools.partial(_row_spec, TM)
    hbm = pl.BlockSpec(memory_space=pl.ANY)
    S = jax.ShapeDtypeStruct
    gwo_shape, gwk_shape = (N_DEV, AW // N_DEV, D), (N_DEV, D // N_DEV, 2 * LANES)
    return pl.pallas_call(
        body, name="a_bwd", grid=(nT,),
        in_specs=[row(D), row(LANES), row(LANES), row(D), _const_spec((1, D)), _const_spec(w_kv.shape),
                  _const_spec(wa_out.shape), _const_spec(ws.shape),
                  _const_spec((1, AW)), _const_spec((1, AW)), pl.BlockSpec((TM, AW), lambda i: (i, 0)),
                  pl.BlockSpec((TM, AW), lambda i: (i, 2)), row(AW), row(AW), row(LANES),
                  row(LANES), row(LANES), row(LANES)] + [hbm] * nr,
        out_specs=[row(3 * AW), _const_spec(gwo_shape), _const_spec(gwk_shape), row(D),
                   _acc_spec((1, D)), _acc_spec((1, 2 * LANES)), _acc_spec((1, AW)),
                   _acc_spec((1, AW)), _acc_spec(ws.shape), _acc_spec((CHUNK, LANES))] + [hbm] * nr,
        out_shape=(S((T, 3 * AW), BF), S(gwo_shape, BF), S(gwk_shape, BF), S((T, D), F32),
                   S((1, D), F32), S((1, 2 * LANES), F32), S((1, AW), F32), S((1, AW), F32),
                   S(ws.shape, F32), S((CHUNK, LANES), F32)) + tuple(S(r.shape, r.dtype) for r in ready),
        scratch_shapes=[pltpu.VMEM((TM, AW), BF), pltpu.VMEM((TM, AW), F32), pltpu.VMEM((AW, D), F32),
                        pltpu.VMEM((D, 2 * LANES), F32)] + _direct_sems(nr),
        compiler_params=_params(("arbitrary",)),
    )(dh1p, dk, dv, h1, g_kv, w_kv, wa_out, ws, ln_g, ln_b, z, z, sv, vhat, rstd, rc, rs1, rs2, *ready)


def _a_in_bwd(dz, wa_in_t, x, dh1, g_a, ready):
    T, D = x.shape
    TM = min(512, T)
    nT = T // TM
    nr = len(ready)

    def body(dz_ref, wain_ref, x_ref, dh1_ref, ga_ref, *rest):
        ready_refs, (dx_ref, n1_ref, dga_ref), rest = rest[:nr], rest[nr:nr + 3], rest[nr + 3:]
        recv_refs, (ssem, rsem, lsem) = rest[:nr], rest[nr:]
        i = pl.program_id(0)
        exchanges = [_Direct(ready_refs[k], recv_refs[k], ssem.at[k], rsem.at[k], lsem.at[k], scatter=True)
                     for k in range(nr)]

        @pl.when(i == 0)
        def _():
            for e in exchanges:
                e.start()
            dga_ref[...] = jnp.zeros_like(dga_ref)

        xv = x_ref[...]
        r1 = lax.rsqrt(jnp.mean(xv * xv, axis=-1, keepdims=True) + EPS)
        xh = xv * r1
        ga = ga_ref[...]
        n1_ref[...] = (xh * ga).astype(BF).T
        dn1 = _dot(dz_ref[...], wain_ref[...])
        dga_ref[...] += jnp.sum(dn1 * xh, axis=0, keepdims=True)
        dx_ref[...] = dh1_ref[...] + _rms_bwd(dn1, xh, r1, ga)

        @pl.when(i == nT - 1)
        def _():
            for e in exchanges:
                e.finish()

    row = functools.partial(_row_spec, TM)
    hbm = pl.BlockSpec(memory_space=pl.ANY)
    S = jax.ShapeDtypeStruct
    return pl.pallas_call(
        body, name="a_in_bwd", grid=(nT,),
        in_specs=[row(dz.shape[1]), _const_spec(wa_in_t.shape), row(D), row(D), _const_spec((1, D))] + [hbm] * nr,
        out_specs=[row(D), _col_spec(TM, D), _acc_spec((1, D))] + [hbm] * nr,
        out_shape=(S((T, D), F32), S((D, T), BF), S((1, D), F32)) + tuple(S(r.shape, r.dtype) for r in ready),
        scratch_shapes=_direct_sems(nr),
        compiler_params=_params(("arbitrary",)),
    )(dz, wa_in_t, x, dh1, g_a, *ready)


def _wgrad(at, b, nblk, name, bt=512):
    K, T = at.shape
    N = b.shape[1] // nblk
    BT = min(bt, T)
    nt = T // BT

    def body(a_ref, b_ref, o_ref, acc):
        t = pl.program_id(0)

        @pl.when(t == 0)
        def _():
            acc[...] = jnp.zeros_like(acc)

        acc[...] += _dot(a_ref[...], b_ref[...])

        @pl.when(t == nt - 1)
        def _():
            for j in range(nblk):
                o_ref[j] = acc[:, j * N:(j + 1) * N].astype(BF)

    return pl.pallas_call(
        body, name=name, grid=(nt,),
        in_specs=[pl.BlockSpec((K, BT), lambda t: (0, t)), pl.BlockSpec((BT, nblk * N), lambda t: (t, 0))],
        out_specs=pl.BlockSpec((nblk, K, N), lambda t: (0, 0, 0)),
        out_shape=jax.ShapeDtypeStruct((nblk, K, N), BF),
        scratch_shapes=[pltpu.VMEM((K, nblk * N), F32)],
        compiler_params=_params(("arbitrary",)),
    )(at, b)


def _wgrad_exchange(a, b, me, small, name):
    K, T = a.shape
    N = b.shape[1] // N_DEV
    BT = T
    nt = T // BT
    last = N_DEV - 1
    n_chip = N_DEV // 2

    def far_of(k, core):
        return jnp.where((core == 0) & ((k == 1) | (k == 2)), k, n_chip - 1 - k)

    def block_of(s, me_i):
        k, odd = s // 2, s % 2
        core = me_i & 1
        return me_i ^ ((far_of(k, jnp.where(odd == 1, core, 1 - core)) << 1) | (1 - odd))

    H = K // 2

    def body(me_ref, a_ref, b_ref, small_ref, recv_ref, full_ref, *scratch):
        (acc, dstage, istage, half, relay, d_s, d_r, i_s, i_r, r_s, r_r, lsem, parts_scr, red_scr, e_s, e_r, e_l, g_s,
         g_r, g_l) = scratch
        s, t = pl.program_id(0), pl.program_id(1)
        x, y, c = (lax.axis_index(ax) for ax in AXES)
        ex = [_Direct(small_ref, parts_scr, e_s, e_r, e_l, scatter=True)]
        regather = _TwoLevel(red_scr, full_ref, g_s, g_r, g_l)

        def to_sibling(k, slot):
            return pltpu.make_async_remote_copy(src_ref=dstage.at[slot], dst_ref=half.at[k], send_sem=d_s.at[k],
                                                recv_sem=d_r.at[k], device_id=(x, y, 1 - c), device_id_type=MESH)

        def to_chip(k, slot):
            over_x = far_of(k, c) == 2
            px, py = jnp.where(over_x, 1 - x, x), jnp.where(over_x, y, 1 - y)
            return pltpu.make_async_remote_copy(src_ref=istage.at[slot], dst_ref=recv_ref.at[jnp.where(over_x, 1, 2)],
                                                send_sem=i_s.at[k], recv_sem=i_r.at[k], device_id=(px, py, c),
                                                device_id_type=MESH)

        def to_relay(j, slot):
            to = (1 - x, y, c) if j == 0 else (x, 1 - y, c)
            return pltpu.make_async_remote_copy(src_ref=istage.at[slot, pl.ds(j * H, H)], dst_ref=relay.at[j],
                                                send_sem=r_s.at[j], recv_sem=r_r.at[j], device_id=to,
                                                device_id_type=MESH)

        @pl.when((s == 0) & (t == 0))
        def _():
            for e in ex:
                e.start()

        acc[...] = _dot(a_ref[...], b_ref[...])

        @pl.when(t == nt - 1)
        def _():
            k = lax.div(s, 2)
            slot = lax.rem(k, 2)

            @pl.when(lax.rem(s, 2) == 0)
            def _():
                @pl.when(k >= 2)
                def _():
                    to_sibling(k - 2, slot).wait_send()

                dstage[slot] = acc[...].astype(BF)
                to_sibling(k, slot).start()

            @pl.when(lax.rem(s, 2) == 1)
            def _():
                to_sibling(k, slot).wait_recv()
                pair = acc[...] + half[k].astype(F32)

                @pl.when(k == 0)
                def _():
                    istage[slot] = pair.astype(BF)
                    for j in range(2):
                        to_relay(j, slot).start()

                @pl.when(k == 1)
                def _():
                    for j in range(2):
                        to_relay(j, slot).wait_recv()

                @pl.when(k == 2)
                def _():
                    for j in range(2):
                        to_relay(j, slot).wait_send()

                @pl.when(k == n_chip - 1)
                def _():
                    to_chip(1, slot).wait_send()
                    istage[slot] = pair.astype(BF)

                @pl.when((k == 1) | (k == 2))
                def _():
                    over_x = far_of(k, c) == 2
                    istage[slot, 0:H] = (pair[:H] + jnp.where(over_x, 0.0, relay[0].astype(F32))).astype(BF)
                    istage[slot, H:K] = (pair[H:] + jnp.where(over_x, relay[1].astype(F32), 0.0)).astype(BF)
                    to_chip(k, slot).start()

            @pl.when(s == last)
            def _():
                own = pltpu.make_async_copy(istage.at[slot], recv_ref.at[0], lsem)
                own.start()
                to_chip(2, 0).wait_send()
                to_sibling(n_chip - 2, 0).wait_send()
                to_sibling(n_chip - 1, 1).wait_send()
                for kk in (1, 2):
                    to_chip(kk, 0).wait_recv()
                own.wait()
                for e in ex:
                    e.finish()
                total = parts_scr[0]
                for dev in range(1, N_DEV):
                    total = total + parts_scr[dev]
                red_scr[...] = total
                regather.start()
                regather.forward()
                regather.finish()

    hbm = pl.BlockSpec(memory_space=pl.ANY)
    dma = pltpu.SemaphoreType.DMA
    grid_spec = pltpu.PrefetchScalarGridSpec(
        num_scalar_prefetch=1, grid=(N_DEV, nt),
        in_specs=[pl.BlockSpec((K, BT), lambda s, t, me_ref: (0, t), pipeline_mode=pl.Buffered(1)),
                  pl.BlockSpec((BT, N), lambda s, t, me_ref: (t, block_of(s, me_ref[0]))), hbm],
        out_specs=[hbm, hbm],
        scratch_shapes=[pltpu.VMEM((K, N), F32), pltpu.VMEM((2, K, N), BF), pltpu.VMEM((2, K, N), BF),
                        pltpu.VMEM((n_chip, K, N), BF), pltpu.VMEM((2, H, N), BF), dma((n_chip,)), dma((n_chip,)),
                        dma((n_chip - 1,)), dma((n_chip - 1,)), dma((2,)), dma((2,)), dma,
                        pltpu.VMEM(small.shape, F32), pltpu.VMEM(small.shape[1:], F32),
                        dma((last,)), dma((last,)), dma, dma((last,)), dma((last,)), dma])
    return pl.pallas_call(
        body, name=name, grid_spec=grid_spec,
        out_shape=[jax.ShapeDtypeStruct((n_chip - 1, K, N), BF), jax.ShapeDtypeStruct(small.shape, F32)],
        compiler_params=_params(("arbitrary", "arbitrary")),
    )(me, a, b, small)


def _my_index():
    return 4 * lax.axis_index("x") + 2 * lax.axis_index("y") + lax.axis_index("c")


def _peer(mask):
    x, y, c = (lax.axis_index(a) for a in AXES)
    return (x ^ ((mask >> 2) & 1), y ^ ((mask >> 1) & 1), c ^ (mask & 1))


def _dev_index(p):
    return 4 * p[0] + 2 * p[1] + p[2]


class _Direct:
    def __init__(self, src, dst, send_sems, recv_sems, local_sem, scatter):
        me = _my_index()
        self.own = pltpu.make_async_copy(src.at[me] if scatter else src, dst.at[me], local_sem)
        self.sends, self.recvs = [], []
        for k in range(1, N_DEV):
            p = _peer(k)
            pi = _dev_index(p)
            sems = dict(send_sem=send_sems.at[k - 1], recv_sem=recv_sems.at[k - 1], device_id=p, device_id_type=MESH)
            self.sends.append(pltpu.make_async_remote_copy(src_ref=src.at[pi] if scatter else src, dst_ref=dst.at[me],
                                                           **sems))
            self.recvs.append(pltpu.make_async_remote_copy(src_ref=src.at[me] if scatter else src, dst_ref=dst.at[pi],
                                                           **sems))

    def start(self):
        self.own.start()
        for cp in self.sends:
            cp.start()

    def finish(self):
        for cp in self.sends:
            cp.wait_send()
        for cp in self.recvs:
            cp.wait_recv()
        self.own.wait()


class _TwoLevel:
    def __init__(self, src, dst, send_sems, recv_sems, local_sem, own=True):
        x, y, c = (lax.axis_index(a) for a in AXES)
        self.me, self.sibling = (x, y, c), (x, y, 1 - c)
        self.chips = [(1 - x, y), (x, 1 - y), (1 - x, 1 - y)]
        self.src, self.dst, self.send_sems, self.recv_sems = src, dst, send_sems, recv_sems
        self.own = pltpu.make_async_copy(src, dst.at[_dev_index(self.me)], local_sem) if own else None

    def _copy(self, k, block, to, from_src=False):
        slot = self.dst.at[_dev_index(block)]
        return pltpu.make_async_remote_copy(src_ref=self.src if from_src else slot, dst_ref=slot,
                                            send_sem=self.send_sems.at[k], recv_sem=self.recv_sems.at[k],
                                            device_id=to, device_id_type=MESH)

    def _firsts(self):
        c = self.me[2]
        return [self._copy(0, self.me, self.sibling, True)] + [self._copy(1 + j, self.me, (*chip, c), True)
                                                               for j, chip in enumerate(self.chips)]

    def _passed(self):
        c = self.me[2]
        return [self._copy(4 + j, (*chip, c), self.sibling) for j, chip in enumerate(self.chips)]

    def start(self):
        if self.own is not None:
            self.own.start()
        for cp in self._firsts():
            cp.start()

    def wait_sibling(self):
        self._copy(0, self.sibling, self.me).wait_recv()

    def wait_chip_and_forward(self, j):
        self._copy(1 + j, (*self.chips[j], self.me[2]), self.me).wait_recv()
        self._passed()[j].start()

    def wait_passed(self, j):
        self._copy(4 + j, (*self.chips[j], 1 - self.me[2]), self.me).wait_recv()

    def wait_sends(self):
        for cp in self._firsts() + self._passed():
            cp.wait_send()
        if self.own is not None:
            self.own.wait()

    def forward(self):
        for j in range(3):
            self.wait_chip_and_forward(j)

    def finish(self):
        self.wait_sibling()
        for j in range(3):
            self.wait_passed(j)
        self.wait_sends()


class _RelayGather:
    def __init__(self, dst, send_sems, recv_sems):
        x, y, c = (lax.axis_index(a) for a in AXES)
        self.c = c
        self.sib, self.xn, self.yn, self.dg = (x, y, 1 - c), (1 - x, y, c), (x, 1 - y, c), (1 - x, 1 - y, c)
        self.me = (x, y, c)
        self.dst, self.send_sems, self.recv_sems = dst, send_sems, recv_sems
        self.half = dst.shape[1] // 2

    def _slot(self, dev, part=None):
        i = _dev_index(dev)
        if part is None:
            return self.dst.at[i]
        return self.dst.at[i, pl.ds(part * self.half, self.half)]

    def _copy(self, k, dev, to, part=None):
        ref = self._slot(dev, part)
        return pltpu.make_async_remote_copy(src_ref=ref, dst_ref=ref, send_sem=self.send_sems.at[k],
                                            recv_sem=self.recv_sems.at[k], device_id=to, device_id_type=MESH)

    def _other(self, dev):
        return (dev[0], dev[1], 1 - self.c)

    def start(self):
        for k, to in enumerate((self.sib, self.xn, self.yn)):
            self._copy(k, self.me, to).start()

    def send_own(self, k):
        return self._copy(k, self.me, (self.sib, self.xn, self.yn)[k])

    def wait_sibling(self):
        self._copy(0, self.sib, self.me).wait_recv()

    def on_x(self):
        self._copy(1, self.xn, self.me).wait_recv()
        self._copy(3, self.xn, self.yn, part=0).start()
        self._copy(5, self.xn, self.sib).start()

    def on_y(self):
        self._copy(2, self.yn, self.me).wait_recv()
        self._copy(4, self.yn, self.xn, part=1).start()
        self._copy(6, self.yn, self.sib).start()

    def on_diag(self):
        self._copy(3, self.dg, self.me, part=0).wait_recv()
        self._copy(4, self.dg, self.me, part=1).wait_recv()
        self._copy(7, self.dg, self.sib).start()

    def wait_passed(self, j):
        self._copy(5 + j, self._other((self.xn, self.yn, self.dg)[j]), self.me).wait_recv()

    def wait_sends(self):
        for k, to in enumerate((self.sib, self.xn, self.yn)):
            self._copy(k, self.me, to).wait_send()
        self._copy(3, self.xn, self.yn, part=0).wait_send()
        self._copy(4, self.yn, self.xn, part=1).wait_send()
        for j, dev in enumerate((self.xn, self.yn, self.dg)):
            self._copy(5 + j, dev, self.sib).wait_send()


def _direct_sems(n):
    if n == 0:
        return []
    return [pltpu.SemaphoreType.DMA((n, 7)), pltpu.SemaphoreType.DMA((n, 7)), pltpu.SemaphoreType.DMA((n,))]


def _adam_math(w, g, m, v):
    m = ADAM_B1 * m + (1.0 - ADAM_B1) * g
    v = ADAM_B2 * v + (1.0 - ADAM_B2) * (g * g)
    m_hat = m / (1.0 - ADAM_B1 ** ADAM_STEP)
    v_hat = v / (1.0 - ADAM_B2 ** ADAM_STEP)
    delta = -ADAM_LR * (m_hat / (jnp.sqrt(v_hat) + ADAM_EPS) + ADAM_WD * w)
    return delta, m, v


def _sum_adam(parts, w, m, v, name):
    R, C = w.shape
    NP = parts.shape[0]
    BR = 4 * CHUNK if R % (4 * CHUNK) == 0 else R

    def body(p_ref, w_ref, m_ref, v_ref, g_ref, d_ref, nm_ref, nv_ref):
        g = p_ref[0].astype(F32)
        for i in range(1, NP):
            g = g + p_ref[i].astype(F32)
        g_ref[...] = g
        d_ref[...], nm_ref[...], nv_ref[...] = _adam_math(w_ref[...], g, m_ref[...], v_ref[...])

    blk = pl.BlockSpec((BR, C), lambda i: (i, 0))
    S = jax.ShapeDtypeStruct((R, C), F32)
    return pl.pallas_call(
        body, name=name, grid=(R // BR,),
        in_specs=[pl.BlockSpec((NP, BR, C), lambda i: (0, i, 0)), blk, blk, blk],
        out_specs=[blk] * 4, out_shape=(S,) * 4,
        compiler_params=_params(("arbitrary",)),
    )(parts, w, m, v)


SUBLANES = 8


def _nrows(size):
    return -(-size // (SUBLANES * LANES)) * SUBLANES


def _view2d(a):
    return a.reshape(-1, LANES) if a.size % LANES == 0 else a.reshape(1, -1)


def _pack_small(parts, total_rows, name):
    arrs = [p[0] for p in parts]

    def body(*refs):
        out = refs[-1]
        out[...] = jnp.zeros_like(out)
        at = 0
        for ref, (a, rows, flag) in zip(refs[:-1], parts):
            val = ref[...].T if flag == "T" else ref[...]
            r, c = (rows, val.shape[1]) if flag == "T" else val.shape
            out[at:at + r, 0:c] = val[:r]
            at += _nrows(r * c)

    return pl.pallas_call(body, name=name, out_shape=jax.ShapeDtypeStruct((total_rows, LANES), F32))(*arrs)


def _small_update(full, me, reps, shards, name):
    n = len(reps) + len(shards)

    def body(me_ref, full_ref, *refs):
        ins, outs = refs[:3 * n], refs[3 * n:]
        at = 0
        for k in range(n):
            w_ref, m_ref, v_ref = ins[3 * k:3 * k + 3]
            r, c = w_ref.shape
            if k < len(reps):
                g = full_ref[at:at + r, 0:c]
                at += _nrows(r * c)
            else:
                seg = full_ref[at:at + N_DEV * r, :]
                row = lax.broadcasted_iota(jnp.int32, seg.shape, 0)
                pick = [jnp.sum(jnp.where(row == r * me_ref[0] + t, seg, 0.0), axis=0, keepdims=True) for t in range(r)]
                g = pick[0] if r == 1 else jnp.concatenate(pick, axis=0)
                at += N_DEV * r
            g_ref, d_ref, nm_ref, nv_ref = outs[4 * k:4 * k + 4]
            g_ref[...] = g
            d_ref[...], nm_ref[...], nv_ref[...] = _adam_math(w_ref[...], g, m_ref[...], v_ref[...])
        outs[4 * n][...] = full_ref[at:at + 1, 0:1]

    flat = [t for p in reps + shards for t in p]
    S = jax.ShapeDtypeStruct
    res = pl.pallas_call(
        body, name=name,
        in_specs=[pl.BlockSpec(memory_space=pltpu.SMEM)] + [pl.BlockSpec(memory_space=pltpu.VMEM)] * (1 + len(flat)),
        out_shape=[S(p[0].shape, F32) for p in reps + shards for _ in range(4)] + [S((1, 1), F32)],
    )(me, full, *flat)
    return [tuple(res[4 * k:4 * k + 4]) for k in range(n)], res[4 * n]


def _rope_tables(T):
    pos = np.arange(T, dtype=np.float32)
    inv_freq = (np.float64(ROPE_THETA) ** (-np.arange(0, HEAD_DIM, 2, dtype=np.float64) / HEAD_DIM)).astype(np.float32)
    ang = (pos[:, None] * inv_freq[None, :]).astype(np.float64)
    cos, sin, zero = np.cos(ang).astype(np.float32), np.sin(ang).astype(np.float32), np.zeros(ang.shape, np.float32)
    c = np.concatenate([cos, cos, cos, cos], axis=1)
    s1 = np.concatenate([-sin, zero, -sin, zero], axis=1)
    s2 = np.concatenate([zero, sin, zero, sin], axis=1)
    return jnp.asarray(c), jnp.asarray(s1), jnp.asarray(s2)


def kernel(x, a_norm_g, a_w_in, a_ln_g, a_ln_b, a_ws, a_bs, a_w_out, kv_norm_g, w_kv, b_kv, b_norm_g, b_w_in, b_bq, b_sinks, b_w_out, final_norm_g, loss_target, m_a_norm_g, m_a_w_in, m_a_ln_g, m_a_ln_b, m_a_ws, m_a_bs, m_a_w_out, m_kv_norm_g, m_w_kv, m_b_kv, m_b_norm_g, m_b_w_in, m_b_bq, m_b_sinks, m_b_w_out, m_final_norm_g, v_a_norm_g, v_a_w_in, v_a_ln_g, v_a_ln_b, v_a_ws, v_a_bs, v_a_w_out, v_kv_norm_g, v_w_kv, v_b_kv, v_b_norm_g, v_b_w_in, v_b_bq, v_b_sinks, v_b_w_out, v_final_norm_g):
    T, D = x.shape[1], x.shape[2]
    AW = a_ln_g.shape[1] * N_DEV
    G = a_ws.shape[1]
    assert w_kv.shape[1] == 2 * LANES and a_ws.shape[2] == CHUNK and T % CHUNK == 0
    me = _my_index()

    xs, tgt = x[0], loss_target[0]
    z, wa_in_t, g_a, ln_g, ln_b, wa_out, wkv = _in_proj(xs, a_w_in[0], [a_norm_g, a_ln_g, a_ln_b], me.reshape(1),
                                                        [a_w_out[0], w_kv])
    wa_in_t = wa_in_t.reshape(-1, D)
    wa_out = wa_out.reshape(AW, D)
    wkv = wkv.reshape(D, 2 * LANES)

    rc, rs1, rs2 = _rope_tables(T)
    ws = a_ws[0]
    bs_t = a_bs[0].T
    g_kv = kv_norm_g.reshape(1, D)
    bkv = b_kv.reshape(1, -1)
    g_f = final_norm_g.reshape(1, D)
    sinks = jnp.repeat(b_sinks.reshape(2, 4, 2).transpose(0, 2, 1).reshape(4, 4), CHUNK, axis=1)
    h1, sv, vhat, rstd, k4, v4, kt, vt, wb_in, wb_out = _a_fwd(
        xs, z, ln_g, ln_b, ws, bs_t, wa_out, g_kv, wkv, bkv, rc, rs1, rs2, [b_w_in[0], b_w_out[0]])
    wb_out = wb_out.reshape(-1, D)
    q, g2, o, dh2, dh2_b, loss, d_gf = _b_fwd(h1, b_norm_g, wb_in, b_bq, rc, rs1, rs2, k4, vt, sinks, wb_out, g_f, tgt)
    dh1p, dz2, n2, y2, dk, dv, d_bq, d_gb, d_sink = _b_bwd(dh2, h1, q, g2, o, k4, v4, kt, sinks, wb_out, wb_in,
                                                           b_norm_g, rc, rs1, rs2)
    d_sink = d_sink[:, :4].reshape(2, 2, 4).transpose(0, 2, 1).reshape(1, 16)
    gw_b_in = _wgrad(n2, dz2, N_DEV, "wgrad_b_in", bt=1024)
    gw_b_out = _wgrad(y2, dh2_b, 1, "wgrad_b_out", bt=1024).reshape(N_DEV, -1, D)
    (dz, gw_a_out, gw_kv, dh1_f, d_gkv, d_bkv, d_lng, d_lnb, d_ws, d_bst, r_b_in, r_b_out) = _a_bwd(
        dh1p, dk, dv, h1, g_kv, wkv, wa_out, ws, ln_g, ln_b, z, sv, vhat, rstd, rc, rs1, rs2, [gw_b_in, gw_b_out])
    dx, n1, d_ga, r_a_out, r_kv = _a_in_bwd(dz, wa_in_t, xs, dh1_f, g_a, [gw_a_out, gw_kv])
    small = [(_view2d(d_ws), None, None), (d_bst, G, "T")] + [(_view2d(a), None, None) for a in (
        d_gkv, d_bkv, d_gb, d_bq, d_sink, d_gf, d_ga, d_lng, d_lnb, loss)]
    used = sum(_nrows(G * CHUNK if flag else a.size) for a, _, flag in small)
    per = -(-used // (SUBLANES * N_DEV)) * SUBLANES
    small_pack = _pack_small(small, per * N_DEV, "pack_small").reshape(N_DEV, per, LANES)
    r_a_in, full_small = _wgrad_exchange(n1, dz, me.reshape(1), small_pack, "wgrad_a_in")

    g_a_in, d_a_in, nm_a_in, nv_a_in = _sum_adam(r_a_in, a_w_in[0], m_a_w_in[0], v_a_w_in[0], "adam_a_in")
    g_a_out, d_a_out, nm_a_out, nv_a_out = _sum_adam(r_a_out, a_w_out[0], m_a_w_out[0], v_a_w_out[0], "adam_a_out")
    g_kvw, d_kvw, nm_kvw, nv_kvw = _sum_adam(r_kv, w_kv, m_w_kv, v_w_kv, "adam_kv")
    g_b_in, d_b_in, nm_b_in, nv_b_in = _sum_adam(r_b_in, b_w_in[0], m_b_w_in[0], v_b_w_in[0], "adam_b_in")
    g_b_out, d_b_out, nm_b_out, nv_b_out = _sum_adam(r_b_out, b_w_out[0], m_b_w_out[0], v_b_w_out[0], "adam_b_out")

    full_small = full_small.reshape(N_DEV * per, LANES)
    reps = [(a_ws, m_a_ws, v_a_ws), (a_bs, m_a_bs, v_a_bs), (kv_norm_g, m_kv_norm_g, v_kv_norm_g),
            (b_kv, m_b_kv, v_b_kv), (b_norm_g, m_b_norm_g, v_b_norm_g), (b_bq, m_b_bq, v_b_bq),
            (b_sinks, m_b_sinks, v_b_sinks), (final_norm_g, m_final_norm_g, v_final_norm_g)]
    shards = [(a_norm_g, m_a_norm_g, v_a_norm_g), (a_ln_g, m_a_ln_g, v_a_ln_g), (a_ln_b, m_a_ln_b, v_a_ln_b)]
    upd, loss = _small_update(full_small, me.reshape(1), [tuple(_view2d(t) for t in p) for p in reps],
                              [tuple(_view2d(t) for t in p) for p in shards], "adam_small")
    loss = loss[0, 0]
    sm_g, sd, snm, snv = ([upd[k][j].reshape(p[0].shape) for k, p in enumerate(reps + shards)] for j in range(4))

    def order(big, sm):
        a_in, a_out, kvw, b_in, b_out = big
        ws_, bs_, kvg, bkv_, bng, bq_, snk, fng, ang, alng, alnb = sm
        return (ang, a_in[None], alng, alnb, ws_, bs_, a_out[None], kvg, kvw, bkv_, bng, b_in[None], bq_, snk,
                b_out[None], fng)

    grads = order((g_a_in, g_a_out, g_kvw, g_b_in, g_b_out), sm_g)
    deltas = order((d_a_in, d_a_out, d_kvw, d_b_in, d_b_out), sd)
    new_m = order((nm_a_in, nm_a_out, nm_kvw, nm_b_in, nm_b_out), snm)
    new_v = order((nv_a_in, nv_a_out, nv_kvw, nv_b_in, nv_b_out), snv)
    return (loss, dx[None], *grads, *deltas, *new_m, *new_v)
```

```python
import functools

import jax
import jax.numpy as jnp
import numpy as np
from jax import lax
from jax.experimental import pallas as pl
from jax.experimental.pallas import tpu as pltpu

CHUNK = 128
HEAD_DIM = 64
ROPE_THETA = 10000.0
EPS = 1e-5
ADAM_LR = 0.001
ADAM_B1 = 0.9
ADAM_B2 = 0.999
ADAM_EPS = 1e-08
ADAM_WD = 0.01
ADAM_STEP = 10
N_DEV = 8
LANES = 128
NEG = -1e30

BF = jnp.bfloat16
F32 = jnp.float32
MESH = pl.DeviceIdType.MESH
AXES = ("x", "y", "c")
VMEM_LIMIT = 56 * 1024 * 1024


def _dot(a, b):
    return jnp.dot(a, b, preferred_element_type=F32)


def _dot_nt(a, b):
    return lax.dot_general(a, b, (((1,), (1,)), ((), ())), preferred_element_type=F32)


def _dot_tn(a, b):
    return lax.dot_general(a, b, (((0,), (0,)), ((), ())), preferred_element_type=F32)


def _const_spec(shape):
    nd = len(shape)
    return pl.BlockSpec(shape, lambda *_: (0,) * nd, pipeline_mode=pl.Buffered(1))


def _acc_spec(shape):
    nd = len(shape)
    return pl.BlockSpec(shape, lambda *_: (0,) * nd)


def _row_spec(tm, width):
    return pl.BlockSpec((tm, width), lambda i: (i, 0))


def _col_spec(tm, height):
    return pl.BlockSpec((height, tm), lambda i: (0, i))


def _params(sem):
    return pltpu.CompilerParams(dimension_semantics=sem, vmem_limit_bytes=VMEM_LIMIT)


def _rot(x, c, s1, s2):
    return x * c + pltpu.roll(x, 96, 1) * s1 + pltpu.roll(x, 32, 1) * s2


def _rot_bwd(d, c, s1, s2):
    return d * c + pltpu.roll(d * s1, 32, 1) + pltpu.roll(d * s2, 96, 1)


def _silu_parts(g):
    sg = jax.nn.sigmoid(g)
    return g * sg, sg * (1.0 + g * (1.0 - sg))


def _rms_bwd(dn, xh, r, g):
    a = dn * g
    return r * (a - xh * jnp.mean(a * xh, axis=-1, keepdims=True))


def _lane_lo(shape):
    return lax.broadcasted_iota(jnp.int32, shape, 1) < HEAD_DIM


def _split4(t):
    lo = _lane_lo(t.shape)
    tr = pltpu.roll(t, HEAD_DIM, 1)
    z = jnp.zeros_like(t)
    return jnp.concatenate([jnp.where(lo, t, z), jnp.where(lo, z, tr), jnp.where(lo, tr, z), jnp.where(lo, z, t)], axis=1)


def _stack_pairs(t, h):
    return jnp.concatenate([t[:, (h * 4 + j) * LANES:(h * 4 + j + 1) * LANES] for j in range(4)], axis=0)


def _upper():
    shape = (CHUNK, 4 * CHUNK)
    return lax.broadcasted_iota(jnp.int32, shape, 0) > (lax.broadcasted_iota(jnp.int32, shape, 1) & (CHUNK - 1))


def _band_rows(tile_ref, before_ref, c, h):
    a = slice(2 * h * LANES, (2 * h + 1) * LANES)
    b = slice((2 * h + 1) * LANES, (2 * h + 2) * LANES)
    cur = slice(c * CHUNK, (c + 1) * CHUNK)

    def prev(cols):
        return before_ref[:, cols] if c == 0 else tile_ref[(c - 1) * CHUNK:c * CHUNK, cols]

    return jnp.concatenate([prev(a), tile_ref[cur, a], prev(b), tile_ref[cur, b]], axis=0)


def _band_cols(tile_ref, before_ref, c, h):
    a = slice(2 * h * LANES, (2 * h + 1) * LANES)
    b = slice((2 * h + 1) * LANES, (2 * h + 2) * LANES)

    def prev(rows):
        return before_ref[0, rows, :] if c == 0 else tile_ref[c - 1, rows, :]

    return jnp.concatenate([prev(a), tile_ref[c, a, :], prev(b), tile_ref[c, b, :]], axis=1)


def _band_specs(tm):
    nc = tm // CHUNK

    def before(i):
        return jnp.maximum(i * nc - 1, 0)

    return (pl.BlockSpec((tm, 4 * LANES), lambda i: (i, 0)),
            pl.BlockSpec((CHUNK, 4 * LANES), lambda i: (before(i), 0)),
            pl.BlockSpec((nc, 4 * LANES, CHUNK), lambda i: (i, 0, 0)),
            pl.BlockSpec((1, 4 * LANES, CHUNK), lambda i: (before(i), 0, 0)))


def _fold(t, upper, has_prev=None):
    out = []
    for k in range(2):
        prev = t[2 * k * CHUNK:(2 * k + 1) * CHUNK]
        if has_prev is not None:
            prev = jnp.where(has_prev, prev, NEG)
        out.append(jnp.where(upper, prev, t[(2 * k + 1) * CHUNK:(2 * k + 2) * CHUNK]))
    return out


def _unfold(fa, fb, upper):
    z = jnp.zeros_like(fa)
    return jnp.concatenate([jnp.where(upper, fa, z), jnp.where(upper, z, fa),
                            jnp.where(upper, fb, z), jnp.where(upper, z, fb)], axis=0)


def _sink_tile(s_ref):
    shape = (4, 4 * LANES)
    row = lax.broadcasted_iota(jnp.int32, shape, 0)
    pair = lax.broadcasted_iota(jnp.int32, shape, 1) // LANES
    idx = (row // 2) * 8 + pair * 2 + row % 2
    tile = jnp.zeros(shape, F32)
    for n in range(16):
        tile = jnp.where(idx == n, s_ref[0, n], tile)
    return tile


def _softmax_sink(f, sink):
    m = jnp.maximum(jnp.max(f, axis=0, keepdims=True), sink)
    p = jnp.exp(f - m)
    es = jnp.exp(sink - m)
    inv = 1.0 / (jnp.sum(p, axis=0, keepdims=True) + es)
    return p * inv, es * inv


class _Riding:
    def __init__(self, shards, gathered, stages, sems, n_steps):
        self.shards, self.stages, self.n_steps = shards, stages, n_steps
        ssem, rsem, lsem = sems
        self.gathers = [_TwoLevel(stages[k], gathered[k], ssem.at[k], rsem.at[k], lsem.at[k])
                        for k in range(len(shards))]

    def begin(self, i):
        @pl.when(i == 0)
        def _():
            for shard, stage, g in zip(self.shards, self.stages, self.gathers):
                stage[...] = shard[...].astype(stage.dtype)
                g.start()

    def end(self, i):
        @pl.when(i == self.n_steps // 2)
        def _():
            for g in self.gathers:
                g.forward()

        @pl.when(i == self.n_steps - 1)
        def _():
            for g in self.gathers:
                g.finish()

    @staticmethod
    def specs(later):
        nl = len(later)
        hbm = pl.BlockSpec(memory_space=pl.ANY)
        return ([_const_spec(w.shape) for w in later], [hbm] * nl,
                tuple(jax.ShapeDtypeStruct((N_DEV,) + w.shape, BF) for w in later),
                [pltpu.VMEM(w.shape, BF) for w in later] + _direct_sems(nl))


PASS_MASKS = ((0, 1, 2, 5, 4, 3, 6, 7), (0, 1, 4, 3, 2, 5, 6, 7))


def _in_proj(x, w_shard, vec_shards, me, later):
    T, D = x.shape
    SH = w_shard.shape[1]
    TM = min(1024, T)
    nT = T // TM
    nl = len(later)
    nv = len(vec_shards)
    widths = [v.shape[1] for v in vec_shards]
    offsets = [sum(widths[:k]) for k in range(nv)]
    vec_shape = (SUBLANES, sum(widths))
    ds = widths[0]
    last = N_DEV - 1
    masks = jnp.asarray(np.array(PASS_MASKS, np.int32).reshape(-1))

    def slot(p, me_ref, masks_ref):
        return me_ref[0] ^ masks_ref[(me_ref[0] & 1) * N_DEV + p]

    def body(me_ref, masks_ref, x_ref, wsh_ref, *rest):
        vsh_refs, rest = rest[:nv], rest[nv:]
        shards, rest = rest[:nl], rest[nl:]
        (z_ref, wt_ref), rest = rest[:2], rest[2:]
        vout_refs, rest = rest[:nv], rest[nv:]
        gathered, rest = rest[:nl], rest[nl:]
        (w_scr, vec_scr, vstage, n1_scr, ga_scr, w_s, w_r, v_s, v_r, v_l), rest = rest[:10], rest[10:]
        stages, sems = rest[:nl], rest[nl:]
        p, i = pl.program_id(0), pl.program_id(1)
        me = _my_index()
        wg = _RelayGather(w_scr, w_s, w_r)
        vg = _Direct(vstage, vec_scr, v_s, v_r, v_l, scatter=False)
        lg = [_TwoLevel(stages[k], gathered[k], sems[0].at[k], sems[1].at[k], sems[2].at[k]) for k in range(nl)]

        def at_pass(k):
            return (p == k) & (i == 0)

        c = lax.axis_index("c")

        @pl.when(at_pass(0))
        def _():
            for ref, off, wd in zip(vsh_refs, offsets, widths):
                vstage[:, off:off + wd] = jnp.broadcast_to(ref[...], (SUBLANES, wd))
            vg.start()
            w_scr[me] = wsh_ref[...].astype(BF)
            wg.send_own(0).start()

            @pl.when(c == 1)
            def _():
                wg.send_own(1).start()

            @pl.when(c == 0)
            def _():
                wg.send_own(2).start()

            vg.finish()
            for j in range(N_DEV):
                ga_scr[:, j * ds:(j + 1) * ds] = vec_scr[j, 0:1, 0:ds]
                for ref, off, wd in zip(vout_refs, offsets, widths):
                    ref[:, j * wd:(j + 1) * wd] = vec_scr[j, 0:1, off:off + wd]

        @pl.when(at_pass(1))
        def _():
            wg.wait_sibling()

        for first, second, landed_first, landed_second in ((1, 2, wg.on_x, wg.on_y), (2, 1, wg.on_y, wg.on_x)):
            mine = c == (1 if first == 1 else 0)

            @pl.when(at_pass(2) & mine)
            def _(second=second, landed_first=landed_first):
                wg.send_own(second).start()
                landed_first()

            @pl.when(at_pass(3) & mine)
            def _(second=second):
                wg.wait_passed(second - 1)

            @pl.when(at_pass(4) & mine)
            def _(landed_second=landed_second):
                landed_second()

            @pl.when(at_pass(5) & mine)
            def _(first=first):
                wg.wait_passed(first - 1)

        @pl.when(at_pass(4))
        def _():
            for k in range(nl):
                stages[k][...] = shards[k][...].astype(BF)
                lg[k].start()

        @pl.when(at_pass(6))
        def _():
            wg.on_diag()

        @pl.when(at_pass(7))
        def _():
            wg.wait_passed(2)

        @pl.when(p == 0)
        def _():
            xv = x_ref[...]
            r1 = lax.rsqrt(jnp.mean(xv * xv, axis=-1, keepdims=True) + EPS)
            n1_scr[i] = (xv * r1 * ga_scr[...]).astype(BF)

        z_ref[...] = _dot(n1_scr[i], w_scr[slot(p, me_ref, masks_ref)]).astype(BF)

        @pl.when(i == 0)
        def _():
            wt_ref[0] = w_scr[slot(p, me_ref, masks_ref)].T

        @pl.when((p == last) & (i == nT - 1))
        def _():
            wg.wait_sends()
            for g in lg:
                g.forward()
            for g in lg:
                g.finish()

    hbm = pl.BlockSpec(memory_space=pl.ANY)
    dma = pltpu.SemaphoreType.DMA
    S = jax.ShapeDtypeStruct
    def whole(shape):
        return pl.BlockSpec(shape, lambda p, i, m, t: (0, 0))

    def once(shape):
        return pl.BlockSpec(shape, lambda p, i, m, t: (0, 0), pipeline_mode=pl.Buffered(1))

    grid_spec = pltpu.PrefetchScalarGridSpec(
        num_scalar_prefetch=2, grid=(N_DEV, nT),
        in_specs=[pl.BlockSpec((TM, D), lambda p, i, m, t: (jnp.where(p == 0, i, nT - 1), 0)), once(w_shard.shape)]
        + [once(v.shape) for v in vec_shards] + [once(w.shape) for w in later],
        out_specs=[pl.BlockSpec((TM, SH), lambda p, i, m, t: (i, slot(p, m, t))),
                   pl.BlockSpec((1, SH, D), lambda p, i, m, t: (slot(p, m, t), 0, 0))]
        + [whole((1, N_DEV * wd)) for wd in widths] + [hbm] * nl,
        scratch_shapes=[pltpu.VMEM((N_DEV, D, SH), BF), pltpu.VMEM((N_DEV,) + vec_shape, F32),
                        pltpu.VMEM(vec_shape, F32), pltpu.VMEM((nT, TM, D), BF), pltpu.VMEM((1, D), F32),
                        dma((8,)), dma((8,)), dma((7,)), dma((7,)), dma]
        + [pltpu.VMEM(w.shape, BF) for w in later] + _direct_sems(nl))
    return pl.pallas_call(
        body, name="a_in_proj", grid_spec=grid_spec,
        out_shape=(S((T, N_DEV * SH), BF), S((N_DEV, SH, D), BF)) + tuple(S((1, N_DEV * wd), F32) for wd in widths)
        + tuple(S((N_DEV,) + w.shape, BF) for w in later),
        compiler_params=_params(("arbitrary", "arbitrary")),
    )(me, masks, x, w_shard, *vec_shards, *later)


def _a_fwd(x, z, ln_g, ln_b, ws, bs_t, wa_out, g_kv, w_kv, b_kv, rc, rs1, rs2, later):
    T, D = x.shape
    AW = wa_out.shape[0]
    G = ws.shape[0]
    TM = min(256, T)
    nT = T // TM
    nC = TM // CHUNK
    nl = len(later)

    def body(x_ref, u_ref, v_ref, gt_ref, lng_ref, lnb_ref, ws_ref, bst_ref, waout_ref, gkv_ref, wkv_ref, bkv_ref,
             rc_ref, rs1_ref, rs2_ref, *rest):
        shards, rest = rest[:nl], rest[nl:]
        (h1_ref, sv_ref, vhat_ref, rstd_ref, k4_ref, v4_ref, kt_ref, vt_ref), rest = rest[:8], rest[8:]
        gathered, sv_scr, stages, sems = rest[:nl], rest[nl], rest[nl + 1:2 * nl + 1], rest[2 * nl + 1:]
        i = pl.program_id(0)
        riding = _Riding(shards, gathered, stages, sems, nT)
        riding.begin(i)
        xv = x_ref[...]
        u = u_ref[...].astype(F32)
        v = v_ref[...].astype(F32)
        gt = gt_ref[...].astype(F32)
        mu = jnp.mean(v, axis=-1, keepdims=True)
        xc = v - mu
        rstd = lax.rsqrt(jnp.mean(xc * xc, axis=-1, keepdims=True) + EPS)
        vhat = xc * rstd
        vln = (vhat * lng_ref[...] + lnb_ref[...]).astype(BF)
        tri = lax.broadcasted_iota(jnp.int32, (CHUNK, CHUNK), 0) >= lax.broadcasted_iota(jnp.int32, (CHUNK, CHUNK), 1)
        for g in range(G):
            wsm = jnp.where(tri, ws_ref[g], 0.0).astype(BF)
            bias = bst_ref[:, g:g + 1]
            for c in range(nC):
                blk = vln[c * CHUNK:(c + 1) * CHUNK, g * CHUNK:(g + 1) * CHUNK]
                sv_scr[c * CHUNK:(c + 1) * CHUNK, g * CHUNK:(g + 1) * CHUNK] = _dot(wsm, blk) + bias
        sv = sv_scr[...]
        silu, _ = _silu_parts(gt)
        y = (u * sv * silu).astype(BF)
        h1 = xv + _dot(y, waout_ref[...])
        h1_ref[...] = h1
        sv_ref[...] = sv.astype(BF)
        vhat_ref[...] = vhat.astype(BF)
        rstd_ref[...] = jnp.broadcast_to(rstd, rstd_ref.shape)
        rkv = lax.rsqrt(jnp.mean(h1 * h1, axis=-1, keepdims=True) + EPS)
        nkv = (h1 * rkv * gkv_ref[...]).astype(BF)
        kv = _dot(nkv, wkv_ref[...]) + bkv_ref[...]
        k_rot = _rot(kv[:, :LANES], rc_ref[...], rs1_ref[...], rs2_ref[...])
        for src, ref, tref in ((k_rot, k4_ref, kt_ref), (kv[:, LANES:], v4_ref, vt_ref)):
            t4 = _split4(src)
            ref[...] = t4.astype(BF)
            for c in range(nC):
                for b in range(4):
                    blk = t4[c * CHUNK:(c + 1) * CHUNK, b * LANES:(b + 1) * LANES]
                    tref[c, b * LANES:(b + 1) * LANES, :] = blk.T.astype(BF)
        riding.end(i)

    row = functools.partial(_row_spec, TM)
    zcol = [pl.BlockSpec((TM, AW), functools.partial(lambda k, i: (i, k), k)) for k in range(3)]
    tr = pl.BlockSpec((nC, 4 * LANES, CHUNK), lambda i: (i, 0, 0))
    r_in, r_out, r_shape, r_scratch = _Riding.specs(later)
    S = jax.ShapeDtypeStruct
    return pl.pallas_call(
        body, name="a_fwd", grid=(nT,),
        in_specs=[row(D)] + zcol + [_const_spec((1, AW)), _const_spec((1, AW)),
                  _const_spec(ws.shape), _const_spec(bs_t.shape), _const_spec(wa_out.shape), _const_spec((1, D)),
                  _const_spec(w_kv.shape), _const_spec((1, 2 * LANES)), row(LANES), row(LANES), row(LANES)] + r_in,
        out_specs=[row(D), row(AW), row(AW), row(LANES), row(4 * LANES), row(4 * LANES), tr, tr] + r_out,
        out_shape=(S((T, D), F32), S((T, AW), BF), S((T, AW), BF), S((T, LANES), F32),
                   S((T, 4 * LANES), BF), S((T, 4 * LANES), BF),
                   S((T // CHUNK, 4 * LANES, CHUNK), BF), S((T // CHUNK, 4 * LANES, CHUNK), BF)) + r_shape,
        scratch_shapes=[pltpu.VMEM((TM, AW), F32)] + r_scratch,
        compiler_params=_params(("arbitrary",)),
    )(x, z, z, z, ln_g, ln_b, ws, bs_t, wa_out, g_kv, w_kv, b_kv, rc, rs1, rs2, *later)


def _b_fwd(h1, g_b, wb_in, bq, rc, rs1, rs2, k4, vt, sinks, wb_out, g_f, target):
    T, D = h1.shape
    BW = wb_out.shape[0]
    SH = wb_in.shape[2]
    TM = min(512, T)
    nC = TM // CHUNK
    nP = BW // LANES

    def body(h1_ref, gb_ref, wbin_ref, bq_ref, rc_ref, rs1_ref, rs2_ref, k4_ref, k4p_ref, vt_ref, vtp_ref, sinks_ref,
             wbout_ref, gf_ref, tgt_ref, q_ref, g2_ref, o_ref, dh2_ref, dh2b_ref, loss_ref, dgf_ref, z_scr, o_scr):
        i = pl.program_id(0)
        sink = _sink_tile(sinks_ref)

        @pl.when(i == 0)
        def _():
            loss_ref[...] = jnp.zeros_like(loss_ref)
            dgf_ref[...] = jnp.zeros_like(dgf_ref)

        h1v = h1_ref[...]
        r2 = lax.rsqrt(jnp.mean(h1v * h1v, axis=-1, keepdims=True) + EPS)
        n2 = (h1v * r2 * gb_ref[...]).astype(BF)
        for j in range(N_DEV):
            z_scr[:, j * SH:(j + 1) * SH] = _dot(n2, wbin_ref[j])
        c_t, s1_t, s2_t = rc_ref[...], rs1_ref[...], rs2_ref[...]
        for p in range(nP):
            cols = slice(p * LANES, (p + 1) * LANES)
            qp = _rot(z_scr[:, cols] + bq_ref[:, cols], c_t, s1_t, s2_t) * (HEAD_DIM ** -0.5)
            q_ref[:, cols] = qp.astype(BF)
        g2 = z_scr[:, BW:]
        g2_ref[...] = g2.astype(BF)
        upper = _upper()
        for c in range(nC):
            ci = i * nC + c
            rows = slice(c * CHUNK, (c + 1) * CHUNK)
            qc = q_ref[rows, :]
            for h in range(2):
                st = _dot_nt(_band_rows(k4_ref, k4p_ref, c, h), _stack_pairs(qc, h))
                fa, fb = _fold(st, upper, ci > 0)
                pa, _ = _softmax_sink(fa, sink[2 * h:2 * h + 1, :])
                pb, _ = _softmax_sink(fb, sink[2 * h + 1:2 * h + 2, :])
                ot = _dot(_band_cols(vt_ref, vtp_ref, c, h), _unfold(pa, pb, upper).astype(BF))
                for j in range(4):
                    o_scr[rows, (h * 4 + j) * LANES:(h * 4 + j + 1) * LANES] = ot[:, j * CHUNK:(j + 1) * CHUNK].T
        o = o_scr[...]
        o_ref[...] = o.astype(BF)
        silu, _ = _silu_parts(g2)
        h2 = h1v + _dot((o * silu).astype(BF), wbout_ref[...])
        rf = lax.rsqrt(jnp.mean(h2 * h2, axis=-1, keepdims=True) + EPS)
        xh = h2 * rf
        gf = gf_ref[...]
        err = xh * gf - tgt_ref[...]
        dyf = err * (1.0 / D)
        dh2 = _rms_bwd(dyf, xh, rf, gf)
        dh2_ref[...] = dh2
        dh2b_ref[...] = dh2.astype(BF)
        loss_ref[...] += 0.5 * jnp.sum(jnp.mean(err * err, axis=-1, keepdims=True), axis=0, keepdims=True)
        dgf_ref[...] += jnp.sum(dyf * xh, axis=0, keepdims=True)

    row = functools.partial(_row_spec, TM)
    rows_tile, rows_before, cols_tile, cols_before = _band_specs(TM)
    S = jax.ShapeDtypeStruct
    return pl.pallas_call(
        body, name="b_fwd", grid=(T // TM,),
        in_specs=[row(D), _const_spec((1, D)), _const_spec(wb_in.shape), _const_spec((1, BW)), row(LANES), row(LANES),
                  row(LANES), rows_tile, rows_before, cols_tile, cols_before, pl.BlockSpec(memory_space=pltpu.SMEM),
                  _const_spec(wb_out.shape), _const_spec((1, D)), row(D)],
        out_specs=[row(BW), row(BW), row(BW), row(D), row(D), _acc_spec((1, 1)), _acc_spec((1, D))],
        out_shape=(S((T, BW), BF), S((T, BW), BF), S((T, BW), BF), S((T, D), F32), S((T, D), BF), S((1, 1), F32),
                   S((1, D), F32)),
        scratch_shapes=[pltpu.VMEM((TM, 2 * BW), F32), pltpu.VMEM((TM, BW), F32)],
        compiler_params=_params(("arbitrary",)),
    )(h1, g_b, wb_in, bq, rc, rs1, rs2, k4, k4, vt, vt, sinks, wb_out, g_f, target)


def _b_bwd(dh2, h1, q, g2, o, k4, v4, kt, sinks, wb_out, wb_in, g_b, rc, rs1, rs2):
    T, D = h1.shape
    BW = wb_out.shape[0]
    SH = wb_in.shape[2]
    TM = min(256, T)
    nT = T // TM
    nC = TM // CHUNK
    nP = BW // LANES

    def body(dh2_ref, h1_ref, q_ref, g2_ref, o_ref, k4_ref, k4p_ref, v4_ref, v4p_ref, kt_ref, ktp_ref, sinks_ref,
             wbout_ref, wbin_ref, gb_ref, rc_ref, rs1_ref, rs2_ref,
             dh1_ref, dz2_ref, n2_ref, y2_ref, dk_ref, dv_ref, dbq_ref, dgb_ref, dsink_ref, do_scr, dq_scr, dsacc_scr):
        i = pl.program_id(0)
        sink = _sink_tile(sinks_ref)

        @pl.when(i == 0)
        def _():
            dk_ref[...] = jnp.zeros_like(dk_ref)
            dv_ref[...] = jnp.zeros_like(dv_ref)
            dbq_ref[...] = jnp.zeros_like(dbq_ref)
            dgb_ref[...] = jnp.zeros_like(dgb_ref)
            dsacc_scr[...] = jnp.zeros_like(dsacc_scr)

        dh2 = dh2_ref[...]
        dy2 = _dot_nt(dh2.astype(BF), wbout_ref[...])
        silu, dsilu = _silu_parts(g2_ref[...].astype(F32))
        do_scr[...] = (dy2 * silu).astype(BF)
        dy2, silu, dsilu = dy2.astype(BF), silu.astype(BF), dsilu.astype(BF)
        ob = o_ref[...]
        y2_ref[...] = (ob * silu).T
        dz2_ref[:, BW:] = dy2 * ob * dsilu
        upper = _upper()
        lo = _lane_lo((2 * CHUNK, LANES))
        for c in range(nC):
            ci = i * nC + c
            rows = slice(c * CHUNK, (c + 1) * CHUNK)
            pci = jnp.maximum(ci - 1, 0)
            prev = pl.multiple_of(pci * CHUNK, CHUNK)
            cur = pl.multiple_of(ci * CHUNK, CHUNK)
            qc = q_ref[rows, :]
            doc = do_scr[rows, :]
            dkb = jnp.zeros((2 * CHUNK, LANES), F32)
            dvb = jnp.zeros((2 * CHUNK, LANES), F32)
            for h in range(2):
                qs = _stack_pairs(qc, h)
                dos = _stack_pairs(doc, h)
                fa, fb = _fold(_dot_nt(_band_rows(k4_ref, k4p_ref, c, h), qs), upper, ci > 0)
                dfa, dfb = _fold(_dot_nt(_band_rows(v4_ref, v4p_ref, c, h), dos), upper)
                folded = []
                for k, (f, df) in enumerate(((fa, dfa), (fb, dfb))):
                    p, ps = _softmax_sink(f, sink[2 * h + k:2 * h + k + 1, :])
                    delta = jnp.sum(p * df, axis=0, keepdims=True)
                    dsacc_scr[2 * h + k:2 * h + k + 1, :] -= ps * delta
                    folded.append((p, p * (df - delta)))
                pt = _unfold(folded[0][0], folded[1][0], upper).astype(BF)
                dst = _unfold(folded[0][1], folded[1][1], upper).astype(BF)
                dqt = _dot(_band_cols(kt_ref, ktp_ref, c, h), dst)
                for j in range(4):
                    dq_scr[rows, (h * 4 + j) * LANES:(h * 4 + j + 1) * LANES] = dqt[:, j * CHUNK:(j + 1) * CHUNK].T
                for acc_name, g in (("k", _dot(dst, qs)), ("v", _dot(pt, dos))):
                    a, b = g[:2 * CHUNK], g[2 * CHUNK:]
                    if h == 0:
                        part = jnp.where(lo, a + pltpu.roll(b, HEAD_DIM, 1), 0.0)
                    else:
                        part = jnp.where(lo, 0.0, pltpu.roll(a, HEAD_DIM, 1) + b)
                    if acc_name == "k":
                        dkb += part
                    else:
                        dvb += part
            dk_ref[pl.ds(prev, CHUNK), :] += dkb[:CHUNK]
            dk_ref[pl.ds(cur, CHUNK), :] += dkb[CHUNK:]
            dv_ref[pl.ds(prev, CHUNK), :] += dvb[:CHUNK]
            dv_ref[pl.ds(cur, CHUNK), :] += dvb[CHUNK:]
        c_t, s1_t, s2_t = rc_ref[...], rs1_ref[...], rs2_ref[...]
        for p in range(nP):
            cols = slice(p * LANES, (p + 1) * LANES)
            dqp = _rot_bwd(dq_scr[:, cols] * (HEAD_DIM ** -0.5), c_t, s1_t, s2_t)
            dbq_ref[:, cols] += jnp.sum(dqp, axis=0, keepdims=True)
            dz2_ref[:, cols] = dqp.astype(BF)
        h1v = h1_ref[...]
        r2 = lax.rsqrt(jnp.mean(h1v * h1v, axis=-1, keepdims=True) + EPS)
        xh = h1v * r2
        gb = gb_ref[...]
        n2_ref[...] = (xh * gb).astype(BF).T
        dn2 = None
        for j in range(N_DEV):
            part = _dot_nt(dz2_ref[:, j * SH:(j + 1) * SH], wbin_ref[j])
            dn2 = part if dn2 is None else dn2 + part
        dgb_ref[...] += jnp.sum(dn2 * xh, axis=0, keepdims=True)
        dh1_ref[...] = dh2 + _rms_bwd(dn2, xh, r2, gb)

        @pl.when(i == nT - 1)
        def _():
            lane = lax.broadcasted_iota(jnp.int32, dsink_ref.shape, 1)
            tot = jnp.zeros(dsink_ref.shape, F32)
            for j in range(4):
                tot += jnp.where(lane == j, jnp.sum(dsacc_scr[:, j * CHUNK:(j + 1) * CHUNK], axis=1, keepdims=True), 0.0)
            dsink_ref[...] = tot

    row = functools.partial(_row_spec, TM)
    rows_tile, rows_before, cols_tile, cols_before = _band_specs(TM)
    S = jax.ShapeDtypeStruct
    return pl.pallas_call(
        body, name="b_bwd", grid=(T // TM,),
        in_specs=[row(D), row(D), row(BW), row(BW), row(BW), rows_tile, rows_before, rows_tile, rows_before,
                  cols_tile, cols_before, pl.BlockSpec(memory_space=pltpu.SMEM), _const_spec(wb_out.shape), _const_spec(wb_in.shape),
                  _const_spec((1, D)), row(LANES), row(LANES), row(LANES)],
        out_specs=[row(D), row(2 * BW), _col_spec(TM, D), _col_spec(TM, BW), _acc_spec((T, LANES)),
                   _acc_spec((T, LANES)), _acc_spec((1, BW)), _acc_spec((1, D)), _acc_spec((4, LANES))],
        out_shape=(S((T, D), F32), S((T, 2 * BW), BF), S((D, T), BF), S((BW, T), BF), S((T, LANES), F32),
                   S((T, LANES), F32), S((1, BW), F32), S((1, D), F32), S((4, LANES), F32)),
        scratch_shapes=[pltpu.VMEM((TM, BW), BF), pltpu.VMEM((TM, BW), F32), pltpu.VMEM((4, 4 * CHUNK), F32)],
        compiler_params=_params(("arbitrary",)),
    )(dh2, h1, q, g2, o, k4, k4, v4, v4, kt, kt, sinks, wb_out, wb_in, g_b, rc, rs1, rs2)


def _a_bwd(dh1p, dk, dv, h1, g_kv, w_kv, wa_out, ws, ln_g, ln_b, z, sv, vhat, rstd, rc, rs1, rs2, ready):
    T, D = h1.shape
    AW = wa_out.shape[0]
    G = ws.shape[0]
    TM = min(256, T)
    nT = T // TM
    nC = TM // CHUNK
    nr = len(ready)

    def body(dh1p_ref, dk_ref, dv_ref, h1_ref, gkv_ref, wkv_ref, waout_ref, ws_ref, lng_ref,
             lnb_ref, u_ref, gt_ref, sv_ref, vhat_ref, rstd_ref, rc_ref, rs1_ref, rs2_ref, *rest):
        ready_refs, rest = rest[:nr], rest[nr:]
        (dz_ref, gwo_ref, gwk_ref, dh1f_ref, dgkv_ref, dbkv_ref, dlng_ref, dlnb_ref,
         dws_ref, dbs_ref), rest = rest[:10], rest[10:]
        recv_refs, (dsv_scr, dvln_scr, acco_scr, acck_scr, ssem, rsem, lsem) = rest[:nr], rest[nr:]
        i = pl.program_id(0)
        exchanges = [_Direct(ready_refs[k], recv_refs[k], ssem.at[k], rsem.at[k], lsem.at[k], scatter=True)
                     for k in range(nr)]

        @pl.when(i == 0)
        def _():
            for e in exchanges:
                e.start()
            for r in (dgkv_ref, dbkv_ref, dlng_ref, dlnb_ref, dws_ref, dbs_ref, acco_scr, acck_scr):
                r[...] = jnp.zeros_like(r)

        dk_pre = _rot_bwd(dk_ref[...], rc_ref[...], rs1_ref[...], rs2_ref[...])
        dkv = jnp.concatenate([dk_pre, dv_ref[...]], axis=1)
        dbkv_ref[...] += jnp.sum(dkv, axis=0, keepdims=True)
        dkv_b = dkv.astype(BF)
        h1v = h1_ref[...]
        rkv = lax.rsqrt(jnp.mean(h1v * h1v, axis=-1, keepdims=True) + EPS)
        xh_kv = h1v * rkv
        gkv = gkv_ref[...]
        acck_scr[...] += _dot((xh_kv * gkv).astype(BF).T, dkv_b)
        dnkv = _dot_nt(dkv_b, wkv_ref[...])
        dgkv_ref[...] += jnp.sum(dnkv * xh_kv, axis=0, keepdims=True)
        dh1 = dh1p_ref[...] + _rms_bwd(dnkv, xh_kv, rkv, gkv)
        dh1_b = dh1.astype(BF)
        dh1f_ref[...] = dh1
        dy = _dot_nt(dh1_b, waout_ref[...]).astype(BF)
        silu, dsilu = _silu_parts(gt_ref[...].astype(F32))
        silu, dsilu = silu.astype(BF), dsilu.astype(BF)
        ub, svb = u_ref[...], sv_ref[...]
        us = ub * silu
        dys = dy * svb
        acco_scr[...] += _dot((us * svb).T, dh1_b)
        dz_ref[:, :AW] = dys * silu
        dz_ref[:, 2 * AW:] = dys * ub * dsilu
        dsv_scr[...] = dy * us
        vhat_v = vhat_ref[...].astype(F32)
        lng = lng_ref[...]
        vln_b = (vhat_v * lng + lnb_ref[...]).astype(BF)
        tri = lax.broadcasted_iota(jnp.int32, (CHUNK, CHUNK), 0) >= lax.broadcasted_iota(jnp.int32, (CHUNK, CHUNK), 1)
        lane = lax.broadcasted_iota(jnp.int32, (CHUNK, LANES), 1)
        dbs = jnp.zeros((CHUNK, LANES), F32)
        for g in range(G):
            wsm = jnp.where(tri, ws_ref[g], 0.0).astype(BF)
            cols = slice(g * CHUNK, (g + 1) * CHUNK)
            dws_g = None
            for c in range(nC):
                rows = slice(c * CHUNK, (c + 1) * CHUNK)
                dsv_cg = dsv_scr[rows, cols]
                dvln_scr[rows, cols] = _dot_tn(wsm, dsv_cg)
                part = _dot_nt(dsv_cg, vln_b[rows, cols])
                dws_g = part if dws_g is None else dws_g + part
                dbs += jnp.where(lane == g, jnp.sum(dsv_cg.astype(F32), axis=-1, keepdims=True), 0.0)
            dws_ref[g] += jnp.where(tri, dws_g, 0.0)
        dbs_ref[...] += dbs
        dvln = dvln_scr[...]
        dlng_ref[...] += jnp.sum(dvln * vhat_v, axis=0, keepdims=True)
        dlnb_ref[...] += jnp.sum(dvln, axis=0, keepdims=True)
        a = dvln * lng
        dvv = rstd_ref[:, 0:1] * (a - jnp.mean(a, axis=-1, keepdims=True)
                                  - vhat_v * jnp.mean(a * vhat_v, axis=-1, keepdims=True))
        dz_ref[:, AW:2 * AW] = dvv.astype(BF)

        @pl.when(i == nT - 1)
        def _():
            for j in range(N_DEV):
                gwo_ref[j] = acco_scr[j * (AW // N_DEV):(j + 1) * (AW // N_DEV)].astype(BF)
                gwk_ref[j] = acck_scr[j * (D // N_DEV):(j + 1) * (D // N_DEV)].astype(BF)
            for e in exchanges:
                e.finish()

    row = functools.partial(_row_spec, TM)
    hbm = pl.BlockSpec(memory_space=pl.ANY)
    S = jax.ShapeDtypeStruct
    gwo_shape, gwk_shape = (N_DEV, AW // N_DEV, D), (N_DEV, D // N_DEV, 2 * LANES)
    return pl.pallas_call(
        body, name="a_bwd", grid=(nT,),
        in_specs=[row(D), row(LANES), row(LANES), row(D), _const_spec((1, D)), _const_spec(w_kv.shape),
                  _const_spec(wa_out.shape), _const_spec(ws.shape),
                  _const_spec((1, AW)), _const_spec((1, AW)), pl.BlockSpec((TM, AW), lambda i: (i, 0)),
                  pl.BlockSpec((TM, AW), lambda i: (i, 2)), row(AW), row(AW), row(LANES),
                  row(LANES), row(LANES), row(LANES)] + [hbm] * nr,
        out_specs=[row(3 * AW), _const_spec(gwo_shape), _const_spec(gwk_shape), row(D),
                   _acc_spec((1, D)), _acc_spec((1, 2 * LANES)), _acc_spec((1, AW)),
                   _acc_spec((1, AW)), _acc_spec(ws.shape), _acc_spec((CHUNK, LANES))] + [hbm] * nr,
        out_shape=(S((T, 3 * AW), BF), S(gwo_shape, BF), S(gwk_shape, BF), S((T, D), F32),
                   S((1, D), F32), S((1, 2 * LANES), F32), S((1, AW), F32), S((1, AW), F32),
                   S(ws.shape, F32), S((CHUNK, LANES), F32)) + tuple(S(r.shape, r.dtype) for r in ready),
        scratch_shapes=[pltpu.VMEM((TM, AW), BF), pltpu.VMEM((TM, AW), F32), pltpu.VMEM((AW, D), F32),
                        pltpu.VMEM((D, 2 * LANES), F32)] + _direct_sems(nr),
        compiler_params=_params(("arbitrary",)),
    )(dh1p, dk, dv, h1, g_kv, w_kv, wa_out, ws, ln_g, ln_b, z, z, sv, vhat, rstd, rc, rs1, rs2, *ready)


def _a_in_bwd(dz, wa_in_t, x, dh1, g_a, ready):
    T, D = x.shape
    TM = min(512, T)
    nT = T // TM
    nr = len(ready)

    def body(dz_ref, wain_ref, x_ref, dh1_ref, ga_ref, *rest):
        ready_refs, (dx_ref, n1_ref, dga_ref), rest = rest[:nr], rest[nr:nr + 3], rest[nr + 3:]
        recv_refs, (ssem, rsem, lsem) = rest[:nr], rest[nr:]
        i = pl.program_id(0)
        exchanges = [_Direct(ready_refs[k], recv_refs[k], ssem.at[k], rsem.at[k], lsem.at[k], scatter=True)
                     for k in range(nr)]

        @pl.when(i == 0)
        def _():
            for e in exchanges:
                e.start()
            dga_ref[...] = jnp.zeros_like(dga_ref)

        xv = x_ref[...]
        r1 = lax.rsqrt(jnp.mean(xv * xv, axis=-1, keepdims=True) + EPS)
        xh = xv * r1
        ga = ga_ref[...]
        n1_ref[...] = (xh * ga).astype(BF).T
        dn1 = _dot(dz_ref[...], wain_ref[...])
        dga_ref[...] += jnp.sum(dn1 * xh, axis=0, keepdims=True)
        dx_ref[...] = dh1_ref[...] + _rms_bwd(dn1, xh, r1, ga)

        @pl.when(i == nT - 1)
        def _():
            for e in exchanges:
                e.finish()

    row = functools.partial(_row_spec, TM)
    hbm = pl.BlockSpec(memory_space=pl.ANY)
    S = jax.ShapeDtypeStruct
    return pl.pallas_call(
        body, name="a_in_bwd", grid=(nT,),
        in_specs=[row(dz.shape[1]), _const_spec(wa_in_t.shape), row(D), row(D), _const_spec((1, D))] + [hbm] * nr,
        out_specs=[row(D), _col_spec(TM, D), _acc_spec((1, D))] + [hbm] * nr,
        out_shape=(S((T, D), F32), S((D, T), BF), S((1, D), F32)) + tuple(S(r.shape, r.dtype) for r in ready),
        scratch_shapes=_direct_sems(nr),
        compiler_params=_params(("arbitrary",)),
    )(dz, wa_in_t, x, dh1, g_a, *ready)


def _wgrad(at, b, nblk, name, bt=512):
    K, T = at.shape
    N = b.shape[1] // nblk
    BT = min(bt, T)
    nt = T // BT

    def body(a_ref, b_ref, o_ref, acc):
        t = pl.program_id(0)

        @pl.when(t == 0)
        def _():
            acc[...] = jnp.zeros_like(acc)

        acc[...] += _dot(a_ref[...], b_ref[...])

        @pl.when(t == nt - 1)
        def _():
            for j in range(nblk):
                o_ref[j] = acc[:, j * N:(j + 1) * N].astype(BF)

    return pl.pallas_call(
        body, name=name, grid=(nt,),
        in_specs=[pl.BlockSpec((K, BT), lambda t: (0, t)), pl.BlockSpec((BT, nblk * N), lambda t: (t, 0))],
        out_specs=pl.BlockSpec((nblk, K, N), lambda t: (0, 0, 0)),
        out_shape=jax.ShapeDtypeStruct((nblk, K, N), BF),
        scratch_shapes=[pltpu.VMEM((K, nblk * N), F32)],
        compiler_params=_params(("arbitrary",)),
    )(at, b)


def _wgrad_exchange(a, b, me, small, name):
    K, T = a.shape
    N = b.shape[1] // N_DEV
    BT = T
    nt = T // BT
    last = N_DEV - 1
    n_chip = N_DEV // 2

    def far_of(k, core):
        return jnp.where((core == 0) & ((k == 1) | (k == 2)), k, n_chip - 1 - k)

    def block_of(s, me_i):
        k, odd = s // 2, s % 2
        core = me_i & 1
        return me_i ^ ((far_of(k, jnp.where(odd == 1, core, 1 - core)) << 1) | (1 - odd))

    H = K // 2

    def body(me_ref, a_ref, b_ref, small_ref, recv_ref, full_ref, *scratch):
        (acc, dstage, istage, half, relay, d_s, d_r, i_s, i_r, r_s, r_r, lsem, parts_scr, red_scr, e_s, e_r, e_l, g_s,
         g_r, g_l) = scratch
        s, t = pl.program_id(0), pl.program_id(1)
        x, y, c = (lax.axis_index(ax) for ax in AXES)
        ex = [_Direct(small_ref, parts_scr, e_s, e_r, e_l, scatter=True)]
        regather = _TwoLevel(red_scr, full_ref, g_s, g_r, g_l)

        def to_sibling(k, slot):
            return pltpu.make_async_remote_copy(src_ref=dstage.at[slot], dst_ref=half.at[k], send_sem=d_s.at[k],
                                                recv_sem=d_r.at[k], device_id=(x, y, 1 - c), device_id_type=MESH)

        def to_chip(k, slot):
            over_x = far_of(k, c) == 2
            px, py = jnp.where(over_x, 1 - x, x), jnp.where(over_x, y, 1 - y)
            return pltpu.make_async_remote_copy(src_ref=istage.at[slot], dst_ref=recv_ref.at[jnp.where(over_x, 1, 2)],
                                                send_sem=i_s.at[k], recv_sem=i_r.at[k], device_id=(px, py, c),
                                                device_id_type=MESH)

        def to_relay(j, slot):
            to = (1 - x, y, c) if j == 0 else (x, 1 - y, c)
            return pltpu.make_async_remote_copy(src_ref=istage.at[slot, pl.ds(j * H, H)], dst_ref=relay.at[j],
                                                send_sem=r_s.at[j], recv_sem=r_r.at[j], device_id=to,
                                                device_id_type=MESH)

        @pl.when((s == 0) & (t == 0))
        def _():
            for e in ex:
                e.start()

        acc[...] = _dot(a_ref[...], b_ref[...])

        @pl.when(t == nt - 1)
        def _():
            k = lax.div(s, 2)
            slot = lax.rem(k, 2)

            @pl.when(lax.rem(s, 2) == 0)
            def _():
                @pl.when(k >= 2)
                def _():
                    to_sibling(k - 2, slot).wait_send()

                dstage[slot] = acc[...].astype(BF)
                to_sibling(k, slot).start()

            @pl.when(lax.rem(s, 2) == 1)
            def _():
                to_sibling(k, slot).wait_recv()
                pair = acc[...] + half[k].astype(F32)

                @pl.when(k == 0)
                def _():
                    istage[slot] = pair.astype(BF)
                    for j in range(2):
                        to_relay(j, slot).start()

                @pl.when(k == 1)
                def _():
                    for j in range(2):
                        to_relay(j, slot).wait_recv()

                @pl.when(k == 2)
                def _():
                    for j in range(2):
                        to_relay(j, slot).wait_send()

                @pl.when(k == n_chip - 1)
                def _():
                    to_chip(1, slot).wait_send()
                    istage[slot] = pair.astype(BF)

                @pl.when((k == 1) | (k == 2))
                def _():
                    over_x = far_of(k, c) == 2
                    istage[slot, 0:H] = (pair[:H] + jnp.where(over_x, 0.0, relay[0].astype(F32))).astype(BF)
                    istage[slot, H:K] = (pair[H:] + jnp.where(over_x, relay[1].astype(F32), 0.0)).astype(BF)
                    to_chip(k, slot).start()

            @pl.when(s == last)
            def _():
                own = pltpu.make_async_copy(istage.at[slot], recv_ref.at[0], lsem)
                own.start()
                to_chip(2, 0).wait_send()
                to_sibling(n_chip - 2, 0).wait_send()
                to_sibling(n_chip - 1, 1).wait_send()
                for kk in (1, 2):
                    to_chip(kk, 0).wait_recv()
                own.wait()
                for e in ex:
                    e.finish()
                total = parts_scr[0]
                for dev in range(1, N_DEV):
                    total = total + parts_scr[dev]
                red_scr[...] = total
                regather.start()
                regather.forward()
                regather.finish()

    hbm = pl.BlockSpec(memory_space=pl.ANY)
    dma = pltpu.SemaphoreType.DMA
    grid_spec = pltpu.PrefetchScalarGridSpec(
        num_scalar_prefetch=1, grid=(N_DEV, nt),
        in_specs=[pl.BlockSpec((K, BT), lambda s, t, me_ref: (0, t), pipeline_mode=pl.Buffered(1)),
                  pl.BlockSpec((BT, N), lambda s, t, me_ref: (t, block_of(s, me_ref[0]))), hbm],
        out_specs=[hbm, hbm],
        scratch_shapes=[pltpu.VMEM((K, N), F32), pltpu.VMEM((2, K, N), BF), pltpu.VMEM((2, K, N), BF),
                        pltpu.VMEM((n_chip, K, N), BF), pltpu.VMEM((2, H, N), BF), dma((n_chip,)), dma((n_chip,)),
                        dma((n_chip - 1,)), dma((n_chip - 1,)), dma((2,)), dma((2,)), dma,
                        pltpu.VMEM(small.shape, F32), pltpu.VMEM(small.shape[1:], F32),
                        dma((last,)), dma((last,)), dma, dma((last,)), dma((last,)), dma])
    return pl.pallas_call(
        body, name=name, grid_spec=grid_spec,
        out_shape=[jax.ShapeDtypeStruct((n_chip - 1, K, N), BF), jax.ShapeDtypeStruct(small.shape, F32)],
        compiler_params=_params(("arbitrary", "arbitrary")),
    )(me, a, b, small)


def _my_index():
    return 4 * lax.axis_index("x") + 2 * lax.axis_index("y") + lax.axis_index("c")


def _peer(mask):
    x, y, c = (lax.axis_index(a) for a in AXES)
    return (x ^ ((mask >> 2) & 1), y ^ ((mask >> 1) & 1), c ^ (mask & 1))


def _dev_index(p):
    return 4 * p[0] + 2 * p[1] + p[2]


class _Direct:
    def __init__(self, src, dst, send_sems, recv_sems, local_sem, scatter):
        me = _my_index()
        self.own = pltpu.make_async_copy(src.at[me] if scatter else src, dst.at[me], local_sem)
        self.sends, self.recvs = [], []
        for k in range(1, N_DEV):
            p = _peer(k)
            pi = _dev_index(p)
            sems = dict(send_sem=send_sems.at[k - 1], recv_sem=recv_sems.at[k - 1], device_id=p, device_id_type=MESH)
            self.sends.append(pltpu.make_async_remote_copy(src_ref=src.at[pi] if scatter else src, dst_ref=dst.at[me],
                                                           **sems))
            self.recvs.append(pltpu.make_async_remote_copy(src_ref=src.at[me] if scatter else src, dst_ref=dst.at[pi],
                                                           **sems))

    def start(self):
        self.own.start()
        for cp in self.sends:
            cp.start()

    def finish(self):
        for cp in self.sends:
            cp.wait_send()
        for cp in self.recvs:
            cp.wait_recv()
        self.own.wait()


class _TwoLevel:
    def __init__(self, src, dst, send_sems, recv_sems, local_sem, own=True):
        x, y, c = (lax.axis_index(a) for a in AXES)
        self.me, self.sibling = (x, y, c), (x, y, 1 - c)
        self.chips = [(1 - x, y), (x, 1 - y), (1 - x, 1 - y)]
        self.src, self.dst, self.send_sems, self.recv_sems = src, dst, send_sems, recv_sems
        self.own = pltpu.make_async_copy(src, dst.at[_dev_index(self.me)], local_sem) if own else None

    def _copy(self, k, block, to, from_src=False):
        slot = self.dst.at[_dev_index(block)]
        return pltpu.make_async_remote_copy(src_ref=self.src if from_src else slot, dst_ref=slot,
                                            send_sem=self.send_sems.at[k], recv_sem=self.recv_sems.at[k],
                                            device_id=to, device_id_type=MESH)

    def _firsts(self):
        c = self.me[2]
        return [self._copy(0, self.me, self.sibling, True)] + [self._copy(1 + j, self.me, (*chip, c), True)
                                                               for j, chip in enumerate(self.chips)]

    def _passed(self):
        c = self.me[2]
        return [self._copy(4 + j, (*chip, c), self.sibling) for j, chip in enumerate(self.chips)]

    def start(self):
        if self.own is not None:
            self.own.start()
        for cp in self._firsts():
            cp.start()

    def wait_sibling(self):
        self._copy(0, self.sibling, self.me).wait_recv()

    def wait_chip_and_forward(self, j):
        self._copy(1 + j, (*self.chips[j], self.me[2]), self.me).wait_recv()
        self._passed()[j].start()

    def wait_passed(self, j):
        self._copy(4 + j, (*self.chips[j], 1 - self.me[2]), self.me).wait_recv()

    def wait_sends(self):
        for cp in self._firsts() + self._passed():
            cp.wait_send()
        if self.own is not None:
            self.own.wait()

    def forward(self):
        for j in range(3):
            self.wait_chip_and_forward(j)

    def finish(self):
        self.wait_sibling()
        for j in range(3):
            self.wait_passed(j)
        self.wait_sends()


class _RelayGather:
    def __init__(self, dst, send_sems, recv_sems):
        x, y, c = (lax.axis_index(a) for a in AXES)
        self.c = c
        self.sib, self.xn, self.yn, self.dg = (x, y, 1 - c), (1 - x, y, c), (x, 1 - y, c), (1 - x, 1 - y, c)
        self.me = (x, y, c)
        self.dst, self.send_sems, self.recv_sems = dst, send_sems, recv_sems
        self.half = dst.shape[1] // 2

    def _slot(self, dev, part=None):
        i = _dev_index(dev)
        if part is None:
            return self.dst.at[i]
        return self.dst.at[i, pl.ds(part * self.half, self.half)]

    def _copy(self, k, dev, to, part=None):
        ref = self._slot(dev, part)
        return pltpu.make_async_remote_copy(src_ref=ref, dst_ref=ref, send_sem=self.send_sems.at[k],
                                            recv_sem=self.recv_sems.at[k], device_id=to, device_id_type=MESH)

    def _other(self, dev):
        return (dev[0], dev[1], 1 - self.c)

    def start(self):
        for k, to in enumerate((self.sib, self.xn, self.yn)):
            self._copy(k, self.me, to).start()

    def send_own(self, k):
        return self._copy(k, self.me, (self.sib, self.xn, self.yn)[k])

    def wait_sibling(self):
        self._copy(0, self.sib, self.me).wait_recv()

    def on_x(self):
        self._copy(1, self.xn, self.me).wait_recv()
        self._copy(3, self.xn, self.yn, part=0).start()
        self._copy(5, self.xn, self.sib).start()

    def on_y(self):
        self._copy(2, self.yn, self.me).wait_recv()
        self._copy(4, self.yn, self.xn, part=1).start()
        self._copy(6, self.yn, self.sib).start()

    def on_diag(self):
        self._copy(3, self.dg, self.me, part=0).wait_recv()
        self._copy(4, self.dg, self.me, part=1).wait_recv()
        self._copy(7, self.dg, self.sib).start()

    def wait_passed(self, j):
        self._copy(5 + j, self._other((self.xn, self.yn, self.dg)[j]), self.me).wait_recv()

    def wait_sends(self):
        for k, to in enumerate((self.sib, self.xn, self.yn)):
            self._copy(k, self.me, to).wait_send()
        self._copy(3, self.xn, self.yn, part=0).wait_send()
        self._copy(4, self.yn, self.xn, part=1).wait_send()
        for j, dev in enumerate((self.xn, self.yn, self.dg)):
            self._copy(5 + j, dev, self.sib).wait_send()


def _direct_sems(n):
    if n == 0:
        return []
    return [pltpu.SemaphoreType.DMA((n, 7)), pltpu.SemaphoreType.DMA((n, 7)), pltpu.SemaphoreType.DMA((n,))]


def _adam_math(w, g, m, v):
    m = ADAM_B1 * m + (1.0 - ADAM_B1) * g
    v = ADAM_B2 * v + (1.0 - ADAM_B2) * (g * g)
    m_hat = m / (1.0 - ADAM_B1 ** ADAM_STEP)
    v_hat = v / (1.0 - ADAM_B2 ** ADAM_STEP)
    delta = -ADAM_LR * (m_hat / (jnp.sqrt(v_hat) + ADAM_EPS) + ADAM_WD * w)
    return delta, m, v


def _sum_adam(parts, w, m, v, name):
    R, C = w.shape
    NP = parts.shape[0]
    BR = 4 * CHUNK if R % (4 * CHUNK) == 0 else R

    def body(p_ref, w_ref, m_ref, v_ref, g_ref, d_ref, nm_ref, nv_ref):
        g = p_ref[0].astype(F32)
        for i in range(1, NP):
            g = g + p_ref[i].astype(F32)
        g_ref[...] = g
        d_ref[...], nm_ref[...], nv_ref[...] = _adam_math(w_ref[...], g, m_ref[...], v_ref[...])

    blk = pl.BlockSpec((BR, C), lambda i: (i, 0))
    S = jax.ShapeDtypeStruct((R, C), F32)
    return pl.pallas_call(
        body, name=name, grid=(R // BR,),
        in_specs=[pl.BlockSpec((NP, BR, C), lambda i: (0, i, 0)), blk, blk, blk],
        out_specs=[blk] * 4, out_shape=(S,) * 4,
        compiler_params=_params(("arbitrary",)),
    )(parts, w, m, v)


SUBLANES = 8


def _nrows(size):
    return -(-size // (SUBLANES * LANES)) * SUBLANES


def _view2d(a):
    return a.reshape(-1, LANES) if a.size % LANES == 0 else a.reshape(1, -1)


def _pack_small(parts, total_rows, name):
    arrs = [p[0] for p in parts]

    def body(*refs):
        out = refs[-1]
        out[...] = jnp.zeros_like(out)
        at = 0
        for ref, (a, rows, flag) in zip(refs[:-1], parts):
            val = ref[...].T if flag == "T" else ref[...]
            r, c = (rows, val.shape[1]) if flag == "T" else val.shape
            out[at:at + r, 0:c] = val[:r]
            at += _nrows(r * c)

    return pl.pallas_call(body, name=name, out_shape=jax.ShapeDtypeStruct((total_rows, LANES), F32))(*arrs)


def _small_update(full, me, reps, shards, name):
    n = len(reps) + len(shards)

    def body(me_ref, full_ref, *refs):
        ins, outs = refs[:3 * n], refs[3 * n:]
        at = 0
        for k in range(n):
            w_ref, m_ref, v_ref = ins[3 * k:3 * k + 3]
            r, c = w_ref.shape
            if k < len(reps):
                g = full_ref[at:at + r, 0:c]
                at += _nrows(r * c)
            else:
                seg = full_ref[at:at + N_DEV * r, :]
                row = lax.broadcasted_iota(jnp.int32, seg.shape, 0)
                pick = [jnp.sum(jnp.where(row == r * me_ref[0] + t, seg, 0.0), axis=0, keepdims=True) for t in range(r)]
                g = pick[0] if r == 1 else jnp.concatenate(pick, axis=0)
                at += N_DEV * r
            g_ref, d_ref, nm_ref, nv_ref = outs[4 * k:4 * k + 4]
            g_ref[...] = g
            d_ref[...], nm_ref[...], nv_ref[...] = _adam_math(w_ref[...], g, m_ref[...], v_ref[...])
        outs[4 * n][...] = full_ref[at:at + 1, 0:1]

    flat = [t for p in reps + shards for t in p]
    S = jax.ShapeDtypeStruct
    res = pl.pallas_call(
        body, name=name,
        in_specs=[pl.BlockSpec(memory_space=pltpu.SMEM)] + [pl.BlockSpec(memory_space=pltpu.VMEM)] * (1 + len(flat)),
        out_shape=[S(p[0].shape, F32) for p in reps + shards for _ in range(4)] + [S((1, 1), F32)],
    )(me, full, *flat)
    return [tuple(res[4 * k:4 * k + 4]) for k in range(n)], res[4 * n]


def _rope_tables(T):
    pos = np.arange(T, dtype=np.float32)
    inv_freq = (np.float64(ROPE_THETA) ** (-np.arange(0, HEAD_DIM, 2, dtype=np.float64) / HEAD_DIM)).astype(np.float32)
    ang = (pos[:, None] * inv_freq[None, :]).astype(np.float64)
    cos, sin, zero = np.cos(ang).astype(np.float32), np.sin(ang).astype(np.float32), np.zeros(ang.shape, np.float32)
    c = np.concatenate([cos, cos, cos, cos], axis=1)
    s1 = np.concatenate([-sin, zero, -sin, zero], axis=1)
    s2 = np.concatenate([zero, sin, zero, sin], axis=1)
    return jnp.asarray(c), jnp.asarray(s1), jnp.asarray(s2)


def kernel(x, a_norm_g, a_w_in, a_ln_g, a_ln_b, a_ws, a_bs, a_w_out, kv_norm_g, w_kv, b_kv, b_norm_g, b_w_in, b_bq, b_sinks, b_w_out, final_norm_g, loss_target, m_a_norm_g, m_a_w_in, m_a_ln_g, m_a_ln_b, m_a_ws, m_a_bs, m_a_w_out, m_kv_norm_g, m_w_kv, m_b_kv, m_b_norm_g, m_b_w_in, m_b_bq, m_b_sinks, m_b_w_out, m_final_norm_g, v_a_norm_g, v_a_w_in, v_a_ln_g, v_a_ln_b, v_a_ws, v_a_bs, v_a_w_out, v_kv_norm_g, v_w_kv, v_b_kv, v_b_norm_g, v_b_w_in, v_b_bq, v_b_sinks, v_b_w_out, v_final_norm_g):
    T, D = x.shape[1], x.shape[2]
    AW = a_ln_g.shape[1] * N_DEV
    G = a_ws.shape[1]
    assert w_kv.shape[1] == 2 * LANES and a_ws.shape[2] == CHUNK and T % CHUNK == 0
    me = _my_index()

    xs, tgt = x[0], loss_target[0]
    z, wa_in_t, g_a, ln_g, ln_b, wa_out, wkv = _in_proj(xs, a_w_in[0], [a_norm_g, a_ln_g, a_ln_b], me.reshape(1),
                                                        [a_w_out[0], w_kv])
    wa_in_t = wa_in_t.reshape(-1, D)
    wa_out = wa_out.reshape(AW, D)
    wkv = wkv.reshape(D, 2 * LANES)

    rc, rs1, rs2 = _rope_tables(T)
    ws = a_ws[0]
    bs_t = a_bs[0].T
    g_kv = kv_norm_g.reshape(1, D)
    bkv = b_kv.reshape(1, -1)
    g_f = final_norm_g.reshape(1, D)
    sinks = b_sinks.reshape(1, 16)
    h1, sv, vhat, rstd, k4, v4, kt, vt, wb_in, wb_out = _a_fwd(
        xs, z, ln_g, ln_b, ws, bs_t, wa_out, g_kv, wkv, bkv, rc, rs1, rs2, [b_w_in[0], b_w_out[0]])
    wb_out = wb_out.reshape(-1, D)
    q, g2, o, dh2, dh2_b, loss, d_gf = _b_fwd(h1, b_norm_g, wb_in, b_bq, rc, rs1, rs2, k4, vt, sinks, wb_out, g_f, tgt)
    dh1p, dz2, n2, y2, dk, dv, d_bq, d_gb, d_sink = _b_bwd(dh2, h1, q, g2, o, k4, v4, kt, sinks, wb_out, wb_in,
                                                           b_norm_g, rc, rs1, rs2)
    d_sink = d_sink[:, :4].reshape(2, 2, 4).transpose(0, 2, 1).reshape(1, 16)
    gw_b_in = _wgrad(n2, dz2, N_DEV, "wgrad_b_in", bt=1024)
    gw_b_out = _wgrad(y2, dh2_b, 1, "wgrad_b_out", bt=1024).reshape(N_DEV, -1, D)
    (dz, gw_a_out, gw_kv, dh1_f, d_gkv, d_bkv, d_lng, d_lnb, d_ws, d_bst, r_b_in, r_b_out) = _a_bwd(
        dh1p, dk, dv, h1, g_kv, wkv, wa_out, ws, ln_g, ln_b, z, sv, vhat, rstd, rc, rs1, rs2, [gw_b_in, gw_b_out])
    dx, n1, d_ga, r_a_out, r_kv = _a_in_bwd(dz, wa_in_t, xs, dh1_f, g_a, [gw_a_out, gw_kv])
    small = [(_view2d(d_ws), None, None), (d_bst, G, "T")] + [(_view2d(a), None, None) for a in (
        d_gkv, d_bkv, d_gb, d_bq, d_sink, d_gf, d_ga, d_lng, d_lnb, loss)]
    used = sum(_nrows(G * CHUNK if flag else a.size) for a, _, flag in small)
    per = -(-used // (SUBLANES * N_DEV)) * SUBLANES
    small_pack = _pack_small(small, per * N_DEV, "pack_small").reshape(N_DEV, per, LANES)
    r_a_in, full_small = _wgrad_exchange(n1, dz, me.reshape(1), small_pack, "wgrad_a_in")

    g_a_in, d_a_in, nm_a_in, nv_a_in = _sum_adam(r_a_in, a_w_in[0], m_a_w_in[0], v_a_w_in[0], "adam_a_in")
    g_a_out, d_a_out, nm_a_out, nv_a_out = _sum_adam(r_a_out, a_w_out[0], m_a_w_out[0], v_a_w_out[0], "adam_a_out")
    g_kvw, d_kvw, nm_kvw, nv_kvw = _sum_adam(r_kv, w_kv, m_w_kv, v_w_kv, "adam_kv")
    g_b_in, d_b_in, nm_b_in, nv_b_in = _sum_adam(r_b_in, b_w_in[0], m_b_w_in[0], v_b_w_in[0], "adam_b_in")
    g_b_out, d_b_out, nm_b_out, nv_b_out = _sum_adam(r_b_out, b_w_out[0], m_b_w_out[0], v_b_w_out[0], "adam_b_out")

    full_small = full_small.reshape(N_DEV * per, LANES)
    reps = [(a_ws, m_a_ws, v_a_ws), (a_bs, m_a_bs, v_a_bs), (kv_norm_g, m_kv_norm_g, v_kv_norm_g),
            (b_kv, m_b_kv, v_b_kv), (b_norm_g, m_b_norm_g, v_b_norm_g), (b_bq, m_b_bq, v_b_bq),
            (b_sinks, m_b_sinks, v_b_sinks), (final_norm_g, m_final_norm_g, v_final_norm_g)]
    shards = [(a_norm_g, m_a_norm_g, v_a_norm_g), (a_ln_g, m_a_ln_g, v_a_ln_g), (a_ln_b, m_a_ln_b, v_a_ln_b)]
    upd, loss = _small_update(full_small, me.reshape(1), [tuple(_view2d(t) for t in p) for p in reps],
                              [tuple(_view2d(t) for t in p) for p in shards], "adam_small")
    loss = loss[0, 0]
    sm_g, sd, snm, snv = ([upd[k][j].reshape(p[0].shape) for k, p in enumerate(reps + shards)] for j in range(4))

    def order(big, sm):
        a_in, a_out, kvw, b_in, b_out = big
        ws_, bs_, kvg, bkv_, bng, bq_, snk, fng, ang, alng, alnb = sm
        return (ang, a_in[None], alng, alnb, ws_, bs_, a_out[None], kvg, kvw, bkv_, bng, b_in[None], bq_, snk,
                b_out[None], fng)

    grads = order((g_a_in, g_a_out, g_kvw, g_b_in, g_b_out), sm_g)
    deltas = order((d_a_in, d_a_out, d_kvw, d_b_in, d_b_out), sd)
    new_m = order((nm_a_in, nm_a_out, nm_kvw, nm_b_in, nm_b_out), snm)
    new_v = order((nv_a_in, nv_a_out, nv_kvw, nv_b_in, nv_b_out), snv)
    return (loss, dx[None], *grads, *deltas, *new_m, *new_v)
```

```python
import functools

import jax
import jax.numpy as jnp
import numpy as np
from jax import lax
from jax.experimental import pallas as pl
from jax.experimental.pallas import tpu as pltpu

CHUNK = 128
HEAD_DIM = 64
ROPE_THETA = 10000.0
EPS = 1e-5
ADAM_LR = 0.001
ADAM_B1 = 0.9
ADAM_B2 = 0.999
ADAM_EPS = 1e-08
ADAM_WD = 0.01
ADAM_STEP = 10
N_DEV = 8
LANES = 128
NEG = -1e30

BF = jnp.bfloat16
F32 = jnp.float32
MESH = pl.DeviceIdType.MESH
AXES = ("x", "y", "c")
VMEM_LIMIT = 56 * 1024 * 1024


def _dot(a, b):
    return jnp.dot(a, b, preferred_element_type=F32)


def _dot_nt(a, b):
    return lax.dot_general(a, b, (((1,), (1,)), ((), ())), preferred_element_type=F32)


def _dot_tn(a, b):
    return lax.dot_general(a, b, (((0,), (0,)), ((), ())), preferred_element_type=F32)


def _const_spec(shape):
    nd = len(shape)
    return pl.BlockSpec(shape, lambda *_: (0,) * nd, pipeline_mode=pl.Buffered(1))


def _acc_spec(shape):
    nd = len(shape)
    return pl.BlockSpec(shape, lambda *_: (0,) * nd)


def _row_spec(tm, width):
    return pl.BlockSpec((tm, width), lambda i: (i, 0))


def _col_spec(tm, height):
    return pl.BlockSpec((height, tm), lambda i: (0, i))


def _params(sem):
    return pltpu.CompilerParams(dimension_semantics=sem, vmem_limit_bytes=VMEM_LIMIT)


def _rot(x, c, s1, s2):
    return x * c + pltpu.roll(x, 96, 1) * s1 + pltpu.roll(x, 32, 1) * s2


def _rot_bwd(d, c, s1, s2):
    return d * c + pltpu.roll(d * s1, 32, 1) + pltpu.roll(d * s2, 96, 1)


def _silu_parts(g):
    sg = jax.nn.sigmoid(g)
    return g * sg, sg * (1.0 + g * (1.0 - sg))


def _rms_bwd(dn, xh, r, g):
    a = dn * g
    return r * (a - xh * jnp.mean(a * xh, axis=-1, keepdims=True))


def _lane_lo(shape):
    return lax.broadcasted_iota(jnp.int32, shape, 1) < HEAD_DIM


def _split4(t):
    lo = _lane_lo(t.shape)
    tr = pltpu.roll(t, HEAD_DIM, 1)
    z = jnp.zeros_like(t)
    return jnp.concatenate([jnp.where(lo, t, z), jnp.where(lo, z, tr), jnp.where(lo, tr, z), jnp.where(lo, z, t)], axis=1)


def _stack_pairs(t, h):
    return jnp.concatenate([t[:, (h * 4 + j) * LANES:(h * 4 + j + 1) * LANES] for j in range(4)], axis=0)


def _upper():
    shape = (CHUNK, 4 * CHUNK)
    return lax.broadcasted_iota(jnp.int32, shape, 0) > (lax.broadcasted_iota(jnp.int32, shape, 1) & (CHUNK - 1))


def _band_rows(tile_ref, before_ref, c, h):
    a = slice(2 * h * LANES, (2 * h + 1) * LANES)
    b = slice((2 * h + 1) * LANES, (2 * h + 2) * LANES)
    cur = slice(c * CHUNK, (c + 1) * CHUNK)

    def prev(cols):
        return before_ref[:, cols] if c == 0 else tile_ref[(c - 1) * CHUNK:c * CHUNK, cols]

    return jnp.concatenate([prev(a), tile_ref[cur, a], prev(b), tile_ref[cur, b]], axis=0)


def _band_cols(tile_ref, before_ref, c, h):
    a = slice(2 * h * LANES, (2 * h + 1) * LANES)
    b = slice((2 * h + 1) * LANES, (2 * h + 2) * LANES)

    def prev(rows):
        return before_ref[0, rows, :] if c == 0 else tile_ref[c - 1, rows, :]

    return jnp.concatenate([prev(a), tile_ref[c, a, :], prev(b), tile_ref[c, b, :]], axis=1)


def _band_specs(tm):
    nc = tm // CHUNK

    def before(i):
        return jnp.maximum(i * nc - 1, 0)

    return (pl.BlockSpec((tm, 4 * LANES), lambda i: (i, 0)),
            pl.BlockSpec((CHUNK, 4 * LANES), lambda i: (before(i), 0)),
            pl.BlockSpec((nc, 4 * LANES, CHUNK), lambda i: (i, 0, 0)),
            pl.BlockSpec((1, 4 * LANES, CHUNK), lambda i: (before(i), 0, 0)))


def _fold(t, upper, has_prev=None):
    out = []
    for k in range(2):
        prev = t[2 * k * CHUNK:(2 * k + 1) * CHUNK]
        if has_prev is not None:
            prev = jnp.where(has_prev, prev, NEG)
        out.append(jnp.where(upper, prev, t[(2 * k + 1) * CHUNK:(2 * k + 2) * CHUNK]))
    return out


def _unfold(fa, fb, upper):
    z = jnp.zeros_like(fa)
    return jnp.concatenate([jnp.where(upper, fa, z), jnp.where(upper, z, fa),
                            jnp.where(upper, fb, z), jnp.where(upper, z, fb)], axis=0)


def _sink_tile(s_ref):
    shape = (4, 4 * LANES)
    row = lax.broadcasted_iota(jnp.int32, shape, 0)
    pair = lax.broadcasted_iota(jnp.int32, shape, 1) // LANES
    idx = (row // 2) * 8 + pair * 2 + row % 2
    tile = jnp.zeros(shape, F32)
    for n in range(16):
        tile = jnp.where(idx == n, s_ref[0, n], tile)
    return tile


def _softmax_sink(f, sink):
    m = jnp.maximum(jnp.max(f, axis=0, keepdims=True), sink)
    p = jnp.exp(f - m)
    es = jnp.exp(sink - m)
    inv = 1.0 / (jnp.sum(p, axis=0, keepdims=True) + es)
    return p * inv, es * inv


class _Riding:
    def __init__(self, shards, gathered, stages, sems, n_steps):
        self.shards, self.stages, self.n_steps = shards, stages, n_steps
        ssem, rsem, lsem = sems
        self.gathers = [_TwoLevel(stages[k], gathered[k], ssem.at[k], rsem.at[k], lsem.at[k])
                        for k in range(len(shards))]

    def begin(self, i):
        @pl.when(i == 0)
        def _():
            for shard, stage, g in zip(self.shards, self.stages, self.gathers):
                stage[...] = shard[...].astype(stage.dtype)
                g.start()

    def end(self, i):
        @pl.when(i == self.n_steps // 2)
        def _():
            for g in self.gathers:
                g.forward()

        @pl.when(i == self.n_steps - 1)
        def _():
            for g in self.gathers:
                g.finish()

    @staticmethod
    def specs(later):
        nl = len(later)
        hbm = pl.BlockSpec(memory_space=pl.ANY)
        return ([_const_spec(w.shape) for w in later], [hbm] * nl,
                tuple(jax.ShapeDtypeStruct((N_DEV,) + w.shape, BF) for w in later),
                [pltpu.VMEM(w.shape, BF) for w in later] + _direct_sems(nl))


PASS_MASKS = ((0, 1, 2, 5, 4, 3, 6, 7), (0, 1, 4, 3, 2, 5, 6, 7))


def _in_proj(x, w_shard, vec_shards, me, later):
    T, D = x.shape
    SH = w_shard.shape[1]
    TM = min(1024, T)
    nT = T // TM
    nl = len(later)
    nv = len(vec_shards)
    widths = [v.shape[1] for v in vec_shards]
    offsets = [sum(widths[:k]) for k in range(nv)]
    vec_shape = (SUBLANES, sum(widths))
    ds = widths[0]
    last = N_DEV - 1
    masks = jnp.asarray(np.array(PASS_MASKS, np.int32).reshape(-1))

    def slot(p, me_ref, masks_ref):
        return me_ref[0] ^ masks_ref[(me_ref[0] & 1) * N_DEV + p]

    def body(me_ref, masks_ref, x_ref, wsh_ref, *rest):
        vsh_refs, rest = rest[:nv], rest[nv:]
        shards, rest = rest[:nl], rest[nl:]
        (z_ref, wt_ref), rest = rest[:2], rest[2:]
        vout_refs, rest = rest[:nv], rest[nv:]
        gathered, rest = rest[:nl], rest[nl:]
        (w_scr, vec_scr, vstage, n1_scr, ga_scr, w_s, w_r, v_s, v_r, v_l), rest = rest[:10], rest[10:]
        stages, sems = rest[:nl], rest[nl:]
        p, i = pl.program_id(0), pl.program_id(1)
        me = _my_index()
        wg = _RelayGather(w_scr, w_s, w_r)
        vg = _Direct(vstage, vec_scr, v_s, v_r, v_l, scatter=False)
        lg = [_TwoLevel(stages[k], gathered[k], sems[0].at[k], sems[1].at[k], sems[2].at[k]) for k in range(nl)]

        def at_pass(k):
            return (p == k) & (i == 0)

        c = lax.axis_index("c")

        @pl.when(at_pass(0))
        def _():
            for ref, off, wd in zip(vsh_refs, offsets, widths):
                vstage[:, off:off + wd] = jnp.broadcast_to(ref[...], (SUBLANES, wd))
            vg.start()
            w_scr[me] = wsh_ref[...].astype(BF)
            wg.send_own(0).start()

            @pl.when(c == 1)
            def _():
                wg.send_own(1).start()

            @pl.when(c == 0)
            def _():
                wg.send_own(2).start()

            vg.finish()
            for j in range(N_DEV):
                ga_scr[:, j * ds:(j + 1) * ds] = vec_scr[j, 0:1, 0:ds]
                for ref, off, wd in zip(vout_refs, offsets, widths):
                    ref[:, j * wd:(j + 1) * wd] = vec_scr[j, 0:1, off:off + wd]

        @pl.when(at_pass(1))
        def _():
            wg.wait_sibling()

        for first, second, landed_first, landed_second in ((1, 2, wg.on_x, wg.on_y), (2, 1, wg.on_y, wg.on_x)):
            mine = c == (1 if first == 1 else 0)

            @pl.when(at_pass(2) & mine)
            def _(second=second, landed_first=landed_first):
                wg.send_own(second).start()
                landed_first()

            @pl.when(at_pass(3) & mine)
            def _(second=second):
                wg.wait_passed(second - 1)

            @pl.when(at_pass(4) & mine)
            def _(landed_second=landed_second):
                landed_second()

            @pl.when(at_pass(5) & mine)
            def _(first=first):
                wg.wait_passed(first - 1)

        @pl.when(at_pass(4))
        def _():
            for k in range(nl):
                stages[k][...] = shards[k][...].astype(BF)
                lg[k].start()

        @pl.when(at_pass(6))
        def _():
            wg.on_diag()

        @pl.when(at_pass(7))
        def _():
            wg.wait_passed(2)

        @pl.when(p == 0)
        def _():
            xv = x_ref[...]
            r1 = lax.rsqrt(jnp.mean(xv * xv, axis=-1, keepdims=True) + EPS)
            n1_scr[i] = (xv * r1 * ga_scr[...]).astype(BF)

        z_ref[...] = _dot(n1_scr[i], w_scr[slot(p, me_ref, masks_ref)]).astype(BF)

        @pl.when(i == 0)
        def _():
            wt_ref[0] = w_scr[slot(p, me_ref, masks_ref)].T

        @pl.when((p == last) & (i == nT - 1))
        def _():
            wg.wait_sends()
            for g in lg:
                g.forward()
            for g in lg:
                g.finish()

    hbm = pl.BlockSpec(memory_space=pl.ANY)
    dma = pltpu.SemaphoreType.DMA
    S = jax.ShapeDtypeStruct
    def whole(shape):
        return pl.BlockSpec(shape, lambda p, i, m, t: (0, 0))

    def once(shape):
        return pl.BlockSpec(shape, lambda p, i, m, t: (0, 0), pipeline_mode=pl.Buffered(1))

    grid_spec = pltpu.PrefetchScalarGridSpec(
        num_scalar_prefetch=2, grid=(N_DEV, nT),
        in_specs=[pl.BlockSpec((TM, D), lambda p, i, m, t: (jnp.where(p == 0, i, nT - 1), 0)), once(w_shard.shape)]
        + [once(v.shape) for v in vec_shards] + [once(w.shape) for w in later],
        out_specs=[pl.BlockSpec((TM, SH), lambda p, i, m, t: (i, slot(p, m, t))),
                   pl.BlockSpec((1, SH, D), lambda p, i, m, t: (slot(p, m, t), 0, 0))]
        + [whole((1, N_DEV * wd)) for wd in widths] + [hbm] * nl,
        scratch_shapes=[pltpu.VMEM((N_DEV, D, SH), BF), pltpu.VMEM((N_DEV,) + vec_shape, F32),
                        pltpu.VMEM(vec_shape, F32), pltpu.VMEM((nT, TM, D), BF), pltpu.VMEM((1, D), F32),
                        dma((8,)), dma((8,)), dma((7,)), dma((7,)), dma]
        + [pltpu.VMEM(w.shape, BF) for w in later] + _direct_sems(nl))
    return pl.pallas_call(
        body, name="a_in_proj", grid_spec=grid_spec,
        out_shape=(S((T, N_DEV * SH), BF), S((N_DEV, SH, D), BF)) + tuple(S((1, N_DEV * wd), F32) for wd in widths)
        + tuple(S((N_DEV,) + w.shape, BF) for w in later),
        compiler_params=_params(("arbitrary", "arbitrary")),
    )(me, masks, x, w_shard, *vec_shards, *later)


def _a_fwd(x, z, ln_g, ln_b, ws, bs, wa_out, g_kv, w_kv, b_kv, rc, rs1, rs2, later):
    T, D = x.shape
    AW = wa_out.shape[0]
    G = ws.shape[0]
    TM = min(512, T)
    nT = T // TM
    nC = TM // CHUNK
    nl = len(later)

    def body(x_ref, u_ref, v_ref, gt_ref, lng_ref, lnb_ref, ws_ref, bs_ref, waout_ref, gkv_ref, wkv_ref, bkv_ref,
             rc_ref, rs1_ref, rs2_ref, *rest):
        shards, rest = rest[:nl], rest[nl:]
        (h1_ref, sv_ref, vhat_ref, rstd_ref, k4_ref, v4_ref, kt_ref, vt_ref), rest = rest[:8], rest[8:]
        gathered, sv_scr, stages, sems = rest[:nl], rest[nl], rest[nl + 1:2 * nl + 1], rest[2 * nl + 1:]
        i = pl.program_id(0)
        riding = _Riding(shards, gathered, stages, sems, nT)
        riding.begin(i)
        xv = x_ref[...]
        u = u_ref[...].astype(F32)
        v = v_ref[...].astype(F32)
        gt = gt_ref[...].astype(F32)
        mu = jnp.mean(v, axis=-1, keepdims=True)
        xc = v - mu
        rstd = lax.rsqrt(jnp.mean(xc * xc, axis=-1, keepdims=True) + EPS)
        vhat = xc * rstd
        vln = (vhat * lng_ref[...] + lnb_ref[...]).astype(BF)
        tri = lax.broadcasted_iota(jnp.int32, (CHUNK, CHUNK), 0) >= lax.broadcasted_iota(jnp.int32, (CHUNK, CHUNK), 1)
        bst = jnp.concatenate([bs_ref[...], jnp.zeros((CHUNK - G, CHUNK), F32)], axis=0).T
        for g in range(G):
            wsm = jnp.where(tri, ws_ref[g], 0.0).astype(BF)
            bias = bst[:, g:g + 1]
            for c in range(nC):
                blk = vln[c * CHUNK:(c + 1) * CHUNK, g * CHUNK:(g + 1) * CHUNK]
                sv_scr[c * CHUNK:(c + 1) * CHUNK, g * CHUNK:(g + 1) * CHUNK] = _dot(wsm, blk) + bias
        sv = sv_scr[...]
        silu, _ = _silu_parts(gt)
        y = (u * sv * silu).astype(BF)
        h1 = xv + _dot(y, waout_ref[...])
        h1_ref[...] = h1
        sv_ref[...] = sv.astype(BF)
        vhat_ref[...] = vhat.astype(BF)
        rstd_ref[...] = jnp.broadcast_to(rstd, rstd_ref.shape)
        rkv = lax.rsqrt(jnp.mean(h1 * h1, axis=-1, keepdims=True) + EPS)
        nkv = (h1 * rkv * gkv_ref[...]).astype(BF)
        kv = _dot(nkv, wkv_ref[...]) + bkv_ref[...]
        k_rot = _rot(kv[:, :LANES], rc_ref[...], rs1_ref[...], rs2_ref[...])
        for src, ref, tref in ((k_rot, k4_ref, kt_ref), (kv[:, LANES:], v4_ref, vt_ref)):
            t4 = _split4(src)
            ref[...] = t4.astype(BF)
            for c in range(nC):
                for b in range(4):
                    blk = t4[c * CHUNK:(c + 1) * CHUNK, b * LANES:(b + 1) * LANES]
                    tref[c, b * LANES:(b + 1) * LANES, :] = blk.T.astype(BF)
        riding.end(i)

    row = functools.partial(_row_spec, TM)
    zcol = [pl.BlockSpec((TM, AW), functools.partial(lambda k, i: (i, k), k)) for k in range(3)]
    tr = pl.BlockSpec((nC, 4 * LANES, CHUNK), lambda i: (i, 0, 0))
    r_in, r_out, r_shape, r_scratch = _Riding.specs(later)
    S = jax.ShapeDtypeStruct
    return pl.pallas_call(
        body, name="a_fwd", grid=(nT,),
        in_specs=[row(D)] + zcol + [_const_spec((1, AW)), _const_spec((1, AW)),
                  _const_spec(ws.shape), _const_spec(bs.shape), _const_spec(wa_out.shape), _const_spec((1, D)),
                  _const_spec(w_kv.shape), _const_spec((1, 2 * LANES)), row(LANES), row(LANES), row(LANES)] + r_in,
        out_specs=[row(D), row(AW), row(AW), row(LANES), row(4 * LANES), row(4 * LANES), tr, tr] + r_out,
        out_shape=(S((T, D), F32), S((T, AW), BF), S((T, AW), BF), S((T, LANES), F32),
                   S((T, 4 * LANES), BF), S((T, 4 * LANES), BF),
                   S((T // CHUNK, 4 * LANES, CHUNK), BF), S((T // CHUNK, 4 * LANES, CHUNK), BF)) + r_shape,
        scratch_shapes=[pltpu.VMEM((TM, AW), F32)] + r_scratch,
        compiler_params=_params(("arbitrary",)),
    )(x, z, z, z, ln_g, ln_b, ws, bs, wa_out, g_kv, w_kv, b_kv, rc, rs1, rs2, *later)


def _b_fwd(h1, g_b, wb_in, bq, rc, rs1, rs2, k4, vt, sinks, wb_out, g_f, target):
    T, D = h1.shape
    BW = wb_out.shape[0]
    SH = wb_in.shape[2]
    TM = min(512, T)
    nC = TM // CHUNK
    nP = BW // LANES

    def body(h1_ref, gb_ref, wbin_ref, bq_ref, rc_ref, rs1_ref, rs2_ref, k4_ref, k4p_ref, vt_ref, vtp_ref, sinks_ref,
             wbout_ref, gf_ref, tgt_ref, q_ref, g2_ref, o_ref, dh2_ref, dh2b_ref, loss_ref, dgf_ref, z_scr, o_scr):
        i = pl.program_id(0)
        sink = _sink_tile(sinks_ref)

        @pl.when(i == 0)
        def _():
            loss_ref[...] = jnp.zeros_like(loss_ref)
            dgf_ref[...] = jnp.zeros_like(dgf_ref)

        h1v = h1_ref[...]
        r2 = lax.rsqrt(jnp.mean(h1v * h1v, axis=-1, keepdims=True) + EPS)
        n2 = (h1v * r2 * gb_ref[...]).astype(BF)
        for j in range(N_DEV):
            z_scr[:, j * SH:(j + 1) * SH] = _dot(n2, wbin_ref[j])
        c_t, s1_t, s2_t = rc_ref[...], rs1_ref[...], rs2_ref[...]
        for p in range(nP):
            cols = slice(p * LANES, (p + 1) * LANES)
            qp = _rot(z_scr[:, cols] + bq_ref[:, cols], c_t, s1_t, s2_t) * (HEAD_DIM ** -0.5)
            q_ref[:, cols] = qp.astype(BF)
        g2 = z_scr[:, BW:]
        g2_ref[...] = g2.astype(BF)
        upper = _upper()
        for c in range(nC):
            ci = i * nC + c
            rows = slice(c * CHUNK, (c + 1) * CHUNK)
            qc = q_ref[rows, :]
            for h in range(2):
                st = _dot_nt(_band_rows(k4_ref, k4p_ref, c, h), _stack_pairs(qc, h))
                fa, fb = _fold(st, upper, ci > 0)
                pa, _ = _softmax_sink(fa, sink[2 * h:2 * h + 1, :])
                pb, _ = _softmax_sink(fb, sink[2 * h + 1:2 * h + 2, :])
                ot = _dot(_band_cols(vt_ref, vtp_ref, c, h), _unfold(pa, pb, upper).astype(BF))
                for j in range(4):
                    o_scr[rows, (h * 4 + j) * LANES:(h * 4 + j + 1) * LANES] = ot[:, j * CHUNK:(j + 1) * CHUNK].T
        o = o_scr[...]
        o_ref[...] = o.astype(BF)
        silu, _ = _silu_parts(g2)
        h2 = h1v + _dot((o * silu).astype(BF), wbout_ref[...])
        rf = lax.rsqrt(jnp.mean(h2 * h2, axis=-1, keepdims=True) + EPS)
        xh = h2 * rf
        gf = gf_ref[...]
        err = xh * gf - tgt_ref[...]
        dyf = err * (1.0 / D)
        dh2 = _rms_bwd(dyf, xh, rf, gf)
        dh2_ref[...] = dh2
        dh2b_ref[...] = dh2.astype(BF)
        loss_ref[...] += 0.5 * jnp.sum(jnp.mean(err * err, axis=-1, keepdims=True), axis=0, keepdims=True)
        dgf_ref[...] += jnp.sum(dyf * xh, axis=0, keepdims=True)

    row = functools.partial(_row_spec, TM)
    rows_tile, rows_before, cols_tile, cols_before = _band_specs(TM)
    S = jax.ShapeDtypeStruct
    return pl.pallas_call(
        body, name="b_fwd", grid=(T // TM,),
        in_specs=[row(D), _const_spec((1, D)), _const_spec(wb_in.shape), _const_spec((1, BW)), row(LANES), row(LANES),
                  row(LANES), rows_tile, rows_before, cols_tile, cols_before, pl.BlockSpec(memory_space=pltpu.SMEM),
                  _const_spec(wb_out.shape), _const_spec((1, D)), row(D)],
        out_specs=[row(BW), row(BW), row(BW), row(D), row(D), _acc_spec((1, 1)), _acc_spec((1, D))],
        out_shape=(S((T, BW), BF), S((T, BW), BF), S((T, BW), BF), S((T, D), F32), S((T, D), BF), S((1, 1), F32),
                   S((1, D), F32)),
        scratch_shapes=[pltpu.VMEM((TM, 2 * BW), F32), pltpu.VMEM((TM, BW), F32)],
        compiler_params=_params(("arbitrary",)),
    )(h1, g_b, wb_in, bq, rc, rs1, rs2, k4, k4, vt, vt, sinks, wb_out, g_f, target)


def _b_bwd(dh2, h1, q, g2, o, k4, v4, kt, sinks, wb_out, wb_in, g_b, rc, rs1, rs2):
    T, D = h1.shape
    BW = wb_out.shape[0]
    SH = wb_in.shape[2]
    TM = min(256, T)
    nT = T // TM
    nC = TM // CHUNK
    nP = BW // LANES

    def body(dh2_ref, h1_ref, q_ref, g2_ref, o_ref, k4_ref, k4p_ref, v4_ref, v4p_ref, kt_ref, ktp_ref, sinks_ref,
             wbout_ref, wbin_ref, gb_ref, rc_ref, rs1_ref, rs2_ref,
             dh1_ref, dz2_ref, n2_ref, y2_ref, dk_ref, dv_ref, dbq_ref, dgb_ref, dsink_ref, do_scr, dq_scr, dsacc_scr):
        i = pl.program_id(0)
        sink = _sink_tile(sinks_ref)

        @pl.when(i == 0)
        def _():
            dk_ref[...] = jnp.zeros_like(dk_ref)
            dv_ref[...] = jnp.zeros_like(dv_ref)
            dbq_ref[...] = jnp.zeros_like(dbq_ref)
            dgb_ref[...] = jnp.zeros_like(dgb_ref)
            dsacc_scr[...] = jnp.zeros_like(dsacc_scr)

        dh2 = dh2_ref[...]
        dy2 = _dot_nt(dh2.astype(BF), wbout_ref[...])
        silu, dsilu = _silu_parts(g2_ref[...].astype(F32))
        do_scr[...] = (dy2 * silu).astype(BF)
        dy2, silu, dsilu = dy2.astype(BF), silu.astype(BF), dsilu.astype(BF)
        ob = o_ref[...]
        y2_ref[...] = (ob * silu).T
        dz2_ref[:, BW:] = dy2 * ob * dsilu
        upper = _upper()
        lo = _lane_lo((2 * CHUNK, LANES))
        for c in range(nC):
            ci = i * nC + c
            rows = slice(c * CHUNK, (c + 1) * CHUNK)
            pci = jnp.maximum(ci - 1, 0)
            prev = pl.multiple_of(pci * CHUNK, CHUNK)
            cur = pl.multiple_of(ci * CHUNK, CHUNK)
            qc = q_ref[rows, :]
            doc = do_scr[rows, :]
            dkb = jnp.zeros((2 * CHUNK, LANES), F32)
            dvb = jnp.zeros((2 * CHUNK, LANES), F32)
            for h in range(2):
                qs = _stack_pairs(qc, h)
                dos = _stack_pairs(doc, h)
                fa, fb = _fold(_dot_nt(_band_rows(k4_ref, k4p_ref, c, h), qs), upper, ci > 0)
                dfa, dfb = _fold(_dot_nt(_band_rows(v4_ref, v4p_ref, c, h), dos), upper)
                folded = []
                for k, (f, df) in enumerate(((fa, dfa), (fb, dfb))):
                    p, ps = _softmax_sink(f, sink[2 * h + k:2 * h + k + 1, :])
                    delta = jnp.sum(p * df, axis=0, keepdims=True)
                    dsacc_scr[2 * h + k:2 * h + k + 1, :] -= ps * delta
                    folded.append((p, p * (df - delta)))
                pt = _unfold(folded[0][0], folded[1][0], upper).astype(BF)
                dst = _unfold(folded[0][1], folded[1][1], upper).astype(BF)
                dqt = _dot(_band_cols(kt_ref, ktp_ref, c, h), dst)
                for j in range(4):
                    dq_scr[rows, (h * 4 + j) * LANES:(h * 4 + j + 1) * LANES] = dqt[:, j * CHUNK:(j + 1) * CHUNK].T
                for acc_name, g in (("k", _dot(dst, qs)), ("v", _dot(pt, dos))):
                    a, b = g[:2 * CHUNK], g[2 * CHUNK:]
                    if h == 0:
                        part = jnp.where(lo, a + pltpu.roll(b, HEAD_DIM, 1), 0.0)
                    else:
                        part = jnp.where(lo, 0.0, pltpu.roll(a, HEAD_DIM, 1) + b)
                    if acc_name == "k":
                        dkb += part
                    else:
                        dvb += part
            dk_ref[pl.ds(prev, CHUNK), :] += dkb[:CHUNK]
            dk_ref[pl.ds(cur, CHUNK), :] += dkb[CHUNK:]
            dv_ref[pl.ds(prev, CHUNK), :] += dvb[:CHUNK]
            dv_ref[pl.ds(cur, CHUNK), :] += dvb[CHUNK:]
        c_t, s1_t, s2_t = rc_ref[...], rs1_ref[...], rs2_ref[...]
        for p in range(nP):
            cols = slice(p * LANES, (p + 1) * LANES)
            dqp = _rot_bwd(dq_scr[:, cols] * (HEAD_DIM ** -0.5), c_t, s1_t, s2_t)
            dbq_ref[:, cols] += jnp.sum(dqp, axis=0, keepdims=True)
            dz2_ref[:, cols] = dqp.astype(BF)
        h1v = h1_ref[...]
        r2 = lax.rsqrt(jnp.mean(h1v * h1v, axis=-1, keepdims=True) + EPS)
        xh = h1v * r2
        gb = gb_ref[...]
        n2_ref[...] = (xh * gb).astype(BF).T
        dn2 = None
        for j in range(N_DEV):
            part = _dot_nt(dz2_ref[:, j * SH:(j + 1) * SH], wbin_ref[j])
            dn2 = part if dn2 is None else dn2 + part
        dgb_ref[...] += jnp.sum(dn2 * xh, axis=0, keepdims=True)
        dh1_ref[...] = dh2 + _rms_bwd(dn2, xh, r2, gb)

        @pl.when(i == nT - 1)
        def _():
            lane = lax.broadcasted_iota(jnp.int32, dsink_ref.shape, 1)
            tot = jnp.zeros(dsink_ref.shape, F32)
            for j in range(4):
                tot += jnp.where(lane == j, jnp.sum(dsacc_scr[:, j * CHUNK:(j + 1) * CHUNK], axis=1, keepdims=True), 0.0)
            dsink_ref[...] = tot

    row = functools.partial(_row_spec, TM)
    rows_tile, rows_before, cols_tile, cols_before = _band_specs(TM)
    S = jax.ShapeDtypeStruct
    return pl.pallas_call(
        body, name="b_bwd", grid=(T // TM,),
        in_specs=[row(D), row(D), row(BW), row(BW), row(BW), rows_tile, rows_before, rows_tile, rows_before,
                  cols_tile, cols_before, pl.BlockSpec(memory_space=pltpu.SMEM), _const_spec(wb_out.shape), _const_spec(wb_in.shape),
                  _const_spec((1, D)), row(LANES), row(LANES), row(LANES)],
        out_specs=[row(D), row(2 * BW), _col_spec(TM, D), _col_spec(TM, BW), _acc_spec((T, LANES)),
                   _acc_spec((T, LANES)), _acc_spec((1, BW)), _acc_spec((1, D)), _acc_spec((4, LANES))],
        out_shape=(S((T, D), F32), S((T, 2 * BW), BF), S((D, T), BF), S((BW, T), BF), S((T, LANES), F32),
                   S((T, LANES), F32), S((1, BW), F32), S((1, D), F32), S((4, LANES), F32)),
        scratch_shapes=[pltpu.VMEM((TM, BW), BF), pltpu.VMEM((TM, BW), F32), pltpu.VMEM((4, 4 * CHUNK), F32)],
        compiler_params=_params(("arbitrary",)),
    )(dh2, h1, q, g2, o, k4, k4, v4, v4, kt, kt, sinks, wb_out, wb_in, g_b, rc, rs1, rs2)


def _a_bwd(dh1p, dk, dv, h1, g_kv, w_kv, wa_out, ws, ln_g, ln_b, z, sv, vhat, rstd, rc, rs1, rs2, ready):
    T, D = h1.shape
    AW = wa_out.shape[0]
    G = ws.shape[0]
    TM = min(256, T)
    nT = T // TM
    nC = TM // CHUNK
    nr = len(ready)

    def body(dh1p_ref, dk_ref, dv_ref, h1_ref, gkv_ref, wkv_ref, waout_ref, ws_ref, lng_ref,
             lnb_ref, u_ref, gt_ref, sv_ref, vhat_ref, rstd_ref, rc_ref, rs1_ref, rs2_ref, *rest):
        ready_refs, rest = rest[:nr], rest[nr:]
        (dz_ref, gwo_ref, gwk_ref, dh1f_ref, dgkv_ref, dbkv_ref, dlng_ref, dlnb_ref,
         dws_ref, dbs_ref), rest = rest[:10], rest[10:]
        recv_refs, (dsv_scr, dvln_scr, acco_scr, acck_scr, ssem, rsem, lsem) = rest[:nr], rest[nr:]
        i = pl.program_id(0)
        exchanges = [_Direct(ready_refs[k], recv_refs[k], ssem.at[k], rsem.at[k], lsem.at[k], scatter=True)
                     for k in range(nr)]

        @pl.when(i == 0)
        def _():
            for e in exchanges:
                e.start()
            for r in (dgkv_ref, dbkv_ref, dlng_ref, dlnb_ref, dws_ref, dbs_ref, acco_scr, acck_scr):
                r[...] = jnp.zeros_like(r)

        dk_pre = _rot_bwd(dk_ref[...], rc_ref[...], rs1_ref[...], rs2_ref[...])
        dkv = jnp.concatenate([dk_pre, dv_ref[...]], axis=1)
        dbkv_ref[...] += jnp.sum(dkv, axis=0, keepdims=True)
        dkv_b = dkv.astype(BF)
        h1v = h1_ref[...]
        rkv = lax.rsqrt(jnp.mean(h1v * h1v, axis=-1, keepdims=True) + EPS)
        xh_kv = h1v * rkv
        gkv = gkv_ref[...]
        acck_scr[...] += _dot((xh_kv * gkv).astype(BF).T, dkv_b)
        dnkv = _dot_nt(dkv_b, wkv_ref[...])
        dgkv_ref[...] += jnp.sum(dnkv * xh_kv, axis=0, keepdims=True)
        dh1 = dh1p_ref[...] + _rms_bwd(dnkv, xh_kv, rkv, gkv)
        dh1_b = dh1.astype(BF)
        dh1f_ref[...] = dh1
        dy = _dot_nt(dh1_b, waout_ref[...]).astype(BF)
        silu, dsilu = _silu_parts(gt_ref[...].astype(F32))
        silu, dsilu = silu.astype(BF), dsilu.astype(BF)
        ub, svb = u_ref[...], sv_ref[...]
        us = ub * silu
        dys = dy * svb
        acco_scr[...] += _dot((us * svb).T, dh1_b)
        dz_ref[:, :AW] = dys * silu
        dz_ref[:, 2 * AW:] = dys * ub * dsilu
        dsv_scr[...] = dy * us
        vhat_v = vhat_ref[...].astype(F32)
        lng = lng_ref[...]
        vln_b = (vhat_v * lng + lnb_ref[...]).astype(BF)
        tri = lax.broadcasted_iota(jnp.int32, (CHUNK, CHUNK), 0) >= lax.broadcasted_iota(jnp.int32, (CHUNK, CHUNK), 1)
        lane = lax.broadcasted_iota(jnp.int32, (CHUNK, LANES), 1)
        dbs = jnp.zeros((CHUNK, LANES), F32)
        for g in range(G):
            wsm = jnp.where(tri, ws_ref[g], 0.0).astype(BF)
            cols = slice(g * CHUNK, (g + 1) * CHUNK)
            dws_g = None
            for c in range(nC):
                rows = slice(c * CHUNK, (c + 1) * CHUNK)
                dsv_cg = dsv_scr[rows, cols]
                dvln_scr[rows, cols] = _dot_tn(wsm, dsv_cg)
                part = _dot_nt(dsv_cg, vln_b[rows, cols])
                dws_g = part if dws_g is None else dws_g + part
                dbs += jnp.where(lane == g, jnp.sum(dsv_cg.astype(F32), axis=-1, keepdims=True), 0.0)
            dws_ref[g] += jnp.where(tri, dws_g, 0.0)
        dbs_ref[...] += dbs
        dvln = dvln_scr[...]
        dlng_ref[...] += jnp.sum(dvln * vhat_v, axis=0, keepdims=True)
        dlnb_ref[...] += jnp.sum(dvln, axis=0, keepdims=True)
        a = dvln * lng
        dvv = rstd_ref[:, 0:1] * (a - jnp.mean(a, axis=-1, keepdims=True)
                                  - vhat_v * jnp.mean(a * vhat_v, axis=-1, keepdims=True))
        dz_ref[:, AW:2 * AW] = dvv.astype(BF)

        @pl.when(i == nT - 1)
        def _():
            for j in range(N_DEV):
                gwo_ref[j] = acco_scr[j * (AW // N_DEV):(j + 1) * (AW // N_DEV)].astype(BF)
                gwk_ref[j] = acck_scr[j * (D // N_DEV):(j + 1) * (D // N_DEV)].astype(BF)
            for e in exchanges:
                e.finish()

    row = functools.partial(_row_spec, TM)
    hbm = pl.BlockSpec(memory_space=pl.ANY)
    S = jax.ShapeDtypeStruct
    gwo_shape, gwk_shape = (N_DEV, AW // N_DEV, D), (N_DEV, D // N_DEV, 2 * LANES)
    return pl.pallas_call(
        body, name="a_bwd", grid=(nT,),
        in_specs=[row(D), row(LANES), row(LANES), row(D), _const_spec((1, D)), _const_spec(w_kv.shape),
                  _const_spec(wa_out.shape), _const_spec(ws.shape),
                  _const_spec((1, AW)), _const_spec((1, AW)), pl.BlockSpec((TM, AW), lambda i: (i, 0)),
                  pl.BlockSpec((TM, AW), lambda i: (i, 2)), row(AW), row(AW), row(LANES),
                  row(LANES), row(LANES), row(LANES)] + [hbm] * nr,
        out_specs=[row(3 * AW), _const_spec(gwo_shape), _const_spec(gwk_shape), row(D),
                   _acc_spec((1, D)), _acc_spec((1, 2 * LANES)), _acc_spec((1, AW)),
                   _acc_spec((1, AW)), _acc_spec(ws.shape), _acc_spec((CHUNK, LANES))] + [hbm] * nr,
        out_shape=(S((T, 3 * AW), BF), S(gwo_shape, BF), S(gwk_shape, BF), S((T, D), F32),
                   S((1, D), F32), S((1, 2 * LANES), F32), S((1, AW), F32), S((1, AW), F32),
                   S(ws.shape, F32), S((CHUNK, LANES), F32)) + tuple(S(r.shape, r.dtype) for r in ready),
        scratch_shapes=[pltpu.VMEM((TM, AW), BF), pltpu.VMEM((TM, AW), F32), pltpu.VMEM((AW, D), F32),
                        pltpu.VMEM((D, 2 * LANES), F32)] + _direct_sems(nr),
        compiler_params=_params(("arbitrary",)),
    )(dh1p, dk, dv, h1, g_kv, w_kv, wa_out, ws, ln_g, ln_b, z, z, sv, vhat, rstd, rc, rs1, rs2, *ready)


def _a_in_bwd(dz, wa_in_t, x, dh1, g_a, ready):
    T, D = x.shape
    TM = min(512, T)
    nT = T // TM
    nr = len(ready)

    def body(dz_ref, wain_ref, x_ref, dh1_ref, ga_ref, *rest):
        ready_refs, (dx_ref, n1_ref, dga_ref), rest = rest[:nr], rest[nr:nr + 3], rest[nr + 3:]
        recv_refs, (ssem, rsem, lsem) = rest[:nr], rest[nr:]
        i = pl.program_id(0)
        exchanges = [_Direct(ready_refs[k], recv_refs[k], ssem.at[k], rsem.at[k], lsem.at[k], scatter=True)
                     for k in range(nr)]

        @pl.when(i == 0)
        def _():
            for e in exchanges:
                e.start()
            dga_ref[...] = jnp.zeros_like(dga_ref)

        xv = x_ref[...]
        r1 = lax.rsqrt(jnp.mean(xv * xv, axis=-1, keepdims=True) + EPS)
        xh = xv * r1
        ga = ga_ref[...]
        n1_ref[...] = (xh * ga).astype(BF).T
        dn1 = _dot(dz_ref[...], wain_ref[...])
        dga_ref[...] += jnp.sum(dn1 * xh, axis=0, keepdims=True)
        dx_ref[...] = dh1_ref[...] + _rms_bwd(dn1, xh, r1, ga)

        @pl.when(i == nT - 1)
        def _():
            for e in exchanges:
                e.finish()

    row = functools.partial(_row_spec, TM)
    hbm = pl.BlockSpec(memory_space=pl.ANY)
    S = jax.ShapeDtypeStruct
    return pl.pallas_call(
        body, name="a_in_bwd", grid=(nT,),
        in_specs=[row(dz.shape[1]), _const_spec(wa_in_t.shape), row(D), row(D), _const_spec((1, D))] + [hbm] * nr,
        out_specs=[row(D), _col_spec(TM, D), _acc_spec((1, D))] + [hbm] * nr,
        out_shape=(S((T, D), F32), S((D, T), BF), S((1, D), F32)) + tuple(S(r.shape, r.dtype) for r in ready),
        scratch_shapes=_direct_sems(nr),
        compiler_params=_params(("arbitrary",)),
    )(dz, wa_in_t, x, dh1, g_a, *ready)


def _wgrad(at, b, nblk, name, bt=512):
    K, T = at.shape
    N = b.shape[1] // nblk
    BT = min(bt, T)
    nt = T // BT

    def body(a_ref, b_ref, o_ref, acc):
        t = pl.program_id(0)

        @pl.when(t == 0)
        def _():
            acc[...] = jnp.zeros_like(acc)

        acc[...] += _dot(a_ref[...], b_ref[...])

        @pl.when(t == nt - 1)
        def _():
            for j in range(nblk):
                o_ref[j] = acc[:, j * N:(j + 1) * N].astype(BF)

    return pl.pallas_call(
        body, name=name, grid=(nt,),
        in_specs=[pl.BlockSpec((K, BT), lambda t: (0, t)), pl.BlockSpec((BT, nblk * N), lambda t: (t, 0))],
        out_specs=pl.BlockSpec((nblk, K, N), lambda t: (0, 0, 0)),
        out_shape=jax.ShapeDtypeStruct((nblk, K, N), BF),
        scratch_shapes=[pltpu.VMEM((K, nblk * N), F32)],
        compiler_params=_params(("arbitrary",)),
    )(at, b)


def _wgrad_exchange(a, b, me, small, name):
    K, T = a.shape
    N = b.shape[1] // N_DEV
    BT = T
    nt = T // BT
    last = N_DEV - 1
    n_chip = N_DEV // 2

    def far_of(k, core):
        return jnp.where((core == 0) & ((k == 1) | (k == 2)), k, n_chip - 1 - k)

    def block_of(s, me_i):
        k, odd = s // 2, s % 2
        core = me_i & 1
        return me_i ^ ((far_of(k, jnp.where(odd == 1, core, 1 - core)) << 1) | (1 - odd))

    H = K // 2

    def body(me_ref, a_ref, b_ref, small_ref, recv_ref, full_ref, *scratch):
        (acc, dstage, istage, half, relay, d_s, d_r, i_s, i_r, r_s, r_r, lsem, parts_scr, red_scr, e_s, e_r, e_l, g_s,
         g_r, g_l) = scratch
        s, t = pl.program_id(0), pl.program_id(1)
        x, y, c = (lax.axis_index(ax) for ax in AXES)
        ex = [_Direct(small_ref, parts_scr, e_s, e_r, e_l, scatter=True)]
        regather = _TwoLevel(red_scr, full_ref, g_s, g_r, g_l)

        def to_sibling(k, slot):
            return pltpu.make_async_remote_copy(src_ref=dstage.at[slot], dst_ref=half.at[k], send_sem=d_s.at[k],
                                                recv_sem=d_r.at[k], device_id=(x, y, 1 - c), device_id_type=MESH)

        def to_chip(k, slot):
            over_x = far_of(k, c) == 2
            px, py = jnp.where(over_x, 1 - x, x), jnp.where(over_x, y, 1 - y)
            return pltpu.make_async_remote_copy(src_ref=istage.at[slot], dst_ref=recv_ref.at[jnp.where(over_x, 1, 2)],
                                                send_sem=i_s.at[k], recv_sem=i_r.at[k], device_id=(px, py, c),
                                                device_id_type=MESH)

        def to_relay(j, slot):
            to = (1 - x, y, c) if j == 0 else (x, 1 - y, c)
            return pltpu.make_async_remote_copy(src_ref=istage.at[slot, pl.ds(j * H, H)], dst_ref=relay.at[j],
                                                send_sem=r_s.at[j], recv_sem=r_r.at[j], device_id=to,
                                                device_id_type=MESH)

        @pl.when((s == 0) & (t == 0))
        def _():
            for e in ex:
                e.start()

        acc[...] = _dot(a_ref[...], b_ref[...])

        @pl.when(t == nt - 1)
        def _():
            k = lax.div(s, 2)
            slot = lax.rem(k, 2)

            @pl.when(lax.rem(s, 2) == 0)
            def _():
                @pl.when(k >= 2)
                def _():
                    to_sibling(k - 2, slot).wait_send()

                dstage[slot] = acc[...].astype(BF)
                to_sibling(k, slot).start()

            @pl.when(lax.rem(s, 2) == 1)
            def _():
                to_sibling(k, slot).wait_recv()
                pair = acc[...] + half[k].astype(F32)

                @pl.when(k == 0)
                def _():
                    istage[slot] = pair.astype(BF)
                    for j in range(2):
                        to_relay(j, slot).start()

                @pl.when(k == 1)
                def _():
                    for j in range(2):
                        to_relay(j, slot).wait_recv()

                @pl.when(k == 2)
                def _():
                    for j in range(2):
                        to_relay(j, slot).wait_send()

                @pl.when(k == n_chip - 1)
                def _():
                    to_chip(1, slot).wait_send()
                    istage[slot] = pair.astype(BF)

                @pl.when((k == 1) | (k == 2))
                def _():
                    over_x = far_of(k, c) == 2
                    istage[slot, 0:H] = (pair[:H] + jnp.where(over_x, 0.0, relay[0].astype(F32))).astype(BF)
                    istage[slot, H:K] = (pair[H:] + jnp.where(over_x, relay[1].astype(F32), 0.0)).astype(BF)
                    to_chip(k, slot).start()

            @pl.when(s == last)
            def _():
                own = pltpu.make_async_copy(istage.at[slot], recv_ref.at[0], lsem)
                own.start()
                to_chip(2, 0).wait_send()
                to_sibling(n_chip - 2, 0).wait_send()
                to_sibling(n_chip - 1, 1).wait_send()
                for kk in (1, 2):
                    to_chip(kk, 0).wait_recv()
                own.wait()
                for e in ex:
                    e.finish()
                total = parts_scr[0]
                for dev in range(1, N_DEV):
                    total = total + parts_scr[dev]
                red_scr[...] = total
                regather.start()
                regather.forward()
                regather.finish()

    hbm = pl.BlockSpec(memory_space=pl.ANY)
    dma = pltpu.SemaphoreType.DMA
    grid_spec = pltpu.PrefetchScalarGridSpec(
        num_scalar_prefetch=1, grid=(N_DEV, nt),
        in_specs=[pl.BlockSpec((K, BT), lambda s, t, me_ref: (0, t), pipeline_mode=pl.Buffered(1)),
                  pl.BlockSpec((BT, N), lambda s, t, me_ref: (t, block_of(s, me_ref[0]))), hbm],
        out_specs=[hbm, hbm],
        scratch_shapes=[pltpu.VMEM((K, N), F32), pltpu.VMEM((2, K, N), BF), pltpu.VMEM((2, K, N), BF),
                        pltpu.VMEM((n_chip, K, N), BF), pltpu.VMEM((2, H, N), BF), dma((n_chip,)), dma((n_chip,)),
                        dma((n_chip - 1,)), dma((n_chip - 1,)), dma((2,)), dma((2,)), dma,
                        pltpu.VMEM(small.shape, F32), pltpu.VMEM(small.shape[1:], F32),
                        dma((last,)), dma((last,)), dma, dma((last,)), dma((last,)), dma])
    return pl.pallas_call(
        body, name=name, grid_spec=grid_spec,
        out_shape=[jax.ShapeDtypeStruct((n_chip - 1, K, N), BF), jax.ShapeDtypeStruct(small.shape, F32)],
        compiler_params=_params(("arbitrary", "arbitrary")),
    )(me, a, b, small)


def _my_index():
    return 4 * lax.axis_index("x") + 2 * lax.axis_index("y") + lax.axis_index("c")


def _peer(mask):
    x, y, c = (lax.axis_index(a) for a in AXES)
    return (x ^ ((mask >> 2) & 1), y ^ ((mask >> 1) & 1), c ^ (mask & 1))


def _dev_index(p):
    return 4 * p[0] + 2 * p[1] + p[2]


class _Direct:
    def __init__(self, src, dst, send_sems, recv_sems, local_sem, scatter):
        me = _my_index()
        self.own = pltpu.make_async_copy(src.at[me] if scatter else src, dst.at[me], local_sem)
        self.sends, self.recvs = [], []
        for k in range(1, N_DEV):
            p = _peer(k)
            pi = _dev_index(p)
            sems = dict(send_sem=send_sems.at[k - 1], recv_sem=recv_sems.at[k - 1], device_id=p, device_id_type=MESH)
            self.sends.append(pltpu.make_async_remote_copy(src_ref=src.at[pi] if scatter else src, dst_ref=dst.at[me],
                                                           **sems))
            self.recvs.append(pltpu.make_async_remote_copy(src_ref=src.at[me] if scatter else src, dst_ref=dst.at[pi],
                                                           **sems))

    def start(self):
        self.own.start()
        for cp in self.sends:
            cp.start()

    def finish(self):
        for cp in self.sends:
            cp.wait_send()
        for cp in self.recvs:
            cp.wait_recv()
        self.own.wait()


class _TwoLevel:
    def __init__(self, src, dst, send_sems, recv_sems, local_sem, own=True):
        x, y, c = (lax.axis_index(a) for a in AXES)
        self.me, self.sibling = (x, y, c), (x, y, 1 - c)
        self.chips = [(1 - x, y), (x, 1 - y), (1 - x, 1 - y)]
        self.src, self.dst, self.send_sems, self.recv_sems = src, dst, send_sems, recv_sems
        self.own = pltpu.make_async_copy(src, dst.at[_dev_index(self.me)], local_sem) if own else None

    def _copy(self, k, block, to, from_src=False):
        slot = self.dst.at[_dev_index(block)]
        return pltpu.make_async_remote_copy(src_ref=self.src if from_src else slot, dst_ref=slot,
                                            send_sem=self.send_sems.at[k], recv_sem=self.recv_sems.at[k],
                                            device_id=to, device_id_type=MESH)

    def _firsts(self):
        c = self.me[2]
        return [self._copy(0, self.me, self.sibling, True)] + [self._copy(1 + j, self.me, (*chip, c), True)
                                                               for j, chip in enumerate(self.chips)]

    def _passed(self):
        c = self.me[2]
        return [self._copy(4 + j, (*chip, c), self.sibling) for j, chip in enumerate(self.chips)]

    def start(self):
        if self.own is not None:
            self.own.start()
        for cp in self._firsts():
            cp.start()

    def wait_sibling(self):
        self._copy(0, self.sibling, self.me).wait_recv()

    def wait_chip_and_forward(self, j):
        self._copy(1 + j, (*self.chips[j], self.me[2]), self.me).wait_recv()
        self._passed()[j].start()

    def wait_passed(self, j):
        self._copy(4 + j, (*self.chips[j], 1 - self.me[2]), self.me).wait_recv()

    def wait_sends(self):
        for cp in self._firsts() + self._passed():
            cp.wait_send()
        if self.own is not None:
            self.own.wait()

    def forward(self):
        for j in range(3):
            self.wait_chip_and_forward(j)

    def finish(self):
        self.wait_sibling()
        for j in range(3):
            self.wait_passed(j)
        self.wait_sends()


class _RelayGather:
    def __init__(self, dst, send_sems, recv_sems):
        x, y, c = (lax.axis_index(a) for a in AXES)
        self.c = c
        self.sib, self.xn, self.yn, self.dg = (x, y, 1 - c), (1 - x, y, c), (x, 1 - y, c), (1 - x, 1 - y, c)
        self.me = (x, y, c)
        self.dst, self.send_sems, self.recv_sems = dst, send_sems, recv_sems
        self.half = dst.shape[1] // 2

    def _slot(self, dev, part=None):
        i = _dev_index(dev)
        if part is None:
            return self.dst.at[i]
        return self.dst.at[i, pl.ds(part * self.half, self.half)]

    def _copy(self, k, dev, to, part=None):
        ref = self._slot(dev, part)
        return pltpu.make_async_remote_copy(src_ref=ref, dst_ref=ref, send_sem=self.send_sems.at[k],
                                            recv_sem=self.recv_sems.at[k], device_id=to, device_id_type=MESH)

    def _other(self, dev):
        return (dev[0], dev[1], 1 - self.c)

    def start(self):
        for k, to in enumerate((self.sib, self.xn, self.yn)):
            self._copy(k, self.me, to).start()

    def send_own(self, k):
        return self._copy(k, self.me, (self.sib, self.xn, self.yn)[k])

    def wait_sibling(self):
        self._copy(0, self.sib, self.me).wait_recv()

    def on_x(self):
        self._copy(1, self.xn, self.me).wait_recv()
        self._copy(3, self.xn, self.yn, part=0).start()
        self._copy(5, self.xn, self.sib).start()

    def on_y(self):
        self._copy(2, self.yn, self.me).wait_recv()
        self._copy(4, self.yn, self.xn, part=1).start()
        self._copy(6, self.yn, self.sib).start()

    def on_diag(self):
        self._copy(3, self.dg, self.me, part=0).wait_recv()
        self._copy(4, self.dg, self.me, part=1).wait_recv()
        self._copy(7, self.dg, self.sib).start()

    def wait_passed(self, j):
        self._copy(5 + j, self._other((self.xn, self.yn, self.dg)[j]), self.me).wait_recv()

    def wait_sends(self):
        for k, to in enumerate((self.sib, self.xn, self.yn)):
            self._copy(k, self.me, to).wait_send()
        self._copy(3, self.xn, self.yn, part=0).wait_send()
        self._copy(4, self.yn, self.xn, part=1).wait_send()
        for j, dev in enumerate((self.xn, self.yn, self.dg)):
            self._copy(5 + j, dev, self.sib).wait_send()


def _direct_sems(n):
    if n == 0:
        return []
    return [pltpu.SemaphoreType.DMA((n, 7)), pltpu.SemaphoreType.DMA((n, 7)), pltpu.SemaphoreType.DMA((n,))]


def _adam_math(w, g, m, v):
    m = ADAM_B1 * m + (1.0 - ADAM_B1) * g
    v = ADAM_B2 * v + (1.0 - ADAM_B2) * (g * g)
    m_hat = m / (1.0 - ADAM_B1 ** ADAM_STEP)
    v_hat = v / (1.0 - ADAM_B2 ** ADAM_STEP)
    delta = -ADAM_LR * (m_hat / (jnp.sqrt(v_hat) + ADAM_EPS) + ADAM_WD * w)
    return delta, m, v


def _sum_adam(parts, w, m, v, name):
    R, C = w.shape
    NP = parts.shape[0]
    BR = 4 * CHUNK if R % (4 * CHUNK) == 0 else R

    def body(p_ref, w_ref, m_ref, v_ref, g_ref, d_ref, nm_ref, nv_ref):
        g = p_ref[0].astype(F32)
        for i in range(1, NP):
            g = g + p_ref[i].astype(F32)
        g_ref[...] = g
        d_ref[...], nm_ref[...], nv_ref[...] = _adam_math(w_ref[...], g, m_ref[...], v_ref[...])

    blk = pl.BlockSpec((BR, C), lambda i: (i, 0))
    S = jax.ShapeDtypeStruct((R, C), F32)
    return pl.pallas_call(
        body, name=name, grid=(R // BR,),
        in_specs=[pl.BlockSpec((NP, BR, C), lambda i: (0, i, 0)), blk, blk, blk],
        out_specs=[blk] * 4, out_shape=(S,) * 4,
        compiler_params=_params(("arbitrary",)),
    )(parts, w, m, v)


SUBLANES = 8


def _nrows(size):
    return -(-size // (SUBLANES * LANES)) * SUBLANES


def _view2d(a):
    return a.reshape(-1, LANES) if a.size % LANES == 0 else a.reshape(1, -1)


def _pack_small(parts, total_rows, name):
    arrs = [p[0] for p in parts]

    def body(*refs):
        out = refs[-1]
        out[...] = jnp.zeros_like(out)
        at = 0
        for ref, (a, rows, flag) in zip(refs[:-1], parts):
            val = ref[...].T if flag == "T" else ref[...]
            r, c = (rows, val.shape[1]) if flag == "T" else val.shape
            out[at:at + r, 0:c] = val[:r]
            at += _nrows(r * c)

    return pl.pallas_call(body, name=name, out_shape=jax.ShapeDtypeStruct((total_rows, LANES), F32))(*arrs)


def _small_update(full, me, reps, shards, name):
    n = len(reps) + len(shards)

    def body(me_ref, full_ref, *refs):
        ins, outs = refs[:3 * n], refs[3 * n:]
        at = 0
        for k in range(n):
            w_ref, m_ref, v_ref = ins[3 * k:3 * k + 3]
            r, c = w_ref.shape
            if k < len(reps):
                g = full_ref[at:at + r, 0:c]
                at += _nrows(r * c)
            else:
                seg = full_ref[at:at + N_DEV * r, :]
                row = lax.broadcasted_iota(jnp.int32, seg.shape, 0)
                pick = [jnp.sum(jnp.where(row == r * me_ref[0] + t, seg, 0.0), axis=0, keepdims=True) for t in range(r)]
                g = pick[0] if r == 1 else jnp.concatenate(pick, axis=0)
                at += N_DEV * r
            g_ref, d_ref, nm_ref, nv_ref = outs[4 * k:4 * k + 4]
            g_ref[...] = g
            d_ref[...], nm_ref[...], nv_ref[...] = _adam_math(w_ref[...], g, m_ref[...], v_ref[...])
        outs[4 * n][...] = full_ref[at:at + 1, 0:1]

    flat = [t for p in reps + shards for t in p]
    S = jax.ShapeDtypeStruct
    res = pl.pallas_call(
        body, name=name,
        in_specs=[pl.BlockSpec(memory_space=pltpu.SMEM)] + [pl.BlockSpec(memory_space=pltpu.VMEM)] * (1 + len(flat)),
        out_shape=[S(p[0].shape, F32) for p in reps + shards for _ in range(4)] + [S((1, 1), F32)],
    )(me, full, *flat)
    return [tuple(res[4 * k:4 * k + 4]) for k in range(n)], res[4 * n]


def _rope_tables(T):
    pos = np.arange(T, dtype=np.float32)
    inv_freq = (np.float64(ROPE_THETA) ** (-np.arange(0, HEAD_DIM, 2, dtype=np.float64) / HEAD_DIM)).astype(np.float32)
    ang = (pos[:, None] * inv_freq[None, :]).astype(np.float64)
    cos, sin, zero = np.cos(ang).astype(np.float32), np.sin(ang).astype(np.float32), np.zeros(ang.shape, np.float32)
    c = np.concatenate([cos, cos, cos, cos], axis=1)
    s1 = np.concatenate([-sin, zero, -sin, zero], axis=1)
    s2 = np.concatenate([zero, sin, zero, sin], axis=1)
    return jnp.asarray(c), jnp.asarray(s1), jnp.asarray(s2)


def kernel(x, a_norm_g, a_w_in, a_ln_g, a_ln_b, a_ws, a_bs, a_w_out, kv_norm_g, w_kv, b_kv, b_norm_g, b_w_in, b_bq, b_sinks, b_w_out, final_norm_g, loss_target, m_a_norm_g, m_a_w_in, m_a_ln_g, m_a_ln_b, m_a_ws, m_a_bs, m_a_w_out, m_kv_norm_g, m_w_kv, m_b_kv, m_b_norm_g, m_b_w_in, m_b_bq, m_b_sinks, m_b_w_out, m_final_norm_g, v_a_norm_g, v_a_w_in, v_a_ln_g, v_a_ln_b, v_a_ws, v_a_bs, v_a_w_out, v_kv_norm_g, v_w_kv, v_b_kv, v_b_norm_g, v_b_w_in, v_b_bq, v_b_sinks, v_b_w_out, v_final_norm_g):
    T, D = x.shape[1], x.shape[2]
    AW = a_ln_g.shape[1] * N_DEV
    G = a_ws.shape[1]
    assert w_kv.shape[1] == 2 * LANES and a_ws.shape[2] == CHUNK and T % CHUNK == 0
    me = _my_index()

    xs, tgt = x[0], loss_target[0]
    z, wa_in_t, g_a, ln_g, ln_b, wa_out, wkv = _in_proj(xs, a_w_in[0], [a_norm_g, a_ln_g, a_ln_b], me.reshape(1),
                                                        [a_w_out[0], w_kv])
    wa_in_t = wa_in_t.reshape(-1, D)
    wa_out = wa_out.reshape(AW, D)
    wkv = wkv.reshape(D, 2 * LANES)

    rc, rs1, rs2 = _rope_tables(T)
    ws = a_ws[0]
    g_kv = kv_norm_g.reshape(1, D)
    bkv = b_kv.reshape(1, -1)
    g_f = final_norm_g.reshape(1, D)
    sinks = b_sinks.reshape(1, 16)
    h1, sv, vhat, rstd, k4, v4, kt, vt, wb_in, wb_out = _a_fwd(
        xs, z, ln_g, ln_b, ws, a_bs[0], wa_out, g_kv, wkv, bkv, rc, rs1, rs2, [b_w_in[0], b_w_out[0]])
    wb_out = wb_out.reshape(-1, D)
    q, g2, o, dh2, dh2_b, loss, d_gf = _b_fwd(h1, b_norm_g, wb_in, b_bq, rc, rs1, rs2, k4, vt, sinks, wb_out, g_f, tgt)
    dh1p, dz2, n2, y2, dk, dv, d_bq, d_gb, d_sink = _b_bwd(dh2, h1, q, g2, o, k4, v4, kt, sinks, wb_out, wb_in,
                                                           b_norm_g, rc, rs1, rs2)
    d_sink = d_sink[:, :4].reshape(2, 2, 4).transpose(0, 2, 1).reshape(1, 16)
    gw_b_in = _wgrad(n2, dz2, N_DEV, "wgrad_b_in", bt=1024)
    gw_b_out = _wgrad(y2, dh2_b, 1, "wgrad_b_out", bt=1024).reshape(N_DEV, -1, D)
    (dz, gw_a_out, gw_kv, dh1_f, d_gkv, d_bkv, d_lng, d_lnb, d_ws, d_bst, r_b_in, r_b_out) = _a_bwd(
        dh1p, dk, dv, h1, g_kv, wkv, wa_out, ws, ln_g, ln_b, z, sv, vhat, rstd, rc, rs1, rs2, [gw_b_in, gw_b_out])
    dx, n1, d_ga, r_a_out, r_kv = _a_in_bwd(dz, wa_in_t, xs, dh1_f, g_a, [gw_a_out, gw_kv])
    small = [(_view2d(d_ws), None, None), (d_bst, G, "T")] + [(_view2d(a), None, None) for a in (
        d_gkv, d_bkv, d_gb, d_bq, d_sink, d_gf, d_ga, d_lng, d_lnb, loss)]
    used = sum(_nrows(G * CHUNK if flag else a.size) for a, _, flag in small)
    per = -(-used // (SUBLANES * N_DEV)) * SUBLANES
    small_pack = _pack_small(small, per * N_DEV, "pack_small").reshape(N_DEV, per, LANES)
    r_a_in, full_small = _wgrad_exchange(n1, dz, me.reshape(1), small_pack, "wgrad_a_in")

    g_a_in, d_a_in, nm_a_in, nv_a_in = _sum_adam(r_a_in, a_w_in[0], m_a_w_in[0], v_a_w_in[0], "adam_a_in")
    g_a_out, d_a_out, nm_a_out, nv_a_out = _sum_adam(r_a_out, a_w_out[0], m_a_w_out[0], v_a_w_out[0], "adam_a_out")
    g_kvw, d_kvw, nm_kvw, nv_kvw = _sum_adam(r_kv, w_kv, m_w_kv, v_w_kv, "adam_kv")
    g_b_in, d_b_in, nm_b_in, nv_b_in = _sum_adam(r_b_in, b_w_in[0], m_b_w_in[0], v_b_w_in[0], "adam_b_in")
    g_b_out, d_b_out, nm_b_out, nv_b_out = _sum_adam(r_b_out, b_w_out[0], m_b_w_out[0], v_b_w_out[0], "adam_b_out")

    full_small = full_small.reshape(N_DEV * per, LANES)
    reps = [(a_ws, m_a_ws, v_a_ws), (a_bs, m_a_bs, v_a_bs), (kv_norm_g, m_kv_norm_g, v_kv_norm_g),
            (b_kv, m_b_kv, v_b_kv), (b_norm_g, m_b_norm_g, v_b_norm_g), (b_bq, m_b_bq, v_b_bq),
            (b_sinks, m_b_sinks, v_b_sinks), (final_norm_g, m_final_norm_g, v_final_norm_g)]
    shards = [(a_norm_g, m_a_norm_g, v_a_norm_g), (a_ln_g, m_a_ln_g, v_a_ln_g), (a_ln_b, m_a_ln_b, v_a_ln_b)]
    upd, loss = _small_update(full_small, me.reshape(1), [tuple(_view2d(t) for t in p) for p in reps],
                              [tuple(_view2d(t) for t in p) for p in shards], "adam_small")
    loss = loss[0, 0]
    sm_g, sd, snm, snv = ([upd[k][j].reshape(p[0].shape) for k, p in enumerate(reps + shards)] for j in range(4))

    def order(big, sm):
        a_in, a_out, kvw, b_in, b_out = big
        ws_, bs_, kvg, bkv_, bng, bq_, snk, fng, ang, alng, alnb = sm
        return (ang, a_in[None], alng, alnb, ws_, bs_, a_out[None], kvg, kvw, bkv_, bng, b_in[None], bq_, snk,
                b_out[None], fng)

    grads = order((g_a_in, g_a_out, g_kvw, g_b_in, g_b_out), sm_g)
    deltas = order((d_a_in, d_a_out, d_kvw, d_b_in, d_b_out), sd)
    new_m = order((nm_a_in, nm_a_out, nm_kvw, nm_b_in, nm_b_out), snm)
    new_v = order((nv_a_in, nv_a_out, nv_kvw, nv_b_in, nv_b_out), snv)
    return (loss, dx[None], *grads, *deltas, *new_m, *new_v)
```

```python
import functools

import jax
import jax.numpy as jnp
import numpy as np
from jax import lax
from jax.experimental import pallas as pl
from jax.experimental.pallas import tpu as pltpu

CHUNK = 128
HEAD_DIM = 64
ROPE_THETA = 10000.0
EPS = 1e-5
ADAM_LR = 0.001
ADAM_B1 = 0.9
ADAM_B2 = 0.999
ADAM_EPS = 1e-08
ADAM_WD = 0.01
ADAM_STEP = 10
N_DEV = 8
LANES = 128
NEG = -1e30

BF = jnp.bfloat16
F32 = jnp.float32
MESH = pl.DeviceIdType.MESH
AXES = ("x", "y", "c")
VMEM_LIMIT = 56 * 1024 * 1024


def _dot(a, b):
    return jnp.dot(a, b, preferred_element_type=F32)


def _dot_nt(a, b):
    return lax.dot_general(a, b, (((1,), (1,)), ((), ())), preferred_element_type=F32)


def _dot_tn(a, b):
    return lax.dot_general(a, b, (((0,), (0,)), ((), ())), preferred_element_type=F32)


def _const_spec(shape):
    nd = len(shape)
    return pl.BlockSpec(shape, lambda *_: (0,) * nd, pipeline_mode=pl.Buffered(1))


def _acc_spec(shape):
    nd = len(shape)
    return pl.BlockSpec(shape, lambda *_: (0,) * nd)


def _row_spec(tm, width):
    return pl.BlockSpec((tm, width), lambda i: (i, 0))


def _col_spec(tm, height):
    return pl.BlockSpec((height, tm), lambda i: (0, i))


def _params(sem):
    return pltpu.CompilerParams(dimension_semantics=sem, vmem_limit_bytes=VMEM_LIMIT)


def _rot(x, c, s1, s2):
    return x * c + pltpu.roll(x, 96, 1) * s1 + pltpu.roll(x, 32, 1) * s2


def _rot_bwd(d, c, s1, s2):
    return d * c + pltpu.roll(d * s1, 32, 1) + pltpu.roll(d * s2, 96, 1)


def _silu_parts(g):
    sg = jax.nn.sigmoid(g)
    return g * sg, sg * (1.0 + g * (1.0 - sg))


def _rms_bwd(dn, xh, r, g):
    a = dn * g
    return r * (a - xh * jnp.mean(a * xh, axis=-1, keepdims=True))


def _lane_lo(shape):
    return lax.broadcasted_iota(jnp.int32, shape, 1) < HEAD_DIM


def _split4(t):
    lo = _lane_lo(t.shape)
    tr = pltpu.roll(t, HEAD_DIM, 1)
    z = jnp.zeros_like(t)
    return jnp.concatenate([jnp.where(lo, t, z), jnp.where(lo, z, tr), jnp.where(lo, tr, z), jnp.where(lo, z, t)], axis=1)


def _stack_pairs(t, h):
    return jnp.concatenate([t[:, (h * 4 + j) * LANES:(h * 4 + j + 1) * LANES] for j in range(4)], axis=0)


def _upper():
    shape = (CHUNK, 4 * CHUNK)
    return lax.broadcasted_iota(jnp.int32, shape, 0) > (lax.broadcasted_iota(jnp.int32, shape, 1) & (CHUNK - 1))


def _band_rows(tile_ref, before_ref, c, h):
    a = slice(2 * h * LANES, (2 * h + 1) * LANES)
    b = slice((2 * h + 1) * LANES, (2 * h + 2) * LANES)
    cur = slice(c * CHUNK, (c + 1) * CHUNK)

    def prev(cols):
        return before_ref[:, cols] if c == 0 else tile_ref[(c - 1) * CHUNK:c * CHUNK, cols]

    return jnp.concatenate([prev(a), tile_ref[cur, a], prev(b), tile_ref[cur, b]], axis=0)


def _band_cols(tile_ref, before_ref, c, h):
    a = slice(2 * h * LANES, (2 * h + 1) * LANES)
    b = slice((2 * h + 1) * LANES, (2 * h + 2) * LANES)

    def prev(rows):
        return before_ref[0, rows, :] if c == 0 else tile_ref[c - 1, rows, :]

    return jnp.concatenate([prev(a), tile_ref[c, a, :], prev(b), tile_ref[c, b, :]], axis=1)


def _band_specs(tm):
    nc = tm // CHUNK

    def before(i):
        return jnp.maximum(i * nc - 1, 0)

    return (pl.BlockSpec((tm, 4 * LANES), lambda i: (i, 0)),
            pl.BlockSpec((CHUNK, 4 * LANES), lambda i: (before(i), 0)),
            pl.BlockSpec((nc, 4 * LANES, CHUNK), lambda i: (i, 0, 0)),
            pl.BlockSpec((1, 4 * LANES, CHUNK), lambda i: (before(i), 0, 0)))


def _fold(t, upper, has_prev=None):
    out = []
    for k in range(2):
        prev = t[2 * k * CHUNK:(2 * k + 1) * CHUNK]
        if has_prev is not None:
            prev = jnp.where(has_prev, prev, NEG)
        out.append(jnp.where(upper, prev, t[(2 * k + 1) * CHUNK:(2 * k + 2) * CHUNK]))
    return out


def _unfold(fa, fb, upper):
    z = jnp.zeros_like(fa)
    return jnp.concatenate([jnp.where(upper, fa, z), jnp.where(upper, z, fa),
                            jnp.where(upper, fb, z), jnp.where(upper, z, fb)], axis=0)


def _sink_tile(s_ref):
    shape = (4, 4 * LANES)
    row = lax.broadcasted_iota(jnp.int32, shape, 0)
    pair = lax.broadcasted_iota(jnp.int32, shape, 1) // LANES
    idx = (row // 2) * 8 + pair * 2 + row % 2
    tile = jnp.zeros(shape, F32)
    for n in range(16):
        tile = jnp.where(idx == n, s_ref[0, n], tile)
    return tile


def _softmax_sink(f, sink):
    m = jnp.maximum(jnp.max(f, axis=0, keepdims=True), sink)
    p = jnp.exp(f - m)
    es = jnp.exp(sink - m)
    inv = 1.0 / (jnp.sum(p, axis=0, keepdims=True) + es)
    return p * inv, es * inv


class _Riding:
    def __init__(self, shards, gathered, stages, sems, n_steps):
        self.shards, self.stages, self.n_steps = shards, stages, n_steps
        ssem, rsem, lsem = sems
        self.gathers = [_TwoLevel(stages[k], gathered[k], ssem.at[k], rsem.at[k], lsem.at[k])
                        for k in range(len(shards))]

    def begin(self, i):
        @pl.when(i == 0)
        def _():
            for shard, stage, g in zip(self.shards, self.stages, self.gathers):
                stage[...] = shard[...].astype(stage.dtype)
                g.start()

    def end(self, i):
        @pl.when(i == self.n_steps // 2)
        def _():
            for g in self.gathers:
                g.forward()

        @pl.when(i == self.n_steps - 1)
        def _():
            for g in self.gathers:
                g.finish()

    @staticmethod
    def specs(later):
        nl = len(later)
        hbm = pl.BlockSpec(memory_space=pl.ANY)
        return ([_const_spec(w.shape) for w in later], [hbm] * nl,
                tuple(jax.ShapeDtypeStruct((N_DEV,) + w.shape, BF) for w in later),
                [pltpu.VMEM(w.shape, BF) for w in later] + _direct_sems(nl))


PASS_MASKS = ((0, 1, 2, 5, 4, 3, 6, 7), (0, 1, 4, 3, 2, 5, 6, 7))


def _in_proj(x, w_shard, vec_shards, me, later):
    T, D = x.shape
    SH = w_shard.shape[1]
    TM = min(1024, T)
    nT = T // TM
    nl = len(later)
    nv = len(vec_shards)
    widths = [v.shape[1] for v in vec_shards]
    offsets = [sum(widths[:k]) for k in range(nv)]
    vec_shape = (SUBLANES, sum(widths))
    ds = widths[0]
    last = N_DEV - 1
    masks = jnp.asarray(np.array(PASS_MASKS, np.int32).reshape(-1))

    def slot(p, me_ref, masks_ref):
        return me_ref[0] ^ masks_ref[(me_ref[0] & 1) * N_DEV + p]

    def body(me_ref, masks_ref, x_ref, wsh_ref, *rest):
        vsh_refs, rest = rest[:nv], rest[nv:]
        shards, rest = rest[:nl], rest[nl:]
        (z_ref, wt_ref), rest = rest[:2], rest[2:]
        vout_refs, rest = rest[:nv], rest[nv:]
        gathered, rest = rest[:nl], rest[nl:]
        (w_scr, vec_scr, vstage, n1_scr, ga_scr, w_s, w_r, v_s, v_r, v_l), rest = rest[:10], rest[10:]
        stages, sems = rest[:nl], rest[nl:]
        p, i = pl.program_id(0), pl.program_id(1)
        me = _my_index()
        wg = _RelayGather(w_scr, w_s, w_r)
        vg = _Direct(vstage, vec_scr, v_s, v_r, v_l, scatter=False)
        lg = [_TwoLevel(stages[k], gathered[k], sems[0].at[k], sems[1].at[k], sems[2].at[k]) for k in range(nl)]

        def at_pass(k):
            return (p == k) & (i == 0)

        c = lax.axis_index("c")

        @pl.when(at_pass(0))
        def _():
            for ref, off, wd in zip(vsh_refs, offsets, widths):
                vstage[:, off:off + wd] = jnp.broadcast_to(ref[...], (SUBLANES, wd))
            vg.start()
            w_scr[me] = wsh_ref[...].astype(BF)
            wg.send_own(0).start()

            @pl.when(c == 1)
            def _():
                wg.send_own(1).start()

            @pl.when(c == 0)
            def _():
                wg.send_own(2).start()

            vg.finish()
            for j in range(N_DEV):
                ga_scr[:, j * ds:(j + 1) * ds] = vec_scr[j, 0:1, 0:ds]
                for ref, off, wd in zip(vout_refs, offsets, widths):
                    ref[:, j * wd:(j + 1) * wd] = vec_scr[j, 0:1, off:off + wd]

        @pl.when(at_pass(1))
        def _():
            wg.wait_sibling()

        for first, second, landed_first, landed_second in ((1, 2, wg.on_x, wg.on_y), (2, 1, wg.on_y, wg.on_x)):
            mine = c == (1 if first == 1 else 0)

            @pl.when(at_pass(2) & mine)
            def _(second=second, landed_first=landed_first):
                wg.send_own(second).start()
                landed_first()

            @pl.when(at_pass(3) & mine)
            def _(second=second):
                wg.wait_passed(second - 1)

            @pl.when(at_pass(4) & mine)
            def _(landed_second=landed_second):
                landed_second()

            @pl.when(at_pass(5) & mine)
            def _(first=first):
                wg.wait_passed(first - 1)

        @pl.when(at_pass(4))
        def _():
            for k in range(nl):
                stages[k][...] = shards[k][...].astype(BF)
                lg[k].start()

        @pl.when(at_pass(6))
        def _():
            wg.on_diag()

        @pl.when(at_pass(7))
        def _():
            wg.wait_passed(2)

        @pl.when(p == 0)
        def _():
            xv = x_ref[...]
            r1 = lax.rsqrt(jnp.mean(xv * xv, axis=-1, keepdims=True) + EPS)
            n1_scr[i] = (xv * r1 * ga_scr[...]).astype(BF)

        z_ref[...] = _dot(n1_scr[i], w_scr[slot(p, me_ref, masks_ref)]).astype(BF)

        @pl.when(i == 0)
        def _():
            wt_ref[0] = w_scr[slot(p, me_ref, masks_ref)].T

        @pl.when((p == last) & (i == nT - 1))
        def _():
            wg.wait_sends()
            for g in lg:
                g.forward()
            for g in lg:
                g.finish()

    hbm = pl.BlockSpec(memory_space=pl.ANY)
    dma = pltpu.SemaphoreType.DMA
    S = jax.ShapeDtypeStruct
    def whole(shape):
        return pl.BlockSpec(shape, lambda p, i, m, t: (0, 0))

    def once(shape):
        return pl.BlockSpec(shape, lambda p, i, m, t: (0, 0), pipeline_mode=pl.Buffered(1))

    grid_spec = pltpu.PrefetchScalarGridSpec(
        num_scalar_prefetch=2, grid=(N_DEV, nT),
        in_specs=[pl.BlockSpec((TM, D), lambda p, i, m, t: (jnp.where(p == 0, i, nT - 1), 0)), once(w_shard.shape)]
        + [once(v.shape) for v in vec_shards] + [once(w.shape) for w in later],
        out_specs=[pl.BlockSpec((TM, SH), lambda p, i, m, t: (i, slot(p, m, t))),
                   pl.BlockSpec((1, SH, D), lambda p, i, m, t: (slot(p, m, t), 0, 0))]
        + [whole((1, N_DEV * wd)) for wd in widths] + [hbm] * nl,
        scratch_shapes=[pltpu.VMEM((N_DEV, D, SH), BF), pltpu.VMEM((N_DEV,) + vec_shape, F32),
                        pltpu.VMEM(vec_shape, F32), pltpu.VMEM((nT, TM, D), BF), pltpu.VMEM((1, D), F32),
                        dma((8,)), dma((8,)), dma((7,)), dma((7,)), dma]
        + [pltpu.VMEM(w.shape, BF) for w in later] + _direct_sems(nl))
    return pl.pallas_call(
        body, name="a_in_proj", grid_spec=grid_spec,
        out_shape=(S((T, N_DEV * SH), BF), S((N_DEV, SH, D), BF)) + tuple(S((1, N_DEV * wd), F32) for wd in widths)
        + tuple(S((N_DEV,) + w.shape, BF) for w in later),
        compiler_params=_params(("arbitrary", "arbitrary")),
    )(me, masks, x, w_shard, *vec_shards, *later)


def _a_fwd(x, z, ln_g, ln_b, ws, bs, wa_out, g_kv, w_kv, b_kv, rc, rs1, rs2, later):
    T, D = x.shape
    AW = wa_out.shape[0]
    G = ws.shape[0]
    TM = min(512, T)
    nT = T // TM
    nC = TM // CHUNK
    nl = len(later)

    def body(x_ref, u_ref, v_ref, gt_ref, lng_ref, lnb_ref, ws_ref, bs_ref, waout_ref, gkv_ref, wkv_ref, bkv_ref,
             rc_ref, rs1_ref, rs2_ref, *rest):
        shards, rest = rest[:nl], rest[nl:]
        (h1_ref, sv_ref, vhat_ref, rstd_ref, k4_ref, v4_ref, kt_ref, vt_ref), rest = rest[:8], rest[8:]
        gathered, sv_scr, stages, sems = rest[:nl], rest[nl], rest[nl + 1:2 * nl + 1], rest[2 * nl + 1:]
        i = pl.program_id(0)
        riding = _Riding(shards, gathered, stages, sems, nT)
        riding.begin(i)
        xv = x_ref[...]
        u = u_ref[...].astype(F32)
        v = v_ref[...].astype(F32)
        gt = gt_ref[...].astype(F32)
        mu = jnp.mean(v, axis=-1, keepdims=True)
        xc = v - mu
        rstd = lax.rsqrt(jnp.mean(xc * xc, axis=-1, keepdims=True) + EPS)
        vhat = xc * rstd
        vln = (vhat * lng_ref[...] + lnb_ref[...]).astype(BF)
        tri = lax.broadcasted_iota(jnp.int32, (CHUNK, CHUNK), 0) >= lax.broadcasted_iota(jnp.int32, (CHUNK, CHUNK), 1)
        bst = jnp.concatenate([bs_ref[...], jnp.zeros((CHUNK - G, CHUNK), F32)], axis=0).T
        for g in range(G):
            wsm = jnp.where(tri, ws_ref[g], 0.0).astype(BF)
            bias = bst[:, g:g + 1]
            for c in range(nC):
                blk = vln[c * CHUNK:(c + 1) * CHUNK, g * CHUNK:(g + 1) * CHUNK]
                sv_scr[c * CHUNK:(c + 1) * CHUNK, g * CHUNK:(g + 1) * CHUNK] = _dot(wsm, blk) + bias
        sv = sv_scr[...]
        silu, _ = _silu_parts(gt)
        y = (u * sv * silu).astype(BF)
        h1 = xv + _dot(y, waout_ref[...])
        h1_ref[...] = h1
        sv_ref[...] = sv.astype(BF)
        vhat_ref[...] = vhat.astype(BF)
        rstd_ref[...] = jnp.broadcast_to(rstd, rstd_ref.shape)
        rkv = lax.rsqrt(jnp.mean(h1 * h1, axis=-1, keepdims=True) + EPS)
        nkv = (h1 * rkv * gkv_ref[...]).astype(BF)
        kv = _dot(nkv, wkv_ref[...]) + bkv_ref[...]
        k_rot = _rot(kv[:, :LANES], rc_ref[...], rs1_ref[...], rs2_ref[...])
        for src, ref, tref in ((k_rot, k4_ref, kt_ref), (kv[:, LANES:], v4_ref, vt_ref)):
            t4 = _split4(src)
            ref[...] = t4.astype(BF)
            for c in range(nC):
                for b in range(4):
                    blk = t4[c * CHUNK:(c + 1) * CHUNK, b * LANES:(b + 1) * LANES]
                    tref[c, b * LANES:(b + 1) * LANES, :] = blk.T.astype(BF)
        riding.end(i)

    row = functools.partial(_row_spec, TM)
    zcol = [pl.BlockSpec((TM, AW), functools.partial(lambda k, i: (i, k), k)) for k in range(3)]
    tr = pl.BlockSpec((nC, 4 * LANES, CHUNK), lambda i: (i, 0, 0))
    r_in, r_out, r_shape, r_scratch = _Riding.specs(later)
    S = jax.ShapeDtypeStruct
    return pl.pallas_call(
        body, name="a_fwd", grid=(nT,),
        in_specs=[row(D)] + zcol + [_const_spec((1, AW)), _const_spec((1, AW)),
                  _const_spec(ws.shape), _const_spec(bs.shape), _const_spec(wa_out.shape), _const_spec((1, D)),
                  _const_spec(w_kv.shape), _const_spec((1, 2 * LANES)), row(LANES), row(LANES), row(LANES)] + r_in,
        out_specs=[row(D), row(AW), row(AW), row(LANES), row(4 * LANES), row(4 * LANES), tr, tr] + r_out,
        out_shape=(S((T, D), F32), S((T, AW), BF), S((T, AW), BF), S((T, LANES), F32),
                   S((T, 4 * LANES), BF), S((T, 4 * LANES), BF),
                   S((T // CHUNK, 4 * LANES, CHUNK), BF), S((T // CHUNK, 4 * LANES, CHUNK), BF)) + r_shape,
        scratch_shapes=[pltpu.VMEM((TM, AW), F32)] + r_scratch,
        compiler_params=_params(("arbitrary",)),
    )(x, z, z, z, ln_g, ln_b, ws, bs, wa_out, g_kv, w_kv, b_kv, rc, rs1, rs2, *later)


def _b_fwd(h1, g_b, wb_in, bq, rc, rs1, rs2, k4, vt, sinks, wb_out, g_f, target):
    T, D = h1.shape
    BW = wb_out.shape[0]
    SH = wb_in.shape[2]
    TM = min(512, T)
    nC = TM // CHUNK
    nP = BW // LANES

    def body(h1_ref, gb_ref, wbin_ref, bq_ref, rc_ref, rs1_ref, rs2_ref, k4_ref, k4p_ref, vt_ref, vtp_ref, sinks_ref,
             wbout_ref, gf_ref, tgt_ref, q_ref, g2_ref, o_ref, dh2_ref, dh2b_ref, loss_ref, dgf_ref, z_scr, o_scr):
        i = pl.program_id(0)
        sink = _sink_tile(sinks_ref)

        @pl.when(i == 0)
        def _():
            loss_ref[...] = jnp.zeros_like(loss_ref)
            dgf_ref[...] = jnp.zeros_like(dgf_ref)

        h1v = h1_ref[...]
        r2 = lax.rsqrt(jnp.mean(h1v * h1v, axis=-1, keepdims=True) + EPS)
        n2 = (h1v * r2 * gb_ref[...]).astype(BF)
        for j in range(N_DEV):
            z_scr[:, j * SH:(j + 1) * SH] = _dot(n2, wbin_ref[j])
        c_t, s1_t, s2_t = rc_ref[...], rs1_ref[...], rs2_ref[...]
        for p in range(nP):
            cols = slice(p * LANES, (p + 1) * LANES)
            qp = _rot(z_scr[:, cols] + bq_ref[:, cols], c_t, s1_t, s2_t) * (HEAD_DIM ** -0.5)
            q_ref[:, cols] = qp.astype(BF)
        g2 = z_scr[:, BW:]
        g2_ref[...] = g2.astype(BF)
        upper = _upper()
        for c in range(nC):
            ci = i * nC + c
            rows = slice(c * CHUNK, (c + 1) * CHUNK)
            qc = q_ref[rows, :]
            for h in range(2):
                st = _dot_nt(_band_rows(k4_ref, k4p_ref, c, h), _stack_pairs(qc, h))
                fa, fb = _fold(st, upper, ci > 0)
                pa, _ = _softmax_sink(fa, sink[2 * h:2 * h + 1, :])
                pb, _ = _softmax_sink(fb, sink[2 * h + 1:2 * h + 2, :])
                ot = _dot(_band_cols(vt_ref, vtp_ref, c, h), _unfold(pa, pb, upper).astype(BF))
                for j in range(4):
                    o_scr[rows, (h * 4 + j) * LANES:(h * 4 + j + 1) * LANES] = ot[:, j * CHUNK:(j + 1) * CHUNK].T
        o = o_scr[...]
        o_ref[...] = o.astype(BF)
        silu, _ = _silu_parts(g2)
        h2 = h1v + _dot((o * silu).astype(BF), wbout_ref[...])
        rf = lax.rsqrt(jnp.mean(h2 * h2, axis=-1, keepdims=True) + EPS)
        xh = h2 * rf
        gf = gf_ref[...]
        err = xh * gf - tgt_ref[...]
        dyf = err * (1.0 / D)
        dh2 = _rms_bwd(dyf, xh, rf, gf)
        dh2_ref[...] = dh2
        dh2b_ref[...] = dh2.astype(BF)
        loss_ref[...] += 0.5 * jnp.sum(jnp.mean(err * err, axis=-1, keepdims=True), axis=0, keepdims=True)
        dgf_ref[...] += jnp.sum(dyf * xh, axis=0, keepdims=True)

    row = functools.partial(_row_spec, TM)
    rows_tile, rows_before, cols_tile, cols_before = _band_specs(TM)
    S = jax.ShapeDtypeStruct
    return pl.pallas_call(
        body, name="b_fwd", grid=(T // TM,),
        in_specs=[row(D), _const_spec((1, D)), _const_spec(wb_in.shape), _const_spec((1, BW)), row(LANES), row(LANES),
                  row(LANES), rows_tile, rows_before, cols_tile, cols_before, pl.BlockSpec(memory_space=pltpu.SMEM),
                  _const_spec(wb_out.shape), _const_spec((1, D)), row(D)],
        out_specs=[row(BW), row(BW), row(BW), row(D), row(D), _acc_spec((1, 1)), _acc_spec((1, D))],
        out_shape=(S((T, BW), BF), S((T, BW), BF), S((T, BW), BF), S((T, D), F32), S((T, D), BF), S((1, 1), F32),
                   S((1, D), F32)),
        scratch_shapes=[pltpu.VMEM((TM, 2 * BW), F32), pltpu.VMEM((TM, BW), F32)],
        compiler_params=_params(("arbitrary",)),
    )(h1, g_b, wb_in, bq, rc, rs1, rs2, k4, k4, vt, vt, sinks, wb_out, g_f, target)


def _b_bwd(dh2, h1, q, g2, o, k4, v4, kt, sinks, wb_out, wb_in, g_b, rc, rs1, rs2):
    T, D = h1.shape
    BW = wb_out.shape[0]
    SH = wb_in.shape[2]
    TM = min(512, T)
    nT = T // TM
    nC = TM // CHUNK
    nP = BW // LANES

    def body(dh2_ref, h1_ref, q_ref, g2_ref, o_ref, k4_ref, k4p_ref, v4_ref, v4p_ref, kt_ref, ktp_ref, sinks_ref,
             wbout_ref, wbin_ref, gb_ref, rc_ref, rs1_ref, rs2_ref,
             dh1_ref, dz2_ref, n2_ref, y2_ref, dk_ref, dv_ref, dbq_ref, dgb_ref, dsink_ref, do_scr, dq_scr, dsacc_scr):
        i = pl.program_id(0)
        sink = _sink_tile(sinks_ref)

        @pl.when(i == 0)
        def _():
            dk_ref[...] = jnp.zeros_like(dk_ref)
            dv_ref[...] = jnp.zeros_like(dv_ref)
            dbq_ref[...] = jnp.zeros_like(dbq_ref)
            dgb_ref[...] = jnp.zeros_like(dgb_ref)
            dsacc_scr[...] = jnp.zeros_like(dsacc_scr)

        dh2 = dh2_ref[...]
        dy2 = _dot_nt(dh2.astype(BF), wbout_ref[...])
        silu, dsilu = _silu_parts(g2_ref[...].astype(F32))
        do_scr[...] = (dy2 * silu).astype(BF)
        dy2, silu, dsilu = dy2.astype(BF), silu.astype(BF), dsilu.astype(BF)
        ob = o_ref[...]
        y2_ref[...] = (ob * silu).T
        dz2_ref[:, BW:] = dy2 * ob * dsilu
        upper = _upper()
        lo = _lane_lo((2 * CHUNK, LANES))
        for c in range(nC):
            ci = i * nC + c
            rows = slice(c * CHUNK, (c + 1) * CHUNK)
            pci = jnp.maximum(ci - 1, 0)
            prev = pl.multiple_of(pci * CHUNK, CHUNK)
            cur = pl.multiple_of(ci * CHUNK, CHUNK)
            qc = q_ref[rows, :]
            doc = do_scr[rows, :]
            dkb = jnp.zeros((2 * CHUNK, LANES), F32)
            dvb = jnp.zeros((2 * CHUNK, LANES), F32)
            for h in range(2):
                qs = _stack_pairs(qc, h)
                dos = _stack_pairs(doc, h)
                fa, fb = _fold(_dot_nt(_band_rows(k4_ref, k4p_ref, c, h), qs), upper, ci > 0)
                dfa, dfb = _fold(_dot_nt(_band_rows(v4_ref, v4p_ref, c, h), dos), upper)
                folded = []
                for k, (f, df) in enumerate(((fa, dfa), (fb, dfb))):
                    p, ps = _softmax_sink(f, sink[2 * h + k:2 * h + k + 1, :])
                    delta = jnp.sum(p * df, axis=0, keepdims=True)
                    dsacc_scr[2 * h + k:2 * h + k + 1, :] -= ps * delta
                    folded.append((p, p * (df - delta)))
                pt = _unfold(folded[0][0], folded[1][0], upper).astype(BF)
                dst = _unfold(folded[0][1], folded[1][1], upper).astype(BF)
                dqt = _dot(_band_cols(kt_ref, ktp_ref, c, h), dst)
                for j in range(4):
                    dq_scr[rows, (h * 4 + j) * LANES:(h * 4 + j + 1) * LANES] = dqt[:, j * CHUNK:(j + 1) * CHUNK].T
                for acc_name, g in (("k", _dot(dst, qs)), ("v", _dot(pt, dos))):
                    a, b = g[:2 * CHUNK], g[2 * CHUNK:]
                    if h == 0:
                        part = jnp.where(lo, a + pltpu.roll(b, HEAD_DIM, 1), 0.0)
                    else:
                        part = jnp.where(lo, 0.0, pltpu.roll(a, HEAD_DIM, 1) + b)
                    if acc_name == "k":
                        dkb += part
                    else:
                        dvb += part
            dk_ref[pl.ds(prev, CHUNK), :] += dkb[:CHUNK]
            dk_ref[pl.ds(cur, CHUNK), :] += dkb[CHUNK:]
            dv_ref[pl.ds(prev, CHUNK), :] += dvb[:CHUNK]
            dv_ref[pl.ds(cur, CHUNK), :] += dvb[CHUNK:]
        c_t, s1_t, s2_t = rc_ref[...], rs1_ref[...], rs2_ref[...]
        for p in range(nP):
            cols = slice(p * LANES, (p + 1) * LANES)
            dqp = _rot_bwd(dq_scr[:, cols] * (HEAD_DIM ** -0.5), c_t, s1_t, s2_t)
            dbq_ref[:, cols] += jnp.sum(dqp, axis=0, keepdims=True)
            dz2_ref[:, cols] = dqp.astype(BF)
        h1v = h1_ref[...]
        r2 = lax.rsqrt(jnp.mean(h1v * h1v, axis=-1, keepdims=True) + EPS)
        xh = h1v * r2
        gb = gb_ref[...]
        n2_ref[...] = (xh * gb).astype(BF).T
        dn2 = None
        for j in range(N_DEV):
            part = _dot_nt(dz2_ref[:, j * SH:(j + 1) * SH], wbin_ref[j])
            dn2 = part if dn2 is None else dn2 + part
        dgb_ref[...] += jnp.sum(dn2 * xh, axis=0, keepdims=True)
        dh1_ref[...] = dh2 + _rms_bwd(dn2, xh, r2, gb)

        @pl.when(i == nT - 1)
        def _():
            lane = lax.broadcasted_iota(jnp.int32, dsink_ref.shape, 1)
            tot = jnp.zeros(dsink_ref.shape, F32)
            for j in range(4):
                tot += jnp.where(lane == j, jnp.sum(dsacc_scr[:, j * CHUNK:(j + 1) * CHUNK], axis=1, keepdims=True), 0.0)
            dsink_ref[...] = tot

    row = functools.partial(_row_spec, TM)
    rows_tile, rows_before, cols_tile, cols_before = _band_specs(TM)
    S = jax.ShapeDtypeStruct
    return pl.pallas_call(
        body, name="b_bwd", grid=(T // TM,),
        in_specs=[row(D), row(D), row(BW), row(BW), row(BW), rows_tile, rows_before, rows_tile, rows_before,
                  cols_tile, cols_before, pl.BlockSpec(memory_space=pltpu.SMEM), _const_spec(wb_out.shape), _const_spec(wb_in.shape),
                  _const_spec((1, D)), row(LANES), row(LANES), row(LANES)],
        out_specs=[row(D), row(2 * BW), _col_spec(TM, D), _col_spec(TM, BW), _acc_spec((T, LANES)),
                   _acc_spec((T, LANES)), _acc_spec((1, BW)), _acc_spec((1, D)), _acc_spec((4, LANES))],
        out_shape=(S((T, D), F32), S((T, 2 * BW), BF), S((D, T), BF), S((BW, T), BF), S((T, LANES), F32),
                   S((T, LANES), F32), S((1, BW), F32), S((1, D), F32), S((4, LANES), F32)),
        scratch_shapes=[pltpu.VMEM((TM, BW), BF), pltpu.VMEM((TM, BW), F32), pltpu.VMEM((4, 4 * CHUNK), F32)],
        compiler_params=_params(("arbitrary",)),
    )(dh2, h1, q, g2, o, k4, k4, v4, v4, kt, kt, sinks, wb_out, wb_in, g_b, rc, rs1, rs2)


def _a_bwd(dh1p, dk, dv, h1, g_kv, w_kv, wa_out, ws, ln_g, ln_b, z, sv, vhat, rstd, rc, rs1, rs2, ready):
    T, D = h1.shape
    AW = wa_out.shape[0]
    G = ws.shape[0]
    TM = min(256, T)
    nT = T // TM
    nC = TM // CHUNK
    nr = len(ready)

    def body(dh1p_ref, dk_ref, dv_ref, h1_ref, gkv_ref, wkv_ref, waout_ref, ws_ref, lng_ref,
             lnb_ref, u_ref, gt_ref, sv_ref, vhat_ref, rstd_ref, rc_ref, rs1_ref, rs2_ref, *rest):
        ready_refs, rest = rest[:nr], rest[nr:]
        (dz_ref, gwo_ref, gwk_ref, dh1f_ref, dgkv_ref, dbkv_ref, dlng_ref, dlnb_ref,
         dws_ref, dbs_ref), rest = rest[:10], rest[10:]
        recv_refs, (dsv_scr, dvln_scr, acco_scr, acck_scr, ssem, rsem, lsem) = rest[:nr], rest[nr:]
        i = pl.program_id(0)
        exchanges = [_Direct(ready_refs[k], recv_refs[k], ssem.at[k], rsem.at[k], lsem.at[k], scatter=True)
                     for k in range(nr)]

        @pl.when(i == 0)
        def _():
            for e in exchanges:
                e.start()
            for r in (dgkv_ref, dbkv_ref, dlng_ref, dlnb_ref, dws_ref, dbs_ref, acco_scr, acck_scr):
                r[...] = jnp.zeros_like(r)

        dk_pre = _rot_bwd(dk_ref[...], rc_ref[...], rs1_ref[...], rs2_ref[...])
        dkv = jnp.concatenate([dk_pre, dv_ref[...]], axis=1)
        dbkv_ref[...] += jnp.sum(dkv, axis=0, keepdims=True)
        dkv_b = dkv.astype(BF)
        h1v = h1_ref[...]
        rkv = lax.rsqrt(jnp.mean(h1v * h1v, axis=-1, keepdims=True) + EPS)
        xh_kv = h1v * rkv
        gkv = gkv_ref[...]
        acck_scr[...] += _dot((xh_kv * gkv).astype(BF).T, dkv_b)
        dnkv = _dot_nt(dkv_b, wkv_ref[...])
        dgkv_ref[...] += jnp.sum(dnkv * xh_kv, axis=0, keepdims=True)
        dh1 = dh1p_ref[...] + _rms_bwd(dnkv, xh_kv, rkv, gkv)
        dh1_b = dh1.astype(BF)
        dh1f_ref[...] = dh1
        dy = _dot_nt(dh1_b, waout_ref[...]).astype(BF)
        silu, dsilu = _silu_parts(gt_ref[...].astype(F32))
        silu, dsilu = silu.astype(BF), dsilu.astype(BF)
        ub, svb = u_ref[...], sv_ref[...]
        us = ub * silu
        dys = dy * svb
        acco_scr[...] += _dot((us * svb).T, dh1_b)
        dz_ref[:, :AW] = dys * silu
        dz_ref[:, 2 * AW:] = dys * ub * dsilu
        dsv_scr[...] = dy * us
        vhat_v = vhat_ref[...].astype(F32)
        lng = lng_ref[...]
        vln_b = (vhat_v * lng + lnb_ref[...]).astype(BF)
        tri = lax.broadcasted_iota(jnp.int32, (CHUNK, CHUNK), 0) >= lax.broadcasted_iota(jnp.int32, (CHUNK, CHUNK), 1)
        lane = lax.broadcasted_iota(jnp.int32, (CHUNK, LANES), 1)
        dbs = jnp.zeros((CHUNK, LANES), F32)
        for g in range(G):
            wsm = jnp.where(tri, ws_ref[g], 0.0).astype(BF)
            cols = slice(g * CHUNK, (g + 1) * CHUNK)
            dws_g = None
            for c in range(nC):
                rows = slice(c * CHUNK, (c + 1) * CHUNK)
                dsv_cg = dsv_scr[rows, cols]
                dvln_scr[rows, cols] = _dot_tn(wsm, dsv_cg)
                part = _dot_nt(dsv_cg, vln_b[rows, cols])
                dws_g = part if dws_g is None else dws_g + part
                dbs += jnp.where(lane == g, jnp.sum(dsv_cg.astype(F32), axis=-1, keepdims=True), 0.0)
            dws_ref[g] += jnp.where(tri, dws_g, 0.0)
        dbs_ref[...] += dbs
        dvln = dvln_scr[...]
        dlng_ref[...] += jnp.sum(dvln * vhat_v, axis=0, keepdims=True)
        dlnb_ref[...] += jnp.sum(dvln, axis=0, keepdims=True)
        a = dvln * lng
        dvv = rstd_ref[:, 0:1] * (a - jnp.mean(a, axis=-1, keepdims=True)
                                  - vhat_v * jnp.mean(a * vhat_v, axis=-1, keepdims=True))
        dz_ref[:, AW:2 * AW] = dvv.astype(BF)

        @pl.when(i == nT - 1)
        def _():
            for j in range(N_DEV):
                gwo_ref[j] = acco_scr[j * (AW // N_DEV):(j + 1) * (AW // N_DEV)].astype(BF)
                gwk_ref[j] = acck_scr[j * (D // N_DEV):(j + 1) * (D // N_DEV)].astype(BF)
            for e in exchanges:
                e.finish()

    row = functools.partial(_row_spec, TM)
    hbm = pl.BlockSpec(memory_space=pl.ANY)
    S = jax.ShapeDtypeStruct
    gwo_shape, gwk_shape = (N_DEV, AW // N_DEV, D), (N_DEV, D // N_DEV, 2 * LANES)
    return pl.pallas_call(
        body, name="a_bwd", grid=(nT,),
        in_specs=[row(D), row(LANES), row(LANES), row(D), _const_spec((1, D)), _const_spec(w_kv.shape),
                  _const_spec(wa_out.shape), _const_spec(ws.shape),
                  _const_spec((1, AW)), _const_spec((1, AW)), pl.BlockSpec((TM, AW), lambda i: (i, 0)),
                  pl.BlockSpec((TM, AW), lambda i: (i, 2)), row(AW), row(AW), row(LANES),
                  row(LANES), row(LANES), row(LANES)] + [hbm] * nr,
        out_specs=[row(3 * AW), _const_spec(gwo_shape), _const_spec(gwk_shape), row(D),
                   _acc_spec((1, D)), _acc_spec((1, 2 * LANES)), _acc_spec((1, AW)),
                   _acc_spec((1, AW)), _acc_spec(ws.shape), _acc_spec((CHUNK, LANES))] + [hbm] * nr,
        out_shape=(S((T, 3 * AW), BF), S(gwo_shape, BF), S(gwk_shape, BF), S((T, D), F32),
                   S((1, D), F32), S((1, 2 * LANES), F32), S((1, AW), F32), S((1, AW), F32),
                   S(ws.shape, F32), S((CHUNK, LANES), F32)) + tuple(S(r.shape, r.dtype) for r in ready),
        scratch_shapes=[pltpu.VMEM((TM, AW), BF), pltpu.VMEM((TM, AW), F32), pltpu.VMEM((AW, D), F32),
                        pltpu.VMEM((D, 2 * LANES), F32)] + _direct_sems(nr),
        compiler_params=_params(("arbitrary",)),
    )(dh1p, dk, dv, h1, g_kv, w_kv, wa_out, ws, ln_g, ln_b, z, z, sv, vhat, rstd, rc, rs1, rs2, *ready)


def _a_in_bwd(dz, wa_in_t, x, dh1, g_a, ready):
    T, D = x.shape
    TM = min(512, T)
    nT = T // TM
    nr = len(ready)

    def body(dz_ref, wain_ref, x_ref, dh1_ref, ga_ref, *rest):
        ready_refs, (dx_ref, n1_ref, dga_ref), rest = rest[:nr], rest[nr:nr + 3], rest[nr + 3:]
        recv_refs, (ssem, rsem, lsem) = rest[:nr], rest[nr:]
        i = pl.program_id(0)
        exchanges = [_Direct(ready_refs[k], recv_refs[k], ssem.at[k], rsem.at[k], lsem.at[k], scatter=True)
                     for k in range(nr)]

        @pl.when(i == 0)
        def _():
            for e in exchanges:
                e.start()
            dga_ref[...] = jnp.zeros_like(dga_ref)

        xv = x_ref[...]
        r1 = lax.rsqrt(jnp.mean(xv * xv, axis=-1, keepdims=True) + EPS)
        xh = xv * r1
        ga = ga_ref[...]
        n1_ref[...] = (xh * ga).astype(BF).T
        dn1 = _dot(dz_ref[...], wain_ref[...])
        dga_ref[...] += jnp.sum(dn1 * xh, axis=0, keepdims=True)
        dx_ref[...] = dh1_ref[...] + _rms_bwd(dn1, xh, r1, ga)

        @pl.when(i == nT - 1)
        def _():
            for e in exchanges:
                e.finish()

    row = functools.partial(_row_spec, TM)
    hbm = pl.BlockSpec(memory_space=pl.ANY)
    S = jax.ShapeDtypeStruct
    return pl.pallas_call(
        body, name="a_in_bwd", grid=(nT,),
        in_specs=[row(dz.shape[1]), _const_spec(wa_in_t.shape), row(D), row(D), _const_spec((1, D))] + [hbm] * nr,
        out_specs=[row(D), _col_spec(TM, D), _acc_spec((1, D))] + [hbm] * nr,
        out_shape=(S((T, D), F32), S((D, T), BF), S((1, D), F32)) + tuple(S(r.shape, r.dtype) for r in ready),
        scratch_shapes=_direct_sems(nr),
        compiler_params=_params(("arbitrary",)),
    )(dz, wa_in_t, x, dh1, g_a, *ready)


def _wgrad(at, b, nblk, name, bt=512):
    K, T = at.shape
    N = b.shape[1] // nblk
    BT = min(bt, T)
    nt = T // BT

    def body(a_ref, b_ref, o_ref, acc):
        t = pl.program_id(0)

        @pl.when(t == 0)
        def _():
            acc[...] = jnp.zeros_like(acc)

        acc[...] += _dot(a_ref[...], b_ref[...])

        @pl.when(t == nt - 1)
        def _():
            for j in range(nblk):
                o_ref[j] = acc[:, j * N:(j + 1) * N].astype(BF)

    return pl.pallas_call(
        body, name=name, grid=(nt,),
        in_specs=[pl.BlockSpec((K, BT), lambda t: (0, t)), pl.BlockSpec((BT, nblk * N), lambda t: (t, 0))],
        out_specs=pl.BlockSpec((nblk, K, N), lambda t: (0, 0, 0)),
        out_shape=jax.ShapeDtypeStruct((nblk, K, N), BF),
        scratch_shapes=[pltpu.VMEM((K, nblk * N), F32)],
        compiler_params=_params(("arbitrary",)),
    )(at, b)


def _wgrad_exchange(a, b, me, small, name):
    K, T = a.shape
    N = b.shape[1] // N_DEV
    BT = T
    nt = T // BT
    last = N_DEV - 1
    n_chip = N_DEV // 2

    def far_of(k, core):
        return jnp.where((core == 0) & ((k == 1) | (k == 2)), k, n_chip - 1 - k)

    def block_of(s, me_i):
        k, odd = s // 2, s % 2
        core = me_i & 1
        return me_i ^ ((far_of(k, jnp.where(odd == 1, core, 1 - core)) << 1) | (1 - odd))

    H = K // 2

    def body(me_ref, a_ref, b_ref, small_ref, recv_ref, full_ref, *scratch):
        (acc, dstage, istage, half, relay, d_s, d_r, i_s, i_r, r_s, r_r, lsem, parts_scr, red_scr, e_s, e_r, e_l, g_s,
         g_r, g_l) = scratch
        s, t = pl.program_id(0), pl.program_id(1)
        x, y, c = (lax.axis_index(ax) for ax in AXES)
        ex = [_Direct(small_ref, parts_scr, e_s, e_r, e_l, scatter=True)]
        regather = _TwoLevel(red_scr, full_ref, g_s, g_r, g_l)

        def to_sibling(k, slot):
            return pltpu.make_async_remote_copy(src_ref=dstage.at[slot], dst_ref=half.at[k], send_sem=d_s.at[k],
                                                recv_sem=d_r.at[k], device_id=(x, y, 1 - c), device_id_type=MESH)

        def to_chip(k, slot):
            over_x = far_of(k, c) == 2
            px, py = jnp.where(over_x, 1 - x, x), jnp.where(over_x, y, 1 - y)
            return pltpu.make_async_remote_copy(src_ref=istage.at[slot], dst_ref=recv_ref.at[jnp.where(over_x, 1, 2)],
                                                send_sem=i_s.at[k], recv_sem=i_r.at[k], device_id=(px, py, c),
                                                device_id_type=MESH)

        def to_relay(j, slot):
            to = (1 - x, y, c) if j == 0 else (x, 1 - y, c)
            return pltpu.make_async_remote_copy(src_ref=istage.at[slot, pl.ds(j * H, H)], dst_ref=relay.at[j],
                                                send_sem=r_s.at[j], recv_sem=r_r.at[j], device_id=to,
                                                device_id_type=MESH)

        @pl.when((s == 0) & (t == 0))
        def _():
            for e in ex:
                e.start()

        acc[...] = _dot(a_ref[...], b_ref[...])

        @pl.when(t == nt - 1)
        def _():
            k = lax.div(s, 2)
            slot = lax.rem(k, 2)

            @pl.when(lax.rem(s, 2) == 0)
            def _():
                @pl.when(k >= 2)
                def _():
                    to_sibling(k - 2, slot).wait_send()

                dstage[slot] = acc[...].astype(BF)
                to_sibling(k, slot).start()

            @pl.when(lax.rem(s, 2) == 1)
            def _():
                to_sibling(k, slot).wait_recv()
                pair = acc[...] + half[k].astype(F32)

                @pl.when(k == 0)
                def _():
                    istage[slot] = pair.astype(BF)
                    for j in range(2):
                        to_relay(j, slot).start()

                @pl.when(k == 1)
                def _():
                    for j in range(2):
                        to_relay(j, slot).wait_recv()

                @pl.when(k == 2)
                def _():
                    for j in range(2):
                        to_relay(j, slot).wait_send()

                @pl.when(k == n_chip - 1)
                def _():
                    to_chip(1, slot).wait_send()
                    istage[slot] = pair.astype(BF)

                @pl.when((k == 1) | (k == 2))
                def _():
                    over_x = far_of(k, c) == 2
                    istage[slot, 0:H] = (pair[:H] + jnp.where(over_x, 0.0, relay[0].astype(F32))).astype(BF)
                    istage[slot, H:K] = (pair[H:] + jnp.where(over_x, relay[1].astype(F32), 0.0)).astype(BF)
                    to_chip(k, slot).start()

            @pl.when(s == last)
            def _():
                own = pltpu.make_async_copy(istage.at[slot], recv_ref.at[0], lsem)
                own.start()
                to_chip(2, 0).wait_send()
                to_sibling(n_chip - 2, 0).wait_send()
                to_sibling(n_chip - 1, 1).wait_send()
                for kk in (1, 2):
                    to_chip(kk, 0).wait_recv()
                own.wait()
                for e in ex:
                    e.finish()
                total = parts_scr[0]
                for dev in range(1, N_DEV):
                    total = total + parts_scr[dev]
                red_scr[...] = total
                regather.start()
                regather.forward()
                regather.finish()

    hbm = pl.BlockSpec(memory_space=pl.ANY)
    dma = pltpu.SemaphoreType.DMA
    grid_spec = pltpu.PrefetchScalarGridSpec(
        num_scalar_prefetch=1, grid=(N_DEV, nt),
        in_specs=[pl.BlockSpec((K, BT), lambda s, t, me_ref: (0, t), pipeline_mode=pl.Buffered(1)),
                  pl.BlockSpec((BT, N), lambda s, t, me_ref: (t, block_of(s, me_ref[0]))), hbm],
        out_specs=[hbm, hbm],
        scratch_shapes=[pltpu.VMEM((K, N), F32), pltpu.VMEM((2, K, N), BF), pltpu.VMEM((2, K, N), BF),
                        pltpu.VMEM((n_chip, K, N), BF), pltpu.VMEM((2, H, N), BF), dma((n_chip,)), dma((n_chip,)),
                        dma((n_chip - 1,)), dma((n_chip - 1,)), dma((2,)), dma((2,)), dma,
                        pltpu.VMEM(small.shape, F32), pltpu.VMEM(small.shape[1:], F32),
                        dma((last,)), dma((last,)), dma, dma((last,)), dma((last,)), dma])
    return pl.pallas_call(
        body, name=name, grid_spec=grid_spec,
        out_shape=[jax.ShapeDtypeStruct((n_chip - 1, K, N), BF), jax.ShapeDtypeStruct(small.shape, F32)],
        compiler_params=_params(("arbitrary", "arbitrary")),
    )(me, a, b, small)


def _my_index():
    return 4 * lax.axis_index("x") + 2 * lax.axis_index("y") + lax.axis_index("c")


def _peer(mask):
    x, y, c = (lax.axis_index(a) for a in AXES)
    return (x ^ ((mask >> 2) & 1), y ^ ((mask >> 1) & 1), c ^ (mask & 1))


def _dev_index(p):
    return 4 * p[0] + 2 * p[1] + p[2]


class _Direct:
    def __init__(self, src, dst, send_sems, recv_sems, local_sem, scatter):
        me = _my_index()
        self.own = pltpu.make_async_copy(src.at[me] if scatter else src, dst.at[me], local_sem)
        self.sends, self.recvs = [], []
        for k in range(1, N_DEV):
            p = _peer(k)
            pi = _dev_index(p)
            sems = dict(send_sem=send_sems.at[k - 1], recv_sem=recv_sems.at[k - 1], device_id=p, device_id_type=MESH)
            self.sends.append(pltpu.make_async_remote_copy(src_ref=src.at[pi] if scatter else src, dst_ref=dst.at[me],
                                                           **sems))
            self.recvs.append(pltpu.make_async_remote_copy(src_ref=src.at[me] if scatter else src, dst_ref=dst.at[pi],
                                                           **sems))

    def start(self):
        self.own.start()
        for cp in self.sends:
            cp.start()

    def finish(self):
        for cp in self.sends:
            cp.wait_send()
        for cp in self.recvs:
            cp.wait_recv()
        self.own.wait()


class _TwoLevel:
    def __init__(self, src, dst, send_sems, recv_sems, local_sem, own=True):
        x, y, c = (lax.axis_index(a) for a in AXES)
        self.me, self.sibling = (x, y, c), (x, y, 1 - c)
        self.chips = [(1 - x, y), (x, 1 - y), (1 - x, 1 - y)]
        self.src, self.dst, self.send_sems, self.recv_sems = src, dst, send_sems, recv_sems
        self.own = pltpu.make_async_copy(src, dst.at[_dev_index(self.me)], local_sem) if own else None

    def _copy(self, k, block, to, from_src=False):
        slot = self.dst.at[_dev_index(block)]
        return pltpu.make_async_remote_copy(src_ref=self.src if from_src else slot, dst_ref=slot,
                                            send_sem=self.send_sems.at[k], recv_sem=self.recv_sems.at[k],
                                            device_id=to, device_id_type=MESH)

    def _firsts(self):
        c = self.me[2]
        return [self._copy(0, self.me, self.sibling, True)] + [self._copy(1 + j, self.me, (*chip, c), True)
                                                               for j, chip in enumerate(self.chips)]

    def _passed(self):
        c = self.me[2]
        return [self._copy(4 + j, (*chip, c), self.sibling) for j, chip in enumerate(self.chips)]

    def start(self):
        if self.own is not None:
            self.own.start()
        for cp in self._firsts():
            cp.start()

    def wait_sibling(self):
        self._copy(0, self.sibling, self.me).wait_recv()

    def wait_chip_and_forward(self, j):
        self._copy(1 + j, (*self.chips[j], self.me[2]), self.me).wait_recv()
        self._passed()[j].start()

    def wait_passed(self, j):
        self._copy(4 + j, (*self.chips[j], 1 - self.me[2]), self.me).wait_recv()

    def wait_sends(self):
        for cp in self._firsts() + self._passed():
            cp.wait_send()
        if self.own is not None:
            self.own.wait()

    def forward(self):
        for j in range(3):
            self.wait_chip_and_forward(j)

    def finish(self):
        self.wait_sibling()
        for j in range(3):
            self.wait_passed(j)
        self.wait_sends()


class _RelayGather:
    def __init__(self, dst, send_sems, recv_sems):
        x, y, c = (lax.axis_index(a) for a in AXES)
        self.c = c
        self.sib, self.xn, self.yn, self.dg = (x, y, 1 - c), (1 - x, y, c), (x, 1 - y, c), (1 - x, 1 - y, c)
        self.me = (x, y, c)
        self.dst, self.send_sems, self.recv_sems = dst, send_sems, recv_sems
        self.half = dst.shape[1] // 2

    def _slot(self, dev, part=None):
        i = _dev_index(dev)
        if part is None:
            return self.dst.at[i]
        return self.dst.at[i, pl.ds(part * self.half, self.half)]

    def _copy(self, k, dev, to, part=None):
        ref = self._slot(dev, part)
        return pltpu.make_async_remote_copy(src_ref=ref, dst_ref=ref, send_sem=self.send_sems.at[k],
                                            recv_sem=self.recv_sems.at[k], device_id=to, device_id_type=MESH)

    def _other(self, dev):
        return (dev[0], dev[1], 1 - self.c)

    def start(self):
        for k, to in enumerate((self.sib, self.xn, self.yn)):
            self._copy(k, self.me, to).start()

    def send_own(self, k):
        return self._copy(k, self.me, (self.sib, self.xn, self.yn)[k])

    def wait_sibling(self):
        self._copy(0, self.sib, self.me).wait_recv()

    def on_x(self):
        self._copy(1, self.xn, self.me).wait_recv()
        self._copy(3, self.xn, self.yn, part=0).start()
        self._copy(5, self.xn, self.sib).start()

    def on_y(self):
        self._copy(2, self.yn, self.me).wait_recv()
        self._copy(4, self.yn, self.xn, part=1).start()
        self._copy(6, self.yn, self.sib).start()

    def on_diag(self):
        self._copy(3, self.dg, self.me, part=0).wait_recv()
        self._copy(4, self.dg, self.me, part=1).wait_recv()
        self._copy(7, self.dg, self.sib).start()

    def wait_passed(self, j):
        self._copy(5 + j, self._other((self.xn, self.yn, self.dg)[j]), self.me).wait_recv()

    def wait_sends(self):
        for k, to in enumerate((self.sib, self.xn, self.yn)):
            self._copy(k, self.me, to).wait_send()
        self._copy(3, self.xn, self.yn, part=0).wait_send()
        self._copy(4, self.yn, self.xn, part=1).wait_send()
        for j, dev in enumerate((self.xn, self.yn, self.dg)):
            self._copy(5 + j, dev, self.sib).wait_send()


def _direct_sems(n):
    if n == 0:
        return []
    return [pltpu.SemaphoreType.DMA((n, 7)), pltpu.SemaphoreType.DMA((n, 7)), pltpu.SemaphoreType.DMA((n,))]


def _adam_math(w, g, m, v):
    m = ADAM_B1 * m + (1.0 - ADAM_B1) * g
    v = ADAM_B2 * v + (1.0 - ADAM_B2) * (g * g)
    m_hat = m / (1.0 - ADAM_B1 ** ADAM_STEP)
    v_hat = v / (1.0 - ADAM_B2 ** ADAM_STEP)
    delta = -ADAM_LR * (m_hat / (jnp.sqrt(v_hat) + ADAM_EPS) + ADAM_WD * w)
    return delta, m, v


def _sum_adam(parts, w, m, v, name):
    R, C = w.shape
    NP = parts.shape[0]
    BR = 4 * CHUNK if R % (4 * CHUNK) == 0 else R

    def body(p_ref, w_ref, m_ref, v_ref, g_ref, d_ref, nm_ref, nv_ref):
        g = p_ref[0].astype(F32)
        for i in range(1, NP):
            g = g + p_ref[i].astype(F32)
        g_ref[...] = g
        d_ref[...], nm_ref[...], nv_ref[...] = _adam_math(w_ref[...], g, m_ref[...], v_ref[...])

    blk = pl.BlockSpec((BR, C), lambda i: (i, 0))
    S = jax.ShapeDtypeStruct((R, C), F32)
    return pl.pallas_call(
        body, name=name, grid=(R // BR,),
        in_specs=[pl.BlockSpec((NP, BR, C), lambda i: (0, i, 0)), blk, blk, blk],
        out_specs=[blk] * 4, out_shape=(S,) * 4,
        compiler_params=_params(("arbitrary",)),
    )(parts, w, m, v)


SUBLANES = 8


def _nrows(size):
    return -(-size // (SUBLANES * LANES)) * SUBLANES


def _view2d(a):
    return a.reshape(-1, LANES) if a.size % LANES == 0 else a.reshape(1, -1)


def _pack_small(parts, total_rows, name):
    arrs = [p[0] for p in parts]

    def body(*refs):
        out = refs[-1]
        out[...] = jnp.zeros_like(out)
        at = 0
        for ref, (a, rows, flag) in zip(refs[:-1], parts):
            val = ref[...].T if flag == "T" else ref[...]
            r, c = (rows, val.shape[1]) if flag == "T" else val.shape
            out[at:at + r, 0:c] = val[:r]
            at += _nrows(r * c)

    return pl.pallas_call(body, name=name, out_shape=jax.ShapeDtypeStruct((total_rows, LANES), F32))(*arrs)


def _small_update(full, me, reps, shards, name):
    n = len(reps) + len(shards)

    def body(me_ref, full_ref, *refs):
        ins, outs = refs[:3 * n], refs[3 * n:]
        at = 0
        for k in range(n):
            w_ref, m_ref, v_ref = ins[3 * k:3 * k + 3]
            r, c = w_ref.shape
            if k < len(reps):
                g = full_ref[at:at + r, 0:c]
                at += _nrows(r * c)
            else:
                seg = full_ref[at:at + N_DEV * r, :]
                row = lax.broadcasted_iota(jnp.int32, seg.shape, 0)
                pick = [jnp.sum(jnp.where(row == r * me_ref[0] + t, seg, 0.0), axis=0, keepdims=True) for t in range(r)]
                g = pick[0] if r == 1 else jnp.concatenate(pick, axis=0)
                at += N_DEV * r
            g_ref, d_ref, nm_ref, nv_ref = outs[4 * k:4 * k + 4]
            g_ref[...] = g
            d_ref[...], nm_ref[...], nv_ref[...] = _adam_math(w_ref[...], g, m_ref[...], v_ref[...])
        outs[4 * n][...] = full_ref[at:at + 1, 0:1]

    flat = [t for p in reps + shards for t in p]
    S = jax.ShapeDtypeStruct
    res = pl.pallas_call(
        body, name=name,
        in_specs=[pl.BlockSpec(memory_space=pltpu.SMEM)] + [pl.BlockSpec(memory_space=pltpu.VMEM)] * (1 + len(flat)),
        out_shape=[S(p[0].shape, F32) for p in reps + shards for _ in range(4)] + [S((1, 1), F32)],
    )(me, full, *flat)
    return [tuple(res[4 * k:4 * k + 4]) for k in range(n)], res[4 * n]


def _rope_tables(T):
    pos = np.arange(T, dtype=np.float32)
    inv_freq = (np.float64(ROPE_THETA) ** (-np.arange(0, HEAD_DIM, 2, dtype=np.float64) / HEAD_DIM)).astype(np.float32)
    ang = (pos[:, None] * inv_freq[None, :]).astype(np.float64)
    cos, sin, zero = np.cos(ang).astype(np.float32), np.sin(ang).astype(np.float32), np.zeros(ang.shape, np.float32)
    c = np.concatenate([cos, cos, cos, cos], axis=1)
    s1 = np.concatenate([-sin, zero, -sin, zero], axis=1)
    s2 = np.concatenate([zero, sin, zero, sin], axis=1)
    return jnp.asarray(c), jnp.asarray(s1), jnp.asarray(s2)


def kernel(x, a_norm_g, a_w_in, a_ln_g, a_ln_b, a_ws, a_bs, a_w_out, kv_norm_g, w_kv, b_kv, b_norm_g, b_w_in, b_bq, b_sinks, b_w_out, final_norm_g, loss_target, m_a_norm_g, m_a_w_in, m_a_ln_g, m_a_ln_b, m_a_ws, m_a_bs, m_a_w_out, m_kv_norm_g, m_w_kv, m_b_kv, m_b_norm_g, m_b_w_in, m_b_bq, m_b_sinks, m_b_w_out, m_final_norm_g, v_a_norm_g, v_a_w_in, v_a_ln_g, v_a_ln_b, v_a_ws, v_a_bs, v_a_w_out, v_kv_norm_g, v_w_kv, v_b_kv, v_b_norm_g, v_b_w_in, v_b_bq, v_b_sinks, v_b_w_out, v_final_norm_g):
    T, D = x.shape[1], x.shape[2]
    AW = a_ln_g.shape[1] * N_DEV
    G = a_ws.shape[1]
    assert w_kv.shape[1] == 2 * LANES and a_ws.shape[2] == CHUNK and T % CHUNK == 0
    me = _my_index()

    xs, tgt = x[0], loss_target[0]
    z, wa_in_t, g_a, ln_g, ln_b, wa_out, wkv = _in_proj(xs, a_w_in[0], [a_norm_g, a_ln_g, a_ln_b], me.reshape(1),
                                                        [a_w_out[0], w_kv])
    wa_in_t = wa_in_t.reshape(-1, D)
    wa_out = wa_out.reshape(AW, D)
    wkv = wkv.reshape(D, 2 * LANES)

    rc, rs1, rs2 = _rope_tables(T)
    ws = a_ws[0]
    g_kv = kv_norm_g.reshape(1, D)
    bkv = b_kv.reshape(1, -1)
    g_f = final_norm_g.reshape(1, D)
    sinks = b_sinks.reshape(1, 16)
    h1, sv, vhat, rstd, k4, v4, kt, vt, wb_in, wb_out = _a_fwd(
        xs, z, ln_g, ln_b, ws, a_bs[0], wa_out, g_kv, wkv, bkv, rc, rs1, rs2, [b_w_in[0], b_w_out[0]])
    wb_out = wb_out.reshape(-1, D)
    q, g2, o, dh2, dh2_b, loss, d_gf = _b_fwd(h1, b_norm_g, wb_in, b_bq, rc, rs1, rs2, k4, vt, sinks, wb_out, g_f, tgt)
    dh1p, dz2, n2, y2, dk, dv, d_bq, d_gb, d_sink = _b_bwd(dh2, h1, q, g2, o, k4, v4, kt, sinks, wb_out, wb_in,
                                                           b_norm_g, rc, rs1, rs2)
    d_sink = d_sink[:, :4].reshape(2, 2, 4).transpose(0, 2, 1).reshape(1, 16)
    gw_b_in = _wgrad(n2, dz2, N_DEV, "wgrad_b_in", bt=1024)
    gw_b_out = _wgrad(y2, dh2_b, 1, "wgrad_b_out", bt=1024).reshape(N_DEV, -1, D)
    (dz, gw_a_out, gw_kv, dh1_f, d_gkv, d_bkv, d_lng, d_lnb, d_ws, d_bst, r_b_in, r_b_out) = _a_bwd(
        dh1p, dk, dv, h1, g_kv, wkv, wa_out, ws, ln_g, ln_b, z, sv, vhat, rstd, rc, rs1, rs2, [gw_b_in, gw_b_out])
    dx, n1, d_ga, r_a_out, r_kv = _a_in_bwd(dz, wa_in_t, xs, dh1_f, g_a, [gw_a_out, gw_kv])
    small = [(_view2d(d_ws), None, None), (d_bst, G, "T")] + [(_view2d(a), None, None) for a in (
        d_gkv, d_bkv, d_gb, d_bq, d_sink, d_gf, d_ga, d_lng, d_lnb, loss)]
    used = sum(_nrows(G * CHUNK if flag else a.size) for a, _, flag in small)
    per = -(-used // (SUBLANES * N_DEV)) * SUBLANES
    small_pack = _pack_small(small, per * N_DEV, "pack_small").reshape(N_DEV, per, LANES)
    r_a_in, full_small = _wgrad_exchange(n1, dz, me.reshape(1), small_pack, "wgrad_a_in")

    g_a_in, d_a_in, nm_a_in, nv_a_in = _sum_adam(r_a_in, a_w_in[0], m_a_w_in[0], v_a_w_in[0], "adam_a_in")
    g_a_out, d_a_out, nm_a_out, nv_a_out = _sum_adam(r_a_out, a_w_out[0], m_a_w_out[0], v_a_w_out[0], "adam_a_out")
    g_kvw, d_kvw, nm_kvw, nv_kvw = _sum_adam(r_kv, w_kv, m_w_kv, v_w_kv, "adam_kv")
    g_b_in, d_b_in, nm_b_in, nv_b_in = _sum_adam(r_b_in, b_w_in[0], m_b_w_in[0], v_b_w_in[0], "adam_b_in")
    g_b_out, d_b_out, nm_b_out, nv_b_out = _sum_adam(r_b_out, b_w_out[0], m_b_w_out[0], v_b_w_out[0], "adam_b_out")

    full_small = full_small.reshape(N_DEV * per, LANES)
    reps = [(a_ws, m_a_ws, v_a_ws), (a_bs, m_a_bs, v_a_bs), (kv_norm_g, m_kv_norm_g, v_kv_norm_g),
            (b_kv, m_b_kv, v_b_kv), (b_norm_g, m_b_norm_g, v_b_norm_g), (b_bq, m_b_bq, v_b_bq),
            (b_sinks, m_b_sinks, v_b_sinks), (final_norm_g, m_final_norm_g, v_final_norm_g)]
    shards = [(a_norm_g, m_a_norm_g, v_a_norm_g), (a_ln_g, m_a_ln_g, v_a_ln_g), (a_ln_b, m_a_ln_b, v_a_ln_b)]
    upd, loss = _small_update(full_small, me.reshape(1), [tuple(_view2d(t) for t in p) for p in reps],
                              [tuple(_view2d(t) for t in p) for p in shards], "adam_small")
    loss = loss[0, 0]
    sm_g, sd, snm, snv = ([upd[k][j].reshape(p[0].shape) for k, p in enumerate(reps + shards)] for j in range(4))

    def order(big, sm):
        a_in, a_out, kvw, b_in, b_out = big
        ws_, bs_, kvg, bkv_, bng, bq_, snk, fng, ang, alng, alnb = sm
        return (ang, a_in[None], alng, alnb, ws_, bs_, a_out[None], kvg, kvw, bkv_, bng, b_in[None], bq_, snk,
                b_out[None], fng)

    grads = order((g_a_in, g_a_out, g_kvw, g_b_in, g_b_out), sm_g)
    deltas = order((d_a_in, d_a_out, d_kvw, d_b_in, d_b_out), sd)
    new_m = order((nm_a_in, nm_a_out, nm_kvw, nm_b_in, nm_b_out), snm)
    new_v = order((nv_a_in, nv_a_out, nv_kvw, nv_b_in, nv_b_out), snv)
    return (loss, dx[None], *grads, *deltas, *new_m, *new_v)
```

```python
import functools

import jax
import jax.numpy as jnp
import numpy as np
from jax import lax
from jax.experimental import pallas as pl
from jax.experimental.pallas import tpu as pltpu

CHUNK = 128
HEAD_DIM = 64
ROPE_THETA = 10000.0
EPS = 1e-5
ADAM_LR = 0.001
ADAM_B1 = 0.9
ADAM_B2 = 0.999
ADAM_EPS = 1e-08
ADAM_WD = 0.01
ADAM_STEP = 10
N_DEV = 8
LANES = 128
NEG = -1e30

BF = jnp.bfloat16
F32 = jnp.float32
MESH = pl.DeviceIdType.MESH
AXES = ("x", "y", "c")
VMEM_LIMIT = 56 * 1024 * 1024


def _dot(a, b):
    return jnp.dot(a, b, preferred_element_type=F32)


def _dot_nt(a, b):
    return lax.dot_general(a, b, (((1,), (1,)), ((), ())), preferred_element_type=F32)


def _dot_tn(a, b):
    return lax.dot_general(a, b, (((0,), (0,)), ((), ())), preferred_element_type=F32)


def _const_spec(shape):
    nd = len(shape)
    return pl.BlockSpec(shape, lambda *_: (0,) * nd, pipeline_mode=pl.Buffered(1))


def _acc_spec(shape):
    nd = len(shape)
    return pl.BlockSpec(shape, lambda *_: (0,) * nd)


def _row_spec(tm, width):
    return pl.BlockSpec((tm, width), lambda i: (i, 0))


def _col_spec(tm, height):
    return pl.BlockSpec((height, tm), lambda i: (0, i))


def _params(sem):
    return pltpu.CompilerParams(dimension_semantics=sem, vmem_limit_bytes=VMEM_LIMIT)


def _rot(x, c, s1, s2):
    return x * c + pltpu.roll(x, 96, 1) * s1 + pltpu.roll(x, 32, 1) * s2


def _rot_bwd(d, c, s1, s2):
    return d * c + pltpu.roll(d * s1, 32, 1) + pltpu.roll(d * s2, 96, 1)


def _silu_parts(g):
    sg = jax.nn.sigmoid(g)
    return g * sg, sg * (1.0 + g * (1.0 - sg))


def _rms_bwd(dn, xh, r, g):
    a = dn * g
    return r * (a - xh * jnp.mean(a * xh, axis=-1, keepdims=True))


def _lane_lo(shape):
    return lax.broadcasted_iota(jnp.int32, shape, 1) < HEAD_DIM


def _split4(t):
    lo = _lane_lo(t.shape)
    tr = pltpu.roll(t, HEAD_DIM, 1)
    z = jnp.zeros_like(t)
    return jnp.concatenate([jnp.where(lo, t, z), jnp.where(lo, z, tr), jnp.where(lo, tr, z), jnp.where(lo, z, t)], axis=1)


def _stack_pairs(t, h):
    return jnp.concatenate([t[:, (h * 4 + j) * LANES:(h * 4 + j + 1) * LANES] for j in range(4)], axis=0)


def _upper():
    shape = (CHUNK, 4 * CHUNK)
    return lax.broadcasted_iota(jnp.int32, shape, 0) > (lax.broadcasted_iota(jnp.int32, shape, 1) & (CHUNK - 1))


def _band_rows(tile_ref, before_ref, c, h):
    a = slice(2 * h * LANES, (2 * h + 1) * LANES)
    b = slice((2 * h + 1) * LANES, (2 * h + 2) * LANES)
    cur = slice(c * CHUNK, (c + 1) * CHUNK)

    def prev(cols):
        return before_ref[:, cols] if c == 0 else tile_ref[(c - 1) * CHUNK:c * CHUNK, cols]

    return jnp.concatenate([prev(a), tile_ref[cur, a], prev(b), tile_ref[cur, b]], axis=0)


def _band_cols(tile_ref, before_ref, c, h):
    a = slice(2 * h * LANES, (2 * h + 1) * LANES)
    b = slice((2 * h + 1) * LANES, (2 * h + 2) * LANES)

    def prev(rows):
        return before_ref[0, rows, :] if c == 0 else tile_ref[c - 1, rows, :]

    return jnp.concatenate([prev(a), tile_ref[c, a, :], prev(b), tile_ref[c, b, :]], axis=1)


def _band_specs(tm):
    nc = tm // CHUNK

    def before(i):
        return jnp.maximum(i * nc - 1, 0)

    return (pl.BlockSpec((tm, 4 * LANES), lambda i: (i, 0)),
            pl.BlockSpec((CHUNK, 4 * LANES), lambda i: (before(i), 0)),
            pl.BlockSpec((nc, 4 * LANES, CHUNK), lambda i: (i, 0, 0)),
            pl.BlockSpec((1, 4 * LANES, CHUNK), lambda i: (before(i), 0, 0)))


def _fold(t, upper, has_prev=None):
    out = []
    for k in range(2):
        prev = t[2 * k * CHUNK:(2 * k + 1) * CHUNK]
        if has_prev is not None:
            prev = jnp.where(has_prev, prev, NEG)
        out.append(jnp.where(upper, prev, t[(2 * k + 1) * CHUNK:(2 * k + 2) * CHUNK]))
    return out


def _unfold(fa, fb, upper):
    z = jnp.zeros_like(fa)
    return jnp.concatenate([jnp.where(upper, fa, z), jnp.where(upper, z, fa),
                            jnp.where(upper, fb, z), jnp.where(upper, z, fb)], axis=0)


def _sink_tile(s_ref):
    shape = (4, 4 * LANES)
    row = lax.broadcasted_iota(jnp.int32, shape, 0)
    pair = lax.broadcasted_iota(jnp.int32, shape, 1) // LANES
    idx = (row // 2) * 8 + pair * 2 + row % 2
    tile = jnp.zeros(shape, F32)
    for n in range(16):
        tile = jnp.where(idx == n, s_ref[0, n], tile)
    return tile


def _softmax_sink(f, sink):
    m = jnp.maximum(jnp.max(f, axis=0, keepdims=True), sink)
    p = jnp.exp(f - m)
    es = jnp.exp(sink - m)
    inv = 1.0 / (jnp.sum(p, axis=0, keepdims=True) + es)
    return p * inv, es * inv


class _Riding:
    def __init__(self, shards, gathered, stages, sems, n_steps):
        self.shards, self.stages, self.n_steps = shards, stages, n_steps
        ssem, rsem, lsem = sems
        self.gathers = [_TwoLevel(stages[k], gathered[k], ssem.at[k], rsem.at[k], lsem.at[k])
                        for k in range(len(shards))]

    def begin(self, i):
        @pl.when(i == 0)
        def _():
            for shard, stage, g in zip(self.shards, self.stages, self.gathers):
                stage[...] = shard[...].astype(stage.dtype)
                g.start()

    def end(self, i):
        @pl.when(i == self.n_steps // 2)
        def _():
            for g in self.gathers:
                g.forward()

        @pl.when(i == self.n_steps - 1)
        def _():
            for g in self.gathers:
                g.finish()

    @staticmethod
    def specs(later):
        nl = len(later)
        hbm = pl.BlockSpec(memory_space=pl.ANY)
        return ([_const_spec(w.shape) for w in later], [hbm] * nl,
                tuple(jax.ShapeDtypeStruct((N_DEV,) + w.shape, BF) for w in later),
                [pltpu.VMEM(w.shape, BF) for w in later] + _direct_sems(nl))


PASS_MASKS = ((0, 1, 2, 5, 4, 3, 6, 7), (0, 1, 4, 3, 2, 5, 6, 7))


def _in_proj(x, w_shard, vec_shards, me, later):
    T, D = x.shape
    SH = w_shard.shape[1]
    TM = min(1024, T)
    nT = T // TM
    nl = len(later)
    nv = len(vec_shards)
    widths = [v.shape[1] for v in vec_shards]
    offsets = [sum(widths[:k]) for k in range(nv)]
    vec_shape = (SUBLANES, sum(widths))
    ds = widths[0]
    last = N_DEV - 1
    masks = jnp.asarray(np.array(PASS_MASKS, np.int32).reshape(-1))

    def slot(p, me_ref, masks_ref):
        return me_ref[0] ^ masks_ref[(me_ref[0] & 1) * N_DEV + p]

    def body(me_ref, masks_ref, x_ref, wsh_ref, *rest):
        vsh_refs, rest = rest[:nv], rest[nv:]
        shards, rest = rest[:nl], rest[nl:]
        (z_ref, wt_ref), rest = rest[:2], rest[2:]
        vout_refs, rest = rest[:nv], rest[nv:]
        gathered, rest = rest[:nl], rest[nl:]
        (w_scr, vec_scr, vstage, n1_scr, ga_scr, w_s, w_r, v_s, v_r, v_l), rest = rest[:10], rest[10:]
        stages, sems = rest[:nl], rest[nl:]
        p, i = pl.program_id(0), pl.program_id(1)
        me = _my_index()
        wg = _RelayGather(w_scr, w_s, w_r)
        vg = _Direct(vstage, vec_scr, v_s, v_r, v_l, scatter=False)
        lg = [_TwoLevel(stages[k], gathered[k], sems[0].at[k], sems[1].at[k], sems[2].at[k]) for k in range(nl)]

        def at_pass(k):
            return (p == k) & (i == 0)

        c = lax.axis_index("c")

        @pl.when(at_pass(0))
        def _():
            for ref, off, wd in zip(vsh_refs, offsets, widths):
                vstage[:, off:off + wd] = jnp.broadcast_to(ref[...], (SUBLANES, wd))
            vg.start()
            w_scr[me] = wsh_ref[...].astype(BF)
            wg.send_own(0).start()

            @pl.when(c == 1)
            def _():
                wg.send_own(1).start()

            @pl.when(c == 0)
            def _():
                wg.send_own(2).start()

            vg.finish()
            for j in range(N_DEV):
                ga_scr[:, j * ds:(j + 1) * ds] = vec_scr[j, 0:1, 0:ds]
                for ref, off, wd in zip(vout_refs, offsets, widths):
                    ref[:, j * wd:(j + 1) * wd] = vec_scr[j, 0:1, off:off + wd]

        @pl.when(at_pass(1))
        def _():
            wg.wait_sibling()

        for first, second, landed_first, landed_second in ((1, 2, wg.on_x, wg.on_y), (2, 1, wg.on_y, wg.on_x)):
            mine = c == (1 if first == 1 else 0)

            @pl.when(at_pass(2) & mine)
            def _(second=second, landed_first=landed_first):
                wg.send_own(second).start()
                landed_first()

            @pl.when(at_pass(3) & mine)
            def _(second=second):
                wg.wait_passed(second - 1)

            @pl.when(at_pass(4) & mine)
            def _(landed_second=landed_second):
                landed_second()

            @pl.when(at_pass(5) & mine)
            def _(first=first):
                wg.wait_passed(first - 1)

        @pl.when(at_pass(4))
        def _():
            for k in range(nl):
                stages[k][...] = shards[k][...].astype(BF)
                lg[k].start()

        @pl.when(at_pass(6))
        def _():
            wg.on_diag()

        @pl.when(at_pass(7))
        def _():
            wg.wait_passed(2)

        @pl.when(p == 0)
        def _():
            xv = x_ref[...]
            r1 = lax.rsqrt(jnp.mean(xv * xv, axis=-1, keepdims=True) + EPS)
            n1_scr[i] = (xv * r1 * ga_scr[...]).astype(BF)

        z_ref[...] = _dot(n1_scr[i], w_scr[slot(p, me_ref, masks_ref)]).astype(BF)

        @pl.when(i == 0)
        def _():
            wt_ref[0] = w_scr[slot(p, me_ref, masks_ref)].T

        @pl.when((p == last) & (i == nT - 1))
        def _():
            wg.wait_sends()
            for g in lg:
                g.forward()
            for g in lg:
                g.finish()

    hbm = pl.BlockSpec(memory_space=pl.ANY)
    dma = pltpu.SemaphoreType.DMA
    S = jax.ShapeDtypeStruct
    def whole(shape):
        return pl.BlockSpec(shape, lambda p, i, m, t: (0, 0))

    def once(shape):
        return pl.BlockSpec(shape, lambda p, i, m, t: (0, 0), pipeline_mode=pl.Buffered(1))

    grid_spec = pltpu.PrefetchScalarGridSpec(
        num_scalar_prefetch=2, grid=(N_DEV, nT),
        in_specs=[pl.BlockSpec((TM, D), lambda p, i, m, t: (jnp.where(p == 0, i, nT - 1), 0)), once(w_shard.shape)]
        + [once(v.shape) for v in vec_shards] + [once(w.shape) for w in later],
        out_specs=[pl.BlockSpec((TM, SH), lambda p, i, m, t: (i, slot(p, m, t))),
                   pl.BlockSpec((1, SH, D), lambda p, i, m, t: (slot(p, m, t), 0, 0))]
        + [whole((1, N_DEV * wd)) for wd in widths] + [hbm] * nl,
        scratch_shapes=[pltpu.VMEM((N_DEV, D, SH), BF), pltpu.VMEM((N_DEV,) + vec_shape, F32),
                        pltpu.VMEM(vec_shape, F32), pltpu.VMEM((nT, TM, D), BF), pltpu.VMEM((1, D), F32),
                        dma((8,)), dma((8,)), dma((7,)), dma((7,)), dma]
        + [pltpu.VMEM(w.shape, BF) for w in later] + _direct_sems(nl))
    return pl.pallas_call(
        body, name="a_in_proj", grid_spec=grid_spec,
        out_shape=(S((T, N_DEV * SH), BF), S((N_DEV, SH, D), BF)) + tuple(S((1, N_DEV * wd), F32) for wd in widths)
        + tuple(S((N_DEV,) + w.shape, BF) for w in later),
        compiler_params=_params(("arbitrary", "arbitrary")),
    )(me, masks, x, w_shard, *vec_shards, *later)


def _a_fwd(x, z, ln_g, ln_b, ws, bs, wa_out, g_kv, w_kv, b_kv, rc, rs1, rs2, later):
    T, D = x.shape
    AW = wa_out.shape[0]
    G = ws.shape[0]
    TM = min(512, T)
    nT = T // TM
    nC = TM // CHUNK
    nl = len(later)

    def body(x_ref, u_ref, v_ref, gt_ref, lng_ref, lnb_ref, ws_ref, bs_ref, waout_ref, gkv_ref, wkv_ref, bkv_ref,
             rc_ref, rs1_ref, rs2_ref, *rest):
        shards, rest = rest[:nl], rest[nl:]
        (h1_ref, sv_ref, vhat_ref, rstd_ref, k4_ref, v4_ref, kt_ref, vt_ref), rest = rest[:8], rest[8:]
        gathered, sv_scr, stages, sems = rest[:nl], rest[nl], rest[nl + 1:2 * nl + 1], rest[2 * nl + 1:]
        i = pl.program_id(0)
        riding = _Riding(shards, gathered, stages, sems, nT)
        riding.begin(i)
        xv = x_ref[...]
        u = u_ref[...].astype(F32)
        v = v_ref[...].astype(F32)
        gt = gt_ref[...].astype(F32)
        mu = jnp.mean(v, axis=-1, keepdims=True)
        xc = v - mu
        rstd = lax.rsqrt(jnp.mean(xc * xc, axis=-1, keepdims=True) + EPS)
        vhat = xc * rstd
        vln = (vhat * lng_ref[...] + lnb_ref[...]).astype(BF)
        tri = lax.broadcasted_iota(jnp.int32, (CHUNK, CHUNK), 0) >= lax.broadcasted_iota(jnp.int32, (CHUNK, CHUNK), 1)
        bst = jnp.concatenate([bs_ref[...], jnp.zeros((CHUNK - G, CHUNK), F32)], axis=0).T
        for g in range(G):
            wsm = jnp.where(tri, ws_ref[g], 0.0).astype(BF)
            bias = bst[:, g:g + 1]
            for c in range(nC):
                blk = vln[c * CHUNK:(c + 1) * CHUNK, g * CHUNK:(g + 1) * CHUNK]
                sv_scr[c * CHUNK:(c + 1) * CHUNK, g * CHUNK:(g + 1) * CHUNK] = _dot(wsm, blk) + bias
        sv = sv_scr[...]
        silu, _ = _silu_parts(gt)
        y = (u * sv * silu).astype(BF)
        h1 = xv + _dot(y, waout_ref[...])
        h1_ref[...] = h1
        sv_ref[...] = sv.astype(BF)
        vhat_ref[...] = vhat.astype(BF)
        rstd_ref[...] = jnp.broadcast_to(rstd, rstd_ref.shape)
        rkv = lax.rsqrt(jnp.mean(h1 * h1, axis=-1, keepdims=True) + EPS)
        nkv = (h1 * rkv * gkv_ref[...]).astype(BF)
        kv = _dot(nkv, wkv_ref[...]) + bkv_ref[...]
        k_rot = _rot(kv[:, :LANES], rc_ref[...], rs1_ref[...], rs2_ref[...])
        for src, ref, tref in ((k_rot, k4_ref, kt_ref), (kv[:, LANES:], v4_ref, vt_ref)):
            t4 = _split4(src)
            ref[...] = t4.astype(BF)
            for c in range(nC):
                for b in range(4):
                    blk = t4[c * CHUNK:(c + 1) * CHUNK, b * LANES:(b + 1) * LANES]
                    tref[c, b * LANES:(b + 1) * LANES, :] = blk.T.astype(BF)
        riding.end(i)

    row = functools.partial(_row_spec, TM)
    zcol = [pl.BlockSpec((TM, AW), functools.partial(lambda k, i: (i, k), k)) for k in range(3)]
    tr = pl.BlockSpec((nC, 4 * LANES, CHUNK), lambda i: (i, 0, 0))
    r_in, r_out, r_shape, r_scratch = _Riding.specs(later)
    S = jax.ShapeDtypeStruct
    return pl.pallas_call(
        body, name="a_fwd", grid=(nT,),
        in_specs=[row(D)] + zcol + [_const_spec((1, AW)), _const_spec((1, AW)),
                  _const_spec(ws.shape), _const_spec(bs.shape), _const_spec(wa_out.shape), _const_spec((1, D)),
                  _const_spec(w_kv.shape), _const_spec((1, 2 * LANES)), row(LANES), row(LANES), row(LANES)] + r_in,
        out_specs=[row(D), row(AW), row(AW), row(LANES), row(4 * LANES), row(4 * LANES), tr, tr] + r_out,
        out_shape=(S((T, D), F32), S((T, AW), BF), S((T, AW), BF), S((T, LANES), F32),
                   S((T, 4 * LANES), BF), S((T, 4 * LANES), BF),
                   S((T // CHUNK, 4 * LANES, CHUNK), BF), S((T // CHUNK, 4 * LANES, CHUNK), BF)) + r_shape,
        scratch_shapes=[pltpu.VMEM((TM, AW), F32)] + r_scratch,
        compiler_params=_params(("arbitrary",)),
    )(x, z, z, z, ln_g, ln_b, ws, bs, wa_out, g_kv, w_kv, b_kv, rc, rs1, rs2, *later)


def _b_fwd(h1, g_b, wb_in, bq, rc, rs1, rs2, k4, vt, sinks, wb_out, g_f, target):
    T, D = h1.shape
    BW = wb_out.shape[0]
    SH = wb_in.shape[2]
    TM = min(512, T)
    nC = TM // CHUNK
    nP = BW // LANES

    def body(h1_ref, gb_ref, wbin_ref, bq_ref, rc_ref, rs1_ref, rs2_ref, k4_ref, k4p_ref, vt_ref, vtp_ref, sinks_ref,
             wbout_ref, gf_ref, tgt_ref, q_ref, g2_ref, o_ref, dh2_ref, dh2b_ref, loss_ref, dgf_ref, z_scr, o_scr):
        i = pl.program_id(0)
        sink = _sink_tile(sinks_ref)

        @pl.when(i == 0)
        def _():
            loss_ref[...] = jnp.zeros_like(loss_ref)
            dgf_ref[...] = jnp.zeros_like(dgf_ref)

        h1v = h1_ref[...]
        r2 = lax.rsqrt(jnp.mean(h1v * h1v, axis=-1, keepdims=True) + EPS)
        n2 = (h1v * r2 * gb_ref[...]).astype(BF)
        for j in range(N_DEV):
            z_scr[:, j * SH:(j + 1) * SH] = _dot(n2, wbin_ref[j])
        c_t, s1_t, s2_t = rc_ref[...], rs1_ref[...], rs2_ref[...]
        for p in range(nP):
            cols = slice(p * LANES, (p + 1) * LANES)
            qp = _rot(z_scr[:, cols] + bq_ref[:, cols], c_t, s1_t, s2_t) * (HEAD_DIM ** -0.5)
            q_ref[:, cols] = qp.astype(BF)
        g2 = z_scr[:, BW:]
        g2_ref[...] = g2.astype(BF)
        upper = _upper()
        for c in range(nC):
            ci = i * nC + c
            rows = slice(c * CHUNK, (c + 1) * CHUNK)
            qc = q_ref[rows, :]
            for h in range(2):
                st = _dot_nt(_band_rows(k4_ref, k4p_ref, c, h), _stack_pairs(qc, h))
                fa, fb = _fold(st, upper, ci > 0)
                pa, _ = _softmax_sink(fa, sink[2 * h:2 * h + 1, :])
                pb, _ = _softmax_sink(fb, sink[2 * h + 1:2 * h + 2, :])
                ot = _dot(_band_cols(vt_ref, vtp_ref, c, h), _unfold(pa, pb, upper).astype(BF))
                for j in range(4):
                    o_scr[rows, (h * 4 + j) * LANES:(h * 4 + j + 1) * LANES] = ot[:, j * CHUNK:(j + 1) * CHUNK].T
        o = o_scr[...]
        o_ref[...] = o.astype(BF)
        silu, _ = _silu_parts(g2)
        h2 = h1v + _dot((o * silu).astype(BF), wbout_ref[...])
        rf = lax.rsqrt(jnp.mean(h2 * h2, axis=-1, keepdims=True) + EPS)
        xh = h2 * rf
        gf = gf_ref[...]
        err = xh * gf - tgt_ref[...]
        dyf = err * (1.0 / D)
        dh2 = _rms_bwd(dyf, xh, rf, gf)
        dh2_ref[...] = dh2
        dh2b_ref[...] = dh2.astype(BF)
        loss_ref[...] += 0.5 * jnp.sum(jnp.mean(err * err, axis=-1, keepdims=True), axis=0, keepdims=True)
        dgf_ref[...] += jnp.sum(dyf * xh, axis=0, keepdims=True)

    row = functools.partial(_row_spec, TM)
    rows_tile, rows_before, cols_tile, cols_before = _band_specs(TM)
    S = jax.ShapeDtypeStruct
    return pl.pallas_call(
        body, name="b_fwd", grid=(T // TM,),
        in_specs=[row(D), _const_spec((1, D)), _const_spec(wb_in.shape), _const_spec((1, BW)), row(LANES), row(LANES),
                  row(LANES), rows_tile, rows_before, cols_tile, cols_before, pl.BlockSpec(memory_space=pltpu.SMEM),
                  _const_spec(wb_out.shape), _const_spec((1, D)), row(D)],
        out_specs=[row(BW), row(BW), row(BW), row(D), row(D), _acc_spec((1, 1)), _acc_spec((1, D))],
        out_shape=(S((T, BW), BF), S((T, BW), BF), S((T, BW), BF), S((T, D), F32), S((T, D), BF), S((1, 1), F32),
                   S((1, D), F32)),
        scratch_shapes=[pltpu.VMEM((TM, 2 * BW), F32), pltpu.VMEM((TM, BW), F32)],
        compiler_params=_params(("arbitrary",)),
    )(h1, g_b, wb_in, bq, rc, rs1, rs2, k4, k4, vt, vt, sinks, wb_out, g_f, target)


def _b_bwd(dh2, h1, q, g2, o, k4, v4, kt, sinks, wb_out, wb_in, g_b, rc, rs1, rs2):
    T, D = h1.shape
    BW = wb_out.shape[0]
    SH = wb_in.shape[2]
    TM = min(512, T)
    nT = T // TM
    nC = TM // CHUNK
    nP = BW // LANES

    def body(dh2_ref, h1_ref, q_ref, g2_ref, o_ref, k4_ref, k4p_ref, v4_ref, v4p_ref, kt_ref, ktp_ref, sinks_ref,
             wbout_ref, wbin_ref, gb_ref, rc_ref, rs1_ref, rs2_ref,
             dh1_ref, dz2_ref, n2_ref, y2_ref, dk_ref, dv_ref, dbq_ref, dgb_ref, dsink_ref, do_scr, dq_scr, dsacc_scr):
        i = pl.program_id(0)
        sink = _sink_tile(sinks_ref)

        @pl.when(i == 0)
        def _():
            dk_ref[...] = jnp.zeros_like(dk_ref)
            dv_ref[...] = jnp.zeros_like(dv_ref)
            dbq_ref[...] = jnp.zeros_like(dbq_ref)
            dgb_ref[...] = jnp.zeros_like(dgb_ref)
            dsacc_scr[...] = jnp.zeros_like(dsacc_scr)

        dh2 = dh2_ref[...]
        dy2 = _dot_nt(dh2.astype(BF), wbout_ref[...])
        silu, dsilu = _silu_parts(g2_ref[...].astype(F32))
        do_scr[...] = (dy2 * silu).astype(BF)
        dy2, silu, dsilu = dy2.astype(BF), silu.astype(BF), dsilu.astype(BF)
        ob = o_ref[...]
        y2_ref[...] = (ob * silu).T
        dz2_ref[:, BW:] = dy2 * ob * dsilu
        upper = _upper()
        lo = _lane_lo((2 * CHUNK, LANES))
        for c in range(nC):
            ci = i * nC + c
            rows = slice(c * CHUNK, (c + 1) * CHUNK)
            pci = jnp.maximum(ci - 1, 0)
            prev = pl.multiple_of(pci * CHUNK, CHUNK)
            cur = pl.multiple_of(ci * CHUNK, CHUNK)
            qc = q_ref[rows, :]
            doc = do_scr[rows, :]
            dkb = jnp.zeros((2 * CHUNK, LANES), F32)
            dvb = jnp.zeros((2 * CHUNK, LANES), F32)
            for h in range(2):
                qs = _stack_pairs(qc, h)
                dos = _stack_pairs(doc, h)
                fa, fb = _fold(_dot_nt(_band_rows(k4_ref, k4p_ref, c, h), qs), upper, ci > 0)
                dfa, dfb = _fold(_dot_nt(_band_rows(v4_ref, v4p_ref, c, h), dos), upper)
                folded = []
                for k, (f, df) in enumerate(((fa, dfa), (fb, dfb))):
                    p, ps = _softmax_sink(f, sink[2 * h + k:2 * h + k + 1, :])
                    delta = jnp.sum(p * df, axis=0, keepdims=True)
                    dsacc_scr[2 * h + k:2 * h + k + 1, :] -= ps * delta
                    folded.append((p, p * (df - delta)))
                pt = _unfold(folded[0][0], folded[1][0], upper).astype(BF)
                dst = _unfold(folded[0][1], folded[1][1], upper).astype(BF)
                dqt = _dot(_band_cols(kt_ref, ktp_ref, c, h), dst)
                for j in range(4):
                    dq_scr[rows, (h * 4 + j) * LANES:(h * 4 + j + 1) * LANES] = dqt[:, j * CHUNK:(j + 1) * CHUNK].T
                for acc_name, g in (("k", _dot(dst, qs)), ("v", _dot(pt, dos))):
                    a, b = g[:2 * CHUNK], g[2 * CHUNK:]
                    if h == 0:
                        part = jnp.where(lo, a + pltpu.roll(b, HEAD_DIM, 1), 0.0)
                    else:
                        part = jnp.where(lo, 0.0, pltpu.roll(a, HEAD_DIM, 1) + b)
                    if acc_name == "k":
                        dkb += part
                    else:
                        dvb += part
            dk_ref[pl.ds(prev, CHUNK), :] += dkb[:CHUNK]
            dk_ref[pl.ds(cur, CHUNK), :] += dkb[CHUNK:]
            dv_ref[pl.ds(prev, CHUNK), :] += dvb[:CHUNK]
            dv_ref[pl.ds(cur, CHUNK), :] += dvb[CHUNK:]
        c_t, s1_t, s2_t = rc_ref[...], rs1_ref[...], rs2_ref[...]
        for p in range(nP):
            cols = slice(p * LANES, (p + 1) * LANES)
            dqp = _rot_bwd(dq_scr[:, cols] * (HEAD_DIM ** -0.5), c_t, s1_t, s2_t)
            dbq_ref[:, cols] += jnp.sum(dqp, axis=0, keepdims=True)
            dz2_ref[:, cols] = dqp.astype(BF)
        h1v = h1_ref[...]
        r2 = lax.rsqrt(jnp.mean(h1v * h1v, axis=-1, keepdims=True) + EPS)
        xh = h1v * r2
        gb = gb_ref[...]
        n2_ref[...] = (xh * gb).astype(BF).T
        dn2 = None
        for j in range(N_DEV):
            part = _dot_nt(dz2_ref[:, j * SH:(j + 1) * SH], wbin_ref[j])
            dn2 = part if dn2 is None else dn2 + part
        dgb_ref[...] += jnp.sum(dn2 * xh, axis=0, keepdims=True)
        dh1_ref[...] = dh2 + _rms_bwd(dn2, xh, r2, gb)

        @pl.when(i == nT - 1)
        def _():
            lane = lax.broadcasted_iota(jnp.int32, dsink_ref.shape, 1)
            tot = jnp.zeros(dsink_ref.shape, F32)
            for j in range(4):
                tot += jnp.where(lane == j, jnp.sum(dsacc_scr[:, j * CHUNK:(j + 1) * CHUNK], axis=1, keepdims=True), 0.0)
            dsink_ref[...] = tot

    row = functools.partial(_row_spec, TM)
    rows_tile, rows_before, cols_tile, cols_before = _band_specs(TM)
    S = jax.ShapeDtypeStruct
    return pl.pallas_call(
        body, name="b_bwd", grid=(T // TM,),
        in_specs=[row(D), row(D), row(BW), row(BW), row(BW), rows_tile, rows_before, rows_tile, rows_before,
                  cols_tile, cols_before, pl.BlockSpec(memory_space=pltpu.SMEM), _const_spec(wb_out.shape), _const_spec(wb_in.shape),
                  _const_spec((1, D)), row(LANES), row(LANES), row(LANES)],
        out_specs=[row(D), row(2 * BW), _col_spec(TM, D), _col_spec(TM, BW), _acc_spec((T, LANES)),
                   _acc_spec((T, LANES)), _acc_spec((1, BW)), _acc_spec((1, D)), _acc_spec((4, LANES))],
        out_shape=(S((T, D), F32), S((T, 2 * BW), BF), S((D, T), BF), S((BW, T), BF), S((T, LANES), F32),
                   S((T, LANES), F32), S((1, BW), F32), S((1, D), F32), S((4, LANES), F32)),
        scratch_shapes=[pltpu.VMEM((TM, BW), BF), pltpu.VMEM((TM, BW), F32), pltpu.VMEM((4, 4 * CHUNK), F32)],
        compiler_params=_params(("arbitrary",)),
    )(dh2, h1, q, g2, o, k4, k4, v4, v4, kt, kt, sinks, wb_out, wb_in, g_b, rc, rs1, rs2)


def _a_bwd(dh1p, dk, dv, h1, g_kv, w_kv, wa_out, ws, ln_g, ln_b, z, sv, vhat, rstd, rc, rs1, rs2, ready):
    T, D = h1.shape
    AW = wa_out.shape[0]
    G = ws.shape[0]
    TM = min(256, T)
    nT = T // TM
    nC = TM // CHUNK
    nr = len(ready)

    def body(dh1p_ref, dk_ref, dv_ref, h1_ref, gkv_ref, wkv_ref, waout_ref, ws_ref, lng_ref,
             lnb_ref, u_ref, gt_ref, sv_ref, vhat_ref, rstd_ref, rc_ref, rs1_ref, rs2_ref, *rest):
        ready_refs, rest = rest[:nr], rest[nr:]
        (dz_ref, gwo_ref, gwk_ref, dh1f_ref, dgkv_ref, dbkv_ref, dlng_ref, dlnb_ref,
         dws_ref, dbs_ref), rest = rest[:10], rest[10:]
        recv_refs, (dsv_scr, dvln_scr, acco_scr, acck_scr, ssem, rsem, lsem) = rest[:nr], rest[nr:]
        i = pl.program_id(0)
        exchanges = [_Direct(ready_refs[k], recv_refs[k], ssem.at[k], rsem.at[k], lsem.at[k], scatter=True)
                     for k in range(nr)]

        @pl.when(i == 0)
        def _():
            for e in exchanges:
                e.start()
            for r in (dgkv_ref, dbkv_ref, dlng_ref, dlnb_ref, dws_ref, dbs_ref, acco_scr, acck_scr):
                r[...] = jnp.zeros_like(r)

        dk_pre = _rot_bwd(dk_ref[...], rc_ref[...], rs1_ref[...], rs2_ref[...])
        dkv = jnp.concatenate([dk_pre, dv_ref[...]], axis=1)
        dbkv_ref[...] += jnp.sum(dkv, axis=0, keepdims=True)
        dkv_b = dkv.astype(BF)
        h1v = h1_ref[...]
        rkv = lax.rsqrt(jnp.mean(h1v * h1v, axis=-1, keepdims=True) + EPS)
        xh_kv = h1v * rkv
        gkv = gkv_ref[...]
        acck_scr[...] += _dot((xh_kv * gkv).astype(BF).T, dkv_b)
        dnkv = _dot_nt(dkv_b, wkv_ref[...])
        dgkv_ref[...] += jnp.sum(dnkv * xh_kv, axis=0, keepdims=True)
        dh1 = dh1p_ref[...] + _rms_bwd(dnkv, xh_kv, rkv, gkv)
        dh1_b = dh1.astype(BF)
        dh1f_ref[...] = dh1
        dy = _dot_nt(dh1_b, waout_ref[...]).astype(BF)
        silu, dsilu = _silu_parts(gt_ref[...].astype(F32))
        silu, dsilu = silu.astype(BF), dsilu.astype(BF)
        ub, svb = u_ref[...], sv_ref[...]
        us = ub * silu
        dys = dy * svb
        acco_scr[...] += _dot((us * svb).T, dh1_b)
        dz_ref[:, :AW] = dys * silu
        dz_ref[:, 2 * AW:] = dys * ub * dsilu
        dsv_scr[...] = dy * us
        vhat_v = vhat_ref[...].astype(F32)
        lng = lng_ref[...]
        vln_b = (vhat_v * lng + lnb_ref[...]).astype(BF)
        tri = lax.broadcasted_iota(jnp.int32, (CHUNK, CHUNK), 0) >= lax.broadcasted_iota(jnp.int32, (CHUNK, CHUNK), 1)
        lane = lax.broadcasted_iota(jnp.int32, (CHUNK, LANES), 1)
        dbs = jnp.zeros((CHUNK, LANES), F32)
        for g in range(G):
            wsm = jnp.where(tri, ws_ref[g], 0.0).astype(BF)
            cols = slice(g * CHUNK, (g + 1) * CHUNK)
            dws_g = None
            for c in range(nC):
                rows = slice(c * CHUNK, (c + 1) * CHUNK)
                dsv_cg = dsv_scr[rows, cols]
                dvln_scr[rows, cols] = _dot_tn(wsm, dsv_cg)
                part = _dot_nt(dsv_cg, vln_b[rows, cols])
                dws_g = part if dws_g is None else dws_g + part
                dbs += jnp.where(lane == g, jnp.sum(dsv_cg.astype(F32), axis=-1, keepdims=True), 0.0)
            dws_ref[g] += jnp.where(tri, dws_g, 0.0)
        dbs_ref[...] += dbs
        dvln = dvln_scr[...]
        dlng_ref[...] += jnp.sum(dvln * vhat_v, axis=0, keepdims=True)
        dlnb_ref[...] += jnp.sum(dvln, axis=0, keepdims=True)
        a = dvln * lng
        dvv = rstd_ref[:, 0:1] * (a - jnp.mean(a, axis=-1, keepdims=True)
                                  - vhat_v * jnp.mean(a * vhat_v, axis=-1, keepdims=True))
        dz_ref[:, AW:2 * AW] = dvv.astype(BF)

        @pl.when(i == nT - 1)
        def _():
            for j in range(N_DEV):
                gwo_ref[j] = acco_scr[j * (AW // N_DEV):(j + 1) * (AW // N_DEV)].astype(BF)
                gwk_ref[j] = acck_scr[j * (D // N_DEV):(j + 1) * (D // N_DEV)].astype(BF)
            for e in exchanges:
                e.finish()

    row = functools.partial(_row_spec, TM)
    hbm = pl.BlockSpec(memory_space=pl.ANY)
    S = jax.ShapeDtypeStruct
    gwo_shape, gwk_shape = (N_DEV, AW // N_DEV, D), (N_DEV, D // N_DEV, 2 * LANES)
    return pl.pallas_call(
        body, name="a_bwd", grid=(nT,),
        in_specs=[row(D), row(LANES), row(LANES), row(D), _const_spec((1, D)), _const_spec(w_kv.shape),
                  _const_spec(wa_out.shape), _const_spec(ws.shape),
                  _const_spec((1, AW)), _const_spec((1, AW)), pl.BlockSpec((TM, AW), lambda i: (i, 0)),
                  pl.BlockSpec((TM, AW), lambda i: (i, 2)), row(AW), row(AW), row(LANES),
                  row(LANES), row(LANES), row(LANES)] + [hbm] * nr,
        out_specs=[row(3 * AW), _const_spec(gwo_shape), _const_spec(gwk_shape), row(D),
                   _acc_spec((1, D)), _acc_spec((1, 2 * LANES)), _acc_spec((1, AW)),
                   _acc_spec((1, AW)), _acc_spec(ws.shape), _acc_spec((CHUNK, LANES))] + [hbm] * nr,
        out_shape=(S((T, 3 * AW), BF), S(gwo_shape, BF), S(gwk_shape, BF), S((T, D), F32),
                   S((1, D), F32), S((1, 2 * LANES), F32), S((1, AW), F32), S((1, AW), F32),
                   S(ws.shape, F32), S((CHUNK, LANES), F32)) + tuple(S(r.shape, r.dtype) for r in ready),
        scratch_shapes=[pltpu.VMEM((TM, AW), BF), pltpu.VMEM((TM, AW), F32), pltpu.VMEM((AW, D), F32),
                        pltpu.VMEM((D, 2 * LANES), F32)] + _direct_sems(nr),
        compiler_params=_params(("arbitrary",)),
    )(dh1p, dk, dv, h1, g_kv, w_kv, wa_out, ws, ln_g, ln_b, z, z, sv, vhat, rstd, rc, rs1, rs2, *ready)


def _a_in_bwd(dz, wa_in_t, x, dh1, g_a, ready):
    T, D = x.shape
    TM = min(512, T)
    nT = T // TM
    nr = len(ready)

    def body(dz_ref, wain_ref, x_ref, dh1_ref, ga_ref, *rest):
        ready_refs, (dx_ref, n1_ref, dga_ref), rest = rest[:nr], rest[nr:nr + 3], rest[nr + 3:]
        recv_refs, (ssem, rsem, lsem) = rest[:nr], rest[nr:]
        i = pl.program_id(0)
        exchanges = [_Direct(ready_refs[k], recv_refs[k], ssem.at[k], rsem.at[k], lsem.at[k], scatter=True)
                     for k in range(nr)]

        @pl.when(i == 0)
        def _():
            for e in exchanges:
                e.start()
            dga_ref[...] = jnp.zeros_like(dga_ref)

        xv = x_ref[...]
        r1 = lax.rsqrt(jnp.mean(xv * xv, axis=-1, keepdims=True) + EPS)
        xh = xv * r1
        ga = ga_ref[...]
        n1_ref[...] = (xh * ga).astype(BF).T
        dn1 = _dot(dz_ref[...], wain_ref[...])
        dga_ref[...] += jnp.sum(dn1 * xh, axis=0, keepdims=True)
        dx_ref[...] = dh1_ref[...] + _rms_bwd(dn1, xh, r1, ga)

        @pl.when(i == nT - 1)
        def _():
            for e in exchanges:
                e.finish()

    row = functools.partial(_row_spec, TM)
    hbm = pl.BlockSpec(memory_space=pl.ANY)
    S = jax.ShapeDtypeStruct
    return pl.pallas_call(
        body, name="a_in_bwd", grid=(nT,),
        in_specs=[row(dz.shape[1]), _const_spec(wa_in_t.shape), row(D), row(D), _const_spec((1, D))] + [hbm] * nr,
        out_specs=[row(D), _col_spec(TM, D), _acc_spec((1, D))] + [hbm] * nr,
        out_shape=(S((T, D), F32), S((D, T), BF), S((1, D), F32)) + tuple(S(r.shape, r.dtype) for r in ready),
        scratch_shapes=_direct_sems(nr),
        compiler_params=_params(("arbitrary",)),
    )(dz, wa_in_t, x, dh1, g_a, *ready)


def _wgrad(problems, name, bt=512):
    T = problems[0][0].shape[1]
    BT = min(bt, T)
    nt = T // BT
    n = len(problems)
    dims = [(at.shape[0], b.shape[1] // nblk, nblk) for at, b, nblk in problems]

    def body(*refs):
        ins, outs, accs = refs[:2 * n], refs[2 * n:3 * n], refs[3 * n:]
        t = pl.program_id(0)

        @pl.when(t == 0)
        def _():
            for acc in accs:
                acc[...] = jnp.zeros_like(acc)

        for k in range(n):
            accs[k][...] += _dot(ins[2 * k][...], ins[2 * k + 1][...])

        @pl.when(t == nt - 1)
        def _():
            for k, (_, N, nblk) in enumerate(dims):
                for j in range(nblk):
                    outs[k][j] = accs[k][:, j * N:(j + 1) * N].astype(BF)

    in_specs = []
    for at, b, _ in problems:
        in_specs += [pl.BlockSpec((at.shape[0], BT), lambda t: (0, t)), pl.BlockSpec((BT, b.shape[1]), lambda t: (t, 0))]
    return pl.pallas_call(
        body, name=name, grid=(nt,), in_specs=in_specs,
        out_specs=[pl.BlockSpec((nblk, K, N), lambda t: (0, 0, 0)) for K, N, nblk in dims],
        out_shape=[jax.ShapeDtypeStruct((nblk, K, N), BF) for K, N, nblk in dims],
        scratch_shapes=[pltpu.VMEM((K, nblk * N), F32) for K, N, nblk in dims],
        compiler_params=_params(("arbitrary",)),
    )(*[operand for at, b, _ in problems for operand in (at, b)])


def _wgrad_exchange(a, b, me, small, name):
    K, T = a.shape
    N = b.shape[1] // N_DEV
    BT = T
    nt = T // BT
    last = N_DEV - 1
    n_chip = N_DEV // 2

    def far_of(k, core):
        return jnp.where((core == 0) & ((k == 1) | (k == 2)), k, n_chip - 1 - k)

    def block_of(s, me_i):
        k, odd = s // 2, s % 2
        core = me_i & 1
        return me_i ^ ((far_of(k, jnp.where(odd == 1, core, 1 - core)) << 1) | (1 - odd))

    H = K // 2

    def body(me_ref, a_ref, b_ref, small_ref, recv_ref, full_ref, *scratch):
        (acc, dstage, istage, half, relay, d_s, d_r, i_s, i_r, r_s, r_r, lsem, parts_scr, red_scr, e_s, e_r, e_l, g_s,
         g_r, g_l) = scratch
        s, t = pl.program_id(0), pl.program_id(1)
        x, y, c = (lax.axis_index(ax) for ax in AXES)
        ex = [_Direct(small_ref, parts_scr, e_s, e_r, e_l, scatter=True)]
        regather = _TwoLevel(red_scr, full_ref, g_s, g_r, g_l)

        def to_sibling(k, slot):
            return pltpu.make_async_remote_copy(src_ref=dstage.at[slot], dst_ref=half.at[k], send_sem=d_s.at[k],
                                                recv_sem=d_r.at[k], device_id=(x, y, 1 - c), device_id_type=MESH)

        def to_chip(k, slot):
            over_x = far_of(k, c) == 2
            px, py = jnp.where(over_x, 1 - x, x), jnp.where(over_x, y, 1 - y)
            return pltpu.make_async_remote_copy(src_ref=istage.at[slot], dst_ref=recv_ref.at[jnp.where(over_x, 1, 2)],
                                                send_sem=i_s.at[k], recv_sem=i_r.at[k], device_id=(px, py, c),
                                                device_id_type=MESH)

        def to_relay(j, slot):
            to = (1 - x, y, c) if j == 0 else (x, 1 - y, c)
            return pltpu.make_async_remote_copy(src_ref=istage.at[slot, pl.ds(j * H, H)], dst_ref=relay.at[j],
                                                send_sem=r_s.at[j], recv_sem=r_r.at[j], device_id=to,
                                                device_id_type=MESH)

        @pl.when((s == 0) & (t == 0))
        def _():
            for e in ex:
                e.start()

        acc[...] = _dot(a_ref[...], b_ref[...])

        @pl.when(t == nt - 1)
        def _():
            k = lax.div(s, 2)
            slot = lax.rem(k, 2)

            @pl.when(lax.rem(s, 2) == 0)
            def _():
                @pl.when(k >= 2)
                def _():
                    to_sibling(k - 2, slot).wait_send()

                dstage[slot] = acc[...].astype(BF)
                to_sibling(k, slot).start()

            @pl.when(lax.rem(s, 2) == 1)
            def _():
                to_sibling(k, slot).wait_recv()
                pair = acc[...] + half[k].astype(F32)

                @pl.when(k == 0)
                def _():
                    istage[slot] = pair.astype(BF)
                    for j in range(2):
                        to_relay(j, slot).start()

                @pl.when(k == 1)
                def _():
                    for j in range(2):
                        to_relay(j, slot).wait_recv()

                @pl.when(k == 2)
                def _():
                    for j in range(2):
                        to_relay(j, slot).wait_send()

                @pl.when(k == n_chip - 1)
                def _():
                    to_chip(1, slot).wait_send()
                    istage[slot] = pair.astype(BF)

                @pl.when((k == 1) | (k == 2))
                def _():
                    over_x = far_of(k, c) == 2
                    istage[slot, 0:H] = (pair[:H] + jnp.where(over_x, 0.0, relay[0].astype(F32))).astype(BF)
                    istage[slot, H:K] = (pair[H:] + jnp.where(over_x, relay[1].astype(F32), 0.0)).astype(BF)
                    to_chip(k, slot).start()

            @pl.when(s == last)
            def _():
                own = pltpu.make_async_copy(istage.at[slot], recv_ref.at[0], lsem)
                own.start()
                to_chip(2, 0).wait_send()
                to_sibling(n_chip - 2, 0).wait_send()
                to_sibling(n_chip - 1, 1).wait_send()
                for kk in (1, 2):
                    to_chip(kk, 0).wait_recv()
                own.wait()
                for e in ex:
                    e.finish()
                total = parts_scr[0]
                for dev in range(1, N_DEV):
                    total = total + parts_scr[dev]
                red_scr[...] = total
                regather.start()
                regather.forward()
                regather.finish()

    hbm = pl.BlockSpec(memory_space=pl.ANY)
    dma = pltpu.SemaphoreType.DMA
    grid_spec = pltpu.PrefetchScalarGridSpec(
        num_scalar_prefetch=1, grid=(N_DEV, nt),
        in_specs=[pl.BlockSpec((K, BT), lambda s, t, me_ref: (0, t), pipeline_mode=pl.Buffered(1)),
                  pl.BlockSpec((BT, N), lambda s, t, me_ref: (t, block_of(s, me_ref[0]))), hbm],
        out_specs=[hbm, hbm],
        scratch_shapes=[pltpu.VMEM((K, N), F32), pltpu.VMEM((2, K, N), BF), pltpu.VMEM((2, K, N), BF),
                        pltpu.VMEM((n_chip, K, N), BF), pltpu.VMEM((2, H, N), BF), dma((n_chip,)), dma((n_chip,)),
                        dma((n_chip - 1,)), dma((n_chip - 1,)), dma((2,)), dma((2,)), dma,
                        pltpu.VMEM(small.shape, F32), pltpu.VMEM(small.shape[1:], F32),
                        dma((last,)), dma((last,)), dma, dma((last,)), dma((last,)), dma])
    return pl.pallas_call(
        body, name=name, grid_spec=grid_spec,
        out_shape=[jax.ShapeDtypeStruct((n_chip - 1, K, N), BF), jax.ShapeDtypeStruct(small.shape, F32)],
        compiler_params=_params(("arbitrary", "arbitrary")),
    )(me, a, b, small)


def _my_index():
    return 4 * lax.axis_index("x") + 2 * lax.axis_index("y") + lax.axis_index("c")


def _peer(mask):
    x, y, c = (lax.axis_index(a) for a in AXES)
    return (x ^ ((mask >> 2) & 1), y ^ ((mask >> 1) & 1), c ^ (mask & 1))


def _dev_index(p):
    return 4 * p[0] + 2 * p[1] + p[2]


class _Direct:
    def __init__(self, src, dst, send_sems, recv_sems, local_sem, scatter):
        me = _my_index()
        self.own = pltpu.make_async_copy(src.at[me] if scatter else src, dst.at[me], local_sem)
        self.sends, self.recvs = [], []
        for k in range(1, N_DEV):
            p = _peer(k)
            pi = _dev_index(p)
            sems = dict(send_sem=send_sems.at[k - 1], recv_sem=recv_sems.at[k - 1], device_id=p, device_id_type=MESH)
            self.sends.append(pltpu.make_async_remote_copy(src_ref=src.at[pi] if scatter else src, dst_ref=dst.at[me],
                                                           **sems))
            self.recvs.append(pltpu.make_async_remote_copy(src_ref=src.at[me] if scatter else src, dst_ref=dst.at[pi],
                                                           **sems))

    def start(self):
        self.own.start()
        for cp in self.sends:
            cp.start()

    def finish(self):
        for cp in self.sends:
            cp.wait_send()
        for cp in self.recvs:
            cp.wait_recv()
        self.own.wait()


class _TwoLevel:
    def __init__(self, src, dst, send_sems, recv_sems, local_sem, own=True):
        x, y, c = (lax.axis_index(a) for a in AXES)
        self.me, self.sibling = (x, y, c), (x, y, 1 - c)
        self.chips = [(1 - x, y), (x, 1 - y), (1 - x, 1 - y)]
        self.src, self.dst, self.send_sems, self.recv_sems = src, dst, send_sems, recv_sems
        self.own = pltpu.make_async_copy(src, dst.at[_dev_index(self.me)], local_sem) if own else None

    def _copy(self, k, block, to, from_src=False):
        slot = self.dst.at[_dev_index(block)]
        return pltpu.make_async_remote_copy(src_ref=self.src if from_src else slot, dst_ref=slot,
                                            send_sem=self.send_sems.at[k], recv_sem=self.recv_sems.at[k],
                                            device_id=to, device_id_type=MESH)

    def _firsts(self):
        c = self.me[2]
        return [self._copy(0, self.me, self.sibling, True)] + [self._copy(1 + j, self.me, (*chip, c), True)
                                                               for j, chip in enumerate(self.chips)]

    def _passed(self):
        c = self.me[2]
        return [self._copy(4 + j, (*chip, c), self.sibling) for j, chip in enumerate(self.chips)]

    def start(self):
        if self.own is not None:
            self.own.start()
        for cp in self._firsts():
            cp.start()

    def wait_sibling(self):
        self._copy(0, self.sibling, self.me).wait_recv()

    def wait_chip_and_forward(self, j):
        self._copy(1 + j, (*self.chips[j], self.me[2]), self.me).wait_recv()
        self._passed()[j].start()

    def wait_passed(self, j):
        self._copy(4 + j, (*self.chips[j], 1 - self.me[2]), self.me).wait_recv()

    def wait_sends(self):
        for cp in self._firsts() + self._passed():
            cp.wait_send()
        if self.own is not None:
            self.own.wait()

    def forward(self):
        for j in range(3):
            self.wait_chip_and_forward(j)

    def finish(self):
        self.wait_sibling()
        for j in range(3):
            self.wait_passed(j)
        self.wait_sends()


class _RelayGather:
    def __init__(self, dst, send_sems, recv_sems):
        x, y, c = (lax.axis_index(a) for a in AXES)
        self.c = c
        self.sib, self.xn, self.yn, self.dg = (x, y, 1 - c), (1 - x, y, c), (x, 1 - y, c), (1 - x, 1 - y, c)
        self.me = (x, y, c)
        self.dst, self.send_sems, self.recv_sems = dst, send_sems, recv_sems
        self.half = dst.shape[1] // 2

    def _slot(self, dev, part=None):
        i = _dev_index(dev)
        if part is None:
            return self.dst.at[i]
        return self.dst.at[i, pl.ds(part * self.half, self.half)]

    def _copy(self, k, dev, to, part=None):
        ref = self._slot(dev, part)
        return pltpu.make_async_remote_copy(src_ref=ref, dst_ref=ref, send_sem=self.send_sems.at[k],
                                            recv_sem=self.recv_sems.at[k], device_id=to, device_id_type=MESH)

    def _other(self, dev):
        return (dev[0], dev[1], 1 - self.c)

    def start(self):
        for k, to in enumerate((self.sib, self.xn, self.yn)):
            self._copy(k, self.me, to).start()

    def send_own(self, k):
        return self._copy(k, self.me, (self.sib, self.xn, self.yn)[k])

    def wait_sibling(self):
        self._copy(0, self.sib, self.me).wait_recv()

    def on_x(self):
        self._copy(1, self.xn, self.me).wait_recv()
        self._copy(3, self.xn, self.yn, part=0).start()
        self._copy(5, self.xn, self.sib).start()

    def on_y(self):
        self._copy(2, self.yn, self.me).wait_recv()
        self._copy(4, self.yn, self.xn, part=1).start()
        self._copy(6, self.yn, self.sib).start()

    def on_diag(self):
        self._copy(3, self.dg, self.me, part=0).wait_recv()
        self._copy(4, self.dg, self.me, part=1).wait_recv()
        self._copy(7, self.dg, self.sib).start()

    def wait_passed(self, j):
        self._copy(5 + j, self._other((self.xn, self.yn, self.dg)[j]), self.me).wait_recv()

    def wait_sends(self):
        for k, to in enumerate((self.sib, self.xn, self.yn)):
            self._copy(k, self.me, to).wait_send()
        self._copy(3, self.xn, self.yn, part=0).wait_send()
        self._copy(4, self.yn, self.xn, part=1).wait_send()
        for j, dev in enumerate((self.xn, self.yn, self.dg)):
            self._copy(5 + j, dev, self.sib).wait_send()


def _direct_sems(n):
    if n == 0:
        return []
    return [pltpu.SemaphoreType.DMA((n, 7)), pltpu.SemaphoreType.DMA((n, 7)), pltpu.SemaphoreType.DMA((n,))]


def _adam_math(w, g, m, v):
    m = ADAM_B1 * m + (1.0 - ADAM_B1) * g
    v = ADAM_B2 * v + (1.0 - ADAM_B2) * (g * g)
    m_hat = m / (1.0 - ADAM_B1 ** ADAM_STEP)
    v_hat = v / (1.0 - ADAM_B2 ** ADAM_STEP)
    delta = -ADAM_LR * (m_hat / (jnp.sqrt(v_hat) + ADAM_EPS) + ADAM_WD * w)
    return delta, m, v


def _sum_adam(parts, w, m, v, name):
    R, C = w.shape
    NP = parts.shape[0]
    BR = 4 * CHUNK if R % (4 * CHUNK) == 0 else R

    def body(p_ref, w_ref, m_ref, v_ref, g_ref, d_ref, nm_ref, nv_ref):
        g = p_ref[0].astype(F32)
        for i in range(1, NP):
            g = g + p_ref[i].astype(F32)
        g_ref[...] = g
        d_ref[...], nm_ref[...], nv_ref[...] = _adam_math(w_ref[...], g, m_ref[...], v_ref[...])

    blk = pl.BlockSpec((BR, C), lambda i: (i, 0))
    S = jax.ShapeDtypeStruct((R, C), F32)
    return pl.pallas_call(
        body, name=name, grid=(R // BR,),
        in_specs=[pl.BlockSpec((NP, BR, C), lambda i: (0, i, 0)), blk, blk, blk],
        out_specs=[blk] * 4, out_shape=(S,) * 4,
        compiler_params=_params(("arbitrary",)),
    )(parts, w, m, v)


SUBLANES = 8


def _nrows(size):
    return -(-size // (SUBLANES * LANES)) * SUBLANES


def _view2d(a):
    return a.reshape(-1, LANES) if a.size % LANES == 0 else a.reshape(1, -1)


def _pack_small(parts, total_rows, name):
    arrs = [p[0] for p in parts]

    def body(*refs):
        out = refs[-1]
        out[...] = jnp.zeros_like(out)
        at = 0
        for ref, (a, rows, flag) in zip(refs[:-1], parts):
            val = ref[...].T if flag == "T" else ref[...]
            r, c = (rows, val.shape[1]) if flag == "T" else val.shape
            out[at:at + r, 0:c] = val[:r]
            at += _nrows(r * c)

    return pl.pallas_call(body, name=name, out_shape=jax.ShapeDtypeStruct((total_rows, LANES), F32))(*arrs)


def _small_update(full, me, reps, shards, name):
    n = len(reps) + len(shards)

    def body(me_ref, full_ref, *refs):
        ins, outs = refs[:3 * n], refs[3 * n:]
        at = 0
        for k in range(n):
            w_ref, m_ref, v_ref = ins[3 * k:3 * k + 3]
            r, c = w_ref.shape
            if k < len(reps):
                g = full_ref[at:at + r, 0:c]
                at += _nrows(r * c)
            else:
                seg = full_ref[at:at + N_DEV * r, :]
                row = lax.broadcasted_iota(jnp.int32, seg.shape, 0)
                pick = [jnp.sum(jnp.where(row == r * me_ref[0] + t, seg, 0.0), axis=0, keepdims=True) for t in range(r)]
                g = pick[0] if r == 1 else jnp.concatenate(pick, axis=0)
                at += N_DEV * r
            g_ref, d_ref, nm_ref, nv_ref = outs[4 * k:4 * k + 4]
            g_ref[...] = g
            d_ref[...], nm_ref[...], nv_ref[...] = _adam_math(w_ref[...], g, m_ref[...], v_ref[...])
        outs[4 * n][...] = full_ref[at:at + 1, 0:1]

    flat = [t for p in reps + shards for t in p]
    S = jax.ShapeDtypeStruct
    res = pl.pallas_call(
        body, name=name,
        in_specs=[pl.BlockSpec(memory_space=pltpu.SMEM)] + [pl.BlockSpec(memory_space=pltpu.VMEM)] * (1 + len(flat)),
        out_shape=[S(p[0].shape, F32) for p in reps + shards for _ in range(4)] + [S((1, 1), F32)],
    )(me, full, *flat)
    return [tuple(res[4 * k:4 * k + 4]) for k in range(n)], res[4 * n]


def _rope_tables(T):
    pos = np.arange(T, dtype=np.float32)
    inv_freq = (np.float64(ROPE_THETA) ** (-np.arange(0, HEAD_DIM, 2, dtype=np.float64) / HEAD_DIM)).astype(np.float32)
    ang = (pos[:, None] * inv_freq[None, :]).astype(np.float64)
    cos, sin, zero = np.cos(ang).astype(np.float32), np.sin(ang).astype(np.float32), np.zeros(ang.shape, np.float32)
    c = np.concatenate([cos, cos, cos, cos], axis=1)
    s1 = np.concatenate([-sin, zero, -sin, zero], axis=1)
    s2 = np.concatenate([zero, sin, zero, sin], axis=1)
    return jnp.asarray(c), jnp.asarray(s1), jnp.asarray(s2)


def kernel(x, a_norm_g, a_w_in, a_ln_g, a_ln_b, a_ws, a_bs, a_w_out, kv_norm_g, w_kv, b_kv, b_norm_g, b_w_in, b_bq, b_sinks, b_w_out, final_norm_g, loss_target, m_a_norm_g, m_a_w_in, m_a_ln_g, m_a_ln_b, m_a_ws, m_a_bs, m_a_w_out, m_kv_norm_g, m_w_kv, m_b_kv, m_b_norm_g, m_b_w_in, m_b_bq, m_b_sinks, m_b_w_out, m_final_norm_g, v_a_norm_g, v_a_w_in, v_a_ln_g, v_a_ln_b, v_a_ws, v_a_bs, v_a_w_out, v_kv_norm_g, v_w_kv, v_b_kv, v_b_norm_g, v_b_w_in, v_b_bq, v_b_sinks, v_b_w_out, v_final_norm_g):
    T, D = x.shape[1], x.shape[2]
    AW = a_ln_g.shape[1] * N_DEV
    G = a_ws.shape[1]
    assert w_kv.shape[1] == 2 * LANES and a_ws.shape[2] == CHUNK and T % CHUNK == 0
    me = _my_index()

    xs, tgt = x[0], loss_target[0]
    z, wa_in_t, g_a, ln_g, ln_b, wa_out, wkv = _in_proj(xs, a_w_in[0], [a_norm_g, a_ln_g, a_ln_b], me.reshape(1),
                                                        [a_w_out[0], w_kv])
    wa_in_t = wa_in_t.reshape(-1, D)
    wa_out = wa_out.reshape(AW, D)
    wkv = wkv.reshape(D, 2 * LANES)

    rc, rs1, rs2 = _rope_tables(T)
    ws = a_ws[0]
    g_kv = kv_norm_g.reshape(1, D)
    bkv = b_kv.reshape(1, -1)
    g_f = final_norm_g.reshape(1, D)
    sinks = b_sinks.reshape(1, 16)
    h1, sv, vhat, rstd, k4, v4, kt, vt, wb_in, wb_out = _a_fwd(
        xs, z, ln_g, ln_b, ws, a_bs[0], wa_out, g_kv, wkv, bkv, rc, rs1, rs2, [b_w_in[0], b_w_out[0]])
    wb_out = wb_out.reshape(-1, D)
    q, g2, o, dh2, dh2_b, loss, d_gf = _b_fwd(h1, b_norm_g, wb_in, b_bq, rc, rs1, rs2, k4, vt, sinks, wb_out, g_f, tgt)
    dh1p, dz2, n2, y2, dk, dv, d_bq, d_gb, d_sink = _b_bwd(dh2, h1, q, g2, o, k4, v4, kt, sinks, wb_out, wb_in,
                                                           b_norm_g, rc, rs1, rs2)
    d_sink = d_sink[:, :4].reshape(2, 2, 4).transpose(0, 2, 1).reshape(1, 16)
    gw_b_in, gw_b_out = _wgrad([(n2, dz2, N_DEV), (y2, dh2_b, 1)], "wgrad_b", bt=1024)
    gw_b_out = gw_b_out.reshape(N_DEV, -1, D)
    (dz, gw_a_out, gw_kv, dh1_f, d_gkv, d_bkv, d_lng, d_lnb, d_ws, d_bst, r_b_in, r_b_out) = _a_bwd(
        dh1p, dk, dv, h1, g_kv, wkv, wa_out, ws, ln_g, ln_b, z, sv, vhat, rstd, rc, rs1, rs2, [gw_b_in, gw_b_out])
    dx, n1, d_ga, r_a_out, r_kv = _a_in_bwd(dz, wa_in_t, xs, dh1_f, g_a, [gw_a_out, gw_kv])
    small = [(_view2d(d_ws), None, None), (d_bst, G, "T")] + [(_view2d(a), None, None) for a in (
        d_gkv, d_bkv, d_gb, d_bq, d_sink, d_gf, d_ga, d_lng, d_lnb, loss)]
    used = sum(_nrows(G * CHUNK if flag else a.size) for a, _, flag in small)
    per = -(-used // (SUBLANES * N_DEV)) * SUBLANES
    small_pack = _pack_small(small, per * N_DEV, "pack_small").reshape(N_DEV, per, LANES)
    r_a_in, full_small = _wgrad_exchange(n1, dz, me.reshape(1), small_pack, "wgrad_a_in")

    g_a_in, d_a_in, nm_a_in, nv_a_in = _sum_adam(r_a_in, a_w_in[0], m_a_w_in[0], v_a_w_in[0], "adam_a_in")
    g_a_out, d_a_out, nm_a_out, nv_a_out = _sum_adam(r_a_out, a_w_out[0], m_a_w_out[0], v_a_w_out[0], "adam_a_out")
    g_kvw, d_kvw, nm_kvw, nv_kvw = _sum_adam(r_kv, w_kv, m_w_kv, v_w_kv, "adam_kv")
    g_b_in, d_b_in, nm_b_in, nv_b_in = _sum_adam(r_b_in, b_w_in[0], m_b_w_in[0], v_b_w_in[0], "adam_b_in")
    g_b_out, d_b_out, nm_b_out, nv_b_out = _sum_adam(r_b_out, b_w_out[0], m_b_w_out[0], v_b_w_out[0], "adam_b_out")

    full_small = full_small.reshape(N_DEV * per, LANES)
    reps = [(a_ws, m_a_ws, v_a_ws), (a_bs, m_a_bs, v_a_bs), (kv_norm_g, m_kv_norm_g, v_kv_norm_g),
            (b_kv, m_b_kv, v_b_kv), (b_norm_g, m_b_norm_g, v_b_norm_g), (b_bq, m_b_bq, v_b_bq),
            (b_sinks, m_b_sinks, v_b_sinks), (final_norm_g, m_final_norm_g, v_final_norm_g)]
    shards = [(a_norm_g, m_a_norm_g, v_a_norm_g), (a_ln_g, m_a_ln_g, v_a_ln_g), (a_ln_b, m_a_ln_b, v_a_ln_b)]
    upd, loss = _small_update(full_small, me.reshape(1), [tuple(_view2d(t) for t in p) for p in reps],
                              [tuple(_view2d(t) for t in p) for p in shards], "adam_small")
    loss = loss[0, 0]
    sm_g, sd, snm, snv = ([upd[k][j].reshape(p[0].shape) for k, p in enumerate(reps + shards)] for j in range(4))

    def order(big, sm):
        a_in, a_out, kvw, b_in, b_out = big
        ws_, bs_, kvg, bkv_, bng, bq_, snk, fng, ang, alng, alnb = sm
        return (ang, a_in[None], alng, alnb, ws_, bs_, a_out[None], kvg, kvw, bkv_, bng, b_in[None], bq_, snk,
                b_out[None], fng)

    grads = order((g_a_in, g_a_out, g_kvw, g_b_in, g_b_out), sm_g)
    deltas = order((d_a_in, d_a_out, d_kvw, d_b_in, d_b_out), sd)
    new_m = order((nm_a_in, nm_a_out, nm_kvw, nm_b_in, nm_b_out), snm)
    new_v = order((nv_a_in, nv_a_out, nv_kvw, nv_b_in, nv_b_out), snv)
    return (loss, dx[None], *grads, *deltas, *new_m, *new_v)
```

```python
import functools

import jax
import jax.numpy as jnp
import numpy as np
from jax import lax
from jax.experimental import pallas as pl
from jax.experimental.pallas import tpu as pltpu

CHUNK = 128
HEAD_DIM = 64
ROPE_THETA = 10000.0
EPS = 1e-5
ADAM_LR = 0.001
ADAM_B1 = 0.9
ADAM_B2 = 0.999
ADAM_EPS = 1e-08
ADAM_WD = 0.01
ADAM_STEP = 10
N_DEV = 8
LANES = 128
NEG = -1e30

BF = jnp.bfloat16
F32 = jnp.float32
MESH = pl.DeviceIdType.MESH
AXES = ("x", "y", "c")
VMEM_LIMIT = 56 * 1024 * 1024


def _dot(a, b):
    return jnp.dot(a, b, preferred_element_type=F32)


def _dot_nt(a, b):
    return lax.dot_general(a, b, (((1,), (1,)), ((), ())), preferred_element_type=F32)


def _dot_tn(a, b):
    return lax.dot_general(a, b, (((0,), (0,)), ((), ())), preferred_element_type=F32)


def _const_spec(shape):
    nd = len(shape)
    return pl.BlockSpec(shape, lambda *_: (0,) * nd, pipeline_mode=pl.Buffered(1))


def _acc_spec(shape):
    nd = len(shape)
    return pl.BlockSpec(shape, lambda *_: (0,) * nd)


def _row_spec(tm, width):
    return pl.BlockSpec((tm, width), lambda i: (i, 0))


def _col_spec(tm, height):
    return pl.BlockSpec((height, tm), lambda i: (0, i))


def _params(sem):
    return pltpu.CompilerParams(dimension_semantics=sem, vmem_limit_bytes=VMEM_LIMIT)


def _rot(x, c, s1, s2):
    return x * c + pltpu.roll(x, 96, 1) * s1 + pltpu.roll(x, 32, 1) * s2


def _rot_bwd(d, c, s1, s2):
    return d * c + pltpu.roll(d * s1, 32, 1) + pltpu.roll(d * s2, 96, 1)


def _silu_parts(g):
    sg = jax.nn.sigmoid(g)
    return g * sg, sg * (1.0 + g * (1.0 - sg))


def _rms_bwd(dn, xh, r, g):
    a = dn * g
    return r * (a - xh * jnp.mean(a * xh, axis=-1, keepdims=True))


def _lane_lo(shape):
    return lax.broadcasted_iota(jnp.int32, shape, 1) < HEAD_DIM


def _split4(t):
    lo = _lane_lo(t.shape)
    tr = pltpu.roll(t, HEAD_DIM, 1)
    z = jnp.zeros_like(t)
    return jnp.concatenate([jnp.where(lo, t, z), jnp.where(lo, z, tr), jnp.where(lo, tr, z), jnp.where(lo, z, t)], axis=1)


def _stack_pairs(t, h):
    return jnp.concatenate([t[:, (h * 4 + j) * LANES:(h * 4 + j + 1) * LANES] for j in range(4)], axis=0)


def _upper():
    shape = (CHUNK, 4 * CHUNK)
    return lax.broadcasted_iota(jnp.int32, shape, 0) > (lax.broadcasted_iota(jnp.int32, shape, 1) & (CHUNK - 1))


def _band_rows(tile_ref, before_ref, c, h):
    a = slice(2 * h * LANES, (2 * h + 1) * LANES)
    b = slice((2 * h + 1) * LANES, (2 * h + 2) * LANES)
    cur = slice(c * CHUNK, (c + 1) * CHUNK)

    def prev(cols):
        return before_ref[:, cols] if c == 0 else tile_ref[(c - 1) * CHUNK:c * CHUNK, cols]

    return jnp.concatenate([prev(a), tile_ref[cur, a], prev(b), tile_ref[cur, b]], axis=0)


def _band_cols(tile_ref, before_ref, c, h):
    a = slice(2 * h * LANES, (2 * h + 1) * LANES)
    b = slice((2 * h + 1) * LANES, (2 * h + 2) * LANES)

    def prev(rows):
        return before_ref[0, rows, :] if c == 0 else tile_ref[c - 1, rows, :]

    return jnp.concatenate([prev(a), tile_ref[c, a, :], prev(b), tile_ref[c, b, :]], axis=1)


def _band_specs(tm):
    nc = tm // CHUNK

    def before(i):
        return jnp.maximum(i * nc - 1, 0)

    return (pl.BlockSpec((tm, 4 * LANES), lambda i: (i, 0)),
            pl.BlockSpec((CHUNK, 4 * LANES), lambda i: (before(i), 0)),
            pl.BlockSpec((nc, 4 * LANES, CHUNK), lambda i: (i, 0, 0)),
            pl.BlockSpec((1, 4 * LANES, CHUNK), lambda i: (before(i), 0, 0)))


def _fold(t, upper, has_prev=None):
    out = []
    for k in range(2):
        prev = t[2 * k * CHUNK:(2 * k + 1) * CHUNK]
        if has_prev is not None:
            prev = jnp.where(has_prev, prev, NEG)
        out.append(jnp.where(upper, prev, t[(2 * k + 1) * CHUNK:(2 * k + 2) * CHUNK]))
    return out


def _unfold(fa, fb, upper):
    z = jnp.zeros_like(fa)
    return jnp.concatenate([jnp.where(upper, fa, z), jnp.where(upper, z, fa),
                            jnp.where(upper, fb, z), jnp.where(upper, z, fb)], axis=0)


def _sink_tile(s_ref):
    shape = (4, 4 * LANES)
    row = lax.broadcasted_iota(jnp.int32, shape, 0)
    pair = lax.broadcasted_iota(jnp.int32, shape, 1) // LANES
    idx = (row // 2) * 8 + pair * 2 + row % 2
    tile = jnp.zeros(shape, F32)
    for n in range(16):
        tile = jnp.where(idx == n, s_ref[0, n], tile)
    return tile


def _softmax_sink(f, sink):
    m = jnp.maximum(jnp.max(f, axis=0, keepdims=True), sink)
    p = jnp.exp(f - m)
    es = jnp.exp(sink - m)
    inv = 1.0 / (jnp.sum(p, axis=0, keepdims=True) + es)
    return p * inv, es * inv


class _Riding:
    def __init__(self, shards, gathered, stages, sems, n_steps):
        self.shards, self.stages, self.n_steps = shards, stages, n_steps
        ssem, rsem, lsem = sems
        self.gathers = [_TwoLevel(stages[k], gathered[k], ssem.at[k], rsem.at[k], lsem.at[k])
                        for k in range(len(shards))]

    def begin(self, i):
        @pl.when(i == 0)
        def _():
            for shard, stage, g in zip(self.shards, self.stages, self.gathers):
                stage[...] = shard[...].astype(stage.dtype)
                g.start()

    def end(self, i):
        @pl.when(i == self.n_steps // 2)
        def _():
            for g in self.gathers:
                g.forward()

        @pl.when(i == self.n_steps - 1)
        def _():
            for g in self.gathers:
                g.finish()

    @staticmethod
    def specs(later):
        nl = len(later)
        hbm = pl.BlockSpec(memory_space=pl.ANY)
        return ([_const_spec(w.shape) for w in later], [hbm] * nl,
                tuple(jax.ShapeDtypeStruct((N_DEV,) + w.shape, BF) for w in later),
                [pltpu.VMEM(w.shape, BF) for w in later] + _direct_sems(nl))


PASS_MASKS = ((0, 1, 2, 5, 4, 3, 6, 7), (0, 1, 4, 3, 2, 5, 6, 7))


def _in_proj(x, w_shard, vec_shards, me, later):
    T, D = x.shape
    SH = w_shard.shape[1]
    TM = min(1024, T)
    nT = T // TM
    nl = len(later)
    nv = len(vec_shards)
    widths = [v.shape[1] for v in vec_shards]
    offsets = [sum(widths[:k]) for k in range(nv)]
    vec_shape = (SUBLANES, sum(widths))
    ds = widths[0]
    last = N_DEV - 1
    masks = jnp.asarray(np.array(PASS_MASKS, np.int32).reshape(-1))

    def slot(p, me_ref, masks_ref):
        return me_ref[0] ^ masks_ref[(me_ref[0] & 1) * N_DEV + p]

    def body(me_ref, masks_ref, x_ref, wsh_ref, *rest):
        vsh_refs, rest = rest[:nv], rest[nv:]
        shards, rest = rest[:nl], rest[nl:]
        (z_ref, wt_ref), rest = rest[:2], rest[2:]
        vout_refs, rest = rest[:nv], rest[nv:]
        gathered, rest = rest[:nl], rest[nl:]
        (w_scr, vec_scr, vstage, n1_scr, ga_scr, w_s, w_r, v_s, v_r, v_l), rest = rest[:10], rest[10:]
        stages, sems = rest[:nl], rest[nl:]
        p, i = pl.program_id(0), pl.program_id(1)
        me = _my_index()
        wg = _RelayGather(w_scr, w_s, w_r)
        vg = _Direct(vstage, vec_scr, v_s, v_r, v_l, scatter=False)
        lg = [_TwoLevel(stages[k], gathered[k], sems[0].at[k], sems[1].at[k], sems[2].at[k]) for k in range(nl)]

        def at_pass(k):
            return (p == k) & (i == 0)

        c = lax.axis_index("c")

        @pl.when(at_pass(0))
        def _():
            for ref, off, wd in zip(vsh_refs, offsets, widths):
                vstage[:, off:off + wd] = jnp.broadcast_to(ref[...], (SUBLANES, wd))
            vg.start()
            w_scr[me] = wsh_ref[...].astype(BF)
            wg.send_own(0).start()

            @pl.when(c == 1)
            def _():
                wg.send_own(1).start()

            @pl.when(c == 0)
            def _():
                wg.send_own(2).start()

            vg.finish()
            for j in range(N_DEV):
                ga_scr[:, j * ds:(j + 1) * ds] = vec_scr[j, 0:1, 0:ds]
                for ref, off, wd in zip(vout_refs, offsets, widths):
                    ref[:, j * wd:(j + 1) * wd] = vec_scr[j, 0:1, off:off + wd]

        @pl.when(at_pass(1))
        def _():
            wg.wait_sibling()

        for first, second, landed_first, landed_second in ((1, 2, wg.on_x, wg.on_y), (2, 1, wg.on_y, wg.on_x)):
            mine = c == (1 if first == 1 else 0)

            @pl.when(at_pass(2) & mine)
            def _(second=second, landed_first=landed_first):
                wg.send_own(second).start()
                landed_first()

            @pl.when(at_pass(3) & mine)
            def _(second=second):
                wg.wait_passed(second - 1)

            @pl.when(at_pass(4) & mine)
            def _(landed_second=landed_second):
                landed_second()

            @pl.when(at_pass(5) & mine)
            def _(first=first):
                wg.wait_passed(first - 1)

        @pl.when(at_pass(4))
        def _():
            for k in range(nl):
                stages[k][...] = shards[k][...].astype(BF)
                lg[k].start()

        @pl.when(at_pass(6))
        def _():
            wg.on_diag()

        @pl.when(at_pass(7))
        def _():
            wg.wait_passed(2)

        @pl.when(p == 0)
        def _():
            xv = x_ref[...]
            r1 = lax.rsqrt(jnp.mean(xv * xv, axis=-1, keepdims=True) + EPS)
            n1_scr[i] = (xv * r1 * ga_scr[...]).astype(BF)

        z_ref[...] = _dot(n1_scr[i], w_scr[slot(p, me_ref, masks_ref)]).astype(BF)

        @pl.when(i == 0)
        def _():
            wt_ref[0] = w_scr[slot(p, me_ref, masks_ref)].T

        @pl.when((p == last) & (i == nT - 1))
        def _():
            wg.wait_sends()
            for g in lg:
                g.forward()
            for g in lg:
                g.finish()

    hbm = pl.BlockSpec(memory_space=pl.ANY)
    dma = pltpu.SemaphoreType.DMA
    S = jax.ShapeDtypeStruct
    def whole(shape):
        return pl.BlockSpec(shape, lambda p, i, m, t: (0, 0))

    def once(shape):
        return pl.BlockSpec(shape, lambda p, i, m, t: (0, 0), pipeline_mode=pl.Buffered(1))

    grid_spec = pltpu.PrefetchScalarGridSpec(
        num_scalar_prefetch=2, grid=(N_DEV, nT),
        in_specs=[pl.BlockSpec((TM, D), lambda p, i, m, t: (jnp.where(p == 0, i, nT - 1), 0)), once(w_shard.shape)]
        + [once(v.shape) for v in vec_shards] + [once(w.shape) for w in later],
        out_specs=[pl.BlockSpec((TM, SH), lambda p, i, m, t: (i, slot(p, m, t))),
                   pl.BlockSpec((1, SH, D), lambda p, i, m, t: (slot(p, m, t), 0, 0))]
        + [whole((1, N_DEV * wd)) for wd in widths] + [hbm] * nl,
        scratch_shapes=[pltpu.VMEM((N_DEV, D, SH), BF), pltpu.VMEM((N_DEV,) + vec_shape, F32),
                        pltpu.VMEM(vec_shape, F32), pltpu.VMEM((nT, TM, D), BF), pltpu.VMEM((1, D), F32),
                        dma((8,)), dma((8,)), dma((7,)), dma((7,)), dma]
        + [pltpu.VMEM(w.shape, BF) for w in later] + _direct_sems(nl))
    return pl.pallas_call(
        body, name="a_in_proj", grid_spec=grid_spec,
        out_shape=(S((T, N_DEV * SH), BF), S((N_DEV, SH, D), BF)) + tuple(S((1, N_DEV * wd), F32) for wd in widths)
        + tuple(S((N_DEV,) + w.shape, BF) for w in later),
        compiler_params=_params(("arbitrary", "arbitrary")),
    )(me, masks, x, w_shard, *vec_shards, *later)


def _a_fwd(x, z, ln_g, ln_b, ws, bs, wa_out, g_kv, w_kv, b_kv, rc, rs1, rs2, later):
    T, D = x.shape
    AW = wa_out.shape[0]
    G = ws.shape[0]
    TM = min(512, T)
    nT = T // TM
    nC = TM // CHUNK
    nl = len(later)

    def body(x_ref, u_ref, v_ref, gt_ref, lng_ref, lnb_ref, ws_ref, bs_ref, waout_ref, gkv_ref, wkv_ref, bkv_ref,
             rc_ref, rs1_ref, rs2_ref, *rest):
        shards, rest = rest[:nl], rest[nl:]
        (h1_ref, sv_ref, vhat_ref, rstd_ref, k4_ref, v4_ref, kt_ref, vt_ref), rest = rest[:8], rest[8:]
        gathered, sv_scr, stages, sems = rest[:nl], rest[nl], rest[nl + 1:2 * nl + 1], rest[2 * nl + 1:]
        i = pl.program_id(0)
        riding = _Riding(shards, gathered, stages, sems, nT)
        riding.begin(i)
        xv = x_ref[...]
        u = u_ref[...].astype(F32)
        v = v_ref[...].astype(F32)
        gt = gt_ref[...].astype(F32)
        mu = jnp.mean(v, axis=-1, keepdims=True)
        xc = v - mu
        rstd = lax.rsqrt(jnp.mean(xc * xc, axis=-1, keepdims=True) + EPS)
        vhat = xc * rstd
        vln = (vhat * lng_ref[...] + lnb_ref[...]).astype(BF)
        tri = lax.broadcasted_iota(jnp.int32, (CHUNK, CHUNK), 0) >= lax.broadcasted_iota(jnp.int32, (CHUNK, CHUNK), 1)
        bst = jnp.concatenate([bs_ref[...], jnp.zeros((CHUNK - G, CHUNK), F32)], axis=0).T
        for g in range(G):
            wsm = jnp.where(tri, ws_ref[g], 0.0).astype(BF)
            bias = bst[:, g:g + 1]
            for c in range(nC):
                blk = vln[c * CHUNK:(c + 1) * CHUNK, g * CHUNK:(g + 1) * CHUNK]
                sv_scr[c * CHUNK:(c + 1) * CHUNK, g * CHUNK:(g + 1) * CHUNK] = _dot(wsm, blk) + bias
        sv = sv_scr[...]
        silu, _ = _silu_parts(gt)
        y = (u * sv * silu).astype(BF)
        h1 = xv + _dot(y, waout_ref[...])
        h1_ref[...] = h1
        sv_ref[...] = sv.astype(BF)
        vhat_ref[...] = vhat.astype(BF)
        rstd_ref[...] = jnp.broadcast_to(rstd, rstd_ref.shape)
        rkv = lax.rsqrt(jnp.mean(h1 * h1, axis=-1, keepdims=True) + EPS)
        nkv = (h1 * rkv * gkv_ref[...]).astype(BF)
        kv = _dot(nkv, wkv_ref[...]) + bkv_ref[...]
        k_rot = _rot(kv[:, :LANES], rc_ref[...], rs1_ref[...], rs2_ref[...])
        for src, ref, tref in ((k_rot, k4_ref, kt_ref), (kv[:, LANES:], v4_ref, vt_ref)):
            t4 = _split4(src)
            ref[...] = t4.astype(BF)
            for c in range(nC):
                for b in range(4):
                    blk = t4[c * CHUNK:(c + 1) * CHUNK, b * LANES:(b + 1) * LANES]
                    tref[c, b * LANES:(b + 1) * LANES, :] = blk.T.astype(BF)
        riding.end(i)

    row = functools.partial(_row_spec, TM)
    zcol = [pl.BlockSpec((TM, AW), functools.partial(lambda k, i: (i, k), k)) for k in range(3)]
    tr = pl.BlockSpec((nC, 4 * LANES, CHUNK), lambda i: (i, 0, 0))
    r_in, r_out, r_shape, r_scratch = _Riding.specs(later)
    S = jax.ShapeDtypeStruct
    return pl.pallas_call(
        body, name="a_fwd", grid=(nT,),
        in_specs=[row(D)] + zcol + [_const_spec((1, AW)), _const_spec((1, AW)),
                  _const_spec(ws.shape), _const_spec(bs.shape), _const_spec(wa_out.shape), _const_spec((1, D)),
                  _const_spec(w_kv.shape), _const_spec((1, 2 * LANES)), row(LANES), row(LANES), row(LANES)] + r_in,
        out_specs=[row(D), row(AW), row(AW), row(LANES), row(4 * LANES), row(4 * LANES), tr, tr] + r_out,
        out_shape=(S((T, D), F32), S((T, AW), BF), S((T, AW), BF), S((T, LANES), F32),
                   S((T, 4 * LANES), BF), S((T, 4 * LANES), BF),
                   S((T // CHUNK, 4 * LANES, CHUNK), BF), S((T // CHUNK, 4 * LANES, CHUNK), BF)) + r_shape,
        scratch_shapes=[pltpu.VMEM((TM, AW), F32)] + r_scratch,
        compiler_params=_params(("arbitrary",)),
    )(x, z, z, z, ln_g, ln_b, ws, bs, wa_out, g_kv, w_kv, b_kv, rc, rs1, rs2, *later)


def _b_fwd(h1, g_b, wb_in, bq, rc, rs1, rs2, k4, vt, sinks, wb_out, g_f, target):
    T, D = h1.shape
    BW = wb_out.shape[0]
    SH = wb_in.shape[2]
    TM = min(512, T)
    nC = TM // CHUNK
    nP = BW // LANES

    def body(h1_ref, gb_ref, wbin_ref, bq_ref, rc_ref, rs1_ref, rs2_ref, k4_ref, k4p_ref, vt_ref, vtp_ref, sinks_ref,
             wbout_ref, gf_ref, tgt_ref, q_ref, g2_ref, o_ref, dh2_ref, dh2b_ref, loss_ref, dgf_ref, z_scr, o_scr):
        i = pl.program_id(0)
        sink = _sink_tile(sinks_ref)

        @pl.when(i == 0)
        def _():
            loss_ref[...] = jnp.zeros_like(loss_ref)
            dgf_ref[...] = jnp.zeros_like(dgf_ref)

        h1v = h1_ref[...]
        r2 = lax.rsqrt(jnp.mean(h1v * h1v, axis=-1, keepdims=True) + EPS)
        n2 = (h1v * r2 * gb_ref[...]).astype(BF)
        for j in range(N_DEV):
            z_scr[:, j * SH:(j + 1) * SH] = _dot(n2, wbin_ref[j])
        c_t, s1_t, s2_t = rc_ref[...], rs1_ref[...], rs2_ref[...]
        for p in range(nP):
            cols = slice(p * LANES, (p + 1) * LANES)
            qp = _rot(z_scr[:, cols] + bq_ref[:, cols], c_t, s1_t, s2_t) * (HEAD_DIM ** -0.5)
            q_ref[:, cols] = qp.astype(BF)
        g2 = z_scr[:, BW:]
        g2_ref[...] = g2.astype(BF)
        upper = _upper()
        for c in range(nC):
            ci = i * nC + c
            rows = slice(c * CHUNK, (c + 1) * CHUNK)
            qc = q_ref[rows, :]
            for h in range(2):
                st = _dot_nt(_band_rows(k4_ref, k4p_ref, c, h), _stack_pairs(qc, h))
                fa, fb = _fold(st, upper, ci > 0)
                pa, _ = _softmax_sink(fa, sink[2 * h:2 * h + 1, :])
                pb, _ = _softmax_sink(fb, sink[2 * h + 1:2 * h + 2, :])
                ot = _dot(_band_cols(vt_ref, vtp_ref, c, h), _unfold(pa, pb, upper).astype(BF))
                for j in range(4):
                    o_scr[rows, (h * 4 + j) * LANES:(h * 4 + j + 1) * LANES] = ot[:, j * CHUNK:(j + 1) * CHUNK].T
        o = o_scr[...]
        o_ref[...] = o.astype(BF)
        silu, _ = _silu_parts(g2)
        h2 = h1v + _dot((o * silu).astype(BF), wbout_ref[...])
        rf = lax.rsqrt(jnp.mean(h2 * h2, axis=-1, keepdims=True) + EPS)
        xh = h2 * rf
        gf = gf_ref[...]
        err = xh * gf - tgt_ref[...]
        dyf = err * (1.0 / D)
        dh2 = _rms_bwd(dyf, xh, rf, gf)
        dh2_ref[...] = dh2
        dh2b_ref[...] = dh2.astype(BF)
        loss_ref[...] += 0.5 * jnp.sum(jnp.mean(err * err, axis=-1, keepdims=True), axis=0, keepdims=True)
        dgf_ref[...] += jnp.sum(dyf * xh, axis=0, keepdims=True)

    row = functools.partial(_row_spec, TM)
    rows_tile, rows_before, cols_tile, cols_before = _band_specs(TM)
    S = jax.ShapeDtypeStruct
    return pl.pallas_call(
        body, name="b_fwd", grid=(T // TM,),
        in_specs=[row(D), _const_spec((1, D)), _const_spec(wb_in.shape), _const_spec((1, BW)), row(LANES), row(LANES),
                  row(LANES), rows_tile, rows_before, cols_tile, cols_before, pl.BlockSpec(memory_space=pltpu.SMEM),
                  _const_spec(wb_out.shape), _const_spec((1, D)), row(D)],
        out_specs=[row(BW), row(BW), row(BW), row(D), row(D), _acc_spec((1, 1)), _acc_spec((1, D))],
        out_shape=(S((T, BW), BF), S((T, BW), BF), S((T, BW), BF), S((T, D), F32), S((T, D), BF), S((1, 1), F32),
                   S((1, D), F32)),
        scratch_shapes=[pltpu.VMEM((TM, 2 * BW), F32), pltpu.VMEM((TM, BW), F32)],
        compiler_params=_params(("arbitrary",)),
    )(h1, g_b, wb_in, bq, rc, rs1, rs2, k4, k4, vt, vt, sinks, wb_out, g_f, target)


def _b_bwd(dh2, h1, q, g2, o, k4, v4, kt, sinks, wb_out, wb_in, g_b, rc, rs1, rs2):
    T, D = h1.shape
    BW = wb_out.shape[0]
    SH = wb_in.shape[2]
    TM = min(512, T)
    nT = T // TM
    nC = TM // CHUNK
    nP = BW // LANES

    def body(dh2_ref, h1_ref, q_ref, g2_ref, o_ref, k4_ref, k4p_ref, v4_ref, v4p_ref, kt_ref, ktp_ref, sinks_ref,
             wbout_ref, wbin_ref, gb_ref, rc_ref, rs1_ref, rs2_ref,
             dh1_ref, dz2_ref, n2_ref, y2_ref, dk_ref, dv_ref, dbq_ref, dgb_ref, dsink_ref, do_scr, dq_scr, dsacc_scr):
        i = pl.program_id(0)
        sink = _sink_tile(sinks_ref)

        @pl.when(i == 0)
        def _():
            dk_ref[...] = jnp.zeros_like(dk_ref)
            dv_ref[...] = jnp.zeros_like(dv_ref)
            dbq_ref[...] = jnp.zeros_like(dbq_ref)
            dgb_ref[...] = jnp.zeros_like(dgb_ref)
            dsacc_scr[...] = jnp.zeros_like(dsacc_scr)

        dh2 = dh2_ref[...]
        dy2 = _dot_nt(dh2.astype(BF), wbout_ref[...])
        silu, dsilu = _silu_parts(g2_ref[...].astype(F32))
        do_scr[...] = (dy2 * silu).astype(BF)
        dy2, silu, dsilu = dy2.astype(BF), silu.astype(BF), dsilu.astype(BF)
        ob = o_ref[...]
        y2_ref[...] = (ob * silu).T
        dz2_ref[:, BW:] = dy2 * ob * dsilu
        upper = _upper()
        lo = _lane_lo((2 * CHUNK, LANES))
        for c in range(nC):
            ci = i * nC + c
            rows = slice(c * CHUNK, (c + 1) * CHUNK)
            pci = jnp.maximum(ci - 1, 0)
            prev = pl.multiple_of(pci * CHUNK, CHUNK)
            cur = pl.multiple_of(ci * CHUNK, CHUNK)
            qc = q_ref[rows, :]
            doc = do_scr[rows, :]
            dkb = jnp.zeros((2 * CHUNK, LANES), F32)
            dvb = jnp.zeros((2 * CHUNK, LANES), F32)
            for h in range(2):
                qs = _stack_pairs(qc, h)
                dos = _stack_pairs(doc, h)
                fa, fb = _fold(_dot_nt(_band_rows(k4_ref, k4p_ref, c, h), qs), upper, ci > 0)
                dfa, dfb = _fold(_dot_nt(_band_rows(v4_ref, v4p_ref, c, h), dos), upper)
                folded = []
                for k, (f, df) in enumerate(((fa, dfa), (fb, dfb))):
                    p, ps = _softmax_sink(f, sink[2 * h + k:2 * h + k + 1, :])
                    delta = jnp.sum(p * df, axis=0, keepdims=True)
                    dsacc_scr[2 * h + k:2 * h + k + 1, :] -= ps * delta
                    folded.append((p, p * (df - delta)))
                pt = _unfold(folded[0][0], folded[1][0], upper).astype(BF)
                dst = _unfold(folded[0][1], folded[1][1], upper).astype(BF)
                dqt = _dot(_band_cols(kt_ref, ktp_ref, c, h), dst)
                for j in range(4):
                    dq_scr[rows, (h * 4 + j) * LANES:(h * 4 + j + 1) * LANES] = dqt[:, j * CHUNK:(j + 1) * CHUNK].T
                for acc_name, g in (("k", _dot(dst, qs)), ("v", _dot(pt, dos))):
                    a, b = g[:2 * CHUNK], g[2 * CHUNK:]
                    if h == 0:
                        part = jnp.where(lo, a + pltpu.roll(b, HEAD_DIM, 1), 0.0)
                    else:
                        part = jnp.where(lo, 0.0, pltpu.roll(a, HEAD_DIM, 1) + b)
                    if acc_name == "k":
                        dkb += part
                    else:
                        dvb += part
            dk_ref[pl.ds(prev, CHUNK), :] += dkb[:CHUNK]
            dk_ref[pl.ds(cur, CHUNK), :] += dkb[CHUNK:]
            dv_ref[pl.ds(prev, CHUNK), :] += dvb[:CHUNK]
            dv_ref[pl.ds(cur, CHUNK), :] += dvb[CHUNK:]
        c_t, s1_t, s2_t = rc_ref[...], rs1_ref[...], rs2_ref[...]
        for p in range(nP):
            cols = slice(p * LANES, (p + 1) * LANES)
            dqp = _rot_bwd(dq_scr[:, cols] * (HEAD_DIM ** -0.5), c_t, s1_t, s2_t)
            dbq_ref[:, cols] += jnp.sum(dqp, axis=0, keepdims=True)
            dz2_ref[:, cols] = dqp.astype(BF)
        h1v = h1_ref[...]
        r2 = lax.rsqrt(jnp.mean(h1v * h1v, axis=-1, keepdims=True) + EPS)
        xh = h1v * r2
        gb = gb_ref[...]
        n2_ref[...] = (xh * gb).astype(BF).T
        dn2 = None
        for j in range(N_DEV):
            part = _dot_nt(dz2_ref[:, j * SH:(j + 1) * SH], wbin_ref[j])
            dn2 = part if dn2 is None else dn2 + part
        dgb_ref[...] += jnp.sum(dn2 * xh, axis=0, keepdims=True)
        dh1_ref[...] = dh2 + _rms_bwd(dn2, xh, r2, gb)

        @pl.when(i == nT - 1)
        def _():
            lane = lax.broadcasted_iota(jnp.int32, dsink_ref.shape, 1)
            tot = jnp.zeros(dsink_ref.shape, F32)
            for j in range(4):
                tot += jnp.where(lane == j, jnp.sum(dsacc_scr[:, j * CHUNK:(j + 1) * CHUNK], axis=1, keepdims=True), 0.0)
            dsink_ref[...] = tot

    row = functools.partial(_row_spec, TM)
    rows_tile, rows_before, cols_tile, cols_before = _band_specs(TM)
    S = jax.ShapeDtypeStruct
    return pl.pallas_call(
        body, name="b_bwd", grid=(T // TM,),
        in_specs=[row(D), row(D), row(BW), row(BW), row(BW), rows_tile, rows_before, rows_tile, rows_before,
                  cols_tile, cols_before, pl.BlockSpec(memory_space=pltpu.SMEM), _const_spec(wb_out.shape), _const_spec(wb_in.shape),
                  _const_spec((1, D)), row(LANES), row(LANES), row(LANES)],
        out_specs=[row(D), row(2 * BW), _col_spec(TM, D), _col_spec(TM, BW), _acc_spec((T, LANES)),
                   _acc_spec((T, LANES)), _acc_spec((1, BW)), _acc_spec((1, D)), _acc_spec((4, LANES))],
        out_shape=(S((T, D), F32), S((T, 2 * BW), BF), S((D, T), BF), S((BW, T), BF), S((T, LANES), F32),
                   S((T, LANES), F32), S((1, BW), F32), S((1, D), F32), S((4, LANES), F32)),
        scratch_shapes=[pltpu.VMEM((TM, BW), BF), pltpu.VMEM((TM, BW), F32), pltpu.VMEM((4, 4 * CHUNK), F32)],
        compiler_params=_params(("arbitrary",)),
    )(dh2, h1, q, g2, o, k4, k4, v4, v4, kt, kt, sinks, wb_out, wb_in, g_b, rc, rs1, rs2)


def _a_bwd(dh1p, dk, dv, h1, g_kv, w_kv, wa_out, ws, ln_g, ln_b, z, sv, vhat, rstd, rc, rs1, rs2, ready):
    T, D = h1.shape
    AW = wa_out.shape[0]
    G = ws.shape[0]
    TM = min(256, T)
    nT = T // TM
    nC = TM // CHUNK
    nr = len(ready)

    def body(dh1p_ref, dk_ref, dv_ref, h1_ref, gkv_ref, wkv_ref, waout_ref, ws_ref, lng_ref,
             lnb_ref, u_ref, gt_ref, sv_ref, vhat_ref, rstd_ref, rc_ref, rs1_ref, rs2_ref, *rest):
        ready_refs, rest = rest[:nr], rest[nr:]
        (dz_ref, gwo_ref, gwk_ref, dh1f_ref, dgkv_ref, dbkv_ref, dlng_ref, dlnb_ref,
         dws_ref, dbs_ref), rest = rest[:10], rest[10:]
        recv_refs, (dsv_scr, dvln_scr, acco_scr, acck_scr, ssem, rsem, lsem) = rest[:nr], rest[nr:]
        i = pl.program_id(0)
        exchanges = [_Direct(ready_refs[k], recv_refs[k], ssem.at[k], rsem.at[k], lsem.at[k], scatter=True)
                     for k in range(nr)]

        @pl.when(i == 0)
        def _():
            for e in exchanges:
                e.start()
            for r in (dgkv_ref, dbkv_ref, dlng_ref, dlnb_ref, dws_ref, dbs_ref, acco_scr, acck_scr):
                r[...] = jnp.zeros_like(r)

        dk_pre = _rot_bwd(dk_ref[...], rc_ref[...], rs1_ref[...], rs2_ref[...])
        dkv = jnp.concatenate([dk_pre, dv_ref[...]], axis=1)
        dbkv_ref[...] += jnp.sum(dkv, axis=0, keepdims=True)
        dkv_b = dkv.astype(BF)
        h1v = h1_ref[...]
        rkv = lax.rsqrt(jnp.mean(h1v * h1v, axis=-1, keepdims=True) + EPS)
        xh_kv = h1v * rkv
        gkv = gkv_ref[...]
        acck_scr[...] += _dot((xh_kv * gkv).astype(BF).T, dkv_b)
        dnkv = _dot_nt(dkv_b, wkv_ref[...])
        dgkv_ref[...] += jnp.sum(dnkv * xh_kv, axis=0, keepdims=True)
        dh1 = dh1p_ref[...] + _rms_bwd(dnkv, xh_kv, rkv, gkv)
        dh1_b = dh1.astype(BF)
        dh1f_ref[...] = dh1
        dy = _dot_nt(dh1_b, waout_ref[...]).astype(BF)
        silu, dsilu = _silu_parts(gt_ref[...].astype(F32))
        silu, dsilu = silu.astype(BF), dsilu.astype(BF)
        ub, svb = u_ref[...], sv_ref[...]
        us = ub * silu
        dys = dy * svb
        acco_scr[...] += _dot((us * svb).T, dh1_b)
        dz_ref[:, :AW] = dys * silu
        dz_ref[:, 2 * AW:] = dys * ub * dsilu
        dsv_scr[...] = dy * us
        vhat_v = vhat_ref[...].astype(F32)
        lng = lng_ref[...]
        vln_b = (vhat_v * lng + lnb_ref[...]).astype(BF)
        tri = lax.broadcasted_iota(jnp.int32, (CHUNK, CHUNK), 0) >= lax.broadcasted_iota(jnp.int32, (CHUNK, CHUNK), 1)
        lane = lax.broadcasted_iota(jnp.int32, (CHUNK, LANES), 1)
        dbs = jnp.zeros((CHUNK, LANES), F32)
        for g in range(G):
            wsm = jnp.where(tri, ws_ref[g], 0.0).astype(BF)
            cols = slice(g * CHUNK, (g + 1) * CHUNK)
            dws_g = None
            for c in range(nC):
                rows = slice(c * CHUNK, (c + 1) * CHUNK)
                dsv_cg = dsv_scr[rows, cols]
                dvln_scr[rows, cols] = _dot_tn(wsm, dsv_cg)
                part = _dot_nt(dsv_cg, vln_b[rows, cols])
                dws_g = part if dws_g is None else dws_g + part
                dbs += jnp.where(lane == g, jnp.sum(dsv_cg.astype(F32), axis=-1, keepdims=True), 0.0)
            dws_ref[g] += jnp.where(tri, dws_g, 0.0)
        dbs_ref[...] += dbs
        dvln = dvln_scr[...]
        dlng_ref[...] += jnp.sum(dvln * vhat_v, axis=0, keepdims=True)
        dlnb_ref[...] += jnp.sum(dvln, axis=0, keepdims=True)
        a = dvln * lng
        dvv = rstd_ref[:, 0:1] * (a - jnp.mean(a, axis=-1, keepdims=True)
                                  - vhat_v * jnp.mean(a * vhat_v, axis=-1, keepdims=True))
        dz_ref[:, AW:2 * AW] = dvv.astype(BF)

        @pl.when(i == nT - 1)
        def _():
            for j in range(N_DEV):
                gwo_ref[j] = acco_scr[j * (AW // N_DEV):(j + 1) * (AW // N_DEV)].astype(BF)
                gwk_ref[j] = acck_scr[j * (D // N_DEV):(j + 1) * (D // N_DEV)].astype(BF)
            for e in exchanges:
                e.finish()

    row = functools.partial(_row_spec, TM)
    hbm = pl.BlockSpec(memory_space=pl.ANY)
    S = jax.ShapeDtypeStruct
    gwo_shape, gwk_shape = (N_DEV, AW // N_DEV, D), (N_DEV, D // N_DEV, 2 * LANES)
    return pl.pallas_call(
        body, name="a_bwd", grid=(nT,),
        in_specs=[row(D), row(LANES), row(LANES), row(D), _const_spec((1, D)), _const_spec(w_kv.shape),
                  _const_spec(wa_out.shape), _const_spec(ws.shape),
                  _const_spec((1, AW)), _const_spec((1, AW)), pl.BlockSpec((TM, AW), lambda i: (i, 0)),
                  pl.BlockSpec((TM, AW), lambda i: (i, 2)), row(AW), row(AW), row(LANES),
                  row(LANES), row(LANES), row(LANES)] + [hbm] * nr,
        out_specs=[row(3 * AW), _const_spec(gwo_shape), _const_spec(gwk_shape), row(D),
                   _acc_spec((1, D)), _acc_spec((1, 2 * LANES)), _acc_spec((1, AW)),
                   _acc_spec((1, AW)), _acc_spec(ws.shape), _acc_spec((CHUNK, LANES))] + [hbm] * nr,
        out_shape=(S((T, 3 * AW), BF), S(gwo_shape, BF), S(gwk_shape, BF), S((T, D), F32),
                   S((1, D), F32), S((1, 2 * LANES), F32), S((1, AW), F32), S((1, AW), F32),
                   S(ws.shape, F32), S((CHUNK, LANES), F32)) + tuple(S(r.shape, r.dtype) for r in ready),
        scratch_shapes=[pltpu.VMEM((TM, AW), BF), pltpu.VMEM((TM, AW), F32), pltpu.VMEM((AW, D), F32),
                        pltpu.VMEM((D, 2 * LANES), F32)] + _direct_sems(nr),
        compiler_params=_params(("arbitrary",)),
    )(dh1p, dk, dv, h1, g_kv, w_kv, wa_out, ws, ln_g, ln_b, z, z, sv, vhat, rstd, rc, rs1, rs2, *ready)


def _a_in_bwd(dz, wa_in_t, x, dh1, g_a, ready):
    T, D = x.shape
    TM = min(512, T)
    nT = T // TM
    nr = len(ready)

    def body(dz_ref, wain_ref, x_ref, dh1_ref, ga_ref, *rest):
        ready_refs, (dx_ref, n1_ref, dga_ref), rest = rest[:nr], rest[nr:nr + 3], rest[nr + 3:]
        recv_refs, (ssem, rsem, lsem) = rest[:nr], rest[nr:]
        i = pl.program_id(0)
        exchanges = [_Direct(ready_refs[k], recv_refs[k], ssem.at[k], rsem.at[k], lsem.at[k], scatter=True)
                     for k in range(nr)]

        @pl.when(i == 0)
        def _():
            for e in exchanges:
                e.start()
            dga_ref[...] = jnp.zeros_like(dga_ref)

        xv = x_ref[...]
        r1 = lax.rsqrt(jnp.mean(xv * xv, axis=-1, keepdims=True) + EPS)
        xh = xv * r1
        ga = ga_ref[...]
        n1_ref[...] = (xh * ga).astype(BF).T
        dn1 = _dot(dz_ref[...], wain_ref[...])
        dga_ref[...] += jnp.sum(dn1 * xh, axis=0, keepdims=True)
        dx_ref[...] = dh1_ref[...] + _rms_bwd(dn1, xh, r1, ga)

        @pl.when(i == nT - 1)
        def _():
            for e in exchanges:
                e.finish()

    row = functools.partial(_row_spec, TM)
    hbm = pl.BlockSpec(memory_space=pl.ANY)
    S = jax.ShapeDtypeStruct
    return pl.pallas_call(
        body, name="a_in_bwd", grid=(nT,),
        in_specs=[row(dz.shape[1]), _const_spec(wa_in_t.shape), row(D), row(D), _const_spec((1, D))] + [hbm] * nr,
        out_specs=[row(D), _col_spec(TM, D), _acc_spec((1, D))] + [hbm] * nr,
        out_shape=(S((T, D), F32), S((D, T), BF), S((1, D), F32)) + tuple(S(r.shape, r.dtype) for r in ready),
        scratch_shapes=_direct_sems(nr),
        compiler_params=_params(("arbitrary",)),
    )(dz, wa_in_t, x, dh1, g_a, *ready)


def _wgrad(problems, name, bt=512):
    T = problems[0][0].shape[1]
    BT = min(bt, T)
    nt = T // BT
    n = len(problems)
    dims = [(at.shape[0], b.shape[1] // nblk, nblk) for at, b, nblk in problems]

    def body(*refs):
        ins, outs, accs = refs[:2 * n], refs[2 * n:3 * n], refs[3 * n:]
        t = pl.program_id(0)

        @pl.when(t == 0)
        def _():
            for acc in accs:
                acc[...] = jnp.zeros_like(acc)

        for k in range(n):
            accs[k][...] += _dot(ins[2 * k][...], ins[2 * k + 1][...])

        @pl.when(t == nt - 1)
        def _():
            for k, (_, N, nblk) in enumerate(dims):
                for j in range(nblk):
                    outs[k][j] = accs[k][:, j * N:(j + 1) * N].astype(BF)

    in_specs = []
    for at, b, _ in problems:
        in_specs += [pl.BlockSpec((at.shape[0], BT), lambda t: (0, t)), pl.BlockSpec((BT, b.shape[1]), lambda t: (t, 0))]
    return pl.pallas_call(
        body, name=name, grid=(nt,), in_specs=in_specs,
        out_specs=[pl.BlockSpec((nblk, K, N), lambda t: (0, 0, 0)) for K, N, nblk in dims],
        out_shape=[jax.ShapeDtypeStruct((nblk, K, N), BF) for K, N, nblk in dims],
        scratch_shapes=[pltpu.VMEM((K, nblk * N), F32) for K, N, nblk in dims],
        compiler_params=_params(("arbitrary",)),
    )(*[operand for at, b, _ in problems for operand in (at, b)])


def _wgrad_exchange(a, b, me, small, name):
    K, T = a.shape
    N = b.shape[1] // N_DEV
    BT = T
    nt = T // BT
    last = N_DEV - 1
    n_chip = N_DEV // 2

    def far_of(k, core):
        return jnp.where((core == 0) & ((k == 1) | (k == 2)), k, n_chip - 1 - k)

    def block_of(s, me_i):
        k, odd = s // 2, s % 2
        core = me_i & 1
        return me_i ^ ((far_of(k, jnp.where(odd == 1, core, 1 - core)) << 1) | (1 - odd))

    H = K // 2

    def body(me_ref, a_ref, b_ref, small_ref, recv_ref, full_ref, *scratch):
        (acc, dstage, istage, half, relay, d_s, d_r, i_s, i_r, r_s, r_r, lsem, parts_scr, red_scr, e_s, e_r, e_l, g_s,
         g_r, g_l) = scratch
        s, t = pl.program_id(0), pl.program_id(1)
        x, y, c = (lax.axis_index(ax) for ax in AXES)
        ex = [_Direct(small_ref, parts_scr, e_s, e_r, e_l, scatter=True)]
        regather = _TwoLevel(red_scr, full_ref, g_s, g_r, g_l)

        def to_sibling(k, slot):
            return pltpu.make_async_remote_copy(src_ref=dstage.at[slot], dst_ref=half.at[k], send_sem=d_s.at[k],
                                                recv_sem=d_r.at[k], device_id=(x, y, 1 - c), device_id_type=MESH)

        def to_chip(k, slot):
            over_x = far_of(k, c) == 2
            px, py = jnp.where(over_x, 1 - x, x), jnp.where(over_x, y, 1 - y)
            return pltpu.make_async_remote_copy(src_ref=istage.at[slot], dst_ref=recv_ref.at[jnp.where(over_x, 1, 2)],
                                                send_sem=i_s.at[k], recv_sem=i_r.at[k], device_id=(px, py, c),
                                                device_id_type=MESH)

        def to_relay(j, slot):
            to = (1 - x, y, c) if j == 0 else (x, 1 - y, c)
            return pltpu.make_async_remote_copy(src_ref=istage.at[slot, pl.ds(j * H, H)], dst_ref=relay.at[j],
                                                send_sem=r_s.at[j], recv_sem=r_r.at[j], device_id=to,
                                                device_id_type=MESH)

        @pl.when((s == 0) & (t == 0))
        def _():
            for e in ex:
                e.start()

        acc[...] = _dot(a_ref[...], b_ref[...])

        @pl.when(t == nt - 1)
        def _():
            k = lax.div(s, 2)
            slot = lax.rem(k, 2)

            @pl.when(lax.rem(s, 2) == 0)
            def _():
                @pl.when(k >= 2)
                def _():
                    to_sibling(k - 2, slot).wait_send()

                dstage[slot] = acc[...].astype(BF)
                to_sibling(k, slot).start()

            @pl.when(lax.rem(s, 2) == 1)
            def _():
                to_sibling(k, slot).wait_recv()
                pair = acc[...] + half[k].astype(F32)

                @pl.when(k == 0)
                def _():
                    istage[slot] = pair.astype(BF)
                    for j in range(2):
                        to_relay(j, slot).start()

                @pl.when(k == 1)
                def _():
                    for j in range(2):
                        to_relay(j, slot).wait_recv()

                @pl.when(k == 2)
                def _():
                    for j in range(2):
                        to_relay(j, slot).wait_send()

                @pl.when(k == n_chip - 1)
                def _():
                    to_chip(1, slot).wait_send()
                    istage[slot] = pair.astype(BF)

                @pl.when((k == 1) | (k == 2))
                def _():
                    over_x = far_of(k, c) == 2
                    istage[slot, 0:H] = (pair[:H] + jnp.where(over_x, 0.0, relay[0].astype(F32))).astype(BF)
                    istage[slot, H:K] = (pair[H:] + jnp.where(over_x, relay[1].astype(F32), 0.0)).astype(BF)
                    to_chip(k, slot).start()

            @pl.when(s == last)
            def _():
                own = pltpu.make_async_copy(istage.at[slot], recv_ref.at[0], lsem)
                own.start()
                to_chip(2, 0).wait_send()
                to_sibling(n_chip - 2, 0).wait_send()
                to_sibling(n_chip - 1, 1).wait_send()
                for kk in (1, 2):
                    to_chip(kk, 0).wait_recv()
                own.wait()
                for e in ex:
                    e.finish()
                total = parts_scr[0]
                for dev in range(1, N_DEV):
                    total = total + parts_scr[dev]
                red_scr[...] = total
                regather.start()
                regather.forward()
                regather.finish()

    hbm = pl.BlockSpec(memory_space=pl.ANY)
    dma = pltpu.SemaphoreType.DMA
    grid_spec = pltpu.PrefetchScalarGridSpec(
        num_scalar_prefetch=1, grid=(N_DEV, nt),
        in_specs=[pl.BlockSpec((K, BT), lambda s, t, me_ref: (0, t), pipeline_mode=pl.Buffered(1)),
                  pl.BlockSpec((BT, N), lambda s, t, me_ref: (t, block_of(s, me_ref[0]))), hbm],
        out_specs=[hbm, hbm],
        scratch_shapes=[pltpu.VMEM((K, N), F32), pltpu.VMEM((2, K, N), BF), pltpu.VMEM((2, K, N), BF),
                        pltpu.VMEM((n_chip, K, N), BF), pltpu.VMEM((2, H, N), BF), dma((n_chip,)), dma((n_chip,)),
                        dma((n_chip - 1,)), dma((n_chip - 1,)), dma((2,)), dma((2,)), dma,
                        pltpu.VMEM(small.shape, F32), pltpu.VMEM(small.shape[1:], F32),
                        dma((last,)), dma((last,)), dma, dma((last,)), dma((last,)), dma])
    return pl.pallas_call(
        body, name=name, grid_spec=grid_spec,
        out_shape=[jax.ShapeDtypeStruct((n_chip - 1, K, N), BF), jax.ShapeDtypeStruct(small.shape, F32)],
        compiler_params=_params(("arbitrary", "arbitrary")),
    )(me, a, b, small)


def _my_index():
    return 4 * lax.axis_index("x") + 2 * lax.axis_index("y") + lax.axis_index("c")


def _peer(mask):
    x, y, c = (lax.axis_index(a) for a in AXES)
    return (x ^ ((mask >> 2) & 1), y ^ ((mask >> 1) & 1), c ^ (mask & 1))


def _dev_index(p):
    return 4 * p[0] + 2 * p[1] + p[2]


class _Direct:
    def __init__(self, src, dst, send_sems, recv_sems, local_sem, scatter):
        me = _my_index()
        self.own = pltpu.make_async_copy(src.at[me] if scatter else src, dst.at[me], local_sem)
        self.sends, self.recvs = [], []
        for k in range(1, N_DEV):
            p = _peer(k)
            pi = _dev_index(p)
            sems = dict(send_sem=send_sems.at[k - 1], recv_sem=recv_sems.at[k - 1], device_id=p, device_id_type=MESH)
            self.sends.append(pltpu.make_async_remote_copy(src_ref=src.at[pi] if scatter else src, dst_ref=dst.at[me],
                                                           **sems))
            self.recvs.append(pltpu.make_async_remote_copy(src_ref=src.at[me] if scatter else src, dst_ref=dst.at[pi],
                                                           **sems))

    def start(self):
        self.own.start()
        for cp in self.sends:
            cp.start()

    def finish(self):
        for cp in self.sends:
            cp.wait_send()
        for cp in self.recvs:
            cp.wait_recv()
        self.own.wait()


class _TwoLevel:
    def __init__(self, src, dst, send_sems, recv_sems, local_sem, own=True):
        x, y, c = (lax.axis_index(a) for a in AXES)
        self.me, self.sibling = (x, y, c), (x, y, 1 - c)
        self.chips = [(1 - x, y), (x, 1 - y), (1 - x, 1 - y)]
        self.src, self.dst, self.send_sems, self.recv_sems = src, dst, send_sems, recv_sems
        self.own = pltpu.make_async_copy(src, dst.at[_dev_index(self.me)], local_sem) if own else None

    def _copy(self, k, block, to, from_src=False):
        slot = self.dst.at[_dev_index(block)]
        return pltpu.make_async_remote_copy(src_ref=self.src if from_src else slot, dst_ref=slot,
                                            send_sem=self.send_sems.at[k], recv_sem=self.recv_sems.at[k],
                                            device_id=to, device_id_type=MESH)

    def _firsts(self):
        c = self.me[2]
        return [self._copy(0, self.me, self.sibling, True)] + [self._copy(1 + j, self.me, (*chip, c), True)
                                                               for j, chip in enumerate(self.chips)]

    def _passed(self):
        c = self.me[2]
        return [self._copy(4 + j, (*chip, c), self.sibling) for j, chip in enumerate(self.chips)]

    def start(self):
        if self.own is not None:
            self.own.start()
        for cp in self._firsts():
            cp.start()

    def wait_sibling(self):
        self._copy(0, self.sibling, self.me).wait_recv()

    def wait_chip_and_forward(self, j):
        self._copy(1 + j, (*self.chips[j], self.me[2]), self.me).wait_recv()
        self._passed()[j].start()

    def wait_passed(self, j):
        self._copy(4 + j, (*self.chips[j], 1 - self.me[2]), self.me).wait_recv()

    def wait_sends(self):
        for cp in self._firsts() + self._passed():
            cp.wait_send()
        if self.own is not None:
            self.own.wait()

    def forward(self):
        for j in range(3):
            self.wait_chip_and_forward(j)

    def finish(self):
        self.wait_sibling()
        for j in range(3):
            self.wait_passed(j)
        self.wait_sends()


class _RelayGather:
    def __init__(self, dst, send_sems, recv_sems):
        x, y, c = (lax.axis_index(a) for a in AXES)
        self.c = c
        self.sib, self.xn, self.yn, self.dg = (x, y, 1 - c), (1 - x, y, c), (x, 1 - y, c), (1 - x, 1 - y, c)
        self.me = (x, y, c)
        self.dst, self.send_sems, self.recv_sems = dst, send_sems, recv_sems
        self.half = dst.shape[1] // 2

    def _slot(self, dev, part=None):
        i = _dev_index(dev)
        if part is None:
            return self.dst.at[i]
        return self.dst.at[i, pl.ds(part * self.half, self.half)]

    def _copy(self, k, dev, to, part=None):
        ref = self._slot(dev, part)
        return pltpu.make_async_remote_copy(src_ref=ref, dst_ref=ref, send_sem=self.send_sems.at[k],
                                            recv_sem=self.recv_sems.at[k], device_id=to, device_id_type=MESH)

    def _other(self, dev):
        return (dev[0], dev[1], 1 - self.c)

    def start(self):
        for k, to in enumerate((self.sib, self.xn, self.yn)):
            self._copy(k, self.me, to).start()

    def send_own(self, k):
        return self._copy(k, self.me, (self.sib, self.xn, self.yn)[k])

    def wait_sibling(self):
        self._copy(0, self.sib, self.me).wait_recv()

    def on_x(self):
        self._copy(1, self.xn, self.me).wait_recv()
        self._copy(3, self.xn, self.yn, part=0).start()
        self._copy(5, self.xn, self.sib).start()

    def on_y(self):
        self._copy(2, self.yn, self.me).wait_recv()
        self._copy(4, self.yn, self.xn, part=1).start()
        self._copy(6, self.yn, self.sib).start()

    def on_diag(self):
        self._copy(3, self.dg, self.me, part=0).wait_recv()
        self._copy(4, self.dg, self.me, part=1).wait_recv()
        self._copy(7, self.dg, self.sib).start()

    def wait_passed(self, j):
        self._copy(5 + j, self._other((self.xn, self.yn, self.dg)[j]), self.me).wait_recv()

    def wait_sends(self):
        for k, to in enumerate((self.sib, self.xn, self.yn)):
            self._copy(k, self.me, to).wait_send()
        self._copy(3, self.xn, self.yn, part=0).wait_send()
        self._copy(4, self.yn, self.xn, part=1).wait_send()
        for j, dev in enumerate((self.xn, self.yn, self.dg)):
            self._copy(5 + j, dev, self.sib).wait_send()


def _direct_sems(n):
    if n == 0:
        return []
    return [pltpu.SemaphoreType.DMA((n, 7)), pltpu.SemaphoreType.DMA((n, 7)), pltpu.SemaphoreType.DMA((n,))]


def _adam_math(w, g, m, v):
    m = ADAM_B1 * m + (1.0 - ADAM_B1) * g
    v = ADAM_B2 * v + (1.0 - ADAM_B2) * (g * g)
    m_hat = m / (1.0 - ADAM_B1 ** ADAM_STEP)
    v_hat = v / (1.0 - ADAM_B2 ** ADAM_STEP)
    delta = -ADAM_LR * (m_hat / (jnp.sqrt(v_hat) + ADAM_EPS) + ADAM_WD * w)
    return delta, m, v


def _sum_adam(tensors, name):
    NB = 2
    n = len(tensors)

    def body(*refs):
        ins, outs = refs[:4 * n], refs[4 * n:]
        for k in range(n):
            p_ref, w_ref, m_ref, v_ref = ins[4 * k:4 * k + 4]
            g_ref, d_ref, nm_ref, nv_ref = outs[4 * k:4 * k + 4]
            g = p_ref[0].astype(F32)
            for i in range(1, p_ref.shape[0]):
                g = g + p_ref[i].astype(F32)
            g_ref[...] = g
            d_ref[...], nm_ref[...], nv_ref[...] = _adam_math(w_ref[...], g, m_ref[...], v_ref[...])

    in_specs, out_specs, out_shape, operands = [], [], [], []
    for parts, w, m, v in tensors:
        R, C = w.shape
        blk = pl.BlockSpec((R // NB, C), lambda i: (i, 0))
        in_specs += [pl.BlockSpec((parts.shape[0], R // NB, C), lambda i: (0, i, 0)), blk, blk, blk]
        out_specs += [blk] * 4
        out_shape += [jax.ShapeDtypeStruct((R, C), F32)] * 4
        operands += [parts, w, m, v]
    res = pl.pallas_call(
        body, name=name, grid=(NB,), in_specs=in_specs, out_specs=out_specs, out_shape=out_shape,
        compiler_params=_params(("arbitrary",)),
    )(*operands)
    return [tuple(res[4 * k:4 * k + 4]) for k in range(n)]


SUBLANES = 8


def _nrows(size):
    return -(-size // (SUBLANES * LANES)) * SUBLANES


def _view2d(a):
    return a.reshape(-1, LANES) if a.size % LANES == 0 else a.reshape(1, -1)


def _pack_small(parts, total_rows, name):
    arrs = [p[0] for p in parts]

    def body(*refs):
        out = refs[-1]
        out[...] = jnp.zeros_like(out)
        at = 0
        for ref, (a, rows, flag) in zip(refs[:-1], parts):
            val = ref[...].T if flag == "T" else ref[...]
            r, c = (rows, val.shape[1]) if flag == "T" else val.shape
            out[at:at + r, 0:c] = val[:r]
            at += _nrows(r * c)

    return pl.pallas_call(body, name=name, out_shape=jax.ShapeDtypeStruct((total_rows, LANES), F32))(*arrs)


def _small_update(full, me, reps, shards, name):
    n = len(reps) + len(shards)

    def body(me_ref, full_ref, *refs):
        ins, outs = refs[:3 * n], refs[3 * n:]
        at = 0
        for k in range(n):
            w_ref, m_ref, v_ref = ins[3 * k:3 * k + 3]
            r, c = w_ref.shape
            if k < len(reps):
                g = full_ref[at:at + r, 0:c]
                at += _nrows(r * c)
            else:
                seg = full_ref[at:at + N_DEV * r, :]
                row = lax.broadcasted_iota(jnp.int32, seg.shape, 0)
                pick = [jnp.sum(jnp.where(row == r * me_ref[0] + t, seg, 0.0), axis=0, keepdims=True) for t in range(r)]
                g = pick[0] if r == 1 else jnp.concatenate(pick, axis=0)
                at += N_DEV * r
            g_ref, d_ref, nm_ref, nv_ref = outs[4 * k:4 * k + 4]
            g_ref[...] = g
            d_ref[...], nm_ref[...], nv_ref[...] = _adam_math(w_ref[...], g, m_ref[...], v_ref[...])
        outs[4 * n][...] = full_ref[at:at + 1, 0:1]

    flat = [t for p in reps + shards for t in p]
    S = jax.ShapeDtypeStruct
    res = pl.pallas_call(
        body, name=name,
        in_specs=[pl.BlockSpec(memory_space=pltpu.SMEM)] + [pl.BlockSpec(memory_space=pltpu.VMEM)] * (1 + len(flat)),
        out_shape=[S(p[0].shape, F32) for p in reps + shards for _ in range(4)] + [S((1, 1), F32)],
    )(me, full, *flat)
    return [tuple(res[4 * k:4 * k + 4]) for k in range(n)], res[4 * n]


def _rope_tables(T):
    pos = np.arange(T, dtype=np.float32)
    inv_freq = (np.float64(ROPE_THETA) ** (-np.arange(0, HEAD_DIM, 2, dtype=np.float64) / HEAD_DIM)).astype(np.float32)
    ang = (pos[:, None] * inv_freq[None, :]).astype(np.float64)
    cos, sin, zero = np.cos(ang).astype(np.float32), np.sin(ang).astype(np.float32), np.zeros(ang.shape, np.float32)
    c = np.concatenate([cos, cos, cos, cos], axis=1)
    s1 = np.concatenate([-sin, zero, -sin, zero], axis=1)
    s2 = np.concatenate([zero, sin, zero, sin], axis=1)
    return jnp.asarray(c), jnp.asarray(s1), jnp.asarray(s2)


def kernel(x, a_norm_g, a_w_in, a_ln_g, a_ln_b, a_ws, a_bs, a_w_out, kv_norm_g, w_kv, b_kv, b_norm_g, b_w_in, b_bq, b_sinks, b_w_out, final_norm_g, loss_target, m_a_norm_g, m_a_w_in, m_a_ln_g, m_a_ln_b, m_a_ws, m_a_bs, m_a_w_out, m_kv_norm_g, m_w_kv, m_b_kv, m_b_norm_g, m_b_w_in, m_b_bq, m_b_sinks, m_b_w_out, m_final_norm_g, v_a_norm_g, v_a_w_in, v_a_ln_g, v_a_ln_b, v_a_ws, v_a_bs, v_a_w_out, v_kv_norm_g, v_w_kv, v_b_kv, v_b_norm_g, v_b_w_in, v_b_bq, v_b_sinks, v_b_w_out, v_final_norm_g):
    T, D = x.shape[1], x.shape[2]
    AW = a_ln_g.shape[1] * N_DEV
    G = a_ws.shape[1]
    assert w_kv.shape[1] == 2 * LANES and a_ws.shape[2] == CHUNK and T % CHUNK == 0
    me = _my_index()

    xs, tgt = x[0], loss_target[0]
    z, wa_in_t, g_a, ln_g, ln_b, wa_out, wkv = _in_proj(xs, a_w_in[0], [a_norm_g, a_ln_g, a_ln_b], me.reshape(1),
                                                        [a_w_out[0], w_kv])
    wa_in_t = wa_in_t.reshape(-1, D)
    wa_out = wa_out.reshape(AW, D)
    wkv = wkv.reshape(D, 2 * LANES)

    rc, rs1, rs2 = _rope_tables(T)
    ws = a_ws[0]
    g_kv = kv_norm_g.reshape(1, D)
    bkv = b_kv.reshape(1, -1)
    g_f = final_norm_g.reshape(1, D)
    sinks = b_sinks.reshape(1, 16)
    h1, sv, vhat, rstd, k4, v4, kt, vt, wb_in, wb_out = _a_fwd(
        xs, z, ln_g, ln_b, ws, a_bs[0], wa_out, g_kv, wkv, bkv, rc, rs1, rs2, [b_w_in[0], b_w_out[0]])
    wb_out = wb_out.reshape(-1, D)
    q, g2, o, dh2, dh2_b, loss, d_gf = _b_fwd(h1, b_norm_g, wb_in, b_bq, rc, rs1, rs2, k4, vt, sinks, wb_out, g_f, tgt)
    dh1p, dz2, n2, y2, dk, dv, d_bq, d_gb, d_sink = _b_bwd(dh2, h1, q, g2, o, k4, v4, kt, sinks, wb_out, wb_in,
                                                           b_norm_g, rc, rs1, rs2)
    d_sink = d_sink[:, :4].reshape(2, 2, 4).transpose(0, 2, 1).reshape(1, 16)
    gw_b_in, gw_b_out = _wgrad([(n2, dz2, N_DEV), (y2, dh2_b, 1)], "wgrad_b", bt=1024)
    gw_b_out = gw_b_out.reshape(N_DEV, -1, D)
    (dz, gw_a_out, gw_kv, dh1_f, d_gkv, d_bkv, d_lng, d_lnb, d_ws, d_bst, r_b_in, r_b_out) = _a_bwd(
        dh1p, dk, dv, h1, g_kv, wkv, wa_out, ws, ln_g, ln_b, z, sv, vhat, rstd, rc, rs1, rs2, [gw_b_in, gw_b_out])
    dx, n1, d_ga, r_a_out, r_kv = _a_in_bwd(dz, wa_in_t, xs, dh1_f, g_a, [gw_a_out, gw_kv])
    small = [(_view2d(d_ws), None, None), (d_bst, G, "T")] + [(_view2d(a), None, None) for a in (
        d_gkv, d_bkv, d_gb, d_bq, d_sink, d_gf, d_ga, d_lng, d_lnb, loss)]
    used = sum(_nrows(G * CHUNK if flag else a.size) for a, _, flag in small)
    per = -(-used // (SUBLANES * N_DEV)) * SUBLANES
    small_pack = _pack_small(small, per * N_DEV, "pack_small").reshape(N_DEV, per, LANES)
    r_a_in, full_small = _wgrad_exchange(n1, dz, me.reshape(1), small_pack, "wgrad_a_in")

    ((g_a_out, d_a_out, nm_a_out, nv_a_out), (g_kvw, d_kvw, nm_kvw, nv_kvw), (g_b_in, d_b_in, nm_b_in, nv_b_in),
     (g_b_out, d_b_out, nm_b_out, nv_b_out)) = _sum_adam(
        [(r_a_out, a_w_out[0], m_a_w_out[0], v_a_w_out[0]), (r_kv, w_kv, m_w_kv, v_w_kv),
         (r_b_in, b_w_in[0], m_b_w_in[0], v_b_w_in[0]), (r_b_out, b_w_out[0], m_b_w_out[0], v_b_w_out[0])], "adam_rest")
    (g_a_in, d_a_in, nm_a_in, nv_a_in), = _sum_adam([(r_a_in, a_w_in[0], m_a_w_in[0], v_a_w_in[0])], "adam_a_in")

    full_small = full_small.reshape(N_DEV * per, LANES)
    reps = [(a_ws, m_a_ws, v_a_ws), (a_bs, m_a_bs, v_a_bs), (kv_norm_g, m_kv_norm_g, v_kv_norm_g),
            (b_kv, m_b_kv, v_b_kv), (b_norm_g, m_b_norm_g, v_b_norm_g), (b_bq, m_b_bq, v_b_bq),
            (b_sinks, m_b_sinks, v_b_sinks), (final_norm_g, m_final_norm_g, v_final_norm_g)]
    shards = [(a_norm_g, m_a_norm_g, v_a_norm_g), (a_ln_g, m_a_ln_g, v_a_ln_g), (a_ln_b, m_a_ln_b, v_a_ln_b)]
    upd, loss = _small_update(full_small, me.reshape(1), [tuple(_view2d(t) for t in p) for p in reps],
                              [tuple(_view2d(t) for t in p) for p in shards], "adam_small")
    loss = loss[0, 0]
    sm_g, sd, snm, snv = ([upd[k][j].reshape(p[0].shape) for k, p in enumerate(reps + shards)] for j in range(4))

    def order(big, sm):
        a_in, a_out, kvw, b_in, b_out = big
        ws_, bs_, kvg, bkv_, bng, bq_, snk, fng, ang, alng, alnb = sm
        return (ang, a_in[None], alng, alnb, ws_, bs_, a_out[None], kvg, kvw, bkv_, bng, b_in[None], bq_, snk,
                b_out[None], fng)

    grads = order((g_a_in, g_a_out, g_kvw, g_b_in, g_b_out), sm_g)
    deltas = order((d_a_in, d_a_out, d_kvw, d_b_in, d_b_out), sd)
    new_m = order((nm_a_in, nm_a_out, nm_kvw, nm_b_in, nm_b_out), snm)
    new_v = order((nv_a_in, nv_a_out, nv_kvw, nv_b_in, nv_b_out), snv)
    return (loss, dx[None], *grads, *deltas, *new_m, *new_v)
```

```python
import functools

import jax
import jax.numpy as jnp
import numpy as np
from jax import lax
from jax.experimental import pallas as pl
from jax.experimental.pallas import tpu as pltpu

CHUNK = 128
HEAD_DIM = 64
ROPE_THETA = 10000.0
EPS = 1e-5
ADAM_LR = 0.001
ADAM_B1 = 0.9
ADAM_B2 = 0.999
ADAM_EPS = 1e-08
ADAM_WD = 0.01
ADAM_STEP = 10
N_DEV = 8
LANES = 128
NEG = -1e30

BF = jnp.bfloat16
F32 = jnp.float32
MESH = pl.DeviceIdType.MESH
AXES = ("x", "y", "c")
VMEM_LIMIT = 56 * 1024 * 1024


def _dot(a, b):
    return jnp.dot(a, b, preferred_element_type=F32)


def _dot_nt(a, b):
    return lax.dot_general(a, b, (((1,), (1,)), ((), ())), preferred_element_type=F32)


def _dot_tn(a, b):
    return lax.dot_general(a, b, (((0,), (0,)), ((), ())), preferred_element_type=F32)


def _const_spec(shape):
    nd = len(shape)
    return pl.BlockSpec(shape, lambda *_: (0,) * nd, pipeline_mode=pl.Buffered(1))


def _acc_spec(shape):
    nd = len(shape)
    return pl.BlockSpec(shape, lambda *_: (0,) * nd)


def _row_spec(tm, width):
    return pl.BlockSpec((tm, width), lambda i: (i, 0))


def _col_spec(tm, height):
    return pl.BlockSpec((height, tm), lambda i: (0, i))


def _params(sem):
    return pltpu.CompilerParams(dimension_semantics=sem, vmem_limit_bytes=VMEM_LIMIT)


def _rot(x, c, s1, s2):
    return x * c + pltpu.roll(x, 96, 1) * s1 + pltpu.roll(x, 32, 1) * s2


def _rot_bwd(d, c, s1, s2):
    return d * c + pltpu.roll(d * s1, 32, 1) + pltpu.roll(d * s2, 96, 1)


def _silu_parts(g):
    sg = jax.nn.sigmoid(g)
    return g * sg, sg * (1.0 + g * (1.0 - sg))


def _rms_bwd(dn, xh, r, g):
    a = dn * g
    return r * (a - xh * jnp.mean(a * xh, axis=-1, keepdims=True))


def _lane_lo(shape):
    return lax.broadcasted_iota(jnp.int32, shape, 1) < HEAD_DIM


def _split4(t):
    lo = _lane_lo(t.shape)
    tr = pltpu.roll(t, HEAD_DIM, 1)
    z = jnp.zeros_like(t)
    return jnp.concatenate([jnp.where(lo, t, z), jnp.where(lo, z, tr), jnp.where(lo, tr, z), jnp.where(lo, z, t)], axis=1)


def _stack_pairs(t, h):
    return jnp.concatenate([t[:, (h * 4 + j) * LANES:(h * 4 + j + 1) * LANES] for j in range(4)], axis=0)


def _upper():
    shape = (CHUNK, 4 * CHUNK)
    return lax.broadcasted_iota(jnp.int32, shape, 0) > (lax.broadcasted_iota(jnp.int32, shape, 1) & (CHUNK - 1))


def _band_rows(tile_ref, before_ref, c, h):
    a = slice(2 * h * LANES, (2 * h + 1) * LANES)
    b = slice((2 * h + 1) * LANES, (2 * h + 2) * LANES)
    cur = slice(c * CHUNK, (c + 1) * CHUNK)

    def prev(cols):
        return before_ref[:, cols] if c == 0 else tile_ref[(c - 1) * CHUNK:c * CHUNK, cols]

    return jnp.concatenate([prev(a), tile_ref[cur, a], prev(b), tile_ref[cur, b]], axis=0)


def _band_cols(tile_ref, before_ref, c, h):
    a = slice(2 * h * LANES, (2 * h + 1) * LANES)
    b = slice((2 * h + 1) * LANES, (2 * h + 2) * LANES)

    def prev(rows):
        return before_ref[0, rows, :] if c == 0 else tile_ref[c - 1, rows, :]

    return jnp.concatenate([prev(a), tile_ref[c, a, :], prev(b), tile_ref[c, b, :]], axis=1)


def _band_specs(tm):
    nc = tm // CHUNK

    def before(i):
        return jnp.maximum(i * nc - 1, 0)

    return (pl.BlockSpec((tm, 4 * LANES), lambda i: (i, 0)),
            pl.BlockSpec((CHUNK, 4 * LANES), lambda i: (before(i), 0)),
            pl.BlockSpec((nc, 4 * LANES, CHUNK), lambda i: (i, 0, 0)),
            pl.BlockSpec((1, 4 * LANES, CHUNK), lambda i: (before(i), 0, 0)))


def _fold(t, upper, has_prev=None):
    out = []
    for k in range(2):
        prev = t[2 * k * CHUNK:(2 * k + 1) * CHUNK]
        if has_prev is not None:
            prev = jnp.where(has_prev, prev, NEG)
        out.append(jnp.where(upper, prev, t[(2 * k + 1) * CHUNK:(2 * k + 2) * CHUNK]))
    return out


def _unfold(fa, fb, upper):
    z = jnp.zeros_like(fa)
    return jnp.concatenate([jnp.where(upper, fa, z), jnp.where(upper, z, fa),
                            jnp.where(upper, fb, z), jnp.where(upper, z, fb)], axis=0)


def _sink_tile(s_ref):
    shape = (4, 4 * LANES)
    row = lax.broadcasted_iota(jnp.int32, shape, 0)
    pair = lax.broadcasted_iota(jnp.int32, shape, 1) // LANES
    idx = (row // 2) * 8 + pair * 2 + row % 2
    tile = jnp.zeros(shape, F32)
    for n in range(16):
        tile = jnp.where(idx == n, s_ref[0, n], tile)
    return tile


def _softmax_sink(f, sink):
    m = jnp.maximum(jnp.max(f, axis=0, keepdims=True), sink)
    p = jnp.exp(f - m)
    es = jnp.exp(sink - m)
    inv = 1.0 / (jnp.sum(p, axis=0, keepdims=True) + es)
    return p * inv, es * inv


class _Riding:
    def __init__(self, shards, gathered, stages, sems, n_steps):
        self.shards, self.stages, self.n_steps = shards, stages, n_steps
        ssem, rsem, lsem = sems
        self.gathers = [_TwoLevel(stages[k], gathered[k], ssem.at[k], rsem.at[k], lsem.at[k])
                        for k in range(len(shards))]

    def begin(self, i):
        @pl.when(i == 0)
        def _():
            for shard, stage, g in zip(self.shards, self.stages, self.gathers):
                stage[...] = shard[...].astype(stage.dtype)
                g.start()

    def end(self, i):
        @pl.when(i == self.n_steps // 2)
        def _():
            for g in self.gathers:
                g.forward()

        @pl.when(i == self.n_steps - 1)
        def _():
            for g in self.gathers:
                g.finish()

    @staticmethod
    def specs(later):
        nl = len(later)
        hbm = pl.BlockSpec(memory_space=pl.ANY)
        return ([_const_spec(w.shape) for w in later], [hbm] * nl,
                tuple(jax.ShapeDtypeStruct((N_DEV,) + w.shape, BF) for w in later),
                [pltpu.VMEM(w.shape, BF) for w in later] + _direct_sems(nl))


PASS_MASKS = ((0, 1, 2, 5, 4, 3, 6, 7), (0, 1, 4, 3, 2, 5, 6, 7))


def _in_proj(x, w_shard, vec_shards, me, later):
    T, D = x.shape
    SH = w_shard.shape[1]
    TM = min(1024, T)
    nT = T // TM
    nl = len(later)
    nv = len(vec_shards)
    widths = [v.shape[1] for v in vec_shards]
    offsets = [sum(widths[:k]) for k in range(nv)]
    vec_shape = (SUBLANES, sum(widths))
    ds = widths[0]
    last = N_DEV - 1
    masks = jnp.asarray(np.array(PASS_MASKS, np.int32).reshape(-1))

    def slot(p, me_ref, masks_ref):
        return me_ref[0] ^ masks_ref[(me_ref[0] & 1) * N_DEV + p]

    def body(me_ref, masks_ref, x_ref, wsh_ref, *rest):
        vsh_refs, rest = rest[:nv], rest[nv:]
        shards, rest = rest[:nl], rest[nl:]
        (z_ref, wt_ref), rest = rest[:2], rest[2:]
        vout_refs, rest = rest[:nv], rest[nv:]
        gathered, rest = rest[:nl], rest[nl:]
        (w_scr, vec_scr, vstage, n1_scr, ga_scr, w_s, w_r, v_s, v_r, v_l), rest = rest[:10], rest[10:]
        stages, sems = rest[:nl], rest[nl:]
        p, i = pl.program_id(0), pl.program_id(1)
        me = _my_index()
        wg = _RelayGather(w_scr, w_s, w_r)
        vg = _Direct(vstage, vec_scr, v_s, v_r, v_l, scatter=False)
        lg = [_TwoLevel(stages[k], gathered[k], sems[0].at[k], sems[1].at[k], sems[2].at[k]) for k in range(nl)]

        def at_pass(k):
            return (p == k) & (i == 0)

        c = lax.axis_index("c")

        @pl.when(at_pass(0))
        def _():
            for ref, off, wd in zip(vsh_refs, offsets, widths):
                vstage[:, off:off + wd] = jnp.broadcast_to(ref[...], (SUBLANES, wd))
            vg.start()
            w_scr[me] = wsh_ref[...].astype(BF)
            wg.send_own(0).start()

            @pl.when(c == 1)
            def _():
                wg.send_own(1).start()

            @pl.when(c == 0)
            def _():
                wg.send_own(2).start()

            vg.finish()
            for j in range(N_DEV):
                ga_scr[:, j * ds:(j + 1) * ds] = vec_scr[j, 0:1, 0:ds]
                for ref, off, wd in zip(vout_refs, offsets, widths):
                    ref[:, j * wd:(j + 1) * wd] = vec_scr[j, 0:1, off:off + wd]

        @pl.when(at_pass(1))
        def _():
            wg.wait_sibling()

        for first, second, landed_first, landed_second in ((1, 2, wg.on_x, wg.on_y), (2, 1, wg.on_y, wg.on_x)):
            mine = c == (1 if first == 1 else 0)

            @pl.when(at_pass(2) & mine)
            def _(second=second, landed_first=landed_first):
                wg.send_own(second).start()
                landed_first()

            @pl.when(at_pass(3) & mine)
            def _(second=second):
                wg.wait_passed(second - 1)

            @pl.when(at_pass(4) & mine)
            def _(landed_second=landed_second):
                landed_second()

            @pl.when(at_pass(5) & mine)
            def _(first=first):
                wg.wait_passed(first - 1)

        @pl.when(at_pass(4))
        def _():
            for k in range(nl):
                stages[k][...] = shards[k][...].astype(BF)
                lg[k].start()

        @pl.when(at_pass(6))
        def _():
            wg.on_diag()

        @pl.when(at_pass(7))
        def _():
            wg.wait_passed(2)

        @pl.when(p == 0)
        def _():
            xv = x_ref[...]
            r1 = lax.rsqrt(jnp.mean(xv * xv, axis=-1, keepdims=True) + EPS)
            n1_scr[i] = (xv * r1 * ga_scr[...]).astype(BF)

        z_ref[...] = _dot(n1_scr[i], w_scr[slot(p, me_ref, masks_ref)]).astype(BF)

        @pl.when(i == 0)
        def _():
            wt_ref[0] = w_scr[slot(p, me_ref, masks_ref)].T

        @pl.when((p == last) & (i == nT - 1))
        def _():
            wg.wait_sends()
            for g in lg:
                g.forward()
            for g in lg:
                g.finish()

    hbm = pl.BlockSpec(memory_space=pl.ANY)
    dma = pltpu.SemaphoreType.DMA
    S = jax.ShapeDtypeStruct
    def whole(shape):
        return pl.BlockSpec(shape, lambda p, i, m, t: (0, 0))

    def once(shape):
        return pl.BlockSpec(shape, lambda p, i, m, t: (0, 0), pipeline_mode=pl.Buffered(1))

    grid_spec = pltpu.PrefetchScalarGridSpec(
        num_scalar_prefetch=2, grid=(N_DEV, nT),
        in_specs=[pl.BlockSpec((TM, D), lambda p, i, m, t: (jnp.where(p == 0, i, nT - 1), 0)), once(w_shard.shape)]
        + [once(v.shape) for v in vec_shards] + [once(w.shape) for w in later],
        out_specs=[pl.BlockSpec((TM, SH), lambda p, i, m, t: (i, slot(p, m, t))),
                   pl.BlockSpec((1, SH, D), lambda p, i, m, t: (slot(p, m, t), 0, 0))]
        + [whole((1, N_DEV * wd)) for wd in widths] + [hbm] * nl,
        scratch_shapes=[pltpu.VMEM((N_DEV, D, SH), BF), pltpu.VMEM((N_DEV,) + vec_shape, F32),
                        pltpu.VMEM(vec_shape, F32), pltpu.VMEM((nT, TM, D), BF), pltpu.VMEM((1, D), F32),
                        dma((8,)), dma((8,)), dma((7,)), dma((7,)), dma]
        + [pltpu.VMEM(w.shape, BF) for w in later] + _direct_sems(nl))
    return pl.pallas_call(
        body, name="a_in_proj", grid_spec=grid_spec,
        out_shape=(S((T, N_DEV * SH), BF), S((N_DEV, SH, D), BF)) + tuple(S((1, N_DEV * wd), F32) for wd in widths)
        + tuple(S((N_DEV,) + w.shape, BF) for w in later),
        compiler_params=_params(("arbitrary", "arbitrary")),
    )(me, masks, x, w_shard, *vec_shards, *later)


def _a_fwd(x, z, ln_g, ln_b, ws, bs, wa_out, g_kv, w_kv, b_kv, rc, rs1, rs2, later):
    T, D = x.shape
    AW = wa_out.shape[0]
    G = ws.shape[0]
    TM = min(512, T)
    nT = T // TM
    nC = TM // CHUNK
    nl = len(later)

    def body(x_ref, u_ref, v_ref, gt_ref, lng_ref, lnb_ref, ws_ref, bs_ref, waout_ref, gkv_ref, wkv_ref, bkv_ref,
             rc_ref, rs1_ref, rs2_ref, *rest):
        shards, rest = rest[:nl], rest[nl:]
        (h1_ref, sv_ref, vhat_ref, rstd_ref, k4_ref, v4_ref, kt_ref, vt_ref), rest = rest[:8], rest[8:]
        gathered, sv_scr, stages, sems = rest[:nl], rest[nl], rest[nl + 1:2 * nl + 1], rest[2 * nl + 1:]
        i = pl.program_id(0)
        riding = _Riding(shards, gathered, stages, sems, nT)
        riding.begin(i)
        xv = x_ref[...]
        u = u_ref[...].astype(F32)
        v = v_ref[...].astype(F32)
        gt = gt_ref[...].astype(F32)
        mu = jnp.mean(v, axis=-1, keepdims=True)
        xc = v - mu
        rstd = lax.rsqrt(jnp.mean(xc * xc, axis=-1, keepdims=True) + EPS)
        vhat = xc * rstd
        vln = (vhat * lng_ref[...] + lnb_ref[...]).astype(BF)
        tri = lax.broadcasted_iota(jnp.int32, (CHUNK, CHUNK), 0) >= lax.broadcasted_iota(jnp.int32, (CHUNK, CHUNK), 1)
        bst = jnp.concatenate([bs_ref[...], jnp.zeros((CHUNK - G, CHUNK), F32)], axis=0).T
        for g in range(G):
            wsm = jnp.where(tri, ws_ref[g], 0.0).astype(BF)
            bias = bst[:, g:g + 1]
            for c in range(nC):
                blk = vln[c * CHUNK:(c + 1) * CHUNK, g * CHUNK:(g + 1) * CHUNK]
                sv_scr[c * CHUNK:(c + 1) * CHUNK, g * CHUNK:(g + 1) * CHUNK] = _dot(wsm, blk) + bias
        sv = sv_scr[...]
        silu, _ = _silu_parts(gt)
        y = (u * sv * silu).astype(BF)
        h1 = xv + _dot(y, waout_ref[...])
        h1_ref[...] = h1
        sv_ref[...] = sv.astype(BF)
        vhat_ref[...] = vhat.astype(BF)
        rstd_ref[...] = jnp.broadcast_to(rstd, rstd_ref.shape)
        rkv = lax.rsqrt(jnp.mean(h1 * h1, axis=-1, keepdims=True) + EPS)
        nkv = (h1 * rkv * gkv_ref[...]).astype(BF)
        kv = _dot(nkv, wkv_ref[...]) + bkv_ref[...]
        k_rot = _rot(kv[:, :LANES], rc_ref[...], rs1_ref[...], rs2_ref[...])
        for src, ref, tref in ((k_rot, k4_ref, kt_ref), (kv[:, LANES:], v4_ref, vt_ref)):
            t4 = _split4(src)
            ref[...] = t4.astype(BF)
            for c in range(nC):
                for b in range(4):
                    blk = t4[c * CHUNK:(c + 1) * CHUNK, b * LANES:(b + 1) * LANES]
                    tref[c, b * LANES:(b + 1) * LANES, :] = blk.T.astype(BF)
        riding.end(i)

    row = functools.partial(_row_spec, TM)
    zcol = [pl.BlockSpec((TM, AW), functools.partial(lambda k, i: (i, k), k)) for k in range(3)]
    tr = pl.BlockSpec((nC, 4 * LANES, CHUNK), lambda i: (i, 0, 0))
    r_in, r_out, r_shape, r_scratch = _Riding.specs(later)
    S = jax.ShapeDtypeStruct
    return pl.pallas_call(
        body, name="a_fwd", grid=(nT,),
        in_specs=[row(D)] + zcol + [_const_spec((1, AW)), _const_spec((1, AW)),
                  _const_spec(ws.shape), _const_spec(bs.shape), _const_spec(wa_out.shape), _const_spec((1, D)),
                  _const_spec(w_kv.shape), _const_spec((1, 2 * LANES)), row(LANES), row(LANES), row(LANES)] + r_in,
        out_specs=[row(D), row(AW), row(AW), row(LANES), row(4 * LANES), row(4 * LANES), tr, tr] + r_out,
        out_shape=(S((T, D), F32), S((T, AW), BF), S((T, AW), BF), S((T, LANES), F32),
                   S((T, 4 * LANES), BF), S((T, 4 * LANES), BF),
                   S((T // CHUNK, 4 * LANES, CHUNK), BF), S((T // CHUNK, 4 * LANES, CHUNK), BF)) + r_shape,
        scratch_shapes=[pltpu.VMEM((TM, AW), F32)] + r_scratch,
        compiler_params=_params(("arbitrary",)),
    )(x, z, z, z, ln_g, ln_b, ws, bs, wa_out, g_kv, w_kv, b_kv, rc, rs1, rs2, *later)


def _b_fwd(h1, g_b, wb_in, bq, rc, rs1, rs2, k4, vt, sinks, wb_out, g_f, target):
    T, D = h1.shape
    BW = wb_out.shape[0]
    SH = wb_in.shape[2]
    TM = min(512, T)
    nC = TM // CHUNK
    nP = BW // LANES

    def body(h1_ref, gb_ref, wbin_ref, bq_ref, rc_ref, rs1_ref, rs2_ref, k4_ref, k4p_ref, vt_ref, vtp_ref, sinks_ref,
             wbout_ref, gf_ref, tgt_ref, q_ref, g2_ref, o_ref, dh2_ref, dh2b_ref, loss_ref, dgf_ref, z_scr, o_scr):
        i = pl.program_id(0)
        sink = _sink_tile(sinks_ref)

        @pl.when(i == 0)
        def _():
            loss_ref[...] = jnp.zeros_like(loss_ref)
            dgf_ref[...] = jnp.zeros_like(dgf_ref)

        h1v = h1_ref[...]
        r2 = lax.rsqrt(jnp.mean(h1v * h1v, axis=-1, keepdims=True) + EPS)
        n2 = (h1v * r2 * gb_ref[...]).astype(BF)
        for j in range(N_DEV):
            z_scr[:, j * SH:(j + 1) * SH] = _dot(n2, wbin_ref[j])
        c_t, s1_t, s2_t = rc_ref[...], rs1_ref[...], rs2_ref[...]
        for p in range(nP):
            cols = slice(p * LANES, (p + 1) * LANES)
            qp = _rot(z_scr[:, cols] + bq_ref[:, cols], c_t, s1_t, s2_t) * (HEAD_DIM ** -0.5)
            q_ref[:, cols] = qp.astype(BF)
        g2 = z_scr[:, BW:]
        g2_ref[...] = g2.astype(BF)
        upper = _upper()
        for c in range(nC):
            ci = i * nC + c
            rows = slice(c * CHUNK, (c + 1) * CHUNK)
            qc = q_ref[rows, :]
            for h in range(2):
                st = _dot_nt(_band_rows(k4_ref, k4p_ref, c, h), _stack_pairs(qc, h))
                fa, fb = _fold(st, upper, ci > 0)
                pa, _ = _softmax_sink(fa, sink[2 * h:2 * h + 1, :])
                pb, _ = _softmax_sink(fb, sink[2 * h + 1:2 * h + 2, :])
                ot = _dot(_band_cols(vt_ref, vtp_ref, c, h), _unfold(pa, pb, upper).astype(BF))
                for j in range(4):
                    o_scr[rows, (h * 4 + j) * LANES:(h * 4 + j + 1) * LANES] = ot[:, j * CHUNK:(j + 1) * CHUNK].T
        o = o_scr[...]
        o_ref[...] = o.astype(BF)
        silu, _ = _silu_parts(g2)
        h2 = h1v + _dot((o * silu).astype(BF), wbout_ref[...])
        rf = lax.rsqrt(jnp.mean(h2 * h2, axis=-1, keepdims=True) + EPS)
        xh = h2 * rf
        gf = gf_ref[...]
        err = xh * gf - tgt_ref[...]
        dyf = err * (1.0 / D)
        dh2 = _rms_bwd(dyf, xh, rf, gf)
        dh2_ref[...] = dh2
        dh2b_ref[...] = dh2.astype(BF)
        loss_ref[...] += 0.5 * jnp.sum(jnp.mean(err * err, axis=-1, keepdims=True), axis=0, keepdims=True)
        dgf_ref[...] += jnp.sum(dyf * xh, axis=0, keepdims=True)

    row = functools.partial(_row_spec, TM)
    rows_tile, rows_before, cols_tile, cols_before = _band_specs(TM)
    S = jax.ShapeDtypeStruct
    return pl.pallas_call(
        body, name="b_fwd", grid=(T // TM,),
        in_specs=[row(D), _const_spec((1, D)), _const_spec(wb_in.shape), _const_spec((1, BW)), row(LANES), row(LANES),
                  row(LANES), rows_tile, rows_before, cols_tile, cols_before, pl.BlockSpec(memory_space=pltpu.SMEM),
                  _const_spec(wb_out.shape), _const_spec((1, D)), row(D)],
        out_specs=[row(BW), row(BW), row(BW), row(D), row(D), _acc_spec((1, 1)), _acc_spec((1, D))],
        out_shape=(S((T, BW), BF), S((T, BW), BF), S((T, BW), BF), S((T, D), F32), S((T, D), BF), S((1, 1), F32),
                   S((1, D), F32)),
        scratch_shapes=[pltpu.VMEM((TM, 2 * BW), F32), pltpu.VMEM((TM, BW), F32)],
        compiler_params=_params(("arbitrary",)),
    )(h1, g_b, wb_in, bq, rc, rs1, rs2, k4, k4, vt, vt, sinks, wb_out, g_f, target)


def _b_bwd(dh2, h1, q, g2, o, k4, v4, kt, sinks, wb_out, wb_in, g_b, rc, rs1, rs2):
    T, D = h1.shape
    BW = wb_out.shape[0]
    SH = wb_in.shape[2]
    TM = min(512, T)
    nT = T // TM
    nC = TM // CHUNK
    nP = BW // LANES

    def body(dh2_ref, h1_ref, q_ref, g2_ref, o_ref, k4_ref, k4p_ref, v4_ref, v4p_ref, kt_ref, ktp_ref, sinks_ref,
             wbout_ref, wbin_ref, gb_ref, rc_ref, rs1_ref, rs2_ref,
             dh1_ref, dz2_ref, n2_ref, y2_ref, dk_ref, dv_ref, dbq_ref, dgb_ref, dsink_ref, do_scr, dq_scr, dsacc_scr):
        i = pl.program_id(0)
        sink = _sink_tile(sinks_ref)

        @pl.when(i == 0)
        def _():
            dk_ref[...] = jnp.zeros_like(dk_ref)
            dv_ref[...] = jnp.zeros_like(dv_ref)
            dbq_ref[...] = jnp.zeros_like(dbq_ref)
            dgb_ref[...] = jnp.zeros_like(dgb_ref)
            dsacc_scr[...] = jnp.zeros_like(dsacc_scr)

        dh2 = dh2_ref[...]
        dy2 = _dot_nt(dh2.astype(BF), wbout_ref[...])
        silu, dsilu = _silu_parts(g2_ref[...].astype(F32))
        do_scr[...] = (dy2 * silu).astype(BF)
        dy2, silu, dsilu = dy2.astype(BF), silu.astype(BF), dsilu.astype(BF)
        ob = o_ref[...]
        y2_ref[...] = (ob * silu).T
        dz2_ref[:, BW:] = dy2 * ob * dsilu
        upper = _upper()
        lo = _lane_lo((2 * CHUNK, LANES))
        for c in range(nC):
            ci = i * nC + c
            rows = slice(c * CHUNK, (c + 1) * CHUNK)
            pci = jnp.maximum(ci - 1, 0)
            prev = pl.multiple_of(pci * CHUNK, CHUNK)
            cur = pl.multiple_of(ci * CHUNK, CHUNK)
            qc = q_ref[rows, :]
            doc = do_scr[rows, :]
            dkb = jnp.zeros((2 * CHUNK, LANES), F32)
            dvb = jnp.zeros((2 * CHUNK, LANES), F32)
            for h in range(2):
                qs = _stack_pairs(qc, h)
                dos = _stack_pairs(doc, h)
                fa, fb = _fold(_dot_nt(_band_rows(k4_ref, k4p_ref, c, h), qs), upper, ci > 0)
                dfa, dfb = _fold(_dot_nt(_band_rows(v4_ref, v4p_ref, c, h), dos), upper)
                folded = []
                for k, (f, df) in enumerate(((fa, dfa), (fb, dfb))):
                    p, ps = _softmax_sink(f, sink[2 * h + k:2 * h + k + 1, :])
                    delta = jnp.sum(p * df, axis=0, keepdims=True)
                    dsacc_scr[2 * h + k:2 * h + k + 1, :] -= ps * delta
                    folded.append((p, p * (df - delta)))
                pt = _unfold(folded[0][0], folded[1][0], upper).astype(BF)
                dst = _unfold(folded[0][1], folded[1][1], upper).astype(BF)
                dqt = _dot(_band_cols(kt_ref, ktp_ref, c, h), dst)
                for j in range(4):
                    dq_scr[rows, (h * 4 + j) * LANES:(h * 4 + j + 1) * LANES] = dqt[:, j * CHUNK:(j + 1) * CHUNK].T
                for acc_name, g in (("k", _dot(dst, qs)), ("v", _dot(pt, dos))):
                    a, b = g[:2 * CHUNK], g[2 * CHUNK:]
                    if h == 0:
                        part = jnp.where(lo, a + pltpu.roll(b, HEAD_DIM, 1), 0.0)
                    else:
                        part = jnp.where(lo, 0.0, pltpu.roll(a, HEAD_DIM, 1) + b)
                    if acc_name == "k":
                        dkb += part
                    else:
                        dvb += part
            dk_ref[pl.ds(prev, CHUNK), :] += dkb[:CHUNK]
            dk_ref[pl.ds(cur, CHUNK), :] += dkb[CHUNK:]
            dv_ref[pl.ds(prev, CHUNK), :] += dvb[:CHUNK]
            dv_ref[pl.ds(cur, CHUNK), :] += dvb[CHUNK:]
        c_t, s1_t, s2_t = rc_ref[...], rs1_ref[...], rs2_ref[...]
        for p in range(nP):
            cols = slice(p * LANES, (p + 1) * LANES)
            dqp = _rot_bwd(dq_scr[:, cols] * (HEAD_DIM ** -0.5), c_t, s1_t, s2_t)
            dbq_ref[:, cols] += jnp.sum(dqp, axis=0, keepdims=True)
            dz2_ref[:, cols] = dqp.astype(BF)
        h1v = h1_ref[...]
        r2 = lax.rsqrt(jnp.mean(h1v * h1v, axis=-1, keepdims=True) + EPS)
        xh = h1v * r2
        gb = gb_ref[...]
        n2_ref[...] = (xh * gb).astype(BF).T
        dn2 = None
        for j in range(N_DEV):
            part = _dot_nt(dz2_ref[:, j * SH:(j + 1) * SH], wbin_ref[j])
            dn2 = part if dn2 is None else dn2 + part
        dgb_ref[...] += jnp.sum(dn2 * xh, axis=0, keepdims=True)
        dh1_ref[...] = dh2 + _rms_bwd(dn2, xh, r2, gb)

        @pl.when(i == nT - 1)
        def _():
            lane = lax.broadcasted_iota(jnp.int32, dsink_ref.shape, 1)
            tot = jnp.zeros(dsink_ref.shape, F32)
            for j in range(4):
                tot += jnp.where(lane == j, jnp.sum(dsacc_scr[:, j * CHUNK:(j + 1) * CHUNK], axis=1, keepdims=True), 0.0)
            dsink_ref[...] = tot

    row = functools.partial(_row_spec, TM)
    rows_tile, rows_before, cols_tile, cols_before = _band_specs(TM)
    S = jax.ShapeDtypeStruct
    return pl.pallas_call(
        body, name="b_bwd", grid=(T // TM,),
        in_specs=[row(D), row(D), row(BW), row(BW), row(BW), rows_tile, rows_before, rows_tile, rows_before,
                  cols_tile, cols_before, pl.BlockSpec(memory_space=pltpu.SMEM), _const_spec(wb_out.shape), _const_spec(wb_in.shape),
                  _const_spec((1, D)), row(LANES), row(LANES), row(LANES)],
        out_specs=[row(D), row(2 * BW), _col_spec(TM, D), _col_spec(TM, BW), _acc_spec((T, LANES)),
                   _acc_spec((T, LANES)), _acc_spec((1, BW)), _acc_spec((1, D)), _acc_spec((4, LANES))],
        out_shape=(S((T, D), F32), S((T, 2 * BW), BF), S((D, T), BF), S((BW, T), BF), S((T, LANES), F32),
                   S((T, LANES), F32), S((1, BW), F32), S((1, D), F32), S((4, LANES), F32)),
        scratch_shapes=[pltpu.VMEM((TM, BW), BF), pltpu.VMEM((TM, BW), F32), pltpu.VMEM((4, 4 * CHUNK), F32)],
        compiler_params=_params(("arbitrary",)),
    )(dh2, h1, q, g2, o, k4, k4, v4, v4, kt, kt, sinks, wb_out, wb_in, g_b, rc, rs1, rs2)


def _a_bwd(dh1p, dk, dv, h1, g_kv, w_kv, wa_out, ws, ln_g, ln_b, z, sv, vhat, rstd, rc, rs1, rs2, ready):
    T, D = h1.shape
    AW = wa_out.shape[0]
    G = ws.shape[0]
    TM = min(256, T)
    nT = T // TM
    nC = TM // CHUNK
    nr = len(ready)

    def body(dh1p_ref, dk_ref, dv_ref, h1_ref, gkv_ref, wkv_ref, waout_ref, ws_ref, lng_ref,
             lnb_ref, u_ref, gt_ref, sv_ref, vhat_ref, rstd_ref, rc_ref, rs1_ref, rs2_ref, *rest):
        ready_refs, rest = rest[:nr], rest[nr:]
        (dz_ref, gwo_ref, gwk_ref, dh1f_ref, dgkv_ref, dbkv_ref, dlng_ref, dlnb_ref,
         dws_ref, dbs_ref), rest = rest[:10], rest[10:]
        recv_refs, (dsv_scr, dvln_scr, acco_scr, acck_scr, ssem, rsem, lsem) = rest[:nr], rest[nr:]
        i = pl.program_id(0)
        exchanges = [_Direct(ready_refs[k], recv_refs[k], ssem.at[k], rsem.at[k], lsem.at[k], scatter=True)
                     for k in range(nr)]

        @pl.when(i == 0)
        def _():
            for e in exchanges:
                e.start()
            for r in (dgkv_ref, dbkv_ref, dlng_ref, dlnb_ref, dws_ref, dbs_ref, acco_scr, acck_scr):
                r[...] = jnp.zeros_like(r)

        dk_pre = _rot_bwd(dk_ref[...], rc_ref[...], rs1_ref[...], rs2_ref[...])
        dkv = jnp.concatenate([dk_pre, dv_ref[...]], axis=1)
        dbkv_ref[...] += jnp.sum(dkv, axis=0, keepdims=True)
        dkv_b = dkv.astype(BF)
        h1v = h1_ref[...]
        rkv = lax.rsqrt(jnp.mean(h1v * h1v, axis=-1, keepdims=True) + EPS)
        xh_kv = h1v * rkv
        gkv = gkv_ref[...]
        acck_scr[...] += _dot((xh_kv * gkv).astype(BF).T, dkv_b)
        dnkv = _dot_nt(dkv_b, wkv_ref[...])
        dgkv_ref[...] += jnp.sum(dnkv * xh_kv, axis=0, keepdims=True)
        dh1 = dh1p_ref[...] + _rms_bwd(dnkv, xh_kv, rkv, gkv)
        dh1_b = dh1.astype(BF)
        dh1f_ref[...] = dh1
        dy = _dot_nt(dh1_b, waout_ref[...]).astype(BF)
        silu, dsilu = _silu_parts(gt_ref[...].astype(F32))
        silu, dsilu = silu.astype(BF), dsilu.astype(BF)
        ub, svb = u_ref[...], sv_ref[...]
        us = ub * silu
        dys = dy * svb
        acco_scr[...] += _dot((us * svb).T, dh1_b)
        dz_ref[:, :AW] = dys * silu
        dz_ref[:, 2 * AW:] = dys * ub * dsilu
        dsv_scr[...] = dy * us
        vhat_v = vhat_ref[...].astype(F32)
        lng = lng_ref[...]
        vln_b = (vhat_v * lng + lnb_ref[...]).astype(BF)
        tri = lax.broadcasted_iota(jnp.int32, (CHUNK, CHUNK), 0) >= lax.broadcasted_iota(jnp.int32, (CHUNK, CHUNK), 1)
        lane = lax.broadcasted_iota(jnp.int32, (CHUNK, LANES), 1)
        dbs = jnp.zeros((CHUNK, LANES), F32)
        for g in range(G):
            wsm = jnp.where(tri, ws_ref[g], 0.0).astype(BF)
            cols = slice(g * CHUNK, (g + 1) * CHUNK)
            dws_g = None
            for c in range(nC):
                rows = slice(c * CHUNK, (c + 1) * CHUNK)
                dsv_cg = dsv_scr[rows, cols]
                dvln_scr[rows, cols] = _dot_tn(wsm, dsv_cg)
                part = _dot_nt(dsv_cg, vln_b[rows, cols])
                dws_g = part if dws_g is None else dws_g + part
                dbs += jnp.where(lane == g, jnp.sum(dsv_cg.astype(F32), axis=-1, keepdims=True), 0.0)
            dws_ref[g] += jnp.where(tri, dws_g, 0.0)
        dbs_ref[...] += dbs
        dvln = dvln_scr[...]
        dlng_ref[...] += jnp.sum(dvln * vhat_v, axis=0, keepdims=True)
        dlnb_ref[...] += jnp.sum(dvln, axis=0, keepdims=True)
        a = dvln * lng
        dvv = rstd_ref[:, 0:1] * (a - jnp.mean(a, axis=-1, keepdims=True)
                                  - vhat_v * jnp.mean(a * vhat_v, axis=-1, keepdims=True))
        dz_ref[:, AW:2 * AW] = dvv.astype(BF)

        @pl.when(i == nT - 1)
        def _():
            for j in range(N_DEV):
                gwo_ref[j] = acco_scr[j * (AW // N_DEV):(j + 1) * (AW // N_DEV)].astype(BF)
                gwk_ref[j] = acck_scr[j * (D // N_DEV):(j + 1) * (D // N_DEV)].astype(BF)
            for e in exchanges:
                e.finish()

    row = functools.partial(_row_spec, TM)
    hbm = pl.BlockSpec(memory_space=pl.ANY)
    S = jax.ShapeDtypeStruct
    gwo_shape, gwk_shape = (N_DEV, AW // N_DEV, D), (N_DEV, D // N_DEV, 2 * LANES)
    return pl.pallas_call(
        body, name="a_bwd", grid=(nT,),
        in_specs=[row(D), row(LANES), row(LANES), row(D), _const_spec((1, D)), _const_spec(w_kv.shape),
                  _const_spec(wa_out.shape), _const_spec(ws.shape),
                  _const_spec((1, AW)), _const_spec((1, AW)), pl.BlockSpec((TM, AW), lambda i: (i, 0)),
                  pl.BlockSpec((TM, AW), lambda i: (i, 2)), row(AW), row(AW), row(LANES),
                  row(LANES), row(LANES), row(LANES)] + [hbm] * nr,
        out_specs=[row(3 * AW), _const_spec(gwo_shape), _const_spec(gwk_shape), row(D),
                   _acc_spec((1, D)), _acc_spec((1, 2 * LANES)), _acc_spec((1, AW)),
                   _acc_spec((1, AW)), _acc_spec(ws.shape), _acc_spec((CHUNK, LANES))] + [hbm] * nr,
        out_shape=(S((T, 3 * AW), BF), S(gwo_shape, BF), S(gwk_shape, BF), S((T, D), F32),
                   S((1, D), F32), S((1, 2 * LANES), F32), S((1, AW), F32), S((1, AW), F32),
                   S(ws.shape, F32), S((CHUNK, LANES), F32)) + tuple(S(r.shape, r.dtype) for r in ready),
        scratch_shapes=[pltpu.VMEM((TM, AW), BF), pltpu.VMEM((TM, AW), F32), pltpu.VMEM((AW, D), F32),
                        pltpu.VMEM((D, 2 * LANES), F32)] + _direct_sems(nr),
        compiler_params=_params(("arbitrary",)),
    )(dh1p, dk, dv, h1, g_kv, w_kv, wa_out, ws, ln_g, ln_b, z, z, sv, vhat, rstd, rc, rs1, rs2, *ready)


def _a_in_bwd(dz, wa_in_t, x, dh1, g_a, ready):
    T, D = x.shape
    TM = min(512, T)
    nT = T // TM
    nr = len(ready)

    def body(dz_ref, wain_ref, x_ref, dh1_ref, ga_ref, *rest):
        ready_refs, (dx_ref, n1_ref, dga_ref), rest = rest[:nr], rest[nr:nr + 3], rest[nr + 3:]
        recv_refs, (ssem, rsem, lsem) = rest[:nr], rest[nr:]
        i = pl.program_id(0)
        exchanges = [_Direct(ready_refs[k], recv_refs[k], ssem.at[k], rsem.at[k], lsem.at[k], scatter=True)
                     for k in range(nr)]

        @pl.when(i == 0)
        def _():
            for e in exchanges:
                e.start()
            dga_ref[...] = jnp.zeros_like(dga_ref)

        xv = x_ref[...]
        r1 = lax.rsqrt(jnp.mean(xv * xv, axis=-1, keepdims=True) + EPS)
        xh = xv * r1
        ga = ga_ref[...]
        n1_ref[...] = (xh * ga).astype(BF).T
        dn1 = _dot(dz_ref[...], wain_ref[...])
        dga_ref[...] += jnp.sum(dn1 * xh, axis=0, keepdims=True)
        dx_ref[...] = dh1_ref[...] + _rms_bwd(dn1, xh, r1, ga)

        @pl.when(i == nT - 1)
        def _():
            for e in exchanges:
                e.finish()

    row = functools.partial(_row_spec, TM)
    hbm = pl.BlockSpec(memory_space=pl.ANY)
    S = jax.ShapeDtypeStruct
    return pl.pallas_call(
        body, name="a_in_bwd", grid=(nT,),
        in_specs=[row(dz.shape[1]), _const_spec(wa_in_t.shape), row(D), row(D), _const_spec((1, D))] + [hbm] * nr,
        out_specs=[row(D), _col_spec(TM, D), _acc_spec((1, D))] + [hbm] * nr,
        out_shape=(S((T, D), F32), S((D, T), BF), S((1, D), F32)) + tuple(S(r.shape, r.dtype) for r in ready),
        scratch_shapes=_direct_sems(nr),
        compiler_params=_params(("arbitrary",)),
    )(dz, wa_in_t, x, dh1, g_a, *ready)


def _wgrad(problems, name, bt=512):
    T = problems[0][0].shape[1]
    BT = min(bt, T)
    nt = T // BT
    n = len(problems)
    dims = [(at.shape[0], b.shape[1] // nblk, nblk) for at, b, nblk in problems]

    def body(*refs):
        ins, outs, accs = refs[:2 * n], refs[2 * n:3 * n], refs[3 * n:]
        t = pl.program_id(0)

        @pl.when(t == 0)
        def _():
            for acc in accs:
                acc[...] = jnp.zeros_like(acc)

        for k in range(n):
            accs[k][...] += _dot(ins[2 * k][...], ins[2 * k + 1][...])

        @pl.when(t == nt - 1)
        def _():
            for k, (_, N, nblk) in enumerate(dims):
                for j in range(nblk):
                    outs[k][j] = accs[k][:, j * N:(j + 1) * N].astype(BF)

    in_specs = []
    for at, b, _ in problems:
        in_specs += [pl.BlockSpec((at.shape[0], BT), lambda t: (0, t)), pl.BlockSpec((BT, b.shape[1]), lambda t: (t, 0))]
    return pl.pallas_call(
        body, name=name, grid=(nt,), in_specs=in_specs,
        out_specs=[pl.BlockSpec((nblk, K, N), lambda t: (0, 0, 0)) for K, N, nblk in dims],
        out_shape=[jax.ShapeDtypeStruct((nblk, K, N), BF) for K, N, nblk in dims],
        scratch_shapes=[pltpu.VMEM((K, nblk * N), F32) for K, N, nblk in dims],
        compiler_params=_params(("arbitrary",)),
    )(*[operand for at, b, _ in problems for operand in (at, b)])


def _wgrad_exchange(a, b, me, small, name):
    K, T = a.shape
    N = b.shape[1] // N_DEV
    BT = T
    nt = T // BT
    last = N_DEV - 1
    n_chip = N_DEV // 2

    def far_of(k, core):
        return jnp.where((core == 0) & ((k == 1) | (k == 2)), k, n_chip - 1 - k)

    def block_of(s, me_i):
        k, odd = s // 2, s % 2
        core = me_i & 1
        return me_i ^ ((far_of(k, jnp.where(odd == 1, core, 1 - core)) << 1) | (1 - odd))

    H = K // 2

    def body(me_ref, a_ref, b_ref, small_ref, recv_ref, full_ref, *scratch):
        (acc, dstage, istage, half, relay, d_s, d_r, i_s, i_r, r_s, r_r, lsem, parts_scr, red_scr, e_s, e_r, e_l, g_s,
         g_r, g_l) = scratch
        s, t = pl.program_id(0), pl.program_id(1)
        x, y, c = (lax.axis_index(ax) for ax in AXES)
        ex = [_Direct(small_ref, parts_scr, e_s, e_r, e_l, scatter=True)]
        regather = _TwoLevel(red_scr, full_ref, g_s, g_r, g_l)

        def to_sibling(k, slot):
            return pltpu.make_async_remote_copy(src_ref=dstage.at[slot], dst_ref=half.at[k], send_sem=d_s.at[k],
                                                recv_sem=d_r.at[k], device_id=(x, y, 1 - c), device_id_type=MESH)

        def to_chip(k, slot):
            over_x = far_of(k, c) == 2
            px, py = jnp.where(over_x, 1 - x, x), jnp.where(over_x, y, 1 - y)
            return pltpu.make_async_remote_copy(src_ref=istage.at[slot], dst_ref=recv_ref.at[jnp.where(over_x, 1, 2)],
                                                send_sem=i_s.at[k], recv_sem=i_r.at[k], device_id=(px, py, c),
                                                device_id_type=MESH)

        def to_relay(j, slot):
            to = (1 - x, y, c) if j == 0 else (x, 1 - y, c)
            return pltpu.make_async_remote_copy(src_ref=istage.at[slot, pl.ds(j * H, H)], dst_ref=relay.at[j],
                                                send_sem=r_s.at[j], recv_sem=r_r.at[j], device_id=to,
                                                device_id_type=MESH)

        @pl.when((s == 0) & (t == 0))
        def _():
            for e in ex:
                e.start()

        acc[...] = _dot(a_ref[...], b_ref[...])

        @pl.when(t == nt - 1)
        def _():
            k = lax.div(s, 2)
            slot = lax.rem(k, 2)

            @pl.when(lax.rem(s, 2) == 0)
            def _():
                @pl.when(k >= 2)
                def _():
                    to_sibling(k - 2, slot).wait_send()

                dstage[slot] = acc[...].astype(BF)
                to_sibling(k, slot).start()

            @pl.when(lax.rem(s, 2) == 1)
            def _():
                to_sibling(k, slot).wait_recv()
                pair = acc[...] + half[k].astype(F32)

                @pl.when(k == 0)
                def _():
                    istage[slot] = pair.astype(BF)
                    for j in range(2):
                        to_relay(j, slot).start()

                @pl.when(k == 1)
                def _():
                    for j in range(2):
                        to_relay(j, slot).wait_recv()

                @pl.when(k == 2)
                def _():
                    for j in range(2):
                        to_relay(j, slot).wait_send()

                @pl.when(k == n_chip - 1)
                def _():
                    to_chip(1, slot).wait_send()
                    istage[slot] = pair.astype(BF)

                @pl.when((k == 1) | (k == 2))
                def _():
                    over_x = far_of(k, c) == 2
                    istage[slot, 0:H] = (pair[:H] + jnp.where(over_x, 0.0, relay[0].astype(F32))).astype(BF)
                    istage[slot, H:K] = (pair[H:] + jnp.where(over_x, relay[1].astype(F32), 0.0)).astype(BF)
                    to_chip(k, slot).start()

            @pl.when(s == last)
            def _():
                own = pltpu.make_async_copy(istage.at[slot], recv_ref.at[0], lsem)
                own.start()
                to_chip(2, 0).wait_send()
                to_sibling(n_chip - 2, 0).wait_send()
                to_sibling(n_chip - 1, 1).wait_send()
                for kk in (1, 2):
                    to_chip(kk, 0).wait_recv()
                own.wait()
                for e in ex:
                    e.finish()
                total = parts_scr[0]
                for dev in range(1, N_DEV):
                    total = total + parts_scr[dev]
                red_scr[...] = total
                regather.start()
                regather.forward()
                regather.finish()

    hbm = pl.BlockSpec(memory_space=pl.ANY)
    dma = pltpu.SemaphoreType.DMA
    grid_spec = pltpu.PrefetchScalarGridSpec(
        num_scalar_prefetch=1, grid=(N_DEV, nt),
        in_specs=[pl.BlockSpec((K, BT), lambda s, t, me_ref: (0, t), pipeline_mode=pl.Buffered(1)),
                  pl.BlockSpec((BT, N), lambda s, t, me_ref: (t, block_of(s, me_ref[0]))), hbm],
        out_specs=[hbm, hbm],
        scratch_shapes=[pltpu.VMEM((K, N), F32), pltpu.VMEM((2, K, N), BF), pltpu.VMEM((2, K, N), BF),
                        pltpu.VMEM((n_chip, K, N), BF), pltpu.VMEM((2, H, N), BF), dma((n_chip,)), dma((n_chip,)),
                        dma((n_chip - 1,)), dma((n_chip - 1,)), dma((2,)), dma((2,)), dma,
                        pltpu.VMEM(small.shape, F32), pltpu.VMEM(small.shape[1:], F32),
                        dma((last,)), dma((last,)), dma, dma((last,)), dma((last,)), dma])
    return pl.pallas_call(
        body, name=name, grid_spec=grid_spec,
        out_shape=[jax.ShapeDtypeStruct((n_chip - 1, K, N), BF), jax.ShapeDtypeStruct(small.shape, F32)],
        compiler_params=_params(("arbitrary", "arbitrary")),
    )(me, a, b, small)


def _my_index():
    return 4 * lax.axis_index("x") + 2 * lax.axis_index("y") + lax.axis_index("c")


def _peer(mask):
    x, y, c = (lax.axis_index(a) for a in AXES)
    return (x ^ ((mask >> 2) & 1), y ^ ((mask >> 1) & 1), c ^ (mask & 1))


def _dev_index(p):
    return 4 * p[0] + 2 * p[1] + p[2]


class _Direct:
    def __init__(self, src, dst, send_sems, recv_sems, local_sem, scatter):
        me = _my_index()
        self.own = pltpu.make_async_copy(src.at[me] if scatter else src, dst.at[me], local_sem)
        self.sends, self.recvs = [], []
        for k in range(1, N_DEV):
            p = _peer(k)
            pi = _dev_index(p)
            sems = dict(send_sem=send_sems.at[k - 1], recv_sem=recv_sems.at[k - 1], device_id=p, device_id_type=MESH)
            self.sends.append(pltpu.make_async_remote_copy(src_ref=src.at[pi] if scatter else src, dst_ref=dst.at[me],
                                                           **sems))
            self.recvs.append(pltpu.make_async_remote_copy(src_ref=src.at[me] if scatter else src, dst_ref=dst.at[pi],
                                                           **sems))

    def start(self):
        self.own.start()
        for cp in self.sends:
            cp.start()

    def finish(self):
        for cp in self.sends:
            cp.wait_send()
        for cp in self.recvs:
            cp.wait_recv()
        self.own.wait()


class _TwoLevel:
    def __init__(self, src, dst, send_sems, recv_sems, local_sem, own=True):
        x, y, c = (lax.axis_index(a) for a in AXES)
        self.me, self.sibling = (x, y, c), (x, y, 1 - c)
        self.chips = [(1 - x, y), (x, 1 - y), (1 - x, 1 - y)]
        self.src, self.dst, self.send_sems, self.recv_sems = src, dst, send_sems, recv_sems
        self.own = pltpu.make_async_copy(src, dst.at[_dev_index(self.me)], local_sem) if own else None

    def _copy(self, k, block, to, from_src=False):
        slot = self.dst.at[_dev_index(block)]
        return pltpu.make_async_remote_copy(src_ref=self.src if from_src else slot, dst_ref=slot,
                                            send_sem=self.send_sems.at[k], recv_sem=self.recv_sems.at[k],
                                            device_id=to, device_id_type=MESH)

    def _firsts(self):
        c = self.me[2]
        return [self._copy(0, self.me, self.sibling, True)] + [self._copy(1 + j, self.me, (*chip, c), True)
                                                               for j, chip in enumerate(self.chips)]

    def _passed(self):
        c = self.me[2]
        return [self._copy(4 + j, (*chip, c), self.sibling) for j, chip in enumerate(self.chips)]

    def start(self):
        if self.own is not None:
            self.own.start()
        for cp in self._firsts():
            cp.start()

    def wait_sibling(self):
        self._copy(0, self.sibling, self.me).wait_recv()

    def wait_chip_and_forward(self, j):
        self._copy(1 + j, (*self.chips[j], self.me[2]), self.me).wait_recv()
        self._passed()[j].start()

    def wait_passed(self, j):
        self._copy(4 + j, (*self.chips[j], 1 - self.me[2]), self.me).wait_recv()

    def wait_sends(self):
        for cp in self._firsts() + self._passed():
            cp.wait_send()
        if self.own is not None:
            self.own.wait()

    def forward(self):
        for j in range(3):
            self.wait_chip_and_forward(j)

    def finish(self):
        self.wait_sibling()
        for j in range(3):
            self.wait_passed(j)
        self.wait_sends()


class _RelayGather:
    def __init__(self, dst, send_sems, recv_sems):
        x, y, c = (lax.axis_index(a) for a in AXES)
        self.c = c
        self.sib, self.xn, self.yn, self.dg = (x, y, 1 - c), (1 - x, y, c), (x, 1 - y, c), (1 - x, 1 - y, c)
        self.me = (x, y, c)
        self.dst, self.send_sems, self.recv_sems = dst, send_sems, recv_sems
        self.half = dst.shape[1] // 2

    def _slot(self, dev, part=None):
        i = _dev_index(dev)
        if part is None:
            return self.dst.at[i]
        return self.dst.at[i, pl.ds(part * self.half, self.half)]

    def _copy(self, k, dev, to, part=None):
        ref = self._slot(dev, part)
        return pltpu.make_async_remote_copy(src_ref=ref, dst_ref=ref, send_sem=self.send_sems.at[k],
                                            recv_sem=self.recv_sems.at[k], device_id=to, device_id_type=MESH)

    def _other(self, dev):
        return (dev[0], dev[1], 1 - self.c)

    def start(self):
        for k, to in enumerate((self.sib, self.xn, self.yn)):
            self._copy(k, self.me, to).start()

    def send_own(self, k):
        return self._copy(k, self.me, (self.sib, self.xn, self.yn)[k])

    def wait_sibling(self):
        self._copy(0, self.sib, self.me).wait_recv()

    def on_x(self):
        self._copy(1, self.xn, self.me).wait_recv()
        self._copy(3, self.xn, self.yn, part=0).start()
        self._copy(5, self.xn, self.sib).start()

    def on_y(self):
        self._copy(2, self.yn, self.me).wait_recv()
        self._copy(4, self.yn, self.xn, part=1).start()
        self._copy(6, self.yn, self.sib).start()

    def on_diag(self):
        self._copy(3, self.dg, self.me, part=0).wait_recv()
        self._copy(4, self.dg, self.me, part=1).wait_recv()
        self._copy(7, self.dg, self.sib).start()

    def wait_passed(self, j):
        self._copy(5 + j, self._other((self.xn, self.yn, self.dg)[j]), self.me).wait_recv()

    def wait_sends(self):
        for k, to in enumerate((self.sib, self.xn, self.yn)):
            self._copy(k, self.me, to).wait_send()
        self._copy(3, self.xn, self.yn, part=0).wait_send()
        self._copy(4, self.yn, self.xn, part=1).wait_send()
        for j, dev in enumerate((self.xn, self.yn, self.dg)):
            self._copy(5 + j, dev, self.sib).wait_send()


def _direct_sems(n):
    if n == 0:
        return []
    return [pltpu.SemaphoreType.DMA((n, 7)), pltpu.SemaphoreType.DMA((n, 7)), pltpu.SemaphoreType.DMA((n,))]


def _adam_math(w, g, m, v):
    m = ADAM_B1 * m + (1.0 - ADAM_B1) * g
    v = ADAM_B2 * v + (1.0 - ADAM_B2) * (g * g)
    m_hat = m / (1.0 - ADAM_B1 ** ADAM_STEP)
    v_hat = v / (1.0 - ADAM_B2 ** ADAM_STEP)
    delta = -ADAM_LR * (m_hat / (jnp.sqrt(v_hat) + ADAM_EPS) + ADAM_WD * w)
    return delta, m, v


def _sum_adam(tensors, name):
    NB = 4
    n = len(tensors)

    def body(*refs):
        ins, outs = refs[:4 * n], refs[4 * n:]
        for k in range(n):
            p_ref, w_ref, m_ref, v_ref = ins[4 * k:4 * k + 4]
            g_ref, d_ref, nm_ref, nv_ref = outs[4 * k:4 * k + 4]
            g = p_ref[0].astype(F32)
            for i in range(1, p_ref.shape[0]):
                g = g + p_ref[i].astype(F32)
            g_ref[...] = g
            d_ref[...], nm_ref[...], nv_ref[...] = _adam_math(w_ref[...], g, m_ref[...], v_ref[...])

    in_specs, out_specs, out_shape, operands = [], [], [], []
    for parts, w, m, v in tensors:
        R, C = w.shape
        blk = pl.BlockSpec((R // NB, C), lambda i: (i, 0))
        in_specs += [pl.BlockSpec((parts.shape[0], R // NB, C), lambda i: (0, i, 0)), blk, blk, blk]
        out_specs += [blk] * 4
        out_shape += [jax.ShapeDtypeStruct((R, C), F32)] * 4
        operands += [parts, w, m, v]
    res = pl.pallas_call(
        body, name=name, grid=(NB,), in_specs=in_specs, out_specs=out_specs, out_shape=out_shape,
        compiler_params=_params(("arbitrary",)),
    )(*operands)
    return [tuple(res[4 * k:4 * k + 4]) for k in range(n)]


SUBLANES = 8


def _nrows(size):
    return -(-size // (SUBLANES * LANES)) * SUBLANES


def _view2d(a):
    return a.reshape(-1, LANES) if a.size % LANES == 0 else a.reshape(1, -1)


def _pack_small(parts, total_rows, name):
    arrs = [p[0] for p in parts]

    def body(*refs):
        out = refs[-1]
        out[...] = jnp.zeros_like(out)
        at = 0
        for ref, (a, rows, flag) in zip(refs[:-1], parts):
            val = ref[...].T if flag == "T" else ref[...]
            r, c = (rows, val.shape[1]) if flag == "T" else val.shape
            out[at:at + r, 0:c] = val[:r]
            at += _nrows(r * c)

    return pl.pallas_call(body, name=name, out_shape=jax.ShapeDtypeStruct((total_rows, LANES), F32))(*arrs)


def _small_update(full, me, reps, shards, name):
    n = len(reps) + len(shards)

    def body(me_ref, full_ref, *refs):
        ins, outs = refs[:3 * n], refs[3 * n:]
        at = 0
        for k in range(n):
            w_ref, m_ref, v_ref = ins[3 * k:3 * k + 3]
            r, c = w_ref.shape
            if k < len(reps):
                g = full_ref[at:at + r, 0:c]
                at += _nrows(r * c)
            else:
                seg = full_ref[at:at + N_DEV * r, :]
                row = lax.broadcasted_iota(jnp.int32, seg.shape, 0)
                pick = [jnp.sum(jnp.where(row == r * me_ref[0] + t, seg, 0.0), axis=0, keepdims=True) for t in range(r)]
                g = pick[0] if r == 1 else jnp.concatenate(pick, axis=0)
                at += N_DEV * r
            g_ref, d_ref, nm_ref, nv_ref = outs[4 * k:4 * k + 4]
            g_ref[...] = g
            d_ref[...], nm_ref[...], nv_ref[...] = _adam_math(w_ref[...], g, m_ref[...], v_ref[...])
        outs[4 * n][...] = full_ref[at:at + 1, 0:1]

    flat = [t for p in reps + shards for t in p]
    S = jax.ShapeDtypeStruct
    res = pl.pallas_call(
        body, name=name,
        in_specs=[pl.BlockSpec(memory_space=pltpu.SMEM)] + [pl.BlockSpec(memory_space=pltpu.VMEM)] * (1 + len(flat)),
        out_shape=[S(p[0].shape, F32) for p in reps + shards for _ in range(4)] + [S((1, 1), F32)],
    )(me, full, *flat)
    return [tuple(res[4 * k:4 * k + 4]) for k in range(n)], res[4 * n]


def _rope_tables(T):
    pos = np.arange(T, dtype=np.float32)
    inv_freq = (np.float64(ROPE_THETA) ** (-np.arange(0, HEAD_DIM, 2, dtype=np.float64) / HEAD_DIM)).astype(np.float32)
    ang = (pos[:, None] * inv_freq[None, :]).astype(np.float64)
    cos, sin, zero = np.cos(ang).astype(np.float32), np.sin(ang).astype(np.float32), np.zeros(ang.shape, np.float32)
    c = np.concatenate([cos, cos, cos, cos], axis=1)
    s1 = np.concatenate([-sin, zero, -sin, zero], axis=1)
    s2 = np.concatenate([zero, sin, zero, sin], axis=1)
    return jnp.asarray(c), jnp.asarray(s1), jnp.asarray(s2)


def kernel(x, a_norm_g, a_w_in, a_ln_g, a_ln_b, a_ws, a_bs, a_w_out, kv_norm_g, w_kv, b_kv, b_norm_g, b_w_in, b_bq, b_sinks, b_w_out, final_norm_g, loss_target, m_a_norm_g, m_a_w_in, m_a_ln_g, m_a_ln_b, m_a_ws, m_a_bs, m_a_w_out, m_kv_norm_g, m_w_kv, m_b_kv, m_b_norm_g, m_b_w_in, m_b_bq, m_b_sinks, m_b_w_out, m_final_norm_g, v_a_norm_g, v_a_w_in, v_a_ln_g, v_a_ln_b, v_a_ws, v_a_bs, v_a_w_out, v_kv_norm_g, v_w_kv, v_b_kv, v_b_norm_g, v_b_w_in, v_b_bq, v_b_sinks, v_b_w_out, v_final_norm_g):
    T, D = x.shape[1], x.shape[2]
    AW = a_ln_g.shape[1] * N_DEV
    G = a_ws.shape[1]
    assert w_kv.shape[1] == 2 * LANES and a_ws.shape[2] == CHUNK and T % CHUNK == 0
    me = _my_index()

    xs, tgt = x[0], loss_target[0]
    z, wa_in_t, g_a, ln_g, ln_b, wa_out, wkv = _in_proj(xs, a_w_in[0], [a_norm_g, a_ln_g, a_ln_b], me.reshape(1),
                                                        [a_w_out[0], w_kv])
    wa_in_t = wa_in_t.reshape(-1, D)
    wa_out = wa_out.reshape(AW, D)
    wkv = wkv.reshape(D, 2 * LANES)

    rc, rs1, rs2 = _rope_tables(T)
    ws = a_ws[0]
    g_kv = kv_norm_g.reshape(1, D)
    bkv = b_kv.reshape(1, -1)
    g_f = final_norm_g.reshape(1, D)
    sinks = b_sinks.reshape(1, 16)
    h1, sv, vhat, rstd, k4, v4, kt, vt, wb_in, wb_out = _a_fwd(
        xs, z, ln_g, ln_b, ws, a_bs[0], wa_out, g_kv, wkv, bkv, rc, rs1, rs2, [b_w_in[0], b_w_out[0]])
    wb_out = wb_out.reshape(-1, D)
    q, g2, o, dh2, dh2_b, loss, d_gf = _b_fwd(h1, b_norm_g, wb_in, b_bq, rc, rs1, rs2, k4, vt, sinks, wb_out, g_f, tgt)
    dh1p, dz2, n2, y2, dk, dv, d_bq, d_gb, d_sink = _b_bwd(dh2, h1, q, g2, o, k4, v4, kt, sinks, wb_out, wb_in,
                                                           b_norm_g, rc, rs1, rs2)
    d_sink = d_sink[:, :4].reshape(2, 2, 4).transpose(0, 2, 1).reshape(1, 16)
    gw_b_in, gw_b_out = _wgrad([(n2, dz2, N_DEV), (y2, dh2_b, 1)], "wgrad_b", bt=1024)
    gw_b_out = gw_b_out.reshape(N_DEV, -1, D)
    (dz, gw_a_out, gw_kv, dh1_f, d_gkv, d_bkv, d_lng, d_lnb, d_ws, d_bst, r_b_in, r_b_out) = _a_bwd(
        dh1p, dk, dv, h1, g_kv, wkv, wa_out, ws, ln_g, ln_b, z, sv, vhat, rstd, rc, rs1, rs2, [gw_b_in, gw_b_out])
    dx, n1, d_ga, r_a_out, r_kv = _a_in_bwd(dz, wa_in_t, xs, dh1_f, g_a, [gw_a_out, gw_kv])
    small = [(_view2d(d_ws), None, None), (d_bst, G, "T")] + [(_view2d(a), None, None) for a in (
        d_gkv, d_bkv, d_gb, d_bq, d_sink, d_gf, d_ga, d_lng, d_lnb, loss)]
    used = sum(_nrows(G * CHUNK if flag else a.size) for a, _, flag in small)
    per = -(-used // (SUBLANES * N_DEV)) * SUBLANES
    small_pack = _pack_small(small, per * N_DEV, "pack_small").reshape(N_DEV, per, LANES)
    r_a_in, full_small = _wgrad_exchange(n1, dz, me.reshape(1), small_pack, "wgrad_a_in")

    ((g_a_out, d_a_out, nm_a_out, nv_a_out), (g_kvw, d_kvw, nm_kvw, nv_kvw), (g_b_in, d_b_in, nm_b_in, nv_b_in),
     (g_b_out, d_b_out, nm_b_out, nv_b_out)) = _sum_adam(
        [(r_a_out, a_w_out[0], m_a_w_out[0], v_a_w_out[0]), (r_kv, w_kv, m_w_kv, v_w_kv),
         (r_b_in, b_w_in[0], m_b_w_in[0], v_b_w_in[0]), (r_b_out, b_w_out[0], m_b_w_out[0], v_b_w_out[0])], "adam_rest")
    (g_a_in, d_a_in, nm_a_in, nv_a_in), = _sum_adam([(r_a_in, a_w_in[0], m_a_w_in[0], v_a_w_in[0])], "adam_a_in")

    full_small = full_small.reshape(N_DEV * per, LANES)
    reps = [(a_ws, m_a_ws, v_a_ws), (a_bs, m_a_bs, v_a_bs), (kv_norm_g, m_kv_norm_g, v_kv_norm_g),
            (b_kv, m_b_kv, v_b_kv), (b_norm_g, m_b_norm_g, v_b_norm_g), (b_bq, m_b_bq, v_b_bq),
            (b_sinks, m_b_sinks, v_b_sinks), (final_norm_g, m_final_norm_g, v_final_norm_g)]
    shards = [(a_norm_g, m_a_norm_g, v_a_norm_g), (a_ln_g, m_a_ln_g, v_a_ln_g), (a_ln_b, m_a_ln_b, v_a_ln_b)]
    upd, loss = _small_update(full_small, me.reshape(1), [tuple(_view2d(t) for t in p) for p in reps],
                              [tuple(_view2d(t) for t in p) for p in shards], "adam_small")
    loss = loss[0, 0]
    sm_g, sd, snm, snv = ([upd[k][j].reshape(p[0].shape) for k, p in enumerate(reps + shards)] for j in range(4))

    def order(big, sm):
        a_in, a_out, kvw, b_in, b_out = big
        ws_, bs_, kvg, bkv_, bng, bq_, snk, fng, ang, alng, alnb = sm
        return (ang, a_in[None], alng, alnb, ws_, bs_, a_out[None], kvg, kvw, bkv_, bng, b_in[None], bq_, snk,
                b_out[None], fng)

    grads = order((g_a_in, g_a_out, g_kvw, g_b_in, g_b_out), sm_g)
    deltas = order((d_a_in, d_a_out, d_kvw, d_b_in, d_b_out), sd)
    new_m = order((nm_a_in, nm_a_out, nm_kvw, nm_b_in, nm_b_out), snm)
    new_v = order((nv_a_in, nv_a_out, nv_kvw, nv_b_in, nv_b_out), snv)
    return (loss, dx[None], *grads, *deltas, *new_m, *new_v)
```

```python
import functools

import jax
import jax.numpy as jnp
import numpy as np
from jax import lax
from jax.experimental import pallas as pl
from jax.experimental.pallas import tpu as pltpu

CHUNK = 128
HEAD_DIM = 64
ROPE_THETA = 10000.0
EPS = 1e-5
ADAM_LR = 0.001
ADAM_B1 = 0.9
ADAM_B2 = 0.999
ADAM_EPS = 1e-08
ADAM_WD = 0.01
ADAM_STEP = 10
N_DEV = 8
LANES = 128
NEG = -1e30

BF = jnp.bfloat16
F32 = jnp.float32
MESH = pl.DeviceIdType.MESH
AXES = ("x", "y", "c")
VMEM_LIMIT = 56 * 1024 * 1024


def _dot(a, b):
    return jnp.dot(a, b, preferred_element_type=F32)


def _dot_nt(a, b):
    return lax.dot_general(a, b, (((1,), (1,)), ((), ())), preferred_element_type=F32)


def _dot_tn(a, b):
    return lax.dot_general(a, b, (((0,), (0,)), ((), ())), preferred_element_type=F32)


def _const_spec(shape):
    nd = len(shape)
    return pl.BlockSpec(shape, lambda *_: (0,) * nd, pipeline_mode=pl.Buffered(1))


def _acc_spec(shape):
    nd = len(shape)
    return pl.BlockSpec(shape, lambda *_: (0,) * nd)


def _row_spec(tm, width):
    return pl.BlockSpec((tm, width), lambda i: (i, 0))


def _col_spec(tm, height):
    return pl.BlockSpec((height, tm), lambda i: (0, i))


def _params(sem):
    return pltpu.CompilerParams(dimension_semantics=sem, vmem_limit_bytes=VMEM_LIMIT)


def _rot(x, c, s1, s2):
    return x * c + pltpu.roll(x, 96, 1) * s1 + pltpu.roll(x, 32, 1) * s2


def _rot_bwd(d, c, s1, s2):
    return d * c + pltpu.roll(d * s1, 32, 1) + pltpu.roll(d * s2, 96, 1)


def _silu_parts(g):
    sg = jax.nn.sigmoid(g)
    return g * sg, sg * (1.0 + g * (1.0 - sg))


def _rms_bwd(dn, xh, r, g):
    a = dn * g
    return r * (a - xh * jnp.mean(a * xh, axis=-1, keepdims=True))


def _lane_lo(shape):
    return lax.broadcasted_iota(jnp.int32, shape, 1) < HEAD_DIM


def _split4(t):
    lo = _lane_lo(t.shape)
    tr = pltpu.roll(t, HEAD_DIM, 1)
    z = jnp.zeros_like(t)
    return jnp.concatenate([jnp.where(lo, t, z), jnp.where(lo, z, tr), jnp.where(lo, tr, z), jnp.where(lo, z, t)], axis=1)


def _stack_pairs(t, h):
    return jnp.concatenate([t[:, (h * 4 + j) * LANES:(h * 4 + j + 1) * LANES] for j in range(4)], axis=0)


def _upper():
    shape = (CHUNK, 4 * CHUNK)
    return lax.broadcasted_iota(jnp.int32, shape, 0) > (lax.broadcasted_iota(jnp.int32, shape, 1) & (CHUNK - 1))


def _band_rows(tile_ref, before_ref, c, h):
    a = slice(2 * h * LANES, (2 * h + 1) * LANES)
    b = slice((2 * h + 1) * LANES, (2 * h + 2) * LANES)
    cur = slice(c * CHUNK, (c + 1) * CHUNK)

    def prev(cols):
        return before_ref[:, cols] if c == 0 else tile_ref[(c - 1) * CHUNK:c * CHUNK, cols]

    return jnp.concatenate([prev(a), tile_ref[cur, a], prev(b), tile_ref[cur, b]], axis=0)


def _band_cols(tile_ref, before_ref, c, h):
    a = slice(2 * h * LANES, (2 * h + 1) * LANES)
    b = slice((2 * h + 1) * LANES, (2 * h + 2) * LANES)

    def prev(rows):
        return before_ref[0, rows, :] if c == 0 else tile_ref[c - 1, rows, :]

    return jnp.concatenate([prev(a), tile_ref[c, a, :], prev(b), tile_ref[c, b, :]], axis=1)


def _band_specs(tm):
    nc = tm // CHUNK

    def before(i):
        return jnp.maximum(i * nc - 1, 0)

    return (pl.BlockSpec((tm, 4 * LANES), lambda i: (i, 0)),
            pl.BlockSpec((CHUNK, 4 * LANES), lambda i: (before(i), 0)),
            pl.BlockSpec((nc, 4 * LANES, CHUNK), lambda i: (i, 0, 0)),
            pl.BlockSpec((1, 4 * LANES, CHUNK), lambda i: (before(i), 0, 0)))


def _fold(t, upper, has_prev=None):
    out = []
    for k in range(2):
        prev = t[2 * k * CHUNK:(2 * k + 1) * CHUNK]
        if has_prev is not None:
            prev = jnp.where(has_prev, prev, NEG)
        out.append(jnp.where(upper, prev, t[(2 * k + 1) * CHUNK:(2 * k + 2) * CHUNK]))
    return out


def _unfold(fa, fb, upper):
    z = jnp.zeros_like(fa)
    return jnp.concatenate([jnp.where(upper, fa, z), jnp.where(upper, z, fa),
                            jnp.where(upper, fb, z), jnp.where(upper, z, fb)], axis=0)


def _sink_tile(s_ref):
    shape = (4, 4 * LANES)
    row = lax.broadcasted_iota(jnp.int32, shape, 0)
    pair = lax.broadcasted_iota(jnp.int32, shape, 1) // LANES
    idx = (row // 2) * 8 + pair * 2 + row % 2
    tile = jnp.zeros(shape, F32)
    for n in range(16):
        tile = jnp.where(idx == n, s_ref[0, n], tile)
    return tile


def _softmax_sink(f, sink):
    m = jnp.maximum(jnp.max(f, axis=0, keepdims=True), sink)
    p = jnp.exp(f - m)
    es = jnp.exp(sink - m)
    inv = 1.0 / (jnp.sum(p, axis=0, keepdims=True) + es)
    return p * inv, es * inv


class _Riding:
    def __init__(self, shards, gathered, stages, sems, n_steps):
        self.shards, self.stages, self.n_steps = shards, stages, n_steps
        ssem, rsem, lsem = sems
        self.gathers = [_TwoLevel(stages[k], gathered[k], ssem.at[k], rsem.at[k], lsem.at[k])
                        for k in range(len(shards))]

    def begin(self, i):
        @pl.when(i == 0)
        def _():
            for shard, stage, g in zip(self.shards, self.stages, self.gathers):
                stage[...] = shard[...].astype(stage.dtype)
                g.start()

    def end(self, i):
        @pl.when(i == self.n_steps // 2)
        def _():
            for g in self.gathers:
                g.forward()

        @pl.when(i == self.n_steps - 1)
        def _():
            for g in self.gathers:
                g.finish()

    @staticmethod
    def specs(later):
        nl = len(later)
        hbm = pl.BlockSpec(memory_space=pl.ANY)
        return ([_const_spec(w.shape) for w in later], [hbm] * nl,
                tuple(jax.ShapeDtypeStruct((N_DEV,) + w.shape, BF) for w in later),
                [pltpu.VMEM(w.shape, BF) for w in later] + _direct_sems(nl))


PASS_MASKS = ((0, 1, 2, 5, 4, 3, 6, 7), (0, 1, 4, 3, 2, 5, 6, 7))


def _in_proj(x, w_shard, vec_shards, me, later):
    T, D = x.shape
    SH = w_shard.shape[1]
    TM = min(1024, T)
    nT = T // TM
    nl = len(later)
    nv = len(vec_shards)
    widths = [v.shape[1] for v in vec_shards]
    offsets = [sum(widths[:k]) for k in range(nv)]
    vec_shape = (SUBLANES, sum(widths))
    ds = widths[0]
    last = N_DEV - 1
    masks = jnp.asarray(np.array(PASS_MASKS, np.int32).reshape(-1))

    def slot(p, me_ref, masks_ref):
        return me_ref[0] ^ masks_ref[(me_ref[0] & 1) * N_DEV + p]

    def body(me_ref, masks_ref, x_ref, wsh_ref, *rest):
        vsh_refs, rest = rest[:nv], rest[nv:]
        shards, rest = rest[:nl], rest[nl:]
        (z_ref, wt_ref), rest = rest[:2], rest[2:]
        vout_refs, rest = rest[:nv], rest[nv:]
        gathered, rest = rest[:nl], rest[nl:]
        (w_scr, vec_scr, vstage, n1_scr, ga_scr, w_s, w_r, v_s, v_r, v_l), rest = rest[:10], rest[10:]
        stages, sems = rest[:nl], rest[nl:]
        p, i = pl.program_id(0), pl.program_id(1)
        me = _my_index()
        wg = _RelayGather(w_scr, w_s, w_r)
        vg = _Direct(vstage, vec_scr, v_s, v_r, v_l, scatter=False)
        lg = [_TwoLevel(stages[k], gathered[k], sems[0].at[k], sems[1].at[k], sems[2].at[k]) for k in range(nl)]

        def at_pass(k):
            return (p == k) & (i == 0)

        c = lax.axis_index("c")

        @pl.when(at_pass(0))
        def _():
            for ref, off, wd in zip(vsh_refs, offsets, widths):
                vstage[:, off:off + wd] = jnp.broadcast_to(ref[...], (SUBLANES, wd))
            vg.start()
            w_scr[me] = wsh_ref[...].astype(BF)
            wg.send_own(0).start()

            @pl.when(c == 1)
            def _():
                wg.send_own(1).start()

            @pl.when(c == 0)
            def _():
                wg.send_own(2).start()

            vg.finish()
            for j in range(N_DEV):
                ga_scr[:, j * ds:(j + 1) * ds] = vec_scr[j, 0:1, 0:ds]
                for ref, off, wd in zip(vout_refs, offsets, widths):
                    ref[:, j * wd:(j + 1) * wd] = vec_scr[j, 0:1, off:off + wd]

        @pl.when(at_pass(1))
        def _():
            wg.wait_sibling()

        for first, second, landed_first, landed_second in ((1, 2, wg.on_x, wg.on_y), (2, 1, wg.on_y, wg.on_x)):
            mine = c == (1 if first == 1 else 0)

            @pl.when(at_pass(2) & mine)
            def _(second=second, landed_first=landed_first):
                wg.send_own(second).start()
                landed_first()

            @pl.when(at_pass(3) & mine)
            def _(second=second):
                wg.wait_passed(second - 1)

            @pl.when(at_pass(4) & mine)
            def _(landed_second=landed_second):
                landed_second()

            @pl.when(at_pass(5) & mine)
            def _(first=first):
                wg.wait_passed(first - 1)

        @pl.when(at_pass(4))
        def _():
            for k in range(nl):
                stages[k][...] = shards[k][...].astype(BF)
                lg[k].start()

        @pl.when(at_pass(6))
        def _():
            wg.on_diag()

        @pl.when(at_pass(7))
        def _():
            wg.wait_passed(2)

        @pl.when(p == 0)
        def _():
            xv = x_ref[...]
            r1 = lax.rsqrt(jnp.mean(xv * xv, axis=-1, keepdims=True) + EPS)
            n1_scr[i] = (xv * r1 * ga_scr[...]).astype(BF)

        z_ref[...] = _dot(n1_scr[i], w_scr[slot(p, me_ref, masks_ref)]).astype(BF)

        @pl.when(i == 0)
        def _():
            wt_ref[0] = w_scr[slot(p, me_ref, masks_ref)].T

        @pl.when((p == last) & (i == nT - 1))
        def _():
            wg.wait_sends()
            for g in lg:
                g.forward()
            for g in lg:
                g.finish()

    hbm = pl.BlockSpec(memory_space=pl.ANY)
    dma = pltpu.SemaphoreType.DMA
    S = jax.ShapeDtypeStruct
    def whole(shape):
        return pl.BlockSpec(shape, lambda p, i, m, t: (0, 0))

    def once(shape):
        return pl.BlockSpec(shape, lambda p, i, m, t: (0, 0), pipeline_mode=pl.Buffered(1))

    grid_spec = pltpu.PrefetchScalarGridSpec(
        num_scalar_prefetch=2, grid=(N_DEV, nT),
        in_specs=[pl.BlockSpec((TM, D), lambda p, i, m, t: (jnp.where(p == 0, i, nT - 1), 0)), once(w_shard.shape)]
        + [once(v.shape) for v in vec_shards] + [once(w.shape) for w in later],
        out_specs=[pl.BlockSpec((TM, SH), lambda p, i, m, t: (i, slot(p, m, t))),
                   pl.BlockSpec((1, SH, D), lambda p, i, m, t: (slot(p, m, t), 0, 0))]
        + [whole((1, N_DEV * wd)) for wd in widths] + [hbm] * nl,
        scratch_shapes=[pltpu.VMEM((N_DEV, D, SH), BF), pltpu.VMEM((N_DEV,) + vec_shape, F32),
                        pltpu.VMEM(vec_shape, F32), pltpu.VMEM((nT, TM, D), BF), pltpu.VMEM((1, D), F32),
                        dma((8,)), dma((8,)), dma((7,)), dma((7,)), dma]
        + [pltpu.VMEM(w.shape, BF) for w in later] + _direct_sems(nl))
    return pl.pallas_call(
        body, name="a_in_proj", grid_spec=grid_spec,
        out_shape=(S((T, N_DEV * SH), BF), S((N_DEV, SH, D), BF)) + tuple(S((1, N_DEV * wd), F32) for wd in widths)
        + tuple(S((N_DEV,) + w.shape, BF) for w in later),
        compiler_params=_params(("arbitrary", "arbitrary")),
    )(me, masks, x, w_shard, *vec_shards, *later)


def _a_fwd(x, z, ln_g, ln_b, ws, bs, wa_out, g_kv, w_kv, b_kv, rc, rs1, rs2, later):
    T, D = x.shape
    AW = wa_out.shape[0]
    G = ws.shape[0]
    TM = min(512, T)
    nT = T // TM
    nC = TM // CHUNK
    nl = len(later)

    def body(x_ref, u_ref, v_ref, gt_ref, lng_ref, lnb_ref, ws_ref, bs_ref, waout_ref, gkv_ref, wkv_ref, bkv_ref,
             rc_ref, rs1_ref, rs2_ref, *rest):
        shards, rest = rest[:nl], rest[nl:]
        (h1_ref, sv_ref, vhat_ref, rstd_ref, k4_ref, v4_ref, kt_ref, vt_ref), rest = rest[:8], rest[8:]
        gathered, sv_scr, stages, sems = rest[:nl], rest[nl], rest[nl + 1:2 * nl + 1], rest[2 * nl + 1:]
        i = pl.program_id(0)
        riding = _Riding(shards, gathered, stages, sems, nT)
        riding.begin(i)
        xv = x_ref[...]
        u = u_ref[...].astype(F32)
        v = v_ref[...].astype(F32)
        gt = gt_ref[...].astype(F32)
        mu = jnp.mean(v, axis=-1, keepdims=True)
        xc = v - mu
        rstd = lax.rsqrt(jnp.mean(xc * xc, axis=-1, keepdims=True) + EPS)
        vhat = xc * rstd
        vln = (vhat * lng_ref[...] + lnb_ref[...]).astype(BF)
        tri = lax.broadcasted_iota(jnp.int32, (CHUNK, CHUNK), 0) >= lax.broadcasted_iota(jnp.int32, (CHUNK, CHUNK), 1)
        bst = jnp.concatenate([bs_ref[...], jnp.zeros((CHUNK - G, CHUNK), F32)], axis=0).T
        for g in range(G):
            wsm = jnp.where(tri, ws_ref[g], 0.0).astype(BF)
            bias = bst[:, g:g + 1]
            for c in range(nC):
                blk = vln[c * CHUNK:(c + 1) * CHUNK, g * CHUNK:(g + 1) * CHUNK]
                sv_scr[c * CHUNK:(c + 1) * CHUNK, g * CHUNK:(g + 1) * CHUNK] = _dot(wsm, blk) + bias
        sv = sv_scr[...]
        silu, _ = _silu_parts(gt)
        y = (u * sv * silu).astype(BF)
        h1 = xv + _dot(y, waout_ref[...])
        h1_ref[...] = h1
        sv_ref[...] = sv.astype(BF)
        vhat_ref[...] = vhat.astype(BF)
        rstd_ref[...] = jnp.broadcast_to(rstd, rstd_ref.shape)
        rkv = lax.rsqrt(jnp.mean(h1 * h1, axis=-1, keepdims=True) + EPS)
        nkv = (h1 * rkv * gkv_ref[...]).astype(BF)
        kv = _dot(nkv, wkv_ref[...]) + bkv_ref[...]
        k_rot = _rot(kv[:, :LANES], rc_ref[...], rs1_ref[...], rs2_ref[...])
        for src, ref, tref in ((k_rot, k4_ref, kt_ref), (kv[:, LANES:], v4_ref, vt_ref)):
            t4 = _split4(src)
            ref[...] = t4.astype(BF)
            for c in range(nC):
                for b in range(4):
                    blk = t4[c * CHUNK:(c + 1) * CHUNK, b * LANES:(b + 1) * LANES]
                    tref[c, b * LANES:(b + 1) * LANES, :] = blk.T.astype(BF)
        riding.end(i)

    row = functools.partial(_row_spec, TM)
    zcol = [pl.BlockSpec((TM, AW), functools.partial(lambda k, i: (i, k), k)) for k in range(3)]
    tr = pl.BlockSpec((nC, 4 * LANES, CHUNK), lambda i: (i, 0, 0))
    r_in, r_out, r_shape, r_scratch = _Riding.specs(later)
    S = jax.ShapeDtypeStruct
    return pl.pallas_call(
        body, name="a_fwd", grid=(nT,),
        in_specs=[row(D)] + zcol + [_const_spec((1, AW)), _const_spec((1, AW)),
                  _const_spec(ws.shape), _const_spec(bs.shape), _const_spec(wa_out.shape), _const_spec((1, D)),
                  _const_spec(w_kv.shape), _const_spec((1, 2 * LANES)), row(LANES), row(LANES), row(LANES)] + r_in,
        out_specs=[row(D), row(AW), row(AW), row(LANES), row(4 * LANES), row(4 * LANES), tr, tr] + r_out,
        out_shape=(S((T, D), F32), S((T, AW), BF), S((T, AW), BF), S((T, LANES), F32),
                   S((T, 4 * LANES), BF), S((T, 4 * LANES), BF),
                   S((T // CHUNK, 4 * LANES, CHUNK), BF), S((T // CHUNK, 4 * LANES, CHUNK), BF)) + r_shape,
        scratch_shapes=[pltpu.VMEM((TM, AW), F32)] + r_scratch,
        compiler_params=_params(("arbitrary",)),
    )(x, z, z, z, ln_g, ln_b, ws, bs, wa_out, g_kv, w_kv, b_kv, rc, rs1, rs2, *later)


def _b_fwd(h1, g_b, wb_in, bq, rc, rs1, rs2, k4, vt, sinks, wb_out, g_f, target):
    T, D = h1.shape
    BW = wb_out.shape[0]
    SH = wb_in.shape[2]
    TM = min(512, T)
    nC = TM // CHUNK
    nP = BW // LANES

    def body(h1_ref, gb_ref, wbin_ref, bq_ref, rc_ref, rs1_ref, rs2_ref, k4_ref, k4p_ref, vt_ref, vtp_ref, sinks_ref,
             wbout_ref, gf_ref, tgt_ref, q_ref, g2_ref, o_ref, dh2_ref, dh2b_ref, loss_ref, dgf_ref, z_scr, o_scr):
        i = pl.program_id(0)
        sink = _sink_tile(sinks_ref)

        @pl.when(i == 0)
        def _():
            loss_ref[...] = jnp.zeros_like(loss_ref)
            dgf_ref[...] = jnp.zeros_like(dgf_ref)

        h1v = h1_ref[...]
        r2 = lax.rsqrt(jnp.mean(h1v * h1v, axis=-1, keepdims=True) + EPS)
        n2 = (h1v * r2 * gb_ref[...]).astype(BF)
        for j in range(N_DEV):
            z_scr[:, j * SH:(j + 1) * SH] = _dot(n2, wbin_ref[j])
        c_t, s1_t, s2_t = rc_ref[...], rs1_ref[...], rs2_ref[...]
        for p in range(nP):
            cols = slice(p * LANES, (p + 1) * LANES)
            qp = _rot(z_scr[:, cols] + bq_ref[:, cols], c_t, s1_t, s2_t) * (HEAD_DIM ** -0.5)
            q_ref[:, cols] = qp.astype(BF)
        g2 = z_scr[:, BW:]
        g2_ref[...] = g2.astype(BF)
        upper = _upper()
        for c in range(nC):
            ci = i * nC + c
            rows = slice(c * CHUNK, (c + 1) * CHUNK)
            qc = q_ref[rows, :]
            for h in range(2):
                st = _dot_nt(_band_rows(k4_ref, k4p_ref, c, h), _stack_pairs(qc, h))
                fa, fb = _fold(st, upper, ci > 0)
                pa, _ = _softmax_sink(fa, sink[2 * h:2 * h + 1, :])
                pb, _ = _softmax_sink(fb, sink[2 * h + 1:2 * h + 2, :])
                ot = _dot(_band_cols(vt_ref, vtp_ref, c, h), _unfold(pa, pb, upper).astype(BF))
                for j in range(4):
                    o_scr[rows, (h * 4 + j) * LANES:(h * 4 + j + 1) * LANES] = ot[:, j * CHUNK:(j + 1) * CHUNK].T
        o = o_scr[...]
        o_ref[...] = o.astype(BF)
        silu, _ = _silu_parts(g2)
        h2 = h1v + _dot((o * silu).astype(BF), wbout_ref[...])
        rf = lax.rsqrt(jnp.mean(h2 * h2, axis=-1, keepdims=True) + EPS)
        xh = h2 * rf
        gf = gf_ref[...]
        err = xh * gf - tgt_ref[...]
        dyf = err * (1.0 / D)
        dh2 = _rms_bwd(dyf, xh, rf, gf)
        dh2_ref[...] = dh2
        dh2b_ref[...] = dh2.astype(BF)
        loss_ref[...] += 0.5 * jnp.sum(jnp.mean(err * err, axis=-1, keepdims=True), axis=0, keepdims=True)
        dgf_ref[...] += jnp.sum(dyf * xh, axis=0, keepdims=True)

    row = functools.partial(_row_spec, TM)
    rows_tile, rows_before, cols_tile, cols_before = _band_specs(TM)
    S = jax.ShapeDtypeStruct
    return pl.pallas_call(
        body, name="b_fwd", grid=(T // TM,),
        in_specs=[row(D), _const_spec((1, D)), _const_spec(wb_in.shape), _const_spec((1, BW)), row(LANES), row(LANES),
                  row(LANES), rows_tile, rows_before, cols_tile, cols_before, pl.BlockSpec(memory_space=pltpu.SMEM),
                  _const_spec(wb_out.shape), _const_spec((1, D)), row(D)],
        out_specs=[row(BW), row(BW), row(BW), row(D), row(D), _acc_spec((1, 1)), _acc_spec((1, D))],
        out_shape=(S((T, BW), BF), S((T, BW), BF), S((T, BW), BF), S((T, D), F32), S((T, D), BF), S((1, 1), F32),
                   S((1, D), F32)),
        scratch_shapes=[pltpu.VMEM((TM, 2 * BW), F32), pltpu.VMEM((TM, BW), F32)],
        compiler_params=_params(("arbitrary",)),
    )(h1, g_b, wb_in, bq, rc, rs1, rs2, k4, k4, vt, vt, sinks, wb_out, g_f, target)


def _b_bwd(dh2, h1, q, g2, o, k4, v4, kt, sinks, wb_out, wb_in, g_b, rc, rs1, rs2):
    T, D = h1.shape
    BW = wb_out.shape[0]
    SH = wb_in.shape[2]
    TM = min(512, T)
    nT = T // TM
    nC = TM // CHUNK
    nP = BW // LANES

    def body(dh2_ref, h1_ref, q_ref, g2_ref, o_ref, k4_ref, k4p_ref, v4_ref, v4p_ref, kt_ref, ktp_ref, sinks_ref,
             wbout_ref, wbin_ref, gb_ref, rc_ref, rs1_ref, rs2_ref,
             dh1_ref, dz2_ref, n2_ref, y2_ref, dk_ref, dv_ref, dbq_ref, dgb_ref, dsink_ref, do_scr, dq_scr, dsacc_scr):
        i = pl.program_id(0)
        sink = _sink_tile(sinks_ref)

        @pl.when(i == 0)
        def _():
            dk_ref[...] = jnp.zeros_like(dk_ref)
            dv_ref[...] = jnp.zeros_like(dv_ref)
            dbq_ref[...] = jnp.zeros_like(dbq_ref)
            dgb_ref[...] = jnp.zeros_like(dgb_ref)
            dsacc_scr[...] = jnp.zeros_like(dsacc_scr)

        dh2 = dh2_ref[...]
        dy2 = _dot_nt(dh2.astype(BF), wbout_ref[...])
        silu, dsilu = _silu_parts(g2_ref[...].astype(F32))
        do_scr[...] = (dy2 * silu).astype(BF)
        dy2, silu, dsilu = dy2.astype(BF), silu.astype(BF), dsilu.astype(BF)
        ob = o_ref[...]
        y2_ref[...] = (ob * silu).T
        dz2_ref[:, BW:] = dy2 * ob * dsilu
        upper = _upper()
        lo = _lane_lo((2 * CHUNK, LANES))
        for c in range(nC):
            ci = i * nC + c
            rows = slice(c * CHUNK, (c + 1) * CHUNK)
            pci = jnp.maximum(ci - 1, 0)
            prev = pl.multiple_of(pci * CHUNK, CHUNK)
            cur = pl.multiple_of(ci * CHUNK, CHUNK)
            qc = q_ref[rows, :]
            doc = do_scr[rows, :]
            dkb = jnp.zeros((2 * CHUNK, LANES), F32)
            dvb = jnp.zeros((2 * CHUNK, LANES), F32)
            for h in range(2):
                qs = _stack_pairs(qc, h)
                dos = _stack_pairs(doc, h)
                fa, fb = _fold(_dot_nt(_band_rows(k4_ref, k4p_ref, c, h), qs), upper, ci > 0)
                dfa, dfb = _fold(_dot_nt(_band_rows(v4_ref, v4p_ref, c, h), dos), upper)
                folded = []
                for k, (f, df) in enumerate(((fa, dfa), (fb, dfb))):
                    p, ps = _softmax_sink(f, sink[2 * h + k:2 * h + k + 1, :])
                    delta = jnp.sum(p * df, axis=0, keepdims=True)
                    dsacc_scr[2 * h + k:2 * h + k + 1, :] -= ps * delta
                    folded.append((p, p * (df - delta)))
                pt = _unfold(folded[0][0], folded[1][0], upper).astype(BF)
                dst = _unfold(folded[0][1], folded[1][1], upper).astype(BF)
                dqt = _dot(_band_cols(kt_ref, ktp_ref, c, h), dst)
                for j in range(4):
                    dq_scr[rows, (h * 4 + j) * LANES:(h * 4 + j + 1) * LANES] = dqt[:, j * CHUNK:(j + 1) * CHUNK].T
                for acc_name, g in (("k", _dot(dst, qs)), ("v", _dot(pt, dos))):
                    a, b = g[:2 * CHUNK], g[2 * CHUNK:]
                    if h == 0:
                        part = jnp.where(lo, a + pltpu.roll(b, HEAD_DIM, 1), 0.0)
                    else:
                        part = jnp.where(lo, 0.0, pltpu.roll(a, HEAD_DIM, 1) + b)
                    if acc_name == "k":
                        dkb += part
                    else:
                        dvb += part
            dk_ref[pl.ds(prev, CHUNK), :] += dkb[:CHUNK]
            dk_ref[pl.ds(cur, CHUNK), :] += dkb[CHUNK:]
            dv_ref[pl.ds(prev, CHUNK), :] += dvb[:CHUNK]
            dv_ref[pl.ds(cur, CHUNK), :] += dvb[CHUNK:]
        c_t, s1_t, s2_t = rc_ref[...], rs1_ref[...], rs2_ref[...]
        for p in range(nP):
            cols = slice(p * LANES, (p + 1) * LANES)
            dqp = _rot_bwd(dq_scr[:, cols] * (HEAD_DIM ** -0.5), c_t, s1_t, s2_t)
            dbq_ref[:, cols] += jnp.sum(dqp, axis=0, keepdims=True)
            dz2_ref[:, cols] = dqp.astype(BF)
        h1v = h1_ref[...]
        r2 = lax.rsqrt(jnp.mean(h1v * h1v, axis=-1, keepdims=True) + EPS)
        xh = h1v * r2
        gb = gb_ref[...]
        n2_ref[...] = (xh * gb).astype(BF).T
        dn2 = None
        for j in range(N_DEV):
            part = _dot_nt(dz2_ref[:, j * SH:(j + 1) * SH], wbin_ref[j])
            dn2 = part if dn2 is None else dn2 + part
        dgb_ref[...] += jnp.sum(dn2 * xh, axis=0, keepdims=True)
        dh1_ref[...] = dh2 + _rms_bwd(dn2, xh, r2, gb)

        @pl.when(i == nT - 1)
        def _():
            lane = lax.broadcasted_iota(jnp.int32, dsink_ref.shape, 1)
            tot = jnp.zeros(dsink_ref.shape, F32)
            for j in range(4):
                tot += jnp.where(lane == j, jnp.sum(dsacc_scr[:, j * CHUNK:(j + 1) * CHUNK], axis=1, keepdims=True), 0.0)
            dsink_ref[...] = tot

    row = functools.partial(_row_spec, TM)
    rows_tile, rows_before, cols_tile, cols_before = _band_specs(TM)
    S = jax.ShapeDtypeStruct
    return pl.pallas_call(
        body, name="b_bwd", grid=(T // TM,),
        in_specs=[row(D), row(D), row(BW), row(BW), row(BW), rows_tile, rows_before, rows_tile, rows_before,
                  cols_tile, cols_before, pl.BlockSpec(memory_space=pltpu.SMEM), _const_spec(wb_out.shape), _const_spec(wb_in.shape),
                  _const_spec((1, D)), row(LANES), row(LANES), row(LANES)],
        out_specs=[row(D), row(2 * BW), _col_spec(TM, D), _col_spec(TM, BW), _acc_spec((T, LANES)),
                   _acc_spec((T, LANES)), _acc_spec((1, BW)), _acc_spec((1, D)), _acc_spec((4, LANES))],
        out_shape=(S((T, D), F32), S((T, 2 * BW), BF), S((D, T), BF), S((BW, T), BF), S((T, LANES), F32),
                   S((T, LANES), F32), S((1, BW), F32), S((1, D), F32), S((4, LANES), F32)),
        scratch_shapes=[pltpu.VMEM((TM, BW), BF), pltpu.VMEM((TM, BW), F32), pltpu.VMEM((4, 4 * CHUNK), F32)],
        compiler_params=_params(("arbitrary",)),
    )(dh2, h1, q, g2, o, k4, k4, v4, v4, kt, kt, sinks, wb_out, wb_in, g_b, rc, rs1, rs2)


def _a_bwd(dh1p, dk, dv, h1, g_kv, w_kv, wa_out, ws, ln_g, ln_b, z, sv, vhat, rstd, rc, rs1, rs2, ready):
    T, D = h1.shape
    AW = wa_out.shape[0]
    G = ws.shape[0]
    TM = min(256, T)
    nT = T // TM
    nC = TM // CHUNK
    nr = len(ready)

    def body(dh1p_ref, dk_ref, dv_ref, h1_ref, gkv_ref, wkv_ref, waout_ref, ws_ref, lng_ref,
             lnb_ref, u_ref, gt_ref, sv_ref, vhat_ref, rstd_ref, rc_ref, rs1_ref, rs2_ref, *rest):
        ready_refs, rest = rest[:nr], rest[nr:]
        (dz_ref, gwo_ref, gwk_ref, dh1f_ref, dgkv_ref, dbkv_ref, dlng_ref, dlnb_ref,
         dws_ref, dbs_ref), rest = rest[:10], rest[10:]
        recv_refs, (dsv_scr, dvln_scr, acco_scr, acck_scr, ssem, rsem, lsem) = rest[:nr], rest[nr:]
        i = pl.program_id(0)
        exchanges = [_Direct(ready_refs[k], recv_refs[k], ssem.at[k], rsem.at[k], lsem.at[k], scatter=True)
                     for k in range(nr)]

        @pl.when(i == 0)
        def _():
            for e in exchanges:
                e.start()
            for r in (dgkv_ref, dbkv_ref, dlng_ref, dlnb_ref, dws_ref, dbs_ref, acco_scr, acck_scr):
                r[...] = jnp.zeros_like(r)

        dk_pre = _rot_bwd(dk_ref[...], rc_ref[...], rs1_ref[...], rs2_ref[...])
        dkv = jnp.concatenate([dk_pre, dv_ref[...]], axis=1)
        dbkv_ref[...] += jnp.sum(dkv, axis=0, keepdims=True)
        dkv_b = dkv.astype(BF)
        h1v = h1_ref[...]
        rkv = lax.rsqrt(jnp.mean(h1v * h1v, axis=-1, keepdims=True) + EPS)
        xh_kv = h1v * rkv
        gkv = gkv_ref[...]
        acck_scr[...] += _dot((xh_kv * gkv).astype(BF).T, dkv_b)
        dnkv = _dot_nt(dkv_b, wkv_ref[...])
        dgkv_ref[...] += jnp.sum(dnkv * xh_kv, axis=0, keepdims=True)
        dh1 = dh1p_ref[...] + _rms_bwd(dnkv, xh_kv, rkv, gkv)
        dh1_b = dh1.astype(BF)
        dh1f_ref[...] = dh1
        dy = _dot_nt(dh1_b, waout_ref[...]).astype(BF)
        silu, dsilu = _silu_parts(gt_ref[...].astype(F32))
        silu, dsilu = silu.astype(BF), dsilu.astype(BF)
        ub, svb = u_ref[...], sv_ref[...]
        us = ub * silu
        dys = dy * svb
        acco_scr[...] += _dot((us * svb).T, dh1_b)
        dz_ref[:, :AW] = dys * silu
        dz_ref[:, 2 * AW:] = dys * ub * dsilu
        dsv_scr[...] = dy * us
        vhat_v = vhat_ref[...].astype(F32)
        lng = lng_ref[...]
        vln_b = (vhat_v * lng + lnb_ref[...]).astype(BF)
        tri = lax.broadcasted_iota(jnp.int32, (CHUNK, CHUNK), 0) >= lax.broadcasted_iota(jnp.int32, (CHUNK, CHUNK), 1)
        lane = lax.broadcasted_iota(jnp.int32, (CHUNK, LANES), 1)
        dbs = jnp.zeros((CHUNK, LANES), F32)
        for g in range(G):
            wsm = jnp.where(tri, ws_ref[g], 0.0).astype(BF)
            cols = slice(g * CHUNK, (g + 1) * CHUNK)
            dws_g = None
            for c in range(nC):
                rows = slice(c * CHUNK, (c + 1) * CHUNK)
                dsv_cg = dsv_scr[rows, cols]
                dvln_scr[rows, cols] = _dot_tn(wsm, dsv_cg)
                part = _dot_nt(dsv_cg, vln_b[rows, cols])
                dws_g = part if dws_g is None else dws_g + part
                dbs += jnp.where(lane == g, jnp.sum(dsv_cg.astype(F32), axis=-1, keepdims=True), 0.0)
            dws_ref[g] += jnp.where(tri, dws_g, 0.0)
        dbs_ref[...] += dbs
        dvln = dvln_scr[...]
        dlng_ref[...] += jnp.sum(dvln * vhat_v, axis=0, keepdims=True)
        dlnb_ref[...] += jnp.sum(dvln, axis=0, keepdims=True)
        a = dvln * lng
        dvv = rstd_ref[:, 0:1] * (a - jnp.mean(a, axis=-1, keepdims=True)
                                  - vhat_v * jnp.mean(a * vhat_v, axis=-1, keepdims=True))
        dz_ref[:, AW:2 * AW] = dvv.astype(BF)

        @pl.when(i == nT - 1)
        def _():
            for j in range(N_DEV):
                gwo_ref[j] = acco_scr[j * (AW // N_DEV):(j + 1) * (AW // N_DEV)].astype(BF)
                gwk_ref[j] = acck_scr[j * (D // N_DEV):(j + 1) * (D // N_DEV)].astype(BF)
            for e in exchanges:
                e.finish()

    row = functools.partial(_row_spec, TM)
    hbm = pl.BlockSpec(memory_space=pl.ANY)
    S = jax.ShapeDtypeStruct
    gwo_shape, gwk_shape = (N_DEV, AW // N_DEV, D), (N_DEV, D // N_DEV, 2 * LANES)
    return pl.pallas_call(
        body, name="a_bwd", grid=(nT,),
        in_specs=[row(D), row(LANES), row(LANES), row(D), _const_spec((1, D)), _const_spec(w_kv.shape),
                  _const_spec(wa_out.shape), _const_spec(ws.shape),
                  _const_spec((1, AW)), _const_spec((1, AW)), pl.BlockSpec((TM, AW), lambda i: (i, 0)),
                  pl.BlockSpec((TM, AW), lambda i: (i, 2)), row(AW), row(AW), row(LANES),
                  row(LANES), row(LANES), row(LANES)] + [hbm] * nr,
        out_specs=[row(3 * AW), _const_spec(gwo_shape), _const_spec(gwk_shape), row(D),
                   _acc_spec((1, D)), _acc_spec((1, 2 * LANES)), _acc_spec((1, AW)),
                   _acc_spec((1, AW)), _acc_spec(ws.shape), _acc_spec((CHUNK, LANES))] + [hbm] * nr,
        out_shape=(S((T, 3 * AW), BF), S(gwo_shape, BF), S(gwk_shape, BF), S((T, D), F32),
                   S((1, D), F32), S((1, 2 * LANES), F32), S((1, AW), F32), S((1, AW), F32),
                   S(ws.shape, F32), S((CHUNK, LANES), F32)) + tuple(S(r.shape, r.dtype) for r in ready),
        scratch_shapes=[pltpu.VMEM((TM, AW), BF), pltpu.VMEM((TM, AW), F32), pltpu.VMEM((AW, D), F32),
                        pltpu.VMEM((D, 2 * LANES), F32)] + _direct_sems(nr),
        compiler_params=_params(("arbitrary",)),
    )(dh1p, dk, dv, h1, g_kv, w_kv, wa_out, ws, ln_g, ln_b, z, z, sv, vhat, rstd, rc, rs1, rs2, *ready)


def _a_in_bwd(dz, wa_in_t, x, dh1, g_a, ready):
    T, D = x.shape
    TM = min(512, T)
    nT = T // TM
    nr = len(ready)

    def body(dz_ref, wain_ref, x_ref, dh1_ref, ga_ref, *rest):
        ready_refs, (dx_ref, n1_ref, dga_ref), rest = rest[:nr], rest[nr:nr + 3], rest[nr + 3:]
        recv_refs, (ssem, rsem, lsem) = rest[:nr], rest[nr:]
        i = pl.program_id(0)
        exchanges = [_Direct(ready_refs[k], recv_refs[k], ssem.at[k], rsem.at[k], lsem.at[k], scatter=True)
                     for k in range(nr)]

        @pl.when(i == 0)
        def _():
            for e in exchanges:
                e.start()
            dga_ref[...] = jnp.zeros_like(dga_ref)

        xv = x_ref[...]
        r1 = lax.rsqrt(jnp.mean(xv * xv, axis=-1, keepdims=True) + EPS)
        xh = xv * r1
        ga = ga_ref[...]
        n1_ref[...] = (xh * ga).astype(BF).T
        dn1 = _dot(dz_ref[...], wain_ref[...])
        dga_ref[...] += jnp.sum(dn1 * xh, axis=0, keepdims=True)
        dx_ref[...] = dh1_ref[...] + _rms_bwd(dn1, xh, r1, ga)

        @pl.when(i == nT - 1)
        def _():
            for e in exchanges:
                e.finish()

    row = functools.partial(_row_spec, TM)
    hbm = pl.BlockSpec(memory_space=pl.ANY)
    S = jax.ShapeDtypeStruct
    return pl.pallas_call(
        body, name="a_in_bwd", grid=(nT,),
        in_specs=[row(dz.shape[1]), _const_spec(wa_in_t.shape), row(D), row(D), _const_spec((1, D))] + [hbm] * nr,
        out_specs=[row(D), _col_spec(TM, D), _acc_spec((1, D))] + [hbm] * nr,
        out_shape=(S((T, D), F32), S((D, T), BF), S((1, D), F32)) + tuple(S(r.shape, r.dtype) for r in ready),
        scratch_shapes=_direct_sems(nr),
        compiler_params=_params(("arbitrary",)),
    )(dz, wa_in_t, x, dh1, g_a, *ready)


def _wgrad(problems, name, bt=512):
    T = problems[0][0].shape[1]
    BT = min(bt, T)
    nt = T // BT
    n = len(problems)
    dims = [(at.shape[0], b.shape[1] // nblk, nblk) for at, b, nblk in problems]

    def body(*refs):
        ins, outs, accs = refs[:2 * n], refs[2 * n:3 * n], refs[3 * n:]
        t = pl.program_id(0)

        @pl.when(t == 0)
        def _():
            for acc in accs:
                acc[...] = jnp.zeros_like(acc)

        for k in range(n):
            accs[k][...] += _dot(ins[2 * k][...], ins[2 * k + 1][...])

        @pl.when(t == nt - 1)
        def _():
            for k, (_, N, nblk) in enumerate(dims):
                for j in range(nblk):
                    outs[k][j] = accs[k][:, j * N:(j + 1) * N].astype(BF)

    in_specs = []
    for at, b, _ in problems:
        in_specs += [pl.BlockSpec((at.shape[0], BT), lambda t: (0, t)), pl.BlockSpec((BT, b.shape[1]), lambda t: (t, 0))]
    return pl.pallas_call(
        body, name=name, grid=(nt,), in_specs=in_specs,
        out_specs=[pl.BlockSpec((nblk, K, N), lambda t: (0, 0, 0)) for K, N, nblk in dims],
        out_shape=[jax.ShapeDtypeStruct((nblk, K, N), BF) for K, N, nblk in dims],
        scratch_shapes=[pltpu.VMEM((K, nblk * N), F32) for K, N, nblk in dims],
        compiler_params=_params(("arbitrary",)),
    )(*[operand for at, b, _ in problems for operand in (at, b)])


def _wgrad_exchange(a, b, me, small, name):
    K, T = a.shape
    N = b.shape[1] // N_DEV
    BT = T
    nt = T // BT
    last = N_DEV - 1
    n_chip = N_DEV // 2

    def far_of(k, core):
        return jnp.where((core == 0) & ((k == 1) | (k == 2)), k, n_chip - 1 - k)

    def block_of(s, me_i):
        k, odd = s // 2, s % 2
        core = me_i & 1
        return me_i ^ ((far_of(k, jnp.where(odd == 1, core, 1 - core)) << 1) | (1 - odd))

    H = K // 2

    def body(me_ref, a_ref, b_ref, small_ref, recv_ref, full_ref, *scratch):
        (acc, dstage, istage, half, relay, d_s, d_r, i_s, i_r, r_s, r_r, lsem, parts_scr, red_scr, e_s, e_r, e_l, g_s,
         g_r, g_l) = scratch
        s, t = pl.program_id(0), pl.program_id(1)
        x, y, c = (lax.axis_index(ax) for ax in AXES)
        ex = [_Direct(small_ref, parts_scr, e_s, e_r, e_l, scatter=True)]
        regather = _TwoLevel(red_scr, full_ref, g_s, g_r, g_l)

        def to_sibling(k, slot):
            return pltpu.make_async_remote_copy(src_ref=dstage.at[slot], dst_ref=half.at[k], send_sem=d_s.at[k],
                                                recv_sem=d_r.at[k], device_id=(x, y, 1 - c), device_id_type=MESH)

        def to_chip(k, slot):
            over_x = far_of(k, c) == 2
            px, py = jnp.where(over_x, 1 - x, x), jnp.where(over_x, y, 1 - y)
            return pltpu.make_async_remote_copy(src_ref=istage.at[slot], dst_ref=recv_ref.at[jnp.where(over_x, 1, 2)],
                                                send_sem=i_s.at[k], recv_sem=i_r.at[k], device_id=(px, py, c),
                                                device_id_type=MESH)

        def to_relay(j, slot):
            to = (1 - x, y, c) if j == 0 else (x, 1 - y, c)
            return pltpu.make_async_remote_copy(src_ref=istage.at[slot, pl.ds(j * H, H)], dst_ref=relay.at[j],
                                                send_sem=r_s.at[j], recv_sem=r_r.at[j], device_id=to,
                                                device_id_type=MESH)

        @pl.when((s == 0) & (t == 0))
        def _():
            for e in ex:
                e.start()

        acc[...] = _dot(a_ref[...], b_ref[...])

        @pl.when(t == nt - 1)
        def _():
            k = lax.div(s, 2)
            slot = lax.rem(k, 2)

            @pl.when(lax.rem(s, 2) == 0)
            def _():
                @pl.when(k >= 2)
                def _():
                    to_sibling(k - 2, slot).wait_send()

                dstage[slot] = acc[...].astype(BF)
                to_sibling(k, slot).start()

            @pl.when(lax.rem(s, 2) == 1)
            def _():
                to_sibling(k, slot).wait_recv()
                pair = acc[...] + half[k].astype(F32)

                @pl.when(k == 0)
                def _():
                    istage[slot] = pair.astype(BF)
                    for j in range(2):
                        to_relay(j, slot).start()

                @pl.when(k == 1)
                def _():
                    for j in range(2):
                        to_relay(j, slot).wait_recv()

                @pl.when(k == 2)
                def _():
                    for j in range(2):
                        to_relay(j, slot).wait_send()

                @pl.when(k == n_chip - 1)
                def _():
                    to_chip(1, slot).wait_send()
                    istage[slot] = pair.astype(BF)

                @pl.when((k == 1) | (k == 2))
                def _():
                    over_x = far_of(k, c) == 2
                    istage[slot, 0:H] = (pair[:H] + jnp.where(over_x, 0.0, relay[0].astype(F32))).astype(BF)
                    istage[slot, H:K] = (pair[H:] + jnp.where(over_x, relay[1].astype(F32), 0.0)).astype(BF)
                    to_chip(k, slot).start()

            @pl.when(s == last)
            def _():
                own = pltpu.make_async_copy(istage.at[slot], recv_ref.at[0], lsem)
                own.start()
                to_chip(2, 0).wait_send()
                to_sibling(n_chip - 2, 0).wait_send()
                to_sibling(n_chip - 1, 1).wait_send()
                for kk in (1, 2):
                    to_chip(kk, 0).wait_recv()
                own.wait()
                for e in ex:
                    e.finish()
                total = parts_scr[0]
                for dev in range(1, N_DEV):
                    total = total + parts_scr[dev]
                red_scr[...] = total
                regather.start()
                regather.forward()
                regather.finish()

    hbm = pl.BlockSpec(memory_space=pl.ANY)
    dma = pltpu.SemaphoreType.DMA
    grid_spec = pltpu.PrefetchScalarGridSpec(
        num_scalar_prefetch=1, grid=(N_DEV, nt),
        in_specs=[pl.BlockSpec((K, BT), lambda s, t, me_ref: (0, t), pipeline_mode=pl.Buffered(1)),
                  pl.BlockSpec((BT, N), lambda s, t, me_ref: (t, block_of(s, me_ref[0]))), hbm],
        out_specs=[hbm, hbm],
        scratch_shapes=[pltpu.VMEM((K, N), F32), pltpu.VMEM((2, K, N), BF), pltpu.VMEM((2, K, N), BF),
                        pltpu.VMEM((n_chip, K, N), BF), pltpu.VMEM((2, H, N), BF), dma((n_chip,)), dma((n_chip,)),
                        dma((n_chip - 1,)), dma((n_chip - 1,)), dma((2,)), dma((2,)), dma,
                        pltpu.VMEM(small.shape, F32), pltpu.VMEM(small.shape[1:], F32),
                        dma((last,)), dma((last,)), dma, dma((last,)), dma((last,)), dma])
    return pl.pallas_call(
        body, name=name, grid_spec=grid_spec,
        out_shape=[jax.ShapeDtypeStruct((n_chip - 1, K, N), BF), jax.ShapeDtypeStruct(small.shape, F32)],
        compiler_params=_params(("arbitrary", "arbitrary")),
    )(me, a, b, small)


def _my_index():
    return 4 * lax.axis_index("x") + 2 * lax.axis_index("y") + lax.axis_index("c")


def _peer(mask):
    x, y, c = (lax.axis_index(a) for a in AXES)
    return (x ^ ((mask >> 2) & 1), y ^ ((mask >> 1) & 1), c ^ (mask & 1))


def _dev_index(p):
    return 4 * p[0] + 2 * p[1] + p[2]


class _Direct:
    def __init__(self, src, dst, send_sems, recv_sems, local_sem, scatter):
        me = _my_index()
        self.own = pltpu.make_async_copy(src.at[me] if scatter else src, dst.at[me], local_sem)
        self.sends, self.recvs = [], []
        for k in range(1, N_DEV):
            p = _peer(k)
            pi = _dev_index(p)
            sems = dict(send_sem=send_sems.at[k - 1], recv_sem=recv_sems.at[k - 1], device_id=p, device_id_type=MESH)
            self.sends.append(pltpu.make_async_remote_copy(src_ref=src.at[pi] if scatter else src, dst_ref=dst.at[me],
                                                           **sems))
            self.recvs.append(pltpu.make_async_remote_copy(src_ref=src.at[me] if scatter else src, dst_ref=dst.at[pi],
                                                           **sems))

    def start(self):
        self.own.start()
        for cp in self.sends:
            cp.start()

    def finish(self):
        for cp in self.sends:
            cp.wait_send()
        for cp in self.recvs:
            cp.wait_recv()
        self.own.wait()


class _TwoLevel:
    def __init__(self, src, dst, send_sems, recv_sems, local_sem, own=True):
        x, y, c = (lax.axis_index(a) for a in AXES)
        self.me, self.sibling = (x, y, c), (x, y, 1 - c)
        self.chips = [(1 - x, y), (x, 1 - y), (1 - x, 1 - y)]
        self.src, self.dst, self.send_sems, self.recv_sems = src, dst, send_sems, recv_sems
        self.own = pltpu.make_async_copy(src, dst.at[_dev_index(self.me)], local_sem) if own else None

    def _copy(self, k, block, to, from_src=False):
        slot = self.dst.at[_dev_index(block)]
        return pltpu.make_async_remote_copy(src_ref=self.src if from_src else slot, dst_ref=slot,
                                            send_sem=self.send_sems.at[k], recv_sem=self.recv_sems.at[k],
                                            device_id=to, device_id_type=MESH)

    def _firsts(self):
        c = self.me[2]
        return [self._copy(0, self.me, self.sibling, True)] + [self._copy(1 + j, self.me, (*chip, c), True)
                                                               for j, chip in enumerate(self.chips)]

    def _passed(self):
        c = self.me[2]
        return [self._copy(4 + j, (*chip, c), self.sibling) for j, chip in enumerate(self.chips)]

    def start(self):
        if self.own is not None:
            self.own.start()
        for cp in self._firsts():
            cp.start()

    def wait_sibling(self):
        self._copy(0, self.sibling, self.me).wait_recv()

    def wait_chip_and_forward(self, j):
        self._copy(1 + j, (*self.chips[j], self.me[2]), self.me).wait_recv()
        self._passed()[j].start()

    def wait_passed(self, j):
        self._copy(4 + j, (*self.chips[j], 1 - self.me[2]), self.me).wait_recv()

    def wait_sends(self):
        for cp in self._firsts() + self._passed():
            cp.wait_send()
        if self.own is not None:
            self.own.wait()

    def forward(self):
        for j in range(3):
            self.wait_chip_and_forward(j)

    def finish(self):
        self.wait_sibling()
        for j in range(3):
            self.wait_passed(j)
        self.wait_sends()


class _RelayGather:
    def __init__(self, dst, send_sems, recv_sems):
        x, y, c = (lax.axis_index(a) for a in AXES)
        self.c = c
        self.sib, self.xn, self.yn, self.dg = (x, y, 1 - c), (1 - x, y, c), (x, 1 - y, c), (1 - x, 1 - y, c)
        self.me = (x, y, c)
        self.dst, self.send_sems, self.recv_sems = dst, send_sems, recv_sems
        self.half = dst.shape[1] // 2

    def _slot(self, dev, part=None):
        i = _dev_index(dev)
        if part is None:
            return self.dst.at[i]
        return self.dst.at[i, pl.ds(part * self.half, self.half)]

    def _copy(self, k, dev, to, part=None):
        ref = self._slot(dev, part)
        return pltpu.make_async_remote_copy(src_ref=ref, dst_ref=ref, send_sem=self.send_sems.at[k],
                                            recv_sem=self.recv_sems.at[k], device_id=to, device_id_type=MESH)

    def _other(self, dev):
        return (dev[0], dev[1], 1 - self.c)

    def start(self):
        for k, to in enumerate((self.sib, self.xn, self.yn)):
            self._copy(k, self.me, to).start()

    def send_own(self, k):
        return self._copy(k, self.me, (self.sib, self.xn, self.yn)[k])

    def wait_sibling(self):
        self._copy(0, self.sib, self.me).wait_recv()

    def on_x(self):
        self._copy(1, self.xn, self.me).wait_recv()
        self._copy(3, self.xn, self.yn, part=0).start()
        self._copy(5, self.xn, self.sib).start()

    def on_y(self):
        self._copy(2, self.yn, self.me).wait_recv()
        self._copy(4, self.yn, self.xn, part=1).start()
        self._copy(6, self.yn, self.sib).start()

    def on_diag(self):
        self._copy(3, self.dg, self.me, part=0).wait_recv()
        self._copy(4, self.dg, self.me, part=1).wait_recv()
        self._copy(7, self.dg, self.sib).start()

    def wait_passed(self, j):
        self._copy(5 + j, self._other((self.xn, self.yn, self.dg)[j]), self.me).wait_recv()

    def wait_sends(self):
        for k, to in enumerate((self.sib, self.xn, self.yn)):
            self._copy(k, self.me, to).wait_send()
        self._copy(3, self.xn, self.yn, part=0).wait_send()
        self._copy(4, self.yn, self.xn, part=1).wait_send()
        for j, dev in enumerate((self.xn, self.yn, self.dg)):
            self._copy(5 + j, dev, self.sib).wait_send()


def _direct_sems(n):
    if n == 0:
        return []
    return [pltpu.SemaphoreType.DMA((n, 7)), pltpu.SemaphoreType.DMA((n, 7)), pltpu.SemaphoreType.DMA((n,))]


def _adam_math(w, g, m, v):
    m = ADAM_B1 * m + (1.0 - ADAM_B1) * g
    v = ADAM_B2 * v + (1.0 - ADAM_B2) * (g * g)
    m_hat = m / (1.0 - ADAM_B1 ** ADAM_STEP)
    v_hat = v / (1.0 - ADAM_B2 ** ADAM_STEP)
    delta = -ADAM_LR * (m_hat / (jnp.sqrt(v_hat) + ADAM_EPS) + ADAM_WD * w)
    return delta, m, v


def _sum_adam(tensors, name, small=(), small_rows=0):
    NB = 2
    n = len(tensors)
    ns = len(small)

    def body(*refs):
        ins, small_refs, outs = refs[:4 * n], refs[4 * n:4 * n + ns], refs[4 * n + ns:]
        if ns:
            @pl.when(pl.program_id(0) == 0)
            def _():
                _pack_small(outs[4 * n], small_refs, small)

        for k in range(n):
            p_ref, w_ref, m_ref, v_ref = ins[4 * k:4 * k + 4]
            g_ref, d_ref, nm_ref, nv_ref = outs[4 * k:4 * k + 4]
            g = p_ref[0].astype(F32)
            for i in range(1, p_ref.shape[0]):
                g = g + p_ref[i].astype(F32)
            g_ref[...] = g
            d_ref[...], nm_ref[...], nv_ref[...] = _adam_math(w_ref[...], g, m_ref[...], v_ref[...])

    in_specs, out_specs, out_shape, operands = [], [], [], []
    for parts, w, m, v in tensors:
        R, C = w.shape
        blk = pl.BlockSpec((R // NB, C), lambda i: (i, 0))
        in_specs += [pl.BlockSpec((parts.shape[0], R // NB, C), lambda i: (0, i, 0)), blk, blk, blk]
        out_specs += [blk] * 4
        out_shape += [jax.ShapeDtypeStruct((R, C), F32)] * 4
        operands += [parts, w, m, v]
    for a, _, _ in small:
        in_specs.append(pl.BlockSpec(a.shape, lambda i: (0, 0)))
        operands.append(a)
    if ns:
        out_specs.append(pl.BlockSpec((small_rows, LANES), lambda i: (0, 0)))
        out_shape.append(jax.ShapeDtypeStruct((small_rows, LANES), F32))
    res = pl.pallas_call(
        body, name=name, grid=(NB,), in_specs=in_specs, out_specs=out_specs, out_shape=out_shape,
        compiler_params=_params(("arbitrary",)),
    )(*operands)
    return [tuple(res[4 * k:4 * k + 4]) for k in range(n)] + list(res[4 * n:])


SUBLANES = 8


def _nrows(size):
    return -(-size // (SUBLANES * LANES)) * SUBLANES


def _view2d(a):
    return a.reshape(-1, LANES) if a.size % LANES == 0 else a.reshape(1, -1)


def _pack_small(out, refs, parts):
    out[...] = jnp.zeros_like(out)
    at = 0
    for ref, (a, rows, flag) in zip(refs, parts):
        val = ref[...].T if flag == "T" else ref[...]
        r, c = (rows, val.shape[1]) if flag == "T" else val.shape
        out[at:at + r, 0:c] = val[:r]
        at += _nrows(r * c)


def _small_update(full, me, reps, shards, name):
    n = len(reps) + len(shards)

    def body(me_ref, full_ref, *refs):
        ins, outs = refs[:3 * n], refs[3 * n:]
        at = 0
        for k in range(n):
            w_ref, m_ref, v_ref = ins[3 * k:3 * k + 3]
            r, c = w_ref.shape
            if k < len(reps):
                g = full_ref[at:at + r, 0:c]
                at += _nrows(r * c)
            else:
                seg = full_ref[at:at + N_DEV * r, :]
                row = lax.broadcasted_iota(jnp.int32, seg.shape, 0)
                pick = [jnp.sum(jnp.where(row == r * me_ref[0] + t, seg, 0.0), axis=0, keepdims=True) for t in range(r)]
                g = pick[0] if r == 1 else jnp.concatenate(pick, axis=0)
                at += N_DEV * r
            g_ref, d_ref, nm_ref, nv_ref = outs[4 * k:4 * k + 4]
            g_ref[...] = g
            d_ref[...], nm_ref[...], nv_ref[...] = _adam_math(w_ref[...], g, m_ref[...], v_ref[...])
        outs[4 * n][...] = full_ref[at:at + 1, 0:1]

    flat = [t for p in reps + shards for t in p]
    S = jax.ShapeDtypeStruct
    res = pl.pallas_call(
        body, name=name,
        in_specs=[pl.BlockSpec(memory_space=pltpu.SMEM)] + [pl.BlockSpec(memory_space=pltpu.VMEM)] * (1 + len(flat)),
        out_shape=[S(p[0].shape, F32) for p in reps + shards for _ in range(4)] + [S((1, 1), F32)],
    )(me, full, *flat)
    return [tuple(res[4 * k:4 * k + 4]) for k in range(n)], res[4 * n]


def _rope_tables(T):
    pos = np.arange(T, dtype=np.float32)
    inv_freq = (np.float64(ROPE_THETA) ** (-np.arange(0, HEAD_DIM, 2, dtype=np.float64) / HEAD_DIM)).astype(np.float32)
    ang = (pos[:, None] * inv_freq[None, :]).astype(np.float64)
    cos, sin, zero = np.cos(ang).astype(np.float32), np.sin(ang).astype(np.float32), np.zeros(ang.shape, np.float32)
    c = np.concatenate([cos, cos, cos, cos], axis=1)
    s1 = np.concatenate([-sin, zero, -sin, zero], axis=1)
    s2 = np.concatenate([zero, sin, zero, sin], axis=1)
    return jnp.asarray(c), jnp.asarray(s1), jnp.asarray(s2)


def kernel(x, a_norm_g, a_w_in, a_ln_g, a_ln_b, a_ws, a_bs, a_w_out, kv_norm_g, w_kv, b_kv, b_norm_g, b_w_in, b_bq, b_sinks, b_w_out, final_norm_g, loss_target, m_a_norm_g, m_a_w_in, m_a_ln_g, m_a_ln_b, m_a_ws, m_a_bs, m_a_w_out, m_kv_norm_g, m_w_kv, m_b_kv, m_b_norm_g, m_b_w_in, m_b_bq, m_b_sinks, m_b_w_out, m_final_norm_g, v_a_norm_g, v_a_w_in, v_a_ln_g, v_a_ln_b, v_a_ws, v_a_bs, v_a_w_out, v_kv_norm_g, v_w_kv, v_b_kv, v_b_norm_g, v_b_w_in, v_b_bq, v_b_sinks, v_b_w_out, v_final_norm_g):
    T, D = x.shape[1], x.shape[2]
    AW = a_ln_g.shape[1] * N_DEV
    G = a_ws.shape[1]
    assert w_kv.shape[1] == 2 * LANES and a_ws.shape[2] == CHUNK and T % CHUNK == 0
    me = _my_index()

    xs, tgt = x[0], loss_target[0]
    z, wa_in_t, g_a, ln_g, ln_b, wa_out, wkv = _in_proj(xs, a_w_in[0], [a_norm_g, a_ln_g, a_ln_b], me.reshape(1),
                                                        [a_w_out[0], w_kv])
    wa_in_t = wa_in_t.reshape(-1, D)
    wa_out = wa_out.reshape(AW, D)
    wkv = wkv.reshape(D, 2 * LANES)

    rc, rs1, rs2 = _rope_tables(T)
    ws = a_ws[0]
    g_kv = kv_norm_g.reshape(1, D)
    bkv = b_kv.reshape(1, -1)
    g_f = final_norm_g.reshape(1, D)
    sinks = b_sinks.reshape(1, 16)
    h1, sv, vhat, rstd, k4, v4, kt, vt, wb_in, wb_out = _a_fwd(
        xs, z, ln_g, ln_b, ws, a_bs[0], wa_out, g_kv, wkv, bkv, rc, rs1, rs2, [b_w_in[0], b_w_out[0]])
    wb_out = wb_out.reshape(-1, D)
    q, g2, o, dh2, dh2_b, loss, d_gf = _b_fwd(h1, b_norm_g, wb_in, b_bq, rc, rs1, rs2, k4, vt, sinks, wb_out, g_f, tgt)
    dh1p, dz2, n2, y2, dk, dv, d_bq, d_gb, d_sink = _b_bwd(dh2, h1, q, g2, o, k4, v4, kt, sinks, wb_out, wb_in,
                                                           b_norm_g, rc, rs1, rs2)
    d_sink = d_sink[:, :4].reshape(2, 2, 4).transpose(0, 2, 1).reshape(1, 16)
    gw_b_in, gw_b_out = _wgrad([(n2, dz2, N_DEV), (y2, dh2_b, 1)], "wgrad_b", bt=1024)
    gw_b_out = gw_b_out.reshape(N_DEV, -1, D)
    (dz, gw_a_out, gw_kv, dh1_f, d_gkv, d_bkv, d_lng, d_lnb, d_ws, d_bst, r_b_in, r_b_out) = _a_bwd(
        dh1p, dk, dv, h1, g_kv, wkv, wa_out, ws, ln_g, ln_b, z, sv, vhat, rstd, rc, rs1, rs2, [gw_b_in, gw_b_out])
    dx, n1, d_ga, r_a_out, r_kv = _a_in_bwd(dz, wa_in_t, xs, dh1_f, g_a, [gw_a_out, gw_kv])
    small = [(_view2d(d_ws), None, None), (d_bst, G, "T")] + [(_view2d(a), None, None) for a in (
        d_gkv, d_bkv, d_gb, d_bq, d_sink, d_gf, d_ga, d_lng, d_lnb, loss)]
    used = sum(_nrows(G * CHUNK if flag else a.size) for a, _, flag in small)
    per = -(-used // (SUBLANES * N_DEV)) * SUBLANES

    ((g_a_out, d_a_out, nm_a_out, nv_a_out), (g_kvw, d_kvw, nm_kvw, nv_kvw), (g_b_in, d_b_in, nm_b_in, nv_b_in),
     (g_b_out, d_b_out, nm_b_out, nv_b_out), small_pack) = _sum_adam(
        [(r_a_out, a_w_out[0], m_a_w_out[0], v_a_w_out[0]), (r_kv, w_kv, m_w_kv, v_w_kv),
         (r_b_in, b_w_in[0], m_b_w_in[0], v_b_w_in[0]), (r_b_out, b_w_out[0], m_b_w_out[0], v_b_w_out[0])], "adam_rest",
        small, per * N_DEV)
    r_a_in, full_small = _wgrad_exchange(n1, dz, me.reshape(1), small_pack.reshape(N_DEV, per, LANES), "wgrad_a_in")
    (g_a_in, d_a_in, nm_a_in, nv_a_in), = _sum_adam([(r_a_in, a_w_in[0], m_a_w_in[0], v_a_w_in[0])], "adam_a_in")

    full_small = full_small.reshape(N_DEV * per, LANES)
    reps = [(a_ws, m_a_ws, v_a_ws), (a_bs, m_a_bs, v_a_bs), (kv_norm_g, m_kv_norm_g, v_kv_norm_g),
            (b_kv, m_b_kv, v_b_kv), (b_norm_g, m_b_norm_g, v_b_norm_g), (b_bq, m_b_bq, v_b_bq),
            (b_sinks, m_b_sinks, v_b_sinks), (final_norm_g, m_final_norm_g, v_final_norm_g)]
    shards = [(a_norm_g, m_a_norm_g, v_a_norm_g), (a_ln_g, m_a_ln_g, v_a_ln_g), (a_ln_b, m_a_ln_b, v_a_ln_b)]
    upd, loss = _small_update(full_small, me.reshape(1), [tuple(_view2d(t) for t in p) for p in reps],
                              [tuple(_view2d(t) for t in p) for p in shards], "adam_small")
    loss = loss[0, 0]
    sm_g, sd, snm, snv = ([upd[k][j].reshape(p[0].shape) for k, p in enumerate(reps + shards)] for j in range(4))

    def order(big, sm):
        a_in, a_out, kvw, b_in, b_out = big
        ws_, bs_, kvg, bkv_, bng, bq_, snk, fng, ang, alng, alnb = sm
        return (ang, a_in[None], alng, alnb, ws_, bs_, a_out[None], kvg, kvw, bkv_, bng, b_in[None], bq_, snk,
                b_out[None], fng)

    grads = order((g_a_in, g_a_out, g_kvw, g_b_in, g_b_out), sm_g)
    deltas = order((d_a_in, d_a_out, d_kvw, d_b_in, d_b_out), sd)
    new_m = order((nm_a_in, nm_a_out, nm_kvw, nm_b_in, nm_b_out), snm)
    new_v = order((nv_a_in, nv_a_out, nv_kvw, nv_b_in, nv_b_out), snv)
    return (loss, dx[None], *grads, *deltas, *new_m, *new_v)
```

```python
import functools

import jax
import jax.numpy as jnp
import numpy as np
from jax import lax
from jax.experimental import pallas as pl
from jax.experimental.pallas import tpu as pltpu

CHUNK = 128
HEAD_DIM = 64
ROPE_THETA = 10000.0
EPS = 1e-5
ADAM_LR = 0.001
ADAM_B1 = 0.9
ADAM_B2 = 0.999
ADAM_EPS = 1e-08
ADAM_WD = 0.01
ADAM_STEP = 10
N_DEV = 8
LANES = 128
NEG = -1e30

BF = jnp.bfloat16
F32 = jnp.float32
MESH = pl.DeviceIdType.MESH
AXES = ("x", "y", "c")
VMEM_LIMIT = 56 * 1024 * 1024


def _dot(a, b):
    return jnp.dot(a, b, preferred_element_type=F32)


def _dot_nt(a, b):
    return lax.dot_general(a, b, (((1,), (1,)), ((), ())), preferred_element_type=F32)


def _dot_tn(a, b):
    return lax.dot_general(a, b, (((0,), (0,)), ((), ())), preferred_element_type=F32)


def _const_spec(shape):
    nd = len(shape)
    return pl.BlockSpec(shape, lambda *_: (0,) * nd, pipeline_mode=pl.Buffered(1))


def _acc_spec(shape):
    nd = len(shape)
    return pl.BlockSpec(shape, lambda *_: (0,) * nd)


def _row_spec(tm, width):
    return pl.BlockSpec((tm, width), lambda i: (i, 0))


def _col_spec(tm, height):
    return pl.BlockSpec((height, tm), lambda i: (0, i))


def _params(sem):
    return pltpu.CompilerParams(dimension_semantics=sem, vmem_limit_bytes=VMEM_LIMIT)


def _rot(x, c, s1, s2):
    return x * c + pltpu.roll(x, 96, 1) * s1 + pltpu.roll(x, 32, 1) * s2


def _rot_bwd(d, c, s1, s2):
    return d * c + pltpu.roll(d * s1, 32, 1) + pltpu.roll(d * s2, 96, 1)


def _silu_parts(g):
    sg = jax.nn.sigmoid(g)
    return g * sg, sg * (1.0 + g * (1.0 - sg))


def _rms_bwd(dn, xh, r, g):
    a = dn * g
    return r * (a - xh * jnp.mean(a * xh, axis=-1, keepdims=True))


def _lane_lo(shape):
    return lax.broadcasted_iota(jnp.int32, shape, 1) < HEAD_DIM


def _split4(t):
    lo = _lane_lo(t.shape)
    tr = pltpu.roll(t, HEAD_DIM, 1)
    z = jnp.zeros_like(t)
    return jnp.concatenate([jnp.where(lo, t, z), jnp.where(lo, z, tr), jnp.where(lo, tr, z), jnp.where(lo, z, t)], axis=1)


def _stack_pairs(t, h):
    return jnp.concatenate([t[:, (h * 4 + j) * LANES:(h * 4 + j + 1) * LANES] for j in range(4)], axis=0)


def _upper():
    shape = (CHUNK, 4 * CHUNK)
    return lax.broadcasted_iota(jnp.int32, shape, 0) > (lax.broadcasted_iota(jnp.int32, shape, 1) & (CHUNK - 1))


def _band_rows(tile_ref, before_ref, c, h):
    a = slice(2 * h * LANES, (2 * h + 1) * LANES)
    b = slice((2 * h + 1) * LANES, (2 * h + 2) * LANES)
    cur = slice(c * CHUNK, (c + 1) * CHUNK)

    def prev(cols):
        return before_ref[:, cols] if c == 0 else tile_ref[(c - 1) * CHUNK:c * CHUNK, cols]

    return jnp.concatenate([prev(a), tile_ref[cur, a], prev(b), tile_ref[cur, b]], axis=0)


def _band_cols(tile_ref, before_ref, c, h):
    a = slice(2 * h * LANES, (2 * h + 1) * LANES)
    b = slice((2 * h + 1) * LANES, (2 * h + 2) * LANES)

    def prev(rows):
        return before_ref[0, rows, :] if c == 0 else tile_ref[c - 1, rows, :]

    return jnp.concatenate([prev(a), tile_ref[c, a, :], prev(b), tile_ref[c, b, :]], axis=1)


def _band_specs(tm):
    nc = tm // CHUNK

    def before(i):
        return jnp.maximum(i * nc - 1, 0)

    return (pl.BlockSpec((tm, 4 * LANES), lambda i: (i, 0)),
            pl.BlockSpec((CHUNK, 4 * LANES), lambda i: (before(i), 0)),
            pl.BlockSpec((nc, 4 * LANES, CHUNK), lambda i: (i, 0, 0)),
            pl.BlockSpec((1, 4 * LANES, CHUNK), lambda i: (before(i), 0, 0)))


def _fold(t, upper, has_prev=None):
    out = []
    for k in range(2):
        prev = t[2 * k * CHUNK:(2 * k + 1) * CHUNK]
        if has_prev is not None:
            prev = jnp.where(has_prev, prev, NEG)
        out.append(jnp.where(upper, prev, t[(2 * k + 1) * CHUNK:(2 * k + 2) * CHUNK]))
    return out


def _unfold(fa, fb, upper):
    z = jnp.zeros_like(fa)
    return jnp.concatenate([jnp.where(upper, fa, z), jnp.where(upper, z, fa),
                            jnp.where(upper, fb, z), jnp.where(upper, z, fb)], axis=0)


def _sink_tile(s_ref):
    shape = (4, 4 * LANES)
    row = lax.broadcasted_iota(jnp.int32, shape, 0)
    pair = lax.broadcasted_iota(jnp.int32, shape, 1) // LANES
    idx = (row // 2) * 8 + pair * 2 + row % 2
    tile = jnp.zeros(shape, F32)
    for n in range(16):
        tile = jnp.where(idx == n, s_ref[0, n], tile)
    return tile


def _softmax_sink(f, sink):
    m = jnp.maximum(jnp.max(f, axis=0, keepdims=True), sink)
    p = jnp.exp(f - m)
    es = jnp.exp(sink - m)
    inv = 1.0 / (jnp.sum(p, axis=0, keepdims=True) + es)
    return p * inv, es * inv


class _Riding:
    def __init__(self, shards, gathered, stages, sems, n_steps):
        self.shards, self.stages, self.n_steps = shards, stages, n_steps
        ssem, rsem, lsem = sems
        self.gathers = [_TwoLevel(stages[k], gathered[k], ssem.at[k], rsem.at[k], lsem.at[k])
                        for k in range(len(shards))]

    def begin(self, i):
        @pl.when(i == 0)
        def _():
            for shard, stage, g in zip(self.shards, self.stages, self.gathers):
                stage[...] = shard[...].astype(stage.dtype)
                g.start()

    def end(self, i):
        @pl.when(i == self.n_steps // 2)
        def _():
            for g in self.gathers:
                g.forward()

        @pl.when(i == self.n_steps - 1)
        def _():
            for g in self.gathers:
                g.finish()

    @staticmethod
    def specs(later):
        nl = len(later)
        hbm = pl.BlockSpec(memory_space=pl.ANY)
        return ([_const_spec(w.shape) for w in later], [hbm] * nl,
                tuple(jax.ShapeDtypeStruct((N_DEV,) + w.shape, BF) for w in later),
                [pltpu.VMEM(w.shape, BF) for w in later] + _direct_sems(nl))


PASS_MASKS = ((0, 1, 2, 5, 4, 3, 6, 7), (0, 1, 4, 3, 2, 5, 6, 7))


def _in_proj(x, w_shard, vec_shards, me, later):
    T, D = x.shape
    SH = w_shard.shape[1]
    TM = min(1024, T)
    nT = T // TM
    nl = len(later)
    nv = len(vec_shards)
    widths = [v.shape[1] for v in vec_shards]
    offsets = [sum(widths[:k]) for k in range(nv)]
    vec_shape = (SUBLANES, sum(widths))
    ds = widths[0]
    last = N_DEV - 1
    masks = jnp.asarray(np.array(PASS_MASKS, np.int32).reshape(-1))

    def slot(p, me_ref, masks_ref):
        return me_ref[0] ^ masks_ref[(me_ref[0] & 1) * N_DEV + p]

    def body(me_ref, masks_ref, x_ref, wsh_ref, *rest):
        vsh_refs, rest = rest[:nv], rest[nv:]
        shards, rest = rest[:nl], rest[nl:]
        (z_ref, wt_ref), rest = rest[:2], rest[2:]
        vout_refs, rest = rest[:nv], rest[nv:]
        gathered, rest = rest[:nl], rest[nl:]
        (w_scr, vec_scr, vstage, n1_scr, ga_scr, w_s, w_r, v_s, v_r, v_l), rest = rest[:10], rest[10:]
        stages, sems = rest[:nl], rest[nl:]
        p, i = pl.program_id(0), pl.program_id(1)
        me = _my_index()
        wg = _RelayGather(w_scr, w_s, w_r)
        vg = _Direct(vstage, vec_scr, v_s, v_r, v_l, scatter=False)
        lg = [_TwoLevel(stages[k], gathered[k], sems[0].at[k], sems[1].at[k], sems[2].at[k]) for k in range(nl)]

        def at_pass(k):
            return (p == k) & (i == 0)

        c = lax.axis_index("c")

        @pl.when(at_pass(0))
        def _():
            for ref, off, wd in zip(vsh_refs, offsets, widths):
                vstage[:, off:off + wd] = jnp.broadcast_to(ref[...], (SUBLANES, wd))
            vg.start()
            w_scr[me] = wsh_ref[...].astype(BF)
            wg.send_own(0).start()

            @pl.when(c == 1)
            def _():
                wg.send_own(1).start()

            @pl.when(c == 0)
            def _():
                wg.send_own(2).start()

            vg.finish()
            for j in range(N_DEV):
                ga_scr[:, j * ds:(j + 1) * ds] = vec_scr[j, 0:1, 0:ds]
                for ref, off, wd in zip(vout_refs, offsets, widths):
                    ref[:, j * wd:(j + 1) * wd] = vec_scr[j, 0:1, off:off + wd]

        @pl.when(at_pass(1))
        def _():
            wg.wait_sibling()

        for first, second, landed_first, landed_second in ((1, 2, wg.on_x, wg.on_y), (2, 1, wg.on_y, wg.on_x)):
            mine = c == (1 if first == 1 else 0)

            @pl.when(at_pass(2) & mine)
            def _(second=second, landed_first=landed_first):
                wg.send_own(second).start()
                landed_first()

            @pl.when(at_pass(3) & mine)
            def _(second=second):
                wg.wait_passed(second - 1)

            @pl.when(at_pass(4) & mine)
            def _(landed_second=landed_second):
                landed_second()

            @pl.when(at_pass(5) & mine)
            def _(first=first):
                wg.wait_passed(first - 1)

        @pl.when(at_pass(4))
        def _():
            for k in range(nl):
                stages[k][...] = shards[k][...].astype(BF)
                lg[k].start()

        @pl.when(at_pass(6))
        def _():
            wg.on_diag()

        @pl.when(at_pass(7))
        def _():
            wg.wait_passed(2)

        @pl.when(p == 0)
        def _():
            xv = x_ref[...]
            r1 = lax.rsqrt(jnp.mean(xv * xv, axis=-1, keepdims=True) + EPS)
            n1_scr[i] = (xv * r1 * ga_scr[...]).astype(BF)

        z_ref[...] = _dot(n1_scr[i], w_scr[slot(p, me_ref, masks_ref)]).astype(BF)

        @pl.when(i == 0)
        def _():
            wt_ref[0] = w_scr[slot(p, me_ref, masks_ref)].T

        @pl.when((p == last) & (i == nT - 1))
        def _():
            wg.wait_sends()
            for g in lg:
                g.forward()
            for g in lg:
                g.finish()

    hbm = pl.BlockSpec(memory_space=pl.ANY)
    dma = pltpu.SemaphoreType.DMA
    S = jax.ShapeDtypeStruct
    def whole(shape):
        return pl.BlockSpec(shape, lambda p, i, m, t: (0, 0))

    def once(shape):
        return pl.BlockSpec(shape, lambda p, i, m, t: (0, 0), pipeline_mode=pl.Buffered(1))

    grid_spec = pltpu.PrefetchScalarGridSpec(
        num_scalar_prefetch=2, grid=(N_DEV, nT),
        in_specs=[pl.BlockSpec((TM, D), lambda p, i, m, t: (jnp.where(p == 0, i, nT - 1), 0)), once(w_shard.shape)]
        + [once(v.shape) for v in vec_shards] + [once(w.shape) for w in later],
        out_specs=[pl.BlockSpec((TM, SH), lambda p, i, m, t: (i, slot(p, m, t))),
                   pl.BlockSpec((1, SH, D), lambda p, i, m, t: (slot(p, m, t), 0, 0))]
        + [whole((1, N_DEV * wd)) for wd in widths] + [hbm] * nl,
        scratch_shapes=[pltpu.VMEM((N_DEV, D, SH), BF), pltpu.VMEM((N_DEV,) + vec_shape, F32),
                        pltpu.VMEM(vec_shape, F32), pltpu.VMEM((nT, TM, D), BF), pltpu.VMEM((1, D), F32),
                        dma((8,)), dma((8,)), dma((7,)), dma((7,)), dma]
        + [pltpu.VMEM(w.shape, BF) for w in later] + _direct_sems(nl))
    return pl.pallas_call(
        body, name="a_in_proj", grid_spec=grid_spec,
        out_shape=(S((T, N_DEV * SH), BF), S((N_DEV, SH, D), BF)) + tuple(S((1, N_DEV * wd), F32) for wd in widths)
        + tuple(S((N_DEV,) + w.shape, BF) for w in later),
        compiler_params=_params(("arbitrary", "arbitrary")),
    )(me, masks, x, w_shard, *vec_shards, *later)


def _a_fwd(x, z, ln_g, ln_b, ws, bs, wa_out, g_kv, w_kv, b_kv, rc, rs1, rs2, later):
    T, D = x.shape
    AW = wa_out.shape[0]
    G = ws.shape[0]
    TM = min(512, T)
    nT = T // TM
    nC = TM // CHUNK
    nl = len(later)

    def body(x_ref, u_ref, v_ref, gt_ref, lng_ref, lnb_ref, ws_ref, bs_ref, waout_ref, gkv_ref, wkv_ref, bkv_ref,
             rc_ref, rs1_ref, rs2_ref, *rest):
        shards, rest = rest[:nl], rest[nl:]
        (h1_ref, sv_ref, vhat_ref, rstd_ref, k4_ref, v4_ref, kt_ref, vt_ref), rest = rest[:8], rest[8:]
        gathered, sv_scr, stages, sems = rest[:nl], rest[nl], rest[nl + 1:2 * nl + 1], rest[2 * nl + 1:]
        i = pl.program_id(0)
        riding = _Riding(shards, gathered, stages, sems, nT)
        riding.begin(i)
        xv = x_ref[...]
        u = u_ref[...].astype(F32)
        v = v_ref[...].astype(F32)
        gt = gt_ref[...].astype(F32)
        mu = jnp.mean(v, axis=-1, keepdims=True)
        xc = v - mu
        rstd = lax.rsqrt(jnp.mean(xc * xc, axis=-1, keepdims=True) + EPS)
        vhat = xc * rstd
        vln = (vhat * lng_ref[...] + lnb_ref[...]).astype(BF)
        tri = lax.broadcasted_iota(jnp.int32, (CHUNK, CHUNK), 0) >= lax.broadcasted_iota(jnp.int32, (CHUNK, CHUNK), 1)
        bst = jnp.concatenate([bs_ref[...], jnp.zeros((CHUNK - G, CHUNK), F32)], axis=0).T
        for g in range(G):
            wsm = jnp.where(tri, ws_ref[g], 0.0).astype(BF)
            bias = bst[:, g:g + 1]
            for c in range(nC):
                blk = vln[c * CHUNK:(c + 1) * CHUNK, g * CHUNK:(g + 1) * CHUNK]
                sv_scr[c * CHUNK:(c + 1) * CHUNK, g * CHUNK:(g + 1) * CHUNK] = _dot(wsm, blk) + bias
        sv = sv_scr[...]
        silu, _ = _silu_parts(gt)
        y = (u * sv * silu).astype(BF)
        h1 = xv + _dot(y, waout_ref[...])
        h1_ref[...] = h1
        sv_ref[...] = sv.astype(BF)
        vhat_ref[...] = vhat.astype(BF)
        rstd_ref[...] = jnp.broadcast_to(rstd, rstd_ref.shape)
        rkv = lax.rsqrt(jnp.mean(h1 * h1, axis=-1, keepdims=True) + EPS)
        nkv = (h1 * rkv * gkv_ref[...]).astype(BF)
        kv = _dot(nkv, wkv_ref[...]) + bkv_ref[...]
        k_rot = _rot(kv[:, :LANES], rc_ref[...], rs1_ref[...], rs2_ref[...])
        for src, ref, tref in ((k_rot, k4_ref, kt_ref), (kv[:, LANES:], v4_ref, vt_ref)):
            t4 = _split4(src)
            ref[...] = t4.astype(BF)
            for c in range(nC):
                for b in range(4):
                    blk = t4[c * CHUNK:(c + 1) * CHUNK, b * LANES:(b + 1) * LANES]
                    tref[c, b * LANES:(b + 1) * LANES, :] = blk.T.astype(BF)
        riding.end(i)

    row = functools.partial(_row_spec, TM)
    zcol = [pl.BlockSpec((TM, AW), functools.partial(lambda k, i: (i, k), k)) for k in range(3)]
    tr = pl.BlockSpec((nC, 4 * LANES, CHUNK), lambda i: (i, 0, 0))
    r_in, r_out, r_shape, r_scratch = _Riding.specs(later)
    S = jax.ShapeDtypeStruct
    return pl.pallas_call(
        body, name="a_fwd", grid=(nT,),
        in_specs=[row(D)] + zcol + [_const_spec((1, AW)), _const_spec((1, AW)),
                  _const_spec(ws.shape), _const_spec(bs.shape), _const_spec(wa_out.shape), _const_spec((1, D)),
                  _const_spec(w_kv.shape), _const_spec((1, 2 * LANES)), row(LANES), row(LANES), row(LANES)] + r_in,
        out_specs=[row(D), row(AW), row(AW), row(LANES), row(4 * LANES), row(4 * LANES), tr, tr] + r_out,
        out_shape=(S((T, D), F32), S((T, AW), BF), S((T, AW), BF), S((T, LANES), F32),
                   S((T, 4 * LANES), BF), S((T, 4 * LANES), BF),
                   S((T // CHUNK, 4 * LANES, CHUNK), BF), S((T // CHUNK, 4 * LANES, CHUNK), BF)) + r_shape,
        scratch_shapes=[pltpu.VMEM((TM, AW), F32)] + r_scratch,
        compiler_params=_params(("arbitrary",)),
    )(x, z, z, z, ln_g, ln_b, ws, bs, wa_out, g_kv, w_kv, b_kv, rc, rs1, rs2, *later)


def _b_fwd(h1, g_b, wb_in, bq, rc, rs1, rs2, k4, vt, sinks, wb_out, g_f, target):
    T, D = h1.shape
    BW = wb_out.shape[0]
    SH = wb_in.shape[2]
    TM = min(512, T)
    nC = TM // CHUNK
    nP = BW // LANES

    def body(h1_ref, gb_ref, wbin_ref, bq_ref, rc_ref, rs1_ref, rs2_ref, k4_ref, k4p_ref, vt_ref, vtp_ref, sinks_ref,
             wbout_ref, gf_ref, tgt_ref, q_ref, g2_ref, o_ref, dh2_ref, dh2b_ref, loss_ref, dgf_ref, z_scr, o_scr):
        i = pl.program_id(0)
        sink = _sink_tile(sinks_ref)

        @pl.when(i == 0)
        def _():
            loss_ref[...] = jnp.zeros_like(loss_ref)
            dgf_ref[...] = jnp.zeros_like(dgf_ref)

        h1v = h1_ref[...]
        r2 = lax.rsqrt(jnp.mean(h1v * h1v, axis=-1, keepdims=True) + EPS)
        n2 = (h1v * r2 * gb_ref[...]).astype(BF)
        for j in range(N_DEV):
            z_scr[:, j * SH:(j + 1) * SH] = _dot(n2, wbin_ref[j])
        c_t, s1_t, s2_t = rc_ref[...], rs1_ref[...], rs2_ref[...]
        for p in range(nP):
            cols = slice(p * LANES, (p + 1) * LANES)
            qp = _rot(z_scr[:, cols] + bq_ref[:, cols], c_t, s1_t, s2_t) * (HEAD_DIM ** -0.5)
            q_ref[:, cols] = qp.astype(BF)
        g2 = z_scr[:, BW:]
        g2_ref[...] = g2.astype(BF)
        upper = _upper()
        for c in range(nC):
            ci = i * nC + c
            rows = slice(c * CHUNK, (c + 1) * CHUNK)
            qc = q_ref[rows, :]
            for h in range(2):
                st = _dot_nt(_band_rows(k4_ref, k4p_ref, c, h), _stack_pairs(qc, h))
                fa, fb = _fold(st, upper, ci > 0)
                pa, _ = _softmax_sink(fa, sink[2 * h:2 * h + 1, :])
                pb, _ = _softmax_sink(fb, sink[2 * h + 1:2 * h + 2, :])
                ot = _dot(_band_cols(vt_ref, vtp_ref, c, h), _unfold(pa, pb, upper).astype(BF))
                for j in range(4):
                    o_scr[rows, (h * 4 + j) * LANES:(h * 4 + j + 1) * LANES] = ot[:, j * CHUNK:(j + 1) * CHUNK].T
        o = o_scr[...]
        o_ref[...] = o.astype(BF)
        silu, _ = _silu_parts(g2)
        h2 = h1v + _dot((o * silu).astype(BF), wbout_ref[...])
        rf = lax.rsqrt(jnp.mean(h2 * h2, axis=-1, keepdims=True) + EPS)
        xh = h2 * rf
        gf = gf_ref[...]
        err = xh * gf - tgt_ref[...]
        dyf = err * (1.0 / D)
        dh2 = _rms_bwd(dyf, xh, rf, gf)
        dh2_ref[...] = dh2
        dh2b_ref[...] = dh2.astype(BF)
        loss_ref[...] += 0.5 * jnp.sum(jnp.mean(err * err, axis=-1, keepdims=True), axis=0, keepdims=True)
        dgf_ref[...] += jnp.sum(dyf * xh, axis=0, keepdims=True)

    row = functools.partial(_row_spec, TM)
    rows_tile, rows_before, cols_tile, cols_before = _band_specs(TM)
    S = jax.ShapeDtypeStruct
    return pl.pallas_call(
        body, name="b_fwd", grid=(T // TM,),
        in_specs=[row(D), _const_spec((1, D)), _const_spec(wb_in.shape), _const_spec((1, BW)), row(LANES), row(LANES),
                  row(LANES), rows_tile, rows_before, cols_tile, cols_before, pl.BlockSpec(memory_space=pltpu.SMEM),
                  _const_spec(wb_out.shape), _const_spec((1, D)), row(D)],
        out_specs=[row(BW), row(BW), row(BW), row(D), row(D), _acc_spec((1, 1)), _acc_spec((1, D))],
        out_shape=(S((T, BW), BF), S((T, BW), BF), S((T, BW), BF), S((T, D), F32), S((T, D), BF), S((1, 1), F32),
                   S((1, D), F32)),
        scratch_shapes=[pltpu.VMEM((TM, 2 * BW), F32), pltpu.VMEM((TM, BW), F32)],
        compiler_params=_params(("arbitrary",)),
    )(h1, g_b, wb_in, bq, rc, rs1, rs2, k4, k4, vt, vt, sinks, wb_out, g_f, target)


def _b_bwd(dh2, h1, q, g2, o, k4, v4, kt, sinks, wb_out, wb_in, g_b, rc, rs1, rs2):
    T, D = h1.shape
    BW = wb_out.shape[0]
    SH = wb_in.shape[2]
    TM = min(512, T)
    nT = T // TM
    nC = TM // CHUNK
    nP = BW // LANES

    def body(dh2_ref, h1_ref, q_ref, g2_ref, o_ref, k4_ref, k4p_ref, v4_ref, v4p_ref, kt_ref, ktp_ref, sinks_ref,
             wbout_ref, wbin_ref, gb_ref, rc_ref, rs1_ref, rs2_ref,
             dh1_ref, dz2_ref, n2_ref, y2_ref, dk_ref, dv_ref, dbq_ref, dgb_ref, dsink_ref, do_scr, dq_scr, dsacc_scr):
        i = pl.program_id(0)
        sink = _sink_tile(sinks_ref)

        @pl.when(i == 0)
        def _():
            dk_ref[...] = jnp.zeros_like(dk_ref)
            dv_ref[...] = jnp.zeros_like(dv_ref)
            dbq_ref[...] = jnp.zeros_like(dbq_ref)
            dgb_ref[...] = jnp.zeros_like(dgb_ref)
            dsacc_scr[...] = jnp.zeros_like(dsacc_scr)

        dh2 = dh2_ref[...]
        dy2 = _dot_nt(dh2.astype(BF), wbout_ref[...])
        silu, dsilu = _silu_parts(g2_ref[...].astype(F32))
        do_scr[...] = (dy2 * silu).astype(BF)
        dy2, silu, dsilu = dy2.astype(BF), silu.astype(BF), dsilu.astype(BF)
        ob = o_ref[...]
        y2_ref[...] = (ob * silu).T
        dz2_ref[:, BW:] = dy2 * ob * dsilu
        upper = _upper()
        lo = _lane_lo((2 * CHUNK, LANES))
        for c in range(nC):
            ci = i * nC + c
            rows = slice(c * CHUNK, (c + 1) * CHUNK)
            pci = jnp.maximum(ci - 1, 0)
            prev = pl.multiple_of(pci * CHUNK, CHUNK)
            cur = pl.multiple_of(ci * CHUNK, CHUNK)
            qc = q_ref[rows, :]
            doc = do_scr[rows, :]
            dkb = jnp.zeros((2 * CHUNK, LANES), F32)
            dvb = jnp.zeros((2 * CHUNK, LANES), F32)
            for h in range(2):
                qs = _stack_pairs(qc, h)
                dos = _stack_pairs(doc, h)
                fa, fb = _fold(_dot_nt(_band_rows(k4_ref, k4p_ref, c, h), qs), upper, ci > 0)
                dfa, dfb = _fold(_dot_nt(_band_rows(v4_ref, v4p_ref, c, h), dos), upper)
                folded = []
                for k, (f, df) in enumerate(((fa, dfa), (fb, dfb))):
                    p, ps = _softmax_sink(f, sink[2 * h + k:2 * h + k + 1, :])
                    delta = jnp.sum(p * df, axis=0, keepdims=True)
                    dsacc_scr[2 * h + k:2 * h + k + 1, :] -= ps * delta
                    folded.append((p, p * (df - delta)))
                pt = _unfold(folded[0][0], folded[1][0], upper).astype(BF)
                dst = _unfold(folded[0][1], folded[1][1], upper).astype(BF)
                dqt = _dot(_band_cols(kt_ref, ktp_ref, c, h), dst)
                for j in range(4):
                    dq_scr[rows, (h * 4 + j) * LANES:(h * 4 + j + 1) * LANES] = dqt[:, j * CHUNK:(j + 1) * CHUNK].T
                for acc_name, g in (("k", _dot(dst, qs)), ("v", _dot(pt, dos))):
                    a, b = g[:2 * CHUNK], g[2 * CHUNK:]
                    if h == 0:
                        part = jnp.where(lo, a + pltpu.roll(b, HEAD_DIM, 1), 0.0)
                    else:
                        part = jnp.where(lo, 0.0, pltpu.roll(a, HEAD_DIM, 1) + b)
                    if acc_name == "k":
                        dkb += part
                    else:
                        dvb += part
            dk_ref[pl.ds(prev, CHUNK), :] += dkb[:CHUNK]
            dk_ref[pl.ds(cur, CHUNK), :] += dkb[CHUNK:]
            dv_ref[pl.ds(prev, CHUNK), :] += dvb[:CHUNK]
            dv_ref[pl.ds(cur, CHUNK), :] += dvb[CHUNK:]
        c_t, s1_t, s2_t = rc_ref[...], rs1_ref[...], rs2_ref[...]
        for p in range(nP):
            cols = slice(p * LANES, (p + 1) * LANES)
            dqp = _rot_bwd(dq_scr[:, cols] * (HEAD_DIM ** -0.5), c_t, s1_t, s2_t)
            dbq_ref[:, cols] += jnp.sum(dqp, axis=0, keepdims=True)
            dz2_ref[:, cols] = dqp.astype(BF)
        h1v = h1_ref[...]
        r2 = lax.rsqrt(jnp.mean(h1v * h1v, axis=-1, keepdims=True) + EPS)
        xh = h1v * r2
        gb = gb_ref[...]
        n2_ref[...] = (xh * gb).astype(BF).T
        dn2 = None
        for j in range(N_DEV):
            part = _dot_nt(dz2_ref[:, j * SH:(j + 1) * SH], wbin_ref[j])
            dn2 = part if dn2 is None else dn2 + part
        dgb_ref[...] += jnp.sum(dn2 * xh, axis=0, keepdims=True)
        dh1_ref[...] = dh2 + _rms_bwd(dn2, xh, r2, gb)

        @pl.when(i == nT - 1)
        def _():
            lane = lax.broadcasted_iota(jnp.int32, dsink_ref.shape, 1)
            tot = jnp.zeros(dsink_ref.shape, F32)
            for j in range(4):
                tot += jnp.where(lane == j, jnp.sum(dsacc_scr[:, j * CHUNK:(j + 1) * CHUNK], axis=1, keepdims=True), 0.0)
            dsink_ref[...] = tot

    row = functools.partial(_row_spec, TM)
    rows_tile, rows_before, cols_tile, cols_before = _band_specs(TM)
    S = jax.ShapeDtypeStruct
    return pl.pallas_call(
        body, name="b_bwd", grid=(T // TM,),
        in_specs=[row(D), row(D), row(BW), row(BW), row(BW), rows_tile, rows_before, rows_tile, rows_before,
                  cols_tile, cols_before, pl.BlockSpec(memory_space=pltpu.SMEM), _const_spec(wb_out.shape), _const_spec(wb_in.shape),
                  _const_spec((1, D)), row(LANES), row(LANES), row(LANES)],
        out_specs=[row(D), row(2 * BW), _col_spec(TM, D), _col_spec(TM, BW), _acc_spec((T, LANES)),
                   _acc_spec((T, LANES)), _acc_spec((1, BW)), _acc_spec((1, D)), _acc_spec((4, LANES))],
        out_shape=(S((T, D), F32), S((T, 2 * BW), BF), S((D, T), BF), S((BW, T), BF), S((T, LANES), F32),
                   S((T, LANES), F32), S((1, BW), F32), S((1, D), F32), S((4, LANES), F32)),
        scratch_shapes=[pltpu.VMEM((TM, BW), BF), pltpu.VMEM((TM, BW), F32), pltpu.VMEM((4, 4 * CHUNK), F32)],
        compiler_params=_params(("arbitrary",)),
    )(dh2, h1, q, g2, o, k4, k4, v4, v4, kt, kt, sinks, wb_out, wb_in, g_b, rc, rs1, rs2)


def _a_bwd(dh1p, dk, dv, h1, g_kv, w_kv, wa_out, ws, ln_g, ln_b, z, sv, vhat, rstd, rc, rs1, rs2, ready):
    T, D = h1.shape
    AW = wa_out.shape[0]
    G = ws.shape[0]
    TM = min(256, T)
    nT = T // TM
    nC = TM // CHUNK
    nr = len(ready)

    def body(dh1p_ref, dk_ref, dv_ref, h1_ref, gkv_ref, wkv_ref, waout_ref, ws_ref, lng_ref,
             lnb_ref, u_ref, gt_ref, sv_ref, vhat_ref, rstd_ref, rc_ref, rs1_ref, rs2_ref, *rest):
        ready_refs, rest = rest[:nr], rest[nr:]
        (dz_ref, gwo_ref, gwk_ref, dh1f_ref, dgkv_ref, dbkv_ref, dlng_ref, dlnb_ref,
         dws_ref, dbs_ref), rest = rest[:10], rest[10:]
        recv_refs, (dsv_scr, dvln_scr, acco_scr, acck_scr, ssem, rsem, lsem) = rest[:nr], rest[nr:]
        i = pl.program_id(0)
        exchanges = [_Direct(ready_refs[k], recv_refs[k], ssem.at[k], rsem.at[k], lsem.at[k], scatter=True)
                     for k in range(nr)]

        @pl.when(i == 0)
        def _():
            for e in exchanges:
                e.start()
            for r in (dgkv_ref, dbkv_ref, dlng_ref, dlnb_ref, dws_ref, dbs_ref, acco_scr, acck_scr):
                r[...] = jnp.zeros_like(r)

        dk_pre = _rot_bwd(dk_ref[...], rc_ref[...], rs1_ref[...], rs2_ref[...])
        dkv = jnp.concatenate([dk_pre, dv_ref[...]], axis=1)
        dbkv_ref[...] += jnp.sum(dkv, axis=0, keepdims=True)
        dkv_b = dkv.astype(BF)
        h1v = h1_ref[...]
        rkv = lax.rsqrt(jnp.mean(h1v * h1v, axis=-1, keepdims=True) + EPS)
        xh_kv = h1v * rkv
        gkv = gkv_ref[...]
        acck_scr[...] += _dot((xh_kv * gkv).astype(BF).T, dkv_b)
        dnkv = _dot_nt(dkv_b, wkv_ref[...])
        dgkv_ref[...] += jnp.sum(dnkv * xh_kv, axis=0, keepdims=True)
        dh1 = dh1p_ref[...] + _rms_bwd(dnkv, xh_kv, rkv, gkv)
        dh1_b = dh1.astype(BF)
        dh1f_ref[...] = dh1
        dy = _dot_nt(dh1_b, waout_ref[...]).astype(BF)
        silu, dsilu = _silu_parts(gt_ref[...].astype(F32))
        silu, dsilu = silu.astype(BF), dsilu.astype(BF)
        ub, svb = u_ref[...], sv_ref[...]
        us = ub * silu
        dys = dy * svb
        acco_scr[...] += _dot((us * svb).T, dh1_b)
        dz_ref[:, :AW] = dys * silu
        dz_ref[:, 2 * AW:] = dys * ub * dsilu
        dsv_scr[...] = dy * us
        vhat_v = vhat_ref[...].astype(F32)
        lng = lng_ref[...]
        vln_b = (vhat_v * lng + lnb_ref[...]).astype(BF)
        tri = lax.broadcasted_iota(jnp.int32, (CHUNK, CHUNK), 0) >= lax.broadcasted_iota(jnp.int32, (CHUNK, CHUNK), 1)
        lane = lax.broadcasted_iota(jnp.int32, (CHUNK, LANES), 1)
        dbs = jnp.zeros((CHUNK, LANES), F32)
        for g in range(G):
            wsm = jnp.where(tri, ws_ref[g], 0.0).astype(BF)
            cols = slice(g * CHUNK, (g + 1) * CHUNK)
            dws_g = None
            for c in range(nC):
                rows = slice(c * CHUNK, (c + 1) * CHUNK)
                dsv_cg = dsv_scr[rows, cols]
                dvln_scr[rows, cols] = _dot_tn(wsm, dsv_cg)
                part = _dot_nt(dsv_cg, vln_b[rows, cols])
                dws_g = part if dws_g is None else dws_g + part
                dbs += jnp.where(lane == g, jnp.sum(dsv_cg.astype(F32), axis=-1, keepdims=True), 0.0)
            dws_ref[g] += jnp.where(tri, dws_g, 0.0)
        dbs_ref[...] += dbs
        dvln = dvln_scr[...]
        dlng_ref[...] += jnp.sum(dvln * vhat_v, axis=0, keepdims=True)
        dlnb_ref[...] += jnp.sum(dvln, axis=0, keepdims=True)
        a = dvln * lng
        dvv = rstd_ref[:, 0:1] * (a - jnp.mean(a, axis=-1, keepdims=True)
                                  - vhat_v * jnp.mean(a * vhat_v, axis=-1, keepdims=True))
        dz_ref[:, AW:2 * AW] = dvv.astype(BF)

        @pl.when(i == nT - 1)
        def _():
            for j in range(N_DEV):
                gwo_ref[j] = acco_scr[j * (AW // N_DEV):(j + 1) * (AW // N_DEV)].astype(BF)
                gwk_ref[j] = acck_scr[j * (D // N_DEV):(j + 1) * (D // N_DEV)].astype(BF)
            for e in exchanges:
                e.finish()

    row = functools.partial(_row_spec, TM)
    hbm = pl.BlockSpec(memory_space=pl.ANY)
    S = jax.ShapeDtypeStruct
    gwo_shape, gwk_shape = (N_DEV, AW // N_DEV, D), (N_DEV, D // N_DEV, 2 * LANES)
    return pl.pallas_call(
        body, name="a_bwd", grid=(nT,),
        in_specs=[row(D), row(LANES), row(LANES), row(D), _const_spec((1, D)), _const_spec(w_kv.shape),
                  _const_spec(wa_out.shape), _const_spec(ws.shape),
                  _const_spec((1, AW)), _const_spec((1, AW)), pl.BlockSpec((TM, AW), lambda i: (i, 0)),
                  pl.BlockSpec((TM, AW), lambda i: (i, 2)), row(AW), row(AW), row(LANES),
                  row(LANES), row(LANES), row(LANES)] + [hbm] * nr,
        out_specs=[row(3 * AW), _const_spec(gwo_shape), _const_spec(gwk_shape), row(D),
                   _acc_spec((1, D)), _acc_spec((1, 2 * LANES)), _acc_spec((1, AW)),
                   _acc_spec((1, AW)), _acc_spec(ws.shape), _acc_spec((CHUNK, LANES))] + [hbm] * nr,
        out_shape=(S((T, 3 * AW), BF), S(gwo_shape, BF), S(gwk_shape, BF), S((T, D), F32),
                   S((1, D), F32), S((1, 2 * LANES), F32), S((1, AW), F32), S((1, AW), F32),
                   S(ws.shape, F32), S((CHUNK, LANES), F32)) + tuple(S(r.shape, r.dtype) for r in ready),
        scratch_shapes=[pltpu.VMEM((TM, AW), BF), pltpu.VMEM((TM, AW), F32), pltpu.VMEM((AW, D), F32),
                        pltpu.VMEM((D, 2 * LANES), F32)] + _direct_sems(nr),
        compiler_params=_params(("arbitrary",)),
    )(dh1p, dk, dv, h1, g_kv, w_kv, wa_out, ws, ln_g, ln_b, z, z, sv, vhat, rstd, rc, rs1, rs2, *ready)


def _a_in_bwd(dz, wa_in_t, x, dh1, g_a, ready):
    T, D = x.shape
    TM = min(512, T)
    nT = T // TM
    nr = len(ready)

    RING = min(3, nT)

    def body(dz_hbm, wain_ref, x_ref, dh1_ref, ga_ref, *rest):
        ready_refs, (dx_ref, n1_ref, dga_ref), rest = rest[:nr], rest[nr:nr + 3], rest[nr + 3:]
        recv_refs, (dz_ring, dz_sem, ssem, rsem, lsem) = rest[:nr], rest[nr:]
        i = pl.program_id(0)
        exchanges = [_Direct(ready_refs[k], recv_refs[k], ssem.at[k], rsem.at[k], lsem.at[k], scatter=True)
                     for k in range(nr)]

        def fetch(tile, slot):
            rows = pl.ds(pl.multiple_of(tile * TM, TM), TM)
            return pltpu.make_async_copy(dz_hbm.at[rows], dz_ring.at[slot], dz_sem.at[slot])

        @pl.when(i == 0)
        def _():
            for k in range(RING):
                fetch(k, k).start()
            for e in exchanges:
                e.start()
            dga_ref[...] = jnp.zeros_like(dga_ref)

        xv = x_ref[...]
        r1 = lax.rsqrt(jnp.mean(xv * xv, axis=-1, keepdims=True) + EPS)
        xh = xv * r1
        ga = ga_ref[...]
        n1_ref[...] = (xh * ga).astype(BF).T
        slot = lax.rem(i, RING)
        fetch(i, slot).wait()
        dn1 = _dot(dz_ring[slot], wain_ref[...])
        dga_ref[...] += jnp.sum(dn1 * xh, axis=0, keepdims=True)
        dx_ref[...] = dh1_ref[...] + _rms_bwd(dn1, xh, r1, ga)

        @pl.when(i + RING < nT)
        def _():
            fetch(i + RING, slot).start()

        @pl.when(i == nT - 1)
        def _():
            for e in exchanges:
                e.finish()

    row = functools.partial(_row_spec, TM)
    hbm = pl.BlockSpec(memory_space=pl.ANY)
    S = jax.ShapeDtypeStruct
    return pl.pallas_call(
        body, name="a_in_bwd", grid=(nT,),
        in_specs=[hbm, _const_spec(wa_in_t.shape), row(D), row(D), _const_spec((1, D))] + [hbm] * nr,
        out_specs=[row(D), _col_spec(TM, D), _acc_spec((1, D))] + [hbm] * nr,
        out_shape=(S((T, D), F32), S((D, T), BF), S((1, D), F32)) + tuple(S(r.shape, r.dtype) for r in ready),
        scratch_shapes=[pltpu.VMEM((RING, TM, dz.shape[1]), BF), pltpu.SemaphoreType.DMA((RING,))] + _direct_sems(nr),
        compiler_params=_params(("arbitrary",)),
    )(dz, wa_in_t, x, dh1, g_a, *ready)


def _wgrad(problems, name, bt=512):
    T = problems[0][0].shape[1]
    BT = min(bt, T)
    nt = T // BT
    n = len(problems)
    dims = [(at.shape[0], b.shape[1] // nblk, nblk) for at, b, nblk in problems]

    def body(*refs):
        ins, outs, accs = refs[:2 * n], refs[2 * n:3 * n], refs[3 * n:]
        t = pl.program_id(0)

        @pl.when(t == 0)
        def _():
            for acc in accs:
                acc[...] = jnp.zeros_like(acc)

        for k in range(n):
            accs[k][...] += _dot(ins[2 * k][...], ins[2 * k + 1][...])

        @pl.when(t == nt - 1)
        def _():
            for k, (_, N, nblk) in enumerate(dims):
                for j in range(nblk):
                    outs[k][j] = accs[k][:, j * N:(j + 1) * N].astype(BF)

    in_specs = []
    for at, b, _ in problems:
        in_specs += [pl.BlockSpec((at.shape[0], BT), lambda t: (0, t)), pl.BlockSpec((BT, b.shape[1]), lambda t: (t, 0))]
    return pl.pallas_call(
        body, name=name, grid=(nt,), in_specs=in_specs,
        out_specs=[pl.BlockSpec((nblk, K, N), lambda t: (0, 0, 0)) for K, N, nblk in dims],
        out_shape=[jax.ShapeDtypeStruct((nblk, K, N), BF) for K, N, nblk in dims],
        scratch_shapes=[pltpu.VMEM((K, nblk * N), F32) for K, N, nblk in dims],
        compiler_params=_params(("arbitrary",)),
    )(*[operand for at, b, _ in problems for operand in (at, b)])


def _wgrad_exchange(a, b, me, small, name):
    K, T = a.shape
    N = b.shape[1] // N_DEV
    BT = T
    nt = T // BT
    last = N_DEV - 1
    n_chip = N_DEV // 2

    def far_of(k, core):
        return jnp.where((core == 0) & ((k == 1) | (k == 2)), k, n_chip - 1 - k)

    def block_of(s, me_i):
        k, odd = s // 2, s % 2
        core = me_i & 1
        return me_i ^ ((far_of(k, jnp.where(odd == 1, core, 1 - core)) << 1) | (1 - odd))

    H = K // 2

    def body(me_ref, a_ref, b_ref, small_ref, recv_ref, full_ref, *scratch):
        (acc, dstage, istage, half, relay, d_s, d_r, i_s, i_r, r_s, r_r, lsem, parts_scr, red_scr, e_s, e_r, e_l, g_s,
         g_r, g_l) = scratch
        s, t = pl.program_id(0), pl.program_id(1)
        x, y, c = (lax.axis_index(ax) for ax in AXES)
        ex = [_Direct(small_ref, parts_scr, e_s, e_r, e_l, scatter=True)]
        regather = _TwoLevel(red_scr, full_ref, g_s, g_r, g_l)

        def to_sibling(k, slot):
            return pltpu.make_async_remote_copy(src_ref=dstage.at[slot], dst_ref=half.at[k], send_sem=d_s.at[k],
                                                recv_sem=d_r.at[k], device_id=(x, y, 1 - c), device_id_type=MESH)

        def to_chip(k, slot):
            over_x = far_of(k, c) == 2
            px, py = jnp.where(over_x, 1 - x, x), jnp.where(over_x, y, 1 - y)
            return pltpu.make_async_remote_copy(src_ref=istage.at[slot], dst_ref=recv_ref.at[jnp.where(over_x, 1, 2)],
                                                send_sem=i_s.at[k], recv_sem=i_r.at[k], device_id=(px, py, c),
                                                device_id_type=MESH)

        def to_relay(j, slot):
            to = (1 - x, y, c) if j == 0 else (x, 1 - y, c)
            return pltpu.make_async_remote_copy(src_ref=istage.at[slot, pl.ds(j * H, H)], dst_ref=relay.at[j],
                                                send_sem=r_s.at[j], recv_sem=r_r.at[j], device_id=to,
                                                device_id_type=MESH)

        @pl.when((s == 0) & (t == 0))
        def _():
            for e in ex:
                e.start()

        acc[...] = _dot(a_ref[...], b_ref[...])

        @pl.when(t == nt - 1)
        def _():
            k = lax.div(s, 2)
            slot = lax.rem(k, 2)

            @pl.when(lax.rem(s, 2) == 0)
            def _():
                @pl.when(k >= 2)
                def _():
                    to_sibling(k - 2, slot).wait_send()

                dstage[slot] = acc[...].astype(BF)
                to_sibling(k, slot).start()

            @pl.when(lax.rem(s, 2) == 1)
            def _():
                to_sibling(k, slot).wait_recv()
                pair = acc[...] + half[k].astype(F32)

                @pl.when(k == 0)
                def _():
                    istage[slot] = pair.astype(BF)
                    for j in range(2):
                        to_relay(j, slot).start()

                @pl.when(k == 1)
                def _():
                    for j in range(2):
                        to_relay(j, slot).wait_recv()

                @pl.when(k == 2)
                def _():
                    for j in range(2):
                        to_relay(j, slot).wait_send()

                @pl.when(k == n_chip - 1)
                def _():
                    to_chip(1, slot).wait_send()
                    istage[slot] = pair.astype(BF)

                @pl.when((k == 1) | (k == 2))
                def _():
                    over_x = far_of(k, c) == 2
                    istage[slot, 0:H] = (pair[:H] + jnp.where(over_x, 0.0, relay[0].astype(F32))).astype(BF)
                    istage[slot, H:K] = (pair[H:] + jnp.where(over_x, relay[1].astype(F32), 0.0)).astype(BF)
                    to_chip(k, slot).start()

            @pl.when(s == last)
            def _():
                own = pltpu.make_async_copy(istage.at[slot], recv_ref.at[0], lsem)
                own.start()
                to_chip(2, 0).wait_send()
                to_sibling(n_chip - 2, 0).wait_send()
                to_sibling(n_chip - 1, 1).wait_send()
                for kk in (1, 2):
                    to_chip(kk, 0).wait_recv()
                own.wait()
                for e in ex:
                    e.finish()
                total = parts_scr[0]
                for dev in range(1, N_DEV):
                    total = total + parts_scr[dev]
                red_scr[...] = total
                regather.start()
                regather.forward()
                regather.finish()

    hbm = pl.BlockSpec(memory_space=pl.ANY)
    dma = pltpu.SemaphoreType.DMA
    grid_spec = pltpu.PrefetchScalarGridSpec(
        num_scalar_prefetch=1, grid=(N_DEV, nt),
        in_specs=[pl.BlockSpec((K, BT), lambda s, t, me_ref: (0, t), pipeline_mode=pl.Buffered(1)),
                  pl.BlockSpec((BT, N), lambda s, t, me_ref: (t, block_of(s, me_ref[0]))), hbm],
        out_specs=[hbm, hbm],
        scratch_shapes=[pltpu.VMEM((K, N), F32), pltpu.VMEM((2, K, N), BF), pltpu.VMEM((2, K, N), BF),
                        pltpu.VMEM((n_chip, K, N), BF), pltpu.VMEM((2, H, N), BF), dma((n_chip,)), dma((n_chip,)),
                        dma((n_chip - 1,)), dma((n_chip - 1,)), dma((2,)), dma((2,)), dma,
                        pltpu.VMEM(small.shape, F32), pltpu.VMEM(small.shape[1:], F32),
                        dma((last,)), dma((last,)), dma, dma((last,)), dma((last,)), dma])
    return pl.pallas_call(
        body, name=name, grid_spec=grid_spec,
        out_shape=[jax.ShapeDtypeStruct((n_chip - 1, K, N), BF), jax.ShapeDtypeStruct(small.shape, F32)],
        compiler_params=_params(("arbitrary", "arbitrary")),
    )(me, a, b, small)


def _my_index():
    return 4 * lax.axis_index("x") + 2 * lax.axis_index("y") + lax.axis_index("c")


def _peer(mask):
    x, y, c = (lax.axis_index(a) for a in AXES)
    return (x ^ ((mask >> 2) & 1), y ^ ((mask >> 1) & 1), c ^ (mask & 1))


def _dev_index(p):
    return 4 * p[0] + 2 * p[1] + p[2]


class _Direct:
    def __init__(self, src, dst, send_sems, recv_sems, local_sem, scatter):
        me = _my_index()
        self.own = pltpu.make_async_copy(src.at[me] if scatter else src, dst.at[me], local_sem)
        self.sends, self.recvs = [], []
        for k in range(1, N_DEV):
            p = _peer(k)
            pi = _dev_index(p)
            sems = dict(send_sem=send_sems.at[k - 1], recv_sem=recv_sems.at[k - 1], device_id=p, device_id_type=MESH)
            self.sends.append(pltpu.make_async_remote_copy(src_ref=src.at[pi] if scatter else src, dst_ref=dst.at[me],
                                                           **sems))
            self.recvs.append(pltpu.make_async_remote_copy(src_ref=src.at[me] if scatter else src, dst_ref=dst.at[pi],
                                                           **sems))

    def start(self):
        self.own.start()
        for cp in self.sends:
            cp.start()

    def finish(self):
        for cp in self.sends:
            cp.wait_send()
        for cp in self.recvs:
            cp.wait_recv()
        self.own.wait()


class _TwoLevel:
    def __init__(self, src, dst, send_sems, recv_sems, local_sem, own=True):
        x, y, c = (lax.axis_index(a) for a in AXES)
        self.me, self.sibling = (x, y, c), (x, y, 1 - c)
        self.chips = [(1 - x, y), (x, 1 - y), (1 - x, 1 - y)]
        self.src, self.dst, self.send_sems, self.recv_sems = src, dst, send_sems, recv_sems
        self.own = pltpu.make_async_copy(src, dst.at[_dev_index(self.me)], local_sem) if own else None

    def _copy(self, k, block, to, from_src=False):
        slot = self.dst.at[_dev_index(block)]
        return pltpu.make_async_remote_copy(src_ref=self.src if from_src else slot, dst_ref=slot,
                                            send_sem=self.send_sems.at[k], recv_sem=self.recv_sems.at[k],
                                            device_id=to, device_id_type=MESH)

    def _firsts(self):
        c = self.me[2]
        return [self._copy(0, self.me, self.sibling, True)] + [self._copy(1 + j, self.me, (*chip, c), True)
                                                               for j, chip in enumerate(self.chips)]

    def _passed(self):
        c = self.me[2]
        return [self._copy(4 + j, (*chip, c), self.sibling) for j, chip in enumerate(self.chips)]

    def start(self):
        if self.own is not None:
            self.own.start()
        for cp in self._firsts():
            cp.start()

    def wait_sibling(self):
        self._copy(0, self.sibling, self.me).wait_recv()

    def wait_chip_and_forward(self, j):
        self._copy(1 + j, (*self.chips[j], self.me[2]), self.me).wait_recv()
        self._passed()[j].start()

    def wait_passed(self, j):
        self._copy(4 + j, (*self.chips[j], 1 - self.me[2]), self.me).wait_recv()

    def wait_sends(self):
        for cp in self._firsts() + self._passed():
            cp.wait_send()
        if self.own is not None:
            self.own.wait()

    def forward(self):
        for j in range(3):
            self.wait_chip_and_forward(j)

    def finish(self):
        self.wait_sibling()
        for j in range(3):
            self.wait_passed(j)
        self.wait_sends()


class _RelayGather:
    def __init__(self, dst, send_sems, recv_sems):
        x, y, c = (lax.axis_index(a) for a in AXES)
        self.c = c
        self.sib, self.xn, self.yn, self.dg = (x, y, 1 - c), (1 - x, y, c), (x, 1 - y, c), (1 - x, 1 - y, c)
        self.me = (x, y, c)
        self.dst, self.send_sems, self.recv_sems = dst, send_sems, recv_sems
        self.half = dst.shape[1] // 2

    def _slot(self, dev, part=None):
        i = _dev_index(dev)
        if part is None:
            return self.dst.at[i]
        return self.dst.at[i, pl.ds(part * self.half, self.half)]

    def _copy(self, k, dev, to, part=None):
        ref = self._slot(dev, part)
        return pltpu.make_async_remote_copy(src_ref=ref, dst_ref=ref, send_sem=self.send_sems.at[k],
                                            recv_sem=self.recv_sems.at[k], device_id=to, device_id_type=MESH)

    def _other(self, dev):
        return (dev[0], dev[1], 1 - self.c)

    def start(self):
        for k, to in enumerate((self.sib, self.xn, self.yn)):
            self._copy(k, self.me, to).start()

    def send_own(self, k):
        return self._copy(k, self.me, (self.sib, self.xn, self.yn)[k])

    def wait_sibling(self):
        self._copy(0, self.sib, self.me).wait_recv()

    def on_x(self):
        self._copy(1, self.xn, self.me).wait_recv()
        self._copy(3, self.xn, self.yn, part=0).start()
        self._copy(5, self.xn, self.sib).start()

    def on_y(self):
        self._copy(2, self.yn, self.me).wait_recv()
        self._copy(4, self.yn, self.xn, part=1).start()
        self._copy(6, self.yn, self.sib).start()

    def on_diag(self):
        self._copy(3, self.dg, self.me, part=0).wait_recv()
        self._copy(4, self.dg, self.me, part=1).wait_recv()
        self._copy(7, self.dg, self.sib).start()

    def wait_passed(self, j):
        self._copy(5 + j, self._other((self.xn, self.yn, self.dg)[j]), self.me).wait_recv()

    def wait_sends(self):
        for k, to in enumerate((self.sib, self.xn, self.yn)):
            self._copy(k, self.me, to).wait_send()
        self._copy(3, self.xn, self.yn, part=0).wait_send()
        self._copy(4, self.yn, self.xn, part=1).wait_send()
        for j, dev in enumerate((self.xn, self.yn, self.dg)):
            self._copy(5 + j, dev, self.sib).wait_send()


def _direct_sems(n):
    if n == 0:
        return []
    return [pltpu.SemaphoreType.DMA((n, 7)), pltpu.SemaphoreType.DMA((n, 7)), pltpu.SemaphoreType.DMA((n,))]


def _adam_math(w, g, m, v):
    m = ADAM_B1 * m + (1.0 - ADAM_B1) * g
    v = ADAM_B2 * v + (1.0 - ADAM_B2) * (g * g)
    m_hat = m / (1.0 - ADAM_B1 ** ADAM_STEP)
    v_hat = v / (1.0 - ADAM_B2 ** ADAM_STEP)
    delta = -ADAM_LR * (m_hat / (jnp.sqrt(v_hat) + ADAM_EPS) + ADAM_WD * w)
    return delta, m, v


def _sum_adam(tensors, name):
    NB = 2
    n = len(tensors)

    def body(*refs):
        ins, outs = refs[:4 * n], refs[4 * n:]
        for k in range(n):
            p_ref, w_ref, m_ref, v_ref = ins[4 * k:4 * k + 4]
            g_ref, d_ref, nm_ref, nv_ref = outs[4 * k:4 * k + 4]
            g = p_ref[0].astype(F32)
            for i in range(1, p_ref.shape[0]):
                g = g + p_ref[i].astype(F32)
            g_ref[...] = g
            d_ref[...], nm_ref[...], nv_ref[...] = _adam_math(w_ref[...], g, m_ref[...], v_ref[...])

    in_specs, out_specs, out_shape, operands = [], [], [], []
    for parts, w, m, v in tensors:
        R, C = w.shape
        blk = pl.BlockSpec((R // NB, C), lambda i: (i, 0))
        in_specs += [pl.BlockSpec((parts.shape[0], R // NB, C), lambda i: (0, i, 0)), blk, blk, blk]
        out_specs += [blk] * 4
        out_shape += [jax.ShapeDtypeStruct((R, C), F32)] * 4
        operands += [parts, w, m, v]
    res = pl.pallas_call(
        body, name=name, grid=(NB,), in_specs=in_specs, out_specs=out_specs, out_shape=out_shape,
        compiler_params=_params(("arbitrary",)),
    )(*operands)
    return [tuple(res[4 * k:4 * k + 4]) for k in range(n)]


SUBLANES = 8


def _nrows(size):
    return -(-size // (SUBLANES * LANES)) * SUBLANES


def _view2d(a):
    return a.reshape(-1, LANES) if a.size % LANES == 0 else a.reshape(1, -1)


def _pack_small(parts, total_rows, name):
    arrs = [p[0] for p in parts]

    def body(*refs):
        out = refs[-1]
        out[...] = jnp.zeros_like(out)
        at = 0
        for ref, (a, rows, flag) in zip(refs[:-1], parts):
            val = ref[...].T if flag == "T" else ref[...]
            r, c = (rows, val.shape[1]) if flag == "T" else val.shape
            out[at:at + r, 0:c] = val[:r]
            at += _nrows(r * c)

    return pl.pallas_call(body, name=name, out_shape=jax.ShapeDtypeStruct((total_rows, LANES), F32))(*arrs)


def _small_update(full, me, reps, shards, name):
    n = len(reps) + len(shards)

    def body(me_ref, full_ref, *refs):
        ins, outs = refs[:3 * n], refs[3 * n:]
        at = 0
        for k in range(n):
            w_ref, m_ref, v_ref = ins[3 * k:3 * k + 3]
            r, c = w_ref.shape
            if k < len(reps):
                g = full_ref[at:at + r, 0:c]
                at += _nrows(r * c)
            else:
                seg = full_ref[at:at + N_DEV * r, :]
                row = lax.broadcasted_iota(jnp.int32, seg.shape, 0)
                pick = [jnp.sum(jnp.where(row == r * me_ref[0] + t, seg, 0.0), axis=0, keepdims=True) for t in range(r)]
                g = pick[0] if r == 1 else jnp.concatenate(pick, axis=0)
                at += N_DEV * r
            g_ref, d_ref, nm_ref, nv_ref = outs[4 * k:4 * k + 4]
            g_ref[...] = g
            d_ref[...], nm_ref[...], nv_ref[...] = _adam_math(w_ref[...], g, m_ref[...], v_ref[...])
        outs[4 * n][...] = full_ref[at:at + 1, 0:1]

    flat = [t for p in reps + shards for t in p]
    S = jax.ShapeDtypeStruct
    res = pl.pallas_call(
        body, name=name,
        in_specs=[pl.BlockSpec(memory_space=pltpu.SMEM)] + [pl.BlockSpec(memory_space=pltpu.VMEM)] * (1 + len(flat)),
        out_shape=[S(p[0].shape, F32) for p in reps + shards for _ in range(4)] + [S((1, 1), F32)],
    )(me, full, *flat)
    return [tuple(res[4 * k:4 * k + 4]) for k in range(n)], res[4 * n]


def _rope_tables(T):
    pos = np.arange(T, dtype=np.float32)
    inv_freq = (np.float64(ROPE_THETA) ** (-np.arange(0, HEAD_DIM, 2, dtype=np.float64) / HEAD_DIM)).astype(np.float32)
    ang = (pos[:, None] * inv_freq[None, :]).astype(np.float64)
    cos, sin, zero = np.cos(ang).astype(np.float32), np.sin(ang).astype(np.float32), np.zeros(ang.shape, np.float32)
    c = np.concatenate([cos, cos, cos, cos], axis=1)
    s1 = np.concatenate([-sin, zero, -sin, zero], axis=1)
    s2 = np.concatenate([zero, sin, zero, sin], axis=1)
    return jnp.asarray(c), jnp.asarray(s1), jnp.asarray(s2)


def kernel(x, a_norm_g, a_w_in, a_ln_g, a_ln_b, a_ws, a_bs, a_w_out, kv_norm_g, w_kv, b_kv, b_norm_g, b_w_in, b_bq, b_sinks, b_w_out, final_norm_g, loss_target, m_a_norm_g, m_a_w_in, m_a_ln_g, m_a_ln_b, m_a_ws, m_a_bs, m_a_w_out, m_kv_norm_g, m_w_kv, m_b_kv, m_b_norm_g, m_b_w_in, m_b_bq, m_b_sinks, m_b_w_out, m_final_norm_g, v_a_norm_g, v_a_w_in, v_a_ln_g, v_a_ln_b, v_a_ws, v_a_bs, v_a_w_out, v_kv_norm_g, v_w_kv, v_b_kv, v_b_norm_g, v_b_w_in, v_b_bq, v_b_sinks, v_b_w_out, v_final_norm_g):
    T, D = x.shape[1], x.shape[2]
    AW = a_ln_g.shape[1] * N_DEV
    G = a_ws.shape[1]
    assert w_kv.shape[1] == 2 * LANES and a_ws.shape[2] == CHUNK and T % CHUNK == 0
    me = _my_index()

    xs, tgt = x[0], loss_target[0]
    z, wa_in_t, g_a, ln_g, ln_b, wa_out, wkv = _in_proj(xs, a_w_in[0], [a_norm_g, a_ln_g, a_ln_b], me.reshape(1),
                                                        [a_w_out[0], w_kv])
    wa_in_t = wa_in_t.reshape(-1, D)
    wa_out = wa_out.reshape(AW, D)
    wkv = wkv.reshape(D, 2 * LANES)

    rc, rs1, rs2 = _rope_tables(T)
    ws = a_ws[0]
    g_kv = kv_norm_g.reshape(1, D)
    bkv = b_kv.reshape(1, -1)
    g_f = final_norm_g.reshape(1, D)
    sinks = b_sinks.reshape(1, 16)
    h1, sv, vhat, rstd, k4, v4, kt, vt, wb_in, wb_out = _a_fwd(
        xs, z, ln_g, ln_b, ws, a_bs[0], wa_out, g_kv, wkv, bkv, rc, rs1, rs2, [b_w_in[0], b_w_out[0]])
    wb_out = wb_out.reshape(-1, D)
    q, g2, o, dh2, dh2_b, loss, d_gf = _b_fwd(h1, b_norm_g, wb_in, b_bq, rc, rs1, rs2, k4, vt, sinks, wb_out, g_f, tgt)
    dh1p, dz2, n2, y2, dk, dv, d_bq, d_gb, d_sink = _b_bwd(dh2, h1, q, g2, o, k4, v4, kt, sinks, wb_out, wb_in,
                                                           b_norm_g, rc, rs1, rs2)
    d_sink = d_sink[:, :4].reshape(2, 2, 4).transpose(0, 2, 1).reshape(1, 16)
    gw_b_in, gw_b_out = _wgrad([(n2, dz2, N_DEV), (y2, dh2_b, 1)], "wgrad_b", bt=1024)
    gw_b_out = gw_b_out.reshape(N_DEV, -1, D)
    (dz, gw_a_out, gw_kv, dh1_f, d_gkv, d_bkv, d_lng, d_lnb, d_ws, d_bst, r_b_in, r_b_out) = _a_bwd(
        dh1p, dk, dv, h1, g_kv, wkv, wa_out, ws, ln_g, ln_b, z, sv, vhat, rstd, rc, rs1, rs2, [gw_b_in, gw_b_out])
    dx, n1, d_ga, r_a_out, r_kv = _a_in_bwd(dz, wa_in_t, xs, dh1_f, g_a, [gw_a_out, gw_kv])
    small = [(_view2d(d_ws), None, None), (d_bst, G, "T")] + [(_view2d(a), None, None) for a in (
        d_gkv, d_bkv, d_gb, d_bq, d_sink, d_gf, d_ga, d_lng, d_lnb, loss)]
    used = sum(_nrows(G * CHUNK if flag else a.size) for a, _, flag in small)
    per = -(-used // (SUBLANES * N_DEV)) * SUBLANES
    small_pack = _pack_small(small, per * N_DEV, "pack_small").reshape(N_DEV, per, LANES)
    r_a_in, full_small = _wgrad_exchange(n1, dz, me.reshape(1), small_pack, "wgrad_a_in")

    ((g_a_out, d_a_out, nm_a_out, nv_a_out), (g_kvw, d_kvw, nm_kvw, nv_kvw), (g_b_in, d_b_in, nm_b_in, nv_b_in),
     (g_b_out, d_b_out, nm_b_out, nv_b_out)) = _sum_adam(
        [(r_a_out, a_w_out[0], m_a_w_out[0], v_a_w_out[0]), (r_kv, w_kv, m_w_kv, v_w_kv),
         (r_b_in, b_w_in[0], m_b_w_in[0], v_b_w_in[0]), (r_b_out, b_w_out[0], m_b_w_out[0], v_b_w_out[0])], "adam_rest")
    (g_a_in, d_a_in, nm_a_in, nv_a_in), = _sum_adam([(r_a_in, a_w_in[0], m_a_w_in[0], v_a_w_in[0])], "adam_a_in")

    full_small = full_small.reshape(N_DEV * per, LANES)
    reps = [(a_ws, m_a_ws, v_a_ws), (a_bs, m_a_bs, v_a_bs), (kv_norm_g, m_kv_norm_g, v_kv_norm_g),
            (b_kv, m_b_kv, v_b_kv), (b_norm_g, m_b_norm_g, v_b_norm_g), (b_bq, m_b_bq, v_b_bq),
            (b_sinks, m_b_sinks, v_b_sinks), (final_norm_g, m_final_norm_g, v_final_norm_g)]
    shards = [(a_norm_g, m_a_norm_g, v_a_norm_g), (a_ln_g, m_a_ln_g, v_a_ln_g), (a_ln_b, m_a_ln_b, v_a_ln_b)]
    upd, loss = _small_update(full_small, me.reshape(1), [tuple(_view2d(t) for t in p) for p in reps],
                              [tuple(_view2d(t) for t in p) for p in shards], "adam_small")
    loss = loss[0, 0]
    sm_g, sd, snm, snv = ([upd[k][j].reshape(p[0].shape) for k, p in enumerate(reps + shards)] for j in range(4))

    def order(big, sm):
        a_in, a_out, kvw, b_in, b_out = big
        ws_, bs_, kvg, bkv_, bng, bq_, snk, fng, ang, alng, alnb = sm
        return (ang, a_in[None], alng, alnb, ws_, bs_, a_out[None], kvg, kvw, bkv_, bng, b_in[None], bq_, snk,
                b_out[None], fng)

    grads = order((g_a_in, g_a_out, g_kvw, g_b_in, g_b_out), sm_g)
    deltas = order((d_a_in, d_a_out, d_kvw, d_b_in, d_b_out), sd)
    new_m = order((nm_a_in, nm_a_out, nm_kvw, nm_b_in, nm_b_out), snm)
    new_v = order((nv_a_in, nv_a_out, nv_kvw, nv_b_in, nv_b_out), snv)
    return (loss, dx[None], *grads, *deltas, *new_m, *new_v)
```
